```python
import math, functools
import jax, jax.numpy as jnp
from jax import lax
import numpy as np

D_MODEL = 1024
BATCH = 8
SEQ = 8192
DEPTH = 2

N_EVEN = (DEPTH + 1) // 2
N_ODD = DEPTH // 2
RMS_EPS = 1e-6

MLA_HEADS = 8
MLA_Q_RANK = 256
MLA_KV_RANK = 128
MLA_NOPE = 64
MLA_ROPE = 32
MLA_V = 64
ROPE_THETA = 10000.0
Q_BLOCK = 128

POOL_WINDOWS = (2, 4, 8, 16)
POOL_GROUP = 128
POOL_WIDTH = POOL_GROUP * len(POOL_WINDOWS)
MIX_AB = MLA_HEADS * MLA_V + POOL_WIDTH
IN_AB = MLA_Q_RANK + MLA_KV_RANK + MLA_ROPE + POOL_WIDTH + MIX_AB

GDN_HEADS = 8
GDN_DK = 128
GDN_DV = 128
CONV_WIDTH = 4
CHUNK = 64
GDN_QK = GDN_HEADS * GDN_DK
GDN_VW = GDN_HEADS * GDN_DV
GDN_CONV_CH = 2 * GDN_QK + GDN_VW
IN_C = GDN_CONV_CH + GDN_VW + 2 * GDN_HEADS

kernel_name = "hybrid_mla_pool_gdn_gated"


def rmsnorm(x, g):
    xf = x.astype(jnp.float32)
    y = xf * lax.rsqrt(jnp.mean(xf * xf, axis=-1, keepdims=True) + RMS_EPS)
    return (y * g.astype(jnp.float32)).astype(x.dtype)


def l2norm(x):
    xf = x.astype(jnp.float32)
    return xf * lax.rsqrt(jnp.sum(xf * xf, axis=-1, keepdims=True) + RMS_EPS)


def rope_tables(positions):
    half = MLA_ROPE // 2
    inv_freq = 1.0 / (ROPE_THETA ** (jnp.arange(half, dtype=jnp.float32) / half))
    ang = positions.astype(jnp.float32)[..., None] * inv_freq
    return jnp.cos(ang), jnp.sin(ang)


def apply_rope(x, cos, sin):
    half = MLA_ROPE // 2
    xf = x.astype(jnp.float32)
    x1, x2 = xf[..., :half], xf[..., half:]
    return jnp.concatenate([x1 * cos - x2 * sin, x2 * cos + x1 * sin], axis=-1).astype(x.dtype)


def mla(q_lat, kv_lat, k_rope, positions, q_a_norm, w_q_b, kv_a_norm, w_kv_b):
    B, S, _ = q_lat.shape
    q = (rmsnorm(q_lat, q_a_norm) @ w_q_b).reshape(B, S, MLA_HEADS, MLA_NOPE + MLA_ROPE)
    q_nope, q_rope = q[..., :MLA_NOPE], q[..., MLA_NOPE:]
    kv = (rmsnorm(kv_lat, kv_a_norm) @ w_kv_b).reshape(B, S, MLA_HEADS, MLA_NOPE + MLA_V)
    k_nope, v = kv[..., :MLA_NOPE], kv[..., MLA_NOPE:]
    cos, sin = rope_tables(positions)
    q_rope = apply_rope(q_rope, cos[:, :, None, :], sin[:, :, None, :])
    k_rope = apply_rope(k_rope, cos, sin)
    scale = (MLA_NOPE + MLA_ROPE) ** -0.5
    nb = S // Q_BLOCK
    qn_b = q_nope.reshape(B, nb, Q_BLOCK, MLA_HEADS, MLA_NOPE).transpose(1, 0, 2, 3, 4)
    qr_b = q_rope.reshape(B, nb, Q_BLOCK, MLA_HEADS, MLA_ROPE).transpose(1, 0, 2, 3, 4)
    key_idx = jnp.arange(S)

    def attend_block(args):
        qn, qr, i = args
        s = (jnp.einsum('bqhd,bkhd->bhqk', qn, k_nope, preferred_element_type=jnp.float32)
             + jnp.einsum('bqhr,bkr->bhqk', qr, k_rope, preferred_element_type=jnp.float32))
        q_idx = i * Q_BLOCK + jnp.arange(Q_BLOCK)
        causal = key_idx[None, :] <= q_idx[:, None]
        p = jax.nn.softmax(jnp.where(causal, s * scale, -jnp.inf), axis=-1)
        return jnp.einsum('bhqk,bkhd->bqhd', p.astype(v.dtype), v)

    o = lax.map(attend_block, (qn_b, qr_b, jnp.arange(nb)))
    return o.transpose(1, 0, 2, 3, 4).reshape(B, S, MLA_HEADS * MLA_V)


def multiscale_pool(xp, pool_w, pool_scale):
    B, S, _ = xp.shape
    xf = xp.astype(jnp.float32)
    csum = jnp.cumsum(xf, axis=1)
    t = jnp.arange(S)
    outs = []
    for g, w in enumerate(POOL_WINDOWS):
        lo, hi = g * POOL_GROUP, (g + 1) * POOL_GROUP
        c = csum[..., lo:hi]
        lagged = jnp.pad(c, ((0, 0), (w, 0), (0, 0)))[:, :S]
        count = jnp.minimum(t + 1, w).astype(jnp.float32)[None, :, None]
        outs.append((c - lagged) / count - xf[..., lo:hi])
    d = jnp.stack(outs, axis=2).astype(xp.dtype)
    y = jnp.einsum('bsgc,gcd->bsgd', d, pool_w).reshape(B, S, POOL_WIDTH)
    return y * pool_scale


def causal_dwconv(x, w):
    S = x.shape[1]
    xp = jnp.pad(x, ((0, 0), (CONV_WIDTH - 1, 0), (0, 0)))
    y = xp[:, 0:S] * w[0]
    for j in range(1, CONV_WIDTH):
        y = y + xp[:, j:j + S] * w[j]
    return y


def chunk_gated_delta(q, k, v, g, beta):
    B, H, S, Dk = q.shape
    Dv = v.shape[-1]
    n = S // CHUNK
    q = (q * Dk ** -0.5).reshape(B, H, n, CHUNK, Dk)
    k = k.reshape(B, H, n, CHUNK, Dk)
    v = v.reshape(B, H, n, CHUNK, Dv)
    beta = beta.reshape(B, H, n, CHUNK)
    gc = jnp.cumsum(g.reshape(B, H, n, CHUNK), axis=-1)
    idx = jnp.arange(CHUNK)
    incl = idx[:, None] >= idx[None, :]
    strict = idx[:, None] > idx[None, :]
    gamma = jnp.exp(jnp.where(incl, gc[..., :, None] - gc[..., None, :], -jnp.inf))
    kb = k * beta[..., None]
    m = jnp.where(strict, jnp.einsum('bhnid,bhnjd->bhnij', kb, k) * gamma, 0.0)
    a_mat = m + jnp.eye(CHUNK, dtype=m.dtype)
    solve = functools.partial(lax.linalg.triangular_solve, left_side=True, lower=True, unit_diagonal=True)
    u = solve(a_mat, v * beta[..., None])
    w = solve(a_mat, kb * jnp.exp(gc)[..., None])
    attn = jnp.einsum('bhnid,bhnjd->bhnij', q, k) * gamma
    q_dec = q * jnp.exp(gc)[..., None]
    k_dec = k * jnp.exp(gc[..., -1:] - gc)[..., None]
    g_last = jnp.exp(gc[..., -1])

    def step(state, inp):
        u_i, w_i, qd_i, kd_i, at_i, gl_i = inp
        v_new = u_i - jnp.einsum('bhck,bhkv->bhcv', w_i, state)
        o_i = jnp.einsum('bhck,bhkv->bhcv', qd_i, state) + jnp.einsum('bhcj,bhjv->bhcv', at_i, v_new)
        state = state * gl_i[..., None, None] + jnp.einsum('bhck,bhcv->bhkv', kd_i, v_new)
        return state, o_i

    mv = lambda a: jnp.moveaxis(a, 2, 0)
    state0 = jnp.zeros((B, H, Dk, Dv), jnp.float32)
    _, o = lax.scan(step, state0, (mv(u), mv(w), mv(q_dec), mv(k_dec), mv(attn), mv(g_last)))
    return jnp.moveaxis(o, 0, 2).reshape(B, H, S, Dv)


def mla_pool_layer(h, positions, w_in, q_a_norm, w_q_b, kv_a_norm, w_kv_b, pool_w, pool_scale, w_out):
    proj = h @ w_in
    o1 = MLA_Q_RANK
    o2 = o1 + MLA_KV_RANK
    o3 = o2 + MLA_ROPE
    o4 = o3 + POOL_WIDTH
    q_lat, kv_lat, k_rope = proj[..., :o1], proj[..., o1:o2], proj[..., o2:o3]
    xp, z = proj[..., o3:o4], proj[..., o4:]
    y_a = mla(q_lat, kv_lat, k_rope, positions, q_a_norm, w_q_b, kv_a_norm, w_kv_b)
    y_b = multiscale_pool(xp, pool_w, pool_scale)
    y = jnp.concatenate([y_a, y_b], axis=-1) * jax.nn.silu(z)
    return y @ w_out


def gdn_layer(h, w_in, conv_w, a_log, dt_bias, o_norm, w_out):
    B, S, _ = h.shape
    proj = h @ w_in
    qkv = jax.nn.silu(causal_dwconv(proj[..., :GDN_CONV_CH], conv_w))
    o1 = GDN_CONV_CH + GDN_VW
    z = proj[..., GDN_CONV_CH:o1]
    a = proj[..., o1:o1 + GDN_HEADS].astype(jnp.float32)
    b = proj[..., o1 + GDN_HEADS:].astype(jnp.float32)
    q = l2norm(qkv[..., :GDN_QK].reshape(B, S, GDN_HEADS, GDN_DK))
    k = l2norm(qkv[..., GDN_QK:2 * GDN_QK].reshape(B, S, GDN_HEADS, GDN_DK))
    v = qkv[..., 2 * GDN_QK:].reshape(B, S, GDN_HEADS, GDN_DV).astype(jnp.float32)
    beta = jax.nn.sigmoid(b)
    g = -jnp.exp(a_log.astype(jnp.float32)) * jax.nn.softplus(a + dt_bias.astype(jnp.float32))
    tr = lambda t: jnp.swapaxes(t, 1, 2)
    o = chunk_gated_delta(tr(q), tr(k), tr(v), tr(g), tr(beta))
    o = rmsnorm(tr(o), o_norm) * jax.nn.silu(z.astype(jnp.float32).reshape(B, S, GDN_HEADS, GDN_DV))
    return o.reshape(B, S, GDN_VW).astype(h.dtype) @ w_out


def _fwd_setup_inputs(seed: int = 0) -> dict:
    key = jax.random.key(seed)
    ks = jax.random.split(key, 24)
    f32 = jnp.float32

    def dense(k, shape, fan_in):
        return jax.random.normal(k, shape, f32) * fan_in ** -0.5

    def gain(k, shape):
        return 1.0 + 0.02 * jax.random.normal(k, shape, f32)

    x = jax.random.normal(ks[0], (BATCH, SEQ, D_MODEL), f32)
    positions = jnp.broadcast_to(jnp.arange(SEQ, dtype=jnp.int32)[None, :], (BATCH, SEQ))
    dt = jnp.exp(jax.random.uniform(ks[16], (N_ODD, GDN_HEADS), f32, math.log(1e-3), math.log(1e-1)))
    return {
        "x": x,
        "positions": positions,
        "norm_ab": gain(ks[1], (N_EVEN, D_MODEL)),
        "w_in_ab": dense(ks[2], (N_EVEN, D_MODEL, IN_AB), D_MODEL),
        "q_a_norm": gain(ks[3], (N_EVEN, MLA_Q_RANK)),
        "w_q_b": dense(ks[4], (N_EVEN, MLA_Q_RANK, MLA_HEADS * (MLA_NOPE + MLA_ROPE)), MLA_Q_RANK),
        "kv_a_norm": gain(ks[5], (N_EVEN, MLA_KV_RANK)),
        "w_kv_b": dense(ks[6], (N_EVEN, MLA_KV_RANK, MLA_HEADS * (MLA_NOPE + MLA_V)), MLA_KV_RANK),
        "pool_w": dense(ks[7], (N_EVEN, len(POOL_WINDOWS), POOL_GROUP, POOL_GROUP), POOL_GROUP),
        "pool_scale": gain(ks[8], (N_EVEN, POOL_WIDTH)),
        "w_out_ab": dense(ks[9], (N_EVEN, MIX_AB, D_MODEL), MIX_AB),
        "norm_c": gain(ks[10], (N_ODD, D_MODEL)),
        "w_in_c": dense(ks[11], (N_ODD, D_MODEL, IN_C), D_MODEL),
        "conv_w": dense(ks[12], (N_ODD, CONV_WIDTH, GDN_CONV_CH), CONV_WIDTH),
        "a_log": jnp.log(jax.random.uniform(ks[13], (N_ODD, GDN_HEADS), f32, 1.0, 16.0)),
        "dt_bias": dt + jnp.log(-jnp.expm1(-dt)),
        "o_norm": gain(ks[14], (N_ODD, GDN_DV)),
        "w_out_c": dense(ks[15], (N_ODD, GDN_VW, D_MODEL), GDN_VW),
        "final_norm": gain(ks[17], (D_MODEL,)),
    }


def _fwd_reference(x, positions, norm_ab, w_in_ab, q_a_norm, w_q_b, kv_a_norm, w_kv_b, pool_w, pool_scale,
              w_out_ab, norm_c, w_in_c, conv_w, a_log, dt_bias, o_norm, w_out_c, final_norm):
    h = x
    for layer in range(DEPTH):
        i = layer // 2
        if layer % 2 == 0:
            h = h + mla_pool_layer(rmsnorm(h, norm_ab[i]), positions, w_in_ab[i], q_a_norm[i], w_q_b[i],
                                   kv_a_norm[i], w_kv_b[i], pool_w[i], pool_scale[i], w_out_ab[i])
        else:
            h = h + gdn_layer(rmsnorm(h, norm_c[i]), w_in_c[i], conv_w[i], a_log[i], dt_bias[i],
                              o_norm[i], w_out_c[i])
    return rmsnorm(h, final_norm)


import jax as _jax
import jax.numpy as _jnp

TWIN_FORMAT = 'train_step'
FWD_PARAMS = ['x', 'positions', 'norm_ab', 'w_in_ab', 'q_a_norm', 'w_q_b', 'kv_a_norm', 'w_kv_b', 'pool_w', 'pool_scale', 'w_out_ab', 'norm_c', 'w_in_c', 'conv_w', 'a_log', 'dt_bias', 'o_norm', 'w_out_c', 'final_norm']
TWIN_WEIGHTS = ['norm_ab', 'w_in_ab', 'q_a_norm', 'w_q_b', 'kv_a_norm', 'w_kv_b', 'pool_w', 'pool_scale', 'w_out_ab', 'norm_c', 'w_in_c', 'conv_w', 'a_log', 'dt_bias', 'o_norm', 'w_out_c', 'final_norm']
TWIN_DIFF_INPUT = 'x'
TWIN_INPUTS = ['x', 'positions', 'norm_ab', 'w_in_ab', 'q_a_norm', 'w_q_b', 'kv_a_norm', 'w_kv_b', 'pool_w', 'pool_scale', 'w_out_ab', 'norm_c', 'w_in_c', 'conv_w', 'a_log', 'dt_bias', 'o_norm', 'w_out_c', 'final_norm', 'loss_target', 'm_norm_ab', 'm_w_in_ab', 'm_q_a_norm', 'm_w_q_b', 'm_kv_a_norm', 'm_w_kv_b', 'm_pool_w', 'm_pool_scale', 'm_w_out_ab', 'm_norm_c', 'm_w_in_c', 'm_conv_w', 'm_a_log', 'm_dt_bias', 'm_o_norm', 'm_w_out_c', 'm_final_norm', 'v_norm_ab', 'v_w_in_ab', 'v_q_a_norm', 'v_w_q_b', 'v_kv_a_norm', 'v_w_kv_b', 'v_pool_w', 'v_pool_scale', 'v_w_out_ab', 'v_norm_c', 'v_w_in_c', 'v_conv_w', 'v_a_log', 'v_dt_bias', 'v_o_norm', 'v_w_out_c', 'v_final_norm']
TWIN_OUTPUTS = ['loss', 'grad_x', 'grad_norm_ab', 'grad_w_in_ab', 'grad_q_a_norm', 'grad_w_q_b', 'grad_kv_a_norm', 'grad_w_kv_b', 'grad_pool_w', 'grad_pool_scale', 'grad_w_out_ab', 'grad_norm_c', 'grad_w_in_c', 'grad_conv_w', 'grad_a_log', 'grad_dt_bias', 'grad_o_norm', 'grad_w_out_c', 'grad_final_norm', 'delta_norm_ab', 'delta_w_in_ab', 'delta_q_a_norm', 'delta_w_q_b', 'delta_kv_a_norm', 'delta_w_kv_b', 'delta_pool_w', 'delta_pool_scale', 'delta_w_out_ab', 'delta_norm_c', 'delta_w_in_c', 'delta_conv_w', 'delta_a_log', 'delta_dt_bias', 'delta_o_norm', 'delta_w_out_c', 'delta_final_norm', 'new_m_norm_ab', 'new_m_w_in_ab', 'new_m_q_a_norm', 'new_m_w_q_b', 'new_m_kv_a_norm', 'new_m_w_kv_b', 'new_m_pool_w', 'new_m_pool_scale', 'new_m_w_out_ab', 'new_m_norm_c', 'new_m_w_in_c', 'new_m_conv_w', 'new_m_a_log', 'new_m_dt_bias', 'new_m_o_norm', 'new_m_w_out_c', 'new_m_final_norm', 'new_v_norm_ab', 'new_v_w_in_ab', 'new_v_q_a_norm', 'new_v_w_q_b', 'new_v_kv_a_norm', 'new_v_w_kv_b', 'new_v_pool_w', 'new_v_pool_scale', 'new_v_w_out_ab', 'new_v_norm_c', 'new_v_w_in_c', 'new_v_conv_w', 'new_v_a_log', 'new_v_dt_bias', 'new_v_o_norm', 'new_v_w_out_c', 'new_v_final_norm']
TWIN_LEAF_KINDS = {'loss': 'loss', 'grad_x': 'grad_x', 'grad_norm_ab': 'grad_w', 'grad_w_in_ab': 'grad_w', 'grad_q_a_norm': 'grad_w', 'grad_w_q_b': 'grad_w', 'grad_kv_a_norm': 'grad_w', 'grad_w_kv_b': 'grad_w', 'grad_pool_w': 'grad_w', 'grad_pool_scale': 'grad_w', 'grad_w_out_ab': 'grad_w', 'grad_norm_c': 'grad_w', 'grad_w_in_c': 'grad_w', 'grad_conv_w': 'grad_w', 'grad_a_log': 'grad_w', 'grad_dt_bias': 'grad_w', 'grad_o_norm': 'grad_w', 'grad_w_out_c': 'grad_w', 'grad_final_norm': 'grad_w', 'delta_norm_ab': 'delta_w', 'delta_w_in_ab': 'delta_w', 'delta_q_a_norm': 'delta_w', 'delta_w_q_b': 'delta_w', 'delta_kv_a_norm': 'delta_w', 'delta_w_kv_b': 'delta_w', 'delta_pool_w': 'delta_w', 'delta_pool_scale': 'delta_w', 'delta_w_out_ab': 'delta_w', 'delta_norm_c': 'delta_w', 'delta_w_in_c': 'delta_w', 'delta_conv_w': 'delta_w', 'delta_a_log': 'delta_w', 'delta_dt_bias': 'delta_w', 'delta_o_norm': 'delta_w', 'delta_w_out_c': 'delta_w', 'delta_final_norm': 'delta_w', 'new_m_norm_ab': 'new_m', 'new_m_w_in_ab': 'new_m', 'new_m_q_a_norm': 'new_m', 'new_m_w_q_b': 'new_m', 'new_m_kv_a_norm': 'new_m', 'new_m_w_kv_b': 'new_m', 'new_m_pool_w': 'new_m', 'new_m_pool_scale': 'new_m', 'new_m_w_out_ab': 'new_m', 'new_m_norm_c': 'new_m', 'new_m_w_in_c': 'new_m', 'new_m_conv_w': 'new_m', 'new_m_a_log': 'new_m', 'new_m_dt_bias': 'new_m', 'new_m_o_norm': 'new_m', 'new_m_w_out_c': 'new_m', 'new_m_final_norm': 'new_m', 'new_v_norm_ab': 'new_v', 'new_v_w_in_ab': 'new_v', 'new_v_q_a_norm': 'new_v', 'new_v_w_q_b': 'new_v', 'new_v_kv_a_norm': 'new_v', 'new_v_w_kv_b': 'new_v', 'new_v_pool_w': 'new_v', 'new_v_pool_scale': 'new_v', 'new_v_w_out_ab': 'new_v', 'new_v_norm_c': 'new_v', 'new_v_w_in_c': 'new_v', 'new_v_conv_w': 'new_v', 'new_v_a_log': 'new_v', 'new_v_dt_bias': 'new_v', 'new_v_o_norm': 'new_v', 'new_v_w_out_c': 'new_v', 'new_v_final_norm': 'new_v'}


def _forward(args):
    return _fwd_reference(*[args[k] for k in FWD_PARAMS])


def _output_shape():
    def fwd():
        inp = _fwd_setup_inputs(0)
        return _fwd_reference(*[inp[k] for k in FWD_PARAMS])
    out = _jax.eval_shape(fwd)
    return out.shape, out.dtype

N_MICROBATCH = 1
ADAM_LR = 0.001
ADAM_B1 = 0.9
ADAM_B2 = 0.999
ADAM_EPS = 1e-08
ADAM_WD = 0.01
ADAM_STEP = 10
PER_EXAMPLE_BATCH_AXIS = {'x': 0, 'positions': 0, 'loss_target': 0}
SHARED_INPUTS = []
_WEIGHT_DTYPES = {'norm_ab': _jnp.float32, 'w_in_ab': _jnp.float32, 'q_a_norm': _jnp.float32, 'w_q_b': _jnp.float32, 'kv_a_norm': _jnp.float32, 'w_kv_b': _jnp.float32, 'pool_w': _jnp.float32, 'pool_scale': _jnp.float32, 'w_out_ab': _jnp.float32, 'norm_c': _jnp.float32, 'w_in_c': _jnp.float32, 'conv_w': _jnp.float32, 'a_log': _jnp.float32, 'dt_bias': _jnp.float32, 'o_norm': _jnp.float32, 'w_out_c': _jnp.float32, 'final_norm': _jnp.float32}
MOMENT_SCALE = {'norm_ab': 1.593662e-01, 'w_in_ab': 1.117021e-01, 'q_a_norm': 4.494349e-02, 'w_q_b': 2.508979e-02, 'kv_a_norm': 9.383189e-02, 'w_kv_b': 3.298628e-02, 'pool_w': 1.485221e-01, 'pool_scale': 1.431867e-01, 'w_out_ab': 1.057720e-01, 'norm_c': 1.997602e-01, 'w_in_c': 1.019030e-01, 'conv_w': 9.372689e-02, 'a_log': 5.575647e-01, 'dt_bias': 5.476084e-01, 'o_norm': 3.337855e-01, 'w_out_c': 1.237613e-01, 'final_norm': 6.405842e+01}


def _to_microbatches(a, axis):
    t = _jnp.moveaxis(a, axis, 0)
    t = t.reshape((N_MICROBATCH, t.shape[0] // N_MICROBATCH) + t.shape[1:])
    return _jnp.moveaxis(t, 1, axis + 1)


def setup_inputs(seed: int = 0) -> dict:
    inp = _fwd_setup_inputs(seed)
    key = _jax.random.fold_in(_jax.random.key(seed), 7919)
    shape, _ = _output_shape()
    out = dict(inp)
    out["loss_target"] = _jax.random.normal(_jax.random.fold_in(key, 0), shape, _jnp.float32)
    for i, name in enumerate(TWIN_WEIGHTS):
        w = inp[name].astype(_jnp.float32)
        if MOMENT_SCALE is None:
            s = _jnp.sqrt(_jnp.mean(_jnp.square(w)) + 1e-30)
        else:
            s = MOMENT_SCALE[name]
        km, kv = _jax.random.split(_jax.random.fold_in(key, i + 1))
        out[name] = w
        out["m_" + name] = s * _jax.random.normal(km, w.shape, _jnp.float32)
        out["v_" + name] = (s * s) * _jax.random.uniform(kv, w.shape, _jnp.float32, 0.5, 1.5)
    if N_MICROBATCH > 1:
        for name, axis in PER_EXAMPLE_BATCH_AXIS.items():
            out[name] = _to_microbatches(out[name], axis)
    return {'x': out['x'], 'positions': out['positions'], 'norm_ab': out['norm_ab'], 'w_in_ab': out['w_in_ab'], 'q_a_norm': out['q_a_norm'], 'w_q_b': out['w_q_b'], 'kv_a_norm': out['kv_a_norm'], 'w_kv_b': out['w_kv_b'], 'pool_w': out['pool_w'], 'pool_scale': out['pool_scale'], 'w_out_ab': out['w_out_ab'], 'norm_c': out['norm_c'], 'w_in_c': out['w_in_c'], 'conv_w': out['conv_w'], 'a_log': out['a_log'], 'dt_bias': out['dt_bias'], 'o_norm': out['o_norm'], 'w_out_c': out['w_out_c'], 'final_norm': out['final_norm'], 'loss_target': out['loss_target'], 'm_norm_ab': out['m_norm_ab'], 'm_w_in_ab': out['m_w_in_ab'], 'm_q_a_norm': out['m_q_a_norm'], 'm_w_q_b': out['m_w_q_b'], 'm_kv_a_norm': out['m_kv_a_norm'], 'm_w_kv_b': out['m_w_kv_b'], 'm_pool_w': out['m_pool_w'], 'm_pool_scale': out['m_pool_scale'], 'm_w_out_ab': out['m_w_out_ab'], 'm_norm_c': out['m_norm_c'], 'm_w_in_c': out['m_w_in_c'], 'm_conv_w': out['m_conv_w'], 'm_a_log': out['m_a_log'], 'm_dt_bias': out['m_dt_bias'], 'm_o_norm': out['m_o_norm'], 'm_w_out_c': out['m_w_out_c'], 'm_final_norm': out['m_final_norm'], 'v_norm_ab': out['v_norm_ab'], 'v_w_in_ab': out['v_w_in_ab'], 'v_q_a_norm': out['v_q_a_norm'], 'v_w_q_b': out['v_w_q_b'], 'v_kv_a_norm': out['v_kv_a_norm'], 'v_w_kv_b': out['v_w_kv_b'], 'v_pool_w': out['v_pool_w'], 'v_pool_scale': out['v_pool_scale'], 'v_w_out_ab': out['v_w_out_ab'], 'v_norm_c': out['v_norm_c'], 'v_w_in_c': out['v_w_in_c'], 'v_conv_w': out['v_conv_w'], 'v_a_log': out['v_a_log'], 'v_dt_bias': out['v_dt_bias'], 'v_o_norm': out['v_o_norm'], 'v_w_out_c': out['v_w_out_c'], 'v_final_norm': out['v_final_norm']}


def _loss(weights, diff, rest, loss_target):
    with _jax.named_scope("forward"):
        args = {**rest, TWIN_DIFF_INPUT: diff, **{k: w.astype(_WEIGHT_DTYPES[k]) for k, w in weights.items()}}
        y = _forward(args)
    with _jax.named_scope("loss_head"):
        err = _jnp.square(y.astype(_jnp.float32) - loss_target)
        return 0.5 * _jnp.sum(_jnp.mean(err, axis=-1)) if err.ndim else 0.5 * err


def _adamw(w, g, m, v):
    m = ADAM_B1 * m + (1.0 - ADAM_B1) * g
    v = ADAM_B2 * v + (1.0 - ADAM_B2) * _jnp.square(g)
    m_hat = m / (1.0 - ADAM_B1 ** ADAM_STEP)
    v_hat = v / (1.0 - ADAM_B2 ** ADAM_STEP)
    delta = -ADAM_LR * (m_hat / (_jnp.sqrt(v_hat) + ADAM_EPS) + ADAM_WD * w)
    return delta, m, v


def reference(x, positions, norm_ab, w_in_ab, q_a_norm, w_q_b, kv_a_norm, w_kv_b, pool_w, pool_scale, w_out_ab, norm_c, w_in_c, conv_w, a_log, dt_bias, o_norm, w_out_c, final_norm, loss_target, m_norm_ab, m_w_in_ab, m_q_a_norm, m_w_q_b, m_kv_a_norm, m_w_kv_b, m_pool_w, m_pool_scale, m_w_out_ab, m_norm_c, m_w_in_c, m_conv_w, m_a_log, m_dt_bias, m_o_norm, m_w_out_c, m_final_norm, v_norm_ab, v_w_in_ab, v_q_a_norm, v_w_q_b, v_kv_a_norm, v_w_kv_b, v_pool_w, v_pool_scale, v_w_out_ab, v_norm_c, v_w_in_c, v_conv_w, v_a_log, v_dt_bias, v_o_norm, v_w_out_c, v_final_norm):
    given = dict(x=x, positions=positions, norm_ab=norm_ab, w_in_ab=w_in_ab, q_a_norm=q_a_norm, w_q_b=w_q_b, kv_a_norm=kv_a_norm, w_kv_b=w_kv_b, pool_w=pool_w, pool_scale=pool_scale, w_out_ab=w_out_ab, norm_c=norm_c, w_in_c=w_in_c, conv_w=conv_w, a_log=a_log, dt_bias=dt_bias, o_norm=o_norm, w_out_c=w_out_c, final_norm=final_norm, loss_target=loss_target, m_norm_ab=m_norm_ab, m_w_in_ab=m_w_in_ab, m_q_a_norm=m_q_a_norm, m_w_q_b=m_w_q_b, m_kv_a_norm=m_kv_a_norm, m_w_kv_b=m_w_kv_b, m_pool_w=m_pool_w, m_pool_scale=m_pool_scale, m_w_out_ab=m_w_out_ab, m_norm_c=m_norm_c, m_w_in_c=m_w_in_c, m_conv_w=m_conv_w, m_a_log=m_a_log, m_dt_bias=m_dt_bias, m_o_norm=m_o_norm, m_w_out_c=m_w_out_c, m_final_norm=m_final_norm, v_norm_ab=v_norm_ab, v_w_in_ab=v_w_in_ab, v_q_a_norm=v_q_a_norm, v_w_q_b=v_w_q_b, v_kv_a_norm=v_kv_a_norm, v_w_kv_b=v_w_kv_b, v_pool_w=v_pool_w, v_pool_scale=v_pool_scale, v_w_out_ab=v_w_out_ab, v_norm_c=v_norm_c, v_w_in_c=v_w_in_c, v_conv_w=v_conv_w, v_a_log=v_a_log, v_dt_bias=v_dt_bias, v_o_norm=v_o_norm, v_w_out_c=v_w_out_c, v_final_norm=v_final_norm)
    weights = {n: given[n] for n in TWIN_WEIGHTS}
    shared = {n: given[n] for n in SHARED_INPUTS}
    per_example = {n: given[n] for n in ['x', 'positions']}
    grad_fn = _jax.value_and_grad(_loss, argnums=(0, 1))

    def one_microbatch(ex, loss_target):
        ex = dict(ex)
        diff = ex.pop(TWIN_DIFF_INPUT)
        return grad_fn(weights, diff, {**shared, **ex}, loss_target)

    if N_MICROBATCH == 1:
        loss, (grad_w, grad_x) = one_microbatch(per_example, given["loss_target"])
    else:
        def body(carry, xs):
            loss_sum, grad_sum = carry
            l_k, (gw_k, gx_k) = one_microbatch(xs[0], xs[1])
            with _jax.named_scope("update"):
                return (loss_sum + l_k, _jax.tree.map(_jnp.add, grad_sum, gw_k)), gx_k

        init = (_jnp.zeros((), _jnp.float32), _jax.tree.map(_jnp.zeros_like, weights))
        (loss, grad_w), grad_x = _jax.lax.scan(body, init, (per_example, given["loss_target"]))
    with _jax.named_scope("update"):
        delta_w, new_m, new_v = {}, {}, {}
        for n in TWIN_WEIGHTS:
            delta_w[n], new_m[n], new_v[n] = _adamw(weights[n], grad_w[n], given["m_" + n], given["v_" + n])
    return (loss, grad_x, *[grad_w[n] for n in TWIN_WEIGHTS], *[delta_w[n] for n in TWIN_WEIGHTS],
            *[new_m[n] for n in TWIN_WEIGHTS], *[new_v[n] for n in TWIN_WEIGHTS])
```

```python
import functools
import math

import jax
import jax.numpy as jnp
from jax import lax
from jax.experimental import pallas as pl
from jax.experimental.pallas import tpu as pltpu

F32 = jnp.float32
BF16 = jnp.bfloat16
HI = lax.Precision.HIGHEST
MESH = pl.DeviceIdType.MESH

RMS_EPS = 1e-6
D_MODEL = 1024
MLA_HEADS = 8
MLA_Q_RANK = 256
MLA_KV_RANK = 128
MLA_NOPE = 64
MLA_ROPE = 32
MLA_V = 64
ROPE_THETA = 10000.0
POOL_WINDOWS = (2, 4, 8, 16)
POOL_GROUP = 128
POOL_WIDTH = 512
POOL_HALO = 16
GDN_HEADS = 8
GDN_DK = 128
CONV_WIDTH = 4
CONV_HALO = 8
CHUNK = 64
IN_AB_PAD = 2048
IN_C_PAD = 4224
ATT_SCALE = (MLA_NOPE + MLA_ROPE) ** -0.5

ADAM_LR = 0.001
ADAM_B1 = 0.9
ADAM_B2 = 0.999
ADAM_EPS = 1e-08
ADAM_WD = 0.01
ADAM_STEP = 10

LANES = 128
VMEM_LIMIT = 56 * 1024 * 1024

ROW_TILE = 256
ATT_TILE = 512
GDN_TILE = 512
MM_TILE = 512

NN = (((1,), (0,)), ((), ()))
NT = (((1,), (1,)), ((), ()))
TN = (((0,), (0,)), ((), ()))


def _dot(a, b, dims=NN, prec=None):
    return lax.dot_general(a, b, dims, precision=prec, preferred_element_type=F32)


def _tile(n, pref):
    if n <= pref:
        return n
    step = LANES if pref >= LANES else 8
    for t in range(pref - pref % step, 0, -step):
        if n % t == 0:
            return t
    return n


def _params(sem):
    return pltpu.CompilerParams(dimension_semantics=sem, vmem_limit_bytes=VMEM_LIMIT)


def _sigmoid(x):
    return 1.0 / (1.0 + jnp.exp(-x))


def _softplus(x):
    return jnp.maximum(x, 0.0) + jnp.log(1.0 + jnp.exp(-jnp.abs(x)))


def _matmul(a, b, mode, *, name, out_dtype=F32, add=None, tm=MM_TILE, tn=MM_TILE, tk=2048):
    if mode == "nn":
        (m, k), (k2, n) = a.shape, b.shape
    elif mode == "nt":
        (m, k), (n, k2) = a.shape, b.shape
    else:
        (k, m), (k2, n) = a.shape, b.shape
    assert k == k2, (a.shape, b.shape, mode)
    tm, tn, tk = _tile(m, tm), _tile(n, tn), _tile(k, tk)
    nk = k // tk
    if mode == "tn":
        a_spec = pl.BlockSpec((tk, tm), lambda i, j, kk: (kk, i))
    else:
        a_spec = pl.BlockSpec((tm, tk), lambda i, j, kk: (i, kk))
    if mode == "nt":
        b_spec = pl.BlockSpec((tn, tk), lambda i, j, kk: (j, kk))
    else:
        b_spec = pl.BlockSpec((tk, tn), lambda i, j, kk: (kk, j))
    o_spec = pl.BlockSpec((tm, tn), lambda i, j, kk: (i, j))
    dims = {"nn": NN, "nt": NT, "tn": TN}[mode]
    has_add = add is not None

    def body(*refs):
        if has_add:
            a_ref, b_ref, add_ref, o_ref, acc = refs
        else:
            a_ref, b_ref, o_ref, acc = refs
        kk = pl.program_id(2)

        @pl.when(kk == 0)
        def _():
            acc[...] = jnp.zeros_like(acc)

        acc[...] += _dot(a_ref[...], b_ref[...], dims)

        @pl.when(kk == nk - 1)
        def _():
            o = acc[...]
            if has_add:
                o = o + add_ref[...]
            o_ref[...] = o.astype(out_dtype)

    in_specs = [a_spec, b_spec] + ([o_spec] if has_add else [])
    args = (a, b) + ((add,) if has_add else ())
    return pl.pallas_call(
        body, name=name, grid=(m // tm, n // tn, nk), in_specs=in_specs, out_specs=o_spec,
        out_shape=jax.ShapeDtypeStruct((m, n), out_dtype), scratch_shapes=[pltpu.VMEM((tm, tn), F32)],
        compiler_params=_params(("parallel", "parallel", "arbitrary")),
    )(*args)


def _rms_fwd(h, g, *, name):
    t, d = h.shape
    tm = _tile(t, 2 * ROW_TILE)

    def body(h_ref, g_ref, o_ref):
        x = h_ref[...]
        r = lax.rsqrt(jnp.mean(x * x, axis=-1, keepdims=True) + RMS_EPS)
        o_ref[...] = (x * r * g_ref[...]).astype(BF16)

    return pl.pallas_call(
        body, name=name, grid=(t // tm,),
        in_specs=[pl.BlockSpec((tm, d), lambda i: (i, 0)), pl.BlockSpec((1, d), lambda i: (0, 0))],
        out_specs=pl.BlockSpec((tm, d), lambda i: (i, 0)),
        out_shape=jax.ShapeDtypeStruct((t, d), BF16), compiler_params=_params(("parallel",)),
    )(h, g)


def _rms_bwd(h, g, dy, dres, *, name, with_bf16):
    t, d = h.shape
    tm = _tile(t, 2 * ROW_TILE)

    def body(h_ref, g_ref, dy_ref, dres_ref, *outs):
        i = pl.program_id(0)
        dh_ref, dg_ref = outs[0], outs[-1]
        x = h_ref[...]
        r = lax.rsqrt(jnp.mean(x * x, axis=-1, keepdims=True) + RMS_EPS)
        xh = x * r
        dyv = dy_ref[...].astype(F32)
        dxh = dyv * g_ref[...]
        dx = r * (dxh - xh * jnp.mean(dxh * xh, axis=-1, keepdims=True))
        dh = dres_ref[...] + dx
        dh_ref[...] = dh
        if with_bf16:
            outs[1][...] = dh.astype(BF16)

        @pl.when(i == 0)
        def _():
            dg_ref[...] = jnp.zeros_like(dg_ref)

        dg_ref[...] += jnp.sum(dyv * xh, axis=0, keepdims=True)

    row = pl.BlockSpec((tm, d), lambda i: (i, 0))
    vec = pl.BlockSpec((1, d), lambda i: (0, 0))
    out_shape = [jax.ShapeDtypeStruct((t, d), F32)]
    out_specs = [row]
    if with_bf16:
        out_shape.append(jax.ShapeDtypeStruct((t, d), BF16))
        out_specs.append(row)
    out_shape.append(jax.ShapeDtypeStruct((1, d), F32))
    out_specs.append(vec)
    return pl.pallas_call(
        body, name=name, grid=(t // tm,), in_specs=[row, vec, row, row], out_specs=out_specs,
        out_shape=out_shape, compiler_params=_params(("arbitrary",)),
    )(h, g, dy, dres)


def _rope_partner(x):
    lane = lax.broadcasted_iota(jnp.int32, x.shape, 1)
    swapped = jnp.where(lane < MLA_NOPE + MLA_ROPE // 2, pltpu.roll(x, LANES - 16, 1), pltpu.roll(x, 16, 1))
    return jnp.where((lane >= MLA_NOPE) & (lane < MLA_NOPE + MLA_ROPE), swapped, 0.0)


def _pool_counts(row0, tm, w):
    t_idx = row0 + lax.broadcasted_iota(jnp.int32, (tm, POOL_GROUP), 0)
    return jnp.minimum(t_idx + 1, w).astype(F32)


def _ab_prep(proj, pos, inv_freq, q_a_norm, kv_a_norm, *, name):
    t = proj.shape[0]
    tm = _tile(t, ROW_TILE)
    hb = tm // POOL_HALO

    def body(p_ref, halo_ref, pos_ref, inv_ref, qg_ref, kg_ref, qn_ref, kvn_ref, kr_ref, d_ref, cos_ref, sin_ref, ext):
        i = pl.program_id(0)
        ql = p_ref[:, 0:MLA_Q_RANK]
        r = lax.rsqrt(jnp.mean(ql * ql, axis=-1, keepdims=True) + RMS_EPS)
        qn_ref[...] = (ql * r * qg_ref[...]).astype(BF16)
        kl = p_ref[:, MLA_Q_RANK:MLA_Q_RANK + MLA_KV_RANK]
        r = lax.rsqrt(jnp.mean(kl * kl, axis=-1, keepdims=True) + RMS_EPS)
        kvn_ref[...] = (kl * r * kg_ref[...]).astype(BF16)
        ang = pos_ref[...].astype(F32) * inv_ref[...]
        lane = lax.broadcasted_iota(jnp.int32, (tm, LANES), 1)
        in_rope = (lane >= MLA_NOPE) & (lane < MLA_NOPE + MLA_ROPE)
        cos_t = jnp.where(in_rope, jnp.cos(ang), 1.0)
        sin_t = jnp.where(in_rope, jnp.sin(ang), 0.0)
        sin_t = jnp.where(lane < MLA_NOPE + MLA_ROPE // 2, -sin_t, sin_t)
        cos_ref[...] = cos_t
        sin_ref[...] = sin_t
        kr = p_ref[:, 384:512]
        kr_ref[...] = kr * cos_t + _rope_partner(kr) * sin_t
        xp = p_ref[:, 512:1024]
        ext[0:POOL_HALO, :] = jnp.where(i > 0, halo_ref[...], 0.0)
        ext[POOL_HALO:POOL_HALO + tm, :] = xp
        for g, w in enumerate(POOL_WINDOWS):
            lo = g * POOL_GROUP
            acc = ext[POOL_HALO:POOL_HALO + tm, lo:lo + POOL_GROUP]
            for s in range(1, w):
                acc = acc + ext[POOL_HALO - s:POOL_HALO - s + tm, lo:lo + POOL_GROUP]
            cnt = _pool_counts(i * tm, tm, w)
            d_ref[:, lo:lo + POOL_GROUP] = (acc / cnt - xp[:, lo:lo + POOL_GROUP]).astype(BF16)

    row = lambda w: pl.BlockSpec((tm, w), lambda i: (i, 0))
    vec = lambda w: pl.BlockSpec((1, w), lambda i: (0, 0))
    return pl.pallas_call(
        body, name=name, grid=(t // tm,),
        in_specs=[row(1024), pl.BlockSpec((POOL_HALO, POOL_WIDTH), lambda i: (jnp.maximum(i * hb - 1, 0), 1)),
                  pl.BlockSpec((tm, 1), lambda i: (i, 0)), vec(LANES), vec(MLA_Q_RANK), vec(MLA_KV_RANK)],
        out_specs=[row(MLA_Q_RANK), row(MLA_KV_RANK), row(LANES), row(POOL_WIDTH), row(LANES), row(LANES)],
        out_shape=[jax.ShapeDtypeStruct((t, MLA_Q_RANK), BF16), jax.ShapeDtypeStruct((t, MLA_KV_RANK), BF16),
                   jax.ShapeDtypeStruct((t, LANES), F32), jax.ShapeDtypeStruct((t, POOL_WIDTH), BF16),
                   jax.ShapeDtypeStruct((t, LANES), F32), jax.ShapeDtypeStruct((t, LANES), F32)],
        scratch_shapes=[pltpu.VMEM((tm + POOL_HALO, POOL_WIDTH), F32)],
        compiler_params=_params(("parallel",)),
    )(proj, proj, pos, inv_freq, q_a_norm, kv_a_norm)


def _qk_rope(qraw, kvk, kr, cos_t, sin_t, *, name):
    t = qraw.shape[0]
    tm = _tile(t, 2 * ROW_TILE)

    def body(q_ref, k_ref, kr_ref, c_ref, s_ref, qo_ref, ko_ref):
        c, s, krv = c_ref[...], s_ref[...], kr_ref[...]
        for h in range(MLA_HEADS):
            sl = slice(h * LANES, (h + 1) * LANES)
            q = q_ref[:, sl]
            qo_ref[:, sl] = (q * c + _rope_partner(q) * s).astype(BF16)
            ko_ref[:, sl] = (k_ref[:, sl] + krv).astype(BF16)

    row = lambda w: pl.BlockSpec((tm, w), lambda i: (i, 0))
    return pl.pallas_call(
        body, name=name, grid=(t // tm,), in_specs=[row(1024), row(1024), row(LANES), row(LANES), row(LANES)],
        out_specs=[row(1024), row(1024)],
        out_shape=[jax.ShapeDtypeStruct((t, 1024), BF16), jax.ShapeDtypeStruct((t, 1024), BF16)],
        compiler_params=_params(("parallel",)),
    )(qraw, kvk, kr, cos_t, sin_t)


def _qk_rope_bwd(dq, dk, cos_t, sin_t, *, name):
    t = dq.shape[0]
    tm = _tile(t, 2 * ROW_TILE)

    def body(dq_ref, dk_ref, c_ref, s_ref, dqo_ref, dko_ref, dkr_ref):
        c, s = c_ref[...], s_ref[...]
        lane = lax.broadcasted_iota(jnp.int32, (tm, LANES), 1)
        in_rope = (lane >= MLA_NOPE) & (lane < MLA_NOPE + MLA_ROPE)
        dkr = jnp.zeros((tm, LANES), F32)
        for h in range(MLA_HEADS):
            sl = slice(h * LANES, (h + 1) * LANES)
            g = dq_ref[:, sl]
            dqo_ref[:, sl] = (g * c + _rope_partner(g * s)).astype(BF16)
            gk = dk_ref[:, sl]
            dko_ref[:, sl] = gk.astype(BF16)
            dkr = dkr + jnp.where(in_rope, gk, 0.0)
        dkr_ref[...] = dkr * c + _rope_partner(dkr * s)

    row = lambda w: pl.BlockSpec((tm, w), lambda i: (i, 0))
    return pl.pallas_call(
        body, name=name, grid=(t // tm,), in_specs=[row(1024), row(1024), row(LANES), row(LANES)],
        out_specs=[row(1024), row(1024), row(LANES)],
        out_shape=[jax.ShapeDtypeStruct((t, 1024), BF16), jax.ShapeDtypeStruct((t, 1024), BF16),
                   jax.ShapeDtypeStruct((t, LANES), F32)],
        compiler_params=_params(("parallel",)),
    )(dq, dk, cos_t, sin_t)


def _ab_prep_bwd(proj, q_a_norm, kv_a_norm, dqn, dkvn_k, dkvn_v, dkr, dd, dz, *, name):
    t = proj.shape[0]
    tm = _tile(t, ROW_TILE)
    hb = tm // POOL_HALO
    last_halo = t // POOL_HALO - 1
    nt = t // tm

    def body(p_ref, qg_ref, kg_ref, dqn_ref, dk1_ref, dk2_ref, dkr_ref, dd_ref, ddn_ref, dz_ref,
             dp_ref, dqg_ref, dkg_ref, ext):
        i = pl.program_id(0)

        @pl.when(i == 0)
        def _():
            dqg_ref[...] = jnp.zeros_like(dqg_ref)
            dkg_ref[...] = jnp.zeros_like(dkg_ref)

        def norm_bwd(x, g, dy, dg_ref):
            r = lax.rsqrt(jnp.mean(x * x, axis=-1, keepdims=True) + RMS_EPS)
            xh = x * r
            dxh = dy * g
            dg_ref[...] += jnp.sum(dy * xh, axis=0, keepdims=True)
            return r * (dxh - xh * jnp.mean(dxh * xh, axis=-1, keepdims=True))

        dql = norm_bwd(p_ref[:, 0:MLA_Q_RANK], qg_ref[...], dqn_ref[...], dqg_ref)
        dp_ref[:, 0:MLA_Q_RANK] = dql.astype(BF16)
        dkl = norm_bwd(p_ref[:, MLA_Q_RANK:384], kg_ref[...], dk1_ref[...] + dk2_ref[...], dkg_ref)
        dp_ref[:, MLA_Q_RANK:384] = dkl.astype(BF16)
        dp_ref[:, 384:512] = dkr_ref[...].astype(BF16)
        ddv = dd_ref[...]
        for g, w in enumerate(POOL_WINDOWS):
            lo = g * POOL_GROUP
            ext[0:tm, lo:lo + POOL_GROUP] = ddv[:, lo:lo + POOL_GROUP] / _pool_counts(i * tm, tm, w)
            nxt = ddn_ref[:, lo:lo + POOL_GROUP] / _pool_counts((i + 1) * tm, POOL_HALO, w)
            ext[tm:tm + POOL_HALO, lo:lo + POOL_GROUP] = jnp.where(i < nt - 1, nxt, 0.0)
        for g, w in enumerate(POOL_WINDOWS):
            lo = g * POOL_GROUP
            acc = ext[0:tm, lo:lo + POOL_GROUP]
            for s in range(1, w):
                acc = acc + ext[s:s + tm, lo:lo + POOL_GROUP]
            dp_ref[:, 512 + lo:512 + lo + POOL_GROUP] = (acc - ddv[:, lo:lo + POOL_GROUP]).astype(BF16)
        dp_ref[:, 1024:2048] = dz_ref[...]

    row = lambda w: pl.BlockSpec((tm, w), lambda i: (i, 0))
    vec = lambda w: pl.BlockSpec((1, w), lambda i: (0, 0))
    return pl.pallas_call(
        body, name=name, grid=(nt,),
        in_specs=[row(1024), vec(MLA_Q_RANK), vec(MLA_KV_RANK), row(MLA_Q_RANK), row(MLA_KV_RANK), row(MLA_KV_RANK),
                  row(LANES), row(POOL_WIDTH),
                  pl.BlockSpec((POOL_HALO, POOL_WIDTH), lambda i: (jnp.minimum((i + 1) * hb, last_halo), 0)),
                  row(1024)],
        out_specs=[row(IN_AB_PAD), vec(MLA_Q_RANK), vec(MLA_KV_RANK)],
        out_shape=[jax.ShapeDtypeStruct((t, IN_AB_PAD), BF16), jax.ShapeDtypeStruct((1, MLA_Q_RANK), F32),
                   jax.ShapeDtypeStruct((1, MLA_KV_RANK), F32)],
        scratch_shapes=[pltpu.VMEM((tm + POOL_HALO, POOL_WIDTH), F32)],
        compiler_params=_params(("arbitrary",)),
    )(proj, q_a_norm, kv_a_norm, dqn, dkvn_k, dkvn_v, dkr, dd, dd, dz)


def _gate_fwd(o, ybraw, proj, pool_scale, *, name):
    t = o.shape[0]
    tm = _tile(t, 2 * ROW_TILE)

    def body(o_ref, yb_ref, z_ref, ps_ref, y_ref):
        z = z_ref[...]
        sz = z * _sigmoid(z)
        y_ref[:, 0:512] = (o_ref[...] * sz[:, 0:512]).astype(BF16)
        y_ref[:, 512:1024] = (yb_ref[...] * ps_ref[...] * sz[:, 512:1024]).astype(BF16)

    row = lambda w: pl.BlockSpec((tm, w), lambda i: (i, 0))
    return pl.pallas_call(
        body, name=name, grid=(t // tm,),
        in_specs=[row(512), row(512), pl.BlockSpec((tm, 1024), lambda i: (i, 1)), pl.BlockSpec((1, 512), lambda i: (0, 0))],
        out_specs=row(1024), out_shape=jax.ShapeDtypeStruct((t, 1024), BF16), compiler_params=_params(("parallel",)),
    )(o, ybraw, proj, pool_scale)


def _gate_bwd(dy, o, ybraw, proj, pool_scale, *, name):
    t = o.shape[0]
    tm = _tile(t, ROW_TILE)

    def body(dy_ref, o_ref, yb_ref, z_ref, ps_ref, do_ref, dl_ref, dyb_ref, dz_ref, dps_ref):
        i = pl.program_id(0)
        z = z_ref[...]
        sg = _sigmoid(z)
        sz = z * sg
        dsz = sg * (1.0 + z * (1.0 - sg))
        dyv = dy_ref[...]
        dcat = dyv * sz
        ov = o_ref[...]
        ybs = yb_ref[...] * ps_ref[...]
        dz_ref[:, 0:512] = (dyv[:, 0:512] * ov * dsz[:, 0:512]).astype(BF16)
        dz_ref[:, 512:1024] = (dyv[:, 512:1024] * ybs * dsz[:, 512:1024]).astype(BF16)
        do = dcat[:, 0:512]
        do_ref[...] = do.astype(BF16)
        r_i = lax.broadcasted_iota(jnp.int32, (512, 512), 0) // MLA_V
        c_i = lax.broadcasted_iota(jnp.int32, (512, 512), 1) // MLA_V
        dl_ref[...] = _dot(do * ov, (r_i == c_i).astype(F32), NN, HI)
        dyb_ref[...] = (dcat[:, 512:1024] * ps_ref[...]).astype(BF16)

        @pl.when(i == 0)
        def _():
            dps_ref[...] = jnp.zeros_like(dps_ref)

        dps_ref[...] += jnp.sum(dcat[:, 512:1024] * yb_ref[...], axis=0, keepdims=True)

    row = lambda w: pl.BlockSpec((tm, w), lambda i: (i, 0))
    vec = pl.BlockSpec((1, 512), lambda i: (0, 0))
    return pl.pallas_call(
        body, name=name, grid=(t // tm,),
        in_specs=[row(1024), row(512), row(512), pl.BlockSpec((tm, 1024), lambda i: (i, 1)), vec],
        out_specs=[row(512), row(512), row(512), row(1024), vec],
        out_shape=[jax.ShapeDtypeStruct((t, 512), BF16), jax.ShapeDtypeStruct((t, 512), F32),
                   jax.ShapeDtypeStruct((t, 512), BF16), jax.ShapeDtypeStruct((t, 1024), BF16),
                   jax.ShapeDtypeStruct((1, 512), F32)],
        compiler_params=_params(("arbitrary",)),
    )(dy, o, ybraw, proj, pool_scale)


def _causal_mask(qi, ki, tq, tk):
    row = qi * tq + lax.broadcasted_iota(jnp.int32, (tq, tk), 0)
    col = ki * tk + lax.broadcasted_iota(jnp.int32, (tq, tk), 1)
    return col <= row


def _attn_fwd(q, k, v, *, name):
    t = q.shape[0]
    tq = _tile(t, ATT_TILE)
    nq = t // tq

    def body(q_ref, k_ref, v_ref, o_ref, lse_ref, m_sc, l_sc, acc_sc):
        qi, ki = pl.program_id(1), pl.program_id(2)

        @pl.when(ki == 0)
        def _():
            m_sc[...] = jnp.full_like(m_sc, -jnp.inf)
            l_sc[...] = jnp.zeros_like(l_sc)
            acc_sc[...] = jnp.zeros_like(acc_sc)

        @pl.when(ki <= qi)
        def _():
            mask = _causal_mask(qi, ki, tq, tq)
            vv = v_ref[...]
            for a in range(2):
                sl = slice(a * LANES, (a + 1) * LANES)
                s = _dot(q_ref[:, sl], k_ref[:, sl], NT) * ATT_SCALE
                s = jnp.where(mask, s, -jnp.inf)
                m_prev = m_sc[a]
                m_new = jnp.maximum(m_prev, jnp.max(s, axis=-1, keepdims=True))
                alpha = jnp.exp(m_prev - m_new)
                p = jnp.exp(s - m_new[:, 0:1])
                l_sc[a] = alpha * l_sc[a] + jnp.sum(p, axis=-1, keepdims=True)
                acc_sc[a] = alpha * acc_sc[a] + _dot(p.astype(BF16), vv)
                m_sc[a] = m_new

        @pl.when(ki == qi)
        def _():
            lane = lax.broadcasted_iota(jnp.int32, (tq, LANES), 1)
            first = lane < MLA_V
            o_ref[...] = jnp.where(first, acc_sc[0] / l_sc[0], acc_sc[1] / l_sc[1])
            lse_ref[...] = jnp.where(first, m_sc[0] + jnp.log(l_sc[0]), m_sc[1] + jnp.log(l_sc[1]))

    return pl.pallas_call(
        body, name=name, grid=(MLA_HEADS // 2, nq, nq),
        in_specs=[pl.BlockSpec((tq, 2 * LANES), lambda h, i, j: (i, h)),
                  pl.BlockSpec((tq, 2 * LANES), lambda h, i, j: (jnp.minimum(i, j), h)),
                  pl.BlockSpec((tq, LANES), lambda h, i, j: (jnp.minimum(i, j), h))],
        out_specs=[pl.BlockSpec((tq, LANES), lambda h, i, j: (i, h)), pl.BlockSpec((tq, LANES), lambda h, i, j: (i, h))],
        out_shape=[jax.ShapeDtypeStruct((t, 512), F32), jax.ShapeDtypeStruct((t, 512), F32)],
        scratch_shapes=[pltpu.VMEM((2, tq, LANES), F32), pltpu.VMEM((2, tq, LANES), F32), pltpu.VMEM((2, tq, LANES), F32)],
        compiler_params=_params(("parallel", "parallel", "arbitrary")),
    )(q, k, v)


def _attn_bwd(q, k, v, do, lse, delta, *, name):
    t = q.shape[0]
    tq = _tile(t, ATT_TILE)
    nq = t // tq

    def body(q_ref, k_ref, v_ref, do_ref, lse_ref, dl_ref, dq_ref, dk_ref, dv_ref, dk_sc, dv_sc):
        ki, qi = pl.program_id(1), pl.program_id(2)

        @pl.when((ki == 0) & (qi == 0))
        def _():
            dq_ref[...] = jnp.zeros_like(dq_ref)

        @pl.when(qi == 0)
        def _():
            dk_sc[...] = jnp.zeros_like(dk_sc)
            dv_sc[...] = jnp.zeros_like(dv_sc)

        @pl.when(qi >= ki)
        def _():
            mask = _causal_mask(qi, ki, tq, tq)
            lane = lax.broadcasted_iota(jnp.int32, (tq, LANES), 1)
            vv = v_ref[...]
            dov = do_ref[...]
            rows = pl.ds(pl.multiple_of(qi * tq, tq), tq)
            for a in range(2):
                sl = slice(a * LANES, (a + 1) * LANES)
                mine = (lane < MLA_V) if a == 0 else (lane >= MLA_V)
                col = a * MLA_V
                qa, ka = q_ref[:, sl], k_ref[:, sl]
                s = _dot(qa, ka, NT) * ATT_SCALE
                p = jnp.where(mask, jnp.exp(s - lse_ref[:, col:col + 1]), 0.0)
                pb = p.astype(BF16)
                dv_sc[a] += _dot(pb, dov, TN)
                dp = _dot(jnp.where(mine, dov, jnp.zeros_like(dov)), vv, NT)
                ds = (p * (dp - dl_ref[:, col:col + 1]) * ATT_SCALE).astype(BF16)
                dk_sc[a] += _dot(ds, qa, TN)
                dq_ref[rows, sl] += _dot(ds, ka, NN)

        @pl.when(qi == nq - 1)
        def _():
            lane = lax.broadcasted_iota(jnp.int32, (tq, LANES), 1)
            dk_ref[:, 0:LANES] = dk_sc[0]
            dk_ref[:, LANES:2 * LANES] = dk_sc[1]
            dv_ref[...] = jnp.where(lane < MLA_V, dv_sc[0], dv_sc[1]).astype(BF16)

    qrow = lambda w: pl.BlockSpec((tq, w), lambda h, j, i: (jnp.maximum(i, j), h))
    krow = lambda w: pl.BlockSpec((tq, w), lambda h, j, i: (j, h))
    return pl.pallas_call(
        body, name=name, grid=(MLA_HEADS // 2, nq, nq),
        in_specs=[qrow(2 * LANES), krow(2 * LANES), krow(LANES), qrow(LANES), qrow(LANES), qrow(LANES)],
        out_specs=[pl.BlockSpec((t, 2 * LANES), lambda h, j, i: (0, h)), krow(2 * LANES), krow(LANES)],
        out_shape=[jax.ShapeDtypeStruct((t, 1024), F32), jax.ShapeDtypeStruct((t, 1024), F32),
                   jax.ShapeDtypeStruct((t, 512), BF16)],
        scratch_shapes=[pltpu.VMEM((2, tq, LANES), F32), pltpu.VMEM((2, tq, LANES), F32)],
        compiler_params=_params(("parallel", "arbitrary", "arbitrary")),
    )(q, k, v, do, lse, delta)


def _conv_rows(ext, tm, w_ref, sec):
    c0 = sec * 1024
    y = ext[CONV_HALO - 3:CONV_HALO - 3 + tm, c0:c0 + 1024] * w_ref[0:1, c0:c0 + 1024]
    for j in range(1, CONV_WIDTH):
        y = y + ext[CONV_HALO - 3 + j:CONV_HALO - 3 + j + tm, c0:c0 + 1024] * w_ref[j:j + 1, c0:c0 + 1024]
    return y


def _c_prep(proj_c, conv_w, a_log, dt_bias, *, name):
    t = proj_c.shape[0]
    tm = _tile(t, ROW_TILE)
    hb = tm // CONV_HALO

    def body(p_ref, halo_ref, ab_ref, w_ref, al_ref, dtb_ref, q_ref, k_ref, v_ref, g_ref, b_ref, ext):
        i = pl.program_id(0)
        ext[0:CONV_HALO, :] = jnp.where(i > 0, halo_ref[...], 0.0)
        ext[CONV_HALO:CONV_HALO + tm, :] = p_ref[...]
        for sec, o_ref in enumerate((q_ref, k_ref, v_ref)):
            y = _conv_rows(ext, tm, w_ref, sec)
            y = y * _sigmoid(y)
            if sec == 2:
                o_ref[...] = y
                continue
            scale = GDN_DK ** -0.5 if sec == 0 else 1.0
            for h in range(GDN_HEADS):
                sl = slice(h * LANES, (h + 1) * LANES)
                blk = y[:, sl]
                r = lax.rsqrt(jnp.sum(blk * blk, axis=-1, keepdims=True) + RMS_EPS)
                o_ref[:, sl] = blk * (r * scale)
        ab = ab_ref[...]
        g = -jnp.exp(al_ref[...]) * _softplus(ab + dtb_ref[...])
        beta = _sigmoid(ab)
        for h in range(GDN_HEADS):
            sl = slice(h * LANES, (h + 1) * LANES)
            g_ref[:, sl] = jnp.broadcast_to(g[:, h:h + 1], (tm, LANES))
            b_ref[:, sl] = jnp.broadcast_to(beta[:, GDN_HEADS + h:GDN_HEADS + h + 1], (tm, LANES))

    row = lambda w: pl.BlockSpec((tm, w), lambda i: (i, 0))
    vec = lambda r, w: pl.BlockSpec((r, w), lambda i: (0, 0))
    out = jax.ShapeDtypeStruct((t, 1024), F32)
    return pl.pallas_call(
        body, name=name, grid=(t // tm,),
        in_specs=[row(3072), pl.BlockSpec((CONV_HALO, 3072), lambda i: (jnp.maximum(i * hb - 1, 0), 0)),
                  pl.BlockSpec((tm, LANES), lambda i: (i, 32)), vec(CONV_WIDTH, 3072), vec(1, LANES), vec(1, LANES)],
        out_specs=[row(1024)] * 5, out_shape=[out] * 5,
        scratch_shapes=[pltpu.VMEM((tm + CONV_HALO, 3072), F32)],
        compiler_params=_params(("parallel",)),
    )(proj_c, proj_c, proj_c, conv_w, a_log, dt_bias)


def _c_prep_bwd(proj_c, conv_w, a_log, dt_bias, dq, dk, dv, dgb, dbb, dz, *, name):
    t = proj_c.shape[0]
    tm = _tile(t, ROW_TILE // 2)
    hb = tm // CONV_HALO
    nt = t // tm
    rev = lambda i: nt - 1 - i

    def body(p_ref, halo_ref, ab_ref, w_ref, al_ref, dtb_ref, dq_ref, dk_ref, dv_ref, dg_ref, db_ref, dz_ref,
             dp_ref, dw_ref, dal_ref, ddt_ref, ext, dyext, carry):
        step = pl.program_id(0)
        i = rev(step)

        @pl.when(step == 0)
        def _():
            dw_ref[...] = jnp.zeros_like(dw_ref)
            dal_ref[...] = jnp.zeros_like(dal_ref)
            ddt_ref[...] = jnp.zeros_like(ddt_ref)
            carry[...] = jnp.zeros_like(carry)

        ext[0:CONV_HALO, :] = jnp.where(i > 0, halo_ref[...], 0.0)
        ext[CONV_HALO:CONV_HALO + tm, :] = p_ref[...]
        for sec, g_ref in enumerate((dq_ref, dk_ref, dv_ref)):
            c0 = sec * 1024
            y = _conv_rows(ext, tm, w_ref, sec)
            sg = _sigmoid(y)
            act = y * sg
            if sec == 2:
                dact = g_ref[...]
            else:
                scale = GDN_DK ** -0.5 if sec == 0 else 1.0
                parts = []
                for h in range(GDN_HEADS):
                    sl = slice(h * LANES, (h + 1) * LANES)
                    blk = act[:, sl]
                    r = lax.rsqrt(jnp.sum(blk * blk, axis=-1, keepdims=True) + RMS_EPS)
                    n = blk * r
                    dn = g_ref[:, sl] * scale
                    parts.append(r * (dn - n * jnp.sum(dn * n, axis=-1, keepdims=True)))
                dact = jnp.concatenate(parts, axis=-1)
            dy = dact * (sg * (1.0 + y * (1.0 - sg)))
            dyext[0:tm, c0:c0 + 1024] = dy
            for j in range(CONV_WIDTH):
                xs = ext[CONV_HALO - 3 + j:CONV_HALO - 3 + j + tm, c0:c0 + 1024]
                dw_ref[j:j + 1, c0:c0 + 1024] += jnp.sum(dy * xs, axis=0, keepdims=True)
        dyext[tm:tm + CONV_HALO, :] = carry[...]
        carry[...] = dyext[0:CONV_HALO, :]
        for sec in range(3):
            c0 = sec * 1024
            dx = dyext[3:3 + tm, c0:c0 + 1024] * w_ref[0:1, c0:c0 + 1024]
            for j in range(1, CONV_WIDTH):
                dx = dx + dyext[3 - j:3 - j + tm, c0:c0 + 1024] * w_ref[j:j + 1, c0:c0 + 1024]
            dp_ref[:, c0:c0 + 1024] = dx.astype(BF16)
        dp_ref[:, 3072:4096] = dz_ref[...]
        lane = lax.broadcasted_iota(jnp.int32, (tm, LANES), 1)
        dg = jnp.zeros((tm, LANES), F32)
        dbeta = jnp.zeros((tm, LANES), F32)
        for h in range(GDN_HEADS):
            sl = slice(h * LANES, (h + 1) * LANES)
            dg = dg + jnp.where(lane == h, dg_ref[:, sl], 0.0)
            dbeta = dbeta + jnp.where(lane == GDN_HEADS + h, db_ref[:, sl], 0.0)
        pre = ab_ref[...] + dtb_ref[...]
        s = _sigmoid(pre)
        a_exp = jnp.exp(al_ref[...])
        dg_da = dg * (-a_exp * s)
        dp_ref[:, 4096:IN_C_PAD] = (dg_da + dbeta * s * (1.0 - s)).astype(BF16)
        dal_ref[...] += jnp.sum(dg * (-a_exp * _softplus(pre)), axis=0, keepdims=True)
        ddt_ref[...] += jnp.sum(dg_da, axis=0, keepdims=True)

    row = lambda w: pl.BlockSpec((tm, w), lambda s: (rev(s), 0))
    vec = lambda r, w: pl.BlockSpec((r, w), lambda s: (0, 0))
    return pl.pallas_call(
        body, name=name, grid=(nt,),
        in_specs=[row(3072), pl.BlockSpec((CONV_HALO, 3072), lambda s: (jnp.maximum(rev(s) * hb - 1, 0), 0)),
                  pl.BlockSpec((tm, LANES), lambda s: (rev(s), 32)), vec(CONV_WIDTH, 3072), vec(1, LANES), vec(1, LANES),
                  row(1024), row(1024), row(1024), row(1024), row(1024), row(1024)],
        out_specs=[row(IN_C_PAD), vec(CONV_WIDTH, 3072), vec(1, LANES), vec(1, LANES)],
        out_shape=[jax.ShapeDtypeStruct((t, IN_C_PAD), BF16), jax.ShapeDtypeStruct((CONV_WIDTH, 3072), F32),
                   jax.ShapeDtypeStruct((1, LANES), F32), jax.ShapeDtypeStruct((1, LANES), F32)],
        scratch_shapes=[pltpu.VMEM((tm + CONV_HALO, 3072), F32), pltpu.VMEM((tm + CONV_HALO, 3072), F32),
                        pltpu.VMEM((CONV_HALO, 3072), F32)],
        compiler_params=_params(("arbitrary",)),
    )(proj_c, proj_c, proj_c, conv_w, a_log, dt_bias, dq, dk, dv, dgb, dbb, dz)


def _o_gate_fwd(o, proj_c, o_norm, *, name):
    t = o.shape[0]
    tm = _tile(t, 2 * ROW_TILE)

    def body(o_ref, z_ref, g_ref, y_ref):
        for h in range(GDN_HEADS):
            sl = slice(h * LANES, (h + 1) * LANES)
            x = o_ref[:, sl]
            r = lax.rsqrt(jnp.mean(x * x, axis=-1, keepdims=True) + RMS_EPS)
            z = z_ref[:, sl]
            y_ref[:, sl] = (x * r * g_ref[...] * (z * _sigmoid(z))).astype(BF16)

    row = pl.BlockSpec((tm, 1024), lambda i: (i, 0))
    return pl.pallas_call(
        body, name=name, grid=(t // tm,),
        in_specs=[row, pl.BlockSpec((tm, 1024), lambda i: (i, 3)), pl.BlockSpec((1, LANES), lambda i: (0, 0))],
        out_specs=row, out_shape=jax.ShapeDtypeStruct((t, 1024), BF16), compiler_params=_params(("parallel",)),
    )(o, proj_c, o_norm)


def _o_gate_bwd(dy, o, proj_c, o_norm, *, name):
    t = o.shape[0]
    tm = _tile(t, 2 * ROW_TILE)

    def body(dy_ref, o_ref, z_ref, g_ref, do_ref, dz_ref, dg_ref):
        i = pl.program_id(0)

        @pl.when(i == 0)
        def _():
            dg_ref[...] = jnp.zeros_like(dg_ref)

        dg = jnp.zeros((1, LANES), F32)
        for h in range(GDN_HEADS):
            sl = slice(h * LANES, (h + 1) * LANES)
            x = o_ref[:, sl]
            r = lax.rsqrt(jnp.mean(x * x, axis=-1, keepdims=True) + RMS_EPS)
            xh = x * r
            z = z_ref[:, sl]
            sg = _sigmoid(z)
            dyv = dy_ref[:, sl]
            dn = dyv * (z * sg)
            dz_ref[:, sl] = (dyv * xh * g_ref[...] * (sg * (1.0 + z * (1.0 - sg)))).astype(BF16)
            dxh = dn * g_ref[...]
            do_ref[:, sl] = r * (dxh - xh * jnp.mean(dxh * xh, axis=-1, keepdims=True))
            dg = dg + jnp.sum(dn * xh, axis=0, keepdims=True)
        dg_ref[...] += dg

    row = pl.BlockSpec((tm, 1024), lambda i: (i, 0))
    vec = pl.BlockSpec((1, LANES), lambda i: (0, 0))
    return pl.pallas_call(
        body, name=name, grid=(t // tm,), in_specs=[row, row, pl.BlockSpec((tm, 1024), lambda i: (i, 3)), vec],
        out_specs=[row, row, vec],
        out_shape=[jax.ShapeDtypeStruct((t, 1024), F32), jax.ShapeDtypeStruct((t, 1024), BF16),
                   jax.ShapeDtypeStruct((1, LANES), F32)],
        compiler_params=_params(("arbitrary",)),
    )(dy, o, proj_c, o_norm)


def _chunk_common(q, k, v, g, beta):
    c = CHUNK
    ri = lax.broadcasted_iota(jnp.int32, (c, c), 0)
    ci = lax.broadcasted_iota(jnp.int32, (c, c), 1)
    incl = ri >= ci
    strict = ri > ci
    ones = jnp.ones((c, LANES), F32)
    gc = _dot(incl.astype(F32), g, NN, HI)
    gc_i = _dot(gc, ones, NT, HI) * (1.0 / LANES)
    gc_j = _dot(ones, gc, NT, HI) * (1.0 / LANES)
    gamma = jnp.where(incl, jnp.exp(jnp.minimum(gc_i - gc_j, 0.0)), 0.0)
    kb = k * beta
    m = jnp.where(strict, _dot(kb, k, NT, HI) * gamma, 0.0)
    eye = (ri == ci).astype(F32)
    tm_ = eye - m
    pw = _dot(m, m, NN, HI)
    for it in range(5):
        tm_ = tm_ + _dot(tm_, pw, NN, HI)
        if it < 4:
            pw = _dot(pw, pw, NN, HI)
    eg = jnp.exp(gc)
    vb = v * beta
    kbe = kb * eg
    u = _dot(tm_, vb, NN, HI)
    w = _dot(tm_, kbe, NN, HI)
    attn = jnp.where(incl, _dot(q, k, NT, HI) * gamma, 0.0)
    gl_row = gc[c - 1:c, :]
    ek = jnp.exp(gl_row - gc)
    return dict(incl=incl, strict=strict, gc=gc, gamma=gamma, kb=kb, m=m, tm=tm_, eg=eg, vb=vb, kbe=kbe, u=u, w=w,
                attn=attn, qd=q * eg, ek=ek, kd=k * ek, glast=jnp.exp(gl_row))


def _gdn_fwd(q, k, v, gb, bb, *, name):
    t = q.shape[0]
    ts = _tile(t, GDN_TILE)
    nc = ts // CHUNK

    def body(q_ref, k_ref, v_ref, g_ref, b_ref, o_ref, st_ref, s_sc):
        @pl.when(pl.program_id(1) == 0)
        def _():
            s_sc[...] = jnp.zeros_like(s_sc)

        def chunk(ci, _):
            rows = pl.ds(pl.multiple_of(ci * CHUNK, CHUNK), CHUNK)
            cm = _chunk_common(q_ref[rows, :], k_ref[rows, :], v_ref[rows, :], g_ref[rows, :], b_ref[rows, :])
            s = s_sc[...]
            st_ref[0, ci] = s
            vn = cm["u"] - _dot(cm["w"], s, NN, HI)
            o_ref[rows, :] = _dot(cm["qd"], s, NN, HI) + _dot(cm["attn"], vn, NN, HI)
            s_sc[...] = s * cm["glast"] + _dot(cm["kd"], vn, TN, HI)
            return 0

        lax.fori_loop(0, nc, chunk, 0)

    blk = pl.BlockSpec((ts, LANES), lambda h, s: (s, h))
    return pl.pallas_call(
        body, name=name, grid=(GDN_HEADS, t // ts), in_specs=[blk] * 5,
        out_specs=[blk, pl.BlockSpec((1, nc, LANES, LANES), lambda h, s: (h, s, 0, 0))],
        out_shape=[jax.ShapeDtypeStruct((t, 1024), F32), jax.ShapeDtypeStruct((GDN_HEADS, t // CHUNK, LANES, LANES), F32)],
        scratch_shapes=[pltpu.VMEM((LANES, LANES), F32)],
        compiler_params=_params(("parallel", "arbitrary")),
    )(q, k, v, gb, bb)


def _gdn_bwd(q, k, v, gb, bb, do, states, *, name):
    t = q.shape[0]
    ts = _tile(t, GDN_TILE)
    nc = ts // CHUNK
    ns = t // ts
    c = CHUNK

    def body(q_ref, k_ref, v_ref, g_ref, b_ref, do_ref, st_ref, dq_ref, dk_ref, dv_ref, dg_ref, db_ref, ds_sc):
        @pl.when(pl.program_id(1) == 0)
        def _():
            ds_sc[...] = jnp.zeros_like(ds_sc)

        ones = jnp.ones((c, LANES), F32)
        rowsum = lambda x: jnp.sum(x, axis=-1, keepdims=True)

        def chunk(step, _):
            ci = nc - 1 - step
            rows = pl.ds(pl.multiple_of(ci * CHUNK, CHUNK), CHUNK)
            qv, kv, vv, beta = q_ref[rows, :], k_ref[rows, :], v_ref[rows, :], b_ref[rows, :]
            cm = _chunk_common(qv, kv, vv, g_ref[rows, :], beta)
            dov = do_ref[rows, :]
            s = st_ref[0, ci]
            dsn = ds_sc[...]
            tmat, gamma, eg = cm["tm"], cm["gamma"], cm["eg"]
            vn = cm["u"] - _dot(cm["w"], s, NN, HI)
            dvn = _dot(cm["attn"], dov, TN, HI) + _dot(cm["kd"], dsn, NN, HI)
            dattn = jnp.where(cm["incl"], _dot(dov, vn, NT, HI), 0.0)
            dqd = _dot(dov, s, NT, HI)
            dkd = _dot(vn, dsn, NT, HI)
            dgl = jnp.sum(rowsum(dsn * s), axis=0, keepdims=True)
            dw = -_dot(dvn, s, NT, HI)
            ds_sc[...] = dsn * cm["glast"] + _dot(cm["qd"], dov, TN, HI) - _dot(cm["w"], dvn, TN, HI)
            dvb = _dot(tmat, dvn, TN, HI)
            dkbe = _dot(tmat, dw, TN, HI)
            dt_ = _dot(dvn, cm["vb"], NT, HI) + _dot(dw, cm["kbe"], NT, HI)
            da = -_dot(_dot(tmat, dt_, TN, HI), tmat, NT, HI)
            dm = jnp.where(cm["strict"], da, 0.0)
            dkk = dm * gamma
            dqk = dattn * gamma
            z = dm * cm["m"] + dattn * cm["attn"]
            dkb = _dot(dkk, kv, NN, HI) + dkbe * eg
            dk = _dot(dkk, cm["kb"], TN, HI) + _dot(dqk, qv, TN, HI) + dkd * cm["ek"] + dkb * beta
            dq = _dot(dqk, kv, NN, HI) + dqd * eg
            dgc = (_dot(z, ones, NN, HI) - _dot(z, ones, TN, HI)
                   + rowsum(dqd * cm["qd"]) - rowsum(dkd * cm["kd"]) + rowsum(dkbe * cm["kbe"]))
            last = jnp.sum(rowsum(dkd * cm["kd"]), axis=0, keepdims=True) + dgl * cm["glast"]
            ri = lax.broadcasted_iota(jnp.int32, (c, LANES), 0)
            dgc = dgc + jnp.where(ri == c - 1, last, 0.0)
            dq_ref[rows, :] = dq
            dk_ref[rows, :] = dk
            dv_ref[rows, :] = dvb * beta
            db_ref[rows, :] = jnp.broadcast_to(rowsum(dkb * kv) + rowsum(dvb * vv), (c, LANES))
            upper = lax.broadcasted_iota(jnp.int32, (c, c), 0) <= lax.broadcasted_iota(jnp.int32, (c, c), 1)
            dg_ref[rows, :] = _dot(upper.astype(F32), dgc, NN, HI)
            return 0

        lax.fori_loop(0, nc, chunk, 0)

    blk = pl.BlockSpec((ts, LANES), lambda h, s: (ns - 1 - s, h))
    out = jax.ShapeDtypeStruct((t, 1024), F32)
    return pl.pallas_call(
        body, name=name, grid=(GDN_HEADS, ns),
        in_specs=[blk] * 6 + [pl.BlockSpec((1, nc, LANES, LANES), lambda h, s: (h, ns - 1 - s, 0, 0))],
        out_specs=[blk] * 5, out_shape=[out] * 5, scratch_shapes=[pltpu.VMEM((LANES, LANES), F32)],
        compiler_params=_params(("parallel", "arbitrary")),
    )(q, k, v, gb, bb, do, states)


def _loss_head(h, g, target, *, name):
    t, d = h.shape
    tm = _tile(t, 2 * ROW_TILE)

    def body(h_ref, g_ref, t_ref, dh_ref, dhb_ref, dg_ref, loss_ref):
        i = pl.program_id(0)
        x = h_ref[...]
        r = lax.rsqrt(jnp.mean(x * x, axis=-1, keepdims=True) + RMS_EPS)
        xh = x * r
        err = xh * g_ref[...] - t_ref[...]
        dy = err * (1.0 / d)
        dxh = dy * g_ref[...]
        dh = r * (dxh - xh * jnp.mean(dxh * xh, axis=-1, keepdims=True))
        dh_ref[...] = dh
        dhb_ref[...] = dh.astype(BF16)

        @pl.when(i == 0)
        def _():
            dg_ref[...] = jnp.zeros_like(dg_ref)
            loss_ref[...] = jnp.zeros_like(loss_ref)

        dg_ref[...] += jnp.sum(dy * xh, axis=0, keepdims=True)
        part = 0.5 * jnp.sum(jnp.mean(err * err, axis=-1, keepdims=True), axis=0, keepdims=True)
        loss_ref[...] += jnp.broadcast_to(part, loss_ref.shape)

    row = pl.BlockSpec((tm, d), lambda i: (i, 0))
    vec = pl.BlockSpec((1, d), lambda i: (0, 0))
    return pl.pallas_call(
        body, name=name, grid=(t // tm,), in_specs=[row, vec, row],
        out_specs=[row, row, vec, pl.BlockSpec((8, LANES), lambda i: (0, 0))],
        out_shape=[jax.ShapeDtypeStruct((t, d), F32), jax.ShapeDtypeStruct((t, d), BF16),
                   jax.ShapeDtypeStruct((1, d), F32), jax.ShapeDtypeStruct((8, LANES), F32)],
        compiler_params=_params(("arbitrary",)),
    )(h, g, target)


def _pad_cols(w, n):
    return jnp.pad(w, ((0, 0), (0, n - w.shape[1])))


def _layout_weights(w):
    z = lambda r, c: jnp.zeros((r, c), F32)
    wi = w["w_in_ab"]
    win = jnp.concatenate([wi[:, :384], z(1024, 64), wi[:, 384:416], z(1024, 32), wi[:, 416:]], axis=1)
    wq = jnp.pad(w["w_q_b"].reshape(MLA_Q_RANK, MLA_HEADS, 96), ((0, 0), (0, 0), (0, 32))).reshape(MLA_Q_RANK, 1024)
    kv3 = w["w_kv_b"].reshape(MLA_KV_RANK, MLA_HEADS, 128)
    wk = jnp.pad(kv3[..., :MLA_NOPE], ((0, 0), (0, 0), (0, 64))).reshape(MLA_KV_RANK, 1024)
    wv = kv3[..., MLA_NOPE:].reshape(MLA_KV_RANK, 512)
    pw = w["pool_w"]
    rows = []
    for g in range(4):
        rows.append(jnp.concatenate([pw[g] if j == g else z(128, 128) for j in range(4)], axis=1))
    wpool = jnp.concatenate(rows, axis=0)
    half = MLA_ROPE // 2
    inv = 1.0 / (ROPE_THETA ** (jnp.arange(half, dtype=F32) / half))
    inv_lane = jnp.concatenate([jnp.zeros((MLA_NOPE,), F32), inv, inv, jnp.zeros((32,), F32)]).reshape(1, LANES)
    return dict(
        win=win.astype(BF16), wq=wq.astype(BF16), wk=wk.astype(BF16), wv=wv.astype(BF16), wpool=wpool.astype(BF16),
        wout_ab=w["w_out_ab"].astype(BF16), winc=_pad_cols(w["w_in_c"], IN_C_PAD).astype(BF16),
        wout_c=w["w_out_c"].astype(BF16), conv_w=w["conv_w"], a_log=_pad_cols(w["a_log"], LANES),
        dt_bias=_pad_cols(w["dt_bias"], LANES), inv_lane=inv_lane,
        norm_ab=w["norm_ab"], q_a_norm=w["q_a_norm"], kv_a_norm=w["kv_a_norm"], pool_scale=w["pool_scale"],
        norm_c=w["norm_c"], o_norm=w["o_norm"], final_norm=w["final_norm"],
    )


def _unlayout_grads(g):
    dwin = g["win"]
    dkv = jnp.concatenate([g["wk"].reshape(MLA_KV_RANK, MLA_HEADS, 128)[..., :MLA_NOPE],
                           g["wv"].reshape(MLA_KV_RANK, MLA_HEADS, MLA_V)], axis=-1).reshape(MLA_KV_RANK, 1024)
    return dict(
        norm_ab=g["norm_ab"],
        w_in_ab=jnp.concatenate([dwin[:, :384], dwin[:, 448:480], dwin[:, 512:]], axis=1),
        q_a_norm=g["q_a_norm"],
        w_q_b=g["wq"].reshape(MLA_Q_RANK, MLA_HEADS, 128)[..., :96].reshape(MLA_Q_RANK, 768),
        kv_a_norm=g["kv_a_norm"],
        w_kv_b=dkv,
        pool_w=jnp.stack([g["wpool"][i * 128:(i + 1) * 128, i * 128:(i + 1) * 128] for i in range(4)]),
        pool_scale=g["pool_scale"],
        w_out_ab=g["wout_ab"],
        norm_c=g["norm_c"],
        w_in_c=g["winc"][:, :4112],
        conv_w=g["conv_w"],
        a_log=g["a_log"][:, :GDN_HEADS],
        dt_bias=g["dt_bias"][:, :GDN_HEADS],
        o_norm=g["o_norm"],
        w_out_c=g["wout_c"],
        final_norm=g["final_norm"],
    )


def _local_step(x, pos, target, lw):
    mm = _matmul
    hn = _rms_fwd(x, lw["norm_ab"], name="rms_ab")
    proj = mm(hn, lw["win"], "nn", name="in_ab")
    qn, kvn, kr, d, cos_t, sin_t = _ab_prep(proj, pos, lw["inv_lane"], lw["q_a_norm"], lw["kv_a_norm"], name="ab_prep")
    qraw = mm(qn, lw["wq"], "nn", name="q_up")
    kvk = mm(kvn, lw["wk"], "nn", name="k_up")
    v = mm(kvn, lw["wv"], "nn", name="v_up", out_dtype=BF16)
    ybraw = mm(d, lw["wpool"], "nn", name="pool_mix")
    q, k = _qk_rope(qraw, kvk, kr, cos_t, sin_t, name="qk_rope")
    o, lse = _attn_fwd(q, k, v, name="attn_fwd")
    y = _gate_fwd(o, ybraw, proj, lw["pool_scale"], name="gate_ab")
    h1 = mm(y, lw["wout_ab"], "nn", name="out_ab", add=x)
    hn1 = _rms_fwd(h1, lw["norm_c"], name="rms_c")
    proj_c = mm(hn1, lw["winc"], "nn", name="in_c")
    q2, k2, v2, gb, bb = _c_prep(proj_c, lw["conv_w"], lw["a_log"], lw["dt_bias"], name="c_prep")
    o2, states = _gdn_fwd(q2, k2, v2, gb, bb, name="gdn_fwd")
    y2 = _o_gate_fwd(o2, proj_c, lw["o_norm"], name="gate_c")
    h2 = mm(y2, lw["wout_c"], "nn", name="out_c", add=h1)
    dh2, dh2b, d_final, loss = _loss_head(h2, lw["final_norm"], target, name="loss_head")
    g = {"final_norm": d_final}
    dy2 = mm(dh2b, lw["wout_c"], "nt", name="out_c_dx")
    g["wout_c"] = mm(y2, dh2b, "tn", name="out_c_dw")
    do2, dz2, g["o_norm"] = _o_gate_bwd(dy2, o2, proj_c, lw["o_norm"], name="gate_c_bwd")
    dq2, dk2, dv2, dgb, dbb = _gdn_bwd(q2, k2, v2, gb, bb, do2, states, name="gdn_bwd")
    dproj_c, g["conv_w"], g["a_log"], g["dt_bias"] = _c_prep_bwd(
        proj_c, lw["conv_w"], lw["a_log"], lw["dt_bias"], dq2, dk2, dv2, dgb, dbb, dz2, name="c_prep_bwd")
    dhn1 = mm(dproj_c, lw["winc"], "nt", name="in_c_dx")
    g["winc"] = mm(hn1, dproj_c, "tn", name="in_c_dw")
    dh1, dh1b, g["norm_c"] = _rms_bwd(h1, lw["norm_c"], dhn1, dh2, name="rms_c_bwd", with_bf16=True)
    dy = mm(dh1b, lw["wout_ab"], "nt", name="out_ab_dx")
    g["wout_ab"] = mm(y, dh1b, "tn", name="out_ab_dw")
    do, delta, dyb, dz, g["pool_scale"] = _gate_bwd(dy, o, ybraw, proj, lw["pool_scale"], name="gate_ab_bwd")
    dq, dk, dv = _attn_bwd(q, k, v, do, lse, delta, name="attn_bwd")
    dd = mm(dyb, lw["wpool"], "nt", name="pool_mix_dx")
    g["wpool"] = mm(d, dyb, "tn", name="pool_mix_dw")
    dqraw, dkb, dkr = _qk_rope_bwd(dq, dk, cos_t, sin_t, name="qk_rope_bwd")
    dqn = mm(dqraw, lw["wq"], "nt", name="q_up_dx")
    g["wq"] = mm(qn, dqraw, "tn", name="q_up_dw")
    dkvn_k = mm(dkb, lw["wk"], "nt", name="k_up_dx")
    dkvn_v = mm(dv, lw["wv"], "nt", name="v_up_dx")
    g["wk"] = mm(kvn, dkb, "tn", name="k_up_dw")
    g["wv"] = mm(kvn, dv, "tn", name="v_up_dw")
    dproj, g["q_a_norm"], g["kv_a_norm"] = _ab_prep_bwd(
        proj, lw["q_a_norm"], lw["kv_a_norm"], dqn, dkvn_k, dkvn_v, dkr, dd, dz, name="ab_prep_bwd")
    dhn = mm(dproj, lw["win"], "nt", name="in_ab_dx")
    g["win"] = mm(hn, dproj, "tn", name="in_ab_dw")
    dx, g["norm_ab"] = _rms_bwd(x, lw["norm_ab"], dhn, dh1, name="rms_ab_bwd", with_bf16=False)
    return loss, dx, g


_HBM = pl.BlockSpec(memory_space=pltpu.HBM)


def _place():
    return lax.axis_index("x"), lax.axis_index("y"), lax.axis_index("c")


def _flip(v, f):
    return 1 - v if f else v


_CHIP_FLIPS = ((1, 0), (0, 1), (1, 1))
_DEV_FLIPS = tuple((fx, fy, fc) for fx in (0, 1) for fy in (0, 1) for fc in (0, 1) if fx or fy or fc)


def _rcopy(src, dst, send_sems, recv_sems, k, to):
    return pltpu.make_async_remote_copy(src_ref=src, dst_ref=dst, send_sem=send_sems.at[k], recv_sem=recv_sems.at[k],
                                        device_id=to, device_id_type=MESH)


def _gather_weights(wb, ws):
    _, rh, _ = wb.shape
    rs = ws.shape[0]

    def body(wb_ref, ws_ref, gb_ref, gs_ref, send_sems, recv_sems, local_sems):
        x, y, c = _place()
        j0 = 2 * x + y
        sib = (x, y, 1 - c)
        chips = [(_flip(x, fx), _flip(y, fy)) for fx, fy in _CHIP_FLIPS]
        own_b = pltpu.make_async_copy(wb_ref, gb_ref.at[j0], local_sems.at[0])
        own_s = pltpu.make_async_copy(ws_ref, gs_ref.at[j0], local_sems.at[1])
        own_b.start()
        own_s.start()
        sends = []
        for k, (px, py) in enumerate(chips):
            sends.append(_rcopy(wb_ref.at[c], gb_ref.at[j0, c], send_sems, recv_sems, k, (px, py, c)))
            sends.append(_rcopy(ws_ref, gs_ref.at[j0], send_sems, recv_sems, 6 + k, (px, py, c)))
        for cp in sends:
            cp.start()
        for k, (px, py) in enumerate(chips):
            jk = 2 * px + py
            _rcopy(wb_ref.at[c], gb_ref.at[jk, c], send_sems, recv_sems, k, (px, py, c)).wait_recv()
            fwd = _rcopy(gb_ref.at[jk, c], gb_ref.at[jk, c], send_sems, recv_sems, 3 + k, sib)
            fwd.start()
            sends.append(fwd)
        for k, (px, py) in enumerate(chips):
            jk = 2 * px + py
            _rcopy(wb_ref.at[c], gb_ref.at[jk, 1 - c], send_sems, recv_sems, 3 + k, sib).wait_recv()
            _rcopy(ws_ref, gs_ref.at[jk], send_sems, recv_sems, 6 + k, (px, py, c)).wait_recv()
        for cp in sends:
            cp.wait_send()
        own_b.wait()
        own_s.wait()

    return pl.pallas_call(
        body, name="gather_weights", in_specs=[_HBM, _HBM], out_specs=[_HBM, _HBM],
        out_shape=[jax.ShapeDtypeStruct((4, 2, rh, LANES), BF16), jax.ShapeDtypeStruct((4, rs, LANES), F32)],
        scratch_shapes=[pltpu.SemaphoreType.DMA((9,)), pltpu.SemaphoreType.DMA((9,)), pltpu.SemaphoreType.DMA((2,))],
    )(wb, ws)


def _sibling_swap(a, *, name):
    def body(a_ref, o_ref, send_sem, recv_sem):
        x, y, c = _place()
        cp = pltpu.make_async_remote_copy(src_ref=a_ref, dst_ref=o_ref, send_sem=send_sem, recv_sem=recv_sem,
                                          device_id=(x, y, 1 - c), device_id_type=MESH)
        cp.start()
        cp.wait()

    return pl.pallas_call(
        body, name=name, in_specs=[_HBM], out_specs=_HBM, out_shape=jax.ShapeDtypeStruct(a.shape, a.dtype),
        scratch_shapes=[pltpu.SemaphoreType.DMA, pltpu.SemaphoreType.DMA],
    )(a)


def _chip_exchange(p, small):
    _, rh, _ = p.shape
    rs = small.shape[0]

    def body(p_ref, s_ref, l_ref, ls_ref, send_sems, recv_sems, local_sems):
        x, y, c = _place()
        j0 = 2 * x + y
        d0 = 2 * j0 + c
        own_p = pltpu.make_async_copy(p_ref.at[j0], l_ref.at[j0], local_sems.at[0])
        own_s = pltpu.make_async_copy(s_ref, ls_ref.at[d0], local_sems.at[1])
        own_p.start()
        own_s.start()
        sends = []
        for k, (fx, fy) in enumerate(_CHIP_FLIPS):
            px, py = _flip(x, fx), _flip(y, fy)
            sends.append(_rcopy(p_ref.at[2 * px + py], l_ref.at[j0], send_sems, recv_sems, k, (px, py, c)))
        for k, (fx, fy, fc) in enumerate(_DEV_FLIPS):
            peer = (_flip(x, fx), _flip(y, fy), _flip(c, fc))
            sends.append(_rcopy(s_ref, ls_ref.at[d0], send_sems, recv_sems, 3 + k, peer))
        for cp in sends:
            cp.start()
        for k, (fx, fy) in enumerate(_CHIP_FLIPS):
            px, py = _flip(x, fx), _flip(y, fy)
            _rcopy(p_ref.at[j0], l_ref.at[2 * px + py], send_sems, recv_sems, k, (px, py, c)).wait_recv()
        for k, (fx, fy, fc) in enumerate(_DEV_FLIPS):
            px, py, pc = _flip(x, fx), _flip(y, fy), _flip(c, fc)
            _rcopy(s_ref, ls_ref.at[4 * px + 2 * py + pc], send_sems, recv_sems, 3 + k, (px, py, pc)).wait_recv()
        for cp in sends:
            cp.wait_send()
        own_p.wait()
        own_s.wait()

    return pl.pallas_call(
        body, name="chip_exchange", in_specs=[_HBM, _HBM], out_specs=[_HBM, _HBM],
        out_shape=[jax.ShapeDtypeStruct((4, rh, LANES), F32), jax.ShapeDtypeStruct((8, rs, LANES), F32)],
        scratch_shapes=[pltpu.SemaphoreType.DMA((10,)), pltpu.SemaphoreType.DMA((10,)), pltpu.SemaphoreType.DMA((2,))],
    )(p, small)


def _sum_slots(a, *, name):
    n, rows, _ = a.shape
    tr = _tile(rows, 1024)

    def body(a_ref, o_ref):
        acc = a_ref[0]
        for s in range(1, n):
            acc = acc + a_ref[s]
        o_ref[...] = acc

    return pl.pallas_call(
        body, name=name, grid=(rows // tr,), in_specs=[pl.BlockSpec((n, tr, LANES), lambda i: (0, i, 0))],
        out_specs=pl.BlockSpec((tr, LANES), lambda i: (i, 0)), out_shape=jax.ShapeDtypeStruct((rows, LANES), F32),
        compiler_params=_params(("parallel",)),
    )(a)


def _adamw(g, w, m, v, *, name):
    rows = g.shape[0]
    tr = _tile(rows, 1024)
    c1 = 1.0 - ADAM_B1 ** ADAM_STEP
    c2 = 1.0 - ADAM_B2 ** ADAM_STEP

    def body(g_ref, w_ref, m_ref, v_ref, d_ref, mo_ref, vo_ref):
        gv = g_ref[...]
        mn = ADAM_B1 * m_ref[...] + (1.0 - ADAM_B1) * gv
        vn = ADAM_B2 * v_ref[...] + (1.0 - ADAM_B2) * (gv * gv)
        mo_ref[...] = mn
        vo_ref[...] = vn
        d_ref[...] = -ADAM_LR * ((mn / c1) / (jnp.sqrt(vn / c2) + ADAM_EPS) + ADAM_WD * w_ref[...])

    blk = pl.BlockSpec((tr, LANES), lambda i: (i, 0))
    out = jax.ShapeDtypeStruct((rows, LANES), F32)
    return pl.pallas_call(
        body, name=name, grid=(rows // tr,), in_specs=[blk] * 4, out_specs=[blk] * 3, out_shape=[out] * 3,
        compiler_params=_params(("parallel",)),
    )(g, w, m, v)


_SHARDED = (("w_in_ab", (1024, 488), 1), ("w_q_b", (256, 192), 1), ("w_kv_b", (128, 256), 1),
            ("w_out_ab", (256, 1024), 0), ("w_in_c", (1024, 1028), 1), ("w_out_c", (256, 1024), 0),
            ("conv_w", (4, 768), 1), ("norm_c", (1, 256), 1))
_N_BF16 = 6
_REPLICATED = (("norm_ab", (1, 1024)), ("q_a_norm", (1, 256)), ("kv_a_norm", (1, 128)), ("pool_w", (4, 128, 128)),
               ("pool_scale", (1, 512)), ("a_log", (1, 8)), ("dt_bias", (1, 8)), ("o_norm", (1, 128)),
               ("final_norm", (1, 1024)))
_ALL_NAMES = ("norm_ab", "w_in_ab", "q_a_norm", "w_q_b", "kv_a_norm", "w_kv_b", "pool_w", "pool_scale", "w_out_ab",
              "norm_c", "w_in_c", "conv_w", "a_log", "dt_bias", "o_norm", "w_out_c", "final_norm")


def _rows_of(shape):
    return max(1, math.prod(shape) // LANES)


def _as_rows(a):
    n = a.size
    if n % LANES:
        a = jnp.pad(a.reshape(1, n), ((0, 0), (0, LANES - n % LANES)))
    return a.reshape(-1, LANES)


def _pack(parts, total_rows):
    rows = jnp.concatenate([_as_rows(p) for p in parts], axis=0)
    return jnp.pad(rows, ((0, total_rows - rows.shape[0]), (0, 0)))


def _unpack(packed, spec):
    out, off = [], 0
    lead = packed.shape[:-2]
    for shape in spec:
        r = _rows_of(shape)
        blk = packed[..., off:off + r, :].reshape(lead + (r * LANES,))[..., :math.prod(shape)]
        out.append(blk.reshape(lead + tuple(shape)))
        off += r
    return out


def _round_up(n, m):
    return -(-n // m) * m


_SH_SHAPES = tuple(s for _, s, _ in _SHARDED)
_RB = sum(_rows_of(s) for s in _SH_SHAPES[:_N_BF16])
_RS = _round_up(sum(_rows_of(s) for s in _SH_SHAPES[_N_BF16:]), 8)
_RG = _round_up(sum(_rows_of(s) for s in _SH_SHAPES), 16)
_REP_SHAPES = tuple(s for _, s in _REPLICATED)
_RR = _round_up(sum(_rows_of(s) for s in _REP_SHAPES) + 1, 8)


def kernel(x, positions, norm_ab, w_in_ab, q_a_norm, w_q_b, kv_a_norm, w_kv_b, pool_w, pool_scale, w_out_ab, norm_c, w_in_c, conv_w, a_log, dt_bias, o_norm, w_out_c, final_norm, loss_target, m_norm_ab, m_w_in_ab, m_q_a_norm, m_w_q_b, m_kv_a_norm, m_w_kv_b, m_pool_w, m_pool_scale, m_w_out_ab, m_norm_c, m_w_in_c, m_conv_w, m_a_log, m_dt_bias, m_o_norm, m_w_out_c, m_final_norm, v_norm_ab, v_w_in_ab, v_q_a_norm, v_w_q_b, v_kv_a_norm, v_w_kv_b, v_pool_w, v_pool_scale, v_w_out_ab, v_norm_c, v_w_in_c, v_conv_w, v_a_log, v_dt_bias, v_o_norm, v_w_out_c, v_final_norm):
    given = dict(locals())
    c = lax.axis_index("c")
    t = x.shape[1]

    def shard_of(prefix, name):
        a = given[prefix + name]
        return a.reshape(a.shape[1:]) if a.ndim > 2 else a.reshape(1, -1)

    sh = [shard_of("", n) for n, _, _ in _SHARDED]
    wb = _pack([a.astype(BF16) for a in sh[:_N_BF16]], _RB).reshape(2, _RB // 2, LANES)
    ws = _pack(sh[_N_BF16:], _RS)
    gb, gs = _gather_weights(wb, ws)
    parts = _unpack(gb.reshape(4, _RB, LANES), _SH_SHAPES[:_N_BF16]) + _unpack(gs, _SH_SHAPES[_N_BF16:])
    full = {}
    for (name, _, axis), p in zip(_SHARDED, parts):
        full[name] = jnp.concatenate([p[j] for j in range(4)], axis=axis)
    for name, _ in _REPLICATED:
        full[name] = shard_of("", name)
    lw = _layout_weights(full)

    loss_tile, dx, g = _local_step(x[0], positions.reshape(t, 1), loss_target[0], lw)
    grads = _unlayout_grads(g)

    per_chip = []
    for j in range(4):
        pieces = []
        for name, shape, axis in _SHARDED:
            n = shape[axis]
            pieces.append(lax.slice_in_dim(grads[name], j * n, (j + 1) * n, axis=axis))
        per_chip.append(_pack(pieces, _RG))
    gfull = jnp.stack(per_chip).reshape(4, 2, _RG // 2, LANES)
    mine = lax.dynamic_index_in_dim(gfull, c, axis=1, keepdims=False)
    other = lax.dynamic_index_in_dim(gfull, 1 - c, axis=1, keepdims=False)
    from_sibling = _sibling_swap(other, name="core_swap_partial")
    pair = jnp.stack([mine, from_sibling]).reshape(2, 4 * (_RG // 2), LANES)
    chip_sum = _sum_slots(pair, name="core_sum").reshape(4, _RG // 2, LANES)
    small = _pack([grads[n] for n, _ in _REPLICATED] + [loss_tile[0:1, :]], _RR)
    landed, small_all = _chip_exchange(chip_sum, small)
    my_half = _sum_slots(landed, name="chip_sum")
    small_sum = _sum_slots(small_all, name="small_sum")
    sib_half = _sibling_swap(my_half, name="core_swap_sum")
    g_shard = jnp.where(c == 0, jnp.concatenate([my_half, sib_half]), jnp.concatenate([sib_half, my_half]))

    w_sh = _pack(sh, _RG)
    m_sh = _pack([shard_of("m_", n) for n, _, _ in _SHARDED], _RG)
    v_sh = _pack([shard_of("v_", n) for n, _, _ in _SHARDED], _RG)
    d_sh, mo_sh, vo_sh = _adamw(g_shard, w_sh, m_sh, v_sh, name="adamw_sharded")
    zero_row = jnp.zeros((1, LANES), F32)
    w_rp = _pack([shard_of("", n) for n, _ in _REPLICATED] + [zero_row], _RR)
    m_rp = _pack([shard_of("m_", n) for n, _ in _REPLICATED] + [zero_row], _RR)
    v_rp = _pack([shard_of("v_", n) for n, _ in _REPLICATED] + [zero_row + 1.0], _RR)
    d_rp, mo_rp, vo_rp = _adamw(small_sum, w_rp, m_rp, v_rp, name="adamw_replicated")

    res = {}
    for key, sh_pack, rp_pack in (("grad", g_shard, small_sum), ("delta", d_sh, d_rp), ("m", mo_sh, mo_rp), ("v", vo_sh, vo_rp)):
        for (name, _, _), a in zip(_SHARDED, _unpack(sh_pack, _SH_SHAPES)):
            res[key, name] = a.reshape(given[name].shape)
        for (name, _), a in zip(_REPLICATED, _unpack(rp_pack, _REP_SHAPES)):
            res[key, name] = a.reshape(given[name].shape)
    loss = small_sum[sum(_rows_of(s) for s in _REP_SHAPES), 0]
    outs = [loss, dx.reshape(x.shape)]
    for key in ("grad", "delta", "m", "v"):
        outs += [res[key, n] for n in _ALL_NAMES]
    return tuple(outs)
```

```python
import functools
import math

import jax
import jax.numpy as jnp
from jax import lax
from jax.experimental import pallas as pl
from jax.experimental.pallas import tpu as pltpu

F32 = jnp.float32
BF16 = jnp.bfloat16
HI = lax.Precision.HIGHEST
MESH = pl.DeviceIdType.MESH

RMS_EPS = 1e-6
D_MODEL = 1024
MLA_HEADS = 8
MLA_Q_RANK = 256
MLA_KV_RANK = 128
MLA_NOPE = 64
MLA_ROPE = 32
MLA_V = 64
ROPE_THETA = 10000.0
POOL_WINDOWS = (2, 4, 8, 16)
POOL_GROUP = 128
POOL_WIDTH = 512
POOL_HALO = 16
GDN_HEADS = 8
GDN_DK = 128
CONV_WIDTH = 4
CONV_HALO = 8
CHUNK = 64
IN_AB_PAD = 2048
IN_C_PAD = 4224
ATT_SCALE = (MLA_NOPE + MLA_ROPE) ** -0.5

ADAM_LR = 0.001
ADAM_B1 = 0.9
ADAM_B2 = 0.999
ADAM_EPS = 1e-08
ADAM_WD = 0.01
ADAM_STEP = 10

LANES = 128
VMEM_LIMIT = 56 * 1024 * 1024

ROW_TILE = 256
ATT_TILE = 512
GDN_TILE = 512
MM_TILE = 512

NN = (((1,), (0,)), ((), ()))
NT = (((1,), (1,)), ((), ()))
TN = (((0,), (0,)), ((), ()))


def _dot(a, b, dims=NN, prec=None):
    return lax.dot_general(a, b, dims, precision=prec, preferred_element_type=F32)


def _tile(n, pref):
    if n <= pref:
        return n
    step = LANES if pref >= LANES else 8
    for t in range(pref - pref % step, 0, -step):
        if n % t == 0:
            return t
    return n


def _params(sem):
    return pltpu.CompilerParams(dimension_semantics=sem, vmem_limit_bytes=VMEM_LIMIT)


def _sigmoid(x):
    return 1.0 / (1.0 + jnp.exp(-x))


def _softplus(x):
    return jnp.maximum(x, 0.0) + jnp.log(1.0 + jnp.exp(-jnp.abs(x)))


def _matmul(a, b, mode, *, name, out_dtype=F32, add=None, tm=MM_TILE, tn=MM_TILE, tk=2048):
    if mode == "nn":
        (m, k), (k2, n) = a.shape, b.shape
    elif mode == "nt":
        (m, k), (n, k2) = a.shape, b.shape
    else:
        (k, m), (k2, n) = a.shape, b.shape
    assert k == k2, (a.shape, b.shape, mode)
    tm, tn, tk = _tile(m, tm), _tile(n, tn), _tile(k, tk)
    nk = k // tk
    if mode == "tn":
        a_spec = pl.BlockSpec((tk, tm), lambda i, j, kk: (kk, i))
    else:
        a_spec = pl.BlockSpec((tm, tk), lambda i, j, kk: (i, kk))
    if mode == "nt":
        b_spec = pl.BlockSpec((tn, tk), lambda i, j, kk: (j, kk))
    else:
        b_spec = pl.BlockSpec((tk, tn), lambda i, j, kk: (kk, j))
    o_spec = pl.BlockSpec((tm, tn), lambda i, j, kk: (i, j))
    dims = {"nn": NN, "nt": NT, "tn": TN}[mode]
    has_add = add is not None

    def body(*refs):
        if has_add:
            a_ref, b_ref, add_ref, o_ref, acc = refs
        else:
            a_ref, b_ref, o_ref, acc = refs
        kk = pl.program_id(2)

        @pl.when(kk == 0)
        def _():
            acc[...] = jnp.zeros_like(acc)

        acc[...] += _dot(a_ref[...], b_ref[...], dims)

        @pl.when(kk == nk - 1)
        def _():
            o = acc[...]
            if has_add:
                o = o + add_ref[...]
            o_ref[...] = o.astype(out_dtype)

    in_specs = [a_spec, b_spec] + ([o_spec] if has_add else [])
    args = (a, b) + ((add,) if has_add else ())
    return pl.pallas_call(
        body, name=name, grid=(m // tm, n // tn, nk), in_specs=in_specs, out_specs=o_spec,
        out_shape=jax.ShapeDtypeStruct((m, n), out_dtype), scratch_shapes=[pltpu.VMEM((tm, tn), F32)],
        compiler_params=_params(("parallel", "parallel", "arbitrary")),
    )(*args)


def _rms_fwd(h, g, *, name):
    t, d = h.shape
    tm = _tile(t, 2 * ROW_TILE)

    def body(h_ref, g_ref, o_ref):
        x = h_ref[...]
        r = lax.rsqrt(jnp.mean(x * x, axis=-1, keepdims=True) + RMS_EPS)
        o_ref[...] = (x * r * g_ref[...]).astype(BF16)

    return pl.pallas_call(
        body, name=name, grid=(t // tm,),
        in_specs=[pl.BlockSpec((tm, d), lambda i: (i, 0)), pl.BlockSpec((1, d), lambda i: (0, 0))],
        out_specs=pl.BlockSpec((tm, d), lambda i: (i, 0)),
        out_shape=jax.ShapeDtypeStruct((t, d), BF16), compiler_params=_params(("parallel",)),
    )(h, g)


def _rms_bwd(h, g, dy, dres, *, name, with_bf16):
    t, d = h.shape
    tm = _tile(t, 2 * ROW_TILE)

    def body(h_ref, g_ref, dy_ref, dres_ref, *outs):
        i = pl.program_id(0)
        dh_ref, dg_ref = outs[0], outs[-1]
        x = h_ref[...]
        r = lax.rsqrt(jnp.mean(x * x, axis=-1, keepdims=True) + RMS_EPS)
        xh = x * r
        dyv = dy_ref[...].astype(F32)
        dxh = dyv * g_ref[...]
        dx = r * (dxh - xh * jnp.mean(dxh * xh, axis=-1, keepdims=True))
        dh = dres_ref[...] + dx
        dh_ref[...] = dh
        if with_bf16:
            outs[1][...] = dh.astype(BF16)

        @pl.when(i == 0)
        def _():
            dg_ref[...] = jnp.zeros_like(dg_ref)

        dg_ref[...] += jnp.sum(dyv * xh, axis=0, keepdims=True)

    row = pl.BlockSpec((tm, d), lambda i: (i, 0))
    vec = pl.BlockSpec((1, d), lambda i: (0, 0))
    out_shape = [jax.ShapeDtypeStruct((t, d), F32)]
    out_specs = [row]
    if with_bf16:
        out_shape.append(jax.ShapeDtypeStruct((t, d), BF16))
        out_specs.append(row)
    out_shape.append(jax.ShapeDtypeStruct((1, d), F32))
    out_specs.append(vec)
    return pl.pallas_call(
        body, name=name, grid=(t // tm,), in_specs=[row, vec, row, row], out_specs=out_specs,
        out_shape=out_shape, compiler_params=_params(("arbitrary",)),
    )(h, g, dy, dres)


def _rope_partner(x):
    lane = lax.broadcasted_iota(jnp.int32, x.shape, 1)
    swapped = jnp.where(lane < MLA_NOPE + MLA_ROPE // 2, pltpu.roll(x, LANES - 16, 1), pltpu.roll(x, 16, 1))
    return jnp.where((lane >= MLA_NOPE) & (lane < MLA_NOPE + MLA_ROPE), swapped, 0.0)


def _pool_counts(row0, tm, w):
    t_idx = row0 + lax.broadcasted_iota(jnp.int32, (tm, POOL_GROUP), 0)
    return jnp.minimum(t_idx + 1, w).astype(F32)


def _ab_prep(proj, pos, inv_freq, q_a_norm, kv_a_norm, *, name):
    t = proj.shape[0]
    tm = _tile(t, ROW_TILE)
    hb = tm // POOL_HALO

    def body(p_ref, halo_ref, pos_ref, inv_ref, qg_ref, kg_ref, qn_ref, kvn_ref, kr_ref, d_ref, cos_ref, sin_ref, ext):
        i = pl.program_id(0)
        ql = p_ref[:, 0:MLA_Q_RANK]
        r = lax.rsqrt(jnp.mean(ql * ql, axis=-1, keepdims=True) + RMS_EPS)
        qn_ref[...] = (ql * r * qg_ref[...]).astype(BF16)
        kl = p_ref[:, MLA_Q_RANK:MLA_Q_RANK + MLA_KV_RANK]
        r = lax.rsqrt(jnp.mean(kl * kl, axis=-1, keepdims=True) + RMS_EPS)
        kvn_ref[...] = (kl * r * kg_ref[...]).astype(BF16)
        ang = pos_ref[...].astype(F32) * inv_ref[...]
        lane = lax.broadcasted_iota(jnp.int32, (tm, LANES), 1)
        in_rope = (lane >= MLA_NOPE) & (lane < MLA_NOPE + MLA_ROPE)
        cos_t = jnp.where(in_rope, jnp.cos(ang), 1.0)
        sin_t = jnp.where(in_rope, jnp.sin(ang), 0.0)
        sin_t = jnp.where(lane < MLA_NOPE + MLA_ROPE // 2, -sin_t, sin_t)
        cos_ref[...] = cos_t
        sin_ref[...] = sin_t
        kr = p_ref[:, 384:512]
        kr_ref[...] = kr * cos_t + _rope_partner(kr) * sin_t
        xp = p_ref[:, 512:1024]
        ext[0:POOL_HALO, :] = jnp.where(i > 0, halo_ref[...], 0.0)
        ext[POOL_HALO:POOL_HALO + tm, :] = xp
        for g, w in enumerate(POOL_WINDOWS):
            lo = g * POOL_GROUP
            acc = ext[POOL_HALO:POOL_HALO + tm, lo:lo + POOL_GROUP]
            for s in range(1, w):
                acc = acc + ext[POOL_HALO - s:POOL_HALO - s + tm, lo:lo + POOL_GROUP]
            cnt = _pool_counts(i * tm, tm, w)
            d_ref[:, lo:lo + POOL_GROUP] = (acc / cnt - xp[:, lo:lo + POOL_GROUP]).astype(BF16)

    row = lambda w: pl.BlockSpec((tm, w), lambda i: (i, 0))
    vec = lambda w: pl.BlockSpec((1, w), lambda i: (0, 0))
    return pl.pallas_call(
        body, name=name, grid=(t // tm,),
        in_specs=[row(1024), pl.BlockSpec((POOL_HALO, POOL_WIDTH), lambda i: (jnp.maximum(i * hb - 1, 0), 1)),
                  pl.BlockSpec((tm, 1), lambda i: (i, 0)), vec(LANES), vec(MLA_Q_RANK), vec(MLA_KV_RANK)],
        out_specs=[row(MLA_Q_RANK), row(MLA_KV_RANK), row(LANES), row(POOL_WIDTH), row(LANES), row(LANES)],
        out_shape=[jax.ShapeDtypeStruct((t, MLA_Q_RANK), BF16), jax.ShapeDtypeStruct((t, MLA_KV_RANK), BF16),
                   jax.ShapeDtypeStruct((t, LANES), F32), jax.ShapeDtypeStruct((t, POOL_WIDTH), BF16),
                   jax.ShapeDtypeStruct((t, LANES), F32), jax.ShapeDtypeStruct((t, LANES), F32)],
        scratch_shapes=[pltpu.VMEM((tm + POOL_HALO, POOL_WIDTH), F32)],
        compiler_params=_params(("parallel",)),
    )(proj, proj, pos, inv_freq, q_a_norm, kv_a_norm)


def _qk_rope(qraw, kvk, kr, cos_t, sin_t, *, name):
    t = qraw.shape[0]
    tm = _tile(t, 2 * ROW_TILE)

    def body(q_ref, k_ref, kr_ref, c_ref, s_ref, qo_ref, ko_ref):
        c, s, krv = c_ref[...], s_ref[...], kr_ref[...]
        for h in range(MLA_HEADS):
            sl = slice(h * LANES, (h + 1) * LANES)
            q = q_ref[:, sl]
            qo_ref[:, sl] = (q * c + _rope_partner(q) * s).astype(BF16)
            ko_ref[:, sl] = (k_ref[:, sl] + krv).astype(BF16)

    row = lambda w: pl.BlockSpec((tm, w), lambda i: (i, 0))
    return pl.pallas_call(
        body, name=name, grid=(t // tm,), in_specs=[row(1024), row(1024), row(LANES), row(LANES), row(LANES)],
        out_specs=[row(1024), row(1024)],
        out_shape=[jax.ShapeDtypeStruct((t, 1024), BF16), jax.ShapeDtypeStruct((t, 1024), BF16)],
        compiler_params=_params(("parallel",)),
    )(qraw, kvk, kr, cos_t, sin_t)


def _qk_rope_bwd(dq, dk, cos_t, sin_t, *, name):
    t = dq.shape[0]
    tm = _tile(t, 2 * ROW_TILE)

    def body(dq_ref, dk_ref, c_ref, s_ref, dqo_ref, dko_ref, dkr_ref):
        c, s = c_ref[...], s_ref[...]
        lane = lax.broadcasted_iota(jnp.int32, (tm, LANES), 1)
        in_rope = (lane >= MLA_NOPE) & (lane < MLA_NOPE + MLA_ROPE)
        dkr = jnp.zeros((tm, LANES), F32)
        for h in range(MLA_HEADS):
            sl = slice(h * LANES, (h + 1) * LANES)
            g = dq_ref[:, sl]
            dqo_ref[:, sl] = (g * c + _rope_partner(g * s)).astype(BF16)
            gk = dk_ref[:, sl]
            dko_ref[:, sl] = gk.astype(BF16)
            dkr = dkr + jnp.where(in_rope, gk, 0.0)
        dkr_ref[...] = dkr * c + _rope_partner(dkr * s)

    row = lambda w: pl.BlockSpec((tm, w), lambda i: (i, 0))
    return pl.pallas_call(
        body, name=name, grid=(t // tm,), in_specs=[row(1024), row(1024), row(LANES), row(LANES)],
        out_specs=[row(1024), row(1024), row(LANES)],
        out_shape=[jax.ShapeDtypeStruct((t, 1024), BF16), jax.ShapeDtypeStruct((t, 1024), BF16),
                   jax.ShapeDtypeStruct((t, LANES), F32)],
        compiler_params=_params(("parallel",)),
    )(dq, dk, cos_t, sin_t)


def _ab_prep_bwd(proj, q_a_norm, kv_a_norm, dqn, dkvn_k, dkvn_v, dkr, dd, dz, *, name):
    t = proj.shape[0]
    tm = _tile(t, ROW_TILE)
    hb = tm // POOL_HALO
    last_halo = t // POOL_HALO - 1
    nt = t // tm

    def body(p_ref, qg_ref, kg_ref, dqn_ref, dk1_ref, dk2_ref, dkr_ref, dd_ref, ddn_ref, dz_ref,
             dp_ref, dqg_ref, dkg_ref, ext):
        i = pl.program_id(0)

        @pl.when(i == 0)
        def _():
            dqg_ref[...] = jnp.zeros_like(dqg_ref)
            dkg_ref[...] = jnp.zeros_like(dkg_ref)

        def norm_bwd(x, g, dy, dg_ref):
            r = lax.rsqrt(jnp.mean(x * x, axis=-1, keepdims=True) + RMS_EPS)
            xh = x * r
            dxh = dy * g
            dg_ref[...] += jnp.sum(dy * xh, axis=0, keepdims=True)
            return r * (dxh - xh * jnp.mean(dxh * xh, axis=-1, keepdims=True))

        dql = norm_bwd(p_ref[:, 0:MLA_Q_RANK], qg_ref[...], dqn_ref[...], dqg_ref)
        dp_ref[:, 0:MLA_Q_RANK] = dql.astype(BF16)
        dkl = norm_bwd(p_ref[:, MLA_Q_RANK:384], kg_ref[...], dk1_ref[...] + dk2_ref[...], dkg_ref)
        dp_ref[:, MLA_Q_RANK:384] = dkl.astype(BF16)
        dp_ref[:, 384:512] = dkr_ref[...].astype(BF16)
        ddv = dd_ref[...]
        for g, w in enumerate(POOL_WINDOWS):
            lo = g * POOL_GROUP
            ext[0:tm, lo:lo + POOL_GROUP] = ddv[:, lo:lo + POOL_GROUP] / _pool_counts(i * tm, tm, w)
            nxt = ddn_ref[:, lo:lo + POOL_GROUP] / _pool_counts((i + 1) * tm, POOL_HALO, w)
            ext[tm:tm + POOL_HALO, lo:lo + POOL_GROUP] = jnp.where(i < nt - 1, nxt, 0.0)
        for g, w in enumerate(POOL_WINDOWS):
            lo = g * POOL_GROUP
            acc = ext[0:tm, lo:lo + POOL_GROUP]
            for s in range(1, w):
                acc = acc + ext[s:s + tm, lo:lo + POOL_GROUP]
            dp_ref[:, 512 + lo:512 + lo + POOL_GROUP] = (acc - ddv[:, lo:lo + POOL_GROUP]).astype(BF16)
        dp_ref[:, 1024:2048] = dz_ref[...]

    row = lambda w: pl.BlockSpec((tm, w), lambda i: (i, 0))
    vec = lambda w: pl.BlockSpec((1, w), lambda i: (0, 0))
    return pl.pallas_call(
        body, name=name, grid=(nt,),
        in_specs=[row(1024), vec(MLA_Q_RANK), vec(MLA_KV_RANK), row(MLA_Q_RANK), row(MLA_KV_RANK), row(MLA_KV_RANK),
                  row(LANES), row(POOL_WIDTH),
                  pl.BlockSpec((POOL_HALO, POOL_WIDTH), lambda i: (jnp.minimum((i + 1) * hb, last_halo), 0)),
                  row(1024)],
        out_specs=[row(IN_AB_PAD), vec(MLA_Q_RANK), vec(MLA_KV_RANK)],
        out_shape=[jax.ShapeDtypeStruct((t, IN_AB_PAD), BF16), jax.ShapeDtypeStruct((1, MLA_Q_RANK), F32),
                   jax.ShapeDtypeStruct((1, MLA_KV_RANK), F32)],
        scratch_shapes=[pltpu.VMEM((tm + POOL_HALO, POOL_WIDTH), F32)],
        compiler_params=_params(("arbitrary",)),
    )(proj, q_a_norm, kv_a_norm, dqn, dkvn_k, dkvn_v, dkr, dd, dd, dz)


def _gate_fwd(o, ybraw, proj, pool_scale, *, name):
    t = o.shape[0]
    tm = _tile(t, 2 * ROW_TILE)

    def body(o_ref, yb_ref, z_ref, ps_ref, y_ref):
        z = z_ref[...]
        sz = z * _sigmoid(z)
        y_ref[:, 0:512] = (o_ref[...] * sz[:, 0:512]).astype(BF16)
        y_ref[:, 512:1024] = (yb_ref[...] * ps_ref[...] * sz[:, 512:1024]).astype(BF16)

    row = lambda w: pl.BlockSpec((tm, w), lambda i: (i, 0))
    return pl.pallas_call(
        body, name=name, grid=(t // tm,),
        in_specs=[row(512), row(512), pl.BlockSpec((tm, 1024), lambda i: (i, 1)), pl.BlockSpec((1, 512), lambda i: (0, 0))],
        out_specs=row(1024), out_shape=jax.ShapeDtypeStruct((t, 1024), BF16), compiler_params=_params(("parallel",)),
    )(o, ybraw, proj, pool_scale)


def _gate_bwd(dy, o, ybraw, proj, pool_scale, *, name):
    t = o.shape[0]
    tm = _tile(t, ROW_TILE)

    def body(dy_ref, o_ref, yb_ref, z_ref, ps_ref, do_ref, dl_ref, dyb_ref, dz_ref, dps_ref):
        i = pl.program_id(0)
        z = z_ref[...]
        sg = _sigmoid(z)
        sz = z * sg
        dsz = sg * (1.0 + z * (1.0 - sg))
        dyv = dy_ref[...]
        dcat = dyv * sz
        ov = o_ref[...]
        ybs = yb_ref[...] * ps_ref[...]
        dz_ref[:, 0:512] = (dyv[:, 0:512] * ov * dsz[:, 0:512]).astype(BF16)
        dz_ref[:, 512:1024] = (dyv[:, 512:1024] * ybs * dsz[:, 512:1024]).astype(BF16)
        do = dcat[:, 0:512]
        do_ref[...] = do.astype(BF16)
        r_i = lax.broadcasted_iota(jnp.int32, (512, 512), 0) // MLA_V
        c_i = lax.broadcasted_iota(jnp.int32, (512, 512), 1) // MLA_V
        dl_ref[...] = _dot(do * ov, (r_i == c_i).astype(F32), NN, HI)
        dyb_ref[...] = (dcat[:, 512:1024] * ps_ref[...]).astype(BF16)

        @pl.when(i == 0)
        def _():
            dps_ref[...] = jnp.zeros_like(dps_ref)

        dps_ref[...] += jnp.sum(dcat[:, 512:1024] * yb_ref[...], axis=0, keepdims=True)

    row = lambda w: pl.BlockSpec((tm, w), lambda i: (i, 0))
    vec = pl.BlockSpec((1, 512), lambda i: (0, 0))
    return pl.pallas_call(
        body, name=name, grid=(t // tm,),
        in_specs=[row(1024), row(512), row(512), pl.BlockSpec((tm, 1024), lambda i: (i, 1)), vec],
        out_specs=[row(512), row(512), row(512), row(1024), vec],
        out_shape=[jax.ShapeDtypeStruct((t, 512), BF16), jax.ShapeDtypeStruct((t, 512), F32),
                   jax.ShapeDtypeStruct((t, 512), BF16), jax.ShapeDtypeStruct((t, 1024), BF16),
                   jax.ShapeDtypeStruct((1, 512), F32)],
        compiler_params=_params(("arbitrary",)),
    )(dy, o, ybraw, proj, pool_scale)


def _causal_mask(qi, ki, tq, tk):
    row = qi * tq + lax.broadcasted_iota(jnp.int32, (tq, tk), 0)
    col = ki * tk + lax.broadcasted_iota(jnp.int32, (tq, tk), 1)
    return col <= row


def _attn_fwd(q, k, v, *, name):
    t = q.shape[0]
    tq = _tile(t, ATT_TILE)
    nq = t // tq

    def body(q_ref, k_ref, v_ref, o_ref, lse_ref, m_sc, l_sc, acc_sc):
        qi, ki = pl.program_id(1), pl.program_id(2)

        @pl.when(ki == 0)
        def _():
            m_sc[...] = jnp.full_like(m_sc, -jnp.inf)
            l_sc[...] = jnp.zeros_like(l_sc)
            acc_sc[...] = jnp.zeros_like(acc_sc)

        @pl.when(ki <= qi)
        def _():
            mask = _causal_mask(qi, ki, tq, tq)
            vv = v_ref[...]
            for a in range(2):
                sl = slice(a * LANES, (a + 1) * LANES)
                s = _dot(q_ref[:, sl], k_ref[:, sl], NT) * ATT_SCALE
                s = jnp.where(mask, s, -jnp.inf)
                m_prev = m_sc[a]
                m_new = jnp.maximum(m_prev, jnp.max(s, axis=-1, keepdims=True))
                alpha = jnp.exp(m_prev - m_new)
                p = jnp.exp(s - m_new[:, 0:1])
                l_sc[a] = alpha * l_sc[a] + jnp.sum(p, axis=-1, keepdims=True)
                acc_sc[a] = alpha * acc_sc[a] + _dot(p.astype(BF16), vv)
                m_sc[a] = m_new

        @pl.when(ki == qi)
        def _():
            lane = lax.broadcasted_iota(jnp.int32, (tq, LANES), 1)
            first = lane < MLA_V
            o_ref[...] = jnp.where(first, acc_sc[0] / l_sc[0], acc_sc[1] / l_sc[1])
            lse_ref[...] = jnp.where(first, m_sc[0] + jnp.log(l_sc[0]), m_sc[1] + jnp.log(l_sc[1]))

    return pl.pallas_call(
        body, name=name, grid=(MLA_HEADS // 2, nq, nq),
        in_specs=[pl.BlockSpec((tq, 2 * LANES), lambda h, i, j: (i, h)),
                  pl.BlockSpec((tq, 2 * LANES), lambda h, i, j: (jnp.minimum(i, j), h)),
                  pl.BlockSpec((tq, LANES), lambda h, i, j: (jnp.minimum(i, j), h))],
        out_specs=[pl.BlockSpec((tq, LANES), lambda h, i, j: (i, h)), pl.BlockSpec((tq, LANES), lambda h, i, j: (i, h))],
        out_shape=[jax.ShapeDtypeStruct((t, 512), F32), jax.ShapeDtypeStruct((t, 512), F32)],
        scratch_shapes=[pltpu.VMEM((2, tq, LANES), F32), pltpu.VMEM((2, tq, LANES), F32), pltpu.VMEM((2, tq, LANES), F32)],
        compiler_params=_params(("parallel", "parallel", "arbitrary")),
    )(q, k, v)


def _attn_bwd(q, k, v, do, lse, delta, *, name):
    t = q.shape[0]
    tq = _tile(t, ATT_TILE)
    nq = t // tq

    def body(q_ref, k_ref, v_ref, do_ref, lse_ref, dl_ref, dq_ref, dk_ref, dv_ref, dk_sc, dv_sc):
        ki, qi = pl.program_id(1), pl.program_id(2)

        @pl.when((ki == 0) & (qi == 0))
        def _():
            dq_ref[...] = jnp.zeros_like(dq_ref)

        @pl.when(qi == 0)
        def _():
            dk_sc[...] = jnp.zeros_like(dk_sc)
            dv_sc[...] = jnp.zeros_like(dv_sc)

        @pl.when(qi >= ki)
        def _():
            mask = _causal_mask(qi, ki, tq, tq)
            lane = lax.broadcasted_iota(jnp.int32, (tq, LANES), 1)
            vv = v_ref[...]
            dov = do_ref[...]
            rows = pl.ds(pl.multiple_of(qi * tq, tq), tq)
            for a in range(2):
                sl = slice(a * LANES, (a + 1) * LANES)
                mine = (lane < MLA_V) if a == 0 else (lane >= MLA_V)
                col = a * MLA_V
                qa, ka = q_ref[:, sl], k_ref[:, sl]
                s = _dot(qa, ka, NT) * ATT_SCALE
                p = jnp.where(mask, jnp.exp(s - lse_ref[:, col:col + 1]), 0.0)
                pb = p.astype(BF16)
                dv_sc[a] += _dot(pb, dov, TN)
                dp = _dot(jnp.where(mine, dov, jnp.zeros_like(dov)), vv, NT)
                ds = (p * (dp - dl_ref[:, col:col + 1]) * ATT_SCALE).astype(BF16)
                dk_sc[a] += _dot(ds, qa, TN)
                dq_ref[rows, sl] += _dot(ds, ka, NN)

        @pl.when(qi == nq - 1)
        def _():
            lane = lax.broadcasted_iota(jnp.int32, (tq, LANES), 1)
            dk_ref[:, 0:LANES] = dk_sc[0]
            dk_ref[:, LANES:2 * LANES] = dk_sc[1]
            dv_ref[...] = jnp.where(lane < MLA_V, dv_sc[0], dv_sc[1]).astype(BF16)

    qrow = lambda w: pl.BlockSpec((tq, w), lambda h, j, i: (jnp.maximum(i, j), h))
    krow = lambda w: pl.BlockSpec((tq, w), lambda h, j, i: (j, h))
    return pl.pallas_call(
        body, name=name, grid=(MLA_HEADS // 2, nq, nq),
        in_specs=[qrow(2 * LANES), krow(2 * LANES), krow(LANES), qrow(LANES), qrow(LANES), qrow(LANES)],
        out_specs=[pl.BlockSpec((t, 2 * LANES), lambda h, j, i: (0, h)), krow(2 * LANES), krow(LANES)],
        out_shape=[jax.ShapeDtypeStruct((t, 1024), F32), jax.ShapeDtypeStruct((t, 1024), F32),
                   jax.ShapeDtypeStruct((t, 512), BF16)],
        scratch_shapes=[pltpu.VMEM((2, tq, LANES), F32), pltpu.VMEM((2, tq, LANES), F32)],
        compiler_params=_params(("parallel", "arbitrary", "arbitrary")),
    )(q, k, v, do, lse, delta)


def _conv_rows(ext, tm, w_ref, sec):
    c0 = sec * 1024
    y = ext[CONV_HALO - 3:CONV_HALO - 3 + tm, c0:c0 + 1024] * w_ref[0:1, c0:c0 + 1024]
    for j in range(1, CONV_WIDTH):
        y = y + ext[CONV_HALO - 3 + j:CONV_HALO - 3 + j + tm, c0:c0 + 1024] * w_ref[j:j + 1, c0:c0 + 1024]
    return y


def _c_prep(proj_c, conv_w, a_log, dt_bias, *, name):
    t = proj_c.shape[0]
    tm = _tile(t, ROW_TILE)
    hb = tm // CONV_HALO

    def body(p_ref, halo_ref, ab_ref, w_ref, al_ref, dtb_ref, q_ref, k_ref, v_ref, g_ref, b_ref, gt_ref, ext):
        i = pl.program_id(0)
        ext[0:CONV_HALO, :] = jnp.where(i > 0, halo_ref[...], 0.0)
        ext[CONV_HALO:CONV_HALO + tm, :] = p_ref[...]
        for sec, o_ref in enumerate((q_ref, k_ref, v_ref)):
            y = _conv_rows(ext, tm, w_ref, sec)
            y = y * _sigmoid(y)
            if sec == 2:
                o_ref[...] = y
                continue
            scale = GDN_DK ** -0.5 if sec == 0 else 1.0
            for h in range(GDN_HEADS):
                sl = slice(h * LANES, (h + 1) * LANES)
                blk = y[:, sl]
                r = lax.rsqrt(jnp.sum(blk * blk, axis=-1, keepdims=True) + RMS_EPS)
                o_ref[:, sl] = blk * (r * scale)
        ab = ab_ref[...]
        g = -jnp.exp(al_ref[...]) * _softplus(ab + dtb_ref[...])
        beta = _sigmoid(ab)
        ri = lax.broadcasted_iota(jnp.int32, (tm, tm), 0)
        ci = lax.broadcasted_iota(jnp.int32, (tm, tm), 1)
        lower = ((ri // CHUNK) == (ci // CHUNK)) & (ri >= ci)
        gc = _dot(lower.astype(F32), g, NN, HI)
        eye = lax.broadcasted_iota(jnp.int32, (LANES, LANES), 0) == lax.broadcasted_iota(jnp.int32, (LANES, LANES), 1)
        gt_ref[...] = _dot(eye.astype(F32), gc, NT, HI)[0:GDN_HEADS, :]
        for h in range(GDN_HEADS):
            sl = slice(h * LANES, (h + 1) * LANES)
            g_ref[:, sl] = jnp.broadcast_to(gc[:, h:h + 1], (tm, LANES))
            b_ref[:, sl] = jnp.broadcast_to(beta[:, GDN_HEADS + h:GDN_HEADS + h + 1], (tm, LANES))

    row = lambda w: pl.BlockSpec((tm, w), lambda i: (i, 0))
    vec = lambda r, w: pl.BlockSpec((r, w), lambda i: (0, 0))
    out = jax.ShapeDtypeStruct((t, 1024), F32)
    return pl.pallas_call(
        body, name=name, grid=(t // tm,),
        in_specs=[row(3072), pl.BlockSpec((CONV_HALO, 3072), lambda i: (jnp.maximum(i * hb - 1, 0), 0)),
                  pl.BlockSpec((tm, LANES), lambda i: (i, 32)), vec(CONV_WIDTH, 3072), vec(1, LANES), vec(1, LANES)],
        out_specs=[row(1024)] * 5 + [pl.BlockSpec((GDN_HEADS, tm), lambda i: (0, i))],
        out_shape=[out] * 5 + [jax.ShapeDtypeStruct((GDN_HEADS, t), F32)],
        scratch_shapes=[pltpu.VMEM((tm + CONV_HALO, 3072), F32)],
        compiler_params=_params(("parallel",)),
    )(proj_c, proj_c, proj_c, conv_w, a_log, dt_bias)


def _c_prep_bwd(proj_c, conv_w, a_log, dt_bias, dq, dk, dv, dgb, dbb, dz, *, name):
    t = proj_c.shape[0]
    tm = _tile(t, ROW_TILE // 2)
    hb = tm // CONV_HALO
    nt = t // tm
    rev = lambda i: nt - 1 - i

    def body(p_ref, halo_ref, ab_ref, w_ref, al_ref, dtb_ref, dq_ref, dk_ref, dv_ref, dg_ref, db_ref, dz_ref,
             dp_ref, dw_ref, dal_ref, ddt_ref, ext, dyext, carry):
        step = pl.program_id(0)
        i = rev(step)

        @pl.when(step == 0)
        def _():
            dw_ref[...] = jnp.zeros_like(dw_ref)
            dal_ref[...] = jnp.zeros_like(dal_ref)
            ddt_ref[...] = jnp.zeros_like(ddt_ref)
            carry[...] = jnp.zeros_like(carry)

        ext[0:CONV_HALO, :] = jnp.where(i > 0, halo_ref[...], 0.0)
        ext[CONV_HALO:CONV_HALO + tm, :] = p_ref[...]
        for sec, g_ref in enumerate((dq_ref, dk_ref, dv_ref)):
            c0 = sec * 1024
            y = _conv_rows(ext, tm, w_ref, sec)
            sg = _sigmoid(y)
            act = y * sg
            if sec == 2:
                dact = g_ref[...]
            else:
                scale = GDN_DK ** -0.5 if sec == 0 else 1.0
                parts = []
                for h in range(GDN_HEADS):
                    sl = slice(h * LANES, (h + 1) * LANES)
                    blk = act[:, sl]
                    r = lax.rsqrt(jnp.sum(blk * blk, axis=-1, keepdims=True) + RMS_EPS)
                    n = blk * r
                    dn = g_ref[:, sl] * scale
                    parts.append(r * (dn - n * jnp.sum(dn * n, axis=-1, keepdims=True)))
                dact = jnp.concatenate(parts, axis=-1)
            dy = dact * (sg * (1.0 + y * (1.0 - sg)))
            dyext[0:tm, c0:c0 + 1024] = dy
            for j in range(CONV_WIDTH):
                xs = ext[CONV_HALO - 3 + j:CONV_HALO - 3 + j + tm, c0:c0 + 1024]
                dw_ref[j:j + 1, c0:c0 + 1024] += jnp.sum(dy * xs, axis=0, keepdims=True)
        dyext[tm:tm + CONV_HALO, :] = carry[...]
        carry[...] = dyext[0:CONV_HALO, :]
        for sec in range(3):
            c0 = sec * 1024
            dx = dyext[3:3 + tm, c0:c0 + 1024] * w_ref[0:1, c0:c0 + 1024]
            for j in range(1, CONV_WIDTH):
                dx = dx + dyext[3 - j:3 - j + tm, c0:c0 + 1024] * w_ref[j:j + 1, c0:c0 + 1024]
            dp_ref[:, c0:c0 + 1024] = dx.astype(BF16)
        dp_ref[:, 3072:4096] = dz_ref[...]
        lane = lax.broadcasted_iota(jnp.int32, (tm, LANES), 1)
        dg = jnp.zeros((tm, LANES), F32)
        dbeta = jnp.zeros((tm, LANES), F32)
        for h in range(GDN_HEADS):
            sl = slice(h * LANES, (h + 1) * LANES)
            dg = dg + jnp.where(lane == h, dg_ref[:, sl], 0.0)
            dbeta = dbeta + jnp.where(lane == GDN_HEADS + h, db_ref[:, sl], 0.0)
        ri = lax.broadcasted_iota(jnp.int32, (tm, tm), 0)
        ci = lax.broadcasted_iota(jnp.int32, (tm, tm), 1)
        upper = ((ri // CHUNK) == (ci // CHUNK)) & (ri <= ci)
        dg = _dot(upper.astype(F32), dg, NN, HI)
        pre = ab_ref[...] + dtb_ref[...]
        s = _sigmoid(pre)
        a_exp = jnp.exp(al_ref[...])
        dg_da = dg * (-a_exp * s)
        dp_ref[:, 4096:IN_C_PAD] = (dg_da + dbeta * s * (1.0 - s)).astype(BF16)
        dal_ref[...] += jnp.sum(dg * (-a_exp * _softplus(pre)), axis=0, keepdims=True)
        ddt_ref[...] += jnp.sum(dg_da, axis=0, keepdims=True)

    row = lambda w: pl.BlockSpec((tm, w), lambda s: (rev(s), 0))
    vec = lambda r, w: pl.BlockSpec((r, w), lambda s: (0, 0))
    return pl.pallas_call(
        body, name=name, grid=(nt,),
        in_specs=[row(3072), pl.BlockSpec((CONV_HALO, 3072), lambda s: (jnp.maximum(rev(s) * hb - 1, 0), 0)),
                  pl.BlockSpec((tm, LANES), lambda s: (rev(s), 32)), vec(CONV_WIDTH, 3072), vec(1, LANES), vec(1, LANES),
                  row(1024), row(1024), row(1024), row(1024), row(1024), row(1024)],
        out_specs=[row(IN_C_PAD), vec(CONV_WIDTH, 3072), vec(1, LANES), vec(1, LANES)],
        out_shape=[jax.ShapeDtypeStruct((t, IN_C_PAD), BF16), jax.ShapeDtypeStruct((CONV_WIDTH, 3072), F32),
                   jax.ShapeDtypeStruct((1, LANES), F32), jax.ShapeDtypeStruct((1, LANES), F32)],
        scratch_shapes=[pltpu.VMEM((tm + CONV_HALO, 3072), F32), pltpu.VMEM((tm + CONV_HALO, 3072), F32),
                        pltpu.VMEM((CONV_HALO, 3072), F32)],
        compiler_params=_params(("arbitrary",)),
    )(proj_c, proj_c, proj_c, conv_w, a_log, dt_bias, dq, dk, dv, dgb, dbb, dz)


def _o_gate_fwd(o, proj_c, o_norm, *, name):
    t = o.shape[0]
    tm = _tile(t, 2 * ROW_TILE)

    def body(o_ref, z_ref, g_ref, y_ref):
        for h in range(GDN_HEADS):
            sl = slice(h * LANES, (h + 1) * LANES)
            x = o_ref[:, sl]
            r = lax.rsqrt(jnp.mean(x * x, axis=-1, keepdims=True) + RMS_EPS)
            z = z_ref[:, sl]
            y_ref[:, sl] = (x * r * g_ref[...] * (z * _sigmoid(z))).astype(BF16)

    row = pl.BlockSpec((tm, 1024), lambda i: (i, 0))
    return pl.pallas_call(
        body, name=name, grid=(t // tm,),
        in_specs=[row, pl.BlockSpec((tm, 1024), lambda i: (i, 3)), pl.BlockSpec((1, LANES), lambda i: (0, 0))],
        out_specs=row, out_shape=jax.ShapeDtypeStruct((t, 1024), BF16), compiler_params=_params(("parallel",)),
    )(o, proj_c, o_norm)


def _o_gate_bwd(dy, o, proj_c, o_norm, *, name):
    t = o.shape[0]
    tm = _tile(t, 2 * ROW_TILE)

    def body(dy_ref, o_ref, z_ref, g_ref, do_ref, dz_ref, dg_ref):
        i = pl.program_id(0)

        @pl.when(i == 0)
        def _():
            dg_ref[...] = jnp.zeros_like(dg_ref)

        dg = jnp.zeros((1, LANES), F32)
        for h in range(GDN_HEADS):
            sl = slice(h * LANES, (h + 1) * LANES)
            x = o_ref[:, sl]
            r = lax.rsqrt(jnp.mean(x * x, axis=-1, keepdims=True) + RMS_EPS)
            xh = x * r
            z = z_ref[:, sl]
            sg = _sigmoid(z)
            dyv = dy_ref[:, sl]
            dn = dyv * (z * sg)
            dz_ref[:, sl] = (dyv * xh * g_ref[...] * (sg * (1.0 + z * (1.0 - sg)))).astype(BF16)
            dxh = dn * g_ref[...]
            do_ref[:, sl] = r * (dxh - xh * jnp.mean(dxh * xh, axis=-1, keepdims=True))
            dg = dg + jnp.sum(dn * xh, axis=0, keepdims=True)
        dg_ref[...] += dg

    row = pl.BlockSpec((tm, 1024), lambda i: (i, 0))
    vec = pl.BlockSpec((1, LANES), lambda i: (0, 0))
    return pl.pallas_call(
        body, name=name, grid=(t // tm,), in_specs=[row, row, pl.BlockSpec((tm, 1024), lambda i: (i, 3)), vec],
        out_specs=[row, row, vec],
        out_shape=[jax.ShapeDtypeStruct((t, 1024), F32), jax.ShapeDtypeStruct((t, 1024), BF16),
                   jax.ShapeDtypeStruct((1, LANES), F32)],
        compiler_params=_params(("arbitrary",)),
    )(dy, o, proj_c, o_norm)


PAIR = 2 * CHUNK
GDN_HP = 2


def _bdot(a, b, dims=NN):
    return _dot(a.astype(BF16), b.astype(BF16), dims)


def _pair_common(q, k, v, gci, gcj, beta):
    ri = lax.broadcasted_iota(jnp.int32, (PAIR, PAIR), 0)
    ci = lax.broadcasted_iota(jnp.int32, (PAIR, PAIR), 1)
    same = (ri // CHUNK) == (ci // CHUNK)
    incl = same & (ri >= ci)
    strict = same & (ri > ci)
    gamma = jnp.where(incl, jnp.exp(jnp.minimum(gci - gcj, 0.0)), 0.0)
    kb = k * beta
    m = jnp.where(strict, _bdot(kb, k, NT) * gamma, 0.0)
    tm_ = (ri == ci).astype(F32) - m
    pw = _bdot(m, m)
    for it in range(5):
        tm_ = tm_ + _bdot(tm_, pw)
        if it < 4:
            pw = _bdot(pw, pw)
    eg = jnp.exp(gci)
    vb = v * beta
    kbe = kb * eg
    uw = _bdot(tm_, jnp.concatenate([vb, kbe], axis=1))
    attn = jnp.where(incl, _bdot(q, k, NT) * gamma, 0.0)
    gl_a, gl_b = gci[CHUNK - 1:CHUNK, :], gci[PAIR - 1:PAIR, :]
    first = lax.broadcasted_iota(jnp.int32, (PAIR, LANES), 0) < CHUNK
    ek = jnp.exp(jnp.where(first, gl_a, gl_b) - gci)
    return dict(incl=incl, strict=strict, gamma=gamma, kb=kb, m=m, tm=tm_, eg=eg, vb=vb, kbe=kbe,
                u=uw[:, :LANES], w=uw[:, LANES:], attn=attn, qd=q * eg, ek=ek, kd=k * ek,
                glast_a=jnp.exp(gl_a), glast_b=jnp.exp(gl_b))


def _gdn_specs(t, ts, order):
    nc = ts // CHUNK
    blk = pl.BlockSpec((ts, GDN_HP * LANES), lambda h, s: (order(s), h))
    row = pl.BlockSpec((GDN_HP, 1, ts), lambda h, s: (h, 0, order(s)))
    st = pl.BlockSpec((GDN_HP, nc, LANES, LANES), lambda h, s: (h, order(s), 0, 0))
    return blk, row, st


def _gdn_fwd(q, k, v, gcb, gct, bb, *, name):
    t = q.shape[0]
    ts = _tile(t, GDN_TILE)
    npair = ts // PAIR

    def body(q_ref, k_ref, v_ref, g_ref, gt_ref, b_ref, o_ref, st_ref, s_sc):
        @pl.when(pl.program_id(1) == 0)
        def _():
            s_sc[...] = jnp.zeros_like(s_sc)

        def pair(pi, _):
            rows = pl.ds(pl.multiple_of(pi * PAIR, PAIR), PAIR)
            for hh in range(GDN_HP):
                sl = slice(hh * LANES, (hh + 1) * LANES)
                cm = _pair_common(q_ref[rows, sl], k_ref[rows, sl], v_ref[rows, sl], g_ref[rows, sl],
                                  gt_ref[hh, :, rows], b_ref[rows, sl])
                u, w, qd, kd = cm["u"], cm["w"], cm["qd"], cm["kd"]
                s0 = s_sc[hh]
                st_ref[hh, 2 * pi] = s0
                r0 = _bdot(jnp.concatenate([w[:CHUNK], qd[:CHUNK]], axis=0), s0)
                vn_a = u[:CHUNK] - r0[:CHUNK]
                s1 = s0 * cm["glast_a"] + _bdot(kd[:CHUNK], vn_a, TN)
                st_ref[hh, 2 * pi + 1] = s1
                r1 = _bdot(jnp.concatenate([w[CHUNK:], qd[CHUNK:]], axis=0), s1)
                vn_b = u[CHUNK:] - r1[:CHUNK]
                s_sc[hh] = s1 * cm["glast_b"] + _bdot(kd[CHUNK:], vn_b, TN)
                vn = jnp.concatenate([vn_a, vn_b], axis=0)
                o_ref[rows, sl] = jnp.concatenate([r0[CHUNK:], r1[CHUNK:]], axis=0) + _bdot(cm["attn"], vn)
            return 0

        lax.fori_loop(0, npair, pair, 0)

    blk, row, st = _gdn_specs(t, ts, lambda s: s)
    return pl.pallas_call(
        body, name=name, grid=(GDN_HEADS // GDN_HP, t // ts), in_specs=[blk, blk, blk, blk, row, blk],
        out_specs=[blk, st],
        out_shape=[jax.ShapeDtypeStruct((t, 1024), F32), jax.ShapeDtypeStruct((GDN_HEADS, t // CHUNK, LANES, LANES), F32)],
        scratch_shapes=[pltpu.VMEM((GDN_HP, LANES, LANES), F32)],
        compiler_params=_params(("parallel", "arbitrary")),
    )(q, k, v, gcb, gct, bb)


def _gdn_bwd(q, k, v, gcb, gct, bb, do, states, *, name):
    t = q.shape[0]
    ts = _tile(t, GDN_TILE)
    npair = ts // PAIR
    ns = t // ts
    c = CHUNK

    def body(q_ref, k_ref, v_ref, g_ref, gt_ref, b_ref, do_ref, st_ref, dq_ref, dk_ref, dv_ref, dg_ref, db_ref, ds_sc):
        @pl.when(pl.program_id(1) == 0)
        def _():
            ds_sc[...] = jnp.zeros_like(ds_sc)

        rowsum = lambda x: jnp.sum(x, axis=-1, keepdims=True)
        total = lambda x: jnp.sum(rowsum(x), axis=0, keepdims=True)
        cat0 = lambda *xs: jnp.concatenate(xs, axis=0)
        cat1 = lambda *xs: jnp.concatenate(xs, axis=1)

        def pair(step, _):
            pi = npair - 1 - step
            rows = pl.ds(pl.multiple_of(pi * PAIR, PAIR), PAIR)
            for hh in range(GDN_HP):
                sl = slice(hh * LANES, (hh + 1) * LANES)
                qv, kv, vv, beta = q_ref[rows, sl], k_ref[rows, sl], v_ref[rows, sl], b_ref[rows, sl]
                cm = _pair_common(qv, kv, vv, g_ref[rows, sl], gt_ref[hh, :, rows], beta)
                dov = do_ref[rows, sl]
                s0, s1 = st_ref[hh, 2 * pi], st_ref[hh, 2 * pi + 1]
                ds2 = ds_sc[hh]
                u, w, qd, kd, attn = cm["u"], cm["w"], cm["qd"], cm["kd"], cm["attn"]
                tmat, gamma, eg = cm["tm"], cm["gamma"], cm["eg"]
                vn_a = u[:c] - _bdot(w[:c], s0)
                vn_b = u[c:] - _bdot(w[c:], s1)
                vn = cat0(vn_a, vn_b)
                dvn_att = _bdot(attn, dov, TN)
                dattn = jnp.where(cm["incl"], _bdot(dov, vn, NT), 0.0)
                dvn_b = dvn_att[c:] + _bdot(kd[c:], ds2)
                rb = _bdot(cat0(dov[c:], dvn_b), s1, NT)
                dkd_b = _bdot(vn_b, ds2, NT)
                dgl_b = total(ds2 * s1)
                ds1 = ds2 * cm["glast_b"] + _bdot(cat0(qd[c:], w[c:]), cat0(dov[c:], -dvn_b), TN)
                dvn_a = dvn_att[:c] + _bdot(kd[:c], ds1)
                ra = _bdot(cat0(dov[:c], dvn_a), s0, NT)
                dkd_a = _bdot(vn_a, ds1, NT)
                dgl_a = total(ds1 * s0)
                ds_sc[hh] = ds1 * cm["glast_a"] + _bdot(cat0(qd[:c], w[:c]), cat0(dov[:c], -dvn_a), TN)
                dvn = cat0(dvn_a, dvn_b)
                dqd = cat0(ra[:c], rb[:c])
                dw = -cat0(ra[c:], rb[c:])
                dkd = cat0(dkd_a, dkd_b)
                dvw = cat1(dvn, dw)
                dvbk = _bdot(tmat, dvw, TN)
                dvb, dkbe = dvbk[:, :LANES], dvbk[:, LANES:]
                dt_ = _bdot(dvw, cat1(cm["vb"], cm["kbe"]), NT)
                da = -_bdot(_bdot(tmat, dt_, TN), tmat, NT)
                dm = jnp.where(cm["strict"], da, 0.0)
                dkk = dm * gamma
                dqk = dattn * gamma
                z = dm * cm["m"] + dattn * attn
                dkb = _bdot(dkk, kv) + dkbe * eg
                dk = _bdot(cat0(dkk, dqk), cat0(cm["kb"], qv), TN) + dkd * cm["ek"] + dkb * beta
                dq = _bdot(dqk, kv) + dqd * eg
                zh = z.astype(BF16)
                zl = (z - zh.astype(F32)).astype(BF16)
                colsum = _dot(cat0(zh, zl), jnp.ones((2 * PAIR, LANES), BF16), TN)
                dkd_kd = dkd * kd
                dgc = rowsum(z) - colsum + rowsum(dqd * qd) - rowsum(dkd_kd) + rowsum(dkbe * cm["kbe"])
                last_a = total(dkd_kd[:c]) + dgl_a * cm["glast_a"]
                last_b = total(dkd_kd[c:]) + dgl_b * cm["glast_b"]
                ri = lax.broadcasted_iota(jnp.int32, (PAIR, LANES), 0)
                dgc = dgc + jnp.where(ri == c - 1, last_a, 0.0) + jnp.where(ri == PAIR - 1, last_b, 0.0)
                dq_ref[rows, sl] = dq
                dk_ref[rows, sl] = dk
                dv_ref[rows, sl] = dvb * beta
                db_ref[rows, sl] = jnp.broadcast_to(rowsum(dkb * kv) + rowsum(dvb * vv), (PAIR, LANES))
                dg_ref[rows, sl] = dgc
            return 0

        lax.fori_loop(0, npair, pair, 0)

    blk, row, st = _gdn_specs(t, ts, lambda s: ns - 1 - s)
    out = jax.ShapeDtypeStruct((t, 1024), F32)
    return pl.pallas_call(
        body, name=name, grid=(GDN_HEADS // GDN_HP, ns), in_specs=[blk, blk, blk, blk, row, blk, blk, st],
        out_specs=[blk] * 5, out_shape=[out] * 5, scratch_shapes=[pltpu.VMEM((GDN_HP, LANES, LANES), F32)],
        compiler_params=_params(("parallel", "arbitrary")),
    )(q, k, v, gcb, gct, bb, do, states)


def _loss_head(h, g, target, *, name):
    t, d = h.shape
    tm = _tile(t, 2 * ROW_TILE)

    def body(h_ref, g_ref, t_ref, dh_ref, dhb_ref, dg_ref, loss_ref):
        i = pl.program_id(0)
        x = h_ref[...]
        r = lax.rsqrt(jnp.mean(x * x, axis=-1, keepdims=True) + RMS_EPS)
        xh = x * r
        err = xh * g_ref[...] - t_ref[...]
        dy = err * (1.0 / d)
        dxh = dy * g_ref[...]
        dh = r * (dxh - xh * jnp.mean(dxh * xh, axis=-1, keepdims=True))
        dh_ref[...] = dh
        dhb_ref[...] = dh.astype(BF16)

        @pl.when(i == 0)
        def _():
            dg_ref[...] = jnp.zeros_like(dg_ref)
            loss_ref[...] = jnp.zeros_like(loss_ref)

        dg_ref[...] += jnp.sum(dy * xh, axis=0, keepdims=True)
        part = 0.5 * jnp.sum(jnp.mean(err * err, axis=-1, keepdims=True), axis=0, keepdims=True)
        loss_ref[...] += jnp.broadcast_to(part, loss_ref.shape)

    row = pl.BlockSpec((tm, d), lambda i: (i, 0))
    vec = pl.BlockSpec((1, d), lambda i: (0, 0))
    return pl.pallas_call(
        body, name=name, grid=(t // tm,), in_specs=[row, vec, row],
        out_specs=[row, row, vec, pl.BlockSpec((8, LANES), lambda i: (0, 0))],
        out_shape=[jax.ShapeDtypeStruct((t, d), F32), jax.ShapeDtypeStruct((t, d), BF16),
                   jax.ShapeDtypeStruct((1, d), F32), jax.ShapeDtypeStruct((8, LANES), F32)],
        compiler_params=_params(("arbitrary",)),
    )(h, g, target)


def _pad_cols(w, n):
    return jnp.pad(w, ((0, 0), (0, n - w.shape[1])))


def _layout_weights(w):
    z = lambda r, c: jnp.zeros((r, c), F32)
    wi = w["w_in_ab"]
    win = jnp.concatenate([wi[:, :384], z(1024, 64), wi[:, 384:416], z(1024, 32), wi[:, 416:]], axis=1)
    wq = jnp.pad(w["w_q_b"].reshape(MLA_Q_RANK, MLA_HEADS, 96), ((0, 0), (0, 0), (0, 32))).reshape(MLA_Q_RANK, 1024)
    kv3 = w["w_kv_b"].reshape(MLA_KV_RANK, MLA_HEADS, 128)
    wk = jnp.pad(kv3[..., :MLA_NOPE], ((0, 0), (0, 0), (0, 64))).reshape(MLA_KV_RANK, 1024)
    wv = kv3[..., MLA_NOPE:].reshape(MLA_KV_RANK, 512)
    pw = w["pool_w"]
    rows = []
    for g in range(4):
        rows.append(jnp.concatenate([pw[g] if j == g else z(128, 128) for j in range(4)], axis=1))
    wpool = jnp.concatenate(rows, axis=0)
    half = MLA_ROPE // 2
    inv = 1.0 / (ROPE_THETA ** (jnp.arange(half, dtype=F32) / half))
    inv_lane = jnp.concatenate([jnp.zeros((MLA_NOPE,), F32), inv, inv, jnp.zeros((32,), F32)]).reshape(1, LANES)
    return dict(
        win=win.astype(BF16), wq=wq.astype(BF16), wk=wk.astype(BF16), wv=wv.astype(BF16), wpool=wpool.astype(BF16),
        wout_ab=w["w_out_ab"].astype(BF16), winc=_pad_cols(w["w_in_c"], IN_C_PAD).astype(BF16),
        wout_c=w["w_out_c"].astype(BF16), conv_w=w["conv_w"], a_log=_pad_cols(w["a_log"], LANES),
        dt_bias=_pad_cols(w["dt_bias"], LANES), inv_lane=inv_lane,
        norm_ab=w["norm_ab"], q_a_norm=w["q_a_norm"], kv_a_norm=w["kv_a_norm"], pool_scale=w["pool_scale"],
        norm_c=w["norm_c"], o_norm=w["o_norm"], final_norm=w["final_norm"],
    )


def _unlayout_grads(g):
    dwin = g["win"]
    dkv = jnp.concatenate([g["wk"].reshape(MLA_KV_RANK, MLA_HEADS, 128)[..., :MLA_NOPE],
                           g["wv"].reshape(MLA_KV_RANK, MLA_HEADS, MLA_V)], axis=-1).reshape(MLA_KV_RANK, 1024)
    return dict(
        norm_ab=g["norm_ab"],
        w_in_ab=jnp.concatenate([dwin[:, :384], dwin[:, 448:480], dwin[:, 512:]], axis=1),
        q_a_norm=g["q_a_norm"],
        w_q_b=g["wq"].reshape(MLA_Q_RANK, MLA_HEADS, 128)[..., :96].reshape(MLA_Q_RANK, 768),
        kv_a_norm=g["kv_a_norm"],
        w_kv_b=dkv,
        pool_w=jnp.stack([g["wpool"][i * 128:(i + 1) * 128, i * 128:(i + 1) * 128] for i in range(4)]),
        pool_scale=g["pool_scale"],
        w_out_ab=g["wout_ab"],
        norm_c=g["norm_c"],
        w_in_c=g["winc"][:, :4112],
        conv_w=g["conv_w"],
        a_log=g["a_log"][:, :GDN_HEADS],
        dt_bias=g["dt_bias"][:, :GDN_HEADS],
        o_norm=g["o_norm"],
        w_out_c=g["wout_c"],
        final_norm=g["final_norm"],
    )


def _local_step(x, pos, target, lw):
    mm = _matmul
    hn = _rms_fwd(x, lw["norm_ab"], name="rms_ab")
    proj = mm(hn, lw["win"], "nn", name="in_ab")
    qn, kvn, kr, d, cos_t, sin_t = _ab_prep(proj, pos, lw["inv_lane"], lw["q_a_norm"], lw["kv_a_norm"], name="ab_prep")
    qraw = mm(qn, lw["wq"], "nn", name="q_up")
    kvk = mm(kvn, lw["wk"], "nn", name="k_up")
    v = mm(kvn, lw["wv"], "nn", name="v_up", out_dtype=BF16)
    ybraw = mm(d, lw["wpool"], "nn", name="pool_mix")
    q, k = _qk_rope(qraw, kvk, kr, cos_t, sin_t, name="qk_rope")
    o, lse = _attn_fwd(q, k, v, name="attn_fwd")
    y = _gate_fwd(o, ybraw, proj, lw["pool_scale"], name="gate_ab")
    h1 = mm(y, lw["wout_ab"], "nn", name="out_ab", add=x)
    hn1 = _rms_fwd(h1, lw["norm_c"], name="rms_c")
    proj_c = mm(hn1, lw["winc"], "nn", name="in_c")
    q2, k2, v2, gb, bb, gt = _c_prep(proj_c, lw["conv_w"], lw["a_log"], lw["dt_bias"], name="c_prep")
    gt = gt.reshape(GDN_HEADS, 1, gt.shape[1])
    o2, states = _gdn_fwd(q2, k2, v2, gb, gt, bb, name="gdn_fwd")
    y2 = _o_gate_fwd(o2, proj_c, lw["o_norm"], name="gate_c")
    h2 = mm(y2, lw["wout_c"], "nn", name="out_c", add=h1)
    dh2, dh2b, d_final, loss = _loss_head(h2, lw["final_norm"], target, name="loss_head")
    g = {"final_norm": d_final}
    dy2 = mm(dh2b, lw["wout_c"], "nt", name="out_c_dx")
    g["wout_c"] = mm(y2, dh2b, "tn", name="out_c_dw")
    do2, dz2, g["o_norm"] = _o_gate_bwd(dy2, o2, proj_c, lw["o_norm"], name="gate_c_bwd")
    dq2, dk2, dv2, dgb, dbb = _gdn_bwd(q2, k2, v2, gb, gt, bb, do2, states, name="gdn_bwd")
    dproj_c, g["conv_w"], g["a_log"], g["dt_bias"] = _c_prep_bwd(
        proj_c, lw["conv_w"], lw["a_log"], lw["dt_bias"], dq2, dk2, dv2, dgb, dbb, dz2, name="c_prep_bwd")
    dhn1 = mm(dproj_c, lw["winc"], "nt", name="in_c_dx")
    g["winc"] = mm(hn1, dproj_c, "tn", name="in_c_dw")
    dh1, dh1b, g["norm_c"] = _rms_bwd(h1, lw["norm_c"], dhn1, dh2, name="rms_c_bwd", with_bf16=True)
    dy = mm(dh1b, lw["wout_ab"], "nt", name="out_ab_dx")
    g["wout_ab"] = mm(y, dh1b, "tn", name="out_ab_dw")
    do, delta, dyb, dz, g["pool_scale"] = _gate_bwd(dy, o, ybraw, proj, lw["pool_scale"], name="gate_ab_bwd")
    dq, dk, dv = _attn_bwd(q, k, v, do, lse, delta, name="attn_bwd")
    dd = mm(dyb, lw["wpool"], "nt", name="pool_mix_dx")
    g["wpool"] = mm(d, dyb, "tn", name="pool_mix_dw")
    dqraw, dkb, dkr = _qk_rope_bwd(dq, dk, cos_t, sin_t, name="qk_rope_bwd")
    dqn = mm(dqraw, lw["wq"], "nt", name="q_up_dx")
    g["wq"] = mm(qn, dqraw, "tn", name="q_up_dw")
    dkvn_k = mm(dkb, lw["wk"], "nt", name="k_up_dx")
    dkvn_v = mm(dv, lw["wv"], "nt", name="v_up_dx")
    g["wk"] = mm(kvn, dkb, "tn", name="k_up_dw")
    g["wv"] = mm(kvn, dv, "tn", name="v_up_dw")
    dproj, g["q_a_norm"], g["kv_a_norm"] = _ab_prep_bwd(
        proj, lw["q_a_norm"], lw["kv_a_norm"], dqn, dkvn_k, dkvn_v, dkr, dd, dz, name="ab_prep_bwd")
    dhn = mm(dproj, lw["win"], "nt", name="in_ab_dx")
    g["win"] = mm(hn, dproj, "tn", name="in_ab_dw")
    dx, g["norm_ab"] = _rms_bwd(x, lw["norm_ab"], dhn, dh1, name="rms_ab_bwd", with_bf16=False)
    return loss, dx, g


_HBM = pl.BlockSpec(memory_space=pltpu.HBM)


def _place():
    return lax.axis_index("x"), lax.axis_index("y"), lax.axis_index("c")


def _flip(v, f):
    return 1 - v if f else v


_CHIP_FLIPS = ((1, 0), (0, 1), (1, 1))
_DEV_FLIPS = tuple((fx, fy, fc) for fx in (0, 1) for fy in (0, 1) for fc in (0, 1) if fx or fy or fc)


def _rcopy(src, dst, send_sems, recv_sems, k, to):
    return pltpu.make_async_remote_copy(src_ref=src, dst_ref=dst, send_sem=send_sems.at[k], recv_sem=recv_sems.at[k],
                                        device_id=to, device_id_type=MESH)


def _gather_weights(wb, ws):
    _, rh, _ = wb.shape
    rs = ws.shape[0]

    def body(wb_ref, ws_ref, gb_ref, gs_ref, send_sems, recv_sems, local_sems):
        x, y, c = _place()
        j0 = 2 * x + y
        sib = (x, y, 1 - c)
        chips = [(_flip(x, fx), _flip(y, fy)) for fx, fy in _CHIP_FLIPS]
        own_b = pltpu.make_async_copy(wb_ref, gb_ref.at[j0], local_sems.at[0])
        own_s = pltpu.make_async_copy(ws_ref, gs_ref.at[j0], local_sems.at[1])
        own_b.start()
        own_s.start()
        sends = []
        for k, (px, py) in enumerate(chips):
            sends.append(_rcopy(wb_ref.at[c], gb_ref.at[j0, c], send_sems, recv_sems, k, (px, py, c)))
            sends.append(_rcopy(ws_ref, gs_ref.at[j0], send_sems, recv_sems, 6 + k, (px, py, c)))
        for cp in sends:
            cp.start()
        for k, (px, py) in enumerate(chips):
            jk = 2 * px + py
            _rcopy(wb_ref.at[c], gb_ref.at[jk, c], send_sems, recv_sems, k, (px, py, c)).wait_recv()
            fwd = _rcopy(gb_ref.at[jk, c], gb_ref.at[jk, c], send_sems, recv_sems, 3 + k, sib)
            fwd.start()
            sends.append(fwd)
        for k, (px, py) in enumerate(chips):
            jk = 2 * px + py
            _rcopy(wb_ref.at[c], gb_ref.at[jk, 1 - c], send_sems, recv_sems, 3 + k, sib).wait_recv()
            _rcopy(ws_ref, gs_ref.at[jk], send_sems, recv_sems, 6 + k, (px, py, c)).wait_recv()
        for cp in sends:
            cp.wait_send()
        own_b.wait()
        own_s.wait()

    return pl.pallas_call(
        body, name="gather_weights", in_specs=[_HBM, _HBM], out_specs=[_HBM, _HBM],
        out_shape=[jax.ShapeDtypeStruct((4, 2, rh, LANES), BF16), jax.ShapeDtypeStruct((4, rs, LANES), F32)],
        scratch_shapes=[pltpu.SemaphoreType.DMA((9,)), pltpu.SemaphoreType.DMA((9,)), pltpu.SemaphoreType.DMA((2,))],
    )(wb, ws)


def _sibling_swap(a, *, name):
    def body(a_ref, o_ref, send_sem, recv_sem):
        x, y, c = _place()
        cp = pltpu.make_async_remote_copy(src_ref=a_ref, dst_ref=o_ref, send_sem=send_sem, recv_sem=recv_sem,
                                          device_id=(x, y, 1 - c), device_id_type=MESH)
        cp.start()
        cp.wait()

    return pl.pallas_call(
        body, name=name, in_specs=[_HBM], out_specs=_HBM, out_shape=jax.ShapeDtypeStruct(a.shape, a.dtype),
        scratch_shapes=[pltpu.SemaphoreType.DMA, pltpu.SemaphoreType.DMA],
    )(a)


def _chip_exchange(p, small):
    _, rh, _ = p.shape
    rs = small.shape[0]

    def body(p_ref, s_ref, l_ref, ls_ref, send_sems, recv_sems, local_sems):
        x, y, c = _place()
        j0 = 2 * x + y
        d0 = 2 * j0 + c
        own_p = pltpu.make_async_copy(p_ref.at[j0], l_ref.at[j0], local_sems.at[0])
        own_s = pltpu.make_async_copy(s_ref, ls_ref.at[d0], local_sems.at[1])
        own_p.start()
        own_s.start()
        sends = []
        for k, (fx, fy) in enumerate(_CHIP_FLIPS):
            px, py = _flip(x, fx), _flip(y, fy)
            sends.append(_rcopy(p_ref.at[2 * px + py], l_ref.at[j0], send_sems, recv_sems, k, (px, py, c)))
        for k, (fx, fy, fc) in enumerate(_DEV_FLIPS):
            peer = (_flip(x, fx), _flip(y, fy), _flip(c, fc))
            sends.append(_rcopy(s_ref, ls_ref.at[d0], send_sems, recv_sems, 3 + k, peer))
        for cp in sends:
            cp.start()
        for k, (fx, fy) in enumerate(_CHIP_FLIPS):
            px, py = _flip(x, fx), _flip(y, fy)
            _rcopy(p_ref.at[j0], l_ref.at[2 * px + py], send_sems, recv_sems, k, (px, py, c)).wait_recv()
        for k, (fx, fy, fc) in enumerate(_DEV_FLIPS):
            px, py, pc = _flip(x, fx), _flip(y, fy), _flip(c, fc)
            _rcopy(s_ref, ls_ref.at[4 * px + 2 * py + pc], send_sems, recv_sems, 3 + k, (px, py, pc)).wait_recv()
        for cp in sends:
            cp.wait_send()
        own_p.wait()
        own_s.wait()

    return pl.pallas_call(
        body, name="chip_exchange", in_specs=[_HBM, _HBM], out_specs=[_HBM, _HBM],
        out_shape=[jax.ShapeDtypeStruct((4, rh, LANES), F32), jax.ShapeDtypeStruct((8, rs, LANES), F32)],
        scratch_shapes=[pltpu.SemaphoreType.DMA((10,)), pltpu.SemaphoreType.DMA((10,)), pltpu.SemaphoreType.DMA((2,))],
    )(p, small)


def _sum_slots(a, *, name):
    n, rows, _ = a.shape
    tr = _tile(rows, 1024)

    def body(a_ref, o_ref):
        acc = a_ref[0]
        for s in range(1, n):
            acc = acc + a_ref[s]
        o_ref[...] = acc

    return pl.pallas_call(
        body, name=name, grid=(rows // tr,), in_specs=[pl.BlockSpec((n, tr, LANES), lambda i: (0, i, 0))],
        out_specs=pl.BlockSpec((tr, LANES), lambda i: (i, 0)), out_shape=jax.ShapeDtypeStruct((rows, LANES), F32),
        compiler_params=_params(("parallel",)),
    )(a)


def _adamw(g, w, m, v, *, name):
    rows = g.shape[0]
    tr = _tile(rows, 1024)
    c1 = 1.0 - ADAM_B1 ** ADAM_STEP
    c2 = 1.0 - ADAM_B2 ** ADAM_STEP

    def body(g_ref, w_ref, m_ref, v_ref, d_ref, mo_ref, vo_ref):
        gv = g_ref[...]
        mn = ADAM_B1 * m_ref[...] + (1.0 - ADAM_B1) * gv
        vn = ADAM_B2 * v_ref[...] + (1.0 - ADAM_B2) * (gv * gv)
        mo_ref[...] = mn
        vo_ref[...] = vn
        d_ref[...] = -ADAM_LR * ((mn / c1) / (jnp.sqrt(vn / c2) + ADAM_EPS) + ADAM_WD * w_ref[...])

    blk = pl.BlockSpec((tr, LANES), lambda i: (i, 0))
    out = jax.ShapeDtypeStruct((rows, LANES), F32)
    return pl.pallas_call(
        body, name=name, grid=(rows // tr,), in_specs=[blk] * 4, out_specs=[blk] * 3, out_shape=[out] * 3,
        compiler_params=_params(("parallel",)),
    )(g, w, m, v)


_SHARDED = (("w_in_ab", (1024, 488), 1), ("w_q_b", (256, 192), 1), ("w_kv_b", (128, 256), 1),
            ("w_out_ab", (256, 1024), 0), ("w_in_c", (1024, 1028), 1), ("w_out_c", (256, 1024), 0),
            ("conv_w", (4, 768), 1), ("norm_c", (1, 256), 1))
_N_BF16 = 6
_REPLICATED = (("norm_ab", (1, 1024)), ("q_a_norm", (1, 256)), ("kv_a_norm", (1, 128)), ("pool_w", (4, 128, 128)),
               ("pool_scale", (1, 512)), ("a_log", (1, 8)), ("dt_bias", (1, 8)), ("o_norm", (1, 128)),
               ("final_norm", (1, 1024)))
_ALL_NAMES = ("norm_ab", "w_in_ab", "q_a_norm", "w_q_b", "kv_a_norm", "w_kv_b", "pool_w", "pool_scale", "w_out_ab",
              "norm_c", "w_in_c", "conv_w", "a_log", "dt_bias", "o_norm", "w_out_c", "final_norm")


def _rows_of(shape):
    return max(1, math.prod(shape) // LANES)


def _as_rows(a):
    n = a.size
    if n % LANES:
        a = jnp.pad(a.reshape(1, n), ((0, 0), (0, LANES - n % LANES)))
    return a.reshape(-1, LANES)


def _pack(parts, total_rows):
    rows = jnp.concatenate([_as_rows(p) for p in parts], axis=0)
    return jnp.pad(rows, ((0, total_rows - rows.shape[0]), (0, 0)))


def _unpack(packed, spec):
    out, off = [], 0
    lead = packed.shape[:-2]
    for shape in spec:
        r = _rows_of(shape)
        blk = packed[..., off:off + r, :].reshape(lead + (r * LANES,))[..., :math.prod(shape)]
        out.append(blk.reshape(lead + tuple(shape)))
        off += r
    return out


def _round_up(n, m):
    return -(-n // m) * m


_SH_SHAPES = tuple(s for _, s, _ in _SHARDED)
_RB = sum(_rows_of(s) for s in _SH_SHAPES[:_N_BF16])
_RS = _round_up(sum(_rows_of(s) for s in _SH_SHAPES[_N_BF16:]), 8)
_RG = _round_up(sum(_rows_of(s) for s in _SH_SHAPES), 16)
_REP_SHAPES = tuple(s for _, s in _REPLICATED)
_RR = _round_up(sum(_rows_of(s) for s in _REP_SHAPES) + 1, 8)


def kernel(x, positions, norm_ab, w_in_ab, q_a_norm, w_q_b, kv_a_norm, w_kv_b, pool_w, pool_scale, w_out_ab, norm_c, w_in_c, conv_w, a_log, dt_bias, o_norm, w_out_c, final_norm, loss_target, m_norm_ab, m_w_in_ab, m_q_a_norm, m_w_q_b, m_kv_a_norm, m_w_kv_b, m_pool_w, m_pool_scale, m_w_out_ab, m_norm_c, m_w_in_c, m_conv_w, m_a_log, m_dt_bias, m_o_norm, m_w_out_c, m_final_norm, v_norm_ab, v_w_in_ab, v_q_a_norm, v_w_q_b, v_kv_a_norm, v_w_kv_b, v_pool_w, v_pool_scale, v_w_out_ab, v_norm_c, v_w_in_c, v_conv_w, v_a_log, v_dt_bias, v_o_norm, v_w_out_c, v_final_norm):
    given = dict(locals())
    c = lax.axis_index("c")
    t = x.shape[1]

    def shard_of(prefix, name):
        a = given[prefix + name]
        return a.reshape(a.shape[1:]) if a.ndim > 2 else a.reshape(1, -1)

    sh = [shard_of("", n) for n, _, _ in _SHARDED]
    wb = _pack([a.astype(BF16) for a in sh[:_N_BF16]], _RB).reshape(2, _RB // 2, LANES)
    ws = _pack(sh[_N_BF16:], _RS)
    gb, gs = _gather_weights(wb, ws)
    parts = _unpack(gb.reshape(4, _RB, LANES), _SH_SHAPES[:_N_BF16]) + _unpack(gs, _SH_SHAPES[_N_BF16:])
    full = {}
    for (name, _, axis), p in zip(_SHARDED, parts):
        full[name] = jnp.concatenate([p[j] for j in range(4)], axis=axis)
    for name, _ in _REPLICATED:
        full[name] = shard_of("", name)
    lw = _layout_weights(full)

    loss_tile, dx, g = _local_step(x[0], positions.reshape(t, 1), loss_target[0], lw)
    grads = _unlayout_grads(g)

    per_chip = []
    for j in range(4):
        pieces = []
        for name, shape, axis in _SHARDED:
            n = shape[axis]
            pieces.append(lax.slice_in_dim(grads[name], j * n, (j + 1) * n, axis=axis))
        per_chip.append(_pack(pieces, _RG))
    gfull = jnp.stack(per_chip).reshape(4, 2, _RG // 2, LANES)
    mine = lax.dynamic_index_in_dim(gfull, c, axis=1, keepdims=False)
    other = lax.dynamic_index_in_dim(gfull, 1 - c, axis=1, keepdims=False)
    from_sibling = _sibling_swap(other, name="core_swap_partial")
    pair = jnp.stack([mine, from_sibling]).reshape(2, 4 * (_RG // 2), LANES)
    chip_sum = _sum_slots(pair, name="core_sum").reshape(4, _RG // 2, LANES)
    small = _pack([grads[n] for n, _ in _REPLICATED] + [loss_tile[0:1, :]], _RR)
    landed, small_all = _chip_exchange(chip_sum, small)
    my_half = _sum_slots(landed, name="chip_sum")
    small_sum = _sum_slots(small_all, name="small_sum")
    sib_half = _sibling_swap(my_half, name="core_swap_sum")
    g_shard = jnp.where(c == 0, jnp.concatenate([my_half, sib_half]), jnp.concatenate([sib_half, my_half]))

    w_sh = _pack(sh, _RG)
    m_sh = _pack([shard_of("m_", n) for n, _, _ in _SHARDED], _RG)
    v_sh = _pack([shard_of("v_", n) for n, _, _ in _SHARDED], _RG)
    d_sh, mo_sh, vo_sh = _adamw(g_shard, w_sh, m_sh, v_sh, name="adamw_sharded")
    zero_row = jnp.zeros((1, LANES), F32)
    w_rp = _pack([shard_of("", n) for n, _ in _REPLICATED] + [zero_row], _RR)
    m_rp = _pack([shard_of("m_", n) for n, _ in _REPLICATED] + [zero_row], _RR)
    v_rp = _pack([shard_of("v_", n) for n, _ in _REPLICATED] + [zero_row + 1.0], _RR)
    d_rp, mo_rp, vo_rp = _adamw(small_sum, w_rp, m_rp, v_rp, name="adamw_replicated")

    res = {}
    for key, sh_pack, rp_pack in (("grad", g_shard, small_sum), ("delta", d_sh, d_rp), ("m", mo_sh, mo_rp), ("v", vo_sh, vo_rp)):
        for (name, _, _), a in zip(_SHARDED, _unpack(sh_pack, _SH_SHAPES)):
            res[key, name] = a.reshape(given[name].shape)
        for (name, _), a in zip(_REPLICATED, _unpack(rp_pack, _REP_SHAPES)):
            res[key, name] = a.reshape(given[name].shape)
    loss = small_sum[sum(_rows_of(s) for s in _REP_SHAPES), 0]
    outs = [loss, dx.reshape(x.shape)]
    for key in ("grad", "delta", "m", "v"):
        outs += [res[key, n] for n in _ALL_NAMES]
    return tuple(outs)
```

```python
import functools
import math

import jax
import jax.numpy as jnp
from jax import lax
from jax.experimental import pallas as pl
from jax.experimental.pallas import tpu as pltpu

F32 = jnp.float32
BF16 = jnp.bfloat16
HI = lax.Precision.HIGHEST
MESH = pl.DeviceIdType.MESH

RMS_EPS = 1e-6
D_MODEL = 1024
MLA_HEADS = 8
MLA_Q_RANK = 256
MLA_KV_RANK = 128
MLA_NOPE = 64
MLA_ROPE = 32
MLA_V = 64
ROPE_THETA = 10000.0
POOL_WINDOWS = (2, 4, 8, 16)
POOL_GROUP = 128
POOL_WIDTH = 512
POOL_HALO = 16
GDN_HEADS = 8
GDN_DK = 128
CONV_WIDTH = 4
CONV_HALO = 8
CHUNK = 64
IN_AB_PAD = 2048
IN_C_PAD = 4224
ATT_SCALE = (MLA_NOPE + MLA_ROPE) ** -0.5

ADAM_LR = 0.001
ADAM_B1 = 0.9
ADAM_B2 = 0.999
ADAM_EPS = 1e-08
ADAM_WD = 0.01
ADAM_STEP = 10

LANES = 128
VMEM_LIMIT = 56 * 1024 * 1024

ROW_TILE = 256
ATT_TILE = 512
GDN_TILE = 256
MM_TILE = 512

NN = (((1,), (0,)), ((), ()))
NT = (((1,), (1,)), ((), ()))
TN = (((0,), (0,)), ((), ()))


def _dot(a, b, dims=NN, prec=None):
    return lax.dot_general(a, b, dims, precision=prec, preferred_element_type=F32)


def _tile(n, pref):
    if n <= pref:
        return n
    step = LANES if pref >= LANES else 8
    for t in range(pref - pref % step, 0, -step):
        if n % t == 0:
            return t
    return n


def _params(sem):
    return pltpu.CompilerParams(dimension_semantics=sem, vmem_limit_bytes=VMEM_LIMIT)


def _sigmoid(x):
    return 1.0 / (1.0 + jnp.exp(-x))


def _softplus(x):
    return jnp.maximum(x, 0.0) + jnp.log(1.0 + jnp.exp(-jnp.abs(x)))


def _matmul(a, b, mode, *, name, out_dtype=F32, add=None, tm=MM_TILE, tn=MM_TILE, tk=2048):
    if mode == "nn":
        (m, k), (k2, n) = a.shape, b.shape
    elif mode == "nt":
        (m, k), (n, k2) = a.shape, b.shape
    else:
        (k, m), (k2, n) = a.shape, b.shape
    assert k == k2, (a.shape, b.shape, mode)
    tm, tn, tk = _tile(m, tm), _tile(n, tn), _tile(k, tk)
    nk = k // tk
    if mode == "tn":
        a_spec = pl.BlockSpec((tk, tm), lambda i, j, kk: (kk, i))
    else:
        a_spec = pl.BlockSpec((tm, tk), lambda i, j, kk: (i, kk))
    if mode == "nt":
        b_spec = pl.BlockSpec((tn, tk), lambda i, j, kk: (j, kk))
    else:
        b_spec = pl.BlockSpec((tk, tn), lambda i, j, kk: (kk, j))
    o_spec = pl.BlockSpec((tm, tn), lambda i, j, kk: (i, j))
    dims = {"nn": NN, "nt": NT, "tn": TN}[mode]
    has_add = add is not None

    def body(*refs):
        if has_add:
            a_ref, b_ref, add_ref, o_ref, acc = refs
        else:
            a_ref, b_ref, o_ref, acc = refs
        kk = pl.program_id(2)

        @pl.when(kk == 0)
        def _():
            acc[...] = jnp.zeros_like(acc)

        acc[...] += _dot(a_ref[...], b_ref[...], dims)

        @pl.when(kk == nk - 1)
        def _():
            o = acc[...]
            if has_add:
                o = o + add_ref[...]
            o_ref[...] = o.astype(out_dtype)

    in_specs = [a_spec, b_spec] + ([o_spec] if has_add else [])
    args = (a, b) + ((add,) if has_add else ())
    return pl.pallas_call(
        body, name=name, grid=(m // tm, n // tn, nk), in_specs=in_specs, out_specs=o_spec,
        out_shape=jax.ShapeDtypeStruct((m, n), out_dtype), scratch_shapes=[pltpu.VMEM((tm, tn), F32)],
        compiler_params=_params(("parallel", "parallel", "arbitrary")),
    )(*args)


def _rms_fwd(h, g, *, name):
    t, d = h.shape
    tm = _tile(t, 2 * ROW_TILE)

    def body(h_ref, g_ref, o_ref):
        x = h_ref[...]
        r = lax.rsqrt(jnp.mean(x * x, axis=-1, keepdims=True) + RMS_EPS)
        o_ref[...] = (x * r * g_ref[...]).astype(BF16)

    return pl.pallas_call(
        body, name=name, grid=(t // tm,),
        in_specs=[pl.BlockSpec((tm, d), lambda i: (i, 0)), pl.BlockSpec((1, d), lambda i: (0, 0))],
        out_specs=pl.BlockSpec((tm, d), lambda i: (i, 0)),
        out_shape=jax.ShapeDtypeStruct((t, d), BF16), compiler_params=_params(("parallel",)),
    )(h, g)


def _rms_bwd(h, g, dy, dres, *, name, with_bf16):
    t, d = h.shape
    tm = _tile(t, 2 * ROW_TILE)

    def body(h_ref, g_ref, dy_ref, dres_ref, *outs):
        i = pl.program_id(0)
        dh_ref, dg_ref = outs[0], outs[-1]
        x = h_ref[...]
        r = lax.rsqrt(jnp.mean(x * x, axis=-1, keepdims=True) + RMS_EPS)
        xh = x * r
        dyv = dy_ref[...].astype(F32)
        dxh = dyv * g_ref[...]
        dx = r * (dxh - xh * jnp.mean(dxh * xh, axis=-1, keepdims=True))
        dh = dres_ref[...] + dx
        dh_ref[...] = dh
        if with_bf16:
            outs[1][...] = dh.astype(BF16)

        @pl.when(i == 0)
        def _():
            dg_ref[...] = jnp.zeros_like(dg_ref)

        dg_ref[...] += jnp.sum(dyv * xh, axis=0, keepdims=True)

    row = pl.BlockSpec((tm, d), lambda i: (i, 0))
    vec = pl.BlockSpec((1, d), lambda i: (0, 0))
    out_shape = [jax.ShapeDtypeStruct((t, d), F32)]
    out_specs = [row]
    if with_bf16:
        out_shape.append(jax.ShapeDtypeStruct((t, d), BF16))
        out_specs.append(row)
    out_shape.append(jax.ShapeDtypeStruct((1, d), F32))
    out_specs.append(vec)
    return pl.pallas_call(
        body, name=name, grid=(t // tm,), in_specs=[row, vec, row, row], out_specs=out_specs,
        out_shape=out_shape, compiler_params=_params(("arbitrary",)),
    )(h, g, dy, dres)


def _rope_partner(x):
    lane = lax.broadcasted_iota(jnp.int32, x.shape, 1)
    swapped = jnp.where(lane < MLA_NOPE + MLA_ROPE // 2, pltpu.roll(x, LANES - 16, 1), pltpu.roll(x, 16, 1))
    return jnp.where((lane >= MLA_NOPE) & (lane < MLA_NOPE + MLA_ROPE), swapped, 0.0)


def _pool_counts(row0, tm, w):
    t_idx = row0 + lax.broadcasted_iota(jnp.int32, (tm, POOL_GROUP), 0)
    return jnp.minimum(t_idx + 1, w).astype(F32)


def _ab_prep(proj, pos, inv_freq, q_a_norm, kv_a_norm, *, name):
    t = proj.shape[0]
    tm = _tile(t, ROW_TILE)
    hb = tm // POOL_HALO

    def body(p_ref, halo_ref, pos_ref, inv_ref, qg_ref, kg_ref, qn_ref, kvn_ref, kr_ref, d_ref, cos_ref, sin_ref, ext):
        i = pl.program_id(0)
        ql = p_ref[:, 0:MLA_Q_RANK]
        r = lax.rsqrt(jnp.mean(ql * ql, axis=-1, keepdims=True) + RMS_EPS)
        qn_ref[...] = (ql * r * qg_ref[...]).astype(BF16)
        kl = p_ref[:, MLA_Q_RANK:MLA_Q_RANK + MLA_KV_RANK]
        r = lax.rsqrt(jnp.mean(kl * kl, axis=-1, keepdims=True) + RMS_EPS)
        kvn_ref[...] = (kl * r * kg_ref[...]).astype(BF16)
        ang = pos_ref[...].astype(F32) * inv_ref[...]
        lane = lax.broadcasted_iota(jnp.int32, (tm, LANES), 1)
        in_rope = (lane >= MLA_NOPE) & (lane < MLA_NOPE + MLA_ROPE)
        cos_t = jnp.where(in_rope, jnp.cos(ang), 1.0)
        sin_t = jnp.where(in_rope, jnp.sin(ang), 0.0)
        sin_t = jnp.where(lane < MLA_NOPE + MLA_ROPE // 2, -sin_t, sin_t)
        cos_ref[...] = cos_t
        sin_ref[...] = sin_t
        kr = p_ref[:, 384:512]
        kr_ref[...] = kr * cos_t + _rope_partner(kr) * sin_t
        xp = p_ref[:, 512:1024]
        ext[0:POOL_HALO, :] = jnp.where(i > 0, halo_ref[...], 0.0)
        ext[POOL_HALO:POOL_HALO + tm, :] = xp
        for g, w in enumerate(POOL_WINDOWS):
            lo = g * POOL_GROUP
            acc = ext[POOL_HALO:POOL_HALO + tm, lo:lo + POOL_GROUP]
            for s in range(1, w):
                acc = acc + ext[POOL_HALO - s:POOL_HALO - s + tm, lo:lo + POOL_GROUP]
            cnt = _pool_counts(i * tm, tm, w)
            d_ref[:, lo:lo + POOL_GROUP] = (acc / cnt - xp[:, lo:lo + POOL_GROUP]).astype(BF16)

    row = lambda w: pl.BlockSpec((tm, w), lambda i: (i, 0))
    vec = lambda w: pl.BlockSpec((1, w), lambda i: (0, 0))
    return pl.pallas_call(
        body, name=name, grid=(t // tm,),
        in_specs=[row(1024), pl.BlockSpec((POOL_HALO, POOL_WIDTH), lambda i: (jnp.maximum(i * hb - 1, 0), 1)),
                  pl.BlockSpec((tm, 1), lambda i: (i, 0)), vec(LANES), vec(MLA_Q_RANK), vec(MLA_KV_RANK)],
        out_specs=[row(MLA_Q_RANK), row(MLA_KV_RANK), row(LANES), row(POOL_WIDTH), row(LANES), row(LANES)],
        out_shape=[jax.ShapeDtypeStruct((t, MLA_Q_RANK), BF16), jax.ShapeDtypeStruct((t, MLA_KV_RANK), BF16),
                   jax.ShapeDtypeStruct((t, LANES), F32), jax.ShapeDtypeStruct((t, POOL_WIDTH), BF16),
                   jax.ShapeDtypeStruct((t, LANES), F32), jax.ShapeDtypeStruct((t, LANES), F32)],
        scratch_shapes=[pltpu.VMEM((tm + POOL_HALO, POOL_WIDTH), F32)],
        compiler_params=_params(("parallel",)),
    )(proj, proj, pos, inv_freq, q_a_norm, kv_a_norm)


def _qk_rope(qraw, kvk, kr, cos_t, sin_t, *, name):
    t = qraw.shape[0]
    tm = _tile(t, 2 * ROW_TILE)

    def body(q_ref, k_ref, kr_ref, c_ref, s_ref, qo_ref, ko_ref):
        c, s, krv = c_ref[...], s_ref[...], kr_ref[...]
        for h in range(MLA_HEADS):
            sl = slice(h * LANES, (h + 1) * LANES)
            q = q_ref[:, sl]
            qo_ref[:, sl] = (q * c + _rope_partner(q) * s).astype(BF16)
            ko_ref[:, sl] = (k_ref[:, sl] + krv).astype(BF16)

    row = lambda w: pl.BlockSpec((tm, w), lambda i: (i, 0))
    return pl.pallas_call(
        body, name=name, grid=(t // tm,), in_specs=[row(1024), row(1024), row(LANES), row(LANES), row(LANES)],
        out_specs=[row(1024), row(1024)],
        out_shape=[jax.ShapeDtypeStruct((t, 1024), BF16), jax.ShapeDtypeStruct((t, 1024), BF16)],
        compiler_params=_params(("parallel",)),
    )(qraw, kvk, kr, cos_t, sin_t)


def _qk_rope_bwd(dq, dk, cos_t, sin_t, *, name):
    t = dq.shape[0]
    tm = _tile(t, 2 * ROW_TILE)

    def body(dq_ref, dk_ref, c_ref, s_ref, dqo_ref, dko_ref, dkr_ref):
        c, s = c_ref[...], s_ref[...]
        lane = lax.broadcasted_iota(jnp.int32, (tm, LANES), 1)
        in_rope = (lane >= MLA_NOPE) & (lane < MLA_NOPE + MLA_ROPE)
        dkr = jnp.zeros((tm, LANES), F32)
        for h in range(MLA_HEADS):
            sl = slice(h * LANES, (h + 1) * LANES)
            g = dq_ref[:, sl]
            dqo_ref[:, sl] = (g * c + _rope_partner(g * s)).astype(BF16)
            gk = dk_ref[:, sl]
            dko_ref[:, sl] = gk.astype(BF16)
            dkr = dkr + jnp.where(in_rope, gk, 0.0)
        dkr_ref[...] = dkr * c + _rope_partner(dkr * s)

    row = lambda w: pl.BlockSpec((tm, w), lambda i: (i, 0))
    return pl.pallas_call(
        body, name=name, grid=(t // tm,), in_specs=[row(1024), row(1024), row(LANES), row(LANES)],
        out_specs=[row(1024), row(1024), row(LANES)],
        out_shape=[jax.ShapeDtypeStruct((t, 1024), BF16), jax.ShapeDtypeStruct((t, 1024), BF16),
                   jax.ShapeDtypeStruct((t, LANES), F32)],
        compiler_params=_params(("parallel",)),
    )(dq, dk, cos_t, sin_t)


def _ab_prep_bwd(proj, q_a_norm, kv_a_norm, dqn, dkvn_k, dkvn_v, dkr, dd, dz, *, name):
    t = proj.shape[0]
    tm = _tile(t, ROW_TILE)
    hb = tm // POOL_HALO
    last_halo = t // POOL_HALO - 1
    nt = t // tm

    def body(p_ref, qg_ref, kg_ref, dqn_ref, dk1_ref, dk2_ref, dkr_ref, dd_ref, ddn_ref, dz_ref,
             dp_ref, dqg_ref, dkg_ref, ext):
        i = pl.program_id(0)

        @pl.when(i == 0)
        def _():
            dqg_ref[...] = jnp.zeros_like(dqg_ref)
            dkg_ref[...] = jnp.zeros_like(dkg_ref)

        def norm_bwd(x, g, dy, dg_ref):
            r = lax.rsqrt(jnp.mean(x * x, axis=-1, keepdims=True) + RMS_EPS)
            xh = x * r
            dxh = dy * g
            dg_ref[...] += jnp.sum(dy * xh, axis=0, keepdims=True)
            return r * (dxh - xh * jnp.mean(dxh * xh, axis=-1, keepdims=True))

        dql = norm_bwd(p_ref[:, 0:MLA_Q_RANK], qg_ref[...], dqn_ref[...], dqg_ref)
        dp_ref[:, 0:MLA_Q_RANK] = dql.astype(BF16)
        dkl = norm_bwd(p_ref[:, MLA_Q_RANK:384], kg_ref[...], dk1_ref[...] + dk2_ref[...], dkg_ref)
        dp_ref[:, MLA_Q_RANK:384] = dkl.astype(BF16)
        dp_ref[:, 384:512] = dkr_ref[...].astype(BF16)
        ddv = dd_ref[...]
        for g, w in enumerate(POOL_WINDOWS):
            lo = g * POOL_GROUP
            ext[0:tm, lo:lo + POOL_GROUP] = ddv[:, lo:lo + POOL_GROUP] / _pool_counts(i * tm, tm, w)
            nxt = ddn_ref[:, lo:lo + POOL_GROUP] / _pool_counts((i + 1) * tm, POOL_HALO, w)
            ext[tm:tm + POOL_HALO, lo:lo + POOL_GROUP] = jnp.where(i < nt - 1, nxt, 0.0)
        for g, w in enumerate(POOL_WINDOWS):
            lo = g * POOL_GROUP
            acc = ext[0:tm, lo:lo + POOL_GROUP]
            for s in range(1, w):
                acc = acc + ext[s:s + tm, lo:lo + POOL_GROUP]
            dp_ref[:, 512 + lo:512 + lo + POOL_GROUP] = (acc - ddv[:, lo:lo + POOL_GROUP]).astype(BF16)
        dp_ref[:, 1024:2048] = dz_ref[...]

    row = lambda w: pl.BlockSpec((tm, w), lambda i: (i, 0))
    vec = lambda w: pl.BlockSpec((1, w), lambda i: (0, 0))
    return pl.pallas_call(
        body, name=name, grid=(nt,),
        in_specs=[row(1024), vec(MLA_Q_RANK), vec(MLA_KV_RANK), row(MLA_Q_RANK), row(MLA_KV_RANK), row(MLA_KV_RANK),
                  row(LANES), row(POOL_WIDTH),
                  pl.BlockSpec((POOL_HALO, POOL_WIDTH), lambda i: (jnp.minimum((i + 1) * hb, last_halo), 0)),
                  row(1024)],
        out_specs=[row(IN_AB_PAD), vec(MLA_Q_RANK), vec(MLA_KV_RANK)],
        out_shape=[jax.ShapeDtypeStruct((t, IN_AB_PAD), BF16), jax.ShapeDtypeStruct((1, MLA_Q_RANK), F32),
                   jax.ShapeDtypeStruct((1, MLA_KV_RANK), F32)],
        scratch_shapes=[pltpu.VMEM((tm + POOL_HALO, POOL_WIDTH), F32)],
        compiler_params=_params(("arbitrary",)),
    )(proj, q_a_norm, kv_a_norm, dqn, dkvn_k, dkvn_v, dkr, dd, dd, dz)


def _gate_fwd(o, ybraw, proj, pool_scale, *, name):
    t = o.shape[0]
    tm = _tile(t, 2 * ROW_TILE)

    def body(o_ref, yb_ref, z_ref, ps_ref, y_ref):
        z = z_ref[...]
        sz = z * _sigmoid(z)
        y_ref[:, 0:512] = (o_ref[...] * sz[:, 0:512]).astype(BF16)
        y_ref[:, 512:1024] = (yb_ref[...] * ps_ref[...] * sz[:, 512:1024]).astype(BF16)

    row = lambda w: pl.BlockSpec((tm, w), lambda i: (i, 0))
    return pl.pallas_call(
        body, name=name, grid=(t // tm,),
        in_specs=[row(512), row(512), pl.BlockSpec((tm, 1024), lambda i: (i, 1)), pl.BlockSpec((1, 512), lambda i: (0, 0))],
        out_specs=row(1024), out_shape=jax.ShapeDtypeStruct((t, 1024), BF16), compiler_params=_params(("parallel",)),
    )(o, ybraw, proj, pool_scale)


def _gate_bwd(dy, o, ybraw, proj, pool_scale, *, name):
    t = o.shape[0]
    tm = _tile(t, ROW_TILE)

    def body(dy_ref, o_ref, yb_ref, z_ref, ps_ref, do_ref, dl_ref, dyb_ref, dz_ref, dps_ref):
        i = pl.program_id(0)
        z = z_ref[...]
        sg = _sigmoid(z)
        sz = z * sg
        dsz = sg * (1.0 + z * (1.0 - sg))
        dyv = dy_ref[...]
        dcat = dyv * sz
        ov = o_ref[...]
        ybs = yb_ref[...] * ps_ref[...]
        dz_ref[:, 0:512] = (dyv[:, 0:512] * ov * dsz[:, 0:512]).astype(BF16)
        dz_ref[:, 512:1024] = (dyv[:, 512:1024] * ybs * dsz[:, 512:1024]).astype(BF16)
        do = dcat[:, 0:512]
        do_ref[...] = do.astype(BF16)
        r_i = lax.broadcasted_iota(jnp.int32, (512, 512), 0) // MLA_V
        c_i = lax.broadcasted_iota(jnp.int32, (512, 512), 1) // MLA_V
        dl_ref[...] = _dot(do * ov, (r_i == c_i).astype(F32), NN, HI)
        dyb_ref[...] = (dcat[:, 512:1024] * ps_ref[...]).astype(BF16)

        @pl.when(i == 0)
        def _():
            dps_ref[...] = jnp.zeros_like(dps_ref)

        dps_ref[...] += jnp.sum(dcat[:, 512:1024] * yb_ref[...], axis=0, keepdims=True)

    row = lambda w: pl.BlockSpec((tm, w), lambda i: (i, 0))
    vec = pl.BlockSpec((1, 512), lambda i: (0, 0))
    return pl.pallas_call(
        body, name=name, grid=(t // tm,),
        in_specs=[row(1024), row(512), row(512), pl.BlockSpec((tm, 1024), lambda i: (i, 1)), vec],
        out_specs=[row(512), row(512), row(512), row(1024), vec],
        out_shape=[jax.ShapeDtypeStruct((t, 512), BF16), jax.ShapeDtypeStruct((t, 512), F32),
                   jax.ShapeDtypeStruct((t, 512), BF16), jax.ShapeDtypeStruct((t, 1024), BF16),
                   jax.ShapeDtypeStruct((1, 512), F32)],
        compiler_params=_params(("arbitrary",)),
    )(dy, o, ybraw, proj, pool_scale)


def _causal_mask(qi, ki, tq, tk):
    row = qi * tq + lax.broadcasted_iota(jnp.int32, (tq, tk), 0)
    col = ki * tk + lax.broadcasted_iota(jnp.int32, (tq, tk), 1)
    return col <= row


def _attn_fwd(q, k, v, *, name):
    t = q.shape[0]
    tq = _tile(t, ATT_TILE)
    nq = t // tq

    def body(q_ref, k_ref, v_ref, o_ref, lse_ref, m_sc, l_sc, acc_sc):
        qi, ki = pl.program_id(1), pl.program_id(2)

        @pl.when(ki == 0)
        def _():
            m_sc[...] = jnp.full_like(m_sc, -jnp.inf)
            l_sc[...] = jnp.zeros_like(l_sc)
            acc_sc[...] = jnp.zeros_like(acc_sc)

        @pl.when(ki <= qi)
        def _():
            mask = _causal_mask(qi, ki, tq, tq)
            vv = v_ref[...]
            for a in range(2):
                sl = slice(a * LANES, (a + 1) * LANES)
                s = _dot(q_ref[:, sl], k_ref[:, sl], NT) * ATT_SCALE
                s = jnp.where(mask, s, -jnp.inf)
                m_prev = m_sc[a]
                m_new = jnp.maximum(m_prev, jnp.max(s, axis=-1, keepdims=True))
                alpha = jnp.exp(m_prev - m_new)
                p = jnp.exp(s - m_new[:, 0:1])
                l_sc[a] = alpha * l_sc[a] + jnp.sum(p, axis=-1, keepdims=True)
                acc_sc[a] = alpha * acc_sc[a] + _dot(p.astype(BF16), vv)
                m_sc[a] = m_new

        @pl.when(ki == qi)
        def _():
            lane = lax.broadcasted_iota(jnp.int32, (tq, LANES), 1)
            first = lane < MLA_V
            o_ref[...] = jnp.where(first, acc_sc[0] / l_sc[0], acc_sc[1] / l_sc[1])
            lse_ref[...] = jnp.where(first, m_sc[0] + jnp.log(l_sc[0]), m_sc[1] + jnp.log(l_sc[1]))

    return pl.pallas_call(
        body, name=name, grid=(MLA_HEADS // 2, nq, nq),
        in_specs=[pl.BlockSpec((tq, 2 * LANES), lambda h, i, j: (i, h)),
                  pl.BlockSpec((tq, 2 * LANES), lambda h, i, j: (jnp.minimum(i, j), h)),
                  pl.BlockSpec((tq, LANES), lambda h, i, j: (jnp.minimum(i, j), h))],
        out_specs=[pl.BlockSpec((tq, LANES), lambda h, i, j: (i, h)), pl.BlockSpec((tq, LANES), lambda h, i, j: (i, h))],
        out_shape=[jax.ShapeDtypeStruct((t, 512), F32), jax.ShapeDtypeStruct((t, 512), F32)],
        scratch_shapes=[pltpu.VMEM((2, tq, LANES), F32), pltpu.VMEM((2, tq, LANES), F32), pltpu.VMEM((2, tq, LANES), F32)],
        compiler_params=_params(("parallel", "parallel", "arbitrary")),
    )(q, k, v)


def _attn_bwd(q, k, v, do, lse, delta, *, name):
    t = q.shape[0]
    tq = _tile(t, ATT_TILE)
    nq = t // tq

    def body(q_ref, k_ref, v_ref, do_ref, lse_ref, dl_ref, dq_ref, dk_ref, dv_ref, dk_sc, dv_sc):
        ki, qi = pl.program_id(1), pl.program_id(2)

        @pl.when((ki == 0) & (qi == 0))
        def _():
            dq_ref[...] = jnp.zeros_like(dq_ref)

        @pl.when(qi == 0)
        def _():
            dk_sc[...] = jnp.zeros_like(dk_sc)
            dv_sc[...] = jnp.zeros_like(dv_sc)

        @pl.when(qi >= ki)
        def _():
            mask = _causal_mask(qi, ki, tq, tq)
            lane = lax.broadcasted_iota(jnp.int32, (tq, LANES), 1)
            vv = v_ref[...]
            dov = do_ref[...]
            rows = pl.ds(pl.multiple_of(qi * tq, tq), tq)
            for a in range(2):
                sl = slice(a * LANES, (a + 1) * LANES)
                mine = (lane < MLA_V) if a == 0 else (lane >= MLA_V)
                col = a * MLA_V
                qa, ka = q_ref[:, sl], k_ref[:, sl]
                s = _dot(qa, ka, NT) * ATT_SCALE
                p = jnp.where(mask, jnp.exp(s - lse_ref[:, col:col + 1]), 0.0)
                pb = p.astype(BF16)
                dv_sc[a] += _dot(pb, dov, TN)
                dp = _dot(jnp.where(mine, dov, jnp.zeros_like(dov)), vv, NT)
                ds = (p * (dp - dl_ref[:, col:col + 1]) * ATT_SCALE).astype(BF16)
                dk_sc[a] += _dot(ds, qa, TN)
                dq_ref[rows, sl] += _dot(ds, ka, NN)

        @pl.when(qi == nq - 1)
        def _():
            lane = lax.broadcasted_iota(jnp.int32, (tq, LANES), 1)
            dk_ref[:, 0:LANES] = dk_sc[0]
            dk_ref[:, LANES:2 * LANES] = dk_sc[1]
            dv_ref[...] = jnp.where(lane < MLA_V, dv_sc[0], dv_sc[1]).astype(BF16)

    qrow = lambda w: pl.BlockSpec((tq, w), lambda h, j, i: (jnp.maximum(i, j), h))
    krow = lambda w: pl.BlockSpec((tq, w), lambda h, j, i: (j, h))
    return pl.pallas_call(
        body, name=name, grid=(MLA_HEADS // 2, nq, nq),
        in_specs=[qrow(2 * LANES), krow(2 * LANES), krow(LANES), qrow(LANES), qrow(LANES), qrow(LANES)],
        out_specs=[pl.BlockSpec((t, 2 * LANES), lambda h, j, i: (0, h)), krow(2 * LANES), krow(LANES)],
        out_shape=[jax.ShapeDtypeStruct((t, 1024), F32), jax.ShapeDtypeStruct((t, 1024), F32),
                   jax.ShapeDtypeStruct((t, 512), BF16)],
        scratch_shapes=[pltpu.VMEM((2, tq, LANES), F32), pltpu.VMEM((2, tq, LANES), F32)],
        compiler_params=_params(("parallel", "arbitrary", "arbitrary")),
    )(q, k, v, do, lse, delta)


def _conv_rows(ext, tm, w_ref, sec):
    c0 = sec * 1024
    y = ext[CONV_HALO - 3:CONV_HALO - 3 + tm, c0:c0 + 1024] * w_ref[0:1, c0:c0 + 1024]
    for j in range(1, CONV_WIDTH):
        y = y + ext[CONV_HALO - 3 + j:CONV_HALO - 3 + j + tm, c0:c0 + 1024] * w_ref[j:j + 1, c0:c0 + 1024]
    return y


def _c_prep(proj_c, conv_w, a_log, dt_bias, *, name):
    t = proj_c.shape[0]
    tm = _tile(t, ROW_TILE)
    hb = tm // CONV_HALO

    def body(p_ref, halo_ref, ab_ref, w_ref, al_ref, dtb_ref, q_ref, k_ref, v_ref, g_ref, b_ref, gt_ref, ext):
        i = pl.program_id(0)
        ext[0:CONV_HALO, :] = jnp.where(i > 0, halo_ref[...], 0.0)
        ext[CONV_HALO:CONV_HALO + tm, :] = p_ref[...]
        for sec, o_ref in enumerate((q_ref, k_ref, v_ref)):
            y = _conv_rows(ext, tm, w_ref, sec)
            y = y * _sigmoid(y)
            if sec == 2:
                o_ref[...] = y
                continue
            scale = GDN_DK ** -0.5 if sec == 0 else 1.0
            for h in range(GDN_HEADS):
                sl = slice(h * LANES, (h + 1) * LANES)
                blk = y[:, sl]
                r = lax.rsqrt(jnp.sum(blk * blk, axis=-1, keepdims=True) + RMS_EPS)
                o_ref[:, sl] = blk * (r * scale)
        ab = ab_ref[...]
        g = -jnp.exp(al_ref[...]) * _softplus(ab + dtb_ref[...])
        beta = _sigmoid(ab)
        ri = lax.broadcasted_iota(jnp.int32, (tm, tm), 0)
        ci = lax.broadcasted_iota(jnp.int32, (tm, tm), 1)
        lower = ((ri // CHUNK) == (ci // CHUNK)) & (ri >= ci)
        gc = _dot(lower.astype(F32), g, NN, HI)
        eye = lax.broadcasted_iota(jnp.int32, (LANES, LANES), 0) == lax.broadcasted_iota(jnp.int32, (LANES, LANES), 1)
        gt_ref[...] = _dot(eye.astype(F32), gc, NT, HI)[0:GDN_HEADS, :]
        for h in range(GDN_HEADS):
            sl = slice(h * LANES, (h + 1) * LANES)
            g_ref[:, sl] = jnp.broadcast_to(gc[:, h:h + 1], (tm, LANES))
            b_ref[:, sl] = jnp.broadcast_to(beta[:, GDN_HEADS + h:GDN_HEADS + h + 1], (tm, LANES))

    row = lambda w: pl.BlockSpec((tm, w), lambda i: (i, 0))
    vec = lambda r, w: pl.BlockSpec((r, w), lambda i: (0, 0))
    out = jax.ShapeDtypeStruct((t, 1024), F32)
    return pl.pallas_call(
        body, name=name, grid=(t // tm,),
        in_specs=[row(3072), pl.BlockSpec((CONV_HALO, 3072), lambda i: (jnp.maximum(i * hb - 1, 0), 0)),
                  pl.BlockSpec((tm, LANES), lambda i: (i, 32)), vec(CONV_WIDTH, 3072), vec(1, LANES), vec(1, LANES)],
        out_specs=[row(1024)] * 5 + [pl.BlockSpec((GDN_HEADS, tm), lambda i: (0, i))],
        out_shape=[out] * 5 + [jax.ShapeDtypeStruct((GDN_HEADS, t), F32)],
        scratch_shapes=[pltpu.VMEM((tm + CONV_HALO, 3072), F32)],
        compiler_params=_params(("parallel",)),
    )(proj_c, proj_c, proj_c, conv_w, a_log, dt_bias)


def _c_prep_bwd(proj_c, conv_w, a_log, dt_bias, dq, dk, dv, dgb, dbb, dz, *, name):
    t = proj_c.shape[0]
    tm = _tile(t, ROW_TILE // 2)
    hb = tm // CONV_HALO
    nt = t // tm
    rev = lambda i: nt - 1 - i

    def body(p_ref, halo_ref, ab_ref, w_ref, al_ref, dtb_ref, dq_ref, dk_ref, dv_ref, dg_ref, db_ref, dz_ref,
             dp_ref, dw_ref, dal_ref, ddt_ref, ext, dyext, carry):
        step = pl.program_id(0)
        i = rev(step)

        @pl.when(step == 0)
        def _():
            dw_ref[...] = jnp.zeros_like(dw_ref)
            dal_ref[...] = jnp.zeros_like(dal_ref)
            ddt_ref[...] = jnp.zeros_like(ddt_ref)
            carry[...] = jnp.zeros_like(carry)

        ext[0:CONV_HALO, :] = jnp.where(i > 0, halo_ref[...], 0.0)
        ext[CONV_HALO:CONV_HALO + tm, :] = p_ref[...]
        for sec, g_ref in enumerate((dq_ref, dk_ref, dv_ref)):
            c0 = sec * 1024
            y = _conv_rows(ext, tm, w_ref, sec)
            sg = _sigmoid(y)
            act = y * sg
            if sec == 2:
                dact = g_ref[...]
            else:
                scale = GDN_DK ** -0.5 if sec == 0 else 1.0
                parts = []
                for h in range(GDN_HEADS):
                    sl = slice(h * LANES, (h + 1) * LANES)
                    blk = act[:, sl]
                    r = lax.rsqrt(jnp.sum(blk * blk, axis=-1, keepdims=True) + RMS_EPS)
                    n = blk * r
                    dn = g_ref[:, sl] * scale
                    parts.append(r * (dn - n * jnp.sum(dn * n, axis=-1, keepdims=True)))
                dact = jnp.concatenate(parts, axis=-1)
            dy = dact * (sg * (1.0 + y * (1.0 - sg)))
            dyext[0:tm, c0:c0 + 1024] = dy
            for j in range(CONV_WIDTH):
                xs = ext[CONV_HALO - 3 + j:CONV_HALO - 3 + j + tm, c0:c0 + 1024]
                dw_ref[j:j + 1, c0:c0 + 1024] += jnp.sum(dy * xs, axis=0, keepdims=True)
        dyext[tm:tm + CONV_HALO, :] = carry[...]
        carry[...] = dyext[0:CONV_HALO, :]
        for sec in range(3):
            c0 = sec * 1024
            dx = dyext[3:3 + tm, c0:c0 + 1024] * w_ref[0:1, c0:c0 + 1024]
            for j in range(1, CONV_WIDTH):
                dx = dx + dyext[3 - j:3 - j + tm, c0:c0 + 1024] * w_ref[j:j + 1, c0:c0 + 1024]
            dp_ref[:, c0:c0 + 1024] = dx.astype(BF16)
        dp_ref[:, 3072:4096] = dz_ref[...]
        lane = lax.broadcasted_iota(jnp.int32, (tm, LANES), 1)
        dg = jnp.zeros((tm, LANES), F32)
        dbeta = jnp.zeros((tm, LANES), F32)
        for h in range(GDN_HEADS):
            sl = slice(h * LANES, (h + 1) * LANES)
            dg = dg + jnp.where(lane == h, dg_ref[:, sl], 0.0)
            dbeta = dbeta + jnp.where(lane == GDN_HEADS + h, db_ref[:, sl], 0.0)
        ri = lax.broadcasted_iota(jnp.int32, (tm, tm), 0)
        ci = lax.broadcasted_iota(jnp.int32, (tm, tm), 1)
        upper = ((ri // CHUNK) == (ci // CHUNK)) & (ri <= ci)
        dg = _dot(upper.astype(F32), dg, NN, HI)
        pre = ab_ref[...] + dtb_ref[...]
        s = _sigmoid(pre)
        a_exp = jnp.exp(al_ref[...])
        dg_da = dg * (-a_exp * s)
        dp_ref[:, 4096:IN_C_PAD] = (dg_da + dbeta * s * (1.0 - s)).astype(BF16)
        dal_ref[...] += jnp.sum(dg * (-a_exp * _softplus(pre)), axis=0, keepdims=True)
        ddt_ref[...] += jnp.sum(dg_da, axis=0, keepdims=True)

    row = lambda w: pl.BlockSpec((tm, w), lambda s: (rev(s), 0))
    vec = lambda r, w: pl.BlockSpec((r, w), lambda s: (0, 0))
    return pl.pallas_call(
        body, name=name, grid=(nt,),
        in_specs=[row(3072), pl.BlockSpec((CONV_HALO, 3072), lambda s: (jnp.maximum(rev(s) * hb - 1, 0), 0)),
                  pl.BlockSpec((tm, LANES), lambda s: (rev(s), 32)), vec(CONV_WIDTH, 3072), vec(1, LANES), vec(1, LANES),
                  row(1024), row(1024), row(1024), row(1024), row(1024), row(1024)],
        out_specs=[row(IN_C_PAD), vec(CONV_WIDTH, 3072), vec(1, LANES), vec(1, LANES)],
        out_shape=[jax.ShapeDtypeStruct((t, IN_C_PAD), BF16), jax.ShapeDtypeStruct((CONV_WIDTH, 3072), F32),
                   jax.ShapeDtypeStruct((1, LANES), F32), jax.ShapeDtypeStruct((1, LANES), F32)],
        scratch_shapes=[pltpu.VMEM((tm + CONV_HALO, 3072), F32), pltpu.VMEM((tm + CONV_HALO, 3072), F32),
                        pltpu.VMEM((CONV_HALO, 3072), F32)],
        compiler_params=_params(("arbitrary",)),
    )(proj_c, proj_c, proj_c, conv_w, a_log, dt_bias, dq, dk, dv, dgb, dbb, dz)


def _o_gate_fwd(o, proj_c, o_norm, *, name):
    t = o.shape[0]
    tm = _tile(t, 2 * ROW_TILE)

    def body(o_ref, z_ref, g_ref, y_ref):
        for h in range(GDN_HEADS):
            sl = slice(h * LANES, (h + 1) * LANES)
            x = o_ref[:, sl]
            r = lax.rsqrt(jnp.mean(x * x, axis=-1, keepdims=True) + RMS_EPS)
            z = z_ref[:, sl]
            y_ref[:, sl] = (x * r * g_ref[...] * (z * _sigmoid(z))).astype(BF16)

    row = pl.BlockSpec((tm, 1024), lambda i: (i, 0))
    return pl.pallas_call(
        body, name=name, grid=(t // tm,),
        in_specs=[row, pl.BlockSpec((tm, 1024), lambda i: (i, 3)), pl.BlockSpec((1, LANES), lambda i: (0, 0))],
        out_specs=row, out_shape=jax.ShapeDtypeStruct((t, 1024), BF16), compiler_params=_params(("parallel",)),
    )(o, proj_c, o_norm)


def _o_gate_bwd(dy, o, proj_c, o_norm, *, name):
    t = o.shape[0]
    tm = _tile(t, 2 * ROW_TILE)

    def body(dy_ref, o_ref, z_ref, g_ref, do_ref, dz_ref, dg_ref):
        i = pl.program_id(0)

        @pl.when(i == 0)
        def _():
            dg_ref[...] = jnp.zeros_like(dg_ref)

        dg = jnp.zeros((1, LANES), F32)
        for h in range(GDN_HEADS):
            sl = slice(h * LANES, (h + 1) * LANES)
            x = o_ref[:, sl]
            r = lax.rsqrt(jnp.mean(x * x, axis=-1, keepdims=True) + RMS_EPS)
            xh = x * r
            z = z_ref[:, sl]
            sg = _sigmoid(z)
            dyv = dy_ref[:, sl]
            dn = dyv * (z * sg)
            dz_ref[:, sl] = (dyv * xh * g_ref[...] * (sg * (1.0 + z * (1.0 - sg)))).astype(BF16)
            dxh = dn * g_ref[...]
            do_ref[:, sl] = r * (dxh - xh * jnp.mean(dxh * xh, axis=-1, keepdims=True))
            dg = dg + jnp.sum(dn * xh, axis=0, keepdims=True)
        dg_ref[...] += dg

    row = pl.BlockSpec((tm, 1024), lambda i: (i, 0))
    vec = pl.BlockSpec((1, LANES), lambda i: (0, 0))
    return pl.pallas_call(
        body, name=name, grid=(t // tm,), in_specs=[row, row, pl.BlockSpec((tm, 1024), lambda i: (i, 3)), vec],
        out_specs=[row, row, vec],
        out_shape=[jax.ShapeDtypeStruct((t, 1024), F32), jax.ShapeDtypeStruct((t, 1024), BF16),
                   jax.ShapeDtypeStruct((1, LANES), F32)],
        compiler_params=_params(("arbitrary",)),
    )(dy, o, proj_c, o_norm)


PAIR = 2 * CHUNK
GDN_HP = 8


def _bdot(a, b, dims=NN):
    return _dot(a.astype(BF16), b.astype(BF16), dims)


def _each(f, *lists):
    return [f(*args) for args in zip(*lists)]


def _pair_common(q, k, v, gci, gcj, beta):
    ri = lax.broadcasted_iota(jnp.int32, (PAIR, PAIR), 0)
    ci = lax.broadcasted_iota(jnp.int32, (PAIR, PAIR), 1)
    same = (ri // CHUNK) == (ci // CHUNK)
    incl = same & (ri >= ci)
    strict = same & (ri > ci)
    eye = (ri == ci).astype(F32)
    first = lax.broadcasted_iota(jnp.int32, (PAIR, LANES), 0) < CHUNK
    gamma = _each(lambda gi, gj: jnp.where(incl, jnp.exp(jnp.minimum(gi - gj, 0.0)), 0.0), gci, gcj)
    kb = _each(jnp.multiply, k, beta)
    kk = _each(lambda a, b: _bdot(a, b, NT), kb, k)
    qk = _each(lambda a, b: _bdot(a, b, NT), q, k)
    m = _each(lambda x, g: jnp.where(strict, x * g, 0.0), kk, gamma)
    tm_ = _each(lambda x: eye - x, m)
    pw = _each(lambda x: _bdot(x, x), m)
    for it in range(5):
        tm_ = _each(lambda x, p: x + _bdot(x, p), tm_, pw)
        if it < 4:
            pw = _each(lambda p: _bdot(p, p), pw)
    eg = _each(jnp.exp, gci)
    vb = _each(jnp.multiply, v, beta)
    kbe = _each(jnp.multiply, kb, eg)
    uw = _each(lambda x, a, b: _bdot(x, jnp.concatenate([a, b], axis=1)), tm_, vb, kbe)
    attn = _each(lambda x, g: jnp.where(incl, x * g, 0.0), qk, gamma)
    gl_a = _each(lambda g: g[CHUNK - 1:CHUNK, :], gci)
    gl_b = _each(lambda g: g[PAIR - 1:PAIR, :], gci)
    ek = _each(lambda a, b, g: jnp.exp(jnp.where(first, a, b) - g), gl_a, gl_b, gci)
    return dict(incl=incl, strict=strict, gamma=gamma, kb=kb, m=m, tm=tm_, eg=eg, vb=vb, kbe=kbe,
                u=_each(lambda x: x[:, :LANES], uw), w=_each(lambda x: x[:, LANES:], uw), attn=attn,
                qd=_each(jnp.multiply, q, eg), ek=ek, kd=_each(jnp.multiply, k, ek),
                glast_a=_each(jnp.exp, gl_a), glast_b=_each(jnp.exp, gl_b))


def _gdn_specs(t, ts, order):
    nc = ts // CHUNK
    blk = pl.BlockSpec((ts, GDN_HP * LANES), lambda h, s: (order(s), h))
    row = pl.BlockSpec((GDN_HP, 1, ts), lambda h, s: (h, 0, order(s)))
    st = pl.BlockSpec((GDN_HP, nc, LANES, LANES), lambda h, s: (h, order(s), 0, 0))
    return blk, row, st


def _gdn_fwd(q, k, v, gcb, gct, bb, *, name):
    t = q.shape[0]
    ts = _tile(t, GDN_TILE)
    npair = ts // PAIR

    def body(q_ref, k_ref, v_ref, g_ref, gt_ref, b_ref, o_ref, st_ref, s_sc):
        @pl.when(pl.program_id(1) == 0)
        def _():
            s_sc[...] = jnp.zeros_like(s_sc)

        def pair(pi, _):
            rows = pl.ds(pl.multiple_of(pi * PAIR, PAIR), PAIR)
            heads = [slice(hh * LANES, (hh + 1) * LANES) for hh in range(GDN_HP)]
            c = CHUNK
            cat0 = lambda *xs: jnp.concatenate(xs, axis=0)
            s0 = [s_sc[hh] for hh in range(GDN_HP)]
            cm = _pair_common([q_ref[rows, sl] for sl in heads], [k_ref[rows, sl] for sl in heads],
                              [v_ref[rows, sl] for sl in heads], [g_ref[rows, sl] for sl in heads],
                              [gt_ref[hh, :, rows] for hh in range(GDN_HP)], [b_ref[rows, sl] for sl in heads])
            u, w, qd, kd = cm["u"], cm["w"], cm["qd"], cm["kd"]
            r0 = _each(lambda w_, q_, s: _bdot(cat0(w_[:c], q_[:c]), s), w, qd, s0)
            vn_a = _each(lambda u_, r: u_[:c] - r[:c], u, r0)
            s1 = _each(lambda s, gl, k_, vn: s * gl + _bdot(k_[:c], vn, TN), s0, cm["glast_a"], kd, vn_a)
            r1 = _each(lambda w_, q_, s: _bdot(cat0(w_[c:], q_[c:]), s), w, qd, s1)
            vn_b = _each(lambda u_, r: u_[c:] - r[:c], u, r1)
            s2 = _each(lambda s, gl, k_, vn: s * gl + _bdot(k_[c:], vn, TN), s1, cm["glast_b"], kd, vn_b)
            o = _each(lambda ra, rb, at, va, vb_: cat0(ra[c:], rb[c:]) + _bdot(at, cat0(va, vb_)),
                      r0, r1, cm["attn"], vn_a, vn_b)
            for hh, sl in enumerate(heads):
                st_ref[hh, 2 * pi] = s0[hh]
                st_ref[hh, 2 * pi + 1] = s1[hh]
                s_sc[hh] = s2[hh]
                o_ref[rows, sl] = o[hh]
            return 0

        lax.fori_loop(0, npair, pair, 0)

    blk, row, st = _gdn_specs(t, ts, lambda s: s)
    return pl.pallas_call(
        body, name=name, grid=(GDN_HEADS // GDN_HP, t // ts), in_specs=[blk, blk, blk, blk, row, blk],
        out_specs=[blk, st],
        out_shape=[jax.ShapeDtypeStruct((t, 1024), F32), jax.ShapeDtypeStruct((GDN_HEADS, t // CHUNK, LANES, LANES), F32)],
        scratch_shapes=[pltpu.VMEM((GDN_HP, LANES, LANES), F32)],
        compiler_params=_params(("parallel", "arbitrary")),
    )(q, k, v, gcb, gct, bb)


def _gdn_bwd(q, k, v, gcb, gct, bb, do, states, *, name):
    t = q.shape[0]
    ts = _tile(t, GDN_TILE)
    npair = ts // PAIR
    ns = t // ts
    c = CHUNK

    def body(q_ref, k_ref, v_ref, g_ref, gt_ref, b_ref, do_ref, st_ref, dq_ref, dk_ref, dv_ref, dg_ref, db_ref, ds_sc):
        @pl.when(pl.program_id(1) == 0)
        def _():
            ds_sc[...] = jnp.zeros_like(ds_sc)

        rowsum = lambda x: jnp.sum(x, axis=-1, keepdims=True)
        total = lambda x: jnp.sum(rowsum(x), axis=0, keepdims=True)
        cat0 = lambda *xs: jnp.concatenate(xs, axis=0)
        cat1 = lambda *xs: jnp.concatenate(xs, axis=1)

        def pair(step, _):
            pi = npair - 1 - step
            rows = pl.ds(pl.multiple_of(pi * PAIR, PAIR), PAIR)
            heads = [slice(hh * LANES, (hh + 1) * LANES) for hh in range(GDN_HP)]
            hs = range(GDN_HP)
            qv, kv, vv = ([r[rows, sl] for sl in heads] for r in (q_ref, k_ref, v_ref))
            beta = [b_ref[rows, sl] for sl in heads]
            dov = [do_ref[rows, sl] for sl in heads]
            s0 = [st_ref[hh, 2 * pi] for hh in hs]
            s1 = [st_ref[hh, 2 * pi + 1] for hh in hs]
            ds2 = [ds_sc[hh] for hh in hs]
            cm = _pair_common(qv, kv, vv, [g_ref[rows, sl] for sl in heads], [gt_ref[hh, :, rows] for hh in hs], beta)
            u, w, qd, kd, attn = cm["u"], cm["w"], cm["qd"], cm["kd"], cm["attn"]
            tmat, gamma, eg = cm["tm"], cm["gamma"], cm["eg"]
            incl, strict = cm["incl"], cm["strict"]
            vn_a = _each(lambda u_, w_, s: u_[:c] - _bdot(w_[:c], s), u, w, s0)
            vn_b = _each(lambda u_, w_, s: u_[c:] - _bdot(w_[c:], s), u, w, s1)
            vn = _each(cat0, vn_a, vn_b)
            dvn_att = _each(lambda a, d: _bdot(a, d, TN), attn, dov)
            dattn = _each(lambda d, v_: jnp.where(incl, _bdot(d, v_, NT), 0.0), dov, vn)
            dvn_b = _each(lambda x, k_, d: x[c:] + _bdot(k_[c:], d), dvn_att, kd, ds2)
            rb = _each(lambda d, x, s: _bdot(cat0(d[c:], x), s, NT), dov, dvn_b, s1)
            dkd_b = _each(lambda v_, d: _bdot(v_, d, NT), vn_b, ds2)
            dgl_b = _each(lambda d, s: total(d * s), ds2, s1)
            ds1 = _each(lambda d, gl, q_, w_, o_, x: d * gl + _bdot(cat0(q_[c:], w_[c:]), cat0(o_[c:], -x), TN),
                        ds2, cm["glast_b"], qd, w, dov, dvn_b)
            dvn_a = _each(lambda x, k_, d: x[:c] + _bdot(k_[:c], d), dvn_att, kd, ds1)
            ra = _each(lambda d, x, s: _bdot(cat0(d[:c], x), s, NT), dov, dvn_a, s0)
            dkd_a = _each(lambda v_, d: _bdot(v_, d, NT), vn_a, ds1)
            dgl_a = _each(lambda d, s: total(d * s), ds1, s0)
            ds0 = _each(lambda d, gl, q_, w_, o_, x: d * gl + _bdot(cat0(q_[:c], w_[:c]), cat0(o_[:c], -x), TN),
                        ds1, cm["glast_a"], qd, w, dov, dvn_a)
            dvn = _each(cat0, dvn_a, dvn_b)
            dqd = _each(lambda a, b: cat0(a[:c], b[:c]), ra, rb)
            dw = _each(lambda a, b: -cat0(a[c:], b[c:]), ra, rb)
            dkd = _each(cat0, dkd_a, dkd_b)
            dvw = _each(cat1, dvn, dw)
            dvbk = _each(lambda t_, x: _bdot(t_, x, TN), tmat, dvw)
            dvb = _each(lambda x: x[:, :LANES], dvbk)
            dkbe = _each(lambda x: x[:, LANES:], dvbk)
            dt_ = _each(lambda x, a, b: _bdot(x, cat1(a, b), NT), dvw, cm["vb"], cm["kbe"])
            da1 = _each(lambda t_, x: _bdot(t_, x, TN), tmat, dt_)
            dm = _each(lambda x, t_: jnp.where(strict, -_bdot(x, t_, NT), 0.0), da1, tmat)
            dkk = _each(jnp.multiply, dm, gamma)
            dqk = _each(jnp.multiply, dattn, gamma)
            z = _each(lambda a, b, c_, d: a * b + c_ * d, dm, cm["m"], dattn, attn)
            dkb = _each(lambda x, k_, y, e: _bdot(x, k_) + y * e, dkk, kv, dkbe, eg)
            dk = _each(lambda a, b, kb_, q_, x, e, y, be: _bdot(cat0(a, b), cat0(kb_, q_), TN) + x * e + y * be,
                       dkk, dqk, cm["kb"], qv, dkd, cm["ek"], dkb, beta)
            dq = _each(lambda x, k_, y, e: _bdot(x, k_) + y * e, dqk, kv, dqd, eg)

            def colsum_of(z_):
                zh = z_.astype(BF16)
                zl = (z_ - zh.astype(F32)).astype(BF16)
                return _dot(cat0(zh, zl), jnp.ones((2 * PAIR, LANES), BF16), TN)

            colsum = _each(colsum_of, z)
            ri = lax.broadcasted_iota(jnp.int32, (PAIR, LANES), 0)
            for hh, sl in enumerate(heads):
                dkd_kd = dkd[hh] * kd[hh]
                dgc = (rowsum(z[hh]) - colsum[hh] + rowsum(dqd[hh] * qd[hh]) - rowsum(dkd_kd)
                       + rowsum(dkbe[hh] * cm["kbe"][hh]))
                last_a = total(dkd_kd[:c]) + dgl_a[hh] * cm["glast_a"][hh]
                last_b = total(dkd_kd[c:]) + dgl_b[hh] * cm["glast_b"][hh]
                dgc = dgc + jnp.where(ri == c - 1, last_a, 0.0) + jnp.where(ri == PAIR - 1, last_b, 0.0)
                ds_sc[hh] = ds0[hh]
                dq_ref[rows, sl] = dq[hh]
                dk_ref[rows, sl] = dk[hh]
                dv_ref[rows, sl] = dvb[hh] * beta[hh]
                db_ref[rows, sl] = jnp.broadcast_to(rowsum(dkb[hh] * kv[hh]) + rowsum(dvb[hh] * vv[hh]), (PAIR, LANES))
                dg_ref[rows, sl] = dgc
            return 0

        lax.fori_loop(0, npair, pair, 0)

    blk, row, st = _gdn_specs(t, ts, lambda s: ns - 1 - s)
    out = jax.ShapeDtypeStruct((t, 1024), F32)
    return pl.pallas_call(
        body, name=name, grid=(GDN_HEADS // GDN_HP, ns), in_specs=[blk, blk, blk, blk, row, blk, blk, st],
        out_specs=[blk] * 5, out_shape=[out] * 5, scratch_shapes=[pltpu.VMEM((GDN_HP, LANES, LANES), F32)],
        compiler_params=_params(("parallel", "arbitrary")),
    )(q, k, v, gcb, gct, bb, do, states)


def _loss_head(h, g, target, *, name):
    t, d = h.shape
    tm = _tile(t, 2 * ROW_TILE)

    def body(h_ref, g_ref, t_ref, dh_ref, dhb_ref, dg_ref, loss_ref):
        i = pl.program_id(0)
        x = h_ref[...]
        r = lax.rsqrt(jnp.mean(x * x, axis=-1, keepdims=True) + RMS_EPS)
        xh = x * r
        err = xh * g_ref[...] - t_ref[...]
        dy = err * (1.0 / d)
        dxh = dy * g_ref[...]
        dh = r * (dxh - xh * jnp.mean(dxh * xh, axis=-1, keepdims=True))
        dh_ref[...] = dh
        dhb_ref[...] = dh.astype(BF16)

        @pl.when(i == 0)
        def _():
            dg_ref[...] = jnp.zeros_like(dg_ref)
            loss_ref[...] = jnp.zeros_like(loss_ref)

        dg_ref[...] += jnp.sum(dy * xh, axis=0, keepdims=True)
        part = 0.5 * jnp.sum(jnp.mean(err * err, axis=-1, keepdims=True), axis=0, keepdims=True)
        loss_ref[...] += jnp.broadcast_to(part, loss_ref.shape)

    row = pl.BlockSpec((tm, d), lambda i: (i, 0))
    vec = pl.BlockSpec((1, d), lambda i: (0, 0))
    return pl.pallas_call(
        body, name=name, grid=(t // tm,), in_specs=[row, vec, row],
        out_specs=[row, row, vec, pl.BlockSpec((8, LANES), lambda i: (0, 0))],
        out_shape=[jax.ShapeDtypeStruct((t, d), F32), jax.ShapeDtypeStruct((t, d), BF16),
                   jax.ShapeDtypeStruct((1, d), F32), jax.ShapeDtypeStruct((8, LANES), F32)],
        compiler_params=_params(("arbitrary",)),
    )(h, g, target)


def _pad_cols(w, n):
    return jnp.pad(w, ((0, 0), (0, n - w.shape[1])))


def _layout_weights(w):
    z = lambda r, c: jnp.zeros((r, c), F32)
    wi = w["w_in_ab"]
    win = jnp.concatenate([wi[:, :384], z(1024, 64), wi[:, 384:416], z(1024, 32), wi[:, 416:]], axis=1)
    wq = jnp.pad(w["w_q_b"].reshape(MLA_Q_RANK, MLA_HEADS, 96), ((0, 0), (0, 0), (0, 32))).reshape(MLA_Q_RANK, 1024)
    kv3 = w["w_kv_b"].reshape(MLA_KV_RANK, MLA_HEADS, 128)
    wk = jnp.pad(kv3[..., :MLA_NOPE], ((0, 0), (0, 0), (0, 64))).reshape(MLA_KV_RANK, 1024)
    wv = kv3[..., MLA_NOPE:].reshape(MLA_KV_RANK, 512)
    pw = w["pool_w"]
    rows = []
    for g in range(4):
        rows.append(jnp.concatenate([pw[g] if j == g else z(128, 128) for j in range(4)], axis=1))
    wpool = jnp.concatenate(rows, axis=0)
    half = MLA_ROPE // 2
    inv = 1.0 / (ROPE_THETA ** (jnp.arange(half, dtype=F32) / half))
    inv_lane = jnp.concatenate([jnp.zeros((MLA_NOPE,), F32), inv, inv, jnp.zeros((32,), F32)]).reshape(1, LANES)
    return dict(
        win=win.astype(BF16), wq=wq.astype(BF16), wk=wk.astype(BF16), wv=wv.astype(BF16), wpool=wpool.astype(BF16),
        wout_ab=w["w_out_ab"].astype(BF16), winc=_pad_cols(w["w_in_c"], IN_C_PAD).astype(BF16),
        wout_c=w["w_out_c"].astype(BF16), conv_w=w["conv_w"], a_log=_pad_cols(w["a_log"], LANES),
        dt_bias=_pad_cols(w["dt_bias"], LANES), inv_lane=inv_lane,
        norm_ab=w["norm_ab"], q_a_norm=w["q_a_norm"], kv_a_norm=w["kv_a_norm"], pool_scale=w["pool_scale"],
        norm_c=w["norm_c"], o_norm=w["o_norm"], final_norm=w["final_norm"],
    )


def _unlayout_grads(g):
    dwin = g["win"]
    dkv = jnp.concatenate([g["wk"].reshape(MLA_KV_RANK, MLA_HEADS, 128)[..., :MLA_NOPE],
                           g["wv"].reshape(MLA_KV_RANK, MLA_HEADS, MLA_V)], axis=-1).reshape(MLA_KV_RANK, 1024)
    return dict(
        norm_ab=g["norm_ab"],
        w_in_ab=jnp.concatenate([dwin[:, :384], dwin[:, 448:480], dwin[:, 512:]], axis=1),
        q_a_norm=g["q_a_norm"],
        w_q_b=g["wq"].reshape(MLA_Q_RANK, MLA_HEADS, 128)[..., :96].reshape(MLA_Q_RANK, 768),
        kv_a_norm=g["kv_a_norm"],
        w_kv_b=dkv,
        pool_w=jnp.stack([g["wpool"][i * 128:(i + 1) * 128, i * 128:(i + 1) * 128] for i in range(4)]),
        pool_scale=g["pool_scale"],
        w_out_ab=g["wout_ab"],
        norm_c=g["norm_c"],
        w_in_c=g["winc"][:, :4112],
        conv_w=g["conv_w"],
        a_log=g["a_log"][:, :GDN_HEADS],
        dt_bias=g["dt_bias"][:, :GDN_HEADS],
        o_norm=g["o_norm"],
        w_out_c=g["wout_c"],
        final_norm=g["final_norm"],
    )


def _local_step(x, pos, target, lw):
    mm = _matmul
    hn = _rms_fwd(x, lw["norm_ab"], name="rms_ab")
    proj = mm(hn, lw["win"], "nn", name="in_ab")
    qn, kvn, kr, d, cos_t, sin_t = _ab_prep(proj, pos, lw["inv_lane"], lw["q_a_norm"], lw["kv_a_norm"], name="ab_prep")
    qraw = mm(qn, lw["wq"], "nn", name="q_up")
    kvk = mm(kvn, lw["wk"], "nn", name="k_up")
    v = mm(kvn, lw["wv"], "nn", name="v_up", out_dtype=BF16)
    ybraw = mm(d, lw["wpool"], "nn", name="pool_mix")
    q, k = _qk_rope(qraw, kvk, kr, cos_t, sin_t, name="qk_rope")
    o, lse = _attn_fwd(q, k, v, name="attn_fwd")
    y = _gate_fwd(o, ybraw, proj, lw["pool_scale"], name="gate_ab")
    h1 = mm(y, lw["wout_ab"], "nn", name="out_ab", add=x)
    hn1 = _rms_fwd(h1, lw["norm_c"], name="rms_c")
    proj_c = mm(hn1, lw["winc"], "nn", name="in_c")
    q2, k2, v2, gb, bb, gt = _c_prep(proj_c, lw["conv_w"], lw["a_log"], lw["dt_bias"], name="c_prep")
    gt = gt.reshape(GDN_HEADS, 1, gt.shape[1])
    o2, states = _gdn_fwd(q2, k2, v2, gb, gt, bb, name="gdn_fwd")
    y2 = _o_gate_fwd(o2, proj_c, lw["o_norm"], name="gate_c")
    h2 = mm(y2, lw["wout_c"], "nn", name="out_c", add=h1)
    dh2, dh2b, d_final, loss = _loss_head(h2, lw["final_norm"], target, name="loss_head")
    g = {"final_norm": d_final}
    dy2 = mm(dh2b, lw["wout_c"], "nt", name="out_c_dx")
    g["wout_c"] = mm(y2, dh2b, "tn", name="out_c_dw")
    do2, dz2, g["o_norm"] = _o_gate_bwd(dy2, o2, proj_c, lw["o_norm"], name="gate_c_bwd")
    dq2, dk2, dv2, dgb, dbb = _gdn_bwd(q2, k2, v2, gb, gt, bb, do2, states, name="gdn_bwd")
    dproj_c, g["conv_w"], g["a_log"], g["dt_bias"] = _c_prep_bwd(
        proj_c, lw["conv_w"], lw["a_log"], lw["dt_bias"], dq2, dk2, dv2, dgb, dbb, dz2, name="c_prep_bwd")
    dhn1 = mm(dproj_c, lw["winc"], "nt", name="in_c_dx")
    g["winc"] = mm(hn1, dproj_c, "tn", name="in_c_dw")
    dh1, dh1b, g["norm_c"] = _rms_bwd(h1, lw["norm_c"], dhn1, dh2, name="rms_c_bwd", with_bf16=True)
    dy = mm(dh1b, lw["wout_ab"], "nt", name="out_ab_dx")
    g["wout_ab"] = mm(y, dh1b, "tn", name="out_ab_dw")
    do, delta, dyb, dz, g["pool_scale"] = _gate_bwd(dy, o, ybraw, proj, lw["pool_scale"], name="gate_ab_bwd")
    dq, dk, dv = _attn_bwd(q, k, v, do, lse, delta, name="attn_bwd")
    dd = mm(dyb, lw["wpool"], "nt", name="pool_mix_dx")
    g["wpool"] = mm(d, dyb, "tn", name="pool_mix_dw")
    dqraw, dkb, dkr = _qk_rope_bwd(dq, dk, cos_t, sin_t, name="qk_rope_bwd")
    dqn = mm(dqraw, lw["wq"], "nt", name="q_up_dx")
    g["wq"] = mm(qn, dqraw, "tn", name="q_up_dw")
    dkvn_k = mm(dkb, lw["wk"], "nt", name="k_up_dx")
    dkvn_v = mm(dv, lw["wv"], "nt", name="v_up_dx")
    g["wk"] = mm(kvn, dkb, "tn", name="k_up_dw")
    g["wv"] = mm(kvn, dv, "tn", name="v_up_dw")
    dproj, g["q_a_norm"], g["kv_a_norm"] = _ab_prep_bwd(
        proj, lw["q_a_norm"], lw["kv_a_norm"], dqn, dkvn_k, dkvn_v, dkr, dd, dz, name="ab_prep_bwd")
    dhn = mm(dproj, lw["win"], "nt", name="in_ab_dx")
    g["win"] = mm(hn, dproj, "tn", name="in_ab_dw")
    dx, g["norm_ab"] = _rms_bwd(x, lw["norm_ab"], dhn, dh1, name="rms_ab_bwd", with_bf16=False)
    return loss, dx, g


_HBM = pl.BlockSpec(memory_space=pltpu.HBM)


def _place():
    return lax.axis_index("x"), lax.axis_index("y"), lax.axis_index("c")


def _flip(v, f):
    return 1 - v if f else v


_CHIP_FLIPS = ((1, 0), (0, 1), (1, 1))
_DEV_FLIPS = tuple((fx, fy, fc) for fx in (0, 1) for fy in (0, 1) for fc in (0, 1) if fx or fy or fc)


def _rcopy(src, dst, send_sems, recv_sems, k, to):
    return pltpu.make_async_remote_copy(src_ref=src, dst_ref=dst, send_sem=send_sems.at[k], recv_sem=recv_sems.at[k],
                                        device_id=to, device_id_type=MESH)


def _gather_weights(wb, ws):
    _, rh, _ = wb.shape
    rs = ws.shape[0]

    def body(wb_ref, ws_ref, gb_ref, gs_ref, send_sems, recv_sems, local_sems):
        x, y, c = _place()
        j0 = 2 * x + y
        sib = (x, y, 1 - c)
        chips = [(_flip(x, fx), _flip(y, fy)) for fx, fy in _CHIP_FLIPS]
        own_b = pltpu.make_async_copy(wb_ref, gb_ref.at[j0], local_sems.at[0])
        own_s = pltpu.make_async_copy(ws_ref, gs_ref.at[j0], local_sems.at[1])
        own_b.start()
        own_s.start()
        sends = []
        for k, (px, py) in enumerate(chips):
            sends.append(_rcopy(wb_ref.at[c], gb_ref.at[j0, c], send_sems, recv_sems, k, (px, py, c)))
            sends.append(_rcopy(ws_ref, gs_ref.at[j0], send_sems, recv_sems, 6 + k, (px, py, c)))
        for cp in sends:
            cp.start()
        for k, (px, py) in enumerate(chips):
            jk = 2 * px + py
            _rcopy(wb_ref.at[c], gb_ref.at[jk, c], send_sems, recv_sems, k, (px, py, c)).wait_recv()
            fwd = _rcopy(gb_ref.at[jk, c], gb_ref.at[jk, c], send_sems, recv_sems, 3 + k, sib)
            fwd.start()
            sends.append(fwd)
        for k, (px, py) in enumerate(chips):
            jk = 2 * px + py
            _rcopy(wb_ref.at[c], gb_ref.at[jk, 1 - c], send_sems, recv_sems, 3 + k, sib).wait_recv()
            _rcopy(ws_ref, gs_ref.at[jk], send_sems, recv_sems, 6 + k, (px, py, c)).wait_recv()
        for cp in sends:
            cp.wait_send()
        own_b.wait()
        own_s.wait()

    return pl.pallas_call(
        body, name="gather_weights", in_specs=[_HBM, _HBM], out_specs=[_HBM, _HBM],
        out_shape=[jax.ShapeDtypeStruct((4, 2, rh, LANES), BF16), jax.ShapeDtypeStruct((4, rs, LANES), F32)],
        scratch_shapes=[pltpu.SemaphoreType.DMA((9,)), pltpu.SemaphoreType.DMA((9,)), pltpu.SemaphoreType.DMA((2,))],
    )(wb, ws)


def _sibling_swap(a, *, name):
    def body(a_ref, o_ref, send_sem, recv_sem):
        x, y, c = _place()
        cp = pltpu.make_async_remote_copy(src_ref=a_ref, dst_ref=o_ref, send_sem=send_sem, recv_sem=recv_sem,
                                          device_id=(x, y, 1 - c), device_id_type=MESH)
        cp.start()
        cp.wait()

    return pl.pallas_call(
        body, name=name, in_specs=[_HBM], out_specs=_HBM, out_shape=jax.ShapeDtypeStruct(a.shape, a.dtype),
        scratch_shapes=[pltpu.SemaphoreType.DMA, pltpu.SemaphoreType.DMA],
    )(a)


def _chip_exchange(p, small):
    _, rh, _ = p.shape
    rs = small.shape[0]

    def body(p_ref, s_ref, l_ref, ls_ref, send_sems, recv_sems, local_sems):
        x, y, c = _place()
        j0 = 2 * x + y
        d0 = 2 * j0 + c
        own_p = pltpu.make_async_copy(p_ref.at[j0], l_ref.at[j0], local_sems.at[0])
        own_s = pltpu.make_async_copy(s_ref, ls_ref.at[d0], local_sems.at[1])
        own_p.start()
        own_s.start()
        sends = []
        for k, (fx, fy) in enumerate(_CHIP_FLIPS):
            px, py = _flip(x, fx), _flip(y, fy)
            sends.append(_rcopy(p_ref.at[2 * px + py], l_ref.at[j0], send_sems, recv_sems, k, (px, py, c)))
        for k, (fx, fy, fc) in enumerate(_DEV_FLIPS):
            peer = (_flip(x, fx), _flip(y, fy), _flip(c, fc))
            sends.append(_rcopy(s_ref, ls_ref.at[d0], send_sems, recv_sems, 3 + k, peer))
        for cp in sends:
            cp.start()
        for k, (fx, fy) in enumerate(_CHIP_FLIPS):
            px, py = _flip(x, fx), _flip(y, fy)
            _rcopy(p_ref.at[j0], l_ref.at[2 * px + py], send_sems, recv_sems, k, (px, py, c)).wait_recv()
        for k, (fx, fy, fc) in enumerate(_DEV_FLIPS):
            px, py, pc = _flip(x, fx), _flip(y, fy), _flip(c, fc)
            _rcopy(s_ref, ls_ref.at[4 * px + 2 * py + pc], send_sems, recv_sems, 3 + k, (px, py, pc)).wait_recv()
        for cp in sends:
            cp.wait_send()
        own_p.wait()
        own_s.wait()

    return pl.pallas_call(
        body, name="chip_exchange", in_specs=[_HBM, _HBM], out_specs=[_HBM, _HBM],
        out_shape=[jax.ShapeDtypeStruct((4, rh, LANES), F32), jax.ShapeDtypeStruct((8, rs, LANES), F32)],
        scratch_shapes=[pltpu.SemaphoreType.DMA((10,)), pltpu.SemaphoreType.DMA((10,)), pltpu.SemaphoreType.DMA((2,))],
    )(p, small)


def _sum_slots(a, *, name):
    n, rows, _ = a.shape
    tr = _tile(rows, 1024)

    def body(a_ref, o_ref):
        acc = a_ref[0]
        for s in range(1, n):
            acc = acc + a_ref[s]
        o_ref[...] = acc

    return pl.pallas_call(
        body, name=name, grid=(rows // tr,), in_specs=[pl.BlockSpec((n, tr, LANES), lambda i: (0, i, 0))],
        out_specs=pl.BlockSpec((tr, LANES), lambda i: (i, 0)), out_shape=jax.ShapeDtypeStruct((rows, LANES), F32),
        compiler_params=_params(("parallel",)),
    )(a)


def _adamw(g, w, m, v, *, name):
    rows = g.shape[0]
    tr = _tile(rows, 1024)
    c1 = 1.0 - ADAM_B1 ** ADAM_STEP
    c2 = 1.0 - ADAM_B2 ** ADAM_STEP

    def body(g_ref, w_ref, m_ref, v_ref, d_ref, mo_ref, vo_ref):
        gv = g_ref[...]
        mn = ADAM_B1 * m_ref[...] + (1.0 - ADAM_B1) * gv
        vn = ADAM_B2 * v_ref[...] + (1.0 - ADAM_B2) * (gv * gv)
        mo_ref[...] = mn
        vo_ref[...] = vn
        d_ref[...] = -ADAM_LR * ((mn / c1) / (jnp.sqrt(vn / c2) + ADAM_EPS) + ADAM_WD * w_ref[...])

    blk = pl.BlockSpec((tr, LANES), lambda i: (i, 0))
    out = jax.ShapeDtypeStruct((rows, LANES), F32)
    return pl.pallas_call(
        body, name=name, grid=(rows // tr,), in_specs=[blk] * 4, out_specs=[blk] * 3, out_shape=[out] * 3,
        compiler_params=_params(("parallel",)),
    )(g, w, m, v)


_SHARDED = (("w_in_ab", (1024, 488), 1), ("w_q_b", (256, 192), 1), ("w_kv_b", (128, 256), 1),
            ("w_out_ab", (256, 1024), 0), ("w_in_c", (1024, 1028), 1), ("w_out_c", (256, 1024), 0),
            ("conv_w", (4, 768), 1), ("norm_c", (1, 256), 1))
_N_BF16 = 6
_REPLICATED = (("norm_ab", (1, 1024)), ("q_a_norm", (1, 256)), ("kv_a_norm", (1, 128)), ("pool_w", (4, 128, 128)),
               ("pool_scale", (1, 512)), ("a_log", (1, 8)), ("dt_bias", (1, 8)), ("o_norm", (1, 128)),
               ("final_norm", (1, 1024)))
_ALL_NAMES = ("norm_ab", "w_in_ab", "q_a_norm", "w_q_b", "kv_a_norm", "w_kv_b", "pool_w", "pool_scale", "w_out_ab",
              "norm_c", "w_in_c", "conv_w", "a_log", "dt_bias", "o_norm", "w_out_c", "final_norm")


def _rows_of(shape):
    return max(1, math.prod(shape) // LANES)


def _as_rows(a):
    n = a.size
    if n % LANES:
        a = jnp.pad(a.reshape(1, n), ((0, 0), (0, LANES - n % LANES)))
    return a.reshape(-1, LANES)


def _pack(parts, total_rows):
    rows = jnp.concatenate([_as_rows(p) for p in parts], axis=0)
    return jnp.pad(rows, ((0, total_rows - rows.shape[0]), (0, 0)))


def _unpack(packed, spec):
    out, off = [], 0
    lead = packed.shape[:-2]
    for shape in spec:
        r = _rows_of(shape)
        blk = packed[..., off:off + r, :].reshape(lead + (r * LANES,))[..., :math.prod(shape)]
        out.append(blk.reshape(lead + tuple(shape)))
        off += r
    return out


def _round_up(n, m):
    return -(-n // m) * m


_SH_SHAPES = tuple(s for _, s, _ in _SHARDED)
_RB = sum(_rows_of(s) for s in _SH_SHAPES[:_N_BF16])
_RS = _round_up(sum(_rows_of(s) for s in _SH_SHAPES[_N_BF16:]), 8)
_RG = _round_up(sum(_rows_of(s) for s in _SH_SHAPES), 16)
_REP_SHAPES = tuple(s for _, s in _REPLICATED)
_RR = _round_up(sum(_rows_of(s) for s in _REP_SHAPES) + 1, 8)


def kernel(x, positions, norm_ab, w_in_ab, q_a_norm, w_q_b, kv_a_norm, w_kv_b, pool_w, pool_scale, w_out_ab, norm_c, w_in_c, conv_w, a_log, dt_bias, o_norm, w_out_c, final_norm, loss_target, m_norm_ab, m_w_in_ab, m_q_a_norm, m_w_q_b, m_kv_a_norm, m_w_kv_b, m_pool_w, m_pool_scale, m_w_out_ab, m_norm_c, m_w_in_c, m_conv_w, m_a_log, m_dt_bias, m_o_norm, m_w_out_c, m_final_norm, v_norm_ab, v_w_in_ab, v_q_a_norm, v_w_q_b, v_kv_a_norm, v_w_kv_b, v_pool_w, v_pool_scale, v_w_out_ab, v_norm_c, v_w_in_c, v_conv_w, v_a_log, v_dt_bias, v_o_norm, v_w_out_c, v_final_norm):
    given = dict(locals())
    c = lax.axis_index("c")
    t = x.shape[1]

    def shard_of(prefix, name):
        a = given[prefix + name]
        return a.reshape(a.shape[1:]) if a.ndim > 2 else a.reshape(1, -1)

    sh = [shard_of("", n) for n, _, _ in _SHARDED]
    wb = _pack([a.astype(BF16) for a in sh[:_N_BF16]], _RB).reshape(2, _RB // 2, LANES)
    ws = _pack(sh[_N_BF16:], _RS)
    gb, gs = _gather_weights(wb, ws)
    parts = _unpack(gb.reshape(4, _RB, LANES), _SH_SHAPES[:_N_BF16]) + _unpack(gs, _SH_SHAPES[_N_BF16:])
    full = {}
    for (name, _, axis), p in zip(_SHARDED, parts):
        full[name] = jnp.concatenate([p[j] for j in range(4)], axis=axis)
    for name, _ in _REPLICATED:
        full[name] = shard_of("", name)
    lw = _layout_weights(full)

    loss_tile, dx, g = _local_step(x[0], positions.reshape(t, 1), loss_target[0], lw)
    grads = _unlayout_grads(g)

    per_chip = []
    for j in range(4):
        pieces = []
        for name, shape, axis in _SHARDED:
            n = shape[axis]
            pieces.append(lax.slice_in_dim(grads[name], j * n, (j + 1) * n, axis=axis))
        per_chip.append(_pack(pieces, _RG))
    gfull = jnp.stack(per_chip).reshape(4, 2, _RG // 2, LANES)
    mine = lax.dynamic_index_in_dim(gfull, c, axis=1, keepdims=False)
    other = lax.dynamic_index_in_dim(gfull, 1 - c, axis=1, keepdims=False)
    from_sibling = _sibling_swap(other, name="core_swap_partial")
    pair = jnp.stack([mine, from_sibling]).reshape(2, 4 * (_RG // 2), LANES)
    chip_sum = _sum_slots(pair, name="core_sum").reshape(4, _RG // 2, LANES)
    small = _pack([grads[n] for n, _ in _REPLICATED] + [loss_tile[0:1, :]], _RR)
    landed, small_all = _chip_exchange(chip_sum, small)
    my_half = _sum_slots(landed, name="chip_sum")
    small_sum = _sum_slots(small_all, name="small_sum")
    sib_half = _sibling_swap(my_half, name="core_swap_sum")
    g_shard = jnp.where(c == 0, jnp.concatenate([my_half, sib_half]), jnp.concatenate([sib_half, my_half]))

    w_sh = _pack(sh, _RG)
    m_sh = _pack([shard_of("m_", n) for n, _, _ in _SHARDED], _RG)
    v_sh = _pack([shard_of("v_", n) for n, _, _ in _SHARDED], _RG)
    d_sh, mo_sh, vo_sh = _adamw(g_shard, w_sh, m_sh, v_sh, name="adamw_sharded")
    zero_row = jnp.zeros((1, LANES), F32)
    w_rp = _pack([shard_of("", n) for n, _ in _REPLICATED] + [zero_row], _RR)
    m_rp = _pack([shard_of("m_", n) for n, _ in _REPLICATED] + [zero_row], _RR)
    v_rp = _pack([shard_of("v_", n) for n, _ in _REPLICATED] + [zero_row + 1.0], _RR)
    d_rp, mo_rp, vo_rp = _adamw(small_sum, w_rp, m_rp, v_rp, name="adamw_replicated")

    res = {}
    for key, sh_pack, rp_pack in (("grad", g_shard, small_sum), ("delta", d_sh, d_rp), ("m", mo_sh, mo_rp), ("v", vo_sh, vo_rp)):
        for (name, _, _), a in zip(_SHARDED, _unpack(sh_pack, _SH_SHAPES)):
            res[key, name] = a.reshape(given[name].shape)
        for (name, _), a in zip(_REPLICATED, _unpack(rp_pack, _REP_SHAPES)):
            res[key, name] = a.reshape(given[name].shape)
    loss = small_sum[sum(_rows_of(s) for s in _REP_SHAPES), 0]
    outs = [loss, dx.reshape(x.shape)]
    for key in ("grad", "delta", "m", "v"):
        outs += [res[key, n] for n in _ALL_NAMES]
    return tuple(outs)
```

```python
import functools
import math

import jax
import jax.numpy as jnp
from jax import lax
from jax.experimental import pallas as pl
from jax.experimental.pallas import tpu as pltpu

F32 = jnp.float32
BF16 = jnp.bfloat16
HI = lax.Precision.HIGHEST
MESH = pl.DeviceIdType.MESH

RMS_EPS = 1e-6
D_MODEL = 1024
MLA_HEADS = 8
MLA_Q_RANK = 256
MLA_KV_RANK = 128
MLA_NOPE = 64
MLA_ROPE = 32
MLA_V = 64
ROPE_THETA = 10000.0
POOL_WINDOWS = (2, 4, 8, 16)
POOL_GROUP = 128
POOL_WIDTH = 512
POOL_HALO = 16
GDN_HEADS = 8
GDN_DK = 128
CONV_WIDTH = 4
CONV_HALO = 8
CHUNK = 64
IN_AB_PAD = 2048
IN_C_PAD = 4224
ATT_SCALE = (MLA_NOPE + MLA_ROPE) ** -0.5

ADAM_LR = 0.001
ADAM_B1 = 0.9
ADAM_B2 = 0.999
ADAM_EPS = 1e-08
ADAM_WD = 0.01
ADAM_STEP = 10

LANES = 128
VMEM_LIMIT = 56 * 1024 * 1024

ROW_TILE = 256
ATT_TILE = 512
GDN_TILE = 256
MM_TILE = 512

NN = (((1,), (0,)), ((), ()))
NT = (((1,), (1,)), ((), ()))
TN = (((0,), (0,)), ((), ()))


def _dot(a, b, dims=NN, prec=None):
    return lax.dot_general(a, b, dims, precision=prec, preferred_element_type=F32)


def _tile(n, pref):
    if n <= pref:
        return n
    step = LANES if pref >= LANES else 8
    for t in range(pref - pref % step, 0, -step):
        if n % t == 0:
            return t
    return n


def _params(sem):
    return pltpu.CompilerParams(dimension_semantics=sem, vmem_limit_bytes=VMEM_LIMIT)


def _sigmoid(x):
    return 1.0 / (1.0 + jnp.exp(-x))


def _softplus(x):
    return jnp.maximum(x, 0.0) + jnp.log(1.0 + jnp.exp(-jnp.abs(x)))


def _matmul(a, b, mode, *, name, out_dtype=F32, add=None, tm=MM_TILE, tn=MM_TILE, tk=2048):
    if mode == "nn":
        (m, k), (k2, n) = a.shape, b.shape
    elif mode == "nt":
        (m, k), (n, k2) = a.shape, b.shape
    else:
        (k, m), (k2, n) = a.shape, b.shape
    assert k == k2, (a.shape, b.shape, mode)
    tm, tn, tk = _tile(m, tm), _tile(n, tn), _tile(k, tk)
    nk = k // tk
    if mode == "tn":
        a_spec = pl.BlockSpec((tk, tm), lambda i, j, kk: (kk, i))
    else:
        a_spec = pl.BlockSpec((tm, tk), lambda i, j, kk: (i, kk))
    if mode == "nt":
        b_spec = pl.BlockSpec((tn, tk), lambda i, j, kk: (j, kk))
    else:
        b_spec = pl.BlockSpec((tk, tn), lambda i, j, kk: (kk, j))
    o_spec = pl.BlockSpec((tm, tn), lambda i, j, kk: (i, j))
    dims = {"nn": NN, "nt": NT, "tn": TN}[mode]
    has_add = add is not None

    def body(*refs):
        if has_add:
            a_ref, b_ref, add_ref, o_ref, acc = refs
        else:
            a_ref, b_ref, o_ref, acc = refs
        kk = pl.program_id(2)

        @pl.when(kk == 0)
        def _():
            acc[...] = jnp.zeros_like(acc)

        acc[...] += _dot(a_ref[...], b_ref[...], dims)

        @pl.when(kk == nk - 1)
        def _():
            o = acc[...]
            if has_add:
                o = o + add_ref[...]
            o_ref[...] = o.astype(out_dtype)

    in_specs = [a_spec, b_spec] + ([o_spec] if has_add else [])
    args = (a, b) + ((add,) if has_add else ())
    return pl.pallas_call(
        body, name=name, grid=(m // tm, n // tn, nk), in_specs=in_specs, out_specs=o_spec,
        out_shape=jax.ShapeDtypeStruct((m, n), out_dtype), scratch_shapes=[pltpu.VMEM((tm, tn), F32)],
        compiler_params=_params(("parallel", "parallel", "arbitrary")),
    )(*args)


def _rms_fwd(h, g, *, name):
    t, d = h.shape
    tm = _tile(t, 2 * ROW_TILE)

    def body(h_ref, g_ref, o_ref):
        x = h_ref[...]
        r = lax.rsqrt(jnp.mean(x * x, axis=-1, keepdims=True) + RMS_EPS)
        o_ref[...] = (x * r * g_ref[...]).astype(BF16)

    return pl.pallas_call(
        body, name=name, grid=(t // tm,),
        in_specs=[pl.BlockSpec((tm, d), lambda i: (i, 0)), pl.BlockSpec((1, d), lambda i: (0, 0))],
        out_specs=pl.BlockSpec((tm, d), lambda i: (i, 0)),
        out_shape=jax.ShapeDtypeStruct((t, d), BF16), compiler_params=_params(("parallel",)),
    )(h, g)


def _rms_bwd(h, g, dy, dres, *, name, with_bf16):
    t, d = h.shape
    tm = _tile(t, 2 * ROW_TILE)

    def body(h_ref, g_ref, dy_ref, dres_ref, *outs):
        i = pl.program_id(0)
        dh_ref, dg_ref = outs[0], outs[-1]
        x = h_ref[...]
        r = lax.rsqrt(jnp.mean(x * x, axis=-1, keepdims=True) + RMS_EPS)
        xh = x * r
        dyv = dy_ref[...].astype(F32)
        dxh = dyv * g_ref[...]
        dx = r * (dxh - xh * jnp.mean(dxh * xh, axis=-1, keepdims=True))
        dh = dres_ref[...] + dx
        dh_ref[...] = dh
        if with_bf16:
            outs[1][...] = dh.astype(BF16)

        @pl.when(i == 0)
        def _():
            dg_ref[...] = jnp.zeros_like(dg_ref)

        dg_ref[...] += jnp.sum(dyv * xh, axis=0, keepdims=True)

    row = pl.BlockSpec((tm, d), lambda i: (i, 0))
    vec = pl.BlockSpec((1, d), lambda i: (0, 0))
    out_shape = [jax.ShapeDtypeStruct((t, d), F32)]
    out_specs = [row]
    if with_bf16:
        out_shape.append(jax.ShapeDtypeStruct((t, d), BF16))
        out_specs.append(row)
    out_shape.append(jax.ShapeDtypeStruct((1, d), F32))
    out_specs.append(vec)
    return pl.pallas_call(
        body, name=name, grid=(t // tm,), in_specs=[row, vec, row, row], out_specs=out_specs,
        out_shape=out_shape, compiler_params=_params(("arbitrary",)),
    )(h, g, dy, dres)


def _rope_partner(x):
    lane = lax.broadcasted_iota(jnp.int32, x.shape, 1)
    swapped = jnp.where(lane < MLA_NOPE + MLA_ROPE // 2, pltpu.roll(x, LANES - 16, 1), pltpu.roll(x, 16, 1))
    return jnp.where((lane >= MLA_NOPE) & (lane < MLA_NOPE + MLA_ROPE), swapped, 0.0)


def _pool_counts(row0, tm, w):
    t_idx = row0 + lax.broadcasted_iota(jnp.int32, (tm, POOL_GROUP), 0)
    return jnp.minimum(t_idx + 1, w).astype(F32)


def _ab_prep(proj, pos, inv_freq, q_a_norm, kv_a_norm, *, name):
    t = proj.shape[0]
    tm = _tile(t, ROW_TILE)
    hb = tm // POOL_HALO

    def body(p_ref, halo_ref, pos_ref, inv_ref, qg_ref, kg_ref, qn_ref, kvn_ref, kr_ref, d_ref, cos_ref, sin_ref, ext):
        i = pl.program_id(0)
        ql = p_ref[:, 0:MLA_Q_RANK]
        r = lax.rsqrt(jnp.mean(ql * ql, axis=-1, keepdims=True) + RMS_EPS)
        qn_ref[...] = (ql * r * qg_ref[...]).astype(BF16)
        kl = p_ref[:, MLA_Q_RANK:MLA_Q_RANK + MLA_KV_RANK]
        r = lax.rsqrt(jnp.mean(kl * kl, axis=-1, keepdims=True) + RMS_EPS)
        kvn_ref[...] = (kl * r * kg_ref[...]).astype(BF16)
        ang = pos_ref[...].astype(F32) * inv_ref[...]
        lane = lax.broadcasted_iota(jnp.int32, (tm, LANES), 1)
        in_rope = (lane >= MLA_NOPE) & (lane < MLA_NOPE + MLA_ROPE)
        cos_t = jnp.where(in_rope, jnp.cos(ang), 1.0)
        sin_t = jnp.where(in_rope, jnp.sin(ang), 0.0)
        sin_t = jnp.where(lane < MLA_NOPE + MLA_ROPE // 2, -sin_t, sin_t)
        cos_ref[...] = cos_t
        sin_ref[...] = sin_t
        kr = p_ref[:, 384:512]
        kr_ref[...] = kr * cos_t + _rope_partner(kr) * sin_t
        xp = p_ref[:, 512:1024]
        ext[0:POOL_HALO, :] = jnp.where(i > 0, halo_ref[...], 0.0)
        ext[POOL_HALO:POOL_HALO + tm, :] = xp
        for g, w in enumerate(POOL_WINDOWS):
            lo = g * POOL_GROUP
            acc = ext[POOL_HALO:POOL_HALO + tm, lo:lo + POOL_GROUP]
            for s in range(1, w):
                acc = acc + ext[POOL_HALO - s:POOL_HALO - s + tm, lo:lo + POOL_GROUP]
            cnt = _pool_counts(i * tm, tm, w)
            d_ref[:, lo:lo + POOL_GROUP] = (acc / cnt - xp[:, lo:lo + POOL_GROUP]).astype(BF16)

    row = lambda w: pl.BlockSpec((tm, w), lambda i: (i, 0))
    vec = lambda w: pl.BlockSpec((1, w), lambda i: (0, 0))
    return pl.pallas_call(
        body, name=name, grid=(t // tm,),
        in_specs=[row(1024), pl.BlockSpec((POOL_HALO, POOL_WIDTH), lambda i: (jnp.maximum(i * hb - 1, 0), 1)),
                  pl.BlockSpec((tm, 1), lambda i: (i, 0)), vec(LANES), vec(MLA_Q_RANK), vec(MLA_KV_RANK)],
        out_specs=[row(MLA_Q_RANK), row(MLA_KV_RANK), row(LANES), row(POOL_WIDTH), row(LANES), row(LANES)],
        out_shape=[jax.ShapeDtypeStruct((t, MLA_Q_RANK), BF16), jax.ShapeDtypeStruct((t, MLA_KV_RANK), BF16),
                   jax.ShapeDtypeStruct((t, LANES), F32), jax.ShapeDtypeStruct((t, POOL_WIDTH), BF16),
                   jax.ShapeDtypeStruct((t, LANES), F32), jax.ShapeDtypeStruct((t, LANES), F32)],
        scratch_shapes=[pltpu.VMEM((tm + POOL_HALO, POOL_WIDTH), F32)],
        compiler_params=_params(("parallel",)),
    )(proj, proj, pos, inv_freq, q_a_norm, kv_a_norm)


def _qk_rope(qraw, kvk, kr, cos_t, sin_t, *, name):
    t = qraw.shape[0]
    tm = _tile(t, 2 * ROW_TILE)

    def body(q_ref, k_ref, kr_ref, c_ref, s_ref, qo_ref, ko_ref):
        c, s, krv = c_ref[...], s_ref[...], kr_ref[...]
        for h in range(MLA_HEADS):
            sl = slice(h * LANES, (h + 1) * LANES)
            q = q_ref[:, sl]
            qo_ref[:, sl] = ((q * c + _rope_partner(q) * s) * ATT_SCALE).astype(BF16)
            ko_ref[:, sl] = (k_ref[:, sl] + krv).astype(BF16)

    row = lambda w: pl.BlockSpec((tm, w), lambda i: (i, 0))
    return pl.pallas_call(
        body, name=name, grid=(t // tm,), in_specs=[row(1024), row(1024), row(LANES), row(LANES), row(LANES)],
        out_specs=[row(1024), row(1024)],
        out_shape=[jax.ShapeDtypeStruct((t, 1024), BF16), jax.ShapeDtypeStruct((t, 1024), BF16)],
        compiler_params=_params(("parallel",)),
    )(qraw, kvk, kr, cos_t, sin_t)


def _qk_rope_bwd(dq, dk, cos_t, sin_t, *, name):
    t = dq.shape[0]
    tm = _tile(t, 2 * ROW_TILE)

    def body(dq_ref, dk_ref, c_ref, s_ref, dqo_ref, dko_ref, dkr_ref):
        c, s = c_ref[...], s_ref[...]
        lane = lax.broadcasted_iota(jnp.int32, (tm, LANES), 1)
        in_rope = (lane >= MLA_NOPE) & (lane < MLA_NOPE + MLA_ROPE)
        dkr = jnp.zeros((tm, LANES), F32)
        for h in range(MLA_HEADS):
            sl = slice(h * LANES, (h + 1) * LANES)
            g = dq_ref[:, sl]
            dqo_ref[:, sl] = ((g * c + _rope_partner(g * s)) * ATT_SCALE).astype(BF16)
            gk = dk_ref[:, sl]
            dko_ref[:, sl] = gk.astype(BF16)
            dkr = dkr + jnp.where(in_rope, gk, 0.0)
        dkr_ref[...] = dkr * c + _rope_partner(dkr * s)

    row = lambda w: pl.BlockSpec((tm, w), lambda i: (i, 0))
    return pl.pallas_call(
        body, name=name, grid=(t // tm,), in_specs=[row(1024), row(1024), row(LANES), row(LANES)],
        out_specs=[row(1024), row(1024), row(LANES)],
        out_shape=[jax.ShapeDtypeStruct((t, 1024), BF16), jax.ShapeDtypeStruct((t, 1024), BF16),
                   jax.ShapeDtypeStruct((t, LANES), F32)],
        compiler_params=_params(("parallel",)),
    )(dq, dk, cos_t, sin_t)


def _ab_prep_bwd(proj, q_a_norm, kv_a_norm, dqn, dkvn_k, dkvn_v, dkr, dd, dz, *, name):
    t = proj.shape[0]
    tm = _tile(t, ROW_TILE)
    hb = tm // POOL_HALO
    last_halo = t // POOL_HALO - 1
    nt = t // tm

    def body(p_ref, qg_ref, kg_ref, dqn_ref, dk1_ref, dk2_ref, dkr_ref, dd_ref, ddn_ref, dz_ref,
             dp_ref, dqg_ref, dkg_ref, ext):
        i = pl.program_id(0)

        @pl.when(i == 0)
        def _():
            dqg_ref[...] = jnp.zeros_like(dqg_ref)
            dkg_ref[...] = jnp.zeros_like(dkg_ref)

        def norm_bwd(x, g, dy, dg_ref):
            r = lax.rsqrt(jnp.mean(x * x, axis=-1, keepdims=True) + RMS_EPS)
            xh = x * r
            dxh = dy * g
            dg_ref[...] += jnp.sum(dy * xh, axis=0, keepdims=True)
            return r * (dxh - xh * jnp.mean(dxh * xh, axis=-1, keepdims=True))

        dql = norm_bwd(p_ref[:, 0:MLA_Q_RANK], qg_ref[...], dqn_ref[...], dqg_ref)
        dp_ref[:, 0:MLA_Q_RANK] = dql.astype(BF16)
        dkl = norm_bwd(p_ref[:, MLA_Q_RANK:384], kg_ref[...], dk1_ref[...] + dk2_ref[...], dkg_ref)
        dp_ref[:, MLA_Q_RANK:384] = dkl.astype(BF16)
        dp_ref[:, 384:512] = dkr_ref[...].astype(BF16)
        ddv = dd_ref[...]
        for g, w in enumerate(POOL_WINDOWS):
            lo = g * POOL_GROUP
            ext[0:tm, lo:lo + POOL_GROUP] = ddv[:, lo:lo + POOL_GROUP] / _pool_counts(i * tm, tm, w)
            nxt = ddn_ref[:, lo:lo + POOL_GROUP] / _pool_counts((i + 1) * tm, POOL_HALO, w)
            ext[tm:tm + POOL_HALO, lo:lo + POOL_GROUP] = jnp.where(i < nt - 1, nxt, 0.0)
        for g, w in enumerate(POOL_WINDOWS):
            lo = g * POOL_GROUP
            acc = ext[0:tm, lo:lo + POOL_GROUP]
            for s in range(1, w):
                acc = acc + ext[s:s + tm, lo:lo + POOL_GROUP]
            dp_ref[:, 512 + lo:512 + lo + POOL_GROUP] = (acc - ddv[:, lo:lo + POOL_GROUP]).astype(BF16)
        dp_ref[:, 1024:2048] = dz_ref[...]

    row = lambda w: pl.BlockSpec((tm, w), lambda i: (i, 0))
    vec = lambda w: pl.BlockSpec((1, w), lambda i: (0, 0))
    return pl.pallas_call(
        body, name=name, grid=(nt,),
        in_specs=[row(1024), vec(MLA_Q_RANK), vec(MLA_KV_RANK), row(MLA_Q_RANK), row(MLA_KV_RANK), row(MLA_KV_RANK),
                  row(LANES), row(POOL_WIDTH),
                  pl.BlockSpec((POOL_HALO, POOL_WIDTH), lambda i: (jnp.minimum((i + 1) * hb, last_halo), 0)),
                  row(1024)],
        out_specs=[row(IN_AB_PAD), vec(MLA_Q_RANK), vec(MLA_KV_RANK)],
        out_shape=[jax.ShapeDtypeStruct((t, IN_AB_PAD), BF16), jax.ShapeDtypeStruct((1, MLA_Q_RANK), F32),
                   jax.ShapeDtypeStruct((1, MLA_KV_RANK), F32)],
        scratch_shapes=[pltpu.VMEM((tm + POOL_HALO, POOL_WIDTH), F32)],
        compiler_params=_params(("arbitrary",)),
    )(proj, q_a_norm, kv_a_norm, dqn, dkvn_k, dkvn_v, dkr, dd, dd, dz)


def _gate_fwd(o, ybraw, proj, pool_scale, *, name):
    t = o.shape[0]
    tm = _tile(t, 2 * ROW_TILE)

    def body(o_ref, yb_ref, z_ref, ps_ref, y_ref):
        z = z_ref[...]
        sz = z * _sigmoid(z)
        y_ref[:, 0:512] = (o_ref[...] * sz[:, 0:512]).astype(BF16)
        y_ref[:, 512:1024] = (yb_ref[...] * ps_ref[...] * sz[:, 512:1024]).astype(BF16)

    row = lambda w: pl.BlockSpec((tm, w), lambda i: (i, 0))
    return pl.pallas_call(
        body, name=name, grid=(t // tm,),
        in_specs=[row(512), row(512), pl.BlockSpec((tm, 1024), lambda i: (i, 1)), pl.BlockSpec((1, 512), lambda i: (0, 0))],
        out_specs=row(1024), out_shape=jax.ShapeDtypeStruct((t, 1024), BF16), compiler_params=_params(("parallel",)),
    )(o, ybraw, proj, pool_scale)


def _gate_bwd(dy, o, ybraw, proj, pool_scale, *, name):
    t = o.shape[0]
    tm = _tile(t, ROW_TILE)

    def body(dy_ref, o_ref, yb_ref, z_ref, ps_ref, do_ref, dl_ref, dyb_ref, dz_ref, dps_ref):
        i = pl.program_id(0)
        z = z_ref[...]
        sg = _sigmoid(z)
        sz = z * sg
        dsz = sg * (1.0 + z * (1.0 - sg))
        dyv = dy_ref[...]
        dcat = dyv * sz
        ov = o_ref[...]
        ybs = yb_ref[...] * ps_ref[...]
        dz_ref[:, 0:512] = (dyv[:, 0:512] * ov * dsz[:, 0:512]).astype(BF16)
        dz_ref[:, 512:1024] = (dyv[:, 512:1024] * ybs * dsz[:, 512:1024]).astype(BF16)
        do = dcat[:, 0:512]
        do_ref[...] = do.astype(BF16)
        r_i = lax.broadcasted_iota(jnp.int32, (512, 512), 0) // MLA_V
        c_i = lax.broadcasted_iota(jnp.int32, (512, 512), 1) // MLA_V
        dl_ref[...] = _dot(do * ov, (r_i == c_i).astype(F32), NN, HI)
        dyb_ref[...] = (dcat[:, 512:1024] * ps_ref[...]).astype(BF16)

        @pl.when(i == 0)
        def _():
            dps_ref[...] = jnp.zeros_like(dps_ref)

        dps_ref[...] += jnp.sum(dcat[:, 512:1024] * yb_ref[...], axis=0, keepdims=True)

    row = lambda w: pl.BlockSpec((tm, w), lambda i: (i, 0))
    vec = pl.BlockSpec((1, 512), lambda i: (0, 0))
    return pl.pallas_call(
        body, name=name, grid=(t // tm,),
        in_specs=[row(1024), row(512), row(512), pl.BlockSpec((tm, 1024), lambda i: (i, 1)), vec],
        out_specs=[row(512), row(512), row(512), row(1024), vec],
        out_shape=[jax.ShapeDtypeStruct((t, 512), BF16), jax.ShapeDtypeStruct((t, 512), F32),
                   jax.ShapeDtypeStruct((t, 512), BF16), jax.ShapeDtypeStruct((t, 1024), BF16),
                   jax.ShapeDtypeStruct((1, 512), F32)],
        compiler_params=_params(("arbitrary",)),
    )(dy, o, ybraw, proj, pool_scale)


ATT_HP_FWD = 4
ATT_HP_BWD = 2


def _diag_mask(tq):
    return lax.broadcasted_iota(jnp.int32, (tq, tq), 1) <= lax.broadcasted_iota(jnp.int32, (tq, tq), 0)


def _block_schedule(nq, key_major):
    if key_major:
        pairs = [(qi, ki) for ki in range(nq) for qi in range(ki, nq)]
    else:
        pairs = [(qi, ki) for qi in range(nq) for ki in range(qi + 1)]
    return jnp.asarray([p[0] for p in pairs], jnp.int32), jnp.asarray([p[1] for p in pairs], jnp.int32)


def _attn_fwd(q, k, v, *, name):
    t = q.shape[0]
    tq = _tile(t, ATT_TILE)
    nq = t // tq
    hp = ATT_HP_FWD
    qi_tab, ki_tab = _block_schedule(nq, key_major=False)

    def body(qi_ref, ki_ref, q_ref, k_ref, v_ref, o_ref, lse_ref, m_sc, l_sc, acc_sc):
        step = pl.program_id(1)
        qi, ki = qi_ref[step], ki_ref[step]

        @pl.when(ki == 0)
        def _():
            m_sc[...] = jnp.full_like(m_sc, -jnp.inf)
            l_sc[...] = jnp.zeros_like(l_sc)
            acc_sc[...] = jnp.zeros_like(acc_sc)

        def block(on_diagonal):
            scores = []
            for h in range(hp):
                sl = slice(h * LANES, (h + 1) * LANES)
                scores.append(_dot(q_ref[:, sl], k_ref[:, sl], NT))
            if on_diagonal:
                mask = _diag_mask(tq)
                scores = [jnp.where(mask, s, -jnp.inf) for s in scores]
            for h, s in enumerate(scores):
                vv = v_ref[:, (h // 2) * LANES:(h // 2 + 1) * LANES]
                m_prev = m_sc[h]
                m_new = jnp.maximum(m_prev, jnp.max(s, axis=-1, keepdims=True))
                alpha = jnp.exp(m_prev - m_new)
                p = jnp.exp(s - m_new[:, 0:1])
                l_sc[h] = alpha * l_sc[h] + jnp.sum(p, axis=-1, keepdims=True)
                acc_sc[h] = alpha * acc_sc[h] + _dot(p.astype(BF16), vv)
                m_sc[h] = m_new

        pl.when(ki < qi)(functools.partial(block, False))
        pl.when(ki == qi)(functools.partial(block, True))

        @pl.when(ki == qi)
        def _():
            first = lax.broadcasted_iota(jnp.int32, (tq, LANES), 1) < MLA_V
            for pr in range(hp // 2):
                a, b = 2 * pr, 2 * pr + 1
                sl = slice(pr * LANES, (pr + 1) * LANES)
                o_ref[:, sl] = jnp.where(first, acc_sc[a] / l_sc[a], acc_sc[b] / l_sc[b])
                lse_ref[:, sl] = jnp.where(first, m_sc[a] + jnp.log(l_sc[a]), m_sc[b] + jnp.log(l_sc[b]))

    grid_spec = pltpu.PrefetchScalarGridSpec(
        num_scalar_prefetch=2, grid=(MLA_HEADS // hp, qi_tab.shape[0]),
        in_specs=[pl.BlockSpec((tq, hp * LANES), lambda g, s, qt, kt: (qt[s], g)),
                  pl.BlockSpec((tq, hp * LANES), lambda g, s, qt, kt: (kt[s], g)),
                  pl.BlockSpec((tq, hp * MLA_V), lambda g, s, qt, kt: (kt[s], g))],
        out_specs=[pl.BlockSpec((tq, hp * MLA_V), lambda g, s, qt, kt: (qt[s], g)),
                   pl.BlockSpec((tq, hp * MLA_V), lambda g, s, qt, kt: (qt[s], g))],
        scratch_shapes=[pltpu.VMEM((hp, tq, LANES), F32), pltpu.VMEM((hp, tq, LANES), F32),
                        pltpu.VMEM((hp, tq, LANES), F32)],
    )
    return pl.pallas_call(
        body, name=name, grid_spec=grid_spec,
        out_shape=[jax.ShapeDtypeStruct((t, 512), F32), jax.ShapeDtypeStruct((t, 512), F32)],
        compiler_params=_params(("parallel", "arbitrary")),
    )(qi_tab, ki_tab, q, k, v)


def _attn_bwd(q, k, v, do, lse, delta, *, name):
    t = q.shape[0]
    tq = _tile(t, ATT_TILE)
    nq = t // tq
    hp = ATT_HP_BWD
    qi_tab, ki_tab = _block_schedule(nq, key_major=True)

    def body(qi_ref, ki_ref, q_ref, k_ref, v_ref, do_ref, lse_ref, dl_ref, dq_ref, dk_ref, dv_ref, dk_sc, dv_sc):
        step = pl.program_id(1)
        qi, ki = qi_ref[step], ki_ref[step]

        @pl.when(step == 0)
        def _():
            dq_ref[...] = jnp.zeros_like(dq_ref)

        @pl.when(qi == ki)
        def _():
            dk_sc[...] = jnp.zeros_like(dk_sc)
            dv_sc[...] = jnp.zeros_like(dv_sc)

        def block(on_diagonal):
            lane = lax.broadcasted_iota(jnp.int32, (tq, LANES), 1)
            rows = pl.ds(pl.multiple_of(qi * tq, tq), tq)
            heads = [slice(h * LANES, (h + 1) * LANES) for h in range(hp)]
            scores = [_dot(q_ref[:, sl], k_ref[:, sl], NT) for sl in heads]
            dps = []
            for h in range(hp):
                dov = do_ref[:, (h // 2) * LANES:(h // 2 + 1) * LANES]
                mine = (lane < MLA_V) if h % 2 == 0 else (lane >= MLA_V)
                dps.append(_dot(jnp.where(mine, dov, jnp.zeros_like(dov)), v_ref[:, (h // 2) * LANES:(h // 2 + 1) * LANES], NT))
            mask = _diag_mask(tq) if on_diagonal else None
            for h, sl in enumerate(heads):
                col = (h // 2) * LANES + (h % 2) * MLA_V
                p = jnp.exp(scores[h] - lse_ref[:, col:col + 1])
                if on_diagonal:
                    p = jnp.where(mask, p, 0.0)
                ds = (p * (dps[h] - dl_ref[:, col:col + 1])).astype(BF16)
                dv_sc[h] += _dot(p.astype(BF16), do_ref[:, (h // 2) * LANES:(h // 2 + 1) * LANES], TN)
                dk_sc[h] += _dot(ds, q_ref[:, sl], TN)
                dq_ref[rows, sl] += _dot(ds, k_ref[:, sl], NN)

        pl.when(qi > ki)(functools.partial(block, False))
        pl.when(qi == ki)(functools.partial(block, True))

        @pl.when(qi == nq - 1)
        def _():
            first = lax.broadcasted_iota(jnp.int32, (tq, LANES), 1) < MLA_V
            for h in range(hp):
                dk_ref[:, h * LANES:(h + 1) * LANES] = dk_sc[h]
            for pr in range(hp // 2):
                dv_ref[:, pr * LANES:(pr + 1) * LANES] = jnp.where(first, dv_sc[2 * pr], dv_sc[2 * pr + 1]).astype(BF16)

    qrow = lambda w: pl.BlockSpec((tq, w), lambda g, s, qt, kt: (qt[s], g))
    krow = lambda w: pl.BlockSpec((tq, w), lambda g, s, qt, kt: (kt[s], g))
    grid_spec = pltpu.PrefetchScalarGridSpec(
        num_scalar_prefetch=2, grid=(MLA_HEADS // hp, qi_tab.shape[0]),
        in_specs=[qrow(hp * LANES), krow(hp * LANES), krow(hp * MLA_V), qrow(hp * MLA_V), qrow(hp * MLA_V), qrow(hp * MLA_V)],
        out_specs=[pl.BlockSpec((t, hp * LANES), lambda g, s, qt, kt: (0, g)), krow(hp * LANES), krow(hp * MLA_V)],
        scratch_shapes=[pltpu.VMEM((hp, tq, LANES), F32), pltpu.VMEM((hp, tq, LANES), F32)],
    )
    return pl.pallas_call(
        body, name=name, grid_spec=grid_spec,
        out_shape=[jax.ShapeDtypeStruct((t, 1024), F32), jax.ShapeDtypeStruct((t, 1024), F32),
                   jax.ShapeDtypeStruct((t, 512), BF16)],
        compiler_params=_params(("parallel", "arbitrary")),
    )(qi_tab, ki_tab, q, k, v, do, lse, delta)


def _conv_rows(ext, tm, w_ref, sec):
    c0 = sec * 1024
    y = ext[CONV_HALO - 3:CONV_HALO - 3 + tm, c0:c0 + 1024] * w_ref[0:1, c0:c0 + 1024]
    for j in range(1, CONV_WIDTH):
        y = y + ext[CONV_HALO - 3 + j:CONV_HALO - 3 + j + tm, c0:c0 + 1024] * w_ref[j:j + 1, c0:c0 + 1024]
    return y


def _c_prep(proj_c, conv_w, a_log, dt_bias, *, name):
    t = proj_c.shape[0]
    tm = _tile(t, ROW_TILE)
    hb = tm // CONV_HALO

    def body(p_ref, halo_ref, ab_ref, w_ref, al_ref, dtb_ref, q_ref, k_ref, v_ref, g_ref, b_ref, gt_ref, ext):
        i = pl.program_id(0)
        ext[0:CONV_HALO, :] = jnp.where(i > 0, halo_ref[...], 0.0)
        ext[CONV_HALO:CONV_HALO + tm, :] = p_ref[...]
        for sec, o_ref in enumerate((q_ref, k_ref, v_ref)):
            y = _conv_rows(ext, tm, w_ref, sec)
            y = y * _sigmoid(y)
            if sec == 2:
                o_ref[...] = y
                continue
            scale = GDN_DK ** -0.5 if sec == 0 else 1.0
            for h in range(GDN_HEADS):
                sl = slice(h * LANES, (h + 1) * LANES)
                blk = y[:, sl]
                r = lax.rsqrt(jnp.sum(blk * blk, axis=-1, keepdims=True) + RMS_EPS)
                o_ref[:, sl] = blk * (r * scale)
        ab = ab_ref[...]
        g = -jnp.exp(al_ref[...]) * _softplus(ab + dtb_ref[...])
        beta = _sigmoid(ab)
        ri = lax.broadcasted_iota(jnp.int32, (tm, tm), 0)
        ci = lax.broadcasted_iota(jnp.int32, (tm, tm), 1)
        lower = ((ri // CHUNK) == (ci // CHUNK)) & (ri >= ci)
        gc = _dot(lower.astype(F32), g, NN, HI)
        eye = lax.broadcasted_iota(jnp.int32, (LANES, LANES), 0) == lax.broadcasted_iota(jnp.int32, (LANES, LANES), 1)
        gt_ref[...] = _dot(eye.astype(F32), gc, NT, HI)[0:GDN_HEADS, :]
        for h in range(GDN_HEADS):
            sl = slice(h * LANES, (h + 1) * LANES)
            g_ref[:, sl] = jnp.broadcast_to(gc[:, h:h + 1], (tm, LANES))
            b_ref[:, sl] = jnp.broadcast_to(beta[:, GDN_HEADS + h:GDN_HEADS + h + 1], (tm, LANES))

    row = lambda w: pl.BlockSpec((tm, w), lambda i: (i, 0))
    vec = lambda r, w: pl.BlockSpec((r, w), lambda i: (0, 0))
    out = jax.ShapeDtypeStruct((t, 1024), F32)
    return pl.pallas_call(
        body, name=name, grid=(t // tm,),
        in_specs=[row(3072), pl.BlockSpec((CONV_HALO, 3072), lambda i: (jnp.maximum(i * hb - 1, 0), 0)),
                  pl.BlockSpec((tm, LANES), lambda i: (i, 32)), vec(CONV_WIDTH, 3072), vec(1, LANES), vec(1, LANES)],
        out_specs=[row(1024)] * 5 + [pl.BlockSpec((GDN_HEADS, tm), lambda i: (0, i))],
        out_shape=[out] * 5 + [jax.ShapeDtypeStruct((GDN_HEADS, t), F32)],
        scratch_shapes=[pltpu.VMEM((tm + CONV_HALO, 3072), F32)],
        compiler_params=_params(("parallel",)),
    )(proj_c, proj_c, proj_c, conv_w, a_log, dt_bias)


def _c_prep_bwd(proj_c, conv_w, a_log, dt_bias, dq, dk, dv, dgb, dbb, dz, *, name):
    t = proj_c.shape[0]
    tm = _tile(t, ROW_TILE // 2)
    hb = tm // CONV_HALO
    nt = t // tm
    rev = lambda i: nt - 1 - i

    def body(p_ref, halo_ref, ab_ref, w_ref, al_ref, dtb_ref, dq_ref, dk_ref, dv_ref, dg_ref, db_ref, dz_ref,
             dp_ref, dw_ref, dal_ref, ddt_ref, ext, dyext, carry):
        step = pl.program_id(0)
        i = rev(step)

        @pl.when(step == 0)
        def _():
            dw_ref[...] = jnp.zeros_like(dw_ref)
            dal_ref[...] = jnp.zeros_like(dal_ref)
            ddt_ref[...] = jnp.zeros_like(ddt_ref)
            carry[...] = jnp.zeros_like(carry)

        ext[0:CONV_HALO, :] = jnp.where(i > 0, halo_ref[...], 0.0)
        ext[CONV_HALO:CONV_HALO + tm, :] = p_ref[...]
        for sec, g_ref in enumerate((dq_ref, dk_ref, dv_ref)):
            c0 = sec * 1024
            y = _conv_rows(ext, tm, w_ref, sec)
            sg = _sigmoid(y)
            act = y * sg
            if sec == 2:
                dact = g_ref[...]
            else:
                scale = GDN_DK ** -0.5 if sec == 0 else 1.0
                parts = []
                for h in range(GDN_HEADS):
                    sl = slice(h * LANES, (h + 1) * LANES)
                    blk = act[:, sl]
                    r = lax.rsqrt(jnp.sum(blk * blk, axis=-1, keepdims=True) + RMS_EPS)
                    n = blk * r
                    dn = g_ref[:, sl] * scale
                    parts.append(r * (dn - n * jnp.sum(dn * n, axis=-1, keepdims=True)))
                dact = jnp.concatenate(parts, axis=-1)
            dy = dact * (sg * (1.0 + y * (1.0 - sg)))
            dyext[0:tm, c0:c0 + 1024] = dy
            for j in range(CONV_WIDTH):
                xs = ext[CONV_HALO - 3 + j:CONV_HALO - 3 + j + tm, c0:c0 + 1024]
                dw_ref[j:j + 1, c0:c0 + 1024] += jnp.sum(dy * xs, axis=0, keepdims=True)
        dyext[tm:tm + CONV_HALO, :] = carry[...]
        carry[...] = dyext[0:CONV_HALO, :]
        for sec in range(3):
            c0 = sec * 1024
            dx = dyext[3:3 + tm, c0:c0 + 1024] * w_ref[0:1, c0:c0 + 1024]
            for j in range(1, CONV_WIDTH):
                dx = dx + dyext[3 - j:3 - j + tm, c0:c0 + 1024] * w_ref[j:j + 1, c0:c0 + 1024]
            dp_ref[:, c0:c0 + 1024] = dx.astype(BF16)
        dp_ref[:, 3072:4096] = dz_ref[...]
        lane = lax.broadcasted_iota(jnp.int32, (tm, LANES), 1)
        dg = jnp.zeros((tm, LANES), F32)
        dbeta = jnp.zeros((tm, LANES), F32)
        for h in range(GDN_HEADS):
            sl = slice(h * LANES, (h + 1) * LANES)
            dg = dg + jnp.where(lane == h, dg_ref[:, sl], 0.0)
            dbeta = dbeta + jnp.where(lane == GDN_HEADS + h, db_ref[:, sl], 0.0)
        ri = lax.broadcasted_iota(jnp.int32, (tm, tm), 0)
        ci = lax.broadcasted_iota(jnp.int32, (tm, tm), 1)
        upper = ((ri // CHUNK) == (ci // CHUNK)) & (ri <= ci)
        dg = _dot(upper.astype(F32), dg, NN, HI)
        pre = ab_ref[...] + dtb_ref[...]
        s = _sigmoid(pre)
        a_exp = jnp.exp(al_ref[...])
        dg_da = dg * (-a_exp * s)
        dp_ref[:, 4096:IN_C_PAD] = (dg_da + dbeta * s * (1.0 - s)).astype(BF16)
        dal_ref[...] += jnp.sum(dg * (-a_exp * _softplus(pre)), axis=0, keepdims=True)
        ddt_ref[...] += jnp.sum(dg_da, axis=0, keepdims=True)

    row = lambda w: pl.BlockSpec((tm, w), lambda s: (rev(s), 0))
    vec = lambda r, w: pl.BlockSpec((r, w), lambda s: (0, 0))
    return pl.pallas_call(
        body, name=name, grid=(nt,),
        in_specs=[row(3072), pl.BlockSpec((CONV_HALO, 3072), lambda s: (jnp.maximum(rev(s) * hb - 1, 0), 0)),
                  pl.BlockSpec((tm, LANES), lambda s: (rev(s), 32)), vec(CONV_WIDTH, 3072), vec(1, LANES), vec(1, LANES),
                  row(1024), row(1024), row(1024), row(1024), row(1024), row(1024)],
        out_specs=[row(IN_C_PAD), vec(CONV_WIDTH, 3072), vec(1, LANES), vec(1, LANES)],
        out_shape=[jax.ShapeDtypeStruct((t, IN_C_PAD), BF16), jax.ShapeDtypeStruct((CONV_WIDTH, 3072), F32),
                   jax.ShapeDtypeStruct((1, LANES), F32), jax.ShapeDtypeStruct((1, LANES), F32)],
        scratch_shapes=[pltpu.VMEM((tm + CONV_HALO, 3072), F32), pltpu.VMEM((tm + CONV_HALO, 3072), F32),
                        pltpu.VMEM((CONV_HALO, 3072), F32)],
        compiler_params=_params(("arbitrary",)),
    )(proj_c, proj_c, proj_c, conv_w, a_log, dt_bias, dq, dk, dv, dgb, dbb, dz)


def _o_gate_fwd(o, proj_c, o_norm, *, name):
    t = o.shape[0]
    tm = _tile(t, 2 * ROW_TILE)

    def body(o_ref, z_ref, g_ref, y_ref):
        for h in range(GDN_HEADS):
            sl = slice(h * LANES, (h + 1) * LANES)
            x = o_ref[:, sl]
            r = lax.rsqrt(jnp.mean(x * x, axis=-1, keepdims=True) + RMS_EPS)
            z = z_ref[:, sl]
            y_ref[:, sl] = (x * r * g_ref[...] * (z * _sigmoid(z))).astype(BF16)

    row = pl.BlockSpec((tm, 1024), lambda i: (i, 0))
    return pl.pallas_call(
        body, name=name, grid=(t // tm,),
        in_specs=[row, pl.BlockSpec((tm, 1024), lambda i: (i, 3)), pl.BlockSpec((1, LANES), lambda i: (0, 0))],
        out_specs=row, out_shape=jax.ShapeDtypeStruct((t, 1024), BF16), compiler_params=_params(("parallel",)),
    )(o, proj_c, o_norm)


def _o_gate_bwd(dy, o, proj_c, o_norm, *, name):
    t = o.shape[0]
    tm = _tile(t, 2 * ROW_TILE)

    def body(dy_ref, o_ref, z_ref, g_ref, do_ref, dz_ref, dg_ref):
        i = pl.program_id(0)

        @pl.when(i == 0)
        def _():
            dg_ref[...] = jnp.zeros_like(dg_ref)

        dg = jnp.zeros((1, LANES), F32)
        for h in range(GDN_HEADS):
            sl = slice(h * LANES, (h + 1) * LANES)
            x = o_ref[:, sl]
            r = lax.rsqrt(jnp.mean(x * x, axis=-1, keepdims=True) + RMS_EPS)
            xh = x * r
            z = z_ref[:, sl]
            sg = _sigmoid(z)
            dyv = dy_ref[:, sl]
            dn = dyv * (z * sg)
            dz_ref[:, sl] = (dyv * xh * g_ref[...] * (sg * (1.0 + z * (1.0 - sg)))).astype(BF16)
            dxh = dn * g_ref[...]
            do_ref[:, sl] = r * (dxh - xh * jnp.mean(dxh * xh, axis=-1, keepdims=True))
            dg = dg + jnp.sum(dn * xh, axis=0, keepdims=True)
        dg_ref[...] += dg

    row = pl.BlockSpec((tm, 1024), lambda i: (i, 0))
    vec = pl.BlockSpec((1, LANES), lambda i: (0, 0))
    return pl.pallas_call(
        body, name=name, grid=(t // tm,), in_specs=[row, row, pl.BlockSpec((tm, 1024), lambda i: (i, 3)), vec],
        out_specs=[row, row, vec],
        out_shape=[jax.ShapeDtypeStruct((t, 1024), F32), jax.ShapeDtypeStruct((t, 1024), BF16),
                   jax.ShapeDtypeStruct((1, LANES), F32)],
        compiler_params=_params(("arbitrary",)),
    )(dy, o, proj_c, o_norm)


PAIR = 2 * CHUNK
GDN_HP = 8


def _bdot(a, b, dims=NN):
    return _dot(a.astype(BF16), b.astype(BF16), dims)


def _each(f, *lists):
    return [f(*args) for args in zip(*lists)]


def _pair_common(q, k, v, gci, gcj, beta):
    ri = lax.broadcasted_iota(jnp.int32, (PAIR, PAIR), 0)
    ci = lax.broadcasted_iota(jnp.int32, (PAIR, PAIR), 1)
    same = (ri // CHUNK) == (ci // CHUNK)
    incl = same & (ri >= ci)
    strict = same & (ri > ci)
    eye = (ri == ci).astype(F32)
    first = lax.broadcasted_iota(jnp.int32, (PAIR, LANES), 0) < CHUNK
    gamma = _each(lambda gi, gj: jnp.where(incl, jnp.exp(jnp.minimum(gi - gj, 0.0)), 0.0), gci, gcj)
    kb = _each(jnp.multiply, k, beta)
    kk = _each(lambda a, b: _bdot(a, b, NT), kb, k)
    qk = _each(lambda a, b: _bdot(a, b, NT), q, k)
    m = _each(lambda x, g: jnp.where(strict, x * g, 0.0), kk, gamma)
    tm_ = _each(lambda x: eye - x, m)
    pw = _each(lambda x: _bdot(x, x), m)
    for it in range(5):
        tm_ = _each(lambda x, p: x + _bdot(x, p), tm_, pw)
        if it < 4:
            pw = _each(lambda p: _bdot(p, p), pw)
    eg = _each(jnp.exp, gci)
    vb = _each(jnp.multiply, v, beta)
    kbe = _each(jnp.multiply, kb, eg)
    uw = _each(lambda x, a, b: _bdot(x, jnp.concatenate([a, b], axis=1)), tm_, vb, kbe)
    attn = _each(lambda x, g: jnp.where(incl, x * g, 0.0), qk, gamma)
    gl_a = _each(lambda g: g[CHUNK - 1:CHUNK, :], gci)
    gl_b = _each(lambda g: g[PAIR - 1:PAIR, :], gci)
    ek = _each(lambda a, b, g: jnp.exp(jnp.where(first, a, b) - g), gl_a, gl_b, gci)
    return dict(incl=incl, strict=strict, gamma=gamma, kb=kb, m=m, tm=tm_, eg=eg, vb=vb, kbe=kbe,
                u=_each(lambda x: x[:, :LANES], uw), w=_each(lambda x: x[:, LANES:], uw), attn=attn,
                qd=_each(jnp.multiply, q, eg), ek=ek, kd=_each(jnp.multiply, k, ek),
                glast_a=_each(jnp.exp, gl_a), glast_b=_each(jnp.exp, gl_b))


def _gdn_specs(t, ts, order):
    nc = ts // CHUNK
    blk = pl.BlockSpec((ts, GDN_HP * LANES), lambda h, s: (order(s), h))
    row = pl.BlockSpec((GDN_HP, 1, ts), lambda h, s: (h, 0, order(s)))
    st = pl.BlockSpec((GDN_HP, nc, LANES, LANES), lambda h, s: (h, order(s), 0, 0))
    return blk, row, st


def _gdn_fwd(q, k, v, gcb, gct, bb, *, name):
    t = q.shape[0]
    ts = _tile(t, GDN_TILE)
    npair = ts // PAIR

    def body(q_ref, k_ref, v_ref, g_ref, gt_ref, b_ref, o_ref, st_ref, s_sc):
        @pl.when(pl.program_id(1) == 0)
        def _():
            s_sc[...] = jnp.zeros_like(s_sc)

        def pair(pi, _):
            rows = pl.ds(pl.multiple_of(pi * PAIR, PAIR), PAIR)
            heads = [slice(hh * LANES, (hh + 1) * LANES) for hh in range(GDN_HP)]
            c = CHUNK
            cat0 = lambda *xs: jnp.concatenate(xs, axis=0)
            s0 = [s_sc[hh] for hh in range(GDN_HP)]
            cm = _pair_common([q_ref[rows, sl] for sl in heads], [k_ref[rows, sl] for sl in heads],
                              [v_ref[rows, sl] for sl in heads], [g_ref[rows, sl] for sl in heads],
                              [gt_ref[hh, :, rows] for hh in range(GDN_HP)], [b_ref[rows, sl] for sl in heads])
            u, w, qd, kd = cm["u"], cm["w"], cm["qd"], cm["kd"]
            r0 = _each(lambda w_, q_, s: _bdot(cat0(w_[:c], q_[:c]), s), w, qd, s0)
            vn_a = _each(lambda u_, r: u_[:c] - r[:c], u, r0)
            s1 = _each(lambda s, gl, k_, vn: s * gl + _bdot(k_[:c], vn, TN), s0, cm["glast_a"], kd, vn_a)
            r1 = _each(lambda w_, q_, s: _bdot(cat0(w_[c:], q_[c:]), s), w, qd, s1)
            vn_b = _each(lambda u_, r: u_[c:] - r[:c], u, r1)
            s2 = _each(lambda s, gl, k_, vn: s * gl + _bdot(k_[c:], vn, TN), s1, cm["glast_b"], kd, vn_b)
            o = _each(lambda ra, rb, at, va, vb_: cat0(ra[c:], rb[c:]) + _bdot(at, cat0(va, vb_)),
                      r0, r1, cm["attn"], vn_a, vn_b)
            for hh, sl in enumerate(heads):
                st_ref[hh, 2 * pi] = s0[hh]
                st_ref[hh, 2 * pi + 1] = s1[hh]
                s_sc[hh] = s2[hh]
                o_ref[rows, sl] = o[hh]
            return 0

        lax.fori_loop(0, npair, pair, 0)

    blk, row, st = _gdn_specs(t, ts, lambda s: s)
    return pl.pallas_call(
        body, name=name, grid=(GDN_HEADS // GDN_HP, t // ts), in_specs=[blk, blk, blk, blk, row, blk],
        out_specs=[blk, st],
        out_shape=[jax.ShapeDtypeStruct((t, 1024), F32), jax.ShapeDtypeStruct((GDN_HEADS, t // CHUNK, LANES, LANES), F32)],
        scratch_shapes=[pltpu.VMEM((GDN_HP, LANES, LANES), F32)],
        compiler_params=_params(("parallel", "arbitrary")),
    )(q, k, v, gcb, gct, bb)


def _gdn_bwd(q, k, v, gcb, gct, bb, do, states, *, name):
    t = q.shape[0]
    ts = _tile(t, GDN_TILE)
    npair = ts // PAIR
    ns = t // ts
    c = CHUNK

    def body(q_ref, k_ref, v_ref, g_ref, gt_ref, b_ref, do_ref, st_ref, dq_ref, dk_ref, dv_ref, dg_ref, db_ref, ds_sc):
        @pl.when(pl.program_id(1) == 0)
        def _():
            ds_sc[...] = jnp.zeros_like(ds_sc)

        rowsum = lambda x: jnp.sum(x, axis=-1, keepdims=True)
        total = lambda x: jnp.sum(rowsum(x), axis=0, keepdims=True)
        cat0 = lambda *xs: jnp.concatenate(xs, axis=0)
        cat1 = lambda *xs: jnp.concatenate(xs, axis=1)

        def pair(step, _):
            pi = npair - 1 - step
            rows = pl.ds(pl.multiple_of(pi * PAIR, PAIR), PAIR)
            heads = [slice(hh * LANES, (hh + 1) * LANES) for hh in range(GDN_HP)]
            hs = range(GDN_HP)
            qv, kv, vv = ([r[rows, sl] for sl in heads] for r in (q_ref, k_ref, v_ref))
            beta = [b_ref[rows, sl] for sl in heads]
            dov = [do_ref[rows, sl] for sl in heads]
            s0 = [st_ref[hh, 2 * pi] for hh in hs]
            s1 = [st_ref[hh, 2 * pi + 1] for hh in hs]
            ds2 = [ds_sc[hh] for hh in hs]
            cm = _pair_common(qv, kv, vv, [g_ref[rows, sl] for sl in heads], [gt_ref[hh, :, rows] for hh in hs], beta)
            u, w, qd, kd, attn = cm["u"], cm["w"], cm["qd"], cm["kd"], cm["attn"]
            tmat, gamma, eg = cm["tm"], cm["gamma"], cm["eg"]
            incl, strict = cm["incl"], cm["strict"]
            vn_a = _each(lambda u_, w_, s: u_[:c] - _bdot(w_[:c], s), u, w, s0)
            vn_b = _each(lambda u_, w_, s: u_[c:] - _bdot(w_[c:], s), u, w, s1)
            vn = _each(cat0, vn_a, vn_b)
            dvn_att = _each(lambda a, d: _bdot(a, d, TN), attn, dov)
            dattn = _each(lambda d, v_: jnp.where(incl, _bdot(d, v_, NT), 0.0), dov, vn)
            dvn_b = _each(lambda x, k_, d: x[c:] + _bdot(k_[c:], d), dvn_att, kd, ds2)
            rb = _each(lambda d, x, s: _bdot(cat0(d[c:], x), s, NT), dov, dvn_b, s1)
            dkd_b = _each(lambda v_, d: _bdot(v_, d, NT), vn_b, ds2)
            dgl_b = _each(lambda d, s: total(d * s), ds2, s1)
            ds1 = _each(lambda d, gl, q_, w_, o_, x: d * gl + _bdot(cat0(q_[c:], w_[c:]), cat0(o_[c:], -x), TN),
                        ds2, cm["glast_b"], qd, w, dov, dvn_b)
            dvn_a = _each(lambda x, k_, d: x[:c] + _bdot(k_[:c], d), dvn_att, kd, ds1)
            ra = _each(lambda d, x, s: _bdot(cat0(d[:c], x), s, NT), dov, dvn_a, s0)
            dkd_a = _each(lambda v_, d: _bdot(v_, d, NT), vn_a, ds1)
            dgl_a = _each(lambda d, s: total(d * s), ds1, s0)
            ds0 = _each(lambda d, gl, q_, w_, o_, x: d * gl + _bdot(cat0(q_[:c], w_[:c]), cat0(o_[:c], -x), TN),
                        ds1, cm["glast_a"], qd, w, dov, dvn_a)
            dvn = _each(cat0, dvn_a, dvn_b)
            dqd = _each(lambda a, b: cat0(a[:c], b[:c]), ra, rb)
            dw = _each(lambda a, b: -cat0(a[c:], b[c:]), ra, rb)
            dkd = _each(cat0, dkd_a, dkd_b)
            dvw = _each(cat1, dvn, dw)
            dvbk = _each(lambda t_, x: _bdot(t_, x, TN), tmat, dvw)
            dvb = _each(lambda x: x[:, :LANES], dvbk)
            dkbe = _each(lambda x: x[:, LANES:], dvbk)
            dt_ = _each(lambda x, a, b: _bdot(x, cat1(a, b), NT), dvw, cm["vb"], cm["kbe"])
            da1 = _each(lambda t_, x: _bdot(t_, x, TN), tmat, dt_)
            dm = _each(lambda x, t_: jnp.where(strict, -_bdot(x, t_, NT), 0.0), da1, tmat)
            dkk = _each(jnp.multiply, dm, gamma)
            dqk = _each(jnp.multiply, dattn, gamma)
            z = _each(lambda a, b, c_, d: a * b + c_ * d, dm, cm["m"], dattn, attn)
            dkb = _each(lambda x, k_, y, e: _bdot(x, k_) + y * e, dkk, kv, dkbe, eg)
            dk = _each(lambda a, b, kb_, q_, x, e, y, be: _bdot(cat0(a, b), cat0(kb_, q_), TN) + x * e + y * be,
                       dkk, dqk, cm["kb"], qv, dkd, cm["ek"], dkb, beta)
            dq = _each(lambda x, k_, y, e: _bdot(x, k_) + y * e, dqk, kv, dqd, eg)

            def colsum_of(z_):
                zh = z_.astype(BF16)
                zl = (z_ - zh.astype(F32)).astype(BF16)
                return _dot(cat0(zh, zl), jnp.ones((2 * PAIR, LANES), BF16), TN)

            colsum = _each(colsum_of, z)
            ri = lax.broadcasted_iota(jnp.int32, (PAIR, LANES), 0)
            for hh, sl in enumerate(heads):
                dkd_kd = dkd[hh] * kd[hh]
                dgc = (rowsum(z[hh]) - colsum[hh] + rowsum(dqd[hh] * qd[hh]) - rowsum(dkd_kd)
                       + rowsum(dkbe[hh] * cm["kbe"][hh]))
                last_a = total(dkd_kd[:c]) + dgl_a[hh] * cm["glast_a"][hh]
                last_b = total(dkd_kd[c:]) + dgl_b[hh] * cm["glast_b"][hh]
                dgc = dgc + jnp.where(ri == c - 1, last_a, 0.0) + jnp.where(ri == PAIR - 1, last_b, 0.0)
                ds_sc[hh] = ds0[hh]
                dq_ref[rows, sl] = dq[hh]
                dk_ref[rows, sl] = dk[hh]
                dv_ref[rows, sl] = dvb[hh] * beta[hh]
                db_ref[rows, sl] = jnp.broadcast_to(rowsum(dkb[hh] * kv[hh]) + rowsum(dvb[hh] * vv[hh]), (PAIR, LANES))
                dg_ref[rows, sl] = dgc
            return 0

        lax.fori_loop(0, npair, pair, 0)

    blk, row, st = _gdn_specs(t, ts, lambda s: ns - 1 - s)
    out = jax.ShapeDtypeStruct((t, 1024), F32)
    return pl.pallas_call(
        body, name=name, grid=(GDN_HEADS // GDN_HP, ns), in_specs=[blk, blk, blk, blk, row, blk, blk, st],
        out_specs=[blk] * 5, out_shape=[out] * 5, scratch_shapes=[pltpu.VMEM((GDN_HP, LANES, LANES), F32)],
        compiler_params=_params(("parallel", "arbitrary")),
    )(q, k, v, gcb, gct, bb, do, states)


def _loss_head(h, g, target, *, name):
    t, d = h.shape
    tm = _tile(t, 2 * ROW_TILE)

    def body(h_ref, g_ref, t_ref, dh_ref, dhb_ref, dg_ref, loss_ref):
        i = pl.program_id(0)
        x = h_ref[...]
        r = lax.rsqrt(jnp.mean(x * x, axis=-1, keepdims=True) + RMS_EPS)
        xh = x * r
        err = xh * g_ref[...] - t_ref[...]
        dy = err * (1.0 / d)
        dxh = dy * g_ref[...]
        dh = r * (dxh - xh * jnp.mean(dxh * xh, axis=-1, keepdims=True))
        dh_ref[...] = dh
        dhb_ref[...] = dh.astype(BF16)

        @pl.when(i == 0)
        def _():
            dg_ref[...] = jnp.zeros_like(dg_ref)
            loss_ref[...] = jnp.zeros_like(loss_ref)

        dg_ref[...] += jnp.sum(dy * xh, axis=0, keepdims=True)
        part = 0.5 * jnp.sum(jnp.mean(err * err, axis=-1, keepdims=True), axis=0, keepdims=True)
        loss_ref[...] += jnp.broadcast_to(part, loss_ref.shape)

    row = pl.BlockSpec((tm, d), lambda i: (i, 0))
    vec = pl.BlockSpec((1, d), lambda i: (0, 0))
    return pl.pallas_call(
        body, name=name, grid=(t // tm,), in_specs=[row, vec, row],
        out_specs=[row, row, vec, pl.BlockSpec((8, LANES), lambda i: (0, 0))],
        out_shape=[jax.ShapeDtypeStruct((t, d), F32), jax.ShapeDtypeStruct((t, d), BF16),
                   jax.ShapeDtypeStruct((1, d), F32), jax.ShapeDtypeStruct((8, LANES), F32)],
        compiler_params=_params(("arbitrary",)),
    )(h, g, target)


def _pad_cols(w, n):
    return jnp.pad(w, ((0, 0), (0, n - w.shape[1])))


def _layout_weights(w):
    z = lambda r, c: jnp.zeros((r, c), F32)
    wi = w["w_in_ab"]
    win = jnp.concatenate([wi[:, :384], z(1024, 64), wi[:, 384:416], z(1024, 32), wi[:, 416:]], axis=1)
    wq = jnp.pad(w["w_q_b"].reshape(MLA_Q_RANK, MLA_HEADS, 96), ((0, 0), (0, 0), (0, 32))).reshape(MLA_Q_RANK, 1024)
    kv3 = w["w_kv_b"].reshape(MLA_KV_RANK, MLA_HEADS, 128)
    wk = jnp.pad(kv3[..., :MLA_NOPE], ((0, 0), (0, 0), (0, 64))).reshape(MLA_KV_RANK, 1024)
    wv = kv3[..., MLA_NOPE:].reshape(MLA_KV_RANK, 512)
    pw = w["pool_w"]
    rows = []
    for g in range(4):
        rows.append(jnp.concatenate([pw[g] if j == g else z(128, 128) for j in range(4)], axis=1))
    wpool = jnp.concatenate(rows, axis=0)
    half = MLA_ROPE // 2
    inv = 1.0 / (ROPE_THETA ** (jnp.arange(half, dtype=F32) / half))
    inv_lane = jnp.concatenate([jnp.zeros((MLA_NOPE,), F32), inv, inv, jnp.zeros((32,), F32)]).reshape(1, LANES)
    return dict(
        win=win.astype(BF16), wq=wq.astype(BF16), wk=wk.astype(BF16), wv=wv.astype(BF16), wpool=wpool.astype(BF16),
        wout_ab=w["w_out_ab"].astype(BF16), winc=_pad_cols(w["w_in_c"], IN_C_PAD).astype(BF16),
        wout_c=w["w_out_c"].astype(BF16), conv_w=w["conv_w"], a_log=_pad_cols(w["a_log"], LANES),
        dt_bias=_pad_cols(w["dt_bias"], LANES), inv_lane=inv_lane,
        norm_ab=w["norm_ab"], q_a_norm=w["q_a_norm"], kv_a_norm=w["kv_a_norm"], pool_scale=w["pool_scale"],
        norm_c=w["norm_c"], o_norm=w["o_norm"], final_norm=w["final_norm"],
    )


def _unlayout_grads(g):
    dwin = g["win"]
    dkv = jnp.concatenate([g["wk"].reshape(MLA_KV_RANK, MLA_HEADS, 128)[..., :MLA_NOPE],
                           g["wv"].reshape(MLA_KV_RANK, MLA_HEADS, MLA_V)], axis=-1).reshape(MLA_KV_RANK, 1024)
    return dict(
        norm_ab=g["norm_ab"],
        w_in_ab=jnp.concatenate([dwin[:, :384], dwin[:, 448:480], dwin[:, 512:]], axis=1),
        q_a_norm=g["q_a_norm"],
        w_q_b=g["wq"].reshape(MLA_Q_RANK, MLA_HEADS, 128)[..., :96].reshape(MLA_Q_RANK, 768),
        kv_a_norm=g["kv_a_norm"],
        w_kv_b=dkv,
        pool_w=jnp.stack([g["wpool"][i * 128:(i + 1) * 128, i * 128:(i + 1) * 128] for i in range(4)]),
        pool_scale=g["pool_scale"],
        w_out_ab=g["wout_ab"],
        norm_c=g["norm_c"],
        w_in_c=g["winc"][:, :4112],
        conv_w=g["conv_w"],
        a_log=g["a_log"][:, :GDN_HEADS],
        dt_bias=g["dt_bias"][:, :GDN_HEADS],
        o_norm=g["o_norm"],
        w_out_c=g["wout_c"],
        final_norm=g["final_norm"],
    )


def _local_step(x, pos, target, lw):
    mm = _matmul
    hn = _rms_fwd(x, lw["norm_ab"], name="rms_ab")
    proj = mm(hn, lw["win"], "nn", name="in_ab")
    qn, kvn, kr, d, cos_t, sin_t = _ab_prep(proj, pos, lw["inv_lane"], lw["q_a_norm"], lw["kv_a_norm"], name="ab_prep")
    qraw = mm(qn, lw["wq"], "nn", name="q_up")
    kvk = mm(kvn, lw["wk"], "nn", name="k_up")
    v = mm(kvn, lw["wv"], "nn", name="v_up", out_dtype=BF16)
    ybraw = mm(d, lw["wpool"], "nn", name="pool_mix")
    q, k = _qk_rope(qraw, kvk, kr, cos_t, sin_t, name="qk_rope")
    o, lse = _attn_fwd(q, k, v, name="attn_fwd")
    y = _gate_fwd(o, ybraw, proj, lw["pool_scale"], name="gate_ab")
    h1 = mm(y, lw["wout_ab"], "nn", name="out_ab", add=x)
    hn1 = _rms_fwd(h1, lw["norm_c"], name="rms_c")
    proj_c = mm(hn1, lw["winc"], "nn", name="in_c")
    q2, k2, v2, gb, bb, gt = _c_prep(proj_c, lw["conv_w"], lw["a_log"], lw["dt_bias"], name="c_prep")
    gt = gt.reshape(GDN_HEADS, 1, gt.shape[1])
    o2, states = _gdn_fwd(q2, k2, v2, gb, gt, bb, name="gdn_fwd")
    y2 = _o_gate_fwd(o2, proj_c, lw["o_norm"], name="gate_c")
    h2 = mm(y2, lw["wout_c"], "nn", name="out_c", add=h1)
    dh2, dh2b, d_final, loss = _loss_head(h2, lw["final_norm"], target, name="loss_head")
    g = {"final_norm": d_final}
    dy2 = mm(dh2b, lw["wout_c"], "nt", name="out_c_dx")
    g["wout_c"] = mm(y2, dh2b, "tn", name="out_c_dw")
    do2, dz2, g["o_norm"] = _o_gate_bwd(dy2, o2, proj_c, lw["o_norm"], name="gate_c_bwd")
    dq2, dk2, dv2, dgb, dbb = _gdn_bwd(q2, k2, v2, gb, gt, bb, do2, states, name="gdn_bwd")
    dproj_c, g["conv_w"], g["a_log"], g["dt_bias"] = _c_prep_bwd(
        proj_c, lw["conv_w"], lw["a_log"], lw["dt_bias"], dq2, dk2, dv2, dgb, dbb, dz2, name="c_prep_bwd")
    dhn1 = mm(dproj_c, lw["winc"], "nt", name="in_c_dx")
    g["winc"] = mm(hn1, dproj_c, "tn", name="in_c_dw")
    dh1, dh1b, g["norm_c"] = _rms_bwd(h1, lw["norm_c"], dhn1, dh2, name="rms_c_bwd", with_bf16=True)
    dy = mm(dh1b, lw["wout_ab"], "nt", name="out_ab_dx")
    g["wout_ab"] = mm(y, dh1b, "tn", name="out_ab_dw")
    do, delta, dyb, dz, g["pool_scale"] = _gate_bwd(dy, o, ybraw, proj, lw["pool_scale"], name="gate_ab_bwd")
    dq, dk, dv = _attn_bwd(q, k, v, do, lse, delta, name="attn_bwd")
    dd = mm(dyb, lw["wpool"], "nt", name="pool_mix_dx")
    g["wpool"] = mm(d, dyb, "tn", name="pool_mix_dw")
    dqraw, dkb, dkr = _qk_rope_bwd(dq, dk, cos_t, sin_t, name="qk_rope_bwd")
    dqn = mm(dqraw, lw["wq"], "nt", name="q_up_dx")
    g["wq"] = mm(qn, dqraw, "tn", name="q_up_dw")
    dkvn_k = mm(dkb, lw["wk"], "nt", name="k_up_dx")
    dkvn_v = mm(dv, lw["wv"], "nt", name="v_up_dx")
    g["wk"] = mm(kvn, dkb, "tn", name="k_up_dw")
    g["wv"] = mm(kvn, dv, "tn", name="v_up_dw")
    dproj, g["q_a_norm"], g["kv_a_norm"] = _ab_prep_bwd(
        proj, lw["q_a_norm"], lw["kv_a_norm"], dqn, dkvn_k, dkvn_v, dkr, dd, dz, name="ab_prep_bwd")
    dhn = mm(dproj, lw["win"], "nt", name="in_ab_dx")
    g["win"] = mm(hn, dproj, "tn", name="in_ab_dw")
    dx, g["norm_ab"] = _rms_bwd(x, lw["norm_ab"], dhn, dh1, name="rms_ab_bwd", with_bf16=False)
    return loss, dx, g


_HBM = pl.BlockSpec(memory_space=pltpu.HBM)


def _place():
    return lax.axis_index("x"), lax.axis_index("y"), lax.axis_index("c")


def _flip(v, f):
    return 1 - v if f else v


_CHIP_FLIPS = ((1, 0), (0, 1), (1, 1))
_DEV_FLIPS = tuple((fx, fy, fc) for fx in (0, 1) for fy in (0, 1) for fc in (0, 1) if fx or fy or fc)


def _rcopy(src, dst, send_sems, recv_sems, k, to):
    return pltpu.make_async_remote_copy(src_ref=src, dst_ref=dst, send_sem=send_sems.at[k], recv_sem=recv_sems.at[k],
                                        device_id=to, device_id_type=MESH)


def _gather_weights(wb, ws):
    _, rh, _ = wb.shape
    rs = ws.shape[0]

    def body(wb_ref, ws_ref, gb_ref, gs_ref, send_sems, recv_sems, local_sems):
        x, y, c = _place()
        j0 = 2 * x + y
        sib = (x, y, 1 - c)
        chips = [(_flip(x, fx), _flip(y, fy)) for fx, fy in _CHIP_FLIPS]
        own_b = pltpu.make_async_copy(wb_ref, gb_ref.at[j0], local_sems.at[0])
        own_s = pltpu.make_async_copy(ws_ref, gs_ref.at[j0], local_sems.at[1])
        own_b.start()
        own_s.start()
        sends = []
        for k, (px, py) in enumerate(chips):
            sends.append(_rcopy(wb_ref.at[c], gb_ref.at[j0, c], send_sems, recv_sems, k, (px, py, c)))
            sends.append(_rcopy(ws_ref, gs_ref.at[j0], send_sems, recv_sems, 6 + k, (px, py, c)))
        for cp in sends:
            cp.start()
        for k, (px, py) in enumerate(chips):
            jk = 2 * px + py
            _rcopy(wb_ref.at[c], gb_ref.at[jk, c], send_sems, recv_sems, k, (px, py, c)).wait_recv()
            fwd = _rcopy(gb_ref.at[jk, c], gb_ref.at[jk, c], send_sems, recv_sems, 3 + k, sib)
            fwd.start()
            sends.append(fwd)
        for k, (px, py) in enumerate(chips):
            jk = 2 * px + py
            _rcopy(wb_ref.at[c], gb_ref.at[jk, 1 - c], send_sems, recv_sems, 3 + k, sib).wait_recv()
            _rcopy(ws_ref, gs_ref.at[jk], send_sems, recv_sems, 6 + k, (px, py, c)).wait_recv()
        for cp in sends:
            cp.wait_send()
        own_b.wait()
        own_s.wait()

    return pl.pallas_call(
        body, name="gather_weights", in_specs=[_HBM, _HBM], out_specs=[_HBM, _HBM],
        out_shape=[jax.ShapeDtypeStruct((4, 2, rh, LANES), BF16), jax.ShapeDtypeStruct((4, rs, LANES), F32)],
        scratch_shapes=[pltpu.SemaphoreType.DMA((9,)), pltpu.SemaphoreType.DMA((9,)), pltpu.SemaphoreType.DMA((2,))],
    )(wb, ws)


def _sibling_swap(a, *, name):
    def body(a_ref, o_ref, send_sem, recv_sem):
        x, y, c = _place()
        cp = pltpu.make_async_remote_copy(src_ref=a_ref, dst_ref=o_ref, send_sem=send_sem, recv_sem=recv_sem,
                                          device_id=(x, y, 1 - c), device_id_type=MESH)
        cp.start()
        cp.wait()

    return pl.pallas_call(
        body, name=name, in_specs=[_HBM], out_specs=_HBM, out_shape=jax.ShapeDtypeStruct(a.shape, a.dtype),
        scratch_shapes=[pltpu.SemaphoreType.DMA, pltpu.SemaphoreType.DMA],
    )(a)


def _chip_exchange(p, small):
    _, rh, _ = p.shape
    rs = small.shape[0]

    def body(p_ref, s_ref, l_ref, ls_ref, send_sems, recv_sems, local_sems):
        x, y, c = _place()
        j0 = 2 * x + y
        d0 = 2 * j0 + c
        own_p = pltpu.make_async_copy(p_ref.at[j0], l_ref.at[j0], local_sems.at[0])
        own_s = pltpu.make_async_copy(s_ref, ls_ref.at[d0], local_sems.at[1])
        own_p.start()
        own_s.start()
        sends = []
        for k, (fx, fy) in enumerate(_CHIP_FLIPS):
            px, py = _flip(x, fx), _flip(y, fy)
            sends.append(_rcopy(p_ref.at[2 * px + py], l_ref.at[j0], send_sems, recv_sems, k, (px, py, c)))
        for k, (fx, fy, fc) in enumerate(_DEV_FLIPS):
            peer = (_flip(x, fx), _flip(y, fy), _flip(c, fc))
            sends.append(_rcopy(s_ref, ls_ref.at[d0], send_sems, recv_sems, 3 + k, peer))
        for cp in sends:
            cp.start()
        for k, (fx, fy) in enumerate(_CHIP_FLIPS):
            px, py = _flip(x, fx), _flip(y, fy)
            _rcopy(p_ref.at[j0], l_ref.at[2 * px + py], send_sems, recv_sems, k, (px, py, c)).wait_recv()
        for k, (fx, fy, fc) in enumerate(_DEV_FLIPS):
            px, py, pc = _flip(x, fx), _flip(y, fy), _flip(c, fc)
            _rcopy(s_ref, ls_ref.at[4 * px + 2 * py + pc], send_sems, recv_sems, 3 + k, (px, py, pc)).wait_recv()
        for cp in sends:
            cp.wait_send()
        own_p.wait()
        own_s.wait()

    return pl.pallas_call(
        body, name="chip_exchange", in_specs=[_HBM, _HBM], out_specs=[_HBM, _HBM],
        out_shape=[jax.ShapeDtypeStruct((4, rh, LANES), F32), jax.ShapeDtypeStruct((8, rs, LANES), F32)],
        scratch_shapes=[pltpu.SemaphoreType.DMA((10,)), pltpu.SemaphoreType.DMA((10,)), pltpu.SemaphoreType.DMA((2,))],
    )(p, small)


def _sum_slots(a, *, name):
    n, rows, _ = a.shape
    tr = _tile(rows, 1024)

    def body(a_ref, o_ref):
        acc = a_ref[0]
        for s in range(1, n):
            acc = acc + a_ref[s]
        o_ref[...] = acc

    return pl.pallas_call(
        body, name=name, grid=(rows // tr,), in_specs=[pl.BlockSpec((n, tr, LANES), lambda i: (0, i, 0))],
        out_specs=pl.BlockSpec((tr, LANES), lambda i: (i, 0)), out_shape=jax.ShapeDtypeStruct((rows, LANES), F32),
        compiler_params=_params(("parallel",)),
    )(a)


def _adamw(g, w, m, v, *, name):
    rows = g.shape[0]
    tr = _tile(rows, 1024)
    c1 = 1.0 - ADAM_B1 ** ADAM_STEP
    c2 = 1.0 - ADAM_B2 ** ADAM_STEP

    def body(g_ref, w_ref, m_ref, v_ref, d_ref, mo_ref, vo_ref):
        gv = g_ref[...]
        mn = ADAM_B1 * m_ref[...] + (1.0 - ADAM_B1) * gv
        vn = ADAM_B2 * v_ref[...] + (1.0 - ADAM_B2) * (gv * gv)
        mo_ref[...] = mn
        vo_ref[...] = vn
        d_ref[...] = -ADAM_LR * ((mn / c1) / (jnp.sqrt(vn / c2) + ADAM_EPS) + ADAM_WD * w_ref[...])

    blk = pl.BlockSpec((tr, LANES), lambda i: (i, 0))
    out = jax.ShapeDtypeStruct((rows, LANES), F32)
    return pl.pallas_call(
        body, name=name, grid=(rows // tr,), in_specs=[blk] * 4, out_specs=[blk] * 3, out_shape=[out] * 3,
        compiler_params=_params(("parallel",)),
    )(g, w, m, v)


_SHARDED = (("w_in_ab", (1024, 488), 1), ("w_q_b", (256, 192), 1), ("w_kv_b", (128, 256), 1),
            ("w_out_ab", (256, 1024), 0), ("w_in_c", (1024, 1028), 1), ("w_out_c", (256, 1024), 0),
            ("conv_w", (4, 768), 1), ("norm_c", (1, 256), 1))
_N_BF16 = 6
_REPLICATED = (("norm_ab", (1, 1024)), ("q_a_norm", (1, 256)), ("kv_a_norm", (1, 128)), ("pool_w", (4, 128, 128)),
               ("pool_scale", (1, 512)), ("a_log", (1, 8)), ("dt_bias", (1, 8)), ("o_norm", (1, 128)),
               ("final_norm", (1, 1024)))
_ALL_NAMES = ("norm_ab", "w_in_ab", "q_a_norm", "w_q_b", "kv_a_norm", "w_kv_b", "pool_w", "pool_scale", "w_out_ab",
              "norm_c", "w_in_c", "conv_w", "a_log", "dt_bias", "o_norm", "w_out_c", "final_norm")


def _rows_of(shape):
    return max(1, math.prod(shape) // LANES)


def _as_rows(a):
    n = a.size
    if n % LANES:
        a = jnp.pad(a.reshape(1, n), ((0, 0), (0, LANES - n % LANES)))
    return a.reshape(-1, LANES)


def _pack(parts, total_rows):
    rows = jnp.concatenate([_as_rows(p) for p in parts], axis=0)
    return jnp.pad(rows, ((0, total_rows - rows.shape[0]), (0, 0)))


def _unpack(packed, spec):
    out, off = [], 0
    lead = packed.shape[:-2]
    for shape in spec:
        r = _rows_of(shape)
        blk = packed[..., off:off + r, :].reshape(lead + (r * LANES,))[..., :math.prod(shape)]
        out.append(blk.reshape(lead + tuple(shape)))
        off += r
    return out


def _round_up(n, m):
    return -(-n // m) * m


_SH_SHAPES = tuple(s for _, s, _ in _SHARDED)
_RB = sum(_rows_of(s) for s in _SH_SHAPES[:_N_BF16])
_RS = _round_up(sum(_rows_of(s) for s in _SH_SHAPES[_N_BF16:]), 8)
_RG = _round_up(sum(_rows_of(s) for s in _SH_SHAPES), 16)
_REP_SHAPES = tuple(s for _, s in _REPLICATED)
_RR = _round_up(sum(_rows_of(s) for s in _REP_SHAPES) + 1, 8)


def kernel(x, positions, norm_ab, w_in_ab, q_a_norm, w_q_b, kv_a_norm, w_kv_b, pool_w, pool_scale, w_out_ab, norm_c, w_in_c, conv_w, a_log, dt_bias, o_norm, w_out_c, final_norm, loss_target, m_norm_ab, m_w_in_ab, m_q_a_norm, m_w_q_b, m_kv_a_norm, m_w_kv_b, m_pool_w, m_pool_scale, m_w_out_ab, m_norm_c, m_w_in_c, m_conv_w, m_a_log, m_dt_bias, m_o_norm, m_w_out_c, m_final_norm, v_norm_ab, v_w_in_ab, v_q_a_norm, v_w_q_b, v_kv_a_norm, v_w_kv_b, v_pool_w, v_pool_scale, v_w_out_ab, v_norm_c, v_w_in_c, v_conv_w, v_a_log, v_dt_bias, v_o_norm, v_w_out_c, v_final_norm):
    given = dict(locals())
    c = lax.axis_index("c")
    t = x.shape[1]

    def shard_of(prefix, name):
        a = given[prefix + name]
        return a.reshape(a.shape[1:]) if a.ndim > 2 else a.reshape(1, -1)

    sh = [shard_of("", n) for n, _, _ in _SHARDED]
    wb = _pack([a.astype(BF16) for a in sh[:_N_BF16]], _RB).reshape(2, _RB // 2, LANES)
    ws = _pack(sh[_N_BF16:], _RS)
    gb, gs = _gather_weights(wb, ws)
    parts = _unpack(gb.reshape(4, _RB, LANES), _SH_SHAPES[:_N_BF16]) + _unpack(gs, _SH_SHAPES[_N_BF16:])
    full = {}
    for (name, _, axis), p in zip(_SHARDED, parts):
        full[name] = jnp.concatenate([p[j] for j in range(4)], axis=axis)
    for name, _ in _REPLICATED:
        full[name] = shard_of("", name)
    lw = _layout_weights(full)

    loss_tile, dx, g = _local_step(x[0], positions.reshape(t, 1), loss_target[0], lw)
    grads = _unlayout_grads(g)

    per_chip = []
    for j in range(4):
        pieces = []
        for name, shape, axis in _SHARDED:
            n = shape[axis]
            pieces.append(lax.slice_in_dim(grads[name], j * n, (j + 1) * n, axis=axis))
        per_chip.append(_pack(pieces, _RG))
    gfull = jnp.stack(per_chip).reshape(4, 2, _RG // 2, LANES)
    mine = lax.dynamic_index_in_dim(gfull, c, axis=1, keepdims=False)
    other = lax.dynamic_index_in_dim(gfull, 1 - c, axis=1, keepdims=False)
    from_sibling = _sibling_swap(other, name="core_swap_partial")
    pair = jnp.stack([mine, from_sibling]).reshape(2, 4 * (_RG // 2), LANES)
    chip_sum = _sum_slots(pair, name="core_sum").reshape(4, _RG // 2, LANES)
    small = _pack([grads[n] for n, _ in _REPLICATED] + [loss_tile[0:1, :]], _RR)
    landed, small_all = _chip_exchange(chip_sum, small)
    my_half = _sum_slots(landed, name="chip_sum")
    small_sum = _sum_slots(small_all, name="small_sum")
    sib_half = _sibling_swap(my_half, name="core_swap_sum")
    g_shard = jnp.where(c == 0, jnp.concatenate([my_half, sib_half]), jnp.concatenate([sib_half, my_half]))

    w_sh = _pack(sh, _RG)
    m_sh = _pack([shard_of("m_", n) for n, _, _ in _SHARDED], _RG)
    v_sh = _pack([shard_of("v_", n) for n, _, _ in _SHARDED], _RG)
    d_sh, mo_sh, vo_sh = _adamw(g_shard, w_sh, m_sh, v_sh, name="adamw_sharded")
    zero_row = jnp.zeros((1, LANES), F32)
    w_rp = _pack([shard_of("", n) for n, _ in _REPLICATED] + [zero_row], _RR)
    m_rp = _pack([shard_of("m_", n) for n, _ in _REPLICATED] + [zero_row], _RR)
    v_rp = _pack([shard_of("v_", n) for n, _ in _REPLICATED] + [zero_row + 1.0], _RR)
    d_rp, mo_rp, vo_rp = _adamw(small_sum, w_rp, m_rp, v_rp, name="adamw_replicated")

    res = {}
    for key, sh_pack, rp_pack in (("grad", g_shard, small_sum), ("delta", d_sh, d_rp), ("m", mo_sh, mo_rp), ("v", vo_sh, vo_rp)):
        for (name, _, _), a in zip(_SHARDED, _unpack(sh_pack, _SH_SHAPES)):
            res[key, name] = a.reshape(given[name].shape)
        for (name, _), a in zip(_REPLICATED, _unpack(rp_pack, _REP_SHAPES)):
            res[key, name] = a.reshape(given[name].shape)
    loss = small_sum[sum(_rows_of(s) for s in _REP_SHAPES), 0]
    outs = [loss, dx.reshape(x.shape)]
    for key in ("grad", "delta", "m", "v"):
        outs += [res[key, n] for n in _ALL_NAMES]
    return tuple(outs)
```

```python
import functools
import math

import jax
import jax.numpy as jnp
from jax import lax
from jax.experimental import pallas as pl
from jax.experimental.pallas import tpu as pltpu

F32 = jnp.float32
BF16 = jnp.bfloat16
HI = lax.Precision.HIGHEST
MESH = pl.DeviceIdType.MESH

RMS_EPS = 1e-6
D_MODEL = 1024
MLA_HEADS = 8
MLA_Q_RANK = 256
MLA_KV_RANK = 128
MLA_NOPE = 64
MLA_ROPE = 32
MLA_V = 64
ROPE_THETA = 10000.0
POOL_WINDOWS = (2, 4, 8, 16)
POOL_GROUP = 128
POOL_WIDTH = 512
POOL_HALO = 16
GDN_HEADS = 8
GDN_DK = 128
CONV_WIDTH = 4
CONV_HALO = 8
CHUNK = 64
IN_AB_PAD = 2048
IN_C_PAD = 4224
ATT_SCALE = (MLA_NOPE + MLA_ROPE) ** -0.5

ADAM_LR = 0.001
ADAM_B1 = 0.9
ADAM_B2 = 0.999
ADAM_EPS = 1e-08
ADAM_WD = 0.01
ADAM_STEP = 10

LANES = 128
VMEM_LIMIT = 56 * 1024 * 1024

ROW_TILE = 256
ATT_TILE = 512
GDN_TILE = 256
MM_TILE = (1024, 1408, 2048)

NN = (((1,), (0,)), ((), ()))
NT = (((1,), (1,)), ((), ()))
TN = (((0,), (0,)), ((), ()))


def _dot(a, b, dims=NN, prec=None):
    return lax.dot_general(a, b, dims, precision=prec, preferred_element_type=F32)


def _tile(n, pref):
    if n <= pref:
        return n
    step = LANES if pref >= LANES else 8
    for t in range(pref - pref % step, 0, -step):
        if n % t == 0:
            return t
    return n


def _params(sem):
    return pltpu.CompilerParams(dimension_semantics=sem, vmem_limit_bytes=VMEM_LIMIT)


def _sigmoid(x):
    return 1.0 / (1.0 + jnp.exp(-x))


def _softplus(x):
    return jnp.maximum(x, 0.0) + jnp.log(1.0 + jnp.exp(-jnp.abs(x)))


def _matmul(a, b, mode, *, name, out_dtype=F32, add=None):
    if mode == "nn":
        (m, k), (k2, n) = a.shape, b.shape
    elif mode == "nt":
        (m, k), (n, k2) = a.shape, b.shape
    else:
        (k, m), (k2, n) = a.shape, b.shape
    assert k == k2, (a.shape, b.shape, mode)
    tm, tn, tk = _tile(m, MM_TILE[0]), _tile(n, MM_TILE[1]), _tile(k, MM_TILE[2])
    nk = k // tk
    if mode == "tn":
        a_spec = pl.BlockSpec((tk, tm), lambda i, j, kk: (kk, i))
    else:
        a_spec = pl.BlockSpec((tm, tk), lambda i, j, kk: (i, kk))
    if mode == "nt":
        b_spec = pl.BlockSpec((tn, tk), lambda i, j, kk: (j, kk))
    else:
        b_spec = pl.BlockSpec((tk, tn), lambda i, j, kk: (kk, j))
    o_spec = pl.BlockSpec((tm, tn), lambda i, j, kk: (i, j))
    dims = {"nn": NN, "nt": NT, "tn": TN}[mode]
    has_add = add is not None

    def body(*refs):
        a_ref, b_ref = refs[0], refs[1]
        add_ref = refs[2] if has_add else None
        o_ref = refs[3] if has_add else refs[2]

        def finish(o):
            if has_add:
                o = o + add_ref[...]
            o_ref[...] = o.astype(out_dtype)

        if nk == 1:
            finish(_dot(a_ref[...], b_ref[...], dims))
            return
        acc = refs[-1]
        kk = pl.program_id(2)

        @pl.when(kk == 0)
        def _():
            acc[...] = jnp.zeros_like(acc)

        acc[...] += _dot(a_ref[...], b_ref[...], dims)

        @pl.when(kk == nk - 1)
        def _():
            finish(acc[...])

    in_specs = [a_spec, b_spec] + ([o_spec] if has_add else [])
    args = (a, b) + ((add,) if has_add else ())
    return pl.pallas_call(
        body, name=name, grid=(m // tm, n // tn, nk), in_specs=in_specs, out_specs=o_spec,
        out_shape=jax.ShapeDtypeStruct((m, n), out_dtype),
        scratch_shapes=[pltpu.VMEM((tm, tn), F32)] if nk > 1 else [],
        compiler_params=_params(("parallel", "parallel", "arbitrary")),
    )(*args)


def _rms_fwd(h, g, *, name):
    t, d = h.shape
    tm = _tile(t, 2 * ROW_TILE)

    def body(h_ref, g_ref, o_ref):
        x = h_ref[...]
        r = lax.rsqrt(jnp.mean(x * x, axis=-1, keepdims=True) + RMS_EPS)
        o_ref[...] = (x * r * g_ref[...]).astype(BF16)

    return pl.pallas_call(
        body, name=name, grid=(t // tm,),
        in_specs=[pl.BlockSpec((tm, d), lambda i: (i, 0)), pl.BlockSpec((1, d), lambda i: (0, 0))],
        out_specs=pl.BlockSpec((tm, d), lambda i: (i, 0)),
        out_shape=jax.ShapeDtypeStruct((t, d), BF16), compiler_params=_params(("parallel",)),
    )(h, g)


def _rms_bwd(h, g, dy, dres, *, name, with_bf16):
    t, d = h.shape
    tm = _tile(t, 2 * ROW_TILE)

    def body(h_ref, g_ref, dy_ref, dres_ref, *outs):
        i = pl.program_id(0)
        dh_ref, dg_ref = outs[0], outs[-1]
        x = h_ref[...]
        r = lax.rsqrt(jnp.mean(x * x, axis=-1, keepdims=True) + RMS_EPS)
        xh = x * r
        dyv = dy_ref[...].astype(F32)
        dxh = dyv * g_ref[...]
        dx = r * (dxh - xh * jnp.mean(dxh * xh, axis=-1, keepdims=True))
        dh = dres_ref[...] + dx
        dh_ref[...] = dh
        if with_bf16:
            outs[1][...] = dh.astype(BF16)

        @pl.when(i == 0)
        def _():
            dg_ref[...] = jnp.zeros_like(dg_ref)

        dg_ref[...] += jnp.sum(dyv * xh, axis=0, keepdims=True)

    row = pl.BlockSpec((tm, d), lambda i: (i, 0))
    vec = pl.BlockSpec((1, d), lambda i: (0, 0))
    out_shape = [jax.ShapeDtypeStruct((t, d), F32)]
    out_specs = [row]
    if with_bf16:
        out_shape.append(jax.ShapeDtypeStruct((t, d), BF16))
        out_specs.append(row)
    out_shape.append(jax.ShapeDtypeStruct((1, d), F32))
    out_specs.append(vec)
    return pl.pallas_call(
        body, name=name, grid=(t // tm,), in_specs=[row, vec, row, row], out_specs=out_specs,
        out_shape=out_shape, compiler_params=_params(("arbitrary",)),
    )(h, g, dy, dres)


def _rope_partner(x):
    lane = lax.broadcasted_iota(jnp.int32, x.shape, 1)
    swapped = jnp.where(lane < MLA_NOPE + MLA_ROPE // 2, pltpu.roll(x, LANES - 16, 1), pltpu.roll(x, 16, 1))
    return jnp.where((lane >= MLA_NOPE) & (lane < MLA_NOPE + MLA_ROPE), swapped, 0.0)


def _pool_counts(row0, tm, w):
    t_idx = row0 + lax.broadcasted_iota(jnp.int32, (tm, POOL_GROUP), 0)
    return jnp.minimum(t_idx + 1, w).astype(F32)


def _ab_prep(proj, pos, inv_freq, q_a_norm, kv_a_norm, *, name):
    t = proj.shape[0]
    tm = _tile(t, ROW_TILE)
    hb = tm // POOL_HALO

    def body(p_ref, halo_ref, pos_ref, inv_ref, qg_ref, kg_ref, qn_ref, kvn_ref, kr_ref, d_ref, cos_ref, sin_ref, ext):
        i = pl.program_id(0)
        ql = p_ref[:, 0:MLA_Q_RANK]
        r = lax.rsqrt(jnp.mean(ql * ql, axis=-1, keepdims=True) + RMS_EPS)
        qn_ref[...] = (ql * r * qg_ref[...]).astype(BF16)
        kl = p_ref[:, MLA_Q_RANK:MLA_Q_RANK + MLA_KV_RANK]
        r = lax.rsqrt(jnp.mean(kl * kl, axis=-1, keepdims=True) + RMS_EPS)
        kvn_ref[...] = (kl * r * kg_ref[...]).astype(BF16)
        ang = pos_ref[...].astype(F32) * inv_ref[...]
        lane = lax.broadcasted_iota(jnp.int32, (tm, LANES), 1)
        in_rope = (lane >= MLA_NOPE) & (lane < MLA_NOPE + MLA_ROPE)
        cos_t = jnp.where(in_rope, jnp.cos(ang), 1.0)
        sin_t = jnp.where(in_rope, jnp.sin(ang), 0.0)
        sin_t = jnp.where(lane < MLA_NOPE + MLA_ROPE // 2, -sin_t, sin_t)
        cos_ref[...] = cos_t
        sin_ref[...] = sin_t
        kr = p_ref[:, 384:512]
        kr_ref[...] = kr * cos_t + _rope_partner(kr) * sin_t
        xp = p_ref[:, 512:1024]
        ext[0:POOL_HALO, :] = jnp.where(i > 0, halo_ref[...], 0.0)
        ext[POOL_HALO:POOL_HALO + tm, :] = xp
        for g, w in enumerate(POOL_WINDOWS):
            lo = g * POOL_GROUP
            acc = ext[POOL_HALO:POOL_HALO + tm, lo:lo + POOL_GROUP]
            for s in range(1, w):
                acc = acc + ext[POOL_HALO - s:POOL_HALO - s + tm, lo:lo + POOL_GROUP]
            cnt = _pool_counts(i * tm, tm, w)
            d_ref[:, lo:lo + POOL_GROUP] = (acc / cnt - xp[:, lo:lo + POOL_GROUP]).astype(BF16)

    row = lambda w: pl.BlockSpec((tm, w), lambda i: (i, 0))
    vec = lambda w: pl.BlockSpec((1, w), lambda i: (0, 0))
    return pl.pallas_call(
        body, name=name, grid=(t // tm,),
        in_specs=[row(1024), pl.BlockSpec((POOL_HALO, POOL_WIDTH), lambda i: (jnp.maximum(i * hb - 1, 0), 1)),
                  pl.BlockSpec((tm, 1), lambda i: (i, 0)), vec(LANES), vec(MLA_Q_RANK), vec(MLA_KV_RANK)],
        out_specs=[row(MLA_Q_RANK), row(MLA_KV_RANK), row(LANES), row(POOL_WIDTH), row(LANES), row(LANES)],
        out_shape=[jax.ShapeDtypeStruct((t, MLA_Q_RANK), BF16), jax.ShapeDtypeStruct((t, MLA_KV_RANK), BF16),
                   jax.ShapeDtypeStruct((t, LANES), F32), jax.ShapeDtypeStruct((t, POOL_WIDTH), BF16),
                   jax.ShapeDtypeStruct((t, LANES), F32), jax.ShapeDtypeStruct((t, LANES), F32)],
        scratch_shapes=[pltpu.VMEM((tm + POOL_HALO, POOL_WIDTH), F32)],
        compiler_params=_params(("parallel",)),
    )(proj, proj, pos, inv_freq, q_a_norm, kv_a_norm)


def _qk_rope(qraw, kvk, kr, cos_t, sin_t, *, name):
    t = qraw.shape[0]
    tm = _tile(t, 2 * ROW_TILE)

    def body(q_ref, k_ref, kr_ref, c_ref, s_ref, qo_ref, ko_ref):
        c, s, krv = c_ref[...], s_ref[...], kr_ref[...]
        for h in range(MLA_HEADS):
            sl = slice(h * LANES, (h + 1) * LANES)
            q = q_ref[:, sl]
            qo_ref[:, sl] = ((q * c + _rope_partner(q) * s) * ATT_SCALE).astype(BF16)
            ko_ref[:, sl] = (k_ref[:, sl] + krv).astype(BF16)

    row = lambda w: pl.BlockSpec((tm, w), lambda i: (i, 0))
    return pl.pallas_call(
        body, name=name, grid=(t // tm,), in_specs=[row(1024), row(1024), row(LANES), row(LANES), row(LANES)],
        out_specs=[row(1024), row(1024)],
        out_shape=[jax.ShapeDtypeStruct((t, 1024), BF16), jax.ShapeDtypeStruct((t, 1024), BF16)],
        compiler_params=_params(("parallel",)),
    )(qraw, kvk, kr, cos_t, sin_t)


def _qk_rope_bwd(dq, dk, cos_t, sin_t, *, name):
    t = dq.shape[0]
    tm = _tile(t, 2 * ROW_TILE)

    def body(dq_ref, dk_ref, c_ref, s_ref, dqo_ref, dko_ref, dkr_ref):
        c, s = c_ref[...], s_ref[...]
        lane = lax.broadcasted_iota(jnp.int32, (tm, LANES), 1)
        in_rope = (lane >= MLA_NOPE) & (lane < MLA_NOPE + MLA_ROPE)
        dkr = jnp.zeros((tm, LANES), F32)
        for h in range(MLA_HEADS):
            sl = slice(h * LANES, (h + 1) * LANES)
            g = dq_ref[:, sl]
            dqo_ref[:, sl] = ((g * c + _rope_partner(g * s)) * ATT_SCALE).astype(BF16)
            gk = dk_ref[:, sl]
            dko_ref[:, sl] = gk.astype(BF16)
            dkr = dkr + jnp.where(in_rope, gk, 0.0)
        dkr_ref[...] = dkr * c + _rope_partner(dkr * s)

    row = lambda w: pl.BlockSpec((tm, w), lambda i: (i, 0))
    return pl.pallas_call(
        body, name=name, grid=(t // tm,), in_specs=[row(1024), row(1024), row(LANES), row(LANES)],
        out_specs=[row(1024), row(1024), row(LANES)],
        out_shape=[jax.ShapeDtypeStruct((t, 1024), BF16), jax.ShapeDtypeStruct((t, 1024), BF16),
                   jax.ShapeDtypeStruct((t, LANES), F32)],
        compiler_params=_params(("parallel",)),
    )(dq, dk, cos_t, sin_t)


def _ab_prep_bwd(proj, q_a_norm, kv_a_norm, dqn, dkvn_k, dkvn_v, dkr, dd, dz, *, name):
    t = proj.shape[0]
    tm = _tile(t, ROW_TILE)
    hb = tm // POOL_HALO
    last_halo = t // POOL_HALO - 1
    nt = t // tm

    def body(p_ref, qg_ref, kg_ref, dqn_ref, dk1_ref, dk2_ref, dkr_ref, dd_ref, ddn_ref, dz_ref,
             dp_ref, dqg_ref, dkg_ref, ext):
        i = pl.program_id(0)

        @pl.when(i == 0)
        def _():
            dqg_ref[...] = jnp.zeros_like(dqg_ref)
            dkg_ref[...] = jnp.zeros_like(dkg_ref)

        def norm_bwd(x, g, dy, dg_ref):
            r = lax.rsqrt(jnp.mean(x * x, axis=-1, keepdims=True) + RMS_EPS)
            xh = x * r
            dxh = dy * g
            dg_ref[...] += jnp.sum(dy * xh, axis=0, keepdims=True)
            return r * (dxh - xh * jnp.mean(dxh * xh, axis=-1, keepdims=True))

        dql = norm_bwd(p_ref[:, 0:MLA_Q_RANK], qg_ref[...], dqn_ref[...], dqg_ref)
        dp_ref[:, 0:MLA_Q_RANK] = dql.astype(BF16)
        dkl = norm_bwd(p_ref[:, MLA_Q_RANK:384], kg_ref[...], dk1_ref[...] + dk2_ref[...], dkg_ref)
        dp_ref[:, MLA_Q_RANK:384] = dkl.astype(BF16)
        dp_ref[:, 384:512] = dkr_ref[...].astype(BF16)
        ddv = dd_ref[...]
        for g, w in enumerate(POOL_WINDOWS):
            lo = g * POOL_GROUP
            ext[0:tm, lo:lo + POOL_GROUP] = ddv[:, lo:lo + POOL_GROUP] / _pool_counts(i * tm, tm, w)
            nxt = ddn_ref[:, lo:lo + POOL_GROUP] / _pool_counts((i + 1) * tm, POOL_HALO, w)
            ext[tm:tm + POOL_HALO, lo:lo + POOL_GROUP] = jnp.where(i < nt - 1, nxt, 0.0)
        for g, w in enumerate(POOL_WINDOWS):
            lo = g * POOL_GROUP
            acc = ext[0:tm, lo:lo + POOL_GROUP]
            for s in range(1, w):
                acc = acc + ext[s:s + tm, lo:lo + POOL_GROUP]
            dp_ref[:, 512 + lo:512 + lo + POOL_GROUP] = (acc - ddv[:, lo:lo + POOL_GROUP]).astype(BF16)
        dp_ref[:, 1024:2048] = dz_ref[...]

    row = lambda w: pl.BlockSpec((tm, w), lambda i: (i, 0))
    vec = lambda w: pl.BlockSpec((1, w), lambda i: (0, 0))
    return pl.pallas_call(
        body, name=name, grid=(nt,),
        in_specs=[row(1024), vec(MLA_Q_RANK), vec(MLA_KV_RANK), row(MLA_Q_RANK), row(MLA_KV_RANK), row(MLA_KV_RANK),
                  row(LANES), row(POOL_WIDTH),
                  pl.BlockSpec((POOL_HALO, POOL_WIDTH), lambda i: (jnp.minimum((i + 1) * hb, last_halo), 0)),
                  row(1024)],
        out_specs=[row(IN_AB_PAD), vec(MLA_Q_RANK), vec(MLA_KV_RANK)],
        out_shape=[jax.ShapeDtypeStruct((t, IN_AB_PAD), BF16), jax.ShapeDtypeStruct((1, MLA_Q_RANK), F32),
                   jax.ShapeDtypeStruct((1, MLA_KV_RANK), F32)],
        scratch_shapes=[pltpu.VMEM((tm + POOL_HALO, POOL_WIDTH), F32)],
        compiler_params=_params(("arbitrary",)),
    )(proj, q_a_norm, kv_a_norm, dqn, dkvn_k, dkvn_v, dkr, dd, dd, dz)


def _gate_fwd(o, ybraw, proj, pool_scale, *, name):
    t = o.shape[0]
    tm = _tile(t, 2 * ROW_TILE)

    def body(o_ref, yb_ref, z_ref, ps_ref, y_ref):
        z = z_ref[...]
        sz = z * _sigmoid(z)
        y_ref[:, 0:512] = (o_ref[...] * sz[:, 0:512]).astype(BF16)
        y_ref[:, 512:1024] = (yb_ref[...] * ps_ref[...] * sz[:, 512:1024]).astype(BF16)

    row = lambda w: pl.BlockSpec((tm, w), lambda i: (i, 0))
    return pl.pallas_call(
        body, name=name, grid=(t // tm,),
        in_specs=[row(512), row(512), pl.BlockSpec((tm, 1024), lambda i: (i, 1)), pl.BlockSpec((1, 512), lambda i: (0, 0))],
        out_specs=row(1024), out_shape=jax.ShapeDtypeStruct((t, 1024), BF16), compiler_params=_params(("parallel",)),
    )(o, ybraw, proj, pool_scale)


def _gate_bwd(dy, o, ybraw, proj, pool_scale, *, name):
    t = o.shape[0]
    tm = _tile(t, ROW_TILE)

    def body(dy_ref, o_ref, yb_ref, z_ref, ps_ref, do_ref, dl_ref, dyb_ref, dz_ref, dps_ref):
        i = pl.program_id(0)
        z = z_ref[...]
        sg = _sigmoid(z)
        sz = z * sg
        dsz = sg * (1.0 + z * (1.0 - sg))
        dyv = dy_ref[...]
        dcat = dyv * sz
        ov = o_ref[...]
        ybs = yb_ref[...] * ps_ref[...]
        dz_ref[:, 0:512] = (dyv[:, 0:512] * ov * dsz[:, 0:512]).astype(BF16)
        dz_ref[:, 512:1024] = (dyv[:, 512:1024] * ybs * dsz[:, 512:1024]).astype(BF16)
        do = dcat[:, 0:512]
        do_ref[...] = do.astype(BF16)
        r_i = lax.broadcasted_iota(jnp.int32, (512, 512), 0) // MLA_V
        c_i = lax.broadcasted_iota(jnp.int32, (512, 512), 1) // MLA_V
        dl_ref[...] = _dot(do * ov, (r_i == c_i).astype(F32), NN, HI)
        dyb_ref[...] = (dcat[:, 512:1024] * ps_ref[...]).astype(BF16)

        @pl.when(i == 0)
        def _():
            dps_ref[...] = jnp.zeros_like(dps_ref)

        dps_ref[...] += jnp.sum(dcat[:, 512:1024] * yb_ref[...], axis=0, keepdims=True)

    row = lambda w: pl.BlockSpec((tm, w), lambda i: (i, 0))
    vec = pl.BlockSpec((1, 512), lambda i: (0, 0))
    return pl.pallas_call(
        body, name=name, grid=(t // tm,),
        in_specs=[row(1024), row(512), row(512), pl.BlockSpec((tm, 1024), lambda i: (i, 1)), vec],
        out_specs=[row(512), row(512), row(512), row(1024), vec],
        out_shape=[jax.ShapeDtypeStruct((t, 512), BF16), jax.ShapeDtypeStruct((t, 512), F32),
                   jax.ShapeDtypeStruct((t, 512), BF16), jax.ShapeDtypeStruct((t, 1024), BF16),
                   jax.ShapeDtypeStruct((1, 512), F32)],
        compiler_params=_params(("arbitrary",)),
    )(dy, o, ybraw, proj, pool_scale)


ATT_HP_FWD = 4
ATT_HP_BWD = 2


def _diag_mask(tq):
    return lax.broadcasted_iota(jnp.int32, (tq, tq), 1) <= lax.broadcasted_iota(jnp.int32, (tq, tq), 0)


def _block_schedule(nq, key_major):
    if key_major:
        pairs = [(qi, ki) for ki in range(nq) for qi in range(ki, nq)]
    else:
        pairs = [(qi, ki) for qi in range(nq) for ki in range(qi + 1)]
    return jnp.asarray([p[0] for p in pairs], jnp.int32), jnp.asarray([p[1] for p in pairs], jnp.int32)


def _attn_fwd(q, k, v, *, name):
    t = q.shape[0]
    tq = _tile(t, ATT_TILE)
    nq = t // tq
    hp = ATT_HP_FWD
    qi_tab, ki_tab = _block_schedule(nq, key_major=False)

    def body(qi_ref, ki_ref, q_ref, k_ref, v_ref, o_ref, lse_ref, m_sc, l_sc, acc_sc):
        step = pl.program_id(1)
        qi, ki = qi_ref[step], ki_ref[step]

        @pl.when(ki == 0)
        def _():
            m_sc[...] = jnp.full_like(m_sc, -jnp.inf)
            l_sc[...] = jnp.zeros_like(l_sc)
            acc_sc[...] = jnp.zeros_like(acc_sc)

        def block(on_diagonal):
            scores = []
            for h in range(hp):
                sl = slice(h * LANES, (h + 1) * LANES)
                scores.append(_dot(q_ref[:, sl], k_ref[:, sl], NT))
            if on_diagonal:
                mask = _diag_mask(tq)
                scores = [jnp.where(mask, s, -jnp.inf) for s in scores]
            for h, s in enumerate(scores):
                vv = v_ref[:, (h // 2) * LANES:(h // 2 + 1) * LANES]
                m_prev = m_sc[h]
                m_new = jnp.maximum(m_prev, jnp.max(s, axis=-1, keepdims=True))
                alpha = jnp.exp(m_prev - m_new)
                p = jnp.exp(s - m_new[:, 0:1])
                l_sc[h] = alpha * l_sc[h] + jnp.sum(p, axis=-1, keepdims=True)
                acc_sc[h] = alpha * acc_sc[h] + _dot(p.astype(BF16), vv)
                m_sc[h] = m_new

        pl.when(ki < qi)(functools.partial(block, False))
        pl.when(ki == qi)(functools.partial(block, True))

        @pl.when(ki == qi)
        def _():
            first = lax.broadcasted_iota(jnp.int32, (tq, LANES), 1) < MLA_V
            for pr in range(hp // 2):
                a, b = 2 * pr, 2 * pr + 1
                sl = slice(pr * LANES, (pr + 1) * LANES)
                o_ref[:, sl] = jnp.where(first, acc_sc[a] / l_sc[a], acc_sc[b] / l_sc[b])
                lse_ref[:, sl] = jnp.where(first, m_sc[a] + jnp.log(l_sc[a]), m_sc[b] + jnp.log(l_sc[b]))

    grid_spec = pltpu.PrefetchScalarGridSpec(
        num_scalar_prefetch=2, grid=(MLA_HEADS // hp, qi_tab.shape[0]),
        in_specs=[pl.BlockSpec((tq, hp * LANES), lambda g, s, qt, kt: (qt[s], g)),
                  pl.BlockSpec((tq, hp * LANES), lambda g, s, qt, kt: (kt[s], g)),
                  pl.BlockSpec((tq, hp * MLA_V), lambda g, s, qt, kt: (kt[s], g))],
        out_specs=[pl.BlockSpec((tq, hp * MLA_V), lambda g, s, qt, kt: (qt[s], g)),
                   pl.BlockSpec((tq, hp * MLA_V), lambda g, s, qt, kt: (qt[s], g))],
        scratch_shapes=[pltpu.VMEM((hp, tq, LANES), F32)] * 3,
    )
    return pl.pallas_call(
        body, name=name, grid_spec=grid_spec,
        out_shape=[jax.ShapeDtypeStruct((t, 512), F32), jax.ShapeDtypeStruct((t, 512), F32)],
        compiler_params=_params(("parallel", "arbitrary")),
    )(qi_tab, ki_tab, q, k, v)


def _attn_bwd(q, k, v, do, lse, delta, *, name):
    t = q.shape[0]
    tq = _tile(t, ATT_TILE)
    nq = t // tq
    hp = ATT_HP_BWD
    qi_tab, ki_tab = _block_schedule(nq, key_major=True)

    def body(qi_ref, ki_ref, q_ref, k_ref, v_ref, do_ref, lse_ref, dl_ref, dq_ref, dk_ref, dv_ref, dk_sc, dv_sc):
        step = pl.program_id(1)
        qi, ki = qi_ref[step], ki_ref[step]

        @pl.when(step == 0)
        def _():
            dq_ref[...] = jnp.zeros_like(dq_ref)

        @pl.when(qi == ki)
        def _():
            dk_sc[...] = jnp.zeros_like(dk_sc)
            dv_sc[...] = jnp.zeros_like(dv_sc)

        def block(on_diagonal):
            lane = lax.broadcasted_iota(jnp.int32, (tq, LANES), 1)
            rows = pl.ds(pl.multiple_of(qi * tq, tq), tq)
            heads = [slice(h * LANES, (h + 1) * LANES) for h in range(hp)]
            scores = [_dot(q_ref[:, sl], k_ref[:, sl], NT) for sl in heads]
            dps = []
            for h in range(hp):
                dov = do_ref[:, (h // 2) * LANES:(h // 2 + 1) * LANES]
                mine = (lane < MLA_V) if h % 2 == 0 else (lane >= MLA_V)
                dps.append(_dot(jnp.where(mine, dov, jnp.zeros_like(dov)), v_ref[:, (h // 2) * LANES:(h // 2 + 1) * LANES], NT))
            mask = _diag_mask(tq) if on_diagonal else None
            for h, sl in enumerate(heads):
                col = (h // 2) * LANES + (h % 2) * MLA_V
                p = jnp.exp(scores[h] - lse_ref[:, col:col + 1])
                if on_diagonal:
                    p = jnp.where(mask, p, 0.0)
                ds = (p * (dps[h] - dl_ref[:, col:col + 1])).astype(BF16)
                dv_sc[h] += _dot(p.astype(BF16), do_ref[:, (h // 2) * LANES:(h // 2 + 1) * LANES], TN)
                dk_sc[h] += _dot(ds, q_ref[:, sl], TN)
                dq_ref[rows, sl] += _dot(ds, k_ref[:, sl], NN)

        pl.when(qi > ki)(functools.partial(block, False))
        pl.when(qi == ki)(functools.partial(block, True))

        @pl.when(qi == nq - 1)
        def _():
            first = lax.broadcasted_iota(jnp.int32, (tq, LANES), 1) < MLA_V
            for h in range(hp):
                dk_ref[:, h * LANES:(h + 1) * LANES] = dk_sc[h]
            for pr in range(hp // 2):
                dv_ref[:, pr * LANES:(pr + 1) * LANES] = jnp.where(first, dv_sc[2 * pr], dv_sc[2 * pr + 1]).astype(BF16)

    qrow = lambda w: pl.BlockSpec((tq, w), lambda g, s, qt, kt: (qt[s], g))
    krow = lambda w: pl.BlockSpec((tq, w), lambda g, s, qt, kt: (kt[s], g))
    grid_spec = pltpu.PrefetchScalarGridSpec(
        num_scalar_prefetch=2, grid=(MLA_HEADS // hp, qi_tab.shape[0]),
        in_specs=[qrow(hp * LANES), krow(hp * LANES), krow(hp * MLA_V), qrow(hp * MLA_V), qrow(hp * MLA_V), qrow(hp * MLA_V)],
        out_specs=[pl.BlockSpec((t, hp * LANES), lambda g, s, qt, kt: (0, g)), krow(hp * LANES), krow(hp * MLA_V)],
        scratch_shapes=[pltpu.VMEM((hp, tq, LANES), F32), pltpu.VMEM((hp, tq, LANES), F32)],
    )
    return pl.pallas_call(
        body, name=name, grid_spec=grid_spec,
        out_shape=[jax.ShapeDtypeStruct((t, 1024), F32), jax.ShapeDtypeStruct((t, 1024), F32),
                   jax.ShapeDtypeStruct((t, 512), BF16)],
        compiler_params=_params(("parallel", "arbitrary")),
    )(qi_tab, ki_tab, q, k, v, do, lse, delta)


def _conv_rows(ext, tm, w_ref, sec):
    c0 = sec * 1024
    y = ext[CONV_HALO - 3:CONV_HALO - 3 + tm, c0:c0 + 1024] * w_ref[0:1, c0:c0 + 1024]
    for j in range(1, CONV_WIDTH):
        y = y + ext[CONV_HALO - 3 + j:CONV_HALO - 3 + j + tm, c0:c0 + 1024] * w_ref[j:j + 1, c0:c0 + 1024]
    return y


def _c_prep(proj_c, conv_w, a_log, dt_bias, *, name):
    t = proj_c.shape[0]
    tm = _tile(t, ROW_TILE)
    hb = tm // CONV_HALO

    def body(p_ref, halo_ref, ab_ref, w_ref, al_ref, dtb_ref, q_ref, k_ref, v_ref, g_ref, b_ref, gt_ref, ext):
        i = pl.program_id(0)
        ext[0:CONV_HALO, :] = jnp.where(i > 0, halo_ref[...], 0.0)
        ext[CONV_HALO:CONV_HALO + tm, :] = p_ref[...]
        for sec, o_ref in enumerate((q_ref, k_ref, v_ref)):
            y = _conv_rows(ext, tm, w_ref, sec)
            y = y * _sigmoid(y)
            if sec == 2:
                o_ref[...] = y
                continue
            scale = GDN_DK ** -0.5 if sec == 0 else 1.0
            for h in range(GDN_HEADS):
                sl = slice(h * LANES, (h + 1) * LANES)
                blk = y[:, sl]
                r = lax.rsqrt(jnp.sum(blk * blk, axis=-1, keepdims=True) + RMS_EPS)
                o_ref[:, sl] = blk * (r * scale)
        ab = ab_ref[...]
        g = -jnp.exp(al_ref[...]) * _softplus(ab + dtb_ref[...])
        beta = _sigmoid(ab)
        ri = lax.broadcasted_iota(jnp.int32, (tm, tm), 0)
        ci = lax.broadcasted_iota(jnp.int32, (tm, tm), 1)
        lower = ((ri // CHUNK) == (ci // CHUNK)) & (ri >= ci)
        gc = _dot(lower.astype(F32), g, NN, HI)
        eye = lax.broadcasted_iota(jnp.int32, (LANES, LANES), 0) == lax.broadcasted_iota(jnp.int32, (LANES, LANES), 1)
        gt_ref[...] = _dot(eye.astype(F32), gc, NT, HI)[0:GDN_HEADS, :]
        for h in range(GDN_HEADS):
            sl = slice(h * LANES, (h + 1) * LANES)
            g_ref[:, sl] = jnp.broadcast_to(gc[:, h:h + 1], (tm, LANES))
            b_ref[:, sl] = jnp.broadcast_to(beta[:, GDN_HEADS + h:GDN_HEADS + h + 1], (tm, LANES))

    row = lambda w: pl.BlockSpec((tm, w), lambda i: (i, 0))
    vec = lambda r, w: pl.BlockSpec((r, w), lambda i: (0, 0))
    out = jax.ShapeDtypeStruct((t, 1024), F32)
    return pl.pallas_call(
        body, name=name, grid=(t // tm,),
        in_specs=[row(3072), pl.BlockSpec((CONV_HALO, 3072), lambda i: (jnp.maximum(i * hb - 1, 0), 0)),
                  pl.BlockSpec((tm, LANES), lambda i: (i, 32)), vec(CONV_WIDTH, 3072), vec(1, LANES), vec(1, LANES)],
        out_specs=[row(1024)] * 5 + [pl.BlockSpec((GDN_HEADS, tm), lambda i: (0, i))],
        out_shape=[out] * 5 + [jax.ShapeDtypeStruct((GDN_HEADS, t), F32)],
        scratch_shapes=[pltpu.VMEM((tm + CONV_HALO, 3072), F32)],
        compiler_params=_params(("parallel",)),
    )(proj_c, proj_c, proj_c, conv_w, a_log, dt_bias)


def _c_prep_bwd(proj_c, conv_w, a_log, dt_bias, dq, dk, dv, dgb, dbb, dz, *, name):
    t = proj_c.shape[0]
    tm = _tile(t, ROW_TILE // 2)
    hb = tm // CONV_HALO
    nt = t // tm
    rev = lambda i: nt - 1 - i

    def body(p_ref, halo_ref, ab_ref, w_ref, al_ref, dtb_ref, dq_ref, dk_ref, dv_ref, dg_ref, db_ref, dz_ref,
             dp_ref, dw_ref, dal_ref, ddt_ref, ext, dyext, carry, taps):
        step = pl.program_id(0)
        i = rev(step)

        @pl.when(step == 0)
        def _():
            dw_ref[...] = jnp.zeros_like(dw_ref)
            dal_ref[...] = jnp.zeros_like(dal_ref)
            ddt_ref[...] = jnp.zeros_like(ddt_ref)
            carry[...] = jnp.zeros_like(carry)

        ext[0:CONV_HALO, :] = jnp.where(i > 0, halo_ref[...], 0.0)
        ext[CONV_HALO:CONV_HALO + tm, :] = p_ref[...]
        for sec, g_ref in enumerate((dq_ref, dk_ref, dv_ref)):
            c0 = sec * 1024
            for j in range(CONV_WIDTH):
                taps[j] = ext[CONV_HALO - 3 + j:CONV_HALO - 3 + j + tm, c0:c0 + 1024]
            y = taps[0] * w_ref[0:1, c0:c0 + 1024]
            for j in range(1, CONV_WIDTH):
                y = y + taps[j] * w_ref[j:j + 1, c0:c0 + 1024]
            sg = _sigmoid(y)
            act = y * sg
            if sec == 2:
                dact = g_ref[...]
            else:
                scale = GDN_DK ** -0.5 if sec == 0 else 1.0
                parts = []
                for h in range(GDN_HEADS):
                    sl = slice(h * LANES, (h + 1) * LANES)
                    blk = act[:, sl]
                    r = lax.rsqrt(jnp.sum(blk * blk, axis=-1, keepdims=True) + RMS_EPS)
                    n = blk * r
                    dn = g_ref[:, sl] * scale
                    parts.append(r * (dn - n * jnp.sum(dn * n, axis=-1, keepdims=True)))
                dact = jnp.concatenate(parts, axis=-1)
            dy = dact * (sg * (1.0 + y * (1.0 - sg)))
            dyext[0:tm, c0:c0 + 1024] = dy
            for j in range(CONV_WIDTH):
                dw_ref[j:j + 1, c0:c0 + 1024] += jnp.sum(dy * taps[j], axis=0, keepdims=True)
        dyext[tm:tm + CONV_HALO, :] = carry[...]
        carry[...] = dyext[0:CONV_HALO, :]
        for sec in range(3):
            c0 = sec * 1024
            dx = dyext[3:3 + tm, c0:c0 + 1024] * w_ref[0:1, c0:c0 + 1024]
            for j in range(1, CONV_WIDTH):
                dx = dx + dyext[3 - j:3 - j + tm, c0:c0 + 1024] * w_ref[j:j + 1, c0:c0 + 1024]
            dp_ref[:, c0:c0 + 1024] = dx.astype(BF16)
        dp_ref[:, 3072:4096] = dz_ref[...]
        lane = lax.broadcasted_iota(jnp.int32, (tm, LANES), 1)
        dg = jnp.zeros((tm, LANES), F32)
        dbeta = jnp.zeros((tm, LANES), F32)
        for h in range(GDN_HEADS):
            sl = slice(h * LANES, (h + 1) * LANES)
            dg = dg + jnp.where(lane == h, dg_ref[:, sl], 0.0)
            dbeta = dbeta + jnp.where(lane == GDN_HEADS + h, db_ref[:, sl], 0.0)
        ri = lax.broadcasted_iota(jnp.int32, (tm, tm), 0)
        ci = lax.broadcasted_iota(jnp.int32, (tm, tm), 1)
        upper = ((ri // CHUNK) == (ci // CHUNK)) & (ri <= ci)
        dg = _dot(upper.astype(F32), dg, NN, HI)
        pre = ab_ref[...] + dtb_ref[...]
        s = _sigmoid(pre)
        a_exp = jnp.exp(al_ref[...])
        dg_da = dg * (-a_exp * s)
        dp_ref[:, 4096:IN_C_PAD] = (dg_da + dbeta * s * (1.0 - s)).astype(BF16)
        dal_ref[...] += jnp.sum(dg * (-a_exp * _softplus(pre)), axis=0, keepdims=True)
        ddt_ref[...] += jnp.sum(dg_da, axis=0, keepdims=True)

    row = lambda w: pl.BlockSpec((tm, w), lambda s: (rev(s), 0))
    vec = lambda r, w: pl.BlockSpec((r, w), lambda s: (0, 0))
    return pl.pallas_call(
        body, name=name, grid=(nt,),
        in_specs=[row(3072), pl.BlockSpec((CONV_HALO, 3072), lambda s: (jnp.maximum(rev(s) * hb - 1, 0), 0)),
                  pl.BlockSpec((tm, LANES), lambda s: (rev(s), 32)), vec(CONV_WIDTH, 3072), vec(1, LANES), vec(1, LANES),
                  row(1024), row(1024), row(1024), row(1024), row(1024), row(1024)],
        out_specs=[row(IN_C_PAD), vec(CONV_WIDTH, 3072), vec(1, LANES), vec(1, LANES)],
        out_shape=[jax.ShapeDtypeStruct((t, IN_C_PAD), BF16), jax.ShapeDtypeStruct((CONV_WIDTH, 3072), F32),
                   jax.ShapeDtypeStruct((1, LANES), F32), jax.ShapeDtypeStruct((1, LANES), F32)],
        scratch_shapes=[pltpu.VMEM((tm + CONV_HALO, 3072), F32), pltpu.VMEM((tm + CONV_HALO, 3072), F32),
                        pltpu.VMEM((CONV_HALO, 3072), F32), pltpu.VMEM((CONV_WIDTH, tm, 1024), F32)],
        compiler_params=_params(("arbitrary",)),
    )(proj_c, proj_c, proj_c, conv_w, a_log, dt_bias, dq, dk, dv, dgb, dbb, dz)


def _o_gate_fwd(o, proj_c, o_norm, *, name):
    t = o.shape[0]
    tm = _tile(t, 2 * ROW_TILE)

    def body(o_ref, z_ref, g_ref, y_ref):
        for h in range(GDN_HEADS):
            sl = slice(h * LANES, (h + 1) * LANES)
            x = o_ref[:, sl]
            r = lax.rsqrt(jnp.mean(x * x, axis=-1, keepdims=True) + RMS_EPS)
            z = z_ref[:, sl]
            y_ref[:, sl] = (x * r * g_ref[...] * (z * _sigmoid(z))).astype(BF16)

    row = pl.BlockSpec((tm, 1024), lambda i: (i, 0))
    return pl.pallas_call(
        body, name=name, grid=(t // tm,),
        in_specs=[row, pl.BlockSpec((tm, 1024), lambda i: (i, 3)), pl.BlockSpec((1, LANES), lambda i: (0, 0))],
        out_specs=row, out_shape=jax.ShapeDtypeStruct((t, 1024), BF16), compiler_params=_params(("parallel",)),
    )(o, proj_c, o_norm)


def _o_gate_bwd(dy, o, proj_c, o_norm, *, name):
    t = o.shape[0]
    tm = _tile(t, 2 * ROW_TILE)

    def body(dy_ref, o_ref, z_ref, g_ref, do_ref, dz_ref, dg_ref):
        i = pl.program_id(0)

        @pl.when(i == 0)
        def _():
            dg_ref[...] = jnp.zeros_like(dg_ref)

        dg = jnp.zeros((1, LANES), F32)
        for h in range(GDN_HEADS):
            sl = slice(h * LANES, (h + 1) * LANES)
            x = o_ref[:, sl]
            r = lax.rsqrt(jnp.mean(x * x, axis=-1, keepdims=True) + RMS_EPS)
            xh = x * r
            z = z_ref[:, sl]
            sg = _sigmoid(z)
            dyv = dy_ref[:, sl]
            dn = dyv * (z * sg)
            dz_ref[:, sl] = (dyv * xh * g_ref[...] * (sg * (1.0 + z * (1.0 - sg)))).astype(BF16)
            dxh = dn * g_ref[...]
            do_ref[:, sl] = r * (dxh - xh * jnp.mean(dxh * xh, axis=-1, keepdims=True))
            dg = dg + jnp.sum(dn * xh, axis=0, keepdims=True)
        dg_ref[...] += dg

    row = pl.BlockSpec((tm, 1024), lambda i: (i, 0))
    vec = pl.BlockSpec((1, LANES), lambda i: (0, 0))
    return pl.pallas_call(
        body, name=name, grid=(t // tm,), in_specs=[row, row, pl.BlockSpec((tm, 1024), lambda i: (i, 3)), vec],
        out_specs=[row, row, vec],
        out_shape=[jax.ShapeDtypeStruct((t, 1024), F32), jax.ShapeDtypeStruct((t, 1024), BF16),
                   jax.ShapeDtypeStruct((1, LANES), F32)],
        compiler_params=_params(("arbitrary",)),
    )(dy, o, proj_c, o_norm)


PAIR = 2 * CHUNK
GDN_HP = 8


def _bdot(a, b, dims=NN):
    return _dot(a.astype(BF16), b.astype(BF16), dims)


def _each(f, *lists):
    return [f(*args) for args in zip(*lists)]


def _pair_common(q, k, v, gci, gcj, beta):
    ri = lax.broadcasted_iota(jnp.int32, (PAIR, PAIR), 0)
    ci = lax.broadcasted_iota(jnp.int32, (PAIR, PAIR), 1)
    same = (ri // CHUNK) == (ci // CHUNK)
    incl = same & (ri >= ci)
    strict = same & (ri > ci)
    eye = (ri == ci).astype(F32)
    first = lax.broadcasted_iota(jnp.int32, (PAIR, LANES), 0) < CHUNK
    gamma = _each(lambda gi, gj: jnp.where(incl, jnp.exp(jnp.minimum(gi - gj, 0.0)), 0.0), gci, gcj)
    kb = _each(jnp.multiply, k, beta)
    kk = _each(lambda a, b: _bdot(a, b, NT), kb, k)
    qk = _each(lambda a, b: _bdot(a, b, NT), q, k)
    m = _each(lambda x, g: jnp.where(strict, x * g, 0.0), kk, gamma)
    tm_ = _each(lambda x: eye - x, m)
    pw = _each(lambda x: _bdot(x, x), m)
    for it in range(5):
        tm_ = _each(lambda x, p: x + _bdot(x, p), tm_, pw)
        if it < 4:
            pw = _each(lambda p: _bdot(p, p), pw)
    eg = _each(jnp.exp, gci)
    vb = _each(jnp.multiply, v, beta)
    kbe = _each(jnp.multiply, kb, eg)
    uw = _each(lambda x, a, b: _bdot(x, jnp.concatenate([a, b], axis=1)), tm_, vb, kbe)
    attn = _each(lambda x, g: jnp.where(incl, x * g, 0.0), qk, gamma)
    gl_a = _each(lambda g: g[CHUNK - 1:CHUNK, :], gci)
    gl_b = _each(lambda g: g[PAIR - 1:PAIR, :], gci)
    ek = _each(lambda a, b, g: jnp.exp(jnp.where(first, a, b) - g), gl_a, gl_b, gci)
    return dict(incl=incl, strict=strict, gamma=gamma, kb=kb, m=m, tm=tm_, eg=eg, vb=vb, kbe=kbe,
                u=_each(lambda x: x[:, :LANES], uw), w=_each(lambda x: x[:, LANES:], uw), attn=attn,
                qd=_each(jnp.multiply, q, eg), ek=ek, kd=_each(jnp.multiply, k, ek),
                glast_a=_each(jnp.exp, gl_a), glast_b=_each(jnp.exp, gl_b))


def _gdn_specs(t, ts, order):
    nc = ts // CHUNK
    blk = pl.BlockSpec((ts, GDN_HP * LANES), lambda h, s: (order(s), h))
    row = pl.BlockSpec((GDN_HP, 1, ts), lambda h, s: (h, 0, order(s)))
    st = pl.BlockSpec((GDN_HP, nc, LANES, LANES), lambda h, s: (h, order(s), 0, 0))
    return blk, row, st


def _gdn_fwd(q, k, v, gcb, gct, bb, *, name):
    t = q.shape[0]
    ts = _tile(t, GDN_TILE)
    npair = ts // PAIR

    def body(q_ref, k_ref, v_ref, g_ref, gt_ref, b_ref, o_ref, st_ref, s_sc):
        @pl.when(pl.program_id(1) == 0)
        def _():
            s_sc[...] = jnp.zeros_like(s_sc)

        def pair(pi, _):
            rows = pl.ds(pl.multiple_of(pi * PAIR, PAIR), PAIR)
            heads = [slice(hh * LANES, (hh + 1) * LANES) for hh in range(GDN_HP)]
            c = CHUNK
            cat0 = lambda *xs: jnp.concatenate(xs, axis=0)
            s0 = [s_sc[hh] for hh in range(GDN_HP)]
            cm = _pair_common([q_ref[rows, sl] for sl in heads], [k_ref[rows, sl] for sl in heads],
                              [v_ref[rows, sl] for sl in heads], [g_ref[rows, sl] for sl in heads],
                              [gt_ref[hh, :, rows] for hh in range(GDN_HP)], [b_ref[rows, sl] for sl in heads])
            u, w, qd, kd = cm["u"], cm["w"], cm["qd"], cm["kd"]
            r0 = _each(lambda w_, q_, s: _bdot(cat0(w_[:c], q_[:c]), s), w, qd, s0)
            vn_a = _each(lambda u_, r: u_[:c] - r[:c], u, r0)
            s1 = _each(lambda s, gl, k_, vn: s * gl + _bdot(k_[:c], vn, TN), s0, cm["glast_a"], kd, vn_a)
            r1 = _each(lambda w_, q_, s: _bdot(cat0(w_[c:], q_[c:]), s), w, qd, s1)
            vn_b = _each(lambda u_, r: u_[c:] - r[:c], u, r1)
            s2 = _each(lambda s, gl, k_, vn: s * gl + _bdot(k_[c:], vn, TN), s1, cm["glast_b"], kd, vn_b)
            o = _each(lambda ra, rb, at, va, vb_: cat0(ra[c:], rb[c:]) + _bdot(at, cat0(va, vb_)),
                      r0, r1, cm["attn"], vn_a, vn_b)
            for hh, sl in enumerate(heads):
                st_ref[hh, 2 * pi] = s0[hh]
                st_ref[hh, 2 * pi + 1] = s1[hh]
                s_sc[hh] = s2[hh]
                o_ref[rows, sl] = o[hh]
            return 0

        lax.fori_loop(0, npair, pair, 0)

    blk, row, st = _gdn_specs(t, ts, lambda s: s)
    return pl.pallas_call(
        body, name=name, grid=(GDN_HEADS // GDN_HP, t // ts), in_specs=[blk, blk, blk, blk, row, blk],
        out_specs=[blk, st],
        out_shape=[jax.ShapeDtypeStruct((t, 1024), F32), jax.ShapeDtypeStruct((GDN_HEADS, t // CHUNK, LANES, LANES), F32)],
        scratch_shapes=[pltpu.VMEM((GDN_HP, LANES, LANES), F32)],
        compiler_params=_params(("parallel", "arbitrary")),
    )(q, k, v, gcb, gct, bb)


def _gdn_bwd(q, k, v, gcb, gct, bb, do, states, *, name):
    t = q.shape[0]
    ts = _tile(t, GDN_TILE)
    npair = ts // PAIR
    ns = t // ts
    c = CHUNK

    def body(q_ref, k_ref, v_ref, g_ref, gt_ref, b_ref, do_ref, st_ref, dq_ref, dk_ref, dv_ref, dg_ref, db_ref, ds_sc):
        @pl.when(pl.program_id(1) == 0)
        def _():
            ds_sc[...] = jnp.zeros_like(ds_sc)

        rowsum = lambda x: jnp.sum(x, axis=-1, keepdims=True)
        total = lambda x: jnp.sum(rowsum(x), axis=0, keepdims=True)
        cat0 = lambda *xs: jnp.concatenate(xs, axis=0)
        cat1 = lambda *xs: jnp.concatenate(xs, axis=1)

        def pair(step, _):
            pi = npair - 1 - step
            rows = pl.ds(pl.multiple_of(pi * PAIR, PAIR), PAIR)
            heads = [slice(hh * LANES, (hh + 1) * LANES) for hh in range(GDN_HP)]
            hs = range(GDN_HP)
            qv, kv, vv = ([r[rows, sl] for sl in heads] for r in (q_ref, k_ref, v_ref))
            beta = [b_ref[rows, sl] for sl in heads]
            dov = [do_ref[rows, sl] for sl in heads]
            s0 = [st_ref[hh, 2 * pi] for hh in hs]
            s1 = [st_ref[hh, 2 * pi + 1] for hh in hs]
            ds2 = [ds_sc[hh] for hh in hs]
            cm = _pair_common(qv, kv, vv, [g_ref[rows, sl] for sl in heads], [gt_ref[hh, :, rows] for hh in hs], beta)
            u, w, qd, kd, attn = cm["u"], cm["w"], cm["qd"], cm["kd"], cm["attn"]
            tmat, gamma, eg = cm["tm"], cm["gamma"], cm["eg"]
            incl, strict = cm["incl"], cm["strict"]
            vn_a = _each(lambda u_, w_, s: u_[:c] - _bdot(w_[:c], s), u, w, s0)
            vn_b = _each(lambda u_, w_, s: u_[c:] - _bdot(w_[c:], s), u, w, s1)
            vn = _each(cat0, vn_a, vn_b)
            dvn_att = _each(lambda a, d: _bdot(a, d, TN), attn, dov)
            dattn = _each(lambda d, v_: jnp.where(incl, _bdot(d, v_, NT), 0.0), dov, vn)
            dvn_b = _each(lambda x, k_, d: x[c:] + _bdot(k_[c:], d), dvn_att, kd, ds2)
            rb = _each(lambda d, x, s: _bdot(cat0(d[c:], x), s, NT), dov, dvn_b, s1)
            dkd_b = _each(lambda v_, d: _bdot(v_, d, NT), vn_b, ds2)
            dgl_b = _each(lambda d, s: total(d * s), ds2, s1)
            ds1 = _each(lambda d, gl, q_, w_, o_, x: d * gl + _bdot(cat0(q_[c:], w_[c:]), cat0(o_[c:], -x), TN),
                        ds2, cm["glast_b"], qd, w, dov, dvn_b)
            dvn_a = _each(lambda x, k_, d: x[:c] + _bdot(k_[:c], d), dvn_att, kd, ds1)
            ra = _each(lambda d, x, s: _bdot(cat0(d[:c], x), s, NT), dov, dvn_a, s0)
            dkd_a = _each(lambda v_, d: _bdot(v_, d, NT), vn_a, ds1)
            dgl_a = _each(lambda d, s: total(d * s), ds1, s0)
            ds0 = _each(lambda d, gl, q_, w_, o_, x: d * gl + _bdot(cat0(q_[:c], w_[:c]), cat0(o_[:c], -x), TN),
                        ds1, cm["glast_a"], qd, w, dov, dvn_a)
            dvn = _each(cat0, dvn_a, dvn_b)
            dqd = _each(lambda a, b: cat0(a[:c], b[:c]), ra, rb)
            dw = _each(lambda a, b: -cat0(a[c:], b[c:]), ra, rb)
            dkd = _each(cat0, dkd_a, dkd_b)
            dvw = _each(cat1, dvn, dw)
            dvbk = _each(lambda t_, x: _bdot(t_, x, TN), tmat, dvw)
            dvb = _each(lambda x: x[:, :LANES], dvbk)
            dkbe = _each(lambda x: x[:, LANES:], dvbk)
            dt_ = _each(lambda x, a, b: _bdot(x, cat1(a, b), NT), dvw, cm["vb"], cm["kbe"])
            da1 = _each(lambda t_, x: _bdot(t_, x, TN), tmat, dt_)
            dm = _each(lambda x, t_: jnp.where(strict, -_bdot(x, t_, NT), 0.0), da1, tmat)
            dkk = _each(jnp.multiply, dm, gamma)
            dqk = _each(jnp.multiply, dattn, gamma)
            z = _each(lambda a, b, c_, d: a * b + c_ * d, dm, cm["m"], dattn, attn)
            dkb = _each(lambda x, k_, y, e: _bdot(x, k_) + y * e, dkk, kv, dkbe, eg)
            dk = _each(lambda a, b, kb_, q_, x, e, y, be: _bdot(cat0(a, b), cat0(kb_, q_), TN) + x * e + y * be,
                       dkk, dqk, cm["kb"], qv, dkd, cm["ek"], dkb, beta)
            dq = _each(lambda x, k_, y, e: _bdot(x, k_) + y * e, dqk, kv, dqd, eg)

            def colsum_of(z_):
                zh = z_.astype(BF16)
                zl = (z_ - zh.astype(F32)).astype(BF16)
                return _dot(cat0(zh, zl), jnp.ones((2 * PAIR, LANES), BF16), TN)

            colsum = _each(colsum_of, z)
            ri = lax.broadcasted_iota(jnp.int32, (PAIR, LANES), 0)
            for hh, sl in enumerate(heads):
                dkd_kd = dkd[hh] * kd[hh]
                dgc = (rowsum(z[hh]) - colsum[hh] + rowsum(dqd[hh] * qd[hh]) - rowsum(dkd_kd)
                       + rowsum(dkbe[hh] * cm["kbe"][hh]))
                last_a = total(dkd_kd[:c]) + dgl_a[hh] * cm["glast_a"][hh]
                last_b = total(dkd_kd[c:]) + dgl_b[hh] * cm["glast_b"][hh]
                dgc = dgc + jnp.where(ri == c - 1, last_a, 0.0) + jnp.where(ri == PAIR - 1, last_b, 0.0)
                ds_sc[hh] = ds0[hh]
                dq_ref[rows, sl] = dq[hh]
                dk_ref[rows, sl] = dk[hh]
                dv_ref[rows, sl] = dvb[hh] * beta[hh]
                db_ref[rows, sl] = jnp.broadcast_to(rowsum(dkb[hh] * kv[hh]) + rowsum(dvb[hh] * vv[hh]), (PAIR, LANES))
                dg_ref[rows, sl] = dgc
            return 0

        lax.fori_loop(0, npair, pair, 0)

    blk, row, st = _gdn_specs(t, ts, lambda s: ns - 1 - s)
    out = jax.ShapeDtypeStruct((t, 1024), F32)
    return pl.pallas_call(
        body, name=name, grid=(GDN_HEADS // GDN_HP, ns), in_specs=[blk, blk, blk, blk, row, blk, blk, st],
        out_specs=[blk] * 5, out_shape=[out] * 5, scratch_shapes=[pltpu.VMEM((GDN_HP, LANES, LANES), F32)],
        compiler_params=_params(("parallel", "arbitrary")),
    )(q, k, v, gcb, gct, bb, do, states)


def _loss_head(h, g, target, *, name):
    t, d = h.shape
    tm = _tile(t, 2 * ROW_TILE)

    def body(h_ref, g_ref, t_ref, dh_ref, dhb_ref, dg_ref, loss_ref):
        i = pl.program_id(0)
        x = h_ref[...]
        r = lax.rsqrt(jnp.mean(x * x, axis=-1, keepdims=True) + RMS_EPS)
        xh = x * r
        err = xh * g_ref[...] - t_ref[...]
        dy = err * (1.0 / d)
        dxh = dy * g_ref[...]
        dh = r * (dxh - xh * jnp.mean(dxh * xh, axis=-1, keepdims=True))
        dh_ref[...] = dh
        dhb_ref[...] = dh.astype(BF16)

        @pl.when(i == 0)
        def _():
            dg_ref[...] = jnp.zeros_like(dg_ref)
            loss_ref[...] = jnp.zeros_like(loss_ref)

        dg_ref[...] += jnp.sum(dy * xh, axis=0, keepdims=True)
        part = 0.5 * jnp.sum(jnp.mean(err * err, axis=-1, keepdims=True), axis=0, keepdims=True)
        loss_ref[...] += jnp.broadcast_to(part, loss_ref.shape)

    row = pl.BlockSpec((tm, d), lambda i: (i, 0))
    vec = pl.BlockSpec((1, d), lambda i: (0, 0))
    return pl.pallas_call(
        body, name=name, grid=(t // tm,), in_specs=[row, vec, row],
        out_specs=[row, row, vec, pl.BlockSpec((8, LANES), lambda i: (0, 0))],
        out_shape=[jax.ShapeDtypeStruct((t, d), F32), jax.ShapeDtypeStruct((t, d), BF16),
                   jax.ShapeDtypeStruct((1, d), F32), jax.ShapeDtypeStruct((8, LANES), F32)],
        compiler_params=_params(("arbitrary",)),
    )(h, g, target)


def _pad_cols(w, n):
    return jnp.pad(w, ((0, 0), (0, n - w.shape[1])))


def _layout_weights(w):
    z = lambda r, c: jnp.zeros((r, c), F32)
    wi = w["w_in_ab"]
    win = jnp.concatenate([wi[:, :384], z(1024, 64), wi[:, 384:416], z(1024, 32), wi[:, 416:]], axis=1)
    wq = jnp.pad(w["w_q_b"].reshape(MLA_Q_RANK, MLA_HEADS, 96), ((0, 0), (0, 0), (0, 32))).reshape(MLA_Q_RANK, 1024)
    kv3 = w["w_kv_b"].reshape(MLA_KV_RANK, MLA_HEADS, 128)
    wk = jnp.pad(kv3[..., :MLA_NOPE], ((0, 0), (0, 0), (0, 64))).reshape(MLA_KV_RANK, 1024)
    wv = kv3[..., MLA_NOPE:].reshape(MLA_KV_RANK, 512)
    pw = w["pool_w"]
    rows = []
    for g in range(4):
        rows.append(jnp.concatenate([pw[g] if j == g else z(128, 128) for j in range(4)], axis=1))
    wpool = jnp.concatenate(rows, axis=0)
    half = MLA_ROPE // 2
    inv = 1.0 / (ROPE_THETA ** (jnp.arange(half, dtype=F32) / half))
    inv_lane = jnp.concatenate([jnp.zeros((MLA_NOPE,), F32), inv, inv, jnp.zeros((32,), F32)]).reshape(1, LANES)
    return dict(
        win=win.astype(BF16), wq=wq.astype(BF16), wk=wk.astype(BF16), wv=wv.astype(BF16), wpool=wpool.astype(BF16),
        wout_ab=w["w_out_ab"].astype(BF16), winc=_pad_cols(w["w_in_c"], IN_C_PAD).astype(BF16),
        wout_c=w["w_out_c"].astype(BF16), conv_w=w["conv_w"], a_log=_pad_cols(w["a_log"], LANES),
        dt_bias=_pad_cols(w["dt_bias"], LANES), inv_lane=inv_lane,
        norm_ab=w["norm_ab"], q_a_norm=w["q_a_norm"], kv_a_norm=w["kv_a_norm"], pool_scale=w["pool_scale"],
        norm_c=w["norm_c"], o_norm=w["o_norm"], final_norm=w["final_norm"],
    )


def _unlayout_grads(g):
    dwin = g["win"]
    dkv = jnp.concatenate([g["wk"].reshape(MLA_KV_RANK, MLA_HEADS, 128)[..., :MLA_NOPE],
                           g["wv"].reshape(MLA_KV_RANK, MLA_HEADS, MLA_V)], axis=-1).reshape(MLA_KV_RANK, 1024)
    return dict(
        norm_ab=g["norm_ab"],
        w_in_ab=jnp.concatenate([dwin[:, :384], dwin[:, 448:480], dwin[:, 512:]], axis=1),
        q_a_norm=g["q_a_norm"],
        w_q_b=g["wq"].reshape(MLA_Q_RANK, MLA_HEADS, 128)[..., :96].reshape(MLA_Q_RANK, 768),
        kv_a_norm=g["kv_a_norm"],
        w_kv_b=dkv,
        pool_w=jnp.stack([g["wpool"][i * 128:(i + 1) * 128, i * 128:(i + 1) * 128] for i in range(4)]),
        pool_scale=g["pool_scale"],
        w_out_ab=g["wout_ab"],
        norm_c=g["norm_c"],
        w_in_c=g["winc"][:, :4112],
        conv_w=g["conv_w"],
        a_log=g["a_log"][:, :GDN_HEADS],
        dt_bias=g["dt_bias"][:, :GDN_HEADS],
        o_norm=g["o_norm"],
        w_out_c=g["wout_c"],
        final_norm=g["final_norm"],
    )


def _local_step(x, pos, target, lw):
    mm = _matmul
    hn = _rms_fwd(x, lw["norm_ab"], name="rms_ab")
    proj = mm(hn, lw["win"], "nn", name="in_ab")
    qn, kvn, kr, d, cos_t, sin_t = _ab_prep(proj, pos, lw["inv_lane"], lw["q_a_norm"], lw["kv_a_norm"], name="ab_prep")
    qraw = mm(qn, lw["wq"], "nn", name="q_up")
    kvk = mm(kvn, lw["wk"], "nn", name="k_up")
    v = mm(kvn, lw["wv"], "nn", name="v_up", out_dtype=BF16)
    ybraw = mm(d, lw["wpool"], "nn", name="pool_mix")
    q, k = _qk_rope(qraw, kvk, kr, cos_t, sin_t, name="qk_rope")
    o, lse = _attn_fwd(q, k, v, name="attn_fwd")
    y = _gate_fwd(o, ybraw, proj, lw["pool_scale"], name="gate_ab")
    h1 = mm(y, lw["wout_ab"], "nn", name="out_ab", add=x)
    hn1 = _rms_fwd(h1, lw["norm_c"], name="rms_c")
    proj_c = mm(hn1, lw["winc"], "nn", name="in_c")
    q2, k2, v2, gb, bb, gt = _c_prep(proj_c, lw["conv_w"], lw["a_log"], lw["dt_bias"], name="c_prep")
    gt = gt.reshape(GDN_HEADS, 1, gt.shape[1])
    o2, states = _gdn_fwd(q2, k2, v2, gb, gt, bb, name="gdn_fwd")
    y2 = _o_gate_fwd(o2, proj_c, lw["o_norm"], name="gate_c")
    h2 = mm(y2, lw["wout_c"], "nn", name="out_c", add=h1)
    dh2, dh2b, d_final, loss = _loss_head(h2, lw["final_norm"], target, name="loss_head")
    g = {"final_norm": d_final}
    dy2 = mm(dh2b, lw["wout_c"], "nt", name="out_c_dx")
    g["wout_c"] = mm(y2, dh2b, "tn", name="out_c_dw")
    do2, dz2, g["o_norm"] = _o_gate_bwd(dy2, o2, proj_c, lw["o_norm"], name="gate_c_bwd")
    dq2, dk2, dv2, dgb, dbb = _gdn_bwd(q2, k2, v2, gb, gt, bb, do2, states, name="gdn_bwd")
    dproj_c, g["conv_w"], g["a_log"], g["dt_bias"] = _c_prep_bwd(
        proj_c, lw["conv_w"], lw["a_log"], lw["dt_bias"], dq2, dk2, dv2, dgb, dbb, dz2, name="c_prep_bwd")
    dhn1 = mm(dproj_c, lw["winc"], "nt", name="in_c_dx")
    g["winc"] = mm(hn1, dproj_c, "tn", name="in_c_dw")
    dh1, dh1b, g["norm_c"] = _rms_bwd(h1, lw["norm_c"], dhn1, dh2, name="rms_c_bwd", with_bf16=True)
    dy = mm(dh1b, lw["wout_ab"], "nt", name="out_ab_dx")
    g["wout_ab"] = mm(y, dh1b, "tn", name="out_ab_dw")
    do, delta, dyb, dz, g["pool_scale"] = _gate_bwd(dy, o, ybraw, proj, lw["pool_scale"], name="gate_ab_bwd")
    dq, dk, dv = _attn_bwd(q, k, v, do, lse, delta, name="attn_bwd")
    dd = mm(dyb, lw["wpool"], "nt", name="pool_mix_dx")
    g["wpool"] = mm(d, dyb, "tn", name="pool_mix_dw")
    dqraw, dkb, dkr = _qk_rope_bwd(dq, dk, cos_t, sin_t, name="qk_rope_bwd")
    dqn = mm(dqraw, lw["wq"], "nt", name="q_up_dx")
    g["wq"] = mm(qn, dqraw, "tn", name="q_up_dw")
    dkvn_k = mm(dkb, lw["wk"], "nt", name="k_up_dx")
    dkvn_v = mm(dv, lw["wv"], "nt", name="v_up_dx")
    g["wk"] = mm(kvn, dkb, "tn", name="k_up_dw")
    g["wv"] = mm(kvn, dv, "tn", name="v_up_dw")
    dproj, g["q_a_norm"], g["kv_a_norm"] = _ab_prep_bwd(
        proj, lw["q_a_norm"], lw["kv_a_norm"], dqn, dkvn_k, dkvn_v, dkr, dd, dz, name="ab_prep_bwd")
    dhn = mm(dproj, lw["win"], "nt", name="in_ab_dx")
    g["win"] = mm(hn, dproj, "tn", name="in_ab_dw")
    dx, g["norm_ab"] = _rms_bwd(x, lw["norm_ab"], dhn, dh1, name="rms_ab_bwd", with_bf16=False)
    return loss, dx, g


_HBM = pl.BlockSpec(memory_space=pltpu.HBM)


def _place():
    return lax.axis_index("x"), lax.axis_index("y"), lax.axis_index("c")


def _flip(v, f):
    return 1 - v if f else v


_CHIP_FLIPS = ((1, 0), (0, 1), (1, 1))
_DEV_FLIPS = tuple((fx, fy, fc) for fx in (0, 1) for fy in (0, 1) for fc in (0, 1) if fx or fy or fc)


def _rcopy(src, dst, send_sems, recv_sems, k, to):
    return pltpu.make_async_remote_copy(src_ref=src, dst_ref=dst, send_sem=send_sems.at[k], recv_sem=recv_sems.at[k],
                                        device_id=to, device_id_type=MESH)


def _gather_weights(wb, ws):
    _, rh, _ = wb.shape
    rs = ws.shape[0]

    def body(wb_ref, ws_ref, gb_ref, gs_ref, send_sems, recv_sems, local_sems):
        x, y, c = _place()
        j0 = 2 * x + y
        sib = (x, y, 1 - c)
        chips = [(_flip(x, fx), _flip(y, fy)) for fx, fy in _CHIP_FLIPS]
        own_b = pltpu.make_async_copy(wb_ref, gb_ref.at[j0], local_sems.at[0])
        own_s = pltpu.make_async_copy(ws_ref, gs_ref.at[j0], local_sems.at[1])
        own_b.start()
        own_s.start()
        sends = []
        for k, (px, py) in enumerate(chips):
            sends.append(_rcopy(wb_ref.at[c], gb_ref.at[j0, c], send_sems, recv_sems, k, (px, py, c)))
            sends.append(_rcopy(ws_ref, gs_ref.at[j0], send_sems, recv_sems, 6 + k, (px, py, c)))
        for cp in sends:
            cp.start()
        for k, (px, py) in enumerate(chips):
            jk = 2 * px + py
            _rcopy(wb_ref.at[c], gb_ref.at[jk, c], send_sems, recv_sems, k, (px, py, c)).wait_recv()
            fwd = _rcopy(gb_ref.at[jk, c], gb_ref.at[jk, c], send_sems, recv_sems, 3 + k, sib)
            fwd.start()
            sends.append(fwd)
        for k, (px, py) in enumerate(chips):
            jk = 2 * px + py
            _rcopy(wb_ref.at[c], gb_ref.at[jk, 1 - c], send_sems, recv_sems, 3 + k, sib).wait_recv()
            _rcopy(ws_ref, gs_ref.at[jk], send_sems, recv_sems, 6 + k, (px, py, c)).wait_recv()
        for cp in sends:
            cp.wait_send()
        own_b.wait()
        own_s.wait()

    return pl.pallas_call(
        body, name="gather_weights", in_specs=[_HBM, _HBM], out_specs=[_HBM, _HBM],
        out_shape=[jax.ShapeDtypeStruct((4, 2, rh, LANES), BF16), jax.ShapeDtypeStruct((4, rs, LANES), F32)],
        scratch_shapes=[pltpu.SemaphoreType.DMA((9,)), pltpu.SemaphoreType.DMA((9,)), pltpu.SemaphoreType.DMA((2,))],
    )(wb, ws)


def _sibling_swap(a, *, name):
    def body(a_ref, o_ref, send_sem, recv_sem):
        x, y, c = _place()
        cp = pltpu.make_async_remote_copy(src_ref=a_ref, dst_ref=o_ref, send_sem=send_sem, recv_sem=recv_sem,
                                          device_id=(x, y, 1 - c), device_id_type=MESH)
        cp.start()
        cp.wait()

    return pl.pallas_call(
        body, name=name, in_specs=[_HBM], out_specs=_HBM, out_shape=jax.ShapeDtypeStruct(a.shape, a.dtype),
        scratch_shapes=[pltpu.SemaphoreType.DMA, pltpu.SemaphoreType.DMA],
    )(a)


def _chip_exchange(p, small):
    _, rh, _ = p.shape
    rs = small.shape[0]

    def body(p_ref, s_ref, l_ref, ls_ref, send_sems, recv_sems, local_sems):
        x, y, c = _place()
        j0 = 2 * x + y
        d0 = 2 * j0 + c
        own_p = pltpu.make_async_copy(p_ref.at[j0], l_ref.at[j0], local_sems.at[0])
        own_s = pltpu.make_async_copy(s_ref, ls_ref.at[d0], local_sems.at[1])
        own_p.start()
        own_s.start()
        sends = []
        for k, (fx, fy) in enumerate(_CHIP_FLIPS):
            px, py = _flip(x, fx), _flip(y, fy)
            sends.append(_rcopy(p_ref.at[2 * px + py], l_ref.at[j0], send_sems, recv_sems, k, (px, py, c)))
        for k, (fx, fy, fc) in enumerate(_DEV_FLIPS):
            peer = (_flip(x, fx), _flip(y, fy), _flip(c, fc))
            sends.append(_rcopy(s_ref, ls_ref.at[d0], send_sems, recv_sems, 3 + k, peer))
        for cp in sends:
            cp.start()
        for k, (fx, fy) in enumerate(_CHIP_FLIPS):
            px, py = _flip(x, fx), _flip(y, fy)
            _rcopy(p_ref.at[j0], l_ref.at[2 * px + py], send_sems, recv_sems, k, (px, py, c)).wait_recv()
        for k, (fx, fy, fc) in enumerate(_DEV_FLIPS):
            px, py, pc = _flip(x, fx), _flip(y, fy), _flip(c, fc)
            _rcopy(s_ref, ls_ref.at[4 * px + 2 * py + pc], send_sems, recv_sems, 3 + k, (px, py, pc)).wait_recv()
        for cp in sends:
            cp.wait_send()
        own_p.wait()
        own_s.wait()

    return pl.pallas_call(
        body, name="chip_exchange", in_specs=[_HBM, _HBM], out_specs=[_HBM, _HBM],
        out_shape=[jax.ShapeDtypeStruct((4, rh, LANES), F32), jax.ShapeDtypeStruct((8, rs, LANES), F32)],
        scratch_shapes=[pltpu.SemaphoreType.DMA((10,)), pltpu.SemaphoreType.DMA((10,)), pltpu.SemaphoreType.DMA((2,))],
    )(p, small)


def _sum_slots(a, *, name):
    n, rows, _ = a.shape
    tr = _tile(rows, 1024)

    def body(a_ref, o_ref):
        acc = a_ref[0]
        for s in range(1, n):
            acc = acc + a_ref[s]
        o_ref[...] = acc

    return pl.pallas_call(
        body, name=name, grid=(rows // tr,), in_specs=[pl.BlockSpec((n, tr, LANES), lambda i: (0, i, 0))],
        out_specs=pl.BlockSpec((tr, LANES), lambda i: (i, 0)), out_shape=jax.ShapeDtypeStruct((rows, LANES), F32),
        compiler_params=_params(("parallel",)),
    )(a)


def _adam_update(g_ref, w_ref, m_ref, v_ref, d_ref, mo_ref, vo_ref):
    gv = g_ref[...]
    mn = ADAM_B1 * m_ref[...] + (1.0 - ADAM_B1) * gv
    vn = ADAM_B2 * v_ref[...] + (1.0 - ADAM_B2) * (gv * gv)
    mo_ref[...] = mn
    vo_ref[...] = vn
    c1 = 1.0 - ADAM_B1 ** ADAM_STEP
    c2 = 1.0 - ADAM_B2 ** ADAM_STEP
    d_ref[...] = -ADAM_LR * ((mn / c1) / (jnp.sqrt(vn / c2) + ADAM_EPS) + ADAM_WD * w_ref[...])


def _adamw_rows(g, w, m, v, *, name):
    rows, cols = g.shape
    tr = _tile(rows, 512)

    def body(*refs):
        _adam_update(*refs)

    blk = pl.BlockSpec((tr, cols), lambda i: (i, 0))
    out = jax.ShapeDtypeStruct((rows, cols), F32)
    return pl.pallas_call(
        body, name=name, grid=(rows // tr,), in_specs=[blk] * 4, out_specs=[blk] * 3, out_shape=[out] * 3,
        compiler_params=_params(("parallel",)),
    )(g, w, m, v)


def _adamw_small(gs, ws, ms, vs, *, name):
    n = len(gs)

    def body(*refs):
        ins, outs = refs[:4 * n], refs[4 * n:]
        for i in range(n):
            _adam_update(ins[i], ins[n + i], ins[2 * n + i], ins[3 * n + i], *outs[3 * i:3 * i + 3])

    out_shape = [jax.ShapeDtypeStruct(g.shape, F32) for g in gs for _ in range(3)]
    vmem = pl.BlockSpec(memory_space=pltpu.VMEM)
    return pl.pallas_call(
        body, name=name, in_specs=[vmem] * (4 * n), out_specs=[vmem] * (3 * n), out_shape=out_shape,
        compiler_params=pltpu.CompilerParams(vmem_limit_bytes=VMEM_LIMIT),
    )(*gs, *ws, *ms, *vs)


_ADAM_ROWWISE = ("w_in_ab", "w_q_b", "w_kv_b", "w_out_ab", "w_in_c", "w_out_c")


_SHARDED = (("w_in_ab", (1024, 488), 1), ("w_q_b", (256, 192), 1), ("w_kv_b", (128, 256), 1),
            ("w_out_ab", (256, 1024), 0), ("w_in_c", (1024, 1028), 1), ("w_out_c", (256, 1024), 0),
            ("conv_w", (4, 768), 1), ("norm_c", (1, 256), 1))
_N_BF16 = 6
_REPLICATED = (("norm_ab", (1, 1024)), ("q_a_norm", (1, 256)), ("kv_a_norm", (1, 128)), ("pool_w", (4, 128, 128)),
               ("pool_scale", (1, 512)), ("a_log", (1, 8)), ("dt_bias", (1, 8)), ("o_norm", (1, 128)),
               ("final_norm", (1, 1024)))
_ALL_NAMES = ("norm_ab", "w_in_ab", "q_a_norm", "w_q_b", "kv_a_norm", "w_kv_b", "pool_w", "pool_scale", "w_out_ab",
              "norm_c", "w_in_c", "conv_w", "a_log", "dt_bias", "o_norm", "w_out_c", "final_norm")


def _rows_of(shape):
    return max(1, math.prod(shape) // LANES)


def _as_rows(a):
    n = a.size
    if n % LANES:
        a = jnp.pad(a.reshape(1, n), ((0, 0), (0, LANES - n % LANES)))
    return a.reshape(-1, LANES)


def _pack(parts, total_rows):
    rows = jnp.concatenate([_as_rows(p) for p in parts], axis=0)
    return jnp.pad(rows, ((0, total_rows - rows.shape[0]), (0, 0)))


def _unpack(packed, spec):
    out, off = [], 0
    lead = packed.shape[:-2]
    for shape in spec:
        r = _rows_of(shape)
        blk = packed[..., off:off + r, :].reshape(lead + (r * LANES,))[..., :math.prod(shape)]
        out.append(blk.reshape(lead + tuple(shape)))
        off += r
    return out


def _round_up(n, m):
    return -(-n // m) * m


_SH_SHAPES = tuple(s for _, s, _ in _SHARDED)
_RB = sum(_rows_of(s) for s in _SH_SHAPES[:_N_BF16])
_RS = _round_up(sum(_rows_of(s) for s in _SH_SHAPES[_N_BF16:]), 8)
_RG = _round_up(sum(_rows_of(s) for s in _SH_SHAPES), 16)
_REP_SHAPES = tuple(s for _, s in _REPLICATED)
_RR = _round_up(sum(_rows_of(s) for s in _REP_SHAPES) + 1, 8)


def kernel(x, positions, norm_ab, w_in_ab, q_a_norm, w_q_b, kv_a_norm, w_kv_b, pool_w, pool_scale, w_out_ab, norm_c, w_in_c, conv_w, a_log, dt_bias, o_norm, w_out_c, final_norm, loss_target, m_norm_ab, m_w_in_ab, m_q_a_norm, m_w_q_b, m_kv_a_norm, m_w_kv_b, m_pool_w, m_pool_scale, m_w_out_ab, m_norm_c, m_w_in_c, m_conv_w, m_a_log, m_dt_bias, m_o_norm, m_w_out_c, m_final_norm, v_norm_ab, v_w_in_ab, v_q_a_norm, v_w_q_b, v_kv_a_norm, v_w_kv_b, v_pool_w, v_pool_scale, v_w_out_ab, v_norm_c, v_w_in_c, v_conv_w, v_a_log, v_dt_bias, v_o_norm, v_w_out_c, v_final_norm):
    given = dict(locals())
    c = lax.axis_index("c")
    t = x.shape[1]

    def shard_of(prefix, name):
        a = given[prefix + name]
        return a.reshape(a.shape[1:]) if a.ndim > 2 else a.reshape(1, -1)

    sh = [shard_of("", n) for n, _, _ in _SHARDED]
    wb = _pack([a.astype(BF16) for a in sh[:_N_BF16]], _RB).reshape(2, _RB // 2, LANES)
    ws = _pack(sh[_N_BF16:], _RS)
    gb, gs = _gather_weights(wb, ws)
    parts = _unpack(gb.reshape(4, _RB, LANES), _SH_SHAPES[:_N_BF16]) + _unpack(gs, _SH_SHAPES[_N_BF16:])
    full = {}
    for (name, _, axis), p in zip(_SHARDED, parts):
        full[name] = jnp.concatenate([p[j] for j in range(4)], axis=axis)
    for name, _ in _REPLICATED:
        full[name] = shard_of("", name)
    lw = _layout_weights(full)

    loss_tile, dx, g = _local_step(x[0], positions.reshape(t, 1), loss_target[0], lw)
    grads = _unlayout_grads(g)

    per_chip = []
    for j in range(4):
        pieces = []
        for name, shape, axis in _SHARDED:
            n = shape[axis]
            pieces.append(lax.slice_in_dim(grads[name], j * n, (j + 1) * n, axis=axis))
        per_chip.append(_pack(pieces, _RG))
    gfull = jnp.stack(per_chip).reshape(4, 2, _RG // 2, LANES)
    mine = lax.dynamic_index_in_dim(gfull, c, axis=1, keepdims=False)
    other = lax.dynamic_index_in_dim(gfull, 1 - c, axis=1, keepdims=False)
    from_sibling = _sibling_swap(other, name="core_swap_partial")
    pair = jnp.stack([mine, from_sibling]).reshape(2, 4 * (_RG // 2), LANES)
    chip_sum = _sum_slots(pair, name="core_sum").reshape(4, _RG // 2, LANES)
    small = _pack([grads[n] for n, _ in _REPLICATED] + [loss_tile[0:1, :]], _RR)
    landed, small_all = _chip_exchange(chip_sum, small)
    my_half = _sum_slots(landed, name="chip_sum")
    small_sum = _sum_slots(small_all, name="small_sum")
    sib_half = _sibling_swap(my_half, name="core_swap_sum")
    g_shard = jnp.where(c == 0, jnp.concatenate([my_half, sib_half]), jnp.concatenate([sib_half, my_half]))

    gnat = dict(zip([n for n, _, _ in _SHARDED], _unpack(g_shard, _SH_SHAPES)))
    gnat.update(zip([n for n, _ in _REPLICATED], _unpack(small_sum, _REP_SHAPES)))
    res = {}
    small_names = [n for n in _ALL_NAMES if n not in _ADAM_ROWWISE]
    for name in _ADAM_ROWWISE:
        out = _adamw_rows(gnat[name], shard_of("", name), shard_of("m_", name), shard_of("v_", name), name="adamw_" + name)
        res["delta", name], res["m", name], res["v", name] = out
    out = _adamw_small([gnat[n] for n in small_names], [shard_of("", n) for n in small_names],
                       [shard_of("m_", n) for n in small_names], [shard_of("v_", n) for n in small_names],
                       name="adamw_small")
    for i, name in enumerate(small_names):
        res["delta", name], res["m", name], res["v", name] = out[3 * i:3 * i + 3]
    for name in _ALL_NAMES:
        res["grad", name] = gnat[name]
    res = {k: a.reshape(given[k[1]].shape) for k, a in res.items()}
    loss = small_sum[sum(_rows_of(s) for s in _REP_SHAPES), 0]
    outs = [loss, dx.reshape(x.shape)]
    for key in ("grad", "delta", "m", "v"):
        outs += [res[key, n] for n in _ALL_NAMES]
    return tuple(outs)
```

```python
import functools
import math

import jax
import jax.numpy as jnp
from jax import lax
from jax.experimental import pallas as pl
from jax.experimental.pallas import tpu as pltpu

F32 = jnp.float32
BF16 = jnp.bfloat16
HI = lax.Precision.HIGHEST
MESH = pl.DeviceIdType.MESH

RMS_EPS = 1e-6
D_MODEL = 1024
MLA_HEADS = 8
MLA_Q_RANK = 256
MLA_KV_RANK = 128
MLA_NOPE = 64
MLA_ROPE = 32
MLA_V = 64
ROPE_THETA = 10000.0
POOL_WINDOWS = (2, 4, 8, 16)
POOL_GROUP = 128
POOL_WIDTH = 512
POOL_HALO = 16
GDN_HEADS = 8
GDN_DK = 128
CONV_WIDTH = 4
CONV_HALO = 8
CHUNK = 64
IN_AB_PAD = 2048
IN_C_PAD = 4224
ATT_SCALE = (MLA_NOPE + MLA_ROPE) ** -0.5

ADAM_LR = 0.001
ADAM_B1 = 0.9
ADAM_B2 = 0.999
ADAM_EPS = 1e-08
ADAM_WD = 0.01
ADAM_STEP = 10

LANES = 128
VMEM_LIMIT = 56 * 1024 * 1024

ROW_TILE = 256
ATT_TILE = 1024
GDN_TILE = 256
MM_TILE = (1024, 1408, 2048)

NN = (((1,), (0,)), ((), ()))
NT = (((1,), (1,)), ((), ()))
TN = (((0,), (0,)), ((), ()))


def _dot(a, b, dims=NN, prec=None):
    return lax.dot_general(a, b, dims, precision=prec, preferred_element_type=F32)


def _tile(n, pref):
    if n <= pref:
        return n
    step = LANES if pref >= LANES else 8
    for t in range(pref - pref % step, 0, -step):
        if n % t == 0:
            return t
    return n


def _params(sem):
    return pltpu.CompilerParams(dimension_semantics=sem, vmem_limit_bytes=VMEM_LIMIT)


def _sigmoid(x):
    return 1.0 / (1.0 + jnp.exp(-x))


def _softplus(x):
    return jnp.maximum(x, 0.0) + jnp.log(1.0 + jnp.exp(-jnp.abs(x)))


def _matmul(a, b, mode, *, name, out_dtype=F32, add=None):
    if mode == "nn":
        (m, k), (k2, n) = a.shape, b.shape
    elif mode == "nt":
        (m, k), (n, k2) = a.shape, b.shape
    else:
        (k, m), (k2, n) = a.shape, b.shape
    assert k == k2, (a.shape, b.shape, mode)
    tm, tn, tk = _tile(m, MM_TILE[0]), _tile(n, MM_TILE[1]), _tile(k, MM_TILE[2])
    nk = k // tk
    if mode == "tn":
        a_spec = pl.BlockSpec((tk, tm), lambda i, j, kk: (kk, i))
    else:
        a_spec = pl.BlockSpec((tm, tk), lambda i, j, kk: (i, kk))
    if mode == "nt":
        b_spec = pl.BlockSpec((tn, tk), lambda i, j, kk: (j, kk))
    else:
        b_spec = pl.BlockSpec((tk, tn), lambda i, j, kk: (kk, j))
    o_spec = pl.BlockSpec((tm, tn), lambda i, j, kk: (i, j))
    dims = {"nn": NN, "nt": NT, "tn": TN}[mode]
    has_add = add is not None

    def body(*refs):
        a_ref, b_ref = refs[0], refs[1]
        add_ref = refs[2] if has_add else None
        o_ref = refs[3] if has_add else refs[2]

        def finish(o):
            if has_add:
                o = o + add_ref[...]
            o_ref[...] = o.astype(out_dtype)

        if nk == 1:
            finish(_dot(a_ref[...], b_ref[...], dims))
            return
        acc = refs[-1]
        kk = pl.program_id(2)

        @pl.when(kk == 0)
        def _():
            acc[...] = jnp.zeros_like(acc)

        acc[...] += _dot(a_ref[...], b_ref[...], dims)

        @pl.when(kk == nk - 1)
        def _():
            finish(acc[...])

    in_specs = [a_spec, b_spec] + ([o_spec] if has_add else [])
    args = (a, b) + ((add,) if has_add else ())
    return pl.pallas_call(
        body, name=name, grid=(m // tm, n // tn, nk), in_specs=in_specs, out_specs=o_spec,
        out_shape=jax.ShapeDtypeStruct((m, n), out_dtype),
        scratch_shapes=[pltpu.VMEM((tm, tn), F32)] if nk > 1 else [],
        compiler_params=_params(("parallel", "parallel", "arbitrary")),
    )(*args)


def _rms_fwd(h, g, *, name):
    t, d = h.shape
    tm = _tile(t, 2 * ROW_TILE)

    def body(h_ref, g_ref, o_ref):
        x = h_ref[...]
        r = lax.rsqrt(jnp.mean(x * x, axis=-1, keepdims=True) + RMS_EPS)
        o_ref[...] = (x * r * g_ref[...]).astype(BF16)

    return pl.pallas_call(
        body, name=name, grid=(t // tm,),
        in_specs=[pl.BlockSpec((tm, d), lambda i: (i, 0)), pl.BlockSpec((1, d), lambda i: (0, 0))],
        out_specs=pl.BlockSpec((tm, d), lambda i: (i, 0)),
        out_shape=jax.ShapeDtypeStruct((t, d), BF16), compiler_params=_params(("parallel",)),
    )(h, g)


def _rms_bwd(h, g, dy, dres, *, name, with_bf16):
    t, d = h.shape
    tm = _tile(t, 2 * ROW_TILE)

    def body(h_ref, g_ref, dy_ref, dres_ref, *outs):
        i = pl.program_id(0)
        dh_ref, dg_ref = outs[0], outs[-1]
        x = h_ref[...]
        r = lax.rsqrt(jnp.mean(x * x, axis=-1, keepdims=True) + RMS_EPS)
        xh = x * r
        dyv = dy_ref[...].astype(F32)
        dxh = dyv * g_ref[...]
        dx = r * (dxh - xh * jnp.mean(dxh * xh, axis=-1, keepdims=True))
        dh = dres_ref[...] + dx
        dh_ref[...] = dh
        if with_bf16:
            outs[1][...] = dh.astype(BF16)

        @pl.when(i == 0)
        def _():
            dg_ref[...] = jnp.zeros_like(dg_ref)

        dg_ref[...] += jnp.sum(dyv * xh, axis=0, keepdims=True)

    row = pl.BlockSpec((tm, d), lambda i: (i, 0))
    vec = pl.BlockSpec((1, d), lambda i: (0, 0))
    out_shape = [jax.ShapeDtypeStruct((t, d), F32)]
    out_specs = [row]
    if with_bf16:
        out_shape.append(jax.ShapeDtypeStruct((t, d), BF16))
        out_specs.append(row)
    out_shape.append(jax.ShapeDtypeStruct((1, d), F32))
    out_specs.append(vec)
    return pl.pallas_call(
        body, name=name, grid=(t // tm,), in_specs=[row, vec, row, row], out_specs=out_specs,
        out_shape=out_shape, compiler_params=_params(("arbitrary",)),
    )(h, g, dy, dres)


def _rope_partner(x):
    lane = lax.broadcasted_iota(jnp.int32, x.shape, 1)
    swapped = jnp.where(lane < MLA_NOPE + MLA_ROPE // 2, pltpu.roll(x, LANES - 16, 1), pltpu.roll(x, 16, 1))
    return jnp.where((lane >= MLA_NOPE) & (lane < MLA_NOPE + MLA_ROPE), swapped, 0.0)


def _pool_counts(row0, tm, w):
    t_idx = row0 + lax.broadcasted_iota(jnp.int32, (tm, POOL_GROUP), 0)
    return jnp.minimum(t_idx + 1, w).astype(F32)


def _ab_prep(proj, pos, inv_freq, q_a_norm, kv_a_norm, *, name):
    t = proj.shape[0]
    tm = _tile(t, ROW_TILE)
    hb = tm // POOL_HALO

    def body(p_ref, halo_ref, pos_ref, inv_ref, qg_ref, kg_ref, qn_ref, kvn_ref, kr_ref, d_ref, cos_ref, sin_ref, ext):
        i = pl.program_id(0)
        ql = p_ref[:, 0:MLA_Q_RANK]
        r = lax.rsqrt(jnp.mean(ql * ql, axis=-1, keepdims=True) + RMS_EPS)
        qn_ref[...] = (ql * r * qg_ref[...]).astype(BF16)
        kl = p_ref[:, MLA_Q_RANK:MLA_Q_RANK + MLA_KV_RANK]
        r = lax.rsqrt(jnp.mean(kl * kl, axis=-1, keepdims=True) + RMS_EPS)
        kvn_ref[...] = (kl * r * kg_ref[...]).astype(BF16)
        ang = pos_ref[...].astype(F32) * inv_ref[...]
        lane = lax.broadcasted_iota(jnp.int32, (tm, LANES), 1)
        in_rope = (lane >= MLA_NOPE) & (lane < MLA_NOPE + MLA_ROPE)
        cos_t = jnp.where(in_rope, jnp.cos(ang), 1.0)
        sin_t = jnp.where(in_rope, jnp.sin(ang), 0.0)
        sin_t = jnp.where(lane < MLA_NOPE + MLA_ROPE // 2, -sin_t, sin_t)
        cos_ref[...] = cos_t
        sin_ref[...] = sin_t
        kr = p_ref[:, 384:512]
        kr_ref[...] = kr * cos_t + _rope_partner(kr) * sin_t
        xp = p_ref[:, 512:1024]
        ext[0:POOL_HALO, :] = jnp.where(i > 0, halo_ref[...], 0.0)
        ext[POOL_HALO:POOL_HALO + tm, :] = xp
        for g, w in enumerate(POOL_WINDOWS):
            lo = g * POOL_GROUP
            acc = ext[POOL_HALO:POOL_HALO + tm, lo:lo + POOL_GROUP]
            for s in range(1, w):
                acc = acc + ext[POOL_HALO - s:POOL_HALO - s + tm, lo:lo + POOL_GROUP]
            cnt = _pool_counts(i * tm, tm, w)
            d_ref[:, lo:lo + POOL_GROUP] = (acc / cnt - xp[:, lo:lo + POOL_GROUP]).astype(BF16)

    row = lambda w: pl.BlockSpec((tm, w), lambda i: (i, 0))
    vec = lambda w: pl.BlockSpec((1, w), lambda i: (0, 0))
    return pl.pallas_call(
        body, name=name, grid=(t // tm,),
        in_specs=[row(1024), pl.BlockSpec((POOL_HALO, POOL_WIDTH), lambda i: (jnp.maximum(i * hb - 1, 0), 1)),
                  pl.BlockSpec((tm, 1), lambda i: (i, 0)), vec(LANES), vec(MLA_Q_RANK), vec(MLA_KV_RANK)],
        out_specs=[row(MLA_Q_RANK), row(MLA_KV_RANK), row(LANES), row(POOL_WIDTH), row(LANES), row(LANES)],
        out_shape=[jax.ShapeDtypeStruct((t, MLA_Q_RANK), BF16), jax.ShapeDtypeStruct((t, MLA_KV_RANK), BF16),
                   jax.ShapeDtypeStruct((t, LANES), F32), jax.ShapeDtypeStruct((t, POOL_WIDTH), BF16),
                   jax.ShapeDtypeStruct((t, LANES), F32), jax.ShapeDtypeStruct((t, LANES), F32)],
        scratch_shapes=[pltpu.VMEM((tm + POOL_HALO, POOL_WIDTH), F32)],
        compiler_params=_params(("parallel",)),
    )(proj, proj, pos, inv_freq, q_a_norm, kv_a_norm)


def _qk_rope(qraw, kvk, kr, cos_t, sin_t, *, name):
    t = qraw.shape[0]
    tm = _tile(t, 2 * ROW_TILE)

    def body(q_ref, k_ref, kr_ref, c_ref, s_ref, qo_ref, ko_ref):
        c, s, krv = c_ref[...], s_ref[...], kr_ref[...]
        for h in range(MLA_HEADS):
            sl = slice(h * LANES, (h + 1) * LANES)
            q = q_ref[:, sl]
            qo_ref[:, sl] = ((q * c + _rope_partner(q) * s) * ATT_SCALE).astype(BF16)
            ko_ref[:, sl] = (k_ref[:, sl] + krv).astype(BF16)

    row = lambda w: pl.BlockSpec((tm, w), lambda i: (i, 0))
    return pl.pallas_call(
        body, name=name, grid=(t // tm,), in_specs=[row(1024), row(1024), row(LANES), row(LANES), row(LANES)],
        out_specs=[row(1024), row(1024)],
        out_shape=[jax.ShapeDtypeStruct((t, 1024), BF16), jax.ShapeDtypeStruct((t, 1024), BF16)],
        compiler_params=_params(("parallel",)),
    )(qraw, kvk, kr, cos_t, sin_t)


def _qk_rope_bwd(dq, dk, cos_t, sin_t, *, name):
    t = dq.shape[0]
    tm = _tile(t, 2 * ROW_TILE)

    def body(dq_ref, dk_ref, c_ref, s_ref, dqo_ref, dko_ref, dkr_ref):
        c, s = c_ref[...], s_ref[...]
        lane = lax.broadcasted_iota(jnp.int32, (tm, LANES), 1)
        in_rope = (lane >= MLA_NOPE) & (lane < MLA_NOPE + MLA_ROPE)
        dkr = jnp.zeros((tm, LANES), F32)
        for h in range(MLA_HEADS):
            sl = slice(h * LANES, (h + 1) * LANES)
            g = dq_ref[:, sl]
            dqo_ref[:, sl] = ((g * c + _rope_partner(g * s)) * ATT_SCALE).astype(BF16)
            gk = dk_ref[:, sl]
            dko_ref[:, sl] = gk.astype(BF16)
            dkr = dkr + jnp.where(in_rope, gk, 0.0)
        dkr_ref[...] = dkr * c + _rope_partner(dkr * s)

    row = lambda w: pl.BlockSpec((tm, w), lambda i: (i, 0))
    return pl.pallas_call(
        body, name=name, grid=(t // tm,), in_specs=[row(1024), row(1024), row(LANES), row(LANES)],
        out_specs=[row(1024), row(1024), row(LANES)],
        out_shape=[jax.ShapeDtypeStruct((t, 1024), BF16), jax.ShapeDtypeStruct((t, 1024), BF16),
                   jax.ShapeDtypeStruct((t, LANES), F32)],
        compiler_params=_params(("parallel",)),
    )(dq, dk, cos_t, sin_t)


def _ab_prep_bwd(proj, q_a_norm, kv_a_norm, dqn, dkvn_k, dkvn_v, dkr, dd, dz, *, name):
    t = proj.shape[0]
    tm = _tile(t, ROW_TILE)
    hb = tm // POOL_HALO
    last_halo = t // POOL_HALO - 1
    nt = t // tm

    def body(p_ref, qg_ref, kg_ref, dqn_ref, dk1_ref, dk2_ref, dkr_ref, dd_ref, ddn_ref, dz_ref,
             dp_ref, dqg_ref, dkg_ref, ext):
        i = pl.program_id(0)

        @pl.when(i == 0)
        def _():
            dqg_ref[...] = jnp.zeros_like(dqg_ref)
            dkg_ref[...] = jnp.zeros_like(dkg_ref)

        def norm_bwd(x, g, dy, dg_ref):
            r = lax.rsqrt(jnp.mean(x * x, axis=-1, keepdims=True) + RMS_EPS)
            xh = x * r
            dxh = dy * g
            dg_ref[...] += jnp.sum(dy * xh, axis=0, keepdims=True)
            return r * (dxh - xh * jnp.mean(dxh * xh, axis=-1, keepdims=True))

        dql = norm_bwd(p_ref[:, 0:MLA_Q_RANK], qg_ref[...], dqn_ref[...], dqg_ref)
        dp_ref[:, 0:MLA_Q_RANK] = dql.astype(BF16)
        dkl = norm_bwd(p_ref[:, MLA_Q_RANK:384], kg_ref[...], dk1_ref[...] + dk2_ref[...], dkg_ref)
        dp_ref[:, MLA_Q_RANK:384] = dkl.astype(BF16)
        dp_ref[:, 384:512] = dkr_ref[...].astype(BF16)
        ddv = dd_ref[...]
        for g, w in enumerate(POOL_WINDOWS):
            lo = g * POOL_GROUP
            ext[0:tm, lo:lo + POOL_GROUP] = ddv[:, lo:lo + POOL_GROUP] / _pool_counts(i * tm, tm, w)
            nxt = ddn_ref[:, lo:lo + POOL_GROUP] / _pool_counts((i + 1) * tm, POOL_HALO, w)
            ext[tm:tm + POOL_HALO, lo:lo + POOL_GROUP] = jnp.where(i < nt - 1, nxt, 0.0)
        for g, w in enumerate(POOL_WINDOWS):
            lo = g * POOL_GROUP
            acc = ext[0:tm, lo:lo + POOL_GROUP]
            for s in range(1, w):
                acc = acc + ext[s:s + tm, lo:lo + POOL_GROUP]
            dp_ref[:, 512 + lo:512 + lo + POOL_GROUP] = (acc - ddv[:, lo:lo + POOL_GROUP]).astype(BF16)
        dp_ref[:, 1024:2048] = dz_ref[...]

    row = lambda w: pl.BlockSpec((tm, w), lambda i: (i, 0))
    vec = lambda w: pl.BlockSpec((1, w), lambda i: (0, 0))
    return pl.pallas_call(
        body, name=name, grid=(nt,),
        in_specs=[row(1024), vec(MLA_Q_RANK), vec(MLA_KV_RANK), row(MLA_Q_RANK), row(MLA_KV_RANK), row(MLA_KV_RANK),
                  row(LANES), row(POOL_WIDTH),
                  pl.BlockSpec((POOL_HALO, POOL_WIDTH), lambda i: (jnp.minimum((i + 1) * hb, last_halo), 0)),
                  row(1024)],
        out_specs=[row(IN_AB_PAD), vec(MLA_Q_RANK), vec(MLA_KV_RANK)],
        out_shape=[jax.ShapeDtypeStruct((t, IN_AB_PAD), BF16), jax.ShapeDtypeStruct((1, MLA_Q_RANK), F32),
                   jax.ShapeDtypeStruct((1, MLA_KV_RANK), F32)],
        scratch_shapes=[pltpu.VMEM((tm + POOL_HALO, POOL_WIDTH), F32)],
        compiler_params=_params(("arbitrary",)),
    )(proj, q_a_norm, kv_a_norm, dqn, dkvn_k, dkvn_v, dkr, dd, dd, dz)


def _gate_fwd(o, ybraw, proj, pool_scale, *, name):
    t = o.shape[0]
    tm = _tile(t, 2 * ROW_TILE)

    def body(o_ref, yb_ref, z_ref, ps_ref, y_ref):
        z = z_ref[...]
        sz = z * _sigmoid(z)
        y_ref[:, 0:512] = (o_ref[...] * sz[:, 0:512]).astype(BF16)
        y_ref[:, 512:1024] = (yb_ref[...] * ps_ref[...] * sz[:, 512:1024]).astype(BF16)

    row = lambda w: pl.BlockSpec((tm, w), lambda i: (i, 0))
    return pl.pallas_call(
        body, name=name, grid=(t // tm,),
        in_specs=[row(512), row(512), pl.BlockSpec((tm, 1024), lambda i: (i, 1)), pl.BlockSpec((1, 512), lambda i: (0, 0))],
        out_specs=row(1024), out_shape=jax.ShapeDtypeStruct((t, 1024), BF16), compiler_params=_params(("parallel",)),
    )(o, ybraw, proj, pool_scale)


def _gate_bwd(dy, o, ybraw, proj, pool_scale, *, name):
    t = o.shape[0]
    tm = _tile(t, ROW_TILE)

    def body(dy_ref, o_ref, yb_ref, z_ref, ps_ref, do_ref, dl_ref, dyb_ref, dz_ref, dps_ref):
        i = pl.program_id(0)
        z = z_ref[...]
        sg = _sigmoid(z)
        sz = z * sg
        dsz = sg * (1.0 + z * (1.0 - sg))
        dyv = dy_ref[...]
        dcat = dyv * sz
        ov = o_ref[...]
        ybs = yb_ref[...] * ps_ref[...]
        dz_ref[:, 0:512] = (dyv[:, 0:512] * ov * dsz[:, 0:512]).astype(BF16)
        dz_ref[:, 512:1024] = (dyv[:, 512:1024] * ybs * dsz[:, 512:1024]).astype(BF16)
        do = dcat[:, 0:512]
        do_ref[...] = do.astype(BF16)
        r_i = lax.broadcasted_iota(jnp.int32, (512, 512), 0) // MLA_V
        c_i = lax.broadcasted_iota(jnp.int32, (512, 512), 1) // MLA_V
        dl_ref[...] = _dot(do * ov, (r_i == c_i).astype(F32), NN, HI)
        dyb_ref[...] = (dcat[:, 512:1024] * ps_ref[...]).astype(BF16)

        @pl.when(i == 0)
        def _():
            dps_ref[...] = jnp.zeros_like(dps_ref)

        dps_ref[...] += jnp.sum(dcat[:, 512:1024] * yb_ref[...], axis=0, keepdims=True)

    row = lambda w: pl.BlockSpec((tm, w), lambda i: (i, 0))
    vec = pl.BlockSpec((1, 512), lambda i: (0, 0))
    return pl.pallas_call(
        body, name=name, grid=(t // tm,),
        in_specs=[row(1024), row(512), row(512), pl.BlockSpec((tm, 1024), lambda i: (i, 1)), vec],
        out_specs=[row(512), row(512), row(512), row(1024), vec],
        out_shape=[jax.ShapeDtypeStruct((t, 512), BF16), jax.ShapeDtypeStruct((t, 512), F32),
                   jax.ShapeDtypeStruct((t, 512), BF16), jax.ShapeDtypeStruct((t, 1024), BF16),
                   jax.ShapeDtypeStruct((1, 512), F32)],
        compiler_params=_params(("arbitrary",)),
    )(dy, o, ybraw, proj, pool_scale)


ATT_HP_FWD = 4
ATT_HP_BWD = 2


def _diag_mask(tq):
    return lax.broadcasted_iota(jnp.int32, (tq, tq), 1) <= lax.broadcasted_iota(jnp.int32, (tq, tq), 0)


def _block_schedule(nq, key_major):
    if key_major:
        pairs = [(qi, ki) for ki in range(nq) for qi in range(ki, nq)]
    else:
        pairs = [(qi, ki) for qi in range(nq) for ki in range(qi + 1)]
    return jnp.asarray([p[0] for p in pairs], jnp.int32), jnp.asarray([p[1] for p in pairs], jnp.int32)


def _attn_fwd(q, k, v, *, name):
    t = q.shape[0]
    tq = _tile(t, ATT_TILE)
    nq = t // tq
    hp = ATT_HP_FWD
    qi_tab, ki_tab = _block_schedule(nq, key_major=False)

    def body(qi_ref, ki_ref, q_ref, k_ref, v_ref, o_ref, lse_ref, m_sc, l_sc, acc_sc):
        step = pl.program_id(1)
        qi, ki = qi_ref[step], ki_ref[step]

        @pl.when(ki == 0)
        def _():
            m_sc[...] = jnp.full_like(m_sc, -jnp.inf)
            l_sc[...] = jnp.zeros_like(l_sc)
            acc_sc[...] = jnp.zeros_like(acc_sc)

        def block(on_diagonal):
            scores = []
            for h in range(hp):
                sl = slice(h * LANES, (h + 1) * LANES)
                scores.append(_dot(q_ref[:, sl], k_ref[:, sl], NT))
            if on_diagonal:
                mask = _diag_mask(tq)
                scores = [jnp.where(mask, s, -jnp.inf) for s in scores]
            for h, s in enumerate(scores):
                vv = v_ref[:, (h // 2) * LANES:(h // 2 + 1) * LANES]
                m_prev = m_sc[h]
                m_new = jnp.maximum(m_prev, jnp.max(s, axis=-1, keepdims=True))
                alpha = jnp.exp(m_prev - m_new)
                p = jnp.exp(s - m_new[:, 0:1])
                l_sc[h] = alpha * l_sc[h] + jnp.sum(p, axis=-1, keepdims=True)
                acc_sc[h] = alpha * acc_sc[h] + _dot(p.astype(BF16), vv)
                m_sc[h] = m_new

        pl.when(ki < qi)(functools.partial(block, False))
        pl.when(ki == qi)(functools.partial(block, True))

        @pl.when(ki == qi)
        def _():
            first = lax.broadcasted_iota(jnp.int32, (tq, LANES), 1) < MLA_V
            for pr in range(hp // 2):
                a, b = 2 * pr, 2 * pr + 1
                sl = slice(pr * LANES, (pr + 1) * LANES)
                o_ref[:, sl] = jnp.where(first, acc_sc[a] / l_sc[a], acc_sc[b] / l_sc[b])
                lse_ref[:, sl] = jnp.where(first, m_sc[a] + jnp.log(l_sc[a]), m_sc[b] + jnp.log(l_sc[b]))

    grid_spec = pltpu.PrefetchScalarGridSpec(
        num_scalar_prefetch=2, grid=(MLA_HEADS // hp, qi_tab.shape[0]),
        in_specs=[pl.BlockSpec((tq, hp * LANES), lambda g, s, qt, kt: (qt[s], g)),
                  pl.BlockSpec((tq, hp * LANES), lambda g, s, qt, kt: (kt[s], g)),
                  pl.BlockSpec((tq, hp * MLA_V), lambda g, s, qt, kt: (kt[s], g))],
        out_specs=[pl.BlockSpec((tq, hp * MLA_V), lambda g, s, qt, kt: (qt[s], g)),
                   pl.BlockSpec((tq, hp * MLA_V), lambda g, s, qt, kt: (qt[s], g))],
        scratch_shapes=[pltpu.VMEM((hp, tq, LANES), F32)] * 3,
    )
    return pl.pallas_call(
        body, name=name, grid_spec=grid_spec,
        out_shape=[jax.ShapeDtypeStruct((t, 512), F32), jax.ShapeDtypeStruct((t, 512), F32)],
        compiler_params=_params(("parallel", "arbitrary")),
    )(qi_tab, ki_tab, q, k, v)


def _attn_bwd(q, k, v, do, lse, delta, *, name):
    t = q.shape[0]
    tq = _tile(t, ATT_TILE)
    nq = t // tq
    hp = ATT_HP_BWD
    qi_tab, ki_tab = _block_schedule(nq, key_major=True)

    def body(qi_ref, ki_ref, q_ref, k_ref, v_ref, do_ref, lse_ref, dl_ref, dq_ref, dk_ref, dv_ref, dk_sc, dv_sc):
        step = pl.program_id(1)
        qi, ki = qi_ref[step], ki_ref[step]

        @pl.when(step == 0)
        def _():
            dq_ref[...] = jnp.zeros_like(dq_ref)

        @pl.when(qi == ki)
        def _():
            dk_sc[...] = jnp.zeros_like(dk_sc)
            dv_sc[...] = jnp.zeros_like(dv_sc)

        def block(on_diagonal):
            lane = lax.broadcasted_iota(jnp.int32, (tq, LANES), 1)
            rows = pl.ds(pl.multiple_of(qi * tq, tq), tq)
            heads = [slice(h * LANES, (h + 1) * LANES) for h in range(hp)]
            scores = [_dot(q_ref[:, sl], k_ref[:, sl], NT) for sl in heads]
            dps = []
            for h in range(hp):
                dov = do_ref[:, (h // 2) * LANES:(h // 2 + 1) * LANES]
                mine = (lane < MLA_V) if h % 2 == 0 else (lane >= MLA_V)
                dps.append(_dot(jnp.where(mine, dov, jnp.zeros_like(dov)), v_ref[:, (h // 2) * LANES:(h // 2 + 1) * LANES], NT))
            mask = _diag_mask(tq) if on_diagonal else None
            for h, sl in enumerate(heads):
                col = (h // 2) * LANES + (h % 2) * MLA_V
                p = jnp.exp(scores[h] - lse_ref[:, col:col + 1])
                if on_diagonal:
                    p = jnp.where(mask, p, 0.0)
                ds = (p * (dps[h] - dl_ref[:, col:col + 1])).astype(BF16)
                dv_sc[h] += _dot(p.astype(BF16), do_ref[:, (h // 2) * LANES:(h // 2 + 1) * LANES], TN)
                dk_sc[h] += _dot(ds, q_ref[:, sl], TN)
                dq_ref[rows, sl] += _dot(ds, k_ref[:, sl], NN)

        pl.when(qi > ki)(functools.partial(block, False))
        pl.when(qi == ki)(functools.partial(block, True))

        @pl.when(qi == nq - 1)
        def _():
            first = lax.broadcasted_iota(jnp.int32, (tq, LANES), 1) < MLA_V
            for h in range(hp):
                dk_ref[:, h * LANES:(h + 1) * LANES] = dk_sc[h]
            for pr in range(hp // 2):
                dv_ref[:, pr * LANES:(pr + 1) * LANES] = jnp.where(first, dv_sc[2 * pr], dv_sc[2 * pr + 1]).astype(BF16)

    qrow = lambda w: pl.BlockSpec((tq, w), lambda g, s, qt, kt: (qt[s], g))
    krow = lambda w: pl.BlockSpec((tq, w), lambda g, s, qt, kt: (kt[s], g))
    grid_spec = pltpu.PrefetchScalarGridSpec(
        num_scalar_prefetch=2, grid=(MLA_HEADS // hp, qi_tab.shape[0]),
        in_specs=[qrow(hp * LANES), krow(hp * LANES), krow(hp * MLA_V), qrow(hp * MLA_V), qrow(hp * MLA_V), qrow(hp * MLA_V)],
        out_specs=[pl.BlockSpec((t, hp * LANES), lambda g, s, qt, kt: (0, g)), krow(hp * LANES), krow(hp * MLA_V)],
        scratch_shapes=[pltpu.VMEM((hp, tq, LANES), F32), pltpu.VMEM((hp, tq, LANES), F32)],
    )
    return pl.pallas_call(
        body, name=name, grid_spec=grid_spec,
        out_shape=[jax.ShapeDtypeStruct((t, 1024), F32), jax.ShapeDtypeStruct((t, 1024), F32),
                   jax.ShapeDtypeStruct((t, 512), BF16)],
        compiler_params=_params(("parallel", "arbitrary")),
    )(qi_tab, ki_tab, q, k, v, do, lse, delta)


def _conv_rows(ext, tm, w_ref, sec):
    c0 = sec * 1024
    y = ext[CONV_HALO - 3:CONV_HALO - 3 + tm, c0:c0 + 1024] * w_ref[0:1, c0:c0 + 1024]
    for j in range(1, CONV_WIDTH):
        y = y + ext[CONV_HALO - 3 + j:CONV_HALO - 3 + j + tm, c0:c0 + 1024] * w_ref[j:j + 1, c0:c0 + 1024]
    return y


def _c_prep(proj_c, conv_w, a_log, dt_bias, *, name):
    t = proj_c.shape[0]
    tm = _tile(t, ROW_TILE)
    hb = tm // CONV_HALO

    def body(p_ref, halo_ref, ab_ref, w_ref, al_ref, dtb_ref, q_ref, k_ref, v_ref, g_ref, b_ref, gt_ref, ext):
        i = pl.program_id(0)
        ext[0:CONV_HALO, :] = jnp.where(i > 0, halo_ref[...], 0.0)
        ext[CONV_HALO:CONV_HALO + tm, :] = p_ref[...]
        for sec, o_ref in enumerate((q_ref, k_ref, v_ref)):
            y = _conv_rows(ext, tm, w_ref, sec)
            y = y * _sigmoid(y)
            if sec == 2:
                o_ref[...] = y
                continue
            scale = GDN_DK ** -0.5 if sec == 0 else 1.0
            for h in range(GDN_HEADS):
                sl = slice(h * LANES, (h + 1) * LANES)
                blk = y[:, sl]
                r = lax.rsqrt(jnp.sum(blk * blk, axis=-1, keepdims=True) + RMS_EPS)
                o_ref[:, sl] = blk * (r * scale)
        ab = ab_ref[...]
        g = -jnp.exp(al_ref[...]) * _softplus(ab + dtb_ref[...])
        beta = _sigmoid(ab)
        ri = lax.broadcasted_iota(jnp.int32, (tm, tm), 0)
        ci = lax.broadcasted_iota(jnp.int32, (tm, tm), 1)
        lower = ((ri // CHUNK) == (ci // CHUNK)) & (ri >= ci)
        gc = _dot(lower.astype(F32), g, NN, HI)
        eye = lax.broadcasted_iota(jnp.int32, (LANES, LANES), 0) == lax.broadcasted_iota(jnp.int32, (LANES, LANES), 1)
        gt_ref[...] = _dot(eye.astype(F32), gc, NT, HI)[0:GDN_HEADS, :]
        for h in range(GDN_HEADS):
            sl = slice(h * LANES, (h + 1) * LANES)
            g_ref[:, sl] = jnp.broadcast_to(gc[:, h:h + 1], (tm, LANES))
            b_ref[:, sl] = jnp.broadcast_to(beta[:, GDN_HEADS + h:GDN_HEADS + h + 1], (tm, LANES))

    row = lambda w: pl.BlockSpec((tm, w), lambda i: (i, 0))
    vec = lambda r, w: pl.BlockSpec((r, w), lambda i: (0, 0))
    out = jax.ShapeDtypeStruct((t, 1024), F32)
    return pl.pallas_call(
        body, name=name, grid=(t // tm,),
        in_specs=[row(3072), pl.BlockSpec((CONV_HALO, 3072), lambda i: (jnp.maximum(i * hb - 1, 0), 0)),
                  pl.BlockSpec((tm, LANES), lambda i: (i, 32)), vec(CONV_WIDTH, 3072), vec(1, LANES), vec(1, LANES)],
        out_specs=[row(1024)] * 5 + [pl.BlockSpec((GDN_HEADS, tm), lambda i: (0, i))],
        out_shape=[out] * 5 + [jax.ShapeDtypeStruct((GDN_HEADS, t), F32)],
        scratch_shapes=[pltpu.VMEM((tm + CONV_HALO, 3072), F32)],
        compiler_params=_params(("parallel",)),
    )(proj_c, proj_c, proj_c, conv_w, a_log, dt_bias)


def _c_prep_bwd(proj_c, conv_w, a_log, dt_bias, dq, dk, dv, dgb, dbb, dz, *, name):
    t = proj_c.shape[0]
    tm = _tile(t, ROW_TILE // 2)
    hb = tm // CONV_HALO
    nt = t // tm
    rev = lambda i: nt - 1 - i

    def body(p_ref, halo_ref, ab_ref, w_ref, al_ref, dtb_ref, dq_ref, dk_ref, dv_ref, dg_ref, db_ref, dz_ref,
             dp_ref, dw_ref, dal_ref, ddt_ref, ext, dyext, carry, taps):
        step = pl.program_id(0)
        i = rev(step)

        @pl.when(step == 0)
        def _():
            dw_ref[...] = jnp.zeros_like(dw_ref)
            dal_ref[...] = jnp.zeros_like(dal_ref)
            ddt_ref[...] = jnp.zeros_like(ddt_ref)
            carry[...] = jnp.zeros_like(carry)

        ext[0:CONV_HALO, :] = jnp.where(i > 0, halo_ref[...], 0.0)
        ext[CONV_HALO:CONV_HALO + tm, :] = p_ref[...]
        for sec, g_ref in enumerate((dq_ref, dk_ref, dv_ref)):
            c0 = sec * 1024
            for j in range(CONV_WIDTH):
                taps[j] = ext[CONV_HALO - 3 + j:CONV_HALO - 3 + j + tm, c0:c0 + 1024]
            y = taps[0] * w_ref[0:1, c0:c0 + 1024]
            for j in range(1, CONV_WIDTH):
                y = y + taps[j] * w_ref[j:j + 1, c0:c0 + 1024]
            sg = _sigmoid(y)
            act = y * sg
            if sec == 2:
                dact = g_ref[...]
            else:
                scale = GDN_DK ** -0.5 if sec == 0 else 1.0
                parts = []
                for h in range(GDN_HEADS):
                    sl = slice(h * LANES, (h + 1) * LANES)
                    blk = act[:, sl]
                    r = lax.rsqrt(jnp.sum(blk * blk, axis=-1, keepdims=True) + RMS_EPS)
                    n = blk * r
                    dn = g_ref[:, sl] * scale
                    parts.append(r * (dn - n * jnp.sum(dn * n, axis=-1, keepdims=True)))
                dact = jnp.concatenate(parts, axis=-1)
            dy = dact * (sg * (1.0 + y * (1.0 - sg)))
            dyext[0:tm, c0:c0 + 1024] = dy
            for j in range(CONV_WIDTH):
                dw_ref[j:j + 1, c0:c0 + 1024] += jnp.sum(dy * taps[j], axis=0, keepdims=True)
        dyext[tm:tm + CONV_HALO, :] = carry[...]
        carry[...] = dyext[0:CONV_HALO, :]
        for sec in range(3):
            c0 = sec * 1024
            dx = dyext[3:3 + tm, c0:c0 + 1024] * w_ref[0:1, c0:c0 + 1024]
            for j in range(1, CONV_WIDTH):
                dx = dx + dyext[3 - j:3 - j + tm, c0:c0 + 1024] * w_ref[j:j + 1, c0:c0 + 1024]
            dp_ref[:, c0:c0 + 1024] = dx.astype(BF16)
        dp_ref[:, 3072:4096] = dz_ref[...]
        lane = lax.broadcasted_iota(jnp.int32, (tm, LANES), 1)
        dg = jnp.zeros((tm, LANES), F32)
        dbeta = jnp.zeros((tm, LANES), F32)
        for h in range(GDN_HEADS):
            sl = slice(h * LANES, (h + 1) * LANES)
            dg = dg + jnp.where(lane == h, dg_ref[:, sl], 0.0)
            dbeta = dbeta + jnp.where(lane == GDN_HEADS + h, db_ref[:, sl], 0.0)
        ri = lax.broadcasted_iota(jnp.int32, (tm, tm), 0)
        ci = lax.broadcasted_iota(jnp.int32, (tm, tm), 1)
        upper = ((ri // CHUNK) == (ci // CHUNK)) & (ri <= ci)
        dg = _dot(upper.astype(F32), dg, NN, HI)
        pre = ab_ref[...] + dtb_ref[...]
        s = _sigmoid(pre)
        a_exp = jnp.exp(al_ref[...])
        dg_da = dg * (-a_exp * s)
        dp_ref[:, 4096:IN_C_PAD] = (dg_da + dbeta * s * (1.0 - s)).astype(BF16)
        dal_ref[...] += jnp.sum(dg * (-a_exp * _softplus(pre)), axis=0, keepdims=True)
        ddt_ref[...] += jnp.sum(dg_da, axis=0, keepdims=True)

    row = lambda w: pl.BlockSpec((tm, w), lambda s: (rev(s), 0))
    vec = lambda r, w: pl.BlockSpec((r, w), lambda s: (0, 0))
    return pl.pallas_call(
        body, name=name, grid=(nt,),
        in_specs=[row(3072), pl.BlockSpec((CONV_HALO, 3072), lambda s: (jnp.maximum(rev(s) * hb - 1, 0), 0)),
                  pl.BlockSpec((tm, LANES), lambda s: (rev(s), 32)), vec(CONV_WIDTH, 3072), vec(1, LANES), vec(1, LANES),
                  row(1024), row(1024), row(1024), row(1024), row(1024), row(1024)],
        out_specs=[row(IN_C_PAD), vec(CONV_WIDTH, 3072), vec(1, LANES), vec(1, LANES)],
        out_shape=[jax.ShapeDtypeStruct((t, IN_C_PAD), BF16), jax.ShapeDtypeStruct((CONV_WIDTH, 3072), F32),
                   jax.ShapeDtypeStruct((1, LANES), F32), jax.ShapeDtypeStruct((1, LANES), F32)],
        scratch_shapes=[pltpu.VMEM((tm + CONV_HALO, 3072), F32), pltpu.VMEM((tm + CONV_HALO, 3072), F32),
                        pltpu.VMEM((CONV_HALO, 3072), F32), pltpu.VMEM((CONV_WIDTH, tm, 1024), F32)],
        compiler_params=_params(("arbitrary",)),
    )(proj_c, proj_c, proj_c, conv_w, a_log, dt_bias, dq, dk, dv, dgb, dbb, dz)


def _o_gate_fwd(o, proj_c, o_norm, *, name):
    t = o.shape[0]
    tm = _tile(t, 2 * ROW_TILE)

    def body(o_ref, z_ref, g_ref, y_ref):
        for h in range(GDN_HEADS):
            sl = slice(h * LANES, (h + 1) * LANES)
            x = o_ref[:, sl]
            r = lax.rsqrt(jnp.mean(x * x, axis=-1, keepdims=True) + RMS_EPS)
            z = z_ref[:, sl]
            y_ref[:, sl] = (x * r * g_ref[...] * (z * _sigmoid(z))).astype(BF16)

    row = pl.BlockSpec((tm, 1024), lambda i: (i, 0))
    return pl.pallas_call(
        body, name=name, grid=(t // tm,),
        in_specs=[row, pl.BlockSpec((tm, 1024), lambda i: (i, 3)), pl.BlockSpec((1, LANES), lambda i: (0, 0))],
        out_specs=row, out_shape=jax.ShapeDtypeStruct((t, 1024), BF16), compiler_params=_params(("parallel",)),
    )(o, proj_c, o_norm)


def _o_gate_bwd(dy, o, proj_c, o_norm, *, name):
    t = o.shape[0]
    tm = _tile(t, 2 * ROW_TILE)

    def body(dy_ref, o_ref, z_ref, g_ref, do_ref, dz_ref, dg_ref):
        i = pl.program_id(0)

        @pl.when(i == 0)
        def _():
            dg_ref[...] = jnp.zeros_like(dg_ref)

        dg = jnp.zeros((1, LANES), F32)
        for h in range(GDN_HEADS):
            sl = slice(h * LANES, (h + 1) * LANES)
            x = o_ref[:, sl]
            r = lax.rsqrt(jnp.mean(x * x, axis=-1, keepdims=True) + RMS_EPS)
            xh = x * r
            z = z_ref[:, sl]
            sg = _sigmoid(z)
            dyv = dy_ref[:, sl]
            dn = dyv * (z * sg)
            dz_ref[:, sl] = (dyv * xh * g_ref[...] * (sg * (1.0 + z * (1.0 - sg)))).astype(BF16)
            dxh = dn * g_ref[...]
            do_ref[:, sl] = r * (dxh - xh * jnp.mean(dxh * xh, axis=-1, keepdims=True))
            dg = dg + jnp.sum(dn * xh, axis=0, keepdims=True)
        dg_ref[...] += dg

    row = pl.BlockSpec((tm, 1024), lambda i: (i, 0))
    vec = pl.BlockSpec((1, LANES), lambda i: (0, 0))
    return pl.pallas_call(
        body, name=name, grid=(t // tm,), in_specs=[row, row, pl.BlockSpec((tm, 1024), lambda i: (i, 3)), vec],
        out_specs=[row, row, vec],
        out_shape=[jax.ShapeDtypeStruct((t, 1024), F32), jax.ShapeDtypeStruct((t, 1024), BF16),
                   jax.ShapeDtypeStruct((1, LANES), F32)],
        compiler_params=_params(("arbitrary",)),
    )(dy, o, proj_c, o_norm)


PAIR = 2 * CHUNK
GDN_HP = 8


def _bdot(a, b, dims=NN):
    return _dot(a.astype(BF16), b.astype(BF16), dims)


def _each(f, *lists):
    return [f(*args) for args in zip(*lists)]


def _pair_common(q, k, v, gci, gcj, beta):
    ri = lax.broadcasted_iota(jnp.int32, (PAIR, PAIR), 0)
    ci = lax.broadcasted_iota(jnp.int32, (PAIR, PAIR), 1)
    same = (ri // CHUNK) == (ci // CHUNK)
    incl = same & (ri >= ci)
    strict = same & (ri > ci)
    eye = (ri == ci).astype(F32)
    first = lax.broadcasted_iota(jnp.int32, (PAIR, LANES), 0) < CHUNK
    gamma = _each(lambda gi, gj: jnp.where(incl, jnp.exp(jnp.minimum(gi - gj, 0.0)), 0.0), gci, gcj)
    kb = _each(jnp.multiply, k, beta)
    kk = _each(lambda a, b: _bdot(a, b, NT), kb, k)
    qk = _each(lambda a, b: _bdot(a, b, NT), q, k)
    m = _each(lambda x, g: jnp.where(strict, x * g, 0.0), kk, gamma)
    tm_ = _each(lambda x: eye - x, m)
    pw = _each(lambda x: _bdot(x, x), m)
    for it in range(5):
        tm_ = _each(lambda x, p: x + _bdot(x, p), tm_, pw)
        if it < 4:
            pw = _each(lambda p: _bdot(p, p), pw)
    eg = _each(jnp.exp, gci)
    vb = _each(jnp.multiply, v, beta)
    kbe = _each(jnp.multiply, kb, eg)
    uw = _each(lambda x, a, b: _bdot(x, jnp.concatenate([a, b], axis=1)), tm_, vb, kbe)
    attn = _each(lambda x, g: jnp.where(incl, x * g, 0.0), qk, gamma)
    gl_a = _each(lambda g: g[CHUNK - 1:CHUNK, :], gci)
    gl_b = _each(lambda g: g[PAIR - 1:PAIR, :], gci)
    ek = _each(lambda a, b, g: jnp.exp(jnp.where(first, a, b) - g), gl_a, gl_b, gci)
    return dict(incl=incl, strict=strict, gamma=gamma, kb=kb, m=m, tm=tm_, eg=eg, vb=vb, kbe=kbe,
                u=_each(lambda x: x[:, :LANES], uw), w=_each(lambda x: x[:, LANES:], uw), attn=attn,
                qd=_each(jnp.multiply, q, eg), ek=ek, kd=_each(jnp.multiply, k, ek),
                glast_a=_each(jnp.exp, gl_a), glast_b=_each(jnp.exp, gl_b))


def _gdn_specs(t, ts, order):
    nc = ts // CHUNK
    blk = pl.BlockSpec((ts, GDN_HP * LANES), lambda h, s: (order(s), h))
    row = pl.BlockSpec((GDN_HP, 1, ts), lambda h, s: (h, 0, order(s)))
    st = pl.BlockSpec((GDN_HP, nc, LANES, LANES), lambda h, s: (h, order(s), 0, 0))
    return blk, row, st


def _gdn_fwd(q, k, v, gcb, gct, bb, *, name):
    t = q.shape[0]
    ts = _tile(t, GDN_TILE)
    npair = ts // PAIR

    def body(q_ref, k_ref, v_ref, g_ref, gt_ref, b_ref, o_ref, st_ref, s_sc):
        @pl.when(pl.program_id(1) == 0)
        def _():
            s_sc[...] = jnp.zeros_like(s_sc)

        def pair(pi, _):
            rows = pl.ds(pl.multiple_of(pi * PAIR, PAIR), PAIR)
            heads = [slice(hh * LANES, (hh + 1) * LANES) for hh in range(GDN_HP)]
            c = CHUNK
            cat0 = lambda *xs: jnp.concatenate(xs, axis=0)
            s0 = [s_sc[hh] for hh in range(GDN_HP)]
            cm = _pair_common([q_ref[rows, sl] for sl in heads], [k_ref[rows, sl] for sl in heads],
                              [v_ref[rows, sl] for sl in heads], [g_ref[rows, sl] for sl in heads],
                              [gt_ref[hh, :, rows] for hh in range(GDN_HP)], [b_ref[rows, sl] for sl in heads])
            u, w, qd, kd = cm["u"], cm["w"], cm["qd"], cm["kd"]
            r0 = _each(lambda w_, q_, s: _bdot(cat0(w_[:c], q_[:c]), s), w, qd, s0)
            vn_a = _each(lambda u_, r: u_[:c] - r[:c], u, r0)
            s1 = _each(lambda s, gl, k_, vn: s * gl + _bdot(k_[:c], vn, TN), s0, cm["glast_a"], kd, vn_a)
            r1 = _each(lambda w_, q_, s: _bdot(cat0(w_[c:], q_[c:]), s), w, qd, s1)
            vn_b = _each(lambda u_, r: u_[c:] - r[:c], u, r1)
            s2 = _each(lambda s, gl, k_, vn: s * gl + _bdot(k_[c:], vn, TN), s1, cm["glast_b"], kd, vn_b)
            o = _each(lambda ra, rb, at, va, vb_: cat0(ra[c:], rb[c:]) + _bdot(at, cat0(va, vb_)),
                      r0, r1, cm["attn"], vn_a, vn_b)
            for hh, sl in enumerate(heads):
                st_ref[hh, 2 * pi] = s0[hh]
                st_ref[hh, 2 * pi + 1] = s1[hh]
                s_sc[hh] = s2[hh]
                o_ref[rows, sl] = o[hh]
            return 0

        lax.fori_loop(0, npair, pair, 0)

    blk, row, st = _gdn_specs(t, ts, lambda s: s)
    return pl.pallas_call(
        body, name=name, grid=(GDN_HEADS // GDN_HP, t // ts), in_specs=[blk, blk, blk, blk, row, blk],
        out_specs=[blk, st],
        out_shape=[jax.ShapeDtypeStruct((t, 1024), F32), jax.ShapeDtypeStruct((GDN_HEADS, t // CHUNK, LANES, LANES), F32)],
        scratch_shapes=[pltpu.VMEM((GDN_HP, LANES, LANES), F32)],
        compiler_params=_params(("parallel", "arbitrary")),
    )(q, k, v, gcb, gct, bb)


def _gdn_bwd(q, k, v, gcb, gct, bb, do, states, *, name):
    t = q.shape[0]
    ts = _tile(t, GDN_TILE)
    npair = ts // PAIR
    ns = t // ts
    c = CHUNK

    def body(q_ref, k_ref, v_ref, g_ref, gt_ref, b_ref, do_ref, st_ref, dq_ref, dk_ref, dv_ref, dg_ref, db_ref, ds_sc):
        @pl.when(pl.program_id(1) == 0)
        def _():
            ds_sc[...] = jnp.zeros_like(ds_sc)

        rowsum = lambda x: jnp.sum(x, axis=-1, keepdims=True)
        total = lambda x: jnp.sum(rowsum(x), axis=0, keepdims=True)
        cat0 = lambda *xs: jnp.concatenate(xs, axis=0)
        cat1 = lambda *xs: jnp.concatenate(xs, axis=1)

        def pair(step, _):
            pi = npair - 1 - step
            rows = pl.ds(pl.multiple_of(pi * PAIR, PAIR), PAIR)
            heads = [slice(hh * LANES, (hh + 1) * LANES) for hh in range(GDN_HP)]
            hs = range(GDN_HP)
            qv, kv, vv = ([r[rows, sl] for sl in heads] for r in (q_ref, k_ref, v_ref))
            beta = [b_ref[rows, sl] for sl in heads]
            dov = [do_ref[rows, sl] for sl in heads]
            s0 = [st_ref[hh, 2 * pi] for hh in hs]
            s1 = [st_ref[hh, 2 * pi + 1] for hh in hs]
            ds2 = [ds_sc[hh] for hh in hs]
            cm = _pair_common(qv, kv, vv, [g_ref[rows, sl] for sl in heads], [gt_ref[hh, :, rows] for hh in hs], beta)
            u, w, qd, kd, attn = cm["u"], cm["w"], cm["qd"], cm["kd"], cm["attn"]
            tmat, gamma, eg = cm["tm"], cm["gamma"], cm["eg"]
            incl, strict = cm["incl"], cm["strict"]
            vn_a = _each(lambda u_, w_, s: u_[:c] - _bdot(w_[:c], s), u, w, s0)
            vn_b = _each(lambda u_, w_, s: u_[c:] - _bdot(w_[c:], s), u, w, s1)
            vn = _each(cat0, vn_a, vn_b)
            dvn_att = _each(lambda a, d: _bdot(a, d, TN), attn, dov)
            dattn = _each(lambda d, v_: jnp.where(incl, _bdot(d, v_, NT), 0.0), dov, vn)
            dvn_b = _each(lambda x, k_, d: x[c:] + _bdot(k_[c:], d), dvn_att, kd, ds2)
            rb = _each(lambda d, x, s: _bdot(cat0(d[c:], x), s, NT), dov, dvn_b, s1)
            dkd_b = _each(lambda v_, d: _bdot(v_, d, NT), vn_b, ds2)
            dgl_b = _each(lambda d, s: total(d * s), ds2, s1)
            ds1 = _each(lambda d, gl, q_, w_, o_, x: d * gl + _bdot(cat0(q_[c:], w_[c:]), cat0(o_[c:], -x), TN),
                        ds2, cm["glast_b"], qd, w, dov, dvn_b)
            dvn_a = _each(lambda x, k_, d: x[:c] + _bdot(k_[:c], d), dvn_att, kd, ds1)
            ra = _each(lambda d, x, s: _bdot(cat0(d[:c], x), s, NT), dov, dvn_a, s0)
            dkd_a = _each(lambda v_, d: _bdot(v_, d, NT), vn_a, ds1)
            dgl_a = _each(lambda d, s: total(d * s), ds1, s0)
            ds0 = _each(lambda d, gl, q_, w_, o_, x: d * gl + _bdot(cat0(q_[:c], w_[:c]), cat0(o_[:c], -x), TN),
                        ds1, cm["glast_a"], qd, w, dov, dvn_a)
            dvn = _each(cat0, dvn_a, dvn_b)
            dqd = _each(lambda a, b: cat0(a[:c], b[:c]), ra, rb)
            dw = _each(lambda a, b: -cat0(a[c:], b[c:]), ra, rb)
            dkd = _each(cat0, dkd_a, dkd_b)
            dvw = _each(cat1, dvn, dw)
            dvbk = _each(lambda t_, x: _bdot(t_, x, TN), tmat, dvw)
            dvb = _each(lambda x: x[:, :LANES], dvbk)
            dkbe = _each(lambda x: x[:, LANES:], dvbk)
            dt_ = _each(lambda x, a, b: _bdot(x, cat1(a, b), NT), dvw, cm["vb"], cm["kbe"])
            da1 = _each(lambda t_, x: _bdot(t_, x, TN), tmat, dt_)
            dm = _each(lambda x, t_: jnp.where(strict, -_bdot(x, t_, NT), 0.0), da1, tmat)
            dkk = _each(jnp.multiply, dm, gamma)
            dqk = _each(jnp.multiply, dattn, gamma)
            z = _each(lambda a, b, c_, d: a * b + c_ * d, dm, cm["m"], dattn, attn)
            dkb = _each(lambda x, k_, y, e: _bdot(x, k_) + y * e, dkk, kv, dkbe, eg)
            dk = _each(lambda a, b, kb_, q_, x, e, y, be: _bdot(cat0(a, b), cat0(kb_, q_), TN) + x * e + y * be,
                       dkk, dqk, cm["kb"], qv, dkd, cm["ek"], dkb, beta)
            dq = _each(lambda x, k_, y, e: _bdot(x, k_) + y * e, dqk, kv, dqd, eg)

            def colsum_of(z_):
                zh = z_.astype(BF16)
                zl = (z_ - zh.astype(F32)).astype(BF16)
                return _dot(cat0(zh, zl), jnp.ones((2 * PAIR, LANES), BF16), TN)

            colsum = _each(colsum_of, z)
            ri = lax.broadcasted_iota(jnp.int32, (PAIR, LANES), 0)
            for hh, sl in enumerate(heads):
                dkd_kd = dkd[hh] * kd[hh]
                dgc = (rowsum(z[hh]) - colsum[hh] + rowsum(dqd[hh] * qd[hh]) - rowsum(dkd_kd)
                       + rowsum(dkbe[hh] * cm["kbe"][hh]))
                last_a = total(dkd_kd[:c]) + dgl_a[hh] * cm["glast_a"][hh]
                last_b = total(dkd_kd[c:]) + dgl_b[hh] * cm["glast_b"][hh]
                dgc = dgc + jnp.where(ri == c - 1, last_a, 0.0) + jnp.where(ri == PAIR - 1, last_b, 0.0)
                ds_sc[hh] = ds0[hh]
                dq_ref[rows, sl] = dq[hh]
                dk_ref[rows, sl] = dk[hh]
                dv_ref[rows, sl] = dvb[hh] * beta[hh]
                db_ref[rows, sl] = jnp.broadcast_to(rowsum(dkb[hh] * kv[hh]) + rowsum(dvb[hh] * vv[hh]), (PAIR, LANES))
                dg_ref[rows, sl] = dgc
            return 0

        lax.fori_loop(0, npair, pair, 0)

    blk, row, st = _gdn_specs(t, ts, lambda s: ns - 1 - s)
    out = jax.ShapeDtypeStruct((t, 1024), F32)
    return pl.pallas_call(
        body, name=name, grid=(GDN_HEADS // GDN_HP, ns), in_specs=[blk, blk, blk, blk, row, blk, blk, st],
        out_specs=[blk] * 5, out_shape=[out] * 5, scratch_shapes=[pltpu.VMEM((GDN_HP, LANES, LANES), F32)],
        compiler_params=_params(("parallel", "arbitrary")),
    )(q, k, v, gcb, gct, bb, do, states)


def _loss_head(h, g, target, *, name):
    t, d = h.shape
    tm = _tile(t, 2 * ROW_TILE)

    def body(h_ref, g_ref, t_ref, dh_ref, dhb_ref, dg_ref, loss_ref):
        i = pl.program_id(0)
        x = h_ref[...]
        r = lax.rsqrt(jnp.mean(x * x, axis=-1, keepdims=True) + RMS_EPS)
        xh = x * r
        err = xh * g_ref[...] - t_ref[...]
        dy = err * (1.0 / d)
        dxh = dy * g_ref[...]
        dh = r * (dxh - xh * jnp.mean(dxh * xh, axis=-1, keepdims=True))
        dh_ref[...] = dh
        dhb_ref[...] = dh.astype(BF16)

        @pl.when(i == 0)
        def _():
            dg_ref[...] = jnp.zeros_like(dg_ref)
            loss_ref[...] = jnp.zeros_like(loss_ref)

        dg_ref[...] += jnp.sum(dy * xh, axis=0, keepdims=True)
        part = 0.5 * jnp.sum(jnp.mean(err * err, axis=-1, keepdims=True), axis=0, keepdims=True)
        loss_ref[...] += jnp.broadcast_to(part, loss_ref.shape)

    row = pl.BlockSpec((tm, d), lambda i: (i, 0))
    vec = pl.BlockSpec((1, d), lambda i: (0, 0))
    return pl.pallas_call(
        body, name=name, grid=(t // tm,), in_specs=[row, vec, row],
        out_specs=[row, row, vec, pl.BlockSpec((8, LANES), lambda i: (0, 0))],
        out_shape=[jax.ShapeDtypeStruct((t, d), F32), jax.ShapeDtypeStruct((t, d), BF16),
                   jax.ShapeDtypeStruct((1, d), F32), jax.ShapeDtypeStruct((8, LANES), F32)],
        compiler_params=_params(("arbitrary",)),
    )(h, g, target)


def _pad_cols(w, n):
    return jnp.pad(w, ((0, 0), (0, n - w.shape[1])))


def _layout_weights(w):
    z = lambda r, c: jnp.zeros((r, c), F32)
    wi = w["w_in_ab"]
    win = jnp.concatenate([wi[:, :384], z(1024, 64), wi[:, 384:416], z(1024, 32), wi[:, 416:]], axis=1)
    wq = jnp.pad(w["w_q_b"].reshape(MLA_Q_RANK, MLA_HEADS, 96), ((0, 0), (0, 0), (0, 32))).reshape(MLA_Q_RANK, 1024)
    kv3 = w["w_kv_b"].reshape(MLA_KV_RANK, MLA_HEADS, 128)
    wk = jnp.pad(kv3[..., :MLA_NOPE], ((0, 0), (0, 0), (0, 64))).reshape(MLA_KV_RANK, 1024)
    wv = kv3[..., MLA_NOPE:].reshape(MLA_KV_RANK, 512)
    pw = w["pool_w"]
    rows = []
    for g in range(4):
        rows.append(jnp.concatenate([pw[g] if j == g else z(128, 128) for j in range(4)], axis=1))
    wpool = jnp.concatenate(rows, axis=0)
    half = MLA_ROPE // 2
    inv = 1.0 / (ROPE_THETA ** (jnp.arange(half, dtype=F32) / half))
    inv_lane = jnp.concatenate([jnp.zeros((MLA_NOPE,), F32), inv, inv, jnp.zeros((32,), F32)]).reshape(1, LANES)
    return dict(
        win=win.astype(BF16), wq=wq.astype(BF16), wk=wk.astype(BF16), wv=wv.astype(BF16), wpool=wpool.astype(BF16),
        wout_ab=w["w_out_ab"].astype(BF16), winc=_pad_cols(w["w_in_c"], IN_C_PAD).astype(BF16),
        wout_c=w["w_out_c"].astype(BF16), conv_w=w["conv_w"], a_log=_pad_cols(w["a_log"], LANES),
        dt_bias=_pad_cols(w["dt_bias"], LANES), inv_lane=inv_lane,
        norm_ab=w["norm_ab"], q_a_norm=w["q_a_norm"], kv_a_norm=w["kv_a_norm"], pool_scale=w["pool_scale"],
        norm_c=w["norm_c"], o_norm=w["o_norm"], final_norm=w["final_norm"],
    )


def _unlayout_grads(g):
    dwin = g["win"]
    dkv = jnp.concatenate([g["wk"].reshape(MLA_KV_RANK, MLA_HEADS, 128)[..., :MLA_NOPE],
                           g["wv"].reshape(MLA_KV_RANK, MLA_HEADS, MLA_V)], axis=-1).reshape(MLA_KV_RANK, 1024)
    return dict(
        norm_ab=g["norm_ab"],
        w_in_ab=jnp.concatenate([dwin[:, :384], dwin[:, 448:480], dwin[:, 512:]], axis=1),
        q_a_norm=g["q_a_norm"],
        w_q_b=g["wq"].reshape(MLA_Q_RANK, MLA_HEADS, 128)[..., :96].reshape(MLA_Q_RANK, 768),
        kv_a_norm=g["kv_a_norm"],
        w_kv_b=dkv,
        pool_w=jnp.stack([g["wpool"][i * 128:(i + 1) * 128, i * 128:(i + 1) * 128] for i in range(4)]),
        pool_scale=g["pool_scale"],
        w_out_ab=g["wout_ab"],
        norm_c=g["norm_c"],
        w_in_c=g["winc"][:, :4112],
        conv_w=g["conv_w"],
        a_log=g["a_log"][:, :GDN_HEADS],
        dt_bias=g["dt_bias"][:, :GDN_HEADS],
        o_norm=g["o_norm"],
        w_out_c=g["wout_c"],
        final_norm=g["final_norm"],
    )


def _local_step(x, pos, target, lw):
    mm = _matmul
    hn = _rms_fwd(x, lw["norm_ab"], name="rms_ab")
    proj = mm(hn, lw["win"], "nn", name="in_ab")
    qn, kvn, kr, d, cos_t, sin_t = _ab_prep(proj, pos, lw["inv_lane"], lw["q_a_norm"], lw["kv_a_norm"], name="ab_prep")
    qraw = mm(qn, lw["wq"], "nn", name="q_up")
    kvk = mm(kvn, lw["wk"], "nn", name="k_up")
    v = mm(kvn, lw["wv"], "nn", name="v_up", out_dtype=BF16)
    ybraw = mm(d, lw["wpool"], "nn", name="pool_mix")
    q, k = _qk_rope(qraw, kvk, kr, cos_t, sin_t, name="qk_rope")
    o, lse = _attn_fwd(q, k, v, name="attn_fwd")
    y = _gate_fwd(o, ybraw, proj, lw["pool_scale"], name="gate_ab")
    h1 = mm(y, lw["wout_ab"], "nn", name="out_ab", add=x)
    hn1 = _rms_fwd(h1, lw["norm_c"], name="rms_c")
    proj_c = mm(hn1, lw["winc"], "nn", name="in_c")
    q2, k2, v2, gb, bb, gt = _c_prep(proj_c, lw["conv_w"], lw["a_log"], lw["dt_bias"], name="c_prep")
    gt = gt.reshape(GDN_HEADS, 1, gt.shape[1])
    o2, states = _gdn_fwd(q2, k2, v2, gb, gt, bb, name="gdn_fwd")
    y2 = _o_gate_fwd(o2, proj_c, lw["o_norm"], name="gate_c")
    h2 = mm(y2, lw["wout_c"], "nn", name="out_c", add=h1)
    dh2, dh2b, d_final, loss = _loss_head(h2, lw["final_norm"], target, name="loss_head")
    g = {"final_norm": d_final}
    dy2 = mm(dh2b, lw["wout_c"], "nt", name="out_c_dx")
    g["wout_c"] = mm(y2, dh2b, "tn", name="out_c_dw")
    do2, dz2, g["o_norm"] = _o_gate_bwd(dy2, o2, proj_c, lw["o_norm"], name="gate_c_bwd")
    dq2, dk2, dv2, dgb, dbb = _gdn_bwd(q2, k2, v2, gb, gt, bb, do2, states, name="gdn_bwd")
    dproj_c, g["conv_w"], g["a_log"], g["dt_bias"] = _c_prep_bwd(
        proj_c, lw["conv_w"], lw["a_log"], lw["dt_bias"], dq2, dk2, dv2, dgb, dbb, dz2, name="c_prep_bwd")
    dhn1 = mm(dproj_c, lw["winc"], "nt", name="in_c_dx")
    g["winc"] = mm(hn1, dproj_c, "tn", name="in_c_dw")
    dh1, dh1b, g["norm_c"] = _rms_bwd(h1, lw["norm_c"], dhn1, dh2, name="rms_c_bwd", with_bf16=True)
    dy = mm(dh1b, lw["wout_ab"], "nt", name="out_ab_dx")
    g["wout_ab"] = mm(y, dh1b, "tn", name="out_ab_dw")
    do, delta, dyb, dz, g["pool_scale"] = _gate_bwd(dy, o, ybraw, proj, lw["pool_scale"], name="gate_ab_bwd")
    dq, dk, dv = _attn_bwd(q, k, v, do, lse, delta, name="attn_bwd")
    dd = mm(dyb, lw["wpool"], "nt", name="pool_mix_dx")
    g["wpool"] = mm(d, dyb, "tn", name="pool_mix_dw")
    dqraw, dkb, dkr = _qk_rope_bwd(dq, dk, cos_t, sin_t, name="qk_rope_bwd")
    dqn = mm(dqraw, lw["wq"], "nt", name="q_up_dx")
    g["wq"] = mm(qn, dqraw, "tn", name="q_up_dw")
    dkvn_k = mm(dkb, lw["wk"], "nt", name="k_up_dx")
    dkvn_v = mm(dv, lw["wv"], "nt", name="v_up_dx")
    g["wk"] = mm(kvn, dkb, "tn", name="k_up_dw")
    g["wv"] = mm(kvn, dv, "tn", name="v_up_dw")
    dproj, g["q_a_norm"], g["kv_a_norm"] = _ab_prep_bwd(
        proj, lw["q_a_norm"], lw["kv_a_norm"], dqn, dkvn_k, dkvn_v, dkr, dd, dz, name="ab_prep_bwd")
    dhn = mm(dproj, lw["win"], "nt", name="in_ab_dx")
    g["win"] = mm(hn, dproj, "tn", name="in_ab_dw")
    dx, g["norm_ab"] = _rms_bwd(x, lw["norm_ab"], dhn, dh1, name="rms_ab_bwd", with_bf16=False)
    return loss, dx, g


_HBM = pl.BlockSpec(memory_space=pltpu.HBM)


def _place():
    return lax.axis_index("x"), lax.axis_index("y"), lax.axis_index("c")


def _flip(v, f):
    return 1 - v if f else v


_CHIP_FLIPS = ((1, 0), (0, 1), (1, 1))
_DEV_FLIPS = tuple((fx, fy, fc) for fx in (0, 1) for fy in (0, 1) for fc in (0, 1) if fx or fy or fc)


def _rcopy(src, dst, send_sems, recv_sems, k, to):
    return pltpu.make_async_remote_copy(src_ref=src, dst_ref=dst, send_sem=send_sems.at[k], recv_sem=recv_sems.at[k],
                                        device_id=to, device_id_type=MESH)


def _gather_weights(wb, ws):
    _, rh, _ = wb.shape
    rs = ws.shape[0]

    def body(wb_ref, ws_ref, gb_ref, gs_ref, send_sems, recv_sems, local_sems):
        x, y, c = _place()
        j0 = 2 * x + y
        sib = (x, y, 1 - c)
        chips = [(_flip(x, fx), _flip(y, fy)) for fx, fy in _CHIP_FLIPS]
        own_b = pltpu.make_async_copy(wb_ref, gb_ref.at[j0], local_sems.at[0])
        own_s = pltpu.make_async_copy(ws_ref, gs_ref.at[j0], local_sems.at[1])
        own_b.start()
        own_s.start()
        sends = []
        for k, (px, py) in enumerate(chips):
            sends.append(_rcopy(wb_ref.at[c], gb_ref.at[j0, c], send_sems, recv_sems, k, (px, py, c)))
            sends.append(_rcopy(ws_ref, gs_ref.at[j0], send_sems, recv_sems, 6 + k, (px, py, c)))
        for cp in sends:
            cp.start()
        for k, (px, py) in enumerate(chips):
            jk = 2 * px + py
            _rcopy(wb_ref.at[c], gb_ref.at[jk, c], send_sems, recv_sems, k, (px, py, c)).wait_recv()
            fwd = _rcopy(gb_ref.at[jk, c], gb_ref.at[jk, c], send_sems, recv_sems, 3 + k, sib)
            fwd.start()
            sends.append(fwd)
        for k, (px, py) in enumerate(chips):
            jk = 2 * px + py
            _rcopy(wb_ref.at[c], gb_ref.at[jk, 1 - c], send_sems, recv_sems, 3 + k, sib).wait_recv()
            _rcopy(ws_ref, gs_ref.at[jk], send_sems, recv_sems, 6 + k, (px, py, c)).wait_recv()
        for cp in sends:
            cp.wait_send()
        own_b.wait()
        own_s.wait()

    return pl.pallas_call(
        body, name="gather_weights", in_specs=[_HBM, _HBM], out_specs=[_HBM, _HBM],
        out_shape=[jax.ShapeDtypeStruct((4, 2, rh, LANES), BF16), jax.ShapeDtypeStruct((4, rs, LANES), F32)],
        scratch_shapes=[pltpu.SemaphoreType.DMA((9,)), pltpu.SemaphoreType.DMA((9,)), pltpu.SemaphoreType.DMA((2,))],
    )(wb, ws)


def _sibling_swap(a, *, name):
    def body(a_ref, o_ref, send_sem, recv_sem):
        x, y, c = _place()
        cp = pltpu.make_async_remote_copy(src_ref=a_ref, dst_ref=o_ref, send_sem=send_sem, recv_sem=recv_sem,
                                          device_id=(x, y, 1 - c), device_id_type=MESH)
        cp.start()
        cp.wait()

    return pl.pallas_call(
        body, name=name, in_specs=[_HBM], out_specs=_HBM, out_shape=jax.ShapeDtypeStruct(a.shape, a.dtype),
        scratch_shapes=[pltpu.SemaphoreType.DMA, pltpu.SemaphoreType.DMA],
    )(a)


def _chip_exchange(p, small):
    _, rh, _ = p.shape
    rs = small.shape[0]

    def body(p_ref, s_ref, l_ref, ls_ref, send_sems, recv_sems, local_sems):
        x, y, c = _place()
        j0 = 2 * x + y
        d0 = 2 * j0 + c
        own_p = pltpu.make_async_copy(p_ref.at[j0], l_ref.at[j0], local_sems.at[0])
        own_s = pltpu.make_async_copy(s_ref, ls_ref.at[d0], local_sems.at[1])
        own_p.start()
        own_s.start()
        sends = []
        for k, (fx, fy) in enumerate(_CHIP_FLIPS):
            px, py = _flip(x, fx), _flip(y, fy)
            sends.append(_rcopy(p_ref.at[2 * px + py], l_ref.at[j0], send_sems, recv_sems, k, (px, py, c)))
        for k, (fx, fy, fc) in enumerate(_DEV_FLIPS):
            peer = (_flip(x, fx), _flip(y, fy), _flip(c, fc))
            sends.append(_rcopy(s_ref, ls_ref.at[d0], send_sems, recv_sems, 3 + k, peer))
        for cp in sends:
            cp.start()
        for k, (fx, fy) in enumerate(_CHIP_FLIPS):
            px, py = _flip(x, fx), _flip(y, fy)
            _rcopy(p_ref.at[j0], l_ref.at[2 * px + py], send_sems, recv_sems, k, (px, py, c)).wait_recv()
        for k, (fx, fy, fc) in enumerate(_DEV_FLIPS):
            px, py, pc = _flip(x, fx), _flip(y, fy), _flip(c, fc)
            _rcopy(s_ref, ls_ref.at[4 * px + 2 * py + pc], send_sems, recv_sems, 3 + k, (px, py, pc)).wait_recv()
        for cp in sends:
            cp.wait_send()
        own_p.wait()
        own_s.wait()

    return pl.pallas_call(
        body, name="chip_exchange", in_specs=[_HBM, _HBM], out_specs=[_HBM, _HBM],
        out_shape=[jax.ShapeDtypeStruct((4, rh, LANES), F32), jax.ShapeDtypeStruct((8, rs, LANES), F32)],
        scratch_shapes=[pltpu.SemaphoreType.DMA((10,)), pltpu.SemaphoreType.DMA((10,)), pltpu.SemaphoreType.DMA((2,))],
    )(p, small)


def _sum_slots(a, *, name):
    n, rows, _ = a.shape
    tr = _tile(rows, 1024)

    def body(a_ref, o_ref):
        acc = a_ref[0]
        for s in range(1, n):
            acc = acc + a_ref[s]
        o_ref[...] = acc

    return pl.pallas_call(
        body, name=name, grid=(rows // tr,), in_specs=[pl.BlockSpec((n, tr, LANES), lambda i: (0, i, 0))],
        out_specs=pl.BlockSpec((tr, LANES), lambda i: (i, 0)), out_shape=jax.ShapeDtypeStruct((rows, LANES), F32),
        compiler_params=_params(("parallel",)),
    )(a)


def _adam_update(g_ref, w_ref, m_ref, v_ref, d_ref, mo_ref, vo_ref):
    gv = g_ref[...]
    mn = ADAM_B1 * m_ref[...] + (1.0 - ADAM_B1) * gv
    vn = ADAM_B2 * v_ref[...] + (1.0 - ADAM_B2) * (gv * gv)
    mo_ref[...] = mn
    vo_ref[...] = vn
    c1 = 1.0 - ADAM_B1 ** ADAM_STEP
    c2 = 1.0 - ADAM_B2 ** ADAM_STEP
    d_ref[...] = -ADAM_LR * ((mn / c1) / (jnp.sqrt(vn / c2) + ADAM_EPS) + ADAM_WD * w_ref[...])


def _adamw_rows(g, w, m, v, *, name):
    rows, cols = g.shape
    tr = _tile(rows, 512)

    def body(*refs):
        _adam_update(*refs)

    blk = pl.BlockSpec((tr, cols), lambda i: (i, 0))
    out = jax.ShapeDtypeStruct((rows, cols), F32)
    return pl.pallas_call(
        body, name=name, grid=(rows // tr,), in_specs=[blk] * 4, out_specs=[blk] * 3, out_shape=[out] * 3,
        compiler_params=_params(("parallel",)),
    )(g, w, m, v)


def _adamw_small(gs, ws, ms, vs, *, name):
    n = len(gs)

    def body(*refs):
        ins, outs = refs[:4 * n], refs[4 * n:]
        for i in range(n):
            _adam_update(ins[i], ins[n + i], ins[2 * n + i], ins[3 * n + i], *outs[3 * i:3 * i + 3])

    out_shape = [jax.ShapeDtypeStruct(g.shape, F32) for g in gs for _ in range(3)]
    vmem = pl.BlockSpec(memory_space=pltpu.VMEM)
    return pl.pallas_call(
        body, name=name, in_specs=[vmem] * (4 * n), out_specs=[vmem] * (3 * n), out_shape=out_shape,
        compiler_params=pltpu.CompilerParams(vmem_limit_bytes=VMEM_LIMIT),
    )(*gs, *ws, *ms, *vs)


_ADAM_ROWWISE = ("w_in_ab", "w_q_b", "w_kv_b", "w_out_ab", "w_in_c", "w_out_c")


_SHARDED = (("w_in_ab", (1024, 488), 1), ("w_q_b", (256, 192), 1), ("w_kv_b", (128, 256), 1),
            ("w_out_ab", (256, 1024), 0), ("w_in_c", (1024, 1028), 1), ("w_out_c", (256, 1024), 0),
            ("conv_w", (4, 768), 1), ("norm_c", (1, 256), 1))
_N_BF16 = 6
_REPLICATED = (("norm_ab", (1, 1024)), ("q_a_norm", (1, 256)), ("kv_a_norm", (1, 128)), ("pool_w", (4, 128, 128)),
               ("pool_scale", (1, 512)), ("a_log", (1, 8)), ("dt_bias", (1, 8)), ("o_norm", (1, 128)),
               ("final_norm", (1, 1024)))
_ALL_NAMES = ("norm_ab", "w_in_ab", "q_a_norm", "w_q_b", "kv_a_norm", "w_kv_b", "pool_w", "pool_scale", "w_out_ab",
              "norm_c", "w_in_c", "conv_w", "a_log", "dt_bias", "o_norm", "w_out_c", "final_norm")


def _rows_of(shape):
    return max(1, math.prod(shape) // LANES)


def _as_rows(a):
    n = a.size
    if n % LANES:
        a = jnp.pad(a.reshape(1, n), ((0, 0), (0, LANES - n % LANES)))
    return a.reshape(-1, LANES)


def _pack(parts, total_rows):
    rows = jnp.concatenate([_as_rows(p) for p in parts], axis=0)
    return jnp.pad(rows, ((0, total_rows - rows.shape[0]), (0, 0)))


def _unpack(packed, spec):
    out, off = [], 0
    lead = packed.shape[:-2]
    for shape in spec:
        r = _rows_of(shape)
        blk = packed[..., off:off + r, :].reshape(lead + (r * LANES,))[..., :math.prod(shape)]
        out.append(blk.reshape(lead + tuple(shape)))
        off += r
    return out


def _round_up(n, m):
    return -(-n // m) * m


_SH_SHAPES = tuple(s for _, s, _ in _SHARDED)
_RB = sum(_rows_of(s) for s in _SH_SHAPES[:_N_BF16])
_RS = _round_up(sum(_rows_of(s) for s in _SH_SHAPES[_N_BF16:]), 8)
_RG = _round_up(sum(_rows_of(s) for s in _SH_SHAPES), 16)
_REP_SHAPES = tuple(s for _, s in _REPLICATED)
_RR = _round_up(sum(_rows_of(s) for s in _REP_SHAPES) + 1, 8)


def kernel(x, positions, norm_ab, w_in_ab, q_a_norm, w_q_b, kv_a_norm, w_kv_b, pool_w, pool_scale, w_out_ab, norm_c, w_in_c, conv_w, a_log, dt_bias, o_norm, w_out_c, final_norm, loss_target, m_norm_ab, m_w_in_ab, m_q_a_norm, m_w_q_b, m_kv_a_norm, m_w_kv_b, m_pool_w, m_pool_scale, m_w_out_ab, m_norm_c, m_w_in_c, m_conv_w, m_a_log, m_dt_bias, m_o_norm, m_w_out_c, m_final_norm, v_norm_ab, v_w_in_ab, v_q_a_norm, v_w_q_b, v_kv_a_norm, v_w_kv_b, v_pool_w, v_pool_scale, v_w_out_ab, v_norm_c, v_w_in_c, v_conv_w, v_a_log, v_dt_bias, v_o_norm, v_w_out_c, v_final_norm):
    given = dict(locals())
    c = lax.axis_index("c")
    t = x.shape[1]

    def shard_of(prefix, name):
        a = given[prefix + name]
        return a.reshape(a.shape[1:]) if a.ndim > 2 else a.reshape(1, -1)

    sh = [shard_of("", n) for n, _, _ in _SHARDED]
    wb = _pack([a.astype(BF16) for a in sh[:_N_BF16]], _RB).reshape(2, _RB // 2, LANES)
    ws = _pack(sh[_N_BF16:], _RS)
    gb, gs = _gather_weights(wb, ws)
    parts = _unpack(gb.reshape(4, _RB, LANES), _SH_SHAPES[:_N_BF16]) + _unpack(gs, _SH_SHAPES[_N_BF16:])
    full = {}
    for (name, _, axis), p in zip(_SHARDED, parts):
        full[name] = jnp.concatenate([p[j] for j in range(4)], axis=axis)
    for name, _ in _REPLICATED:
        full[name] = shard_of("", name)
    lw = _layout_weights(full)

    loss_tile, dx, g = _local_step(x[0], positions.reshape(t, 1), loss_target[0], lw)
    grads = _unlayout_grads(g)

    per_chip = []
    for j in range(4):
        pieces = []
        for name, shape, axis in _SHARDED:
            n = shape[axis]
            pieces.append(lax.slice_in_dim(grads[name], j * n, (j + 1) * n, axis=axis))
        per_chip.append(_pack(pieces, _RG))
    gfull = jnp.stack(per_chip).reshape(4, 2, _RG // 2, LANES)
    mine = lax.dynamic_index_in_dim(gfull, c, axis=1, keepdims=False)
    other = lax.dynamic_index_in_dim(gfull, 1 - c, axis=1, keepdims=False)
    from_sibling = _sibling_swap(other, name="core_swap_partial")
    pair = jnp.stack([mine, from_sibling]).reshape(2, 4 * (_RG // 2), LANES)
    chip_sum = _sum_slots(pair, name="core_sum").reshape(4, _RG // 2, LANES)
    small = _pack([grads[n] for n, _ in _REPLICATED] + [loss_tile[0:1, :]], _RR)
    landed, small_all = _chip_exchange(chip_sum, small)
    my_half = _sum_slots(landed, name="chip_sum")
    small_sum = _sum_slots(small_all, name="small_sum")
    sib_half = _sibling_swap(my_half, name="core_swap_sum")
    g_shard = jnp.where(c == 0, jnp.concatenate([my_half, sib_half]), jnp.concatenate([sib_half, my_half]))

    gnat = dict(zip([n for n, _, _ in _SHARDED], _unpack(g_shard, _SH_SHAPES)))
    gnat.update(zip([n for n, _ in _REPLICATED], _unpack(small_sum, _REP_SHAPES)))
    res = {}
    small_names = [n for n in _ALL_NAMES if n not in _ADAM_ROWWISE]
    for name in _ADAM_ROWWISE:
        out = _adamw_rows(gnat[name], shard_of("", name), shard_of("m_", name), shard_of("v_", name), name="adamw_" + name)
        res["delta", name], res["m", name], res["v", name] = out
    out = _adamw_small([gnat[n] for n in small_names], [shard_of("", n) for n in small_names],
                       [shard_of("m_", n) for n in small_names], [shard_of("v_", n) for n in small_names],
                       name="adamw_small")
    for i, name in enumerate(small_names):
        res["delta", name], res["m", name], res["v", name] = out[3 * i:3 * i + 3]
    for name in _ALL_NAMES:
        res["grad", name] = gnat[name]
    res = {k: a.reshape(given[k[1]].shape) for k, a in res.items()}
    loss = small_sum[sum(_rows_of(s) for s in _REP_SHAPES), 0]
    outs = [loss, dx.reshape(x.shape)]
    for key in ("grad", "delta", "m", "v"):
        outs += [res[key, n] for n in _ALL_NAMES]
    return tuple(outs)
```

```python
import functools

import jax
import jax.numpy as jnp
from jax import lax
from jax.experimental import pallas as pl
from jax.experimental.pallas import tpu as pltpu

F32 = jnp.float32
BF16 = jnp.bfloat16
HI = lax.Precision.HIGHEST
MESH = pl.DeviceIdType.MESH

RMS_EPS = 1e-6
D_MODEL = 1024
MLA_HEADS = 8
MLA_Q_RANK = 256
MLA_KV_RANK = 128
MLA_NOPE = 64
MLA_ROPE = 32
MLA_V = 64
ROPE_THETA = 10000.0
POOL_WINDOWS = (2, 4, 8, 16)
POOL_GROUP = 128
POOL_WIDTH = 512
POOL_HALO = 16
GDN_HEADS = 8
GDN_DK = 128
CONV_WIDTH = 4
CONV_HALO = 8
CHUNK = 64
IN_AB_PAD = 2048
IN_C_PAD = 4224
ATT_SCALE = (MLA_NOPE + MLA_ROPE) ** -0.5

ADAM_LR = 0.001
ADAM_B1 = 0.9
ADAM_B2 = 0.999
ADAM_EPS = 1e-08
ADAM_WD = 0.01
ADAM_STEP = 10

LANES = 128
VMEM_LIMIT = 56 * 1024 * 1024

ROW_TILE = 256
ATT_TILE = 1024
GDN_TILE = 256
MM_TILE = (1024, 1408, 2048)

NN = (((1,), (0,)), ((), ()))
NT = (((1,), (1,)), ((), ()))
TN = (((0,), (0,)), ((), ()))


def _dot(a, b, dims=NN, prec=None):
    return lax.dot_general(a, b, dims, precision=prec, preferred_element_type=F32)


def _tile(n, pref):
    if n <= pref:
        return n
    step = LANES if pref >= LANES else 8
    for t in range(pref - pref % step, 0, -step):
        if n % t == 0:
            return t
    return n


def _params(sem):
    return pltpu.CompilerParams(dimension_semantics=sem, vmem_limit_bytes=VMEM_LIMIT)


def _sigmoid(x):
    return 1.0 / (1.0 + jnp.exp(-x))


def _softplus(x):
    return jnp.maximum(x, 0.0) + jnp.log(1.0 + jnp.exp(-jnp.abs(x)))


def _matmul(a, b, mode, *, name, out_dtype=F32, add=None):
    if mode == "nn":
        (m, k), (k2, n) = a.shape, b.shape
    elif mode == "nt":
        (m, k), (n, k2) = a.shape, b.shape
    else:
        (k, m), (k2, n) = a.shape, b.shape
    assert k == k2, (a.shape, b.shape, mode)
    tm, tn, tk = _tile(m, MM_TILE[0]), _tile(n, MM_TILE[1]), _tile(k, MM_TILE[2])
    nk = k // tk
    if mode == "tn":
        a_spec = pl.BlockSpec((tk, tm), lambda i, j, kk: (kk, i))
    else:
        a_spec = pl.BlockSpec((tm, tk), lambda i, j, kk: (i, kk))
    if mode == "nt":
        b_spec = pl.BlockSpec((tn, tk), lambda i, j, kk: (j, kk))
    else:
        b_spec = pl.BlockSpec((tk, tn), lambda i, j, kk: (kk, j))
    o_spec = pl.BlockSpec((tm, tn), lambda i, j, kk: (i, j))
    dims = {"nn": NN, "nt": NT, "tn": TN}[mode]
    has_add = add is not None

    def body(*refs):
        a_ref, b_ref = refs[0], refs[1]
        add_ref = refs[2] if has_add else None
        o_ref = refs[3] if has_add else refs[2]

        def finish(o):
            if has_add:
                o = o + add_ref[...]
            o_ref[...] = o.astype(out_dtype)

        if nk == 1:
            finish(_dot(a_ref[...], b_ref[...], dims))
            return
        acc = refs[-1]
        kk = pl.program_id(2)

        @pl.when(kk == 0)
        def _():
            acc[...] = jnp.zeros_like(acc)

        acc[...] += _dot(a_ref[...], b_ref[...], dims)

        @pl.when(kk == nk - 1)
        def _():
            finish(acc[...])

    in_specs = [a_spec, b_spec] + ([o_spec] if has_add else [])
    args = (a, b) + ((add,) if has_add else ())
    return pl.pallas_call(
        body, name=name, grid=(m // tm, n // tn, nk), in_specs=in_specs, out_specs=o_spec,
        out_shape=jax.ShapeDtypeStruct((m, n), out_dtype),
        scratch_shapes=[pltpu.VMEM((tm, tn), F32)] if nk > 1 else [],
        compiler_params=_params(("parallel", "parallel", "arbitrary")),
    )(*args)


def _rms_fwd(h, g, *, name):
    t, d = h.shape
    tm = _tile(t, 2 * ROW_TILE)

    def body(h_ref, g_ref, o_ref):
        x = h_ref[...]
        r = lax.rsqrt(jnp.mean(x * x, axis=-1, keepdims=True) + RMS_EPS)
        o_ref[...] = (x * r * g_ref[...]).astype(BF16)

    return pl.pallas_call(
        body, name=name, grid=(t // tm,),
        in_specs=[pl.BlockSpec((tm, d), lambda i: (i, 0)), pl.BlockSpec((1, d), lambda i: (0, 0))],
        out_specs=pl.BlockSpec((tm, d), lambda i: (i, 0)),
        out_shape=jax.ShapeDtypeStruct((t, d), BF16), compiler_params=_params(("parallel",)),
    )(h, g)


def _rms_bwd(h, g, dy, dres, *, name, with_bf16):
    t, d = h.shape
    tm = _tile(t, 2 * ROW_TILE)

    def body(h_ref, g_ref, dy_ref, dres_ref, *outs):
        i = pl.program_id(0)
        dh_ref, dg_ref = outs[0], outs[-1]
        x = h_ref[...]
        r = lax.rsqrt(jnp.mean(x * x, axis=-1, keepdims=True) + RMS_EPS)
        xh = x * r
        dyv = dy_ref[...].astype(F32)
        dxh = dyv * g_ref[...]
        dx = r * (dxh - xh * jnp.mean(dxh * xh, axis=-1, keepdims=True))
        dh = dres_ref[...] + dx
        dh_ref[...] = dh
        if with_bf16:
            outs[1][...] = dh.astype(BF16)

        @pl.when(i == 0)
        def _():
            dg_ref[...] = jnp.zeros_like(dg_ref)

        dg_ref[...] += jnp.sum(dyv * xh, axis=0, keepdims=True)

    row = pl.BlockSpec((tm, d), lambda i: (i, 0))
    vec = pl.BlockSpec((1, d), lambda i: (0, 0))
    out_shape = [jax.ShapeDtypeStruct((t, d), F32)]
    out_specs = [row]
    if with_bf16:
        out_shape.append(jax.ShapeDtypeStruct((t, d), BF16))
        out_specs.append(row)
    out_shape.append(jax.ShapeDtypeStruct((1, d), F32))
    out_specs.append(vec)
    return pl.pallas_call(
        body, name=name, grid=(t // tm,), in_specs=[row, vec, row, row], out_specs=out_specs,
        out_shape=out_shape, compiler_params=_params(("arbitrary",)),
    )(h, g, dy, dres)


def _rope_partner(x):
    lane = lax.broadcasted_iota(jnp.int32, x.shape, 1)
    swapped = jnp.where(lane < MLA_NOPE + MLA_ROPE // 2, pltpu.roll(x, LANES - 16, 1), pltpu.roll(x, 16, 1))
    return jnp.where((lane >= MLA_NOPE) & (lane < MLA_NOPE + MLA_ROPE), swapped, 0.0)


def _pool_counts(row0, tm, w):
    t_idx = row0 + lax.broadcasted_iota(jnp.int32, (tm, POOL_GROUP), 0)
    return jnp.minimum(t_idx + 1, w).astype(F32)


def _ab_prep(proj, pos, inv_freq, q_a_norm, kv_a_norm, *, name):
    t = proj.shape[0]
    tm = _tile(t, ROW_TILE)
    hb = tm // POOL_HALO

    def body(p_ref, halo_ref, pos_ref, inv_ref, qg_ref, kg_ref, qn_ref, kvn_ref, kr_ref, d_ref, cos_ref, sin_ref, ext):
        i = pl.program_id(0)
        ql = p_ref[:, 0:MLA_Q_RANK]
        r = lax.rsqrt(jnp.mean(ql * ql, axis=-1, keepdims=True) + RMS_EPS)
        qn_ref[...] = (ql * r * qg_ref[...]).astype(BF16)
        kl = p_ref[:, MLA_Q_RANK:MLA_Q_RANK + MLA_KV_RANK]
        r = lax.rsqrt(jnp.mean(kl * kl, axis=-1, keepdims=True) + RMS_EPS)
        kvn_ref[...] = (kl * r * kg_ref[...]).astype(BF16)
        ang = pos_ref[...].astype(F32) * inv_ref[...]
        lane = lax.broadcasted_iota(jnp.int32, (tm, LANES), 1)
        in_rope = (lane >= MLA_NOPE) & (lane < MLA_NOPE + MLA_ROPE)
        cos_t = jnp.where(in_rope, jnp.cos(ang), 1.0)
        sin_t = jnp.where(in_rope, jnp.sin(ang), 0.0)
        sin_t = jnp.where(lane < MLA_NOPE + MLA_ROPE // 2, -sin_t, sin_t)
        cos_ref[...] = cos_t
        sin_ref[...] = sin_t
        kr = p_ref[:, 384:512]
        kr_ref[...] = kr * cos_t + _rope_partner(kr) * sin_t
        xp = p_ref[:, 512:1024]
        ext[0:POOL_HALO, :] = jnp.where(i > 0, halo_ref[...], 0.0)
        ext[POOL_HALO:POOL_HALO + tm, :] = xp
        for g, w in enumerate(POOL_WINDOWS):
            lo = g * POOL_GROUP
            acc = ext[POOL_HALO:POOL_HALO + tm, lo:lo + POOL_GROUP]
            for s in range(1, w):
                acc = acc + ext[POOL_HALO - s:POOL_HALO - s + tm, lo:lo + POOL_GROUP]
            cnt = _pool_counts(i * tm, tm, w)
            d_ref[:, lo:lo + POOL_GROUP] = (acc / cnt - xp[:, lo:lo + POOL_GROUP]).astype(BF16)

    row = lambda w: pl.BlockSpec((tm, w), lambda i: (i, 0))
    vec = lambda w: pl.BlockSpec((1, w), lambda i: (0, 0))
    return pl.pallas_call(
        body, name=name, grid=(t // tm,),
        in_specs=[row(1024), pl.BlockSpec((POOL_HALO, POOL_WIDTH), lambda i: (jnp.maximum(i * hb - 1, 0), 1)),
                  pl.BlockSpec((tm, 1), lambda i: (i, 0)), vec(LANES), vec(MLA_Q_RANK), vec(MLA_KV_RANK)],
        out_specs=[row(MLA_Q_RANK), row(MLA_KV_RANK), row(LANES), row(POOL_WIDTH), row(LANES), row(LANES)],
        out_shape=[jax.ShapeDtypeStruct((t, MLA_Q_RANK), BF16), jax.ShapeDtypeStruct((t, MLA_KV_RANK), BF16),
                   jax.ShapeDtypeStruct((t, LANES), F32), jax.ShapeDtypeStruct((t, POOL_WIDTH), BF16),
                   jax.ShapeDtypeStruct((t, LANES), F32), jax.ShapeDtypeStruct((t, LANES), F32)],
        scratch_shapes=[pltpu.VMEM((tm + POOL_HALO, POOL_WIDTH), F32)],
        compiler_params=_params(("parallel",)),
    )(proj, proj, pos, inv_freq, q_a_norm, kv_a_norm)


def _qk_rope(qraw, kvk, kr, cos_t, sin_t, *, name):
    t = qraw.shape[0]
    tm = _tile(t, 2 * ROW_TILE)

    def body(q_ref, k_ref, kr_ref, c_ref, s_ref, qo_ref, ko_ref):
        c, s, krv = c_ref[...], s_ref[...], kr_ref[...]
        for h in range(MLA_HEADS):
            sl = slice(h * LANES, (h + 1) * LANES)
            q = q_ref[:, sl]
            qo_ref[:, sl] = ((q * c + _rope_partner(q) * s) * ATT_SCALE).astype(BF16)
            ko_ref[:, sl] = (k_ref[:, sl] + krv).astype(BF16)

    row = lambda w: pl.BlockSpec((tm, w), lambda i: (i, 0))
    return pl.pallas_call(
        body, name=name, grid=(t // tm,), in_specs=[row(1024), row(1024), row(LANES), row(LANES), row(LANES)],
        out_specs=[row(1024), row(1024)],
        out_shape=[jax.ShapeDtypeStruct((t, 1024), BF16), jax.ShapeDtypeStruct((t, 1024), BF16)],
        compiler_params=_params(("parallel",)),
    )(qraw, kvk, kr, cos_t, sin_t)


def _qk_rope_bwd(dq, dk, cos_t, sin_t, *, name):
    t = dq.shape[0]
    tm = _tile(t, 2 * ROW_TILE)

    def body(dq_ref, dk_ref, c_ref, s_ref, dqo_ref, dko_ref, dkr_ref):
        c, s = c_ref[...], s_ref[...]
        lane = lax.broadcasted_iota(jnp.int32, (tm, LANES), 1)
        in_rope = (lane >= MLA_NOPE) & (lane < MLA_NOPE + MLA_ROPE)
        dkr = jnp.zeros((tm, LANES), F32)
        for h in range(MLA_HEADS):
            sl = slice(h * LANES, (h + 1) * LANES)
            g = dq_ref[:, sl]
            dqo_ref[:, sl] = ((g * c + _rope_partner(g * s)) * ATT_SCALE).astype(BF16)
            gk = dk_ref[:, sl]
            dko_ref[:, sl] = gk.astype(BF16)
            dkr = dkr + jnp.where(in_rope, gk, 0.0)
        dkr_ref[...] = dkr * c + _rope_partner(dkr * s)

    row = lambda w: pl.BlockSpec((tm, w), lambda i: (i, 0))
    return pl.pallas_call(
        body, name=name, grid=(t // tm,), in_specs=[row(1024), row(1024), row(LANES), row(LANES)],
        out_specs=[row(1024), row(1024), row(LANES)],
        out_shape=[jax.ShapeDtypeStruct((t, 1024), BF16), jax.ShapeDtypeStruct((t, 1024), BF16),
                   jax.ShapeDtypeStruct((t, LANES), F32)],
        compiler_params=_params(("parallel",)),
    )(dq, dk, cos_t, sin_t)


def _ab_prep_bwd(proj, q_a_norm, kv_a_norm, dqn, dkvn_k, dkvn_v, dkr, dd, dz, *, name):
    t = proj.shape[0]
    tm = _tile(t, ROW_TILE)
    hb = tm // POOL_HALO
    last_halo = t // POOL_HALO - 1
    nt = t // tm

    def body(p_ref, qg_ref, kg_ref, dqn_ref, dk1_ref, dk2_ref, dkr_ref, dd_ref, ddn_ref, dz_ref,
             dp_ref, dqg_ref, dkg_ref, ext):
        i = pl.program_id(0)

        @pl.when(i == 0)
        def _():
            dqg_ref[...] = jnp.zeros_like(dqg_ref)
            dkg_ref[...] = jnp.zeros_like(dkg_ref)

        def norm_bwd(x, g, dy, dg_ref):
            r = lax.rsqrt(jnp.mean(x * x, axis=-1, keepdims=True) + RMS_EPS)
            xh = x * r
            dxh = dy * g
            dg_ref[...] += jnp.sum(dy * xh, axis=0, keepdims=True)
            return r * (dxh - xh * jnp.mean(dxh * xh, axis=-1, keepdims=True))

        dql = norm_bwd(p_ref[:, 0:MLA_Q_RANK], qg_ref[...], dqn_ref[...], dqg_ref)
        dp_ref[:, 0:MLA_Q_RANK] = dql.astype(BF16)
        dkl = norm_bwd(p_ref[:, MLA_Q_RANK:384], kg_ref[...], dk1_ref[...] + dk2_ref[...], dkg_ref)
        dp_ref[:, MLA_Q_RANK:384] = dkl.astype(BF16)
        dp_ref[:, 384:512] = dkr_ref[...].astype(BF16)
        ddv = dd_ref[...]
        for g, w in enumerate(POOL_WINDOWS):
            lo = g * POOL_GROUP
            ext[0:tm, lo:lo + POOL_GROUP] = ddv[:, lo:lo + POOL_GROUP] / _pool_counts(i * tm, tm, w)
            nxt = ddn_ref[:, lo:lo + POOL_GROUP] / _pool_counts((i + 1) * tm, POOL_HALO, w)
            ext[tm:tm + POOL_HALO, lo:lo + POOL_GROUP] = jnp.where(i < nt - 1, nxt, 0.0)
        for g, w in enumerate(POOL_WINDOWS):
            lo = g * POOL_GROUP
            acc = ext[0:tm, lo:lo + POOL_GROUP]
            for s in range(1, w):
                acc = acc + ext[s:s + tm, lo:lo + POOL_GROUP]
            dp_ref[:, 512 + lo:512 + lo + POOL_GROUP] = (acc - ddv[:, lo:lo + POOL_GROUP]).astype(BF16)
        dp_ref[:, 1024:2048] = dz_ref[...]

    row = lambda w: pl.BlockSpec((tm, w), lambda i: (i, 0))
    vec = lambda w: pl.BlockSpec((1, w), lambda i: (0, 0))
    return pl.pallas_call(
        body, name=name, grid=(nt,),
        in_specs=[row(1024), vec(MLA_Q_RANK), vec(MLA_KV_RANK), row(MLA_Q_RANK), row(MLA_KV_RANK), row(MLA_KV_RANK),
                  row(LANES), row(POOL_WIDTH),
                  pl.BlockSpec((POOL_HALO, POOL_WIDTH), lambda i: (jnp.minimum((i + 1) * hb, last_halo), 0)),
                  row(1024)],
        out_specs=[row(IN_AB_PAD), vec(MLA_Q_RANK), vec(MLA_KV_RANK)],
        out_shape=[jax.ShapeDtypeStruct((t, IN_AB_PAD), BF16), jax.ShapeDtypeStruct((1, MLA_Q_RANK), F32),
                   jax.ShapeDtypeStruct((1, MLA_KV_RANK), F32)],
        scratch_shapes=[pltpu.VMEM((tm + POOL_HALO, POOL_WIDTH), F32)],
        compiler_params=_params(("arbitrary",)),
    )(proj, q_a_norm, kv_a_norm, dqn, dkvn_k, dkvn_v, dkr, dd, dd, dz)


def _gate_fwd(o, ybraw, proj, pool_scale, *, name):
    t = o.shape[0]
    tm = _tile(t, 2 * ROW_TILE)

    def body(o_ref, yb_ref, z_ref, ps_ref, y_ref):
        z = z_ref[...]
        sz = z * _sigmoid(z)
        y_ref[:, 0:512] = (o_ref[...] * sz[:, 0:512]).astype(BF16)
        y_ref[:, 512:1024] = (yb_ref[...] * ps_ref[...] * sz[:, 512:1024]).astype(BF16)

    row = lambda w: pl.BlockSpec((tm, w), lambda i: (i, 0))
    return pl.pallas_call(
        body, name=name, grid=(t // tm,),
        in_specs=[row(512), row(512), pl.BlockSpec((tm, 1024), lambda i: (i, 1)), pl.BlockSpec((1, 512), lambda i: (0, 0))],
        out_specs=row(1024), out_shape=jax.ShapeDtypeStruct((t, 1024), BF16), compiler_params=_params(("parallel",)),
    )(o, ybraw, proj, pool_scale)


def _gate_bwd(dy, o, ybraw, proj, pool_scale, *, name):
    t = o.shape[0]
    tm = _tile(t, ROW_TILE)

    def body(dy_ref, o_ref, yb_ref, z_ref, ps_ref, do_ref, dl_ref, dyb_ref, dz_ref, dps_ref):
        i = pl.program_id(0)
        z = z_ref[...]
        sg = _sigmoid(z)
        sz = z * sg
        dsz = sg * (1.0 + z * (1.0 - sg))
        dyv = dy_ref[...]
        dcat = dyv * sz
        ov = o_ref[...]
        ybs = yb_ref[...] * ps_ref[...]
        dz_ref[:, 0:512] = (dyv[:, 0:512] * ov * dsz[:, 0:512]).astype(BF16)
        dz_ref[:, 512:1024] = (dyv[:, 512:1024] * ybs * dsz[:, 512:1024]).astype(BF16)
        do = dcat[:, 0:512]
        do_ref[...] = do.astype(BF16)
        r_i = lax.broadcasted_iota(jnp.int32, (512, 512), 0) // MLA_V
        c_i = lax.broadcasted_iota(jnp.int32, (512, 512), 1) // MLA_V
        dl_ref[...] = _dot(do * ov, (r_i == c_i).astype(F32), NN, HI)
        dyb_ref[...] = (dcat[:, 512:1024] * ps_ref[...]).astype(BF16)

        @pl.when(i == 0)
        def _():
            dps_ref[...] = jnp.zeros_like(dps_ref)

        dps_ref[...] += jnp.sum(dcat[:, 512:1024] * yb_ref[...], axis=0, keepdims=True)

    row = lambda w: pl.BlockSpec((tm, w), lambda i: (i, 0))
    vec = pl.BlockSpec((1, 512), lambda i: (0, 0))
    return pl.pallas_call(
        body, name=name, grid=(t // tm,),
        in_specs=[row(1024), row(512), row(512), pl.BlockSpec((tm, 1024), lambda i: (i, 1)), vec],
        out_specs=[row(512), row(512), row(512), row(1024), vec],
        out_shape=[jax.ShapeDtypeStruct((t, 512), BF16), jax.ShapeDtypeStruct((t, 512), F32),
                   jax.ShapeDtypeStruct((t, 512), BF16), jax.ShapeDtypeStruct((t, 1024), BF16),
                   jax.ShapeDtypeStruct((1, 512), F32)],
        compiler_params=_params(("arbitrary",)),
    )(dy, o, ybraw, proj, pool_scale)


ATT_HP_FWD = 4
ATT_HP_BWD = 2


def _diag_mask(tq):
    return lax.broadcasted_iota(jnp.int32, (tq, tq), 1) <= lax.broadcasted_iota(jnp.int32, (tq, tq), 0)


def _block_schedule(nq, key_major):
    if key_major:
        pairs = [(qi, ki) for ki in range(nq) for qi in range(ki, nq)]
    else:
        pairs = [(qi, ki) for qi in range(nq) for ki in range(qi + 1)]
    return jnp.asarray([p[0] for p in pairs], jnp.int32), jnp.asarray([p[1] for p in pairs], jnp.int32)


def _attn_fwd(q, k, v, *, name):
    t = q.shape[0]
    tq = _tile(t, ATT_TILE)
    nq = t // tq
    hp = ATT_HP_FWD
    qi_tab, ki_tab = _block_schedule(nq, key_major=False)

    def body(qi_ref, ki_ref, q_ref, k_ref, v_ref, o_ref, lse_ref, m_sc, l_sc, acc_sc):
        step = pl.program_id(1)
        qi, ki = qi_ref[step], ki_ref[step]

        @pl.when(ki == 0)
        def _():
            m_sc[...] = jnp.full_like(m_sc, -jnp.inf)
            l_sc[...] = jnp.zeros_like(l_sc)
            acc_sc[...] = jnp.zeros_like(acc_sc)

        def block(on_diagonal):
            scores = []
            for h in range(hp):
                sl = slice(h * LANES, (h + 1) * LANES)
                scores.append(_dot(q_ref[:, sl], k_ref[:, sl], NT))
            if on_diagonal:
                mask = _diag_mask(tq)
                scores = [jnp.where(mask, s, -jnp.inf) for s in scores]
            for h, s in enumerate(scores):
                vv = v_ref[:, (h // 2) * LANES:(h // 2 + 1) * LANES]
                m_prev = m_sc[h]
                m_new = jnp.maximum(m_prev, jnp.max(s, axis=-1, keepdims=True))
                alpha = jnp.exp(m_prev - m_new)
                p = jnp.exp(s - m_new[:, 0:1])
                l_sc[h] = alpha * l_sc[h] + jnp.sum(p, axis=-1, keepdims=True)
                acc_sc[h] = alpha * acc_sc[h] + _dot(p.astype(BF16), vv)
                m_sc[h] = m_new

        pl.when(ki < qi)(functools.partial(block, False))
        pl.when(ki == qi)(functools.partial(block, True))

        @pl.when(ki == qi)
        def _():
            first = lax.broadcasted_iota(jnp.int32, (tq, LANES), 1) < MLA_V
            for pr in range(hp // 2):
                a, b = 2 * pr, 2 * pr + 1
                sl = slice(pr * LANES, (pr + 1) * LANES)
                o_ref[:, sl] = jnp.where(first, acc_sc[a] / l_sc[a], acc_sc[b] / l_sc[b])
                lse_ref[:, sl] = jnp.where(first, m_sc[a] + jnp.log(l_sc[a]), m_sc[b] + jnp.log(l_sc[b]))

    grid_spec = pltpu.PrefetchScalarGridSpec(
        num_scalar_prefetch=2, grid=(MLA_HEADS // hp, qi_tab.shape[0]),
        in_specs=[pl.BlockSpec((tq, hp * LANES), lambda g, s, qt, kt: (qt[s], g)),
                  pl.BlockSpec((tq, hp * LANES), lambda g, s, qt, kt: (kt[s], g)),
                  pl.BlockSpec((tq, hp * MLA_V), lambda g, s, qt, kt: (kt[s], g))],
        out_specs=[pl.BlockSpec((tq, hp * MLA_V), lambda g, s, qt, kt: (qt[s], g)),
                   pl.BlockSpec((tq, hp * MLA_V), lambda g, s, qt, kt: (qt[s], g))],
        scratch_shapes=[pltpu.VMEM((hp, tq, LANES), F32)] * 3,
    )
    return pl.pallas_call(
        body, name=name, grid_spec=grid_spec,
        out_shape=[jax.ShapeDtypeStruct((t, 512), F32), jax.ShapeDtypeStruct((t, 512), F32)],
        compiler_params=_params(("parallel", "arbitrary")),
    )(qi_tab, ki_tab, q, k, v)


def _attn_bwd(q, k, v, do, lse, delta, *, name):
    t = q.shape[0]
    tq = _tile(t, ATT_TILE)
    nq = t // tq
    hp = ATT_HP_BWD
    qi_tab, ki_tab = _block_schedule(nq, key_major=True)

    def body(qi_ref, ki_ref, q_ref, k_ref, v_ref, do_ref, lse_ref, dl_ref, dq_ref, dk_ref, dv_ref, dk_sc, dv_sc):
        step = pl.program_id(1)
        qi, ki = qi_ref[step], ki_ref[step]

        @pl.when(step == 0)
        def _():
            dq_ref[...] = jnp.zeros_like(dq_ref)

        @pl.when(qi == ki)
        def _():
            dk_sc[...] = jnp.zeros_like(dk_sc)
            dv_sc[...] = jnp.zeros_like(dv_sc)

        def block(on_diagonal):
            lane = lax.broadcasted_iota(jnp.int32, (tq, LANES), 1)
            rows = pl.ds(pl.multiple_of(qi * tq, tq), tq)
            heads = [slice(h * LANES, (h + 1) * LANES) for h in range(hp)]
            scores = [_dot(q_ref[:, sl], k_ref[:, sl], NT) for sl in heads]
            dps = []
            for h in range(hp):
                dov = do_ref[:, (h // 2) * LANES:(h // 2 + 1) * LANES]
                mine = (lane < MLA_V) if h % 2 == 0 else (lane >= MLA_V)
                dps.append(_dot(jnp.where(mine, dov, jnp.zeros_like(dov)), v_ref[:, (h // 2) * LANES:(h // 2 + 1) * LANES], NT))
            mask = _diag_mask(tq) if on_diagonal else None
            for h, sl in enumerate(heads):
                col = (h // 2) * LANES + (h % 2) * MLA_V
                p = jnp.exp(scores[h] - lse_ref[:, col:col + 1])
                if on_diagonal:
                    p = jnp.where(mask, p, 0.0)
                ds = (p * (dps[h] - dl_ref[:, col:col + 1])).astype(BF16)
                dv_sc[h] += _dot(p.astype(BF16), do_ref[:, (h // 2) * LANES:(h // 2 + 1) * LANES], TN)
                dk_sc[h] += _dot(ds, q_ref[:, sl], TN)
                dq_ref[rows, sl] += _dot(ds, k_ref[:, sl], NN)

        pl.when(qi > ki)(functools.partial(block, False))
        pl.when(qi == ki)(functools.partial(block, True))

        @pl.when(qi == nq - 1)
        def _():
            first = lax.broadcasted_iota(jnp.int32, (tq, LANES), 1) < MLA_V
            for h in range(hp):
                dk_ref[:, h * LANES:(h + 1) * LANES] = dk_sc[h]
            for pr in range(hp // 2):
                dv_ref[:, pr * LANES:(pr + 1) * LANES] = jnp.where(first, dv_sc[2 * pr], dv_sc[2 * pr + 1]).astype(BF16)

    qrow = lambda w: pl.BlockSpec((tq, w), lambda g, s, qt, kt: (qt[s], g))
    krow = lambda w: pl.BlockSpec((tq, w), lambda g, s, qt, kt: (kt[s], g))
    grid_spec = pltpu.PrefetchScalarGridSpec(
        num_scalar_prefetch=2, grid=(MLA_HEADS // hp, qi_tab.shape[0]),
        in_specs=[qrow(hp * LANES), krow(hp * LANES), krow(hp * MLA_V), qrow(hp * MLA_V), qrow(hp * MLA_V), qrow(hp * MLA_V)],
        out_specs=[pl.BlockSpec((t, hp * LANES), lambda g, s, qt, kt: (0, g)), krow(hp * LANES), krow(hp * MLA_V)],
        scratch_shapes=[pltpu.VMEM((hp, tq, LANES), F32), pltpu.VMEM((hp, tq, LANES), F32)],
    )
    return pl.pallas_call(
        body, name=name, grid_spec=grid_spec,
        out_shape=[jax.ShapeDtypeStruct((t, 1024), F32), jax.ShapeDtypeStruct((t, 1024), F32),
                   jax.ShapeDtypeStruct((t, 512), BF16)],
        compiler_params=_params(("parallel", "arbitrary")),
    )(qi_tab, ki_tab, q, k, v, do, lse, delta)


def _conv_rows(ext, tm, w_ref, sec):
    c0 = sec * 1024
    y = ext[CONV_HALO - 3:CONV_HALO - 3 + tm, c0:c0 + 1024] * w_ref[0:1, c0:c0 + 1024]
    for j in range(1, CONV_WIDTH):
        y = y + ext[CONV_HALO - 3 + j:CONV_HALO - 3 + j + tm, c0:c0 + 1024] * w_ref[j:j + 1, c0:c0 + 1024]
    return y


def _c_prep(proj_c, conv_w, a_log, dt_bias, *, name):
    t = proj_c.shape[0]
    tm = _tile(t, ROW_TILE)
    hb = tm // CONV_HALO

    def body(p_ref, halo_ref, ab_ref, w_ref, al_ref, dtb_ref, q_ref, k_ref, v_ref, g_ref, b_ref, gt_ref, ext):
        i = pl.program_id(0)
        ext[0:CONV_HALO, :] = jnp.where(i > 0, halo_ref[...], 0.0)
        ext[CONV_HALO:CONV_HALO + tm, :] = p_ref[...]
        for sec, o_ref in enumerate((q_ref, k_ref, v_ref)):
            y = _conv_rows(ext, tm, w_ref, sec)
            y = y * _sigmoid(y)
            if sec == 2:
                o_ref[...] = y
                continue
            scale = GDN_DK ** -0.5 if sec == 0 else 1.0
            for h in range(GDN_HEADS):
                sl = slice(h * LANES, (h + 1) * LANES)
                blk = y[:, sl]
                r = lax.rsqrt(jnp.sum(blk * blk, axis=-1, keepdims=True) + RMS_EPS)
                o_ref[:, sl] = blk * (r * scale)
        ab = ab_ref[...]
        g = -jnp.exp(al_ref[...]) * _softplus(ab + dtb_ref[...])
        beta = _sigmoid(ab)
        ri = lax.broadcasted_iota(jnp.int32, (tm, tm), 0)
        ci = lax.broadcasted_iota(jnp.int32, (tm, tm), 1)
        lower = ((ri // CHUNK) == (ci // CHUNK)) & (ri >= ci)
        gc = _dot(lower.astype(F32), g, NN, HI)
        eye = lax.broadcasted_iota(jnp.int32, (LANES, LANES), 0) == lax.broadcasted_iota(jnp.int32, (LANES, LANES), 1)
        gt_ref[...] = _dot(eye.astype(F32), gc, NT, HI)[0:GDN_HEADS, :]
        for h in range(GDN_HEADS):
            sl = slice(h * LANES, (h + 1) * LANES)
            g_ref[:, sl] = jnp.broadcast_to(gc[:, h:h + 1], (tm, LANES))
            b_ref[:, sl] = jnp.broadcast_to(beta[:, GDN_HEADS + h:GDN_HEADS + h + 1], (tm, LANES))

    row = lambda w: pl.BlockSpec((tm, w), lambda i: (i, 0))
    vec = lambda r, w: pl.BlockSpec((r, w), lambda i: (0, 0))
    out = jax.ShapeDtypeStruct((t, 1024), F32)
    return pl.pallas_call(
        body, name=name, grid=(t // tm,),
        in_specs=[row(3072), pl.BlockSpec((CONV_HALO, 3072), lambda i: (jnp.maximum(i * hb - 1, 0), 0)),
                  pl.BlockSpec((tm, LANES), lambda i: (i, 32)), vec(CONV_WIDTH, 3072), vec(1, LANES), vec(1, LANES)],
        out_specs=[row(1024)] * 5 + [pl.BlockSpec((GDN_HEADS, tm), lambda i: (0, i))],
        out_shape=[out] * 5 + [jax.ShapeDtypeStruct((GDN_HEADS, t), F32)],
        scratch_shapes=[pltpu.VMEM((tm + CONV_HALO, 3072), F32)],
        compiler_params=_params(("parallel",)),
    )(proj_c, proj_c, proj_c, conv_w, a_log, dt_bias)


def _c_prep_bwd(proj_c, conv_w, a_log, dt_bias, dq, dk, dv, dgb, dbb, dz, *, name):
    t = proj_c.shape[0]
    tm = _tile(t, ROW_TILE // 2)
    hb = tm // CONV_HALO
    nt = t // tm
    rev = lambda i: nt - 1 - i

    def body(p_ref, halo_ref, ab_ref, w_ref, al_ref, dtb_ref, dq_ref, dk_ref, dv_ref, dg_ref, db_ref, dz_ref,
             dp_ref, dw_ref, dal_ref, ddt_ref, ext, dyext, carry, taps):
        step = pl.program_id(0)
        i = rev(step)

        @pl.when(step == 0)
        def _():
            dw_ref[...] = jnp.zeros_like(dw_ref)
            dal_ref[...] = jnp.zeros_like(dal_ref)
            ddt_ref[...] = jnp.zeros_like(ddt_ref)
            carry[...] = jnp.zeros_like(carry)

        ext[0:CONV_HALO, :] = jnp.where(i > 0, halo_ref[...], 0.0)
        ext[CONV_HALO:CONV_HALO + tm, :] = p_ref[...]
        for sec, g_ref in enumerate((dq_ref, dk_ref, dv_ref)):
            c0 = sec * 1024
            for j in range(CONV_WIDTH):
                taps[j] = ext[CONV_HALO - 3 + j:CONV_HALO - 3 + j + tm, c0:c0 + 1024]
            y = taps[0] * w_ref[0:1, c0:c0 + 1024]
            for j in range(1, CONV_WIDTH):
                y = y + taps[j] * w_ref[j:j + 1, c0:c0 + 1024]
            sg = _sigmoid(y)
            act = y * sg
            if sec == 2:
                dact = g_ref[...]
            else:
                scale = GDN_DK ** -0.5 if sec == 0 else 1.0
                parts = []
                for h in range(GDN_HEADS):
                    sl = slice(h * LANES, (h + 1) * LANES)
                    blk = act[:, sl]
                    r = lax.rsqrt(jnp.sum(blk * blk, axis=-1, keepdims=True) + RMS_EPS)
                    n = blk * r
                    dn = g_ref[:, sl] * scale
                    parts.append(r * (dn - n * jnp.sum(dn * n, axis=-1, keepdims=True)))
                dact = jnp.concatenate(parts, axis=-1)
            dy = dact * (sg * (1.0 + y * (1.0 - sg)))
            dyext[0:tm, c0:c0 + 1024] = dy
            for j in range(CONV_WIDTH):
                dw_ref[j:j + 1, c0:c0 + 1024] += jnp.sum(dy * taps[j], axis=0, keepdims=True)
        dyext[tm:tm + CONV_HALO, :] = carry[...]
        carry[...] = dyext[0:CONV_HALO, :]
        for sec in range(3):
            c0 = sec * 1024
            dx = dyext[3:3 + tm, c0:c0 + 1024] * w_ref[0:1, c0:c0 + 1024]
            for j in range(1, CONV_WIDTH):
                dx = dx + dyext[3 - j:3 - j + tm, c0:c0 + 1024] * w_ref[j:j + 1, c0:c0 + 1024]
            dp_ref[:, c0:c0 + 1024] = dx.astype(BF16)
        dp_ref[:, 3072:4096] = dz_ref[...]
        lane = lax.broadcasted_iota(jnp.int32, (tm, LANES), 1)
        dg = jnp.zeros((tm, LANES), F32)
        dbeta = jnp.zeros((tm, LANES), F32)
        for h in range(GDN_HEADS):
            sl = slice(h * LANES, (h + 1) * LANES)
            dg = dg + jnp.where(lane == h, dg_ref[:, sl], 0.0)
            dbeta = dbeta + jnp.where(lane == GDN_HEADS + h, db_ref[:, sl], 0.0)
        ri = lax.broadcasted_iota(jnp.int32, (tm, tm), 0)
        ci = lax.broadcasted_iota(jnp.int32, (tm, tm), 1)
        upper = ((ri // CHUNK) == (ci // CHUNK)) & (ri <= ci)
        dg = _dot(upper.astype(F32), dg, NN, HI)
        pre = ab_ref[...] + dtb_ref[...]
        s = _sigmoid(pre)
        a_exp = jnp.exp(al_ref[...])
        dg_da = dg * (-a_exp * s)
        dp_ref[:, 4096:IN_C_PAD] = (dg_da + dbeta * s * (1.0 - s)).astype(BF16)
        dal_ref[...] += jnp.sum(dg * (-a_exp * _softplus(pre)), axis=0, keepdims=True)
        ddt_ref[...] += jnp.sum(dg_da, axis=0, keepdims=True)

    row = lambda w: pl.BlockSpec((tm, w), lambda s: (rev(s), 0))
    vec = lambda r, w: pl.BlockSpec((r, w), lambda s: (0, 0))
    return pl.pallas_call(
        body, name=name, grid=(nt,),
        in_specs=[row(3072), pl.BlockSpec((CONV_HALO, 3072), lambda s: (jnp.maximum(rev(s) * hb - 1, 0), 0)),
                  pl.BlockSpec((tm, LANES), lambda s: (rev(s), 32)), vec(CONV_WIDTH, 3072), vec(1, LANES), vec(1, LANES),
                  row(1024), row(1024), row(1024), row(1024), row(1024), row(1024)],
        out_specs=[row(IN_C_PAD), vec(CONV_WIDTH, 3072), vec(1, LANES), vec(1, LANES)],
        out_shape=[jax.ShapeDtypeStruct((t, IN_C_PAD), BF16), jax.ShapeDtypeStruct((CONV_WIDTH, 3072), F32),
                   jax.ShapeDtypeStruct((1, LANES), F32), jax.ShapeDtypeStruct((1, LANES), F32)],
        scratch_shapes=[pltpu.VMEM((tm + CONV_HALO, 3072), F32), pltpu.VMEM((tm + CONV_HALO, 3072), F32),
                        pltpu.VMEM((CONV_HALO, 3072), F32), pltpu.VMEM((CONV_WIDTH, tm, 1024), F32)],
        compiler_params=_params(("arbitrary",)),
    )(proj_c, proj_c, proj_c, conv_w, a_log, dt_bias, dq, dk, dv, dgb, dbb, dz)


def _o_gate_fwd(o, proj_c, o_norm, *, name):
    t = o.shape[0]
    tm = _tile(t, 2 * ROW_TILE)

    def body(o_ref, z_ref, g_ref, y_ref):
        for h in range(GDN_HEADS):
            sl = slice(h * LANES, (h + 1) * LANES)
            x = o_ref[:, sl]
            r = lax.rsqrt(jnp.mean(x * x, axis=-1, keepdims=True) + RMS_EPS)
            z = z_ref[:, sl]
            y_ref[:, sl] = (x * r * g_ref[...] * (z * _sigmoid(z))).astype(BF16)

    row = pl.BlockSpec((tm, 1024), lambda i: (i, 0))
    return pl.pallas_call(
        body, name=name, grid=(t // tm,),
        in_specs=[row, pl.BlockSpec((tm, 1024), lambda i: (i, 3)), pl.BlockSpec((1, LANES), lambda i: (0, 0))],
        out_specs=row, out_shape=jax.ShapeDtypeStruct((t, 1024), BF16), compiler_params=_params(("parallel",)),
    )(o, proj_c, o_norm)


def _o_gate_bwd(dy, o, proj_c, o_norm, *, name):
    t = o.shape[0]
    tm = _tile(t, 2 * ROW_TILE)

    def body(dy_ref, o_ref, z_ref, g_ref, do_ref, dz_ref, dg_ref):
        i = pl.program_id(0)

        @pl.when(i == 0)
        def _():
            dg_ref[...] = jnp.zeros_like(dg_ref)

        dg = jnp.zeros((1, LANES), F32)
        for h in range(GDN_HEADS):
            sl = slice(h * LANES, (h + 1) * LANES)
            x = o_ref[:, sl]
            r = lax.rsqrt(jnp.mean(x * x, axis=-1, keepdims=True) + RMS_EPS)
            xh = x * r
            z = z_ref[:, sl]
            sg = _sigmoid(z)
            dyv = dy_ref[:, sl]
            dn = dyv * (z * sg)
            dz_ref[:, sl] = (dyv * xh * g_ref[...] * (sg * (1.0 + z * (1.0 - sg)))).astype(BF16)
            dxh = dn * g_ref[...]
            do_ref[:, sl] = r * (dxh - xh * jnp.mean(dxh * xh, axis=-1, keepdims=True))
            dg = dg + jnp.sum(dn * xh, axis=0, keepdims=True)
        dg_ref[...] += dg

    row = pl.BlockSpec((tm, 1024), lambda i: (i, 0))
    vec = pl.BlockSpec((1, LANES), lambda i: (0, 0))
    return pl.pallas_call(
        body, name=name, grid=(t // tm,), in_specs=[row, row, pl.BlockSpec((tm, 1024), lambda i: (i, 3)), vec],
        out_specs=[row, row, vec],
        out_shape=[jax.ShapeDtypeStruct((t, 1024), F32), jax.ShapeDtypeStruct((t, 1024), BF16),
                   jax.ShapeDtypeStruct((1, LANES), F32)],
        compiler_params=_params(("arbitrary",)),
    )(dy, o, proj_c, o_norm)


PAIR = 2 * CHUNK
GDN_HP = 8


def _bdot(a, b, dims=NN):
    return _dot(a.astype(BF16), b.astype(BF16), dims)


def _each(f, *lists):
    return [f(*args) for args in zip(*lists)]


def _pair_common(q, k, v, gci, gcj, beta):
    ri = lax.broadcasted_iota(jnp.int32, (PAIR, PAIR), 0)
    ci = lax.broadcasted_iota(jnp.int32, (PAIR, PAIR), 1)
    same = (ri // CHUNK) == (ci // CHUNK)
    incl = same & (ri >= ci)
    strict = same & (ri > ci)
    eye = (ri == ci).astype(F32)
    first = lax.broadcasted_iota(jnp.int32, (PAIR, LANES), 0) < CHUNK
    gamma = _each(lambda gi, gj: jnp.where(incl, jnp.exp(jnp.minimum(gi - gj, 0.0)), 0.0), gci, gcj)
    kb = _each(jnp.multiply, k, beta)
    kk = _each(lambda a, b: _bdot(a, b, NT), kb, k)
    qk = _each(lambda a, b: _bdot(a, b, NT), q, k)
    m = _each(lambda x, g: jnp.where(strict, x * g, 0.0), kk, gamma)
    tm_ = _each(lambda x: eye - x, m)
    pw = _each(lambda x: _bdot(x, x), m)
    for it in range(5):
        tm_ = _each(lambda x, p: x + _bdot(x, p), tm_, pw)
        if it < 4:
            pw = _each(lambda p: _bdot(p, p), pw)
    eg = _each(jnp.exp, gci)
    vb = _each(jnp.multiply, v, beta)
    kbe = _each(jnp.multiply, kb, eg)
    uw = _each(lambda x, a, b: _bdot(x, jnp.concatenate([a, b], axis=1)), tm_, vb, kbe)
    attn = _each(lambda x, g: jnp.where(incl, x * g, 0.0), qk, gamma)
    gl_a = _each(lambda g: g[CHUNK - 1:CHUNK, :], gci)
    gl_b = _each(lambda g: g[PAIR - 1:PAIR, :], gci)
    ek = _each(lambda a, b, g: jnp.exp(jnp.where(first, a, b) - g), gl_a, gl_b, gci)
    return dict(incl=incl, strict=strict, gamma=gamma, kb=kb, m=m, tm=tm_, eg=eg, vb=vb, kbe=kbe,
                u=_each(lambda x: x[:, :LANES], uw), w=_each(lambda x: x[:, LANES:], uw), attn=attn,
                qd=_each(jnp.multiply, q, eg), ek=ek, kd=_each(jnp.multiply, k, ek),
                glast_a=_each(jnp.exp, gl_a), glast_b=_each(jnp.exp, gl_b))


def _gdn_specs(t, ts, order):
    nc = ts // CHUNK
    blk = pl.BlockSpec((ts, GDN_HP * LANES), lambda h, s: (order(s), h))
    row = pl.BlockSpec((GDN_HP, 1, ts), lambda h, s: (h, 0, order(s)))
    st = pl.BlockSpec((GDN_HP, nc, LANES, LANES), lambda h, s: (h, order(s), 0, 0))
    return blk, row, st


def _gdn_fwd(q, k, v, gcb, gct, bb, *, name):
    t = q.shape[0]
    ts = _tile(t, GDN_TILE)
    npair = ts // PAIR

    def body(q_ref, k_ref, v_ref, g_ref, gt_ref, b_ref, o_ref, st_ref, s_sc):
        @pl.when(pl.program_id(1) == 0)
        def _():
            s_sc[...] = jnp.zeros_like(s_sc)

        def pair(pi, _):
            rows = pl.ds(pl.multiple_of(pi * PAIR, PAIR), PAIR)
            heads = [slice(hh * LANES, (hh + 1) * LANES) for hh in range(GDN_HP)]
            c = CHUNK
            cat0 = lambda *xs: jnp.concatenate(xs, axis=0)
            s0 = [s_sc[hh] for hh in range(GDN_HP)]
            cm = _pair_common([q_ref[rows, sl] for sl in heads], [k_ref[rows, sl] for sl in heads],
                              [v_ref[rows, sl] for sl in heads], [g_ref[rows, sl] for sl in heads],
                              [gt_ref[hh, :, rows] for hh in range(GDN_HP)], [b_ref[rows, sl] for sl in heads])
            u, w, qd, kd = cm["u"], cm["w"], cm["qd"], cm["kd"]
            r0 = _each(lambda w_, q_, s: _bdot(cat0(w_[:c], q_[:c]), s), w, qd, s0)
            vn_a = _each(lambda u_, r: u_[:c] - r[:c], u, r0)
            s1 = _each(lambda s, gl, k_, vn: s * gl + _bdot(k_[:c], vn, TN), s0, cm["glast_a"], kd, vn_a)
            r1 = _each(lambda w_, q_, s: _bdot(cat0(w_[c:], q_[c:]), s), w, qd, s1)
            vn_b = _each(lambda u_, r: u_[c:] - r[:c], u, r1)
            s2 = _each(lambda s, gl, k_, vn: s * gl + _bdot(k_[c:], vn, TN), s1, cm["glast_b"], kd, vn_b)
            o = _each(lambda ra, rb, at, va, vb_: cat0(ra[c:], rb[c:]) + _bdot(at, cat0(va, vb_)),
                      r0, r1, cm["attn"], vn_a, vn_b)
            for hh, sl in enumerate(heads):
                st_ref[hh, 2 * pi] = s0[hh]
                st_ref[hh, 2 * pi + 1] = s1[hh]
                s_sc[hh] = s2[hh]
                o_ref[rows, sl] = o[hh]
            return 0

        lax.fori_loop(0, npair, pair, 0)

    blk, row, st = _gdn_specs(t, ts, lambda s: s)
    return pl.pallas_call(
        body, name=name, grid=(GDN_HEADS // GDN_HP, t // ts), in_specs=[blk, blk, blk, blk, row, blk],
        out_specs=[blk, st],
        out_shape=[jax.ShapeDtypeStruct((t, 1024), F32), jax.ShapeDtypeStruct((GDN_HEADS, t // CHUNK, LANES, LANES), F32)],
        scratch_shapes=[pltpu.VMEM((GDN_HP, LANES, LANES), F32)],
        compiler_params=_params(("parallel", "arbitrary")),
    )(q, k, v, gcb, gct, bb)


def _gdn_bwd(q, k, v, gcb, gct, bb, do, states, *, name):
    t = q.shape[0]
    ts = _tile(t, GDN_TILE)
    npair = ts // PAIR
    ns = t // ts
    c = CHUNK

    def body(q_ref, k_ref, v_ref, g_ref, gt_ref, b_ref, do_ref, st_ref, dq_ref, dk_ref, dv_ref, dg_ref, db_ref, ds_sc):
        @pl.when(pl.program_id(1) == 0)
        def _():
            ds_sc[...] = jnp.zeros_like(ds_sc)

        rowsum = lambda x: jnp.sum(x, axis=-1, keepdims=True)
        total = lambda x: jnp.sum(rowsum(x), axis=0, keepdims=True)
        cat0 = lambda *xs: jnp.concatenate(xs, axis=0)
        cat1 = lambda *xs: jnp.concatenate(xs, axis=1)

        def pair(step, _):
            pi = npair - 1 - step
            rows = pl.ds(pl.multiple_of(pi * PAIR, PAIR), PAIR)
            heads = [slice(hh * LANES, (hh + 1) * LANES) for hh in range(GDN_HP)]
            hs = range(GDN_HP)
            qv, kv, vv = ([r[rows, sl] for sl in heads] for r in (q_ref, k_ref, v_ref))
            beta = [b_ref[rows, sl] for sl in heads]
            dov = [do_ref[rows, sl] for sl in heads]
            s0 = [st_ref[hh, 2 * pi] for hh in hs]
            s1 = [st_ref[hh, 2 * pi + 1] for hh in hs]
            ds2 = [ds_sc[hh] for hh in hs]
            cm = _pair_common(qv, kv, vv, [g_ref[rows, sl] for sl in heads], [gt_ref[hh, :, rows] for hh in hs], beta)
            u, w, qd, kd, attn = cm["u"], cm["w"], cm["qd"], cm["kd"], cm["attn"]
            tmat, gamma, eg = cm["tm"], cm["gamma"], cm["eg"]
            incl, strict = cm["incl"], cm["strict"]
            vn_a = _each(lambda u_, w_, s: u_[:c] - _bdot(w_[:c], s), u, w, s0)
            vn_b = _each(lambda u_, w_, s: u_[c:] - _bdot(w_[c:], s), u, w, s1)
            vn = _each(cat0, vn_a, vn_b)
            dvn_att = _each(lambda a, d: _bdot(a, d, TN), attn, dov)
            dattn = _each(lambda d, v_: jnp.where(incl, _bdot(d, v_, NT), 0.0), dov, vn)
            dvn_b = _each(lambda x, k_, d: x[c:] + _bdot(k_[c:], d), dvn_att, kd, ds2)
            rb = _each(lambda d, x, s: _bdot(cat0(d[c:], x), s, NT), dov, dvn_b, s1)
            dkd_b = _each(lambda v_, d: _bdot(v_, d, NT), vn_b, ds2)
            dgl_b = _each(lambda d, s: total(d * s), ds2, s1)
            ds1 = _each(lambda d, gl, q_, w_, o_, x: d * gl + _bdot(cat0(q_[c:], w_[c:]), cat0(o_[c:], -x), TN),
                        ds2, cm["glast_b"], qd, w, dov, dvn_b)
            dvn_a = _each(lambda x, k_, d: x[:c] + _bdot(k_[:c], d), dvn_att, kd, ds1)
            ra = _each(lambda d, x, s: _bdot(cat0(d[:c], x), s, NT), dov, dvn_a, s0)
            dkd_a = _each(lambda v_, d: _bdot(v_, d, NT), vn_a, ds1)
            dgl_a = _each(lambda d, s: total(d * s), ds1, s0)
            ds0 = _each(lambda d, gl, q_, w_, o_, x: d * gl + _bdot(cat0(q_[:c], w_[:c]), cat0(o_[:c], -x), TN),
                        ds1, cm["glast_a"], qd, w, dov, dvn_a)
            dvn = _each(cat0, dvn_a, dvn_b)
            dqd = _each(lambda a, b: cat0(a[:c], b[:c]), ra, rb)
            dw = _each(lambda a, b: -cat0(a[c:], b[c:]), ra, rb)
            dkd = _each(cat0, dkd_a, dkd_b)
            dvw = _each(cat1, dvn, dw)
            dvbk = _each(lambda t_, x: _bdot(t_, x, TN), tmat, dvw)
            dvb = _each(lambda x: x[:, :LANES], dvbk)
            dkbe = _each(lambda x: x[:, LANES:], dvbk)
            dt_ = _each(lambda x, a, b: _bdot(x, cat1(a, b), NT), dvw, cm["vb"], cm["kbe"])
            da1 = _each(lambda t_, x: _bdot(t_, x, TN), tmat, dt_)
            dm = _each(lambda x, t_: jnp.where(strict, -_bdot(x, t_, NT), 0.0), da1, tmat)
            dkk = _each(jnp.multiply, dm, gamma)
            dqk = _each(jnp.multiply, dattn, gamma)
            z = _each(lambda a, b, c_, d: a * b + c_ * d, dm, cm["m"], dattn, attn)
            dkb = _each(lambda x, k_, y, e: _bdot(x, k_) + y * e, dkk, kv, dkbe, eg)
            dk = _each(lambda a, b, kb_, q_, x, e, y, be: _bdot(cat0(a, b), cat0(kb_, q_), TN) + x * e + y * be,
                       dkk, dqk, cm["kb"], qv, dkd, cm["ek"], dkb, beta)
            dq = _each(lambda x, k_, y, e: _bdot(x, k_) + y * e, dqk, kv, dqd, eg)

            def colsum_of(z_):
                zh = z_.astype(BF16)
                zl = (z_ - zh.astype(F32)).astype(BF16)
                return _dot(cat0(zh, zl), jnp.ones((2 * PAIR, LANES), BF16), TN)

            colsum = _each(colsum_of, z)
            ri = lax.broadcasted_iota(jnp.int32, (PAIR, LANES), 0)
            for hh, sl in enumerate(heads):
                dkd_kd = dkd[hh] * kd[hh]
                dgc = (rowsum(z[hh]) - colsum[hh] + rowsum(dqd[hh] * qd[hh]) - rowsum(dkd_kd)
                       + rowsum(dkbe[hh] * cm["kbe"][hh]))
                last_a = total(dkd_kd[:c]) + dgl_a[hh] * cm["glast_a"][hh]
                last_b = total(dkd_kd[c:]) + dgl_b[hh] * cm["glast_b"][hh]
                dgc = dgc + jnp.where(ri == c - 1, last_a, 0.0) + jnp.where(ri == PAIR - 1, last_b, 0.0)
                ds_sc[hh] = ds0[hh]
                dq_ref[rows, sl] = dq[hh]
                dk_ref[rows, sl] = dk[hh]
                dv_ref[rows, sl] = dvb[hh] * beta[hh]
                db_ref[rows, sl] = jnp.broadcast_to(rowsum(dkb[hh] * kv[hh]) + rowsum(dvb[hh] * vv[hh]), (PAIR, LANES))
                dg_ref[rows, sl] = dgc
            return 0

        lax.fori_loop(0, npair, pair, 0)

    blk, row, st = _gdn_specs(t, ts, lambda s: ns - 1 - s)
    out = jax.ShapeDtypeStruct((t, 1024), F32)
    return pl.pallas_call(
        body, name=name, grid=(GDN_HEADS // GDN_HP, ns), in_specs=[blk, blk, blk, blk, row, blk, blk, st],
        out_specs=[blk] * 5, out_shape=[out] * 5, scratch_shapes=[pltpu.VMEM((GDN_HP, LANES, LANES), F32)],
        compiler_params=_params(("parallel", "arbitrary")),
    )(q, k, v, gcb, gct, bb, do, states)


def _loss_head(h, g, target, *, name):
    t, d = h.shape
    tm = _tile(t, 2 * ROW_TILE)

    def body(h_ref, g_ref, t_ref, dh_ref, dhb_ref, dg_ref, loss_ref):
        i = pl.program_id(0)
        x = h_ref[...]
        r = lax.rsqrt(jnp.mean(x * x, axis=-1, keepdims=True) + RMS_EPS)
        xh = x * r
        err = xh * g_ref[...] - t_ref[...]
        dy = err * (1.0 / d)
        dxh = dy * g_ref[...]
        dh = r * (dxh - xh * jnp.mean(dxh * xh, axis=-1, keepdims=True))
        dh_ref[...] = dh
        dhb_ref[...] = dh.astype(BF16)

        @pl.when(i == 0)
        def _():
            dg_ref[...] = jnp.zeros_like(dg_ref)
            loss_ref[...] = jnp.zeros_like(loss_ref)

        dg_ref[...] += jnp.sum(dy * xh, axis=0, keepdims=True)
        part = 0.5 * jnp.sum(jnp.mean(err * err, axis=-1, keepdims=True), axis=0, keepdims=True)
        loss_ref[...] += jnp.broadcast_to(part, loss_ref.shape)

    row = pl.BlockSpec((tm, d), lambda i: (i, 0))
    vec = pl.BlockSpec((1, d), lambda i: (0, 0))
    return pl.pallas_call(
        body, name=name, grid=(t // tm,), in_specs=[row, vec, row],
        out_specs=[row, row, vec, pl.BlockSpec((8, LANES), lambda i: (0, 0))],
        out_shape=[jax.ShapeDtypeStruct((t, d), F32), jax.ShapeDtypeStruct((t, d), BF16),
                   jax.ShapeDtypeStruct((1, d), F32), jax.ShapeDtypeStruct((8, LANES), F32)],
        compiler_params=_params(("arbitrary",)),
    )(h, g, target)


def _pad_cols(w, n):
    return jnp.pad(w, ((0, 0), (0, n - w.shape[1])))


def _layout_weights(w):
    z = lambda r, c: jnp.zeros((r, c), F32)
    wi = w["w_in_ab"]
    win = jnp.concatenate([wi[:, :384], z(1024, 64), wi[:, 384:416], z(1024, 32), wi[:, 416:]], axis=1)
    wq = jnp.pad(w["w_q_b"].reshape(MLA_Q_RANK, MLA_HEADS, 96), ((0, 0), (0, 0), (0, 32))).reshape(MLA_Q_RANK, 1024)
    kv3 = w["w_kv_b"].reshape(MLA_KV_RANK, MLA_HEADS, 128)
    wk = jnp.pad(kv3[..., :MLA_NOPE], ((0, 0), (0, 0), (0, 64))).reshape(MLA_KV_RANK, 1024)
    wv = kv3[..., MLA_NOPE:].reshape(MLA_KV_RANK, 512)
    pw = w["pool_w"]
    rows = []
    for g in range(4):
        rows.append(jnp.concatenate([pw[g] if j == g else z(128, 128) for j in range(4)], axis=1))
    wpool = jnp.concatenate(rows, axis=0)
    half = MLA_ROPE // 2
    inv = 1.0 / (ROPE_THETA ** (jnp.arange(half, dtype=F32) / half))
    inv_lane = jnp.concatenate([jnp.zeros((MLA_NOPE,), F32), inv, inv, jnp.zeros((32,), F32)]).reshape(1, LANES)
    return dict(
        win=win.astype(BF16), wq=wq.astype(BF16), wk=wk.astype(BF16), wv=wv.astype(BF16), wpool=wpool.astype(BF16),
        wout_ab=w["w_out_ab"].astype(BF16), winc=_pad_cols(w["w_in_c"], IN_C_PAD).astype(BF16),
        wout_c=w["w_out_c"].astype(BF16), conv_w=w["conv_w"], a_log=_pad_cols(w["a_log"], LANES),
        dt_bias=_pad_cols(w["dt_bias"], LANES), inv_lane=inv_lane,
        norm_ab=w["norm_ab"], q_a_norm=w["q_a_norm"], kv_a_norm=w["kv_a_norm"], pool_scale=w["pool_scale"],
        norm_c=w["norm_c"], o_norm=w["o_norm"], final_norm=w["final_norm"],
    )


def _unlayout_grads(g):
    dwin = g["win"]
    dkv = jnp.concatenate([g["wk"].reshape(MLA_KV_RANK, MLA_HEADS, 128)[..., :MLA_NOPE],
                           g["wv"].reshape(MLA_KV_RANK, MLA_HEADS, MLA_V)], axis=-1).reshape(MLA_KV_RANK, 1024)
    return dict(
        norm_ab=g["norm_ab"],
        w_in_ab=jnp.concatenate([dwin[:, :384], dwin[:, 448:480], dwin[:, 512:]], axis=1),
        q_a_norm=g["q_a_norm"],
        w_q_b=g["wq"].reshape(MLA_Q_RANK, MLA_HEADS, 128)[..., :96].reshape(MLA_Q_RANK, 768),
        kv_a_norm=g["kv_a_norm"],
        w_kv_b=dkv,
        pool_w=jnp.stack([g["wpool"][i * 128:(i + 1) * 128, i * 128:(i + 1) * 128] for i in range(4)]),
        pool_scale=g["pool_scale"],
        w_out_ab=g["wout_ab"],
        norm_c=g["norm_c"],
        w_in_c=g["winc"][:, :4112],
        conv_w=g["conv_w"],
        a_log=g["a_log"][:, :GDN_HEADS],
        dt_bias=g["dt_bias"][:, :GDN_HEADS],
        o_norm=g["o_norm"],
        w_out_c=g["wout_c"],
        final_norm=g["final_norm"],
    )


def _local_step(x, pos, target, lw):
    mm = _matmul
    hn = _rms_fwd(x, lw["norm_ab"], name="rms_ab")
    proj = mm(hn, lw["win"], "nn", name="in_ab")
    qn, kvn, kr, d, cos_t, sin_t = _ab_prep(proj, pos, lw["inv_lane"], lw["q_a_norm"], lw["kv_a_norm"], name="ab_prep")
    qraw = mm(qn, lw["wq"], "nn", name="q_up")
    kvk = mm(kvn, lw["wk"], "nn", name="k_up")
    v = mm(kvn, lw["wv"], "nn", name="v_up", out_dtype=BF16)
    ybraw = mm(d, lw["wpool"], "nn", name="pool_mix")
    q, k = _qk_rope(qraw, kvk, kr, cos_t, sin_t, name="qk_rope")
    o, lse = _attn_fwd(q, k, v, name="attn_fwd")
    y = _gate_fwd(o, ybraw, proj, lw["pool_scale"], name="gate_ab")
    h1 = mm(y, lw["wout_ab"], "nn", name="out_ab", add=x)
    hn1 = _rms_fwd(h1, lw["norm_c"], name="rms_c")
    proj_c = mm(hn1, lw["winc"], "nn", name="in_c")
    q2, k2, v2, gb, bb, gt = _c_prep(proj_c, lw["conv_w"], lw["a_log"], lw["dt_bias"], name="c_prep")
    gt = gt.reshape(GDN_HEADS, 1, gt.shape[1])
    o2, states = _gdn_fwd(q2, k2, v2, gb, gt, bb, name="gdn_fwd")
    y2 = _o_gate_fwd(o2, proj_c, lw["o_norm"], name="gate_c")
    h2 = mm(y2, lw["wout_c"], "nn", name="out_c", add=h1)
    dh2, dh2b, d_final, loss = _loss_head(h2, lw["final_norm"], target, name="loss_head")
    g = {"final_norm": d_final}
    dy2 = mm(dh2b, lw["wout_c"], "nt", name="out_c_dx")
    g["wout_c"] = mm(y2, dh2b, "tn", name="out_c_dw")
    do2, dz2, g["o_norm"] = _o_gate_bwd(dy2, o2, proj_c, lw["o_norm"], name="gate_c_bwd")
    dq2, dk2, dv2, dgb, dbb = _gdn_bwd(q2, k2, v2, gb, gt, bb, do2, states, name="gdn_bwd")
    dproj_c, g["conv_w"], g["a_log"], g["dt_bias"] = _c_prep_bwd(
        proj_c, lw["conv_w"], lw["a_log"], lw["dt_bias"], dq2, dk2, dv2, dgb, dbb, dz2, name="c_prep_bwd")
    dhn1 = mm(dproj_c, lw["winc"], "nt", name="in_c_dx")
    g["winc"] = mm(hn1, dproj_c, "tn", name="in_c_dw")
    dh1, dh1b, g["norm_c"] = _rms_bwd(h1, lw["norm_c"], dhn1, dh2, name="rms_c_bwd", with_bf16=True)
    dy = mm(dh1b, lw["wout_ab"], "nt", name="out_ab_dx")
    g["wout_ab"] = mm(y, dh1b, "tn", name="out_ab_dw")
    do, delta, dyb, dz, g["pool_scale"] = _gate_bwd(dy, o, ybraw, proj, lw["pool_scale"], name="gate_ab_bwd")
    dq, dk, dv = _attn_bwd(q, k, v, do, lse, delta, name="attn_bwd")
    dd = mm(dyb, lw["wpool"], "nt", name="pool_mix_dx")
    g["wpool"] = mm(d, dyb, "tn", name="pool_mix_dw")
    dqraw, dkb, dkr = _qk_rope_bwd(dq, dk, cos_t, sin_t, name="qk_rope_bwd")
    dqn = mm(dqraw, lw["wq"], "nt", name="q_up_dx")
    g["wq"] = mm(qn, dqraw, "tn", name="q_up_dw")
    dkvn_k = mm(dkb, lw["wk"], "nt", name="k_up_dx")
    dkvn_v = mm(dv, lw["wv"], "nt", name="v_up_dx")
    g["wk"] = mm(kvn, dkb, "tn", name="k_up_dw")
    g["wv"] = mm(kvn, dv, "tn", name="v_up_dw")
    dproj, g["q_a_norm"], g["kv_a_norm"] = _ab_prep_bwd(
        proj, lw["q_a_norm"], lw["kv_a_norm"], dqn, dkvn_k, dkvn_v, dkr, dd, dz, name="ab_prep_bwd")
    dhn = mm(dproj, lw["win"], "nt", name="in_ab_dx")
    g["win"] = mm(hn, dproj, "tn", name="in_ab_dw")
    dx, g["norm_ab"] = _rms_bwd(x, lw["norm_ab"], dhn, dh1, name="rms_ab_bwd", with_bf16=False)
    return loss, dx, g


_HBM = pl.BlockSpec(memory_space=pltpu.HBM)


def _place():
    return lax.axis_index("x"), lax.axis_index("y"), lax.axis_index("c")


def _flip(v, f):
    return 1 - v if f else v


_CHIP_FLIPS = ((1, 0), (0, 1), (1, 1))
_DEV_FLIPS = tuple((fx, fy, fc) for fx in (0, 1) for fy in (0, 1) for fc in (0, 1) if fx or fy or fc)


def _rcopy(src, dst, send_sems, recv_sems, k, to):
    return pltpu.make_async_remote_copy(src_ref=src, dst_ref=dst, send_sem=send_sems.at[k], recv_sem=recv_sems.at[k],
                                        device_id=to, device_id_type=MESH)


def _my_half(ref, c, axis):
    rh = ref.shape[axis] // 2
    idx = [slice(None)] * len(ref.shape)
    idx[axis] = pl.ds(c * rh, rh)
    return ref.at[tuple(idx)]


def _gather_weights(bigs, smalls):
    nb, ns = len(bigs), len(smalls)

    def body(*refs):
        ins, outs = refs[:nb + ns], refs[nb + ns:2 * (nb + ns)]
        send_sems, recv_sems, local_sems = refs[2 * (nb + ns):]
        x, y, c = _place()
        j0 = 2 * x + y
        sib = (x, y, 1 - c)
        chips = [(_flip(x, fx), _flip(y, fy)) for fx, fy in _CHIP_FLIPS]
        local = [pltpu.make_async_copy(i_ref, o_ref.at[j0], local_sems.at[a])
                 for a, (i_ref, o_ref) in enumerate(zip(ins, outs))]
        for cp in local:
            cp.start()
        sends = []
        for k, (px, py) in enumerate(chips):
            for a in range(nb):
                sends.append(_rcopy(_my_half(ins[a], c, 0), _my_half(outs[a].at[j0], c, 0), send_sems, recv_sems,
                                    6 * a + k, (px, py, c)))
            for s in range(ns):
                sends.append(_rcopy(ins[nb + s], outs[nb + s].at[j0], send_sems, recv_sems, 6 * nb + 3 * s + k, (px, py, c)))
        for cp in sends:
            cp.start()
        for k, (px, py) in enumerate(chips):
            jk = 2 * px + py
            for a in range(nb):
                landed = _my_half(outs[a].at[jk], c, 0)
                _rcopy(landed, landed, send_sems, recv_sems, 6 * a + k, (px, py, c)).wait_recv()
                fwd = _rcopy(landed, landed, send_sems, recv_sems, 6 * a + 3 + k, sib)
                fwd.start()
                sends.append(fwd)
        for k, (px, py) in enumerate(chips):
            jk = 2 * px + py
            for a in range(nb):
                other = _my_half(outs[a].at[jk], 1 - c, 0)
                _rcopy(other, other, send_sems, recv_sems, 6 * a + 3 + k, sib).wait_recv()
            for s in range(ns):
                _rcopy(ins[nb + s], outs[nb + s].at[jk], send_sems, recv_sems, 6 * nb + 3 * s + k, (px, py, c)).wait_recv()
        for cp in sends:
            cp.wait_send()
        for cp in local:
            cp.wait()

    arrays = list(bigs) + list(smalls)
    n_sem = 6 * nb + 3 * ns
    return pl.pallas_call(
        body, name="gather_weights", in_specs=[_HBM] * len(arrays), out_specs=[_HBM] * len(arrays),
        out_shape=[jax.ShapeDtypeStruct((4,) + a.shape, a.dtype) for a in arrays],
        scratch_shapes=[pltpu.SemaphoreType.DMA((n_sem,)), pltpu.SemaphoreType.DMA((n_sem,)),
                        pltpu.SemaphoreType.DMA((len(arrays),))],
    )(*arrays)


def _core_swap_partial(gs):
    n = len(gs)

    def body(*refs):
        ins, outs = refs[:n], refs[n:2 * n]
        send_sems, recv_sems = refs[2 * n:]
        x, y, c = _place()
        copies = [_rcopy(_my_half(i_ref, 1 - c, 1), o_ref, send_sems, recv_sems, a, (x, y, 1 - c))
                  for a, (i_ref, o_ref) in enumerate(zip(ins, outs))]
        for cp in copies:
            cp.start()
        for cp in copies:
            cp.wait()

    return pl.pallas_call(
        body, name="core_swap_partial", in_specs=[_HBM] * n, out_specs=[_HBM] * n,
        out_shape=[jax.ShapeDtypeStruct((4, g.shape[1] // 2, g.shape[2]), g.dtype) for g in gs],
        scratch_shapes=[pltpu.SemaphoreType.DMA((n,)), pltpu.SemaphoreType.DMA((n,))],
    )(*gs)


def _core_swap_sum(fs):
    n = len(fs)

    def body(*refs):
        ins, outs = refs[:n], refs[n:2 * n]
        send_sems, recv_sems = refs[2 * n:]
        x, y, c = _place()
        copies = [_rcopy(_my_half(i_ref, c, 0), _my_half(o_ref, c, 0), send_sems, recv_sems, a, (x, y, 1 - c))
                  for a, (i_ref, o_ref) in enumerate(zip(ins, outs))]
        for cp in copies:
            cp.start()
        for a, cp in enumerate(copies):
            cp.wait_send()
            theirs = _my_half(outs[a], 1 - c, 0)
            _rcopy(theirs, theirs, send_sems, recv_sems, a, (x, y, 1 - c)).wait_recv()

    return pl.pallas_call(
        body, name="core_swap_sum", in_specs=[_HBM] * n, out_specs=[_HBM] * n,
        out_shape=[jax.ShapeDtypeStruct(f.shape, f.dtype) for f in fs],
        input_output_aliases={a: a for a in range(n)},
        scratch_shapes=[pltpu.SemaphoreType.DMA((n,)), pltpu.SemaphoreType.DMA((n,))],
    )(*fs)


def _chip_exchange(ps, small):
    n = len(ps)
    rs = small.shape[0]

    def body(*refs):
        p_refs, s_ref = refs[:n], refs[n]
        l_refs, ls_ref = refs[n + 1:2 * n + 1], refs[2 * n + 1]
        send_sems, recv_sems, local_sems = refs[2 * n + 2:]
        x, y, c = _place()
        j0 = 2 * x + y
        d0 = 2 * j0 + c
        local = [pltpu.make_async_copy(p.at[j0], l.at[j0], local_sems.at[a]) for a, (p, l) in enumerate(zip(p_refs, l_refs))]
        local.append(pltpu.make_async_copy(s_ref, ls_ref.at[d0], local_sems.at[n]))
        for cp in local:
            cp.start()
        sends = []
        for k, (fx, fy) in enumerate(_CHIP_FLIPS):
            px, py = _flip(x, fx), _flip(y, fy)
            for a in range(n):
                sends.append(_rcopy(p_refs[a].at[2 * px + py], l_refs[a].at[j0], send_sems, recv_sems, 3 * a + k, (px, py, c)))
        for k, (fx, fy, fc) in enumerate(_DEV_FLIPS):
            peer = (_flip(x, fx), _flip(y, fy), _flip(c, fc))
            sends.append(_rcopy(s_ref, ls_ref.at[d0], send_sems, recv_sems, 3 * n + k, peer))
        for cp in sends:
            cp.start()
        for k, (fx, fy) in enumerate(_CHIP_FLIPS):
            px, py = _flip(x, fx), _flip(y, fy)
            for a in range(n):
                _rcopy(p_refs[a].at[j0], l_refs[a].at[2 * px + py], send_sems, recv_sems, 3 * a + k, (px, py, c)).wait_recv()
        for k, (fx, fy, fc) in enumerate(_DEV_FLIPS):
            px, py, pc = _flip(x, fx), _flip(y, fy), _flip(c, fc)
            _rcopy(s_ref, ls_ref.at[4 * px + 2 * py + pc], send_sems, recv_sems, 3 * n + k, (px, py, pc)).wait_recv()
        for cp in sends:
            cp.wait_send()
        for cp in local:
            cp.wait()

    n_sem = 3 * n + 7
    return pl.pallas_call(
        body, name="chip_exchange", in_specs=[_HBM] * (n + 1), out_specs=[_HBM] * (n + 1),
        out_shape=[jax.ShapeDtypeStruct(p.shape, F32) for p in ps] + [jax.ShapeDtypeStruct((8, rs, LANES), F32)],
        scratch_shapes=[pltpu.SemaphoreType.DMA((n_sem,)), pltpu.SemaphoreType.DMA((n_sem,)),
                        pltpu.SemaphoreType.DMA((n + 1,))],
    )(*ps, small)


def _core_sum(g, part, core, *, name):
    _, rh, cols = part.shape
    tr = _tile(rh, 256)
    nb = rh // tr

    def body(c_ref, g_ref, p_ref, o_ref):
        o_ref[...] = g_ref[...] + p_ref[...]

    grid_spec = pltpu.PrefetchScalarGridSpec(
        num_scalar_prefetch=1, grid=(4, nb),
        in_specs=[pl.BlockSpec((1, tr, cols), lambda j, i, c: (j, c[0] * nb + i, 0)),
                  pl.BlockSpec((1, tr, cols), lambda j, i, c: (j, i, 0))],
        out_specs=pl.BlockSpec((1, tr, cols), lambda j, i, c: (j, i, 0)),
    )
    return pl.pallas_call(
        body, name=name, grid_spec=grid_spec, out_shape=jax.ShapeDtypeStruct(part.shape, F32),
        compiler_params=_params(("parallel", "parallel")),
    )(core, g, part)


def _chip_sum(landed, core, *, name):
    _, rh, cols = landed.shape
    tr = _tile(rh, 256)
    nb = rh // tr

    def body(c_ref, l_ref, o_ref):
        o_ref[...] = ((l_ref[0] + l_ref[1]) + l_ref[2]) + l_ref[3]

    grid_spec = pltpu.PrefetchScalarGridSpec(
        num_scalar_prefetch=1, grid=(nb,),
        in_specs=[pl.BlockSpec((4, tr, cols), lambda i, c: (0, i, 0))],
        out_specs=pl.BlockSpec((tr, cols), lambda i, c: (c[0] * nb + i, 0)),
    )
    return pl.pallas_call(
        body, name=name, grid_spec=grid_spec, out_shape=jax.ShapeDtypeStruct((2 * rh, cols), F32),
        compiler_params=_params(("parallel",)),
    )(core, landed)


_ROW_POOL_W, _ROW_NORM_AB, _ROW_FINAL, _ROW_POOL_SCALE, _ROW_Q_NORM = 0, 512, 520, 528, 532
_ROW_KV_NORM, _ROW_O_NORM, _ROW_A_LOG, _ROW_DT_BIAS, _ROW_LOSS = 534, 535, 536, 537, 538
_ROW_CONV, _ROW_NORM_C, _SMALL_ROWS = 544, 640, 672
_CONV_ROWS = CONV_WIDTH * 6


def _put_rows(dst_ref, row0, src, width):
    for r in range(width // LANES):
        dst_ref[row0 + r:row0 + r + 1, :] = src[:, r * LANES:(r + 1) * LANES]


def _pack_small(g, loss_tile):
    names = ("wpool", "norm_ab", "final_norm", "pool_scale", "q_a_norm", "kv_a_norm", "o_norm", "a_log", "dt_bias",
             "conv_w", "norm_c")

    def body(wpool, norm_ab, final_norm, pool_scale, q_norm, kv_norm, o_norm, a_log, dt_bias, conv_w, norm_c, loss, o_ref):
        o_ref[...] = jnp.zeros_like(o_ref)
        for gi in range(4):
            o_ref[_ROW_POOL_W + gi * 128:_ROW_POOL_W + (gi + 1) * 128, :] = wpool[gi * 128:(gi + 1) * 128, gi * 128:(gi + 1) * 128]
        _put_rows(o_ref, _ROW_NORM_AB, norm_ab[...], 1024)
        _put_rows(o_ref, _ROW_FINAL, final_norm[...], 1024)
        _put_rows(o_ref, _ROW_POOL_SCALE, pool_scale[...], 512)
        _put_rows(o_ref, _ROW_Q_NORM, q_norm[...], 256)
        for row, ref in ((_ROW_KV_NORM, kv_norm), (_ROW_O_NORM, o_norm), (_ROW_A_LOG, a_log), (_ROW_DT_BIAS, dt_bias)):
            o_ref[row:row + 1, :] = ref[...]
        o_ref[_ROW_LOSS:_ROW_LOSS + 1, :] = loss[0:1, :]
        for j in range(4):
            for r in range(CONV_WIDTH):
                _put_rows(o_ref, _ROW_CONV + j * _CONV_ROWS + r * 6, conv_w[r:r + 1, j * 768:(j + 1) * 768], 768)
            _put_rows(o_ref, _ROW_NORM_C + j * 8, norm_c[:, j * 256:(j + 1) * 256], 256)

    vmem = pl.BlockSpec(memory_space=pltpu.VMEM)
    return pl.pallas_call(
        body, name="pack_small", in_specs=[vmem] * 12, out_specs=vmem,
        out_shape=jax.ShapeDtypeStruct((_SMALL_ROWS, LANES), F32),
    )(*[g[n] for n in names], loss_tile)


_SMALL_NAMES = ("pool_w", "norm_ab", "final_norm", "pool_scale", "q_a_norm", "kv_a_norm", "o_norm", "a_log", "dt_bias",
                "conv_w", "norm_c")


def _take_rows(src, row0, width):
    return jnp.concatenate([src[row0 + r:row0 + r + 1, :] for r in range(width // LANES)], axis=1)


def _small_update(small_all, ws, ms, vs):
    n = len(_SMALL_NAMES)

    def body(*refs):
        a_ref = refs[0]
        w_refs, m_refs, v_refs = refs[1:1 + n], refs[1 + n:1 + 2 * n], refs[1 + 2 * n:1 + 3 * n]
        outs = refs[1 + 3 * n:1 + 7 * n]
        loss_ref, tot = refs[1 + 7 * n], refs[2 + 7 * n]
        acc = a_ref[0]
        for d in range(1, 8):
            acc = acc + a_ref[d]
        tot[...] = acc
        x, y, _ = _place()
        j0 = 2 * x + y
        conv = tot[pl.ds(pl.multiple_of(_ROW_CONV + j0 * _CONV_ROWS, 8), _CONV_ROWS), :]
        norm_c = tot[pl.ds(pl.multiple_of(_ROW_NORM_C + j0 * 8, 8), 8), :]
        whole = tot[_ROW_NORM_AB:_ROW_CONV, :]
        at = lambda row: row - _ROW_NORM_AB
        grads = {
            "norm_ab": _take_rows(whole, at(_ROW_NORM_AB), 1024), "final_norm": _take_rows(whole, at(_ROW_FINAL), 1024),
            "pool_scale": _take_rows(whole, at(_ROW_POOL_SCALE), 512), "q_a_norm": _take_rows(whole, at(_ROW_Q_NORM), 256),
            "kv_a_norm": whole[at(_ROW_KV_NORM):at(_ROW_KV_NORM) + 1, :], "o_norm": whole[at(_ROW_O_NORM):at(_ROW_O_NORM) + 1, :],
            "a_log": tot[_ROW_A_LOG:_ROW_A_LOG + 1, 0:GDN_HEADS],
            "dt_bias": tot[_ROW_DT_BIAS:_ROW_DT_BIAS + 1, 0:GDN_HEADS],
            "norm_c": _take_rows(norm_c, 0, 256),
        }
        loss_ref[...] = whole[at(_ROW_LOSS):at(_ROW_LOSS) + 1, :]
        for i, name in enumerate(_SMALL_NAMES):
            g_out = outs[4 * i]
            if name == "pool_w":
                for gi in range(4):
                    g_out[gi] = tot[_ROW_POOL_W + gi * 128:_ROW_POOL_W + (gi + 1) * 128, :]
            elif name == "conv_w":
                for r in range(CONV_WIDTH):
                    g_out[r:r + 1, :] = _take_rows(conv, r * 6, 768)
            else:
                g_out[...] = grads[name]
            _adam_update(g_out, w_refs[i], m_refs[i], v_refs[i], *outs[4 * i + 1:4 * i + 4])

    vmem = pl.BlockSpec(memory_space=pltpu.VMEM)
    out_shape = [jax.ShapeDtypeStruct(w.shape, F32) for w in ws for _ in range(4)] + [jax.ShapeDtypeStruct((1, LANES), F32)]
    return pl.pallas_call(
        body, name="small_update", in_specs=[vmem] * (1 + 3 * n), out_specs=[vmem] * (4 * n + 1), out_shape=out_shape,
        scratch_shapes=[pltpu.VMEM((_SMALL_ROWS, LANES), F32)],
        compiler_params=pltpu.CompilerParams(vmem_limit_bytes=VMEM_LIMIT),
    )(small_all, *ws, *ms, *vs)


def _adam_update(g_ref, w_ref, m_ref, v_ref, d_ref, mo_ref, vo_ref):
    gv = g_ref[...]
    mn = ADAM_B1 * m_ref[...] + (1.0 - ADAM_B1) * gv
    vn = ADAM_B2 * v_ref[...] + (1.0 - ADAM_B2) * (gv * gv)
    mo_ref[...] = mn
    vo_ref[...] = vn
    c1 = 1.0 - ADAM_B1 ** ADAM_STEP
    c2 = 1.0 - ADAM_B2 ** ADAM_STEP
    d_ref[...] = -ADAM_LR * ((mn / c1) / (jnp.sqrt(vn / c2) + ADAM_EPS) + ADAM_WD * w_ref[...])


def _adamw_rows(g, w, m, v, *, name):
    rows, cols = g.shape
    tr = _tile(rows, 512)

    def body(*refs):
        _adam_update(*refs)

    blk = pl.BlockSpec((tr, cols), lambda i: (i, 0))
    out = jax.ShapeDtypeStruct((rows, cols), F32)
    return pl.pallas_call(
        body, name=name, grid=(rows // tr,), in_specs=[blk] * 4, out_specs=[blk] * 3, out_shape=[out] * 3,
        compiler_params=_params(("parallel",)),
    )(g, w, m, v)


_ADAM_ROWWISE = ("w_in_ab", "w_q_b", "w_kv_b", "w_out_ab", "w_in_c", "w_out_c")


_SHARD_AXIS = {"w_in_ab": 1, "w_q_b": 1, "w_kv_b": 1, "w_out_ab": 0, "w_in_c": 1, "w_out_c": 0, "conv_w": 1, "norm_c": 1}
_ALL_NAMES = ("norm_ab", "w_in_ab", "q_a_norm", "w_q_b", "kv_a_norm", "w_kv_b", "pool_w", "pool_scale", "w_out_ab",
              "norm_c", "w_in_c", "conv_w", "a_log", "dt_bias", "o_norm", "w_out_c", "final_norm")


def _join_shards(a, axis):
    _, r, c = a.shape
    return a.reshape(4 * r, c) if axis == 0 else jnp.transpose(a, (1, 0, 2)).reshape(r, 4 * c)


def _split_shards(a, axis):
    r, c = a.shape
    return a.reshape(4, r // 4, c) if axis == 0 else jnp.transpose(a.reshape(r, 4, c // 4), (1, 0, 2))


def kernel(x, positions, norm_ab, w_in_ab, q_a_norm, w_q_b, kv_a_norm, w_kv_b, pool_w, pool_scale, w_out_ab, norm_c, w_in_c, conv_w, a_log, dt_bias, o_norm, w_out_c, final_norm, loss_target, m_norm_ab, m_w_in_ab, m_q_a_norm, m_w_q_b, m_kv_a_norm, m_w_kv_b, m_pool_w, m_pool_scale, m_w_out_ab, m_norm_c, m_w_in_c, m_conv_w, m_a_log, m_dt_bias, m_o_norm, m_w_out_c, m_final_norm, v_norm_ab, v_w_in_ab, v_q_a_norm, v_w_q_b, v_kv_a_norm, v_w_kv_b, v_pool_w, v_pool_scale, v_w_out_ab, v_norm_c, v_w_in_c, v_conv_w, v_a_log, v_dt_bias, v_o_norm, v_w_out_c, v_final_norm):
    given = dict(locals())
    c = lax.axis_index("c")
    t = x.shape[1]

    def shard_of(prefix, name):
        a = given[prefix + name]
        return a.reshape(a.shape[1:]) if a.ndim > 2 else a.reshape(1, -1)

    big, small_sharded = _ADAM_ROWWISE, ("conv_w", "norm_c")
    gathered = _gather_weights([shard_of("", n).astype(BF16) for n in big], [shard_of("", n) for n in small_sharded])
    full = {n: _join_shards(a, _SHARD_AXIS[n]) for n, a in zip(big + small_sharded, gathered)}
    for name in _ALL_NAMES:
        if name not in full:
            full[name] = shard_of("", name)
    lw = _layout_weights(full)

    loss_tile, dx, g = _local_step(x[0], positions.reshape(t, 1), loss_target[0], lw)
    grads = _unlayout_grads(g)

    core = c.astype(jnp.int32).reshape(1)
    slots = [_split_shards(grads[n], _SHARD_AXIS[n]) for n in big]
    partial = _core_swap_partial(slots)
    chip_part = [_core_sum(s, p, core, name="core_sum_" + n) for n, s, p in zip(big, slots, partial)]
    exchanged = _chip_exchange(chip_part, _pack_small(g, loss_tile))
    halves = [_chip_sum(l, core, name="chip_sum_" + n) for n, l in zip(big, exchanged[:-1])]
    gbig = dict(zip(big, _core_swap_sum(halves)))

    res = {}
    for name in big:
        res["grad", name] = gbig[name]
        out = _adamw_rows(gbig[name], shard_of("", name), shard_of("m_", name), shard_of("v_", name), name="adamw_" + name)
        res["delta", name], res["m", name], res["v", name] = out
    out = _small_update(exchanged[-1], [shard_of("", n) for n in _SMALL_NAMES], [shard_of("m_", n) for n in _SMALL_NAMES],
                        [shard_of("v_", n) for n in _SMALL_NAMES])
    for i, name in enumerate(_SMALL_NAMES):
        res["grad", name], res["delta", name], res["m", name], res["v", name] = out[4 * i:4 * i + 4]
    res = {k: a.reshape(given[k[1]].shape) for k, a in res.items()}
    loss = out[-1][0, 0]
    outs = [loss, dx.reshape(x.shape)]
    for key in ("grad", "delta", "m", "v"):
        outs += [res[key, n] for n in _ALL_NAMES]
    return tuple(outs)
```

```python
import functools

import jax
import jax.numpy as jnp
from jax import lax
from jax.experimental import pallas as pl
from jax.experimental.pallas import tpu as pltpu

F32 = jnp.float32
BF16 = jnp.bfloat16
HI = lax.Precision.HIGHEST
MESH = pl.DeviceIdType.MESH

RMS_EPS = 1e-6
D_MODEL = 1024
MLA_HEADS = 8
MLA_Q_RANK = 256
MLA_KV_RANK = 128
MLA_NOPE = 64
MLA_ROPE = 32
MLA_V = 64
ROPE_THETA = 10000.0
POOL_WINDOWS = (2, 4, 8, 16)
POOL_GROUP = 128
POOL_WIDTH = 512
POOL_HALO = 16
GDN_HEADS = 8
GDN_DK = 128
CONV_WIDTH = 4
CONV_HALO = 8
CHUNK = 64
IN_AB_PAD = 2048
IN_C_PAD = 4224
ATT_SCALE = (MLA_NOPE + MLA_ROPE) ** -0.5

ADAM_LR = 0.001
ADAM_B1 = 0.9
ADAM_B2 = 0.999
ADAM_EPS = 1e-08
ADAM_WD = 0.01
ADAM_STEP = 10

LANES = 128
VMEM_LIMIT = 56 * 1024 * 1024

ROW_TILE = 256
ATT_TILE = 1024
GDN_TILE = 256
MM_TILE = (1024, 1408, 2048)

NN = (((1,), (0,)), ((), ()))
NT = (((1,), (1,)), ((), ()))
TN = (((0,), (0,)), ((), ()))


def _dot(a, b, dims=NN, prec=None):
    return lax.dot_general(a, b, dims, precision=prec, preferred_element_type=F32)


def _tile(n, pref):
    if n <= pref:
        return n
    step = LANES if pref >= LANES else 8
    for t in range(pref - pref % step, 0, -step):
        if n % t == 0:
            return t
    return n


def _params(sem):
    return pltpu.CompilerParams(dimension_semantics=sem, vmem_limit_bytes=VMEM_LIMIT)


def _sigmoid(x):
    return 1.0 / (1.0 + jnp.exp(-x))


def _softplus(x):
    return jnp.maximum(x, 0.0) + jnp.log(1.0 + jnp.exp(-jnp.abs(x)))


def _matmul(a, b, mode, *, name, out_dtype=F32, add=None):
    if mode == "nn":
        (m, k), (k2, n) = a.shape, b.shape
    elif mode == "nt":
        (m, k), (n, k2) = a.shape, b.shape
    else:
        (k, m), (k2, n) = a.shape, b.shape
    assert k == k2, (a.shape, b.shape, mode)
    tm, tn, tk = _tile(m, MM_TILE[0]), _tile(n, MM_TILE[1]), _tile(k, MM_TILE[2])
    nk = k // tk
    if mode == "tn":
        a_spec = pl.BlockSpec((tk, tm), lambda i, j, kk: (kk, i))
    else:
        a_spec = pl.BlockSpec((tm, tk), lambda i, j, kk: (i, kk))
    if mode == "nt":
        b_spec = pl.BlockSpec((tn, tk), lambda i, j, kk: (j, kk))
    else:
        b_spec = pl.BlockSpec((tk, tn), lambda i, j, kk: (kk, j))
    o_spec = pl.BlockSpec((tm, tn), lambda i, j, kk: (i, j))
    dims = {"nn": NN, "nt": NT, "tn": TN}[mode]
    has_add = add is not None

    def body(*refs):
        a_ref, b_ref = refs[0], refs[1]
        add_ref = refs[2] if has_add else None
        o_ref = refs[3] if has_add else refs[2]

        def finish(o):
            if has_add:
                o = o + add_ref[...]
            o_ref[...] = o.astype(out_dtype)

        if nk == 1:
            finish(_dot(a_ref[...], b_ref[...], dims))
            return
        acc = refs[-1]
        kk = pl.program_id(2)

        @pl.when(kk == 0)
        def _():
            acc[...] = jnp.zeros_like(acc)

        acc[...] += _dot(a_ref[...], b_ref[...], dims)

        @pl.when(kk == nk - 1)
        def _():
            finish(acc[...])

    in_specs = [a_spec, b_spec] + ([o_spec] if has_add else [])
    args = (a, b) + ((add,) if has_add else ())
    return pl.pallas_call(
        body, name=name, grid=(m // tm, n // tn, nk), in_specs=in_specs, out_specs=o_spec,
        out_shape=jax.ShapeDtypeStruct((m, n), out_dtype),
        scratch_shapes=[pltpu.VMEM((tm, tn), F32)] if nk > 1 else [],
        compiler_params=_params(("parallel", "parallel", "arbitrary")),
    )(*args)


def _rms_fwd(h, g, *, name):
    t, d = h.shape
    tm = _tile(t, 2 * ROW_TILE)

    def body(h_ref, g_ref, o_ref):
        x = h_ref[...]
        r = lax.rsqrt(jnp.mean(x * x, axis=-1, keepdims=True) + RMS_EPS)
        o_ref[...] = (x * r * g_ref[...]).astype(BF16)

    return pl.pallas_call(
        body, name=name, grid=(t // tm,),
        in_specs=[pl.BlockSpec((tm, d), lambda i: (i, 0)), pl.BlockSpec((1, d), lambda i: (0, 0))],
        out_specs=pl.BlockSpec((tm, d), lambda i: (i, 0)),
        out_shape=jax.ShapeDtypeStruct((t, d), BF16), compiler_params=_params(("parallel",)),
    )(h, g)


def _rms_bwd(h, g, dy, dres, *, name, with_bf16):
    t, d = h.shape
    tm = _tile(t, 2 * ROW_TILE)

    def body(h_ref, g_ref, dy_ref, dres_ref, *outs):
        i = pl.program_id(0)
        dh_ref, dg_ref = outs[0], outs[-1]
        x = h_ref[...]
        r = lax.rsqrt(jnp.mean(x * x, axis=-1, keepdims=True) + RMS_EPS)
        xh = x * r
        dyv = dy_ref[...].astype(F32)
        dxh = dyv * g_ref[...]
        dx = r * (dxh - xh * jnp.mean(dxh * xh, axis=-1, keepdims=True))
        dh = dres_ref[...] + dx
        dh_ref[...] = dh
        if with_bf16:
            outs[1][...] = dh.astype(BF16)

        @pl.when(i == 0)
        def _():
            dg_ref[...] = jnp.zeros_like(dg_ref)

        dg_ref[...] += jnp.sum(dyv * xh, axis=0, keepdims=True)

    row = pl.BlockSpec((tm, d), lambda i: (i, 0))
    vec = pl.BlockSpec((1, d), lambda i: (0, 0))
    out_shape = [jax.ShapeDtypeStruct((t, d), F32)]
    out_specs = [row]
    if with_bf16:
        out_shape.append(jax.ShapeDtypeStruct((t, d), BF16))
        out_specs.append(row)
    out_shape.append(jax.ShapeDtypeStruct((1, d), F32))
    out_specs.append(vec)
    return pl.pallas_call(
        body, name=name, grid=(t // tm,), in_specs=[row, vec, row, row], out_specs=out_specs,
        out_shape=out_shape, compiler_params=_params(("arbitrary",)),
    )(h, g, dy, dres)


def _rope_partner(x):
    lane = lax.broadcasted_iota(jnp.int32, x.shape, 1)
    swapped = jnp.where(lane < MLA_NOPE + MLA_ROPE // 2, pltpu.roll(x, LANES - 16, 1), pltpu.roll(x, 16, 1))
    return jnp.where((lane >= MLA_NOPE) & (lane < MLA_NOPE + MLA_ROPE), swapped, 0.0)


def _pool_counts(row0, tm, w):
    t_idx = row0 + lax.broadcasted_iota(jnp.int32, (tm, POOL_GROUP), 0)
    return jnp.minimum(t_idx + 1, w).astype(F32)


def _ab_prep(proj, pos, inv_freq, q_a_norm, kv_a_norm, *, name):
    t = proj.shape[0]
    tm = _tile(t, ROW_TILE)
    hb = tm // POOL_HALO

    def body(p_ref, halo_ref, pos_ref, inv_ref, qg_ref, kg_ref, qn_ref, kvn_ref, kr_ref, d_ref, cos_ref, sin_ref, ext):
        i = pl.program_id(0)
        ql = p_ref[:, 0:MLA_Q_RANK]
        r = lax.rsqrt(jnp.mean(ql * ql, axis=-1, keepdims=True) + RMS_EPS)
        qn_ref[...] = (ql * r * qg_ref[...]).astype(BF16)
        kl = p_ref[:, MLA_Q_RANK:MLA_Q_RANK + MLA_KV_RANK]
        r = lax.rsqrt(jnp.mean(kl * kl, axis=-1, keepdims=True) + RMS_EPS)
        kvn_ref[...] = (kl * r * kg_ref[...]).astype(BF16)
        ang = pos_ref[...].astype(F32) * inv_ref[...]
        lane = lax.broadcasted_iota(jnp.int32, (tm, LANES), 1)
        in_rope = (lane >= MLA_NOPE) & (lane < MLA_NOPE + MLA_ROPE)
        cos_t = jnp.where(in_rope, jnp.cos(ang), 1.0)
        sin_t = jnp.where(in_rope, jnp.sin(ang), 0.0)
        sin_t = jnp.where(lane < MLA_NOPE + MLA_ROPE // 2, -sin_t, sin_t)
        cos_ref[...] = cos_t
        sin_ref[...] = sin_t
        kr = p_ref[:, 384:512]
        kr_ref[...] = kr * cos_t + _rope_partner(kr) * sin_t
        xp = p_ref[:, 512:1024]
        ext[0:POOL_HALO, :] = jnp.where(i > 0, halo_ref[...], 0.0)
        ext[POOL_HALO:POOL_HALO + tm, :] = xp
        for g, w in enumerate(POOL_WINDOWS):
            lo = g * POOL_GROUP
            acc = ext[POOL_HALO:POOL_HALO + tm, lo:lo + POOL_GROUP]
            for s in range(1, w):
                acc = acc + ext[POOL_HALO - s:POOL_HALO - s + tm, lo:lo + POOL_GROUP]
            cnt = _pool_counts(i * tm, tm, w)
            d_ref[:, lo:lo + POOL_GROUP] = (acc / cnt - xp[:, lo:lo + POOL_GROUP]).astype(BF16)

    row = lambda w: pl.BlockSpec((tm, w), lambda i: (i, 0))
    vec = lambda w: pl.BlockSpec((1, w), lambda i: (0, 0))
    return pl.pallas_call(
        body, name=name, grid=(t // tm,),
        in_specs=[row(1024), pl.BlockSpec((POOL_HALO, POOL_WIDTH), lambda i: (jnp.maximum(i * hb - 1, 0), 1)),
                  pl.BlockSpec((tm, 1), lambda i: (i, 0)), vec(LANES), vec(MLA_Q_RANK), vec(MLA_KV_RANK)],
        out_specs=[row(MLA_Q_RANK), row(MLA_KV_RANK), row(LANES), row(POOL_WIDTH), row(LANES), row(LANES)],
        out_shape=[jax.ShapeDtypeStruct((t, MLA_Q_RANK), BF16), jax.ShapeDtypeStruct((t, MLA_KV_RANK), BF16),
                   jax.ShapeDtypeStruct((t, LANES), F32), jax.ShapeDtypeStruct((t, POOL_WIDTH), BF16),
                   jax.ShapeDtypeStruct((t, LANES), F32), jax.ShapeDtypeStruct((t, LANES), F32)],
        scratch_shapes=[pltpu.VMEM((tm + POOL_HALO, POOL_WIDTH), F32)],
        compiler_params=_params(("parallel",)),
    )(proj, proj, pos, inv_freq, q_a_norm, kv_a_norm)


def _qk_rope(qraw, kvk, kr, cos_t, sin_t, *, name):
    t = qraw.shape[0]
    tm = _tile(t, 2 * ROW_TILE)

    def body(q_ref, k_ref, kr_ref, c_ref, s_ref, qo_ref, ko_ref):
        c, s, krv = c_ref[...], s_ref[...], kr_ref[...]
        for h in range(MLA_HEADS):
            sl = slice(h * LANES, (h + 1) * LANES)
            q = q_ref[:, sl]
            qo_ref[:, sl] = ((q * c + _rope_partner(q) * s) * ATT_SCALE).astype(BF16)
            ko_ref[:, sl] = (k_ref[:, sl] + krv).astype(BF16)

    row = lambda w: pl.BlockSpec((tm, w), lambda i: (i, 0))
    return pl.pallas_call(
        body, name=name, grid=(t // tm,), in_specs=[row(1024), row(1024), row(LANES), row(LANES), row(LANES)],
        out_specs=[row(1024), row(1024)],
        out_shape=[jax.ShapeDtypeStruct((t, 1024), BF16), jax.ShapeDtypeStruct((t, 1024), BF16)],
        compiler_params=_params(("parallel",)),
    )(qraw, kvk, kr, cos_t, sin_t)


def _qk_rope_bwd(dq, dk, cos_t, sin_t, *, name):
    t = dq.shape[0]
    tm = _tile(t, 2 * ROW_TILE)

    def body(dq_ref, dk_ref, c_ref, s_ref, dqo_ref, dko_ref, dkr_ref):
        c, s = c_ref[...], s_ref[...]
        lane = lax.broadcasted_iota(jnp.int32, (tm, LANES), 1)
        in_rope = (lane >= MLA_NOPE) & (lane < MLA_NOPE + MLA_ROPE)
        dkr = jnp.zeros((tm, LANES), F32)
        for h in range(MLA_HEADS):
            sl = slice(h * LANES, (h + 1) * LANES)
            g = dq_ref[:, sl]
            dqo_ref[:, sl] = ((g * c + _rope_partner(g * s)) * ATT_SCALE).astype(BF16)
            gk = dk_ref[:, sl]
            dko_ref[:, sl] = gk.astype(BF16)
            dkr = dkr + jnp.where(in_rope, gk, 0.0)
        dkr_ref[...] = dkr * c + _rope_partner(dkr * s)

    row = lambda w: pl.BlockSpec((tm, w), lambda i: (i, 0))
    return pl.pallas_call(
        body, name=name, grid=(t // tm,), in_specs=[row(1024), row(1024), row(LANES), row(LANES)],
        out_specs=[row(1024), row(1024), row(LANES)],
        out_shape=[jax.ShapeDtypeStruct((t, 1024), BF16), jax.ShapeDtypeStruct((t, 1024), BF16),
                   jax.ShapeDtypeStruct((t, LANES), F32)],
        compiler_params=_params(("parallel",)),
    )(dq, dk, cos_t, sin_t)


def _ab_prep_bwd(proj, q_a_norm, kv_a_norm, dqn, dkvn_k, dkvn_v, dkr, dd, dz, *, name):
    t = proj.shape[0]
    tm = _tile(t, ROW_TILE)
    hb = tm // POOL_HALO
    last_halo = t // POOL_HALO - 1
    nt = t // tm

    def body(p_ref, qg_ref, kg_ref, dqn_ref, dk1_ref, dk2_ref, dkr_ref, dd_ref, ddn_ref, dz_ref,
             dp_ref, dqg_ref, dkg_ref, ext):
        i = pl.program_id(0)

        @pl.when(i == 0)
        def _():
            dqg_ref[...] = jnp.zeros_like(dqg_ref)
            dkg_ref[...] = jnp.zeros_like(dkg_ref)

        def norm_bwd(x, g, dy, dg_ref):
            r = lax.rsqrt(jnp.mean(x * x, axis=-1, keepdims=True) + RMS_EPS)
            xh = x * r
            dxh = dy * g
            dg_ref[...] += jnp.sum(dy * xh, axis=0, keepdims=True)
            return r * (dxh - xh * jnp.mean(dxh * xh, axis=-1, keepdims=True))

        dql = norm_bwd(p_ref[:, 0:MLA_Q_RANK], qg_ref[...], dqn_ref[...], dqg_ref)
        dp_ref[:, 0:MLA_Q_RANK] = dql.astype(BF16)
        dkl = norm_bwd(p_ref[:, MLA_Q_RANK:384], kg_ref[...], dk1_ref[...] + dk2_ref[...], dkg_ref)
        dp_ref[:, MLA_Q_RANK:384] = dkl.astype(BF16)
        dp_ref[:, 384:512] = dkr_ref[...].astype(BF16)
        ddv = dd_ref[...]
        for g, w in enumerate(POOL_WINDOWS):
            lo = g * POOL_GROUP
            ext[0:tm, lo:lo + POOL_GROUP] = ddv[:, lo:lo + POOL_GROUP] / _pool_counts(i * tm, tm, w)
            nxt = ddn_ref[:, lo:lo + POOL_GROUP] / _pool_counts((i + 1) * tm, POOL_HALO, w)
            ext[tm:tm + POOL_HALO, lo:lo + POOL_GROUP] = jnp.where(i < nt - 1, nxt, 0.0)
        for g, w in enumerate(POOL_WINDOWS):
            lo = g * POOL_GROUP
            acc = ext[0:tm, lo:lo + POOL_GROUP]
            for s in range(1, w):
                acc = acc + ext[s:s + tm, lo:lo + POOL_GROUP]
            dp_ref[:, 512 + lo:512 + lo + POOL_GROUP] = (acc - ddv[:, lo:lo + POOL_GROUP]).astype(BF16)
        dp_ref[:, 1024:2048] = dz_ref[...]

    row = lambda w: pl.BlockSpec((tm, w), lambda i: (i, 0))
    vec = lambda w: pl.BlockSpec((1, w), lambda i: (0, 0))
    return pl.pallas_call(
        body, name=name, grid=(nt,),
        in_specs=[row(1024), vec(MLA_Q_RANK), vec(MLA_KV_RANK), row(MLA_Q_RANK), row(MLA_KV_RANK), row(MLA_KV_RANK),
                  row(LANES), row(POOL_WIDTH),
                  pl.BlockSpec((POOL_HALO, POOL_WIDTH), lambda i: (jnp.minimum((i + 1) * hb, last_halo), 0)),
                  row(1024)],
        out_specs=[row(IN_AB_PAD), vec(MLA_Q_RANK), vec(MLA_KV_RANK)],
        out_shape=[jax.ShapeDtypeStruct((t, IN_AB_PAD), BF16), jax.ShapeDtypeStruct((1, MLA_Q_RANK), F32),
                   jax.ShapeDtypeStruct((1, MLA_KV_RANK), F32)],
        scratch_shapes=[pltpu.VMEM((tm + POOL_HALO, POOL_WIDTH), F32)],
        compiler_params=_params(("arbitrary",)),
    )(proj, q_a_norm, kv_a_norm, dqn, dkvn_k, dkvn_v, dkr, dd, dd, dz)


def _gate_fwd(o, ybraw, proj, pool_scale, *, name):
    t = o.shape[0]
    tm = _tile(t, 2 * ROW_TILE)

    def body(o_ref, yb_ref, z_ref, ps_ref, y_ref):
        z = z_ref[...]
        sz = z * _sigmoid(z)
        y_ref[:, 0:512] = (o_ref[...] * sz[:, 0:512]).astype(BF16)
        y_ref[:, 512:1024] = (yb_ref[...] * ps_ref[...] * sz[:, 512:1024]).astype(BF16)

    row = lambda w: pl.BlockSpec((tm, w), lambda i: (i, 0))
    return pl.pallas_call(
        body, name=name, grid=(t // tm,),
        in_specs=[row(512), row(512), pl.BlockSpec((tm, 1024), lambda i: (i, 1)), pl.BlockSpec((1, 512), lambda i: (0, 0))],
        out_specs=row(1024), out_shape=jax.ShapeDtypeStruct((t, 1024), BF16), compiler_params=_params(("parallel",)),
    )(o, ybraw, proj, pool_scale)


def _gate_bwd(dy, o, ybraw, proj, pool_scale, *, name):
    t = o.shape[0]
    tm = _tile(t, ROW_TILE)

    def body(dy_ref, o_ref, yb_ref, z_ref, ps_ref, do_ref, dl_ref, dyb_ref, dz_ref, dps_ref):
        i = pl.program_id(0)
        z = z_ref[...]
        sg = _sigmoid(z)
        sz = z * sg
        dsz = sg * (1.0 + z * (1.0 - sg))
        dyv = dy_ref[...]
        dcat = dyv * sz
        ov = o_ref[...]
        ybs = yb_ref[...] * ps_ref[...]
        dz_ref[:, 0:512] = (dyv[:, 0:512] * ov * dsz[:, 0:512]).astype(BF16)
        dz_ref[:, 512:1024] = (dyv[:, 512:1024] * ybs * dsz[:, 512:1024]).astype(BF16)
        do = dcat[:, 0:512]
        do_ref[...] = do.astype(BF16)
        r_i = lax.broadcasted_iota(jnp.int32, (512, 512), 0) // MLA_V
        c_i = lax.broadcasted_iota(jnp.int32, (512, 512), 1) // MLA_V
        dl_ref[...] = _dot(do * ov, (r_i == c_i).astype(F32), NN, HI)
        dyb_ref[...] = (dcat[:, 512:1024] * ps_ref[...]).astype(BF16)

        @pl.when(i == 0)
        def _():
            dps_ref[...] = jnp.zeros_like(dps_ref)

        dps_ref[...] += jnp.sum(dcat[:, 512:1024] * yb_ref[...], axis=0, keepdims=True)

    row = lambda w: pl.BlockSpec((tm, w), lambda i: (i, 0))
    vec = pl.BlockSpec((1, 512), lambda i: (0, 0))
    return pl.pallas_call(
        body, name=name, grid=(t // tm,),
        in_specs=[row(1024), row(512), row(512), pl.BlockSpec((tm, 1024), lambda i: (i, 1)), vec],
        out_specs=[row(512), row(512), row(512), row(1024), vec],
        out_shape=[jax.ShapeDtypeStruct((t, 512), BF16), jax.ShapeDtypeStruct((t, 512), F32),
                   jax.ShapeDtypeStruct((t, 512), BF16), jax.ShapeDtypeStruct((t, 1024), BF16),
                   jax.ShapeDtypeStruct((1, 512), F32)],
        compiler_params=_params(("arbitrary",)),
    )(dy, o, ybraw, proj, pool_scale)


ATT_HP_FWD = 4
ATT_HP_BWD = 2


def _diag_mask(tq):
    return lax.broadcasted_iota(jnp.int32, (tq, tq), 1) <= lax.broadcasted_iota(jnp.int32, (tq, tq), 0)


def _block_schedule(nq, key_major):
    if key_major:
        pairs = [(qi, ki) for ki in range(nq) for qi in range(ki, nq)]
    else:
        pairs = [(qi, ki) for qi in range(nq) for ki in range(qi + 1)]
    return jnp.asarray([p[0] for p in pairs], jnp.int32), jnp.asarray([p[1] for p in pairs], jnp.int32)


def _attn_fwd(q, k, v, *, name):
    t = q.shape[0]
    tq = _tile(t, ATT_TILE)
    nq = t // tq
    hp = ATT_HP_FWD
    qi_tab, ki_tab = _block_schedule(nq, key_major=False)

    def body(qi_ref, ki_ref, q_ref, k_ref, v_ref, o_ref, lse_ref, m_sc, l_sc, acc_sc):
        step = pl.program_id(1)
        qi, ki = qi_ref[step], ki_ref[step]

        @pl.when(ki == 0)
        def _():
            m_sc[...] = jnp.full_like(m_sc, -jnp.inf)
            l_sc[...] = jnp.zeros_like(l_sc)
            acc_sc[...] = jnp.zeros_like(acc_sc)

        def block(on_diagonal):
            scores = []
            for h in range(hp):
                sl = slice(h * LANES, (h + 1) * LANES)
                scores.append(_dot(q_ref[:, sl], k_ref[:, sl], NT))
            if on_diagonal:
                mask = _diag_mask(tq)
                scores = [jnp.where(mask, s, -jnp.inf) for s in scores]
            for h, s in enumerate(scores):
                vv = v_ref[:, (h // 2) * LANES:(h // 2 + 1) * LANES]
                m_prev = m_sc[h]
                m_new = jnp.maximum(m_prev, jnp.max(s, axis=-1, keepdims=True))
                alpha = jnp.exp(m_prev - m_new)
                p = jnp.exp(s - m_new[:, 0:1])
                l_sc[h] = alpha * l_sc[h] + jnp.sum(p, axis=-1, keepdims=True)
                acc_sc[h] = alpha * acc_sc[h] + _dot(p.astype(BF16), vv)
                m_sc[h] = m_new

        pl.when(ki < qi)(functools.partial(block, False))
        pl.when(ki == qi)(functools.partial(block, True))

        @pl.when(ki == qi)
        def _():
            first = lax.broadcasted_iota(jnp.int32, (tq, LANES), 1) < MLA_V
            for pr in range(hp // 2):
                a, b = 2 * pr, 2 * pr + 1
                sl = slice(pr * LANES, (pr + 1) * LANES)
                o_ref[:, sl] = jnp.where(first, acc_sc[a] / l_sc[a], acc_sc[b] / l_sc[b])
                lse_ref[:, sl] = jnp.where(first, m_sc[a] + jnp.log(l_sc[a]), m_sc[b] + jnp.log(l_sc[b]))

    grid_spec = pltpu.PrefetchScalarGridSpec(
        num_scalar_prefetch=2, grid=(MLA_HEADS // hp, qi_tab.shape[0]),
        in_specs=[pl.BlockSpec((tq, hp * LANES), lambda g, s, qt, kt: (qt[s], g)),
                  pl.BlockSpec((tq, hp * LANES), lambda g, s, qt, kt: (kt[s], g)),
                  pl.BlockSpec((tq, hp * MLA_V), lambda g, s, qt, kt: (kt[s], g))],
        out_specs=[pl.BlockSpec((tq, hp * MLA_V), lambda g, s, qt, kt: (qt[s], g)),
                   pl.BlockSpec((tq, hp * MLA_V), lambda g, s, qt, kt: (qt[s], g))],
        scratch_shapes=[pltpu.VMEM((hp, tq, LANES), F32)] * 3,
    )
    return pl.pallas_call(
        body, name=name, grid_spec=grid_spec,
        out_shape=[jax.ShapeDtypeStruct((t, 512), F32), jax.ShapeDtypeStruct((t, 512), F32)],
        compiler_params=_params(("parallel", "arbitrary")),
    )(qi_tab, ki_tab, q, k, v)


def _attn_bwd(q, k, v, do, lse, delta, *, name):
    t = q.shape[0]
    tq = _tile(t, ATT_TILE)
    nq = t // tq
    hp = ATT_HP_BWD
    qi_tab, ki_tab = _block_schedule(nq, key_major=True)

    def body(qi_ref, ki_ref, q_ref, k_ref, v_ref, do_ref, lse_ref, dl_ref, dq_ref, dk_ref, dv_ref, dk_sc, dv_sc):
        step = pl.program_id(1)
        qi, ki = qi_ref[step], ki_ref[step]

        @pl.when(step == 0)
        def _():
            dq_ref[...] = jnp.zeros_like(dq_ref)

        @pl.when(qi == ki)
        def _():
            dk_sc[...] = jnp.zeros_like(dk_sc)
            dv_sc[...] = jnp.zeros_like(dv_sc)

        def block(on_diagonal):
            lane = lax.broadcasted_iota(jnp.int32, (tq, LANES), 1)
            rows = pl.ds(pl.multiple_of(qi * tq, tq), tq)
            heads = [slice(h * LANES, (h + 1) * LANES) for h in range(hp)]
            scores = [_dot(q_ref[:, sl], k_ref[:, sl], NT) for sl in heads]
            dps = []
            for h in range(hp):
                dov = do_ref[:, (h // 2) * LANES:(h // 2 + 1) * LANES]
                mine = (lane < MLA_V) if h % 2 == 0 else (lane >= MLA_V)
                dps.append(_dot(jnp.where(mine, dov, jnp.zeros_like(dov)), v_ref[:, (h // 2) * LANES:(h // 2 + 1) * LANES], NT))
            mask = _diag_mask(tq) if on_diagonal else None
            for h, sl in enumerate(heads):
                col = (h // 2) * LANES + (h % 2) * MLA_V
                p = jnp.exp(scores[h] - lse_ref[:, col:col + 1])
                if on_diagonal:
                    p = jnp.where(mask, p, 0.0)
                ds = (p * (dps[h] - dl_ref[:, col:col + 1])).astype(BF16)
                dv_sc[h] += _dot(p.astype(BF16), do_ref[:, (h // 2) * LANES:(h // 2 + 1) * LANES], TN)
                dk_sc[h] += _dot(ds, q_ref[:, sl], TN)
                dq_ref[rows, sl] += _dot(ds, k_ref[:, sl], NN)

        pl.when(qi > ki)(functools.partial(block, False))
        pl.when(qi == ki)(functools.partial(block, True))

        @pl.when(qi == nq - 1)
        def _():
            first = lax.broadcasted_iota(jnp.int32, (tq, LANES), 1) < MLA_V
            for h in range(hp):
                dk_ref[:, h * LANES:(h + 1) * LANES] = dk_sc[h]
            for pr in range(hp // 2):
                dv_ref[:, pr * LANES:(pr + 1) * LANES] = jnp.where(first, dv_sc[2 * pr], dv_sc[2 * pr + 1]).astype(BF16)

    qrow = lambda w: pl.BlockSpec((tq, w), lambda g, s, qt, kt: (qt[s], g))
    krow = lambda w: pl.BlockSpec((tq, w), lambda g, s, qt, kt: (kt[s], g))
    grid_spec = pltpu.PrefetchScalarGridSpec(
        num_scalar_prefetch=2, grid=(MLA_HEADS // hp, qi_tab.shape[0]),
        in_specs=[qrow(hp * LANES), krow(hp * LANES), krow(hp * MLA_V), qrow(hp * MLA_V), qrow(hp * MLA_V), qrow(hp * MLA_V)],
        out_specs=[pl.BlockSpec((t, hp * LANES), lambda g, s, qt, kt: (0, g)), krow(hp * LANES), krow(hp * MLA_V)],
        scratch_shapes=[pltpu.VMEM((hp, tq, LANES), F32), pltpu.VMEM((hp, tq, LANES), F32)],
    )
    return pl.pallas_call(
        body, name=name, grid_spec=grid_spec,
        out_shape=[jax.ShapeDtypeStruct((t, 1024), F32), jax.ShapeDtypeStruct((t, 1024), F32),
                   jax.ShapeDtypeStruct((t, 512), BF16)],
        compiler_params=_params(("parallel", "arbitrary")),
    )(qi_tab, ki_tab, q, k, v, do, lse, delta)


def _conv_rows(ext, tm, w_ref, sec):
    c0 = sec * 1024
    y = ext[CONV_HALO - 3:CONV_HALO - 3 + tm, c0:c0 + 1024] * w_ref[0:1, c0:c0 + 1024]
    for j in range(1, CONV_WIDTH):
        y = y + ext[CONV_HALO - 3 + j:CONV_HALO - 3 + j + tm, c0:c0 + 1024] * w_ref[j:j + 1, c0:c0 + 1024]
    return y


def _c_prep(proj_c, conv_w, a_log, dt_bias, *, name):
    t = proj_c.shape[0]
    tm = _tile(t, ROW_TILE)
    hb = tm // CONV_HALO

    def body(p_ref, halo_ref, ab_ref, w_ref, al_ref, dtb_ref, q_ref, k_ref, v_ref, g_ref, b_ref, gt_ref, ext):
        i = pl.program_id(0)
        ext[0:CONV_HALO, :] = jnp.where(i > 0, halo_ref[...], 0.0)
        ext[CONV_HALO:CONV_HALO + tm, :] = p_ref[...]
        for sec, o_ref in enumerate((q_ref, k_ref, v_ref)):
            y = _conv_rows(ext, tm, w_ref, sec)
            y = y * _sigmoid(y)
            if sec == 2:
                o_ref[...] = y
                continue
            scale = GDN_DK ** -0.5 if sec == 0 else 1.0
            for h in range(GDN_HEADS):
                sl = slice(h * LANES, (h + 1) * LANES)
                blk = y[:, sl]
                r = lax.rsqrt(jnp.sum(blk * blk, axis=-1, keepdims=True) + RMS_EPS)
                o_ref[:, sl] = blk * (r * scale)
        ab = ab_ref[...]
        g = -jnp.exp(al_ref[...]) * _softplus(ab + dtb_ref[...])
        beta = _sigmoid(ab)
        ri = lax.broadcasted_iota(jnp.int32, (tm, tm), 0)
        ci = lax.broadcasted_iota(jnp.int32, (tm, tm), 1)
        lower = ((ri // CHUNK) == (ci // CHUNK)) & (ri >= ci)
        gc = _dot(lower.astype(F32), g, NN, HI)
        eye = lax.broadcasted_iota(jnp.int32, (LANES, LANES), 0) == lax.broadcasted_iota(jnp.int32, (LANES, LANES), 1)
        gt_ref[...] = _dot(eye.astype(F32), gc, NT, HI)[0:GDN_HEADS, :]
        for h in range(GDN_HEADS):
            sl = slice(h * LANES, (h + 1) * LANES)
            g_ref[:, sl] = jnp.broadcast_to(gc[:, h:h + 1], (tm, LANES))
            b_ref[:, sl] = jnp.broadcast_to(beta[:, GDN_HEADS + h:GDN_HEADS + h + 1], (tm, LANES))

    row = lambda w: pl.BlockSpec((tm, w), lambda i: (i, 0))
    vec = lambda r, w: pl.BlockSpec((r, w), lambda i: (0, 0))
    out = jax.ShapeDtypeStruct((t, 1024), F32)
    return pl.pallas_call(
        body, name=name, grid=(t // tm,),
        in_specs=[row(3072), pl.BlockSpec((CONV_HALO, 3072), lambda i: (jnp.maximum(i * hb - 1, 0), 0)),
                  pl.BlockSpec((tm, LANES), lambda i: (i, 32)), vec(CONV_WIDTH, 3072), vec(1, LANES), vec(1, LANES)],
        out_specs=[row(1024)] * 5 + [pl.BlockSpec((GDN_HEADS, tm), lambda i: (0, i))],
        out_shape=[out] * 5 + [jax.ShapeDtypeStruct((GDN_HEADS, t), F32)],
        scratch_shapes=[pltpu.VMEM((tm + CONV_HALO, 3072), F32)],
        compiler_params=_params(("parallel",)),
    )(proj_c, proj_c, proj_c, conv_w, a_log, dt_bias)


def _c_prep_bwd(proj_c, conv_w, a_log, dt_bias, dq, dk, dv, dgb, dbb, dz, *, name):
    t = proj_c.shape[0]
    tm = _tile(t, ROW_TILE // 2)
    hb = tm // CONV_HALO
    nt = t // tm
    rev = lambda i: nt - 1 - i

    def body(p_ref, halo_ref, ab_ref, w_ref, al_ref, dtb_ref, dq_ref, dk_ref, dv_ref, dg_ref, db_ref, dz_ref,
             dp_ref, dw_ref, dal_ref, ddt_ref, ext, dyext, carry, taps):
        step = pl.program_id(0)
        i = rev(step)

        @pl.when(step == 0)
        def _():
            dw_ref[...] = jnp.zeros_like(dw_ref)
            dal_ref[...] = jnp.zeros_like(dal_ref)
            ddt_ref[...] = jnp.zeros_like(ddt_ref)
            carry[...] = jnp.zeros_like(carry)

        ext[0:CONV_HALO, :] = jnp.where(i > 0, halo_ref[...], 0.0)
        ext[CONV_HALO:CONV_HALO + tm, :] = p_ref[...]
        for sec, g_ref in enumerate((dq_ref, dk_ref, dv_ref)):
            c0 = sec * 1024
            for j in range(CONV_WIDTH):
                taps[j] = ext[CONV_HALO - 3 + j:CONV_HALO - 3 + j + tm, c0:c0 + 1024]
            y = taps[0] * w_ref[0:1, c0:c0 + 1024]
            for j in range(1, CONV_WIDTH):
                y = y + taps[j] * w_ref[j:j + 1, c0:c0 + 1024]
            sg = _sigmoid(y)
            act = y * sg
            if sec == 2:
                dact = g_ref[...]
            else:
                scale = GDN_DK ** -0.5 if sec == 0 else 1.0
                parts = []
                for h in range(GDN_HEADS):
                    sl = slice(h * LANES, (h + 1) * LANES)
                    blk = act[:, sl]
                    r = lax.rsqrt(jnp.sum(blk * blk, axis=-1, keepdims=True) + RMS_EPS)
                    n = blk * r
                    dn = g_ref[:, sl] * scale
                    parts.append(r * (dn - n * jnp.sum(dn * n, axis=-1, keepdims=True)))
                dact = jnp.concatenate(parts, axis=-1)
            dy = dact * (sg * (1.0 + y * (1.0 - sg)))
            dyext[0:tm, c0:c0 + 1024] = dy
            for j in range(CONV_WIDTH):
                dw_ref[j:j + 1, c0:c0 + 1024] += jnp.sum(dy * taps[j], axis=0, keepdims=True)
        dyext[tm:tm + CONV_HALO, :] = carry[...]
        carry[...] = dyext[0:CONV_HALO, :]
        for sec in range(3):
            c0 = sec * 1024
            dx = dyext[3:3 + tm, c0:c0 + 1024] * w_ref[0:1, c0:c0 + 1024]
            for j in range(1, CONV_WIDTH):
                dx = dx + dyext[3 - j:3 - j + tm, c0:c0 + 1024] * w_ref[j:j + 1, c0:c0 + 1024]
            dp_ref[:, c0:c0 + 1024] = dx.astype(BF16)
        dp_ref[:, 3072:4096] = dz_ref[...]
        lane = lax.broadcasted_iota(jnp.int32, (tm, LANES), 1)
        dg = jnp.zeros((tm, LANES), F32)
        dbeta = jnp.zeros((tm, LANES), F32)
        for h in range(GDN_HEADS):
            sl = slice(h * LANES, (h + 1) * LANES)
            dg = dg + jnp.where(lane == h, dg_ref[:, sl], 0.0)
            dbeta = dbeta + jnp.where(lane == GDN_HEADS + h, db_ref[:, sl], 0.0)
        ri = lax.broadcasted_iota(jnp.int32, (tm, tm), 0)
        ci = lax.broadcasted_iota(jnp.int32, (tm, tm), 1)
        upper = ((ri // CHUNK) == (ci // CHUNK)) & (ri <= ci)
        dg = _dot(upper.astype(F32), dg, NN, HI)
        pre = ab_ref[...] + dtb_ref[...]
        s = _sigmoid(pre)
        a_exp = jnp.exp(al_ref[...])
        dg_da = dg * (-a_exp * s)
        dp_ref[:, 4096:IN_C_PAD] = (dg_da + dbeta * s * (1.0 - s)).astype(BF16)
        dal_ref[...] += jnp.sum(dg * (-a_exp * _softplus(pre)), axis=0, keepdims=True)
        ddt_ref[...] += jnp.sum(dg_da, axis=0, keepdims=True)

    row = lambda w: pl.BlockSpec((tm, w), lambda s: (rev(s), 0))
    vec = lambda r, w: pl.BlockSpec((r, w), lambda s: (0, 0))
    return pl.pallas_call(
        body, name=name, grid=(nt,),
        in_specs=[row(3072), pl.BlockSpec((CONV_HALO, 3072), lambda s: (jnp.maximum(rev(s) * hb - 1, 0), 0)),
                  pl.BlockSpec((tm, LANES), lambda s: (rev(s), 32)), vec(CONV_WIDTH, 3072), vec(1, LANES), vec(1, LANES),
                  row(1024), row(1024), row(1024), row(1024), row(1024), row(1024)],
        out_specs=[row(IN_C_PAD), vec(CONV_WIDTH, 3072), vec(1, LANES), vec(1, LANES)],
        out_shape=[jax.ShapeDtypeStruct((t, IN_C_PAD), BF16), jax.ShapeDtypeStruct((CONV_WIDTH, 3072), F32),
                   jax.ShapeDtypeStruct((1, LANES), F32), jax.ShapeDtypeStruct((1, LANES), F32)],
        scratch_shapes=[pltpu.VMEM((tm + CONV_HALO, 3072), F32), pltpu.VMEM((tm + CONV_HALO, 3072), F32),
                        pltpu.VMEM((CONV_HALO, 3072), F32), pltpu.VMEM((CONV_WIDTH, tm, 1024), F32)],
        compiler_params=_params(("arbitrary",)),
    )(proj_c, proj_c, proj_c, conv_w, a_log, dt_bias, dq, dk, dv, dgb, dbb, dz)


def _o_gate_fwd(o, proj_c, o_norm, *, name):
    t = o.shape[0]
    tm = _tile(t, 2 * ROW_TILE)

    def body(o_ref, z_ref, g_ref, y_ref):
        for h in range(GDN_HEADS):
            sl = slice(h * LANES, (h + 1) * LANES)
            x = o_ref[:, sl]
            r = lax.rsqrt(jnp.mean(x * x, axis=-1, keepdims=True) + RMS_EPS)
            z = z_ref[:, sl]
            y_ref[:, sl] = (x * r * g_ref[...] * (z * _sigmoid(z))).astype(BF16)

    row = pl.BlockSpec((tm, 1024), lambda i: (i, 0))
    return pl.pallas_call(
        body, name=name, grid=(t // tm,),
        in_specs=[row, pl.BlockSpec((tm, 1024), lambda i: (i, 3)), pl.BlockSpec((1, LANES), lambda i: (0, 0))],
        out_specs=row, out_shape=jax.ShapeDtypeStruct((t, 1024), BF16), compiler_params=_params(("parallel",)),
    )(o, proj_c, o_norm)


def _o_gate_bwd(dy, o, proj_c, o_norm, *, name):
    t = o.shape[0]
    tm = _tile(t, 2 * ROW_TILE)

    def body(dy_ref, o_ref, z_ref, g_ref, do_ref, dz_ref, dg_ref):
        i = pl.program_id(0)

        @pl.when(i == 0)
        def _():
            dg_ref[...] = jnp.zeros_like(dg_ref)

        dg = jnp.zeros((1, LANES), F32)
        for h in range(GDN_HEADS):
            sl = slice(h * LANES, (h + 1) * LANES)
            x = o_ref[:, sl]
            r = lax.rsqrt(jnp.mean(x * x, axis=-1, keepdims=True) + RMS_EPS)
            xh = x * r
            z = z_ref[:, sl]
            sg = _sigmoid(z)
            dyv = dy_ref[:, sl]
            dn = dyv * (z * sg)
            dz_ref[:, sl] = (dyv * xh * g_ref[...] * (sg * (1.0 + z * (1.0 - sg)))).astype(BF16)
            dxh = dn * g_ref[...]
            do_ref[:, sl] = r * (dxh - xh * jnp.mean(dxh * xh, axis=-1, keepdims=True))
            dg = dg + jnp.sum(dn * xh, axis=0, keepdims=True)
        dg_ref[...] += dg

    row = pl.BlockSpec((tm, 1024), lambda i: (i, 0))
    vec = pl.BlockSpec((1, LANES), lambda i: (0, 0))
    return pl.pallas_call(
        body, name=name, grid=(t // tm,), in_specs=[row, row, pl.BlockSpec((tm, 1024), lambda i: (i, 3)), vec],
        out_specs=[row, row, vec],
        out_shape=[jax.ShapeDtypeStruct((t, 1024), F32), jax.ShapeDtypeStruct((t, 1024), BF16),
                   jax.ShapeDtypeStruct((1, LANES), F32)],
        compiler_params=_params(("arbitrary",)),
    )(dy, o, proj_c, o_norm)


PAIR = 2 * CHUNK
GDN_HP = 8


def _bdot(a, b, dims=NN):
    return _dot(a.astype(BF16), b.astype(BF16), dims)


def _each(f, *lists):
    return [f(*args) for args in zip(*lists)]


def _pair_common(q, k, v, gci, gcj, beta):
    ri = lax.broadcasted_iota(jnp.int32, (PAIR, PAIR), 0)
    ci = lax.broadcasted_iota(jnp.int32, (PAIR, PAIR), 1)
    same = (ri // CHUNK) == (ci // CHUNK)
    incl = same & (ri >= ci)
    strict = same & (ri > ci)
    eye = (ri == ci).astype(F32)
    first = lax.broadcasted_iota(jnp.int32, (PAIR, LANES), 0) < CHUNK
    gamma = _each(lambda gi, gj: jnp.where(incl, jnp.exp(jnp.minimum(gi - gj, 0.0)), 0.0), gci, gcj)
    kb = _each(jnp.multiply, k, beta)
    kk = _each(lambda a, b: _bdot(a, b, NT), kb, k)
    qk = _each(lambda a, b: _bdot(a, b, NT), q, k)
    m = _each(lambda x, g: jnp.where(strict, x * g, 0.0), kk, gamma)
    tm_ = _each(lambda x: eye - x, m)
    pw = _each(lambda x: _bdot(x, x), m)
    for it in range(5):
        tm_ = _each(lambda x, p: x + _bdot(x, p), tm_, pw)
        if it < 4:
            pw = _each(lambda p: _bdot(p, p), pw)
    eg = _each(jnp.exp, gci)
    vb = _each(jnp.multiply, v, beta)
    kbe = _each(jnp.multiply, kb, eg)
    uw = _each(lambda x, a, b: _bdot(x, jnp.concatenate([a, b], axis=1)), tm_, vb, kbe)
    attn = _each(lambda x, g: jnp.where(incl, x * g, 0.0), qk, gamma)
    gl_a = _each(lambda g: g[CHUNK - 1:CHUNK, :], gci)
    gl_b = _each(lambda g: g[PAIR - 1:PAIR, :], gci)
    ek = _each(lambda a, b, g: jnp.exp(jnp.where(first, a, b) - g), gl_a, gl_b, gci)
    return dict(incl=incl, strict=strict, gamma=gamma, kb=kb, m=m, tm=tm_, eg=eg, vb=vb, kbe=kbe,
                u=_each(lambda x: x[:, :LANES], uw), w=_each(lambda x: x[:, LANES:], uw), attn=attn,
                qd=_each(jnp.multiply, q, eg), ek=ek, kd=_each(jnp.multiply, k, ek),
                glast_a=_each(jnp.exp, gl_a), glast_b=_each(jnp.exp, gl_b))


def _gdn_specs(t, ts, order):
    nc = ts // CHUNK
    blk = pl.BlockSpec((ts, GDN_HP * LANES), lambda h, s: (order(s), h))
    row = pl.BlockSpec((GDN_HP, 1, ts), lambda h, s: (h, 0, order(s)))
    st = pl.BlockSpec((GDN_HP, nc, LANES, LANES), lambda h, s: (h, order(s), 0, 0))
    return blk, row, st


def _gdn_fwd(q, k, v, gcb, gct, bb, *, name):
    t = q.shape[0]
    ts = _tile(t, GDN_TILE)
    npair = ts // PAIR

    def body(q_ref, k_ref, v_ref, g_ref, gt_ref, b_ref, o_ref, st_ref, s_sc):
        @pl.when(pl.program_id(1) == 0)
        def _():
            s_sc[...] = jnp.zeros_like(s_sc)

        def pair(pi, _):
            rows = pl.ds(pl.multiple_of(pi * PAIR, PAIR), PAIR)
            heads = [slice(hh * LANES, (hh + 1) * LANES) for hh in range(GDN_HP)]
            c = CHUNK
            cat0 = lambda *xs: jnp.concatenate(xs, axis=0)
            s0 = [s_sc[hh] for hh in range(GDN_HP)]
            cm = _pair_common([q_ref[rows, sl] for sl in heads], [k_ref[rows, sl] for sl in heads],
                              [v_ref[rows, sl] for sl in heads], [g_ref[rows, sl] for sl in heads],
                              [gt_ref[hh, :, rows] for hh in range(GDN_HP)], [b_ref[rows, sl] for sl in heads])
            u, w, qd, kd = cm["u"], cm["w"], cm["qd"], cm["kd"]
            r0 = _each(lambda w_, q_, s: _bdot(cat0(w_[:c], q_[:c]), s), w, qd, s0)
            vn_a = _each(lambda u_, r: u_[:c] - r[:c], u, r0)
            s1 = _each(lambda s, gl, k_, vn: s * gl + _bdot(k_[:c], vn, TN), s0, cm["glast_a"], kd, vn_a)
            r1 = _each(lambda w_, q_, s: _bdot(cat0(w_[c:], q_[c:]), s), w, qd, s1)
            vn_b = _each(lambda u_, r: u_[c:] - r[:c], u, r1)
            s2 = _each(lambda s, gl, k_, vn: s * gl + _bdot(k_[c:], vn, TN), s1, cm["glast_b"], kd, vn_b)
            o = _each(lambda ra, rb, at, va, vb_: cat0(ra[c:], rb[c:]) + _bdot(at, cat0(va, vb_)),
                      r0, r1, cm["attn"], vn_a, vn_b)
            for hh, sl in enumerate(heads):
                st_ref[hh, 2 * pi] = s0[hh]
                st_ref[hh, 2 * pi + 1] = s1[hh]
                s_sc[hh] = s2[hh]
                o_ref[rows, sl] = o[hh]
            return 0

        lax.fori_loop(0, npair, pair, 0)

    blk, row, st = _gdn_specs(t, ts, lambda s: s)
    return pl.pallas_call(
        body, name=name, grid=(GDN_HEADS // GDN_HP, t // ts), in_specs=[blk, blk, blk, blk, row, blk],
        out_specs=[blk, st],
        out_shape=[jax.ShapeDtypeStruct((t, 1024), F32), jax.ShapeDtypeStruct((GDN_HEADS, t // CHUNK, LANES, LANES), F32)],
        scratch_shapes=[pltpu.VMEM((GDN_HP, LANES, LANES), F32)],
        compiler_params=_params(("parallel", "arbitrary")),
    )(q, k, v, gcb, gct, bb)


def _gdn_bwd(q, k, v, gcb, gct, bb, do, states, *, name):
    t = q.shape[0]
    ts = _tile(t, GDN_TILE)
    npair = ts // PAIR
    ns = t // ts
    c = CHUNK

    def body(q_ref, k_ref, v_ref, g_ref, gt_ref, b_ref, do_ref, st_ref, dq_ref, dk_ref, dv_ref, dg_ref, db_ref, ds_sc):
        @pl.when(pl.program_id(1) == 0)
        def _():
            ds_sc[...] = jnp.zeros_like(ds_sc)

        rowsum = lambda x: jnp.sum(x, axis=-1, keepdims=True)
        total = lambda x: jnp.sum(rowsum(x), axis=0, keepdims=True)
        cat0 = lambda *xs: jnp.concatenate(xs, axis=0)
        cat1 = lambda *xs: jnp.concatenate(xs, axis=1)

        def pair(step, _):
            pi = npair - 1 - step
            rows = pl.ds(pl.multiple_of(pi * PAIR, PAIR), PAIR)
            heads = [slice(hh * LANES, (hh + 1) * LANES) for hh in range(GDN_HP)]
            hs = range(GDN_HP)
            qv, kv, vv = ([r[rows, sl] for sl in heads] for r in (q_ref, k_ref, v_ref))
            beta = [b_ref[rows, sl] for sl in heads]
            dov = [do_ref[rows, sl] for sl in heads]
            s0 = [st_ref[hh, 2 * pi] for hh in hs]
            s1 = [st_ref[hh, 2 * pi + 1] for hh in hs]
            ds2 = [ds_sc[hh] for hh in hs]
            cm = _pair_common(qv, kv, vv, [g_ref[rows, sl] for sl in heads], [gt_ref[hh, :, rows] for hh in hs], beta)
            u, w, qd, kd, attn = cm["u"], cm["w"], cm["qd"], cm["kd"], cm["attn"]
            tmat, gamma, eg = cm["tm"], cm["gamma"], cm["eg"]
            incl, strict = cm["incl"], cm["strict"]
            vn_a = _each(lambda u_, w_, s: u_[:c] - _bdot(w_[:c], s), u, w, s0)
            vn_b = _each(lambda u_, w_, s: u_[c:] - _bdot(w_[c:], s), u, w, s1)
            vn = _each(cat0, vn_a, vn_b)
            dvn_att = _each(lambda a, d: _bdot(a, d, TN), attn, dov)
            dattn = _each(lambda d, v_: jnp.where(incl, _bdot(d, v_, NT), 0.0), dov, vn)
            dvn_b = _each(lambda x, k_, d: x[c:] + _bdot(k_[c:], d), dvn_att, kd, ds2)
            rb = _each(lambda d, x, s: _bdot(cat0(d[c:], x), s, NT), dov, dvn_b, s1)
            dkd_b = _each(lambda v_, d: _bdot(v_, d, NT), vn_b, ds2)
            dgl_b = _each(lambda d, s: total(d * s), ds2, s1)
            ds1 = _each(lambda d, gl, q_, w_, o_, x: d * gl + _bdot(cat0(q_[c:], w_[c:]), cat0(o_[c:], -x), TN),
                        ds2, cm["glast_b"], qd, w, dov, dvn_b)
            dvn_a = _each(lambda x, k_, d: x[:c] + _bdot(k_[:c], d), dvn_att, kd, ds1)
            ra = _each(lambda d, x, s: _bdot(cat0(d[:c], x), s, NT), dov, dvn_a, s0)
            dkd_a = _each(lambda v_, d: _bdot(v_, d, NT), vn_a, ds1)
            dgl_a = _each(lambda d, s: total(d * s), ds1, s0)
            ds0 = _each(lambda d, gl, q_, w_, o_, x: d * gl + _bdot(cat0(q_[:c], w_[:c]), cat0(o_[:c], -x), TN),
                        ds1, cm["glast_a"], qd, w, dov, dvn_a)
            dvn = _each(cat0, dvn_a, dvn_b)
            dqd = _each(lambda a, b: cat0(a[:c], b[:c]), ra, rb)
            dw = _each(lambda a, b: -cat0(a[c:], b[c:]), ra, rb)
            dkd = _each(cat0, dkd_a, dkd_b)
            dvw = _each(cat1, dvn, dw)
            dvbk = _each(lambda t_, x: _bdot(t_, x, TN), tmat, dvw)
            dvb = _each(lambda x: x[:, :LANES], dvbk)
            dkbe = _each(lambda x: x[:, LANES:], dvbk)
            dt_ = _each(lambda x, a, b: _bdot(x, cat1(a, b), NT), dvw, cm["vb"], cm["kbe"])
            da1 = _each(lambda t_, x: _bdot(t_, x, TN), tmat, dt_)
            dm = _each(lambda x, t_: jnp.where(strict, -_bdot(x, t_, NT), 0.0), da1, tmat)
            dkk = _each(jnp.multiply, dm, gamma)
            dqk = _each(jnp.multiply, dattn, gamma)
            z = _each(lambda a, b, c_, d: a * b + c_ * d, dm, cm["m"], dattn, attn)
            dkb = _each(lambda x, k_, y, e: _bdot(x, k_) + y * e, dkk, kv, dkbe, eg)
            dk = _each(lambda a, b, kb_, q_, x, e, y, be: _bdot(cat0(a, b), cat0(kb_, q_), TN) + x * e + y * be,
                       dkk, dqk, cm["kb"], qv, dkd, cm["ek"], dkb, beta)
            dq = _each(lambda x, k_, y, e: _bdot(x, k_) + y * e, dqk, kv, dqd, eg)

            def colsum_of(z_):
                zh = z_.astype(BF16)
                zl = (z_ - zh.astype(F32)).astype(BF16)
                return _dot(cat0(zh, zl), jnp.ones((2 * PAIR, LANES), BF16), TN)

            colsum = _each(colsum_of, z)
            ri = lax.broadcasted_iota(jnp.int32, (PAIR, LANES), 0)
            for hh, sl in enumerate(heads):
                dkd_kd = dkd[hh] * kd[hh]
                dgc = (rowsum(z[hh]) - colsum[hh] + rowsum(dqd[hh] * qd[hh]) - rowsum(dkd_kd)
                       + rowsum(dkbe[hh] * cm["kbe"][hh]))
                last_a = total(dkd_kd[:c]) + dgl_a[hh] * cm["glast_a"][hh]
                last_b = total(dkd_kd[c:]) + dgl_b[hh] * cm["glast_b"][hh]
                dgc = dgc + jnp.where(ri == c - 1, last_a, 0.0) + jnp.where(ri == PAIR - 1, last_b, 0.0)
                ds_sc[hh] = ds0[hh]
                dq_ref[rows, sl] = dq[hh]
                dk_ref[rows, sl] = dk[hh]
                dv_ref[rows, sl] = dvb[hh] * beta[hh]
                db_ref[rows, sl] = jnp.broadcast_to(rowsum(dkb[hh] * kv[hh]) + rowsum(dvb[hh] * vv[hh]), (PAIR, LANES))
                dg_ref[rows, sl] = dgc
            return 0

        lax.fori_loop(0, npair, pair, 0)

    blk, row, st = _gdn_specs(t, ts, lambda s: ns - 1 - s)
    out = jax.ShapeDtypeStruct((t, 1024), F32)
    return pl.pallas_call(
        body, name=name, grid=(GDN_HEADS // GDN_HP, ns), in_specs=[blk, blk, blk, blk, row, blk, blk, st],
        out_specs=[blk] * 5, out_shape=[out] * 5, scratch_shapes=[pltpu.VMEM((GDN_HP, LANES, LANES), F32)],
        compiler_params=_params(("parallel", "arbitrary")),
    )(q, k, v, gcb, gct, bb, do, states)


def _loss_head(h, g, target, *, name):
    t, d = h.shape
    tm = _tile(t, 2 * ROW_TILE)

    def body(h_ref, g_ref, t_ref, dh_ref, dhb_ref, dg_ref, loss_ref):
        i = pl.program_id(0)
        x = h_ref[...]
        r = lax.rsqrt(jnp.mean(x * x, axis=-1, keepdims=True) + RMS_EPS)
        xh = x * r
        err = xh * g_ref[...] - t_ref[...]
        dy = err * (1.0 / d)
        dxh = dy * g_ref[...]
        dh = r * (dxh - xh * jnp.mean(dxh * xh, axis=-1, keepdims=True))
        dh_ref[...] = dh
        dhb_ref[...] = dh.astype(BF16)

        @pl.when(i == 0)
        def _():
            dg_ref[...] = jnp.zeros_like(dg_ref)
            loss_ref[...] = jnp.zeros_like(loss_ref)

        dg_ref[...] += jnp.sum(dy * xh, axis=0, keepdims=True)
        part = 0.5 * jnp.sum(jnp.mean(err * err, axis=-1, keepdims=True), axis=0, keepdims=True)
        loss_ref[...] += jnp.broadcast_to(part, loss_ref.shape)

    row = pl.BlockSpec((tm, d), lambda i: (i, 0))
    vec = pl.BlockSpec((1, d), lambda i: (0, 0))
    return pl.pallas_call(
        body, name=name, grid=(t // tm,), in_specs=[row, vec, row],
        out_specs=[row, row, vec, pl.BlockSpec((8, LANES), lambda i: (0, 0))],
        out_shape=[jax.ShapeDtypeStruct((t, d), F32), jax.ShapeDtypeStruct((t, d), BF16),
                   jax.ShapeDtypeStruct((1, d), F32), jax.ShapeDtypeStruct((8, LANES), F32)],
        compiler_params=_params(("arbitrary",)),
    )(h, g, target)


def _pad_cols(w, n):
    return jnp.pad(w, ((0, 0), (0, n - w.shape[1])))


def _layout_odd(w):
    return dict(
        winc=_pad_cols(w["w_in_c"], IN_C_PAD).astype(BF16), wout_c=w["w_out_c"].astype(BF16), conv_w=w["conv_w"],
        a_log=_pad_cols(w["a_log"], LANES), dt_bias=_pad_cols(w["dt_bias"], LANES),
        norm_c=w["norm_c"], o_norm=w["o_norm"], final_norm=w["final_norm"],
    )


def _layout_weights(w):
    return {**_layout_even(w), **_layout_odd(w)}


def _layout_even(w):
    z = lambda r, c: jnp.zeros((r, c), w["w_in_ab"].dtype)
    wi = w["w_in_ab"]
    win = jnp.concatenate([wi[:, :384], z(1024, 64), wi[:, 384:416], z(1024, 32), wi[:, 416:]], axis=1)
    wq = jnp.pad(w["w_q_b"].reshape(MLA_Q_RANK, MLA_HEADS, 96), ((0, 0), (0, 0), (0, 32))).reshape(MLA_Q_RANK, 1024)
    kv3 = w["w_kv_b"].reshape(MLA_KV_RANK, MLA_HEADS, 128)
    wk = jnp.pad(kv3[..., :MLA_NOPE], ((0, 0), (0, 0), (0, 64))).reshape(MLA_KV_RANK, 1024)
    wv = kv3[..., MLA_NOPE:].reshape(MLA_KV_RANK, 512)
    pw = w["pool_w"]
    rows = []
    for g in range(4):
        rows.append(jnp.concatenate([pw[g] if j == g else z(128, 128) for j in range(4)], axis=1))
    wpool = jnp.concatenate(rows, axis=0)
    half = MLA_ROPE // 2
    inv = 1.0 / (ROPE_THETA ** (jnp.arange(half, dtype=F32) / half))
    inv_lane = jnp.concatenate([jnp.zeros((MLA_NOPE,), F32), inv, inv, jnp.zeros((32,), F32)]).reshape(1, LANES)
    return dict(
        win=win.astype(BF16), wq=wq.astype(BF16), wk=wk.astype(BF16), wv=wv.astype(BF16), wpool=wpool.astype(BF16),
        wout_ab=w["w_out_ab"].astype(BF16), inv_lane=inv_lane,
        norm_ab=w["norm_ab"], q_a_norm=w["q_a_norm"], kv_a_norm=w["kv_a_norm"], pool_scale=w["pool_scale"],
    )


def _unlayout_grads(g):
    dwin = g["win"]
    dkv = jnp.concatenate([g["wk"].reshape(MLA_KV_RANK, MLA_HEADS, 128)[..., :MLA_NOPE],
                           g["wv"].reshape(MLA_KV_RANK, MLA_HEADS, MLA_V)], axis=-1).reshape(MLA_KV_RANK, 1024)
    return dict(
        norm_ab=g["norm_ab"],
        w_in_ab=jnp.concatenate([dwin[:, :384], dwin[:, 448:480], dwin[:, 512:]], axis=1),
        q_a_norm=g["q_a_norm"],
        w_q_b=g["wq"].reshape(MLA_Q_RANK, MLA_HEADS, 128)[..., :96].reshape(MLA_Q_RANK, 768),
        kv_a_norm=g["kv_a_norm"],
        w_kv_b=dkv,
        pool_w=jnp.stack([g["wpool"][i * 128:(i + 1) * 128, i * 128:(i + 1) * 128] for i in range(4)]),
        pool_scale=g["pool_scale"],
        w_out_ab=g["wout_ab"],
        norm_c=g["norm_c"],
        w_in_c=g["winc"][:, :4112],
        conv_w=g["conv_w"],
        a_log=g["a_log"][:, :GDN_HEADS],
        dt_bias=g["dt_bias"][:, :GDN_HEADS],
        o_norm=g["o_norm"],
        w_out_c=g["wout_c"],
        final_norm=g["final_norm"],
    )


def _local_step(x, pos, target, lw, odd_weights=None, on_odd_grads=None):
    mm = _matmul
    hn = _rms_fwd(x, lw["norm_ab"], name="rms_ab")
    proj = mm(hn, lw["win"], "nn", name="in_ab")
    qn, kvn, kr, d, cos_t, sin_t = _ab_prep(proj, pos, lw["inv_lane"], lw["q_a_norm"], lw["kv_a_norm"], name="ab_prep")
    qraw = mm(qn, lw["wq"], "nn", name="q_up")
    kvk = mm(kvn, lw["wk"], "nn", name="k_up")
    v = mm(kvn, lw["wv"], "nn", name="v_up", out_dtype=BF16)
    ybraw = mm(d, lw["wpool"], "nn", name="pool_mix")
    q, k = _qk_rope(qraw, kvk, kr, cos_t, sin_t, name="qk_rope")
    o, lse = _attn_fwd(q, k, v, name="attn_fwd")
    y = _gate_fwd(o, ybraw, proj, lw["pool_scale"], name="gate_ab")
    h1 = mm(y, lw["wout_ab"], "nn", name="out_ab", add=x)
    lo = lw if odd_weights is None else odd_weights(h1)
    hn1 = _rms_fwd(h1, lo["norm_c"], name="rms_c")
    proj_c = mm(hn1, lo["winc"], "nn", name="in_c")
    q2, k2, v2, gb, bb, gt = _c_prep(proj_c, lo["conv_w"], lo["a_log"], lo["dt_bias"], name="c_prep")
    gt = gt.reshape(GDN_HEADS, 1, gt.shape[1])
    o2, states = _gdn_fwd(q2, k2, v2, gb, gt, bb, name="gdn_fwd")
    y2 = _o_gate_fwd(o2, proj_c, lo["o_norm"], name="gate_c")
    h2 = mm(y2, lo["wout_c"], "nn", name="out_c", add=h1)
    dh2, dh2b, d_final, loss = _loss_head(h2, lo["final_norm"], target, name="loss_head")
    g = {"final_norm": d_final}
    dy2 = mm(dh2b, lo["wout_c"], "nt", name="out_c_dx")
    g["wout_c"] = mm(y2, dh2b, "tn", name="out_c_dw")
    do2, dz2, g["o_norm"] = _o_gate_bwd(dy2, o2, proj_c, lo["o_norm"], name="gate_c_bwd")
    dq2, dk2, dv2, dgb, dbb = _gdn_bwd(q2, k2, v2, gb, gt, bb, do2, states, name="gdn_bwd")
    dproj_c, g["conv_w"], g["a_log"], g["dt_bias"] = _c_prep_bwd(
        proj_c, lo["conv_w"], lo["a_log"], lo["dt_bias"], dq2, dk2, dv2, dgb, dbb, dz2, name="c_prep_bwd")
    dhn1 = mm(dproj_c, lo["winc"], "nt", name="in_c_dx")
    g["winc"] = mm(hn1, dproj_c, "tn", name="in_c_dw")
    dh1, dh1b, g["norm_c"] = _rms_bwd(h1, lo["norm_c"], dhn1, dh2, name="rms_c_bwd", with_bf16=True)
    pool_scale = lw["pool_scale"]
    if on_odd_grads is not None:
        pool_scale = pool_scale + on_odd_grads(g)
    dy = mm(dh1b, lw["wout_ab"], "nt", name="out_ab_dx")
    g["wout_ab"] = mm(y, dh1b, "tn", name="out_ab_dw")
    do, delta, dyb, dz, g["pool_scale"] = _gate_bwd(dy, o, ybraw, proj, pool_scale, name="gate_ab_bwd")
    dq, dk, dv = _attn_bwd(q, k, v, do, lse, delta, name="attn_bwd")
    dd = mm(dyb, lw["wpool"], "nt", name="pool_mix_dx")
    g["wpool"] = mm(d, dyb, "tn", name="pool_mix_dw")
    dqraw, dkb, dkr = _qk_rope_bwd(dq, dk, cos_t, sin_t, name="qk_rope_bwd")
    dqn = mm(dqraw, lw["wq"], "nt", name="q_up_dx")
    g["wq"] = mm(qn, dqraw, "tn", name="q_up_dw")
    dkvn_k = mm(dkb, lw["wk"], "nt", name="k_up_dx")
    dkvn_v = mm(dv, lw["wv"], "nt", name="v_up_dx")
    g["wk"] = mm(kvn, dkb, "tn", name="k_up_dw")
    g["wv"] = mm(kvn, dv, "tn", name="v_up_dw")
    dproj, g["q_a_norm"], g["kv_a_norm"] = _ab_prep_bwd(
        proj, lw["q_a_norm"], lw["kv_a_norm"], dqn, dkvn_k, dkvn_v, dkr, dd, dz, name="ab_prep_bwd")
    dhn = mm(dproj, lw["win"], "nt", name="in_ab_dx")
    g["win"] = mm(hn, dproj, "tn", name="in_ab_dw")
    dx, g["norm_ab"] = _rms_bwd(x, lw["norm_ab"], dhn, dh1, name="rms_ab_bwd", with_bf16=False)
    return loss, dx, g


_HBM = pl.BlockSpec(memory_space=pltpu.HBM)


def _place():
    return lax.axis_index("x"), lax.axis_index("y"), lax.axis_index("c")


def _flip(v, f):
    return 1 - v if f else v


_CHIP_FLIPS = ((1, 0), (0, 1), (1, 1))
_DEV_FLIPS = tuple((fx, fy, fc) for fx in (0, 1) for fy in (0, 1) for fc in (0, 1) if fx or fy or fc)


def _rcopy(src, dst, send_sems, recv_sems, k, to):
    return pltpu.make_async_remote_copy(src_ref=src, dst_ref=dst, send_sem=send_sems.at[k], recv_sem=recv_sems.at[k],
                                        device_id=to, device_id_type=MESH)


def _my_half(ref, c, axis):
    rh = ref.shape[axis] // 2
    idx = [slice(None)] * len(ref.shape)
    idx[axis] = pl.ds(c * rh, rh)
    return ref.at[tuple(idx)]


def _gather_weights(bigs, smalls):
    nb, ns = len(bigs), len(smalls)

    def body(*refs):
        ins, outs = refs[:nb + ns], refs[nb + ns:2 * (nb + ns)]
        send_sems, recv_sems, local_sems = refs[2 * (nb + ns):]
        x, y, c = _place()
        j0 = 2 * x + y
        sib = (x, y, 1 - c)
        chips = [(_flip(x, fx), _flip(y, fy)) for fx, fy in _CHIP_FLIPS]
        local = [pltpu.make_async_copy(i_ref, o_ref.at[j0], local_sems.at[a])
                 for a, (i_ref, o_ref) in enumerate(zip(ins, outs))]
        for cp in local:
            cp.start()
        sends = []
        for k, (px, py) in enumerate(chips):
            for a in range(nb):
                sends.append(_rcopy(_my_half(ins[a], c, 0), _my_half(outs[a].at[j0], c, 0), send_sems, recv_sems,
                                    6 * a + k, (px, py, c)))
            for s in range(ns):
                sends.append(_rcopy(ins[nb + s], outs[nb + s].at[j0], send_sems, recv_sems, 6 * nb + 3 * s + k, (px, py, c)))
        for cp in sends:
            cp.start()
        for k, (px, py) in enumerate(chips):
            jk = 2 * px + py
            for a in range(nb):
                landed = _my_half(outs[a].at[jk], c, 0)
                _rcopy(landed, landed, send_sems, recv_sems, 6 * a + k, (px, py, c)).wait_recv()
                fwd = _rcopy(landed, landed, send_sems, recv_sems, 6 * a + 3 + k, sib)
                fwd.start()
                sends.append(fwd)
        for k, (px, py) in enumerate(chips):
            jk = 2 * px + py
            for a in range(nb):
                other = _my_half(outs[a].at[jk], 1 - c, 0)
                _rcopy(other, other, send_sems, recv_sems, 6 * a + 3 + k, sib).wait_recv()
            for s in range(ns):
                _rcopy(ins[nb + s], outs[nb + s].at[jk], send_sems, recv_sems, 6 * nb + 3 * s + k, (px, py, c)).wait_recv()
        for cp in sends:
            cp.wait_send()
        for cp in local:
            cp.wait()

    arrays = list(bigs) + list(smalls)
    n_sem = 6 * nb + 3 * ns
    return pl.pallas_call(
        body, name="gather_weights", in_specs=[_HBM] * len(arrays), out_specs=[_HBM] * len(arrays),
        out_shape=[jax.ShapeDtypeStruct((4,) + a.shape, a.dtype) for a in arrays],
        scratch_shapes=[pltpu.SemaphoreType.DMA((n_sem,)), pltpu.SemaphoreType.DMA((n_sem,)),
                        pltpu.SemaphoreType.DMA((len(arrays),))],
    )(*arrays)


def _core_swap_partial(gs, *, name):
    n = len(gs)

    def body(*refs):
        ins, outs = refs[:n], refs[n:2 * n]
        send_sems, recv_sems = refs[2 * n:]
        x, y, c = _place()
        copies = [_rcopy(_my_half(i_ref, 1 - c, 1), o_ref, send_sems, recv_sems, a, (x, y, 1 - c))
                  for a, (i_ref, o_ref) in enumerate(zip(ins, outs))]
        for cp in copies:
            cp.start()
        for cp in copies:
            cp.wait()

    return pl.pallas_call(
        body, name=name, in_specs=[_HBM] * n, out_specs=[_HBM] * n,
        out_shape=[jax.ShapeDtypeStruct((4, g.shape[1] // 2, g.shape[2]), g.dtype) for g in gs],
        scratch_shapes=[pltpu.SemaphoreType.DMA((n,)), pltpu.SemaphoreType.DMA((n,))],
    )(*gs)


def _core_swap_sum(fs):
    n = len(fs)

    def body(*refs):
        ins, outs = refs[:n], refs[n:2 * n]
        send_sems, recv_sems = refs[2 * n:]
        x, y, c = _place()
        copies = [_rcopy(_my_half(i_ref, c, 0), _my_half(o_ref, c, 0), send_sems, recv_sems, a, (x, y, 1 - c))
                  for a, (i_ref, o_ref) in enumerate(zip(ins, outs))]
        for cp in copies:
            cp.start()
        for a, cp in enumerate(copies):
            cp.wait_send()
            theirs = _my_half(outs[a], 1 - c, 0)
            _rcopy(theirs, theirs, send_sems, recv_sems, a, (x, y, 1 - c)).wait_recv()

    return pl.pallas_call(
        body, name="core_swap_sum", in_specs=[_HBM] * n, out_specs=[_HBM] * n,
        out_shape=[jax.ShapeDtypeStruct(f.shape, f.dtype) for f in fs],
        input_output_aliases={a: a for a in range(n)},
        scratch_shapes=[pltpu.SemaphoreType.DMA((n,)), pltpu.SemaphoreType.DMA((n,))],
    )(*fs)


_SEM = pl.BlockSpec(memory_space=pltpu.SEMAPHORE)
_ANY = pl.BlockSpec(memory_space=pl.ANY)
_DATAFLOW = pltpu.SideEffectType.DATAFLOW_SIDE_EFFECTING


def _to_chips_copies(srcs, lands, send_sems, recv_sems, per_chip_slot):
    x, y, c = _place()
    j0 = 2 * x + y
    out = []
    for k, (fx, fy) in enumerate(_CHIP_FLIPS):
        px, py = _flip(x, fx), _flip(y, fy)
        jk = 2 * px + py
        for a, (src, land) in enumerate(zip(srcs, lands)):
            piece = src.at[jk] if per_chip_slot else src
            out.append((_rcopy(piece, land.at[j0], send_sems, recv_sems, 3 * a + k, (px, py, c)),
                        _rcopy(piece, land.at[jk], send_sems, recv_sems, 3 * a + k, (px, py, c))))
    return out


def _to_chips_start(arrays, *, per_chip_slot, name):
    n = len(arrays)
    lands = [lax.empty((4,) + (a.shape[1:] if per_chip_slot else a.shape), a.dtype) for a in arrays]

    def body(*refs):
        srcs, land_refs, send_sems, recv_sems, token = refs[:n], refs[n:2 * n], refs[2 * n], refs[2 * n + 1], refs[-1]
        for send, _ in _to_chips_copies(srcs, land_refs, send_sems, recv_sems, per_chip_slot):
            send.start()
        token[...] = jnp.zeros_like(token)

    held = [pltpu.with_memory_space_constraint(a, pltpu.HBM) for a in list(arrays) + lands]
    return pl.pallas_call(
        body, name=name, in_specs=[_HBM] * (2 * n),
        out_specs=(_SEM, _SEM, *[_HBM] * (2 * n), pl.BlockSpec(memory_space=pltpu.VMEM)),
        out_shape=(pltpu.SemaphoreType.DMA((3 * n,)), pltpu.SemaphoreType.DMA((3 * n,)),
                   *[pltpu.HBM(a.shape, a.dtype) for a in held], jax.ShapeDtypeStruct((8, LANES), F32)),
        input_output_aliases={i: 2 + i for i in range(2 * n)},
        compiler_params=pltpu.CompilerParams(has_side_effects=_DATAFLOW),
    )(*held)


def _to_chips_wait(started, after, *, per_chip_slot, name):
    send_sems, recv_sems, held = started[0], started[1], started[2:-1]
    n = len(held) // 2

    def body(*refs):
        srcs, land_refs, s_sems, r_sems = refs[:n], refs[n:2 * n], refs[2 * n], refs[2 * n + 1]
        for send, arrival in _to_chips_copies(srcs, land_refs, s_sems, r_sems, per_chip_slot):
            send.wait_send()
            arrival.wait_recv()

    out = pl.pallas_call(
        body, name=name, in_specs=[_HBM] * (2 * n) + [_SEM, _SEM, _ANY], out_specs=[_HBM] * (2 * n),
        out_shape=[pltpu.HBM(a.shape, a.dtype) for a in held],
        input_output_aliases={i: i for i in range(2 * n)},
        compiler_params=pltpu.CompilerParams(has_side_effects=_DATAFLOW),
    )(*held, send_sems, recv_sems, after)
    return out[n:]


def _chip_exchange(ps, small):
    n = len(ps)
    rs = small.shape[0]

    def body(*refs):
        p_refs, s_ref = refs[:n], refs[n]
        l_refs, ls_ref = refs[n + 1:2 * n + 1], refs[2 * n + 1]
        send_sems, recv_sems, local_sems = refs[2 * n + 2:]
        x, y, c = _place()
        j0 = 2 * x + y
        d0 = 2 * j0 + c
        local = [pltpu.make_async_copy(p.at[j0], l.at[j0], local_sems.at[a]) for a, (p, l) in enumerate(zip(p_refs, l_refs))]
        local.append(pltpu.make_async_copy(s_ref, ls_ref.at[d0], local_sems.at[n]))
        for cp in local:
            cp.start()
        sends = []
        for k, (fx, fy) in enumerate(_CHIP_FLIPS):
            px, py = _flip(x, fx), _flip(y, fy)
            for a in range(n):
                sends.append(_rcopy(p_refs[a].at[2 * px + py], l_refs[a].at[j0], send_sems, recv_sems, 3 * a + k, (px, py, c)))
        for k, (fx, fy, fc) in enumerate(_DEV_FLIPS):
            peer = (_flip(x, fx), _flip(y, fy), _flip(c, fc))
            sends.append(_rcopy(s_ref, ls_ref.at[d0], send_sems, recv_sems, 3 * n + k, peer))
        for cp in sends:
            cp.start()
        for k, (fx, fy) in enumerate(_CHIP_FLIPS):
            px, py = _flip(x, fx), _flip(y, fy)
            for a in range(n):
                _rcopy(p_refs[a].at[j0], l_refs[a].at[2 * px + py], send_sems, recv_sems, 3 * a + k, (px, py, c)).wait_recv()
        for k, (fx, fy, fc) in enumerate(_DEV_FLIPS):
            px, py, pc = _flip(x, fx), _flip(y, fy), _flip(c, fc)
            _rcopy(s_ref, ls_ref.at[4 * px + 2 * py + pc], send_sems, recv_sems, 3 * n + k, (px, py, pc)).wait_recv()
        for cp in sends:
            cp.wait_send()
        for cp in local:
            cp.wait()

    n_sem = 3 * n + 7
    return pl.pallas_call(
        body, name="chip_exchange", in_specs=[_HBM] * (n + 1), out_specs=[_HBM] * (n + 1),
        out_shape=[jax.ShapeDtypeStruct(p.shape, F32) for p in ps] + [jax.ShapeDtypeStruct((8, rs, LANES), F32)],
        scratch_shapes=[pltpu.SemaphoreType.DMA((n_sem,)), pltpu.SemaphoreType.DMA((n_sem,)),
                        pltpu.SemaphoreType.DMA((n + 1,))],
    )(*ps, small)


def _core_sum(g, part, core, *, name):
    _, rh, cols = part.shape
    tr = _tile(rh, 256)
    nb = rh // tr

    def body(c_ref, g_ref, p_ref, o_ref):
        o_ref[...] = g_ref[...] + p_ref[...]

    grid_spec = pltpu.PrefetchScalarGridSpec(
        num_scalar_prefetch=1, grid=(4, nb),
        in_specs=[pl.BlockSpec((1, tr, cols), lambda j, i, c: (j, c[0] * nb + i, 0)),
                  pl.BlockSpec((1, tr, cols), lambda j, i, c: (j, i, 0))],
        out_specs=pl.BlockSpec((1, tr, cols), lambda j, i, c: (j, i, 0)),
    )
    return pl.pallas_call(
        body, name=name, grid_spec=grid_spec, out_shape=jax.ShapeDtypeStruct(part.shape, F32),
        compiler_params=_params(("parallel", "parallel")),
    )(core, g, part)


def _chip_sum(landed, core, *, name):
    _, rh, cols = landed.shape
    tr = _tile(rh, 256)
    nb = rh // tr

    def body(c_ref, l_ref, o_ref):
        o_ref[...] = ((l_ref[0] + l_ref[1]) + l_ref[2]) + l_ref[3]

    grid_spec = pltpu.PrefetchScalarGridSpec(
        num_scalar_prefetch=1, grid=(nb,),
        in_specs=[pl.BlockSpec((4, tr, cols), lambda i, c: (0, i, 0))],
        out_specs=pl.BlockSpec((tr, cols), lambda i, c: (c[0] * nb + i, 0)),
    )
    return pl.pallas_call(
        body, name=name, grid_spec=grid_spec, out_shape=jax.ShapeDtypeStruct((2 * rh, cols), F32),
        compiler_params=_params(("parallel",)),
    )(core, landed)


_ROW_POOL_W, _ROW_NORM_AB, _ROW_FINAL, _ROW_POOL_SCALE, _ROW_Q_NORM = 0, 512, 520, 528, 532
_ROW_KV_NORM, _ROW_O_NORM, _ROW_A_LOG, _ROW_DT_BIAS, _ROW_LOSS = 534, 535, 536, 537, 538
_ROW_CONV, _ROW_NORM_C, _SMALL_ROWS = 544, 640, 672
_CONV_ROWS = CONV_WIDTH * 6


def _put_rows(dst_ref, row0, src, width):
    for r in range(width // LANES):
        dst_ref[row0 + r:row0 + r + 1, :] = src[:, r * LANES:(r + 1) * LANES]


def _pack_small(g, loss_tile):
    names = ("wpool", "norm_ab", "final_norm", "pool_scale", "q_a_norm", "kv_a_norm", "o_norm", "a_log", "dt_bias",
             "conv_w", "norm_c")

    def body(wpool, norm_ab, final_norm, pool_scale, q_norm, kv_norm, o_norm, a_log, dt_bias, conv_w, norm_c, loss, o_ref):
        o_ref[...] = jnp.zeros_like(o_ref)
        for gi in range(4):
            o_ref[_ROW_POOL_W + gi * 128:_ROW_POOL_W + (gi + 1) * 128, :] = wpool[gi * 128:(gi + 1) * 128, gi * 128:(gi + 1) * 128]
        _put_rows(o_ref, _ROW_NORM_AB, norm_ab[...], 1024)
        _put_rows(o_ref, _ROW_FINAL, final_norm[...], 1024)
        _put_rows(o_ref, _ROW_POOL_SCALE, pool_scale[...], 512)
        _put_rows(o_ref, _ROW_Q_NORM, q_norm[...], 256)
        for row, ref in ((_ROW_KV_NORM, kv_norm), (_ROW_O_NORM, o_norm), (_ROW_A_LOG, a_log), (_ROW_DT_BIAS, dt_bias)):
            o_ref[row:row + 1, :] = ref[...]
        o_ref[_ROW_LOSS:_ROW_LOSS + 1, :] = loss[0:1, :]
        for j in range(4):
            for r in range(CONV_WIDTH):
                _put_rows(o_ref, _ROW_CONV + j * _CONV_ROWS + r * 6, conv_w[r:r + 1, j * 768:(j + 1) * 768], 768)
            _put_rows(o_ref, _ROW_NORM_C + j * 8, norm_c[:, j * 256:(j + 1) * 256], 256)

    vmem = pl.BlockSpec(memory_space=pltpu.VMEM)
    return pl.pallas_call(
        body, name="pack_small", in_specs=[vmem] * 12, out_specs=vmem,
        out_shape=jax.ShapeDtypeStruct((_SMALL_ROWS, LANES), F32),
    )(*[g[n] for n in names], loss_tile)


_SMALL_NAMES = ("pool_w", "norm_ab", "final_norm", "pool_scale", "q_a_norm", "kv_a_norm", "o_norm", "a_log", "dt_bias",
                "conv_w", "norm_c")


def _take_rows(src, row0, width):
    return jnp.concatenate([src[row0 + r:row0 + r + 1, :] for r in range(width // LANES)], axis=1)


def _small_update(small_all, ws, ms, vs):
    n = len(_SMALL_NAMES)

    def body(*refs):
        a_ref = refs[0]
        w_refs, m_refs, v_refs = refs[1:1 + n], refs[1 + n:1 + 2 * n], refs[1 + 2 * n:1 + 3 * n]
        outs = refs[1 + 3 * n:1 + 7 * n]
        loss_ref, tot = refs[1 + 7 * n], refs[2 + 7 * n]
        acc = a_ref[0]
        for d in range(1, 8):
            acc = acc + a_ref[d]
        tot[...] = acc
        x, y, _ = _place()
        j0 = 2 * x + y
        conv = tot[pl.ds(pl.multiple_of(_ROW_CONV + j0 * _CONV_ROWS, 8), _CONV_ROWS), :]
        norm_c = tot[pl.ds(pl.multiple_of(_ROW_NORM_C + j0 * 8, 8), 8), :]
        whole = tot[_ROW_NORM_AB:_ROW_CONV, :]
        at = lambda row: row - _ROW_NORM_AB
        grads = {
            "norm_ab": _take_rows(whole, at(_ROW_NORM_AB), 1024), "final_norm": _take_rows(whole, at(_ROW_FINAL), 1024),
            "pool_scale": _take_rows(whole, at(_ROW_POOL_SCALE), 512), "q_a_norm": _take_rows(whole, at(_ROW_Q_NORM), 256),
            "kv_a_norm": whole[at(_ROW_KV_NORM):at(_ROW_KV_NORM) + 1, :], "o_norm": whole[at(_ROW_O_NORM):at(_ROW_O_NORM) + 1, :],
            "a_log": tot[_ROW_A_LOG:_ROW_A_LOG + 1, 0:GDN_HEADS],
            "dt_bias": tot[_ROW_DT_BIAS:_ROW_DT_BIAS + 1, 0:GDN_HEADS],
            "norm_c": _take_rows(norm_c, 0, 256),
        }
        loss_ref[...] = whole[at(_ROW_LOSS):at(_ROW_LOSS) + 1, :]
        for i, name in enumerate(_SMALL_NAMES):
            g_out = outs[4 * i]
            if name == "pool_w":
                for gi in range(4):
                    g_out[gi] = tot[_ROW_POOL_W + gi * 128:_ROW_POOL_W + (gi + 1) * 128, :]
            elif name == "conv_w":
                for r in range(CONV_WIDTH):
                    g_out[r:r + 1, :] = _take_rows(conv, r * 6, 768)
            else:
                g_out[...] = grads[name]
            _adam_update(g_out, w_refs[i], m_refs[i], v_refs[i], *outs[4 * i + 1:4 * i + 4])

    vmem = pl.BlockSpec(memory_space=pltpu.VMEM)
    out_shape = [jax.ShapeDtypeStruct(w.shape, F32) for w in ws for _ in range(4)] + [jax.ShapeDtypeStruct((1, LANES), F32)]
    return pl.pallas_call(
        body, name="small_update", in_specs=[vmem] * (1 + 3 * n), out_specs=[vmem] * (4 * n + 1), out_shape=out_shape,
        scratch_shapes=[pltpu.VMEM((_SMALL_ROWS, LANES), F32)],
        compiler_params=pltpu.CompilerParams(vmem_limit_bytes=VMEM_LIMIT),
    )(small_all, *ws, *ms, *vs)


def _adam_update(g_ref, w_ref, m_ref, v_ref, d_ref, mo_ref, vo_ref):
    gv = g_ref[...]
    mn = ADAM_B1 * m_ref[...] + (1.0 - ADAM_B1) * gv
    vn = ADAM_B2 * v_ref[...] + (1.0 - ADAM_B2) * (gv * gv)
    mo_ref[...] = mn
    vo_ref[...] = vn
    c1 = 1.0 - ADAM_B1 ** ADAM_STEP
    c2 = 1.0 - ADAM_B2 ** ADAM_STEP
    d_ref[...] = -ADAM_LR * ((mn / c1) / (jnp.sqrt(vn / c2) + ADAM_EPS) + ADAM_WD * w_ref[...])


def _adamw_rows(g, w, m, v, *, name):
    rows, cols = g.shape
    tr = _tile(rows, 512)

    def body(*refs):
        _adam_update(*refs)

    blk = pl.BlockSpec((tr, cols), lambda i: (i, 0))
    out = jax.ShapeDtypeStruct((rows, cols), F32)
    return pl.pallas_call(
        body, name=name, grid=(rows // tr,), in_specs=[blk] * 4, out_specs=[blk] * 3, out_shape=[out] * 3,
        compiler_params=_params(("parallel",)),
    )(g, w, m, v)


_ADAM_ROWWISE = ("w_in_ab", "w_q_b", "w_kv_b", "w_out_ab", "w_in_c", "w_out_c")


_SHARD_AXIS = {"w_in_ab": 1, "w_q_b": 1, "w_kv_b": 1, "w_out_ab": 0, "w_in_c": 1, "w_out_c": 0, "conv_w": 1, "norm_c": 1}
_ALL_NAMES = ("norm_ab", "w_in_ab", "q_a_norm", "w_q_b", "kv_a_norm", "w_kv_b", "pool_w", "pool_scale", "w_out_ab",
              "norm_c", "w_in_c", "conv_w", "a_log", "dt_bias", "o_norm", "w_out_c", "final_norm")


def _join_shards(a, axis):
    _, r, c = a.shape
    return a.reshape(4 * r, c) if axis == 0 else jnp.transpose(a, (1, 0, 2)).reshape(r, 4 * c)


def _split_shards(a, axis):
    r, c = a.shape
    return a.reshape(4, r // 4, c) if axis == 0 else jnp.transpose(a.reshape(r, 4, c // 4), (1, 0, 2))


def kernel(x, positions, norm_ab, w_in_ab, q_a_norm, w_q_b, kv_a_norm, w_kv_b, pool_w, pool_scale, w_out_ab, norm_c, w_in_c, conv_w, a_log, dt_bias, o_norm, w_out_c, final_norm, loss_target, m_norm_ab, m_w_in_ab, m_q_a_norm, m_w_q_b, m_kv_a_norm, m_w_kv_b, m_pool_w, m_pool_scale, m_w_out_ab, m_norm_c, m_w_in_c, m_conv_w, m_a_log, m_dt_bias, m_o_norm, m_w_out_c, m_final_norm, v_norm_ab, v_w_in_ab, v_q_a_norm, v_w_q_b, v_kv_a_norm, v_w_kv_b, v_pool_w, v_pool_scale, v_w_out_ab, v_norm_c, v_w_in_c, v_conv_w, v_a_log, v_dt_bias, v_o_norm, v_w_out_c, v_final_norm):
    given = dict(locals())
    c = lax.axis_index("c")
    t = x.shape[1]

    def shard_of(prefix, name):
        a = given[prefix + name]
        return a.reshape(a.shape[1:]) if a.ndim > 2 else a.reshape(1, -1)

    big, big_even, big_odd, small_sharded = _ADAM_ROWWISE, _ADAM_ROWWISE[:4], _ADAM_ROWWISE[4:], ("conv_w", "norm_c")
    chip = 2 * lax.axis_index("x") + lax.axis_index("y")
    core = c.astype(jnp.int32).reshape(1)
    late = big_odd + small_sharded
    late_shards = [shard_of("", n).astype(BF16) for n in big_odd] + [shard_of("", n) for n in small_sharded]
    gather_odd = _to_chips_start(late_shards, per_chip_slot=False, name="gather_odd_start")
    gathered = _gather_weights([shard_of("", n).astype(BF16) for n in big_even], [])
    full = {n: _join_shards(a, _SHARD_AXIS[n]) for n, a in zip(big_even, gathered)}
    for name in ("norm_ab", "q_a_norm", "kv_a_norm", "pool_w", "pool_scale"):
        full[name] = shard_of("", name)
    lw = _layout_even(full)
    lw["norm_ab"] = lw["norm_ab"] + gather_odd[-1][0, 0]

    def odd_weights(h1):
        landed = _to_chips_wait(gather_odd, h1, per_chip_slot=False, name="gather_odd_wait")
        w = {}
        for name, land, own in zip(late, landed, late_shards):
            w[name] = _join_shards(lax.dynamic_update_index_in_dim(land, own, chip, 0), _SHARD_AXIS[name])
        for name in ("a_log", "dt_bias", "o_norm", "final_norm"):
            w[name] = shard_of("", name)
        return _layout_odd(w)

    def chip_partials(names, grads, tag):
        slots = [_split_shards(grads[n], _SHARD_AXIS[n]) for n in names]
        partial = _core_swap_partial(slots, name="core_swap_partial_" + tag)
        return [_core_sum(s, p, core, name="core_sum_" + n) for n, s, p in zip(names, slots, partial)]

    odd = {}

    def on_odd_grads(g):
        odd["part"] = chip_partials(big_odd, {"w_in_c": g["winc"][:, :4112], "w_out_c": g["wout_c"]}, "odd")
        odd["started"] = _to_chips_start(odd["part"], per_chip_slot=True, name="exchange_odd_start")
        return odd["started"][-1][0, 0]

    loss_tile, dx, g = _local_step(x[0], positions.reshape(t, 1), loss_target[0], lw, odd_weights, on_odd_grads)
    grads = _unlayout_grads(g)
    exchanged = _chip_exchange(chip_partials(big_even, grads, "even"), _pack_small(g, loss_tile))
    landed_odd = _to_chips_wait(odd["started"], exchanged[-1], per_chip_slot=True, name="exchange_odd_wait")
    landed_odd = [lax.dynamic_update_index_in_dim(l, lax.dynamic_index_in_dim(p, chip, 0, keepdims=False), chip, 0)
                  for l, p in zip(landed_odd, odd["part"])]
    halves = [_chip_sum(l, core, name="chip_sum_" + n) for n, l in zip(big, list(exchanged[:-1]) + landed_odd)]
    gbig = dict(zip(big, _core_swap_sum(halves)))

    res = {}
    for name in big:
        res["grad", name] = gbig[name]
        out = _adamw_rows(gbig[name], shard_of("", name), shard_of("m_", name), shard_of("v_", name), name="adamw_" + name)
        res["delta", name], res["m", name], res["v", name] = out
    out = _small_update(exchanged[-1], [shard_of("", n) for n in _SMALL_NAMES], [shard_of("m_", n) for n in _SMALL_NAMES],
                        [shard_of("v_", n) for n in _SMALL_NAMES])
    for i, name in enumerate(_SMALL_NAMES):
        res["grad", name], res["delta", name], res["m", name], res["v", name] = out[4 * i:4 * i + 4]
    res = {k: a.reshape(given[k[1]].shape) for k, a in res.items()}
    loss = out[-1][0, 0]
    outs = [loss, dx.reshape(x.shape)]
    for key in ("grad", "delta", "m", "v"):
        outs += [res[key, n] for n in _ALL_NAMES]
    return tuple(outs)
```

```python
import functools

import jax
import jax.numpy as jnp
from jax import lax
from jax.experimental import pallas as pl
from jax.experimental.pallas import tpu as pltpu

F32 = jnp.float32
BF16 = jnp.bfloat16
HI = lax.Precision.HIGHEST
MESH = pl.DeviceIdType.MESH

RMS_EPS = 1e-6
D_MODEL = 1024
MLA_HEADS = 8
MLA_Q_RANK = 256
MLA_KV_RANK = 128
MLA_NOPE = 64
MLA_ROPE = 32
MLA_V = 64
ROPE_THETA = 10000.0
POOL_WINDOWS = (2, 4, 8, 16)
POOL_GROUP = 128
POOL_WIDTH = 512
POOL_HALO = 16
GDN_HEADS = 8
GDN_DK = 128
CONV_WIDTH = 4
CONV_HALO = 8
CHUNK = 64
IN_AB_PAD = 2048
IN_C_PAD = 4224
ATT_SCALE = (MLA_NOPE + MLA_ROPE) ** -0.5

ADAM_LR = 0.001
ADAM_B1 = 0.9
ADAM_B2 = 0.999
ADAM_EPS = 1e-08
ADAM_WD = 0.01
ADAM_STEP = 10

LANES = 128
VMEM_LIMIT = 56 * 1024 * 1024

ROW_TILE = 256
ATT_TILE = 1024
GDN_TILE = 256
MM_TILE = (1024, 1408, 2048)

NN = (((1,), (0,)), ((), ()))
NT = (((1,), (1,)), ((), ()))
TN = (((0,), (0,)), ((), ()))


def _dot(a, b, dims=NN, prec=None):
    return lax.dot_general(a, b, dims, precision=prec, preferred_element_type=F32)


def _tile(n, pref):
    if n <= pref:
        return n
    step = LANES if pref >= LANES else 8
    for t in range(pref - pref % step, 0, -step):
        if n % t == 0:
            return t
    return n


def _params(sem):
    return pltpu.CompilerParams(dimension_semantics=sem, vmem_limit_bytes=VMEM_LIMIT)


def _sigmoid(x):
    return 1.0 / (1.0 + jnp.exp(-x))


def _softplus(x):
    return jnp.maximum(x, 0.0) + jnp.log(1.0 + jnp.exp(-jnp.abs(x)))


def _matmul(a, b, mode, *, name, out_dtype=F32, add=None):
    if mode == "nn":
        (m, k), (k2, n) = a.shape, b.shape
    elif mode == "nt":
        (m, k), (n, k2) = a.shape, b.shape
    else:
        (k, m), (k2, n) = a.shape, b.shape
    assert k == k2, (a.shape, b.shape, mode)
    tm, tn, tk = _tile(m, MM_TILE[0]), _tile(n, MM_TILE[1]), _tile(k, MM_TILE[2])
    nk = k // tk
    if mode == "tn":
        a_spec = pl.BlockSpec((tk, tm), lambda i, j, kk: (kk, i))
    else:
        a_spec = pl.BlockSpec((tm, tk), lambda i, j, kk: (i, kk))
    if mode == "nt":
        b_spec = pl.BlockSpec((tn, tk), lambda i, j, kk: (j, kk))
    else:
        b_spec = pl.BlockSpec((tk, tn), lambda i, j, kk: (kk, j))
    o_spec = pl.BlockSpec((tm, tn), lambda i, j, kk: (i, j))
    dims = {"nn": NN, "nt": NT, "tn": TN}[mode]
    has_add = add is not None

    def body(*refs):
        a_ref, b_ref = refs[0], refs[1]
        add_ref = refs[2] if has_add else None
        o_ref = refs[3] if has_add else refs[2]

        def finish(o):
            if has_add:
                o = o + add_ref[...]
            o_ref[...] = o.astype(out_dtype)

        if nk == 1:
            finish(_dot(a_ref[...], b_ref[...], dims))
            return
        acc = refs[-1]
        kk = pl.program_id(2)

        @pl.when(kk == 0)
        def _():
            acc[...] = jnp.zeros_like(acc)

        acc[...] += _dot(a_ref[...], b_ref[...], dims)

        @pl.when(kk == nk - 1)
        def _():
            finish(acc[...])

    in_specs = [a_spec, b_spec] + ([o_spec] if has_add else [])
    args = (a, b) + ((add,) if has_add else ())
    return pl.pallas_call(
        body, name=name, grid=(m // tm, n // tn, nk), in_specs=in_specs, out_specs=o_spec,
        out_shape=jax.ShapeDtypeStruct((m, n), out_dtype),
        scratch_shapes=[pltpu.VMEM((tm, tn), F32)] if nk > 1 else [],
        compiler_params=_params(("parallel", "parallel", "arbitrary")),
    )(*args)


def _rms_fwd(h, g, *, name):
    t, d = h.shape
    tm = _tile(t, 2 * ROW_TILE)

    def body(h_ref, g_ref, o_ref):
        x = h_ref[...]
        r = lax.rsqrt(jnp.mean(x * x, axis=-1, keepdims=True) + RMS_EPS)
        o_ref[...] = (x * r * g_ref[...]).astype(BF16)

    return pl.pallas_call(
        body, name=name, grid=(t // tm,),
        in_specs=[pl.BlockSpec((tm, d), lambda i: (i, 0)), pl.BlockSpec((1, d), lambda i: (0, 0))],
        out_specs=pl.BlockSpec((tm, d), lambda i: (i, 0)),
        out_shape=jax.ShapeDtypeStruct((t, d), BF16), compiler_params=_params(("parallel",)),
    )(h, g)


def _rms_bwd(h, g, dy, dres, *, name, with_bf16):
    t, d = h.shape
    tm = _tile(t, 2 * ROW_TILE)

    def body(h_ref, g_ref, dy_ref, dres_ref, *outs):
        i = pl.program_id(0)
        dh_ref, dg_ref = outs[0], outs[-1]
        x = h_ref[...]
        r = lax.rsqrt(jnp.mean(x * x, axis=-1, keepdims=True) + RMS_EPS)
        xh = x * r
        dyv = dy_ref[...].astype(F32)
        dxh = dyv * g_ref[...]
        dx = r * (dxh - xh * jnp.mean(dxh * xh, axis=-1, keepdims=True))
        dh = dres_ref[...] + dx
        dh_ref[...] = dh
        if with_bf16:
            outs[1][...] = dh.astype(BF16)

        @pl.when(i == 0)
        def _():
            dg_ref[...] = jnp.zeros_like(dg_ref)

        dg_ref[...] += jnp.sum(dyv * xh, axis=0, keepdims=True)

    row = pl.BlockSpec((tm, d), lambda i: (i, 0))
    vec = pl.BlockSpec((1, d), lambda i: (0, 0))
    out_shape = [jax.ShapeDtypeStruct((t, d), F32)]
    out_specs = [row]
    if with_bf16:
        out_shape.append(jax.ShapeDtypeStruct((t, d), BF16))
        out_specs.append(row)
    out_shape.append(jax.ShapeDtypeStruct((1, d), F32))
    out_specs.append(vec)
    return pl.pallas_call(
        body, name=name, grid=(t // tm,), in_specs=[row, vec, row, row], out_specs=out_specs,
        out_shape=out_shape, compiler_params=_params(("arbitrary",)),
    )(h, g, dy, dres)


def _rope_partner(x):
    lane = lax.broadcasted_iota(jnp.int32, x.shape, 1)
    swapped = jnp.where(lane < MLA_NOPE + MLA_ROPE // 2, pltpu.roll(x, LANES - 16, 1), pltpu.roll(x, 16, 1))
    return jnp.where((lane >= MLA_NOPE) & (lane < MLA_NOPE + MLA_ROPE), swapped, 0.0)


def _pool_counts(row0, tm, w):
    t_idx = row0 + lax.broadcasted_iota(jnp.int32, (tm, POOL_GROUP), 0)
    return jnp.minimum(t_idx + 1, w).astype(F32)


def _ab_prep(proj, pos, inv_freq, q_a_norm, kv_a_norm, wq, wk, wv, wpool, *, name):
    t = proj.shape[0]
    tm = _tile(t, ROW_TILE)
    hb = tm // POOL_HALO

    def body(p_ref, halo_ref, pos_ref, inv_ref, qg_ref, kg_ref, wq_ref, wk_ref, wv_ref, wp_ref,
             q_ref, k_ref, v_ref, yb_ref, qn_ref, kvn_ref, d_ref, cos_ref, sin_ref, ext):
        i = pl.program_id(0)
        ql = p_ref[:, 0:MLA_Q_RANK]
        r = lax.rsqrt(jnp.mean(ql * ql, axis=-1, keepdims=True) + RMS_EPS)
        qn = (ql * r * qg_ref[...]).astype(BF16)
        qn_ref[...] = qn
        kl = p_ref[:, MLA_Q_RANK:MLA_Q_RANK + MLA_KV_RANK]
        r = lax.rsqrt(jnp.mean(kl * kl, axis=-1, keepdims=True) + RMS_EPS)
        kvn = (kl * r * kg_ref[...]).astype(BF16)
        kvn_ref[...] = kvn
        ang = pos_ref[...].astype(F32) * inv_ref[...]
        lane = lax.broadcasted_iota(jnp.int32, (tm, LANES), 1)
        in_rope = (lane >= MLA_NOPE) & (lane < MLA_NOPE + MLA_ROPE)
        cos_t = jnp.where(in_rope, jnp.cos(ang), 1.0)
        sin_t = jnp.where(in_rope, jnp.sin(ang), 0.0)
        sin_t = jnp.where(lane < MLA_NOPE + MLA_ROPE // 2, -sin_t, sin_t)
        cos_ref[...] = cos_t
        sin_ref[...] = sin_t
        kr = p_ref[:, 384:512]
        kr = kr * cos_t + _rope_partner(kr) * sin_t
        qraw = _dot(qn, wq_ref[...])
        kvk = _dot(kvn, wk_ref[...])
        for h in range(MLA_HEADS):
            sl = slice(h * LANES, (h + 1) * LANES)
            qh = qraw[:, sl]
            q_ref[:, sl] = ((qh * cos_t + _rope_partner(qh) * sin_t) * ATT_SCALE).astype(BF16)
            k_ref[:, sl] = (kvk[:, sl] + kr).astype(BF16)
        v_ref[...] = _dot(kvn, wv_ref[...]).astype(BF16)
        xp = p_ref[:, 512:1024]
        ext[0:POOL_HALO, :] = jnp.where(i > 0, halo_ref[...], 0.0)
        ext[POOL_HALO:POOL_HALO + tm, :] = xp
        for g, w in enumerate(POOL_WINDOWS):
            lo = g * POOL_GROUP
            acc = ext[POOL_HALO:POOL_HALO + tm, lo:lo + POOL_GROUP]
            for s in range(1, w):
                acc = acc + ext[POOL_HALO - s:POOL_HALO - s + tm, lo:lo + POOL_GROUP]
            cnt = _pool_counts(i * tm, tm, w)
            d_ref[:, lo:lo + POOL_GROUP] = (acc / cnt - xp[:, lo:lo + POOL_GROUP]).astype(BF16)
        yb_ref[...] = _dot(d_ref[...], wp_ref[...])

    row = lambda w: pl.BlockSpec((tm, w), lambda i: (i, 0))
    vec = lambda w: pl.BlockSpec((1, w), lambda i: (0, 0))
    whole = lambda a: pl.BlockSpec(a.shape, lambda i: (0, 0))
    return pl.pallas_call(
        body, name=name, grid=(t // tm,),
        in_specs=[row(1024), pl.BlockSpec((POOL_HALO, POOL_WIDTH), lambda i: (jnp.maximum(i * hb - 1, 0), 1)),
                  pl.BlockSpec((tm, 1), lambda i: (i, 0)), vec(LANES), vec(MLA_Q_RANK), vec(MLA_KV_RANK),
                  whole(wq), whole(wk), whole(wv), whole(wpool)],
        out_specs=[row(1024), row(1024), row(512), row(512), row(MLA_Q_RANK), row(MLA_KV_RANK), row(POOL_WIDTH),
                   row(LANES), row(LANES)],
        out_shape=[jax.ShapeDtypeStruct((t, 1024), BF16), jax.ShapeDtypeStruct((t, 1024), BF16),
                   jax.ShapeDtypeStruct((t, 512), BF16), jax.ShapeDtypeStruct((t, 512), F32),
                   jax.ShapeDtypeStruct((t, MLA_Q_RANK), BF16), jax.ShapeDtypeStruct((t, MLA_KV_RANK), BF16),
                   jax.ShapeDtypeStruct((t, POOL_WIDTH), BF16), jax.ShapeDtypeStruct((t, LANES), F32),
                   jax.ShapeDtypeStruct((t, LANES), F32)],
        scratch_shapes=[pltpu.VMEM((tm + POOL_HALO, POOL_WIDTH), F32)],
        compiler_params=_params(("parallel",)),
    )(proj, proj, pos, inv_freq, q_a_norm, kv_a_norm, wq, wk, wv, wpool)


def _ab_prep_bwd(proj, q_a_norm, kv_a_norm, dq, dk, dv, cos_t, sin_t, dyb, dz, wq, wk, wv, wpool, *, name):
    t = proj.shape[0]
    tm = _tile(t, ROW_TILE)
    hb = tm // POOL_HALO
    last_halo = t // POOL_HALO - 1
    nt = t // tm

    def body(p_ref, qg_ref, kg_ref, dq_ref, dk_ref, dv_ref, c_ref, s_ref, dyb_ref, dybn_ref, dz_ref,
             wq_ref, wk_ref, wv_ref, wp_ref, dp_ref, dqr_ref, dkb_ref, dqg_ref, dkg_ref, ext):
        i = pl.program_id(0)

        @pl.when(i == 0)
        def _():
            dqg_ref[...] = jnp.zeros_like(dqg_ref)
            dkg_ref[...] = jnp.zeros_like(dkg_ref)

        def norm_bwd(x, g, dy, dg_ref):
            r = lax.rsqrt(jnp.mean(x * x, axis=-1, keepdims=True) + RMS_EPS)
            xh = x * r
            dxh = dy * g
            dg_ref[...] += jnp.sum(dy * xh, axis=0, keepdims=True)
            return r * (dxh - xh * jnp.mean(dxh * xh, axis=-1, keepdims=True))

        c, s = c_ref[...], s_ref[...]
        lane = lax.broadcasted_iota(jnp.int32, (tm, LANES), 1)
        in_rope = (lane >= MLA_NOPE) & (lane < MLA_NOPE + MLA_ROPE)
        dkr = jnp.zeros((tm, LANES), F32)
        for h in range(MLA_HEADS):
            sl = slice(h * LANES, (h + 1) * LANES)
            g = dq_ref[:, sl]
            dqr_ref[:, sl] = ((g * c + _rope_partner(g * s)) * ATT_SCALE).astype(BF16)
            gk = dk_ref[:, sl]
            dkb_ref[:, sl] = gk.astype(BF16)
            dkr = dkr + jnp.where(in_rope, gk, 0.0)
        dkr = dkr * c + _rope_partner(dkr * s)
        dqn = _dot(dqr_ref[...], wq_ref[...], NT)
        dkvn = _dot(dkb_ref[...], wk_ref[...], NT) + _dot(dv_ref[...], wv_ref[...], NT)
        dql = norm_bwd(p_ref[:, 0:MLA_Q_RANK], qg_ref[...], dqn, dqg_ref)
        dp_ref[:, 0:MLA_Q_RANK] = dql.astype(BF16)
        dkl = norm_bwd(p_ref[:, MLA_Q_RANK:384], kg_ref[...], dkvn, dkg_ref)
        dp_ref[:, MLA_Q_RANK:384] = dkl.astype(BF16)
        dp_ref[:, 384:512] = dkr.astype(BF16)
        ddv = _dot(dyb_ref[...], wp_ref[...], NT)
        ddn = _dot(dybn_ref[...], wp_ref[...], NT)
        for g, w in enumerate(POOL_WINDOWS):
            lo = g * POOL_GROUP
            ext[0:tm, lo:lo + POOL_GROUP] = ddv[:, lo:lo + POOL_GROUP] / _pool_counts(i * tm, tm, w)
            nxt = ddn[:, lo:lo + POOL_GROUP] / _pool_counts((i + 1) * tm, POOL_HALO, w)
            ext[tm:tm + POOL_HALO, lo:lo + POOL_GROUP] = jnp.where(i < nt - 1, nxt, 0.0)
        for g, w in enumerate(POOL_WINDOWS):
            lo = g * POOL_GROUP
            acc = ext[0:tm, lo:lo + POOL_GROUP]
            for s in range(1, w):
                acc = acc + ext[s:s + tm, lo:lo + POOL_GROUP]
            dp_ref[:, 512 + lo:512 + lo + POOL_GROUP] = (acc - ddv[:, lo:lo + POOL_GROUP]).astype(BF16)
        dp_ref[:, 1024:2048] = dz_ref[...]

    row = lambda w: pl.BlockSpec((tm, w), lambda i: (i, 0))
    vec = lambda w: pl.BlockSpec((1, w), lambda i: (0, 0))
    whole = lambda a: pl.BlockSpec(a.shape, lambda i: (0, 0))
    return pl.pallas_call(
        body, name=name, grid=(nt,),
        in_specs=[row(1024), vec(MLA_Q_RANK), vec(MLA_KV_RANK), row(1024), row(1024), row(512), row(LANES), row(LANES),
                  row(POOL_WIDTH),
                  pl.BlockSpec((POOL_HALO, POOL_WIDTH), lambda i: (jnp.minimum((i + 1) * hb, last_halo), 0)),
                  row(1024), whole(wq), whole(wk), whole(wv), whole(wpool)],
        out_specs=[row(IN_AB_PAD), row(1024), row(1024), vec(MLA_Q_RANK), vec(MLA_KV_RANK)],
        out_shape=[jax.ShapeDtypeStruct((t, IN_AB_PAD), BF16), jax.ShapeDtypeStruct((t, 1024), BF16),
                   jax.ShapeDtypeStruct((t, 1024), BF16), jax.ShapeDtypeStruct((1, MLA_Q_RANK), F32),
                   jax.ShapeDtypeStruct((1, MLA_KV_RANK), F32)],
        scratch_shapes=[pltpu.VMEM((tm + POOL_HALO, POOL_WIDTH), F32)],
        compiler_params=_params(("arbitrary",)),
    )(proj, q_a_norm, kv_a_norm, dq, dk, dv, cos_t, sin_t, dyb, dyb, dz, wq, wk, wv, wpool)


def _gate_fwd(o, ybraw, proj, pool_scale, *, name):
    t = o.shape[0]
    tm = _tile(t, 2 * ROW_TILE)

    def body(o_ref, yb_ref, z_ref, ps_ref, y_ref):
        z = z_ref[...]
        sz = z * _sigmoid(z)
        y_ref[:, 0:512] = (o_ref[...] * sz[:, 0:512]).astype(BF16)
        y_ref[:, 512:1024] = (yb_ref[...] * ps_ref[...] * sz[:, 512:1024]).astype(BF16)

    row = lambda w: pl.BlockSpec((tm, w), lambda i: (i, 0))
    return pl.pallas_call(
        body, name=name, grid=(t // tm,),
        in_specs=[row(512), row(512), pl.BlockSpec((tm, 1024), lambda i: (i, 1)), pl.BlockSpec((1, 512), lambda i: (0, 0))],
        out_specs=row(1024), out_shape=jax.ShapeDtypeStruct((t, 1024), BF16), compiler_params=_params(("parallel",)),
    )(o, ybraw, proj, pool_scale)


def _gate_bwd(dy, o, ybraw, proj, pool_scale, *, name):
    t = o.shape[0]
    tm = _tile(t, ROW_TILE)

    def body(dy_ref, o_ref, yb_ref, z_ref, ps_ref, do_ref, dl_ref, dyb_ref, dz_ref, dps_ref):
        i = pl.program_id(0)
        z = z_ref[...]
        sg = _sigmoid(z)
        sz = z * sg
        dsz = sg * (1.0 + z * (1.0 - sg))
        dyv = dy_ref[...]
        dcat = dyv * sz
        ov = o_ref[...]
        ybs = yb_ref[...] * ps_ref[...]
        dz_ref[:, 0:512] = (dyv[:, 0:512] * ov * dsz[:, 0:512]).astype(BF16)
        dz_ref[:, 512:1024] = (dyv[:, 512:1024] * ybs * dsz[:, 512:1024]).astype(BF16)
        do = dcat[:, 0:512]
        do_ref[...] = do.astype(BF16)
        r_i = lax.broadcasted_iota(jnp.int32, (512, 512), 0) // MLA_V
        c_i = lax.broadcasted_iota(jnp.int32, (512, 512), 1) // MLA_V
        dl_ref[...] = _dot(do * ov, (r_i == c_i).astype(F32), NN, HI)
        dyb_ref[...] = (dcat[:, 512:1024] * ps_ref[...]).astype(BF16)

        @pl.when(i == 0)
        def _():
            dps_ref[...] = jnp.zeros_like(dps_ref)

        dps_ref[...] += jnp.sum(dcat[:, 512:1024] * yb_ref[...], axis=0, keepdims=True)

    row = lambda w: pl.BlockSpec((tm, w), lambda i: (i, 0))
    vec = pl.BlockSpec((1, 512), lambda i: (0, 0))
    return pl.pallas_call(
        body, name=name, grid=(t // tm,),
        in_specs=[row(1024), row(512), row(512), pl.BlockSpec((tm, 1024), lambda i: (i, 1)), vec],
        out_specs=[row(512), row(512), row(512), row(1024), vec],
        out_shape=[jax.ShapeDtypeStruct((t, 512), BF16), jax.ShapeDtypeStruct((t, 512), F32),
                   jax.ShapeDtypeStruct((t, 512), BF16), jax.ShapeDtypeStruct((t, 1024), BF16),
                   jax.ShapeDtypeStruct((1, 512), F32)],
        compiler_params=_params(("arbitrary",)),
    )(dy, o, ybraw, proj, pool_scale)


ATT_HP_FWD = 4
ATT_HP_BWD = 2


def _diag_mask(tq):
    return lax.broadcasted_iota(jnp.int32, (tq, tq), 1) <= lax.broadcasted_iota(jnp.int32, (tq, tq), 0)


def _block_schedule(nq, key_major):
    if key_major:
        pairs = [(qi, ki) for ki in range(nq) for qi in range(ki, nq)]
    else:
        pairs = [(qi, ki) for qi in range(nq) for ki in range(qi + 1)]
    return jnp.asarray([p[0] for p in pairs], jnp.int32), jnp.asarray([p[1] for p in pairs], jnp.int32)


def _attn_fwd(q, k, v, *, name):
    t = q.shape[0]
    tq = _tile(t, ATT_TILE)
    nq = t // tq
    hp = ATT_HP_FWD
    qi_tab, ki_tab = _block_schedule(nq, key_major=False)

    def body(qi_ref, ki_ref, q_ref, k_ref, v_ref, o_ref, lse_ref, m_sc, l_sc, acc_sc):
        step = pl.program_id(1)
        qi, ki = qi_ref[step], ki_ref[step]

        @pl.when(ki == 0)
        def _():
            m_sc[...] = jnp.full_like(m_sc, -jnp.inf)
            l_sc[...] = jnp.zeros_like(l_sc)
            acc_sc[...] = jnp.zeros_like(acc_sc)

        def block(on_diagonal):
            scores = []
            for h in range(hp):
                sl = slice(h * LANES, (h + 1) * LANES)
                scores.append(_dot(q_ref[:, sl], k_ref[:, sl], NT))
            if on_diagonal:
                mask = _diag_mask(tq)
                scores = [jnp.where(mask, s, -jnp.inf) for s in scores]
            for h, s in enumerate(scores):
                vv = v_ref[:, (h // 2) * LANES:(h // 2 + 1) * LANES]
                m_prev = m_sc[h]
                m_new = jnp.maximum(m_prev, jnp.max(s, axis=-1, keepdims=True))
                alpha = jnp.exp(m_prev - m_new)
                p = jnp.exp(s - m_new[:, 0:1])
                l_sc[h] = alpha * l_sc[h] + jnp.sum(p, axis=-1, keepdims=True)
                acc_sc[h] = alpha * acc_sc[h] + _dot(p.astype(BF16), vv)
                m_sc[h] = m_new

        pl.when(ki < qi)(functools.partial(block, False))
        pl.when(ki == qi)(functools.partial(block, True))

        @pl.when(ki == qi)
        def _():
            first = lax.broadcasted_iota(jnp.int32, (tq, LANES), 1) < MLA_V
            for pr in range(hp // 2):
                a, b = 2 * pr, 2 * pr + 1
                sl = slice(pr * LANES, (pr + 1) * LANES)
                o_ref[:, sl] = jnp.where(first, acc_sc[a] / l_sc[a], acc_sc[b] / l_sc[b])
                lse_ref[:, sl] = jnp.where(first, m_sc[a] + jnp.log(l_sc[a]), m_sc[b] + jnp.log(l_sc[b]))

    grid_spec = pltpu.PrefetchScalarGridSpec(
        num_scalar_prefetch=2, grid=(MLA_HEADS // hp, qi_tab.shape[0]),
        in_specs=[pl.BlockSpec((tq, hp * LANES), lambda g, s, qt, kt: (qt[s], g)),
                  pl.BlockSpec((tq, hp * LANES), lambda g, s, qt, kt: (kt[s], g)),
                  pl.BlockSpec((tq, hp * MLA_V), lambda g, s, qt, kt: (kt[s], g))],
        out_specs=[pl.BlockSpec((tq, hp * MLA_V), lambda g, s, qt, kt: (qt[s], g)),
                   pl.BlockSpec((tq, hp * MLA_V), lambda g, s, qt, kt: (qt[s], g))],
        scratch_shapes=[pltpu.VMEM((hp, tq, LANES), F32)] * 3,
    )
    return pl.pallas_call(
        body, name=name, grid_spec=grid_spec,
        out_shape=[jax.ShapeDtypeStruct((t, 512), F32), jax.ShapeDtypeStruct((t, 512), F32)],
        compiler_params=_params(("parallel", "arbitrary")),
    )(qi_tab, ki_tab, q, k, v)


def _attn_bwd(q, k, v, do, lse, delta, *, name):
    t = q.shape[0]
    tq = _tile(t, ATT_TILE)
    nq = t // tq
    hp = ATT_HP_BWD
    qi_tab, ki_tab = _block_schedule(nq, key_major=True)

    def body(qi_ref, ki_ref, q_ref, k_ref, v_ref, do_ref, lse_ref, dl_ref, dq_ref, dk_ref, dv_ref, dk_sc, dv_sc):
        step = pl.program_id(1)
        qi, ki = qi_ref[step], ki_ref[step]

        @pl.when(step == 0)
        def _():
            dq_ref[...] = jnp.zeros_like(dq_ref)

        @pl.when(qi == ki)
        def _():
            dk_sc[...] = jnp.zeros_like(dk_sc)
            dv_sc[...] = jnp.zeros_like(dv_sc)

        def block(on_diagonal):
            lane = lax.broadcasted_iota(jnp.int32, (tq, LANES), 1)
            rows = pl.ds(pl.multiple_of(qi * tq, tq), tq)
            heads = [slice(h * LANES, (h + 1) * LANES) for h in range(hp)]
            scores = [_dot(q_ref[:, sl], k_ref[:, sl], NT) for sl in heads]
            dps = []
            for h in range(hp):
                dov = do_ref[:, (h // 2) * LANES:(h // 2 + 1) * LANES]
                mine = (lane < MLA_V) if h % 2 == 0 else (lane >= MLA_V)
                dps.append(_dot(jnp.where(mine, dov, jnp.zeros_like(dov)), v_ref[:, (h // 2) * LANES:(h // 2 + 1) * LANES], NT))
            mask = _diag_mask(tq) if on_diagonal else None
            for h, sl in enumerate(heads):
                col = (h // 2) * LANES + (h % 2) * MLA_V
                p = jnp.exp(scores[h] - lse_ref[:, col:col + 1])
                if on_diagonal:
                    p = jnp.where(mask, p, 0.0)
                ds = (p * (dps[h] - dl_ref[:, col:col + 1])).astype(BF16)
                dv_sc[h] += _dot(p.astype(BF16), do_ref[:, (h // 2) * LANES:(h // 2 + 1) * LANES], TN)
                dk_sc[h] += _dot(ds, q_ref[:, sl], TN)
                dq_ref[rows, sl] += _dot(ds, k_ref[:, sl], NN)

        pl.when(qi > ki)(functools.partial(block, False))
        pl.when(qi == ki)(functools.partial(block, True))

        @pl.when(qi == nq - 1)
        def _():
            first = lax.broadcasted_iota(jnp.int32, (tq, LANES), 1) < MLA_V
            for h in range(hp):
                dk_ref[:, h * LANES:(h + 1) * LANES] = dk_sc[h]
            for pr in range(hp // 2):
                dv_ref[:, pr * LANES:(pr + 1) * LANES] = jnp.where(first, dv_sc[2 * pr], dv_sc[2 * pr + 1]).astype(BF16)

    qrow = lambda w: pl.BlockSpec((tq, w), lambda g, s, qt, kt: (qt[s], g))
    krow = lambda w: pl.BlockSpec((tq, w), lambda g, s, qt, kt: (kt[s], g))
    grid_spec = pltpu.PrefetchScalarGridSpec(
        num_scalar_prefetch=2, grid=(MLA_HEADS // hp, qi_tab.shape[0]),
        in_specs=[qrow(hp * LANES), krow(hp * LANES), krow(hp * MLA_V), qrow(hp * MLA_V), qrow(hp * MLA_V), qrow(hp * MLA_V)],
        out_specs=[pl.BlockSpec((t, hp * LANES), lambda g, s, qt, kt: (0, g)), krow(hp * LANES), krow(hp * MLA_V)],
        scratch_shapes=[pltpu.VMEM((hp, tq, LANES), F32), pltpu.VMEM((hp, tq, LANES), F32)],
    )
    return pl.pallas_call(
        body, name=name, grid_spec=grid_spec,
        out_shape=[jax.ShapeDtypeStruct((t, 1024), F32), jax.ShapeDtypeStruct((t, 1024), F32),
                   jax.ShapeDtypeStruct((t, 512), BF16)],
        compiler_params=_params(("parallel", "arbitrary")),
    )(qi_tab, ki_tab, q, k, v, do, lse, delta)


def _conv_rows(ext, tm, w_ref, sec):
    c0 = sec * 1024
    y = ext[CONV_HALO - 3:CONV_HALO - 3 + tm, c0:c0 + 1024] * w_ref[0:1, c0:c0 + 1024]
    for j in range(1, CONV_WIDTH):
        y = y + ext[CONV_HALO - 3 + j:CONV_HALO - 3 + j + tm, c0:c0 + 1024] * w_ref[j:j + 1, c0:c0 + 1024]
    return y


def _c_prep(proj_c, conv_w, a_log, dt_bias, *, name):
    t = proj_c.shape[0]
    tm = _tile(t, ROW_TILE)
    hb = tm // CONV_HALO

    def body(p_ref, halo_ref, ab_ref, w_ref, al_ref, dtb_ref, q_ref, k_ref, v_ref, g_ref, b_ref, gt_ref, ext):
        i = pl.program_id(0)
        ext[0:CONV_HALO, :] = jnp.where(i > 0, halo_ref[...], 0.0)
        ext[CONV_HALO:CONV_HALO + tm, :] = p_ref[...]
        for sec, o_ref in enumerate((q_ref, k_ref, v_ref)):
            y = _conv_rows(ext, tm, w_ref, sec)
            y = y * _sigmoid(y)
            if sec == 2:
                o_ref[...] = y
                continue
            scale = GDN_DK ** -0.5 if sec == 0 else 1.0
            for h in range(GDN_HEADS):
                sl = slice(h * LANES, (h + 1) * LANES)
                blk = y[:, sl]
                r = lax.rsqrt(jnp.sum(blk * blk, axis=-1, keepdims=True) + RMS_EPS)
                o_ref[:, sl] = blk * (r * scale)
        ab = ab_ref[...]
        g = -jnp.exp(al_ref[...]) * _softplus(ab + dtb_ref[...])
        beta = _sigmoid(ab)
        ri = lax.broadcasted_iota(jnp.int32, (tm, tm), 0)
        ci = lax.broadcasted_iota(jnp.int32, (tm, tm), 1)
        lower = ((ri // CHUNK) == (ci // CHUNK)) & (ri >= ci)
        gc = _dot(lower.astype(F32), g, NN, HI)
        eye = lax.broadcasted_iota(jnp.int32, (LANES, LANES), 0) == lax.broadcasted_iota(jnp.int32, (LANES, LANES), 1)
        gt_ref[...] = _dot(eye.astype(F32), gc, NT, HI)[0:GDN_HEADS, :]
        for h in range(GDN_HEADS):
            sl = slice(h * LANES, (h + 1) * LANES)
            g_ref[:, sl] = jnp.broadcast_to(gc[:, h:h + 1], (tm, LANES))
            b_ref[:, sl] = jnp.broadcast_to(beta[:, GDN_HEADS + h:GDN_HEADS + h + 1], (tm, LANES))

    row = lambda w: pl.BlockSpec((tm, w), lambda i: (i, 0))
    vec = lambda r, w: pl.BlockSpec((r, w), lambda i: (0, 0))
    out = jax.ShapeDtypeStruct((t, 1024), F32)
    return pl.pallas_call(
        body, name=name, grid=(t // tm,),
        in_specs=[row(3072), pl.BlockSpec((CONV_HALO, 3072), lambda i: (jnp.maximum(i * hb - 1, 0), 0)),
                  pl.BlockSpec((tm, LANES), lambda i: (i, 32)), vec(CONV_WIDTH, 3072), vec(1, LANES), vec(1, LANES)],
        out_specs=[row(1024)] * 5 + [pl.BlockSpec((GDN_HEADS, tm), lambda i: (0, i))],
        out_shape=[out] * 5 + [jax.ShapeDtypeStruct((GDN_HEADS, t), F32)],
        scratch_shapes=[pltpu.VMEM((tm + CONV_HALO, 3072), F32)],
        compiler_params=_params(("parallel",)),
    )(proj_c, proj_c, proj_c, conv_w, a_log, dt_bias)


def _c_prep_bwd(proj_c, conv_w, a_log, dt_bias, dq, dk, dv, dgb, dbb, dz, *, name):
    t = proj_c.shape[0]
    tm = _tile(t, ROW_TILE // 2)
    hb = tm // CONV_HALO
    nt = t // tm
    rev = lambda i: nt - 1 - i

    def body(p_ref, halo_ref, ab_ref, w_ref, al_ref, dtb_ref, dq_ref, dk_ref, dv_ref, dg_ref, db_ref, dz_ref,
             dp_ref, dw_ref, dal_ref, ddt_ref, ext, dyext, carry, taps):
        step = pl.program_id(0)
        i = rev(step)

        @pl.when(step == 0)
        def _():
            dw_ref[...] = jnp.zeros_like(dw_ref)
            dal_ref[...] = jnp.zeros_like(dal_ref)
            ddt_ref[...] = jnp.zeros_like(ddt_ref)
            carry[...] = jnp.zeros_like(carry)

        ext[0:CONV_HALO, :] = jnp.where(i > 0, halo_ref[...], 0.0)
        ext[CONV_HALO:CONV_HALO + tm, :] = p_ref[...]
        for sec, g_ref in enumerate((dq_ref, dk_ref, dv_ref)):
            c0 = sec * 1024
            for j in range(CONV_WIDTH):
                taps[j] = ext[CONV_HALO - 3 + j:CONV_HALO - 3 + j + tm, c0:c0 + 1024]
            y = taps[0] * w_ref[0:1, c0:c0 + 1024]
            for j in range(1, CONV_WIDTH):
                y = y + taps[j] * w_ref[j:j + 1, c0:c0 + 1024]
            sg = _sigmoid(y)
            act = y * sg
            if sec == 2:
                dact = g_ref[...]
            else:
                scale = GDN_DK ** -0.5 if sec == 0 else 1.0
                parts = []
                for h in range(GDN_HEADS):
                    sl = slice(h * LANES, (h + 1) * LANES)
                    blk = act[:, sl]
                    r = lax.rsqrt(jnp.sum(blk * blk, axis=-1, keepdims=True) + RMS_EPS)
                    n = blk * r
                    dn = g_ref[:, sl] * scale
                    parts.append(r * (dn - n * jnp.sum(dn * n, axis=-1, keepdims=True)))
                dact = jnp.concatenate(parts, axis=-1)
            dy = dact * (sg * (1.0 + y * (1.0 - sg)))
            dyext[0:tm, c0:c0 + 1024] = dy
            for j in range(CONV_WIDTH):
                dw_ref[j:j + 1, c0:c0 + 1024] += jnp.sum(dy * taps[j], axis=0, keepdims=True)
        dyext[tm:tm + CONV_HALO, :] = carry[...]
        carry[...] = dyext[0:CONV_HALO, :]
        for sec in range(3):
            c0 = sec * 1024
            dx = dyext[3:3 + tm, c0:c0 + 1024] * w_ref[0:1, c0:c0 + 1024]
            for j in range(1, CONV_WIDTH):
                dx = dx + dyext[3 - j:3 - j + tm, c0:c0 + 1024] * w_ref[j:j + 1, c0:c0 + 1024]
            dp_ref[:, c0:c0 + 1024] = dx.astype(BF16)
        dp_ref[:, 3072:4096] = dz_ref[...]
        lane = lax.broadcasted_iota(jnp.int32, (tm, LANES), 1)
        dg = jnp.zeros((tm, LANES), F32)
        dbeta = jnp.zeros((tm, LANES), F32)
        for h in range(GDN_HEADS):
            sl = slice(h * LANES, (h + 1) * LANES)
            dg = dg + jnp.where(lane == h, dg_ref[:, sl], 0.0)
            dbeta = dbeta + jnp.where(lane == GDN_HEADS + h, db_ref[:, sl], 0.0)
        ri = lax.broadcasted_iota(jnp.int32, (tm, tm), 0)
        ci = lax.broadcasted_iota(jnp.int32, (tm, tm), 1)
        upper = ((ri // CHUNK) == (ci // CHUNK)) & (ri <= ci)
        dg = _dot(upper.astype(F32), dg, NN, HI)
        pre = ab_ref[...] + dtb_ref[...]
        s = _sigmoid(pre)
        a_exp = jnp.exp(al_ref[...])
        dg_da = dg * (-a_exp * s)
        dp_ref[:, 4096:IN_C_PAD] = (dg_da + dbeta * s * (1.0 - s)).astype(BF16)
        dal_ref[...] += jnp.sum(dg * (-a_exp * _softplus(pre)), axis=0, keepdims=True)
        ddt_ref[...] += jnp.sum(dg_da, axis=0, keepdims=True)

    row = lambda w: pl.BlockSpec((tm, w), lambda s: (rev(s), 0))
    vec = lambda r, w: pl.BlockSpec((r, w), lambda s: (0, 0))
    return pl.pallas_call(
        body, name=name, grid=(nt,),
        in_specs=[row(3072), pl.BlockSpec((CONV_HALO, 3072), lambda s: (jnp.maximum(rev(s) * hb - 1, 0), 0)),
                  pl.BlockSpec((tm, LANES), lambda s: (rev(s), 32)), vec(CONV_WIDTH, 3072), vec(1, LANES), vec(1, LANES),
                  row(1024), row(1024), row(1024), row(1024), row(1024), row(1024)],
        out_specs=[row(IN_C_PAD), vec(CONV_WIDTH, 3072), vec(1, LANES), vec(1, LANES)],
        out_shape=[jax.ShapeDtypeStruct((t, IN_C_PAD), BF16), jax.ShapeDtypeStruct((CONV_WIDTH, 3072), F32),
                   jax.ShapeDtypeStruct((1, LANES), F32), jax.ShapeDtypeStruct((1, LANES), F32)],
        scratch_shapes=[pltpu.VMEM((tm + CONV_HALO, 3072), F32), pltpu.VMEM((tm + CONV_HALO, 3072), F32),
                        pltpu.VMEM((CONV_HALO, 3072), F32), pltpu.VMEM((CONV_WIDTH, tm, 1024), F32)],
        compiler_params=_params(("arbitrary",)),
    )(proj_c, proj_c, proj_c, conv_w, a_log, dt_bias, dq, dk, dv, dgb, dbb, dz)


def _o_gate_fwd(o, proj_c, o_norm, *, name):
    t = o.shape[0]
    tm = _tile(t, 2 * ROW_TILE)

    def body(o_ref, z_ref, g_ref, y_ref):
        for h in range(GDN_HEADS):
            sl = slice(h * LANES, (h + 1) * LANES)
            x = o_ref[:, sl]
            r = lax.rsqrt(jnp.mean(x * x, axis=-1, keepdims=True) + RMS_EPS)
            z = z_ref[:, sl]
            y_ref[:, sl] = (x * r * g_ref[...] * (z * _sigmoid(z))).astype(BF16)

    row = pl.BlockSpec((tm, 1024), lambda i: (i, 0))
    return pl.pallas_call(
        body, name=name, grid=(t // tm,),
        in_specs=[row, pl.BlockSpec((tm, 1024), lambda i: (i, 3)), pl.BlockSpec((1, LANES), lambda i: (0, 0))],
        out_specs=row, out_shape=jax.ShapeDtypeStruct((t, 1024), BF16), compiler_params=_params(("parallel",)),
    )(o, proj_c, o_norm)


def _o_gate_bwd(dy, o, proj_c, o_norm, *, name):
    t = o.shape[0]
    tm = _tile(t, 2 * ROW_TILE)

    def body(dy_ref, o_ref, z_ref, g_ref, do_ref, dz_ref, dg_ref):
        i = pl.program_id(0)

        @pl.when(i == 0)
        def _():
            dg_ref[...] = jnp.zeros_like(dg_ref)

        dg = jnp.zeros((1, LANES), F32)
        for h in range(GDN_HEADS):
            sl = slice(h * LANES, (h + 1) * LANES)
            x = o_ref[:, sl]
            r = lax.rsqrt(jnp.mean(x * x, axis=-1, keepdims=True) + RMS_EPS)
            xh = x * r
            z = z_ref[:, sl]
            sg = _sigmoid(z)
            dyv = dy_ref[:, sl]
            dn = dyv * (z * sg)
            dz_ref[:, sl] = (dyv * xh * g_ref[...] * (sg * (1.0 + z * (1.0 - sg)))).astype(BF16)
            dxh = dn * g_ref[...]
            do_ref[:, sl] = r * (dxh - xh * jnp.mean(dxh * xh, axis=-1, keepdims=True))
            dg = dg + jnp.sum(dn * xh, axis=0, keepdims=True)
        dg_ref[...] += dg

    row = pl.BlockSpec((tm, 1024), lambda i: (i, 0))
    vec = pl.BlockSpec((1, LANES), lambda i: (0, 0))
    return pl.pallas_call(
        body, name=name, grid=(t // tm,), in_specs=[row, row, pl.BlockSpec((tm, 1024), lambda i: (i, 3)), vec],
        out_specs=[row, row, vec],
        out_shape=[jax.ShapeDtypeStruct((t, 1024), F32), jax.ShapeDtypeStruct((t, 1024), BF16),
                   jax.ShapeDtypeStruct((1, LANES), F32)],
        compiler_params=_params(("arbitrary",)),
    )(dy, o, proj_c, o_norm)


PAIR = 2 * CHUNK
GDN_HP = 8


def _bdot(a, b, dims=NN):
    return _dot(a.astype(BF16), b.astype(BF16), dims)


def _each(f, *lists):
    return [f(*args) for args in zip(*lists)]


def _pair_common(q, k, v, gci, gcj, beta):
    ri = lax.broadcasted_iota(jnp.int32, (PAIR, PAIR), 0)
    ci = lax.broadcasted_iota(jnp.int32, (PAIR, PAIR), 1)
    same = (ri // CHUNK) == (ci // CHUNK)
    incl = same & (ri >= ci)
    strict = same & (ri > ci)
    eye = (ri == ci).astype(F32)
    first = lax.broadcasted_iota(jnp.int32, (PAIR, LANES), 0) < CHUNK
    gamma = _each(lambda gi, gj: jnp.where(incl, jnp.exp(jnp.minimum(gi - gj, 0.0)), 0.0), gci, gcj)
    kb = _each(jnp.multiply, k, beta)
    kk = _each(lambda a, b: _bdot(a, b, NT), kb, k)
    qk = _each(lambda a, b: _bdot(a, b, NT), q, k)
    m = _each(lambda x, g: jnp.where(strict, x * g, 0.0), kk, gamma)
    tm_ = _each(lambda x: eye - x, m)
    pw = _each(lambda x: _bdot(x, x), m)
    for it in range(5):
        tm_ = _each(lambda x, p: x + _bdot(x, p), tm_, pw)
        if it < 4:
            pw = _each(lambda p: _bdot(p, p), pw)
    eg = _each(jnp.exp, gci)
    vb = _each(jnp.multiply, v, beta)
    kbe = _each(jnp.multiply, kb, eg)
    uw = _each(lambda x, a, b: _bdot(x, jnp.concatenate([a, b], axis=1)), tm_, vb, kbe)
    attn = _each(lambda x, g: jnp.where(incl, x * g, 0.0), qk, gamma)
    gl_a = _each(lambda g: g[CHUNK - 1:CHUNK, :], gci)
    gl_b = _each(lambda g: g[PAIR - 1:PAIR, :], gci)
    ek = _each(lambda a, b, g: jnp.exp(jnp.where(first, a, b) - g), gl_a, gl_b, gci)
    return dict(incl=incl, strict=strict, gamma=gamma, kb=kb, m=m, tm=tm_, eg=eg, vb=vb, kbe=kbe,
                u=_each(lambda x: x[:, :LANES], uw), w=_each(lambda x: x[:, LANES:], uw), attn=attn,
                qd=_each(jnp.multiply, q, eg), ek=ek, kd=_each(jnp.multiply, k, ek),
                glast_a=_each(jnp.exp, gl_a), glast_b=_each(jnp.exp, gl_b))


def _gdn_specs(t, ts, order):
    nc = ts // CHUNK
    blk = pl.BlockSpec((ts, GDN_HP * LANES), lambda h, s: (order(s), h))
    row = pl.BlockSpec((GDN_HP, 1, ts), lambda h, s: (h, 0, order(s)))
    st = pl.BlockSpec((GDN_HP, nc, LANES, LANES), lambda h, s: (h, order(s), 0, 0))
    return blk, row, st


def _gdn_fwd(q, k, v, gcb, gct, bb, *, name):
    t = q.shape[0]
    ts = _tile(t, GDN_TILE)
    npair = ts // PAIR

    def body(q_ref, k_ref, v_ref, g_ref, gt_ref, b_ref, o_ref, st_ref, s_sc):
        @pl.when(pl.program_id(1) == 0)
        def _():
            s_sc[...] = jnp.zeros_like(s_sc)

        def pair(pi, _):
            rows = pl.ds(pl.multiple_of(pi * PAIR, PAIR), PAIR)
            heads = [slice(hh * LANES, (hh + 1) * LANES) for hh in range(GDN_HP)]
            c = CHUNK
            cat0 = lambda *xs: jnp.concatenate(xs, axis=0)
            s0 = [s_sc[hh] for hh in range(GDN_HP)]
            cm = _pair_common([q_ref[rows, sl] for sl in heads], [k_ref[rows, sl] for sl in heads],
                              [v_ref[rows, sl] for sl in heads], [g_ref[rows, sl] for sl in heads],
                              [gt_ref[hh, :, rows] for hh in range(GDN_HP)], [b_ref[rows, sl] for sl in heads])
            u, w, qd, kd = cm["u"], cm["w"], cm["qd"], cm["kd"]
            r0 = _each(lambda w_, q_, s: _bdot(cat0(w_[:c], q_[:c]), s), w, qd, s0)
            vn_a = _each(lambda u_, r: u_[:c] - r[:c], u, r0)
            s1 = _each(lambda s, gl, k_, vn: s * gl + _bdot(k_[:c], vn, TN), s0, cm["glast_a"], kd, vn_a)
            r1 = _each(lambda w_, q_, s: _bdot(cat0(w_[c:], q_[c:]), s), w, qd, s1)
            vn_b = _each(lambda u_, r: u_[c:] - r[:c], u, r1)
            s2 = _each(lambda s, gl, k_, vn: s * gl + _bdot(k_[c:], vn, TN), s1, cm["glast_b"], kd, vn_b)
            o = _each(lambda ra, rb, at, va, vb_: cat0(ra[c:], rb[c:]) + _bdot(at, cat0(va, vb_)),
                      r0, r1, cm["attn"], vn_a, vn_b)
            for hh, sl in enumerate(heads):
                st_ref[hh, 2 * pi] = s0[hh]
                st_ref[hh, 2 * pi + 1] = s1[hh]
                s_sc[hh] = s2[hh]
                o_ref[rows, sl] = o[hh]
            return 0

        lax.fori_loop(0, npair, pair, 0)

    blk, row, st = _gdn_specs(t, ts, lambda s: s)
    return pl.pallas_call(
        body, name=name, grid=(GDN_HEADS // GDN_HP, t // ts), in_specs=[blk, blk, blk, blk, row, blk],
        out_specs=[blk, st],
        out_shape=[jax.ShapeDtypeStruct((t, 1024), F32), jax.ShapeDtypeStruct((GDN_HEADS, t // CHUNK, LANES, LANES), F32)],
        scratch_shapes=[pltpu.VMEM((GDN_HP, LANES, LANES), F32)],
        compiler_params=_params(("parallel", "arbitrary")),
    )(q, k, v, gcb, gct, bb)


def _gdn_bwd(q, k, v, gcb, gct, bb, do, states, *, name):
    t = q.shape[0]
    ts = _tile(t, GDN_TILE)
    npair = ts // PAIR
    ns = t // ts
    c = CHUNK

    def body(q_ref, k_ref, v_ref, g_ref, gt_ref, b_ref, do_ref, st_ref, dq_ref, dk_ref, dv_ref, dg_ref, db_ref, ds_sc):
        @pl.when(pl.program_id(1) == 0)
        def _():
            ds_sc[...] = jnp.zeros_like(ds_sc)

        rowsum = lambda x: jnp.sum(x, axis=-1, keepdims=True)
        total = lambda x: jnp.sum(rowsum(x), axis=0, keepdims=True)
        cat0 = lambda *xs: jnp.concatenate(xs, axis=0)
        cat1 = lambda *xs: jnp.concatenate(xs, axis=1)

        def pair(step, _):
            pi = npair - 1 - step
            rows = pl.ds(pl.multiple_of(pi * PAIR, PAIR), PAIR)
            heads = [slice(hh * LANES, (hh + 1) * LANES) for hh in range(GDN_HP)]
            hs = range(GDN_HP)
            qv, kv, vv = ([r[rows, sl] for sl in heads] for r in (q_ref, k_ref, v_ref))
            beta = [b_ref[rows, sl] for sl in heads]
            dov = [do_ref[rows, sl] for sl in heads]
            s0 = [st_ref[hh, 2 * pi] for hh in hs]
            s1 = [st_ref[hh, 2 * pi + 1] for hh in hs]
            ds2 = [ds_sc[hh] for hh in hs]
            cm = _pair_common(qv, kv, vv, [g_ref[rows, sl] for sl in heads], [gt_ref[hh, :, rows] for hh in hs], beta)
            u, w, qd, kd, attn = cm["u"], cm["w"], cm["qd"], cm["kd"], cm["attn"]
            tmat, gamma, eg = cm["tm"], cm["gamma"], cm["eg"]
            incl, strict = cm["incl"], cm["strict"]
            vn_a = _each(lambda u_, w_, s: u_[:c] - _bdot(w_[:c], s), u, w, s0)
            vn_b = _each(lambda u_, w_, s: u_[c:] - _bdot(w_[c:], s), u, w, s1)
            vn = _each(cat0, vn_a, vn_b)
            dvn_att = _each(lambda a, d: _bdot(a, d, TN), attn, dov)
            dattn = _each(lambda d, v_: jnp.where(incl, _bdot(d, v_, NT), 0.0), dov, vn)
            dvn_b = _each(lambda x, k_, d: x[c:] + _bdot(k_[c:], d), dvn_att, kd, ds2)
            rb = _each(lambda d, x, s: _bdot(cat0(d[c:], x), s, NT), dov, dvn_b, s1)
            dkd_b = _each(lambda v_, d: _bdot(v_, d, NT), vn_b, ds2)
            dgl_b = _each(lambda d, s: total(d * s), ds2, s1)
            ds1 = _each(lambda d, gl, q_, w_, o_, x: d * gl + _bdot(cat0(q_[c:], w_[c:]), cat0(o_[c:], -x), TN),
                        ds2, cm["glast_b"], qd, w, dov, dvn_b)
            dvn_a = _each(lambda x, k_, d: x[:c] + _bdot(k_[:c], d), dvn_att, kd, ds1)
            ra = _each(lambda d, x, s: _bdot(cat0(d[:c], x), s, NT), dov, dvn_a, s0)
            dkd_a = _each(lambda v_, d: _bdot(v_, d, NT), vn_a, ds1)
            dgl_a = _each(lambda d, s: total(d * s), ds1, s0)
            ds0 = _each(lambda d, gl, q_, w_, o_, x: d * gl + _bdot(cat0(q_[:c], w_[:c]), cat0(o_[:c], -x), TN),
                        ds1, cm["glast_a"], qd, w, dov, dvn_a)
            dvn = _each(cat0, dvn_a, dvn_b)
            dqd = _each(lambda a, b: cat0(a[:c], b[:c]), ra, rb)
            dw = _each(lambda a, b: -cat0(a[c:], b[c:]), ra, rb)
            dkd = _each(cat0, dkd_a, dkd_b)
            dvw = _each(cat1, dvn, dw)
            dvbk = _each(lambda t_, x: _bdot(t_, x, TN), tmat, dvw)
            dvb = _each(lambda x: x[:, :LANES], dvbk)
            dkbe = _each(lambda x: x[:, LANES:], dvbk)
            dt_ = _each(lambda x, a, b: _bdot(x, cat1(a, b), NT), dvw, cm["vb"], cm["kbe"])
            da1 = _each(lambda t_, x: _bdot(t_, x, TN), tmat, dt_)
            dm = _each(lambda x, t_: jnp.where(strict, -_bdot(x, t_, NT), 0.0), da1, tmat)
            dkk = _each(jnp.multiply, dm, gamma)
            dqk = _each(jnp.multiply, dattn, gamma)
            z = _each(lambda a, b, c_, d: a * b + c_ * d, dm, cm["m"], dattn, attn)
            dkb = _each(lambda x, k_, y, e: _bdot(x, k_) + y * e, dkk, kv, dkbe, eg)
            dk = _each(lambda a, b, kb_, q_, x, e, y, be: _bdot(cat0(a, b), cat0(kb_, q_), TN) + x * e + y * be,
                       dkk, dqk, cm["kb"], qv, dkd, cm["ek"], dkb, beta)
            dq = _each(lambda x, k_, y, e: _bdot(x, k_) + y * e, dqk, kv, dqd, eg)

            def colsum_of(z_):
                zh = z_.astype(BF16)
                zl = (z_ - zh.astype(F32)).astype(BF16)
                return _dot(cat0(zh, zl), jnp.ones((2 * PAIR, LANES), BF16), TN)

            colsum = _each(colsum_of, z)
            ri = lax.broadcasted_iota(jnp.int32, (PAIR, LANES), 0)
            for hh, sl in enumerate(heads):
                dkd_kd = dkd[hh] * kd[hh]
                dgc = (rowsum(z[hh]) - colsum[hh] + rowsum(dqd[hh] * qd[hh]) - rowsum(dkd_kd)
                       + rowsum(dkbe[hh] * cm["kbe"][hh]))
                last_a = total(dkd_kd[:c]) + dgl_a[hh] * cm["glast_a"][hh]
                last_b = total(dkd_kd[c:]) + dgl_b[hh] * cm["glast_b"][hh]
                dgc = dgc + jnp.where(ri == c - 1, last_a, 0.0) + jnp.where(ri == PAIR - 1, last_b, 0.0)
                ds_sc[hh] = ds0[hh]
                dq_ref[rows, sl] = dq[hh]
                dk_ref[rows, sl] = dk[hh]
                dv_ref[rows, sl] = dvb[hh] * beta[hh]
                db_ref[rows, sl] = jnp.broadcast_to(rowsum(dkb[hh] * kv[hh]) + rowsum(dvb[hh] * vv[hh]), (PAIR, LANES))
                dg_ref[rows, sl] = dgc
            return 0

        lax.fori_loop(0, npair, pair, 0)

    blk, row, st = _gdn_specs(t, ts, lambda s: ns - 1 - s)
    out = jax.ShapeDtypeStruct((t, 1024), F32)
    return pl.pallas_call(
        body, name=name, grid=(GDN_HEADS // GDN_HP, ns), in_specs=[blk, blk, blk, blk, row, blk, blk, st],
        out_specs=[blk] * 5, out_shape=[out] * 5, scratch_shapes=[pltpu.VMEM((GDN_HP, LANES, LANES), F32)],
        compiler_params=_params(("parallel", "arbitrary")),
    )(q, k, v, gcb, gct, bb, do, states)


def _loss_head(h, g, target, *, name):
    t, d = h.shape
    tm = _tile(t, 2 * ROW_TILE)

    def body(h_ref, g_ref, t_ref, dh_ref, dhb_ref, dg_ref, loss_ref):
        i = pl.program_id(0)
        x = h_ref[...]
        r = lax.rsqrt(jnp.mean(x * x, axis=-1, keepdims=True) + RMS_EPS)
        xh = x * r
        err = xh * g_ref[...] - t_ref[...]
        dy = err * (1.0 / d)
        dxh = dy * g_ref[...]
        dh = r * (dxh - xh * jnp.mean(dxh * xh, axis=-1, keepdims=True))
        dh_ref[...] = dh
        dhb_ref[...] = dh.astype(BF16)

        @pl.when(i == 0)
        def _():
            dg_ref[...] = jnp.zeros_like(dg_ref)
            loss_ref[...] = jnp.zeros_like(loss_ref)

        dg_ref[...] += jnp.sum(dy * xh, axis=0, keepdims=True)
        part = 0.5 * jnp.sum(jnp.mean(err * err, axis=-1, keepdims=True), axis=0, keepdims=True)
        loss_ref[...] += jnp.broadcast_to(part, loss_ref.shape)

    row = pl.BlockSpec((tm, d), lambda i: (i, 0))
    vec = pl.BlockSpec((1, d), lambda i: (0, 0))
    return pl.pallas_call(
        body, name=name, grid=(t // tm,), in_specs=[row, vec, row],
        out_specs=[row, row, vec, pl.BlockSpec((8, LANES), lambda i: (0, 0))],
        out_shape=[jax.ShapeDtypeStruct((t, d), F32), jax.ShapeDtypeStruct((t, d), BF16),
                   jax.ShapeDtypeStruct((1, d), F32), jax.ShapeDtypeStruct((8, LANES), F32)],
        compiler_params=_params(("arbitrary",)),
    )(h, g, target)


def _pad_cols(w, n):
    return jnp.pad(w, ((0, 0), (0, n - w.shape[1])))


def _layout_odd(w):
    return dict(
        winc=_pad_cols(w["w_in_c"], IN_C_PAD).astype(BF16), wout_c=w["w_out_c"].astype(BF16), conv_w=w["conv_w"],
        a_log=_pad_cols(w["a_log"], LANES), dt_bias=_pad_cols(w["dt_bias"], LANES),
        norm_c=w["norm_c"], o_norm=w["o_norm"], final_norm=w["final_norm"],
    )


def _layout_weights(w):
    return {**_layout_even(w), **_layout_odd(w)}


def _layout_even(w):
    z = lambda r, c: jnp.zeros((r, c), w["w_in_ab"].dtype)
    wi = w["w_in_ab"]
    win = jnp.concatenate([wi[:, :384], z(1024, 64), wi[:, 384:416], z(1024, 32), wi[:, 416:]], axis=1)
    wq = jnp.pad(w["w_q_b"].reshape(MLA_Q_RANK, MLA_HEADS, 96), ((0, 0), (0, 0), (0, 32))).reshape(MLA_Q_RANK, 1024)
    kv3 = w["w_kv_b"].reshape(MLA_KV_RANK, MLA_HEADS, 128)
    wk = jnp.pad(kv3[..., :MLA_NOPE], ((0, 0), (0, 0), (0, 64))).reshape(MLA_KV_RANK, 1024)
    wv = kv3[..., MLA_NOPE:].reshape(MLA_KV_RANK, 512)
    pw = w["pool_w"]
    rows = []
    for g in range(4):
        rows.append(jnp.concatenate([pw[g] if j == g else z(128, 128) for j in range(4)], axis=1))
    wpool = jnp.concatenate(rows, axis=0)
    half = MLA_ROPE // 2
    inv = 1.0 / (ROPE_THETA ** (jnp.arange(half, dtype=F32) / half))
    inv_lane = jnp.concatenate([jnp.zeros((MLA_NOPE,), F32), inv, inv, jnp.zeros((32,), F32)]).reshape(1, LANES)
    return dict(
        win=win.astype(BF16), wq=wq.astype(BF16), wk=wk.astype(BF16), wv=wv.astype(BF16), wpool=wpool.astype(BF16),
        wout_ab=w["w_out_ab"].astype(BF16), inv_lane=inv_lane,
        norm_ab=w["norm_ab"], q_a_norm=w["q_a_norm"], kv_a_norm=w["kv_a_norm"], pool_scale=w["pool_scale"],
    )


def _unlayout_grads(g, names):
    out = {}
    for name in names:
        if name == "w_in_ab":
            dwin = g["win"]
            out[name] = jnp.concatenate([dwin[:, :384], dwin[:, 448:480], dwin[:, 512:]], axis=1)
        elif name == "w_q_b":
            out[name] = g["wq"].reshape(MLA_Q_RANK, MLA_HEADS, 128)[..., :96].reshape(MLA_Q_RANK, 768)
        elif name == "w_kv_b":
            out[name] = jnp.concatenate([g["wk"].reshape(MLA_KV_RANK, MLA_HEADS, 128)[..., :MLA_NOPE],
                                         g["wv"].reshape(MLA_KV_RANK, MLA_HEADS, MLA_V)], axis=-1).reshape(MLA_KV_RANK, 1024)
        elif name == "w_in_c":
            out[name] = g["winc"][:, :4112]
        else:
            out[name] = g[{"w_out_ab": "wout_ab", "w_out_c": "wout_c"}[name]]
    return out


def _local_step(x, pos, target, lw, odd_weights=None, on_grads=None):
    mm = _matmul
    hn = _rms_fwd(x, lw["norm_ab"], name="rms_ab")
    proj = mm(hn, lw["win"], "nn", name="in_ab")
    q, k, v, ybraw, qn, kvn, d, cos_t, sin_t = _ab_prep(
        proj, pos, lw["inv_lane"], lw["q_a_norm"], lw["kv_a_norm"], lw["wq"], lw["wk"], lw["wv"], lw["wpool"], name="ab_prep")
    o, lse = _attn_fwd(q, k, v, name="attn_fwd")
    y = _gate_fwd(o, ybraw, proj, lw["pool_scale"], name="gate_ab")
    h1 = mm(y, lw["wout_ab"], "nn", name="out_ab", add=x)
    lo = lw if odd_weights is None else odd_weights(h1)
    hn1 = _rms_fwd(h1, lo["norm_c"], name="rms_c")
    proj_c = mm(hn1, lo["winc"], "nn", name="in_c")
    q2, k2, v2, gb, bb, gt = _c_prep(proj_c, lo["conv_w"], lo["a_log"], lo["dt_bias"], name="c_prep")
    gt = gt.reshape(GDN_HEADS, 1, gt.shape[1])
    o2, states = _gdn_fwd(q2, k2, v2, gb, gt, bb, name="gdn_fwd")
    y2 = _o_gate_fwd(o2, proj_c, lo["o_norm"], name="gate_c")
    h2 = mm(y2, lo["wout_c"], "nn", name="out_c", add=h1)
    dh2, dh2b, d_final, loss = _loss_head(h2, lo["final_norm"], target, name="loss_head")
    g = {"final_norm": d_final}
    dy2 = mm(dh2b, lo["wout_c"], "nt", name="out_c_dx")
    g["wout_c"] = mm(y2, dh2b, "tn", name="out_c_dw")
    do2, dz2, g["o_norm"] = _o_gate_bwd(dy2, o2, proj_c, lo["o_norm"], name="gate_c_bwd")
    dq2, dk2, dv2, dgb, dbb = _gdn_bwd(q2, k2, v2, gb, gt, bb, do2, states, name="gdn_bwd")
    dproj_c, g["conv_w"], g["a_log"], g["dt_bias"] = _c_prep_bwd(
        proj_c, lo["conv_w"], lo["a_log"], lo["dt_bias"], dq2, dk2, dv2, dgb, dbb, dz2, name="c_prep_bwd")
    dhn1 = mm(dproj_c, lo["winc"], "nt", name="in_c_dx")
    g["winc"] = mm(hn1, dproj_c, "tn", name="in_c_dw")
    dh1, dh1b, g["norm_c"] = _rms_bwd(h1, lo["norm_c"], dhn1, dh2, name="rms_c_bwd", with_bf16=True)
    notify = (lambda tag: 0.0) if on_grads is None else (lambda tag: on_grads(tag, g))
    pool_scale = lw["pool_scale"] + notify("odd")
    dy = mm(dh1b, lw["wout_ab"], "nt", name="out_ab_dx")
    g["wout_ab"] = mm(y, dh1b, "tn", name="out_ab_dw")
    pool_scale = pool_scale + notify("out_ab")
    do, delta, dyb, dz, g["pool_scale"] = _gate_bwd(dy, o, ybraw, proj, pool_scale, name="gate_ab_bwd")
    dq, dk, dv = _attn_bwd(q, k, v, do, lse, delta, name="attn_bwd")
    dproj, dqraw, dkb, g["q_a_norm"], g["kv_a_norm"] = _ab_prep_bwd(
        proj, lw["q_a_norm"], lw["kv_a_norm"], dq, dk, dv, cos_t, sin_t, dyb, dz,
        lw["wq"], lw["wk"], lw["wv"], lw["wpool"], name="ab_prep_bwd")
    g["wpool"] = mm(d, dyb, "tn", name="pool_mix_dw")
    g["wq"] = mm(qn, dqraw, "tn", name="q_up_dw")
    g["wk"] = mm(kvn, dkb, "tn", name="k_up_dw")
    g["wv"] = mm(kvn, dv, "tn", name="v_up_dw")
    g["win"] = mm(hn, dproj, "tn", name="in_ab_dw")
    norm_ab = lw["norm_ab"] + notify("in_ab")
    dhn = mm(dproj, lw["win"], "nt", name="in_ab_dx")
    dx, g["norm_ab"] = _rms_bwd(x, norm_ab, dhn, dh1, name="rms_ab_bwd", with_bf16=False)
    return loss, dx, g


_HBM = pl.BlockSpec(memory_space=pltpu.HBM)


def _place():
    return lax.axis_index("x"), lax.axis_index("y"), lax.axis_index("c")


def _flip(v, f):
    return 1 - v if f else v


_CHIP_FLIPS = ((1, 0), (0, 1), (1, 1))
_DEV_FLIPS = tuple((fx, fy, fc) for fx in (0, 1) for fy in (0, 1) for fc in (0, 1) if fx or fy or fc)


def _rcopy(src, dst, send_sems, recv_sems, k, to):
    return pltpu.make_async_remote_copy(src_ref=src, dst_ref=dst, send_sem=send_sems.at[k], recv_sem=recv_sems.at[k],
                                        device_id=to, device_id_type=MESH)


def _my_half(ref, c, axis):
    rh = ref.shape[axis] // 2
    idx = [slice(None)] * len(ref.shape)
    idx[axis] = pl.ds(c * rh, rh)
    return ref.at[tuple(idx)]


def _gather_weights(bigs, smalls):
    nb, ns = len(bigs), len(smalls)

    def body(*refs):
        ins, outs = refs[:nb + ns], refs[nb + ns:2 * (nb + ns)]
        send_sems, recv_sems, local_sems = refs[2 * (nb + ns):]
        x, y, c = _place()
        j0 = 2 * x + y
        sib = (x, y, 1 - c)
        chips = [(_flip(x, fx), _flip(y, fy)) for fx, fy in _CHIP_FLIPS]
        local = [pltpu.make_async_copy(i_ref, o_ref.at[j0], local_sems.at[a])
                 for a, (i_ref, o_ref) in enumerate(zip(ins, outs))]
        for cp in local:
            cp.start()
        sends = []
        for k, (px, py) in enumerate(chips):
            for a in range(nb):
                sends.append(_rcopy(_my_half(ins[a], c, 0), _my_half(outs[a].at[j0], c, 0), send_sems, recv_sems,
                                    6 * a + k, (px, py, c)))
            for s in range(ns):
                sends.append(_rcopy(ins[nb + s], outs[nb + s].at[j0], send_sems, recv_sems, 6 * nb + 3 * s + k, (px, py, c)))
        for cp in sends:
            cp.start()
        for k, (px, py) in enumerate(chips):
            jk = 2 * px + py
            for a in range(nb):
                landed = _my_half(outs[a].at[jk], c, 0)
                _rcopy(landed, landed, send_sems, recv_sems, 6 * a + k, (px, py, c)).wait_recv()
                fwd = _rcopy(landed, landed, send_sems, recv_sems, 6 * a + 3 + k, sib)
                fwd.start()
                sends.append(fwd)
        for k, (px, py) in enumerate(chips):
            jk = 2 * px + py
            for a in range(nb):
                other = _my_half(outs[a].at[jk], 1 - c, 0)
                _rcopy(other, other, send_sems, recv_sems, 6 * a + 3 + k, sib).wait_recv()
            for s in range(ns):
                _rcopy(ins[nb + s], outs[nb + s].at[jk], send_sems, recv_sems, 6 * nb + 3 * s + k, (px, py, c)).wait_recv()
        for cp in sends:
            cp.wait_send()
        for cp in local:
            cp.wait()

    arrays = list(bigs) + list(smalls)
    n_sem = 6 * nb + 3 * ns
    return pl.pallas_call(
        body, name="gather_weights", in_specs=[_HBM] * len(arrays), out_specs=[_HBM] * len(arrays),
        out_shape=[jax.ShapeDtypeStruct((4,) + a.shape, a.dtype) for a in arrays],
        scratch_shapes=[pltpu.SemaphoreType.DMA((n_sem,)), pltpu.SemaphoreType.DMA((n_sem,)),
                        pltpu.SemaphoreType.DMA((len(arrays),))],
    )(*arrays)


def _core_swap_partial(gs, *, name):
    n = len(gs)

    def body(*refs):
        ins, outs = refs[:n], refs[n:2 * n]
        send_sems, recv_sems = refs[2 * n:]
        x, y, c = _place()
        copies = [_rcopy(_my_half(i_ref, 1 - c, 1), o_ref, send_sems, recv_sems, a, (x, y, 1 - c))
                  for a, (i_ref, o_ref) in enumerate(zip(ins, outs))]
        for cp in copies:
            cp.start()
        for cp in copies:
            cp.wait()

    return pl.pallas_call(
        body, name=name, in_specs=[_HBM] * n, out_specs=[_HBM] * n,
        out_shape=[jax.ShapeDtypeStruct((4, g.shape[1] // 2, g.shape[2]), g.dtype) for g in gs],
        scratch_shapes=[pltpu.SemaphoreType.DMA((n,)), pltpu.SemaphoreType.DMA((n,))],
    )(*gs)


def _core_swap_sum(fs):
    n = len(fs)

    def body(*refs):
        ins, outs = refs[:n], refs[n:2 * n]
        send_sems, recv_sems = refs[2 * n:]
        x, y, c = _place()
        copies = [_rcopy(_my_half(i_ref, c, 0), _my_half(o_ref, c, 0), send_sems, recv_sems, a, (x, y, 1 - c))
                  for a, (i_ref, o_ref) in enumerate(zip(ins, outs))]
        for cp in copies:
            cp.start()
        for a, cp in enumerate(copies):
            cp.wait_send()
            theirs = _my_half(outs[a], 1 - c, 0)
            _rcopy(theirs, theirs, send_sems, recv_sems, a, (x, y, 1 - c)).wait_recv()

    return pl.pallas_call(
        body, name="core_swap_sum", in_specs=[_HBM] * n, out_specs=[_HBM] * n,
        out_shape=[jax.ShapeDtypeStruct(f.shape, f.dtype) for f in fs],
        input_output_aliases={a: a for a in range(n)},
        scratch_shapes=[pltpu.SemaphoreType.DMA((n,)), pltpu.SemaphoreType.DMA((n,))],
    )(*fs)


_SEM = pl.BlockSpec(memory_space=pltpu.SEMAPHORE)
_ANY = pl.BlockSpec(memory_space=pl.ANY)
_DATAFLOW = pltpu.SideEffectType.DATAFLOW_SIDE_EFFECTING


def _to_chips_copies(srcs, lands, send_sems, recv_sems, per_chip_slot):
    x, y, c = _place()
    j0 = 2 * x + y
    out = []
    for k, (fx, fy) in enumerate(_CHIP_FLIPS):
        px, py = _flip(x, fx), _flip(y, fy)
        jk = 2 * px + py
        for a, (src, land) in enumerate(zip(srcs, lands)):
            piece = src.at[jk] if per_chip_slot else src
            out.append((_rcopy(piece, land.at[j0], send_sems, recv_sems, 3 * a + k, (px, py, c)),
                        _rcopy(piece, land.at[jk], send_sems, recv_sems, 3 * a + k, (px, py, c))))
    return out


def _to_chips_start(arrays, *, per_chip_slot, name):
    n = len(arrays)
    lands = [lax.empty((4,) + (a.shape[1:] if per_chip_slot else a.shape), a.dtype) for a in arrays]

    def body(*refs):
        srcs, land_refs, send_sems, recv_sems, token = refs[:n], refs[n:2 * n], refs[2 * n], refs[2 * n + 1], refs[-1]
        for send, _ in _to_chips_copies(srcs, land_refs, send_sems, recv_sems, per_chip_slot):
            send.start()
        token[...] = jnp.zeros_like(token)

    held = [pltpu.with_memory_space_constraint(a, pltpu.HBM) for a in list(arrays) + lands]
    return pl.pallas_call(
        body, name=name, in_specs=[_HBM] * (2 * n),
        out_specs=(_SEM, _SEM, *[_HBM] * (2 * n), pl.BlockSpec(memory_space=pltpu.VMEM)),
        out_shape=(pltpu.SemaphoreType.DMA((3 * n,)), pltpu.SemaphoreType.DMA((3 * n,)),
                   *[pltpu.HBM(a.shape, a.dtype) for a in held], jax.ShapeDtypeStruct((8, LANES), F32)),
        input_output_aliases={i: 2 + i for i in range(2 * n)},
        compiler_params=pltpu.CompilerParams(has_side_effects=_DATAFLOW),
    )(*held)


def _to_chips_wait(started, after, *, per_chip_slot, name):
    send_sems, recv_sems, held = started[0], started[1], started[2:-1]
    n = len(held) // 2

    def body(*refs):
        srcs, land_refs, s_sems, r_sems = refs[:n], refs[n:2 * n], refs[2 * n], refs[2 * n + 1]
        for send, arrival in _to_chips_copies(srcs, land_refs, s_sems, r_sems, per_chip_slot):
            send.wait_send()
            arrival.wait_recv()

    out = pl.pallas_call(
        body, name=name, in_specs=[_HBM] * (2 * n) + [_SEM, _SEM, _ANY], out_specs=[_HBM] * (2 * n),
        out_shape=[pltpu.HBM(a.shape, a.dtype) for a in held],
        input_output_aliases={i: i for i in range(2 * n)},
        compiler_params=pltpu.CompilerParams(has_side_effects=_DATAFLOW),
    )(*held, send_sems, recv_sems, after)
    return out[n:]


def _chip_exchange(ps, small):
    n = len(ps)
    rs = small.shape[0]

    def body(*refs):
        p_refs, s_ref = refs[:n], refs[n]
        l_refs, ls_ref = refs[n + 1:2 * n + 1], refs[2 * n + 1]
        send_sems, recv_sems, local_sems = refs[2 * n + 2:]
        x, y, c = _place()
        j0 = 2 * x + y
        d0 = 2 * j0 + c
        local = [pltpu.make_async_copy(p.at[j0], l.at[j0], local_sems.at[a]) for a, (p, l) in enumerate(zip(p_refs, l_refs))]
        local.append(pltpu.make_async_copy(s_ref, ls_ref.at[d0], local_sems.at[n]))
        for cp in local:
            cp.start()
        sends = []
        for k, (fx, fy) in enumerate(_CHIP_FLIPS):
            px, py = _flip(x, fx), _flip(y, fy)
            for a in range(n):
                sends.append(_rcopy(p_refs[a].at[2 * px + py], l_refs[a].at[j0], send_sems, recv_sems, 3 * a + k, (px, py, c)))
        for k, (fx, fy, fc) in enumerate(_DEV_FLIPS):
            peer = (_flip(x, fx), _flip(y, fy), _flip(c, fc))
            sends.append(_rcopy(s_ref, ls_ref.at[d0], send_sems, recv_sems, 3 * n + k, peer))
        for cp in sends:
            cp.start()
        for k, (fx, fy) in enumerate(_CHIP_FLIPS):
            px, py = _flip(x, fx), _flip(y, fy)
            for a in range(n):
                _rcopy(p_refs[a].at[j0], l_refs[a].at[2 * px + py], send_sems, recv_sems, 3 * a + k, (px, py, c)).wait_recv()
        for k, (fx, fy, fc) in enumerate(_DEV_FLIPS):
            px, py, pc = _flip(x, fx), _flip(y, fy), _flip(c, fc)
            _rcopy(s_ref, ls_ref.at[4 * px + 2 * py + pc], send_sems, recv_sems, 3 * n + k, (px, py, pc)).wait_recv()
        for cp in sends:
            cp.wait_send()
        for cp in local:
            cp.wait()

    n_sem = 3 * n + 7
    return pl.pallas_call(
        body, name="chip_exchange", in_specs=[_HBM] * (n + 1), out_specs=[_HBM] * (n + 1),
        out_shape=[jax.ShapeDtypeStruct(p.shape, F32) for p in ps] + [jax.ShapeDtypeStruct((8, rs, LANES), F32)],
        scratch_shapes=[pltpu.SemaphoreType.DMA((n_sem,)), pltpu.SemaphoreType.DMA((n_sem,)),
                        pltpu.SemaphoreType.DMA((n + 1,))],
    )(*ps, small)


def _core_sum(g, part, core, *, name):
    _, rh, cols = part.shape
    tr = _tile(rh, 256)
    nb = rh // tr

    def body(c_ref, g_ref, p_ref, o_ref):
        o_ref[...] = g_ref[...] + p_ref[...]

    grid_spec = pltpu.PrefetchScalarGridSpec(
        num_scalar_prefetch=1, grid=(4, nb),
        in_specs=[pl.BlockSpec((1, tr, cols), lambda j, i, c: (j, c[0] * nb + i, 0)),
                  pl.BlockSpec((1, tr, cols), lambda j, i, c: (j, i, 0))],
        out_specs=pl.BlockSpec((1, tr, cols), lambda j, i, c: (j, i, 0)),
    )
    return pl.pallas_call(
        body, name=name, grid_spec=grid_spec, out_shape=jax.ShapeDtypeStruct(part.shape, F32),
        compiler_params=_params(("parallel", "parallel")),
    )(core, g, part)


def _chip_sum(landed, core, *, name):
    _, rh, cols = landed.shape
    tr = _tile(rh, 256)
    nb = rh // tr

    def body(c_ref, l_ref, o_ref):
        o_ref[...] = ((l_ref[0] + l_ref[1]) + l_ref[2]) + l_ref[3]

    grid_spec = pltpu.PrefetchScalarGridSpec(
        num_scalar_prefetch=1, grid=(nb,),
        in_specs=[pl.BlockSpec((4, tr, cols), lambda i, c: (0, i, 0))],
        out_specs=pl.BlockSpec((tr, cols), lambda i, c: (c[0] * nb + i, 0)),
    )
    return pl.pallas_call(
        body, name=name, grid_spec=grid_spec, out_shape=jax.ShapeDtypeStruct((2 * rh, cols), F32),
        compiler_params=_params(("parallel",)),
    )(core, landed)


_ROW_POOL_W, _ROW_NORM_AB, _ROW_FINAL, _ROW_POOL_SCALE, _ROW_Q_NORM = 0, 512, 520, 528, 532
_ROW_KV_NORM, _ROW_O_NORM, _ROW_A_LOG, _ROW_DT_BIAS, _ROW_LOSS = 534, 535, 536, 537, 538
_ROW_CONV, _ROW_NORM_C, _SMALL_ROWS = 544, 640, 672
_CONV_ROWS = CONV_WIDTH * 6


def _put_rows(dst_ref, row0, src, width):
    for r in range(width // LANES):
        dst_ref[row0 + r:row0 + r + 1, :] = src[:, r * LANES:(r + 1) * LANES]


def _pack_small(g, loss_tile):
    names = ("wpool", "norm_ab", "final_norm", "pool_scale", "q_a_norm", "kv_a_norm", "o_norm", "a_log", "dt_bias",
             "conv_w", "norm_c")

    def body(wpool, norm_ab, final_norm, pool_scale, q_norm, kv_norm, o_norm, a_log, dt_bias, conv_w, norm_c, loss, o_ref):
        o_ref[...] = jnp.zeros_like(o_ref)
        for gi in range(4):
            o_ref[_ROW_POOL_W + gi * 128:_ROW_POOL_W + (gi + 1) * 128, :] = wpool[gi * 128:(gi + 1) * 128, gi * 128:(gi + 1) * 128]
        _put_rows(o_ref, _ROW_NORM_AB, norm_ab[...], 1024)
        _put_rows(o_ref, _ROW_FINAL, final_norm[...], 1024)
        _put_rows(o_ref, _ROW_POOL_SCALE, pool_scale[...], 512)
        _put_rows(o_ref, _ROW_Q_NORM, q_norm[...], 256)
        for row, ref in ((_ROW_KV_NORM, kv_norm), (_ROW_O_NORM, o_norm), (_ROW_A_LOG, a_log), (_ROW_DT_BIAS, dt_bias)):
            o_ref[row:row + 1, :] = ref[...]
        o_ref[_ROW_LOSS:_ROW_LOSS + 1, :] = loss[0:1, :]
        for j in range(4):
            for r in range(CONV_WIDTH):
                _put_rows(o_ref, _ROW_CONV + j * _CONV_ROWS + r * 6, conv_w[r:r + 1, j * 768:(j + 1) * 768], 768)
            _put_rows(o_ref, _ROW_NORM_C + j * 8, norm_c[:, j * 256:(j + 1) * 256], 256)

    vmem = pl.BlockSpec(memory_space=pltpu.VMEM)
    return pl.pallas_call(
        body, name="pack_small", in_specs=[vmem] * 12, out_specs=vmem,
        out_shape=jax.ShapeDtypeStruct((_SMALL_ROWS, LANES), F32),
    )(*[g[n] for n in names], loss_tile)


_SMALL_NAMES = ("pool_w", "norm_ab", "final_norm", "pool_scale", "q_a_norm", "kv_a_norm", "o_norm", "a_log", "dt_bias",
                "conv_w", "norm_c")


def _take_rows(src, row0, width):
    return jnp.concatenate([src[row0 + r:row0 + r + 1, :] for r in range(width // LANES)], axis=1)


def _small_update(small_all, ws, ms, vs):
    n = len(_SMALL_NAMES)

    def body(*refs):
        a_ref = refs[0]
        w_refs, m_refs, v_refs = refs[1:1 + n], refs[1 + n:1 + 2 * n], refs[1 + 2 * n:1 + 3 * n]
        outs = refs[1 + 3 * n:1 + 7 * n]
        loss_ref, tot = refs[1 + 7 * n], refs[2 + 7 * n]
        acc = a_ref[0]
        for d in range(1, 8):
            acc = acc + a_ref[d]
        tot[...] = acc
        x, y, _ = _place()
        j0 = 2 * x + y
        conv = tot[pl.ds(pl.multiple_of(_ROW_CONV + j0 * _CONV_ROWS, 8), _CONV_ROWS), :]
        norm_c = tot[pl.ds(pl.multiple_of(_ROW_NORM_C + j0 * 8, 8), 8), :]
        whole = tot[_ROW_NORM_AB:_ROW_CONV, :]
        at = lambda row: row - _ROW_NORM_AB
        grads = {
            "norm_ab": _take_rows(whole, at(_ROW_NORM_AB), 1024), "final_norm": _take_rows(whole, at(_ROW_FINAL), 1024),
            "pool_scale": _take_rows(whole, at(_ROW_POOL_SCALE), 512), "q_a_norm": _take_rows(whole, at(_ROW_Q_NORM), 256),
            "kv_a_norm": whole[at(_ROW_KV_NORM):at(_ROW_KV_NORM) + 1, :], "o_norm": whole[at(_ROW_O_NORM):at(_ROW_O_NORM) + 1, :],
            "a_log": tot[_ROW_A_LOG:_ROW_A_LOG + 1, 0:GDN_HEADS],
            "dt_bias": tot[_ROW_DT_BIAS:_ROW_DT_BIAS + 1, 0:GDN_HEADS],
            "norm_c": _take_rows(norm_c, 0, 256),
        }
        loss_ref[...] = whole[at(_ROW_LOSS):at(_ROW_LOSS) + 1, :]
        for i, name in enumerate(_SMALL_NAMES):
            g_out = outs[4 * i]
            if name == "pool_w":
                for gi in range(4):
                    g_out[gi] = tot[_ROW_POOL_W + gi * 128:_ROW_POOL_W + (gi + 1) * 128, :]
            elif name == "conv_w":
                for r in range(CONV_WIDTH):
                    g_out[r:r + 1, :] = _take_rows(conv, r * 6, 768)
            else:
                g_out[...] = grads[name]
            _adam_update(g_out, w_refs[i], m_refs[i], v_refs[i], *outs[4 * i + 1:4 * i + 4])

    vmem = pl.BlockSpec(memory_space=pltpu.VMEM)
    out_shape = [jax.ShapeDtypeStruct(w.shape, F32) for w in ws for _ in range(4)] + [jax.ShapeDtypeStruct((1, LANES), F32)]
    return pl.pallas_call(
        body, name="small_update", in_specs=[vmem] * (1 + 3 * n), out_specs=[vmem] * (4 * n + 1), out_shape=out_shape,
        scratch_shapes=[pltpu.VMEM((_SMALL_ROWS, LANES), F32)],
        compiler_params=pltpu.CompilerParams(vmem_limit_bytes=VMEM_LIMIT),
    )(small_all, *ws, *ms, *vs)


def _adam_update(g_ref, w_ref, m_ref, v_ref, d_ref, mo_ref, vo_ref):
    gv = g_ref[...]
    mn = ADAM_B1 * m_ref[...] + (1.0 - ADAM_B1) * gv
    vn = ADAM_B2 * v_ref[...] + (1.0 - ADAM_B2) * (gv * gv)
    mo_ref[...] = mn
    vo_ref[...] = vn
    c1 = 1.0 - ADAM_B1 ** ADAM_STEP
    c2 = 1.0 - ADAM_B2 ** ADAM_STEP
    d_ref[...] = -ADAM_LR * ((mn / c1) / (jnp.sqrt(vn / c2) + ADAM_EPS) + ADAM_WD * w_ref[...])


def _adamw_rows(g, w, m, v, *, name):
    rows, cols = g.shape
    tr = _tile(rows, 512)

    def body(*refs):
        _adam_update(*refs)

    blk = pl.BlockSpec((tr, cols), lambda i: (i, 0))
    out = jax.ShapeDtypeStruct((rows, cols), F32)
    return pl.pallas_call(
        body, name=name, grid=(rows // tr,), in_specs=[blk] * 4, out_specs=[blk] * 3, out_shape=[out] * 3,
        compiler_params=_params(("parallel",)),
    )(g, w, m, v)


_ADAM_ROWWISE = ("w_in_ab", "w_q_b", "w_kv_b", "w_out_ab", "w_in_c", "w_out_c")


_SHARD_AXIS = {"w_in_ab": 1, "w_q_b": 1, "w_kv_b": 1, "w_out_ab": 0, "w_in_c": 1, "w_out_c": 0, "conv_w": 1, "norm_c": 1}
_ALL_NAMES = ("norm_ab", "w_in_ab", "q_a_norm", "w_q_b", "kv_a_norm", "w_kv_b", "pool_w", "pool_scale", "w_out_ab",
              "norm_c", "w_in_c", "conv_w", "a_log", "dt_bias", "o_norm", "w_out_c", "final_norm")


def _join_shards(a, axis):
    _, r, c = a.shape
    return a.reshape(4 * r, c) if axis == 0 else jnp.transpose(a, (1, 0, 2)).reshape(r, 4 * c)


def _split_shards(a, axis):
    r, c = a.shape
    return a.reshape(4, r // 4, c) if axis == 0 else jnp.transpose(a.reshape(r, 4, c // 4), (1, 0, 2))


def kernel(x, positions, norm_ab, w_in_ab, q_a_norm, w_q_b, kv_a_norm, w_kv_b, pool_w, pool_scale, w_out_ab, norm_c, w_in_c, conv_w, a_log, dt_bias, o_norm, w_out_c, final_norm, loss_target, m_norm_ab, m_w_in_ab, m_q_a_norm, m_w_q_b, m_kv_a_norm, m_w_kv_b, m_pool_w, m_pool_scale, m_w_out_ab, m_norm_c, m_w_in_c, m_conv_w, m_a_log, m_dt_bias, m_o_norm, m_w_out_c, m_final_norm, v_norm_ab, v_w_in_ab, v_q_a_norm, v_w_q_b, v_kv_a_norm, v_w_kv_b, v_pool_w, v_pool_scale, v_w_out_ab, v_norm_c, v_w_in_c, v_conv_w, v_a_log, v_dt_bias, v_o_norm, v_w_out_c, v_final_norm):
    given = dict(locals())
    c = lax.axis_index("c")
    t = x.shape[1]

    def shard_of(prefix, name):
        a = given[prefix + name]
        return a.reshape(a.shape[1:]) if a.ndim > 2 else a.reshape(1, -1)

    big, big_even, big_odd, small_sharded = _ADAM_ROWWISE, _ADAM_ROWWISE[:4], _ADAM_ROWWISE[4:], ("conv_w", "norm_c")
    chip = 2 * lax.axis_index("x") + lax.axis_index("y")
    core = c.astype(jnp.int32).reshape(1)
    late = big_odd + small_sharded
    late_shards = [shard_of("", n).astype(BF16) for n in big_odd] + [shard_of("", n) for n in small_sharded]
    gather_odd = _to_chips_start(late_shards, per_chip_slot=False, name="gather_odd_start")
    gathered = _gather_weights([shard_of("", n).astype(BF16) for n in big_even], [])
    full = {n: _join_shards(a, _SHARD_AXIS[n]) for n, a in zip(big_even, gathered)}
    for name in ("norm_ab", "q_a_norm", "kv_a_norm", "pool_w", "pool_scale"):
        full[name] = shard_of("", name)
    lw = _layout_even(full)
    lw["norm_ab"] = lw["norm_ab"] + gather_odd[-1][0, 0]

    def odd_weights(h1):
        landed = _to_chips_wait(gather_odd, h1, per_chip_slot=False, name="gather_odd_wait")
        w = {}
        for name, land, own in zip(late, landed, late_shards):
            w[name] = _join_shards(lax.dynamic_update_index_in_dim(land, own, chip, 0), _SHARD_AXIS[name])
        for name in ("a_log", "dt_bias", "o_norm", "final_norm"):
            w[name] = shard_of("", name)
        return _layout_odd(w)

    def chip_partials(names, grads, tag):
        slots = [_split_shards(grads[n], _SHARD_AXIS[n]) for n in names]
        partial = _core_swap_partial(slots, name="core_swap_partial_" + tag)
        return [_core_sum(s, p, core, name="core_sum_" + n) for n, s, p in zip(names, slots, partial)]

    groups = {"odd": big_odd, "out_ab": ("w_out_ab",), "in_ab": ("w_in_ab", "w_q_b", "w_kv_b")}
    sent = {}

    def on_grads(tag, g):
        part = chip_partials(groups[tag], _unlayout_grads(g, groups[tag]), tag)
        sent[tag] = (part, _to_chips_start(part, per_chip_slot=True, name="exchange_" + tag + "_start"))
        return sent[tag][1][-1][0, 0]

    loss_tile, dx, g = _local_step(x[0], positions.reshape(t, 1), loss_target[0], lw, odd_weights, on_grads)
    small_all = _chip_exchange([], _pack_small(g, loss_tile))[-1]
    halves = {}
    for tag, names in groups.items():
        part, started = sent[tag]
        landed = _to_chips_wait(started, small_all, per_chip_slot=True, name="exchange_" + tag + "_wait")
        for n, l, p in zip(names, landed, part):
            l = lax.dynamic_update_index_in_dim(l, lax.dynamic_index_in_dim(p, chip, 0, keepdims=False), chip, 0)
            halves[n] = _chip_sum(l, core, name="chip_sum_" + n)
    gbig = dict(zip(big, _core_swap_sum([halves[n] for n in big])))

    res = {}
    for name in big:
        res["grad", name] = gbig[name]
        out = _adamw_rows(gbig[name], shard_of("", name), shard_of("m_", name), shard_of("v_", name), name="adamw_" + name)
        res["delta", name], res["m", name], res["v", name] = out
    out = _small_update(small_all, [shard_of("", n) for n in _SMALL_NAMES], [shard_of("m_", n) for n in _SMALL_NAMES],
                        [shard_of("v_", n) for n in _SMALL_NAMES])
    for i, name in enumerate(_SMALL_NAMES):
        res["grad", name], res["delta", name], res["m", name], res["v", name] = out[4 * i:4 * i + 4]
    res = {k: a.reshape(given[k[1]].shape) for k, a in res.items()}
    loss = out[-1][0, 0]
    outs = [loss, dx.reshape(x.shape)]
    for key in ("grad", "delta", "m", "v"):
        outs += [res[key, n] for n in _ALL_NAMES]
    return tuple(outs)
```

```python
import functools

import jax
import jax.numpy as jnp
from jax import lax
from jax.experimental import pallas as pl
from jax.experimental.pallas import tpu as pltpu

F32 = jnp.float32
BF16 = jnp.bfloat16
HI = lax.Precision.HIGHEST
MESH = pl.DeviceIdType.MESH

RMS_EPS = 1e-6
D_MODEL = 1024
MLA_HEADS = 8
MLA_Q_RANK = 256
MLA_KV_RANK = 128
MLA_NOPE = 64
MLA_ROPE = 32
MLA_V = 64
ROPE_THETA = 10000.0
POOL_WINDOWS = (2, 4, 8, 16)
POOL_GROUP = 128
POOL_WIDTH = 512
POOL_HALO = 16
GDN_HEADS = 8
GDN_DK = 128
CONV_WIDTH = 4
CONV_HALO = 8
CHUNK = 64
IN_AB_PAD = 2048
IN_C_PAD = 4224
ATT_SCALE = (MLA_NOPE + MLA_ROPE) ** -0.5

ADAM_LR = 0.001
ADAM_B1 = 0.9
ADAM_B2 = 0.999
ADAM_EPS = 1e-08
ADAM_WD = 0.01
ADAM_STEP = 10

LANES = 128
VMEM_LIMIT = 56 * 1024 * 1024

ROW_TILE = 256
ATT_TILE = 1024
GDN_TILE = 256
MM_TILE = (1024, 1408, 2048)

NN = (((1,), (0,)), ((), ()))
NT = (((1,), (1,)), ((), ()))
TN = (((0,), (0,)), ((), ()))


def _dot(a, b, dims=NN, prec=None):
    return lax.dot_general(a, b, dims, precision=prec, preferred_element_type=F32)


def _tile(n, pref):
    if n <= pref:
        return n
    step = LANES if pref >= LANES else 8
    for t in range(pref - pref % step, 0, -step):
        if n % t == 0:
            return t
    return n


def _params(sem):
    return pltpu.CompilerParams(dimension_semantics=sem, vmem_limit_bytes=VMEM_LIMIT)


def _sigmoid(x):
    return 0.5 * jnp.tanh(0.5 * x) + 0.5


def _softplus(x):
    return jnp.maximum(x, 0.0) + jnp.log(1.0 + jnp.exp(-jnp.abs(x)))


def _matmul(a, b, mode, *, name, out_dtype=F32, add=None):
    if mode == "nn":
        (m, k), (k2, n) = a.shape, b.shape
    elif mode == "nt":
        (m, k), (n, k2) = a.shape, b.shape
    else:
        (k, m), (k2, n) = a.shape, b.shape
    assert k == k2, (a.shape, b.shape, mode)
    tm, tn, tk = _tile(m, MM_TILE[0]), _tile(n, MM_TILE[1]), _tile(k, MM_TILE[2])
    nk = k // tk
    if mode == "tn":
        a_spec = pl.BlockSpec((tk, tm), lambda i, j, kk: (kk, i))
    else:
        a_spec = pl.BlockSpec((tm, tk), lambda i, j, kk: (i, kk))
    if mode == "nt":
        b_spec = pl.BlockSpec((tn, tk), lambda i, j, kk: (j, kk))
    else:
        b_spec = pl.BlockSpec((tk, tn), lambda i, j, kk: (kk, j))
    o_spec = pl.BlockSpec((tm, tn), lambda i, j, kk: (i, j))
    dims = {"nn": NN, "nt": NT, "tn": TN}[mode]
    has_add = add is not None

    def body(*refs):
        a_ref, b_ref = refs[0], refs[1]
        add_ref = refs[2] if has_add else None
        o_ref = refs[3] if has_add else refs[2]

        def finish(o):
            if has_add:
                o = o + add_ref[...]
            o_ref[...] = o.astype(out_dtype)

        if nk == 1:
            finish(_dot(a_ref[...], b_ref[...], dims))
            return
        acc = refs[-1]
        kk = pl.program_id(2)

        @pl.when(kk == 0)
        def _():
            acc[...] = jnp.zeros_like(acc)

        acc[...] += _dot(a_ref[...], b_ref[...], dims)

        @pl.when(kk == nk - 1)
        def _():
            finish(acc[...])

    in_specs = [a_spec, b_spec] + ([o_spec] if has_add else [])
    args = (a, b) + ((add,) if has_add else ())
    return pl.pallas_call(
        body, name=name, grid=(m // tm, n // tn, nk), in_specs=in_specs, out_specs=o_spec,
        out_shape=jax.ShapeDtypeStruct((m, n), out_dtype),
        scratch_shapes=[pltpu.VMEM((tm, tn), F32)] if nk > 1 else [],
        compiler_params=_params(("parallel", "parallel", "arbitrary")),
    )(*args)


def _rms_fwd(h, g, *, name):
    t, d = h.shape
    tm = _tile(t, 2 * ROW_TILE)

    def body(h_ref, g_ref, o_ref):
        x = h_ref[...]
        r = lax.rsqrt(jnp.mean(x * x, axis=-1, keepdims=True) + RMS_EPS)
        o_ref[...] = (x * r * g_ref[...]).astype(BF16)

    return pl.pallas_call(
        body, name=name, grid=(t // tm,),
        in_specs=[pl.BlockSpec((tm, d), lambda i: (i, 0)), pl.BlockSpec((1, d), lambda i: (0, 0))],
        out_specs=pl.BlockSpec((tm, d), lambda i: (i, 0)),
        out_shape=jax.ShapeDtypeStruct((t, d), BF16), compiler_params=_params(("parallel",)),
    )(h, g)


def _matmul_rms_bwd(dproj, w, h, g, dres, *, name, with_bf16):
    t, k = dproj.shape
    d = w.shape[0]
    tm, tk = _tile(t, 2 * ROW_TILE), _tile(k, MM_TILE[2])
    nk = k // tk

    def body(dp_ref, w_ref, h_ref, g_ref, dres_ref, *rest):
        i, kk = pl.program_id(0), pl.program_id(1)
        dh_ref, dg_ref = rest[0], rest[2 if with_bf16 else 1]

        def finish(dyv):
            x = h_ref[...]
            r = lax.rsqrt(jnp.mean(x * x, axis=-1, keepdims=True) + RMS_EPS)
            xh = x * r
            dxh = dyv * g_ref[...]
            dh = dres_ref[...] + r * (dxh - xh * jnp.mean(dxh * xh, axis=-1, keepdims=True))
            dh_ref[...] = dh
            if with_bf16:
                rest[1][...] = dh.astype(BF16)

            @pl.when(i == 0)
            def _():
                dg_ref[...] = jnp.zeros_like(dg_ref)

            dg_ref[...] += jnp.sum(dyv * xh, axis=0, keepdims=True)

        if nk == 1:
            finish(_dot(dp_ref[...], w_ref[...], NT))
            return
        acc = rest[-1]

        @pl.when(kk == 0)
        def _():
            acc[...] = jnp.zeros_like(acc)

        acc[...] += _dot(dp_ref[...], w_ref[...], NT)

        @pl.when(kk == nk - 1)
        def _():
            finish(acc[...])

    row = pl.BlockSpec((tm, d), lambda i, kk: (i, 0))
    vec = pl.BlockSpec((1, d), lambda i, kk: (0, 0))
    out_shape = [jax.ShapeDtypeStruct((t, d), F32)] + ([jax.ShapeDtypeStruct((t, d), BF16)] if with_bf16 else [])
    out_specs = [row] * len(out_shape) + [vec]
    out_shape.append(jax.ShapeDtypeStruct((1, d), F32))
    return pl.pallas_call(
        body, name=name, grid=(t // tm, nk),
        in_specs=[pl.BlockSpec((tm, tk), lambda i, kk: (i, kk)), pl.BlockSpec((d, tk), lambda i, kk: (0, kk)), row, vec, row],
        out_specs=out_specs, out_shape=out_shape, scratch_shapes=[pltpu.VMEM((tm, d), F32)] if nk > 1 else [],
        compiler_params=_params(("arbitrary", "arbitrary")),
    )(dproj, w, h, g, dres)


def _rope_partner(x):
    lane = lax.broadcasted_iota(jnp.int32, x.shape, 1)
    swapped = jnp.where(lane < MLA_NOPE + MLA_ROPE // 2, pltpu.roll(x, LANES - 16, 1), pltpu.roll(x, 16, 1))
    return jnp.where((lane >= MLA_NOPE) & (lane < MLA_NOPE + MLA_ROPE), swapped, 0.0)


def _pool_counts(row0, tm, w):
    t_idx = row0 + lax.broadcasted_iota(jnp.int32, (tm, POOL_GROUP), 0)
    return jnp.minimum(t_idx + 1, w).astype(F32)


def _ab_prep(proj, pos, inv_freq, q_a_norm, kv_a_norm, wq, wk, wv, wpool, *, name):
    t = proj.shape[0]
    tm = _tile(t, ROW_TILE)
    hb = tm // POOL_HALO

    def body(p_ref, halo_ref, pos_ref, inv_ref, qg_ref, kg_ref, wq_ref, wk_ref, wv_ref, wp_ref,
             q_ref, k_ref, v_ref, yb_ref, qn_ref, kvn_ref, d_ref, cos_ref, sin_ref, ext):
        i = pl.program_id(0)
        ql = p_ref[:, 0:MLA_Q_RANK]
        r = lax.rsqrt(jnp.mean(ql * ql, axis=-1, keepdims=True) + RMS_EPS)
        qn = (ql * r * qg_ref[...]).astype(BF16)
        qn_ref[...] = qn
        kl = p_ref[:, MLA_Q_RANK:MLA_Q_RANK + MLA_KV_RANK]
        r = lax.rsqrt(jnp.mean(kl * kl, axis=-1, keepdims=True) + RMS_EPS)
        kvn = (kl * r * kg_ref[...]).astype(BF16)
        kvn_ref[...] = kvn
        ang = pos_ref[...].astype(F32) * inv_ref[...]
        lane = lax.broadcasted_iota(jnp.int32, (tm, LANES), 1)
        in_rope = (lane >= MLA_NOPE) & (lane < MLA_NOPE + MLA_ROPE)
        cos_t = jnp.where(in_rope, jnp.cos(ang), 1.0)
        sin_t = jnp.where(in_rope, jnp.sin(ang), 0.0)
        sin_t = jnp.where(lane < MLA_NOPE + MLA_ROPE // 2, -sin_t, sin_t)
        cos_ref[...] = cos_t
        sin_ref[...] = sin_t
        kr = p_ref[:, 384:512]
        kr = kr * cos_t + _rope_partner(kr) * sin_t
        qraw = _dot(qn, wq_ref[...])
        kvk = _dot(kvn, wk_ref[...])
        for h in range(MLA_HEADS):
            sl = slice(h * LANES, (h + 1) * LANES)
            qh = qraw[:, sl]
            q_ref[:, sl] = ((qh * cos_t + _rope_partner(qh) * sin_t) * ATT_SCALE).astype(BF16)
            k_ref[:, sl] = (kvk[:, sl] + kr).astype(BF16)
        v_ref[...] = _dot(kvn, wv_ref[...]).astype(BF16)
        xp = p_ref[:, 512:1024]
        ext[0:POOL_HALO, :] = jnp.where(i > 0, halo_ref[...], 0.0)
        ext[POOL_HALO:POOL_HALO + tm, :] = xp
        for g, w in enumerate(POOL_WINDOWS):
            lo = g * POOL_GROUP
            acc = ext[POOL_HALO:POOL_HALO + tm, lo:lo + POOL_GROUP]
            for s in range(1, w):
                acc = acc + ext[POOL_HALO - s:POOL_HALO - s + tm, lo:lo + POOL_GROUP]
            cnt = _pool_counts(i * tm, tm, w)
            d_ref[:, lo:lo + POOL_GROUP] = (acc / cnt - xp[:, lo:lo + POOL_GROUP]).astype(BF16)
        yb_ref[...] = _dot(d_ref[...], wp_ref[...])

    row = lambda w: pl.BlockSpec((tm, w), lambda i: (i, 0))
    vec = lambda w: pl.BlockSpec((1, w), lambda i: (0, 0))
    whole = lambda a: pl.BlockSpec(a.shape, lambda i: (0, 0))
    return pl.pallas_call(
        body, name=name, grid=(t // tm,),
        in_specs=[row(1024), pl.BlockSpec((POOL_HALO, POOL_WIDTH), lambda i: (jnp.maximum(i * hb - 1, 0), 1)),
                  pl.BlockSpec((tm, 1), lambda i: (i, 0)), vec(LANES), vec(MLA_Q_RANK), vec(MLA_KV_RANK),
                  whole(wq), whole(wk), whole(wv), whole(wpool)],
        out_specs=[row(1024), row(1024), row(512), row(512), row(MLA_Q_RANK), row(MLA_KV_RANK), row(POOL_WIDTH),
                   row(LANES), row(LANES)],
        out_shape=[jax.ShapeDtypeStruct((t, 1024), BF16), jax.ShapeDtypeStruct((t, 1024), BF16),
                   jax.ShapeDtypeStruct((t, 512), BF16), jax.ShapeDtypeStruct((t, 512), F32),
                   jax.ShapeDtypeStruct((t, MLA_Q_RANK), BF16), jax.ShapeDtypeStruct((t, MLA_KV_RANK), BF16),
                   jax.ShapeDtypeStruct((t, POOL_WIDTH), BF16), jax.ShapeDtypeStruct((t, LANES), F32),
                   jax.ShapeDtypeStruct((t, LANES), F32)],
        scratch_shapes=[pltpu.VMEM((tm + POOL_HALO, POOL_WIDTH), F32)],
        compiler_params=_params(("parallel",)),
    )(proj, proj, pos, inv_freq, q_a_norm, kv_a_norm, wq, wk, wv, wpool)


def _ab_prep_bwd(proj, q_a_norm, kv_a_norm, dq, dk, dv, cos_t, sin_t, dyb, dz, wq, wk, wv, wpool, *, name):
    t = proj.shape[0]
    tm = _tile(t, ROW_TILE)
    hb = tm // POOL_HALO
    last_halo = t // POOL_HALO - 1
    nt = t // tm

    def body(p_ref, qg_ref, kg_ref, dq_ref, dk_ref, dv_ref, c_ref, s_ref, dyb_ref, dybn_ref, dz_ref,
             wq_ref, wk_ref, wv_ref, wp_ref, dp_ref, dqr_ref, dkb_ref, dqg_ref, dkg_ref, ext):
        i = pl.program_id(0)

        @pl.when(i == 0)
        def _():
            dqg_ref[...] = jnp.zeros_like(dqg_ref)
            dkg_ref[...] = jnp.zeros_like(dkg_ref)

        def norm_bwd(x, g, dy, dg_ref):
            r = lax.rsqrt(jnp.mean(x * x, axis=-1, keepdims=True) + RMS_EPS)
            xh = x * r
            dxh = dy * g
            dg_ref[...] += jnp.sum(dy * xh, axis=0, keepdims=True)
            return r * (dxh - xh * jnp.mean(dxh * xh, axis=-1, keepdims=True))

        c, s = c_ref[...], s_ref[...]
        lane = lax.broadcasted_iota(jnp.int32, (tm, LANES), 1)
        in_rope = (lane >= MLA_NOPE) & (lane < MLA_NOPE + MLA_ROPE)
        dkr = jnp.zeros((tm, LANES), F32)
        for h in range(MLA_HEADS):
            sl = slice(h * LANES, (h + 1) * LANES)
            g = dq_ref[:, sl]
            dqr_ref[:, sl] = ((g * c + _rope_partner(g * s)) * ATT_SCALE).astype(BF16)
            gk = dk_ref[:, sl]
            dkb_ref[:, sl] = gk.astype(BF16)
            dkr = dkr + jnp.where(in_rope, gk, 0.0)
        dkr = dkr * c + _rope_partner(dkr * s)
        dqn = _dot(dqr_ref[...], wq_ref[...], NT)
        dkvn = _dot(dkb_ref[...], wk_ref[...], NT) + _dot(dv_ref[...], wv_ref[...], NT)
        dql = norm_bwd(p_ref[:, 0:MLA_Q_RANK], qg_ref[...], dqn, dqg_ref)
        dp_ref[:, 0:MLA_Q_RANK] = dql.astype(BF16)
        dkl = norm_bwd(p_ref[:, MLA_Q_RANK:384], kg_ref[...], dkvn, dkg_ref)
        dp_ref[:, MLA_Q_RANK:384] = dkl.astype(BF16)
        dp_ref[:, 384:512] = dkr.astype(BF16)
        ddv = _dot(dyb_ref[...], wp_ref[...], NT)
        ddn = _dot(dybn_ref[...], wp_ref[...], NT)
        for g, w in enumerate(POOL_WINDOWS):
            lo = g * POOL_GROUP
            ext[0:tm, lo:lo + POOL_GROUP] = ddv[:, lo:lo + POOL_GROUP] / _pool_counts(i * tm, tm, w)
            nxt = ddn[:, lo:lo + POOL_GROUP] / _pool_counts((i + 1) * tm, POOL_HALO, w)
            ext[tm:tm + POOL_HALO, lo:lo + POOL_GROUP] = jnp.where(i < nt - 1, nxt, 0.0)
        for g, w in enumerate(POOL_WINDOWS):
            lo = g * POOL_GROUP
            acc = ext[0:tm, lo:lo + POOL_GROUP]
            for s in range(1, w):
                acc = acc + ext[s:s + tm, lo:lo + POOL_GROUP]
            dp_ref[:, 512 + lo:512 + lo + POOL_GROUP] = (acc - ddv[:, lo:lo + POOL_GROUP]).astype(BF16)
        dp_ref[:, 1024:2048] = dz_ref[...]

    row = lambda w: pl.BlockSpec((tm, w), lambda i: (i, 0))
    vec = lambda w: pl.BlockSpec((1, w), lambda i: (0, 0))
    whole = lambda a: pl.BlockSpec(a.shape, lambda i: (0, 0))
    return pl.pallas_call(
        body, name=name, grid=(nt,),
        in_specs=[row(1024), vec(MLA_Q_RANK), vec(MLA_KV_RANK), row(1024), row(1024), row(512), row(LANES), row(LANES),
                  row(POOL_WIDTH),
                  pl.BlockSpec((POOL_HALO, POOL_WIDTH), lambda i: (jnp.minimum((i + 1) * hb, last_halo), 0)),
                  row(1024), whole(wq), whole(wk), whole(wv), whole(wpool)],
        out_specs=[row(IN_AB_PAD), row(1024), row(1024), vec(MLA_Q_RANK), vec(MLA_KV_RANK)],
        out_shape=[jax.ShapeDtypeStruct((t, IN_AB_PAD), BF16), jax.ShapeDtypeStruct((t, 1024), BF16),
                   jax.ShapeDtypeStruct((t, 1024), BF16), jax.ShapeDtypeStruct((1, MLA_Q_RANK), F32),
                   jax.ShapeDtypeStruct((1, MLA_KV_RANK), F32)],
        scratch_shapes=[pltpu.VMEM((tm + POOL_HALO, POOL_WIDTH), F32)],
        compiler_params=_params(("arbitrary",)),
    )(proj, q_a_norm, kv_a_norm, dq, dk, dv, cos_t, sin_t, dyb, dyb, dz, wq, wk, wv, wpool)


def _gate_fwd(o, ybraw, proj, pool_scale, *, name):
    t = o.shape[0]
    tm = _tile(t, 2 * ROW_TILE)

    def body(o_ref, yb_ref, z_ref, ps_ref, y_ref):
        z = z_ref[...]
        sz = z * _sigmoid(z)
        y_ref[:, 0:512] = (o_ref[...] * sz[:, 0:512]).astype(BF16)
        y_ref[:, 512:1024] = (yb_ref[...] * ps_ref[...] * sz[:, 512:1024]).astype(BF16)

    row = lambda w: pl.BlockSpec((tm, w), lambda i: (i, 0))
    return pl.pallas_call(
        body, name=name, grid=(t // tm,),
        in_specs=[row(512), row(512), pl.BlockSpec((tm, 1024), lambda i: (i, 1)), pl.BlockSpec((1, 512), lambda i: (0, 0))],
        out_specs=row(1024), out_shape=jax.ShapeDtypeStruct((t, 1024), BF16), compiler_params=_params(("parallel",)),
    )(o, ybraw, proj, pool_scale)


def _gate_bwd(dy, o, ybraw, proj, pool_scale, *, name):
    t = o.shape[0]
    tm = _tile(t, ROW_TILE)

    def body(dy_ref, o_ref, yb_ref, z_ref, ps_ref, do_ref, dl_ref, dyb_ref, dz_ref, dps_ref):
        i = pl.program_id(0)
        z = z_ref[...]
        sg = _sigmoid(z)
        sz = z * sg
        dsz = sg * (1.0 + z * (1.0 - sg))
        dyv = dy_ref[...]
        dcat = dyv * sz
        ov = o_ref[...]
        ybs = yb_ref[...] * ps_ref[...]
        dz_ref[:, 0:512] = (dyv[:, 0:512] * ov * dsz[:, 0:512]).astype(BF16)
        dz_ref[:, 512:1024] = (dyv[:, 512:1024] * ybs * dsz[:, 512:1024]).astype(BF16)
        do = dcat[:, 0:512]
        do_ref[...] = do.astype(BF16)
        r_i = lax.broadcasted_iota(jnp.int32, (512, 512), 0) // MLA_V
        c_i = lax.broadcasted_iota(jnp.int32, (512, 512), 1) // MLA_V
        dl_ref[...] = _dot(do * ov, (r_i == c_i).astype(F32), NN, HI)
        dyb_ref[...] = (dcat[:, 512:1024] * ps_ref[...]).astype(BF16)

        @pl.when(i == 0)
        def _():
            dps_ref[...] = jnp.zeros_like(dps_ref)

        dps_ref[...] += jnp.sum(dcat[:, 512:1024] * yb_ref[...], axis=0, keepdims=True)

    row = lambda w: pl.BlockSpec((tm, w), lambda i: (i, 0))
    vec = pl.BlockSpec((1, 512), lambda i: (0, 0))
    return pl.pallas_call(
        body, name=name, grid=(t // tm,),
        in_specs=[row(1024), row(512), row(512), pl.BlockSpec((tm, 1024), lambda i: (i, 1)), vec],
        out_specs=[row(512), row(512), row(512), row(1024), vec],
        out_shape=[jax.ShapeDtypeStruct((t, 512), BF16), jax.ShapeDtypeStruct((t, 512), F32),
                   jax.ShapeDtypeStruct((t, 512), BF16), jax.ShapeDtypeStruct((t, 1024), BF16),
                   jax.ShapeDtypeStruct((1, 512), F32)],
        compiler_params=_params(("arbitrary",)),
    )(dy, o, ybraw, proj, pool_scale)


ATT_HP_FWD = 4
ATT_HP_BWD = 2


def _diag_mask(tq):
    return lax.broadcasted_iota(jnp.int32, (tq, tq), 1) <= lax.broadcasted_iota(jnp.int32, (tq, tq), 0)


def _block_schedule(nq, key_major):
    if key_major:
        pairs = [(qi, ki) for ki in range(nq) for qi in range(ki, nq)]
    else:
        pairs = [(qi, ki) for qi in range(nq) for ki in range(qi + 1)]
    return jnp.asarray([p[0] for p in pairs], jnp.int32), jnp.asarray([p[1] for p in pairs], jnp.int32)


def _attn_fwd(q, k, v, *, name):
    t = q.shape[0]
    tq = _tile(t, ATT_TILE)
    nq = t // tq
    hp = ATT_HP_FWD
    qi_tab, ki_tab = _block_schedule(nq, key_major=False)

    def body(qi_ref, ki_ref, q_ref, k_ref, v_ref, o_ref, lse_ref, m_sc, l_sc, acc_sc):
        step = pl.program_id(1)
        qi, ki = qi_ref[step], ki_ref[step]

        @pl.when(ki == 0)
        def _():
            m_sc[...] = jnp.full_like(m_sc, -jnp.inf)
            l_sc[...] = jnp.zeros_like(l_sc)
            acc_sc[...] = jnp.zeros_like(acc_sc)

        def block(on_diagonal):
            scores = []
            for h in range(hp):
                sl = slice(h * LANES, (h + 1) * LANES)
                scores.append(_dot(q_ref[:, sl], k_ref[:, sl], NT))
            if on_diagonal:
                mask = _diag_mask(tq)
                scores = [jnp.where(mask, s, -jnp.inf) for s in scores]
            for h, s in enumerate(scores):
                vv = v_ref[:, (h // 2) * LANES:(h // 2 + 1) * LANES]
                m_prev = m_sc[h]
                m_new = jnp.maximum(m_prev, jnp.max(s, axis=-1, keepdims=True))
                alpha = jnp.exp(m_prev - m_new)
                p = jnp.exp(s - m_new[:, 0:1])
                l_sc[h] = alpha * l_sc[h] + jnp.sum(p, axis=-1, keepdims=True)
                acc_sc[h] = alpha * acc_sc[h] + _dot(p.astype(BF16), vv)
                m_sc[h] = m_new

        pl.when(ki < qi)(functools.partial(block, False))
        pl.when(ki == qi)(functools.partial(block, True))

        @pl.when(ki == qi)
        def _():
            first = lax.broadcasted_iota(jnp.int32, (tq, LANES), 1) < MLA_V
            for pr in range(hp // 2):
                a, b = 2 * pr, 2 * pr + 1
                sl = slice(pr * LANES, (pr + 1) * LANES)
                o_ref[:, sl] = jnp.where(first, acc_sc[a] / l_sc[a], acc_sc[b] / l_sc[b])
                lse_ref[:, sl] = jnp.where(first, m_sc[a] + jnp.log(l_sc[a]), m_sc[b] + jnp.log(l_sc[b]))

    grid_spec = pltpu.PrefetchScalarGridSpec(
        num_scalar_prefetch=2, grid=(MLA_HEADS // hp, qi_tab.shape[0]),
        in_specs=[pl.BlockSpec((tq, hp * LANES), lambda g, s, qt, kt: (qt[s], g)),
                  pl.BlockSpec((tq, hp * LANES), lambda g, s, qt, kt: (kt[s], g)),
                  pl.BlockSpec((tq, hp * MLA_V), lambda g, s, qt, kt: (kt[s], g))],
        out_specs=[pl.BlockSpec((tq, hp * MLA_V), lambda g, s, qt, kt: (qt[s], g)),
                   pl.BlockSpec((tq, hp * MLA_V), lambda g, s, qt, kt: (qt[s], g))],
        scratch_shapes=[pltpu.VMEM((hp, tq, LANES), F32)] * 3,
    )
    return pl.pallas_call(
        body, name=name, grid_spec=grid_spec,
        out_shape=[jax.ShapeDtypeStruct((t, 512), F32), jax.ShapeDtypeStruct((t, 512), F32)],
        compiler_params=_params(("parallel", "arbitrary")),
    )(qi_tab, ki_tab, q, k, v)


def _attn_bwd(q, k, v, do, lse, delta, *, name):
    t = q.shape[0]
    tq = _tile(t, ATT_TILE)
    nq = t // tq
    hp = ATT_HP_BWD
    qi_tab, ki_tab = _block_schedule(nq, key_major=True)

    def body(qi_ref, ki_ref, q_ref, k_ref, v_ref, do_ref, lse_ref, dl_ref, dq_ref, dk_ref, dv_ref, dk_sc, dv_sc):
        step = pl.program_id(1)
        qi, ki = qi_ref[step], ki_ref[step]

        @pl.when(step == 0)
        def _():
            dq_ref[...] = jnp.zeros_like(dq_ref)

        @pl.when(qi == ki)
        def _():
            dk_sc[...] = jnp.zeros_like(dk_sc)
            dv_sc[...] = jnp.zeros_like(dv_sc)

        def block(on_diagonal):
            lane = lax.broadcasted_iota(jnp.int32, (tq, LANES), 1)
            rows = pl.ds(pl.multiple_of(qi * tq, tq), tq)
            heads = [slice(h * LANES, (h + 1) * LANES) for h in range(hp)]
            scores = [_dot(q_ref[:, sl], k_ref[:, sl], NT) for sl in heads]
            dps = []
            for h in range(hp):
                dov = do_ref[:, (h // 2) * LANES:(h // 2 + 1) * LANES]
                mine = (lane < MLA_V) if h % 2 == 0 else (lane >= MLA_V)
                dps.append(_dot(jnp.where(mine, dov, jnp.zeros_like(dov)), v_ref[:, (h // 2) * LANES:(h // 2 + 1) * LANES], NT))
            mask = _diag_mask(tq) if on_diagonal else None
            for h, sl in enumerate(heads):
                col = (h // 2) * LANES + (h % 2) * MLA_V
                p = jnp.exp(scores[h] - lse_ref[:, col:col + 1])
                if on_diagonal:
                    p = jnp.where(mask, p, 0.0)
                ds = (p * (dps[h] - dl_ref[:, col:col + 1])).astype(BF16)
                dv_sc[h] += _dot(p.astype(BF16), do_ref[:, (h // 2) * LANES:(h // 2 + 1) * LANES], TN)
                dk_sc[h] += _dot(ds, q_ref[:, sl], TN)
                dq_ref[rows, sl] += _dot(ds, k_ref[:, sl], NN)

        pl.when(qi > ki)(functools.partial(block, False))
        pl.when(qi == ki)(functools.partial(block, True))

        @pl.when(qi == nq - 1)
        def _():
            first = lax.broadcasted_iota(jnp.int32, (tq, LANES), 1) < MLA_V
            for h in range(hp):
                dk_ref[:, h * LANES:(h + 1) * LANES] = dk_sc[h]
            for pr in range(hp // 2):
                dv_ref[:, pr * LANES:(pr + 1) * LANES] = jnp.where(first, dv_sc[2 * pr], dv_sc[2 * pr + 1]).astype(BF16)

    qrow = lambda w: pl.BlockSpec((tq, w), lambda g, s, qt, kt: (qt[s], g))
    krow = lambda w: pl.BlockSpec((tq, w), lambda g, s, qt, kt: (kt[s], g))
    grid_spec = pltpu.PrefetchScalarGridSpec(
        num_scalar_prefetch=2, grid=(MLA_HEADS // hp, qi_tab.shape[0]),
        in_specs=[qrow(hp * LANES), krow(hp * LANES), krow(hp * MLA_V), qrow(hp * MLA_V), qrow(hp * MLA_V), qrow(hp * MLA_V)],
        out_specs=[pl.BlockSpec((t, hp * LANES), lambda g, s, qt, kt: (0, g)), krow(hp * LANES), krow(hp * MLA_V)],
        scratch_shapes=[pltpu.VMEM((hp, tq, LANES), F32), pltpu.VMEM((hp, tq, LANES), F32)],
    )
    return pl.pallas_call(
        body, name=name, grid_spec=grid_spec,
        out_shape=[jax.ShapeDtypeStruct((t, 1024), F32), jax.ShapeDtypeStruct((t, 1024), F32),
                   jax.ShapeDtypeStruct((t, 512), BF16)],
        compiler_params=_params(("parallel", "arbitrary")),
    )(qi_tab, ki_tab, q, k, v, do, lse, delta)


def _conv_rows(ext, tm, w_ref, sec):
    c0 = sec * 1024
    y = ext[CONV_HALO - 3:CONV_HALO - 3 + tm, c0:c0 + 1024] * w_ref[0:1, c0:c0 + 1024]
    for j in range(1, CONV_WIDTH):
        y = y + ext[CONV_HALO - 3 + j:CONV_HALO - 3 + j + tm, c0:c0 + 1024] * w_ref[j:j + 1, c0:c0 + 1024]
    return y


def _c_prep(proj_c, conv_w, a_log, dt_bias, *, name):
    t = proj_c.shape[0]
    tm = _tile(t, ROW_TILE)
    hb = tm // CONV_HALO

    def body(p_ref, halo_ref, ab_ref, w_ref, al_ref, dtb_ref, q_ref, k_ref, v_ref, g_ref, b_ref, gt_ref, ext):
        i = pl.program_id(0)
        ext[0:CONV_HALO, :] = jnp.where(i > 0, halo_ref[...], 0.0)
        ext[CONV_HALO:CONV_HALO + tm, :] = p_ref[...]
        for sec, o_ref in enumerate((q_ref, k_ref, v_ref)):
            y = _conv_rows(ext, tm, w_ref, sec)
            y = y * _sigmoid(y)
            if sec == 2:
                o_ref[...] = y
                continue
            scale = GDN_DK ** -0.5 if sec == 0 else 1.0
            for h in range(GDN_HEADS):
                sl = slice(h * LANES, (h + 1) * LANES)
                blk = y[:, sl]
                r = lax.rsqrt(jnp.sum(blk * blk, axis=-1, keepdims=True) + RMS_EPS)
                o_ref[:, sl] = blk * (r * scale)
        ab = ab_ref[...]
        g = -jnp.exp(al_ref[...]) * _softplus(ab + dtb_ref[...])
        beta = _sigmoid(ab)
        ri = lax.broadcasted_iota(jnp.int32, (tm, tm), 0)
        ci = lax.broadcasted_iota(jnp.int32, (tm, tm), 1)
        lower = ((ri // CHUNK) == (ci // CHUNK)) & (ri >= ci)
        gc = _dot(lower.astype(F32), g, NN, HI)
        eye = lax.broadcasted_iota(jnp.int32, (LANES, LANES), 0) == lax.broadcasted_iota(jnp.int32, (LANES, LANES), 1)
        gt_ref[...] = _dot(eye.astype(F32), gc, NT, HI)[0:GDN_HEADS, :]
        for h in range(GDN_HEADS):
            sl = slice(h * LANES, (h + 1) * LANES)
            g_ref[:, sl] = jnp.broadcast_to(gc[:, h:h + 1], (tm, LANES))
            b_ref[:, sl] = jnp.broadcast_to(beta[:, GDN_HEADS + h:GDN_HEADS + h + 1], (tm, LANES))

    row = lambda w: pl.BlockSpec((tm, w), lambda i: (i, 0))
    vec = lambda r, w: pl.BlockSpec((r, w), lambda i: (0, 0))
    out = jax.ShapeDtypeStruct((t, 1024), F32)
    return pl.pallas_call(
        body, name=name, grid=(t // tm,),
        in_specs=[row(3072), pl.BlockSpec((CONV_HALO, 3072), lambda i: (jnp.maximum(i * hb - 1, 0), 0)),
                  pl.BlockSpec((tm, LANES), lambda i: (i, 32)), vec(CONV_WIDTH, 3072), vec(1, LANES), vec(1, LANES)],
        out_specs=[row(1024)] * 5 + [pl.BlockSpec((GDN_HEADS, tm), lambda i: (0, i))],
        out_shape=[out] * 5 + [jax.ShapeDtypeStruct((GDN_HEADS, t), F32)],
        scratch_shapes=[pltpu.VMEM((tm + CONV_HALO, 3072), F32)],
        compiler_params=_params(("parallel",)),
    )(proj_c, proj_c, proj_c, conv_w, a_log, dt_bias)


def _c_prep_bwd(proj_c, conv_w, a_log, dt_bias, dq, dk, dv, dgb, dbb, dz, *, name):
    t = proj_c.shape[0]
    tm = _tile(t, ROW_TILE // 2)
    hb = tm // CONV_HALO
    nt = t // tm
    rev = lambda i: nt - 1 - i

    def body(p_ref, halo_ref, ab_ref, w_ref, al_ref, dtb_ref, dq_ref, dk_ref, dv_ref, dg_ref, db_ref, dz_ref,
             dp_ref, dw_ref, dal_ref, ddt_ref, ext, dyext, carry, taps):
        step = pl.program_id(0)
        i = rev(step)

        @pl.when(step == 0)
        def _():
            dw_ref[...] = jnp.zeros_like(dw_ref)
            dal_ref[...] = jnp.zeros_like(dal_ref)
            ddt_ref[...] = jnp.zeros_like(ddt_ref)
            carry[...] = jnp.zeros_like(carry)

        ext[0:CONV_HALO, :] = jnp.where(i > 0, halo_ref[...], 0.0)
        ext[CONV_HALO:CONV_HALO + tm, :] = p_ref[...]
        for sec, g_ref in enumerate((dq_ref, dk_ref, dv_ref)):
            c0 = sec * 1024
            for j in range(CONV_WIDTH):
                taps[j] = ext[CONV_HALO - 3 + j:CONV_HALO - 3 + j + tm, c0:c0 + 1024]
            y = taps[0] * w_ref[0:1, c0:c0 + 1024]
            for j in range(1, CONV_WIDTH):
                y = y + taps[j] * w_ref[j:j + 1, c0:c0 + 1024]
            sg = _sigmoid(y)
            act = y * sg
            if sec == 2:
                dact = g_ref[...]
            else:
                scale = GDN_DK ** -0.5 if sec == 0 else 1.0
                parts = []
                for h in range(GDN_HEADS):
                    sl = slice(h * LANES, (h + 1) * LANES)
                    blk = act[:, sl]
                    r = lax.rsqrt(jnp.sum(blk * blk, axis=-1, keepdims=True) + RMS_EPS)
                    n = blk * r
                    dn = g_ref[:, sl] * scale
                    parts.append(r * (dn - n * jnp.sum(dn * n, axis=-1, keepdims=True)))
                dact = jnp.concatenate(parts, axis=-1)
            dy = dact * (sg * (1.0 + y * (1.0 - sg)))
            dyext[0:tm, c0:c0 + 1024] = dy
            for j in range(CONV_WIDTH):
                dw_ref[j:j + 1, c0:c0 + 1024] += jnp.sum(dy * taps[j], axis=0, keepdims=True)
        dyext[tm:tm + CONV_HALO, :] = carry[...]
        carry[...] = dyext[0:CONV_HALO, :]
        for sec in range(3):
            c0 = sec * 1024
            dx = dyext[3:3 + tm, c0:c0 + 1024] * w_ref[0:1, c0:c0 + 1024]
            for j in range(1, CONV_WIDTH):
                dx = dx + dyext[3 - j:3 - j + tm, c0:c0 + 1024] * w_ref[j:j + 1, c0:c0 + 1024]
            dp_ref[:, c0:c0 + 1024] = dx.astype(BF16)
        dp_ref[:, 3072:4096] = dz_ref[...]
        lane = lax.broadcasted_iota(jnp.int32, (tm, LANES), 1)
        dg = jnp.zeros((tm, LANES), F32)
        dbeta = jnp.zeros((tm, LANES), F32)
        for h in range(GDN_HEADS):
            sl = slice(h * LANES, (h + 1) * LANES)
            dg = dg + jnp.where(lane == h, dg_ref[:, sl], 0.0)
            dbeta = dbeta + jnp.where(lane == GDN_HEADS + h, db_ref[:, sl], 0.0)
        ri = lax.broadcasted_iota(jnp.int32, (tm, tm), 0)
        ci = lax.broadcasted_iota(jnp.int32, (tm, tm), 1)
        upper = ((ri // CHUNK) == (ci // CHUNK)) & (ri <= ci)
        dg = _dot(upper.astype(F32), dg, NN, HI)
        pre = ab_ref[...] + dtb_ref[...]
        s = _sigmoid(pre)
        a_exp = jnp.exp(al_ref[...])
        dg_da = dg * (-a_exp * s)
        dp_ref[:, 4096:IN_C_PAD] = (dg_da + dbeta * s * (1.0 - s)).astype(BF16)
        dal_ref[...] += jnp.sum(dg * (-a_exp * _softplus(pre)), axis=0, keepdims=True)
        ddt_ref[...] += jnp.sum(dg_da, axis=0, keepdims=True)

    row = lambda w: pl.BlockSpec((tm, w), lambda s: (rev(s), 0))
    vec = lambda r, w: pl.BlockSpec((r, w), lambda s: (0, 0))
    return pl.pallas_call(
        body, name=name, grid=(nt,),
        in_specs=[row(3072), pl.BlockSpec((CONV_HALO, 3072), lambda s: (jnp.maximum(rev(s) * hb - 1, 0), 0)),
                  pl.BlockSpec((tm, LANES), lambda s: (rev(s), 32)), vec(CONV_WIDTH, 3072), vec(1, LANES), vec(1, LANES),
                  row(1024), row(1024), row(1024), row(1024), row(1024), row(1024)],
        out_specs=[row(IN_C_PAD), vec(CONV_WIDTH, 3072), vec(1, LANES), vec(1, LANES)],
        out_shape=[jax.ShapeDtypeStruct((t, IN_C_PAD), BF16), jax.ShapeDtypeStruct((CONV_WIDTH, 3072), F32),
                   jax.ShapeDtypeStruct((1, LANES), F32), jax.ShapeDtypeStruct((1, LANES), F32)],
        scratch_shapes=[pltpu.VMEM((tm + CONV_HALO, 3072), F32), pltpu.VMEM((tm + CONV_HALO, 3072), F32),
                        pltpu.VMEM((CONV_HALO, 3072), F32), pltpu.VMEM((CONV_WIDTH, tm, 1024), F32)],
        compiler_params=_params(("arbitrary",)),
    )(proj_c, proj_c, proj_c, conv_w, a_log, dt_bias, dq, dk, dv, dgb, dbb, dz)


def _o_gate_fwd(o, proj_c, o_norm, *, name):
    t = o.shape[0]
    tm = _tile(t, 2 * ROW_TILE)

    def body(o_ref, z_ref, g_ref, y_ref):
        for h in range(GDN_HEADS):
            sl = slice(h * LANES, (h + 1) * LANES)
            x = o_ref[:, sl]
            r = lax.rsqrt(jnp.mean(x * x, axis=-1, keepdims=True) + RMS_EPS)
            z = z_ref[:, sl]
            y_ref[:, sl] = (x * r * g_ref[...] * (z * _sigmoid(z))).astype(BF16)

    row = pl.BlockSpec((tm, 1024), lambda i: (i, 0))
    return pl.pallas_call(
        body, name=name, grid=(t // tm,),
        in_specs=[row, pl.BlockSpec((tm, 1024), lambda i: (i, 3)), pl.BlockSpec((1, LANES), lambda i: (0, 0))],
        out_specs=row, out_shape=jax.ShapeDtypeStruct((t, 1024), BF16), compiler_params=_params(("parallel",)),
    )(o, proj_c, o_norm)


def _o_gate_bwd(dy, o, proj_c, o_norm, *, name):
    t = o.shape[0]
    tm = _tile(t, 2 * ROW_TILE)

    def body(dy_ref, o_ref, z_ref, g_ref, do_ref, dz_ref, dg_ref):
        i = pl.program_id(0)

        @pl.when(i == 0)
        def _():
            dg_ref[...] = jnp.zeros_like(dg_ref)

        dg = jnp.zeros((1, LANES), F32)
        for h in range(GDN_HEADS):
            sl = slice(h * LANES, (h + 1) * LANES)
            x = o_ref[:, sl]
            r = lax.rsqrt(jnp.mean(x * x, axis=-1, keepdims=True) + RMS_EPS)
            xh = x * r
            z = z_ref[:, sl]
            sg = _sigmoid(z)
            dyv = dy_ref[:, sl]
            dn = dyv * (z * sg)
            dz_ref[:, sl] = (dyv * xh * g_ref[...] * (sg * (1.0 + z * (1.0 - sg)))).astype(BF16)
            dxh = dn * g_ref[...]
            do_ref[:, sl] = r * (dxh - xh * jnp.mean(dxh * xh, axis=-1, keepdims=True))
            dg = dg + jnp.sum(dn * xh, axis=0, keepdims=True)
        dg_ref[...] += dg

    row = pl.BlockSpec((tm, 1024), lambda i: (i, 0))
    vec = pl.BlockSpec((1, LANES), lambda i: (0, 0))
    return pl.pallas_call(
        body, name=name, grid=(t // tm,), in_specs=[row, row, pl.BlockSpec((tm, 1024), lambda i: (i, 3)), vec],
        out_specs=[row, row, vec],
        out_shape=[jax.ShapeDtypeStruct((t, 1024), F32), jax.ShapeDtypeStruct((t, 1024), BF16),
                   jax.ShapeDtypeStruct((1, LANES), F32)],
        compiler_params=_params(("arbitrary",)),
    )(dy, o, proj_c, o_norm)


PAIR = 2 * CHUNK
GDN_HP = 8


def _bdot(a, b, dims=NN):
    return _dot(a.astype(BF16), b.astype(BF16), dims)


def _each(f, *lists):
    return [f(*args) for args in zip(*lists)]


def _pair_common(q, k, v, gci, gcj, beta):
    ri = lax.broadcasted_iota(jnp.int32, (PAIR, PAIR), 0)
    ci = lax.broadcasted_iota(jnp.int32, (PAIR, PAIR), 1)
    same = (ri // CHUNK) == (ci // CHUNK)
    incl = same & (ri >= ci)
    strict = same & (ri > ci)
    eye = (ri == ci).astype(F32)
    first = lax.broadcasted_iota(jnp.int32, (PAIR, LANES), 0) < CHUNK
    gamma = _each(lambda gi, gj: jnp.where(incl, jnp.exp(jnp.minimum(gi - gj, 0.0)), 0.0), gci, gcj)
    kb = _each(jnp.multiply, k, beta)
    kk = _each(lambda a, b: _bdot(a, b, NT), kb, k)
    qk = _each(lambda a, b: _bdot(a, b, NT), q, k)
    m = _each(lambda x, g: jnp.where(strict, x * g, 0.0), kk, gamma)
    tm_ = _each(lambda x: eye - x, m)
    pw = _each(lambda x: _bdot(x, x), m)
    for it in range(5):
        tm_ = _each(lambda x, p: x + _bdot(x, p), tm_, pw)
        if it < 4:
            pw = _each(lambda p: _bdot(p, p), pw)
    eg = _each(jnp.exp, gci)
    vb = _each(jnp.multiply, v, beta)
    kbe = _each(jnp.multiply, kb, eg)
    uw = _each(lambda x, a, b: _bdot(x, jnp.concatenate([a, b], axis=1)), tm_, vb, kbe)
    attn = _each(lambda x, g: jnp.where(incl, x * g, 0.0), qk, gamma)
    gl_a = _each(lambda g: g[CHUNK - 1:CHUNK, :], gci)
    gl_b = _each(lambda g: g[PAIR - 1:PAIR, :], gci)
    ek = _each(lambda a, b, g: jnp.exp(jnp.where(first, a, b) - g), gl_a, gl_b, gci)
    return dict(incl=incl, strict=strict, gamma=gamma, kb=kb, m=m, tm=tm_, eg=eg, vb=vb, kbe=kbe,
                u=_each(lambda x: x[:, :LANES], uw), w=_each(lambda x: x[:, LANES:], uw), attn=attn,
                qd=_each(jnp.multiply, q, eg), ek=ek, kd=_each(jnp.multiply, k, ek),
                glast_a=_each(jnp.exp, gl_a), glast_b=_each(jnp.exp, gl_b))


def _gdn_specs(t, ts, order):
    nc = ts // CHUNK
    blk = pl.BlockSpec((ts, GDN_HP * LANES), lambda h, s: (order(s), h))
    row = pl.BlockSpec((GDN_HP, 1, ts), lambda h, s: (h, 0, order(s)))
    st = pl.BlockSpec((GDN_HP, nc, LANES, LANES), lambda h, s: (h, order(s), 0, 0))
    return blk, row, st


def _gdn_fwd(q, k, v, gcb, gct, bb, *, name):
    t = q.shape[0]
    ts = _tile(t, GDN_TILE)
    npair = ts // PAIR

    def body(q_ref, k_ref, v_ref, g_ref, gt_ref, b_ref, o_ref, st_ref, s_sc):
        @pl.when(pl.program_id(1) == 0)
        def _():
            s_sc[...] = jnp.zeros_like(s_sc)

        def pair(pi, _):
            rows = pl.ds(pl.multiple_of(pi * PAIR, PAIR), PAIR)
            heads = [slice(hh * LANES, (hh + 1) * LANES) for hh in range(GDN_HP)]
            c = CHUNK
            cat0 = lambda *xs: jnp.concatenate(xs, axis=0)
            s0 = [s_sc[hh] for hh in range(GDN_HP)]
            cm = _pair_common([q_ref[rows, sl] for sl in heads], [k_ref[rows, sl] for sl in heads],
                              [v_ref[rows, sl] for sl in heads], [g_ref[rows, sl] for sl in heads],
                              [gt_ref[hh, :, rows] for hh in range(GDN_HP)], [b_ref[rows, sl] for sl in heads])
            u, w, qd, kd = cm["u"], cm["w"], cm["qd"], cm["kd"]
            r0 = _each(lambda w_, q_, s: _bdot(cat0(w_[:c], q_[:c]), s), w, qd, s0)
            vn_a = _each(lambda u_, r: u_[:c] - r[:c], u, r0)
            s1 = _each(lambda s, gl, k_, vn: s * gl + _bdot(k_[:c], vn, TN), s0, cm["glast_a"], kd, vn_a)
            r1 = _each(lambda w_, q_, s: _bdot(cat0(w_[c:], q_[c:]), s), w, qd, s1)
            vn_b = _each(lambda u_, r: u_[c:] - r[:c], u, r1)
            s2 = _each(lambda s, gl, k_, vn: s * gl + _bdot(k_[c:], vn, TN), s1, cm["glast_b"], kd, vn_b)
            o = _each(lambda ra, rb, at, va, vb_: cat0(ra[c:], rb[c:]) + _bdot(at, cat0(va, vb_)),
                      r0, r1, cm["attn"], vn_a, vn_b)
            for hh, sl in enumerate(heads):
                st_ref[hh, 2 * pi] = s0[hh]
                st_ref[hh, 2 * pi + 1] = s1[hh]
                s_sc[hh] = s2[hh]
                o_ref[rows, sl] = o[hh]
            return 0

        lax.fori_loop(0, npair, pair, 0)

    blk, row, st = _gdn_specs(t, ts, lambda s: s)
    return pl.pallas_call(
        body, name=name, grid=(GDN_HEADS // GDN_HP, t // ts), in_specs=[blk, blk, blk, blk, row, blk],
        out_specs=[blk, st],
        out_shape=[jax.ShapeDtypeStruct((t, 1024), F32), jax.ShapeDtypeStruct((GDN_HEADS, t // CHUNK, LANES, LANES), F32)],
        scratch_shapes=[pltpu.VMEM((GDN_HP, LANES, LANES), F32)],
        compiler_params=_params(("parallel", "arbitrary")),
    )(q, k, v, gcb, gct, bb)


def _gdn_bwd(q, k, v, gcb, gct, bb, do, states, *, name):
    t = q.shape[0]
    ts = _tile(t, GDN_TILE)
    npair = ts // PAIR
    ns = t // ts
    c = CHUNK

    def body(q_ref, k_ref, v_ref, g_ref, gt_ref, b_ref, do_ref, st_ref, dq_ref, dk_ref, dv_ref, dg_ref, db_ref, ds_sc):
        @pl.when(pl.program_id(1) == 0)
        def _():
            ds_sc[...] = jnp.zeros_like(ds_sc)

        rowsum = lambda x: jnp.sum(x, axis=-1, keepdims=True)
        total = lambda x: jnp.sum(rowsum(x), axis=0, keepdims=True)
        cat0 = lambda *xs: jnp.concatenate(xs, axis=0)
        cat1 = lambda *xs: jnp.concatenate(xs, axis=1)

        def pair(step, _):
            pi = npair - 1 - step
            rows = pl.ds(pl.multiple_of(pi * PAIR, PAIR), PAIR)
            heads = [slice(hh * LANES, (hh + 1) * LANES) for hh in range(GDN_HP)]
            hs = range(GDN_HP)
            qv, kv, vv = ([r[rows, sl] for sl in heads] for r in (q_ref, k_ref, v_ref))
            beta = [b_ref[rows, sl] for sl in heads]
            dov = [do_ref[rows, sl] for sl in heads]
            s0 = [st_ref[hh, 2 * pi] for hh in hs]
            s1 = [st_ref[hh, 2 * pi + 1] for hh in hs]
            ds2 = [ds_sc[hh] for hh in hs]
            cm = _pair_common(qv, kv, vv, [g_ref[rows, sl] for sl in heads], [gt_ref[hh, :, rows] for hh in hs], beta)
            u, w, qd, kd, attn = cm["u"], cm["w"], cm["qd"], cm["kd"], cm["attn"]
            tmat, gamma, eg = cm["tm"], cm["gamma"], cm["eg"]
            incl, strict = cm["incl"], cm["strict"]
            vn_a = _each(lambda u_, w_, s: u_[:c] - _bdot(w_[:c], s), u, w, s0)
            vn_b = _each(lambda u_, w_, s: u_[c:] - _bdot(w_[c:], s), u, w, s1)
            vn = _each(cat0, vn_a, vn_b)
            dvn_att = _each(lambda a, d: _bdot(a, d, TN), attn, dov)
            dattn = _each(lambda d, v_: jnp.where(incl, _bdot(d, v_, NT), 0.0), dov, vn)
            dvn_b = _each(lambda x, k_, d: x[c:] + _bdot(k_[c:], d), dvn_att, kd, ds2)
            rb = _each(lambda d, x, s: _bdot(cat0(d[c:], x), s, NT), dov, dvn_b, s1)
            dkd_b = _each(lambda v_, d: _bdot(v_, d, NT), vn_b, ds2)
            dgl_b = _each(lambda d, s: total(d * s), ds2, s1)
            ds1 = _each(lambda d, gl, q_, w_, o_, x: d * gl + _bdot(cat0(q_[c:], w_[c:]), cat0(o_[c:], -x), TN),
                        ds2, cm["glast_b"], qd, w, dov, dvn_b)
            dvn_a = _each(lambda x, k_, d: x[:c] + _bdot(k_[:c], d), dvn_att, kd, ds1)
            ra = _each(lambda d, x, s: _bdot(cat0(d[:c], x), s, NT), dov, dvn_a, s0)
            dkd_a = _each(lambda v_, d: _bdot(v_, d, NT), vn_a, ds1)
            dgl_a = _each(lambda d, s: total(d * s), ds1, s0)
            ds0 = _each(lambda d, gl, q_, w_, o_, x: d * gl + _bdot(cat0(q_[:c], w_[:c]), cat0(o_[:c], -x), TN),
                        ds1, cm["glast_a"], qd, w, dov, dvn_a)
            dvn = _each(cat0, dvn_a, dvn_b)
            dqd = _each(lambda a, b: cat0(a[:c], b[:c]), ra, rb)
            dw = _each(lambda a, b: -cat0(a[c:], b[c:]), ra, rb)
            dkd = _each(cat0, dkd_a, dkd_b)
            dvw = _each(cat1, dvn, dw)
            dvbk = _each(lambda t_, x: _bdot(t_, x, TN), tmat, dvw)
            dvb = _each(lambda x: x[:, :LANES], dvbk)
            dkbe = _each(lambda x: x[:, LANES:], dvbk)
            dt_ = _each(lambda x, a, b: _bdot(x, cat1(a, b), NT), dvw, cm["vb"], cm["kbe"])
            da1 = _each(lambda t_, x: _bdot(t_, x, TN), tmat, dt_)
            dm = _each(lambda x, t_: jnp.where(strict, -_bdot(x, t_, NT), 0.0), da1, tmat)
            dkk = _each(jnp.multiply, dm, gamma)
            dqk = _each(jnp.multiply, dattn, gamma)
            z = _each(lambda a, b, c_, d: a * b + c_ * d, dm, cm["m"], dattn, attn)
            dkb = _each(lambda x, k_, y, e: _bdot(x, k_) + y * e, dkk, kv, dkbe, eg)
            dk = _each(lambda a, b, kb_, q_, x, e, y, be: _bdot(cat0(a, b), cat0(kb_, q_), TN) + x * e + y * be,
                       dkk, dqk, cm["kb"], qv, dkd, cm["ek"], dkb, beta)
            dq = _each(lambda x, k_, y, e: _bdot(x, k_) + y * e, dqk, kv, dqd, eg)

            def colsum_of(z_):
                zh = z_.astype(BF16)
                zl = (z_ - zh.astype(F32)).astype(BF16)
                return _dot(cat0(zh, zl), jnp.ones((2 * PAIR, LANES), BF16), TN)

            colsum = _each(colsum_of, z)
            ri = lax.broadcasted_iota(jnp.int32, (PAIR, LANES), 0)
            for hh, sl in enumerate(heads):
                dkd_kd = dkd[hh] * kd[hh]
                dgc = (rowsum(z[hh]) - colsum[hh] + rowsum(dqd[hh] * qd[hh]) - rowsum(dkd_kd)
                       + rowsum(dkbe[hh] * cm["kbe"][hh]))
                last_a = total(dkd_kd[:c]) + dgl_a[hh] * cm["glast_a"][hh]
                last_b = total(dkd_kd[c:]) + dgl_b[hh] * cm["glast_b"][hh]
                dgc = dgc + jnp.where(ri == c - 1, last_a, 0.0) + jnp.where(ri == PAIR - 1, last_b, 0.0)
                ds_sc[hh] = ds0[hh]
                dq_ref[rows, sl] = dq[hh]
                dk_ref[rows, sl] = dk[hh]
                dv_ref[rows, sl] = dvb[hh] * beta[hh]
                db_ref[rows, sl] = jnp.broadcast_to(rowsum(dkb[hh] * kv[hh]) + rowsum(dvb[hh] * vv[hh]), (PAIR, LANES))
                dg_ref[rows, sl] = dgc
            return 0

        lax.fori_loop(0, npair, pair, 0)

    blk, row, st = _gdn_specs(t, ts, lambda s: ns - 1 - s)
    out = jax.ShapeDtypeStruct((t, 1024), F32)
    return pl.pallas_call(
        body, name=name, grid=(GDN_HEADS // GDN_HP, ns), in_specs=[blk, blk, blk, blk, row, blk, blk, st],
        out_specs=[blk] * 5, out_shape=[out] * 5, scratch_shapes=[pltpu.VMEM((GDN_HP, LANES, LANES), F32)],
        compiler_params=_params(("parallel", "arbitrary")),
    )(q, k, v, gcb, gct, bb, do, states)


def _out_proj_loss(y, w, hres, g, target, *, name):
    t, d = hres.shape
    tm = _tile(t, 2 * ROW_TILE)

    def body(y_ref, w_ref, h_ref, g_ref, t_ref, dh_ref, dhb_ref, dg_ref, loss_ref):
        i = pl.program_id(0)
        x = h_ref[...] + _dot(y_ref[...], w_ref[...])
        r = lax.rsqrt(jnp.mean(x * x, axis=-1, keepdims=True) + RMS_EPS)
        xh = x * r
        err = xh * g_ref[...] - t_ref[...]
        dy = err * (1.0 / d)
        dxh = dy * g_ref[...]
        dh = r * (dxh - xh * jnp.mean(dxh * xh, axis=-1, keepdims=True))
        dh_ref[...] = dh
        dhb_ref[...] = dh.astype(BF16)

        @pl.when(i == 0)
        def _():
            dg_ref[...] = jnp.zeros_like(dg_ref)
            loss_ref[...] = jnp.zeros_like(loss_ref)

        dg_ref[...] += jnp.sum(dy * xh, axis=0, keepdims=True)
        part = 0.5 * jnp.sum(jnp.mean(err * err, axis=-1, keepdims=True), axis=0, keepdims=True)
        loss_ref[...] += jnp.broadcast_to(part, loss_ref.shape)

    row = pl.BlockSpec((tm, d), lambda i: (i, 0))
    vec = pl.BlockSpec((1, d), lambda i: (0, 0))
    return pl.pallas_call(
        body, name=name, grid=(t // tm,),
        in_specs=[pl.BlockSpec((tm, y.shape[1]), lambda i: (i, 0)), pl.BlockSpec(w.shape, lambda i: (0, 0)), row, vec, row],
        out_specs=[row, row, vec, pl.BlockSpec((8, LANES), lambda i: (0, 0))],
        out_shape=[jax.ShapeDtypeStruct((t, d), F32), jax.ShapeDtypeStruct((t, d), BF16),
                   jax.ShapeDtypeStruct((1, d), F32), jax.ShapeDtypeStruct((8, LANES), F32)],
        compiler_params=_params(("arbitrary",)),
    )(y, w, hres, g, target)


def _pad_cols(w, n):
    return jnp.pad(w, ((0, 0), (0, n - w.shape[1])))


def _layout_odd(w):
    return dict(
        winc=_pad_cols(w["w_in_c"], IN_C_PAD).astype(BF16), wout_c=w["w_out_c"].astype(BF16), conv_w=w["conv_w"],
        a_log=_pad_cols(w["a_log"], LANES), dt_bias=_pad_cols(w["dt_bias"], LANES),
        norm_c=w["norm_c"], o_norm=w["o_norm"], final_norm=w["final_norm"],
    )


def _layout_weights(w):
    return {**_layout_even(w), **_layout_odd(w)}


def _layout_even(w):
    z = lambda r, c: jnp.zeros((r, c), w["w_in_ab"].dtype)
    wi = w["w_in_ab"]
    win = jnp.concatenate([wi[:, :384], z(1024, 64), wi[:, 384:416], z(1024, 32), wi[:, 416:]], axis=1)
    wq = jnp.pad(w["w_q_b"].reshape(MLA_Q_RANK, MLA_HEADS, 96), ((0, 0), (0, 0), (0, 32))).reshape(MLA_Q_RANK, 1024)
    kv3 = w["w_kv_b"].reshape(MLA_KV_RANK, MLA_HEADS, 128)
    wk = jnp.pad(kv3[..., :MLA_NOPE], ((0, 0), (0, 0), (0, 64))).reshape(MLA_KV_RANK, 1024)
    wv = kv3[..., MLA_NOPE:].reshape(MLA_KV_RANK, 512)
    pw = w["pool_w"]
    rows = []
    for g in range(4):
        rows.append(jnp.concatenate([pw[g] if j == g else z(128, 128) for j in range(4)], axis=1))
    wpool = jnp.concatenate(rows, axis=0)
    half = MLA_ROPE // 2
    inv = 1.0 / (ROPE_THETA ** (jnp.arange(half, dtype=F32) / half))
    inv_lane = jnp.concatenate([jnp.zeros((MLA_NOPE,), F32), inv, inv, jnp.zeros((32,), F32)]).reshape(1, LANES)
    return dict(
        win=win.astype(BF16), wq=wq.astype(BF16), wk=wk.astype(BF16), wv=wv.astype(BF16), wpool=wpool.astype(BF16),
        wout_ab=w["w_out_ab"].astype(BF16), inv_lane=inv_lane,
        norm_ab=w["norm_ab"], q_a_norm=w["q_a_norm"], kv_a_norm=w["kv_a_norm"], pool_scale=w["pool_scale"],
    )


def _unlayout_grads(g, names):
    out = {}
    for name in names:
        if name == "w_in_ab":
            dwin = g["win"]
            out[name] = jnp.concatenate([dwin[:, :384], dwin[:, 448:480], dwin[:, 512:]], axis=1)
        elif name == "w_q_b":
            out[name] = g["wq"].reshape(MLA_Q_RANK, MLA_HEADS, 128)[..., :96].reshape(MLA_Q_RANK, 768)
        elif name == "w_kv_b":
            out[name] = jnp.concatenate([g["wk"].reshape(MLA_KV_RANK, MLA_HEADS, 128)[..., :MLA_NOPE],
                                         g["wv"].reshape(MLA_KV_RANK, MLA_HEADS, MLA_V)], axis=-1).reshape(MLA_KV_RANK, 1024)
        elif name == "w_in_c":
            out[name] = g["winc"][:, :4112]
        else:
            out[name] = g[{"w_out_ab": "wout_ab", "w_out_c": "wout_c"}[name]]
    return out


def _local_step(x, pos, target, lw, odd_weights=None, on_grads=None):
    mm = _matmul
    hn = _rms_fwd(x, lw["norm_ab"], name="rms_ab")
    proj = mm(hn, lw["win"], "nn", name="in_ab")
    q, k, v, ybraw, qn, kvn, d, cos_t, sin_t = _ab_prep(
        proj, pos, lw["inv_lane"], lw["q_a_norm"], lw["kv_a_norm"], lw["wq"], lw["wk"], lw["wv"], lw["wpool"], name="ab_prep")
    o, lse = _attn_fwd(q, k, v, name="attn_fwd")
    y = _gate_fwd(o, ybraw, proj, lw["pool_scale"], name="gate_ab")
    h1 = mm(y, lw["wout_ab"], "nn", name="out_ab", add=x)
    lo = lw if odd_weights is None else odd_weights(h1)
    hn1 = _rms_fwd(h1, lo["norm_c"], name="rms_c")
    proj_c = mm(hn1, lo["winc"], "nn", name="in_c")
    q2, k2, v2, gb, bb, gt = _c_prep(proj_c, lo["conv_w"], lo["a_log"], lo["dt_bias"], name="c_prep")
    gt = gt.reshape(GDN_HEADS, 1, gt.shape[1])
    o2, states = _gdn_fwd(q2, k2, v2, gb, gt, bb, name="gdn_fwd")
    y2 = _o_gate_fwd(o2, proj_c, lo["o_norm"], name="gate_c")
    dh2, dh2b, d_final, loss = _out_proj_loss(y2, lo["wout_c"], h1, lo["final_norm"], target, name="out_c_loss")
    g = {"final_norm": d_final}
    dy2 = mm(dh2b, lo["wout_c"], "nt", name="out_c_dx")
    g["wout_c"] = mm(y2, dh2b, "tn", name="out_c_dw")
    do2, dz2, g["o_norm"] = _o_gate_bwd(dy2, o2, proj_c, lo["o_norm"], name="gate_c_bwd")
    dq2, dk2, dv2, dgb, dbb = _gdn_bwd(q2, k2, v2, gb, gt, bb, do2, states, name="gdn_bwd")
    dproj_c, g["conv_w"], g["a_log"], g["dt_bias"] = _c_prep_bwd(
        proj_c, lo["conv_w"], lo["a_log"], lo["dt_bias"], dq2, dk2, dv2, dgb, dbb, dz2, name="c_prep_bwd")
    g["winc"] = mm(hn1, dproj_c, "tn", name="in_c_dw")
    notify = (lambda tag: 0.0) if on_grads is None else (lambda tag: on_grads(tag, g))
    pool_scale = lw["pool_scale"] + notify("odd")
    dh1, dh1b, g["norm_c"] = _matmul_rms_bwd(dproj_c, lo["winc"], h1, lo["norm_c"], dh2, name="in_c_dx_rms", with_bf16=True)
    dy = mm(dh1b, lw["wout_ab"], "nt", name="out_ab_dx")
    g["wout_ab"] = mm(y, dh1b, "tn", name="out_ab_dw")
    pool_scale = pool_scale + notify("out_ab")
    do, delta, dyb, dz, g["pool_scale"] = _gate_bwd(dy, o, ybraw, proj, pool_scale, name="gate_ab_bwd")
    dq, dk, dv = _attn_bwd(q, k, v, do, lse, delta, name="attn_bwd")
    dproj, dqraw, dkb, g["q_a_norm"], g["kv_a_norm"] = _ab_prep_bwd(
        proj, lw["q_a_norm"], lw["kv_a_norm"], dq, dk, dv, cos_t, sin_t, dyb, dz,
        lw["wq"], lw["wk"], lw["wv"], lw["wpool"], name="ab_prep_bwd")
    g["wpool"] = mm(d, dyb, "tn", name="pool_mix_dw")
    g["wq"] = mm(qn, dqraw, "tn", name="q_up_dw")
    g["wk"] = mm(kvn, dkb, "tn", name="k_up_dw")
    g["wv"] = mm(kvn, dv, "tn", name="v_up_dw")
    g["win"] = mm(hn, dproj, "tn", name="in_ab_dw")
    norm_ab = lw["norm_ab"] + notify("in_ab")
    dx, g["norm_ab"] = _matmul_rms_bwd(dproj, lw["win"], x, norm_ab, dh1, name="in_ab_dx_rms", with_bf16=False)
    return loss, dx, g


_HBM = pl.BlockSpec(memory_space=pltpu.HBM)


def _place():
    return lax.axis_index("x"), lax.axis_index("y"), lax.axis_index("c")


def _flip(v, f):
    return 1 - v if f else v


_CHIP_FLIPS = ((1, 0), (0, 1), (1, 1))
_DEV_FLIPS = tuple((fx, fy, fc) for fx in (0, 1) for fy in (0, 1) for fc in (0, 1) if fx or fy or fc)


def _rcopy(src, dst, send_sems, recv_sems, k, to):
    return pltpu.make_async_remote_copy(src_ref=src, dst_ref=dst, send_sem=send_sems.at[k], recv_sem=recv_sems.at[k],
                                        device_id=to, device_id_type=MESH)


def _my_half(ref, c, axis):
    rh = ref.shape[axis] // 2
    idx = [slice(None)] * len(ref.shape)
    idx[axis] = pl.ds(c * rh, rh)
    return ref.at[tuple(idx)]


def _gather_weights(bigs, smalls):
    nb, ns = len(bigs), len(smalls)

    def body(*refs):
        ins, outs = refs[:nb + ns], refs[nb + ns:2 * (nb + ns)]
        send_sems, recv_sems, local_sems = refs[2 * (nb + ns):]
        x, y, c = _place()
        j0 = 2 * x + y
        sib = (x, y, 1 - c)
        chips = [(_flip(x, fx), _flip(y, fy)) for fx, fy in _CHIP_FLIPS]
        local = [pltpu.make_async_copy(i_ref, o_ref.at[j0], local_sems.at[a])
                 for a, (i_ref, o_ref) in enumerate(zip(ins, outs))]
        for cp in local:
            cp.start()
        sends = []
        for k, (px, py) in enumerate(chips):
            for a in range(nb):
                sends.append(_rcopy(_my_half(ins[a], c, 0), _my_half(outs[a].at[j0], c, 0), send_sems, recv_sems,
                                    6 * a + k, (px, py, c)))
            for s in range(ns):
                sends.append(_rcopy(ins[nb + s], outs[nb + s].at[j0], send_sems, recv_sems, 6 * nb + 3 * s + k, (px, py, c)))
        for cp in sends:
            cp.start()
        for k, (px, py) in enumerate(chips):
            jk = 2 * px + py
            for a in range(nb):
                landed = _my_half(outs[a].at[jk], c, 0)
                _rcopy(landed, landed, send_sems, recv_sems, 6 * a + k, (px, py, c)).wait_recv()
                fwd = _rcopy(landed, landed, send_sems, recv_sems, 6 * a + 3 + k, sib)
                fwd.start()
                sends.append(fwd)
        for k, (px, py) in enumerate(chips):
            jk = 2 * px + py
            for a in range(nb):
                other = _my_half(outs[a].at[jk], 1 - c, 0)
                _rcopy(other, other, send_sems, recv_sems, 6 * a + 3 + k, sib).wait_recv()
            for s in range(ns):
                _rcopy(ins[nb + s], outs[nb + s].at[jk], send_sems, recv_sems, 6 * nb + 3 * s + k, (px, py, c)).wait_recv()
        for cp in sends:
            cp.wait_send()
        for cp in local:
            cp.wait()

    arrays = list(bigs) + list(smalls)
    n_sem = 6 * nb + 3 * ns
    return pl.pallas_call(
        body, name="gather_weights", in_specs=[_HBM] * len(arrays), out_specs=[_HBM] * len(arrays),
        out_shape=[jax.ShapeDtypeStruct((4,) + a.shape, a.dtype) for a in arrays],
        scratch_shapes=[pltpu.SemaphoreType.DMA((n_sem,)), pltpu.SemaphoreType.DMA((n_sem,)),
                        pltpu.SemaphoreType.DMA((len(arrays),))],
    )(*arrays)


def _core_swap_partial(gs, *, name):
    n = len(gs)

    def body(*refs):
        ins, outs = refs[:n], refs[n:2 * n]
        send_sems, recv_sems = refs[2 * n:]
        x, y, c = _place()
        copies = [_rcopy(_my_half(i_ref, 1 - c, 1), o_ref, send_sems, recv_sems, a, (x, y, 1 - c))
                  for a, (i_ref, o_ref) in enumerate(zip(ins, outs))]
        for cp in copies:
            cp.start()
        for cp in copies:
            cp.wait()

    return pl.pallas_call(
        body, name=name, in_specs=[_HBM] * n, out_specs=[_HBM] * n,
        out_shape=[jax.ShapeDtypeStruct((4, g.shape[1] // 2, g.shape[2]), g.dtype) for g in gs],
        scratch_shapes=[pltpu.SemaphoreType.DMA((n,)), pltpu.SemaphoreType.DMA((n,))],
    )(*gs)


def _core_swap_sum(fs):
    n = len(fs)

    def body(*refs):
        ins, outs = refs[:n], refs[n:2 * n]
        send_sems, recv_sems = refs[2 * n:]
        x, y, c = _place()
        copies = [_rcopy(_my_half(i_ref, c, 0), _my_half(o_ref, c, 0), send_sems, recv_sems, a, (x, y, 1 - c))
                  for a, (i_ref, o_ref) in enumerate(zip(ins, outs))]
        for cp in copies:
            cp.start()
        for a, cp in enumerate(copies):
            cp.wait_send()
            theirs = _my_half(outs[a], 1 - c, 0)
            _rcopy(theirs, theirs, send_sems, recv_sems, a, (x, y, 1 - c)).wait_recv()

    return pl.pallas_call(
        body, name="core_swap_sum", in_specs=[_HBM] * n, out_specs=[_HBM] * n,
        out_shape=[jax.ShapeDtypeStruct(f.shape, f.dtype) for f in fs],
        input_output_aliases={a: a for a in range(n)},
        scratch_shapes=[pltpu.SemaphoreType.DMA((n,)), pltpu.SemaphoreType.DMA((n,))],
    )(*fs)


_SEM = pl.BlockSpec(memory_space=pltpu.SEMAPHORE)
_ANY = pl.BlockSpec(memory_space=pl.ANY)
_DATAFLOW = pltpu.SideEffectType.DATAFLOW_SIDE_EFFECTING


def _to_chips_copies(srcs, lands, send_sems, recv_sems, per_chip_slot):
    x, y, c = _place()
    j0 = 2 * x + y
    out = []
    for k, (fx, fy) in enumerate(_CHIP_FLIPS):
        px, py = _flip(x, fx), _flip(y, fy)
        jk = 2 * px + py
        for a, (src, land) in enumerate(zip(srcs, lands)):
            piece = src.at[jk] if per_chip_slot else src
            out.append((_rcopy(piece, land.at[j0], send_sems, recv_sems, 3 * a + k, (px, py, c)),
                        _rcopy(piece, land.at[jk], send_sems, recv_sems, 3 * a + k, (px, py, c))))
    return out


def _to_chips_start(arrays, *, per_chip_slot, name):
    n = len(arrays)
    lands = [lax.empty((4,) + (a.shape[1:] if per_chip_slot else a.shape), a.dtype) for a in arrays]

    def body(*refs):
        srcs, land_refs, send_sems, recv_sems, token = refs[:n], refs[n:2 * n], refs[2 * n], refs[2 * n + 1], refs[-1]
        for send, _ in _to_chips_copies(srcs, land_refs, send_sems, recv_sems, per_chip_slot):
            send.start()
        token[...] = jnp.zeros_like(token)

    held = [pltpu.with_memory_space_constraint(a, pltpu.HBM) for a in list(arrays) + lands]
    return pl.pallas_call(
        body, name=name, in_specs=[_HBM] * (2 * n),
        out_specs=(_SEM, _SEM, *[_HBM] * (2 * n), pl.BlockSpec(memory_space=pltpu.VMEM)),
        out_shape=(pltpu.SemaphoreType.DMA((3 * n,)), pltpu.SemaphoreType.DMA((3 * n,)),
                   *[pltpu.HBM(a.shape, a.dtype) for a in held], jax.ShapeDtypeStruct((8, LANES), F32)),
        input_output_aliases={i: 2 + i for i in range(2 * n)},
        compiler_params=pltpu.CompilerParams(has_side_effects=_DATAFLOW),
    )(*held)


def _to_chips_wait(started, after, *, per_chip_slot, name):
    send_sems, recv_sems, held = started[0], started[1], started[2:-1]
    n = len(held) // 2

    def body(*refs):
        srcs, land_refs, s_sems, r_sems = refs[:n], refs[n:2 * n], refs[2 * n], refs[2 * n + 1]
        for send, arrival in _to_chips_copies(srcs, land_refs, s_sems, r_sems, per_chip_slot):
            send.wait_send()
            arrival.wait_recv()

    out = pl.pallas_call(
        body, name=name, in_specs=[_HBM] * (2 * n) + [_SEM, _SEM, _ANY], out_specs=[_HBM] * (2 * n),
        out_shape=[pltpu.HBM(a.shape, a.dtype) for a in held],
        input_output_aliases={i: i for i in range(2 * n)},
        compiler_params=pltpu.CompilerParams(has_side_effects=_DATAFLOW),
    )(*held, send_sems, recv_sems, after)
    return out[n:]


def _chip_exchange(ps, small):
    n = len(ps)
    rs = small.shape[0]

    def body(*refs):
        p_refs, s_ref = refs[:n], refs[n]
        l_refs, ls_ref = refs[n + 1:2 * n + 1], refs[2 * n + 1]
        send_sems, recv_sems, local_sems = refs[2 * n + 2:]
        x, y, c = _place()
        j0 = 2 * x + y
        d0 = 2 * j0 + c
        local = [pltpu.make_async_copy(p.at[j0], l.at[j0], local_sems.at[a]) for a, (p, l) in enumerate(zip(p_refs, l_refs))]
        local.append(pltpu.make_async_copy(s_ref, ls_ref.at[d0], local_sems.at[n]))
        for cp in local:
            cp.start()
        sends = []
        for k, (fx, fy) in enumerate(_CHIP_FLIPS):
            px, py = _flip(x, fx), _flip(y, fy)
            for a in range(n):
                sends.append(_rcopy(p_refs[a].at[2 * px + py], l_refs[a].at[j0], send_sems, recv_sems, 3 * a + k, (px, py, c)))
        for k, (fx, fy, fc) in enumerate(_DEV_FLIPS):
            peer = (_flip(x, fx), _flip(y, fy), _flip(c, fc))
            sends.append(_rcopy(s_ref, ls_ref.at[d0], send_sems, recv_sems, 3 * n + k, peer))
        for cp in sends:
            cp.start()
        for k, (fx, fy) in enumerate(_CHIP_FLIPS):
            px, py = _flip(x, fx), _flip(y, fy)
            for a in range(n):
                _rcopy(p_refs[a].at[j0], l_refs[a].at[2 * px + py], send_sems, recv_sems, 3 * a + k, (px, py, c)).wait_recv()
        for k, (fx, fy, fc) in enumerate(_DEV_FLIPS):
            px, py, pc = _flip(x, fx), _flip(y, fy), _flip(c, fc)
            _rcopy(s_ref, ls_ref.at[4 * px + 2 * py + pc], send_sems, recv_sems, 3 * n + k, (px, py, pc)).wait_recv()
        for cp in sends:
            cp.wait_send()
        for cp in local:
            cp.wait()

    n_sem = 3 * n + 7
    return pl.pallas_call(
        body, name="chip_exchange", in_specs=[_HBM] * (n + 1), out_specs=[_HBM] * (n + 1),
        out_shape=[jax.ShapeDtypeStruct(p.shape, F32) for p in ps] + [jax.ShapeDtypeStruct((8, rs, LANES), F32)],
        scratch_shapes=[pltpu.SemaphoreType.DMA((n_sem,)), pltpu.SemaphoreType.DMA((n_sem,)),
                        pltpu.SemaphoreType.DMA((n + 1,))],
    )(*ps, small)


def _core_sum(g, part, core, *, name):
    _, rh, cols = part.shape
    tr = _tile(rh, 256)
    nb = rh // tr

    def body(c_ref, g_ref, p_ref, o_ref):
        o_ref[...] = g_ref[...] + p_ref[...]

    grid_spec = pltpu.PrefetchScalarGridSpec(
        num_scalar_prefetch=1, grid=(4, nb),
        in_specs=[pl.BlockSpec((1, tr, cols), lambda j, i, c: (j, c[0] * nb + i, 0)),
                  pl.BlockSpec((1, tr, cols), lambda j, i, c: (j, i, 0))],
        out_specs=pl.BlockSpec((1, tr, cols), lambda j, i, c: (j, i, 0)),
    )
    return pl.pallas_call(
        body, name=name, grid_spec=grid_spec, out_shape=jax.ShapeDtypeStruct(part.shape, F32),
        compiler_params=_params(("parallel", "parallel")),
    )(core, g, part)


def _chip_sum(landed, core, *, name):
    _, rh, cols = landed.shape
    tr = _tile(rh, 256)
    nb = rh // tr

    def body(c_ref, l_ref, o_ref):
        o_ref[...] = ((l_ref[0] + l_ref[1]) + l_ref[2]) + l_ref[3]

    grid_spec = pltpu.PrefetchScalarGridSpec(
        num_scalar_prefetch=1, grid=(nb,),
        in_specs=[pl.BlockSpec((4, tr, cols), lambda i, c: (0, i, 0))],
        out_specs=pl.BlockSpec((tr, cols), lambda i, c: (c[0] * nb + i, 0)),
    )
    return pl.pallas_call(
        body, name=name, grid_spec=grid_spec, out_shape=jax.ShapeDtypeStruct((2 * rh, cols), F32),
        compiler_params=_params(("parallel",)),
    )(core, landed)


_ROW_POOL_W, _ROW_NORM_AB, _ROW_FINAL, _ROW_POOL_SCALE, _ROW_Q_NORM = 0, 512, 520, 528, 532
_ROW_KV_NORM, _ROW_O_NORM, _ROW_A_LOG, _ROW_DT_BIAS, _ROW_LOSS = 534, 535, 536, 537, 538
_ROW_CONV, _ROW_NORM_C, _SMALL_ROWS = 544, 640, 672
_CONV_ROWS = CONV_WIDTH * 6


def _put_rows(dst_ref, row0, src, width):
    for r in range(width // LANES):
        dst_ref[row0 + r:row0 + r + 1, :] = src[:, r * LANES:(r + 1) * LANES]


def _pack_small(g, loss_tile):
    names = ("wpool", "norm_ab", "final_norm", "pool_scale", "q_a_norm", "kv_a_norm", "o_norm", "a_log", "dt_bias",
             "conv_w", "norm_c")

    def body(wpool, norm_ab, final_norm, pool_scale, q_norm, kv_norm, o_norm, a_log, dt_bias, conv_w, norm_c, loss, o_ref):
        o_ref[...] = jnp.zeros_like(o_ref)
        for gi in range(4):
            o_ref[_ROW_POOL_W + gi * 128:_ROW_POOL_W + (gi + 1) * 128, :] = wpool[gi * 128:(gi + 1) * 128, gi * 128:(gi + 1) * 128]
        _put_rows(o_ref, _ROW_NORM_AB, norm_ab[...], 1024)
        _put_rows(o_ref, _ROW_FINAL, final_norm[...], 1024)
        _put_rows(o_ref, _ROW_POOL_SCALE, pool_scale[...], 512)
        _put_rows(o_ref, _ROW_Q_NORM, q_norm[...], 256)
        for row, ref in ((_ROW_KV_NORM, kv_norm), (_ROW_O_NORM, o_norm), (_ROW_A_LOG, a_log), (_ROW_DT_BIAS, dt_bias)):
            o_ref[row:row + 1, :] = ref[...]
        o_ref[_ROW_LOSS:_ROW_LOSS + 1, :] = loss[0:1, :]
        for j in range(4):
            for r in range(CONV_WIDTH):
                _put_rows(o_ref, _ROW_CONV + j * _CONV_ROWS + r * 6, conv_w[r:r + 1, j * 768:(j + 1) * 768], 768)
            _put_rows(o_ref, _ROW_NORM_C + j * 8, norm_c[:, j * 256:(j + 1) * 256], 256)

    vmem = pl.BlockSpec(memory_space=pltpu.VMEM)
    return pl.pallas_call(
        body, name="pack_small", in_specs=[vmem] * 12, out_specs=vmem,
        out_shape=jax.ShapeDtypeStruct((_SMALL_ROWS, LANES), F32),
    )(*[g[n] for n in names], loss_tile)


_SMALL_NAMES = ("pool_w", "norm_ab", "final_norm", "pool_scale", "q_a_norm", "kv_a_norm", "o_norm", "a_log", "dt_bias",
                "conv_w", "norm_c")


def _take_rows(src, row0, width):
    return jnp.concatenate([src[row0 + r:row0 + r + 1, :] for r in range(width // LANES)], axis=1)


def _small_update(small_all, ws, ms, vs):
    n = len(_SMALL_NAMES)

    def body(*refs):
        a_ref = refs[0]
        w_refs, m_refs, v_refs = refs[1:1 + n], refs[1 + n:1 + 2 * n], refs[1 + 2 * n:1 + 3 * n]
        outs = refs[1 + 3 * n:1 + 7 * n]
        loss_ref, tot = refs[1 + 7 * n], refs[2 + 7 * n]
        acc = a_ref[0]
        for d in range(1, 8):
            acc = acc + a_ref[d]
        tot[...] = acc
        x, y, _ = _place()
        j0 = 2 * x + y
        conv = tot[pl.ds(pl.multiple_of(_ROW_CONV + j0 * _CONV_ROWS, 8), _CONV_ROWS), :]
        norm_c = tot[pl.ds(pl.multiple_of(_ROW_NORM_C + j0 * 8, 8), 8), :]
        whole = tot[_ROW_NORM_AB:_ROW_CONV, :]
        at = lambda row: row - _ROW_NORM_AB
        grads = {
            "norm_ab": _take_rows(whole, at(_ROW_NORM_AB), 1024), "final_norm": _take_rows(whole, at(_ROW_FINAL), 1024),
            "pool_scale": _take_rows(whole, at(_ROW_POOL_SCALE), 512), "q_a_norm": _take_rows(whole, at(_ROW_Q_NORM), 256),
            "kv_a_norm": whole[at(_ROW_KV_NORM):at(_ROW_KV_NORM) + 1, :], "o_norm": whole[at(_ROW_O_NORM):at(_ROW_O_NORM) + 1, :],
            "a_log": tot[_ROW_A_LOG:_ROW_A_LOG + 1, 0:GDN_HEADS],
            "dt_bias": tot[_ROW_DT_BIAS:_ROW_DT_BIAS + 1, 0:GDN_HEADS],
            "norm_c": _take_rows(norm_c, 0, 256),
        }
        loss_ref[...] = whole[at(_ROW_LOSS):at(_ROW_LOSS) + 1, :]
        for i, name in enumerate(_SMALL_NAMES):
            g_out = outs[4 * i]
            if name == "pool_w":
                for gi in range(4):
                    g_out[gi] = tot[_ROW_POOL_W + gi * 128:_ROW_POOL_W + (gi + 1) * 128, :]
            elif name == "conv_w":
                for r in range(CONV_WIDTH):
                    g_out[r:r + 1, :] = _take_rows(conv, r * 6, 768)
            else:
                g_out[...] = grads[name]
            _adam_update(g_out, w_refs[i], m_refs[i], v_refs[i], *outs[4 * i + 1:4 * i + 4])

    vmem = pl.BlockSpec(memory_space=pltpu.VMEM)
    out_shape = [jax.ShapeDtypeStruct(w.shape, F32) for w in ws for _ in range(4)] + [jax.ShapeDtypeStruct((1, LANES), F32)]
    return pl.pallas_call(
        body, name="small_update", in_specs=[vmem] * (1 + 3 * n), out_specs=[vmem] * (4 * n + 1), out_shape=out_shape,
        scratch_shapes=[pltpu.VMEM((_SMALL_ROWS, LANES), F32)],
        compiler_params=pltpu.CompilerParams(vmem_limit_bytes=VMEM_LIMIT),
    )(small_all, *ws, *ms, *vs)


def _adam_update(g_ref, w_ref, m_ref, v_ref, d_ref, mo_ref, vo_ref):
    gv = g_ref[...]
    mn = ADAM_B1 * m_ref[...] + (1.0 - ADAM_B1) * gv
    vn = ADAM_B2 * v_ref[...] + (1.0 - ADAM_B2) * (gv * gv)
    mo_ref[...] = mn
    vo_ref[...] = vn
    c1 = 1.0 - ADAM_B1 ** ADAM_STEP
    c2 = 1.0 - ADAM_B2 ** ADAM_STEP
    d_ref[...] = -ADAM_LR * ((mn / c1) / (jnp.sqrt(vn / c2) + ADAM_EPS) + ADAM_WD * w_ref[...])


def _adamw_rows(g, w, m, v, *, name):
    rows, cols = g.shape
    tr = _tile(rows, 512)

    def body(*refs):
        _adam_update(*refs)

    blk = pl.BlockSpec((tr, cols), lambda i: (i, 0))
    out = jax.ShapeDtypeStruct((rows, cols), F32)
    return pl.pallas_call(
        body, name=name, grid=(rows // tr,), in_specs=[blk] * 4, out_specs=[blk] * 3, out_shape=[out] * 3,
        compiler_params=_params(("parallel",)),
    )(g, w, m, v)


_ADAM_ROWWISE = ("w_in_ab", "w_q_b", "w_kv_b", "w_out_ab", "w_in_c", "w_out_c")


_SHARD_AXIS = {"w_in_ab": 1, "w_q_b": 1, "w_kv_b": 1, "w_out_ab": 0, "w_in_c": 1, "w_out_c": 0, "conv_w": 1, "norm_c": 1}
_ALL_NAMES = ("norm_ab", "w_in_ab", "q_a_norm", "w_q_b", "kv_a_norm", "w_kv_b", "pool_w", "pool_scale", "w_out_ab",
              "norm_c", "w_in_c", "conv_w", "a_log", "dt_bias", "o_norm", "w_out_c", "final_norm")


def _join_shards(a, axis):
    _, r, c = a.shape
    return a.reshape(4 * r, c) if axis == 0 else jnp.transpose(a, (1, 0, 2)).reshape(r, 4 * c)


def _split_shards(a, axis):
    r, c = a.shape
    return a.reshape(4, r // 4, c) if axis == 0 else jnp.transpose(a.reshape(r, 4, c // 4), (1, 0, 2))


def kernel(x, positions, norm_ab, w_in_ab, q_a_norm, w_q_b, kv_a_norm, w_kv_b, pool_w, pool_scale, w_out_ab, norm_c, w_in_c, conv_w, a_log, dt_bias, o_norm, w_out_c, final_norm, loss_target, m_norm_ab, m_w_in_ab, m_q_a_norm, m_w_q_b, m_kv_a_norm, m_w_kv_b, m_pool_w, m_pool_scale, m_w_out_ab, m_norm_c, m_w_in_c, m_conv_w, m_a_log, m_dt_bias, m_o_norm, m_w_out_c, m_final_norm, v_norm_ab, v_w_in_ab, v_q_a_norm, v_w_q_b, v_kv_a_norm, v_w_kv_b, v_pool_w, v_pool_scale, v_w_out_ab, v_norm_c, v_w_in_c, v_conv_w, v_a_log, v_dt_bias, v_o_norm, v_w_out_c, v_final_norm):
    given = dict(locals())
    c = lax.axis_index("c")
    t = x.shape[1]

    def shard_of(prefix, name):
        a = given[prefix + name]
        return a.reshape(a.shape[1:]) if a.ndim > 2 else a.reshape(1, -1)

    big, big_even, big_odd, small_sharded = _ADAM_ROWWISE, _ADAM_ROWWISE[:4], _ADAM_ROWWISE[4:], ("conv_w", "norm_c")
    chip = 2 * lax.axis_index("x") + lax.axis_index("y")
    core = c.astype(jnp.int32).reshape(1)
    late = big_odd + small_sharded
    late_shards = [shard_of("", n).astype(BF16) for n in big_odd] + [shard_of("", n) for n in small_sharded]
    gather_odd = _to_chips_start(late_shards, per_chip_slot=False, name="gather_odd_start")
    gathered = _gather_weights([shard_of("", n).astype(BF16) for n in big_even], [])
    full = {n: _join_shards(a, _SHARD_AXIS[n]) for n, a in zip(big_even, gathered)}
    for name in ("norm_ab", "q_a_norm", "kv_a_norm", "pool_w", "pool_scale"):
        full[name] = shard_of("", name)
    lw = _layout_even(full)
    lw["norm_ab"] = lw["norm_ab"] + gather_odd[-1][0, 0]

    def odd_weights(h1):
        landed = _to_chips_wait(gather_odd, h1, per_chip_slot=False, name="gather_odd_wait")
        w = {}
        for name, land, own in zip(late, landed, late_shards):
            w[name] = _join_shards(lax.dynamic_update_index_in_dim(land, own, chip, 0), _SHARD_AXIS[name])
        for name in ("a_log", "dt_bias", "o_norm", "final_norm"):
            w[name] = shard_of("", name)
        return _layout_odd(w)

    def chip_partials(names, grads, tag):
        slots = [_split_shards(grads[n], _SHARD_AXIS[n]) for n in names]
        partial = _core_swap_partial(slots, name="core_swap_partial_" + tag)
        return [_core_sum(s, p, core, name="core_sum_" + n) for n, s, p in zip(names, slots, partial)]

    groups = {"odd": big_odd, "out_ab": ("w_out_ab",), "in_ab": ("w_in_ab", "w_q_b", "w_kv_b")}
    sent = {}

    def on_grads(tag, g):
        part = chip_partials(groups[tag], _unlayout_grads(g, groups[tag]), tag)
        sent[tag] = (part, _to_chips_start(part, per_chip_slot=True, name="exchange_" + tag + "_start"))
        return sent[tag][1][-1][0, 0]

    loss_tile, dx, g = _local_step(x[0], positions.reshape(t, 1), loss_target[0], lw, odd_weights, on_grads)
    small_all = _chip_exchange([], _pack_small(g, loss_tile))[-1]
    halves = {}
    for tag, names in groups.items():
        part, started = sent[tag]
        landed = _to_chips_wait(started, small_all, per_chip_slot=True, name="exchange_" + tag + "_wait")
        for n, l, p in zip(names, landed, part):
            l = lax.dynamic_update_index_in_dim(l, lax.dynamic_index_in_dim(p, chip, 0, keepdims=False), chip, 0)
            halves[n] = _chip_sum(l, core, name="chip_sum_" + n)
    gbig = dict(zip(big, _core_swap_sum([halves[n] for n in big])))

    res = {}
    for name in big:
        res["grad", name] = gbig[name]
        out = _adamw_rows(gbig[name], shard_of("", name), shard_of("m_", name), shard_of("v_", name), name="adamw_" + name)
        res["delta", name], res["m", name], res["v", name] = out
    out = _small_update(small_all, [shard_of("", n) for n in _SMALL_NAMES], [shard_of("m_", n) for n in _SMALL_NAMES],
                        [shard_of("v_", n) for n in _SMALL_NAMES])
    for i, name in enumerate(_SMALL_NAMES):
        res["grad", name], res["delta", name], res["m", name], res["v", name] = out[4 * i:4 * i + 4]
    res = {k: a.reshape(given[k[1]].shape) for k, a in res.items()}
    loss = out[-1][0, 0]
    outs = [loss, dx.reshape(x.shape)]
    for key in ("grad", "delta", "m", "v"):
        outs += [res[key, n] for n in _ALL_NAMES]
    return tuple(outs)
```

```python
import functools

import jax
import jax.numpy as jnp
from jax import lax
from jax.experimental import pallas as pl
from jax.experimental.pallas import tpu as pltpu

F32 = jnp.float32
BF16 = jnp.bfloat16
HI = lax.Precision.HIGHEST
MESH = pl.DeviceIdType.MESH

RMS_EPS = 1e-6
D_MODEL = 1024
MLA_HEADS = 8
MLA_Q_RANK = 256
MLA_KV_RANK = 128
MLA_NOPE = 64
MLA_ROPE = 32
MLA_V = 64
ROPE_THETA = 10000.0
POOL_WINDOWS = (2, 4, 8, 16)
POOL_GROUP = 128
POOL_WIDTH = 512
POOL_HALO = 16
GDN_HEADS = 8
GDN_DK = 128
CONV_WIDTH = 4
CONV_HALO = 8
CHUNK = 64
IN_AB_PAD = 2048
IN_C_PAD = 4224
ATT_SCALE = (MLA_NOPE + MLA_ROPE) ** -0.5

ADAM_LR = 0.001
ADAM_B1 = 0.9
ADAM_B2 = 0.999
ADAM_EPS = 1e-08
ADAM_WD = 0.01
ADAM_STEP = 10

LANES = 128
VMEM_LIMIT = 56 * 1024 * 1024

ROW_TILE = 256
ATT_TILE = 1024
GDN_TILE = 256
MM_TILE = (1024, 1408, 2048)

NN = (((1,), (0,)), ((), ()))
NT = (((1,), (1,)), ((), ()))
TN = (((0,), (0,)), ((), ()))


def _dot(a, b, dims=NN, prec=None):
    return lax.dot_general(a, b, dims, precision=prec, preferred_element_type=F32)


def _tile(n, pref):
    if n <= pref:
        return n
    step = LANES if pref >= LANES else 8
    for t in range(pref - pref % step, 0, -step):
        if n % t == 0:
            return t
    return n


def _params(sem):
    return pltpu.CompilerParams(dimension_semantics=sem, vmem_limit_bytes=VMEM_LIMIT)


def _sigmoid(x):
    return 0.5 * jnp.tanh(0.5 * x) + 0.5


def _softplus(x):
    return jnp.maximum(x, 0.0) + jnp.log(1.0 + jnp.exp(-jnp.abs(x)))


def _matmul(a, b, mode, *, name, out_dtype=F32, add=None):
    if mode == "nn":
        (m, k), (k2, n) = a.shape, b.shape
    elif mode == "nt":
        (m, k), (n, k2) = a.shape, b.shape
    else:
        (k, m), (k2, n) = a.shape, b.shape
    assert k == k2, (a.shape, b.shape, mode)
    tm, tn, tk = _tile(m, MM_TILE[0]), _tile(n, MM_TILE[1]), _tile(k, MM_TILE[2])
    nk = k // tk
    if mode == "tn":
        a_spec = pl.BlockSpec((tk, tm), lambda i, j, kk: (kk, i))
    else:
        a_spec = pl.BlockSpec((tm, tk), lambda i, j, kk: (i, kk))
    if mode == "nt":
        b_spec = pl.BlockSpec((tn, tk), lambda i, j, kk: (j, kk))
    else:
        b_spec = pl.BlockSpec((tk, tn), lambda i, j, kk: (kk, j))
    o_spec = pl.BlockSpec((tm, tn), lambda i, j, kk: (i, j))
    dims = {"nn": NN, "nt": NT, "tn": TN}[mode]
    has_add = add is not None

    def body(*refs):
        a_ref, b_ref = refs[0], refs[1]
        add_ref = refs[2] if has_add else None
        o_ref = refs[3] if has_add else refs[2]

        def finish(o):
            if has_add:
                o = o + add_ref[...]
            o_ref[...] = o.astype(out_dtype)

        if nk == 1:
            finish(_dot(a_ref[...], b_ref[...], dims))
            return
        acc = refs[-1]
        kk = pl.program_id(2)

        @pl.when(kk == 0)
        def _():
            acc[...] = jnp.zeros_like(acc)

        acc[...] += _dot(a_ref[...], b_ref[...], dims)

        @pl.when(kk == nk - 1)
        def _():
            finish(acc[...])

    in_specs = [a_spec, b_spec] + ([o_spec] if has_add else [])
    args = (a, b) + ((add,) if has_add else ())
    return pl.pallas_call(
        body, name=name, grid=(m // tm, n // tn, nk), in_specs=in_specs, out_specs=o_spec,
        out_shape=jax.ShapeDtypeStruct((m, n), out_dtype),
        scratch_shapes=[pltpu.VMEM((tm, tn), F32)] if nk > 1 else [],
        compiler_params=_params(("parallel", "parallel", "arbitrary")),
    )(*args)


def _rms_fwd(h, g, *, name):
    t, d = h.shape
    tm = _tile(t, 2 * ROW_TILE)

    def body(h_ref, g_ref, o_ref):
        x = h_ref[...]
        r = lax.rsqrt(jnp.mean(x * x, axis=-1, keepdims=True) + RMS_EPS)
        o_ref[...] = (x * r * g_ref[...]).astype(BF16)

    return pl.pallas_call(
        body, name=name, grid=(t // tm,),
        in_specs=[pl.BlockSpec((tm, d), lambda i: (i, 0)), pl.BlockSpec((1, d), lambda i: (0, 0))],
        out_specs=pl.BlockSpec((tm, d), lambda i: (i, 0)),
        out_shape=jax.ShapeDtypeStruct((t, d), BF16), compiler_params=_params(("parallel",)),
    )(h, g)


def _matmul_rms_bwd(dproj, w, h, g, dres, *, name, with_bf16):
    t, k = dproj.shape
    d = w.shape[0]
    tm = _tile(t, 2 * ROW_TILE)

    def body(dp_ref, w_ref, h_ref, g_ref, dres_ref, *outs):
        i = pl.program_id(0)
        dh_ref, dg_ref = outs[0], outs[-1]
        dyv = _dot(dp_ref[...], w_ref[...], NT)
        x = h_ref[...]
        r = lax.rsqrt(jnp.mean(x * x, axis=-1, keepdims=True) + RMS_EPS)
        xh = x * r
        dxh = dyv * g_ref[...]
        dh = dres_ref[...] + r * (dxh - xh * jnp.mean(dxh * xh, axis=-1, keepdims=True))
        dh_ref[...] = dh
        if with_bf16:
            outs[1][...] = dh.astype(BF16)

        @pl.when(i == 0)
        def _():
            dg_ref[...] = jnp.zeros_like(dg_ref)

        dg_ref[...] += jnp.sum(dyv * xh, axis=0, keepdims=True)

    row = pl.BlockSpec((tm, d), lambda i: (i, 0))
    vec = pl.BlockSpec((1, d), lambda i: (0, 0))
    out_shape = [jax.ShapeDtypeStruct((t, d), F32)] + ([jax.ShapeDtypeStruct((t, d), BF16)] if with_bf16 else [])
    out_specs = [row] * len(out_shape) + [vec]
    out_shape.append(jax.ShapeDtypeStruct((1, d), F32))
    return pl.pallas_call(
        body, name=name, grid=(t // tm,),
        in_specs=[pl.BlockSpec((tm, k), lambda i: (i, 0)), pl.BlockSpec((d, k), lambda i: (0, 0)), row, vec, row],
        out_specs=out_specs, out_shape=out_shape, compiler_params=_params(("arbitrary",)),
    )(dproj, w, h, g, dres)


def _rope_partner(x):
    lane = lax.broadcasted_iota(jnp.int32, x.shape, 1)
    swapped = jnp.where(lane < MLA_NOPE + MLA_ROPE // 2, pltpu.roll(x, LANES - 16, 1), pltpu.roll(x, 16, 1))
    return jnp.where((lane >= MLA_NOPE) & (lane < MLA_NOPE + MLA_ROPE), swapped, 0.0)


def _pool_counts(row0, tm, w):
    t_idx = row0 + lax.broadcasted_iota(jnp.int32, (tm, POOL_GROUP), 0)
    return jnp.minimum(t_idx + 1, w).astype(F32)


def _ab_prep(proj, pos, inv_freq, q_a_norm, kv_a_norm, wq, wk, wv, wpool, *, name):
    t = proj.shape[0]
    tm = _tile(t, ROW_TILE)
    hb = tm // POOL_HALO

    def body(p_ref, halo_ref, pos_ref, inv_ref, qg_ref, kg_ref, wq_ref, wk_ref, wv_ref, wp_ref,
             q_ref, k_ref, v_ref, yb_ref, qn_ref, kvn_ref, d_ref, cos_ref, sin_ref, ext):
        i = pl.program_id(0)
        ql = p_ref[:, 0:MLA_Q_RANK]
        r = lax.rsqrt(jnp.mean(ql * ql, axis=-1, keepdims=True) + RMS_EPS)
        qn = (ql * r * qg_ref[...]).astype(BF16)
        qn_ref[...] = qn
        kl = p_ref[:, MLA_Q_RANK:MLA_Q_RANK + MLA_KV_RANK]
        r = lax.rsqrt(jnp.mean(kl * kl, axis=-1, keepdims=True) + RMS_EPS)
        kvn = (kl * r * kg_ref[...]).astype(BF16)
        kvn_ref[...] = kvn
        ang = pos_ref[...].astype(F32) * inv_ref[...]
        lane = lax.broadcasted_iota(jnp.int32, (tm, LANES), 1)
        in_rope = (lane >= MLA_NOPE) & (lane < MLA_NOPE + MLA_ROPE)
        cos_t = jnp.where(in_rope, jnp.cos(ang), 1.0)
        sin_t = jnp.where(in_rope, jnp.sin(ang), 0.0)
        sin_t = jnp.where(lane < MLA_NOPE + MLA_ROPE // 2, -sin_t, sin_t)
        cos_ref[...] = cos_t
        sin_ref[...] = sin_t
        kr = p_ref[:, 384:512]
        kr = kr * cos_t + _rope_partner(kr) * sin_t
        qraw = _dot(qn, wq_ref[...])
        kvk = _dot(kvn, wk_ref[...])
        for h in range(MLA_HEADS):
            sl = slice(h * LANES, (h + 1) * LANES)
            qh = qraw[:, sl]
            q_ref[:, sl] = ((qh * cos_t + _rope_partner(qh) * sin_t) * ATT_SCALE).astype(BF16)
            k_ref[:, sl] = (kvk[:, sl] + kr).astype(BF16)
        v_ref[...] = _dot(kvn, wv_ref[...]).astype(BF16)
        xp = p_ref[:, 512:1024]
        ext[0:POOL_HALO, :] = jnp.where(i > 0, halo_ref[...], 0.0)
        ext[POOL_HALO:POOL_HALO + tm, :] = xp
        for g, w in enumerate(POOL_WINDOWS):
            lo = g * POOL_GROUP
            acc = ext[POOL_HALO:POOL_HALO + tm, lo:lo + POOL_GROUP]
            for s in range(1, w):
                acc = acc + ext[POOL_HALO - s:POOL_HALO - s + tm, lo:lo + POOL_GROUP]
            cnt = _pool_counts(i * tm, tm, w)
            d_ref[:, lo:lo + POOL_GROUP] = (acc / cnt - xp[:, lo:lo + POOL_GROUP]).astype(BF16)
        yb_ref[...] = _dot(d_ref[...], wp_ref[...])

    row = lambda w: pl.BlockSpec((tm, w), lambda i: (i, 0))
    vec = lambda w: pl.BlockSpec((1, w), lambda i: (0, 0))
    whole = lambda a: pl.BlockSpec(a.shape, lambda i: (0, 0))
    return pl.pallas_call(
        body, name=name, grid=(t // tm,),
        in_specs=[row(1024), pl.BlockSpec((POOL_HALO, POOL_WIDTH), lambda i: (jnp.maximum(i * hb - 1, 0), 1)),
                  pl.BlockSpec((tm, 1), lambda i: (i, 0)), vec(LANES), vec(MLA_Q_RANK), vec(MLA_KV_RANK),
                  whole(wq), whole(wk), whole(wv), whole(wpool)],
        out_specs=[row(1024), row(1024), row(512), row(512), row(MLA_Q_RANK), row(MLA_KV_RANK), row(POOL_WIDTH),
                   row(LANES), row(LANES)],
        out_shape=[jax.ShapeDtypeStruct((t, 1024), BF16), jax.ShapeDtypeStruct((t, 1024), BF16),
                   jax.ShapeDtypeStruct((t, 512), BF16), jax.ShapeDtypeStruct((t, 512), F32),
                   jax.ShapeDtypeStruct((t, MLA_Q_RANK), BF16), jax.ShapeDtypeStruct((t, MLA_KV_RANK), BF16),
                   jax.ShapeDtypeStruct((t, POOL_WIDTH), BF16), jax.ShapeDtypeStruct((t, LANES), F32),
                   jax.ShapeDtypeStruct((t, LANES), F32)],
        scratch_shapes=[pltpu.VMEM((tm + POOL_HALO, POOL_WIDTH), F32)],
        compiler_params=_params(("parallel",)),
    )(proj, proj, pos, inv_freq, q_a_norm, kv_a_norm, wq, wk, wv, wpool)


def _ab_prep_bwd(proj, q_a_norm, kv_a_norm, dq, dk, dv, cos_t, sin_t, dyb, dz, wq, wk, wv, wpool, *, name):
    t = proj.shape[0]
    tm = _tile(t, ROW_TILE)
    hb = tm // POOL_HALO
    last_halo = t // POOL_HALO - 1
    nt = t // tm

    def body(p_ref, qg_ref, kg_ref, dq_ref, dk_ref, dv_ref, c_ref, s_ref, dyb_ref, dybn_ref, dz_ref,
             wq_ref, wk_ref, wv_ref, wp_ref, dp_ref, dqr_ref, dkb_ref, dqg_ref, dkg_ref, ext):
        i = pl.program_id(0)

        @pl.when(i == 0)
        def _():
            dqg_ref[...] = jnp.zeros_like(dqg_ref)
            dkg_ref[...] = jnp.zeros_like(dkg_ref)

        def norm_bwd(x, g, dy, dg_ref):
            r = lax.rsqrt(jnp.mean(x * x, axis=-1, keepdims=True) + RMS_EPS)
            xh = x * r
            dxh = dy * g
            dg_ref[...] += jnp.sum(dy * xh, axis=0, keepdims=True)
            return r * (dxh - xh * jnp.mean(dxh * xh, axis=-1, keepdims=True))

        c, s = c_ref[...], s_ref[...]
        lane = lax.broadcasted_iota(jnp.int32, (tm, LANES), 1)
        in_rope = (lane >= MLA_NOPE) & (lane < MLA_NOPE + MLA_ROPE)
        dkr = jnp.zeros((tm, LANES), F32)
        for h in range(MLA_HEADS):
            sl = slice(h * LANES, (h + 1) * LANES)
            g = dq_ref[:, sl]
            dqr_ref[:, sl] = ((g * c + _rope_partner(g * s)) * ATT_SCALE).astype(BF16)
            gk = dk_ref[:, sl]
            dkb_ref[:, sl] = gk.astype(BF16)
            dkr = dkr + jnp.where(in_rope, gk, 0.0)
        dkr = dkr * c + _rope_partner(dkr * s)
        dqn = _dot(dqr_ref[...], wq_ref[...], NT)
        dkvn = _dot(dkb_ref[...], wk_ref[...], NT) + _dot(dv_ref[...], wv_ref[...], NT)
        dql = norm_bwd(p_ref[:, 0:MLA_Q_RANK], qg_ref[...], dqn, dqg_ref)
        dp_ref[:, 0:MLA_Q_RANK] = dql.astype(BF16)
        dkl = norm_bwd(p_ref[:, MLA_Q_RANK:384], kg_ref[...], dkvn, dkg_ref)
        dp_ref[:, MLA_Q_RANK:384] = dkl.astype(BF16)
        dp_ref[:, 384:512] = dkr.astype(BF16)
        ddv = _dot(dyb_ref[...], wp_ref[...], NT)
        ddn = _dot(dybn_ref[...], wp_ref[...], NT)
        for g, w in enumerate(POOL_WINDOWS):
            lo = g * POOL_GROUP
            ext[0:tm, lo:lo + POOL_GROUP] = ddv[:, lo:lo + POOL_GROUP] / _pool_counts(i * tm, tm, w)
            nxt = ddn[:, lo:lo + POOL_GROUP] / _pool_counts((i + 1) * tm, POOL_HALO, w)
            ext[tm:tm + POOL_HALO, lo:lo + POOL_GROUP] = jnp.where(i < nt - 1, nxt, 0.0)
        for g, w in enumerate(POOL_WINDOWS):
            lo = g * POOL_GROUP
            acc = ext[0:tm, lo:lo + POOL_GROUP]
            for s in range(1, w):
                acc = acc + ext[s:s + tm, lo:lo + POOL_GROUP]
            dp_ref[:, 512 + lo:512 + lo + POOL_GROUP] = (acc - ddv[:, lo:lo + POOL_GROUP]).astype(BF16)
        dp_ref[:, 1024:2048] = dz_ref[...]

    row = lambda w: pl.BlockSpec((tm, w), lambda i: (i, 0))
    vec = lambda w: pl.BlockSpec((1, w), lambda i: (0, 0))
    whole = lambda a: pl.BlockSpec(a.shape, lambda i: (0, 0))
    return pl.pallas_call(
        body, name=name, grid=(nt,),
        in_specs=[row(1024), vec(MLA_Q_RANK), vec(MLA_KV_RANK), row(1024), row(1024), row(512), row(LANES), row(LANES),
                  row(POOL_WIDTH),
                  pl.BlockSpec((POOL_HALO, POOL_WIDTH), lambda i: (jnp.minimum((i + 1) * hb, last_halo), 0)),
                  row(1024), whole(wq), whole(wk), whole(wv), whole(wpool)],
        out_specs=[row(IN_AB_PAD), row(1024), row(1024), vec(MLA_Q_RANK), vec(MLA_KV_RANK)],
        out_shape=[jax.ShapeDtypeStruct((t, IN_AB_PAD), BF16), jax.ShapeDtypeStruct((t, 1024), BF16),
                   jax.ShapeDtypeStruct((t, 1024), BF16), jax.ShapeDtypeStruct((1, MLA_Q_RANK), F32),
                   jax.ShapeDtypeStruct((1, MLA_KV_RANK), F32)],
        scratch_shapes=[pltpu.VMEM((tm + POOL_HALO, POOL_WIDTH), F32)],
        compiler_params=_params(("arbitrary",)),
    )(proj, q_a_norm, kv_a_norm, dq, dk, dv, cos_t, sin_t, dyb, dyb, dz, wq, wk, wv, wpool)


def _gate_out_proj(o, ybraw, proj, pool_scale, w, hres, *, name):
    t = o.shape[0]
    tm = _tile(t, 2 * ROW_TILE)

    def body(o_ref, yb_ref, z_ref, ps_ref, w_ref, h_ref, ho_ref, y_ref):
        z = z_ref[...]
        sz = z * _sigmoid(z)
        y_ref[:, 0:512] = (o_ref[...] * sz[:, 0:512]).astype(BF16)
        y_ref[:, 512:1024] = (yb_ref[...] * ps_ref[...] * sz[:, 512:1024]).astype(BF16)
        ho_ref[...] = h_ref[...] + _dot(y_ref[...], w_ref[...])

    row = lambda w_: pl.BlockSpec((tm, w_), lambda i: (i, 0))
    return pl.pallas_call(
        body, name=name, grid=(t // tm,),
        in_specs=[row(512), row(512), pl.BlockSpec((tm, 1024), lambda i: (i, 1)), pl.BlockSpec((1, 512), lambda i: (0, 0)),
                  pl.BlockSpec(w.shape, lambda i: (0, 0)), row(1024)],
        out_specs=[row(1024), row(1024)],
        out_shape=[jax.ShapeDtypeStruct((t, 1024), F32), jax.ShapeDtypeStruct((t, 1024), BF16)],
        compiler_params=_params(("parallel",)),
    )(o, ybraw, proj, pool_scale, w, hres)


def _gate_bwd(dy, o, ybraw, proj, pool_scale, *, name):
    t = o.shape[0]
    tm = _tile(t, ROW_TILE)

    def body(dy_ref, o_ref, yb_ref, z_ref, ps_ref, do_ref, dl_ref, dyb_ref, dz_ref, dps_ref):
        i = pl.program_id(0)
        z = z_ref[...]
        sg = _sigmoid(z)
        sz = z * sg
        dsz = sg * (1.0 + z * (1.0 - sg))
        dyv = dy_ref[...]
        dcat = dyv * sz
        ov = o_ref[...]
        ybs = yb_ref[...] * ps_ref[...]
        dz_ref[:, 0:512] = (dyv[:, 0:512] * ov * dsz[:, 0:512]).astype(BF16)
        dz_ref[:, 512:1024] = (dyv[:, 512:1024] * ybs * dsz[:, 512:1024]).astype(BF16)
        do = dcat[:, 0:512]
        do_ref[...] = do.astype(BF16)
        r_i = lax.broadcasted_iota(jnp.int32, (512, 512), 0) // MLA_V
        c_i = lax.broadcasted_iota(jnp.int32, (512, 512), 1) // MLA_V
        dl_ref[...] = _dot(do * ov, (r_i == c_i).astype(F32), NN, HI)
        dyb_ref[...] = (dcat[:, 512:1024] * ps_ref[...]).astype(BF16)

        @pl.when(i == 0)
        def _():
            dps_ref[...] = jnp.zeros_like(dps_ref)

        dps_ref[...] += jnp.sum(dcat[:, 512:1024] * yb_ref[...], axis=0, keepdims=True)

    row = lambda w: pl.BlockSpec((tm, w), lambda i: (i, 0))
    vec = pl.BlockSpec((1, 512), lambda i: (0, 0))
    return pl.pallas_call(
        body, name=name, grid=(t // tm,),
        in_specs=[row(1024), row(512), row(512), pl.BlockSpec((tm, 1024), lambda i: (i, 1)), vec],
        out_specs=[row(512), row(512), row(512), row(1024), vec],
        out_shape=[jax.ShapeDtypeStruct((t, 512), BF16), jax.ShapeDtypeStruct((t, 512), F32),
                   jax.ShapeDtypeStruct((t, 512), BF16), jax.ShapeDtypeStruct((t, 1024), BF16),
                   jax.ShapeDtypeStruct((1, 512), F32)],
        compiler_params=_params(("arbitrary",)),
    )(dy, o, ybraw, proj, pool_scale)


ATT_HP_FWD = 4
ATT_HP_BWD = 2


def _diag_mask(tq):
    return lax.broadcasted_iota(jnp.int32, (tq, tq), 1) <= lax.broadcasted_iota(jnp.int32, (tq, tq), 0)


def _block_schedule(nq, key_major):
    if key_major:
        pairs = [(qi, ki) for ki in range(nq) for qi in range(ki, nq)]
    else:
        pairs = [(qi, ki) for qi in range(nq) for ki in range(qi + 1)]
    return jnp.asarray([p[0] for p in pairs], jnp.int32), jnp.asarray([p[1] for p in pairs], jnp.int32)


def _attn_fwd(q, k, v, *, name):
    t = q.shape[0]
    tq = _tile(t, ATT_TILE)
    nq = t // tq
    hp = ATT_HP_FWD
    qi_tab, ki_tab = _block_schedule(nq, key_major=False)

    def body(qi_ref, ki_ref, q_ref, k_ref, v_ref, o_ref, lse_ref, m_sc, l_sc, acc_sc):
        step = pl.program_id(1)
        qi, ki = qi_ref[step], ki_ref[step]

        @pl.when(ki == 0)
        def _():
            m_sc[...] = jnp.full_like(m_sc, -jnp.inf)
            l_sc[...] = jnp.zeros_like(l_sc)
            acc_sc[...] = jnp.zeros_like(acc_sc)

        def block(on_diagonal):
            scores = []
            for h in range(hp):
                sl = slice(h * LANES, (h + 1) * LANES)
                scores.append(_dot(q_ref[:, sl], k_ref[:, sl], NT))
            if on_diagonal:
                mask = _diag_mask(tq)
                scores = [jnp.where(mask, s, -jnp.inf) for s in scores]
            for h, s in enumerate(scores):
                vv = v_ref[:, (h // 2) * LANES:(h // 2 + 1) * LANES]
                m_prev = m_sc[h]
                m_new = jnp.maximum(m_prev, jnp.max(s, axis=-1, keepdims=True))
                alpha = jnp.exp(m_prev - m_new)
                p = jnp.exp(s - m_new[:, 0:1])
                l_sc[h] = alpha * l_sc[h] + jnp.sum(p, axis=-1, keepdims=True)
                acc_sc[h] = alpha * acc_sc[h] + _dot(p.astype(BF16), vv)
                m_sc[h] = m_new

        pl.when(ki < qi)(functools.partial(block, False))
        pl.when(ki == qi)(functools.partial(block, True))

        @pl.when(ki == qi)
        def _():
            first = lax.broadcasted_iota(jnp.int32, (tq, LANES), 1) < MLA_V
            for pr in range(hp // 2):
                a, b = 2 * pr, 2 * pr + 1
                sl = slice(pr * LANES, (pr + 1) * LANES)
                o_ref[:, sl] = jnp.where(first, acc_sc[a] / l_sc[a], acc_sc[b] / l_sc[b])
                lse_ref[:, sl] = jnp.where(first, m_sc[a] + jnp.log(l_sc[a]), m_sc[b] + jnp.log(l_sc[b]))

    grid_spec = pltpu.PrefetchScalarGridSpec(
        num_scalar_prefetch=2, grid=(MLA_HEADS // hp, qi_tab.shape[0]),
        in_specs=[pl.BlockSpec((tq, hp * LANES), lambda g, s, qt, kt: (qt[s], g)),
                  pl.BlockSpec((tq, hp * LANES), lambda g, s, qt, kt: (kt[s], g)),
                  pl.BlockSpec((tq, hp * MLA_V), lambda g, s, qt, kt: (kt[s], g))],
        out_specs=[pl.BlockSpec((tq, hp * MLA_V), lambda g, s, qt, kt: (qt[s], g)),
                   pl.BlockSpec((tq, hp * MLA_V), lambda g, s, qt, kt: (qt[s], g))],
        scratch_shapes=[pltpu.VMEM((hp, tq, LANES), F32)] * 3,
    )
    return pl.pallas_call(
        body, name=name, grid_spec=grid_spec,
        out_shape=[jax.ShapeDtypeStruct((t, 512), F32), jax.ShapeDtypeStruct((t, 512), F32)],
        compiler_params=_params(("parallel", "arbitrary")),
    )(qi_tab, ki_tab, q, k, v)


def _attn_bwd(q, k, v, do, lse, delta, *, name):
    t = q.shape[0]
    tq = _tile(t, ATT_TILE)
    nq = t // tq
    hp = ATT_HP_BWD
    qi_tab, ki_tab = _block_schedule(nq, key_major=True)

    def body(qi_ref, ki_ref, q_ref, k_ref, v_ref, do_ref, lse_ref, dl_ref, dq_ref, dk_ref, dv_ref, dk_sc, dv_sc):
        step = pl.program_id(1)
        qi, ki = qi_ref[step], ki_ref[step]

        @pl.when(step == 0)
        def _():
            dq_ref[...] = jnp.zeros_like(dq_ref)

        @pl.when(qi == ki)
        def _():
            dk_sc[...] = jnp.zeros_like(dk_sc)
            dv_sc[...] = jnp.zeros_like(dv_sc)

        def block(on_diagonal):
            lane = lax.broadcasted_iota(jnp.int32, (tq, LANES), 1)
            rows = pl.ds(pl.multiple_of(qi * tq, tq), tq)
            heads = [slice(h * LANES, (h + 1) * LANES) for h in range(hp)]
            scores = [_dot(q_ref[:, sl], k_ref[:, sl], NT) for sl in heads]
            dps = []
            for h in range(hp):
                dov = do_ref[:, (h // 2) * LANES:(h // 2 + 1) * LANES]
                mine = (lane < MLA_V) if h % 2 == 0 else (lane >= MLA_V)
                dps.append(_dot(jnp.where(mine, dov, jnp.zeros_like(dov)), v_ref[:, (h // 2) * LANES:(h // 2 + 1) * LANES], NT))
            mask = _diag_mask(tq) if on_diagonal else None
            for h, sl in enumerate(heads):
                col = (h // 2) * LANES + (h % 2) * MLA_V
                p = jnp.exp(scores[h] - lse_ref[:, col:col + 1])
                if on_diagonal:
                    p = jnp.where(mask, p, 0.0)
                ds = (p * (dps[h] - dl_ref[:, col:col + 1])).astype(BF16)
                dv_sc[h] += _dot(p.astype(BF16), do_ref[:, (h // 2) * LANES:(h // 2 + 1) * LANES], TN)
                dk_sc[h] += _dot(ds, q_ref[:, sl], TN)
                dq_ref[rows, sl] += _dot(ds, k_ref[:, sl], NN)

        pl.when(qi > ki)(functools.partial(block, False))
        pl.when(qi == ki)(functools.partial(block, True))

        @pl.when(qi == nq - 1)
        def _():
            first = lax.broadcasted_iota(jnp.int32, (tq, LANES), 1) < MLA_V
            for h in range(hp):
                dk_ref[:, h * LANES:(h + 1) * LANES] = dk_sc[h]
            for pr in range(hp // 2):
                dv_ref[:, pr * LANES:(pr + 1) * LANES] = jnp.where(first, dv_sc[2 * pr], dv_sc[2 * pr + 1]).astype(BF16)

    qrow = lambda w: pl.BlockSpec((tq, w), lambda g, s, qt, kt: (qt[s], g))
    krow = lambda w: pl.BlockSpec((tq, w), lambda g, s, qt, kt: (kt[s], g))
    grid_spec = pltpu.PrefetchScalarGridSpec(
        num_scalar_prefetch=2, grid=(MLA_HEADS // hp, qi_tab.shape[0]),
        in_specs=[qrow(hp * LANES), krow(hp * LANES), krow(hp * MLA_V), qrow(hp * MLA_V), qrow(hp * MLA_V), qrow(hp * MLA_V)],
        out_specs=[pl.BlockSpec((t, hp * LANES), lambda g, s, qt, kt: (0, g)), krow(hp * LANES), krow(hp * MLA_V)],
        scratch_shapes=[pltpu.VMEM((hp, tq, LANES), F32), pltpu.VMEM((hp, tq, LANES), F32)],
    )
    return pl.pallas_call(
        body, name=name, grid_spec=grid_spec,
        out_shape=[jax.ShapeDtypeStruct((t, 1024), F32), jax.ShapeDtypeStruct((t, 1024), F32),
                   jax.ShapeDtypeStruct((t, 512), BF16)],
        compiler_params=_params(("parallel", "arbitrary")),
    )(qi_tab, ki_tab, q, k, v, do, lse, delta)


def _conv_rows(ext, tm, w_ref, sec):
    c0 = sec * 1024
    y = ext[CONV_HALO - 3:CONV_HALO - 3 + tm, c0:c0 + 1024] * w_ref[0:1, c0:c0 + 1024]
    for j in range(1, CONV_WIDTH):
        y = y + ext[CONV_HALO - 3 + j:CONV_HALO - 3 + j + tm, c0:c0 + 1024] * w_ref[j:j + 1, c0:c0 + 1024]
    return y


def _c_prep(proj_c, conv_w, a_log, dt_bias, *, name):
    t = proj_c.shape[0]
    tm = _tile(t, ROW_TILE)
    hb = tm // CONV_HALO

    def body(p_ref, halo_ref, ab_ref, w_ref, al_ref, dtb_ref, q_ref, k_ref, v_ref, g_ref, b_ref, gt_ref, ext):
        i = pl.program_id(0)
        ext[0:CONV_HALO, :] = jnp.where(i > 0, halo_ref[...], 0.0)
        ext[CONV_HALO:CONV_HALO + tm, :] = p_ref[...]
        for sec, o_ref in enumerate((q_ref, k_ref, v_ref)):
            y = _conv_rows(ext, tm, w_ref, sec)
            y = y * _sigmoid(y)
            if sec == 2:
                o_ref[...] = y
                continue
            scale = GDN_DK ** -0.5 if sec == 0 else 1.0
            for h in range(GDN_HEADS):
                sl = slice(h * LANES, (h + 1) * LANES)
                blk = y[:, sl]
                r = lax.rsqrt(jnp.sum(blk * blk, axis=-1, keepdims=True) + RMS_EPS)
                o_ref[:, sl] = blk * (r * scale)
        ab = ab_ref[...]
        g = -jnp.exp(al_ref[...]) * _softplus(ab + dtb_ref[...])
        beta = _sigmoid(ab)
        ri = lax.broadcasted_iota(jnp.int32, (tm, tm), 0)
        ci = lax.broadcasted_iota(jnp.int32, (tm, tm), 1)
        lower = ((ri // CHUNK) == (ci // CHUNK)) & (ri >= ci)
        gc = _dot(lower.astype(F32), g, NN, HI)
        eye = lax.broadcasted_iota(jnp.int32, (LANES, LANES), 0) == lax.broadcasted_iota(jnp.int32, (LANES, LANES), 1)
        gt_ref[...] = _dot(eye.astype(F32), gc, NT, HI)[0:GDN_HEADS, :]
        for h in range(GDN_HEADS):
            sl = slice(h * LANES, (h + 1) * LANES)
            g_ref[:, sl] = jnp.broadcast_to(gc[:, h:h + 1], (tm, LANES))
            b_ref[:, sl] = jnp.broadcast_to(beta[:, GDN_HEADS + h:GDN_HEADS + h + 1], (tm, LANES))

    row = lambda w: pl.BlockSpec((tm, w), lambda i: (i, 0))
    vec = lambda r, w: pl.BlockSpec((r, w), lambda i: (0, 0))
    out = jax.ShapeDtypeStruct((t, 1024), F32)
    return pl.pallas_call(
        body, name=name, grid=(t // tm,),
        in_specs=[row(3072), pl.BlockSpec((CONV_HALO, 3072), lambda i: (jnp.maximum(i * hb - 1, 0), 0)),
                  pl.BlockSpec((tm, LANES), lambda i: (i, 32)), vec(CONV_WIDTH, 3072), vec(1, LANES), vec(1, LANES)],
        out_specs=[row(1024)] * 5 + [pl.BlockSpec((GDN_HEADS, tm), lambda i: (0, i))],
        out_shape=[out] * 5 + [jax.ShapeDtypeStruct((GDN_HEADS, t), F32)],
        scratch_shapes=[pltpu.VMEM((tm + CONV_HALO, 3072), F32)],
        compiler_params=_params(("parallel",)),
    )(proj_c, proj_c, proj_c, conv_w, a_log, dt_bias)


def _c_prep_bwd(proj_c, conv_w, a_log, dt_bias, dq, dk, dv, dgb, dbb, dz, *, name):
    t = proj_c.shape[0]
    tm = _tile(t, ROW_TILE // 2)
    hb = tm // CONV_HALO
    nt = t // tm
    rev = lambda i: nt - 1 - i

    def body(p_ref, halo_ref, ab_ref, w_ref, al_ref, dtb_ref, dq_ref, dk_ref, dv_ref, dg_ref, db_ref, dz_ref,
             dp_ref, dw_ref, dal_ref, ddt_ref, ext, dyext, carry, taps):
        step = pl.program_id(0)
        i = rev(step)

        @pl.when(step == 0)
        def _():
            dw_ref[...] = jnp.zeros_like(dw_ref)
            dal_ref[...] = jnp.zeros_like(dal_ref)
            ddt_ref[...] = jnp.zeros_like(ddt_ref)
            carry[...] = jnp.zeros_like(carry)

        ext[0:CONV_HALO, :] = jnp.where(i > 0, halo_ref[...], 0.0)
        ext[CONV_HALO:CONV_HALO + tm, :] = p_ref[...]
        for sec, g_ref in enumerate((dq_ref, dk_ref, dv_ref)):
            c0 = sec * 1024
            for j in range(CONV_WIDTH):
                taps[j] = ext[CONV_HALO - 3 + j:CONV_HALO - 3 + j + tm, c0:c0 + 1024]
            y = taps[0] * w_ref[0:1, c0:c0 + 1024]
            for j in range(1, CONV_WIDTH):
                y = y + taps[j] * w_ref[j:j + 1, c0:c0 + 1024]
            sg = _sigmoid(y)
            act = y * sg
            if sec == 2:
                dact = g_ref[...]
            else:
                scale = GDN_DK ** -0.5 if sec == 0 else 1.0
                parts = []
                for h in range(GDN_HEADS):
                    sl = slice(h * LANES, (h + 1) * LANES)
                    blk = act[:, sl]
                    r = lax.rsqrt(jnp.sum(blk * blk, axis=-1, keepdims=True) + RMS_EPS)
                    n = blk * r
                    dn = g_ref[:, sl] * scale
                    parts.append(r * (dn - n * jnp.sum(dn * n, axis=-1, keepdims=True)))
                dact = jnp.concatenate(parts, axis=-1)
            dy = dact * (sg * (1.0 + y * (1.0 - sg)))
            dyext[0:tm, c0:c0 + 1024] = dy
            for j in range(CONV_WIDTH):
                dw_ref[j:j + 1, c0:c0 + 1024] += jnp.sum(dy * taps[j], axis=0, keepdims=True)
        dyext[tm:tm + CONV_HALO, :] = carry[...]
        carry[...] = dyext[0:CONV_HALO, :]
        for sec in range(3):
            c0 = sec * 1024
            dx = dyext[3:3 + tm, c0:c0 + 1024] * w_ref[0:1, c0:c0 + 1024]
            for j in range(1, CONV_WIDTH):
                dx = dx + dyext[3 - j:3 - j + tm, c0:c0 + 1024] * w_ref[j:j + 1, c0:c0 + 1024]
            dp_ref[:, c0:c0 + 1024] = dx.astype(BF16)
        dp_ref[:, 3072:4096] = dz_ref[...]
        lane = lax.broadcasted_iota(jnp.int32, (tm, LANES), 1)
        dg = jnp.zeros((tm, LANES), F32)
        dbeta = jnp.zeros((tm, LANES), F32)
        for h in range(GDN_HEADS):
            sl = slice(h * LANES, (h + 1) * LANES)
            dg = dg + jnp.where(lane == h, dg_ref[:, sl], 0.0)
            dbeta = dbeta + jnp.where(lane == GDN_HEADS + h, db_ref[:, sl], 0.0)
        ri = lax.broadcasted_iota(jnp.int32, (tm, tm), 0)
        ci = lax.broadcasted_iota(jnp.int32, (tm, tm), 1)
        upper = ((ri // CHUNK) == (ci // CHUNK)) & (ri <= ci)
        dg = _dot(upper.astype(F32), dg, NN, HI)
        pre = ab_ref[...] + dtb_ref[...]
        s = _sigmoid(pre)
        a_exp = jnp.exp(al_ref[...])
        dg_da = dg * (-a_exp * s)
        dp_ref[:, 4096:IN_C_PAD] = (dg_da + dbeta * s * (1.0 - s)).astype(BF16)
        dal_ref[...] += jnp.sum(dg * (-a_exp * _softplus(pre)), axis=0, keepdims=True)
        ddt_ref[...] += jnp.sum(dg_da, axis=0, keepdims=True)

    row = lambda w: pl.BlockSpec((tm, w), lambda s: (rev(s), 0))
    vec = lambda r, w: pl.BlockSpec((r, w), lambda s: (0, 0))
    return pl.pallas_call(
        body, name=name, grid=(nt,),
        in_specs=[row(3072), pl.BlockSpec((CONV_HALO, 3072), lambda s: (jnp.maximum(rev(s) * hb - 1, 0), 0)),
                  pl.BlockSpec((tm, LANES), lambda s: (rev(s), 32)), vec(CONV_WIDTH, 3072), vec(1, LANES), vec(1, LANES),
                  row(1024), row(1024), row(1024), row(1024), row(1024), row(1024)],
        out_specs=[row(IN_C_PAD), vec(CONV_WIDTH, 3072), vec(1, LANES), vec(1, LANES)],
        out_shape=[jax.ShapeDtypeStruct((t, IN_C_PAD), BF16), jax.ShapeDtypeStruct((CONV_WIDTH, 3072), F32),
                   jax.ShapeDtypeStruct((1, LANES), F32), jax.ShapeDtypeStruct((1, LANES), F32)],
        scratch_shapes=[pltpu.VMEM((tm + CONV_HALO, 3072), F32), pltpu.VMEM((tm + CONV_HALO, 3072), F32),
                        pltpu.VMEM((CONV_HALO, 3072), F32), pltpu.VMEM((CONV_WIDTH, tm, 1024), F32)],
        compiler_params=_params(("arbitrary",)),
    )(proj_c, proj_c, proj_c, conv_w, a_log, dt_bias, dq, dk, dv, dgb, dbb, dz)


def _o_gate_fwd(o, proj_c, o_norm, *, name):
    t = o.shape[0]
    tm = _tile(t, 2 * ROW_TILE)

    def body(o_ref, z_ref, g_ref, y_ref):
        for h in range(GDN_HEADS):
            sl = slice(h * LANES, (h + 1) * LANES)
            x = o_ref[:, sl]
            r = lax.rsqrt(jnp.mean(x * x, axis=-1, keepdims=True) + RMS_EPS)
            z = z_ref[:, sl]
            y_ref[:, sl] = (x * r * g_ref[...] * (z * _sigmoid(z))).astype(BF16)

    row = pl.BlockSpec((tm, 1024), lambda i: (i, 0))
    return pl.pallas_call(
        body, name=name, grid=(t // tm,),
        in_specs=[row, pl.BlockSpec((tm, 1024), lambda i: (i, 3)), pl.BlockSpec((1, LANES), lambda i: (0, 0))],
        out_specs=row, out_shape=jax.ShapeDtypeStruct((t, 1024), BF16), compiler_params=_params(("parallel",)),
    )(o, proj_c, o_norm)


def _o_gate_bwd(dy, o, proj_c, o_norm, *, name):
    t = o.shape[0]
    tm = _tile(t, 2 * ROW_TILE)

    def body(dy_ref, o_ref, z_ref, g_ref, do_ref, dz_ref, dg_ref):
        i = pl.program_id(0)

        @pl.when(i == 0)
        def _():
            dg_ref[...] = jnp.zeros_like(dg_ref)

        dg = jnp.zeros((1, LANES), F32)
        for h in range(GDN_HEADS):
            sl = slice(h * LANES, (h + 1) * LANES)
            x = o_ref[:, sl]
            r = lax.rsqrt(jnp.mean(x * x, axis=-1, keepdims=True) + RMS_EPS)
            xh = x * r
            z = z_ref[:, sl]
            sg = _sigmoid(z)
            dyv = dy_ref[:, sl]
            dn = dyv * (z * sg)
            dz_ref[:, sl] = (dyv * xh * g_ref[...] * (sg * (1.0 + z * (1.0 - sg)))).astype(BF16)
            dxh = dn * g_ref[...]
            do_ref[:, sl] = r * (dxh - xh * jnp.mean(dxh * xh, axis=-1, keepdims=True))
            dg = dg + jnp.sum(dn * xh, axis=0, keepdims=True)
        dg_ref[...] += dg

    row = pl.BlockSpec((tm, 1024), lambda i: (i, 0))
    vec = pl.BlockSpec((1, LANES), lambda i: (0, 0))
    return pl.pallas_call(
        body, name=name, grid=(t // tm,), in_specs=[row, row, pl.BlockSpec((tm, 1024), lambda i: (i, 3)), vec],
        out_specs=[row, row, vec],
        out_shape=[jax.ShapeDtypeStruct((t, 1024), F32), jax.ShapeDtypeStruct((t, 1024), BF16),
                   jax.ShapeDtypeStruct((1, LANES), F32)],
        compiler_params=_params(("arbitrary",)),
    )(dy, o, proj_c, o_norm)


PAIR = 2 * CHUNK
GDN_HP = 8


def _bdot(a, b, dims=NN):
    return _dot(a.astype(BF16), b.astype(BF16), dims)


def _each(f, *lists):
    return [f(*args) for args in zip(*lists)]


def _pair_common(q, k, v, gci, gcj, beta):
    ri = lax.broadcasted_iota(jnp.int32, (PAIR, PAIR), 0)
    ci = lax.broadcasted_iota(jnp.int32, (PAIR, PAIR), 1)
    same = (ri // CHUNK) == (ci // CHUNK)
    incl = same & (ri >= ci)
    strict = same & (ri > ci)
    eye = (ri == ci).astype(F32)
    first = lax.broadcasted_iota(jnp.int32, (PAIR, LANES), 0) < CHUNK
    gamma = _each(lambda gi, gj: jnp.where(incl, jnp.exp(jnp.minimum(gi - gj, 0.0)), 0.0), gci, gcj)
    kb = _each(jnp.multiply, k, beta)
    kk = _each(lambda a, b: _bdot(a, b, NT), kb, k)
    qk = _each(lambda a, b: _bdot(a, b, NT), q, k)
    m = _each(lambda x, g: jnp.where(strict, x * g, 0.0), kk, gamma)
    tm_ = _each(lambda x: eye - x, m)
    pw = _each(lambda x: _bdot(x, x), m)
    for it in range(5):
        tm_ = _each(lambda x, p: x + _bdot(x, p), tm_, pw)
        if it < 4:
            pw = _each(lambda p: _bdot(p, p), pw)
    eg = _each(jnp.exp, gci)
    vb = _each(jnp.multiply, v, beta)
    kbe = _each(jnp.multiply, kb, eg)
    uw = _each(lambda x, a, b: _bdot(x, jnp.concatenate([a, b], axis=1)), tm_, vb, kbe)
    attn = _each(lambda x, g: jnp.where(incl, x * g, 0.0), qk, gamma)
    gl_a = _each(lambda g: g[CHUNK - 1:CHUNK, :], gci)
    gl_b = _each(lambda g: g[PAIR - 1:PAIR, :], gci)
    ek = _each(lambda a, b, g: jnp.exp(jnp.where(first, a, b) - g), gl_a, gl_b, gci)
    return dict(incl=incl, strict=strict, gamma=gamma, kb=kb, m=m, tm=tm_, eg=eg, vb=vb, kbe=kbe,
                u=_each(lambda x: x[:, :LANES], uw), w=_each(lambda x: x[:, LANES:], uw), attn=attn,
                qd=_each(jnp.multiply, q, eg), ek=ek, kd=_each(jnp.multiply, k, ek),
                glast_a=_each(jnp.exp, gl_a), glast_b=_each(jnp.exp, gl_b))


def _gdn_specs(t, ts, order):
    nc = ts // CHUNK
    blk = pl.BlockSpec((ts, GDN_HP * LANES), lambda h, s: (order(s), h))
    row = pl.BlockSpec((GDN_HP, 1, ts), lambda h, s: (h, 0, order(s)))
    st = pl.BlockSpec((GDN_HP, nc, LANES, LANES), lambda h, s: (h, order(s), 0, 0))
    return blk, row, st


def _gdn_fwd(q, k, v, gcb, gct, bb, *, name):
    t = q.shape[0]
    ts = _tile(t, GDN_TILE)
    npair = ts // PAIR

    def body(q_ref, k_ref, v_ref, g_ref, gt_ref, b_ref, o_ref, st_ref, s_sc):
        @pl.when(pl.program_id(1) == 0)
        def _():
            s_sc[...] = jnp.zeros_like(s_sc)

        def pair(pi, _):
            rows = pl.ds(pl.multiple_of(pi * PAIR, PAIR), PAIR)
            heads = [slice(hh * LANES, (hh + 1) * LANES) for hh in range(GDN_HP)]
            c = CHUNK
            cat0 = lambda *xs: jnp.concatenate(xs, axis=0)
            s0 = [s_sc[hh] for hh in range(GDN_HP)]
            cm = _pair_common([q_ref[rows, sl] for sl in heads], [k_ref[rows, sl] for sl in heads],
                              [v_ref[rows, sl] for sl in heads], [g_ref[rows, sl] for sl in heads],
                              [gt_ref[hh, :, rows] for hh in range(GDN_HP)], [b_ref[rows, sl] for sl in heads])
            u, w, qd, kd = cm["u"], cm["w"], cm["qd"], cm["kd"]
            r0 = _each(lambda w_, q_, s: _bdot(cat0(w_[:c], q_[:c]), s), w, qd, s0)
            vn_a = _each(lambda u_, r: u_[:c] - r[:c], u, r0)
            s1 = _each(lambda s, gl, k_, vn: s * gl + _bdot(k_[:c], vn, TN), s0, cm["glast_a"], kd, vn_a)
            r1 = _each(lambda w_, q_, s: _bdot(cat0(w_[c:], q_[c:]), s), w, qd, s1)
            vn_b = _each(lambda u_, r: u_[c:] - r[:c], u, r1)
            s2 = _each(lambda s, gl, k_, vn: s * gl + _bdot(k_[c:], vn, TN), s1, cm["glast_b"], kd, vn_b)
            o = _each(lambda ra, rb, at, va, vb_: cat0(ra[c:], rb[c:]) + _bdot(at, cat0(va, vb_)),
                      r0, r1, cm["attn"], vn_a, vn_b)
            for hh, sl in enumerate(heads):
                st_ref[hh, 2 * pi] = s0[hh]
                st_ref[hh, 2 * pi + 1] = s1[hh]
                s_sc[hh] = s2[hh]
                o_ref[rows, sl] = o[hh]
            return 0

        lax.fori_loop(0, npair, pair, 0)

    blk, row, st = _gdn_specs(t, ts, lambda s: s)
    return pl.pallas_call(
        body, name=name, grid=(GDN_HEADS // GDN_HP, t // ts), in_specs=[blk, blk, blk, blk, row, blk],
        out_specs=[blk, st],
        out_shape=[jax.ShapeDtypeStruct((t, 1024), F32), jax.ShapeDtypeStruct((GDN_HEADS, t // CHUNK, LANES, LANES), F32)],
        scratch_shapes=[pltpu.VMEM((GDN_HP, LANES, LANES), F32)],
        compiler_params=_params(("parallel", "arbitrary")),
    )(q, k, v, gcb, gct, bb)


def _gdn_bwd(q, k, v, gcb, gct, bb, do, states, *, name):
    t = q.shape[0]
    ts = _tile(t, GDN_TILE)
    npair = ts // PAIR
    ns = t // ts
    c = CHUNK

    def body(q_ref, k_ref, v_ref, g_ref, gt_ref, b_ref, do_ref, st_ref, dq_ref, dk_ref, dv_ref, dg_ref, db_ref, ds_sc):
        @pl.when(pl.program_id(1) == 0)
        def _():
            ds_sc[...] = jnp.zeros_like(ds_sc)

        rowsum = lambda x: jnp.sum(x, axis=-1, keepdims=True)
        total = lambda x: jnp.sum(rowsum(x), axis=0, keepdims=True)
        cat0 = lambda *xs: jnp.concatenate(xs, axis=0)
        cat1 = lambda *xs: jnp.concatenate(xs, axis=1)

        def pair(step, _):
            pi = npair - 1 - step
            rows = pl.ds(pl.multiple_of(pi * PAIR, PAIR), PAIR)
            heads = [slice(hh * LANES, (hh + 1) * LANES) for hh in range(GDN_HP)]
            hs = range(GDN_HP)
            qv, kv, vv = ([r[rows, sl] for sl in heads] for r in (q_ref, k_ref, v_ref))
            beta = [b_ref[rows, sl] for sl in heads]
            dov = [do_ref[rows, sl] for sl in heads]
            s0 = [st_ref[hh, 2 * pi] for hh in hs]
            s1 = [st_ref[hh, 2 * pi + 1] for hh in hs]
            ds2 = [ds_sc[hh] for hh in hs]
            cm = _pair_common(qv, kv, vv, [g_ref[rows, sl] for sl in heads], [gt_ref[hh, :, rows] for hh in hs], beta)
            u, w, qd, kd, attn = cm["u"], cm["w"], cm["qd"], cm["kd"], cm["attn"]
            tmat, gamma, eg = cm["tm"], cm["gamma"], cm["eg"]
            incl, strict = cm["incl"], cm["strict"]
            vn_a = _each(lambda u_, w_, s: u_[:c] - _bdot(w_[:c], s), u, w, s0)
            vn_b = _each(lambda u_, w_, s: u_[c:] - _bdot(w_[c:], s), u, w, s1)
            vn = _each(cat0, vn_a, vn_b)
            dvn_att = _each(lambda a, d: _bdot(a, d, TN), attn, dov)
            dattn = _each(lambda d, v_: jnp.where(incl, _bdot(d, v_, NT), 0.0), dov, vn)
            dvn_b = _each(lambda x, k_, d: x[c:] + _bdot(k_[c:], d), dvn_att, kd, ds2)
            rb = _each(lambda d, x, s: _bdot(cat0(d[c:], x), s, NT), dov, dvn_b, s1)
            dkd_b = _each(lambda v_, d: _bdot(v_, d, NT), vn_b, ds2)
            dgl_b = _each(lambda d, s: total(d * s), ds2, s1)
            ds1 = _each(lambda d, gl, q_, w_, o_, x: d * gl + _bdot(cat0(q_[c:], w_[c:]), cat0(o_[c:], -x), TN),
                        ds2, cm["glast_b"], qd, w, dov, dvn_b)
            dvn_a = _each(lambda x, k_, d: x[:c] + _bdot(k_[:c], d), dvn_att, kd, ds1)
            ra = _each(lambda d, x, s: _bdot(cat0(d[:c], x), s, NT), dov, dvn_a, s0)
            dkd_a = _each(lambda v_, d: _bdot(v_, d, NT), vn_a, ds1)
            dgl_a = _each(lambda d, s: total(d * s), ds1, s0)
            ds0 = _each(lambda d, gl, q_, w_, o_, x: d * gl + _bdot(cat0(q_[:c], w_[:c]), cat0(o_[:c], -x), TN),
                        ds1, cm["glast_a"], qd, w, dov, dvn_a)
            dvn = _each(cat0, dvn_a, dvn_b)
            dqd = _each(lambda a, b: cat0(a[:c], b[:c]), ra, rb)
            dw = _each(lambda a, b: -cat0(a[c:], b[c:]), ra, rb)
            dkd = _each(cat0, dkd_a, dkd_b)
            dvw = _each(cat1, dvn, dw)
            dvbk = _each(lambda t_, x: _bdot(t_, x, TN), tmat, dvw)
            dvb = _each(lambda x: x[:, :LANES], dvbk)
            dkbe = _each(lambda x: x[:, LANES:], dvbk)
            dt_ = _each(lambda x, a, b: _bdot(x, cat1(a, b), NT), dvw, cm["vb"], cm["kbe"])
            da1 = _each(lambda t_, x: _bdot(t_, x, TN), tmat, dt_)
            dm = _each(lambda x, t_: jnp.where(strict, -_bdot(x, t_, NT), 0.0), da1, tmat)
            dkk = _each(jnp.multiply, dm, gamma)
            dqk = _each(jnp.multiply, dattn, gamma)
            z = _each(lambda a, b, c_, d: a * b + c_ * d, dm, cm["m"], dattn, attn)
            dkb = _each(lambda x, k_, y, e: _bdot(x, k_) + y * e, dkk, kv, dkbe, eg)
            dk = _each(lambda a, b, kb_, q_, x, e, y, be: _bdot(cat0(a, b), cat0(kb_, q_), TN) + x * e + y * be,
                       dkk, dqk, cm["kb"], qv, dkd, cm["ek"], dkb, beta)
            dq = _each(lambda x, k_, y, e: _bdot(x, k_) + y * e, dqk, kv, dqd, eg)

            def colsum_of(z_):
                zh = z_.astype(BF16)
                zl = (z_ - zh.astype(F32)).astype(BF16)
                return _dot(cat0(zh, zl), jnp.ones((2 * PAIR, LANES), BF16), TN)

            colsum = _each(colsum_of, z)
            ri = lax.broadcasted_iota(jnp.int32, (PAIR, LANES), 0)
            for hh, sl in enumerate(heads):
                dkd_kd = dkd[hh] * kd[hh]
                dgc = (rowsum(z[hh]) - colsum[hh] + rowsum(dqd[hh] * qd[hh]) - rowsum(dkd_kd)
                       + rowsum(dkbe[hh] * cm["kbe"][hh]))
                last_a = total(dkd_kd[:c]) + dgl_a[hh] * cm["glast_a"][hh]
                last_b = total(dkd_kd[c:]) + dgl_b[hh] * cm["glast_b"][hh]
                dgc = dgc + jnp.where(ri == c - 1, last_a, 0.0) + jnp.where(ri == PAIR - 1, last_b, 0.0)
                ds_sc[hh] = ds0[hh]
                dq_ref[rows, sl] = dq[hh]
                dk_ref[rows, sl] = dk[hh]
                dv_ref[rows, sl] = dvb[hh] * beta[hh]
                db_ref[rows, sl] = jnp.broadcast_to(rowsum(dkb[hh] * kv[hh]) + rowsum(dvb[hh] * vv[hh]), (PAIR, LANES))
                dg_ref[rows, sl] = dgc
            return 0

        lax.fori_loop(0, npair, pair, 0)

    blk, row, st = _gdn_specs(t, ts, lambda s: ns - 1 - s)
    out = jax.ShapeDtypeStruct((t, 1024), F32)
    return pl.pallas_call(
        body, name=name, grid=(GDN_HEADS // GDN_HP, ns), in_specs=[blk, blk, blk, blk, row, blk, blk, st],
        out_specs=[blk] * 5, out_shape=[out] * 5, scratch_shapes=[pltpu.VMEM((GDN_HP, LANES, LANES), F32)],
        compiler_params=_params(("parallel", "arbitrary")),
    )(q, k, v, gcb, gct, bb, do, states)


def _out_proj_loss(y, w, hres, g, target, *, name):
    t, d = hres.shape
    tm = _tile(t, 2 * ROW_TILE)

    def body(y_ref, w_ref, h_ref, g_ref, t_ref, dh_ref, dhb_ref, dg_ref, loss_ref):
        i = pl.program_id(0)
        x = h_ref[...] + _dot(y_ref[...], w_ref[...])
        r = lax.rsqrt(jnp.mean(x * x, axis=-1, keepdims=True) + RMS_EPS)
        xh = x * r
        err = xh * g_ref[...] - t_ref[...]
        dy = err * (1.0 / d)
        dxh = dy * g_ref[...]
        dh = r * (dxh - xh * jnp.mean(dxh * xh, axis=-1, keepdims=True))
        dh_ref[...] = dh
        dhb_ref[...] = dh.astype(BF16)

        @pl.when(i == 0)
        def _():
            dg_ref[...] = jnp.zeros_like(dg_ref)
            loss_ref[...] = jnp.zeros_like(loss_ref)

        dg_ref[...] += jnp.sum(dy * xh, axis=0, keepdims=True)
        part = 0.5 * jnp.sum(jnp.mean(err * err, axis=-1, keepdims=True), axis=0, keepdims=True)
        loss_ref[...] += jnp.broadcast_to(part, loss_ref.shape)

    row = pl.BlockSpec((tm, d), lambda i: (i, 0))
    vec = pl.BlockSpec((1, d), lambda i: (0, 0))
    return pl.pallas_call(
        body, name=name, grid=(t // tm,),
        in_specs=[pl.BlockSpec((tm, y.shape[1]), lambda i: (i, 0)), pl.BlockSpec(w.shape, lambda i: (0, 0)), row, vec, row],
        out_specs=[row, row, vec, pl.BlockSpec((8, LANES), lambda i: (0, 0))],
        out_shape=[jax.ShapeDtypeStruct((t, d), F32), jax.ShapeDtypeStruct((t, d), BF16),
                   jax.ShapeDtypeStruct((1, d), F32), jax.ShapeDtypeStruct((8, LANES), F32)],
        compiler_params=_params(("arbitrary",)),
    )(y, w, hres, g, target)


def _pad_cols(w, n):
    return jnp.pad(w, ((0, 0), (0, n - w.shape[1])))


def _layout_odd(w):
    return dict(
        winc=_pad_cols(w["w_in_c"], IN_C_PAD).astype(BF16), wout_c=w["w_out_c"].astype(BF16), conv_w=w["conv_w"],
        a_log=_pad_cols(w["a_log"], LANES), dt_bias=_pad_cols(w["dt_bias"], LANES),
        norm_c=w["norm_c"], o_norm=w["o_norm"], final_norm=w["final_norm"],
    )


def _layout_weights(w):
    return {**_layout_even(w), **_layout_odd(w)}


def _layout_even(w):
    z = lambda r, c: jnp.zeros((r, c), w["w_in_ab"].dtype)
    wi = w["w_in_ab"]
    win = jnp.concatenate([wi[:, :384], z(1024, 64), wi[:, 384:416], z(1024, 32), wi[:, 416:]], axis=1)
    wq = jnp.pad(w["w_q_b"].reshape(MLA_Q_RANK, MLA_HEADS, 96), ((0, 0), (0, 0), (0, 32))).reshape(MLA_Q_RANK, 1024)
    kv3 = w["w_kv_b"].reshape(MLA_KV_RANK, MLA_HEADS, 128)
    wk = jnp.pad(kv3[..., :MLA_NOPE], ((0, 0), (0, 0), (0, 64))).reshape(MLA_KV_RANK, 1024)
    wv = kv3[..., MLA_NOPE:].reshape(MLA_KV_RANK, 512)
    pw = w["pool_w"]
    rows = []
    for g in range(4):
        rows.append(jnp.concatenate([pw[g] if j == g else z(128, 128) for j in range(4)], axis=1))
    wpool = jnp.concatenate(rows, axis=0)
    half = MLA_ROPE // 2
    inv = 1.0 / (ROPE_THETA ** (jnp.arange(half, dtype=F32) / half))
    inv_lane = jnp.concatenate([jnp.zeros((MLA_NOPE,), F32), inv, inv, jnp.zeros((32,), F32)]).reshape(1, LANES)
    return dict(
        win=win.astype(BF16), wq=wq.astype(BF16), wk=wk.astype(BF16), wv=wv.astype(BF16), wpool=wpool.astype(BF16),
        wout_ab=w["w_out_ab"].astype(BF16), inv_lane=inv_lane,
        norm_ab=w["norm_ab"], q_a_norm=w["q_a_norm"], kv_a_norm=w["kv_a_norm"], pool_scale=w["pool_scale"],
    )


def _unlayout_grads(g, names):
    out = {}
    for name in names:
        if name == "w_in_ab":
            dwin = g["win"]
            out[name] = jnp.concatenate([dwin[:, :384], dwin[:, 448:480], dwin[:, 512:]], axis=1)
        elif name == "w_q_b":
            out[name] = g["wq"].reshape(MLA_Q_RANK, MLA_HEADS, 128)[..., :96].reshape(MLA_Q_RANK, 768)
        elif name == "w_kv_b":
            out[name] = jnp.concatenate([g["wk"].reshape(MLA_KV_RANK, MLA_HEADS, 128)[..., :MLA_NOPE],
                                         g["wv"].reshape(MLA_KV_RANK, MLA_HEADS, MLA_V)], axis=-1).reshape(MLA_KV_RANK, 1024)
        elif name == "w_in_c":
            out[name] = g["winc"][:, :4112]
        else:
            out[name] = g[{"w_out_ab": "wout_ab", "w_out_c": "wout_c"}[name]]
    return out


def _local_step(x, pos, target, lw, odd_weights=None, on_grads=None):
    mm = _matmul
    hn = _rms_fwd(x, lw["norm_ab"], name="rms_ab")
    proj = mm(hn, lw["win"], "nn", name="in_ab")
    q, k, v, ybraw, qn, kvn, d, cos_t, sin_t = _ab_prep(
        proj, pos, lw["inv_lane"], lw["q_a_norm"], lw["kv_a_norm"], lw["wq"], lw["wk"], lw["wv"], lw["wpool"], name="ab_prep")
    o, lse = _attn_fwd(q, k, v, name="attn_fwd")
    h1, y = _gate_out_proj(o, ybraw, proj, lw["pool_scale"], lw["wout_ab"], x, name="gate_out_ab")
    lo = lw if odd_weights is None else odd_weights(h1)
    hn1 = _rms_fwd(h1, lo["norm_c"], name="rms_c")
    proj_c = mm(hn1, lo["winc"], "nn", name="in_c")
    q2, k2, v2, gb, bb, gt = _c_prep(proj_c, lo["conv_w"], lo["a_log"], lo["dt_bias"], name="c_prep")
    gt = gt.reshape(GDN_HEADS, 1, gt.shape[1])
    o2, states = _gdn_fwd(q2, k2, v2, gb, gt, bb, name="gdn_fwd")
    y2 = _o_gate_fwd(o2, proj_c, lo["o_norm"], name="gate_c")
    dh2, dh2b, d_final, loss = _out_proj_loss(y2, lo["wout_c"], h1, lo["final_norm"], target, name="out_c_loss")
    g = {"final_norm": d_final}
    dy2 = mm(dh2b, lo["wout_c"], "nt", name="out_c_dx")
    g["wout_c"] = mm(y2, dh2b, "tn", name="out_c_dw")
    do2, dz2, g["o_norm"] = _o_gate_bwd(dy2, o2, proj_c, lo["o_norm"], name="gate_c_bwd")
    dq2, dk2, dv2, dgb, dbb = _gdn_bwd(q2, k2, v2, gb, gt, bb, do2, states, name="gdn_bwd")
    dproj_c, g["conv_w"], g["a_log"], g["dt_bias"] = _c_prep_bwd(
        proj_c, lo["conv_w"], lo["a_log"], lo["dt_bias"], dq2, dk2, dv2, dgb, dbb, dz2, name="c_prep_bwd")
    g["winc"] = mm(hn1, dproj_c, "tn", name="in_c_dw")
    notify = (lambda tag: 0.0) if on_grads is None else (lambda tag: on_grads(tag, g))
    pool_scale = lw["pool_scale"] + notify("odd")
    dh1, dh1b, g["norm_c"] = _matmul_rms_bwd(dproj_c, lo["winc"], h1, lo["norm_c"], dh2, name="in_c_dx_rms", with_bf16=True)
    dy = mm(dh1b, lw["wout_ab"], "nt", name="out_ab_dx")
    g["wout_ab"] = mm(y, dh1b, "tn", name="out_ab_dw")
    pool_scale = pool_scale + notify("out_ab")
    do, delta, dyb, dz, g["pool_scale"] = _gate_bwd(dy, o, ybraw, proj, pool_scale, name="gate_ab_bwd")
    dq, dk, dv = _attn_bwd(q, k, v, do, lse, delta, name="attn_bwd")
    dproj, dqraw, dkb, g["q_a_norm"], g["kv_a_norm"] = _ab_prep_bwd(
        proj, lw["q_a_norm"], lw["kv_a_norm"], dq, dk, dv, cos_t, sin_t, dyb, dz,
        lw["wq"], lw["wk"], lw["wv"], lw["wpool"], name="ab_prep_bwd")
    g["wpool"] = mm(d, dyb, "tn", name="pool_mix_dw")
    g["wq"] = mm(qn, dqraw, "tn", name="q_up_dw")
    g["wk"] = mm(kvn, dkb, "tn", name="k_up_dw")
    g["wv"] = mm(kvn, dv, "tn", name="v_up_dw")
    g["win"] = mm(hn, dproj, "tn", name="in_ab_dw")
    norm_ab = lw["norm_ab"] + notify("in_ab")
    dx, g["norm_ab"] = _matmul_rms_bwd(dproj, lw["win"], x, norm_ab, dh1, name="in_ab_dx_rms", with_bf16=False)
    return loss, dx, g


_HBM = pl.BlockSpec(memory_space=pltpu.HBM)


def _place():
    return lax.axis_index("x"), lax.axis_index("y"), lax.axis_index("c")


def _flip(v, f):
    return 1 - v if f else v


_CHIP_FLIPS = ((1, 0), (0, 1), (1, 1))
_DEV_FLIPS = tuple((fx, fy, fc) for fx in (0, 1) for fy in (0, 1) for fc in (0, 1) if fx or fy or fc)


def _rcopy(src, dst, send_sems, recv_sems, k, to):
    return pltpu.make_async_remote_copy(src_ref=src, dst_ref=dst, send_sem=send_sems.at[k], recv_sem=recv_sems.at[k],
                                        device_id=to, device_id_type=MESH)


def _my_half(ref, c, axis):
    rh = ref.shape[axis] // 2
    idx = [slice(None)] * len(ref.shape)
    idx[axis] = pl.ds(c * rh, rh)
    return ref.at[tuple(idx)]


def _gather_weights(bigs, smalls):
    nb, ns = len(bigs), len(smalls)

    def body(*refs):
        ins, outs = refs[:nb + ns], refs[nb + ns:2 * (nb + ns)]
        send_sems, recv_sems, local_sems = refs[2 * (nb + ns):]
        x, y, c = _place()
        j0 = 2 * x + y
        sib = (x, y, 1 - c)
        chips = [(_flip(x, fx), _flip(y, fy)) for fx, fy in _CHIP_FLIPS]
        local = [pltpu.make_async_copy(i_ref, o_ref.at[j0], local_sems.at[a])
                 for a, (i_ref, o_ref) in enumerate(zip(ins, outs))]
        for cp in local:
            cp.start()
        sends = []
        for k, (px, py) in enumerate(chips):
            for a in range(nb):
                sends.append(_rcopy(_my_half(ins[a], c, 0), _my_half(outs[a].at[j0], c, 0), send_sems, recv_sems,
                                    6 * a + k, (px, py, c)))
            for s in range(ns):
                sends.append(_rcopy(ins[nb + s], outs[nb + s].at[j0], send_sems, recv_sems, 6 * nb + 3 * s + k, (px, py, c)))
        for cp in sends:
            cp.start()
        for k, (px, py) in enumerate(chips):
            jk = 2 * px + py
            for a in range(nb):
                landed = _my_half(outs[a].at[jk], c, 0)
                _rcopy(landed, landed, send_sems, recv_sems, 6 * a + k, (px, py, c)).wait_recv()
                fwd = _rcopy(landed, landed, send_sems, recv_sems, 6 * a + 3 + k, sib)
                fwd.start()
                sends.append(fwd)
        for k, (px, py) in enumerate(chips):
            jk = 2 * px + py
            for a in range(nb):
                other = _my_half(outs[a].at[jk], 1 - c, 0)
                _rcopy(other, other, send_sems, recv_sems, 6 * a + 3 + k, sib).wait_recv()
            for s in range(ns):
                _rcopy(ins[nb + s], outs[nb + s].at[jk], send_sems, recv_sems, 6 * nb + 3 * s + k, (px, py, c)).wait_recv()
        for cp in sends:
            cp.wait_send()
        for cp in local:
            cp.wait()

    arrays = list(bigs) + list(smalls)
    n_sem = 6 * nb + 3 * ns
    return pl.pallas_call(
        body, name="gather_weights", in_specs=[_HBM] * len(arrays), out_specs=[_HBM] * len(arrays),
        out_shape=[jax.ShapeDtypeStruct((4,) + a.shape, a.dtype) for a in arrays],
        scratch_shapes=[pltpu.SemaphoreType.DMA((n_sem,)), pltpu.SemaphoreType.DMA((n_sem,)),
                        pltpu.SemaphoreType.DMA((len(arrays),))],
    )(*arrays)


def _core_swap_partial(gs, *, name):
    n = len(gs)

    def body(*refs):
        ins, outs = refs[:n], refs[n:2 * n]
        send_sems, recv_sems = refs[2 * n:]
        x, y, c = _place()
        copies = [_rcopy(_my_half(i_ref, 1 - c, 1), o_ref, send_sems, recv_sems, a, (x, y, 1 - c))
                  for a, (i_ref, o_ref) in enumerate(zip(ins, outs))]
        for cp in copies:
            cp.start()
        for cp in copies:
            cp.wait()

    return pl.pallas_call(
        body, name=name, in_specs=[_HBM] * n, out_specs=[_HBM] * n,
        out_shape=[jax.ShapeDtypeStruct((4, g.shape[1] // 2, g.shape[2]), g.dtype) for g in gs],
        scratch_shapes=[pltpu.SemaphoreType.DMA((n,)), pltpu.SemaphoreType.DMA((n,))],
    )(*gs)


def _core_swap_sum(fs):
    n = len(fs)

    def body(*refs):
        ins, outs = refs[:n], refs[n:2 * n]
        send_sems, recv_sems = refs[2 * n:]
        x, y, c = _place()
        copies = [_rcopy(_my_half(i_ref, c, 0), _my_half(o_ref, c, 0), send_sems, recv_sems, a, (x, y, 1 - c))
                  for a, (i_ref, o_ref) in enumerate(zip(ins, outs))]
        for cp in copies:
            cp.start()
        for a, cp in enumerate(copies):
            cp.wait_send()
            theirs = _my_half(outs[a], 1 - c, 0)
            _rcopy(theirs, theirs, send_sems, recv_sems, a, (x, y, 1 - c)).wait_recv()

    return pl.pallas_call(
        body, name="core_swap_sum", in_specs=[_HBM] * n, out_specs=[_HBM] * n,
        out_shape=[jax.ShapeDtypeStruct(f.shape, f.dtype) for f in fs],
        input_output_aliases={a: a for a in range(n)},
        scratch_shapes=[pltpu.SemaphoreType.DMA((n,)), pltpu.SemaphoreType.DMA((n,))],
    )(*fs)


_SEM = pl.BlockSpec(memory_space=pltpu.SEMAPHORE)
_ANY = pl.BlockSpec(memory_space=pl.ANY)
_DATAFLOW = pltpu.SideEffectType.DATAFLOW_SIDE_EFFECTING


def _to_chips_copies(srcs, lands, send_sems, recv_sems, per_chip_slot):
    x, y, c = _place()
    j0 = 2 * x + y
    out = []
    for k, (fx, fy) in enumerate(_CHIP_FLIPS):
        px, py = _flip(x, fx), _flip(y, fy)
        jk = 2 * px + py
        for a, (src, land) in enumerate(zip(srcs, lands)):
            piece = src.at[jk] if per_chip_slot else src
            out.append((_rcopy(piece, land.at[j0], send_sems, recv_sems, 3 * a + k, (px, py, c)),
                        _rcopy(piece, land.at[jk], send_sems, recv_sems, 3 * a + k, (px, py, c))))
    return out


def _to_chips_start(arrays, *, per_chip_slot, name):
    n = len(arrays)
    lands = [lax.empty((4,) + (a.shape[1:] if per_chip_slot else a.shape), a.dtype) for a in arrays]

    def body(*refs):
        srcs, land_refs, send_sems, recv_sems, token = refs[:n], refs[n:2 * n], refs[2 * n], refs[2 * n + 1], refs[-1]
        for send, _ in _to_chips_copies(srcs, land_refs, send_sems, recv_sems, per_chip_slot):
            send.start()
        token[...] = jnp.zeros_like(token)

    held = [pltpu.with_memory_space_constraint(a, pltpu.HBM) for a in list(arrays) + lands]
    return pl.pallas_call(
        body, name=name, in_specs=[_HBM] * (2 * n),
        out_specs=(_SEM, _SEM, *[_HBM] * (2 * n), pl.BlockSpec(memory_space=pltpu.VMEM)),
        out_shape=(pltpu.SemaphoreType.DMA((3 * n,)), pltpu.SemaphoreType.DMA((3 * n,)),
                   *[pltpu.HBM(a.shape, a.dtype) for a in held], jax.ShapeDtypeStruct((8, LANES), F32)),
        input_output_aliases={i: 2 + i for i in range(2 * n)},
        compiler_params=pltpu.CompilerParams(has_side_effects=_DATAFLOW),
    )(*held)


def _to_chips_wait(started, after, *, per_chip_slot, name):
    send_sems, recv_sems, held = started[0], started[1], started[2:-1]
    n = len(held) // 2

    def body(*refs):
        srcs, land_refs, s_sems, r_sems = refs[:n], refs[n:2 * n], refs[2 * n], refs[2 * n + 1]
        for send, arrival in _to_chips_copies(srcs, land_refs, s_sems, r_sems, per_chip_slot):
            send.wait_send()
            arrival.wait_recv()

    out = pl.pallas_call(
        body, name=name, in_specs=[_HBM] * (2 * n) + [_SEM, _SEM, _ANY], out_specs=[_HBM] * (2 * n),
        out_shape=[pltpu.HBM(a.shape, a.dtype) for a in held],
        input_output_aliases={i: i for i in range(2 * n)},
        compiler_params=pltpu.CompilerParams(has_side_effects=_DATAFLOW),
    )(*held, send_sems, recv_sems, after)
    return out[n:]


def _chip_exchange(ps, small):
    n = len(ps)
    rs = small.shape[0]

    def body(*refs):
        p_refs, s_ref = refs[:n], refs[n]
        l_refs, ls_ref = refs[n + 1:2 * n + 1], refs[2 * n + 1]
        send_sems, recv_sems, local_sems = refs[2 * n + 2:]
        x, y, c = _place()
        j0 = 2 * x + y
        d0 = 2 * j0 + c
        local = [pltpu.make_async_copy(p.at[j0], l.at[j0], local_sems.at[a]) for a, (p, l) in enumerate(zip(p_refs, l_refs))]
        local.append(pltpu.make_async_copy(s_ref, ls_ref.at[d0], local_sems.at[n]))
        for cp in local:
            cp.start()
        sends = []
        for k, (fx, fy) in enumerate(_CHIP_FLIPS):
            px, py = _flip(x, fx), _flip(y, fy)
            for a in range(n):
                sends.append(_rcopy(p_refs[a].at[2 * px + py], l_refs[a].at[j0], send_sems, recv_sems, 3 * a + k, (px, py, c)))
        for k, (fx, fy, fc) in enumerate(_DEV_FLIPS):
            peer = (_flip(x, fx), _flip(y, fy), _flip(c, fc))
            sends.append(_rcopy(s_ref, ls_ref.at[d0], send_sems, recv_sems, 3 * n + k, peer))
        for cp in sends:
            cp.start()
        for k, (fx, fy) in enumerate(_CHIP_FLIPS):
            px, py = _flip(x, fx), _flip(y, fy)
            for a in range(n):
                _rcopy(p_refs[a].at[j0], l_refs[a].at[2 * px + py], send_sems, recv_sems, 3 * a + k, (px, py, c)).wait_recv()
        for k, (fx, fy, fc) in enumerate(_DEV_FLIPS):
            px, py, pc = _flip(x, fx), _flip(y, fy), _flip(c, fc)
            _rcopy(s_ref, ls_ref.at[4 * px + 2 * py + pc], send_sems, recv_sems, 3 * n + k, (px, py, pc)).wait_recv()
        for cp in sends:
            cp.wait_send()
        for cp in local:
            cp.wait()

    n_sem = 3 * n + 7
    return pl.pallas_call(
        body, name="chip_exchange", in_specs=[_HBM] * (n + 1), out_specs=[_HBM] * (n + 1),
        out_shape=[jax.ShapeDtypeStruct(p.shape, F32) for p in ps] + [jax.ShapeDtypeStruct((8, rs, LANES), F32)],
        scratch_shapes=[pltpu.SemaphoreType.DMA((n_sem,)), pltpu.SemaphoreType.DMA((n_sem,)),
                        pltpu.SemaphoreType.DMA((n + 1,))],
    )(*ps, small)


def _core_sum(g, part, core, *, name):
    _, rh, cols = part.shape
    tr = _tile(rh, 256)
    nb = rh // tr

    def body(c_ref, g_ref, p_ref, o_ref):
        o_ref[...] = g_ref[...] + p_ref[...]

    grid_spec = pltpu.PrefetchScalarGridSpec(
        num_scalar_prefetch=1, grid=(4, nb),
        in_specs=[pl.BlockSpec((1, tr, cols), lambda j, i, c: (j, c[0] * nb + i, 0)),
                  pl.BlockSpec((1, tr, cols), lambda j, i, c: (j, i, 0))],
        out_specs=pl.BlockSpec((1, tr, cols), lambda j, i, c: (j, i, 0)),
    )
    return pl.pallas_call(
        body, name=name, grid_spec=grid_spec, out_shape=jax.ShapeDtypeStruct(part.shape, F32),
        compiler_params=_params(("parallel", "parallel")),
    )(core, g, part)


def _chip_sum(landed, core, *, name):
    _, rh, cols = landed.shape
    tr = _tile(rh, 256)
    nb = rh // tr

    def body(c_ref, l_ref, o_ref):
        o_ref[...] = ((l_ref[0] + l_ref[1]) + l_ref[2]) + l_ref[3]

    grid_spec = pltpu.PrefetchScalarGridSpec(
        num_scalar_prefetch=1, grid=(nb,),
        in_specs=[pl.BlockSpec((4, tr, cols), lambda i, c: (0, i, 0))],
        out_specs=pl.BlockSpec((tr, cols), lambda i, c: (c[0] * nb + i, 0)),
    )
    return pl.pallas_call(
        body, name=name, grid_spec=grid_spec, out_shape=jax.ShapeDtypeStruct((2 * rh, cols), F32),
        compiler_params=_params(("parallel",)),
    )(core, landed)


_ROW_POOL_W, _ROW_NORM_AB, _ROW_FINAL, _ROW_POOL_SCALE, _ROW_Q_NORM = 0, 512, 520, 528, 532
_ROW_KV_NORM, _ROW_O_NORM, _ROW_A_LOG, _ROW_DT_BIAS, _ROW_LOSS = 534, 535, 536, 537, 538
_ROW_CONV, _ROW_NORM_C, _SMALL_ROWS = 544, 640, 672
_CONV_ROWS = CONV_WIDTH * 6


def _put_rows(dst_ref, row0, src, width):
    for r in range(width // LANES):
        dst_ref[row0 + r:row0 + r + 1, :] = src[:, r * LANES:(r + 1) * LANES]


def _pack_small(g, loss_tile):
    names = ("wpool", "norm_ab", "final_norm", "pool_scale", "q_a_norm", "kv_a_norm", "o_norm", "a_log", "dt_bias",
             "conv_w", "norm_c")

    def body(wpool, norm_ab, final_norm, pool_scale, q_norm, kv_norm, o_norm, a_log, dt_bias, conv_w, norm_c, loss, o_ref):
        o_ref[...] = jnp.zeros_like(o_ref)
        for gi in range(4):
            o_ref[_ROW_POOL_W + gi * 128:_ROW_POOL_W + (gi + 1) * 128, :] = wpool[gi * 128:(gi + 1) * 128, gi * 128:(gi + 1) * 128]
        _put_rows(o_ref, _ROW_NORM_AB, norm_ab[...], 1024)
        _put_rows(o_ref, _ROW_FINAL, final_norm[...], 1024)
        _put_rows(o_ref, _ROW_POOL_SCALE, pool_scale[...], 512)
        _put_rows(o_ref, _ROW_Q_NORM, q_norm[...], 256)
        for row, ref in ((_ROW_KV_NORM, kv_norm), (_ROW_O_NORM, o_norm), (_ROW_A_LOG, a_log), (_ROW_DT_BIAS, dt_bias)):
            o_ref[row:row + 1, :] = ref[...]
        o_ref[_ROW_LOSS:_ROW_LOSS + 1, :] = loss[0:1, :]
        for j in range(4):
            for r in range(CONV_WIDTH):
                _put_rows(o_ref, _ROW_CONV + j * _CONV_ROWS + r * 6, conv_w[r:r + 1, j * 768:(j + 1) * 768], 768)
            _put_rows(o_ref, _ROW_NORM_C + j * 8, norm_c[:, j * 256:(j + 1) * 256], 256)

    vmem = pl.BlockSpec(memory_space=pltpu.VMEM)
    return pl.pallas_call(
        body, name="pack_small", in_specs=[vmem] * 12, out_specs=vmem,
        out_shape=jax.ShapeDtypeStruct((_SMALL_ROWS, LANES), F32),
    )(*[g[n] for n in names], loss_tile)


_SMALL_NAMES = ("pool_w", "norm_ab", "final_norm", "pool_scale", "q_a_norm", "kv_a_norm", "o_norm", "a_log", "dt_bias",
                "conv_w", "norm_c")


def _take_rows(src, row0, width):
    return jnp.concatenate([src[row0 + r:row0 + r + 1, :] for r in range(width // LANES)], axis=1)


def _small_update(small_all, ws, ms, vs):
    n = len(_SMALL_NAMES)

    def body(*refs):
        a_ref = refs[0]
        w_refs, m_refs, v_refs = refs[1:1 + n], refs[1 + n:1 + 2 * n], refs[1 + 2 * n:1 + 3 * n]
        outs = refs[1 + 3 * n:1 + 7 * n]
        loss_ref, tot = refs[1 + 7 * n], refs[2 + 7 * n]
        acc = a_ref[0]
        for d in range(1, 8):
            acc = acc + a_ref[d]
        tot[...] = acc
        x, y, _ = _place()
        j0 = 2 * x + y
        conv = tot[pl.ds(pl.multiple_of(_ROW_CONV + j0 * _CONV_ROWS, 8), _CONV_ROWS), :]
        norm_c = tot[pl.ds(pl.multiple_of(_ROW_NORM_C + j0 * 8, 8), 8), :]
        whole = tot[_ROW_NORM_AB:_ROW_CONV, :]
        at = lambda row: row - _ROW_NORM_AB
        grads = {
            "norm_ab": _take_rows(whole, at(_ROW_NORM_AB), 1024), "final_norm": _take_rows(whole, at(_ROW_FINAL), 1024),
            "pool_scale": _take_rows(whole, at(_ROW_POOL_SCALE), 512), "q_a_norm": _take_rows(whole, at(_ROW_Q_NORM), 256),
            "kv_a_norm": whole[at(_ROW_KV_NORM):at(_ROW_KV_NORM) + 1, :], "o_norm": whole[at(_ROW_O_NORM):at(_ROW_O_NORM) + 1, :],
            "a_log": tot[_ROW_A_LOG:_ROW_A_LOG + 1, 0:GDN_HEADS],
            "dt_bias": tot[_ROW_DT_BIAS:_ROW_DT_BIAS + 1, 0:GDN_HEADS],
            "norm_c": _take_rows(norm_c, 0, 256),
        }
        loss_ref[...] = whole[at(_ROW_LOSS):at(_ROW_LOSS) + 1, :]
        for i, name in enumerate(_SMALL_NAMES):
            g_out = outs[4 * i]
            if name == "pool_w":
                for gi in range(4):
                    g_out[gi] = tot[_ROW_POOL_W + gi * 128:_ROW_POOL_W + (gi + 1) * 128, :]
            elif name == "conv_w":
                for r in range(CONV_WIDTH):
                    g_out[r:r + 1, :] = _take_rows(conv, r * 6, 768)
            else:
                g_out[...] = grads[name]
            _adam_update(g_out, w_refs[i], m_refs[i], v_refs[i], *outs[4 * i + 1:4 * i + 4])

    vmem = pl.BlockSpec(memory_space=pltpu.VMEM)
    out_shape = [jax.ShapeDtypeStruct(w.shape, F32) for w in ws for _ in range(4)] + [jax.ShapeDtypeStruct((1, LANES), F32)]
    return pl.pallas_call(
        body, name="small_update", in_specs=[vmem] * (1 + 3 * n), out_specs=[vmem] * (4 * n + 1), out_shape=out_shape,
        scratch_shapes=[pltpu.VMEM((_SMALL_ROWS, LANES), F32)],
        compiler_params=pltpu.CompilerParams(vmem_limit_bytes=VMEM_LIMIT),
    )(small_all, *ws, *ms, *vs)


def _adam_update(g_ref, w_ref, m_ref, v_ref, d_ref, mo_ref, vo_ref):
    gv = g_ref[...]
    mn = ADAM_B1 * m_ref[...] + (1.0 - ADAM_B1) * gv
    vn = ADAM_B2 * v_ref[...] + (1.0 - ADAM_B2) * (gv * gv)
    mo_ref[...] = mn
    vo_ref[...] = vn
    c1 = 1.0 - ADAM_B1 ** ADAM_STEP
    c2 = 1.0 - ADAM_B2 ** ADAM_STEP
    d_ref[...] = -ADAM_LR * ((mn / c1) / (jnp.sqrt(vn / c2) + ADAM_EPS) + ADAM_WD * w_ref[...])


def _adamw_rows(g, w, m, v, *, name):
    rows, cols = g.shape
    tr = _tile(rows, 512)

    def body(*refs):
        _adam_update(*refs)

    blk = pl.BlockSpec((tr, cols), lambda i: (i, 0))
    out = jax.ShapeDtypeStruct((rows, cols), F32)
    return pl.pallas_call(
        body, name=name, grid=(rows // tr,), in_specs=[blk] * 4, out_specs=[blk] * 3, out_shape=[out] * 3,
        compiler_params=_params(("parallel",)),
    )(g, w, m, v)


_ADAM_ROWWISE = ("w_in_ab", "w_q_b", "w_kv_b", "w_out_ab", "w_in_c", "w_out_c")


_SHARD_AXIS = {"w_in_ab": 1, "w_q_b": 1, "w_kv_b": 1, "w_out_ab": 0, "w_in_c": 1, "w_out_c": 0, "conv_w": 1, "norm_c": 1}
_ALL_NAMES = ("norm_ab", "w_in_ab", "q_a_norm", "w_q_b", "kv_a_norm", "w_kv_b", "pool_w", "pool_scale", "w_out_ab",
              "norm_c", "w_in_c", "conv_w", "a_log", "dt_bias", "o_norm", "w_out_c", "final_norm")


def _join_shards(a, axis):
    _, r, c = a.shape
    return a.reshape(4 * r, c) if axis == 0 else jnp.transpose(a, (1, 0, 2)).reshape(r, 4 * c)


def _split_shards(a, axis):
    r, c = a.shape
    return a.reshape(4, r // 4, c) if axis == 0 else jnp.transpose(a.reshape(r, 4, c // 4), (1, 0, 2))


def kernel(x, positions, norm_ab, w_in_ab, q_a_norm, w_q_b, kv_a_norm, w_kv_b, pool_w, pool_scale, w_out_ab, norm_c, w_in_c, conv_w, a_log, dt_bias, o_norm, w_out_c, final_norm, loss_target, m_norm_ab, m_w_in_ab, m_q_a_norm, m_w_q_b, m_kv_a_norm, m_w_kv_b, m_pool_w, m_pool_scale, m_w_out_ab, m_norm_c, m_w_in_c, m_conv_w, m_a_log, m_dt_bias, m_o_norm, m_w_out_c, m_final_norm, v_norm_ab, v_w_in_ab, v_q_a_norm, v_w_q_b, v_kv_a_norm, v_w_kv_b, v_pool_w, v_pool_scale, v_w_out_ab, v_norm_c, v_w_in_c, v_conv_w, v_a_log, v_dt_bias, v_o_norm, v_w_out_c, v_final_norm):
    given = dict(locals())
    c = lax.axis_index("c")
    t = x.shape[1]

    def shard_of(prefix, name):
        a = given[prefix + name]
        return a.reshape(a.shape[1:]) if a.ndim > 2 else a.reshape(1, -1)

    big, big_even, big_odd, small_sharded = _ADAM_ROWWISE, _ADAM_ROWWISE[:4], _ADAM_ROWWISE[4:], ("conv_w", "norm_c")
    chip = 2 * lax.axis_index("x") + lax.axis_index("y")
    core = c.astype(jnp.int32).reshape(1)
    late = big_odd + small_sharded
    late_shards = [shard_of("", n).astype(BF16) for n in big_odd] + [shard_of("", n) for n in small_sharded]
    gather_odd = _to_chips_start(late_shards, per_chip_slot=False, name="gather_odd_start")
    gathered = _gather_weights([shard_of("", n).astype(BF16) for n in big_even], [])
    full = {n: _join_shards(a, _SHARD_AXIS[n]) for n, a in zip(big_even, gathered)}
    for name in ("norm_ab", "q_a_norm", "kv_a_norm", "pool_w", "pool_scale"):
        full[name] = shard_of("", name)
    lw = _layout_even(full)
    lw["norm_ab"] = lw["norm_ab"] + gather_odd[-1][0, 0]

    def odd_weights(h1):
        landed = _to_chips_wait(gather_odd, h1, per_chip_slot=False, name="gather_odd_wait")
        w = {}
        for name, land, own in zip(late, landed, late_shards):
            w[name] = _join_shards(lax.dynamic_update_index_in_dim(land, own, chip, 0), _SHARD_AXIS[name])
        for name in ("a_log", "dt_bias", "o_norm", "final_norm"):
            w[name] = shard_of("", name)
        return _layout_odd(w)

    def chip_partials(names, grads, tag):
        slots = [_split_shards(grads[n], _SHARD_AXIS[n]) for n in names]
        partial = _core_swap_partial(slots, name="core_swap_partial_" + tag)
        return [_core_sum(s, p, core, name="core_sum_" + n) for n, s, p in zip(names, slots, partial)]

    groups = {"odd": big_odd, "out_ab": ("w_out_ab",), "in_ab": ("w_in_ab", "w_q_b", "w_kv_b")}
    sent = {}

    def on_grads(tag, g):
        part = chip_partials(groups[tag], _unlayout_grads(g, groups[tag]), tag)
        sent[tag] = (part, _to_chips_start(part, per_chip_slot=True, name="exchange_" + tag + "_start"))
        return sent[tag][1][-1][0, 0]

    loss_tile, dx, g = _local_step(x[0], positions.reshape(t, 1), loss_target[0], lw, odd_weights, on_grads)
    small_all = _chip_exchange([], _pack_small(g, loss_tile))[-1]
    halves = {}
    for tag, names in groups.items():
        part, started = sent[tag]
        landed = _to_chips_wait(started, small_all, per_chip_slot=True, name="exchange_" + tag + "_wait")
        for n, l, p in zip(names, landed, part):
            l = lax.dynamic_update_index_in_dim(l, lax.dynamic_index_in_dim(p, chip, 0, keepdims=False), chip, 0)
            halves[n] = _chip_sum(l, core, name="chip_sum_" + n)
    gbig = dict(zip(big, _core_swap_sum([halves[n] for n in big])))

    res = {}
    for name in big:
        res["grad", name] = gbig[name]
        out = _adamw_rows(gbig[name], shard_of("", name), shard_of("m_", name), shard_of("v_", name), name="adamw_" + name)
        res["delta", name], res["m", name], res["v", name] = out
    out = _small_update(small_all, [shard_of("", n) for n in _SMALL_NAMES], [shard_of("m_", n) for n in _SMALL_NAMES],
                        [shard_of("v_", n) for n in _SMALL_NAMES])
    for i, name in enumerate(_SMALL_NAMES):
        res["grad", name], res["delta", name], res["m", name], res["v", name] = out[4 * i:4 * i + 4]
    res = {k: a.reshape(given[k[1]].shape) for k, a in res.items()}
    loss = out[-1][0, 0]
    outs = [loss, dx.reshape(x.shape)]
    for key in ("grad", "delta", "m", "v"):
        outs += [res[key, n] for n in _ALL_NAMES]
    return tuple(outs)
```

```python
import functools

import jax
import jax.numpy as jnp
from jax import lax
from jax.experimental import pallas as pl
from jax.experimental.pallas import tpu as pltpu

F32 = jnp.float32
BF16 = jnp.bfloat16
HI = lax.Precision.HIGHEST
MESH = pl.DeviceIdType.MESH

RMS_EPS = 1e-6
MLA_HEADS = 8
MLA_Q_RANK = 256
MLA_KV_RANK = 128
MLA_NOPE = 64
MLA_ROPE = 32
MLA_V = 64
ROPE_THETA = 10000.0
POOL_WINDOWS = (2, 4, 8, 16)
POOL_GROUP = 128
POOL_WIDTH = 512
POOL_HALO = 16
GDN_HEADS = 8
GDN_DK = 128
CONV_WIDTH = 4
CONV_HALO = 8
CHUNK = 64
IN_AB_PAD = 2048
IN_C_PAD = 4224
ATT_SCALE = (MLA_NOPE + MLA_ROPE) ** -0.5

ADAM_LR = 0.001
ADAM_B1 = 0.9
ADAM_B2 = 0.999
ADAM_EPS = 1e-08
ADAM_WD = 0.01
ADAM_STEP = 10

LANES = 128
VMEM_LIMIT = 56 * 1024 * 1024

ROW_TILE = 256
ATT_TILE = 1024
GDN_TILE = 256
MM_TILE = (1024, 1408, 2048)

NN = (((1,), (0,)), ((), ()))
NT = (((1,), (1,)), ((), ()))
TN = (((0,), (0,)), ((), ()))


def _dot(a, b, dims=NN, prec=None):
    return lax.dot_general(a, b, dims, precision=prec, preferred_element_type=F32)


def _tile(n, pref):
    if n <= pref:
        return n
    step = LANES if pref >= LANES else 8
    for t in range(pref - pref % step, 0, -step):
        if n % t == 0:
            return t
    return n


def _params(sem):
    return pltpu.CompilerParams(dimension_semantics=sem, vmem_limit_bytes=VMEM_LIMIT)


def _sigmoid(x):
    return 0.5 * jnp.tanh(0.5 * x) + 0.5


def _softplus(x):
    return jnp.maximum(x, 0.0) + jnp.log(1.0 + jnp.exp(-jnp.abs(x)))


def _matmul(a, b, mode, *, name):
    if mode == "nn":
        (m, k), (k2, n) = a.shape, b.shape
    elif mode == "nt":
        (m, k), (n, k2) = a.shape, b.shape
    else:
        (k, m), (k2, n) = a.shape, b.shape
    assert k == k2, (a.shape, b.shape, mode)
    tm, tn, tk = _tile(m, MM_TILE[0]), _tile(n, MM_TILE[1]), _tile(k, MM_TILE[2])
    nk = k // tk
    if mode == "tn":
        a_spec = pl.BlockSpec((tk, tm), lambda i, j, kk: (kk, i))
    else:
        a_spec = pl.BlockSpec((tm, tk), lambda i, j, kk: (i, kk))
    if mode == "nt":
        b_spec = pl.BlockSpec((tn, tk), lambda i, j, kk: (j, kk))
    else:
        b_spec = pl.BlockSpec((tk, tn), lambda i, j, kk: (kk, j))
    o_spec = pl.BlockSpec((tm, tn), lambda i, j, kk: (i, j))
    dims = {"nn": NN, "nt": NT, "tn": TN}[mode]

    def body(a_ref, b_ref, o_ref, *scratch):
        if nk == 1:
            o_ref[...] = _dot(a_ref[...], b_ref[...], dims)
            return
        acc = scratch[0]
        kk = pl.program_id(2)

        @pl.when(kk == 0)
        def _():
            acc[...] = jnp.zeros_like(acc)

        acc[...] += _dot(a_ref[...], b_ref[...], dims)

        @pl.when(kk == nk - 1)
        def _():
            o_ref[...] = acc[...]

    return pl.pallas_call(
        body, name=name, grid=(m // tm, n // tn, nk), in_specs=[a_spec, b_spec], out_specs=o_spec,
        out_shape=jax.ShapeDtypeStruct((m, n), F32),
        scratch_shapes=[pltpu.VMEM((tm, tn), F32)] if nk > 1 else [],
        compiler_params=_params(("parallel", "parallel", "arbitrary")),
    )(a, b)


def _rms_in_proj(h, g, w, *, name):
    t, d = h.shape
    n = w.shape[1]
    tm, tn = _tile(t, MM_TILE[0]), _tile(n, MM_TILE[1])

    def body(h_ref, g_ref, w_ref, o_ref, hn_ref):
        @pl.when(pl.program_id(1) == 0)
        def _():
            x = h_ref[...]
            r = lax.rsqrt(jnp.mean(x * x, axis=-1, keepdims=True) + RMS_EPS)
            hn_ref[...] = (x * r * g_ref[...]).astype(BF16)

        o_ref[...] = _dot(hn_ref[...], w_ref[...])

    return pl.pallas_call(
        body, name=name, grid=(t // tm, n // tn),
        in_specs=[pl.BlockSpec((tm, d), lambda i, j: (i, 0)), pl.BlockSpec((1, d), lambda i, j: (0, 0)),
                  pl.BlockSpec((d, tn), lambda i, j: (0, j))],
        out_specs=[pl.BlockSpec((tm, tn), lambda i, j: (i, j)), pl.BlockSpec((tm, d), lambda i, j: (i, 0))],
        out_shape=[jax.ShapeDtypeStruct((t, n), F32), jax.ShapeDtypeStruct((t, d), BF16)],
        compiler_params=_params(("parallel", "arbitrary")),
    )(h, g, w)


def _matmul_rms_bwd(dproj, w, h, g, dres, *, name, with_bf16):
    t, k = dproj.shape
    d = w.shape[0]
    tm = _tile(t, 2 * ROW_TILE)

    def body(dp_ref, w_ref, h_ref, g_ref, dres_ref, *outs):
        i = pl.program_id(0)
        dh_ref, dg_ref = outs[0], outs[-1]
        dyv = _dot(dp_ref[...], w_ref[...], NT)
        x = h_ref[...]
        r = lax.rsqrt(jnp.mean(x * x, axis=-1, keepdims=True) + RMS_EPS)
        xh = x * r
        dxh = dyv * g_ref[...]
        dh = dres_ref[...] + r * (dxh - xh * jnp.mean(dxh * xh, axis=-1, keepdims=True))
        dh_ref[...] = dh
        if with_bf16:
            outs[1][...] = dh.astype(BF16)

        @pl.when(i == 0)
        def _():
            dg_ref[...] = jnp.zeros_like(dg_ref)

        dg_ref[...] += jnp.sum(dyv * xh, axis=0, keepdims=True)

    row = pl.BlockSpec((tm, d), lambda i: (i, 0))
    vec = pl.BlockSpec((1, d), lambda i: (0, 0))
    out_shape = [jax.ShapeDtypeStruct((t, d), F32)] + ([jax.ShapeDtypeStruct((t, d), BF16)] if with_bf16 else [])
    out_specs = [row] * len(out_shape) + [vec]
    out_shape.append(jax.ShapeDtypeStruct((1, d), F32))
    return pl.pallas_call(
        body, name=name, grid=(t // tm,),
        in_specs=[pl.BlockSpec((tm, k), lambda i: (i, 0)), pl.BlockSpec((d, k), lambda i: (0, 0)), row, vec, row],
        out_specs=out_specs, out_shape=out_shape, compiler_params=_params(("arbitrary",)),
    )(dproj, w, h, g, dres)


def _rope_partner(x):
    lane = lax.broadcasted_iota(jnp.int32, x.shape, 1)
    swapped = jnp.where(lane < MLA_NOPE + MLA_ROPE // 2, pltpu.roll(x, LANES - 16, 1), pltpu.roll(x, 16, 1))
    return jnp.where((lane >= MLA_NOPE) & (lane < MLA_NOPE + MLA_ROPE), swapped, 0.0)


def _pool_counts(row0, tm, w):
    t_idx = row0 + lax.broadcasted_iota(jnp.int32, (tm, POOL_GROUP), 0)
    return jnp.minimum(t_idx + 1, w).astype(F32)


def _ab_prep(proj, pos, inv_freq, q_a_norm, kv_a_norm, wq, wk, wv, wpool, *, name):
    t = proj.shape[0]
    tm = _tile(t, ROW_TILE)
    hb = tm // POOL_HALO

    def body(p_ref, halo_ref, pos_ref, inv_ref, qg_ref, kg_ref, wq_ref, wk_ref, wv_ref, wp_ref,
             q_ref, k_ref, v_ref, yb_ref, qn_ref, kvn_ref, d_ref, cos_ref, sin_ref, ext):
        i = pl.program_id(0)
        ql = p_ref[:, 0:MLA_Q_RANK]
        r = lax.rsqrt(jnp.mean(ql * ql, axis=-1, keepdims=True) + RMS_EPS)
        qn = (ql * r * qg_ref[...]).astype(BF16)
        qn_ref[...] = qn
        kl = p_ref[:, MLA_Q_RANK:MLA_Q_RANK + MLA_KV_RANK]
        r = lax.rsqrt(jnp.mean(kl * kl, axis=-1, keepdims=True) + RMS_EPS)
        kvn = (kl * r * kg_ref[...]).astype(BF16)
        kvn_ref[...] = kvn
        ang = pos_ref[...].astype(F32) * inv_ref[...]
        lane = lax.broadcasted_iota(jnp.int32, (tm, LANES), 1)
        in_rope = (lane >= MLA_NOPE) & (lane < MLA_NOPE + MLA_ROPE)
        cos_t = jnp.where(in_rope, jnp.cos(ang), 1.0)
        sin_t = jnp.where(in_rope, jnp.sin(ang), 0.0)
        sin_t = jnp.where(lane < MLA_NOPE + MLA_ROPE // 2, -sin_t, sin_t)
        cos_ref[...] = cos_t
        sin_ref[...] = sin_t
        kr = p_ref[:, 384:512]
        kr = kr * cos_t + _rope_partner(kr) * sin_t
        qraw = _dot(qn, wq_ref[...])
        kvk = _dot(kvn, wk_ref[...])
        for h in range(MLA_HEADS):
            sl = slice(h * LANES, (h + 1) * LANES)
            qh = qraw[:, sl]
            q_ref[:, sl] = ((qh * cos_t + _rope_partner(qh) * sin_t) * ATT_SCALE).astype(BF16)
            k_ref[:, sl] = (kvk[:, sl] + kr).astype(BF16)
        v_ref[...] = _dot(kvn, wv_ref[...]).astype(BF16)
        xp = p_ref[:, 512:1024]
        ext[0:POOL_HALO, :] = jnp.where(i > 0, halo_ref[...], 0.0)
        ext[POOL_HALO:POOL_HALO + tm, :] = xp
        for g, w in enumerate(POOL_WINDOWS):
            lo = g * POOL_GROUP
            acc = ext[POOL_HALO:POOL_HALO + tm, lo:lo + POOL_GROUP]
            for s in range(1, w):
                acc = acc + ext[POOL_HALO - s:POOL_HALO - s + tm, lo:lo + POOL_GROUP]
            cnt = _pool_counts(i * tm, tm, w)
            d_ref[:, lo:lo + POOL_GROUP] = (acc / cnt - xp[:, lo:lo + POOL_GROUP]).astype(BF16)
        yb_ref[...] = _dot(d_ref[...], wp_ref[...])

    row = lambda w: pl.BlockSpec((tm, w), lambda i: (i, 0))
    vec = lambda w: pl.BlockSpec((1, w), lambda i: (0, 0))
    whole = lambda a: pl.BlockSpec(a.shape, lambda i: (0, 0))
    return pl.pallas_call(
        body, name=name, grid=(t // tm,),
        in_specs=[row(1024), pl.BlockSpec((POOL_HALO, POOL_WIDTH), lambda i: (jnp.maximum(i * hb - 1, 0), 1)),
                  pl.BlockSpec((tm, 1), lambda i: (i, 0)), vec(LANES), vec(MLA_Q_RANK), vec(MLA_KV_RANK),
                  whole(wq), whole(wk), whole(wv), whole(wpool)],
        out_specs=[row(1024), row(1024), row(512), row(512), row(MLA_Q_RANK), row(MLA_KV_RANK), row(POOL_WIDTH),
                   row(LANES), row(LANES)],
        out_shape=[jax.ShapeDtypeStruct((t, 1024), BF16), jax.ShapeDtypeStruct((t, 1024), BF16),
                   jax.ShapeDtypeStruct((t, 512), BF16), jax.ShapeDtypeStruct((t, 512), F32),
                   jax.ShapeDtypeStruct((t, MLA_Q_RANK), BF16), jax.ShapeDtypeStruct((t, MLA_KV_RANK), BF16),
                   jax.ShapeDtypeStruct((t, POOL_WIDTH), BF16), jax.ShapeDtypeStruct((t, LANES), F32),
                   jax.ShapeDtypeStruct((t, LANES), F32)],
        scratch_shapes=[pltpu.VMEM((tm + POOL_HALO, POOL_WIDTH), F32)],
        compiler_params=_params(("parallel",)),
    )(proj, proj, pos, inv_freq, q_a_norm, kv_a_norm, wq, wk, wv, wpool)


def _ab_prep_bwd(proj, q_a_norm, kv_a_norm, dq, dk, dv, cos_t, sin_t, dyb, dz, wq, wk, wv, wpool, *, name):
    t = proj.shape[0]
    tm = _tile(t, ROW_TILE)
    hb = tm // POOL_HALO
    last_halo = t // POOL_HALO - 1
    nt = t // tm

    def body(p_ref, qg_ref, kg_ref, dq_ref, dk_ref, dv_ref, c_ref, s_ref, dyb_ref, dybn_ref, dz_ref,
             wq_ref, wk_ref, wv_ref, wp_ref, dp_ref, dqr_ref, dkb_ref, dqg_ref, dkg_ref, ext):
        i = pl.program_id(0)

        @pl.when(i == 0)
        def _():
            dqg_ref[...] = jnp.zeros_like(dqg_ref)
            dkg_ref[...] = jnp.zeros_like(dkg_ref)

        def norm_bwd(x, g, dy, dg_ref):
            r = lax.rsqrt(jnp.mean(x * x, axis=-1, keepdims=True) + RMS_EPS)
            xh = x * r
            dxh = dy * g
            dg_ref[...] += jnp.sum(dy * xh, axis=0, keepdims=True)
            return r * (dxh - xh * jnp.mean(dxh * xh, axis=-1, keepdims=True))

        c, s = c_ref[...], s_ref[...]
        lane = lax.broadcasted_iota(jnp.int32, (tm, LANES), 1)
        in_rope = (lane >= MLA_NOPE) & (lane < MLA_NOPE + MLA_ROPE)
        dkr = jnp.zeros((tm, LANES), F32)
        for h in range(MLA_HEADS):
            sl = slice(h * LANES, (h + 1) * LANES)
            g = dq_ref[:, sl]
            dqr_ref[:, sl] = ((g * c + _rope_partner(g * s)) * ATT_SCALE).astype(BF16)
            gk = dk_ref[:, sl]
            dkb_ref[:, sl] = gk.astype(BF16)
            dkr = dkr + jnp.where(in_rope, gk, 0.0)
        dkr = dkr * c + _rope_partner(dkr * s)
        dqn = _dot(dqr_ref[...], wq_ref[...], NT)
        dkvn = _dot(dkb_ref[...], wk_ref[...], NT) + _dot(dv_ref[...], wv_ref[...], NT)
        dql = norm_bwd(p_ref[:, 0:MLA_Q_RANK], qg_ref[...], dqn, dqg_ref)
        dp_ref[:, 0:MLA_Q_RANK] = dql.astype(BF16)
        dkl = norm_bwd(p_ref[:, MLA_Q_RANK:384], kg_ref[...], dkvn, dkg_ref)
        dp_ref[:, MLA_Q_RANK:384] = dkl.astype(BF16)
        dp_ref[:, 384:512] = dkr.astype(BF16)
        ddv = _dot(dyb_ref[...], wp_ref[...], NT)
        ddn = _dot(dybn_ref[...], wp_ref[...], NT)
        for g, w in enumerate(POOL_WINDOWS):
            lo = g * POOL_GROUP
            ext[0:tm, lo:lo + POOL_GROUP] = ddv[:, lo:lo + POOL_GROUP] / _pool_counts(i * tm, tm, w)
            nxt = ddn[:, lo:lo + POOL_GROUP] / _pool_counts((i + 1) * tm, POOL_HALO, w)
            ext[tm:tm + POOL_HALO, lo:lo + POOL_GROUP] = jnp.where(i < nt - 1, nxt, 0.0)
        for g, w in enumerate(POOL_WINDOWS):
            lo = g * POOL_GROUP
            acc = ext[0:tm, lo:lo + POOL_GROUP]
            for s in range(1, w):
                acc = acc + ext[s:s + tm, lo:lo + POOL_GROUP]
            dp_ref[:, 512 + lo:512 + lo + POOL_GROUP] = (acc - ddv[:, lo:lo + POOL_GROUP]).astype(BF16)
        dp_ref[:, 1024:2048] = dz_ref[...]

    row = lambda w: pl.BlockSpec((tm, w), lambda i: (i, 0))
    vec = lambda w: pl.BlockSpec((1, w), lambda i: (0, 0))
    whole = lambda a: pl.BlockSpec(a.shape, lambda i: (0, 0))
    return pl.pallas_call(
        body, name=name, grid=(nt,),
        in_specs=[row(1024), vec(MLA_Q_RANK), vec(MLA_KV_RANK), row(1024), row(1024), row(512), row(LANES), row(LANES),
                  row(POOL_WIDTH),
                  pl.BlockSpec((POOL_HALO, POOL_WIDTH), lambda i: (jnp.minimum((i + 1) * hb, last_halo), 0)),
                  row(1024), whole(wq), whole(wk), whole(wv), whole(wpool)],
        out_specs=[row(IN_AB_PAD), row(1024), row(1024), vec(MLA_Q_RANK), vec(MLA_KV_RANK)],
        out_shape=[jax.ShapeDtypeStruct((t, IN_AB_PAD), BF16), jax.ShapeDtypeStruct((t, 1024), BF16),
                   jax.ShapeDtypeStruct((t, 1024), BF16), jax.ShapeDtypeStruct((1, MLA_Q_RANK), F32),
                   jax.ShapeDtypeStruct((1, MLA_KV_RANK), F32)],
        scratch_shapes=[pltpu.VMEM((tm + POOL_HALO, POOL_WIDTH), F32)],
        compiler_params=_params(("arbitrary",)),
    )(proj, q_a_norm, kv_a_norm, dq, dk, dv, cos_t, sin_t, dyb, dyb, dz, wq, wk, wv, wpool)


def _gate_out_proj(o, ybraw, proj, pool_scale, w, hres, *, name):
    t = o.shape[0]
    tm = _tile(t, 2 * ROW_TILE)

    def body(o_ref, yb_ref, z_ref, ps_ref, w_ref, h_ref, ho_ref, y_ref):
        z = z_ref[...]
        sz = z * _sigmoid(z)
        y_ref[:, 0:512] = (o_ref[...] * sz[:, 0:512]).astype(BF16)
        y_ref[:, 512:1024] = (yb_ref[...] * ps_ref[...] * sz[:, 512:1024]).astype(BF16)
        ho_ref[...] = h_ref[...] + _dot(y_ref[...], w_ref[...])

    row = lambda w_: pl.BlockSpec((tm, w_), lambda i: (i, 0))
    return pl.pallas_call(
        body, name=name, grid=(t // tm,),
        in_specs=[row(512), row(512), pl.BlockSpec((tm, 1024), lambda i: (i, 1)), pl.BlockSpec((1, 512), lambda i: (0, 0)),
                  pl.BlockSpec(w.shape, lambda i: (0, 0)), row(1024)],
        out_specs=[row(1024), row(1024)],
        out_shape=[jax.ShapeDtypeStruct((t, 1024), F32), jax.ShapeDtypeStruct((t, 1024), BF16)],
        compiler_params=_params(("parallel",)),
    )(o, ybraw, proj, pool_scale, w, hres)


def _gate_bwd(dy, o, ybraw, proj, pool_scale, *, name):
    t = o.shape[0]
    tm = _tile(t, ROW_TILE)

    def body(dy_ref, o_ref, yb_ref, z_ref, ps_ref, do_ref, dl_ref, dyb_ref, dz_ref, dps_ref):
        i = pl.program_id(0)
        z = z_ref[...]
        sg = _sigmoid(z)
        sz = z * sg
        dsz = sg * (1.0 + z * (1.0 - sg))
        dyv = dy_ref[...]
        dcat = dyv * sz
        ov = o_ref[...]
        ybs = yb_ref[...] * ps_ref[...]
        dz_ref[:, 0:512] = (dyv[:, 0:512] * ov * dsz[:, 0:512]).astype(BF16)
        dz_ref[:, 512:1024] = (dyv[:, 512:1024] * ybs * dsz[:, 512:1024]).astype(BF16)
        do = dcat[:, 0:512]
        do_ref[...] = do.astype(BF16)
        r_i = lax.broadcasted_iota(jnp.int32, (512, 512), 0) // MLA_V
        c_i = lax.broadcasted_iota(jnp.int32, (512, 512), 1) // MLA_V
        dl_ref[...] = _dot(do * ov, (r_i == c_i).astype(F32), NN, HI)
        dyb_ref[...] = (dcat[:, 512:1024] * ps_ref[...]).astype(BF16)

        @pl.when(i == 0)
        def _():
            dps_ref[...] = jnp.zeros_like(dps_ref)

        dps_ref[...] += jnp.sum(dcat[:, 512:1024] * yb_ref[...], axis=0, keepdims=True)

    row = lambda w: pl.BlockSpec((tm, w), lambda i: (i, 0))
    vec = pl.BlockSpec((1, 512), lambda i: (0, 0))
    return pl.pallas_call(
        body, name=name, grid=(t // tm,),
        in_specs=[row(1024), row(512), row(512), pl.BlockSpec((tm, 1024), lambda i: (i, 1)), vec],
        out_specs=[row(512), row(512), row(512), row(1024), vec],
        out_shape=[jax.ShapeDtypeStruct((t, 512), BF16), jax.ShapeDtypeStruct((t, 512), F32),
                   jax.ShapeDtypeStruct((t, 512), BF16), jax.ShapeDtypeStruct((t, 1024), BF16),
                   jax.ShapeDtypeStruct((1, 512), F32)],
        compiler_params=_params(("arbitrary",)),
    )(dy, o, ybraw, proj, pool_scale)


ATT_HP_FWD = 4
ATT_HP_BWD = 2


def _diag_mask(tq):
    return lax.broadcasted_iota(jnp.int32, (tq, tq), 1) <= lax.broadcasted_iota(jnp.int32, (tq, tq), 0)


def _block_schedule(nq, key_major):
    if key_major:
        pairs = [(qi, ki) for ki in range(nq) for qi in range(ki, nq)]
    else:
        pairs = [(qi, ki) for qi in range(nq) for ki in range(qi + 1)]
    return jnp.asarray([p[0] for p in pairs], jnp.int32), jnp.asarray([p[1] for p in pairs], jnp.int32)


def _attn_fwd(q, k, v, *, name):
    t = q.shape[0]
    tq = _tile(t, ATT_TILE)
    nq = t // tq
    hp = ATT_HP_FWD
    qi_tab, ki_tab = _block_schedule(nq, key_major=False)

    def body(qi_ref, ki_ref, q_ref, k_ref, v_ref, o_ref, lse_ref, m_sc, l_sc, acc_sc):
        step = pl.program_id(1)
        qi, ki = qi_ref[step], ki_ref[step]

        @pl.when(ki == 0)
        def _():
            m_sc[...] = jnp.full_like(m_sc, -jnp.inf)
            l_sc[...] = jnp.zeros_like(l_sc)
            acc_sc[...] = jnp.zeros_like(acc_sc)

        def block(on_diagonal):
            scores = []
            for h in range(hp):
                sl = slice(h * LANES, (h + 1) * LANES)
                scores.append(_dot(q_ref[:, sl], k_ref[:, sl], NT))
            if on_diagonal:
                mask = _diag_mask(tq)
                scores = [jnp.where(mask, s, -jnp.inf) for s in scores]
            for h, s in enumerate(scores):
                vv = v_ref[:, (h // 2) * LANES:(h // 2 + 1) * LANES]
                m_prev = m_sc[h]
                m_new = jnp.maximum(m_prev, jnp.max(s, axis=-1, keepdims=True))
                alpha = jnp.exp(m_prev - m_new)
                p = jnp.exp(s - m_new[:, 0:1])
                l_sc[h] = alpha * l_sc[h] + jnp.sum(p, axis=-1, keepdims=True)
                acc_sc[h] = alpha * acc_sc[h] + _dot(p.astype(BF16), vv)
                m_sc[h] = m_new

        pl.when(ki < qi)(functools.partial(block, False))
        pl.when(ki == qi)(functools.partial(block, True))

        @pl.when(ki == qi)
        def _():
            first = lax.broadcasted_iota(jnp.int32, (tq, LANES), 1) < MLA_V
            for pr in range(hp // 2):
                a, b = 2 * pr, 2 * pr + 1
                sl = slice(pr * LANES, (pr + 1) * LANES)
                o_ref[:, sl] = jnp.where(first, acc_sc[a] / l_sc[a], acc_sc[b] / l_sc[b])
                lse_ref[:, sl] = jnp.where(first, m_sc[a] + jnp.log(l_sc[a]), m_sc[b] + jnp.log(l_sc[b]))

    grid_spec = pltpu.PrefetchScalarGridSpec(
        num_scalar_prefetch=2, grid=(MLA_HEADS // hp, qi_tab.shape[0]),
        in_specs=[pl.BlockSpec((tq, hp * LANES), lambda g, s, qt, kt: (qt[s], g)),
                  pl.BlockSpec((tq, hp * LANES), lambda g, s, qt, kt: (kt[s], g)),
                  pl.BlockSpec((tq, hp * MLA_V), lambda g, s, qt, kt: (kt[s], g))],
        out_specs=[pl.BlockSpec((tq, hp * MLA_V), lambda g, s, qt, kt: (qt[s], g)),
                   pl.BlockSpec((tq, hp * MLA_V), lambda g, s, qt, kt: (qt[s], g))],
        scratch_shapes=[pltpu.VMEM((hp, tq, LANES), F32)] * 3,
    )
    return pl.pallas_call(
        body, name=name, grid_spec=grid_spec,
        out_shape=[jax.ShapeDtypeStruct((t, 512), F32), jax.ShapeDtypeStruct((t, 512), F32)],
        compiler_params=_params(("parallel", "arbitrary")),
    )(qi_tab, ki_tab, q, k, v)


def _attn_bwd(q, k, v, do, lse, delta, *, name):
    t = q.shape[0]
    tq = _tile(t, ATT_TILE)
    nq = t // tq
    hp = ATT_HP_BWD
    qi_tab, ki_tab = _block_schedule(nq, key_major=True)

    def body(qi_ref, ki_ref, q_ref, k_ref, v_ref, do_ref, lse_ref, dl_ref, dq_ref, dk_ref, dv_ref, dk_sc, dv_sc):
        step = pl.program_id(1)
        qi, ki = qi_ref[step], ki_ref[step]

        @pl.when(step == 0)
        def _():
            dq_ref[...] = jnp.zeros_like(dq_ref)

        @pl.when(qi == ki)
        def _():
            dk_sc[...] = jnp.zeros_like(dk_sc)
            dv_sc[...] = jnp.zeros_like(dv_sc)

        def block(on_diagonal):
            lane = lax.broadcasted_iota(jnp.int32, (tq, LANES), 1)
            rows = pl.ds(pl.multiple_of(qi * tq, tq), tq)
            heads = [slice(h * LANES, (h + 1) * LANES) for h in range(hp)]
            scores = [_dot(q_ref[:, sl], k_ref[:, sl], NT) for sl in heads]
            dps = []
            for h in range(hp):
                dov = do_ref[:, (h // 2) * LANES:(h // 2 + 1) * LANES]
                mine = (lane < MLA_V) if h % 2 == 0 else (lane >= MLA_V)
                dps.append(_dot(jnp.where(mine, dov, jnp.zeros_like(dov)), v_ref[:, (h // 2) * LANES:(h // 2 + 1) * LANES], NT))
            mask = _diag_mask(tq) if on_diagonal else None
            for h, sl in enumerate(heads):
                col = (h // 2) * LANES + (h % 2) * MLA_V
                p = jnp.exp(scores[h] - lse_ref[:, col:col + 1])
                if on_diagonal:
                    p = jnp.where(mask, p, 0.0)
                ds = (p * (dps[h] - dl_ref[:, col:col + 1])).astype(BF16)
                dv_sc[h] += _dot(p.astype(BF16), do_ref[:, (h // 2) * LANES:(h // 2 + 1) * LANES], TN)
                dk_sc[h] += _dot(ds, q_ref[:, sl], TN)
                dq_ref[rows, sl] += _dot(ds, k_ref[:, sl], NN)

        pl.when(qi > ki)(functools.partial(block, False))
        pl.when(qi == ki)(functools.partial(block, True))

        @pl.when(qi == nq - 1)
        def _():
            first = lax.broadcasted_iota(jnp.int32, (tq, LANES), 1) < MLA_V
            for h in range(hp):
                dk_ref[:, h * LANES:(h + 1) * LANES] = dk_sc[h]
            for pr in range(hp // 2):
                dv_ref[:, pr * LANES:(pr + 1) * LANES] = jnp.where(first, dv_sc[2 * pr], dv_sc[2 * pr + 1]).astype(BF16)

    qrow = lambda w: pl.BlockSpec((tq, w), lambda g, s, qt, kt: (qt[s], g))
    krow = lambda w: pl.BlockSpec((tq, w), lambda g, s, qt, kt: (kt[s], g))
    grid_spec = pltpu.PrefetchScalarGridSpec(
        num_scalar_prefetch=2, grid=(MLA_HEADS // hp, qi_tab.shape[0]),
        in_specs=[qrow(hp * LANES), krow(hp * LANES), krow(hp * MLA_V), qrow(hp * MLA_V), qrow(hp * MLA_V), qrow(hp * MLA_V)],
        out_specs=[pl.BlockSpec((t, hp * LANES), lambda g, s, qt, kt: (0, g)), krow(hp * LANES), krow(hp * MLA_V)],
        scratch_shapes=[pltpu.VMEM((hp, tq, LANES), F32), pltpu.VMEM((hp, tq, LANES), F32)],
    )
    return pl.pallas_call(
        body, name=name, grid_spec=grid_spec,
        out_shape=[jax.ShapeDtypeStruct((t, 1024), F32), jax.ShapeDtypeStruct((t, 1024), F32),
                   jax.ShapeDtypeStruct((t, 512), BF16)],
        compiler_params=_params(("parallel", "arbitrary")),
    )(qi_tab, ki_tab, q, k, v, do, lse, delta)


def _conv_rows(ext, tm, w_ref, sec):
    c0 = sec * 1024
    y = ext[CONV_HALO - 3:CONV_HALO - 3 + tm, c0:c0 + 1024] * w_ref[0:1, c0:c0 + 1024]
    for j in range(1, CONV_WIDTH):
        y = y + ext[CONV_HALO - 3 + j:CONV_HALO - 3 + j + tm, c0:c0 + 1024] * w_ref[j:j + 1, c0:c0 + 1024]
    return y


def _c_prep(proj_c, conv_w, a_log, dt_bias, *, name):
    t = proj_c.shape[0]
    tm = _tile(t, ROW_TILE)
    hb = tm // CONV_HALO

    def body(p_ref, halo_ref, ab_ref, w_ref, al_ref, dtb_ref, q_ref, k_ref, v_ref, g_ref, b_ref, gt_ref, ext):
        i = pl.program_id(0)
        ext[0:CONV_HALO, :] = jnp.where(i > 0, halo_ref[...], 0.0)
        ext[CONV_HALO:CONV_HALO + tm, :] = p_ref[...]
        for sec, o_ref in enumerate((q_ref, k_ref, v_ref)):
            y = _conv_rows(ext, tm, w_ref, sec)
            y = y * _sigmoid(y)
            if sec == 2:
                o_ref[...] = y
                continue
            scale = GDN_DK ** -0.5 if sec == 0 else 1.0
            for h in range(GDN_HEADS):
                sl = slice(h * LANES, (h + 1) * LANES)
                blk = y[:, sl]
                r = lax.rsqrt(jnp.sum(blk * blk, axis=-1, keepdims=True) + RMS_EPS)
                o_ref[:, sl] = blk * (r * scale)
        ab = ab_ref[...]
        g = -jnp.exp(al_ref[...]) * _softplus(ab + dtb_ref[...])
        beta = _sigmoid(ab)
        ri = lax.broadcasted_iota(jnp.int32, (tm, tm), 0)
        ci = lax.broadcasted_iota(jnp.int32, (tm, tm), 1)
        lower = ((ri // CHUNK) == (ci // CHUNK)) & (ri >= ci)
        gc = _dot(lower.astype(F32), g, NN, HI)
        eye = lax.broadcasted_iota(jnp.int32, (LANES, LANES), 0) == lax.broadcasted_iota(jnp.int32, (LANES, LANES), 1)
        gt_ref[...] = _dot(eye.astype(F32), gc, NT, HI)[0:GDN_HEADS, :]
        for h in range(GDN_HEADS):
            sl = slice(h * LANES, (h + 1) * LANES)
            g_ref[:, sl] = jnp.broadcast_to(gc[:, h:h + 1], (tm, LANES))
            b_ref[:, sl] = jnp.broadcast_to(beta[:, GDN_HEADS + h:GDN_HEADS + h + 1], (tm, LANES))

    row = lambda w: pl.BlockSpec((tm, w), lambda i: (i, 0))
    vec = lambda r, w: pl.BlockSpec((r, w), lambda i: (0, 0))
    out = jax.ShapeDtypeStruct((t, 1024), F32)
    return pl.pallas_call(
        body, name=name, grid=(t // tm,),
        in_specs=[row(3072), pl.BlockSpec((CONV_HALO, 3072), lambda i: (jnp.maximum(i * hb - 1, 0), 0)),
                  pl.BlockSpec((tm, LANES), lambda i: (i, 32)), vec(CONV_WIDTH, 3072), vec(1, LANES), vec(1, LANES)],
        out_specs=[row(1024)] * 5 + [pl.BlockSpec((GDN_HEADS, tm), lambda i: (0, i))],
        out_shape=[out] * 5 + [jax.ShapeDtypeStruct((GDN_HEADS, t), F32)],
        scratch_shapes=[pltpu.VMEM((tm + CONV_HALO, 3072), F32)],
        compiler_params=_params(("parallel",)),
    )(proj_c, proj_c, proj_c, conv_w, a_log, dt_bias)


def _c_prep_bwd(proj_c, conv_w, a_log, dt_bias, dq, dk, dv, dgb, dbb, dz, *, name):
    t = proj_c.shape[0]
    tm = _tile(t, ROW_TILE // 2)
    hb = tm // CONV_HALO
    nt = t // tm
    rev = lambda i: nt - 1 - i

    def body(p_ref, halo_ref, ab_ref, w_ref, al_ref, dtb_ref, dq_ref, dk_ref, dv_ref, dg_ref, db_ref, dz_ref,
             dp_ref, dw_ref, dal_ref, ddt_ref, ext, dyext, carry, taps):
        step = pl.program_id(0)
        i = rev(step)

        @pl.when(step == 0)
        def _():
            dw_ref[...] = jnp.zeros_like(dw_ref)
            dal_ref[...] = jnp.zeros_like(dal_ref)
            ddt_ref[...] = jnp.zeros_like(ddt_ref)
            carry[...] = jnp.zeros_like(carry)

        ext[0:CONV_HALO, :] = jnp.where(i > 0, halo_ref[...], 0.0)
        ext[CONV_HALO:CONV_HALO + tm, :] = p_ref[...]
        for sec, g_ref in enumerate((dq_ref, dk_ref, dv_ref)):
            c0 = sec * 1024
            for j in range(CONV_WIDTH):
                taps[j] = ext[CONV_HALO - 3 + j:CONV_HALO - 3 + j + tm, c0:c0 + 1024]
            y = taps[0] * w_ref[0:1, c0:c0 + 1024]
            for j in range(1, CONV_WIDTH):
                y = y + taps[j] * w_ref[j:j + 1, c0:c0 + 1024]
            sg = _sigmoid(y)
            act = y * sg
            if sec == 2:
                dact = g_ref[...]
            else:
                scale = GDN_DK ** -0.5 if sec == 0 else 1.0
                parts = []
                for h in range(GDN_HEADS):
                    sl = slice(h * LANES, (h + 1) * LANES)
                    blk = act[:, sl]
                    r = lax.rsqrt(jnp.sum(blk * blk, axis=-1, keepdims=True) + RMS_EPS)
                    n = blk * r
                    dn = g_ref[:, sl] * scale
                    parts.append(r * (dn - n * jnp.sum(dn * n, axis=-1, keepdims=True)))
                dact = jnp.concatenate(parts, axis=-1)
            dy = dact * (sg * (1.0 + y * (1.0 - sg)))
            dyext[0:tm, c0:c0 + 1024] = dy
            for j in range(CONV_WIDTH):
                dw_ref[j:j + 1, c0:c0 + 1024] += jnp.sum(dy * taps[j], axis=0, keepdims=True)
        dyext[tm:tm + CONV_HALO, :] = carry[...]
        carry[...] = dyext[0:CONV_HALO, :]
        for sec in range(3):
            c0 = sec * 1024
            dx = dyext[3:3 + tm, c0:c0 + 1024] * w_ref[0:1, c0:c0 + 1024]
            for j in range(1, CONV_WIDTH):
                dx = dx + dyext[3 - j:3 - j + tm, c0:c0 + 1024] * w_ref[j:j + 1, c0:c0 + 1024]
            dp_ref[:, c0:c0 + 1024] = dx.astype(BF16)
        dp_ref[:, 3072:4096] = dz_ref[...]
        lane = lax.broadcasted_iota(jnp.int32, (tm, LANES), 1)
        dg = jnp.zeros((tm, LANES), F32)
        dbeta = jnp.zeros((tm, LANES), F32)
        for h in range(GDN_HEADS):
            sl = slice(h * LANES, (h + 1) * LANES)
            dg = dg + jnp.where(lane == h, dg_ref[:, sl], 0.0)
            dbeta = dbeta + jnp.where(lane == GDN_HEADS + h, db_ref[:, sl], 0.0)
        ri = lax.broadcasted_iota(jnp.int32, (tm, tm), 0)
        ci = lax.broadcasted_iota(jnp.int32, (tm, tm), 1)
        upper = ((ri // CHUNK) == (ci // CHUNK)) & (ri <= ci)
        dg = _dot(upper.astype(F32), dg, NN, HI)
        pre = ab_ref[...] + dtb_ref[...]
        s = _sigmoid(pre)
        a_exp = jnp.exp(al_ref[...])
        dg_da = dg * (-a_exp * s)
        dp_ref[:, 4096:IN_C_PAD] = (dg_da + dbeta * s * (1.0 - s)).astype(BF16)
        dal_ref[...] += jnp.sum(dg * (-a_exp * _softplus(pre)), axis=0, keepdims=True)
        ddt_ref[...] += jnp.sum(dg_da, axis=0, keepdims=True)

    row = lambda w: pl.BlockSpec((tm, w), lambda s: (rev(s), 0))
    vec = lambda r, w: pl.BlockSpec((r, w), lambda s: (0, 0))
    return pl.pallas_call(
        body, name=name, grid=(nt,),
        in_specs=[row(3072), pl.BlockSpec((CONV_HALO, 3072), lambda s: (jnp.maximum(rev(s) * hb - 1, 0), 0)),
                  pl.BlockSpec((tm, LANES), lambda s: (rev(s), 32)), vec(CONV_WIDTH, 3072), vec(1, LANES), vec(1, LANES),
                  row(1024), row(1024), row(1024), row(1024), row(1024), row(1024)],
        out_specs=[row(IN_C_PAD), vec(CONV_WIDTH, 3072), vec(1, LANES), vec(1, LANES)],
        out_shape=[jax.ShapeDtypeStruct((t, IN_C_PAD), BF16), jax.ShapeDtypeStruct((CONV_WIDTH, 3072), F32),
                   jax.ShapeDtypeStruct((1, LANES), F32), jax.ShapeDtypeStruct((1, LANES), F32)],
        scratch_shapes=[pltpu.VMEM((tm + CONV_HALO, 3072), F32), pltpu.VMEM((tm + CONV_HALO, 3072), F32),
                        pltpu.VMEM((CONV_HALO, 3072), F32), pltpu.VMEM((CONV_WIDTH, tm, 1024), F32)],
        compiler_params=_params(("arbitrary",)),
    )(proj_c, proj_c, proj_c, conv_w, a_log, dt_bias, dq, dk, dv, dgb, dbb, dz)


def _o_gate_bwd(dy, o, proj_c, o_norm, *, name):
    t = o.shape[0]
    tm = _tile(t, 2 * ROW_TILE)

    def body(dy_ref, o_ref, z_ref, g_ref, do_ref, dz_ref, dg_ref):
        i = pl.program_id(0)

        @pl.when(i == 0)
        def _():
            dg_ref[...] = jnp.zeros_like(dg_ref)

        dg = jnp.zeros((1, LANES), F32)
        for h in range(GDN_HEADS):
            sl = slice(h * LANES, (h + 1) * LANES)
            x = o_ref[:, sl]
            r = lax.rsqrt(jnp.mean(x * x, axis=-1, keepdims=True) + RMS_EPS)
            xh = x * r
            z = z_ref[:, sl]
            sg = _sigmoid(z)
            dyv = dy_ref[:, sl]
            dn = dyv * (z * sg)
            dz_ref[:, sl] = (dyv * xh * g_ref[...] * (sg * (1.0 + z * (1.0 - sg)))).astype(BF16)
            dxh = dn * g_ref[...]
            do_ref[:, sl] = r * (dxh - xh * jnp.mean(dxh * xh, axis=-1, keepdims=True))
            dg = dg + jnp.sum(dn * xh, axis=0, keepdims=True)
        dg_ref[...] += dg

    row = pl.BlockSpec((tm, 1024), lambda i: (i, 0))
    vec = pl.BlockSpec((1, LANES), lambda i: (0, 0))
    return pl.pallas_call(
        body, name=name, grid=(t // tm,), in_specs=[row, row, pl.BlockSpec((tm, 1024), lambda i: (i, 3)), vec],
        out_specs=[row, row, vec],
        out_shape=[jax.ShapeDtypeStruct((t, 1024), F32), jax.ShapeDtypeStruct((t, 1024), BF16),
                   jax.ShapeDtypeStruct((1, LANES), F32)],
        compiler_params=_params(("arbitrary",)),
    )(dy, o, proj_c, o_norm)


PAIR = 2 * CHUNK
GDN_HP = 8


def _bdot(a, b, dims=NN):
    return _dot(a.astype(BF16), b.astype(BF16), dims)


def _each(f, *lists):
    return [f(*args) for args in zip(*lists)]


def _pair_common(q, k, v, gci, gcj, beta):
    ri = lax.broadcasted_iota(jnp.int32, (PAIR, PAIR), 0)
    ci = lax.broadcasted_iota(jnp.int32, (PAIR, PAIR), 1)
    same = (ri // CHUNK) == (ci // CHUNK)
    incl = same & (ri >= ci)
    strict = same & (ri > ci)
    eye = (ri == ci).astype(F32)
    first = lax.broadcasted_iota(jnp.int32, (PAIR, LANES), 0) < CHUNK
    gamma = _each(lambda gi, gj: jnp.where(incl, jnp.exp(jnp.minimum(gi - gj, 0.0)), 0.0), gci, gcj)
    kb = _each(jnp.multiply, k, beta)
    kk = _each(lambda a, b: _bdot(a, b, NT), kb, k)
    qk = _each(lambda a, b: _bdot(a, b, NT), q, k)
    m = _each(lambda x, g: jnp.where(strict, x * g, 0.0), kk, gamma)
    tm_ = _each(lambda x: eye - x, m)
    pw = _each(lambda x: _bdot(x, x), m)
    for it in range(5):
        tm_ = _each(lambda x, p: x + _bdot(x, p), tm_, pw)
        if it < 4:
            pw = _each(lambda p: _bdot(p, p), pw)
    eg = _each(jnp.exp, gci)
    vb = _each(jnp.multiply, v, beta)
    kbe = _each(jnp.multiply, kb, eg)
    uw = _each(lambda x, a, b: _bdot(x, jnp.concatenate([a, b], axis=1)), tm_, vb, kbe)
    attn = _each(lambda x, g: jnp.where(incl, x * g, 0.0), qk, gamma)
    gl_a = _each(lambda g: g[CHUNK - 1:CHUNK, :], gci)
    gl_b = _each(lambda g: g[PAIR - 1:PAIR, :], gci)
    ek = _each(lambda a, b, g: jnp.exp(jnp.where(first, a, b) - g), gl_a, gl_b, gci)
    return dict(incl=incl, strict=strict, gamma=gamma, kb=kb, m=m, tm=tm_, eg=eg, vb=vb, kbe=kbe,
                u=_each(lambda x: x[:, :LANES], uw), w=_each(lambda x: x[:, LANES:], uw), attn=attn,
                qd=_each(jnp.multiply, q, eg), ek=ek, kd=_each(jnp.multiply, k, ek),
                glast_a=_each(jnp.exp, gl_a), glast_b=_each(jnp.exp, gl_b))


def _gdn_specs(t, ts, order):
    nc = ts // CHUNK
    blk = pl.BlockSpec((ts, GDN_HP * LANES), lambda h, s: (order(s), h))
    row = pl.BlockSpec((GDN_HP, 1, ts), lambda h, s: (h, 0, order(s)))
    st = pl.BlockSpec((GDN_HP, nc, LANES, LANES), lambda h, s: (h, order(s), 0, 0))
    return blk, row, st


def _gdn_fwd(q, k, v, gcb, gct, bb, *, name):
    t = q.shape[0]
    ts = _tile(t, GDN_TILE)
    npair = ts // PAIR

    def body(q_ref, k_ref, v_ref, g_ref, gt_ref, b_ref, o_ref, st_ref, s_sc):
        @pl.when(pl.program_id(1) == 0)
        def _():
            s_sc[...] = jnp.zeros_like(s_sc)

        def pair(pi, _):
            rows = pl.ds(pl.multiple_of(pi * PAIR, PAIR), PAIR)
            heads = [slice(hh * LANES, (hh + 1) * LANES) for hh in range(GDN_HP)]
            c = CHUNK
            cat0 = lambda *xs: jnp.concatenate(xs, axis=0)
            s0 = [s_sc[hh] for hh in range(GDN_HP)]
            cm = _pair_common([q_ref[rows, sl] for sl in heads], [k_ref[rows, sl] for sl in heads],
                              [v_ref[rows, sl] for sl in heads], [g_ref[rows, sl] for sl in heads],
                              [gt_ref[hh, :, rows] for hh in range(GDN_HP)], [b_ref[rows, sl] for sl in heads])
            u, w, qd, kd = cm["u"], cm["w"], cm["qd"], cm["kd"]
            r0 = _each(lambda w_, q_, s: _bdot(cat0(w_[:c], q_[:c]), s), w, qd, s0)
            vn_a = _each(lambda u_, r: u_[:c] - r[:c], u, r0)
            s1 = _each(lambda s, gl, k_, vn: s * gl + _bdot(k_[:c], vn, TN), s0, cm["glast_a"], kd, vn_a)
            r1 = _each(lambda w_, q_, s: _bdot(cat0(w_[c:], q_[c:]), s), w, qd, s1)
            vn_b = _each(lambda u_, r: u_[c:] - r[:c], u, r1)
            s2 = _each(lambda s, gl, k_, vn: s * gl + _bdot(k_[c:], vn, TN), s1, cm["glast_b"], kd, vn_b)
            o = _each(lambda ra, rb, at, va, vb_: cat0(ra[c:], rb[c:]) + _bdot(at, cat0(va, vb_)),
                      r0, r1, cm["attn"], vn_a, vn_b)
            for hh, sl in enumerate(heads):
                st_ref[hh, 2 * pi] = s0[hh]
                st_ref[hh, 2 * pi + 1] = s1[hh]
                s_sc[hh] = s2[hh]
                o_ref[rows, sl] = o[hh]
            return 0

        lax.fori_loop(0, npair, pair, 0)

    blk, row, st = _gdn_specs(t, ts, lambda s: s)
    return pl.pallas_call(
        body, name=name, grid=(GDN_HEADS // GDN_HP, t // ts), in_specs=[blk, blk, blk, blk, row, blk],
        out_specs=[blk, st],
        out_shape=[jax.ShapeDtypeStruct((t, 1024), F32), jax.ShapeDtypeStruct((GDN_HEADS, t // CHUNK, LANES, LANES), F32)],
        scratch_shapes=[pltpu.VMEM((GDN_HP, LANES, LANES), F32)],
        compiler_params=_params(("parallel", "arbitrary")),
    )(q, k, v, gcb, gct, bb)


def _gdn_bwd(q, k, v, gcb, gct, bb, do, states, *, name):
    t = q.shape[0]
    ts = _tile(t, GDN_TILE)
    npair = ts // PAIR
    ns = t // ts
    c = CHUNK

    def body(q_ref, k_ref, v_ref, g_ref, gt_ref, b_ref, do_ref, st_ref, dq_ref, dk_ref, dv_ref, dg_ref, db_ref, ds_sc):
        @pl.when(pl.program_id(1) == 0)
        def _():
            ds_sc[...] = jnp.zeros_like(ds_sc)

        rowsum = lambda x: jnp.sum(x, axis=-1, keepdims=True)
        total = lambda x: jnp.sum(rowsum(x), axis=0, keepdims=True)
        cat0 = lambda *xs: jnp.concatenate(xs, axis=0)
        cat1 = lambda *xs: jnp.concatenate(xs, axis=1)

        def pair(step, _):
            pi = npair - 1 - step
            rows = pl.ds(pl.multiple_of(pi * PAIR, PAIR), PAIR)
            heads = [slice(hh * LANES, (hh + 1) * LANES) for hh in range(GDN_HP)]
            hs = range(GDN_HP)
            qv, kv, vv = ([r[rows, sl] for sl in heads] for r in (q_ref, k_ref, v_ref))
            beta = [b_ref[rows, sl] for sl in heads]
            dov = [do_ref[rows, sl] for sl in heads]
            s0 = [st_ref[hh, 2 * pi] for hh in hs]
            s1 = [st_ref[hh, 2 * pi + 1] for hh in hs]
            ds2 = [ds_sc[hh] for hh in hs]
            cm = _pair_common(qv, kv, vv, [g_ref[rows, sl] for sl in heads], [gt_ref[hh, :, rows] for hh in hs], beta)
            u, w, qd, kd, attn = cm["u"], cm["w"], cm["qd"], cm["kd"], cm["attn"]
            tmat, gamma, eg = cm["tm"], cm["gamma"], cm["eg"]
            incl, strict = cm["incl"], cm["strict"]
            vn_a = _each(lambda u_, w_, s: u_[:c] - _bdot(w_[:c], s), u, w, s0)
            vn_b = _each(lambda u_, w_, s: u_[c:] - _bdot(w_[c:], s), u, w, s1)
            vn = _each(cat0, vn_a, vn_b)
            dvn_att = _each(lambda a, d: _bdot(a, d, TN), attn, dov)
            dattn = _each(lambda d, v_: jnp.where(incl, _bdot(d, v_, NT), 0.0), dov, vn)
            dvn_b = _each(lambda x, k_, d: x[c:] + _bdot(k_[c:], d), dvn_att, kd, ds2)
            rb = _each(lambda d, x, s: _bdot(cat0(d[c:], x), s, NT), dov, dvn_b, s1)
            dkd_b = _each(lambda v_, d: _bdot(v_, d, NT), vn_b, ds2)
            dgl_b = _each(lambda d, s: total(d * s), ds2, s1)
            ds1 = _each(lambda d, gl, q_, w_, o_, x: d * gl + _bdot(cat0(q_[c:], w_[c:]), cat0(o_[c:], -x), TN),
                        ds2, cm["glast_b"], qd, w, dov, dvn_b)
            dvn_a = _each(lambda x, k_, d: x[:c] + _bdot(k_[:c], d), dvn_att, kd, ds1)
            ra = _each(lambda d, x, s: _bdot(cat0(d[:c], x), s, NT), dov, dvn_a, s0)
            dkd_a = _each(lambda v_, d: _bdot(v_, d, NT), vn_a, ds1)
            dgl_a = _each(lambda d, s: total(d * s), ds1, s0)
            ds0 = _each(lambda d, gl, q_, w_, o_, x: d * gl + _bdot(cat0(q_[:c], w_[:c]), cat0(o_[:c], -x), TN),
                        ds1, cm["glast_a"], qd, w, dov, dvn_a)
            dvn = _each(cat0, dvn_a, dvn_b)
            dqd = _each(lambda a, b: cat0(a[:c], b[:c]), ra, rb)
            dw = _each(lambda a, b: -cat0(a[c:], b[c:]), ra, rb)
            dkd = _each(cat0, dkd_a, dkd_b)
            dvw = _each(cat1, dvn, dw)
            dvbk = _each(lambda t_, x: _bdot(t_, x, TN), tmat, dvw)
            dvb = _each(lambda x: x[:, :LANES], dvbk)
            dkbe = _each(lambda x: x[:, LANES:], dvbk)
            dt_ = _each(lambda x, a, b: _bdot(x, cat1(a, b), NT), dvw, cm["vb"], cm["kbe"])
            da1 = _each(lambda t_, x: _bdot(t_, x, TN), tmat, dt_)
            dm = _each(lambda x, t_: jnp.where(strict, -_bdot(x, t_, NT), 0.0), da1, tmat)
            dkk = _each(jnp.multiply, dm, gamma)
            dqk = _each(jnp.multiply, dattn, gamma)
            z = _each(lambda a, b, c_, d: a * b + c_ * d, dm, cm["m"], dattn, attn)
            dkb = _each(lambda x, k_, y, e: _bdot(x, k_) + y * e, dkk, kv, dkbe, eg)
            dk = _each(lambda a, b, kb_, q_, x, e, y, be: _bdot(cat0(a, b), cat0(kb_, q_), TN) + x * e + y * be,
                       dkk, dqk, cm["kb"], qv, dkd, cm["ek"], dkb, beta)
            dq = _each(lambda x, k_, y, e: _bdot(x, k_) + y * e, dqk, kv, dqd, eg)

            def colsum_of(z_):
                zh = z_.astype(BF16)
                zl = (z_ - zh.astype(F32)).astype(BF16)
                return _dot(cat0(zh, zl), jnp.ones((2 * PAIR, LANES), BF16), TN)

            colsum = _each(colsum_of, z)
            ri = lax.broadcasted_iota(jnp.int32, (PAIR, LANES), 0)
            for hh, sl in enumerate(heads):
                dkd_kd = dkd[hh] * kd[hh]
                dgc = (rowsum(z[hh]) - colsum[hh] + rowsum(dqd[hh] * qd[hh]) - rowsum(dkd_kd)
                       + rowsum(dkbe[hh] * cm["kbe"][hh]))
                last_a = total(dkd_kd[:c]) + dgl_a[hh] * cm["glast_a"][hh]
                last_b = total(dkd_kd[c:]) + dgl_b[hh] * cm["glast_b"][hh]
                dgc = dgc + jnp.where(ri == c - 1, last_a, 0.0) + jnp.where(ri == PAIR - 1, last_b, 0.0)
                ds_sc[hh] = ds0[hh]
                dq_ref[rows, sl] = dq[hh]
                dk_ref[rows, sl] = dk[hh]
                dv_ref[rows, sl] = dvb[hh] * beta[hh]
                db_ref[rows, sl] = jnp.broadcast_to(rowsum(dkb[hh] * kv[hh]) + rowsum(dvb[hh] * vv[hh]), (PAIR, LANES))
                dg_ref[rows, sl] = dgc
            return 0

        lax.fori_loop(0, npair, pair, 0)

    blk, row, st = _gdn_specs(t, ts, lambda s: ns - 1 - s)
    out = jax.ShapeDtypeStruct((t, 1024), F32)
    return pl.pallas_call(
        body, name=name, grid=(GDN_HEADS // GDN_HP, ns), in_specs=[blk, blk, blk, blk, row, blk, blk, st],
        out_specs=[blk] * 5, out_shape=[out] * 5, scratch_shapes=[pltpu.VMEM((GDN_HP, LANES, LANES), F32)],
        compiler_params=_params(("parallel", "arbitrary")),
    )(q, k, v, gcb, gct, bb, do, states)


def _gate_out_proj_loss(o, proj_c, o_norm, w, hres, g, target, *, name):
    t, d = hres.shape
    tm = _tile(t, 2 * ROW_TILE)

    def body(o_ref, z_ref, on_ref, w_ref, h_ref, g_ref, t_ref, dh_ref, dhb_ref, y_ref, dg_ref, loss_ref):
        i = pl.program_id(0)
        for hd in range(GDN_HEADS):
            sl = slice(hd * LANES, (hd + 1) * LANES)
            ov = o_ref[:, sl]
            rr = lax.rsqrt(jnp.mean(ov * ov, axis=-1, keepdims=True) + RMS_EPS)
            z = z_ref[:, sl]
            y_ref[:, sl] = (ov * rr * on_ref[...] * (z * _sigmoid(z))).astype(BF16)
        x = h_ref[...] + _dot(y_ref[...], w_ref[...])
        r = lax.rsqrt(jnp.mean(x * x, axis=-1, keepdims=True) + RMS_EPS)
        xh = x * r
        err = xh * g_ref[...] - t_ref[...]
        dy = err * (1.0 / d)
        dxh = dy * g_ref[...]
        dh = r * (dxh - xh * jnp.mean(dxh * xh, axis=-1, keepdims=True))
        dh_ref[...] = dh
        dhb_ref[...] = dh.astype(BF16)

        @pl.when(i == 0)
        def _():
            dg_ref[...] = jnp.zeros_like(dg_ref)
            loss_ref[...] = jnp.zeros_like(loss_ref)

        dg_ref[...] += jnp.sum(dy * xh, axis=0, keepdims=True)
        part = 0.5 * jnp.sum(jnp.mean(err * err, axis=-1, keepdims=True), axis=0, keepdims=True)
        loss_ref[...] += jnp.broadcast_to(part, loss_ref.shape)

    row = pl.BlockSpec((tm, d), lambda i: (i, 0))
    vec = pl.BlockSpec((1, d), lambda i: (0, 0))
    return pl.pallas_call(
        body, name=name, grid=(t // tm,),
        in_specs=[row, pl.BlockSpec((tm, 1024), lambda i: (i, 3)), pl.BlockSpec((1, LANES), lambda i: (0, 0)),
                  pl.BlockSpec(w.shape, lambda i: (0, 0)), row, vec, row],
        out_specs=[row, row, row, vec, pl.BlockSpec((8, LANES), lambda i: (0, 0))],
        out_shape=[jax.ShapeDtypeStruct((t, d), F32), jax.ShapeDtypeStruct((t, d), BF16), jax.ShapeDtypeStruct((t, d), BF16),
                   jax.ShapeDtypeStruct((1, d), F32), jax.ShapeDtypeStruct((8, LANES), F32)],
        compiler_params=_params(("arbitrary",)),
    )(o, proj_c, o_norm, w, hres, g, target)


def _pad_cols(w, n):
    return jnp.pad(w, ((0, 0), (0, n - w.shape[1])))


def _layout_odd(w):
    return dict(
        winc=_pad_cols(w["w_in_c"], IN_C_PAD).astype(BF16), wout_c=w["w_out_c"].astype(BF16), conv_w=w["conv_w"],
        a_log=_pad_cols(w["a_log"], LANES), dt_bias=_pad_cols(w["dt_bias"], LANES),
        norm_c=w["norm_c"], o_norm=w["o_norm"], final_norm=w["final_norm"],
    )


def _layout_even(w):
    z = lambda r, c: jnp.zeros((r, c), w["w_in_ab"].dtype)
    wi = w["w_in_ab"]
    win = jnp.concatenate([wi[:, :384], z(1024, 64), wi[:, 384:416], z(1024, 32), wi[:, 416:]], axis=1)
    wq = jnp.pad(w["w_q_b"].reshape(MLA_Q_RANK, MLA_HEADS, 96), ((0, 0), (0, 0), (0, 32))).reshape(MLA_Q_RANK, 1024)
    kv3 = w["w_kv_b"].reshape(MLA_KV_RANK, MLA_HEADS, 128)
    wk = jnp.pad(kv3[..., :MLA_NOPE], ((0, 0), (0, 0), (0, 64))).reshape(MLA_KV_RANK, 1024)
    wv = kv3[..., MLA_NOPE:].reshape(MLA_KV_RANK, 512)
    pw = w["pool_w"]
    rows = []
    for g in range(4):
        rows.append(jnp.concatenate([pw[g] if j == g else z(128, 128) for j in range(4)], axis=1))
    wpool = jnp.concatenate(rows, axis=0)
    half = MLA_ROPE // 2
    inv = 1.0 / (ROPE_THETA ** (jnp.arange(half, dtype=F32) / half))
    inv_lane = jnp.concatenate([jnp.zeros((MLA_NOPE,), F32), inv, inv, jnp.zeros((32,), F32)]).reshape(1, LANES)
    return dict(
        win=win.astype(BF16), wq=wq.astype(BF16), wk=wk.astype(BF16), wv=wv.astype(BF16), wpool=wpool.astype(BF16),
        wout_ab=w["w_out_ab"].astype(BF16), inv_lane=inv_lane,
        norm_ab=w["norm_ab"], q_a_norm=w["q_a_norm"], kv_a_norm=w["kv_a_norm"], pool_scale=w["pool_scale"],
    )


def _unlayout_grads(g, names):
    out = {}
    for name in names:
        if name == "w_in_ab":
            dwin = g["win"]
            out[name] = jnp.concatenate([dwin[:, :384], dwin[:, 448:480], dwin[:, 512:]], axis=1)
        elif name == "w_q_b":
            out[name] = g["wq"].reshape(MLA_Q_RANK, MLA_HEADS, 128)[..., :96].reshape(MLA_Q_RANK, 768)
        elif name == "w_kv_b":
            out[name] = jnp.concatenate([g["wk"].reshape(MLA_KV_RANK, MLA_HEADS, 128)[..., :MLA_NOPE],
                                         g["wv"].reshape(MLA_KV_RANK, MLA_HEADS, MLA_V)], axis=-1).reshape(MLA_KV_RANK, 1024)
        elif name == "w_in_c":
            out[name] = g["winc"][:, :4112]
        else:
            out[name] = g[{"w_out_ab": "wout_ab", "w_out_c": "wout_c"}[name]]
    return out


def _local_step(x, pos, target, lw, odd_weights=None, on_grads=None):
    mm = _matmul
    proj, hn = _rms_in_proj(x, lw["norm_ab"], lw["win"], name="rms_in_ab")
    q, k, v, ybraw, qn, kvn, d, cos_t, sin_t = _ab_prep(
        proj, pos, lw["inv_lane"], lw["q_a_norm"], lw["kv_a_norm"], lw["wq"], lw["wk"], lw["wv"], lw["wpool"], name="ab_prep")
    o, lse = _attn_fwd(q, k, v, name="attn_fwd")
    h1, y = _gate_out_proj(o, ybraw, proj, lw["pool_scale"], lw["wout_ab"], x, name="gate_out_ab")
    lo = lw if odd_weights is None else odd_weights(h1)
    proj_c, hn1 = _rms_in_proj(h1, lo["norm_c"], lo["winc"], name="rms_in_c")
    q2, k2, v2, gb, bb, gt = _c_prep(proj_c, lo["conv_w"], lo["a_log"], lo["dt_bias"], name="c_prep")
    gt = gt.reshape(GDN_HEADS, 1, gt.shape[1])
    o2, states = _gdn_fwd(q2, k2, v2, gb, gt, bb, name="gdn_fwd")
    dh2, dh2b, y2, d_final, loss = _gate_out_proj_loss(
        o2, proj_c, lo["o_norm"], lo["wout_c"], h1, lo["final_norm"], target, name="gate_out_c_loss")
    g = {"final_norm": d_final}
    dy2 = mm(dh2b, lo["wout_c"], "nt", name="out_c_dx")
    g["wout_c"] = mm(y2, dh2b, "tn", name="out_c_dw")
    do2, dz2, g["o_norm"] = _o_gate_bwd(dy2, o2, proj_c, lo["o_norm"], name="gate_c_bwd")
    dq2, dk2, dv2, dgb, dbb = _gdn_bwd(q2, k2, v2, gb, gt, bb, do2, states, name="gdn_bwd")
    dproj_c, g["conv_w"], g["a_log"], g["dt_bias"] = _c_prep_bwd(
        proj_c, lo["conv_w"], lo["a_log"], lo["dt_bias"], dq2, dk2, dv2, dgb, dbb, dz2, name="c_prep_bwd")
    g["winc"] = mm(hn1, dproj_c, "tn", name="in_c_dw")
    notify = (lambda tag: 0.0) if on_grads is None else (lambda tag: on_grads(tag, g))
    pool_scale = lw["pool_scale"] + notify("odd")
    dh1, dh1b, g["norm_c"] = _matmul_rms_bwd(dproj_c, lo["winc"], h1, lo["norm_c"], dh2, name="in_c_dx_rms", with_bf16=True)
    dy = mm(dh1b, lw["wout_ab"], "nt", name="out_ab_dx")
    g["wout_ab"] = mm(y, dh1b, "tn", name="out_ab_dw")
    pool_scale = pool_scale + notify("out_ab")
    do, delta, dyb, dz, g["pool_scale"] = _gate_bwd(dy, o, ybraw, proj, pool_scale, name="gate_ab_bwd")
    dq, dk, dv = _attn_bwd(q, k, v, do, lse, delta, name="attn_bwd")
    dproj, dqraw, dkb, g["q_a_norm"], g["kv_a_norm"] = _ab_prep_bwd(
        proj, lw["q_a_norm"], lw["kv_a_norm"], dq, dk, dv, cos_t, sin_t, dyb, dz,
        lw["wq"], lw["wk"], lw["wv"], lw["wpool"], name="ab_prep_bwd")
    g["wpool"] = mm(d, dyb, "tn", name="pool_mix_dw")
    g["wq"] = mm(qn, dqraw, "tn", name="q_up_dw")
    g["wk"] = mm(kvn, dkb, "tn", name="k_up_dw")
    g["wv"] = mm(kvn, dv, "tn", name="v_up_dw")
    g["win"] = mm(hn, dproj, "tn", name="in_ab_dw")
    norm_ab = lw["norm_ab"] + notify("in_ab")
    dx, g["norm_ab"] = _matmul_rms_bwd(dproj, lw["win"], x, norm_ab, dh1, name="in_ab_dx_rms", with_bf16=False)
    return loss, dx, g


_HBM = pl.BlockSpec(memory_space=pltpu.HBM)


def _place():
    return lax.axis_index("x"), lax.axis_index("y"), lax.axis_index("c")


def _flip(v, f):
    return 1 - v if f else v


_CHIP_FLIPS = ((1, 0), (0, 1), (1, 1))
_DEV_FLIPS = tuple((fx, fy, fc) for fx in (0, 1) for fy in (0, 1) for fc in (0, 1) if fx or fy or fc)


def _rcopy(src, dst, send_sems, recv_sems, k, to):
    return pltpu.make_async_remote_copy(src_ref=src, dst_ref=dst, send_sem=send_sems.at[k], recv_sem=recv_sems.at[k],
                                        device_id=to, device_id_type=MESH)


def _my_half(ref, c, axis):
    rh = ref.shape[axis] // 2
    idx = [slice(None)] * len(ref.shape)
    idx[axis] = pl.ds(c * rh, rh)
    return ref.at[tuple(idx)]


def _gather_weights(bigs, smalls):
    nb, ns = len(bigs), len(smalls)

    def body(*refs):
        ins, outs = refs[:nb + ns], refs[nb + ns:2 * (nb + ns)]
        send_sems, recv_sems, local_sems = refs[2 * (nb + ns):]
        x, y, c = _place()
        j0 = 2 * x + y
        sib = (x, y, 1 - c)
        chips = [(_flip(x, fx), _flip(y, fy)) for fx, fy in _CHIP_FLIPS]
        local = [pltpu.make_async_copy(i_ref, o_ref.at[j0], local_sems.at[a])
                 for a, (i_ref, o_ref) in enumerate(zip(ins, outs))]
        for cp in local:
            cp.start()
        sends = []
        for k, (px, py) in enumerate(chips):
            for a in range(nb):
                sends.append(_rcopy(_my_half(ins[a], c, 0), _my_half(outs[a].at[j0], c, 0), send_sems, recv_sems,
                                    6 * a + k, (px, py, c)))
            for s in range(ns):
                sends.append(_rcopy(ins[nb + s], outs[nb + s].at[j0], send_sems, recv_sems, 6 * nb + 3 * s + k, (px, py, c)))
        for cp in sends:
            cp.start()
        for k, (px, py) in enumerate(chips):
            jk = 2 * px + py
            for a in range(nb):
                landed = _my_half(outs[a].at[jk], c, 0)
                _rcopy(landed, landed, send_sems, recv_sems, 6 * a + k, (px, py, c)).wait_recv()
                fwd = _rcopy(landed, landed, send_sems, recv_sems, 6 * a + 3 + k, sib)
                fwd.start()
                sends.append(fwd)
        for k, (px, py) in enumerate(chips):
            jk = 2 * px + py
            for a in range(nb):
                other = _my_half(outs[a].at[jk], 1 - c, 0)
                _rcopy(other, other, send_sems, recv_sems, 6 * a + 3 + k, sib).wait_recv()
            for s in range(ns):
                _rcopy(ins[nb + s], outs[nb + s].at[jk], send_sems, recv_sems, 6 * nb + 3 * s + k, (px, py, c)).wait_recv()
        for cp in sends:
            cp.wait_send()
        for cp in local:
            cp.wait()

    arrays = list(bigs) + list(smalls)
    n_sem = 6 * nb + 3 * ns
    return pl.pallas_call(
        body, name="gather_weights", in_specs=[_HBM] * len(arrays), out_specs=[_HBM] * len(arrays),
        out_shape=[jax.ShapeDtypeStruct((4,) + a.shape, a.dtype) for a in arrays],
        scratch_shapes=[pltpu.SemaphoreType.DMA((n_sem,)), pltpu.SemaphoreType.DMA((n_sem,)),
                        pltpu.SemaphoreType.DMA((len(arrays),))],
    )(*arrays)


def _core_swap_partial(gs, *, name):
    n = len(gs)

    def body(*refs):
        ins, outs = refs[:n], refs[n:2 * n]
        send_sems, recv_sems = refs[2 * n:]
        x, y, c = _place()
        copies = [_rcopy(_my_half(i_ref, 1 - c, 1), o_ref, send_sems, recv_sems, a, (x, y, 1 - c))
                  for a, (i_ref, o_ref) in enumerate(zip(ins, outs))]
        for cp in copies:
            cp.start()
        for cp in copies:
            cp.wait()

    return pl.pallas_call(
        body, name=name, in_specs=[_HBM] * n, out_specs=[_HBM] * n,
        out_shape=[jax.ShapeDtypeStruct((4, g.shape[1] // 2, g.shape[2]), g.dtype) for g in gs],
        scratch_shapes=[pltpu.SemaphoreType.DMA((n,)), pltpu.SemaphoreType.DMA((n,))],
    )(*gs)


def _core_swap_sum(fs):
    n = len(fs)

    def body(*refs):
        ins, outs = refs[:n], refs[n:2 * n]
        send_sems, recv_sems = refs[2 * n:]
        x, y, c = _place()
        copies = [_rcopy(_my_half(i_ref, c, 0), _my_half(o_ref, c, 0), send_sems, recv_sems, a, (x, y, 1 - c))
                  for a, (i_ref, o_ref) in enumerate(zip(ins, outs))]
        for cp in copies:
            cp.start()
        for a, cp in enumerate(copies):
            cp.wait_send()
            theirs = _my_half(outs[a], 1 - c, 0)
            _rcopy(theirs, theirs, send_sems, recv_sems, a, (x, y, 1 - c)).wait_recv()

    return pl.pallas_call(
        body, name="core_swap_sum", in_specs=[_HBM] * n, out_specs=[_HBM] * n,
        out_shape=[jax.ShapeDtypeStruct(f.shape, f.dtype) for f in fs],
        input_output_aliases={a: a for a in range(n)},
        scratch_shapes=[pltpu.SemaphoreType.DMA((n,)), pltpu.SemaphoreType.DMA((n,))],
    )(*fs)


_SEM = pl.BlockSpec(memory_space=pltpu.SEMAPHORE)
_ANY = pl.BlockSpec(memory_space=pl.ANY)
_DATAFLOW = pltpu.SideEffectType.DATAFLOW_SIDE_EFFECTING


def _to_chips_copies(srcs, lands, send_sems, recv_sems, per_chip_slot):
    x, y, c = _place()
    j0 = 2 * x + y
    out = []
    for k, (fx, fy) in enumerate(_CHIP_FLIPS):
        px, py = _flip(x, fx), _flip(y, fy)
        jk = 2 * px + py
        for a, (src, land) in enumerate(zip(srcs, lands)):
            piece = src.at[jk] if per_chip_slot else src
            out.append((_rcopy(piece, land.at[j0], send_sems, recv_sems, 3 * a + k, (px, py, c)),
                        _rcopy(piece, land.at[jk], send_sems, recv_sems, 3 * a + k, (px, py, c))))
    return out


def _to_chips_start(arrays, *, per_chip_slot, name):
    n = len(arrays)
    lands = [lax.empty((4,) + (a.shape[1:] if per_chip_slot else a.shape), a.dtype) for a in arrays]

    def body(*refs):
        srcs, land_refs, send_sems, recv_sems, token = refs[:n], refs[n:2 * n], refs[2 * n], refs[2 * n + 1], refs[-1]
        for send, _ in _to_chips_copies(srcs, land_refs, send_sems, recv_sems, per_chip_slot):
            send.start()
        token[...] = jnp.zeros_like(token)

    held = [pltpu.with_memory_space_constraint(a, pltpu.HBM) for a in list(arrays) + lands]
    return pl.pallas_call(
        body, name=name, in_specs=[_HBM] * (2 * n),
        out_specs=(_SEM, _SEM, *[_HBM] * (2 * n), pl.BlockSpec(memory_space=pltpu.VMEM)),
        out_shape=(pltpu.SemaphoreType.DMA((3 * n,)), pltpu.SemaphoreType.DMA((3 * n,)),
                   *[pltpu.HBM(a.shape, a.dtype) for a in held], jax.ShapeDtypeStruct((8, LANES), F32)),
        input_output_aliases={i: 2 + i for i in range(2 * n)},
        compiler_params=pltpu.CompilerParams(has_side_effects=_DATAFLOW),
    )(*held)


def _to_chips_wait(started, after, *, per_chip_slot, name):
    send_sems, recv_sems, held = started[0], started[1], started[2:-1]
    n = len(held) // 2

    def body(*refs):
        srcs, land_refs, s_sems, r_sems = refs[:n], refs[n:2 * n], refs[2 * n], refs[2 * n + 1]
        for send, arrival in _to_chips_copies(srcs, land_refs, s_sems, r_sems, per_chip_slot):
            send.wait_send()
            arrival.wait_recv()

    out = pl.pallas_call(
        body, name=name, in_specs=[_HBM] * (2 * n) + [_SEM, _SEM, _ANY], out_specs=[_HBM] * (2 * n),
        out_shape=[pltpu.HBM(a.shape, a.dtype) for a in held],
        input_output_aliases={i: i for i in range(2 * n)},
        compiler_params=pltpu.CompilerParams(has_side_effects=_DATAFLOW),
    )(*held, send_sems, recv_sems, after)
    return out[n:]


def _chip_exchange(ps, small):
    n = len(ps)
    rs = small.shape[0]

    def body(*refs):
        p_refs, s_ref = refs[:n], refs[n]
        l_refs, ls_ref = refs[n + 1:2 * n + 1], refs[2 * n + 1]
        send_sems, recv_sems, local_sems = refs[2 * n + 2:]
        x, y, c = _place()
        j0 = 2 * x + y
        d0 = 2 * j0 + c
        local = [pltpu.make_async_copy(p.at[j0], l.at[j0], local_sems.at[a]) for a, (p, l) in enumerate(zip(p_refs, l_refs))]
        local.append(pltpu.make_async_copy(s_ref, ls_ref.at[d0], local_sems.at[n]))
        for cp in local:
            cp.start()
        sends = []
        for k, (fx, fy) in enumerate(_CHIP_FLIPS):
            px, py = _flip(x, fx), _flip(y, fy)
            for a in range(n):
                sends.append(_rcopy(p_refs[a].at[2 * px + py], l_refs[a].at[j0], send_sems, recv_sems, 3 * a + k, (px, py, c)))
        for k, (fx, fy, fc) in enumerate(_DEV_FLIPS):
            peer = (_flip(x, fx), _flip(y, fy), _flip(c, fc))
            sends.append(_rcopy(s_ref, ls_ref.at[d0], send_sems, recv_sems, 3 * n + k, peer))
        for cp in sends:
            cp.start()
        for k, (fx, fy) in enumerate(_CHIP_FLIPS):
            px, py = _flip(x, fx), _flip(y, fy)
            for a in range(n):
                _rcopy(p_refs[a].at[j0], l_refs[a].at[2 * px + py], send_sems, recv_sems, 3 * a + k, (px, py, c)).wait_recv()
        for k, (fx, fy, fc) in enumerate(_DEV_FLIPS):
            px, py, pc = _flip(x, fx), _flip(y, fy), _flip(c, fc)
            _rcopy(s_ref, ls_ref.at[4 * px + 2 * py + pc], send_sems, recv_sems, 3 * n + k, (px, py, pc)).wait_recv()
        for cp in sends:
            cp.wait_send()
        for cp in local:
            cp.wait()

    n_sem = 3 * n + 7
    return pl.pallas_call(
        body, name="chip_exchange", in_specs=[_HBM] * (n + 1), out_specs=[_HBM] * (n + 1),
        out_shape=[jax.ShapeDtypeStruct(p.shape, F32) for p in ps] + [jax.ShapeDtypeStruct((8, rs, LANES), F32)],
        scratch_shapes=[pltpu.SemaphoreType.DMA((n_sem,)), pltpu.SemaphoreType.DMA((n_sem,)),
                        pltpu.SemaphoreType.DMA((n + 1,))],
    )(*ps, small)


def _core_sum(g, part, core, *, name):
    _, rh, cols = part.shape
    tr = _tile(rh, 256)
    nb = rh // tr

    def body(c_ref, g_ref, p_ref, o_ref):
        o_ref[...] = g_ref[...] + p_ref[...]

    grid_spec = pltpu.PrefetchScalarGridSpec(
        num_scalar_prefetch=1, grid=(4, nb),
        in_specs=[pl.BlockSpec((1, tr, cols), lambda j, i, c: (j, c[0] * nb + i, 0)),
                  pl.BlockSpec((1, tr, cols), lambda j, i, c: (j, i, 0))],
        out_specs=pl.BlockSpec((1, tr, cols), lambda j, i, c: (j, i, 0)),
    )
    return pl.pallas_call(
        body, name=name, grid_spec=grid_spec, out_shape=jax.ShapeDtypeStruct(part.shape, F32),
        compiler_params=_params(("parallel", "parallel")),
    )(core, g, part)


def _chip_sum(landed, core, *, name):
    _, rh, cols = landed.shape
    tr = _tile(rh, 256)
    nb = rh // tr

    def body(c_ref, l_ref, o_ref):
        o_ref[...] = ((l_ref[0] + l_ref[1]) + l_ref[2]) + l_ref[3]

    grid_spec = pltpu.PrefetchScalarGridSpec(
        num_scalar_prefetch=1, grid=(nb,),
        in_specs=[pl.BlockSpec((4, tr, cols), lambda i, c: (0, i, 0))],
        out_specs=pl.BlockSpec((tr, cols), lambda i, c: (c[0] * nb + i, 0)),
    )
    return pl.pallas_call(
        body, name=name, grid_spec=grid_spec, out_shape=jax.ShapeDtypeStruct((2 * rh, cols), F32),
        compiler_params=_params(("parallel",)),
    )(core, landed)


_ROW_POOL_W, _ROW_NORM_AB, _ROW_FINAL, _ROW_POOL_SCALE, _ROW_Q_NORM = 0, 512, 520, 528, 532
_ROW_KV_NORM, _ROW_O_NORM, _ROW_A_LOG, _ROW_DT_BIAS, _ROW_LOSS = 534, 535, 536, 537, 538
_ROW_CONV, _ROW_NORM_C, _SMALL_ROWS = 544, 640, 672
_CONV_ROWS = CONV_WIDTH * 6


def _put_rows(dst_ref, row0, src, width):
    for r in range(width // LANES):
        dst_ref[row0 + r:row0 + r + 1, :] = src[:, r * LANES:(r + 1) * LANES]


def _pack_small(g, loss_tile):
    names = ("wpool", "norm_ab", "final_norm", "pool_scale", "q_a_norm", "kv_a_norm", "o_norm", "a_log", "dt_bias",
             "conv_w", "norm_c")

    def body(wpool, norm_ab, final_norm, pool_scale, q_norm, kv_norm, o_norm, a_log, dt_bias, conv_w, norm_c, loss, o_ref):
        o_ref[...] = jnp.zeros_like(o_ref)
        for gi in range(4):
            o_ref[_ROW_POOL_W + gi * 128:_ROW_POOL_W + (gi + 1) * 128, :] = wpool[gi * 128:(gi + 1) * 128, gi * 128:(gi + 1) * 128]
        _put_rows(o_ref, _ROW_NORM_AB, norm_ab[...], 1024)
        _put_rows(o_ref, _ROW_FINAL, final_norm[...], 1024)
        _put_rows(o_ref, _ROW_POOL_SCALE, pool_scale[...], 512)
        _put_rows(o_ref, _ROW_Q_NORM, q_norm[...], 256)
        for row, ref in ((_ROW_KV_NORM, kv_norm), (_ROW_O_NORM, o_norm), (_ROW_A_LOG, a_log), (_ROW_DT_BIAS, dt_bias)):
            o_ref[row:row + 1, :] = ref[...]
        o_ref[_ROW_LOSS:_ROW_LOSS + 1, :] = loss[0:1, :]
        for j in range(4):
            for r in range(CONV_WIDTH):
                _put_rows(o_ref, _ROW_CONV + j * _CONV_ROWS + r * 6, conv_w[r:r + 1, j * 768:(j + 1) * 768], 768)
            _put_rows(o_ref, _ROW_NORM_C + j * 8, norm_c[:, j * 256:(j + 1) * 256], 256)

    vmem = pl.BlockSpec(memory_space=pltpu.VMEM)
    return pl.pallas_call(
        body, name="pack_small", in_specs=[vmem] * 12, out_specs=vmem,
        out_shape=jax.ShapeDtypeStruct((_SMALL_ROWS, LANES), F32),
    )(*[g[n] for n in names], loss_tile)


_SMALL_NAMES = ("pool_w", "norm_ab", "final_norm", "pool_scale", "q_a_norm", "kv_a_norm", "o_norm", "a_log", "dt_bias",
                "conv_w", "norm_c")


def _take_rows(src, row0, width):
    return jnp.concatenate([src[row0 + r:row0 + r + 1, :] for r in range(width // LANES)], axis=1)


def _small_update(small_all, ws, ms, vs):
    n = len(_SMALL_NAMES)

    def body(*refs):
        a_ref = refs[0]
        w_refs, m_refs, v_refs = refs[1:1 + n], refs[1 + n:1 + 2 * n], refs[1 + 2 * n:1 + 3 * n]
        outs = refs[1 + 3 * n:1 + 7 * n]
        loss_ref, tot = refs[1 + 7 * n], refs[2 + 7 * n]
        acc = a_ref[0]
        for d in range(1, 8):
            acc = acc + a_ref[d]
        tot[...] = acc
        x, y, _ = _place()
        j0 = 2 * x + y
        conv = tot[pl.ds(pl.multiple_of(_ROW_CONV + j0 * _CONV_ROWS, 8), _CONV_ROWS), :]
        norm_c = tot[pl.ds(pl.multiple_of(_ROW_NORM_C + j0 * 8, 8), 8), :]
        whole = tot[_ROW_NORM_AB:_ROW_CONV, :]
        at = lambda row: row - _ROW_NORM_AB
        grads = {
            "norm_ab": _take_rows(whole, at(_ROW_NORM_AB), 1024), "final_norm": _take_rows(whole, at(_ROW_FINAL), 1024),
            "pool_scale": _take_rows(whole, at(_ROW_POOL_SCALE), 512), "q_a_norm": _take_rows(whole, at(_ROW_Q_NORM), 256),
            "kv_a_norm": whole[at(_ROW_KV_NORM):at(_ROW_KV_NORM) + 1, :], "o_norm": whole[at(_ROW_O_NORM):at(_ROW_O_NORM) + 1, :],
            "a_log": tot[_ROW_A_LOG:_ROW_A_LOG + 1, 0:GDN_HEADS],
            "dt_bias": tot[_ROW_DT_BIAS:_ROW_DT_BIAS + 1, 0:GDN_HEADS],
            "norm_c": _take_rows(norm_c, 0, 256),
        }
        loss_ref[...] = whole[at(_ROW_LOSS):at(_ROW_LOSS) + 1, :]
        for i, name in enumerate(_SMALL_NAMES):
            g_out = outs[4 * i]
            if name == "pool_w":
                for gi in range(4):
                    g_out[gi] = tot[_ROW_POOL_W + gi * 128:_ROW_POOL_W + (gi + 1) * 128, :]
            elif name == "conv_w":
                for r in range(CONV_WIDTH):
                    g_out[r:r + 1, :] = _take_rows(conv, r * 6, 768)
            else:
                g_out[...] = grads[name]
            _adam_update(g_out, w_refs[i], m_refs[i], v_refs[i], *outs[4 * i + 1:4 * i + 4])

    vmem = pl.BlockSpec(memory_space=pltpu.VMEM)
    out_shape = [jax.ShapeDtypeStruct(w.shape, F32) for w in ws for _ in range(4)] + [jax.ShapeDtypeStruct((1, LANES), F32)]
    return pl.pallas_call(
        body, name="small_update", in_specs=[vmem] * (1 + 3 * n), out_specs=[vmem] * (4 * n + 1), out_shape=out_shape,
        scratch_shapes=[pltpu.VMEM((_SMALL_ROWS, LANES), F32)],
        compiler_params=pltpu.CompilerParams(vmem_limit_bytes=VMEM_LIMIT),
    )(small_all, *ws, *ms, *vs)


def _adam_update(g_ref, w_ref, m_ref, v_ref, d_ref, mo_ref, vo_ref):
    gv = g_ref[...]
    mn = ADAM_B1 * m_ref[...] + (1.0 - ADAM_B1) * gv
    vn = ADAM_B2 * v_ref[...] + (1.0 - ADAM_B2) * (gv * gv)
    mo_ref[...] = mn
    vo_ref[...] = vn
    c1 = 1.0 - ADAM_B1 ** ADAM_STEP
    c2 = 1.0 - ADAM_B2 ** ADAM_STEP
    d_ref[...] = -ADAM_LR * ((mn / c1) / (jnp.sqrt(vn / c2) + ADAM_EPS) + ADAM_WD * w_ref[...])


def _adamw_rows(g, w, m, v, *, name):
    rows, cols = g.shape
    tr = _tile(rows, 512)

    def body(*refs):
        _adam_update(*refs)

    blk = pl.BlockSpec((tr, cols), lambda i: (i, 0))
    out = jax.ShapeDtypeStruct((rows, cols), F32)
    return pl.pallas_call(
        body, name=name, grid=(rows // tr,), in_specs=[blk] * 4, out_specs=[blk] * 3, out_shape=[out] * 3,
        compiler_params=_params(("parallel",)),
    )(g, w, m, v)


_ADAM_ROWWISE = ("w_in_ab", "w_q_b", "w_kv_b", "w_out_ab", "w_in_c", "w_out_c")


_SHARD_AXIS = {"w_in_ab": 1, "w_q_b": 1, "w_kv_b": 1, "w_out_ab": 0, "w_in_c": 1, "w_out_c": 0, "conv_w": 1, "norm_c": 1}
_ALL_NAMES = ("norm_ab", "w_in_ab", "q_a_norm", "w_q_b", "kv_a_norm", "w_kv_b", "pool_w", "pool_scale", "w_out_ab",
              "norm_c", "w_in_c", "conv_w", "a_log", "dt_bias", "o_norm", "w_out_c", "final_norm")


def _join_shards(a, axis):
    _, r, c = a.shape
    return a.reshape(4 * r, c) if axis == 0 else jnp.transpose(a, (1, 0, 2)).reshape(r, 4 * c)


def _split_shards(a, axis):
    r, c = a.shape
    return a.reshape(4, r // 4, c) if axis == 0 else jnp.transpose(a.reshape(r, 4, c // 4), (1, 0, 2))


def kernel(x, positions, norm_ab, w_in_ab, q_a_norm, w_q_b, kv_a_norm, w_kv_b, pool_w, pool_scale, w_out_ab, norm_c, w_in_c, conv_w, a_log, dt_bias, o_norm, w_out_c, final_norm, loss_target, m_norm_ab, m_w_in_ab, m_q_a_norm, m_w_q_b, m_kv_a_norm, m_w_kv_b, m_pool_w, m_pool_scale, m_w_out_ab, m_norm_c, m_w_in_c, m_conv_w, m_a_log, m_dt_bias, m_o_norm, m_w_out_c, m_final_norm, v_norm_ab, v_w_in_ab, v_q_a_norm, v_w_q_b, v_kv_a_norm, v_w_kv_b, v_pool_w, v_pool_scale, v_w_out_ab, v_norm_c, v_w_in_c, v_conv_w, v_a_log, v_dt_bias, v_o_norm, v_w_out_c, v_final_norm):
    given = dict(locals())
    c = lax.axis_index("c")
    t = x.shape[1]

    def shard_of(prefix, name):
        a = given[prefix + name]
        return a.reshape(a.shape[1:]) if a.ndim > 2 else a.reshape(1, -1)

    big, big_even, big_odd, small_sharded = _ADAM_ROWWISE, _ADAM_ROWWISE[:4], _ADAM_ROWWISE[4:], ("conv_w", "norm_c")
    chip = 2 * lax.axis_index("x") + lax.axis_index("y")
    core = c.astype(jnp.int32).reshape(1)
    late = big_odd + small_sharded
    late_shards = [shard_of("", n).astype(BF16) for n in big_odd] + [shard_of("", n) for n in small_sharded]
    gather_odd = _to_chips_start(late_shards, per_chip_slot=False, name="gather_odd_start")
    gathered = _gather_weights([shard_of("", n).astype(BF16) for n in big_even], [])
    full = {n: _join_shards(a, _SHARD_AXIS[n]) for n, a in zip(big_even, gathered)}
    for name in ("norm_ab", "q_a_norm", "kv_a_norm", "pool_w", "pool_scale"):
        full[name] = shard_of("", name)
    lw = _layout_even(full)
    lw["norm_ab"] = lw["norm_ab"] + gather_odd[-1][0, 0]

    def odd_weights(h1):
        landed = _to_chips_wait(gather_odd, h1, per_chip_slot=False, name="gather_odd_wait")
        w = {}
        for name, land, own in zip(late, landed, late_shards):
            w[name] = _join_shards(lax.dynamic_update_index_in_dim(land, own, chip, 0), _SHARD_AXIS[name])
        for name in ("a_log", "dt_bias", "o_norm", "final_norm"):
            w[name] = shard_of("", name)
        return _layout_odd(w)

    def chip_partials(names, grads, tag):
        slots = [_split_shards(grads[n], _SHARD_AXIS[n]) for n in names]
        partial = _core_swap_partial(slots, name="core_swap_partial_" + tag)
        return [_core_sum(s, p, core, name="core_sum_" + n) for n, s, p in zip(names, slots, partial)]

    groups = {"odd": big_odd, "out_ab": ("w_out_ab",), "in_ab": ("w_in_ab", "w_q_b", "w_kv_b")}
    sent = {}

    def on_grads(tag, g):
        part = chip_partials(groups[tag], _unlayout_grads(g, groups[tag]), tag)
        sent[tag] = (part, _to_chips_start(part, per_chip_slot=True, name="exchange_" + tag + "_start"))
        return sent[tag][1][-1][0, 0]

    loss_tile, dx, g = _local_step(x[0], positions.reshape(t, 1), loss_target[0], lw, odd_weights, on_grads)
    small_all = _chip_exchange([], _pack_small(g, loss_tile))[-1]
    halves = {}
    for tag, names in groups.items():
        part, started = sent[tag]
        landed = _to_chips_wait(started, small_all, per_chip_slot=True, name="exchange_" + tag + "_wait")
        for n, l, p in zip(names, landed, part):
            l = lax.dynamic_update_index_in_dim(l, lax.dynamic_index_in_dim(p, chip, 0, keepdims=False), chip, 0)
            halves[n] = _chip_sum(l, core, name="chip_sum_" + n)
    gbig = dict(zip(big, _core_swap_sum([halves[n] for n in big])))

    res = {}
    for name in big:
        res["grad", name] = gbig[name]
        out = _adamw_rows(gbig[name], shard_of("", name), shard_of("m_", name), shard_of("v_", name), name="adamw_" + name)
        res["delta", name], res["m", name], res["v", name] = out
    out = _small_update(small_all, [shard_of("", n) for n in _SMALL_NAMES], [shard_of("m_", n) for n in _SMALL_NAMES],
                        [shard_of("v_", n) for n in _SMALL_NAMES])
    for i, name in enumerate(_SMALL_NAMES):
        res["grad", name], res["delta", name], res["m", name], res["v", name] = out[4 * i:4 * i + 4]
    res = {k: a.reshape(given[k[1]].shape) for k, a in res.items()}
    loss = out[-1][0, 0]
    outs = [loss, dx.reshape(x.shape)]
    for key in ("grad", "delta", "m", "v"):
        outs += [res[key, n] for n in _ALL_NAMES]
    return tuple(outs)
```

```python
import functools

import jax
import jax.numpy as jnp
from jax import lax
from jax.experimental import pallas as pl
from jax.experimental.pallas import tpu as pltpu

F32 = jnp.float32
BF16 = jnp.bfloat16
HI = lax.Precision.HIGHEST
MESH = pl.DeviceIdType.MESH

RMS_EPS = 1e-6
MLA_HEADS = 8
MLA_Q_RANK = 256
MLA_KV_RANK = 128
MLA_NOPE = 64
MLA_ROPE = 32
MLA_V = 64
ROPE_THETA = 10000.0
POOL_WINDOWS = (2, 4, 8, 16)
POOL_GROUP = 128
POOL_WIDTH = 512
POOL_HALO = 16
GDN_HEADS = 8
GDN_DK = 128
CONV_WIDTH = 4
CONV_HALO = 8
CHUNK = 64
IN_AB_PAD = 2048
IN_C_PAD = 4224
ATT_SCALE = (MLA_NOPE + MLA_ROPE) ** -0.5

ADAM_LR = 0.001
ADAM_B1 = 0.9
ADAM_B2 = 0.999
ADAM_EPS = 1e-08
ADAM_WD = 0.01
ADAM_STEP = 10

LANES = 128
VMEM_LIMIT = 56 * 1024 * 1024

ROW_TILE = 256
ATT_TILE = 1024
GDN_TILE = 256
MM_TILE = (1024, 1408, 2048)

NN = (((1,), (0,)), ((), ()))
NT = (((1,), (1,)), ((), ()))
TN = (((0,), (0,)), ((), ()))


def _dot(a, b, dims=NN, prec=None):
    return lax.dot_general(a, b, dims, precision=prec, preferred_element_type=F32)


def _tile(n, pref):
    if n <= pref:
        return n
    step = LANES if pref >= LANES else 8
    for t in range(pref - pref % step, 0, -step):
        if n % t == 0:
            return t
    return n


def _params(sem):
    return pltpu.CompilerParams(dimension_semantics=sem, vmem_limit_bytes=VMEM_LIMIT)


def _sigmoid(x):
    return 0.5 * jnp.tanh(0.5 * x) + 0.5


def _softplus(x):
    return jnp.maximum(x, 0.0) + jnp.log(1.0 + jnp.exp(-jnp.abs(x)))


def _matmul(a, b, mode, *, name):
    if mode == "nn":
        (m, k), (k2, n) = a.shape, b.shape
    elif mode == "nt":
        (m, k), (n, k2) = a.shape, b.shape
    else:
        (k, m), (k2, n) = a.shape, b.shape
    assert k == k2, (a.shape, b.shape, mode)
    tm, tn, tk = _tile(m, MM_TILE[0]), _tile(n, MM_TILE[1]), _tile(k, MM_TILE[2])
    nk = k // tk
    if mode == "tn":
        a_spec = pl.BlockSpec((tk, tm), lambda i, j, kk: (kk, i))
    else:
        a_spec = pl.BlockSpec((tm, tk), lambda i, j, kk: (i, kk))
    if mode == "nt":
        b_spec = pl.BlockSpec((tn, tk), lambda i, j, kk: (j, kk))
    else:
        b_spec = pl.BlockSpec((tk, tn), lambda i, j, kk: (kk, j))
    o_spec = pl.BlockSpec((tm, tn), lambda i, j, kk: (i, j))
    dims = {"nn": NN, "nt": NT, "tn": TN}[mode]

    def body(a_ref, b_ref, o_ref, *scratch):
        if nk == 1:
            o_ref[...] = _dot(a_ref[...], b_ref[...], dims)
            return
        acc = scratch[0]
        kk = pl.program_id(2)

        @pl.when(kk == 0)
        def _():
            acc[...] = jnp.zeros_like(acc)

        acc[...] += _dot(a_ref[...], b_ref[...], dims)

        @pl.when(kk == nk - 1)
        def _():
            o_ref[...] = acc[...]

    return pl.pallas_call(
        body, name=name, grid=(m // tm, n // tn, nk), in_specs=[a_spec, b_spec], out_specs=o_spec,
        out_shape=jax.ShapeDtypeStruct((m, n), F32),
        scratch_shapes=[pltpu.VMEM((tm, tn), F32)] if nk > 1 else [],
        compiler_params=_params(("parallel", "parallel", "arbitrary")),
    )(a, b)


def _rms_in_proj(h, g, w, *, name):
    t, d = h.shape
    n = w.shape[1]
    tm, tn = _tile(t, MM_TILE[0]), _tile(n, MM_TILE[1])

    def body(h_ref, g_ref, w_ref, o_ref, hn_ref):
        @pl.when(pl.program_id(1) == 0)
        def _():
            x = h_ref[...]
            r = lax.rsqrt(jnp.mean(x * x, axis=-1, keepdims=True) + RMS_EPS)
            hn_ref[...] = (x * r * g_ref[...]).astype(BF16)

        o_ref[...] = _dot(hn_ref[...], w_ref[...])

    return pl.pallas_call(
        body, name=name, grid=(t // tm, n // tn),
        in_specs=[pl.BlockSpec((tm, d), lambda i, j: (i, 0)), pl.BlockSpec((1, d), lambda i, j: (0, 0)),
                  pl.BlockSpec((d, tn), lambda i, j: (0, j))],
        out_specs=[pl.BlockSpec((tm, tn), lambda i, j: (i, j)), pl.BlockSpec((tm, d), lambda i, j: (i, 0))],
        out_shape=[jax.ShapeDtypeStruct((t, n), F32), jax.ShapeDtypeStruct((t, d), BF16)],
        compiler_params=_params(("parallel", "arbitrary")),
    )(h, g, w)


def _matmul_rms_bwd(dproj, w, h, g, dres, *, name, with_bf16):
    t, k = dproj.shape
    d = w.shape[0]
    tm = _tile(t, 2 * ROW_TILE)

    def body(dp_ref, w_ref, h_ref, g_ref, dres_ref, *outs):
        i = pl.program_id(0)
        dh_ref, dg_ref = outs[0], outs[-1]
        dyv = _dot(dp_ref[...], w_ref[...], NT)
        x = h_ref[...]
        r = lax.rsqrt(jnp.mean(x * x, axis=-1, keepdims=True) + RMS_EPS)
        xh = x * r
        dxh = dyv * g_ref[...]
        dh = dres_ref[...] + r * (dxh - xh * jnp.mean(dxh * xh, axis=-1, keepdims=True))
        dh_ref[...] = dh
        if with_bf16:
            outs[1][...] = dh.astype(BF16)

        @pl.when(i == 0)
        def _():
            dg_ref[...] = jnp.zeros_like(dg_ref)

        dg_ref[...] += jnp.sum(dyv * xh, axis=0, keepdims=True)

    row = pl.BlockSpec((tm, d), lambda i: (i, 0))
    vec = pl.BlockSpec((1, d), lambda i: (0, 0))
    out_shape = [jax.ShapeDtypeStruct((t, d), F32)] + ([jax.ShapeDtypeStruct((t, d), BF16)] if with_bf16 else [])
    out_specs = [row] * len(out_shape) + [vec]
    out_shape.append(jax.ShapeDtypeStruct((1, d), F32))
    return pl.pallas_call(
        body, name=name, grid=(t // tm,),
        in_specs=[pl.BlockSpec((tm, k), lambda i: (i, 0)), pl.BlockSpec((d, k), lambda i: (0, 0)), row, vec, row],
        out_specs=out_specs, out_shape=out_shape, compiler_params=_params(("arbitrary",)),
    )(dproj, w, h, g, dres)


def _rope_partner(x):
    lane = lax.broadcasted_iota(jnp.int32, x.shape, 1)
    swapped = jnp.where(lane < MLA_NOPE + MLA_ROPE // 2, pltpu.roll(x, LANES - 16, 1), pltpu.roll(x, 16, 1))
    return jnp.where((lane >= MLA_NOPE) & (lane < MLA_NOPE + MLA_ROPE), swapped, 0.0)


def _pool_counts(row0, tm, w):
    t_idx = row0 + lax.broadcasted_iota(jnp.int32, (tm, POOL_GROUP), 0)
    return jnp.minimum(t_idx + 1, w).astype(F32)


def _ab_prep(proj, pos, inv_freq, q_a_norm, kv_a_norm, wq, wk, wv, wpool, *, name):
    t = proj.shape[0]
    tm = _tile(t, ROW_TILE)
    hb = tm // POOL_HALO

    def body(p_ref, halo_ref, pos_ref, inv_ref, qg_ref, kg_ref, wq_ref, wk_ref, wv_ref, wp_ref,
             q_ref, k_ref, v_ref, yb_ref, qn_ref, kvn_ref, d_ref, cos_ref, sin_ref, ext):
        i = pl.program_id(0)
        ql = p_ref[:, 0:MLA_Q_RANK]
        r = lax.rsqrt(jnp.mean(ql * ql, axis=-1, keepdims=True) + RMS_EPS)
        qn = (ql * r * qg_ref[...]).astype(BF16)
        qn_ref[...] = qn
        kl = p_ref[:, MLA_Q_RANK:MLA_Q_RANK + MLA_KV_RANK]
        r = lax.rsqrt(jnp.mean(kl * kl, axis=-1, keepdims=True) + RMS_EPS)
        kvn = (kl * r * kg_ref[...]).astype(BF16)
        kvn_ref[...] = kvn
        ang = pos_ref[...].astype(F32) * inv_ref[...]
        lane = lax.broadcasted_iota(jnp.int32, (tm, LANES), 1)
        in_rope = (lane >= MLA_NOPE) & (lane < MLA_NOPE + MLA_ROPE)
        cos_t = jnp.where(in_rope, jnp.cos(ang), 1.0)
        sin_t = jnp.where(in_rope, jnp.sin(ang), 0.0)
        sin_t = jnp.where(lane < MLA_NOPE + MLA_ROPE // 2, -sin_t, sin_t)
        cos_ref[...] = cos_t
        sin_ref[...] = sin_t
        kr = p_ref[:, 384:512]
        kr = kr * cos_t + _rope_partner(kr) * sin_t
        qraw = _dot(qn, wq_ref[...])
        kvk = _dot(kvn, wk_ref[...])
        for h in range(MLA_HEADS):
            sl = slice(h * LANES, (h + 1) * LANES)
            qh = qraw[:, sl]
            q_ref[:, sl] = ((qh * cos_t + _rope_partner(qh) * sin_t) * ATT_SCALE).astype(BF16)
            k_ref[:, sl] = (kvk[:, sl] + kr).astype(BF16)
        v_ref[...] = _dot(kvn, wv_ref[...]).astype(BF16)
        xp = p_ref[:, 512:1024]
        ext[0:POOL_HALO, :] = jnp.where(i > 0, halo_ref[...], 0.0)
        ext[POOL_HALO:POOL_HALO + tm, :] = xp
        for g, w in enumerate(POOL_WINDOWS):
            lo = g * POOL_GROUP
            acc = ext[POOL_HALO:POOL_HALO + tm, lo:lo + POOL_GROUP]
            for s in range(1, w):
                acc = acc + ext[POOL_HALO - s:POOL_HALO - s + tm, lo:lo + POOL_GROUP]
            cnt = _pool_counts(i * tm, tm, w)
            d_ref[:, lo:lo + POOL_GROUP] = (acc / cnt - xp[:, lo:lo + POOL_GROUP]).astype(BF16)
        yb_ref[...] = _dot(d_ref[...], wp_ref[...])

    row = lambda w: pl.BlockSpec((tm, w), lambda i: (i, 0))
    vec = lambda w: pl.BlockSpec((1, w), lambda i: (0, 0))
    whole = lambda a: pl.BlockSpec(a.shape, lambda i: (0, 0))
    return pl.pallas_call(
        body, name=name, grid=(t // tm,),
        in_specs=[row(1024), pl.BlockSpec((POOL_HALO, POOL_WIDTH), lambda i: (jnp.maximum(i * hb - 1, 0), 1)),
                  pl.BlockSpec((tm, 1), lambda i: (i, 0)), vec(LANES), vec(MLA_Q_RANK), vec(MLA_KV_RANK),
                  whole(wq), whole(wk), whole(wv), whole(wpool)],
        out_specs=[row(1024), row(1024), row(512), row(512), row(MLA_Q_RANK), row(MLA_KV_RANK), row(POOL_WIDTH),
                   row(LANES), row(LANES)],
        out_shape=[jax.ShapeDtypeStruct((t, 1024), BF16), jax.ShapeDtypeStruct((t, 1024), BF16),
                   jax.ShapeDtypeStruct((t, 512), BF16), jax.ShapeDtypeStruct((t, 512), F32),
                   jax.ShapeDtypeStruct((t, MLA_Q_RANK), BF16), jax.ShapeDtypeStruct((t, MLA_KV_RANK), BF16),
                   jax.ShapeDtypeStruct((t, POOL_WIDTH), BF16), jax.ShapeDtypeStruct((t, LANES), F32),
                   jax.ShapeDtypeStruct((t, LANES), F32)],
        scratch_shapes=[pltpu.VMEM((tm + POOL_HALO, POOL_WIDTH), F32)],
        compiler_params=_params(("parallel",)),
    )(proj, proj, pos, inv_freq, q_a_norm, kv_a_norm, wq, wk, wv, wpool)


def _ab_prep_bwd(proj, q_a_norm, kv_a_norm, dq, dk, dv, cos_t, sin_t, dyb, dz, wq, wk, wv, wpool, *, name):
    t = proj.shape[0]
    tm = _tile(t, ROW_TILE)
    hb = tm // POOL_HALO
    last_halo = t // POOL_HALO - 1
    nt = t // tm

    def body(p_ref, qg_ref, kg_ref, dq_ref, dk_ref, dv_ref, c_ref, s_ref, dyb_ref, dybn_ref, dz_ref,
             wq_ref, wk_ref, wv_ref, wp_ref, dp_ref, dqr_ref, dkb_ref, dqg_ref, dkg_ref, ext):
        i = pl.program_id(0)

        @pl.when(i == 0)
        def _():
            dqg_ref[...] = jnp.zeros_like(dqg_ref)
            dkg_ref[...] = jnp.zeros_like(dkg_ref)

        def norm_bwd(x, g, dy, dg_ref):
            r = lax.rsqrt(jnp.mean(x * x, axis=-1, keepdims=True) + RMS_EPS)
            xh = x * r
            dxh = dy * g
            dg_ref[...] += jnp.sum(dy * xh, axis=0, keepdims=True)
            return r * (dxh - xh * jnp.mean(dxh * xh, axis=-1, keepdims=True))

        c, s = c_ref[...], s_ref[...]
        lane = lax.broadcasted_iota(jnp.int32, (tm, LANES), 1)
        in_rope = (lane >= MLA_NOPE) & (lane < MLA_NOPE + MLA_ROPE)
        dkr = jnp.zeros((tm, LANES), F32)
        for h in range(MLA_HEADS):
            sl = slice(h * LANES, (h + 1) * LANES)
            g = dq_ref[:, sl]
            dqr_ref[:, sl] = ((g * c + _rope_partner(g * s)) * ATT_SCALE).astype(BF16)
            gk = dk_ref[:, sl]
            dkb_ref[:, sl] = gk.astype(BF16)
            dkr = dkr + jnp.where(in_rope, gk, 0.0)
        dkr = dkr * c + _rope_partner(dkr * s)
        dqn = _dot(dqr_ref[...], wq_ref[...], NT)
        dkvn = _dot(dkb_ref[...], wk_ref[...], NT) + _dot(dv_ref[...], wv_ref[...], NT)
        dql = norm_bwd(p_ref[:, 0:MLA_Q_RANK], qg_ref[...], dqn, dqg_ref)
        dp_ref[:, 0:MLA_Q_RANK] = dql.astype(BF16)
        dkl = norm_bwd(p_ref[:, MLA_Q_RANK:384], kg_ref[...], dkvn, dkg_ref)
        dp_ref[:, MLA_Q_RANK:384] = dkl.astype(BF16)
        dp_ref[:, 384:512] = dkr.astype(BF16)
        ddv = _dot(dyb_ref[...], wp_ref[...], NT)
        ddn = _dot(dybn_ref[...], wp_ref[...], NT)
        for g, w in enumerate(POOL_WINDOWS):
            lo = g * POOL_GROUP
            ext[0:tm, lo:lo + POOL_GROUP] = ddv[:, lo:lo + POOL_GROUP] / _pool_counts(i * tm, tm, w)
            nxt = ddn[:, lo:lo + POOL_GROUP] / _pool_counts((i + 1) * tm, POOL_HALO, w)
            ext[tm:tm + POOL_HALO, lo:lo + POOL_GROUP] = jnp.where(i < nt - 1, nxt, 0.0)
        for g, w in enumerate(POOL_WINDOWS):
            lo = g * POOL_GROUP
            acc = ext[0:tm, lo:lo + POOL_GROUP]
            for s in range(1, w):
                acc = acc + ext[s:s + tm, lo:lo + POOL_GROUP]
            dp_ref[:, 512 + lo:512 + lo + POOL_GROUP] = (acc - ddv[:, lo:lo + POOL_GROUP]).astype(BF16)
        dp_ref[:, 1024:2048] = dz_ref[...]

    row = lambda w: pl.BlockSpec((tm, w), lambda i: (i, 0))
    vec = lambda w: pl.BlockSpec((1, w), lambda i: (0, 0))
    whole = lambda a: pl.BlockSpec(a.shape, lambda i: (0, 0))
    return pl.pallas_call(
        body, name=name, grid=(nt,),
        in_specs=[row(1024), vec(MLA_Q_RANK), vec(MLA_KV_RANK), row(1024), row(1024), row(512), row(LANES), row(LANES),
                  row(POOL_WIDTH),
                  pl.BlockSpec((POOL_HALO, POOL_WIDTH), lambda i: (jnp.minimum((i + 1) * hb, last_halo), 0)),
                  row(1024), whole(wq), whole(wk), whole(wv), whole(wpool)],
        out_specs=[row(IN_AB_PAD), row(1024), row(1024), vec(MLA_Q_RANK), vec(MLA_KV_RANK)],
        out_shape=[jax.ShapeDtypeStruct((t, IN_AB_PAD), BF16), jax.ShapeDtypeStruct((t, 1024), BF16),
                   jax.ShapeDtypeStruct((t, 1024), BF16), jax.ShapeDtypeStruct((1, MLA_Q_RANK), F32),
                   jax.ShapeDtypeStruct((1, MLA_KV_RANK), F32)],
        scratch_shapes=[pltpu.VMEM((tm + POOL_HALO, POOL_WIDTH), F32)],
        compiler_params=_params(("arbitrary",)),
    )(proj, q_a_norm, kv_a_norm, dq, dk, dv, cos_t, sin_t, dyb, dyb, dz, wq, wk, wv, wpool)


def _gate_out_proj(o, ybraw, proj, pool_scale, w, hres, *, name):
    t = o.shape[0]
    tm = _tile(t, 2 * ROW_TILE)

    def body(o_ref, yb_ref, z_ref, ps_ref, w_ref, h_ref, ho_ref, y_ref):
        z = z_ref[...]
        sz = z * _sigmoid(z)
        y_ref[:, 0:512] = (o_ref[...] * sz[:, 0:512]).astype(BF16)
        y_ref[:, 512:1024] = (yb_ref[...] * ps_ref[...] * sz[:, 512:1024]).astype(BF16)
        ho_ref[...] = h_ref[...] + _dot(y_ref[...], w_ref[...])

    row = lambda w_: pl.BlockSpec((tm, w_), lambda i: (i, 0))
    return pl.pallas_call(
        body, name=name, grid=(t // tm,),
        in_specs=[row(512), row(512), pl.BlockSpec((tm, 1024), lambda i: (i, 1)), pl.BlockSpec((1, 512), lambda i: (0, 0)),
                  pl.BlockSpec(w.shape, lambda i: (0, 0)), row(1024)],
        out_specs=[row(1024), row(1024)],
        out_shape=[jax.ShapeDtypeStruct((t, 1024), F32), jax.ShapeDtypeStruct((t, 1024), BF16)],
        compiler_params=_params(("parallel",)),
    )(o, ybraw, proj, pool_scale, w, hres)


def _gate_bwd(dy, o, ybraw, proj, pool_scale, *, name):
    t = o.shape[0]
    tm = _tile(t, ROW_TILE)

    def body(dy_ref, o_ref, yb_ref, z_ref, ps_ref, do_ref, dl_ref, dyb_ref, dz_ref, dps_ref):
        i = pl.program_id(0)
        z = z_ref[...]
        sg = _sigmoid(z)
        sz = z * sg
        dsz = sg * (1.0 + z * (1.0 - sg))
        dyv = dy_ref[...]
        dcat = dyv * sz
        ov = o_ref[...]
        ybs = yb_ref[...] * ps_ref[...]
        dz_ref[:, 0:512] = (dyv[:, 0:512] * ov * dsz[:, 0:512]).astype(BF16)
        dz_ref[:, 512:1024] = (dyv[:, 512:1024] * ybs * dsz[:, 512:1024]).astype(BF16)
        do = dcat[:, 0:512]
        do_ref[...] = do.astype(BF16)
        r_i = lax.broadcasted_iota(jnp.int32, (512, 512), 0) // MLA_V
        c_i = lax.broadcasted_iota(jnp.int32, (512, 512), 1) // MLA_V
        dl_ref[...] = _dot(do * ov, (r_i == c_i).astype(F32), NN, HI)
        dyb_ref[...] = (dcat[:, 512:1024] * ps_ref[...]).astype(BF16)

        @pl.when(i == 0)
        def _():
            dps_ref[...] = jnp.zeros_like(dps_ref)

        dps_ref[...] += jnp.sum(dcat[:, 512:1024] * yb_ref[...], axis=0, keepdims=True)

    row = lambda w: pl.BlockSpec((tm, w), lambda i: (i, 0))
    vec = pl.BlockSpec((1, 512), lambda i: (0, 0))
    return pl.pallas_call(
        body, name=name, grid=(t // tm,),
        in_specs=[row(1024), row(512), row(512), pl.BlockSpec((tm, 1024), lambda i: (i, 1)), vec],
        out_specs=[row(512), row(512), row(512), row(1024), vec],
        out_shape=[jax.ShapeDtypeStruct((t, 512), BF16), jax.ShapeDtypeStruct((t, 512), F32),
                   jax.ShapeDtypeStruct((t, 512), BF16), jax.ShapeDtypeStruct((t, 1024), BF16),
                   jax.ShapeDtypeStruct((1, 512), F32)],
        compiler_params=_params(("arbitrary",)),
    )(dy, o, ybraw, proj, pool_scale)


ATT_HP_FWD = 4
ATT_HP_BWD = 2


def _diag_mask(tq):
    return lax.broadcasted_iota(jnp.int32, (tq, tq), 1) <= lax.broadcasted_iota(jnp.int32, (tq, tq), 0)


def _block_schedule(nq, key_major):
    if key_major:
        pairs = [(qi, ki) for ki in range(nq) for qi in range(ki, nq)]
    else:
        pairs = [(qi, ki) for qi in range(nq) for ki in range(qi + 1)]
    return jnp.asarray([p[0] for p in pairs], jnp.int32), jnp.asarray([p[1] for p in pairs], jnp.int32)


def _attn_fwd(q, k, v, *, name):
    t = q.shape[0]
    tq = _tile(t, ATT_TILE)
    nq = t // tq
    hp = ATT_HP_FWD
    qi_tab, ki_tab = _block_schedule(nq, key_major=False)

    def body(qi_ref, ki_ref, q_ref, k_ref, v_ref, o_ref, lse_ref, m_sc, l_sc, acc_sc):
        step = pl.program_id(1)
        qi, ki = qi_ref[step], ki_ref[step]

        @pl.when(ki == 0)
        def _():
            m_sc[...] = jnp.full_like(m_sc, -jnp.inf)
            l_sc[...] = jnp.zeros_like(l_sc)
            acc_sc[...] = jnp.zeros_like(acc_sc)

        def block(on_diagonal):
            scores = []
            for h in range(hp):
                sl = slice(h * LANES, (h + 1) * LANES)
                scores.append(_dot(q_ref[:, sl], k_ref[:, sl], NT))
            if on_diagonal:
                mask = _diag_mask(tq)
                scores = [jnp.where(mask, s, -jnp.inf) for s in scores]
            for h, s in enumerate(scores):
                vv = v_ref[:, (h // 2) * LANES:(h // 2 + 1) * LANES]
                m_prev = m_sc[h]
                m_new = jnp.maximum(m_prev, jnp.max(s, axis=-1, keepdims=True))
                alpha = jnp.exp(m_prev - m_new)
                p = jnp.exp(s - m_new[:, 0:1])
                l_sc[h] = alpha * l_sc[h] + jnp.sum(p, axis=-1, keepdims=True)
                acc_sc[h] = alpha * acc_sc[h] + _dot(p.astype(BF16), vv)
                m_sc[h] = m_new

        pl.when(ki < qi)(functools.partial(block, False))
        pl.when(ki == qi)(functools.partial(block, True))

        @pl.when(ki == qi)
        def _():
            first = lax.broadcasted_iota(jnp.int32, (tq, LANES), 1) < MLA_V
            for pr in range(hp // 2):
                a, b = 2 * pr, 2 * pr + 1
                sl = slice(pr * LANES, (pr + 1) * LANES)
                o_ref[:, sl] = jnp.where(first, acc_sc[a] / l_sc[a], acc_sc[b] / l_sc[b])
                lse_ref[:, sl] = jnp.where(first, m_sc[a] + jnp.log(l_sc[a]), m_sc[b] + jnp.log(l_sc[b]))

    grid_spec = pltpu.PrefetchScalarGridSpec(
        num_scalar_prefetch=2, grid=(MLA_HEADS // hp, qi_tab.shape[0]),
        in_specs=[pl.BlockSpec((tq, hp * LANES), lambda g, s, qt, kt: (qt[s], g)),
                  pl.BlockSpec((tq, hp * LANES), lambda g, s, qt, kt: (kt[s], g)),
                  pl.BlockSpec((tq, hp * MLA_V), lambda g, s, qt, kt: (kt[s], g))],
        out_specs=[pl.BlockSpec((tq, hp * MLA_V), lambda g, s, qt, kt: (qt[s], g)),
                   pl.BlockSpec((tq, hp * MLA_V), lambda g, s, qt, kt: (qt[s], g))],
        scratch_shapes=[pltpu.VMEM((hp, tq, LANES), F32)] * 3,
    )
    return pl.pallas_call(
        body, name=name, grid_spec=grid_spec,
        out_shape=[jax.ShapeDtypeStruct((t, 512), F32), jax.ShapeDtypeStruct((t, 512), F32)],
        compiler_params=_params(("parallel", "arbitrary")),
    )(qi_tab, ki_tab, q, k, v)


def _attn_bwd(q, k, v, do, lse, delta, *, name):
    t = q.shape[0]
    tq = _tile(t, ATT_TILE)
    nq = t // tq
    hp = ATT_HP_BWD
    qi_tab, ki_tab = _block_schedule(nq, key_major=True)

    def body(qi_ref, ki_ref, q_ref, k_ref, v_ref, do_ref, lse_ref, dl_ref, dq_ref, dk_ref, dv_ref, dk_sc, dv_sc):
        step = pl.program_id(1)
        qi, ki = qi_ref[step], ki_ref[step]

        @pl.when(step == 0)
        def _():
            dq_ref[...] = jnp.zeros_like(dq_ref)

        @pl.when(qi == ki)
        def _():
            dk_sc[...] = jnp.zeros_like(dk_sc)
            dv_sc[...] = jnp.zeros_like(dv_sc)

        def block(on_diagonal):
            lane = lax.broadcasted_iota(jnp.int32, (tq, LANES), 1)
            rows = pl.ds(pl.multiple_of(qi * tq, tq), tq)
            heads = [slice(h * LANES, (h + 1) * LANES) for h in range(hp)]
            scores = [_dot(q_ref[:, sl], k_ref[:, sl], NT) for sl in heads]
            dps = []
            for h in range(hp):
                dov = do_ref[:, (h // 2) * LANES:(h // 2 + 1) * LANES]
                mine = (lane < MLA_V) if h % 2 == 0 else (lane >= MLA_V)
                dps.append(_dot(jnp.where(mine, dov, jnp.zeros_like(dov)), v_ref[:, (h // 2) * LANES:(h // 2 + 1) * LANES], NT))
            mask = _diag_mask(tq) if on_diagonal else None
            for h, sl in enumerate(heads):
                col = (h // 2) * LANES + (h % 2) * MLA_V
                p = jnp.exp(scores[h] - lse_ref[:, col:col + 1])
                if on_diagonal:
                    p = jnp.where(mask, p, 0.0)
                ds = (p * (dps[h] - dl_ref[:, col:col + 1])).astype(BF16)
                dv_sc[h] += _dot(p.astype(BF16), do_ref[:, (h // 2) * LANES:(h // 2 + 1) * LANES], TN)
                dk_sc[h] += _dot(ds, q_ref[:, sl], TN)
                dq_ref[rows, sl] += _dot(ds, k_ref[:, sl], NN)

        pl.when(qi > ki)(functools.partial(block, False))
        pl.when(qi == ki)(functools.partial(block, True))

        @pl.when(qi == nq - 1)
        def _():
            first = lax.broadcasted_iota(jnp.int32, (tq, LANES), 1) < MLA_V
            for h in range(hp):
                dk_ref[:, h * LANES:(h + 1) * LANES] = dk_sc[h]
            for pr in range(hp // 2):
                dv_ref[:, pr * LANES:(pr + 1) * LANES] = jnp.where(first, dv_sc[2 * pr], dv_sc[2 * pr + 1]).astype(BF16)

    qrow = lambda w: pl.BlockSpec((tq, w), lambda g, s, qt, kt: (qt[s], g))
    krow = lambda w: pl.BlockSpec((tq, w), lambda g, s, qt, kt: (kt[s], g))
    grid_spec = pltpu.PrefetchScalarGridSpec(
        num_scalar_prefetch=2, grid=(MLA_HEADS // hp, qi_tab.shape[0]),
        in_specs=[qrow(hp * LANES), krow(hp * LANES), krow(hp * MLA_V), qrow(hp * MLA_V), qrow(hp * MLA_V), qrow(hp * MLA_V)],
        out_specs=[pl.BlockSpec((t, hp * LANES), lambda g, s, qt, kt: (0, g)), krow(hp * LANES), krow(hp * MLA_V)],
        scratch_shapes=[pltpu.VMEM((hp, tq, LANES), F32), pltpu.VMEM((hp, tq, LANES), F32)],
    )
    return pl.pallas_call(
        body, name=name, grid_spec=grid_spec,
        out_shape=[jax.ShapeDtypeStruct((t, 1024), F32), jax.ShapeDtypeStruct((t, 1024), F32),
                   jax.ShapeDtypeStruct((t, 512), BF16)],
        compiler_params=_params(("parallel", "arbitrary")),
    )(qi_tab, ki_tab, q, k, v, do, lse, delta)


def _conv_rows(ext, tm, w_ref, sec):
    c0 = sec * 1024
    y = ext[CONV_HALO - 3:CONV_HALO - 3 + tm, c0:c0 + 1024] * w_ref[0:1, c0:c0 + 1024]
    for j in range(1, CONV_WIDTH):
        y = y + ext[CONV_HALO - 3 + j:CONV_HALO - 3 + j + tm, c0:c0 + 1024] * w_ref[j:j + 1, c0:c0 + 1024]
    return y


def _c_prep(proj_c, conv_w, a_log, dt_bias, *, name):
    t = proj_c.shape[0]
    tm = _tile(t, ROW_TILE)
    hb = tm // CONV_HALO

    def body(p_ref, halo_ref, ab_ref, w_ref, al_ref, dtb_ref, q_ref, k_ref, v_ref, g_ref, b_ref, gt_ref, ext):
        i = pl.program_id(0)
        ext[0:CONV_HALO, :] = jnp.where(i > 0, halo_ref[...], 0.0)
        ext[CONV_HALO:CONV_HALO + tm, :] = p_ref[...]
        for sec, o_ref in enumerate((q_ref, k_ref, v_ref)):
            y = _conv_rows(ext, tm, w_ref, sec)
            y = y * _sigmoid(y)
            if sec == 2:
                o_ref[...] = y
                continue
            scale = GDN_DK ** -0.5 if sec == 0 else 1.0
            for h in range(GDN_HEADS):
                sl = slice(h * LANES, (h + 1) * LANES)
                blk = y[:, sl]
                r = lax.rsqrt(jnp.sum(blk * blk, axis=-1, keepdims=True) + RMS_EPS)
                o_ref[:, sl] = blk * (r * scale)
        ab = ab_ref[...]
        g = -jnp.exp(al_ref[...]) * _softplus(ab + dtb_ref[...])
        beta = _sigmoid(ab)
        ri = lax.broadcasted_iota(jnp.int32, (tm, tm), 0)
        ci = lax.broadcasted_iota(jnp.int32, (tm, tm), 1)
        lower = ((ri // CHUNK) == (ci // CHUNK)) & (ri >= ci)
        gc = _dot(lower.astype(F32), g, NN, HI)
        eye = lax.broadcasted_iota(jnp.int32, (LANES, LANES), 0) == lax.broadcasted_iota(jnp.int32, (LANES, LANES), 1)
        gt_ref[...] = _dot(eye.astype(F32), gc, NT, HI)[0:GDN_HEADS, :]
        for h in range(GDN_HEADS):
            sl = slice(h * LANES, (h + 1) * LANES)
            g_ref[:, sl] = jnp.broadcast_to(gc[:, h:h + 1], (tm, LANES))
            b_ref[:, sl] = jnp.broadcast_to(beta[:, GDN_HEADS + h:GDN_HEADS + h + 1], (tm, LANES))

    row = lambda w: pl.BlockSpec((tm, w), lambda i: (i, 0))
    vec = lambda r, w: pl.BlockSpec((r, w), lambda i: (0, 0))
    out = jax.ShapeDtypeStruct((t, 1024), F32)
    return pl.pallas_call(
        body, name=name, grid=(t // tm,),
        in_specs=[row(3072), pl.BlockSpec((CONV_HALO, 3072), lambda i: (jnp.maximum(i * hb - 1, 0), 0)),
                  pl.BlockSpec((tm, LANES), lambda i: (i, 32)), vec(CONV_WIDTH, 3072), vec(1, LANES), vec(1, LANES)],
        out_specs=[row(1024)] * 5 + [pl.BlockSpec((GDN_HEADS, tm), lambda i: (0, i))],
        out_shape=[out] * 5 + [jax.ShapeDtypeStruct((GDN_HEADS, t), F32)],
        scratch_shapes=[pltpu.VMEM((tm + CONV_HALO, 3072), F32)],
        compiler_params=_params(("parallel",)),
    )(proj_c, proj_c, proj_c, conv_w, a_log, dt_bias)


def _c_prep_bwd(proj_c, conv_w, a_log, dt_bias, dq, dk, dv, dgb, dbb, dz, *, name):
    t = proj_c.shape[0]
    tm = _tile(t, ROW_TILE // 2)
    hb = tm // CONV_HALO
    nt = t // tm
    rev = lambda i: nt - 1 - i

    def body(p_ref, halo_ref, ab_ref, w_ref, al_ref, dtb_ref, dq_ref, dk_ref, dv_ref, dg_ref, db_ref, dz_ref,
             dp_ref, dw_ref, dal_ref, ddt_ref, ext, dyext, carry, taps):
        step = pl.program_id(0)
        i = rev(step)

        @pl.when(step == 0)
        def _():
            dw_ref[...] = jnp.zeros_like(dw_ref)
            dal_ref[...] = jnp.zeros_like(dal_ref)
            ddt_ref[...] = jnp.zeros_like(ddt_ref)
            carry[...] = jnp.zeros_like(carry)

        ext[0:CONV_HALO, :] = jnp.where(i > 0, halo_ref[...], 0.0)
        ext[CONV_HALO:CONV_HALO + tm, :] = p_ref[...]
        for sec, g_ref in enumerate((dq_ref, dk_ref, dv_ref)):
            c0 = sec * 1024
            for j in range(CONV_WIDTH):
                taps[j] = ext[CONV_HALO - 3 + j:CONV_HALO - 3 + j + tm, c0:c0 + 1024]
            y = taps[0] * w_ref[0:1, c0:c0 + 1024]
            for j in range(1, CONV_WIDTH):
                y = y + taps[j] * w_ref[j:j + 1, c0:c0 + 1024]
            sg = _sigmoid(y)
            act = y * sg
            if sec == 2:
                dact = g_ref[...]
            else:
                scale = GDN_DK ** -0.5 if sec == 0 else 1.0
                parts = []
                for h in range(GDN_HEADS):
                    sl = slice(h * LANES, (h + 1) * LANES)
                    blk = act[:, sl]
                    r = lax.rsqrt(jnp.sum(blk * blk, axis=-1, keepdims=True) + RMS_EPS)
                    n = blk * r
                    dn = g_ref[:, sl] * scale
                    parts.append(r * (dn - n * jnp.sum(dn * n, axis=-1, keepdims=True)))
                dact = jnp.concatenate(parts, axis=-1)
            dy = dact * (sg * (1.0 + y * (1.0 - sg)))
            dyext[0:tm, c0:c0 + 1024] = dy
            for j in range(CONV_WIDTH):
                dw_ref[j:j + 1, c0:c0 + 1024] += jnp.sum(dy * taps[j], axis=0, keepdims=True)
        dyext[tm:tm + CONV_HALO, :] = carry[...]
        carry[...] = dyext[0:CONV_HALO, :]
        for sec in range(3):
            c0 = sec * 1024
            dx = dyext[3:3 + tm, c0:c0 + 1024] * w_ref[0:1, c0:c0 + 1024]
            for j in range(1, CONV_WIDTH):
                dx = dx + dyext[3 - j:3 - j + tm, c0:c0 + 1024] * w_ref[j:j + 1, c0:c0 + 1024]
            dp_ref[:, c0:c0 + 1024] = dx.astype(BF16)
        dp_ref[:, 3072:4096] = dz_ref[...]
        lane = lax.broadcasted_iota(jnp.int32, (tm, LANES), 1)
        dg = jnp.zeros((tm, LANES), F32)
        dbeta = jnp.zeros((tm, LANES), F32)
        for h in range(GDN_HEADS):
            sl = slice(h * LANES, (h + 1) * LANES)
            dg = dg + jnp.where(lane == h, dg_ref[:, sl], 0.0)
            dbeta = dbeta + jnp.where(lane == GDN_HEADS + h, db_ref[:, sl], 0.0)
        ri = lax.broadcasted_iota(jnp.int32, (tm, tm), 0)
        ci = lax.broadcasted_iota(jnp.int32, (tm, tm), 1)
        upper = ((ri // CHUNK) == (ci // CHUNK)) & (ri <= ci)
        dg = _dot(upper.astype(F32), dg, NN, HI)
        pre = ab_ref[...] + dtb_ref[...]
        s = _sigmoid(pre)
        a_exp = jnp.exp(al_ref[...])
        dg_da = dg * (-a_exp * s)
        dp_ref[:, 4096:IN_C_PAD] = (dg_da + dbeta * s * (1.0 - s)).astype(BF16)
        dal_ref[...] += jnp.sum(dg * (-a_exp * _softplus(pre)), axis=0, keepdims=True)
        ddt_ref[...] += jnp.sum(dg_da, axis=0, keepdims=True)

    row = lambda w: pl.BlockSpec((tm, w), lambda s: (rev(s), 0))
    vec = lambda r, w: pl.BlockSpec((r, w), lambda s: (0, 0))
    return pl.pallas_call(
        body, name=name, grid=(nt,),
        in_specs=[row(3072), pl.BlockSpec((CONV_HALO, 3072), lambda s: (jnp.maximum(rev(s) * hb - 1, 0), 0)),
                  pl.BlockSpec((tm, LANES), lambda s: (rev(s), 32)), vec(CONV_WIDTH, 3072), vec(1, LANES), vec(1, LANES),
                  row(1024), row(1024), row(1024), row(1024), row(1024), row(1024)],
        out_specs=[row(IN_C_PAD), vec(CONV_WIDTH, 3072), vec(1, LANES), vec(1, LANES)],
        out_shape=[jax.ShapeDtypeStruct((t, IN_C_PAD), BF16), jax.ShapeDtypeStruct((CONV_WIDTH, 3072), F32),
                   jax.ShapeDtypeStruct((1, LANES), F32), jax.ShapeDtypeStruct((1, LANES), F32)],
        scratch_shapes=[pltpu.VMEM((tm + CONV_HALO, 3072), F32), pltpu.VMEM((tm + CONV_HALO, 3072), F32),
                        pltpu.VMEM((CONV_HALO, 3072), F32), pltpu.VMEM((CONV_WIDTH, tm, 1024), F32)],
        compiler_params=_params(("arbitrary",)),
    )(proj_c, proj_c, proj_c, conv_w, a_log, dt_bias, dq, dk, dv, dgb, dbb, dz)


def _o_gate_bwd(dy, o, proj_c, o_norm, *, name):
    t = o.shape[0]
    tm = _tile(t, 2 * ROW_TILE)

    def body(dy_ref, o_ref, z_ref, g_ref, do_ref, dz_ref, dg_ref):
        i = pl.program_id(0)

        @pl.when(i == 0)
        def _():
            dg_ref[...] = jnp.zeros_like(dg_ref)

        dg = jnp.zeros((1, LANES), F32)
        for h in range(GDN_HEADS):
            sl = slice(h * LANES, (h + 1) * LANES)
            x = o_ref[:, sl]
            r = lax.rsqrt(jnp.mean(x * x, axis=-1, keepdims=True) + RMS_EPS)
            xh = x * r
            z = z_ref[:, sl]
            sg = _sigmoid(z)
            dyv = dy_ref[:, sl]
            dn = dyv * (z * sg)
            dz_ref[:, sl] = (dyv * xh * g_ref[...] * (sg * (1.0 + z * (1.0 - sg)))).astype(BF16)
            dxh = dn * g_ref[...]
            do_ref[:, sl] = r * (dxh - xh * jnp.mean(dxh * xh, axis=-1, keepdims=True))
            dg = dg + jnp.sum(dn * xh, axis=0, keepdims=True)
        dg_ref[...] += dg

    row = pl.BlockSpec((tm, 1024), lambda i: (i, 0))
    vec = pl.BlockSpec((1, LANES), lambda i: (0, 0))
    return pl.pallas_call(
        body, name=name, grid=(t // tm,), in_specs=[row, row, pl.BlockSpec((tm, 1024), lambda i: (i, 3)), vec],
        out_specs=[row, row, vec],
        out_shape=[jax.ShapeDtypeStruct((t, 1024), F32), jax.ShapeDtypeStruct((t, 1024), BF16),
                   jax.ShapeDtypeStruct((1, LANES), F32)],
        compiler_params=_params(("arbitrary",)),
    )(dy, o, proj_c, o_norm)


PAIR = 2 * CHUNK
GDN_HP = 8


def _bdot(a, b, dims=NN):
    return _dot(a.astype(BF16), b.astype(BF16), dims)


def _each(f, *lists):
    return [f(*args) for args in zip(*lists)]


def _pair_common(q, k, v, gci, gcj, beta):
    ri = lax.broadcasted_iota(jnp.int32, (PAIR, PAIR), 0)
    ci = lax.broadcasted_iota(jnp.int32, (PAIR, PAIR), 1)
    same = (ri // CHUNK) == (ci // CHUNK)
    incl = same & (ri >= ci)
    strict = same & (ri > ci)
    eye = (ri == ci).astype(F32)
    first = lax.broadcasted_iota(jnp.int32, (PAIR, LANES), 0) < CHUNK
    gamma = _each(lambda gi, gj: jnp.where(incl, jnp.exp(jnp.minimum(gi - gj, 0.0)), 0.0), gci, gcj)
    kb = _each(jnp.multiply, k, beta)
    kk = _each(lambda a, b: _bdot(a, b, NT), kb, k)
    qk = _each(lambda a, b: _bdot(a, b, NT), q, k)
    m = _each(lambda x, g: jnp.where(strict, x * g, 0.0), kk, gamma)
    tm_ = _each(lambda x: eye - x, m)
    pw = _each(lambda x: _bdot(x, x), m)
    for it in range(5):
        tm_ = _each(lambda x, p: x + _bdot(x, p), tm_, pw)
        if it < 4:
            pw = _each(lambda p: _bdot(p, p), pw)
    eg = _each(jnp.exp, gci)
    vb = _each(jnp.multiply, v, beta)
    kbe = _each(jnp.multiply, kb, eg)
    uw = _each(lambda x, a, b: _bdot(x, jnp.concatenate([a, b], axis=1)), tm_, vb, kbe)
    attn = _each(lambda x, g: jnp.where(incl, x * g, 0.0), qk, gamma)
    gl_a = _each(lambda g: g[CHUNK - 1:CHUNK, :], gci)
    gl_b = _each(lambda g: g[PAIR - 1:PAIR, :], gci)
    ek = _each(lambda a, b, g: jnp.exp(jnp.where(first, a, b) - g), gl_a, gl_b, gci)
    return dict(incl=incl, strict=strict, gamma=gamma, kb=kb, m=m, tm=tm_, eg=eg, vb=vb, kbe=kbe,
                u=_each(lambda x: x[:, :LANES], uw), w=_each(lambda x: x[:, LANES:], uw), attn=attn,
                qd=_each(jnp.multiply, q, eg), ek=ek, kd=_each(jnp.multiply, k, ek),
                glast_a=_each(jnp.exp, gl_a), glast_b=_each(jnp.exp, gl_b))


def _gdn_specs(t, ts, order):
    nc = ts // CHUNK
    blk = pl.BlockSpec((ts, GDN_HP * LANES), lambda h, s: (order(s), h))
    row = pl.BlockSpec((GDN_HP, 1, ts), lambda h, s: (h, 0, order(s)))
    st = pl.BlockSpec((GDN_HP, nc, LANES, LANES), lambda h, s: (h, order(s), 0, 0))
    return blk, row, st


def _gdn_fwd(q, k, v, gcb, gct, bb, *, name):
    t = q.shape[0]
    ts = _tile(t, GDN_TILE)
    npair = ts // PAIR

    def body(q_ref, k_ref, v_ref, g_ref, gt_ref, b_ref, o_ref, st_ref, s_sc):
        @pl.when(pl.program_id(1) == 0)
        def _():
            s_sc[...] = jnp.zeros_like(s_sc)

        def pair(pi, _):
            rows = pl.ds(pl.multiple_of(pi * PAIR, PAIR), PAIR)
            heads = [slice(hh * LANES, (hh + 1) * LANES) for hh in range(GDN_HP)]
            c = CHUNK
            cat0 = lambda *xs: jnp.concatenate(xs, axis=0)
            s0 = [s_sc[hh] for hh in range(GDN_HP)]
            cm = _pair_common([q_ref[rows, sl] for sl in heads], [k_ref[rows, sl] for sl in heads],
                              [v_ref[rows, sl] for sl in heads], [g_ref[rows, sl] for sl in heads],
                              [gt_ref[hh, :, rows] for hh in range(GDN_HP)], [b_ref[rows, sl] for sl in heads])
            u, w, qd, kd = cm["u"], cm["w"], cm["qd"], cm["kd"]
            r0 = _each(lambda w_, q_, s: _bdot(cat0(w_[:c], q_[:c]), s), w, qd, s0)
            vn_a = _each(lambda u_, r: u_[:c] - r[:c], u, r0)
            s1 = _each(lambda s, gl, k_, vn: s * gl + _bdot(k_[:c], vn, TN), s0, cm["glast_a"], kd, vn_a)
            r1 = _each(lambda w_, q_, s: _bdot(cat0(w_[c:], q_[c:]), s), w, qd, s1)
            vn_b = _each(lambda u_, r: u_[c:] - r[:c], u, r1)
            s2 = _each(lambda s, gl, k_, vn: s * gl + _bdot(k_[c:], vn, TN), s1, cm["glast_b"], kd, vn_b)
            o = _each(lambda ra, rb, at, va, vb_: cat0(ra[c:], rb[c:]) + _bdot(at, cat0(va, vb_)),
                      r0, r1, cm["attn"], vn_a, vn_b)
            for hh, sl in enumerate(heads):
                st_ref[hh, 2 * pi] = s0[hh]
                st_ref[hh, 2 * pi + 1] = s1[hh]
                s_sc[hh] = s2[hh]
                o_ref[rows, sl] = o[hh]
            return 0

        lax.fori_loop(0, npair, pair, 0)

    blk, row, st = _gdn_specs(t, ts, lambda s: s)
    return pl.pallas_call(
        body, name=name, grid=(GDN_HEADS // GDN_HP, t // ts), in_specs=[blk, blk, blk, blk, row, blk],
        out_specs=[blk, st],
        out_shape=[jax.ShapeDtypeStruct((t, 1024), F32), jax.ShapeDtypeStruct((GDN_HEADS, t // CHUNK, LANES, LANES), F32)],
        scratch_shapes=[pltpu.VMEM((GDN_HP, LANES, LANES), F32)],
        compiler_params=_params(("parallel", "arbitrary")),
    )(q, k, v, gcb, gct, bb)


def _gdn_bwd(q, k, v, gcb, gct, bb, do, states, *, name):
    t = q.shape[0]
    ts = _tile(t, GDN_TILE)
    npair = ts // PAIR
    ns = t // ts
    c = CHUNK

    def body(q_ref, k_ref, v_ref, g_ref, gt_ref, b_ref, do_ref, st_ref, dq_ref, dk_ref, dv_ref, dg_ref, db_ref, ds_sc):
        @pl.when(pl.program_id(1) == 0)
        def _():
            ds_sc[...] = jnp.zeros_like(ds_sc)

        rowsum = lambda x: jnp.sum(x, axis=-1, keepdims=True)
        total = lambda x: jnp.sum(rowsum(x), axis=0, keepdims=True)
        cat0 = lambda *xs: jnp.concatenate(xs, axis=0)
        cat1 = lambda *xs: jnp.concatenate(xs, axis=1)

        def pair(step, _):
            pi = npair - 1 - step
            rows = pl.ds(pl.multiple_of(pi * PAIR, PAIR), PAIR)
            heads = [slice(hh * LANES, (hh + 1) * LANES) for hh in range(GDN_HP)]
            hs = range(GDN_HP)
            qv, kv, vv = ([r[rows, sl] for sl in heads] for r in (q_ref, k_ref, v_ref))
            beta = [b_ref[rows, sl] for sl in heads]
            dov = [do_ref[rows, sl] for sl in heads]
            s0 = [st_ref[hh, 2 * pi] for hh in hs]
            s1 = [st_ref[hh, 2 * pi + 1] for hh in hs]
            ds2 = [ds_sc[hh] for hh in hs]
            cm = _pair_common(qv, kv, vv, [g_ref[rows, sl] for sl in heads], [gt_ref[hh, :, rows] for hh in hs], beta)
            u, w, qd, kd, attn = cm["u"], cm["w"], cm["qd"], cm["kd"], cm["attn"]
            tmat, gamma, eg = cm["tm"], cm["gamma"], cm["eg"]
            incl, strict = cm["incl"], cm["strict"]
            vn_a = _each(lambda u_, w_, s: u_[:c] - _bdot(w_[:c], s), u, w, s0)
            vn_b = _each(lambda u_, w_, s: u_[c:] - _bdot(w_[c:], s), u, w, s1)
            vn = _each(cat0, vn_a, vn_b)
            dvn_att = _each(lambda a, d: _bdot(a, d, TN), attn, dov)
            dattn = _each(lambda d, v_: jnp.where(incl, _bdot(d, v_, NT), 0.0), dov, vn)
            dvn_b = _each(lambda x, k_, d: x[c:] + _bdot(k_[c:], d), dvn_att, kd, ds2)
            rb = _each(lambda d, x, s: _bdot(cat0(d[c:], x), s, NT), dov, dvn_b, s1)
            dkd_b = _each(lambda v_, d: _bdot(v_, d, NT), vn_b, ds2)
            dgl_b = _each(lambda d, s: total(d * s), ds2, s1)
            ds1 = _each(lambda d, gl, q_, w_, o_, x: d * gl + _bdot(cat0(q_[c:], w_[c:]), cat0(o_[c:], -x), TN),
                        ds2, cm["glast_b"], qd, w, dov, dvn_b)
            dvn_a = _each(lambda x, k_, d: x[:c] + _bdot(k_[:c], d), dvn_att, kd, ds1)
            ra = _each(lambda d, x, s: _bdot(cat0(d[:c], x), s, NT), dov, dvn_a, s0)
            dkd_a = _each(lambda v_, d: _bdot(v_, d, NT), vn_a, ds1)
            dgl_a = _each(lambda d, s: total(d * s), ds1, s0)
            ds0 = _each(lambda d, gl, q_, w_, o_, x: d * gl + _bdot(cat0(q_[:c], w_[:c]), cat0(o_[:c], -x), TN),
                        ds1, cm["glast_a"], qd, w, dov, dvn_a)
            dvn = _each(cat0, dvn_a, dvn_b)
            dqd = _each(lambda a, b: cat0(a[:c], b[:c]), ra, rb)
            dw = _each(lambda a, b: -cat0(a[c:], b[c:]), ra, rb)
            dkd = _each(cat0, dkd_a, dkd_b)
            dvw = _each(cat1, dvn, dw)
            dvbk = _each(lambda t_, x: _bdot(t_, x, TN), tmat, dvw)
            dvb = _each(lambda x: x[:, :LANES], dvbk)
            dkbe = _each(lambda x: x[:, LANES:], dvbk)
            dt_ = _each(lambda x, a, b: _bdot(x, cat1(a, b), NT), dvw, cm["vb"], cm["kbe"])
            da1 = _each(lambda t_, x: _bdot(t_, x, TN), tmat, dt_)
            dm = _each(lambda x, t_: jnp.where(strict, -_bdot(x, t_, NT), 0.0), da1, tmat)
            dkk = _each(jnp.multiply, dm, gamma)
            dqk = _each(jnp.multiply, dattn, gamma)
            z = _each(lambda a, b, c_, d: a * b + c_ * d, dm, cm["m"], dattn, attn)
            dkb = _each(lambda x, k_, y, e: _bdot(x, k_) + y * e, dkk, kv, dkbe, eg)
            dk = _each(lambda a, b, kb_, q_, x, e, y, be: _bdot(cat0(a, b), cat0(kb_, q_), TN) + x * e + y * be,
                       dkk, dqk, cm["kb"], qv, dkd, cm["ek"], dkb, beta)
            dq = _each(lambda x, k_, y, e: _bdot(x, k_) + y * e, dqk, kv, dqd, eg)

            def colsum_of(z_):
                zh = z_.astype(BF16)
                zl = (z_ - zh.astype(F32)).astype(BF16)
                return _dot(cat0(zh, zl), jnp.ones((2 * PAIR, LANES), BF16), TN)

            colsum = _each(colsum_of, z)
            ri = lax.broadcasted_iota(jnp.int32, (PAIR, LANES), 0)
            for hh, sl in enumerate(heads):
                dkd_kd = dkd[hh] * kd[hh]
                dgc = (rowsum(z[hh]) - colsum[hh] + rowsum(dqd[hh] * qd[hh]) - rowsum(dkd_kd)
                       + rowsum(dkbe[hh] * cm["kbe"][hh]))
                last_a = total(dkd_kd[:c]) + dgl_a[hh] * cm["glast_a"][hh]
                last_b = total(dkd_kd[c:]) + dgl_b[hh] * cm["glast_b"][hh]
                dgc = dgc + jnp.where(ri == c - 1, last_a, 0.0) + jnp.where(ri == PAIR - 1, last_b, 0.0)
                ds_sc[hh] = ds0[hh]
                dq_ref[rows, sl] = dq[hh]
                dk_ref[rows, sl] = dk[hh]
                dv_ref[rows, sl] = dvb[hh] * beta[hh]
                db_ref[rows, sl] = jnp.broadcast_to(rowsum(dkb[hh] * kv[hh]) + rowsum(dvb[hh] * vv[hh]), (PAIR, LANES))
                dg_ref[rows, sl] = dgc
            return 0

        lax.fori_loop(0, npair, pair, 0)

    blk, row, st = _gdn_specs(t, ts, lambda s: ns - 1 - s)
    out = jax.ShapeDtypeStruct((t, 1024), F32)
    return pl.pallas_call(
        body, name=name, grid=(GDN_HEADS // GDN_HP, ns), in_specs=[blk, blk, blk, blk, row, blk, blk, st],
        out_specs=[blk] * 5, out_shape=[out] * 5, scratch_shapes=[pltpu.VMEM((GDN_HP, LANES, LANES), F32)],
        compiler_params=_params(("parallel", "arbitrary")),
    )(q, k, v, gcb, gct, bb, do, states)


def _gate_out_proj_loss(o, proj_c, o_norm, w, hres, g, target, *, name):
    t, d = hres.shape
    tm = _tile(t, 2 * ROW_TILE)

    def body(o_ref, z_ref, on_ref, w_ref, h_ref, g_ref, t_ref, dh_ref, dhb_ref, y_ref, dg_ref, loss_ref):
        i = pl.program_id(0)
        for hd in range(GDN_HEADS):
            sl = slice(hd * LANES, (hd + 1) * LANES)
            ov = o_ref[:, sl]
            rr = lax.rsqrt(jnp.mean(ov * ov, axis=-1, keepdims=True) + RMS_EPS)
            z = z_ref[:, sl]
            y_ref[:, sl] = (ov * rr * on_ref[...] * (z * _sigmoid(z))).astype(BF16)
        x = h_ref[...] + _dot(y_ref[...], w_ref[...])
        r = lax.rsqrt(jnp.mean(x * x, axis=-1, keepdims=True) + RMS_EPS)
        xh = x * r
        err = xh * g_ref[...] - t_ref[...]
        dy = err * (1.0 / d)
        dxh = dy * g_ref[...]
        dh = r * (dxh - xh * jnp.mean(dxh * xh, axis=-1, keepdims=True))
        dh_ref[...] = dh
        dhb_ref[...] = dh.astype(BF16)

        @pl.when(i == 0)
        def _():
            dg_ref[...] = jnp.zeros_like(dg_ref)
            loss_ref[...] = jnp.zeros_like(loss_ref)

        dg_ref[...] += jnp.sum(dy * xh, axis=0, keepdims=True)
        part = 0.5 * jnp.sum(jnp.mean(err * err, axis=-1, keepdims=True), axis=0, keepdims=True)
        loss_ref[...] += jnp.broadcast_to(part, loss_ref.shape)

    row = pl.BlockSpec((tm, d), lambda i: (i, 0))
    vec = pl.BlockSpec((1, d), lambda i: (0, 0))
    return pl.pallas_call(
        body, name=name, grid=(t // tm,),
        in_specs=[row, pl.BlockSpec((tm, 1024), lambda i: (i, 3)), pl.BlockSpec((1, LANES), lambda i: (0, 0)),
                  pl.BlockSpec(w.shape, lambda i: (0, 0)), row, vec, row],
        out_specs=[row, row, row, vec, pl.BlockSpec((8, LANES), lambda i: (0, 0))],
        out_shape=[jax.ShapeDtypeStruct((t, d), F32), jax.ShapeDtypeStruct((t, d), BF16), jax.ShapeDtypeStruct((t, d), BF16),
                   jax.ShapeDtypeStruct((1, d), F32), jax.ShapeDtypeStruct((8, LANES), F32)],
        compiler_params=_params(("arbitrary",)),
    )(o, proj_c, o_norm, w, hres, g, target)


def _pad_cols(w, n):
    return jnp.pad(w, ((0, 0), (0, n - w.shape[1])))


def _layout_odd(w):
    return dict(
        winc=_pad_cols(w["w_in_c"], IN_C_PAD).astype(BF16), wout_c=w["w_out_c"].astype(BF16), conv_w=w["conv_w"],
        a_log=_pad_cols(w["a_log"], LANES), dt_bias=_pad_cols(w["dt_bias"], LANES),
        norm_c=w["norm_c"], o_norm=w["o_norm"], final_norm=w["final_norm"],
    )


def _layout_even(w):
    z = lambda r, c: jnp.zeros((r, c), w["w_in_ab"].dtype)
    wi = w["w_in_ab"]
    win = jnp.concatenate([wi[:, :384], z(1024, 64), wi[:, 384:416], z(1024, 32), wi[:, 416:]], axis=1)
    wq = jnp.pad(w["w_q_b"].reshape(MLA_Q_RANK, MLA_HEADS, 96), ((0, 0), (0, 0), (0, 32))).reshape(MLA_Q_RANK, 1024)
    kv3 = w["w_kv_b"].reshape(MLA_KV_RANK, MLA_HEADS, 128)
    wk = jnp.pad(kv3[..., :MLA_NOPE], ((0, 0), (0, 0), (0, 64))).reshape(MLA_KV_RANK, 1024)
    wv = kv3[..., MLA_NOPE:].reshape(MLA_KV_RANK, 512)
    pw = w["pool_w"]
    rows = []
    for g in range(4):
        rows.append(jnp.concatenate([pw[g] if j == g else z(128, 128) for j in range(4)], axis=1))
    wpool = jnp.concatenate(rows, axis=0)
    half = MLA_ROPE // 2
    inv = 1.0 / (ROPE_THETA ** (jnp.arange(half, dtype=F32) / half))
    inv_lane = jnp.concatenate([jnp.zeros((MLA_NOPE,), F32), inv, inv, jnp.zeros((32,), F32)]).reshape(1, LANES)
    return dict(
        win=win.astype(BF16), wq=wq.astype(BF16), wk=wk.astype(BF16), wv=wv.astype(BF16), wpool=wpool.astype(BF16),
        wout_ab=w["w_out_ab"].astype(BF16), inv_lane=inv_lane,
        norm_ab=w["norm_ab"], q_a_norm=w["q_a_norm"], kv_a_norm=w["kv_a_norm"], pool_scale=w["pool_scale"],
    )


def _unlayout_grads(g, names):
    out = {}
    for name in names:
        if name == "w_in_ab":
            dwin = g["win"]
            out[name] = jnp.concatenate([dwin[:, :384], dwin[:, 448:480], dwin[:, 512:]], axis=1)
        elif name == "w_q_b":
            out[name] = g["wq"].reshape(MLA_Q_RANK, MLA_HEADS, 128)[..., :96].reshape(MLA_Q_RANK, 768)
        elif name == "w_kv_b":
            out[name] = jnp.concatenate([g["wk"].reshape(MLA_KV_RANK, MLA_HEADS, 128)[..., :MLA_NOPE],
                                         g["wv"].reshape(MLA_KV_RANK, MLA_HEADS, MLA_V)], axis=-1).reshape(MLA_KV_RANK, 1024)
        elif name == "w_in_c":
            out[name] = g["winc"][:, :4112]
        else:
            out[name] = g[{"w_out_ab": "wout_ab", "w_out_c": "wout_c"}[name]]
    return out


def _local_step(x, pos, target, lw, odd_weights=None, on_grads=None):
    mm = _matmul
    proj, hn = _rms_in_proj(x, lw["norm_ab"], lw["win"], name="rms_in_ab")
    q, k, v, ybraw, qn, kvn, d, cos_t, sin_t = _ab_prep(
        proj, pos, lw["inv_lane"], lw["q_a_norm"], lw["kv_a_norm"], lw["wq"], lw["wk"], lw["wv"], lw["wpool"], name="ab_prep")
    o, lse = _attn_fwd(q, k, v, name="attn_fwd")
    h1, y = _gate_out_proj(o, ybraw, proj, lw["pool_scale"], lw["wout_ab"], x, name="gate_out_ab")
    lo = lw if odd_weights is None else odd_weights(h1)
    proj_c, hn1 = _rms_in_proj(h1, lo["norm_c"], lo["winc"], name="rms_in_c")
    q2, k2, v2, gb, bb, gt = _c_prep(proj_c, lo["conv_w"], lo["a_log"], lo["dt_bias"], name="c_prep")
    gt = gt.reshape(GDN_HEADS, 1, gt.shape[1])
    o2, states = _gdn_fwd(q2, k2, v2, gb, gt, bb, name="gdn_fwd")
    dh2, dh2b, y2, d_final, loss = _gate_out_proj_loss(
        o2, proj_c, lo["o_norm"], lo["wout_c"], h1, lo["final_norm"], target, name="gate_out_c_loss")
    g = {"final_norm": d_final}
    dy2 = mm(dh2b, lo["wout_c"], "nt", name="out_c_dx")
    g["wout_c"] = mm(y2, dh2b, "tn", name="out_c_dw")
    do2, dz2, g["o_norm"] = _o_gate_bwd(dy2, o2, proj_c, lo["o_norm"], name="gate_c_bwd")
    dq2, dk2, dv2, dgb, dbb = _gdn_bwd(q2, k2, v2, gb, gt, bb, do2, states, name="gdn_bwd")
    dproj_c, g["conv_w"], g["a_log"], g["dt_bias"] = _c_prep_bwd(
        proj_c, lo["conv_w"], lo["a_log"], lo["dt_bias"], dq2, dk2, dv2, dgb, dbb, dz2, name="c_prep_bwd")
    g["winc"] = mm(hn1, dproj_c, "tn", name="in_c_dw")
    notify = (lambda tag: 0.0) if on_grads is None else (lambda tag: on_grads(tag, g))
    pool_scale = lw["pool_scale"] + notify("odd")
    dh1, dh1b, g["norm_c"] = _matmul_rms_bwd(dproj_c, lo["winc"], h1, lo["norm_c"], dh2, name="in_c_dx_rms", with_bf16=True)
    dy = mm(dh1b, lw["wout_ab"], "nt", name="out_ab_dx")
    g["wout_ab"] = mm(y, dh1b, "tn", name="out_ab_dw")
    pool_scale = pool_scale + notify("out_ab")
    do, delta, dyb, dz, g["pool_scale"] = _gate_bwd(dy, o, ybraw, proj, pool_scale, name="gate_ab_bwd")
    dq, dk, dv = _attn_bwd(q, k, v, do, lse, delta, name="attn_bwd")
    dproj, dqraw, dkb, g["q_a_norm"], g["kv_a_norm"] = _ab_prep_bwd(
        proj, lw["q_a_norm"], lw["kv_a_norm"], dq, dk, dv, cos_t, sin_t, dyb, dz,
        lw["wq"], lw["wk"], lw["wv"], lw["wpool"], name="ab_prep_bwd")
    g["wpool"] = mm(d, dyb, "tn", name="pool_mix_dw")
    g["wq"] = mm(qn, dqraw, "tn", name="q_up_dw")
    g["wk"] = mm(kvn, dkb, "tn", name="k_up_dw")
    g["wv"] = mm(kvn, dv, "tn", name="v_up_dw")
    g["win"] = mm(hn, dproj, "tn", name="in_ab_dw")
    norm_ab = lw["norm_ab"] + notify("in_ab")
    dx, g["norm_ab"] = _matmul_rms_bwd(dproj, lw["win"], x, norm_ab, dh1, name="in_ab_dx_rms", with_bf16=False)
    return loss, dx, g


_HBM = pl.BlockSpec(memory_space=pltpu.HBM)


def _place():
    return lax.axis_index("x"), lax.axis_index("y"), lax.axis_index("c")


def _flip(v, f):
    return 1 - v if f else v


_CHIP_FLIPS = ((1, 0), (0, 1), (1, 1))
_DEV_FLIPS = tuple((fx, fy, fc) for fx in (0, 1) for fy in (0, 1) for fc in (0, 1) if fx or fy or fc)


def _rcopy(src, dst, send_sems, recv_sems, k, to):
    return pltpu.make_async_remote_copy(src_ref=src, dst_ref=dst, send_sem=send_sems.at[k], recv_sem=recv_sems.at[k],
                                        device_id=to, device_id_type=MESH)


def _my_half(ref, c, axis):
    rh = ref.shape[axis] // 2
    idx = [slice(None)] * len(ref.shape)
    idx[axis] = pl.ds(c * rh, rh)
    return ref.at[tuple(idx)]


def _gather_weights(bigs, smalls):
    nb, ns = len(bigs), len(smalls)

    def body(*refs):
        ins, outs = refs[:nb + ns], refs[nb + ns:2 * (nb + ns)]
        send_sems, recv_sems, local_sems = refs[2 * (nb + ns):]
        x, y, c = _place()
        j0 = 2 * x + y
        sib = (x, y, 1 - c)
        chips = [(_flip(x, fx), _flip(y, fy)) for fx, fy in _CHIP_FLIPS]
        local = [pltpu.make_async_copy(i_ref, o_ref.at[j0], local_sems.at[a])
                 for a, (i_ref, o_ref) in enumerate(zip(ins, outs))]
        for cp in local:
            cp.start()
        sends = []
        for k, (px, py) in enumerate(chips):
            for a in range(nb):
                sends.append(_rcopy(_my_half(ins[a], c, 0), _my_half(outs[a].at[j0], c, 0), send_sems, recv_sems,
                                    6 * a + k, (px, py, c)))
            for s in range(ns):
                sends.append(_rcopy(ins[nb + s], outs[nb + s].at[j0], send_sems, recv_sems, 6 * nb + 3 * s + k, (px, py, c)))
        for cp in sends:
            cp.start()
        for k, (px, py) in enumerate(chips):
            jk = 2 * px + py
            for a in range(nb):
                landed = _my_half(outs[a].at[jk], c, 0)
                _rcopy(landed, landed, send_sems, recv_sems, 6 * a + k, (px, py, c)).wait_recv()
                fwd = _rcopy(landed, landed, send_sems, recv_sems, 6 * a + 3 + k, sib)
                fwd.start()
                sends.append(fwd)
        for k, (px, py) in enumerate(chips):
            jk = 2 * px + py
            for a in range(nb):
                other = _my_half(outs[a].at[jk], 1 - c, 0)
                _rcopy(other, other, send_sems, recv_sems, 6 * a + 3 + k, sib).wait_recv()
            for s in range(ns):
                _rcopy(ins[nb + s], outs[nb + s].at[jk], send_sems, recv_sems, 6 * nb + 3 * s + k, (px, py, c)).wait_recv()
        for cp in sends:
            cp.wait_send()
        for cp in local:
            cp.wait()

    arrays = list(bigs) + list(smalls)
    n_sem = 6 * nb + 3 * ns
    return pl.pallas_call(
        body, name="gather_weights", in_specs=[_HBM] * len(arrays), out_specs=[_HBM] * len(arrays),
        out_shape=[jax.ShapeDtypeStruct((4,) + a.shape, a.dtype) for a in arrays],
        scratch_shapes=[pltpu.SemaphoreType.DMA((n_sem,)), pltpu.SemaphoreType.DMA((n_sem,)),
                        pltpu.SemaphoreType.DMA((len(arrays),))],
    )(*arrays)


def _core_swap_partial(gs, *, name):
    n = len(gs)

    def body(*refs):
        ins, outs = refs[:n], refs[n:2 * n]
        send_sems, recv_sems = refs[2 * n:]
        x, y, c = _place()
        copies = [_rcopy(_my_half(i_ref, 1 - c, 1), o_ref, send_sems, recv_sems, a, (x, y, 1 - c))
                  for a, (i_ref, o_ref) in enumerate(zip(ins, outs))]
        for cp in copies:
            cp.start()
        for cp in copies:
            cp.wait()

    return pl.pallas_call(
        body, name=name, in_specs=[_HBM] * n, out_specs=[_HBM] * n,
        out_shape=[jax.ShapeDtypeStruct((4, g.shape[1] // 2, g.shape[2]), g.dtype) for g in gs],
        scratch_shapes=[pltpu.SemaphoreType.DMA((n,)), pltpu.SemaphoreType.DMA((n,))],
    )(*gs)


def _core_swap_sum(fs):
    n = len(fs)

    def body(*refs):
        ins, outs = refs[:n], refs[n:2 * n]
        send_sems, recv_sems = refs[2 * n:]
        x, y, c = _place()
        copies = [_rcopy(_my_half(i_ref, c, 0), _my_half(o_ref, c, 0), send_sems, recv_sems, a, (x, y, 1 - c))
                  for a, (i_ref, o_ref) in enumerate(zip(ins, outs))]
        for cp in copies:
            cp.start()
        for a, cp in enumerate(copies):
            cp.wait_send()
            theirs = _my_half(outs[a], 1 - c, 0)
            _rcopy(theirs, theirs, send_sems, recv_sems, a, (x, y, 1 - c)).wait_recv()

    return pl.pallas_call(
        body, name="core_swap_sum", in_specs=[_HBM] * n, out_specs=[_HBM] * n,
        out_shape=[jax.ShapeDtypeStruct(f.shape, f.dtype) for f in fs],
        input_output_aliases={a: a for a in range(n)},
        scratch_shapes=[pltpu.SemaphoreType.DMA((n,)), pltpu.SemaphoreType.DMA((n,))],
    )(*fs)


_SEM = pl.BlockSpec(memory_space=pltpu.SEMAPHORE)
_ANY = pl.BlockSpec(memory_space=pl.ANY)
_DATAFLOW = pltpu.SideEffectType.DATAFLOW_SIDE_EFFECTING


def _to_chips_copies(srcs, lands, send_sems, recv_sems, per_chip_slot):
    x, y, c = _place()
    j0 = 2 * x + y
    out = []
    for k, (fx, fy) in enumerate(_CHIP_FLIPS):
        px, py = _flip(x, fx), _flip(y, fy)
        jk = 2 * px + py
        for a, (src, land) in enumerate(zip(srcs, lands)):
            piece = src.at[jk] if per_chip_slot else src
            out.append((_rcopy(piece, land.at[j0], send_sems, recv_sems, 3 * a + k, (px, py, c)),
                        _rcopy(piece, land.at[jk], send_sems, recv_sems, 3 * a + k, (px, py, c))))
    return out


def _to_chips_start(arrays, *, per_chip_slot, name):
    n = len(arrays)
    lands = [lax.empty((4,) + (a.shape[1:] if per_chip_slot else a.shape), a.dtype) for a in arrays]

    def body(*refs):
        srcs, land_refs, send_sems, recv_sems, token = refs[:n], refs[n:2 * n], refs[2 * n], refs[2 * n + 1], refs[-1]
        for send, _ in _to_chips_copies(srcs, land_refs, send_sems, recv_sems, per_chip_slot):
            send.start()
        token[...] = jnp.zeros_like(token)

    held = [pltpu.with_memory_space_constraint(a, pltpu.HBM) for a in list(arrays) + lands]
    return pl.pallas_call(
        body, name=name, in_specs=[_HBM] * (2 * n),
        out_specs=(_SEM, _SEM, *[_HBM] * (2 * n), pl.BlockSpec(memory_space=pltpu.VMEM)),
        out_shape=(pltpu.SemaphoreType.DMA((3 * n,)), pltpu.SemaphoreType.DMA((3 * n,)),
                   *[pltpu.HBM(a.shape, a.dtype) for a in held], jax.ShapeDtypeStruct((8, LANES), F32)),
        input_output_aliases={i: 2 + i for i in range(2 * n)},
        compiler_params=pltpu.CompilerParams(has_side_effects=_DATAFLOW),
    )(*held)


def _to_chips_wait(started, after, *, per_chip_slot, name):
    send_sems, recv_sems, held = started[0], started[1], started[2:-1]
    n = len(held) // 2

    def body(*refs):
        srcs, land_refs, s_sems, r_sems = refs[:n], refs[n:2 * n], refs[2 * n], refs[2 * n + 1]
        for send, arrival in _to_chips_copies(srcs, land_refs, s_sems, r_sems, per_chip_slot):
            send.wait_send()
            arrival.wait_recv()

    out = pl.pallas_call(
        body, name=name, in_specs=[_HBM] * (2 * n) + [_SEM, _SEM, _ANY], out_specs=[_HBM] * (2 * n),
        out_shape=[pltpu.HBM(a.shape, a.dtype) for a in held],
        input_output_aliases={i: i for i in range(2 * n)},
        compiler_params=pltpu.CompilerParams(has_side_effects=_DATAFLOW),
    )(*held, send_sems, recv_sems, after)
    return out[n:]


def _chip_exchange(ps, small):
    n = len(ps)
    rs = small.shape[0]

    def body(*refs):
        p_refs, s_ref = refs[:n], refs[n]
        l_refs, ls_ref = refs[n + 1:2 * n + 1], refs[2 * n + 1]
        send_sems, recv_sems, local_sems = refs[2 * n + 2:]
        x, y, c = _place()
        j0 = 2 * x + y
        d0 = 2 * j0 + c
        local = [pltpu.make_async_copy(p.at[j0], l.at[j0], local_sems.at[a]) for a, (p, l) in enumerate(zip(p_refs, l_refs))]
        local.append(pltpu.make_async_copy(s_ref, ls_ref.at[d0], local_sems.at[n]))
        for cp in local:
            cp.start()
        sends = []
        for k, (fx, fy) in enumerate(_CHIP_FLIPS):
            px, py = _flip(x, fx), _flip(y, fy)
            for a in range(n):
                sends.append(_rcopy(p_refs[a].at[2 * px + py], l_refs[a].at[j0], send_sems, recv_sems, 3 * a + k, (px, py, c)))
        for k, (fx, fy, fc) in enumerate(_DEV_FLIPS):
            peer = (_flip(x, fx), _flip(y, fy), _flip(c, fc))
            sends.append(_rcopy(s_ref, ls_ref.at[d0], send_sems, recv_sems, 3 * n + k, peer))
        for cp in sends:
            cp.start()
        for k, (fx, fy) in enumerate(_CHIP_FLIPS):
            px, py = _flip(x, fx), _flip(y, fy)
            for a in range(n):
                _rcopy(p_refs[a].at[j0], l_refs[a].at[2 * px + py], send_sems, recv_sems, 3 * a + k, (px, py, c)).wait_recv()
        for k, (fx, fy, fc) in enumerate(_DEV_FLIPS):
            px, py, pc = _flip(x, fx), _flip(y, fy), _flip(c, fc)
            _rcopy(s_ref, ls_ref.at[4 * px + 2 * py + pc], send_sems, recv_sems, 3 * n + k, (px, py, pc)).wait_recv()
        for cp in sends:
            cp.wait_send()
        for cp in local:
            cp.wait()

    n_sem = 3 * n + 7
    return pl.pallas_call(
        body, name="chip_exchange", in_specs=[_HBM] * (n + 1), out_specs=[_HBM] * (n + 1),
        out_shape=[jax.ShapeDtypeStruct(p.shape, F32) for p in ps] + [jax.ShapeDtypeStruct((8, rs, LANES), F32)],
        scratch_shapes=[pltpu.SemaphoreType.DMA((n_sem,)), pltpu.SemaphoreType.DMA((n_sem,)),
                        pltpu.SemaphoreType.DMA((n + 1,))],
    )(*ps, small)


def _core_sum(g, part, core, *, name):
    _, rh, cols = part.shape
    tr = _tile(rh, 256)
    nb = rh // tr

    def body(c_ref, g_ref, p_ref, o_ref):
        o_ref[...] = g_ref[...] + p_ref[...]

    grid_spec = pltpu.PrefetchScalarGridSpec(
        num_scalar_prefetch=1, grid=(4, nb),
        in_specs=[pl.BlockSpec((1, tr, cols), lambda j, i, c: (j, c[0] * nb + i, 0)),
                  pl.BlockSpec((1, tr, cols), lambda j, i, c: (j, i, 0))],
        out_specs=pl.BlockSpec((1, tr, cols), lambda j, i, c: (j, i, 0)),
    )
    return pl.pallas_call(
        body, name=name, grid_spec=grid_spec, out_shape=jax.ShapeDtypeStruct(part.shape, F32),
        compiler_params=_params(("parallel", "parallel")),
    )(core, g, part)


def _chip_sum(landed, core, *, name):
    _, rh, cols = landed.shape
    tr = _tile(rh, 256)
    nb = rh // tr

    def body(c_ref, l_ref, o_ref):
        o_ref[...] = ((l_ref[0] + l_ref[1]) + l_ref[2]) + l_ref[3]

    grid_spec = pltpu.PrefetchScalarGridSpec(
        num_scalar_prefetch=1, grid=(nb,),
        in_specs=[pl.BlockSpec((4, tr, cols), lambda i, c: (0, i, 0))],
        out_specs=pl.BlockSpec((tr, cols), lambda i, c: (c[0] * nb + i, 0)),
    )
    return pl.pallas_call(
        body, name=name, grid_spec=grid_spec, out_shape=jax.ShapeDtypeStruct((2 * rh, cols), F32),
        compiler_params=_params(("parallel",)),
    )(core, landed)


_ROW_POOL_W, _ROW_NORM_AB, _ROW_FINAL, _ROW_POOL_SCALE, _ROW_Q_NORM = 0, 512, 520, 528, 532
_ROW_KV_NORM, _ROW_O_NORM, _ROW_A_LOG, _ROW_DT_BIAS, _ROW_LOSS = 534, 535, 536, 537, 538
_ROW_CONV, _ROW_NORM_C, _SMALL_ROWS = 544, 640, 672
_CONV_ROWS = CONV_WIDTH * 6


def _put_rows(dst_ref, row0, src, width):
    for r in range(width // LANES):
        dst_ref[row0 + r:row0 + r + 1, :] = src[:, r * LANES:(r + 1) * LANES]


def _pack_small(g, loss_tile):
    names = ("wpool", "norm_ab", "final_norm", "pool_scale", "q_a_norm", "kv_a_norm", "o_norm", "a_log", "dt_bias",
             "conv_w", "norm_c")

    def body(wpool, norm_ab, final_norm, pool_scale, q_norm, kv_norm, o_norm, a_log, dt_bias, conv_w, norm_c, loss, o_ref):
        o_ref[...] = jnp.zeros_like(o_ref)
        for gi in range(4):
            o_ref[_ROW_POOL_W + gi * 128:_ROW_POOL_W + (gi + 1) * 128, :] = wpool[gi * 128:(gi + 1) * 128, gi * 128:(gi + 1) * 128]
        _put_rows(o_ref, _ROW_NORM_AB, norm_ab[...], 1024)
        _put_rows(o_ref, _ROW_FINAL, final_norm[...], 1024)
        _put_rows(o_ref, _ROW_POOL_SCALE, pool_scale[...], 512)
        _put_rows(o_ref, _ROW_Q_NORM, q_norm[...], 256)
        for row, ref in ((_ROW_KV_NORM, kv_norm), (_ROW_O_NORM, o_norm), (_ROW_A_LOG, a_log), (_ROW_DT_BIAS, dt_bias)):
            o_ref[row:row + 1, :] = ref[...]
        o_ref[_ROW_LOSS:_ROW_LOSS + 1, :] = loss[0:1, :]
        for j in range(4):
            for r in range(CONV_WIDTH):
                _put_rows(o_ref, _ROW_CONV + j * _CONV_ROWS + r * 6, conv_w[r:r + 1, j * 768:(j + 1) * 768], 768)
            _put_rows(o_ref, _ROW_NORM_C + j * 8, norm_c[:, j * 256:(j + 1) * 256], 256)

    vmem = pl.BlockSpec(memory_space=pltpu.VMEM)
    return pl.pallas_call(
        body, name="pack_small", in_specs=[vmem] * 12, out_specs=vmem,
        out_shape=jax.ShapeDtypeStruct((_SMALL_ROWS, LANES), F32),
    )(*[g[n] for n in names], loss_tile)


_SMALL_NAMES = ("pool_w", "norm_ab", "final_norm", "pool_scale", "q_a_norm", "kv_a_norm", "o_norm", "a_log", "dt_bias",
                "conv_w", "norm_c")


def _take_rows(src, row0, width):
    return jnp.concatenate([src[row0 + r:row0 + r + 1, :] for r in range(width // LANES)], axis=1)


def _small_update(small_all, ws, ms, vs):
    n = len(_SMALL_NAMES)

    def body(*refs):
        a_ref = refs[0]
        w_refs, m_refs, v_refs = refs[1:1 + n], refs[1 + n:1 + 2 * n], refs[1 + 2 * n:1 + 3 * n]
        outs = refs[1 + 3 * n:1 + 7 * n]
        loss_ref, tot = refs[1 + 7 * n], refs[2 + 7 * n]
        acc = a_ref[0]
        for d in range(1, 8):
            acc = acc + a_ref[d]
        tot[...] = acc
        x, y, _ = _place()
        j0 = 2 * x + y
        conv = tot[pl.ds(pl.multiple_of(_ROW_CONV + j0 * _CONV_ROWS, 8), _CONV_ROWS), :]
        norm_c = tot[pl.ds(pl.multiple_of(_ROW_NORM_C + j0 * 8, 8), 8), :]
        whole = tot[_ROW_NORM_AB:_ROW_CONV, :]
        at = lambda row: row - _ROW_NORM_AB
        grads = {
            "norm_ab": _take_rows(whole, at(_ROW_NORM_AB), 1024), "final_norm": _take_rows(whole, at(_ROW_FINAL), 1024),
            "pool_scale": _take_rows(whole, at(_ROW_POOL_SCALE), 512), "q_a_norm": _take_rows(whole, at(_ROW_Q_NORM), 256),
            "kv_a_norm": whole[at(_ROW_KV_NORM):at(_ROW_KV_NORM) + 1, :], "o_norm": whole[at(_ROW_O_NORM):at(_ROW_O_NORM) + 1, :],
            "a_log": tot[_ROW_A_LOG:_ROW_A_LOG + 1, 0:GDN_HEADS],
            "dt_bias": tot[_ROW_DT_BIAS:_ROW_DT_BIAS + 1, 0:GDN_HEADS],
            "norm_c": _take_rows(norm_c, 0, 256),
        }
        loss_ref[...] = whole[at(_ROW_LOSS):at(_ROW_LOSS) + 1, :]
        for i, name in enumerate(_SMALL_NAMES):
            g_out = outs[4 * i]
            if name == "pool_w":
                for gi in range(4):
                    g_out[gi] = tot[_ROW_POOL_W + gi * 128:_ROW_POOL_W + (gi + 1) * 128, :]
            elif name == "conv_w":
                for r in range(CONV_WIDTH):
                    g_out[r:r + 1, :] = _take_rows(conv, r * 6, 768)
            else:
                g_out[...] = grads[name]
            _adam_update(g_out, w_refs[i], m_refs[i], v_refs[i], *outs[4 * i + 1:4 * i + 4])

    vmem = pl.BlockSpec(memory_space=pltpu.VMEM)
    out_shape = [jax.ShapeDtypeStruct(w.shape, F32) for w in ws for _ in range(4)] + [jax.ShapeDtypeStruct((1, LANES), F32)]
    return pl.pallas_call(
        body, name="small_update", in_specs=[vmem] * (1 + 3 * n), out_specs=[vmem] * (4 * n + 1), out_shape=out_shape,
        scratch_shapes=[pltpu.VMEM((_SMALL_ROWS, LANES), F32)],
        compiler_params=pltpu.CompilerParams(vmem_limit_bytes=VMEM_LIMIT),
    )(small_all, *ws, *ms, *vs)


def _adam_update(g_ref, w_ref, m_ref, v_ref, d_ref, mo_ref, vo_ref):
    gv = g_ref[...]
    mn = ADAM_B1 * m_ref[...] + (1.0 - ADAM_B1) * gv
    vn = ADAM_B2 * v_ref[...] + (1.0 - ADAM_B2) * (gv * gv)
    mo_ref[...] = mn
    vo_ref[...] = vn
    c1 = 1.0 - ADAM_B1 ** ADAM_STEP
    c2 = 1.0 - ADAM_B2 ** ADAM_STEP
    d_ref[...] = -ADAM_LR * ((mn / c1) / (jnp.sqrt(vn / c2) + ADAM_EPS) + ADAM_WD * w_ref[...])


def _adamw_rows(g, w, m, v, *, name):
    rows, cols = g.shape
    if rows % LANES == 0:
        tr = _tile(rows, 512)
        blk, steps = pl.BlockSpec((tr, cols), lambda i: (i, 0)), rows // tr
    else:
        tc = _tile(cols, 256)
        blk, steps = pl.BlockSpec((rows, tc), lambda i: (0, i)), cols // tc

    def body(*refs):
        _adam_update(*refs)

    out = jax.ShapeDtypeStruct((rows, cols), F32)
    return pl.pallas_call(
        body, name=name, grid=(steps,), in_specs=[blk] * 4, out_specs=[blk] * 3, out_shape=[out] * 3,
        compiler_params=_params(("parallel",)),
    )(g, w, m, v)


_ADAM_ROWWISE = ("w_in_ab", "w_q_b", "w_kv_b", "w_out_ab", "w_in_c", "w_out_c")


_SHARD_AXIS = {"w_in_ab": 1, "w_q_b": 1, "w_kv_b": 1, "w_out_ab": 0, "w_in_c": 1, "w_out_c": 0, "conv_w": 1, "norm_c": 1}
_ALL_NAMES = ("norm_ab", "w_in_ab", "q_a_norm", "w_q_b", "kv_a_norm", "w_kv_b", "pool_w", "pool_scale", "w_out_ab",
              "norm_c", "w_in_c", "conv_w", "a_log", "dt_bias", "o_norm", "w_out_c", "final_norm")


def _join_shards(a, axis):
    _, r, c = a.shape
    return a.reshape(4 * r, c) if axis == 0 else jnp.transpose(a, (1, 0, 2)).reshape(r, 4 * c)


def _split_shards(a, axis):
    r, c = a.shape
    return a.reshape(4, r // 4, c) if axis == 0 else jnp.transpose(a.reshape(r, 4, c // 4), (1, 0, 2))


def kernel(x, positions, norm_ab, w_in_ab, q_a_norm, w_q_b, kv_a_norm, w_kv_b, pool_w, pool_scale, w_out_ab, norm_c, w_in_c, conv_w, a_log, dt_bias, o_norm, w_out_c, final_norm, loss_target, m_norm_ab, m_w_in_ab, m_q_a_norm, m_w_q_b, m_kv_a_norm, m_w_kv_b, m_pool_w, m_pool_scale, m_w_out_ab, m_norm_c, m_w_in_c, m_conv_w, m_a_log, m_dt_bias, m_o_norm, m_w_out_c, m_final_norm, v_norm_ab, v_w_in_ab, v_q_a_norm, v_w_q_b, v_kv_a_norm, v_w_kv_b, v_pool_w, v_pool_scale, v_w_out_ab, v_norm_c, v_w_in_c, v_conv_w, v_a_log, v_dt_bias, v_o_norm, v_w_out_c, v_final_norm):
    given = dict(locals())
    c = lax.axis_index("c")
    t = x.shape[1]

    def shard_of(prefix, name):
        a = given[prefix + name]
        return a.reshape(a.shape[1:]) if a.ndim > 2 else a.reshape(1, -1)

    big, big_even, big_odd, small_sharded = _ADAM_ROWWISE, _ADAM_ROWWISE[:4], _ADAM_ROWWISE[4:], ("conv_w", "norm_c")
    chip = 2 * lax.axis_index("x") + lax.axis_index("y")
    core = c.astype(jnp.int32).reshape(1)
    late = big_odd + small_sharded
    late_shards = [shard_of("", n).astype(BF16) for n in big_odd] + [shard_of("", n) for n in small_sharded]
    gather_odd = _to_chips_start(late_shards, per_chip_slot=False, name="gather_odd_start")
    gathered = _gather_weights([shard_of("", n).astype(BF16) for n in big_even], [])
    full = {n: _join_shards(a, _SHARD_AXIS[n]) for n, a in zip(big_even, gathered)}
    for name in ("norm_ab", "q_a_norm", "kv_a_norm", "pool_w", "pool_scale"):
        full[name] = shard_of("", name)
    lw = _layout_even(full)
    lw["norm_ab"] = lw["norm_ab"] + gather_odd[-1][0, 0]

    def odd_weights(h1):
        landed = _to_chips_wait(gather_odd, h1, per_chip_slot=False, name="gather_odd_wait")
        w = {}
        for name, land, own in zip(late, landed, late_shards):
            w[name] = _join_shards(lax.dynamic_update_index_in_dim(land, own, chip, 0), _SHARD_AXIS[name])
        for name in ("a_log", "dt_bias", "o_norm", "final_norm"):
            w[name] = shard_of("", name)
        return _layout_odd(w)

    def chip_partials(names, grads, tag):
        slots = [_split_shards(grads[n], _SHARD_AXIS[n]) for n in names]
        partial = _core_swap_partial(slots, name="core_swap_partial_" + tag)
        return [_core_sum(s, p, core, name="core_sum_" + n) for n, s, p in zip(names, slots, partial)]

    groups = {"odd": big_odd, "out_ab": ("w_out_ab",), "in_ab": ("w_in_ab", "w_q_b", "w_kv_b")}
    sent = {}

    def on_grads(tag, g):
        part = chip_partials(groups[tag], _unlayout_grads(g, groups[tag]), tag)
        sent[tag] = (part, _to_chips_start(part, per_chip_slot=True, name="exchange_" + tag + "_start"))
        return sent[tag][1][-1][0, 0]

    loss_tile, dx, g = _local_step(x[0], positions.reshape(t, 1), loss_target[0], lw, odd_weights, on_grads)
    small_all = _chip_exchange([], _pack_small(g, loss_tile))[-1]
    halves = {}
    for tag, names in groups.items():
        part, started = sent[tag]
        landed = _to_chips_wait(started, small_all, per_chip_slot=True, name="exchange_" + tag + "_wait")
        for n, l, p in zip(names, landed, part):
            l = lax.dynamic_update_index_in_dim(l, lax.dynamic_index_in_dim(p, chip, 0, keepdims=False), chip, 0)
            halves[n] = _chip_sum(l, core, name="chip_sum_" + n)
    gbig = dict(zip(big, _core_swap_sum([halves[n] for n in big])))

    res = {}
    for name in big:
        res["grad", name] = gbig[name]
        operands = [gbig[name], shard_of("", name), shard_of("m_", name), shard_of("v_", name)]
        flip = operands[0].shape[1] % LANES != 0
        if flip:
            operands = [jnp.transpose(a) for a in operands]
        out = _adamw_rows(*operands, name="adamw_" + name)
        res["delta", name], res["m", name], res["v", name] = [jnp.transpose(a) for a in out] if flip else out
    out = _small_update(small_all, [shard_of("", n) for n in _SMALL_NAMES], [shard_of("m_", n) for n in _SMALL_NAMES],
                        [shard_of("v_", n) for n in _SMALL_NAMES])
    for i, name in enumerate(_SMALL_NAMES):
        res["grad", name], res["delta", name], res["m", name], res["v", name] = out[4 * i:4 * i + 4]
    res = {k: a.reshape(given[k[1]].shape) for k, a in res.items()}
    loss = out[-1][0, 0]
    outs = [loss, dx.reshape(x.shape)]
    for key in ("grad", "delta", "m", "v"):
        outs += [res[key, n] for n in _ALL_NAMES]
    return tuple(outs)
```

```python
import functools

import jax
import jax.numpy as jnp
from jax import lax
from jax.experimental import pallas as pl
from jax.experimental.pallas import tpu as pltpu

F32 = jnp.float32
BF16 = jnp.bfloat16
HI = lax.Precision.HIGHEST
MESH = pl.DeviceIdType.MESH

RMS_EPS = 1e-6
MLA_HEADS = 8
MLA_Q_RANK = 256
MLA_KV_RANK = 128
MLA_NOPE = 64
MLA_ROPE = 32
MLA_V = 64
ROPE_THETA = 10000.0
POOL_WINDOWS = (2, 4, 8, 16)
POOL_GROUP = 128
POOL_WIDTH = 512
POOL_HALO = 16
GDN_HEADS = 8
GDN_DK = 128
CONV_WIDTH = 4
CONV_HALO = 8
CHUNK = 64
IN_AB_PAD = 2048
IN_C_PAD = 4224
ATT_SCALE = (MLA_NOPE + MLA_ROPE) ** -0.5

ADAM_LR = 0.001
ADAM_B1 = 0.9
ADAM_B2 = 0.999
ADAM_EPS = 1e-08
ADAM_WD = 0.01
ADAM_STEP = 10

LANES = 128
VMEM_LIMIT = 56 * 1024 * 1024

ROW_TILE = 256
ATT_TILE = 1024
GDN_TILE = 256
MM_TILE = (1024, 1408, 2048)

NN = (((1,), (0,)), ((), ()))
NT = (((1,), (1,)), ((), ()))
TN = (((0,), (0,)), ((), ()))


def _dot(a, b, dims=NN, prec=None):
    return lax.dot_general(a, b, dims, precision=prec, preferred_element_type=F32)


def _tile(n, pref):
    if n <= pref:
        return n
    step = LANES if pref >= LANES else 8
    for t in range(pref - pref % step, 0, -step):
        if n % t == 0:
            return t
    return n


def _params(sem):
    return pltpu.CompilerParams(dimension_semantics=sem, vmem_limit_bytes=VMEM_LIMIT)


def _sigmoid(x):
    return 0.5 * jnp.tanh(0.5 * x) + 0.5


def _softplus(x):
    return jnp.maximum(x, 0.0) + jnp.log(1.0 + jnp.exp(-jnp.abs(x)))


def _matmul(a, b, mode, *, name):
    if mode == "nn":
        (m, k), (k2, n) = a.shape, b.shape
    elif mode == "nt":
        (m, k), (n, k2) = a.shape, b.shape
    else:
        (k, m), (k2, n) = a.shape, b.shape
    assert k == k2, (a.shape, b.shape, mode)
    tm, tn, tk = _tile(m, MM_TILE[0]), _tile(n, MM_TILE[1]), _tile(k, MM_TILE[2])
    nk = k // tk
    if mode == "tn":
        a_spec = pl.BlockSpec((tk, tm), lambda i, j, kk: (kk, i))
    else:
        a_spec = pl.BlockSpec((tm, tk), lambda i, j, kk: (i, kk))
    if mode == "nt":
        b_spec = pl.BlockSpec((tn, tk), lambda i, j, kk: (j, kk))
    else:
        b_spec = pl.BlockSpec((tk, tn), lambda i, j, kk: (kk, j))
    o_spec = pl.BlockSpec((tm, tn), lambda i, j, kk: (i, j))
    dims = {"nn": NN, "nt": NT, "tn": TN}[mode]

    def body(a_ref, b_ref, o_ref, *scratch):
        if nk == 1:
            o_ref[...] = _dot(a_ref[...], b_ref[...], dims)
            return
        acc = scratch[0]
        kk = pl.program_id(2)

        @pl.when(kk == 0)
        def _():
            acc[...] = jnp.zeros_like(acc)

        acc[...] += _dot(a_ref[...], b_ref[...], dims)

        @pl.when(kk == nk - 1)
        def _():
            o_ref[...] = acc[...]

    return pl.pallas_call(
        body, name=name, grid=(m // tm, n // tn, nk), in_specs=[a_spec, b_spec], out_specs=o_spec,
        out_shape=jax.ShapeDtypeStruct((m, n), F32),
        scratch_shapes=[pltpu.VMEM((tm, tn), F32)] if nk > 1 else [],
        compiler_params=_params(("parallel", "parallel", "arbitrary")),
    )(a, b)


def _rms_in_proj(h, g, w, *, name):
    t, d = h.shape
    n = w.shape[1]
    tm, tn = _tile(t, MM_TILE[0]), _tile(n, MM_TILE[1])

    def body(h_ref, g_ref, w_ref, o_ref, hn_ref):
        @pl.when(pl.program_id(1) == 0)
        def _():
            x = h_ref[...]
            r = lax.rsqrt(jnp.mean(x * x, axis=-1, keepdims=True) + RMS_EPS)
            hn_ref[...] = (x * r * g_ref[...]).astype(BF16)

        o_ref[...] = _dot(hn_ref[...], w_ref[...])

    return pl.pallas_call(
        body, name=name, grid=(t // tm, n // tn),
        in_specs=[pl.BlockSpec((tm, d), lambda i, j: (i, 0)), pl.BlockSpec((1, d), lambda i, j: (0, 0)),
                  pl.BlockSpec((d, tn), lambda i, j: (0, j))],
        out_specs=[pl.BlockSpec((tm, tn), lambda i, j: (i, j)), pl.BlockSpec((tm, d), lambda i, j: (i, 0))],
        out_shape=[jax.ShapeDtypeStruct((t, n), F32), jax.ShapeDtypeStruct((t, d), BF16)],
        compiler_params=_params(("parallel", "arbitrary")),
    )(h, g, w)


def _matmul_rms_bwd(dproj, w, h, g, dres, *, name, with_bf16):
    t, k = dproj.shape
    d = w.shape[0]
    tm = _tile(t, 2 * ROW_TILE)

    def body(dp_ref, w_ref, h_ref, g_ref, dres_ref, *outs):
        i = pl.program_id(0)
        dh_ref, dg_ref = outs[0], outs[-1]
        dyv = _dot(dp_ref[...], w_ref[...], NT)
        x = h_ref[...]
        r = lax.rsqrt(jnp.mean(x * x, axis=-1, keepdims=True) + RMS_EPS)
        xh = x * r
        dxh = dyv * g_ref[...]
        dh = dres_ref[...] + r * (dxh - xh * jnp.mean(dxh * xh, axis=-1, keepdims=True))
        dh_ref[...] = dh
        if with_bf16:
            outs[1][...] = dh.astype(BF16)

        @pl.when(i == 0)
        def _():
            dg_ref[...] = jnp.zeros_like(dg_ref)

        dg_ref[...] += jnp.sum(dyv * xh, axis=0, keepdims=True)

    row = pl.BlockSpec((tm, d), lambda i: (i, 0))
    vec = pl.BlockSpec((1, d), lambda i: (0, 0))
    out_shape = [jax.ShapeDtypeStruct((t, d), F32)] + ([jax.ShapeDtypeStruct((t, d), BF16)] if with_bf16 else [])
    out_specs = [row] * len(out_shape) + [vec]
    out_shape.append(jax.ShapeDtypeStruct((1, d), F32))
    return pl.pallas_call(
        body, name=name, grid=(t // tm,),
        in_specs=[pl.BlockSpec((tm, k), lambda i: (i, 0)), pl.BlockSpec((d, k), lambda i: (0, 0)), row, vec, row],
        out_specs=out_specs, out_shape=out_shape, compiler_params=_params(("arbitrary",)),
    )(dproj, w, h, g, dres)


def _rope_partner(x):
    lane = lax.broadcasted_iota(jnp.int32, x.shape, 1)
    swapped = jnp.where(lane < MLA_NOPE + MLA_ROPE // 2, pltpu.roll(x, LANES - 16, 1), pltpu.roll(x, 16, 1))
    return jnp.where((lane >= MLA_NOPE) & (lane < MLA_NOPE + MLA_ROPE), swapped, 0.0)


def _pool_counts(row0, tm, w):
    t_idx = row0 + lax.broadcasted_iota(jnp.int32, (tm, POOL_GROUP), 0)
    return jnp.minimum(t_idx + 1, w).astype(F32)


def _ab_prep(proj, pos, inv_freq, q_a_norm, kv_a_norm, wq, wk, wv, wpool, *, name):
    t = proj.shape[0]
    tm = _tile(t, ROW_TILE)
    hb = tm // POOL_HALO

    def body(p_ref, halo_ref, pos_ref, inv_ref, qg_ref, kg_ref, wq_ref, wk_ref, wv_ref, wp_ref,
             q_ref, k_ref, v_ref, yb_ref, qn_ref, kvn_ref, d_ref, cos_ref, sin_ref, ext):
        i = pl.program_id(0)
        ql = p_ref[:, 0:MLA_Q_RANK]
        r = lax.rsqrt(jnp.mean(ql * ql, axis=-1, keepdims=True) + RMS_EPS)
        qn = (ql * r * qg_ref[...]).astype(BF16)
        qn_ref[...] = qn
        kl = p_ref[:, MLA_Q_RANK:MLA_Q_RANK + MLA_KV_RANK]
        r = lax.rsqrt(jnp.mean(kl * kl, axis=-1, keepdims=True) + RMS_EPS)
        kvn = (kl * r * kg_ref[...]).astype(BF16)
        kvn_ref[...] = kvn
        ang = pos_ref[...].astype(F32) * inv_ref[...]
        lane = lax.broadcasted_iota(jnp.int32, (tm, LANES), 1)
        in_rope = (lane >= MLA_NOPE) & (lane < MLA_NOPE + MLA_ROPE)
        cos_t = jnp.where(in_rope, jnp.cos(ang), 1.0)
        sin_t = jnp.where(in_rope, jnp.sin(ang), 0.0)
        sin_t = jnp.where(lane < MLA_NOPE + MLA_ROPE // 2, -sin_t, sin_t)
        cos_ref[...] = cos_t
        sin_ref[...] = sin_t
        kr = p_ref[:, 384:512]
        kr = kr * cos_t + _rope_partner(kr) * sin_t
        qraw = _dot(qn, wq_ref[...])
        kvk = _dot(kvn, wk_ref[...])
        for h in range(MLA_HEADS):
            sl = slice(h * LANES, (h + 1) * LANES)
            qh = qraw[:, sl]
            q_ref[:, sl] = ((qh * cos_t + _rope_partner(qh) * sin_t) * ATT_SCALE).astype(BF16)
            k_ref[:, sl] = (kvk[:, sl] + kr).astype(BF16)
        v_ref[...] = _dot(kvn, wv_ref[...]).astype(BF16)
        xp = p_ref[:, 512:1024]
        ext[0:POOL_HALO, :] = jnp.where(i > 0, halo_ref[...], 0.0)
        ext[POOL_HALO:POOL_HALO + tm, :] = xp
        for g, w in enumerate(POOL_WINDOWS):
            lo = g * POOL_GROUP
            acc = ext[POOL_HALO:POOL_HALO + tm, lo:lo + POOL_GROUP]
            for s in range(1, w):
                acc = acc + ext[POOL_HALO - s:POOL_HALO - s + tm, lo:lo + POOL_GROUP]
            cnt = _pool_counts(i * tm, tm, w)
            d_ref[:, lo:lo + POOL_GROUP] = (acc / cnt - xp[:, lo:lo + POOL_GROUP]).astype(BF16)
        yb_ref[...] = _dot(d_ref[...], wp_ref[...])

    row = lambda w: pl.BlockSpec((tm, w), lambda i: (i, 0))
    vec = lambda w: pl.BlockSpec((1, w), lambda i: (0, 0))
    whole = lambda a: pl.BlockSpec(a.shape, lambda i: (0, 0))
    return pl.pallas_call(
        body, name=name, grid=(t // tm,),
        in_specs=[row(1024), pl.BlockSpec((POOL_HALO, POOL_WIDTH), lambda i: (jnp.maximum(i * hb - 1, 0), 1)),
                  pl.BlockSpec((tm, 1), lambda i: (i, 0)), vec(LANES), vec(MLA_Q_RANK), vec(MLA_KV_RANK),
                  whole(wq), whole(wk), whole(wv), whole(wpool)],
        out_specs=[row(1024), row(1024), row(512), row(512), row(MLA_Q_RANK), row(MLA_KV_RANK), row(POOL_WIDTH),
                   row(LANES), row(LANES)],
        out_shape=[jax.ShapeDtypeStruct((t, 1024), BF16), jax.ShapeDtypeStruct((t, 1024), BF16),
                   jax.ShapeDtypeStruct((t, 512), BF16), jax.ShapeDtypeStruct((t, 512), F32),
                   jax.ShapeDtypeStruct((t, MLA_Q_RANK), BF16), jax.ShapeDtypeStruct((t, MLA_KV_RANK), BF16),
                   jax.ShapeDtypeStruct((t, POOL_WIDTH), BF16), jax.ShapeDtypeStruct((t, LANES), F32),
                   jax.ShapeDtypeStruct((t, LANES), F32)],
        scratch_shapes=[pltpu.VMEM((tm + POOL_HALO, POOL_WIDTH), F32)],
        compiler_params=_params(("parallel",)),
    )(proj, proj, pos, inv_freq, q_a_norm, kv_a_norm, wq, wk, wv, wpool)


def _ab_prep_bwd(proj, q_a_norm, kv_a_norm, dq, dk, dv, cos_t, sin_t, dyb, dz, wq, wk, wv, wpool, *, name):
    t = proj.shape[0]
    tm = _tile(t, ROW_TILE)
    hb = tm // POOL_HALO
    last_halo = t // POOL_HALO - 1
    nt = t // tm

    def body(p_ref, qg_ref, kg_ref, dq_ref, dk_ref, dv_ref, c_ref, s_ref, dyb_ref, dybn_ref, dz_ref,
             wq_ref, wk_ref, wv_ref, wp_ref, dp_ref, dqr_ref, dkb_ref, dqg_ref, dkg_ref, ext):
        i = pl.program_id(0)

        @pl.when(i == 0)
        def _():
            dqg_ref[...] = jnp.zeros_like(dqg_ref)
            dkg_ref[...] = jnp.zeros_like(dkg_ref)

        def norm_bwd(x, g, dy, dg_ref):
            r = lax.rsqrt(jnp.mean(x * x, axis=-1, keepdims=True) + RMS_EPS)
            xh = x * r
            dxh = dy * g
            dg_ref[...] += jnp.sum(dy * xh, axis=0, keepdims=True)
            return r * (dxh - xh * jnp.mean(dxh * xh, axis=-1, keepdims=True))

        c, s = c_ref[...], s_ref[...]
        lane = lax.broadcasted_iota(jnp.int32, (tm, LANES), 1)
        in_rope = (lane >= MLA_NOPE) & (lane < MLA_NOPE + MLA_ROPE)
        dkr = jnp.zeros((tm, LANES), F32)
        for h in range(MLA_HEADS):
            sl = slice(h * LANES, (h + 1) * LANES)
            g = dq_ref[:, sl]
            dqr_ref[:, sl] = ((g * c + _rope_partner(g * s)) * ATT_SCALE).astype(BF16)
            gk = dk_ref[:, sl]
            dkb_ref[:, sl] = gk.astype(BF16)
            dkr = dkr + jnp.where(in_rope, gk, 0.0)
        dkr = dkr * c + _rope_partner(dkr * s)
        dqn = _dot(dqr_ref[...], wq_ref[...], NT)
        dkvn = _dot(dkb_ref[...], wk_ref[...], NT) + _dot(dv_ref[...], wv_ref[...], NT)
        dql = norm_bwd(p_ref[:, 0:MLA_Q_RANK], qg_ref[...], dqn, dqg_ref)
        dp_ref[:, 0:MLA_Q_RANK] = dql.astype(BF16)
        dkl = norm_bwd(p_ref[:, MLA_Q_RANK:384], kg_ref[...], dkvn, dkg_ref)
        dp_ref[:, MLA_Q_RANK:384] = dkl.astype(BF16)
        dp_ref[:, 384:512] = dkr.astype(BF16)
        ddv = _dot(dyb_ref[...], wp_ref[...], NT)
        ddn = _dot(dybn_ref[...], wp_ref[...], NT)
        for g, w in enumerate(POOL_WINDOWS):
            lo = g * POOL_GROUP
            ext[0:tm, lo:lo + POOL_GROUP] = ddv[:, lo:lo + POOL_GROUP] / _pool_counts(i * tm, tm, w)
            nxt = ddn[:, lo:lo + POOL_GROUP] / _pool_counts((i + 1) * tm, POOL_HALO, w)
            ext[tm:tm + POOL_HALO, lo:lo + POOL_GROUP] = jnp.where(i < nt - 1, nxt, 0.0)
        for g, w in enumerate(POOL_WINDOWS):
            lo = g * POOL_GROUP
            acc = ext[0:tm, lo:lo + POOL_GROUP]
            for s in range(1, w):
                acc = acc + ext[s:s + tm, lo:lo + POOL_GROUP]
            dp_ref[:, 512 + lo:512 + lo + POOL_GROUP] = (acc - ddv[:, lo:lo + POOL_GROUP]).astype(BF16)
        dp_ref[:, 1024:2048] = dz_ref[...]

    row = lambda w: pl.BlockSpec((tm, w), lambda i: (i, 0))
    vec = lambda w: pl.BlockSpec((1, w), lambda i: (0, 0))
    whole = lambda a: pl.BlockSpec(a.shape, lambda i: (0, 0))
    return pl.pallas_call(
        body, name=name, grid=(nt,),
        in_specs=[row(1024), vec(MLA_Q_RANK), vec(MLA_KV_RANK), row(1024), row(1024), row(512), row(LANES), row(LANES),
                  row(POOL_WIDTH),
                  pl.BlockSpec((POOL_HALO, POOL_WIDTH), lambda i: (jnp.minimum((i + 1) * hb, last_halo), 0)),
                  row(1024), whole(wq), whole(wk), whole(wv), whole(wpool)],
        out_specs=[row(IN_AB_PAD), row(1024), row(1024), vec(MLA_Q_RANK), vec(MLA_KV_RANK)],
        out_shape=[jax.ShapeDtypeStruct((t, IN_AB_PAD), BF16), jax.ShapeDtypeStruct((t, 1024), BF16),
                   jax.ShapeDtypeStruct((t, 1024), BF16), jax.ShapeDtypeStruct((1, MLA_Q_RANK), F32),
                   jax.ShapeDtypeStruct((1, MLA_KV_RANK), F32)],
        scratch_shapes=[pltpu.VMEM((tm + POOL_HALO, POOL_WIDTH), F32)],
        compiler_params=_params(("arbitrary",)),
    )(proj, q_a_norm, kv_a_norm, dq, dk, dv, cos_t, sin_t, dyb, dyb, dz, wq, wk, wv, wpool)


def _gate_out_proj(o, ybraw, proj, pool_scale, w, hres, *, name):
    t = o.shape[0]
    tm = _tile(t, 2 * ROW_TILE)

    def body(o_ref, yb_ref, z_ref, ps_ref, w_ref, h_ref, ho_ref, y_ref):
        z = z_ref[...]
        sz = z * _sigmoid(z)
        y_ref[:, 0:512] = (o_ref[...] * sz[:, 0:512]).astype(BF16)
        y_ref[:, 512:1024] = (yb_ref[...] * ps_ref[...] * sz[:, 512:1024]).astype(BF16)
        ho_ref[...] = h_ref[...] + _dot(y_ref[...], w_ref[...])

    row = lambda w_: pl.BlockSpec((tm, w_), lambda i: (i, 0))
    return pl.pallas_call(
        body, name=name, grid=(t // tm,),
        in_specs=[row(512), row(512), pl.BlockSpec((tm, 1024), lambda i: (i, 1)), pl.BlockSpec((1, 512), lambda i: (0, 0)),
                  pl.BlockSpec(w.shape, lambda i: (0, 0)), row(1024)],
        out_specs=[row(1024), row(1024)],
        out_shape=[jax.ShapeDtypeStruct((t, 1024), F32), jax.ShapeDtypeStruct((t, 1024), BF16)],
        compiler_params=_params(("parallel",)),
    )(o, ybraw, proj, pool_scale, w, hres)


def _gate_bwd(dy, o, ybraw, proj, pool_scale, *, name):
    t = o.shape[0]
    tm = _tile(t, ROW_TILE)

    def body(dy_ref, o_ref, yb_ref, z_ref, ps_ref, do_ref, dl_ref, dyb_ref, dz_ref, dps_ref):
        i = pl.program_id(0)
        z = z_ref[...]
        sg = _sigmoid(z)
        sz = z * sg
        dsz = sg * (1.0 + z * (1.0 - sg))
        dyv = dy_ref[...]
        dcat = dyv * sz
        ov = o_ref[...]
        ybs = yb_ref[...] * ps_ref[...]
        dz_ref[:, 0:512] = (dyv[:, 0:512] * ov * dsz[:, 0:512]).astype(BF16)
        dz_ref[:, 512:1024] = (dyv[:, 512:1024] * ybs * dsz[:, 512:1024]).astype(BF16)
        do = dcat[:, 0:512]
        do_ref[...] = do.astype(BF16)
        r_i = (lax.broadcasted_iota(jnp.int32, (1024, 512), 0) % 512) // MLA_V
        c_i = lax.broadcasted_iota(jnp.int32, (1024, 512), 1) // MLA_V
        prod = do * ov
        hi = prod.astype(BF16)
        lo = (prod - hi.astype(F32)).astype(BF16)
        dl_ref[...] = _dot(jnp.concatenate([hi, lo], axis=1), (r_i == c_i).astype(BF16))
        dyb_ref[...] = (dcat[:, 512:1024] * ps_ref[...]).astype(BF16)

        @pl.when(i == 0)
        def _():
            dps_ref[...] = jnp.zeros_like(dps_ref)

        dps_ref[...] += jnp.sum(dcat[:, 512:1024] * yb_ref[...], axis=0, keepdims=True)

    row = lambda w: pl.BlockSpec((tm, w), lambda i: (i, 0))
    vec = pl.BlockSpec((1, 512), lambda i: (0, 0))
    return pl.pallas_call(
        body, name=name, grid=(t // tm,),
        in_specs=[row(1024), row(512), row(512), pl.BlockSpec((tm, 1024), lambda i: (i, 1)), vec],
        out_specs=[row(512), row(512), row(512), row(1024), vec],
        out_shape=[jax.ShapeDtypeStruct((t, 512), BF16), jax.ShapeDtypeStruct((t, 512), F32),
                   jax.ShapeDtypeStruct((t, 512), BF16), jax.ShapeDtypeStruct((t, 1024), BF16),
                   jax.ShapeDtypeStruct((1, 512), F32)],
        compiler_params=_params(("arbitrary",)),
    )(dy, o, ybraw, proj, pool_scale)


ATT_HP_FWD = 4
ATT_HP_BWD = 2


def _diag_mask(tq):
    return lax.broadcasted_iota(jnp.int32, (tq, tq), 1) <= lax.broadcasted_iota(jnp.int32, (tq, tq), 0)


def _block_schedule(nq, key_major):
    if key_major:
        pairs = [(qi, ki) for ki in range(nq) for qi in range(ki, nq)]
    else:
        pairs = [(qi, ki) for qi in range(nq) for ki in range(qi + 1)]
    return jnp.asarray([p[0] for p in pairs], jnp.int32), jnp.asarray([p[1] for p in pairs], jnp.int32)


def _attn_fwd(q, k, v, *, name):
    t = q.shape[0]
    tq = _tile(t, ATT_TILE)
    nq = t // tq
    hp = ATT_HP_FWD
    qi_tab, ki_tab = _block_schedule(nq, key_major=False)

    def body(qi_ref, ki_ref, q_ref, k_ref, v_ref, o_ref, lse_ref, m_sc, l_sc, acc_sc):
        step = pl.program_id(1)
        qi, ki = qi_ref[step], ki_ref[step]

        @pl.when(ki == 0)
        def _():
            m_sc[...] = jnp.full_like(m_sc, -jnp.inf)
            l_sc[...] = jnp.zeros_like(l_sc)
            acc_sc[...] = jnp.zeros_like(acc_sc)

        def block(on_diagonal):
            scores = []
            for h in range(hp):
                sl = slice(h * LANES, (h + 1) * LANES)
                scores.append(_dot(q_ref[:, sl], k_ref[:, sl], NT))
            if on_diagonal:
                mask = _diag_mask(tq)
                scores = [jnp.where(mask, s, -jnp.inf) for s in scores]
            for h, s in enumerate(scores):
                vv = v_ref[:, (h // 2) * LANES:(h // 2 + 1) * LANES]
                m_prev = m_sc[h]
                m_new = jnp.maximum(m_prev, jnp.max(s, axis=-1, keepdims=True))
                alpha = jnp.exp(m_prev - m_new)
                p = jnp.exp(s - m_new[:, 0:1])
                l_sc[h] = alpha * l_sc[h] + jnp.sum(p, axis=-1, keepdims=True)
                acc_sc[h] = alpha * acc_sc[h] + _dot(p.astype(BF16), vv)
                m_sc[h] = m_new

        pl.when(ki < qi)(functools.partial(block, False))
        pl.when(ki == qi)(functools.partial(block, True))

        @pl.when(ki == qi)
        def _():
            first = lax.broadcasted_iota(jnp.int32, (tq, LANES), 1) < MLA_V
            for pr in range(hp // 2):
                a, b = 2 * pr, 2 * pr + 1
                sl = slice(pr * LANES, (pr + 1) * LANES)
                o_ref[:, sl] = jnp.where(first, acc_sc[a] / l_sc[a], acc_sc[b] / l_sc[b])
                lse_ref[:, sl] = jnp.where(first, m_sc[a] + jnp.log(l_sc[a]), m_sc[b] + jnp.log(l_sc[b]))

    grid_spec = pltpu.PrefetchScalarGridSpec(
        num_scalar_prefetch=2, grid=(MLA_HEADS // hp, qi_tab.shape[0]),
        in_specs=[pl.BlockSpec((tq, hp * LANES), lambda g, s, qt, kt: (qt[s], g)),
                  pl.BlockSpec((tq, hp * LANES), lambda g, s, qt, kt: (kt[s], g)),
                  pl.BlockSpec((tq, hp * MLA_V), lambda g, s, qt, kt: (kt[s], g))],
        out_specs=[pl.BlockSpec((tq, hp * MLA_V), lambda g, s, qt, kt: (qt[s], g)),
                   pl.BlockSpec((tq, hp * MLA_V), lambda g, s, qt, kt: (qt[s], g))],
        scratch_shapes=[pltpu.VMEM((hp, tq, LANES), F32)] * 3,
    )
    return pl.pallas_call(
        body, name=name, grid_spec=grid_spec,
        out_shape=[jax.ShapeDtypeStruct((t, 512), F32), jax.ShapeDtypeStruct((t, 512), F32)],
        compiler_params=_params(("parallel", "arbitrary")),
    )(qi_tab, ki_tab, q, k, v)


def _attn_bwd(q, k, v, do, lse, delta, *, name):
    t = q.shape[0]
    tq = _tile(t, ATT_TILE)
    nq = t // tq
    hp = ATT_HP_BWD
    qi_tab, ki_tab = _block_schedule(nq, key_major=True)

    def body(qi_ref, ki_ref, q_ref, k_ref, v_ref, do_ref, lse_ref, dl_ref, dq_ref, dk_ref, dv_ref, dk_sc, dv_sc):
        step = pl.program_id(1)
        qi, ki = qi_ref[step], ki_ref[step]

        @pl.when(step == 0)
        def _():
            dq_ref[...] = jnp.zeros_like(dq_ref)

        @pl.when(qi == ki)
        def _():
            dk_sc[...] = jnp.zeros_like(dk_sc)
            dv_sc[...] = jnp.zeros_like(dv_sc)

        def block(on_diagonal):
            lane = lax.broadcasted_iota(jnp.int32, (tq, LANES), 1)
            rows = pl.ds(pl.multiple_of(qi * tq, tq), tq)
            heads = [slice(h * LANES, (h + 1) * LANES) for h in range(hp)]
            scores = [_dot(q_ref[:, sl], k_ref[:, sl], NT) for sl in heads]
            dps = []
            for h in range(hp):
                dov = do_ref[:, (h // 2) * LANES:(h // 2 + 1) * LANES]
                mine = (lane < MLA_V) if h % 2 == 0 else (lane >= MLA_V)
                dps.append(_dot(jnp.where(mine, dov, jnp.zeros_like(dov)), v_ref[:, (h // 2) * LANES:(h // 2 + 1) * LANES], NT))
            mask = _diag_mask(tq) if on_diagonal else None
            for h, sl in enumerate(heads):
                col = (h // 2) * LANES + (h % 2) * MLA_V
                p = jnp.exp(scores[h] - lse_ref[:, col:col + 1])
                if on_diagonal:
                    p = jnp.where(mask, p, 0.0)
                ds = (p * (dps[h] - dl_ref[:, col:col + 1])).astype(BF16)
                dv_sc[h] += _dot(p.astype(BF16), do_ref[:, (h // 2) * LANES:(h // 2 + 1) * LANES], TN)
                dk_sc[h] += _dot(ds, q_ref[:, sl], TN)
                dq_ref[rows, sl] += _dot(ds, k_ref[:, sl], NN)

        pl.when(qi > ki)(functools.partial(block, False))
        pl.when(qi == ki)(functools.partial(block, True))

        @pl.when(qi == nq - 1)
        def _():
            first = lax.broadcasted_iota(jnp.int32, (tq, LANES), 1) < MLA_V
            for h in range(hp):
                dk_ref[:, h * LANES:(h + 1) * LANES] = dk_sc[h]
            for pr in range(hp // 2):
                dv_ref[:, pr * LANES:(pr + 1) * LANES] = jnp.where(first, dv_sc[2 * pr], dv_sc[2 * pr + 1]).astype(BF16)

    qrow = lambda w: pl.BlockSpec((tq, w), lambda g, s, qt, kt: (qt[s], g))
    krow = lambda w: pl.BlockSpec((tq, w), lambda g, s, qt, kt: (kt[s], g))
    grid_spec = pltpu.PrefetchScalarGridSpec(
        num_scalar_prefetch=2, grid=(MLA_HEADS // hp, qi_tab.shape[0]),
        in_specs=[qrow(hp * LANES), krow(hp * LANES), krow(hp * MLA_V), qrow(hp * MLA_V), qrow(hp * MLA_V), qrow(hp * MLA_V)],
        out_specs=[pl.BlockSpec((t, hp * LANES), lambda g, s, qt, kt: (0, g)), krow(hp * LANES), krow(hp * MLA_V)],
        scratch_shapes=[pltpu.VMEM((hp, tq, LANES), F32), pltpu.VMEM((hp, tq, LANES), F32)],
    )
    return pl.pallas_call(
        body, name=name, grid_spec=grid_spec,
        out_shape=[jax.ShapeDtypeStruct((t, 1024), F32), jax.ShapeDtypeStruct((t, 1024), F32),
                   jax.ShapeDtypeStruct((t, 512), BF16)],
        compiler_params=_params(("parallel", "arbitrary")),
    )(qi_tab, ki_tab, q, k, v, do, lse, delta)


def _conv_rows(ext, tm, w_ref, sec):
    c0 = sec * 1024
    y = ext[CONV_HALO - 3:CONV_HALO - 3 + tm, c0:c0 + 1024] * w_ref[0:1, c0:c0 + 1024]
    for j in range(1, CONV_WIDTH):
        y = y + ext[CONV_HALO - 3 + j:CONV_HALO - 3 + j + tm, c0:c0 + 1024] * w_ref[j:j + 1, c0:c0 + 1024]
    return y


def _c_prep(proj_c, conv_w, a_log, dt_bias, *, name):
    t = proj_c.shape[0]
    tm = _tile(t, ROW_TILE)
    hb = tm // CONV_HALO

    def body(p_ref, halo_ref, ab_ref, w_ref, al_ref, dtb_ref, q_ref, k_ref, v_ref, g_ref, b_ref, gt_ref, ext):
        i = pl.program_id(0)
        ext[0:CONV_HALO, :] = jnp.where(i > 0, halo_ref[...], 0.0)
        ext[CONV_HALO:CONV_HALO + tm, :] = p_ref[...]
        for sec, o_ref in enumerate((q_ref, k_ref, v_ref)):
            y = _conv_rows(ext, tm, w_ref, sec)
            y = y * _sigmoid(y)
            if sec == 2:
                o_ref[...] = y
                continue
            scale = GDN_DK ** -0.5 if sec == 0 else 1.0
            for h in range(GDN_HEADS):
                sl = slice(h * LANES, (h + 1) * LANES)
                blk = y[:, sl]
                r = lax.rsqrt(jnp.sum(blk * blk, axis=-1, keepdims=True) + RMS_EPS)
                o_ref[:, sl] = blk * (r * scale)
        ab = ab_ref[...]
        g = -jnp.exp(al_ref[...]) * _softplus(ab + dtb_ref[...])
        beta = _sigmoid(ab)
        ri = lax.broadcasted_iota(jnp.int32, (tm, tm), 0)
        ci = lax.broadcasted_iota(jnp.int32, (tm, tm), 1)
        lower = ((ri // CHUNK) == (ci // CHUNK)) & (ri >= ci)
        gc = _dot(lower.astype(F32), g, NN, HI)
        eye = lax.broadcasted_iota(jnp.int32, (LANES, LANES), 0) == lax.broadcasted_iota(jnp.int32, (LANES, LANES), 1)
        gt_ref[...] = _dot(eye.astype(F32), gc, NT, HI)[0:GDN_HEADS, :]
        for h in range(GDN_HEADS):
            sl = slice(h * LANES, (h + 1) * LANES)
            g_ref[:, sl] = jnp.broadcast_to(gc[:, h:h + 1], (tm, LANES))
            b_ref[:, sl] = jnp.broadcast_to(beta[:, GDN_HEADS + h:GDN_HEADS + h + 1], (tm, LANES))

    row = lambda w: pl.BlockSpec((tm, w), lambda i: (i, 0))
    vec = lambda r, w: pl.BlockSpec((r, w), lambda i: (0, 0))
    out = jax.ShapeDtypeStruct((t, 1024), F32)
    return pl.pallas_call(
        body, name=name, grid=(t // tm,),
        in_specs=[row(3072), pl.BlockSpec((CONV_HALO, 3072), lambda i: (jnp.maximum(i * hb - 1, 0), 0)),
                  pl.BlockSpec((tm, LANES), lambda i: (i, 32)), vec(CONV_WIDTH, 3072), vec(1, LANES), vec(1, LANES)],
        out_specs=[row(1024)] * 5 + [pl.BlockSpec((GDN_HEADS, tm), lambda i: (0, i))],
        out_shape=[out] * 5 + [jax.ShapeDtypeStruct((GDN_HEADS, t), F32)],
        scratch_shapes=[pltpu.VMEM((tm + CONV_HALO, 3072), F32)],
        compiler_params=_params(("parallel",)),
    )(proj_c, proj_c, proj_c, conv_w, a_log, dt_bias)


def _c_prep_bwd(proj_c, conv_w, a_log, dt_bias, dq, dk, dv, dgb, dbb, dz, *, name):
    t = proj_c.shape[0]
    tm = _tile(t, ROW_TILE // 2)
    hb = tm // CONV_HALO
    nt = t // tm
    rev = lambda i: nt - 1 - i

    def body(p_ref, halo_ref, ab_ref, w_ref, al_ref, dtb_ref, dq_ref, dk_ref, dv_ref, dg_ref, db_ref, dz_ref,
             dp_ref, dw_ref, dal_ref, ddt_ref, ext, dyext, carry, taps):
        step = pl.program_id(0)
        i = rev(step)

        @pl.when(step == 0)
        def _():
            dw_ref[...] = jnp.zeros_like(dw_ref)
            dal_ref[...] = jnp.zeros_like(dal_ref)
            ddt_ref[...] = jnp.zeros_like(ddt_ref)
            carry[...] = jnp.zeros_like(carry)

        ext[0:CONV_HALO, :] = jnp.where(i > 0, halo_ref[...], 0.0)
        ext[CONV_HALO:CONV_HALO + tm, :] = p_ref[...]
        for sec, g_ref in enumerate((dq_ref, dk_ref, dv_ref)):
            c0 = sec * 1024
            for j in range(CONV_WIDTH):
                taps[j] = ext[CONV_HALO - 3 + j:CONV_HALO - 3 + j + tm, c0:c0 + 1024]
            y = taps[0] * w_ref[0:1, c0:c0 + 1024]
            for j in range(1, CONV_WIDTH):
                y = y + taps[j] * w_ref[j:j + 1, c0:c0 + 1024]
            sg = _sigmoid(y)
            act = y * sg
            if sec == 2:
                dact = g_ref[...]
            else:
                scale = GDN_DK ** -0.5 if sec == 0 else 1.0
                parts = []
                for h in range(GDN_HEADS):
                    sl = slice(h * LANES, (h + 1) * LANES)
                    blk = act[:, sl]
                    r = lax.rsqrt(jnp.sum(blk * blk, axis=-1, keepdims=True) + RMS_EPS)
                    n = blk * r
                    dn = g_ref[:, sl] * scale
                    parts.append(r * (dn - n * jnp.sum(dn * n, axis=-1, keepdims=True)))
                dact = jnp.concatenate(parts, axis=-1)
            dy = dact * (sg * (1.0 + y * (1.0 - sg)))
            dyext[0:tm, c0:c0 + 1024] = dy
            for j in range(CONV_WIDTH):
                dw_ref[j:j + 1, c0:c0 + 1024] += jnp.sum(dy * taps[j], axis=0, keepdims=True)
        dyext[tm:tm + CONV_HALO, :] = carry[...]
        carry[...] = dyext[0:CONV_HALO, :]
        for sec in range(3):
            c0 = sec * 1024
            dx = dyext[3:3 + tm, c0:c0 + 1024] * w_ref[0:1, c0:c0 + 1024]
            for j in range(1, CONV_WIDTH):
                dx = dx + dyext[3 - j:3 - j + tm, c0:c0 + 1024] * w_ref[j:j + 1, c0:c0 + 1024]
            dp_ref[:, c0:c0 + 1024] = dx.astype(BF16)
        dp_ref[:, 3072:4096] = dz_ref[...]
        lane = lax.broadcasted_iota(jnp.int32, (tm, LANES), 1)
        dg = jnp.zeros((tm, LANES), F32)
        dbeta = jnp.zeros((tm, LANES), F32)
        for h in range(GDN_HEADS):
            sl = slice(h * LANES, (h + 1) * LANES)
            dg = dg + jnp.where(lane == h, dg_ref[:, sl], 0.0)
            dbeta = dbeta + jnp.where(lane == GDN_HEADS + h, db_ref[:, sl], 0.0)
        ri = lax.broadcasted_iota(jnp.int32, (tm, tm), 0)
        ci = lax.broadcasted_iota(jnp.int32, (tm, tm), 1)
        upper = ((ri // CHUNK) == (ci // CHUNK)) & (ri <= ci)
        dg = _dot(upper.astype(F32), dg, NN, HI)
        pre = ab_ref[...] + dtb_ref[...]
        s = _sigmoid(pre)
        a_exp = jnp.exp(al_ref[...])
        dg_da = dg * (-a_exp * s)
        dp_ref[:, 4096:IN_C_PAD] = (dg_da + dbeta * s * (1.0 - s)).astype(BF16)
        dal_ref[...] += jnp.sum(dg * (-a_exp * _softplus(pre)), axis=0, keepdims=True)
        ddt_ref[...] += jnp.sum(dg_da, axis=0, keepdims=True)

    row = lambda w: pl.BlockSpec((tm, w), lambda s: (rev(s), 0))
    vec = lambda r, w: pl.BlockSpec((r, w), lambda s: (0, 0))
    return pl.pallas_call(
        body, name=name, grid=(nt,),
        in_specs=[row(3072), pl.BlockSpec((CONV_HALO, 3072), lambda s: (jnp.maximum(rev(s) * hb - 1, 0), 0)),
                  pl.BlockSpec((tm, LANES), lambda s: (rev(s), 32)), vec(CONV_WIDTH, 3072), vec(1, LANES), vec(1, LANES),
                  row(1024), row(1024), row(1024), row(1024), row(1024), row(1024)],
        out_specs=[row(IN_C_PAD), vec(CONV_WIDTH, 3072), vec(1, LANES), vec(1, LANES)],
        out_shape=[jax.ShapeDtypeStruct((t, IN_C_PAD), BF16), jax.ShapeDtypeStruct((CONV_WIDTH, 3072), F32),
                   jax.ShapeDtypeStruct((1, LANES), F32), jax.ShapeDtypeStruct((1, LANES), F32)],
        scratch_shapes=[pltpu.VMEM((tm + CONV_HALO, 3072), F32), pltpu.VMEM((tm + CONV_HALO, 3072), F32),
                        pltpu.VMEM((CONV_HALO, 3072), F32), pltpu.VMEM((CONV_WIDTH, tm, 1024), F32)],
        compiler_params=_params(("arbitrary",)),
    )(proj_c, proj_c, proj_c, conv_w, a_log, dt_bias, dq, dk, dv, dgb, dbb, dz)


def _o_gate_bwd(dy, o, proj_c, o_norm, *, name):
    t = o.shape[0]
    tm = _tile(t, 2 * ROW_TILE)

    def body(dy_ref, o_ref, z_ref, g_ref, do_ref, dz_ref, dg_ref):
        i = pl.program_id(0)

        @pl.when(i == 0)
        def _():
            dg_ref[...] = jnp.zeros_like(dg_ref)

        dg = jnp.zeros((1, LANES), F32)
        for h in range(GDN_HEADS):
            sl = slice(h * LANES, (h + 1) * LANES)
            x = o_ref[:, sl]
            r = lax.rsqrt(jnp.mean(x * x, axis=-1, keepdims=True) + RMS_EPS)
            xh = x * r
            z = z_ref[:, sl]
            sg = _sigmoid(z)
            dyv = dy_ref[:, sl]
            dn = dyv * (z * sg)
            dz_ref[:, sl] = (dyv * xh * g_ref[...] * (sg * (1.0 + z * (1.0 - sg)))).astype(BF16)
            dxh = dn * g_ref[...]
            do_ref[:, sl] = r * (dxh - xh * jnp.mean(dxh * xh, axis=-1, keepdims=True))
            dg = dg + jnp.sum(dn * xh, axis=0, keepdims=True)
        dg_ref[...] += dg

    row = pl.BlockSpec((tm, 1024), lambda i: (i, 0))
    vec = pl.BlockSpec((1, LANES), lambda i: (0, 0))
    return pl.pallas_call(
        body, name=name, grid=(t // tm,), in_specs=[row, row, pl.BlockSpec((tm, 1024), lambda i: (i, 3)), vec],
        out_specs=[row, row, vec],
        out_shape=[jax.ShapeDtypeStruct((t, 1024), F32), jax.ShapeDtypeStruct((t, 1024), BF16),
                   jax.ShapeDtypeStruct((1, LANES), F32)],
        compiler_params=_params(("arbitrary",)),
    )(dy, o, proj_c, o_norm)


PAIR = 2 * CHUNK
GDN_HP = 8


def _bdot(a, b, dims=NN):
    return _dot(a.astype(BF16), b.astype(BF16), dims)


def _each(f, *lists):
    return [f(*args) for args in zip(*lists)]


def _pair_common(q, k, v, gci, gcj, beta):
    ri = lax.broadcasted_iota(jnp.int32, (PAIR, PAIR), 0)
    ci = lax.broadcasted_iota(jnp.int32, (PAIR, PAIR), 1)
    same = (ri // CHUNK) == (ci // CHUNK)
    incl = same & (ri >= ci)
    strict = same & (ri > ci)
    eye = (ri == ci).astype(F32)
    first = lax.broadcasted_iota(jnp.int32, (PAIR, LANES), 0) < CHUNK
    gamma = _each(lambda gi, gj: jnp.where(incl, jnp.exp(jnp.minimum(gi - gj, 0.0)), 0.0), gci, gcj)
    kb = _each(jnp.multiply, k, beta)
    kk = _each(lambda a, b: _bdot(a, b, NT), kb, k)
    qk = _each(lambda a, b: _bdot(a, b, NT), q, k)
    m = _each(lambda x, g: jnp.where(strict, x * g, 0.0), kk, gamma)
    tm_ = _each(lambda x: eye - x, m)
    pw = _each(lambda x: _bdot(x, x), m)
    for it in range(5):
        tm_ = _each(lambda x, p: x + _bdot(x, p), tm_, pw)
        if it < 4:
            pw = _each(lambda p: _bdot(p, p), pw)
    eg = _each(jnp.exp, gci)
    vb = _each(jnp.multiply, v, beta)
    kbe = _each(jnp.multiply, kb, eg)
    uw = _each(lambda x, a, b: _bdot(x, jnp.concatenate([a, b], axis=1)), tm_, vb, kbe)
    attn = _each(lambda x, g: jnp.where(incl, x * g, 0.0), qk, gamma)
    gl_a = _each(lambda g: g[CHUNK - 1:CHUNK, :], gci)
    gl_b = _each(lambda g: g[PAIR - 1:PAIR, :], gci)
    ek = _each(lambda a, b, g: jnp.exp(jnp.where(first, a, b) - g), gl_a, gl_b, gci)
    return dict(incl=incl, strict=strict, gamma=gamma, kb=kb, m=m, tm=tm_, eg=eg, vb=vb, kbe=kbe,
                u=_each(lambda x: x[:, :LANES], uw), w=_each(lambda x: x[:, LANES:], uw), attn=attn,
                qd=_each(jnp.multiply, q, eg), ek=ek, kd=_each(jnp.multiply, k, ek),
                glast_a=_each(jnp.exp, gl_a), glast_b=_each(jnp.exp, gl_b))


def _gdn_specs(t, ts, order):
    nc = ts // CHUNK
    blk = pl.BlockSpec((ts, GDN_HP * LANES), lambda h, s: (order(s), h))
    row = pl.BlockSpec((GDN_HP, 1, ts), lambda h, s: (h, 0, order(s)))
    st = pl.BlockSpec((GDN_HP, nc, LANES, LANES), lambda h, s: (h, order(s), 0, 0))
    return blk, row, st


def _gdn_fwd(q, k, v, gcb, gct, bb, *, name):
    t = q.shape[0]
    ts = _tile(t, GDN_TILE)
    npair = ts // PAIR

    def body(q_ref, k_ref, v_ref, g_ref, gt_ref, b_ref, o_ref, st_ref, s_sc):
        @pl.when(pl.program_id(1) == 0)
        def _():
            s_sc[...] = jnp.zeros_like(s_sc)

        def pair(pi, _):
            rows = pl.ds(pl.multiple_of(pi * PAIR, PAIR), PAIR)
            heads = [slice(hh * LANES, (hh + 1) * LANES) for hh in range(GDN_HP)]
            c = CHUNK
            cat0 = lambda *xs: jnp.concatenate(xs, axis=0)
            s0 = [s_sc[hh] for hh in range(GDN_HP)]
            cm = _pair_common([q_ref[rows, sl] for sl in heads], [k_ref[rows, sl] for sl in heads],
                              [v_ref[rows, sl] for sl in heads], [g_ref[rows, sl] for sl in heads],
                              [gt_ref[hh, :, rows] for hh in range(GDN_HP)], [b_ref[rows, sl] for sl in heads])
            u, w, qd, kd = cm["u"], cm["w"], cm["qd"], cm["kd"]
            r0 = _each(lambda w_, q_, s: _bdot(cat0(w_[:c], q_[:c]), s), w, qd, s0)
            vn_a = _each(lambda u_, r: u_[:c] - r[:c], u, r0)
            s1 = _each(lambda s, gl, k_, vn: s * gl + _bdot(k_[:c], vn, TN), s0, cm["glast_a"], kd, vn_a)
            r1 = _each(lambda w_, q_, s: _bdot(cat0(w_[c:], q_[c:]), s), w, qd, s1)
            vn_b = _each(lambda u_, r: u_[c:] - r[:c], u, r1)
            s2 = _each(lambda s, gl, k_, vn: s * gl + _bdot(k_[c:], vn, TN), s1, cm["glast_b"], kd, vn_b)
            o = _each(lambda ra, rb, at, va, vb_: cat0(ra[c:], rb[c:]) + _bdot(at, cat0(va, vb_)),
                      r0, r1, cm["attn"], vn_a, vn_b)
            for hh, sl in enumerate(heads):
                st_ref[hh, 2 * pi] = s0[hh]
                st_ref[hh, 2 * pi + 1] = s1[hh]
                s_sc[hh] = s2[hh]
                o_ref[rows, sl] = o[hh]
            return 0

        lax.fori_loop(0, npair, pair, 0)

    blk, row, st = _gdn_specs(t, ts, lambda s: s)
    return pl.pallas_call(
        body, name=name, grid=(GDN_HEADS // GDN_HP, t // ts), in_specs=[blk, blk, blk, blk, row, blk],
        out_specs=[blk, st],
        out_shape=[jax.ShapeDtypeStruct((t, 1024), F32), jax.ShapeDtypeStruct((GDN_HEADS, t // CHUNK, LANES, LANES), F32)],
        scratch_shapes=[pltpu.VMEM((GDN_HP, LANES, LANES), F32)],
        compiler_params=_params(("parallel", "arbitrary")),
    )(q, k, v, gcb, gct, bb)


def _gdn_bwd(q, k, v, gcb, gct, bb, do, states, *, name):
    t = q.shape[0]
    ts = _tile(t, GDN_TILE)
    npair = ts // PAIR
    ns = t // ts
    c = CHUNK

    def body(q_ref, k_ref, v_ref, g_ref, gt_ref, b_ref, do_ref, st_ref, dq_ref, dk_ref, dv_ref, dg_ref, db_ref, ds_sc):
        @pl.when(pl.program_id(1) == 0)
        def _():
            ds_sc[...] = jnp.zeros_like(ds_sc)

        rowsum = lambda x: jnp.sum(x, axis=-1, keepdims=True)
        total = lambda x: jnp.sum(rowsum(x), axis=0, keepdims=True)
        cat0 = lambda *xs: jnp.concatenate(xs, axis=0)
        cat1 = lambda *xs: jnp.concatenate(xs, axis=1)

        def pair(step, _):
            pi = npair - 1 - step
            rows = pl.ds(pl.multiple_of(pi * PAIR, PAIR), PAIR)
            heads = [slice(hh * LANES, (hh + 1) * LANES) for hh in range(GDN_HP)]
            hs = range(GDN_HP)
            qv, kv, vv = ([r[rows, sl] for sl in heads] for r in (q_ref, k_ref, v_ref))
            beta = [b_ref[rows, sl] for sl in heads]
            dov = [do_ref[rows, sl] for sl in heads]
            s0 = [st_ref[hh, 2 * pi] for hh in hs]
            s1 = [st_ref[hh, 2 * pi + 1] for hh in hs]
            ds2 = [ds_sc[hh] for hh in hs]
            cm = _pair_common(qv, kv, vv, [g_ref[rows, sl] for sl in heads], [gt_ref[hh, :, rows] for hh in hs], beta)
            u, w, qd, kd, attn = cm["u"], cm["w"], cm["qd"], cm["kd"], cm["attn"]
            tmat, gamma, eg = cm["tm"], cm["gamma"], cm["eg"]
            incl, strict = cm["incl"], cm["strict"]
            vn_a = _each(lambda u_, w_, s: u_[:c] - _bdot(w_[:c], s), u, w, s0)
            vn_b = _each(lambda u_, w_, s: u_[c:] - _bdot(w_[c:], s), u, w, s1)
            vn = _each(cat0, vn_a, vn_b)
            dvn_att = _each(lambda a, d: _bdot(a, d, TN), attn, dov)
            dattn = _each(lambda d, v_: jnp.where(incl, _bdot(d, v_, NT), 0.0), dov, vn)
            dvn_b = _each(lambda x, k_, d: x[c:] + _bdot(k_[c:], d), dvn_att, kd, ds2)
            rb = _each(lambda d, x, s: _bdot(cat0(d[c:], x), s, NT), dov, dvn_b, s1)
            dkd_b = _each(lambda v_, d: _bdot(v_, d, NT), vn_b, ds2)
            dgl_b = _each(lambda d, s: total(d * s), ds2, s1)
            ds1 = _each(lambda d, gl, q_, w_, o_, x: d * gl + _bdot(cat0(q_[c:], w_[c:]), cat0(o_[c:], -x), TN),
                        ds2, cm["glast_b"], qd, w, dov, dvn_b)
            dvn_a = _each(lambda x, k_, d: x[:c] + _bdot(k_[:c], d), dvn_att, kd, ds1)
            ra = _each(lambda d, x, s: _bdot(cat0(d[:c], x), s, NT), dov, dvn_a, s0)
            dkd_a = _each(lambda v_, d: _bdot(v_, d, NT), vn_a, ds1)
            dgl_a = _each(lambda d, s: total(d * s), ds1, s0)
            ds0 = _each(lambda d, gl, q_, w_, o_, x: d * gl + _bdot(cat0(q_[:c], w_[:c]), cat0(o_[:c], -x), TN),
                        ds1, cm["glast_a"], qd, w, dov, dvn_a)
            dvn = _each(cat0, dvn_a, dvn_b)
            dqd = _each(lambda a, b: cat0(a[:c], b[:c]), ra, rb)
            dw = _each(lambda a, b: -cat0(a[c:], b[c:]), ra, rb)
            dkd = _each(cat0, dkd_a, dkd_b)
            dvw = _each(cat1, dvn, dw)
            dvbk = _each(lambda t_, x: _bdot(t_, x, TN), tmat, dvw)
            dvb = _each(lambda x: x[:, :LANES], dvbk)
            dkbe = _each(lambda x: x[:, LANES:], dvbk)
            dt_ = _each(lambda x, a, b: _bdot(x, cat1(a, b), NT), dvw, cm["vb"], cm["kbe"])
            da1 = _each(lambda t_, x: _bdot(t_, x, TN), tmat, dt_)
            dm = _each(lambda x, t_: jnp.where(strict, -_bdot(x, t_, NT), 0.0), da1, tmat)
            dkk = _each(jnp.multiply, dm, gamma)
            dqk = _each(jnp.multiply, dattn, gamma)
            z = _each(lambda a, b, c_, d: a * b + c_ * d, dm, cm["m"], dattn, attn)
            dkb = _each(lambda x, k_, y, e: _bdot(x, k_) + y * e, dkk, kv, dkbe, eg)
            dk = _each(lambda a, b, kb_, q_, x, e, y, be: _bdot(cat0(a, b), cat0(kb_, q_), TN) + x * e + y * be,
                       dkk, dqk, cm["kb"], qv, dkd, cm["ek"], dkb, beta)
            dq = _each(lambda x, k_, y, e: _bdot(x, k_) + y * e, dqk, kv, dqd, eg)

            def colsum_of(z_):
                zh = z_.astype(BF16)
                zl = (z_ - zh.astype(F32)).astype(BF16)
                return _dot(cat0(zh, zl), jnp.ones((2 * PAIR, LANES), BF16), TN)

            colsum = _each(colsum_of, z)
            ri = lax.broadcasted_iota(jnp.int32, (PAIR, LANES), 0)
            for hh, sl in enumerate(heads):
                dkd_kd = dkd[hh] * kd[hh]
                dgc = (rowsum(z[hh]) - colsum[hh] + rowsum(dqd[hh] * qd[hh]) - rowsum(dkd_kd)
                       + rowsum(dkbe[hh] * cm["kbe"][hh]))
                last_a = total(dkd_kd[:c]) + dgl_a[hh] * cm["glast_a"][hh]
                last_b = total(dkd_kd[c:]) + dgl_b[hh] * cm["glast_b"][hh]
                dgc = dgc + jnp.where(ri == c - 1, last_a, 0.0) + jnp.where(ri == PAIR - 1, last_b, 0.0)
                ds_sc[hh] = ds0[hh]
                dq_ref[rows, sl] = dq[hh]
                dk_ref[rows, sl] = dk[hh]
                dv_ref[rows, sl] = dvb[hh] * beta[hh]
                db_ref[rows, sl] = jnp.broadcast_to(rowsum(dkb[hh] * kv[hh]) + rowsum(dvb[hh] * vv[hh]), (PAIR, LANES))
                dg_ref[rows, sl] = dgc
            return 0

        lax.fori_loop(0, npair, pair, 0)

    blk, row, st = _gdn_specs(t, ts, lambda s: ns - 1 - s)
    out = jax.ShapeDtypeStruct((t, 1024), F32)
    return pl.pallas_call(
        body, name=name, grid=(GDN_HEADS // GDN_HP, ns), in_specs=[blk, blk, blk, blk, row, blk, blk, st],
        out_specs=[blk] * 5, out_shape=[out] * 5, scratch_shapes=[pltpu.VMEM((GDN_HP, LANES, LANES), F32)],
        compiler_params=_params(("parallel", "arbitrary")),
    )(q, k, v, gcb, gct, bb, do, states)


def _gate_out_proj_loss(o, proj_c, o_norm, w, hres, g, target, *, name):
    t, d = hres.shape
    tm = _tile(t, 2 * ROW_TILE)

    def body(o_ref, z_ref, on_ref, w_ref, h_ref, g_ref, t_ref, dh_ref, dhb_ref, y_ref, dg_ref, loss_ref):
        i = pl.program_id(0)
        for hd in range(GDN_HEADS):
            sl = slice(hd * LANES, (hd + 1) * LANES)
            ov = o_ref[:, sl]
            rr = lax.rsqrt(jnp.mean(ov * ov, axis=-1, keepdims=True) + RMS_EPS)
            z = z_ref[:, sl]
            y_ref[:, sl] = (ov * rr * on_ref[...] * (z * _sigmoid(z))).astype(BF16)
        x = h_ref[...] + _dot(y_ref[...], w_ref[...])
        r = lax.rsqrt(jnp.mean(x * x, axis=-1, keepdims=True) + RMS_EPS)
        xh = x * r
        err = xh * g_ref[...] - t_ref[...]
        dy = err * (1.0 / d)
        dxh = dy * g_ref[...]
        dh = r * (dxh - xh * jnp.mean(dxh * xh, axis=-1, keepdims=True))
        dh_ref[...] = dh
        dhb_ref[...] = dh.astype(BF16)

        @pl.when(i == 0)
        def _():
            dg_ref[...] = jnp.zeros_like(dg_ref)
            loss_ref[...] = jnp.zeros_like(loss_ref)

        dg_ref[...] += jnp.sum(dy * xh, axis=0, keepdims=True)
        part = 0.5 * jnp.sum(jnp.mean(err * err, axis=-1, keepdims=True), axis=0, keepdims=True)
        loss_ref[...] += jnp.broadcast_to(part, loss_ref.shape)

    row = pl.BlockSpec((tm, d), lambda i: (i, 0))
    vec = pl.BlockSpec((1, d), lambda i: (0, 0))
    return pl.pallas_call(
        body, name=name, grid=(t // tm,),
        in_specs=[row, pl.BlockSpec((tm, 1024), lambda i: (i, 3)), pl.BlockSpec((1, LANES), lambda i: (0, 0)),
                  pl.BlockSpec(w.shape, lambda i: (0, 0)), row, vec, row],
        out_specs=[row, row, row, vec, pl.BlockSpec((8, LANES), lambda i: (0, 0))],
        out_shape=[jax.ShapeDtypeStruct((t, d), F32), jax.ShapeDtypeStruct((t, d), BF16), jax.ShapeDtypeStruct((t, d), BF16),
                   jax.ShapeDtypeStruct((1, d), F32), jax.ShapeDtypeStruct((8, LANES), F32)],
        compiler_params=_params(("arbitrary",)),
    )(o, proj_c, o_norm, w, hres, g, target)


def _pad_cols(w, n):
    return jnp.pad(w, ((0, 0), (0, n - w.shape[1])))


def _layout_odd(w):
    return dict(
        winc=_pad_cols(w["w_in_c"], IN_C_PAD).astype(BF16), wout_c=w["w_out_c"].astype(BF16), conv_w=w["conv_w"],
        a_log=_pad_cols(w["a_log"], LANES), dt_bias=_pad_cols(w["dt_bias"], LANES),
        norm_c=w["norm_c"], o_norm=w["o_norm"], final_norm=w["final_norm"],
    )


def _layout_in_ab(w):
    z = lambda r, c: jnp.zeros((r, c), w["w_in_ab"].dtype)
    wi = w["w_in_ab"]
    win = jnp.concatenate([wi[:, :384], z(1024, 64), wi[:, 384:416], z(1024, 32), wi[:, 416:]], axis=1)
    pw = w["pool_w"]
    rows = []
    for g in range(4):
        rows.append(jnp.concatenate([pw[g] if j == g else jnp.zeros((128, 128), F32) for j in range(4)], axis=1))
    wpool = jnp.concatenate(rows, axis=0)
    half = MLA_ROPE // 2
    inv = 1.0 / (ROPE_THETA ** (jnp.arange(half, dtype=F32) / half))
    inv_lane = jnp.concatenate([jnp.zeros((MLA_NOPE,), F32), inv, inv, jnp.zeros((32,), F32)]).reshape(1, LANES)
    return dict(win=win.astype(BF16), wpool=wpool.astype(BF16), inv_lane=inv_lane, norm_ab=w["norm_ab"],
                q_a_norm=w["q_a_norm"], kv_a_norm=w["kv_a_norm"], pool_scale=w["pool_scale"])


def _layout_mid(w):
    wq = jnp.pad(w["w_q_b"].reshape(MLA_Q_RANK, MLA_HEADS, 96), ((0, 0), (0, 0), (0, 32))).reshape(MLA_Q_RANK, 1024)
    kv3 = w["w_kv_b"].reshape(MLA_KV_RANK, MLA_HEADS, 128)
    wk = jnp.pad(kv3[..., :MLA_NOPE], ((0, 0), (0, 0), (0, 64))).reshape(MLA_KV_RANK, 1024)
    wv = kv3[..., MLA_NOPE:].reshape(MLA_KV_RANK, 512)
    return dict(wq=wq.astype(BF16), wk=wk.astype(BF16), wv=wv.astype(BF16), wout_ab=w["w_out_ab"].astype(BF16))


def _unlayout_grads(g, names):
    out = {}
    for name in names:
        if name == "w_in_ab":
            dwin = g["win"]
            out[name] = jnp.concatenate([dwin[:, :384], dwin[:, 448:480], dwin[:, 512:]], axis=1)
        elif name == "w_q_b":
            out[name] = g["wq"].reshape(MLA_Q_RANK, MLA_HEADS, 128)[..., :96].reshape(MLA_Q_RANK, 768)
        elif name == "w_kv_b":
            out[name] = jnp.concatenate([g["wk"].reshape(MLA_KV_RANK, MLA_HEADS, 128)[..., :MLA_NOPE],
                                         g["wv"].reshape(MLA_KV_RANK, MLA_HEADS, MLA_V)], axis=-1).reshape(MLA_KV_RANK, 1024)
        elif name == "w_in_c":
            out[name] = g["winc"][:, :4112]
        else:
            out[name] = g[{"w_out_ab": "wout_ab", "w_out_c": "wout_c"}[name]]
    return out


def _local_step(x, pos, target, lw, more_weights=None, on_grads=None):
    mm = _matmul
    proj, hn = _rms_in_proj(x, lw["norm_ab"], lw["win"], name="rms_in_ab")
    if more_weights is not None:
        lw = {**lw, **more_weights("mid", proj)}
    q, k, v, ybraw, qn, kvn, d, cos_t, sin_t = _ab_prep(
        proj, pos, lw["inv_lane"], lw["q_a_norm"], lw["kv_a_norm"], lw["wq"], lw["wk"], lw["wv"], lw["wpool"], name="ab_prep")
    o, lse = _attn_fwd(q, k, v, name="attn_fwd")
    h1, y = _gate_out_proj(o, ybraw, proj, lw["pool_scale"], lw["wout_ab"], x, name="gate_out_ab")
    lo = lw if more_weights is None else more_weights("odd", h1)
    proj_c, hn1 = _rms_in_proj(h1, lo["norm_c"], lo["winc"], name="rms_in_c")
    q2, k2, v2, gb, bb, gt = _c_prep(proj_c, lo["conv_w"], lo["a_log"], lo["dt_bias"], name="c_prep")
    gt = gt.reshape(GDN_HEADS, 1, gt.shape[1])
    o2, states = _gdn_fwd(q2, k2, v2, gb, gt, bb, name="gdn_fwd")
    dh2, dh2b, y2, d_final, loss = _gate_out_proj_loss(
        o2, proj_c, lo["o_norm"], lo["wout_c"], h1, lo["final_norm"], target, name="gate_out_c_loss")
    g = {"final_norm": d_final}
    dy2 = mm(dh2b, lo["wout_c"], "nt", name="out_c_dx")
    g["wout_c"] = mm(y2, dh2b, "tn", name="out_c_dw")
    do2, dz2, g["o_norm"] = _o_gate_bwd(dy2, o2, proj_c, lo["o_norm"], name="gate_c_bwd")
    dq2, dk2, dv2, dgb, dbb = _gdn_bwd(q2, k2, v2, gb, gt, bb, do2, states, name="gdn_bwd")
    dproj_c, g["conv_w"], g["a_log"], g["dt_bias"] = _c_prep_bwd(
        proj_c, lo["conv_w"], lo["a_log"], lo["dt_bias"], dq2, dk2, dv2, dgb, dbb, dz2, name="c_prep_bwd")
    g["winc"] = mm(hn1, dproj_c, "tn", name="in_c_dw")
    notify = (lambda tag: 0.0) if on_grads is None else (lambda tag: on_grads(tag, g))
    pool_scale = lw["pool_scale"] + notify("odd")
    dh1, dh1b, g["norm_c"] = _matmul_rms_bwd(dproj_c, lo["winc"], h1, lo["norm_c"], dh2, name="in_c_dx_rms", with_bf16=True)
    dy = mm(dh1b, lw["wout_ab"], "nt", name="out_ab_dx")
    g["wout_ab"] = mm(y, dh1b, "tn", name="out_ab_dw")
    pool_scale = pool_scale + notify("out_ab")
    do, delta, dyb, dz, g["pool_scale"] = _gate_bwd(dy, o, ybraw, proj, pool_scale, name="gate_ab_bwd")
    dq, dk, dv = _attn_bwd(q, k, v, do, lse, delta, name="attn_bwd")
    dproj, dqraw, dkb, g["q_a_norm"], g["kv_a_norm"] = _ab_prep_bwd(
        proj, lw["q_a_norm"], lw["kv_a_norm"], dq, dk, dv, cos_t, sin_t, dyb, dz,
        lw["wq"], lw["wk"], lw["wv"], lw["wpool"], name="ab_prep_bwd")
    g["wpool"] = mm(d, dyb, "tn", name="pool_mix_dw")
    g["wq"] = mm(qn, dqraw, "tn", name="q_up_dw")
    g["wk"] = mm(kvn, dkb, "tn", name="k_up_dw")
    g["wv"] = mm(kvn, dv, "tn", name="v_up_dw")
    g["win"] = mm(hn, dproj, "tn", name="in_ab_dw")
    norm_ab = lw["norm_ab"] + notify("in_ab")
    dx, g["norm_ab"] = _matmul_rms_bwd(dproj, lw["win"], x, norm_ab, dh1, name="in_ab_dx_rms", with_bf16=False)
    return loss, dx, g


_HBM = pl.BlockSpec(memory_space=pltpu.HBM)


def _place():
    return lax.axis_index("x"), lax.axis_index("y"), lax.axis_index("c")


def _flip(v, f):
    return 1 - v if f else v


_CHIP_FLIPS = ((1, 0), (0, 1), (1, 1))
_DEV_FLIPS = tuple((fx, fy, fc) for fx in (0, 1) for fy in (0, 1) for fc in (0, 1) if fx or fy or fc)


def _rcopy(src, dst, send_sems, recv_sems, k, to):
    return pltpu.make_async_remote_copy(src_ref=src, dst_ref=dst, send_sem=send_sems.at[k], recv_sem=recv_sems.at[k],
                                        device_id=to, device_id_type=MESH)


def _my_half(ref, c, axis):
    rh = ref.shape[axis] // 2
    idx = [slice(None)] * len(ref.shape)
    idx[axis] = pl.ds(c * rh, rh)
    return ref.at[tuple(idx)]


def _gather_weights(bigs, smalls):
    nb, ns = len(bigs), len(smalls)

    def body(*refs):
        ins, outs = refs[:nb + ns], refs[nb + ns:2 * (nb + ns)]
        send_sems, recv_sems, local_sems = refs[2 * (nb + ns):]
        x, y, c = _place()
        j0 = 2 * x + y
        sib = (x, y, 1 - c)
        chips = [(_flip(x, fx), _flip(y, fy)) for fx, fy in _CHIP_FLIPS]
        local = [pltpu.make_async_copy(i_ref, o_ref.at[j0], local_sems.at[a])
                 for a, (i_ref, o_ref) in enumerate(zip(ins, outs))]
        for cp in local:
            cp.start()
        sends = []
        for k, (px, py) in enumerate(chips):
            for a in range(nb):
                sends.append(_rcopy(_my_half(ins[a], c, 0), _my_half(outs[a].at[j0], c, 0), send_sems, recv_sems,
                                    6 * a + k, (px, py, c)))
            for s in range(ns):
                sends.append(_rcopy(ins[nb + s], outs[nb + s].at[j0], send_sems, recv_sems, 6 * nb + 3 * s + k, (px, py, c)))
        for cp in sends:
            cp.start()
        for k, (px, py) in enumerate(chips):
            jk = 2 * px + py
            for a in range(nb):
                landed = _my_half(outs[a].at[jk], c, 0)
                _rcopy(landed, landed, send_sems, recv_sems, 6 * a + k, (px, py, c)).wait_recv()
                fwd = _rcopy(landed, landed, send_sems, recv_sems, 6 * a + 3 + k, sib)
                fwd.start()
                sends.append(fwd)
        for k, (px, py) in enumerate(chips):
            jk = 2 * px + py
            for a in range(nb):
                other = _my_half(outs[a].at[jk], 1 - c, 0)
                _rcopy(other, other, send_sems, recv_sems, 6 * a + 3 + k, sib).wait_recv()
            for s in range(ns):
                _rcopy(ins[nb + s], outs[nb + s].at[jk], send_sems, recv_sems, 6 * nb + 3 * s + k, (px, py, c)).wait_recv()
        for cp in sends:
            cp.wait_send()
        for cp in local:
            cp.wait()

    arrays = list(bigs) + list(smalls)
    n_sem = 6 * nb + 3 * ns
    return pl.pallas_call(
        body, name="gather_weights", in_specs=[_HBM] * len(arrays), out_specs=[_HBM] * len(arrays),
        out_shape=[jax.ShapeDtypeStruct((4,) + a.shape, a.dtype) for a in arrays],
        scratch_shapes=[pltpu.SemaphoreType.DMA((n_sem,)), pltpu.SemaphoreType.DMA((n_sem,)),
                        pltpu.SemaphoreType.DMA((len(arrays),))],
    )(*arrays)


def _core_swap_partial(gs, *, name):
    n = len(gs)

    def body(*refs):
        ins, outs = refs[:n], refs[n:2 * n]
        send_sems, recv_sems = refs[2 * n:]
        x, y, c = _place()
        copies = [_rcopy(_my_half(i_ref, 1 - c, 1), o_ref, send_sems, recv_sems, a, (x, y, 1 - c))
                  for a, (i_ref, o_ref) in enumerate(zip(ins, outs))]
        for cp in copies:
            cp.start()
        for cp in copies:
            cp.wait()

    return pl.pallas_call(
        body, name=name, in_specs=[_HBM] * n, out_specs=[_HBM] * n,
        out_shape=[jax.ShapeDtypeStruct((4, g.shape[1] // 2, g.shape[2]), g.dtype) for g in gs],
        scratch_shapes=[pltpu.SemaphoreType.DMA((n,)), pltpu.SemaphoreType.DMA((n,))],
    )(*gs)


def _core_swap_sum(fs):
    n = len(fs)

    def body(*refs):
        ins, outs = refs[:n], refs[n:2 * n]
        send_sems, recv_sems = refs[2 * n:]
        x, y, c = _place()
        copies = [_rcopy(_my_half(i_ref, c, 0), _my_half(o_ref, c, 0), send_sems, recv_sems, a, (x, y, 1 - c))
                  for a, (i_ref, o_ref) in enumerate(zip(ins, outs))]
        for cp in copies:
            cp.start()
        for a, cp in enumerate(copies):
            cp.wait_send()
            theirs = _my_half(outs[a], 1 - c, 0)
            _rcopy(theirs, theirs, send_sems, recv_sems, a, (x, y, 1 - c)).wait_recv()

    return pl.pallas_call(
        body, name="core_swap_sum", in_specs=[_HBM] * n, out_specs=[_HBM] * n,
        out_shape=[jax.ShapeDtypeStruct(f.shape, f.dtype) for f in fs],
        input_output_aliases={a: a for a in range(n)},
        scratch_shapes=[pltpu.SemaphoreType.DMA((n,)), pltpu.SemaphoreType.DMA((n,))],
    )(*fs)


_SEM = pl.BlockSpec(memory_space=pltpu.SEMAPHORE)
_ANY = pl.BlockSpec(memory_space=pl.ANY)
_DATAFLOW = pltpu.SideEffectType.DATAFLOW_SIDE_EFFECTING


def _to_chips_copies(srcs, lands, send_sems, recv_sems, per_chip_slot):
    x, y, c = _place()
    j0 = 2 * x + y
    out = []
    for k, (fx, fy) in enumerate(_CHIP_FLIPS):
        px, py = _flip(x, fx), _flip(y, fy)
        jk = 2 * px + py
        for a, (src, land) in enumerate(zip(srcs, lands)):
            piece = src.at[jk] if per_chip_slot else src
            out.append((_rcopy(piece, land.at[j0], send_sems, recv_sems, 3 * a + k, (px, py, c)),
                        _rcopy(piece, land.at[jk], send_sems, recv_sems, 3 * a + k, (px, py, c))))
    return out


def _to_chips_start(arrays, *, per_chip_slot, name):
    n = len(arrays)
    lands = [lax.empty((4,) + (a.shape[1:] if per_chip_slot else a.shape), a.dtype) for a in arrays]

    def body(*refs):
        srcs, land_refs, send_sems, recv_sems, token = refs[:n], refs[n:2 * n], refs[2 * n], refs[2 * n + 1], refs[-1]
        for send, _ in _to_chips_copies(srcs, land_refs, send_sems, recv_sems, per_chip_slot):
            send.start()
        token[...] = jnp.zeros_like(token)

    held = [pltpu.with_memory_space_constraint(a, pltpu.HBM) for a in list(arrays) + lands]
    return pl.pallas_call(
        body, name=name, in_specs=[_HBM] * (2 * n),
        out_specs=(_SEM, _SEM, *[_HBM] * (2 * n), pl.BlockSpec(memory_space=pltpu.VMEM)),
        out_shape=(pltpu.SemaphoreType.DMA((3 * n,)), pltpu.SemaphoreType.DMA((3 * n,)),
                   *[pltpu.HBM(a.shape, a.dtype) for a in held], jax.ShapeDtypeStruct((8, LANES), F32)),
        input_output_aliases={i: 2 + i for i in range(2 * n)},
        compiler_params=pltpu.CompilerParams(has_side_effects=_DATAFLOW),
    )(*held)


def _to_chips_wait(started, after, *, per_chip_slot, name):
    send_sems, recv_sems, held = started[0], started[1], started[2:-1]
    n = len(held) // 2

    def body(*refs):
        srcs, land_refs, s_sems, r_sems = refs[:n], refs[n:2 * n], refs[2 * n], refs[2 * n + 1]
        for send, arrival in _to_chips_copies(srcs, land_refs, s_sems, r_sems, per_chip_slot):
            send.wait_send()
            arrival.wait_recv()

    out = pl.pallas_call(
        body, name=name, in_specs=[_HBM] * (2 * n) + [_SEM, _SEM, _ANY], out_specs=[_HBM] * (2 * n),
        out_shape=[pltpu.HBM(a.shape, a.dtype) for a in held],
        input_output_aliases={i: i for i in range(2 * n)},
        compiler_params=pltpu.CompilerParams(has_side_effects=_DATAFLOW),
    )(*held, send_sems, recv_sems, after)
    return out[n:]


def _chip_exchange(ps, small):
    n = len(ps)
    rs = small.shape[0]

    def body(*refs):
        p_refs, s_ref = refs[:n], refs[n]
        l_refs, ls_ref = refs[n + 1:2 * n + 1], refs[2 * n + 1]
        send_sems, recv_sems, local_sems = refs[2 * n + 2:]
        x, y, c = _place()
        j0 = 2 * x + y
        d0 = 2 * j0 + c
        local = [pltpu.make_async_copy(p.at[j0], l.at[j0], local_sems.at[a]) for a, (p, l) in enumerate(zip(p_refs, l_refs))]
        local.append(pltpu.make_async_copy(s_ref, ls_ref.at[d0], local_sems.at[n]))
        for cp in local:
            cp.start()
        sends = []
        for k, (fx, fy) in enumerate(_CHIP_FLIPS):
            px, py = _flip(x, fx), _flip(y, fy)
            for a in range(n):
                sends.append(_rcopy(p_refs[a].at[2 * px + py], l_refs[a].at[j0], send_sems, recv_sems, 3 * a + k, (px, py, c)))
        for k, (fx, fy, fc) in enumerate(_DEV_FLIPS):
            peer = (_flip(x, fx), _flip(y, fy), _flip(c, fc))
            sends.append(_rcopy(s_ref, ls_ref.at[d0], send_sems, recv_sems, 3 * n + k, peer))
        for cp in sends:
            cp.start()
        for k, (fx, fy) in enumerate(_CHIP_FLIPS):
            px, py = _flip(x, fx), _flip(y, fy)
            for a in range(n):
                _rcopy(p_refs[a].at[j0], l_refs[a].at[2 * px + py], send_sems, recv_sems, 3 * a + k, (px, py, c)).wait_recv()
        for k, (fx, fy, fc) in enumerate(_DEV_FLIPS):
            px, py, pc = _flip(x, fx), _flip(y, fy), _flip(c, fc)
            _rcopy(s_ref, ls_ref.at[4 * px + 2 * py + pc], send_sems, recv_sems, 3 * n + k, (px, py, pc)).wait_recv()
        for cp in sends:
            cp.wait_send()
        for cp in local:
            cp.wait()

    n_sem = 3 * n + 7
    return pl.pallas_call(
        body, name="chip_exchange", in_specs=[_HBM] * (n + 1), out_specs=[_HBM] * (n + 1),
        out_shape=[jax.ShapeDtypeStruct(p.shape, F32) for p in ps] + [jax.ShapeDtypeStruct((8, rs, LANES), F32)],
        scratch_shapes=[pltpu.SemaphoreType.DMA((n_sem,)), pltpu.SemaphoreType.DMA((n_sem,)),
                        pltpu.SemaphoreType.DMA((n + 1,))],
    )(*ps, small)


def _core_sum(g, part, core, *, name):
    _, rh, cols = part.shape
    tr = _tile(rh, 256)
    nb = rh // tr

    def body(c_ref, g_ref, p_ref, o_ref):
        o_ref[...] = g_ref[...] + p_ref[...]

    grid_spec = pltpu.PrefetchScalarGridSpec(
        num_scalar_prefetch=1, grid=(4, nb),
        in_specs=[pl.BlockSpec((1, tr, cols), lambda j, i, c: (j, c[0] * nb + i, 0)),
                  pl.BlockSpec((1, tr, cols), lambda j, i, c: (j, i, 0))],
        out_specs=pl.BlockSpec((1, tr, cols), lambda j, i, c: (j, i, 0)),
    )
    return pl.pallas_call(
        body, name=name, grid_spec=grid_spec, out_shape=jax.ShapeDtypeStruct(part.shape, F32),
        compiler_params=_params(("parallel", "parallel")),
    )(core, g, part)


def _chip_sum(landed, core, *, name):
    _, rh, cols = landed.shape
    tr = _tile(rh, 256)
    nb = rh // tr

    def body(c_ref, l_ref, o_ref):
        o_ref[...] = ((l_ref[0] + l_ref[1]) + l_ref[2]) + l_ref[3]

    grid_spec = pltpu.PrefetchScalarGridSpec(
        num_scalar_prefetch=1, grid=(nb,),
        in_specs=[pl.BlockSpec((4, tr, cols), lambda i, c: (0, i, 0))],
        out_specs=pl.BlockSpec((tr, cols), lambda i, c: (c[0] * nb + i, 0)),
    )
    return pl.pallas_call(
        body, name=name, grid_spec=grid_spec, out_shape=jax.ShapeDtypeStruct((2 * rh, cols), F32),
        compiler_params=_params(("parallel",)),
    )(core, landed)


_ROW_POOL_W, _ROW_NORM_AB, _ROW_FINAL, _ROW_POOL_SCALE, _ROW_Q_NORM = 0, 512, 520, 528, 532
_ROW_KV_NORM, _ROW_O_NORM, _ROW_A_LOG, _ROW_DT_BIAS, _ROW_LOSS = 534, 535, 536, 537, 538
_ROW_CONV, _ROW_NORM_C, _SMALL_ROWS = 544, 640, 672
_CONV_ROWS = CONV_WIDTH * 6


def _put_rows(dst_ref, row0, src, width):
    for r in range(width // LANES):
        dst_ref[row0 + r:row0 + r + 1, :] = src[:, r * LANES:(r + 1) * LANES]


def _pack_small(g, loss_tile):
    names = ("wpool", "norm_ab", "final_norm", "pool_scale", "q_a_norm", "kv_a_norm", "o_norm", "a_log", "dt_bias",
             "conv_w", "norm_c")

    def body(wpool, norm_ab, final_norm, pool_scale, q_norm, kv_norm, o_norm, a_log, dt_bias, conv_w, norm_c, loss, o_ref):
        o_ref[...] = jnp.zeros_like(o_ref)
        for gi in range(4):
            o_ref[_ROW_POOL_W + gi * 128:_ROW_POOL_W + (gi + 1) * 128, :] = wpool[gi * 128:(gi + 1) * 128, gi * 128:(gi + 1) * 128]
        _put_rows(o_ref, _ROW_NORM_AB, norm_ab[...], 1024)
        _put_rows(o_ref, _ROW_FINAL, final_norm[...], 1024)
        _put_rows(o_ref, _ROW_POOL_SCALE, pool_scale[...], 512)
        _put_rows(o_ref, _ROW_Q_NORM, q_norm[...], 256)
        for row, ref in ((_ROW_KV_NORM, kv_norm), (_ROW_O_NORM, o_norm), (_ROW_A_LOG, a_log), (_ROW_DT_BIAS, dt_bias)):
            o_ref[row:row + 1, :] = ref[...]
        o_ref[_ROW_LOSS:_ROW_LOSS + 1, :] = loss[0:1, :]
        for j in range(4):
            for r in range(CONV_WIDTH):
                _put_rows(o_ref, _ROW_CONV + j * _CONV_ROWS + r * 6, conv_w[r:r + 1, j * 768:(j + 1) * 768], 768)
            _put_rows(o_ref, _ROW_NORM_C + j * 8, norm_c[:, j * 256:(j + 1) * 256], 256)

    vmem = pl.BlockSpec(memory_space=pltpu.VMEM)
    return pl.pallas_call(
        body, name="pack_small", in_specs=[vmem] * 12, out_specs=vmem,
        out_shape=jax.ShapeDtypeStruct((_SMALL_ROWS, LANES), F32),
    )(*[g[n] for n in names], loss_tile)


_SMALL_NAMES = ("pool_w", "norm_ab", "final_norm", "pool_scale", "q_a_norm", "kv_a_norm", "o_norm", "a_log", "dt_bias",
                "conv_w", "norm_c")


def _take_rows(src, row0, width):
    return jnp.concatenate([src[row0 + r:row0 + r + 1, :] for r in range(width // LANES)], axis=1)


def _small_update(small_all, ws, ms, vs):
    n = len(_SMALL_NAMES)

    def body(*refs):
        a_ref = refs[0]
        w_refs, m_refs, v_refs = refs[1:1 + n], refs[1 + n:1 + 2 * n], refs[1 + 2 * n:1 + 3 * n]
        outs = refs[1 + 3 * n:1 + 7 * n]
        loss_ref, tot = refs[1 + 7 * n], refs[2 + 7 * n]
        acc = a_ref[0]
        for d in range(1, 8):
            acc = acc + a_ref[d]
        tot[...] = acc
        x, y, _ = _place()
        j0 = 2 * x + y
        conv = tot[pl.ds(pl.multiple_of(_ROW_CONV + j0 * _CONV_ROWS, 8), _CONV_ROWS), :]
        norm_c = tot[pl.ds(pl.multiple_of(_ROW_NORM_C + j0 * 8, 8), 8), :]
        whole = tot[_ROW_NORM_AB:_ROW_CONV, :]
        at = lambda row: row - _ROW_NORM_AB
        grads = {
            "norm_ab": _take_rows(whole, at(_ROW_NORM_AB), 1024), "final_norm": _take_rows(whole, at(_ROW_FINAL), 1024),
            "pool_scale": _take_rows(whole, at(_ROW_POOL_SCALE), 512), "q_a_norm": _take_rows(whole, at(_ROW_Q_NORM), 256),
            "kv_a_norm": whole[at(_ROW_KV_NORM):at(_ROW_KV_NORM) + 1, :], "o_norm": whole[at(_ROW_O_NORM):at(_ROW_O_NORM) + 1, :],
            "a_log": tot[_ROW_A_LOG:_ROW_A_LOG + 1, 0:GDN_HEADS],
            "dt_bias": tot[_ROW_DT_BIAS:_ROW_DT_BIAS + 1, 0:GDN_HEADS],
            "norm_c": _take_rows(norm_c, 0, 256),
        }
        loss_ref[...] = whole[at(_ROW_LOSS):at(_ROW_LOSS) + 1, :]
        for i, name in enumerate(_SMALL_NAMES):
            g_out = outs[4 * i]
            if name == "pool_w":
                for gi in range(4):
                    g_out[gi] = tot[_ROW_POOL_W + gi * 128:_ROW_POOL_W + (gi + 1) * 128, :]
            elif name == "conv_w":
                for r in range(CONV_WIDTH):
                    g_out[r:r + 1, :] = _take_rows(conv, r * 6, 768)
            else:
                g_out[...] = grads[name]
            _adam_update(g_out, w_refs[i], m_refs[i], v_refs[i], *outs[4 * i + 1:4 * i + 4])

    vmem = pl.BlockSpec(memory_space=pltpu.VMEM)
    out_shape = [jax.ShapeDtypeStruct(w.shape, F32) for w in ws for _ in range(4)] + [jax.ShapeDtypeStruct((1, LANES), F32)]
    return pl.pallas_call(
        body, name="small_update", in_specs=[vmem] * (1 + 3 * n), out_specs=[vmem] * (4 * n + 1), out_shape=out_shape,
        scratch_shapes=[pltpu.VMEM((_SMALL_ROWS, LANES), F32)],
        compiler_params=pltpu.CompilerParams(vmem_limit_bytes=VMEM_LIMIT),
    )(small_all, *ws, *ms, *vs)


def _adam_update(g_ref, w_ref, m_ref, v_ref, d_ref, mo_ref, vo_ref):
    gv = g_ref[...]
    mn = ADAM_B1 * m_ref[...] + (1.0 - ADAM_B1) * gv
    vn = ADAM_B2 * v_ref[...] + (1.0 - ADAM_B2) * (gv * gv)
    mo_ref[...] = mn
    vo_ref[...] = vn
    c1 = 1.0 - ADAM_B1 ** ADAM_STEP
    c2 = 1.0 - ADAM_B2 ** ADAM_STEP
    d_ref[...] = -ADAM_LR * ((mn / c1) / (jnp.sqrt(vn / c2) + ADAM_EPS) + ADAM_WD * w_ref[...])


def _adamw_rows(g, w, m, v, *, name):
    rows, cols = g.shape
    if rows % LANES == 0:
        tr = _tile(rows, 512)
        blk, steps = pl.BlockSpec((tr, cols), lambda i: (i, 0)), rows // tr
    else:
        tc = _tile(cols, 256)
        blk, steps = pl.BlockSpec((rows, tc), lambda i: (0, i)), cols // tc

    def body(*refs):
        _adam_update(*refs)

    out = jax.ShapeDtypeStruct((rows, cols), F32)
    return pl.pallas_call(
        body, name=name, grid=(steps,), in_specs=[blk] * 4, out_specs=[blk] * 3, out_shape=[out] * 3,
        compiler_params=_params(("parallel",)),
    )(g, w, m, v)


_ADAM_ROWWISE = ("w_in_ab", "w_q_b", "w_kv_b", "w_out_ab", "w_in_c", "w_out_c")


_SHARD_AXIS = {"w_in_ab": 1, "w_q_b": 1, "w_kv_b": 1, "w_out_ab": 0, "w_in_c": 1, "w_out_c": 0, "conv_w": 1, "norm_c": 1}
_ALL_NAMES = ("norm_ab", "w_in_ab", "q_a_norm", "w_q_b", "kv_a_norm", "w_kv_b", "pool_w", "pool_scale", "w_out_ab",
              "norm_c", "w_in_c", "conv_w", "a_log", "dt_bias", "o_norm", "w_out_c", "final_norm")


def _join_shards(a, axis):
    _, r, c = a.shape
    return a.reshape(4 * r, c) if axis == 0 else jnp.transpose(a, (1, 0, 2)).reshape(r, 4 * c)


def _split_shards(a, axis):
    r, c = a.shape
    return a.reshape(4, r // 4, c) if axis == 0 else jnp.transpose(a.reshape(r, 4, c // 4), (1, 0, 2))


def kernel(x, positions, norm_ab, w_in_ab, q_a_norm, w_q_b, kv_a_norm, w_kv_b, pool_w, pool_scale, w_out_ab, norm_c, w_in_c, conv_w, a_log, dt_bias, o_norm, w_out_c, final_norm, loss_target, m_norm_ab, m_w_in_ab, m_q_a_norm, m_w_q_b, m_kv_a_norm, m_w_kv_b, m_pool_w, m_pool_scale, m_w_out_ab, m_norm_c, m_w_in_c, m_conv_w, m_a_log, m_dt_bias, m_o_norm, m_w_out_c, m_final_norm, v_norm_ab, v_w_in_ab, v_q_a_norm, v_w_q_b, v_kv_a_norm, v_w_kv_b, v_pool_w, v_pool_scale, v_w_out_ab, v_norm_c, v_w_in_c, v_conv_w, v_a_log, v_dt_bias, v_o_norm, v_w_out_c, v_final_norm):
    given = dict(locals())
    c = lax.axis_index("c")
    t = x.shape[1]

    def shard_of(prefix, name):
        a = given[prefix + name]
        return a.reshape(a.shape[1:]) if a.ndim > 2 else a.reshape(1, -1)

    big, big_even, big_odd, small_sharded = _ADAM_ROWWISE, _ADAM_ROWWISE[:4], _ADAM_ROWWISE[4:], ("conv_w", "norm_c")
    chip = 2 * lax.axis_index("x") + lax.axis_index("y")
    core = c.astype(jnp.int32).reshape(1)
    later = {"mid": big_even[1:], "odd": big_odd + small_sharded}
    travelling = {}
    for tag, names in later.items():
        shards = [shard_of("", n).astype(BF16) if n in big else shard_of("", n) for n in names]
        travelling[tag] = (shards, _to_chips_start(shards, per_chip_slot=False, name="gather_" + tag + "_start"))
    gathered = _gather_weights([shard_of("", "w_in_ab").astype(BF16)], [])
    full = {"w_in_ab": _join_shards(gathered[0], _SHARD_AXIS["w_in_ab"])}
    for name in ("norm_ab", "q_a_norm", "kv_a_norm", "pool_w", "pool_scale"):
        full[name] = shard_of("", name)
    lw = _layout_in_ab(full)
    lw["norm_ab"] = lw["norm_ab"] + travelling["mid"][1][-1][0, 0] + travelling["odd"][1][-1][0, 0]

    def more_weights(tag, after):
        shards, started = travelling[tag]
        landed = _to_chips_wait(started, after, per_chip_slot=False, name="gather_" + tag + "_wait")
        w = {}
        for name, land, own in zip(later[tag], landed, shards):
            w[name] = _join_shards(lax.dynamic_update_index_in_dim(land, own, chip, 0), _SHARD_AXIS[name])
        if tag == "mid":
            return _layout_mid(w)
        for name in ("a_log", "dt_bias", "o_norm", "final_norm"):
            w[name] = shard_of("", name)
        return _layout_odd(w)

    def chip_partials(names, grads, tag):
        slots = [_split_shards(grads[n], _SHARD_AXIS[n]) for n in names]
        partial = _core_swap_partial(slots, name="core_swap_partial_" + tag)
        return [_core_sum(s, p, core, name="core_sum_" + n) for n, s, p in zip(names, slots, partial)]

    groups = {"odd": big_odd, "out_ab": ("w_out_ab",), "in_ab": ("w_in_ab", "w_q_b", "w_kv_b")}
    sent = {}

    def on_grads(tag, g):
        part = chip_partials(groups[tag], _unlayout_grads(g, groups[tag]), tag)
        sent[tag] = (part, _to_chips_start(part, per_chip_slot=True, name="exchange_" + tag + "_start"))
        return sent[tag][1][-1][0, 0]

    loss_tile, dx, g = _local_step(x[0], positions.reshape(t, 1), loss_target[0], lw, more_weights, on_grads)
    small_all = _chip_exchange([], _pack_small(g, loss_tile))[-1]
    halves = {}
    for tag, names in groups.items():
        part, started = sent[tag]
        landed = _to_chips_wait(started, small_all, per_chip_slot=True, name="exchange_" + tag + "_wait")
        for n, l, p in zip(names, landed, part):
            l = lax.dynamic_update_index_in_dim(l, lax.dynamic_index_in_dim(p, chip, 0, keepdims=False), chip, 0)
            halves[n] = _chip_sum(l, core, name="chip_sum_" + n)
    gbig = dict(zip(big, _core_swap_sum([halves[n] for n in big])))

    res = {}
    for name in big:
        res["grad", name] = gbig[name]
        operands = [gbig[name], shard_of("", name), shard_of("m_", name), shard_of("v_", name)]
        flip = operands[0].shape[1] % LANES != 0
        if flip:
            operands = [jnp.transpose(a) for a in operands]
        out = _adamw_rows(*operands, name="adamw_" + name)
        res["delta", name], res["m", name], res["v", name] = [jnp.transpose(a) for a in out] if flip else out
    out = _small_update(small_all, [shard_of("", n) for n in _SMALL_NAMES], [shard_of("m_", n) for n in _SMALL_NAMES],
                        [shard_of("v_", n) for n in _SMALL_NAMES])
    for i, name in enumerate(_SMALL_NAMES):
        res["grad", name], res["delta", name], res["m", name], res["v", name] = out[4 * i:4 * i + 4]
    res = {k: a.reshape(given[k[1]].shape) for k, a in res.items()}
    loss = out[-1][0, 0]
    outs = [loss, dx.reshape(x.shape)]
    for key in ("grad", "delta", "m", "v"):
        outs += [res[key, n] for n in _ALL_NAMES]
    return tuple(outs)
```

```python
import functools

import jax
import jax.numpy as jnp
from jax import lax
from jax.experimental import pallas as pl
from jax.experimental.pallas import tpu as pltpu

F32 = jnp.float32
BF16 = jnp.bfloat16
HI = lax.Precision.HIGHEST
MESH = pl.DeviceIdType.MESH

RMS_EPS = 1e-6
MLA_HEADS = 8
MLA_Q_RANK = 256
MLA_KV_RANK = 128
MLA_NOPE = 64
MLA_ROPE = 32
MLA_V = 64
ROPE_THETA = 10000.0
POOL_WINDOWS = (2, 4, 8, 16)
POOL_GROUP = 128
POOL_WIDTH = 512
POOL_HALO = 16
GDN_HEADS = 8
GDN_DK = 128
CONV_WIDTH = 4
CONV_HALO = 8
CHUNK = 64
IN_AB_PAD = 2048
IN_C_PAD = 4224
ATT_SCALE = (MLA_NOPE + MLA_ROPE) ** -0.5

ADAM_LR = 0.001
ADAM_B1 = 0.9
ADAM_B2 = 0.999
ADAM_EPS = 1e-08
ADAM_WD = 0.01
ADAM_STEP = 10

LANES = 128
VMEM_LIMIT = 56 * 1024 * 1024

ROW_TILE = 256
ATT_TILE = 1024
GDN_TILE = 256
MM_TILE = (1024, 1408, 2048)

NN = (((1,), (0,)), ((), ()))
NT = (((1,), (1,)), ((), ()))
TN = (((0,), (0,)), ((), ()))


def _dot(a, b, dims=NN, prec=None):
    return lax.dot_general(a, b, dims, precision=prec, preferred_element_type=F32)


def _tile(n, pref):
    if n <= pref:
        return n
    step = LANES if pref >= LANES else 8
    for t in range(pref - pref % step, 0, -step):
        if n % t == 0:
            return t
    return n


def _params(sem):
    return pltpu.CompilerParams(dimension_semantics=sem, vmem_limit_bytes=VMEM_LIMIT)


def _sigmoid(x):
    return 0.5 * jnp.tanh(0.5 * x) + 0.5


def _softplus(x):
    return jnp.maximum(x, 0.0) + jnp.log(1.0 + jnp.exp(-jnp.abs(x)))


def _matmul(a, b, mode, *, name):
    if mode == "nn":
        (m, k), (k2, n) = a.shape, b.shape
    elif mode == "nt":
        (m, k), (n, k2) = a.shape, b.shape
    else:
        (k, m), (k2, n) = a.shape, b.shape
    assert k == k2, (a.shape, b.shape, mode)
    tm, tn, tk = _tile(m, MM_TILE[0]), _tile(n, MM_TILE[1]), _tile(k, MM_TILE[2])
    nk = k // tk
    if mode == "tn":
        a_spec = pl.BlockSpec((tk, tm), lambda i, j, kk: (kk, i))
    else:
        a_spec = pl.BlockSpec((tm, tk), lambda i, j, kk: (i, kk))
    if mode == "nt":
        b_spec = pl.BlockSpec((tn, tk), lambda i, j, kk: (j, kk))
    else:
        b_spec = pl.BlockSpec((tk, tn), lambda i, j, kk: (kk, j))
    o_spec = pl.BlockSpec((tm, tn), lambda i, j, kk: (i, j))
    dims = {"nn": NN, "nt": NT, "tn": TN}[mode]

    def body(a_ref, b_ref, o_ref, *scratch):
        if nk == 1:
            o_ref[...] = _dot(a_ref[...], b_ref[...], dims)
            return
        acc = scratch[0]
        kk = pl.program_id(2)

        @pl.when(kk == 0)
        def _():
            acc[...] = jnp.zeros_like(acc)

        acc[...] += _dot(a_ref[...], b_ref[...], dims)

        @pl.when(kk == nk - 1)
        def _():
            o_ref[...] = acc[...]

    return pl.pallas_call(
        body, name=name, grid=(m // tm, n // tn, nk), in_specs=[a_spec, b_spec], out_specs=o_spec,
        out_shape=jax.ShapeDtypeStruct((m, n), F32),
        scratch_shapes=[pltpu.VMEM((tm, tn), F32)] if nk > 1 else [],
        compiler_params=_params(("parallel", "parallel", "arbitrary")),
    )(a, b)


def _rms_in_proj(h, g, w, *, name):
    t, d = h.shape
    n = w.shape[1]
    tm, tn = _tile(t, MM_TILE[0]), _tile(n, MM_TILE[1])

    def body(h_ref, g_ref, w_ref, o_ref, hn_ref):
        @pl.when(pl.program_id(1) == 0)
        def _():
            x = h_ref[...]
            r = lax.rsqrt(jnp.mean(x * x, axis=-1, keepdims=True) + RMS_EPS)
            hn_ref[...] = (x * r * g_ref[...]).astype(BF16)

        o_ref[...] = _dot(hn_ref[...], w_ref[...])

    return pl.pallas_call(
        body, name=name, grid=(t // tm, n // tn),
        in_specs=[pl.BlockSpec((tm, d), lambda i, j: (i, 0)), pl.BlockSpec((1, d), lambda i, j: (0, 0)),
                  pl.BlockSpec((d, tn), lambda i, j: (0, j))],
        out_specs=[pl.BlockSpec((tm, tn), lambda i, j: (i, j)), pl.BlockSpec((tm, d), lambda i, j: (i, 0))],
        out_shape=[jax.ShapeDtypeStruct((t, n), F32), jax.ShapeDtypeStruct((t, d), BF16)],
        compiler_params=_params(("parallel", "arbitrary")),
    )(h, g, w)


def _matmul_rms_bwd(dproj, w, h, g, dres, *, name, with_bf16):
    t, k = dproj.shape
    d = w.shape[0]
    tm = _tile(t, 2 * ROW_TILE)

    def body(dp_ref, w_ref, h_ref, g_ref, dres_ref, *outs):
        i = pl.program_id(0)
        dh_ref, dg_ref = outs[0], outs[-1]
        dyv = _dot(dp_ref[...], w_ref[...], NT)
        x = h_ref[...]
        r = lax.rsqrt(jnp.mean(x * x, axis=-1, keepdims=True) + RMS_EPS)
        xh = x * r
        dxh = dyv * g_ref[...]
        dh = dres_ref[...] + r * (dxh - xh * jnp.mean(dxh * xh, axis=-1, keepdims=True))
        dh_ref[...] = dh
        if with_bf16:
            outs[1][...] = dh.astype(BF16)

        @pl.when(i == 0)
        def _():
            dg_ref[...] = jnp.zeros_like(dg_ref)

        dg_ref[...] += jnp.sum(dyv * xh, axis=0, keepdims=True)

    row = pl.BlockSpec((tm, d), lambda i: (i, 0))
    vec = pl.BlockSpec((1, d), lambda i: (0, 0))
    out_shape = [jax.ShapeDtypeStruct((t, d), F32)] + ([jax.ShapeDtypeStruct((t, d), BF16)] if with_bf16 else [])
    out_specs = [row] * len(out_shape) + [vec]
    out_shape.append(jax.ShapeDtypeStruct((1, d), F32))
    return pl.pallas_call(
        body, name=name, grid=(t // tm,),
        in_specs=[pl.BlockSpec((tm, k), lambda i: (i, 0)), pl.BlockSpec((d, k), lambda i: (0, 0)), row, vec, row],
        out_specs=out_specs, out_shape=out_shape, compiler_params=_params(("arbitrary",)),
    )(dproj, w, h, g, dres)


def _rope_partner(x):
    lane = lax.broadcasted_iota(jnp.int32, x.shape, 1)
    swapped = jnp.where(lane < MLA_NOPE + MLA_ROPE // 2, pltpu.roll(x, LANES - 16, 1), pltpu.roll(x, 16, 1))
    return jnp.where((lane >= MLA_NOPE) & (lane < MLA_NOPE + MLA_ROPE), swapped, 0.0)


def _pool_counts(row0, tm, w):
    t_idx = row0 + lax.broadcasted_iota(jnp.int32, (tm, POOL_GROUP), 0)
    return jnp.minimum(t_idx + 1, w).astype(F32)


def _ab_prep(proj, pos, inv_freq, q_a_norm, kv_a_norm, wq, wk, wv, wpool, *, name):
    t = proj.shape[0]
    tm = _tile(t, ROW_TILE)
    hb = tm // POOL_HALO

    def body(p_ref, halo_ref, pos_ref, inv_ref, qg_ref, kg_ref, wq_ref, wk_ref, wv_ref, wp_ref,
             q_ref, k_ref, v_ref, yb_ref, qn_ref, kvn_ref, d_ref, cos_ref, sin_ref, ext):
        i = pl.program_id(0)
        ql = p_ref[:, 0:MLA_Q_RANK]
        r = lax.rsqrt(jnp.mean(ql * ql, axis=-1, keepdims=True) + RMS_EPS)
        qn = (ql * r * qg_ref[...]).astype(BF16)
        qn_ref[...] = qn
        kl = p_ref[:, MLA_Q_RANK:MLA_Q_RANK + MLA_KV_RANK]
        r = lax.rsqrt(jnp.mean(kl * kl, axis=-1, keepdims=True) + RMS_EPS)
        kvn = (kl * r * kg_ref[...]).astype(BF16)
        kvn_ref[...] = kvn
        ang = pos_ref[...].astype(F32) * inv_ref[...]
        lane = lax.broadcasted_iota(jnp.int32, (tm, LANES), 1)
        in_rope = (lane >= MLA_NOPE) & (lane < MLA_NOPE + MLA_ROPE)
        cos_t = jnp.where(in_rope, jnp.cos(ang), 1.0)
        sin_t = jnp.where(in_rope, jnp.sin(ang), 0.0)
        sin_t = jnp.where(lane < MLA_NOPE + MLA_ROPE // 2, -sin_t, sin_t)
        cos_ref[...] = cos_t
        sin_ref[...] = sin_t
        kr = p_ref[:, 384:512]
        kr = kr * cos_t + _rope_partner(kr) * sin_t
        qraw = _dot(qn, wq_ref[...])
        kvk = _dot(kvn, wk_ref[...])
        for h in range(MLA_HEADS):
            sl = slice(h * LANES, (h + 1) * LANES)
            qh = qraw[:, sl]
            q_ref[:, sl] = ((qh * cos_t + _rope_partner(qh) * sin_t) * ATT_SCALE).astype(BF16)
            k_ref[:, sl] = (kvk[:, sl] + kr).astype(BF16)
        v_ref[...] = _dot(kvn, wv_ref[...]).astype(BF16)
        xp = p_ref[:, 512:1024]
        ext[0:POOL_HALO, :] = jnp.where(i > 0, halo_ref[...], 0.0)
        ext[POOL_HALO:POOL_HALO + tm, :] = xp
        for g, w in enumerate(POOL_WINDOWS):
            lo = g * POOL_GROUP
            acc = ext[POOL_HALO:POOL_HALO + tm, lo:lo + POOL_GROUP]
            for s in range(1, w):
                acc = acc + ext[POOL_HALO - s:POOL_HALO - s + tm, lo:lo + POOL_GROUP]
            cnt = _pool_counts(i * tm, tm, w)
            d_ref[:, lo:lo + POOL_GROUP] = (acc / cnt - xp[:, lo:lo + POOL_GROUP]).astype(BF16)
        yb_ref[...] = _dot(d_ref[...], wp_ref[...])

    row = lambda w: pl.BlockSpec((tm, w), lambda i: (i, 0))
    vec = lambda w: pl.BlockSpec((1, w), lambda i: (0, 0))
    whole = lambda a: pl.BlockSpec(a.shape, lambda i: (0, 0))
    return pl.pallas_call(
        body, name=name, grid=(t // tm,),
        in_specs=[row(1024), pl.BlockSpec((POOL_HALO, POOL_WIDTH), lambda i: (jnp.maximum(i * hb - 1, 0), 1)),
                  pl.BlockSpec((tm, 1), lambda i: (i, 0)), vec(LANES), vec(MLA_Q_RANK), vec(MLA_KV_RANK),
                  whole(wq), whole(wk), whole(wv), whole(wpool)],
        out_specs=[row(1024), row(1024), row(512), row(512), row(MLA_Q_RANK), row(MLA_KV_RANK), row(POOL_WIDTH),
                   row(LANES), row(LANES)],
        out_shape=[jax.ShapeDtypeStruct((t, 1024), BF16), jax.ShapeDtypeStruct((t, 1024), BF16),
                   jax.ShapeDtypeStruct((t, 512), BF16), jax.ShapeDtypeStruct((t, 512), F32),
                   jax.ShapeDtypeStruct((t, MLA_Q_RANK), BF16), jax.ShapeDtypeStruct((t, MLA_KV_RANK), BF16),
                   jax.ShapeDtypeStruct((t, POOL_WIDTH), BF16), jax.ShapeDtypeStruct((t, LANES), F32),
                   jax.ShapeDtypeStruct((t, LANES), F32)],
        scratch_shapes=[pltpu.VMEM((tm + POOL_HALO, POOL_WIDTH), F32)],
        compiler_params=_params(("parallel",)),
    )(proj, proj, pos, inv_freq, q_a_norm, kv_a_norm, wq, wk, wv, wpool)


def _ab_prep_bwd(proj, q_a_norm, kv_a_norm, dq, dk, dv, cos_t, sin_t, dyb, dz, wq, wk, wv, wpool, *, name):
    t = proj.shape[0]
    tm = _tile(t, ROW_TILE)
    hb = tm // POOL_HALO
    last_halo = t // POOL_HALO - 1
    nt = t // tm

    def body(p_ref, qg_ref, kg_ref, dq_ref, dk_ref, dv_ref, c_ref, s_ref, dyb_ref, dybn_ref, dz_ref,
             wq_ref, wk_ref, wv_ref, wp_ref, dp_ref, dqr_ref, dkb_ref, dqg_ref, dkg_ref, ext):
        i = pl.program_id(0)

        @pl.when(i == 0)
        def _():
            dqg_ref[...] = jnp.zeros_like(dqg_ref)
            dkg_ref[...] = jnp.zeros_like(dkg_ref)

        def norm_bwd(x, g, dy, dg_ref):
            r = lax.rsqrt(jnp.mean(x * x, axis=-1, keepdims=True) + RMS_EPS)
            xh = x * r
            dxh = dy * g
            dg_ref[...] += jnp.sum(dy * xh, axis=0, keepdims=True)
            return r * (dxh - xh * jnp.mean(dxh * xh, axis=-1, keepdims=True))

        c, s = c_ref[...], s_ref[...]
        lane = lax.broadcasted_iota(jnp.int32, (tm, LANES), 1)
        in_rope = (lane >= MLA_NOPE) & (lane < MLA_NOPE + MLA_ROPE)
        dkr = jnp.zeros((tm, LANES), F32)
        for h in range(MLA_HEADS):
            sl = slice(h * LANES, (h + 1) * LANES)
            g = dq_ref[:, sl]
            dqr_ref[:, sl] = ((g * c + _rope_partner(g * s)) * ATT_SCALE).astype(BF16)
            gk = dk_ref[:, sl]
            dkb_ref[:, sl] = gk.astype(BF16)
            dkr = dkr + jnp.where(in_rope, gk, 0.0)
        dkr = dkr * c + _rope_partner(dkr * s)
        dqn = _dot(dqr_ref[...], wq_ref[...], NT)
        dkvn = _dot(dkb_ref[...], wk_ref[...], NT) + _dot(dv_ref[...], wv_ref[...], NT)
        dql = norm_bwd(p_ref[:, 0:MLA_Q_RANK], qg_ref[...], dqn, dqg_ref)
        dp_ref[:, 0:MLA_Q_RANK] = dql.astype(BF16)
        dkl = norm_bwd(p_ref[:, MLA_Q_RANK:384], kg_ref[...], dkvn, dkg_ref)
        dp_ref[:, MLA_Q_RANK:384] = dkl.astype(BF16)
        dp_ref[:, 384:512] = dkr.astype(BF16)
        ddv = _dot(dyb_ref[...], wp_ref[...], NT)
        ddn = _dot(dybn_ref[...], wp_ref[...], NT)
        for g, w in enumerate(POOL_WINDOWS):
            lo = g * POOL_GROUP
            ext[0:tm, lo:lo + POOL_GROUP] = ddv[:, lo:lo + POOL_GROUP] / _pool_counts(i * tm, tm, w)
            nxt = ddn[:, lo:lo + POOL_GROUP] / _pool_counts((i + 1) * tm, POOL_HALO, w)
            ext[tm:tm + POOL_HALO, lo:lo + POOL_GROUP] = jnp.where(i < nt - 1, nxt, 0.0)
        for g, w in enumerate(POOL_WINDOWS):
            lo = g * POOL_GROUP
            acc = ext[0:tm, lo:lo + POOL_GROUP]
            for s in range(1, w):
                acc = acc + ext[s:s + tm, lo:lo + POOL_GROUP]
            dp_ref[:, 512 + lo:512 + lo + POOL_GROUP] = (acc - ddv[:, lo:lo + POOL_GROUP]).astype(BF16)
        dp_ref[:, 1024:2048] = dz_ref[...]

    row = lambda w: pl.BlockSpec((tm, w), lambda i: (i, 0))
    vec = lambda w: pl.BlockSpec((1, w), lambda i: (0, 0))
    whole = lambda a: pl.BlockSpec(a.shape, lambda i: (0, 0))
    return pl.pallas_call(
        body, name=name, grid=(nt,),
        in_specs=[row(1024), vec(MLA_Q_RANK), vec(MLA_KV_RANK), row(1024), row(1024), row(512), row(LANES), row(LANES),
                  row(POOL_WIDTH),
                  pl.BlockSpec((POOL_HALO, POOL_WIDTH), lambda i: (jnp.minimum((i + 1) * hb, last_halo), 0)),
                  row(1024), whole(wq), whole(wk), whole(wv), whole(wpool)],
        out_specs=[row(IN_AB_PAD), row(1024), row(1024), vec(MLA_Q_RANK), vec(MLA_KV_RANK)],
        out_shape=[jax.ShapeDtypeStruct((t, IN_AB_PAD), BF16), jax.ShapeDtypeStruct((t, 1024), BF16),
                   jax.ShapeDtypeStruct((t, 1024), BF16), jax.ShapeDtypeStruct((1, MLA_Q_RANK), F32),
                   jax.ShapeDtypeStruct((1, MLA_KV_RANK), F32)],
        scratch_shapes=[pltpu.VMEM((tm + POOL_HALO, POOL_WIDTH), F32)],
        compiler_params=_params(("arbitrary",)),
    )(proj, q_a_norm, kv_a_norm, dq, dk, dv, cos_t, sin_t, dyb, dyb, dz, wq, wk, wv, wpool)


def _gate_out_proj(o, ybraw, proj, pool_scale, w, hres, *, name):
    t = o.shape[0]
    tm = _tile(t, 2 * ROW_TILE)

    def body(o_ref, yb_ref, z_ref, ps_ref, w_ref, h_ref, ho_ref, y_ref):
        z = z_ref[...]
        sz = z * _sigmoid(z)
        y_ref[:, 0:512] = (o_ref[...] * sz[:, 0:512]).astype(BF16)
        y_ref[:, 512:1024] = (yb_ref[...] * ps_ref[...] * sz[:, 512:1024]).astype(BF16)
        ho_ref[...] = h_ref[...] + _dot(y_ref[...], w_ref[...])

    row = lambda w_: pl.BlockSpec((tm, w_), lambda i: (i, 0))
    return pl.pallas_call(
        body, name=name, grid=(t // tm,),
        in_specs=[row(512), row(512), pl.BlockSpec((tm, 1024), lambda i: (i, 1)), pl.BlockSpec((1, 512), lambda i: (0, 0)),
                  pl.BlockSpec(w.shape, lambda i: (0, 0)), row(1024)],
        out_specs=[row(1024), row(1024)],
        out_shape=[jax.ShapeDtypeStruct((t, 1024), F32), jax.ShapeDtypeStruct((t, 1024), BF16)],
        compiler_params=_params(("parallel",)),
    )(o, ybraw, proj, pool_scale, w, hres)


def _gate_bwd(dy, o, ybraw, proj, pool_scale, *, name):
    t = o.shape[0]
    tm = _tile(t, ROW_TILE)

    def body(dy_ref, o_ref, yb_ref, z_ref, ps_ref, do_ref, dl_ref, dyb_ref, dz_ref, dps_ref):
        i = pl.program_id(0)
        z = z_ref[...]
        sg = _sigmoid(z)
        sz = z * sg
        dsz = sg * (1.0 + z * (1.0 - sg))
        dyv = dy_ref[...]
        dcat = dyv * sz
        ov = o_ref[...]
        ybs = yb_ref[...] * ps_ref[...]
        dz_ref[:, 0:512] = (dyv[:, 0:512] * ov * dsz[:, 0:512]).astype(BF16)
        dz_ref[:, 512:1024] = (dyv[:, 512:1024] * ybs * dsz[:, 512:1024]).astype(BF16)
        do = dcat[:, 0:512]
        do_ref[...] = do.astype(BF16)
        r_i = (lax.broadcasted_iota(jnp.int32, (1024, 512), 0) % 512) // MLA_V
        c_i = lax.broadcasted_iota(jnp.int32, (1024, 512), 1) // MLA_V
        prod = do * ov
        hi = prod.astype(BF16)
        lo = (prod - hi.astype(F32)).astype(BF16)
        dl_ref[...] = _dot(jnp.concatenate([hi, lo], axis=1), (r_i == c_i).astype(BF16))
        dyb_ref[...] = (dcat[:, 512:1024] * ps_ref[...]).astype(BF16)

        @pl.when(i == 0)
        def _():
            dps_ref[...] = jnp.zeros_like(dps_ref)

        dps_ref[...] += jnp.sum(dcat[:, 512:1024] * yb_ref[...], axis=0, keepdims=True)

    row = lambda w: pl.BlockSpec((tm, w), lambda i: (i, 0))
    vec = pl.BlockSpec((1, 512), lambda i: (0, 0))
    return pl.pallas_call(
        body, name=name, grid=(t // tm,),
        in_specs=[row(1024), row(512), row(512), pl.BlockSpec((tm, 1024), lambda i: (i, 1)), vec],
        out_specs=[row(512), row(512), row(512), row(1024), vec],
        out_shape=[jax.ShapeDtypeStruct((t, 512), BF16), jax.ShapeDtypeStruct((t, 512), F32),
                   jax.ShapeDtypeStruct((t, 512), BF16), jax.ShapeDtypeStruct((t, 1024), BF16),
                   jax.ShapeDtypeStruct((1, 512), F32)],
        compiler_params=_params(("arbitrary",)),
    )(dy, o, ybraw, proj, pool_scale)


ATT_HP_FWD = 4
ATT_HP_BWD = 2


def _diag_mask(tq):
    return lax.broadcasted_iota(jnp.int32, (tq, tq), 1) <= lax.broadcasted_iota(jnp.int32, (tq, tq), 0)


def _block_schedule(nq, key_major):
    if key_major:
        pairs = [(qi, ki) for ki in range(nq) for qi in range(ki, nq)]
    else:
        pairs = [(qi, ki) for qi in range(nq) for ki in range(qi + 1)]
    return jnp.asarray([p[0] for p in pairs], jnp.int32), jnp.asarray([p[1] for p in pairs], jnp.int32)


def _attn_fwd(q, k, v, *, name):
    t = q.shape[0]
    tq = _tile(t, ATT_TILE)
    nq = t // tq
    hp = ATT_HP_FWD
    qi_tab, ki_tab = _block_schedule(nq, key_major=False)

    def body(qi_ref, ki_ref, q_ref, k_ref, v_ref, o_ref, lse_ref, m_sc, l_sc, acc_sc):
        step = pl.program_id(1)
        qi, ki = qi_ref[step], ki_ref[step]

        @pl.when(ki == 0)
        def _():
            m_sc[...] = jnp.full_like(m_sc, -jnp.inf)
            l_sc[...] = jnp.zeros_like(l_sc)
            acc_sc[...] = jnp.zeros_like(acc_sc)

        def block(on_diagonal):
            scores = []
            for h in range(hp):
                sl = slice(h * LANES, (h + 1) * LANES)
                scores.append(_dot(q_ref[:, sl], k_ref[:, sl], NT))
            if on_diagonal:
                mask = _diag_mask(tq)
                scores = [jnp.where(mask, s, -jnp.inf) for s in scores]
            for h, s in enumerate(scores):
                vv = v_ref[:, (h // 2) * LANES:(h // 2 + 1) * LANES]
                m_prev = m_sc[h]
                m_new = jnp.maximum(m_prev, jnp.max(s, axis=-1, keepdims=True))
                alpha = jnp.exp(m_prev - m_new)
                p = jnp.exp(s - m_new[:, 0:1])
                l_sc[h] = alpha * l_sc[h] + jnp.sum(p, axis=-1, keepdims=True)
                acc_sc[h] = alpha * acc_sc[h] + _dot(p.astype(BF16), vv)
                m_sc[h] = m_new

        pl.when(ki < qi)(functools.partial(block, False))
        pl.when(ki == qi)(functools.partial(block, True))

        @pl.when(ki == qi)
        def _():
            first = lax.broadcasted_iota(jnp.int32, (tq, LANES), 1) < MLA_V
            for pr in range(hp // 2):
                a, b = 2 * pr, 2 * pr + 1
                sl = slice(pr * LANES, (pr + 1) * LANES)
                o_ref[:, sl] = jnp.where(first, acc_sc[a] / l_sc[a], acc_sc[b] / l_sc[b])
                lse_ref[:, sl] = jnp.where(first, m_sc[a] + jnp.log(l_sc[a]), m_sc[b] + jnp.log(l_sc[b]))

    grid_spec = pltpu.PrefetchScalarGridSpec(
        num_scalar_prefetch=2, grid=(MLA_HEADS // hp, qi_tab.shape[0]),
        in_specs=[pl.BlockSpec((tq, hp * LANES), lambda g, s, qt, kt: (qt[s], g)),
                  pl.BlockSpec((tq, hp * LANES), lambda g, s, qt, kt: (kt[s], g)),
                  pl.BlockSpec((tq, hp * MLA_V), lambda g, s, qt, kt: (kt[s], g))],
        out_specs=[pl.BlockSpec((tq, hp * MLA_V), lambda g, s, qt, kt: (qt[s], g)),
                   pl.BlockSpec((tq, hp * MLA_V), lambda g, s, qt, kt: (qt[s], g))],
        scratch_shapes=[pltpu.VMEM((hp, tq, LANES), F32)] * 3,
    )
    return pl.pallas_call(
        body, name=name, grid_spec=grid_spec,
        out_shape=[jax.ShapeDtypeStruct((t, 512), F32), jax.ShapeDtypeStruct((t, 512), F32)],
        compiler_params=_params(("parallel", "arbitrary")),
    )(qi_tab, ki_tab, q, k, v)


def _attn_bwd(q, k, v, do, lse, delta, *, name):
    t = q.shape[0]
    tq = _tile(t, ATT_TILE)
    nq = t // tq
    hp = ATT_HP_BWD
    qi_tab, ki_tab = _block_schedule(nq, key_major=True)

    def body(qi_ref, ki_ref, q_ref, k_ref, v_ref, do_ref, lse_ref, dl_ref, dq_ref, dk_ref, dv_ref, dk_sc, dv_sc):
        step = pl.program_id(1)
        qi, ki = qi_ref[step], ki_ref[step]

        @pl.when(step == 0)
        def _():
            dq_ref[...] = jnp.zeros_like(dq_ref)

        @pl.when(qi == ki)
        def _():
            dk_sc[...] = jnp.zeros_like(dk_sc)
            dv_sc[...] = jnp.zeros_like(dv_sc)

        def block(on_diagonal):
            lane = lax.broadcasted_iota(jnp.int32, (tq, LANES), 1)
            rows = pl.ds(pl.multiple_of(qi * tq, tq), tq)
            heads = [slice(h * LANES, (h + 1) * LANES) for h in range(hp)]
            scores = [_dot(q_ref[:, sl], k_ref[:, sl], NT) for sl in heads]
            dps = []
            for h in range(hp):
                dov = do_ref[:, (h // 2) * LANES:(h // 2 + 1) * LANES]
                mine = (lane < MLA_V) if h % 2 == 0 else (lane >= MLA_V)
                dps.append(_dot(jnp.where(mine, dov, jnp.zeros_like(dov)), v_ref[:, (h // 2) * LANES:(h // 2 + 1) * LANES], NT))
            mask = _diag_mask(tq) if on_diagonal else None
            for h, sl in enumerate(heads):
                col = (h // 2) * LANES + (h % 2) * MLA_V
                p = jnp.exp(scores[h] - lse_ref[:, col:col + 1])
                if on_diagonal:
                    p = jnp.where(mask, p, 0.0)
                ds = (p * (dps[h] - dl_ref[:, col:col + 1])).astype(BF16)
                dv_sc[h] += _dot(p.astype(BF16), do_ref[:, (h // 2) * LANES:(h // 2 + 1) * LANES], TN)
                dk_sc[h] += _dot(ds, q_ref[:, sl], TN)
                dq_ref[rows, sl] += _dot(ds, k_ref[:, sl], NN)

        pl.when(qi > ki)(functools.partial(block, False))
        pl.when(qi == ki)(functools.partial(block, True))

        @pl.when(qi == nq - 1)
        def _():
            first = lax.broadcasted_iota(jnp.int32, (tq, LANES), 1) < MLA_V
            for h in range(hp):
                dk_ref[:, h * LANES:(h + 1) * LANES] = dk_sc[h]
            for pr in range(hp // 2):
                dv_ref[:, pr * LANES:(pr + 1) * LANES] = jnp.where(first, dv_sc[2 * pr], dv_sc[2 * pr + 1]).astype(BF16)

    qrow = lambda w: pl.BlockSpec((tq, w), lambda g, s, qt, kt: (qt[s], g))
    krow = lambda w: pl.BlockSpec((tq, w), lambda g, s, qt, kt: (kt[s], g))
    grid_spec = pltpu.PrefetchScalarGridSpec(
        num_scalar_prefetch=2, grid=(MLA_HEADS // hp, qi_tab.shape[0]),
        in_specs=[qrow(hp * LANES), krow(hp * LANES), krow(hp * MLA_V), qrow(hp * MLA_V), qrow(hp * MLA_V), qrow(hp * MLA_V)],
        out_specs=[pl.BlockSpec((t, hp * LANES), lambda g, s, qt, kt: (0, g)), krow(hp * LANES), krow(hp * MLA_V)],
        scratch_shapes=[pltpu.VMEM((hp, tq, LANES), F32), pltpu.VMEM((hp, tq, LANES), F32)],
    )
    return pl.pallas_call(
        body, name=name, grid_spec=grid_spec,
        out_shape=[jax.ShapeDtypeStruct((t, 1024), F32), jax.ShapeDtypeStruct((t, 1024), F32),
                   jax.ShapeDtypeStruct((t, 512), BF16)],
        compiler_params=_params(("parallel", "arbitrary")),
    )(qi_tab, ki_tab, q, k, v, do, lse, delta)


def _conv_rows(ext, tm, w_ref, sec):
    c0 = sec * 1024
    y = ext[CONV_HALO - 3:CONV_HALO - 3 + tm, c0:c0 + 1024] * w_ref[0:1, c0:c0 + 1024]
    for j in range(1, CONV_WIDTH):
        y = y + ext[CONV_HALO - 3 + j:CONV_HALO - 3 + j + tm, c0:c0 + 1024] * w_ref[j:j + 1, c0:c0 + 1024]
    return y


def _c_prep(proj_c, conv_w, a_log, dt_bias, *, name):
    t = proj_c.shape[0]
    tm = _tile(t, ROW_TILE)
    hb = tm // CONV_HALO

    def body(p_ref, halo_ref, ab_ref, w_ref, al_ref, dtb_ref, q_ref, k_ref, v_ref, g_ref, b_ref, gt_ref, ext):
        i = pl.program_id(0)
        ext[0:CONV_HALO, :] = jnp.where(i > 0, halo_ref[...], 0.0)
        ext[CONV_HALO:CONV_HALO + tm, :] = p_ref[...]
        for sec, o_ref in enumerate((q_ref, k_ref, v_ref)):
            y = _conv_rows(ext, tm, w_ref, sec)
            y = y * _sigmoid(y)
            if sec == 2:
                o_ref[...] = y
                continue
            scale = GDN_DK ** -0.5 if sec == 0 else 1.0
            for h in range(GDN_HEADS):
                sl = slice(h * LANES, (h + 1) * LANES)
                blk = y[:, sl]
                r = lax.rsqrt(jnp.sum(blk * blk, axis=-1, keepdims=True) + RMS_EPS)
                o_ref[:, sl] = blk * (r * scale)
        ab = ab_ref[...]
        g = -jnp.exp(al_ref[...]) * _softplus(ab + dtb_ref[...])
        beta = _sigmoid(ab)
        ri = lax.broadcasted_iota(jnp.int32, (tm, tm), 0)
        ci = lax.broadcasted_iota(jnp.int32, (tm, tm), 1)
        lower = ((ri // CHUNK) == (ci // CHUNK)) & (ri >= ci)
        gc = _dot(lower.astype(F32), g, NN, HI)
        eye = lax.broadcasted_iota(jnp.int32, (LANES, LANES), 0) == lax.broadcasted_iota(jnp.int32, (LANES, LANES), 1)
        gt_ref[...] = _dot(eye.astype(F32), gc, NT, HI)[0:GDN_HEADS, :]
        for h in range(GDN_HEADS):
            sl = slice(h * LANES, (h + 1) * LANES)
            g_ref[:, sl] = jnp.broadcast_to(gc[:, h:h + 1], (tm, LANES))
            b_ref[:, sl] = jnp.broadcast_to(beta[:, GDN_HEADS + h:GDN_HEADS + h + 1], (tm, LANES))

    row = lambda w: pl.BlockSpec((tm, w), lambda i: (i, 0))
    vec = lambda r, w: pl.BlockSpec((r, w), lambda i: (0, 0))
    out = jax.ShapeDtypeStruct((t, 1024), F32)
    return pl.pallas_call(
        body, name=name, grid=(t // tm,),
        in_specs=[row(3072), pl.BlockSpec((CONV_HALO, 3072), lambda i: (jnp.maximum(i * hb - 1, 0), 0)),
                  pl.BlockSpec((tm, LANES), lambda i: (i, 32)), vec(CONV_WIDTH, 3072), vec(1, LANES), vec(1, LANES)],
        out_specs=[row(1024)] * 5 + [pl.BlockSpec((GDN_HEADS, tm), lambda i: (0, i))],
        out_shape=[out] * 5 + [jax.ShapeDtypeStruct((GDN_HEADS, t), F32)],
        scratch_shapes=[pltpu.VMEM((tm + CONV_HALO, 3072), F32)],
        compiler_params=_params(("parallel",)),
    )(proj_c, proj_c, proj_c, conv_w, a_log, dt_bias)


def _c_prep_bwd(proj_c, conv_w, a_log, dt_bias, dq, dk, dv, dgb, dbb, dz, *, name):
    t = proj_c.shape[0]
    tm = _tile(t, ROW_TILE // 2)
    hb = tm // CONV_HALO
    nt = t // tm
    rev = lambda i: nt - 1 - i

    def body(p_ref, halo_ref, ab_ref, w_ref, al_ref, dtb_ref, dq_ref, dk_ref, dv_ref, dg_ref, db_ref, dz_ref,
             dp_ref, dw_ref, dal_ref, ddt_ref, ext, dyext, carry, taps):
        step = pl.program_id(0)
        i = rev(step)

        @pl.when(step == 0)
        def _():
            dw_ref[...] = jnp.zeros_like(dw_ref)
            dal_ref[...] = jnp.zeros_like(dal_ref)
            ddt_ref[...] = jnp.zeros_like(ddt_ref)
            carry[...] = jnp.zeros_like(carry)

        ext[0:CONV_HALO, :] = jnp.where(i > 0, halo_ref[...], 0.0)
        ext[CONV_HALO:CONV_HALO + tm, :] = p_ref[...]
        for sec, g_ref in enumerate((dq_ref, dk_ref, dv_ref)):
            c0 = sec * 1024
            for j in range(CONV_WIDTH):
                taps[j] = ext[CONV_HALO - 3 + j:CONV_HALO - 3 + j + tm, c0:c0 + 1024]
            y = taps[0] * w_ref[0:1, c0:c0 + 1024]
            for j in range(1, CONV_WIDTH):
                y = y + taps[j] * w_ref[j:j + 1, c0:c0 + 1024]
            sg = _sigmoid(y)
            act = y * sg
            if sec == 2:
                dact = g_ref[...]
            else:
                scale = GDN_DK ** -0.5 if sec == 0 else 1.0
                parts = []
                for h in range(GDN_HEADS):
                    sl = slice(h * LANES, (h + 1) * LANES)
                    blk = act[:, sl]
                    r = lax.rsqrt(jnp.sum(blk * blk, axis=-1, keepdims=True) + RMS_EPS)
                    n = blk * r
                    dn = g_ref[:, sl] * scale
                    parts.append(r * (dn - n * jnp.sum(dn * n, axis=-1, keepdims=True)))
                dact = jnp.concatenate(parts, axis=-1)
            dy = dact * (sg * (1.0 + y * (1.0 - sg)))
            dyext[0:tm, c0:c0 + 1024] = dy
            for j in range(CONV_WIDTH):
                dw_ref[j:j + 1, c0:c0 + 1024] += jnp.sum(dy * taps[j], axis=0, keepdims=True)
        dyext[tm:tm + CONV_HALO, :] = carry[...]
        carry[...] = dyext[0:CONV_HALO, :]
        for sec in range(3):
            c0 = sec * 1024
            dx = dyext[3:3 + tm, c0:c0 + 1024] * w_ref[0:1, c0:c0 + 1024]
            for j in range(1, CONV_WIDTH):
                dx = dx + dyext[3 - j:3 - j + tm, c0:c0 + 1024] * w_ref[j:j + 1, c0:c0 + 1024]
            dp_ref[:, c0:c0 + 1024] = dx.astype(BF16)
        dp_ref[:, 3072:4096] = dz_ref[...]
        lane = lax.broadcasted_iota(jnp.int32, (tm, LANES), 1)
        dg = jnp.zeros((tm, LANES), F32)
        dbeta = jnp.zeros((tm, LANES), F32)
        for h in range(GDN_HEADS):
            sl = slice(h * LANES, (h + 1) * LANES)
            dg = dg + jnp.where(lane == h, dg_ref[:, sl], 0.0)
            dbeta = dbeta + jnp.where(lane == GDN_HEADS + h, db_ref[:, sl], 0.0)
        ri = lax.broadcasted_iota(jnp.int32, (tm, tm), 0)
        ci = lax.broadcasted_iota(jnp.int32, (tm, tm), 1)
        upper = ((ri // CHUNK) == (ci // CHUNK)) & (ri <= ci)
        dg = _dot(upper.astype(F32), dg, NN, HI)
        pre = ab_ref[...] + dtb_ref[...]
        s = _sigmoid(pre)
        a_exp = jnp.exp(al_ref[...])
        dg_da = dg * (-a_exp * s)
        dp_ref[:, 4096:IN_C_PAD] = (dg_da + dbeta * s * (1.0 - s)).astype(BF16)
        dal_ref[...] += jnp.sum(dg * (-a_exp * _softplus(pre)), axis=0, keepdims=True)
        ddt_ref[...] += jnp.sum(dg_da, axis=0, keepdims=True)

    row = lambda w: pl.BlockSpec((tm, w), lambda s: (rev(s), 0))
    vec = lambda r, w: pl.BlockSpec((r, w), lambda s: (0, 0))
    return pl.pallas_call(
        body, name=name, grid=(nt,),
        in_specs=[row(3072), pl.BlockSpec((CONV_HALO, 3072), lambda s: (jnp.maximum(rev(s) * hb - 1, 0), 0)),
                  pl.BlockSpec((tm, LANES), lambda s: (rev(s), 32)), vec(CONV_WIDTH, 3072), vec(1, LANES), vec(1, LANES),
                  row(1024), row(1024), row(1024), row(1024), row(1024), row(1024)],
        out_specs=[row(IN_C_PAD), vec(CONV_WIDTH, 3072), vec(1, LANES), vec(1, LANES)],
        out_shape=[jax.ShapeDtypeStruct((t, IN_C_PAD), BF16), jax.ShapeDtypeStruct((CONV_WIDTH, 3072), F32),
                   jax.ShapeDtypeStruct((1, LANES), F32), jax.ShapeDtypeStruct((1, LANES), F32)],
        scratch_shapes=[pltpu.VMEM((tm + CONV_HALO, 3072), F32), pltpu.VMEM((tm + CONV_HALO, 3072), F32),
                        pltpu.VMEM((CONV_HALO, 3072), F32), pltpu.VMEM((CONV_WIDTH, tm, 1024), F32)],
        compiler_params=_params(("arbitrary",)),
    )(proj_c, proj_c, proj_c, conv_w, a_log, dt_bias, dq, dk, dv, dgb, dbb, dz)


def _o_gate_bwd(dy, o, proj_c, o_norm, *, name):
    t = o.shape[0]
    tm = _tile(t, 2 * ROW_TILE)

    def body(dy_ref, o_ref, z_ref, g_ref, do_ref, dz_ref, dg_ref):
        i = pl.program_id(0)

        @pl.when(i == 0)
        def _():
            dg_ref[...] = jnp.zeros_like(dg_ref)

        dg = jnp.zeros((1, LANES), F32)
        for h in range(GDN_HEADS):
            sl = slice(h * LANES, (h + 1) * LANES)
            x = o_ref[:, sl]
            r = lax.rsqrt(jnp.mean(x * x, axis=-1, keepdims=True) + RMS_EPS)
            xh = x * r
            z = z_ref[:, sl]
            sg = _sigmoid(z)
            dyv = dy_ref[:, sl]
            dn = dyv * (z * sg)
            dz_ref[:, sl] = (dyv * xh * g_ref[...] * (sg * (1.0 + z * (1.0 - sg)))).astype(BF16)
            dxh = dn * g_ref[...]
            do_ref[:, sl] = r * (dxh - xh * jnp.mean(dxh * xh, axis=-1, keepdims=True))
            dg = dg + jnp.sum(dn * xh, axis=0, keepdims=True)
        dg_ref[...] += dg

    row = pl.BlockSpec((tm, 1024), lambda i: (i, 0))
    vec = pl.BlockSpec((1, LANES), lambda i: (0, 0))
    return pl.pallas_call(
        body, name=name, grid=(t // tm,), in_specs=[row, row, pl.BlockSpec((tm, 1024), lambda i: (i, 3)), vec],
        out_specs=[row, row, vec],
        out_shape=[jax.ShapeDtypeStruct((t, 1024), F32), jax.ShapeDtypeStruct((t, 1024), BF16),
                   jax.ShapeDtypeStruct((1, LANES), F32)],
        compiler_params=_params(("arbitrary",)),
    )(dy, o, proj_c, o_norm)


PAIR = 2 * CHUNK
GDN_HP = 8


def _bdot(a, b, dims=NN):
    return _dot(a.astype(BF16), b.astype(BF16), dims)


def _each(f, *lists):
    return [f(*args) for args in zip(*lists)]


def _pair_common(q, k, v, gci, gcj, beta):
    ri = lax.broadcasted_iota(jnp.int32, (PAIR, PAIR), 0)
    ci = lax.broadcasted_iota(jnp.int32, (PAIR, PAIR), 1)
    same = (ri // CHUNK) == (ci // CHUNK)
    incl = same & (ri >= ci)
    strict = same & (ri > ci)
    eye = (ri == ci).astype(F32)
    first = lax.broadcasted_iota(jnp.int32, (PAIR, LANES), 0) < CHUNK
    gamma = _each(lambda gi, gj: jnp.where(incl, jnp.exp(jnp.minimum(gi - gj, 0.0)), 0.0), gci, gcj)
    kb = _each(jnp.multiply, k, beta)
    kk = _each(lambda a, b: _bdot(a, b, NT), kb, k)
    qk = _each(lambda a, b: _bdot(a, b, NT), q, k)
    m = _each(lambda x, g: jnp.where(strict, x * g, 0.0), kk, gamma)
    tm_ = _each(lambda x: eye - x, m)
    pw = _each(lambda x: _bdot(x, x), m)
    for it in range(5):
        tm_ = _each(lambda x, p: x + _bdot(x, p), tm_, pw)
        if it < 4:
            pw = _each(lambda p: _bdot(p, p), pw)
    eg = _each(jnp.exp, gci)
    vb = _each(jnp.multiply, v, beta)
    kbe = _each(jnp.multiply, kb, eg)
    uw = _each(lambda x, a, b: _bdot(x, jnp.concatenate([a, b], axis=1)), tm_, vb, kbe)
    attn = _each(lambda x, g: jnp.where(incl, x * g, 0.0), qk, gamma)
    gl_a = _each(lambda g: g[CHUNK - 1:CHUNK, :], gci)
    gl_b = _each(lambda g: g[PAIR - 1:PAIR, :], gci)
    ek = _each(lambda a, b, g: jnp.exp(jnp.where(first, a, b) - g), gl_a, gl_b, gci)
    return dict(incl=incl, strict=strict, gamma=gamma, kb=kb, m=m, tm=tm_, eg=eg, vb=vb, kbe=kbe,
                u=_each(lambda x: x[:, :LANES], uw), w=_each(lambda x: x[:, LANES:], uw), attn=attn,
                qd=_each(jnp.multiply, q, eg), ek=ek, kd=_each(jnp.multiply, k, ek),
                glast_a=_each(jnp.exp, gl_a), glast_b=_each(jnp.exp, gl_b))


def _gdn_specs(t, ts, order):
    nc = ts // CHUNK
    blk = pl.BlockSpec((ts, GDN_HP * LANES), lambda h, s: (order(s), h))
    row = pl.BlockSpec((GDN_HP, 1, ts), lambda h, s: (h, 0, order(s)))
    st = pl.BlockSpec((GDN_HP, nc, LANES, LANES), lambda h, s: (h, order(s), 0, 0))
    return blk, row, st


def _gdn_fwd(q, k, v, gcb, gct, bb, *, name):
    t = q.shape[0]
    ts = _tile(t, GDN_TILE)
    npair = ts // PAIR

    def body(q_ref, k_ref, v_ref, g_ref, gt_ref, b_ref, o_ref, st_ref, s_sc):
        @pl.when(pl.program_id(1) == 0)
        def _():
            s_sc[...] = jnp.zeros_like(s_sc)

        def pair(pi, _):
            rows = pl.ds(pl.multiple_of(pi * PAIR, PAIR), PAIR)
            heads = [slice(hh * LANES, (hh + 1) * LANES) for hh in range(GDN_HP)]
            c = CHUNK
            cat0 = lambda *xs: jnp.concatenate(xs, axis=0)
            s0 = [s_sc[hh] for hh in range(GDN_HP)]
            cm = _pair_common([q_ref[rows, sl] for sl in heads], [k_ref[rows, sl] for sl in heads],
                              [v_ref[rows, sl] for sl in heads], [g_ref[rows, sl] for sl in heads],
                              [gt_ref[hh, :, rows] for hh in range(GDN_HP)], [b_ref[rows, sl] for sl in heads])
            u, w, qd, kd = cm["u"], cm["w"], cm["qd"], cm["kd"]
            r0 = _each(lambda w_, q_, s: _bdot(cat0(w_[:c], q_[:c]), s), w, qd, s0)
            vn_a = _each(lambda u_, r: u_[:c] - r[:c], u, r0)
            s1 = _each(lambda s, gl, k_, vn: s * gl + _bdot(k_[:c], vn, TN), s0, cm["glast_a"], kd, vn_a)
            r1 = _each(lambda w_, q_, s: _bdot(cat0(w_[c:], q_[c:]), s), w, qd, s1)
            vn_b = _each(lambda u_, r: u_[c:] - r[:c], u, r1)
            s2 = _each(lambda s, gl, k_, vn: s * gl + _bdot(k_[c:], vn, TN), s1, cm["glast_b"], kd, vn_b)
            o = _each(lambda ra, rb, at, va, vb_: cat0(ra[c:], rb[c:]) + _bdot(at, cat0(va, vb_)),
                      r0, r1, cm["attn"], vn_a, vn_b)
            for hh, sl in enumerate(heads):
                st_ref[hh, 2 * pi] = s0[hh]
                st_ref[hh, 2 * pi + 1] = s1[hh]
                s_sc[hh] = s2[hh]
                o_ref[rows, sl] = o[hh]
            return 0

        lax.fori_loop(0, npair, pair, 0)

    blk, row, st = _gdn_specs(t, ts, lambda s: s)
    return pl.pallas_call(
        body, name=name, grid=(GDN_HEADS // GDN_HP, t // ts), in_specs=[blk, blk, blk, blk, row, blk],
        out_specs=[blk, st],
        out_shape=[jax.ShapeDtypeStruct((t, 1024), F32), jax.ShapeDtypeStruct((GDN_HEADS, t // CHUNK, LANES, LANES), F32)],
        scratch_shapes=[pltpu.VMEM((GDN_HP, LANES, LANES), F32)],
        compiler_params=_params(("parallel", "arbitrary")),
    )(q, k, v, gcb, gct, bb)


def _gdn_bwd(q, k, v, gcb, gct, bb, do, states, *, name):
    t = q.shape[0]
    ts = _tile(t, GDN_TILE)
    npair = ts // PAIR
    ns = t // ts
    c = CHUNK

    def body(q_ref, k_ref, v_ref, g_ref, gt_ref, b_ref, do_ref, st_ref, dq_ref, dk_ref, dv_ref, dg_ref, db_ref, ds_sc):
        @pl.when(pl.program_id(1) == 0)
        def _():
            ds_sc[...] = jnp.zeros_like(ds_sc)

        rowsum = lambda x: jnp.sum(x, axis=-1, keepdims=True)
        total = lambda x: jnp.sum(rowsum(x), axis=0, keepdims=True)
        cat0 = lambda *xs: jnp.concatenate(xs, axis=0)
        cat1 = lambda *xs: jnp.concatenate(xs, axis=1)

        def pair(step, _):
            pi = npair - 1 - step
            rows = pl.ds(pl.multiple_of(pi * PAIR, PAIR), PAIR)
            heads = [slice(hh * LANES, (hh + 1) * LANES) for hh in range(GDN_HP)]
            hs = range(GDN_HP)
            qv, kv, vv = ([r[rows, sl] for sl in heads] for r in (q_ref, k_ref, v_ref))
            beta = [b_ref[rows, sl] for sl in heads]
            dov = [do_ref[rows, sl] for sl in heads]
            s0 = [st_ref[hh, 2 * pi] for hh in hs]
            s1 = [st_ref[hh, 2 * pi + 1] for hh in hs]
            ds2 = [ds_sc[hh] for hh in hs]
            cm = _pair_common(qv, kv, vv, [g_ref[rows, sl] for sl in heads], [gt_ref[hh, :, rows] for hh in hs], beta)
            u, w, qd, kd, attn = cm["u"], cm["w"], cm["qd"], cm["kd"], cm["attn"]
            tmat, gamma, eg = cm["tm"], cm["gamma"], cm["eg"]
            incl, strict = cm["incl"], cm["strict"]
            vn_a = _each(lambda u_, w_, s: u_[:c] - _bdot(w_[:c], s), u, w, s0)
            vn_b = _each(lambda u_, w_, s: u_[c:] - _bdot(w_[c:], s), u, w, s1)
            vn = _each(cat0, vn_a, vn_b)
            dvn_att = _each(lambda a, d: _bdot(a, d, TN), attn, dov)
            dattn = _each(lambda d, v_: jnp.where(incl, _bdot(d, v_, NT), 0.0), dov, vn)
            dvn_b = _each(lambda x, k_, d: x[c:] + _bdot(k_[c:], d), dvn_att, kd, ds2)
            rb = _each(lambda d, x, s: _bdot(cat0(d[c:], x), s, NT), dov, dvn_b, s1)
            dkd_b = _each(lambda v_, d: _bdot(v_, d, NT), vn_b, ds2)
            dgl_b = _each(lambda d, s: total(d * s), ds2, s1)
            ds1 = _each(lambda d, gl, q_, w_, o_, x: d * gl + _bdot(cat0(q_[c:], w_[c:]), cat0(o_[c:], -x), TN),
                        ds2, cm["glast_b"], qd, w, dov, dvn_b)
            dvn_a = _each(lambda x, k_, d: x[:c] + _bdot(k_[:c], d), dvn_att, kd, ds1)
            ra = _each(lambda d, x, s: _bdot(cat0(d[:c], x), s, NT), dov, dvn_a, s0)
            dkd_a = _each(lambda v_, d: _bdot(v_, d, NT), vn_a, ds1)
            dgl_a = _each(lambda d, s: total(d * s), ds1, s0)
            ds0 = _each(lambda d, gl, q_, w_, o_, x: d * gl + _bdot(cat0(q_[:c], w_[:c]), cat0(o_[:c], -x), TN),
                        ds1, cm["glast_a"], qd, w, dov, dvn_a)
            dvn = _each(cat0, dvn_a, dvn_b)
            dqd = _each(lambda a, b: cat0(a[:c], b[:c]), ra, rb)
            dw = _each(lambda a, b: -cat0(a[c:], b[c:]), ra, rb)
            dkd = _each(cat0, dkd_a, dkd_b)
            dvw = _each(cat1, dvn, dw)
            dvbk = _each(lambda t_, x: _bdot(t_, x, TN), tmat, dvw)
            dvb = _each(lambda x: x[:, :LANES], dvbk)
            dkbe = _each(lambda x: x[:, LANES:], dvbk)
            dt_ = _each(lambda x, a, b: _bdot(x, cat1(a, b), NT), dvw, cm["vb"], cm["kbe"])
            da1 = _each(lambda t_, x: _bdot(t_, x, TN), tmat, dt_)
            dm = _each(lambda x, t_: jnp.where(strict, -_bdot(x, t_, NT), 0.0), da1, tmat)
            dkk = _each(jnp.multiply, dm, gamma)
            dqk = _each(jnp.multiply, dattn, gamma)
            z = _each(lambda a, b, c_, d: a * b + c_ * d, dm, cm["m"], dattn, attn)
            dkb = _each(lambda x, k_, y, e: _bdot(x, k_) + y * e, dkk, kv, dkbe, eg)
            dk = _each(lambda a, b, kb_, q_, x, e, y, be: _bdot(cat0(a, b), cat0(kb_, q_), TN) + x * e + y * be,
                       dkk, dqk, cm["kb"], qv, dkd, cm["ek"], dkb, beta)
            dq = _each(lambda x, k_, y, e: _bdot(x, k_) + y * e, dqk, kv, dqd, eg)

            def colsum_of(z_):
                zh = z_.astype(BF16)
                zl = (z_ - zh.astype(F32)).astype(BF16)
                return _dot(cat0(zh, zl), jnp.ones((2 * PAIR, LANES), BF16), TN)

            colsum = _each(colsum_of, z)
            ri = lax.broadcasted_iota(jnp.int32, (PAIR, LANES), 0)
            for hh, sl in enumerate(heads):
                dkd_kd = dkd[hh] * kd[hh]
                dgc = (rowsum(z[hh]) - colsum[hh] + rowsum(dqd[hh] * qd[hh]) - rowsum(dkd_kd)
                       + rowsum(dkbe[hh] * cm["kbe"][hh]))
                last_a = total(dkd_kd[:c]) + dgl_a[hh] * cm["glast_a"][hh]
                last_b = total(dkd_kd[c:]) + dgl_b[hh] * cm["glast_b"][hh]
                dgc = dgc + jnp.where(ri == c - 1, last_a, 0.0) + jnp.where(ri == PAIR - 1, last_b, 0.0)
                ds_sc[hh] = ds0[hh]
                dq_ref[rows, sl] = dq[hh]
                dk_ref[rows, sl] = dk[hh]
                dv_ref[rows, sl] = dvb[hh] * beta[hh]
                db_ref[rows, sl] = jnp.broadcast_to(rowsum(dkb[hh] * kv[hh]) + rowsum(dvb[hh] * vv[hh]), (PAIR, LANES))
                dg_ref[rows, sl] = dgc
            return 0

        lax.fori_loop(0, npair, pair, 0)

    blk, row, st = _gdn_specs(t, ts, lambda s: ns - 1 - s)
    out = jax.ShapeDtypeStruct((t, 1024), F32)
    return pl.pallas_call(
        body, name=name, grid=(GDN_HEADS // GDN_HP, ns), in_specs=[blk, blk, blk, blk, row, blk, blk, st],
        out_specs=[blk] * 5, out_shape=[out] * 5, scratch_shapes=[pltpu.VMEM((GDN_HP, LANES, LANES), F32)],
        compiler_params=_params(("parallel", "arbitrary")),
    )(q, k, v, gcb, gct, bb, do, states)


def _gate_out_proj_loss(o, proj_c, o_norm, w, hres, g, target, *, name):
    t, d = hres.shape
    tm = _tile(t, 2 * ROW_TILE)

    def body(o_ref, z_ref, on_ref, w_ref, h_ref, g_ref, t_ref, dh_ref, dhb_ref, y_ref, dg_ref, loss_ref):
        i = pl.program_id(0)
        for hd in range(GDN_HEADS):
            sl = slice(hd * LANES, (hd + 1) * LANES)
            ov = o_ref[:, sl]
            rr = lax.rsqrt(jnp.mean(ov * ov, axis=-1, keepdims=True) + RMS_EPS)
            z = z_ref[:, sl]
            y_ref[:, sl] = (ov * rr * on_ref[...] * (z * _sigmoid(z))).astype(BF16)
        x = h_ref[...] + _dot(y_ref[...], w_ref[...])
        r = lax.rsqrt(jnp.mean(x * x, axis=-1, keepdims=True) + RMS_EPS)
        xh = x * r
        err = xh * g_ref[...] - t_ref[...]
        dy = err * (1.0 / d)
        dxh = dy * g_ref[...]
        dh = r * (dxh - xh * jnp.mean(dxh * xh, axis=-1, keepdims=True))
        dh_ref[...] = dh
        dhb_ref[...] = dh.astype(BF16)

        @pl.when(i == 0)
        def _():
            dg_ref[...] = jnp.zeros_like(dg_ref)
            loss_ref[...] = jnp.zeros_like(loss_ref)

        dg_ref[...] += jnp.sum(dy * xh, axis=0, keepdims=True)
        part = 0.5 * jnp.sum(jnp.mean(err * err, axis=-1, keepdims=True), axis=0, keepdims=True)
        loss_ref[...] += jnp.broadcast_to(part, loss_ref.shape)

    row = pl.BlockSpec((tm, d), lambda i: (i, 0))
    vec = pl.BlockSpec((1, d), lambda i: (0, 0))
    return pl.pallas_call(
        body, name=name, grid=(t // tm,),
        in_specs=[row, pl.BlockSpec((tm, 1024), lambda i: (i, 3)), pl.BlockSpec((1, LANES), lambda i: (0, 0)),
                  pl.BlockSpec(w.shape, lambda i: (0, 0)), row, vec, row],
        out_specs=[row, row, row, vec, pl.BlockSpec((8, LANES), lambda i: (0, 0))],
        out_shape=[jax.ShapeDtypeStruct((t, d), F32), jax.ShapeDtypeStruct((t, d), BF16), jax.ShapeDtypeStruct((t, d), BF16),
                   jax.ShapeDtypeStruct((1, d), F32), jax.ShapeDtypeStruct((8, LANES), F32)],
        compiler_params=_params(("arbitrary",)),
    )(o, proj_c, o_norm, w, hres, g, target)


def _pad_cols(w, n):
    return jnp.pad(w, ((0, 0), (0, n - w.shape[1])))


def _layout_odd(w):
    return dict(
        winc=_pad_cols(w["w_in_c"], IN_C_PAD).astype(BF16), wout_c=w["w_out_c"].astype(BF16), conv_w=w["conv_w"],
        a_log=_pad_cols(w["a_log"], LANES), dt_bias=_pad_cols(w["dt_bias"], LANES),
        norm_c=w["norm_c"], o_norm=w["o_norm"], final_norm=w["final_norm"],
    )


def _layout_in_ab(w):
    z = lambda r, c: jnp.zeros((r, c), w["w_in_ab"].dtype)
    wi = w["w_in_ab"]
    win = jnp.concatenate([wi[:, :384], z(1024, 64), wi[:, 384:416], z(1024, 32), wi[:, 416:]], axis=1)
    pw = w["pool_w"]
    rows = []
    for g in range(4):
        rows.append(jnp.concatenate([pw[g] if j == g else jnp.zeros((128, 128), F32) for j in range(4)], axis=1))
    wpool = jnp.concatenate(rows, axis=0)
    half = MLA_ROPE // 2
    inv = 1.0 / (ROPE_THETA ** (jnp.arange(half, dtype=F32) / half))
    inv_lane = jnp.concatenate([jnp.zeros((MLA_NOPE,), F32), inv, inv, jnp.zeros((32,), F32)]).reshape(1, LANES)
    return dict(win=win.astype(BF16), wpool=wpool.astype(BF16), inv_lane=inv_lane, norm_ab=w["norm_ab"],
                q_a_norm=w["q_a_norm"], kv_a_norm=w["kv_a_norm"], pool_scale=w["pool_scale"])


def _layout_mid(w):
    wq = jnp.pad(w["w_q_b"].reshape(MLA_Q_RANK, MLA_HEADS, 96), ((0, 0), (0, 0), (0, 32))).reshape(MLA_Q_RANK, 1024)
    kv3 = w["w_kv_b"].reshape(MLA_KV_RANK, MLA_HEADS, 128)
    wk = jnp.pad(kv3[..., :MLA_NOPE], ((0, 0), (0, 0), (0, 64))).reshape(MLA_KV_RANK, 1024)
    wv = kv3[..., MLA_NOPE:].reshape(MLA_KV_RANK, 512)
    return dict(wq=wq.astype(BF16), wk=wk.astype(BF16), wv=wv.astype(BF16), wout_ab=w["w_out_ab"].astype(BF16))


def _unlayout_grads(g, names):
    out = {}
    for name in names:
        if name == "w_in_ab":
            dwin = g["win"]
            out[name] = jnp.concatenate([dwin[:, :384], dwin[:, 448:480], dwin[:, 512:]], axis=1)
        elif name == "w_q_b":
            out[name] = g["wq"].reshape(MLA_Q_RANK, MLA_HEADS, 128)[..., :96].reshape(MLA_Q_RANK, 768)
        elif name == "w_kv_b":
            out[name] = jnp.concatenate([g["wk"].reshape(MLA_KV_RANK, MLA_HEADS, 128)[..., :MLA_NOPE],
                                         g["wv"].reshape(MLA_KV_RANK, MLA_HEADS, MLA_V)], axis=-1).reshape(MLA_KV_RANK, 1024)
        elif name == "w_in_c":
            out[name] = g["winc"][:, :4112]
        else:
            out[name] = g[{"w_out_ab": "wout_ab", "w_out_c": "wout_c"}[name]]
    return out


def _local_step(x, pos, target, lw, more_weights=None, on_grads=None):
    mm = _matmul
    proj, hn = _rms_in_proj(x, lw["norm_ab"], lw["win"], name="rms_in_ab")
    if more_weights is not None:
        lw = {**lw, **more_weights("mid", proj)}
    q, k, v, ybraw, qn, kvn, d, cos_t, sin_t = _ab_prep(
        proj, pos, lw["inv_lane"], lw["q_a_norm"], lw["kv_a_norm"], lw["wq"], lw["wk"], lw["wv"], lw["wpool"], name="ab_prep")
    o, lse = _attn_fwd(q, k, v, name="attn_fwd")
    h1, y = _gate_out_proj(o, ybraw, proj, lw["pool_scale"], lw["wout_ab"], x, name="gate_out_ab")
    lo = lw if more_weights is None else more_weights("odd", h1)
    proj_c, hn1 = _rms_in_proj(h1, lo["norm_c"], lo["winc"], name="rms_in_c")
    q2, k2, v2, gb, bb, gt = _c_prep(proj_c, lo["conv_w"], lo["a_log"], lo["dt_bias"], name="c_prep")
    gt = gt.reshape(GDN_HEADS, 1, gt.shape[1])
    o2, states = _gdn_fwd(q2, k2, v2, gb, gt, bb, name="gdn_fwd")
    dh2, dh2b, y2, d_final, loss = _gate_out_proj_loss(
        o2, proj_c, lo["o_norm"], lo["wout_c"], h1, lo["final_norm"], target, name="gate_out_c_loss")
    g = {"final_norm": d_final}
    dy2 = mm(dh2b, lo["wout_c"], "nt", name="out_c_dx")
    g["wout_c"] = mm(y2, dh2b, "tn", name="out_c_dw")
    do2, dz2, g["o_norm"] = _o_gate_bwd(dy2, o2, proj_c, lo["o_norm"], name="gate_c_bwd")
    dq2, dk2, dv2, dgb, dbb = _gdn_bwd(q2, k2, v2, gb, gt, bb, do2, states, name="gdn_bwd")
    dproj_c, g["conv_w"], g["a_log"], g["dt_bias"] = _c_prep_bwd(
        proj_c, lo["conv_w"], lo["a_log"], lo["dt_bias"], dq2, dk2, dv2, dgb, dbb, dz2, name="c_prep_bwd")
    g["winc"] = mm(hn1, dproj_c, "tn", name="in_c_dw")
    notify = (lambda tag: 0.0) if on_grads is None else (lambda tag: on_grads(tag, g))
    pool_scale = lw["pool_scale"] + notify("odd")
    dh1, dh1b, g["norm_c"] = _matmul_rms_bwd(dproj_c, lo["winc"], h1, lo["norm_c"], dh2, name="in_c_dx_rms", with_bf16=True)
    dy = mm(dh1b, lw["wout_ab"], "nt", name="out_ab_dx")
    g["wout_ab"] = mm(y, dh1b, "tn", name="out_ab_dw")
    pool_scale = pool_scale + notify("out_ab")
    do, delta, dyb, dz, g["pool_scale"] = _gate_bwd(dy, o, ybraw, proj, pool_scale, name="gate_ab_bwd")
    dq, dk, dv = _attn_bwd(q, k, v, do, lse, delta, name="attn_bwd")
    dproj, dqraw, dkb, g["q_a_norm"], g["kv_a_norm"] = _ab_prep_bwd(
        proj, lw["q_a_norm"], lw["kv_a_norm"], dq, dk, dv, cos_t, sin_t, dyb, dz,
        lw["wq"], lw["wk"], lw["wv"], lw["wpool"], name="ab_prep_bwd")
    g["wpool"] = mm(d, dyb, "tn", name="pool_mix_dw")
    g["wq"] = mm(qn, dqraw, "tn", name="q_up_dw")
    g["wk"] = mm(kvn, dkb, "tn", name="k_up_dw")
    g["wv"] = mm(kvn, dv, "tn", name="v_up_dw")
    g["win"] = mm(hn, dproj, "tn", name="in_ab_dw")
    norm_ab = lw["norm_ab"] + notify("in_ab")
    dx, g["norm_ab"] = _matmul_rms_bwd(dproj, lw["win"], x, norm_ab, dh1, name="in_ab_dx_rms", with_bf16=False)
    return loss, dx, g


_HBM = pl.BlockSpec(memory_space=pltpu.HBM)


def _place():
    return lax.axis_index("x"), lax.axis_index("y"), lax.axis_index("c")


def _flip(v, f):
    return 1 - v if f else v


_CHIP_FLIPS = ((1, 0), (0, 1), (1, 1))
_DEV_FLIPS = tuple((fx, fy, fc) for fx in (0, 1) for fy in (0, 1) for fc in (0, 1) if fx or fy or fc)


def _rcopy(src, dst, send_sems, recv_sems, k, to):
    return pltpu.make_async_remote_copy(src_ref=src, dst_ref=dst, send_sem=send_sems.at[k], recv_sem=recv_sems.at[k],
                                        device_id=to, device_id_type=MESH)


def _my_half(ref, c, axis):
    rh = ref.shape[axis] // 2
    idx = [slice(None)] * len(ref.shape)
    idx[axis] = pl.ds(c * rh, rh)
    return ref.at[tuple(idx)]


def _gather_weights(bigs, smalls):
    nb, ns = len(bigs), len(smalls)

    def body(*refs):
        ins, outs = refs[:nb + ns], refs[nb + ns:2 * (nb + ns)]
        send_sems, recv_sems, local_sems = refs[2 * (nb + ns):]
        x, y, c = _place()
        j0 = 2 * x + y
        sib = (x, y, 1 - c)
        chips = [(_flip(x, fx), _flip(y, fy)) for fx, fy in _CHIP_FLIPS]
        local = [pltpu.make_async_copy(i_ref, o_ref.at[j0], local_sems.at[a])
                 for a, (i_ref, o_ref) in enumerate(zip(ins, outs))]
        for cp in local:
            cp.start()
        sends = []
        for k, (px, py) in enumerate(chips):
            for a in range(nb):
                sends.append(_rcopy(_my_half(ins[a], c, 0), _my_half(outs[a].at[j0], c, 0), send_sems, recv_sems,
                                    6 * a + k, (px, py, c)))
            for s in range(ns):
                sends.append(_rcopy(ins[nb + s], outs[nb + s].at[j0], send_sems, recv_sems, 6 * nb + 3 * s + k, (px, py, c)))
        for cp in sends:
            cp.start()
        for k, (px, py) in enumerate(chips):
            jk = 2 * px + py
            for a in range(nb):
                landed = _my_half(outs[a].at[jk], c, 0)
                _rcopy(landed, landed, send_sems, recv_sems, 6 * a + k, (px, py, c)).wait_recv()
                fwd = _rcopy(landed, landed, send_sems, recv_sems, 6 * a + 3 + k, sib)
                fwd.start()
                sends.append(fwd)
        for k, (px, py) in enumerate(chips):
            jk = 2 * px + py
            for a in range(nb):
                other = _my_half(outs[a].at[jk], 1 - c, 0)
                _rcopy(other, other, send_sems, recv_sems, 6 * a + 3 + k, sib).wait_recv()
            for s in range(ns):
                _rcopy(ins[nb + s], outs[nb + s].at[jk], send_sems, recv_sems, 6 * nb + 3 * s + k, (px, py, c)).wait_recv()
        for cp in sends:
            cp.wait_send()
        for cp in local:
            cp.wait()

    arrays = list(bigs) + list(smalls)
    n_sem = 6 * nb + 3 * ns
    return pl.pallas_call(
        body, name="gather_weights", in_specs=[_HBM] * len(arrays), out_specs=[_HBM] * len(arrays),
        out_shape=[jax.ShapeDtypeStruct((4,) + a.shape, a.dtype) for a in arrays],
        scratch_shapes=[pltpu.SemaphoreType.DMA((n_sem,)), pltpu.SemaphoreType.DMA((n_sem,)),
                        pltpu.SemaphoreType.DMA((len(arrays),))],
    )(*arrays)


def _core_swap_partial(gs, *, name):
    n = len(gs)

    def body(*refs):
        ins, outs = refs[:n], refs[n:2 * n]
        send_sems, recv_sems = refs[2 * n:]
        x, y, c = _place()
        copies = [_rcopy(_my_half(i_ref, 1 - c, 1), o_ref, send_sems, recv_sems, a, (x, y, 1 - c))
                  for a, (i_ref, o_ref) in enumerate(zip(ins, outs))]
        for cp in copies:
            cp.start()
        for cp in copies:
            cp.wait()

    return pl.pallas_call(
        body, name=name, in_specs=[_HBM] * n, out_specs=[_HBM] * n,
        out_shape=[jax.ShapeDtypeStruct((4, g.shape[1] // 2, g.shape[2]), g.dtype) for g in gs],
        scratch_shapes=[pltpu.SemaphoreType.DMA((n,)), pltpu.SemaphoreType.DMA((n,))],
    )(*gs)


def _core_swap_sum(fs):
    n = len(fs)

    def body(*refs):
        ins, outs = refs[:n], refs[n:2 * n]
        send_sems, recv_sems = refs[2 * n:]
        x, y, c = _place()
        copies = [_rcopy(_my_half(i_ref, c, 0), _my_half(o_ref, c, 0), send_sems, recv_sems, a, (x, y, 1 - c))
                  for a, (i_ref, o_ref) in enumerate(zip(ins, outs))]
        for cp in copies:
            cp.start()
        for a, cp in enumerate(copies):
            cp.wait_send()
            theirs = _my_half(outs[a], 1 - c, 0)
            _rcopy(theirs, theirs, send_sems, recv_sems, a, (x, y, 1 - c)).wait_recv()

    return pl.pallas_call(
        body, name="core_swap_sum", in_specs=[_HBM] * n, out_specs=[_HBM] * n,
        out_shape=[jax.ShapeDtypeStruct(f.shape, f.dtype) for f in fs],
        input_output_aliases={a: a for a in range(n)},
        scratch_shapes=[pltpu.SemaphoreType.DMA((n,)), pltpu.SemaphoreType.DMA((n,))],
    )(*fs)


_SEM = pl.BlockSpec(memory_space=pltpu.SEMAPHORE)
_ANY = pl.BlockSpec(memory_space=pl.ANY)
_DATAFLOW = pltpu.SideEffectType.DATAFLOW_SIDE_EFFECTING


def _to_chips_copies(srcs, lands, send_sems, recv_sems, per_chip_slot):
    x, y, c = _place()
    j0 = 2 * x + y
    out = []
    for k, (fx, fy) in enumerate(_CHIP_FLIPS):
        px, py = _flip(x, fx), _flip(y, fy)
        jk = 2 * px + py
        for a, (src, land) in enumerate(zip(srcs, lands)):
            piece = src.at[jk] if per_chip_slot else src
            out.append((_rcopy(piece, land.at[j0], send_sems, recv_sems, 3 * a + k, (px, py, c)),
                        _rcopy(piece, land.at[jk], send_sems, recv_sems, 3 * a + k, (px, py, c))))
    return out


def _to_chips_start(arrays, *, per_chip_slot, name, after=None):
    n = len(arrays)
    lands = [lax.empty((4,) + (a.shape[1:] if per_chip_slot else a.shape), a.dtype) for a in arrays]
    extra = [] if after is None else [after]

    def body(*refs):
        srcs, land_refs, token = refs[:n], refs[n:2 * n], refs[-1]
        send_sems, recv_sems = refs[2 * n + len(extra)], refs[2 * n + len(extra) + 1]
        for send, _ in _to_chips_copies(srcs, land_refs, send_sems, recv_sems, per_chip_slot):
            send.start()
        token[...] = jnp.zeros_like(token)

    held = [pltpu.with_memory_space_constraint(a, pltpu.HBM) for a in list(arrays) + lands]
    return pl.pallas_call(
        body, name=name, in_specs=[_HBM] * (2 * n) + [_ANY] * len(extra),
        out_specs=(_SEM, _SEM, *[_HBM] * (2 * n), pl.BlockSpec(memory_space=pltpu.VMEM)),
        out_shape=(pltpu.SemaphoreType.DMA((3 * n,)), pltpu.SemaphoreType.DMA((3 * n,)),
                   *[pltpu.HBM(a.shape, a.dtype) for a in held], jax.ShapeDtypeStruct((8, LANES), F32)),
        input_output_aliases={i: 2 + i for i in range(2 * n)},
        compiler_params=pltpu.CompilerParams(has_side_effects=_DATAFLOW),
    )(*held, *extra)


def _to_chips_wait(started, after, *, per_chip_slot, name):
    send_sems, recv_sems, held = started[0], started[1], started[2:-1]
    n = len(held) // 2

    def body(*refs):
        srcs, land_refs, s_sems, r_sems = refs[:n], refs[n:2 * n], refs[2 * n], refs[2 * n + 1]
        for send, arrival in _to_chips_copies(srcs, land_refs, s_sems, r_sems, per_chip_slot):
            send.wait_send()
            arrival.wait_recv()

    out = pl.pallas_call(
        body, name=name, in_specs=[_HBM] * (2 * n) + [_SEM, _SEM, _ANY], out_specs=[_HBM] * (2 * n),
        out_shape=[pltpu.HBM(a.shape, a.dtype) for a in held],
        input_output_aliases={i: i for i in range(2 * n)},
        compiler_params=pltpu.CompilerParams(has_side_effects=_DATAFLOW),
    )(*held, send_sems, recv_sems, after)
    return out[n:]


def _chip_exchange(ps, small):
    n = len(ps)
    rs = small.shape[0]

    def body(*refs):
        p_refs, s_ref = refs[:n], refs[n]
        l_refs, ls_ref = refs[n + 1:2 * n + 1], refs[2 * n + 1]
        send_sems, recv_sems, local_sems = refs[2 * n + 2:]
        x, y, c = _place()
        j0 = 2 * x + y
        d0 = 2 * j0 + c
        local = [pltpu.make_async_copy(p.at[j0], l.at[j0], local_sems.at[a]) for a, (p, l) in enumerate(zip(p_refs, l_refs))]
        local.append(pltpu.make_async_copy(s_ref, ls_ref.at[d0], local_sems.at[n]))
        for cp in local:
            cp.start()
        sends = []
        for k, (fx, fy) in enumerate(_CHIP_FLIPS):
            px, py = _flip(x, fx), _flip(y, fy)
            for a in range(n):
                sends.append(_rcopy(p_refs[a].at[2 * px + py], l_refs[a].at[j0], send_sems, recv_sems, 3 * a + k, (px, py, c)))
        for k, (fx, fy, fc) in enumerate(_DEV_FLIPS):
            peer = (_flip(x, fx), _flip(y, fy), _flip(c, fc))
            sends.append(_rcopy(s_ref, ls_ref.at[d0], send_sems, recv_sems, 3 * n + k, peer))
        for cp in sends:
            cp.start()
        for k, (fx, fy) in enumerate(_CHIP_FLIPS):
            px, py = _flip(x, fx), _flip(y, fy)
            for a in range(n):
                _rcopy(p_refs[a].at[j0], l_refs[a].at[2 * px + py], send_sems, recv_sems, 3 * a + k, (px, py, c)).wait_recv()
        for k, (fx, fy, fc) in enumerate(_DEV_FLIPS):
            px, py, pc = _flip(x, fx), _flip(y, fy), _flip(c, fc)
            _rcopy(s_ref, ls_ref.at[4 * px + 2 * py + pc], send_sems, recv_sems, 3 * n + k, (px, py, pc)).wait_recv()
        for cp in sends:
            cp.wait_send()
        for cp in local:
            cp.wait()

    n_sem = 3 * n + 7
    return pl.pallas_call(
        body, name="chip_exchange", in_specs=[_HBM] * (n + 1), out_specs=[_HBM] * (n + 1),
        out_shape=[jax.ShapeDtypeStruct(p.shape, F32) for p in ps] + [jax.ShapeDtypeStruct((8, rs, LANES), F32)],
        scratch_shapes=[pltpu.SemaphoreType.DMA((n_sem,)), pltpu.SemaphoreType.DMA((n_sem,)),
                        pltpu.SemaphoreType.DMA((n + 1,))],
    )(*ps, small)


def _core_sum(g, part, core, *, name):
    _, rh, cols = part.shape
    tr = _tile(rh, 256)
    nb = rh // tr

    def body(c_ref, g_ref, p_ref, o_ref):
        o_ref[...] = g_ref[...] + p_ref[...]

    grid_spec = pltpu.PrefetchScalarGridSpec(
        num_scalar_prefetch=1, grid=(4, nb),
        in_specs=[pl.BlockSpec((1, tr, cols), lambda j, i, c: (j, c[0] * nb + i, 0)),
                  pl.BlockSpec((1, tr, cols), lambda j, i, c: (j, i, 0))],
        out_specs=pl.BlockSpec((1, tr, cols), lambda j, i, c: (j, i, 0)),
    )
    return pl.pallas_call(
        body, name=name, grid_spec=grid_spec, out_shape=jax.ShapeDtypeStruct(part.shape, F32),
        compiler_params=_params(("parallel", "parallel")),
    )(core, g, part)


def _chip_sum(landed, core, *, name):
    _, rh, cols = landed.shape
    tr = _tile(rh, 256)
    nb = rh // tr

    def body(c_ref, l_ref, o_ref):
        o_ref[...] = ((l_ref[0] + l_ref[1]) + l_ref[2]) + l_ref[3]

    grid_spec = pltpu.PrefetchScalarGridSpec(
        num_scalar_prefetch=1, grid=(nb,),
        in_specs=[pl.BlockSpec((4, tr, cols), lambda i, c: (0, i, 0))],
        out_specs=pl.BlockSpec((tr, cols), lambda i, c: (c[0] * nb + i, 0)),
    )
    return pl.pallas_call(
        body, name=name, grid_spec=grid_spec, out_shape=jax.ShapeDtypeStruct((2 * rh, cols), F32),
        compiler_params=_params(("parallel",)),
    )(core, landed)


_ROW_POOL_W, _ROW_NORM_AB, _ROW_FINAL, _ROW_POOL_SCALE, _ROW_Q_NORM = 0, 512, 520, 528, 532
_ROW_KV_NORM, _ROW_O_NORM, _ROW_A_LOG, _ROW_DT_BIAS, _ROW_LOSS = 534, 535, 536, 537, 538
_ROW_CONV, _ROW_NORM_C, _SMALL_ROWS = 544, 640, 672
_CONV_ROWS = CONV_WIDTH * 6


def _put_rows(dst_ref, row0, src, width):
    for r in range(width // LANES):
        dst_ref[row0 + r:row0 + r + 1, :] = src[:, r * LANES:(r + 1) * LANES]


def _pack_small(g, loss_tile):
    names = ("wpool", "norm_ab", "final_norm", "pool_scale", "q_a_norm", "kv_a_norm", "o_norm", "a_log", "dt_bias",
             "conv_w", "norm_c")

    def body(wpool, norm_ab, final_norm, pool_scale, q_norm, kv_norm, o_norm, a_log, dt_bias, conv_w, norm_c, loss, o_ref):
        o_ref[...] = jnp.zeros_like(o_ref)
        for gi in range(4):
            o_ref[_ROW_POOL_W + gi * 128:_ROW_POOL_W + (gi + 1) * 128, :] = wpool[gi * 128:(gi + 1) * 128, gi * 128:(gi + 1) * 128]
        _put_rows(o_ref, _ROW_NORM_AB, norm_ab[...], 1024)
        _put_rows(o_ref, _ROW_FINAL, final_norm[...], 1024)
        _put_rows(o_ref, _ROW_POOL_SCALE, pool_scale[...], 512)
        _put_rows(o_ref, _ROW_Q_NORM, q_norm[...], 256)
        for row, ref in ((_ROW_KV_NORM, kv_norm), (_ROW_O_NORM, o_norm), (_ROW_A_LOG, a_log), (_ROW_DT_BIAS, dt_bias)):
            o_ref[row:row + 1, :] = ref[...]
        o_ref[_ROW_LOSS:_ROW_LOSS + 1, :] = loss[0:1, :]
        for j in range(4):
            for r in range(CONV_WIDTH):
                _put_rows(o_ref, _ROW_CONV + j * _CONV_ROWS + r * 6, conv_w[r:r + 1, j * 768:(j + 1) * 768], 768)
            _put_rows(o_ref, _ROW_NORM_C + j * 8, norm_c[:, j * 256:(j + 1) * 256], 256)

    vmem = pl.BlockSpec(memory_space=pltpu.VMEM)
    return pl.pallas_call(
        body, name="pack_small", in_specs=[vmem] * 12, out_specs=vmem,
        out_shape=jax.ShapeDtypeStruct((_SMALL_ROWS, LANES), F32),
    )(*[g[n] for n in names], loss_tile)


_SMALL_NAMES = ("pool_w", "norm_ab", "final_norm", "pool_scale", "q_a_norm", "kv_a_norm", "o_norm", "a_log", "dt_bias",
                "conv_w", "norm_c")


def _take_rows(src, row0, width):
    return jnp.concatenate([src[row0 + r:row0 + r + 1, :] for r in range(width // LANES)], axis=1)


def _small_update(small_all, ws, ms, vs):
    n = len(_SMALL_NAMES)

    def body(*refs):
        a_ref = refs[0]
        w_refs, m_refs, v_refs = refs[1:1 + n], refs[1 + n:1 + 2 * n], refs[1 + 2 * n:1 + 3 * n]
        outs = refs[1 + 3 * n:1 + 7 * n]
        loss_ref, tot = refs[1 + 7 * n], refs[2 + 7 * n]
        acc = a_ref[0]
        for d in range(1, 8):
            acc = acc + a_ref[d]
        tot[...] = acc
        x, y, _ = _place()
        j0 = 2 * x + y
        conv = tot[pl.ds(pl.multiple_of(_ROW_CONV + j0 * _CONV_ROWS, 8), _CONV_ROWS), :]
        norm_c = tot[pl.ds(pl.multiple_of(_ROW_NORM_C + j0 * 8, 8), 8), :]
        whole = tot[_ROW_NORM_AB:_ROW_CONV, :]
        at = lambda row: row - _ROW_NORM_AB
        grads = {
            "norm_ab": _take_rows(whole, at(_ROW_NORM_AB), 1024), "final_norm": _take_rows(whole, at(_ROW_FINAL), 1024),
            "pool_scale": _take_rows(whole, at(_ROW_POOL_SCALE), 512), "q_a_norm": _take_rows(whole, at(_ROW_Q_NORM), 256),
            "kv_a_norm": whole[at(_ROW_KV_NORM):at(_ROW_KV_NORM) + 1, :], "o_norm": whole[at(_ROW_O_NORM):at(_ROW_O_NORM) + 1, :],
            "a_log": tot[_ROW_A_LOG:_ROW_A_LOG + 1, 0:GDN_HEADS],
            "dt_bias": tot[_ROW_DT_BIAS:_ROW_DT_BIAS + 1, 0:GDN_HEADS],
            "norm_c": _take_rows(norm_c, 0, 256),
        }
        loss_ref[...] = whole[at(_ROW_LOSS):at(_ROW_LOSS) + 1, :]
        for i, name in enumerate(_SMALL_NAMES):
            g_out = outs[4 * i]
            if name == "pool_w":
                for gi in range(4):
                    g_out[gi] = tot[_ROW_POOL_W + gi * 128:_ROW_POOL_W + (gi + 1) * 128, :]
            elif name == "conv_w":
                for r in range(CONV_WIDTH):
                    g_out[r:r + 1, :] = _take_rows(conv, r * 6, 768)
            else:
                g_out[...] = grads[name]
            _adam_update(g_out, w_refs[i], m_refs[i], v_refs[i], *outs[4 * i + 1:4 * i + 4])

    vmem = pl.BlockSpec(memory_space=pltpu.VMEM)
    out_shape = [jax.ShapeDtypeStruct(w.shape, F32) for w in ws for _ in range(4)] + [jax.ShapeDtypeStruct((1, LANES), F32)]
    return pl.pallas_call(
        body, name="small_update", in_specs=[vmem] * (1 + 3 * n), out_specs=[vmem] * (4 * n + 1), out_shape=out_shape,
        scratch_shapes=[pltpu.VMEM((_SMALL_ROWS, LANES), F32)],
        compiler_params=pltpu.CompilerParams(vmem_limit_bytes=VMEM_LIMIT),
    )(small_all, *ws, *ms, *vs)


def _adam_update(g_ref, w_ref, m_ref, v_ref, d_ref, mo_ref, vo_ref):
    gv = g_ref[...]
    mn = ADAM_B1 * m_ref[...] + (1.0 - ADAM_B1) * gv
    vn = ADAM_B2 * v_ref[...] + (1.0 - ADAM_B2) * (gv * gv)
    mo_ref[...] = mn
    vo_ref[...] = vn
    c1 = 1.0 - ADAM_B1 ** ADAM_STEP
    c2 = 1.0 - ADAM_B2 ** ADAM_STEP
    d_ref[...] = -ADAM_LR * ((mn / c1) / (jnp.sqrt(vn / c2) + ADAM_EPS) + ADAM_WD * w_ref[...])


def _adamw_rows(g, w, m, v, *, name):
    rows, cols = g.shape
    if rows % LANES == 0:
        tr = _tile(rows, 512)
        blk, steps = pl.BlockSpec((tr, cols), lambda i: (i, 0)), rows // tr
    else:
        tc = _tile(cols, 256)
        blk, steps = pl.BlockSpec((rows, tc), lambda i: (0, i)), cols // tc

    def body(*refs):
        _adam_update(*refs)

    out = jax.ShapeDtypeStruct((rows, cols), F32)
    return pl.pallas_call(
        body, name=name, grid=(steps,), in_specs=[blk] * 4, out_specs=[blk] * 3, out_shape=[out] * 3,
        compiler_params=_params(("parallel",)),
    )(g, w, m, v)


_ADAM_ROWWISE = ("w_in_ab", "w_q_b", "w_kv_b", "w_out_ab", "w_in_c", "w_out_c")


_SHARD_AXIS = {"w_in_ab": 1, "w_q_b": 1, "w_kv_b": 1, "w_out_ab": 0, "w_in_c": 1, "w_out_c": 0, "conv_w": 1, "norm_c": 1}
_ALL_NAMES = ("norm_ab", "w_in_ab", "q_a_norm", "w_q_b", "kv_a_norm", "w_kv_b", "pool_w", "pool_scale", "w_out_ab",
              "norm_c", "w_in_c", "conv_w", "a_log", "dt_bias", "o_norm", "w_out_c", "final_norm")


def _join_shards(a, axis):
    _, r, c = a.shape
    return a.reshape(4 * r, c) if axis == 0 else jnp.transpose(a, (1, 0, 2)).reshape(r, 4 * c)


def _split_shards(a, axis):
    r, c = a.shape
    return a.reshape(4, r // 4, c) if axis == 0 else jnp.transpose(a.reshape(r, 4, c // 4), (1, 0, 2))


def kernel(x, positions, norm_ab, w_in_ab, q_a_norm, w_q_b, kv_a_norm, w_kv_b, pool_w, pool_scale, w_out_ab, norm_c, w_in_c, conv_w, a_log, dt_bias, o_norm, w_out_c, final_norm, loss_target, m_norm_ab, m_w_in_ab, m_q_a_norm, m_w_q_b, m_kv_a_norm, m_w_kv_b, m_pool_w, m_pool_scale, m_w_out_ab, m_norm_c, m_w_in_c, m_conv_w, m_a_log, m_dt_bias, m_o_norm, m_w_out_c, m_final_norm, v_norm_ab, v_w_in_ab, v_q_a_norm, v_w_q_b, v_kv_a_norm, v_w_kv_b, v_pool_w, v_pool_scale, v_w_out_ab, v_norm_c, v_w_in_c, v_conv_w, v_a_log, v_dt_bias, v_o_norm, v_w_out_c, v_final_norm):
    given = dict(locals())
    c = lax.axis_index("c")
    t = x.shape[1]

    def shard_of(prefix, name):
        a = given[prefix + name]
        return a.reshape(a.shape[1:]) if a.ndim > 2 else a.reshape(1, -1)

    big, big_even, big_odd, small_sharded = _ADAM_ROWWISE, _ADAM_ROWWISE[:4], _ADAM_ROWWISE[4:], ("conv_w", "norm_c")
    chip = 2 * lax.axis_index("x") + lax.axis_index("y")
    core = c.astype(jnp.int32).reshape(1)
    later = {"mid": big_even[1:], "odd": big_odd + small_sharded}
    travelling = {}

    def send(tag, after=None):
        shards = [shard_of("", n).astype(BF16) if n in big else shard_of("", n) for n in later[tag]]
        started = _to_chips_start(shards, per_chip_slot=False, name="gather_" + tag + "_start", after=after)
        travelling[tag] = (shards, started)
        return started[-1][0, 0]

    mid_sent = send("mid")
    gathered = _gather_weights([shard_of("", "w_in_ab").astype(BF16)], [])
    full = {"w_in_ab": _join_shards(gathered[0], _SHARD_AXIS["w_in_ab"])}
    for name in ("norm_ab", "q_a_norm", "kv_a_norm", "pool_w", "pool_scale"):
        full[name] = shard_of("", name)
    lw = _layout_in_ab(full)
    lw["norm_ab"] = lw["norm_ab"] + mid_sent

    def more_weights(tag, after):
        shards, started = travelling[tag]
        landed = _to_chips_wait(started, after, per_chip_slot=False, name="gather_" + tag + "_wait")
        w = {}
        for name, land, own in zip(later[tag], landed, shards):
            w[name] = _join_shards(lax.dynamic_update_index_in_dim(land, own, chip, 0), _SHARD_AXIS[name])
        if tag == "mid":
            out = _layout_mid(w)
            out["wq"] = out["wq"] + send("odd", after=landed[0]).astype(BF16)
            return out
        for name in ("a_log", "dt_bias", "o_norm", "final_norm"):
            w[name] = shard_of("", name)
        return _layout_odd(w)

    def chip_partials(names, grads, tag):
        slots = [_split_shards(grads[n], _SHARD_AXIS[n]) for n in names]
        partial = _core_swap_partial(slots, name="core_swap_partial_" + tag)
        return [_core_sum(s, p, core, name="core_sum_" + n) for n, s, p in zip(names, slots, partial)]

    groups = {"odd": big_odd, "out_ab": ("w_out_ab",), "in_ab": ("w_in_ab", "w_q_b", "w_kv_b")}
    sent = {}

    def on_grads(tag, g):
        part = chip_partials(groups[tag], _unlayout_grads(g, groups[tag]), tag)
        sent[tag] = (part, _to_chips_start(part, per_chip_slot=True, name="exchange_" + tag + "_start"))
        return sent[tag][1][-1][0, 0]

    loss_tile, dx, g = _local_step(x[0], positions.reshape(t, 1), loss_target[0], lw, more_weights, on_grads)
    small_all = _chip_exchange([], _pack_small(g, loss_tile))[-1]
    halves = {}
    for tag, names in groups.items():
        part, started = sent[tag]
        landed = _to_chips_wait(started, small_all, per_chip_slot=True, name="exchange_" + tag + "_wait")
        for n, l, p in zip(names, landed, part):
            l = lax.dynamic_update_index_in_dim(l, lax.dynamic_index_in_dim(p, chip, 0, keepdims=False), chip, 0)
            halves[n] = _chip_sum(l, core, name="chip_sum_" + n)
    gbig = dict(zip(big, _core_swap_sum([halves[n] for n in big])))

    res = {}
    for name in big:
        res["grad", name] = gbig[name]
        operands = [gbig[name], shard_of("", name), shard_of("m_", name), shard_of("v_", name)]
        flip = operands[0].shape[1] % LANES != 0
        if flip:
            operands = [jnp.transpose(a) for a in operands]
        out = _adamw_rows(*operands, name="adamw_" + name)
        res["delta", name], res["m", name], res["v", name] = [jnp.transpose(a) for a in out] if flip else out
    out = _small_update(small_all, [shard_of("", n) for n in _SMALL_NAMES], [shard_of("m_", n) for n in _SMALL_NAMES],
                        [shard_of("v_", n) for n in _SMALL_NAMES])
    for i, name in enumerate(_SMALL_NAMES):
        res["grad", name], res["delta", name], res["m", name], res["v", name] = out[4 * i:4 * i + 4]
    res = {k: a.reshape(given[k[1]].shape) for k, a in res.items()}
    loss = out[-1][0, 0]
    outs = [loss, dx.reshape(x.shape)]
    for key in ("grad", "delta", "m", "v"):
        outs += [res[key, n] for n in _ALL_NAMES]
    return tuple(outs)
```

```python
import functools

import jax
import jax.numpy as jnp
from jax import lax
from jax.experimental import pallas as pl
from jax.experimental.pallas import tpu as pltpu

F32 = jnp.float32
BF16 = jnp.bfloat16
HI = lax.Precision.HIGHEST
MESH = pl.DeviceIdType.MESH

RMS_EPS = 1e-6
MLA_HEADS = 8
MLA_Q_RANK = 256
MLA_KV_RANK = 128
MLA_NOPE = 64
MLA_ROPE = 32
MLA_V = 64
ROPE_THETA = 10000.0
POOL_WINDOWS = (2, 4, 8, 16)
POOL_GROUP = 128
POOL_WIDTH = 512
POOL_HALO = 16
GDN_HEADS = 8
GDN_DK = 128
CONV_WIDTH = 4
CONV_HALO = 8
CHUNK = 64
IN_AB_PAD = 2048
IN_C_PAD = 4224
ATT_SCALE = (MLA_NOPE + MLA_ROPE) ** -0.5

ADAM_LR = 0.001
ADAM_B1 = 0.9
ADAM_B2 = 0.999
ADAM_EPS = 1e-08
ADAM_WD = 0.01
ADAM_STEP = 10

LANES = 128
VMEM_LIMIT = 56 * 1024 * 1024

ROW_TILE = 256
ATT_TILE = 1024
GDN_TILE = 256
MM_TILE = (1024, 1408, 2048)

NN = (((1,), (0,)), ((), ()))
NT = (((1,), (1,)), ((), ()))
TN = (((0,), (0,)), ((), ()))


def _dot(a, b, dims=NN, prec=None):
    return lax.dot_general(a, b, dims, precision=prec, preferred_element_type=F32)


def _tile(n, pref):
    if n <= pref:
        return n
    step = LANES if pref >= LANES else 8
    for t in range(pref - pref % step, 0, -step):
        if n % t == 0:
            return t
    return n


def _params(sem):
    return pltpu.CompilerParams(dimension_semantics=sem, vmem_limit_bytes=VMEM_LIMIT)


def _sigmoid(x):
    return 0.5 * jnp.tanh(0.5 * x) + 0.5


def _softplus(x):
    return jnp.maximum(x, 0.0) + jnp.log(1.0 + jnp.exp(-jnp.abs(x)))


def _matmul(a, b, mode, *, name):
    if mode == "nn":
        (m, k), (k2, n) = a.shape, b.shape
    elif mode == "nt":
        (m, k), (n, k2) = a.shape, b.shape
    else:
        (k, m), (k2, n) = a.shape, b.shape
    assert k == k2, (a.shape, b.shape, mode)
    tm, tn, tk = _tile(m, MM_TILE[0]), _tile(n, MM_TILE[1]), _tile(k, MM_TILE[2])
    nk = k // tk
    if mode == "tn":
        a_spec = pl.BlockSpec((tk, tm), lambda i, j, kk: (kk, i))
    else:
        a_spec = pl.BlockSpec((tm, tk), lambda i, j, kk: (i, kk))
    if mode == "nt":
        b_spec = pl.BlockSpec((tn, tk), lambda i, j, kk: (j, kk))
    else:
        b_spec = pl.BlockSpec((tk, tn), lambda i, j, kk: (kk, j))
    o_spec = pl.BlockSpec((tm, tn), lambda i, j, kk: (i, j))
    dims = {"nn": NN, "nt": NT, "tn": TN}[mode]

    def body(a_ref, b_ref, o_ref, *scratch):
        if nk == 1:
            o_ref[...] = _dot(a_ref[...], b_ref[...], dims)
            return
        acc = scratch[0]
        kk = pl.program_id(2)

        @pl.when(kk == 0)
        def _():
            acc[...] = jnp.zeros_like(acc)

        acc[...] += _dot(a_ref[...], b_ref[...], dims)

        @pl.when(kk == nk - 1)
        def _():
            o_ref[...] = acc[...]

    return pl.pallas_call(
        body, name=name, grid=(m // tm, n // tn, nk), in_specs=[a_spec, b_spec], out_specs=o_spec,
        out_shape=jax.ShapeDtypeStruct((m, n), F32),
        scratch_shapes=[pltpu.VMEM((tm, tn), F32)] if nk > 1 else [],
        compiler_params=_params(("parallel", "parallel", "arbitrary")),
    )(a, b)


def _rms_in_proj(h, g, w, *, name):
    t, d = h.shape
    n = w.shape[1]
    tm, tn = _tile(t, MM_TILE[0]), _tile(n, MM_TILE[1])

    def body(h_ref, g_ref, w_ref, o_ref, hn_ref):
        @pl.when(pl.program_id(1) == 0)
        def _():
            x = h_ref[...]
            r = lax.rsqrt(jnp.mean(x * x, axis=-1, keepdims=True) + RMS_EPS)
            hn_ref[...] = (x * r * g_ref[...]).astype(BF16)

        o_ref[...] = _dot(hn_ref[...], w_ref[...])

    return pl.pallas_call(
        body, name=name, grid=(t // tm, n // tn),
        in_specs=[pl.BlockSpec((tm, d), lambda i, j: (i, 0)), pl.BlockSpec((1, d), lambda i, j: (0, 0)),
                  pl.BlockSpec((d, tn), lambda i, j: (0, j))],
        out_specs=[pl.BlockSpec((tm, tn), lambda i, j: (i, j)), pl.BlockSpec((tm, d), lambda i, j: (i, 0))],
        out_shape=[jax.ShapeDtypeStruct((t, n), F32), jax.ShapeDtypeStruct((t, d), BF16)],
        compiler_params=_params(("parallel", "arbitrary")),
    )(h, g, w)


def _matmul_rms_bwd(dproj, w, h, g, dres, *, name, with_bf16):
    t, k = dproj.shape
    d = w.shape[0]
    tm = _tile(t, 2 * ROW_TILE)

    def body(dp_ref, w_ref, h_ref, g_ref, dres_ref, *outs):
        i = pl.program_id(0)
        dh_ref, dg_ref = outs[0], outs[-1]
        dyv = _dot(dp_ref[...], w_ref[...], NT)
        x = h_ref[...]
        r = lax.rsqrt(jnp.mean(x * x, axis=-1, keepdims=True) + RMS_EPS)
        xh = x * r
        dxh = dyv * g_ref[...]
        dh = dres_ref[...] + r * (dxh - xh * jnp.mean(dxh * xh, axis=-1, keepdims=True))
        dh_ref[...] = dh
        if with_bf16:
            outs[1][...] = dh.astype(BF16)

        @pl.when(i == 0)
        def _():
            dg_ref[...] = jnp.zeros_like(dg_ref)

        dg_ref[...] += jnp.sum(dyv * xh, axis=0, keepdims=True)

    row = pl.BlockSpec((tm, d), lambda i: (i, 0))
    vec = pl.BlockSpec((1, d), lambda i: (0, 0))
    out_shape = [jax.ShapeDtypeStruct((t, d), F32)] + ([jax.ShapeDtypeStruct((t, d), BF16)] if with_bf16 else [])
    out_specs = [row] * len(out_shape) + [vec]
    out_shape.append(jax.ShapeDtypeStruct((1, d), F32))
    return pl.pallas_call(
        body, name=name, grid=(t // tm,),
        in_specs=[pl.BlockSpec((tm, k), lambda i: (i, 0)), pl.BlockSpec((d, k), lambda i: (0, 0)), row, vec, row],
        out_specs=out_specs, out_shape=out_shape, compiler_params=_params(("arbitrary",)),
    )(dproj, w, h, g, dres)


def _rope_partner(x):
    lane = lax.broadcasted_iota(jnp.int32, x.shape, 1)
    swapped = jnp.where(lane < MLA_NOPE + MLA_ROPE // 2, pltpu.roll(x, LANES - 16, 1), pltpu.roll(x, 16, 1))
    return jnp.where((lane >= MLA_NOPE) & (lane < MLA_NOPE + MLA_ROPE), swapped, 0.0)


def _pool_counts(row0, tm, w):
    t_idx = row0 + lax.broadcasted_iota(jnp.int32, (tm, POOL_GROUP), 0)
    return jnp.minimum(t_idx + 1, w).astype(F32)


def _ab_prep(proj, pos, inv_freq, q_a_norm, kv_a_norm, wq, wk, wv, wpool, *, name):
    t = proj.shape[0]
    tm = _tile(t, ROW_TILE)
    hb = tm // POOL_HALO

    def body(p_ref, halo_ref, pos_ref, inv_ref, qg_ref, kg_ref, wq_ref, wk_ref, wv_ref, wp_ref,
             q_ref, k_ref, v_ref, yb_ref, qn_ref, kvn_ref, d_ref, cos_ref, sin_ref, ext):
        i = pl.program_id(0)
        ql = p_ref[:, 0:MLA_Q_RANK]
        r = lax.rsqrt(jnp.mean(ql * ql, axis=-1, keepdims=True) + RMS_EPS)
        qn = (ql * r * qg_ref[...]).astype(BF16)
        qn_ref[...] = qn
        kl = p_ref[:, MLA_Q_RANK:MLA_Q_RANK + MLA_KV_RANK]
        r = lax.rsqrt(jnp.mean(kl * kl, axis=-1, keepdims=True) + RMS_EPS)
        kvn = (kl * r * kg_ref[...]).astype(BF16)
        kvn_ref[...] = kvn
        ang = pos_ref[...].astype(F32) * inv_ref[...]
        lane = lax.broadcasted_iota(jnp.int32, (tm, LANES), 1)
        in_rope = (lane >= MLA_NOPE) & (lane < MLA_NOPE + MLA_ROPE)
        cos_t = jnp.where(in_rope, jnp.cos(ang), 1.0)
        sin_t = jnp.where(in_rope, jnp.sin(ang), 0.0)
        sin_t = jnp.where(lane < MLA_NOPE + MLA_ROPE // 2, -sin_t, sin_t)
        cos_ref[...] = cos_t
        sin_ref[...] = sin_t
        kr = p_ref[:, 384:512]
        kr = kr * cos_t + _rope_partner(kr) * sin_t
        qraw = _dot(qn, wq_ref[...])
        kvk = _dot(kvn, wk_ref[...])
        for h in range(MLA_HEADS):
            sl = slice(h * LANES, (h + 1) * LANES)
            qh = qraw[:, sl]
            q_ref[:, sl] = ((qh * cos_t + _rope_partner(qh) * sin_t) * ATT_SCALE).astype(BF16)
            k_ref[:, sl] = (kvk[:, sl] + kr).astype(BF16)
        v_ref[...] = _dot(kvn, wv_ref[...]).astype(BF16)
        xp = p_ref[:, 512:1024]
        ext[0:POOL_HALO, :] = jnp.where(i > 0, halo_ref[...], 0.0)
        ext[POOL_HALO:POOL_HALO + tm, :] = xp
        for g, w in enumerate(POOL_WINDOWS):
            lo = g * POOL_GROUP
            acc = ext[POOL_HALO:POOL_HALO + tm, lo:lo + POOL_GROUP]
            for s in range(1, w):
                acc = acc + ext[POOL_HALO - s:POOL_HALO - s + tm, lo:lo + POOL_GROUP]
            cnt = _pool_counts(i * tm, tm, w)
            d_ref[:, lo:lo + POOL_GROUP] = (acc / cnt - xp[:, lo:lo + POOL_GROUP]).astype(BF16)
        yb_ref[...] = _dot(d_ref[...], wp_ref[...])

    row = lambda w: pl.BlockSpec((tm, w), lambda i: (i, 0))
    vec = lambda w: pl.BlockSpec((1, w), lambda i: (0, 0))
    whole = lambda a: pl.BlockSpec(a.shape, lambda i: (0, 0))
    return pl.pallas_call(
        body, name=name, grid=(t // tm,),
        in_specs=[row(1024), pl.BlockSpec((POOL_HALO, POOL_WIDTH), lambda i: (jnp.maximum(i * hb - 1, 0), 1)),
                  pl.BlockSpec((tm, 1), lambda i: (i, 0)), vec(LANES), vec(MLA_Q_RANK), vec(MLA_KV_RANK),
                  whole(wq), whole(wk), whole(wv), whole(wpool)],
        out_specs=[row(1024), row(1024), row(512), row(512), row(MLA_Q_RANK), row(MLA_KV_RANK), row(POOL_WIDTH),
                   row(LANES), row(LANES)],
        out_shape=[jax.ShapeDtypeStruct((t, 1024), BF16), jax.ShapeDtypeStruct((t, 1024), BF16),
                   jax.ShapeDtypeStruct((t, 512), BF16), jax.ShapeDtypeStruct((t, 512), F32),
                   jax.ShapeDtypeStruct((t, MLA_Q_RANK), BF16), jax.ShapeDtypeStruct((t, MLA_KV_RANK), BF16),
                   jax.ShapeDtypeStruct((t, POOL_WIDTH), BF16), jax.ShapeDtypeStruct((t, LANES), F32),
                   jax.ShapeDtypeStruct((t, LANES), F32)],
        scratch_shapes=[pltpu.VMEM((tm + POOL_HALO, POOL_WIDTH), F32)],
        compiler_params=_params(("parallel",)),
    )(proj, proj, pos, inv_freq, q_a_norm, kv_a_norm, wq, wk, wv, wpool)


def _ab_prep_bwd(proj, q_a_norm, kv_a_norm, dq, dk, dv, cos_t, sin_t, dyb, dz, wq, wk, wv, wpool, *, name):
    t = proj.shape[0]
    tm = _tile(t, ROW_TILE)
    hb = tm // POOL_HALO
    last_halo = t // POOL_HALO - 1
    nt = t // tm

    def body(p_ref, qg_ref, kg_ref, dq_ref, dk_ref, dv_ref, c_ref, s_ref, dyb_ref, dybn_ref, dz_ref,
             wq_ref, wk_ref, wv_ref, wp_ref, dp_ref, dqr_ref, dkb_ref, dqg_ref, dkg_ref, ext):
        i = pl.program_id(0)

        @pl.when(i == 0)
        def _():
            dqg_ref[...] = jnp.zeros_like(dqg_ref)
            dkg_ref[...] = jnp.zeros_like(dkg_ref)

        def norm_bwd(x, g, dy, dg_ref):
            r = lax.rsqrt(jnp.mean(x * x, axis=-1, keepdims=True) + RMS_EPS)
            xh = x * r
            dxh = dy * g
            dg_ref[...] += jnp.sum(dy * xh, axis=0, keepdims=True)
            return r * (dxh - xh * jnp.mean(dxh * xh, axis=-1, keepdims=True))

        c, s = c_ref[...], s_ref[...]
        lane = lax.broadcasted_iota(jnp.int32, (tm, LANES), 1)
        in_rope = (lane >= MLA_NOPE) & (lane < MLA_NOPE + MLA_ROPE)
        dkr = jnp.zeros((tm, LANES), F32)
        for h in range(MLA_HEADS):
            sl = slice(h * LANES, (h + 1) * LANES)
            g = dq_ref[:, sl]
            dqr_ref[:, sl] = ((g * c + _rope_partner(g * s)) * ATT_SCALE).astype(BF16)
            gk = dk_ref[:, sl]
            dkb_ref[:, sl] = gk.astype(BF16)
            dkr = dkr + jnp.where(in_rope, gk, 0.0)
        dkr = dkr * c + _rope_partner(dkr * s)
        dqn = _dot(dqr_ref[...], wq_ref[...], NT)
        dkvn = _dot(dkb_ref[...], wk_ref[...], NT) + _dot(dv_ref[...], wv_ref[...], NT)
        dql = norm_bwd(p_ref[:, 0:MLA_Q_RANK], qg_ref[...], dqn, dqg_ref)
        dp_ref[:, 0:MLA_Q_RANK] = dql.astype(BF16)
        dkl = norm_bwd(p_ref[:, MLA_Q_RANK:384], kg_ref[...], dkvn, dkg_ref)
        dp_ref[:, MLA_Q_RANK:384] = dkl.astype(BF16)
        dp_ref[:, 384:512] = dkr.astype(BF16)
        ddv = _dot(dyb_ref[...], wp_ref[...], NT)
        ddn = _dot(dybn_ref[...], wp_ref[...], NT)
        for g, w in enumerate(POOL_WINDOWS):
            lo = g * POOL_GROUP
            ext[0:tm, lo:lo + POOL_GROUP] = ddv[:, lo:lo + POOL_GROUP] / _pool_counts(i * tm, tm, w)
            nxt = ddn[:, lo:lo + POOL_GROUP] / _pool_counts((i + 1) * tm, POOL_HALO, w)
            ext[tm:tm + POOL_HALO, lo:lo + POOL_GROUP] = jnp.where(i < nt - 1, nxt, 0.0)
        for g, w in enumerate(POOL_WINDOWS):
            lo = g * POOL_GROUP
            acc = ext[0:tm, lo:lo + POOL_GROUP]
            for s in range(1, w):
                acc = acc + ext[s:s + tm, lo:lo + POOL_GROUP]
            dp_ref[:, 512 + lo:512 + lo + POOL_GROUP] = (acc - ddv[:, lo:lo + POOL_GROUP]).astype(BF16)
        dp_ref[:, 1024:2048] = dz_ref[...]

    row = lambda w: pl.BlockSpec((tm, w), lambda i: (i, 0))
    vec = lambda w: pl.BlockSpec((1, w), lambda i: (0, 0))
    whole = lambda a: pl.BlockSpec(a.shape, lambda i: (0, 0))
    return pl.pallas_call(
        body, name=name, grid=(nt,),
        in_specs=[row(1024), vec(MLA_Q_RANK), vec(MLA_KV_RANK), row(1024), row(1024), row(512), row(LANES), row(LANES),
                  row(POOL_WIDTH),
                  pl.BlockSpec((POOL_HALO, POOL_WIDTH), lambda i: (jnp.minimum((i + 1) * hb, last_halo), 0)),
                  row(1024), whole(wq), whole(wk), whole(wv), whole(wpool)],
        out_specs=[row(IN_AB_PAD), row(1024), row(1024), vec(MLA_Q_RANK), vec(MLA_KV_RANK)],
        out_shape=[jax.ShapeDtypeStruct((t, IN_AB_PAD), BF16), jax.ShapeDtypeStruct((t, 1024), BF16),
                   jax.ShapeDtypeStruct((t, 1024), BF16), jax.ShapeDtypeStruct((1, MLA_Q_RANK), F32),
                   jax.ShapeDtypeStruct((1, MLA_KV_RANK), F32)],
        scratch_shapes=[pltpu.VMEM((tm + POOL_HALO, POOL_WIDTH), F32)],
        compiler_params=_params(("arbitrary",)),
    )(proj, q_a_norm, kv_a_norm, dq, dk, dv, cos_t, sin_t, dyb, dyb, dz, wq, wk, wv, wpool)


def _gate_out_proj(o, ybraw, proj, pool_scale, w, hres, *, name):
    t = o.shape[0]
    tm = _tile(t, 2 * ROW_TILE)

    def body(o_ref, yb_ref, z_ref, ps_ref, w_ref, h_ref, ho_ref, y_ref):
        z = z_ref[...]
        sz = z * _sigmoid(z)
        y_ref[:, 0:512] = (o_ref[...] * sz[:, 0:512]).astype(BF16)
        y_ref[:, 512:1024] = (yb_ref[...] * ps_ref[...] * sz[:, 512:1024]).astype(BF16)
        ho_ref[...] = h_ref[...] + _dot(y_ref[...], w_ref[...])

    row = lambda w_: pl.BlockSpec((tm, w_), lambda i: (i, 0))
    return pl.pallas_call(
        body, name=name, grid=(t // tm,),
        in_specs=[row(512), row(512), pl.BlockSpec((tm, 1024), lambda i: (i, 1)), pl.BlockSpec((1, 512), lambda i: (0, 0)),
                  pl.BlockSpec(w.shape, lambda i: (0, 0)), row(1024)],
        out_specs=[row(1024), row(1024)],
        out_shape=[jax.ShapeDtypeStruct((t, 1024), F32), jax.ShapeDtypeStruct((t, 1024), BF16)],
        compiler_params=_params(("parallel",)),
    )(o, ybraw, proj, pool_scale, w, hres)


def _gate_bwd(dy, o, ybraw, proj, pool_scale, *, name):
    t = o.shape[0]
    tm = _tile(t, ROW_TILE)

    def body(dy_ref, o_ref, yb_ref, z_ref, ps_ref, do_ref, dl_ref, dyb_ref, dz_ref, dps_ref):
        i = pl.program_id(0)
        z = z_ref[...]
        sg = _sigmoid(z)
        sz = z * sg
        dsz = sg * (1.0 + z * (1.0 - sg))
        dyv = dy_ref[...]
        dcat = dyv * sz
        ov = o_ref[...]
        ybs = yb_ref[...] * ps_ref[...]
        dz_ref[:, 0:512] = (dyv[:, 0:512] * ov * dsz[:, 0:512]).astype(BF16)
        dz_ref[:, 512:1024] = (dyv[:, 512:1024] * ybs * dsz[:, 512:1024]).astype(BF16)
        do = dcat[:, 0:512]
        do_ref[...] = do.astype(BF16)
        r_i = (lax.broadcasted_iota(jnp.int32, (1024, 512), 0) % 512) // MLA_V
        c_i = lax.broadcasted_iota(jnp.int32, (1024, 512), 1) // MLA_V
        prod = do * ov
        hi = prod.astype(BF16)
        lo = (prod - hi.astype(F32)).astype(BF16)
        dl_ref[...] = _dot(jnp.concatenate([hi, lo], axis=1), (r_i == c_i).astype(BF16))
        dyb_ref[...] = (dcat[:, 512:1024] * ps_ref[...]).astype(BF16)

        @pl.when(i == 0)
        def _():
            dps_ref[...] = jnp.zeros_like(dps_ref)

        dps_ref[...] += jnp.sum(dcat[:, 512:1024] * yb_ref[...], axis=0, keepdims=True)

    row = lambda w: pl.BlockSpec((tm, w), lambda i: (i, 0))
    vec = pl.BlockSpec((1, 512), lambda i: (0, 0))
    return pl.pallas_call(
        body, name=name, grid=(t // tm,),
        in_specs=[row(1024), row(512), row(512), pl.BlockSpec((tm, 1024), lambda i: (i, 1)), vec],
        out_specs=[row(512), row(512), row(512), row(1024), vec],
        out_shape=[jax.ShapeDtypeStruct((t, 512), BF16), jax.ShapeDtypeStruct((t, 512), F32),
                   jax.ShapeDtypeStruct((t, 512), BF16), jax.ShapeDtypeStruct((t, 1024), BF16),
                   jax.ShapeDtypeStruct((1, 512), F32)],
        compiler_params=_params(("arbitrary",)),
    )(dy, o, ybraw, proj, pool_scale)


ATT_HP_FWD = 4
ATT_HP_BWD = 2


def _diag_mask(tq):
    return lax.broadcasted_iota(jnp.int32, (tq, tq), 1) <= lax.broadcasted_iota(jnp.int32, (tq, tq), 0)


def _block_schedule(nq, key_major):
    if key_major:
        pairs = [(qi, ki) for ki in range(nq) for qi in range(ki, nq)]
    else:
        pairs = [(qi, ki) for qi in range(nq) for ki in range(qi + 1)]
    return jnp.asarray([p[0] for p in pairs], jnp.int32), jnp.asarray([p[1] for p in pairs], jnp.int32)


def _attn_fwd(q, k, v, *, name):
    t = q.shape[0]
    tq = _tile(t, ATT_TILE)
    nq = t // tq
    hp = ATT_HP_FWD
    qi_tab, ki_tab = _block_schedule(nq, key_major=False)

    def body(qi_ref, ki_ref, q_ref, k_ref, v_ref, o_ref, lse_ref, m_sc, l_sc, acc_sc):
        step = pl.program_id(1)
        qi, ki = qi_ref[step], ki_ref[step]

        @pl.when(ki == 0)
        def _():
            m_sc[...] = jnp.full_like(m_sc, -jnp.inf)
            l_sc[...] = jnp.zeros_like(l_sc)
            acc_sc[...] = jnp.zeros_like(acc_sc)

        def block(on_diagonal):
            scores = []
            for h in range(hp):
                sl = slice(h * LANES, (h + 1) * LANES)
                scores.append(_dot(q_ref[:, sl], k_ref[:, sl], NT))
            if on_diagonal:
                mask = _diag_mask(tq)
                scores = [jnp.where(mask, s, -jnp.inf) for s in scores]
            for h, s in enumerate(scores):
                vv = v_ref[:, (h // 2) * LANES:(h // 2 + 1) * LANES]
                m_prev = m_sc[h]
                m_new = jnp.maximum(m_prev, jnp.max(s, axis=-1, keepdims=True))
                alpha = jnp.exp(m_prev - m_new)
                p = jnp.exp(s - m_new[:, 0:1])
                l_sc[h] = alpha * l_sc[h] + jnp.sum(p, axis=-1, keepdims=True)
                acc_sc[h] = alpha * acc_sc[h] + _dot(p.astype(BF16), vv)
                m_sc[h] = m_new

        pl.when(ki < qi)(functools.partial(block, False))
        pl.when(ki == qi)(functools.partial(block, True))

        @pl.when(ki == qi)
        def _():
            first = lax.broadcasted_iota(jnp.int32, (tq, LANES), 1) < MLA_V
            for pr in range(hp // 2):
                a, b = 2 * pr, 2 * pr + 1
                sl = slice(pr * LANES, (pr + 1) * LANES)
                o_ref[:, sl] = jnp.where(first, acc_sc[a] / l_sc[a], acc_sc[b] / l_sc[b])
                lse_ref[:, sl] = jnp.where(first, m_sc[a] + jnp.log(l_sc[a]), m_sc[b] + jnp.log(l_sc[b]))

    grid_spec = pltpu.PrefetchScalarGridSpec(
        num_scalar_prefetch=2, grid=(MLA_HEADS // hp, qi_tab.shape[0]),
        in_specs=[pl.BlockSpec((tq, hp * LANES), lambda g, s, qt, kt: (qt[s], g)),
                  pl.BlockSpec((tq, hp * LANES), lambda g, s, qt, kt: (kt[s], g)),
                  pl.BlockSpec((tq, hp * MLA_V), lambda g, s, qt, kt: (kt[s], g))],
        out_specs=[pl.BlockSpec((tq, hp * MLA_V), lambda g, s, qt, kt: (qt[s], g)),
                   pl.BlockSpec((tq, hp * MLA_V), lambda g, s, qt, kt: (qt[s], g))],
        scratch_shapes=[pltpu.VMEM((hp, tq, LANES), F32)] * 3,
    )
    return pl.pallas_call(
        body, name=name, grid_spec=grid_spec,
        out_shape=[jax.ShapeDtypeStruct((t, 512), F32), jax.ShapeDtypeStruct((t, 512), F32)],
        compiler_params=_params(("parallel", "arbitrary")),
    )(qi_tab, ki_tab, q, k, v)


def _attn_bwd(q, k, v, do, lse, delta, *, name):
    t = q.shape[0]
    tq = _tile(t, ATT_TILE)
    nq = t // tq
    hp = ATT_HP_BWD
    qi_tab, ki_tab = _block_schedule(nq, key_major=True)

    def body(qi_ref, ki_ref, q_ref, k_ref, v_ref, do_ref, lse_ref, dl_ref, dq_ref, dk_ref, dv_ref, dk_sc, dv_sc):
        step = pl.program_id(1)
        qi, ki = qi_ref[step], ki_ref[step]

        @pl.when(step == 0)
        def _():
            dq_ref[...] = jnp.zeros_like(dq_ref)

        @pl.when(qi == ki)
        def _():
            dk_sc[...] = jnp.zeros_like(dk_sc)
            dv_sc[...] = jnp.zeros_like(dv_sc)

        def block(on_diagonal):
            lane = lax.broadcasted_iota(jnp.int32, (tq, LANES), 1)
            rows = pl.ds(pl.multiple_of(qi * tq, tq), tq)
            heads = [slice(h * LANES, (h + 1) * LANES) for h in range(hp)]
            scores = [_dot(q_ref[:, sl], k_ref[:, sl], NT) for sl in heads]
            dps = []
            for h in range(hp):
                dov = do_ref[:, (h // 2) * LANES:(h // 2 + 1) * LANES]
                mine = (lane < MLA_V) if h % 2 == 0 else (lane >= MLA_V)
                dps.append(_dot(jnp.where(mine, dov, jnp.zeros_like(dov)), v_ref[:, (h // 2) * LANES:(h // 2 + 1) * LANES], NT))
            mask = _diag_mask(tq) if on_diagonal else None
            for h, sl in enumerate(heads):
                col = (h // 2) * LANES + (h % 2) * MLA_V
                p = jnp.exp(scores[h] - lse_ref[:, col:col + 1])
                if on_diagonal:
                    p = jnp.where(mask, p, 0.0)
                ds = (p * (dps[h] - dl_ref[:, col:col + 1])).astype(BF16)
                dv_sc[h] += _dot(p.astype(BF16), do_ref[:, (h // 2) * LANES:(h // 2 + 1) * LANES], TN)
                dk_sc[h] += _dot(ds, q_ref[:, sl], TN)
                dq_ref[rows, sl] += _dot(ds, k_ref[:, sl], NN)

        pl.when(qi > ki)(functools.partial(block, False))
        pl.when(qi == ki)(functools.partial(block, True))

        @pl.when(qi == nq - 1)
        def _():
            first = lax.broadcasted_iota(jnp.int32, (tq, LANES), 1) < MLA_V
            for h in range(hp):
                dk_ref[:, h * LANES:(h + 1) * LANES] = dk_sc[h]
            for pr in range(hp // 2):
                dv_ref[:, pr * LANES:(pr + 1) * LANES] = jnp.where(first, dv_sc[2 * pr], dv_sc[2 * pr + 1]).astype(BF16)

    qrow = lambda w: pl.BlockSpec((tq, w), lambda g, s, qt, kt: (qt[s], g))
    krow = lambda w: pl.BlockSpec((tq, w), lambda g, s, qt, kt: (kt[s], g))
    grid_spec = pltpu.PrefetchScalarGridSpec(
        num_scalar_prefetch=2, grid=(MLA_HEADS // hp, qi_tab.shape[0]),
        in_specs=[qrow(hp * LANES), krow(hp * LANES), krow(hp * MLA_V), qrow(hp * MLA_V), qrow(hp * MLA_V), qrow(hp * MLA_V)],
        out_specs=[pl.BlockSpec((t, hp * LANES), lambda g, s, qt, kt: (0, g)), krow(hp * LANES), krow(hp * MLA_V)],
        scratch_shapes=[pltpu.VMEM((hp, tq, LANES), F32), pltpu.VMEM((hp, tq, LANES), F32)],
    )
    return pl.pallas_call(
        body, name=name, grid_spec=grid_spec,
        out_shape=[jax.ShapeDtypeStruct((t, 1024), F32), jax.ShapeDtypeStruct((t, 1024), F32),
                   jax.ShapeDtypeStruct((t, 512), BF16)],
        compiler_params=_params(("parallel", "arbitrary")),
    )(qi_tab, ki_tab, q, k, v, do, lse, delta)


def _conv_rows(ext, tm, w_ref, sec):
    c0 = sec * 1024
    y = ext[CONV_HALO - 3:CONV_HALO - 3 + tm, c0:c0 + 1024] * w_ref[0:1, c0:c0 + 1024]
    for j in range(1, CONV_WIDTH):
        y = y + ext[CONV_HALO - 3 + j:CONV_HALO - 3 + j + tm, c0:c0 + 1024] * w_ref[j:j + 1, c0:c0 + 1024]
    return y


def _c_prep(proj_c, conv_w, a_log, dt_bias, *, name):
    t = proj_c.shape[0]
    tm = _tile(t, ROW_TILE)
    hb = tm // CONV_HALO

    def body(p_ref, halo_ref, ab_ref, w_ref, al_ref, dtb_ref, q_ref, k_ref, v_ref, g_ref, b_ref, gt_ref, ext):
        i = pl.program_id(0)
        ext[0:CONV_HALO, :] = jnp.where(i > 0, halo_ref[...], 0.0)
        ext[CONV_HALO:CONV_HALO + tm, :] = p_ref[...]
        for sec, o_ref in enumerate((q_ref, k_ref, v_ref)):
            y = _conv_rows(ext, tm, w_ref, sec)
            y = y * _sigmoid(y)
            if sec == 2:
                o_ref[...] = y
                continue
            scale = GDN_DK ** -0.5 if sec == 0 else 1.0
            for h in range(GDN_HEADS):
                sl = slice(h * LANES, (h + 1) * LANES)
                blk = y[:, sl]
                r = lax.rsqrt(jnp.sum(blk * blk, axis=-1, keepdims=True) + RMS_EPS)
                o_ref[:, sl] = blk * (r * scale)
        ab = ab_ref[...]
        g = -jnp.exp(al_ref[...]) * _softplus(ab + dtb_ref[...])
        beta = _sigmoid(ab)
        ri = lax.broadcasted_iota(jnp.int32, (tm, tm), 0)
        ci = lax.broadcasted_iota(jnp.int32, (tm, tm), 1)
        lower = ((ri // CHUNK) == (ci // CHUNK)) & (ri >= ci)
        gc = _dot(lower.astype(F32), g, NN, HI)
        eye = lax.broadcasted_iota(jnp.int32, (LANES, LANES), 0) == lax.broadcasted_iota(jnp.int32, (LANES, LANES), 1)
        gt_ref[...] = _dot(eye.astype(F32), gc, NT, HI)[0:GDN_HEADS, :]
        for h in range(GDN_HEADS):
            sl = slice(h * LANES, (h + 1) * LANES)
            g_ref[:, sl] = jnp.broadcast_to(gc[:, h:h + 1], (tm, LANES))
            b_ref[:, sl] = jnp.broadcast_to(beta[:, GDN_HEADS + h:GDN_HEADS + h + 1], (tm, LANES))

    row = lambda w: pl.BlockSpec((tm, w), lambda i: (i, 0))
    vec = lambda r, w: pl.BlockSpec((r, w), lambda i: (0, 0))
    out = jax.ShapeDtypeStruct((t, 1024), F32)
    return pl.pallas_call(
        body, name=name, grid=(t // tm,),
        in_specs=[row(3072), pl.BlockSpec((CONV_HALO, 3072), lambda i: (jnp.maximum(i * hb - 1, 0), 0)),
                  pl.BlockSpec((tm, LANES), lambda i: (i, 32)), vec(CONV_WIDTH, 3072), vec(1, LANES), vec(1, LANES)],
        out_specs=[row(1024)] * 5 + [pl.BlockSpec((GDN_HEADS, tm), lambda i: (0, i))],
        out_shape=[out] * 5 + [jax.ShapeDtypeStruct((GDN_HEADS, t), F32)],
        scratch_shapes=[pltpu.VMEM((tm + CONV_HALO, 3072), F32)],
        compiler_params=_params(("parallel",)),
    )(proj_c, proj_c, proj_c, conv_w, a_log, dt_bias)


def _c_prep_bwd(proj_c, conv_w, a_log, dt_bias, dq, dk, dv, dgb, dbb, dz, *, name):
    t = proj_c.shape[0]
    tm = _tile(t, ROW_TILE // 2)
    hb = tm // CONV_HALO
    nt = t // tm
    rev = lambda i: nt - 1 - i

    def body(p_ref, halo_ref, ab_ref, w_ref, al_ref, dtb_ref, dq_ref, dk_ref, dv_ref, dg_ref, db_ref, dz_ref,
             dp_ref, dw_ref, dal_ref, ddt_ref, ext, dyext, carry, taps):
        step = pl.program_id(0)
        i = rev(step)

        @pl.when(step == 0)
        def _():
            dw_ref[...] = jnp.zeros_like(dw_ref)
            dal_ref[...] = jnp.zeros_like(dal_ref)
            ddt_ref[...] = jnp.zeros_like(ddt_ref)
            carry[...] = jnp.zeros_like(carry)

        ext[0:CONV_HALO, :] = jnp.where(i > 0, halo_ref[...], 0.0)
        ext[CONV_HALO:CONV_HALO + tm, :] = p_ref[...]
        for sec, g_ref in enumerate((dq_ref, dk_ref, dv_ref)):
            c0 = sec * 1024
            for j in range(CONV_WIDTH):
                taps[j] = ext[CONV_HALO - 3 + j:CONV_HALO - 3 + j + tm, c0:c0 + 1024]
            y = taps[0] * w_ref[0:1, c0:c0 + 1024]
            for j in range(1, CONV_WIDTH):
                y = y + taps[j] * w_ref[j:j + 1, c0:c0 + 1024]
            sg = _sigmoid(y)
            act = y * sg
            if sec == 2:
                dact = g_ref[...]
            else:
                scale = GDN_DK ** -0.5 if sec == 0 else 1.0
                parts = []
                for h in range(GDN_HEADS):
                    sl = slice(h * LANES, (h + 1) * LANES)
                    blk = act[:, sl]
                    r = lax.rsqrt(jnp.sum(blk * blk, axis=-1, keepdims=True) + RMS_EPS)
                    n = blk * r
                    dn = g_ref[:, sl] * scale
                    parts.append(r * (dn - n * jnp.sum(dn * n, axis=-1, keepdims=True)))
                dact = jnp.concatenate(parts, axis=-1)
            dy = dact * (sg * (1.0 + y * (1.0 - sg)))
            dyext[0:tm, c0:c0 + 1024] = dy
            for j in range(CONV_WIDTH):
                dw_ref[j:j + 1, c0:c0 + 1024] += jnp.sum(dy * taps[j], axis=0, keepdims=True)
        dyext[tm:tm + CONV_HALO, :] = carry[...]
        carry[...] = dyext[0:CONV_HALO, :]
        for sec in range(3):
            c0 = sec * 1024
            dx = dyext[3:3 + tm, c0:c0 + 1024] * w_ref[0:1, c0:c0 + 1024]
            for j in range(1, CONV_WIDTH):
                dx = dx + dyext[3 - j:3 - j + tm, c0:c0 + 1024] * w_ref[j:j + 1, c0:c0 + 1024]
            dp_ref[:, c0:c0 + 1024] = dx.astype(BF16)
        dp_ref[:, 3072:4096] = dz_ref[...]
        lane = lax.broadcasted_iota(jnp.int32, (tm, LANES), 1)
        dg = jnp.zeros((tm, LANES), F32)
        dbeta = jnp.zeros((tm, LANES), F32)
        for h in range(GDN_HEADS):
            sl = slice(h * LANES, (h + 1) * LANES)
            dg = dg + jnp.where(lane == h, dg_ref[:, sl], 0.0)
            dbeta = dbeta + jnp.where(lane == GDN_HEADS + h, db_ref[:, sl], 0.0)
        ri = lax.broadcasted_iota(jnp.int32, (tm, tm), 0)
        ci = lax.broadcasted_iota(jnp.int32, (tm, tm), 1)
        upper = ((ri // CHUNK) == (ci // CHUNK)) & (ri <= ci)
        dg = _dot(upper.astype(F32), dg, NN, HI)
        pre = ab_ref[...] + dtb_ref[...]
        s = _sigmoid(pre)
        a_exp = jnp.exp(al_ref[...])
        dg_da = dg * (-a_exp * s)
        dp_ref[:, 4096:IN_C_PAD] = (dg_da + dbeta * s * (1.0 - s)).astype(BF16)
        dal_ref[...] += jnp.sum(dg * (-a_exp * _softplus(pre)), axis=0, keepdims=True)
        ddt_ref[...] += jnp.sum(dg_da, axis=0, keepdims=True)

    row = lambda w: pl.BlockSpec((tm, w), lambda s: (rev(s), 0))
    vec = lambda r, w: pl.BlockSpec((r, w), lambda s: (0, 0))
    return pl.pallas_call(
        body, name=name, grid=(nt,),
        in_specs=[row(3072), pl.BlockSpec((CONV_HALO, 3072), lambda s: (jnp.maximum(rev(s) * hb - 1, 0), 0)),
                  pl.BlockSpec((tm, LANES), lambda s: (rev(s), 32)), vec(CONV_WIDTH, 3072), vec(1, LANES), vec(1, LANES),
                  row(1024), row(1024), row(1024), row(1024), row(1024), row(1024)],
        out_specs=[row(IN_C_PAD), vec(CONV_WIDTH, 3072), vec(1, LANES), vec(1, LANES)],
        out_shape=[jax.ShapeDtypeStruct((t, IN_C_PAD), BF16), jax.ShapeDtypeStruct((CONV_WIDTH, 3072), F32),
                   jax.ShapeDtypeStruct((1, LANES), F32), jax.ShapeDtypeStruct((1, LANES), F32)],
        scratch_shapes=[pltpu.VMEM((tm + CONV_HALO, 3072), F32), pltpu.VMEM((tm + CONV_HALO, 3072), F32),
                        pltpu.VMEM((CONV_HALO, 3072), F32), pltpu.VMEM((CONV_WIDTH, tm, 1024), F32)],
        compiler_params=_params(("arbitrary",)),
    )(proj_c, proj_c, proj_c, conv_w, a_log, dt_bias, dq, dk, dv, dgb, dbb, dz)


def _o_gate_bwd(dy, o, proj_c, o_norm, *, name):
    t = o.shape[0]
    tm = _tile(t, 2 * ROW_TILE)

    def body(dy_ref, o_ref, z_ref, g_ref, do_ref, dz_ref, dg_ref):
        i = pl.program_id(0)

        @pl.when(i == 0)
        def _():
            dg_ref[...] = jnp.zeros_like(dg_ref)

        dg = jnp.zeros((1, LANES), F32)
        for h in range(GDN_HEADS):
            sl = slice(h * LANES, (h + 1) * LANES)
            x = o_ref[:, sl]
            r = lax.rsqrt(jnp.mean(x * x, axis=-1, keepdims=True) + RMS_EPS)
            xh = x * r
            z = z_ref[:, sl]
            sg = _sigmoid(z)
            dyv = dy_ref[:, sl]
            dn = dyv * (z * sg)
            dz_ref[:, sl] = (dyv * xh * g_ref[...] * (sg * (1.0 + z * (1.0 - sg)))).astype(BF16)
            dxh = dn * g_ref[...]
            do_ref[:, sl] = r * (dxh - xh * jnp.mean(dxh * xh, axis=-1, keepdims=True))
            dg = dg + jnp.sum(dn * xh, axis=0, keepdims=True)
        dg_ref[...] += dg

    row = pl.BlockSpec((tm, 1024), lambda i: (i, 0))
    vec = pl.BlockSpec((1, LANES), lambda i: (0, 0))
    return pl.pallas_call(
        body, name=name, grid=(t // tm,), in_specs=[row, row, pl.BlockSpec((tm, 1024), lambda i: (i, 3)), vec],
        out_specs=[row, row, vec],
        out_shape=[jax.ShapeDtypeStruct((t, 1024), F32), jax.ShapeDtypeStruct((t, 1024), BF16),
                   jax.ShapeDtypeStruct((1, LANES), F32)],
        compiler_params=_params(("arbitrary",)),
    )(dy, o, proj_c, o_norm)


PAIR = 2 * CHUNK
GDN_HP = 8


def _bdot(a, b, dims=NN):
    return _dot(a.astype(BF16), b.astype(BF16), dims)


def _each(f, *lists):
    return [f(*args) for args in zip(*lists)]


def _pair_common(q, k, v, gci, gcj, beta):
    ri = lax.broadcasted_iota(jnp.int32, (PAIR, PAIR), 0)
    ci = lax.broadcasted_iota(jnp.int32, (PAIR, PAIR), 1)
    same = (ri // CHUNK) == (ci // CHUNK)
    incl = same & (ri >= ci)
    strict = same & (ri > ci)
    eye = (ri == ci).astype(F32)
    first = lax.broadcasted_iota(jnp.int32, (PAIR, LANES), 0) < CHUNK
    gamma = _each(lambda gi, gj: jnp.where(incl, jnp.exp(jnp.minimum(gi - gj, 0.0)), 0.0), gci, gcj)
    kb = _each(jnp.multiply, k, beta)
    kk = _each(lambda a, b: _bdot(a, b, NT), kb, k)
    qk = _each(lambda a, b: _bdot(a, b, NT), q, k)
    m = _each(lambda x, g: jnp.where(strict, x * g, 0.0), kk, gamma)
    tm_ = _each(lambda x: eye - x, m)
    pw = _each(lambda x: _bdot(x, x), m)
    for it in range(5):
        tm_ = _each(lambda x, p: x + _bdot(x, p), tm_, pw)
        if it < 4:
            pw = _each(lambda p: _bdot(p, p), pw)
    eg = _each(jnp.exp, gci)
    vb = _each(jnp.multiply, v, beta)
    kbe = _each(jnp.multiply, kb, eg)
    uw = _each(lambda x, a, b: _bdot(x, jnp.concatenate([a, b], axis=1)), tm_, vb, kbe)
    attn = _each(lambda x, g: jnp.where(incl, x * g, 0.0), qk, gamma)
    gl_a = _each(lambda g: g[CHUNK - 1:CHUNK, :], gci)
    gl_b = _each(lambda g: g[PAIR - 1:PAIR, :], gci)
    ek = _each(lambda a, b, g: jnp.exp(jnp.where(first, a, b) - g), gl_a, gl_b, gci)
    return dict(incl=incl, strict=strict, gamma=gamma, kb=kb, m=m, tm=tm_, eg=eg, vb=vb, kbe=kbe,
                u=_each(lambda x: x[:, :LANES], uw), w=_each(lambda x: x[:, LANES:], uw), attn=attn,
                qd=_each(jnp.multiply, q, eg), ek=ek, kd=_each(jnp.multiply, k, ek),
                glast_a=_each(jnp.exp, gl_a), glast_b=_each(jnp.exp, gl_b))


def _gdn_specs(t, ts, order):
    nc = ts // CHUNK
    blk = pl.BlockSpec((ts, GDN_HP * LANES), lambda h, s: (order(s), h))
    row = pl.BlockSpec((GDN_HP, 1, ts), lambda h, s: (h, 0, order(s)))
    st = pl.BlockSpec((GDN_HP, nc, LANES, LANES), lambda h, s: (h, order(s), 0, 0))
    return blk, row, st


def _gdn_fwd(q, k, v, gcb, gct, bb, *, name):
    t = q.shape[0]
    ts = _tile(t, GDN_TILE)
    npair = ts // PAIR

    def body(q_ref, k_ref, v_ref, g_ref, gt_ref, b_ref, o_ref, st_ref, s_sc):
        @pl.when(pl.program_id(1) == 0)
        def _():
            s_sc[...] = jnp.zeros_like(s_sc)

        def pair(pi, _):
            rows = pl.ds(pl.multiple_of(pi * PAIR, PAIR), PAIR)
            heads = [slice(hh * LANES, (hh + 1) * LANES) for hh in range(GDN_HP)]
            c = CHUNK
            cat0 = lambda *xs: jnp.concatenate(xs, axis=0)
            s0 = [s_sc[hh] for hh in range(GDN_HP)]
            cm = _pair_common([q_ref[rows, sl] for sl in heads], [k_ref[rows, sl] for sl in heads],
                              [v_ref[rows, sl] for sl in heads], [g_ref[rows, sl] for sl in heads],
                              [gt_ref[hh, :, rows] for hh in range(GDN_HP)], [b_ref[rows, sl] for sl in heads])
            u, w, qd, kd = cm["u"], cm["w"], cm["qd"], cm["kd"]
            r0 = _each(lambda w_, q_, s: _bdot(cat0(w_[:c], q_[:c]), s), w, qd, s0)
            vn_a = _each(lambda u_, r: u_[:c] - r[:c], u, r0)
            s1 = _each(lambda s, gl, k_, vn: s * gl + _bdot(k_[:c], vn, TN), s0, cm["glast_a"], kd, vn_a)
            r1 = _each(lambda w_, q_, s: _bdot(cat0(w_[c:], q_[c:]), s), w, qd, s1)
            vn_b = _each(lambda u_, r: u_[c:] - r[:c], u, r1)
            s2 = _each(lambda s, gl, k_, vn: s * gl + _bdot(k_[c:], vn, TN), s1, cm["glast_b"], kd, vn_b)
            o = _each(lambda ra, rb, at, va, vb_: cat0(ra[c:], rb[c:]) + _bdot(at, cat0(va, vb_)),
                      r0, r1, cm["attn"], vn_a, vn_b)
            for hh, sl in enumerate(heads):
                st_ref[hh, 2 * pi] = s0[hh]
                st_ref[hh, 2 * pi + 1] = s1[hh]
                s_sc[hh] = s2[hh]
                o_ref[rows, sl] = o[hh]
            return 0

        lax.fori_loop(0, npair, pair, 0)

    blk, row, st = _gdn_specs(t, ts, lambda s: s)
    return pl.pallas_call(
        body, name=name, grid=(GDN_HEADS // GDN_HP, t // ts), in_specs=[blk, blk, blk, blk, row, blk],
        out_specs=[blk, st],
        out_shape=[jax.ShapeDtypeStruct((t, 1024), F32), jax.ShapeDtypeStruct((GDN_HEADS, t // CHUNK, LANES, LANES), F32)],
        scratch_shapes=[pltpu.VMEM((GDN_HP, LANES, LANES), F32)],
        compiler_params=_params(("parallel", "arbitrary")),
    )(q, k, v, gcb, gct, bb)


def _gdn_bwd(q, k, v, gcb, gct, bb, do, states, *, name):
    t = q.shape[0]
    ts = _tile(t, GDN_TILE)
    npair = ts // PAIR
    ns = t // ts
    c = CHUNK

    def body(q_ref, k_ref, v_ref, g_ref, gt_ref, b_ref, do_ref, st_ref, dq_ref, dk_ref, dv_ref, dg_ref, db_ref, ds_sc):
        @pl.when(pl.program_id(1) == 0)
        def _():
            ds_sc[...] = jnp.zeros_like(ds_sc)

        rowsum = lambda x: jnp.sum(x, axis=-1, keepdims=True)
        total = lambda x: jnp.sum(rowsum(x), axis=0, keepdims=True)
        cat0 = lambda *xs: jnp.concatenate(xs, axis=0)
        cat1 = lambda *xs: jnp.concatenate(xs, axis=1)

        def pair(step, _):
            pi = npair - 1 - step
            rows = pl.ds(pl.multiple_of(pi * PAIR, PAIR), PAIR)
            heads = [slice(hh * LANES, (hh + 1) * LANES) for hh in range(GDN_HP)]
            hs = range(GDN_HP)
            qv, kv, vv = ([r[rows, sl] for sl in heads] for r in (q_ref, k_ref, v_ref))
            beta = [b_ref[rows, sl] for sl in heads]
            dov = [do_ref[rows, sl] for sl in heads]
            s0 = [st_ref[hh, 2 * pi] for hh in hs]
            s1 = [st_ref[hh, 2 * pi + 1] for hh in hs]
            ds2 = [ds_sc[hh] for hh in hs]
            cm = _pair_common(qv, kv, vv, [g_ref[rows, sl] for sl in heads], [gt_ref[hh, :, rows] for hh in hs], beta)
            u, w, qd, kd, attn = cm["u"], cm["w"], cm["qd"], cm["kd"], cm["attn"]
            tmat, gamma, eg = cm["tm"], cm["gamma"], cm["eg"]
            incl, strict = cm["incl"], cm["strict"]
            vn_a = _each(lambda u_, w_, s: u_[:c] - _bdot(w_[:c], s), u, w, s0)
            vn_b = _each(lambda u_, w_, s: u_[c:] - _bdot(w_[c:], s), u, w, s1)
            vn = _each(cat0, vn_a, vn_b)
            dvn_att = _each(lambda a, d: _bdot(a, d, TN), attn, dov)
            dattn = _each(lambda d, v_: jnp.where(incl, _bdot(d, v_, NT), 0.0), dov, vn)
            dvn_b = _each(lambda x, k_, d: x[c:] + _bdot(k_[c:], d), dvn_att, kd, ds2)
            rb = _each(lambda d, x, s: _bdot(cat0(d[c:], x), s, NT), dov, dvn_b, s1)
            dkd_b = _each(lambda v_, d: _bdot(v_, d, NT), vn_b, ds2)
            dgl_b = _each(lambda d, s: total(d * s), ds2, s1)
            ds1 = _each(lambda d, gl, q_, w_, o_, x: d * gl + _bdot(cat0(q_[c:], w_[c:]), cat0(o_[c:], -x), TN),
                        ds2, cm["glast_b"], qd, w, dov, dvn_b)
            dvn_a = _each(lambda x, k_, d: x[:c] + _bdot(k_[:c], d), dvn_att, kd, ds1)
            ra = _each(lambda d, x, s: _bdot(cat0(d[:c], x), s, NT), dov, dvn_a, s0)
            dkd_a = _each(lambda v_, d: _bdot(v_, d, NT), vn_a, ds1)
            dgl_a = _each(lambda d, s: total(d * s), ds1, s0)
            ds0 = _each(lambda d, gl, q_, w_, o_, x: d * gl + _bdot(cat0(q_[:c], w_[:c]), cat0(o_[:c], -x), TN),
                        ds1, cm["glast_a"], qd, w, dov, dvn_a)
            dvn = _each(cat0, dvn_a, dvn_b)
            dqd = _each(lambda a, b: cat0(a[:c], b[:c]), ra, rb)
            dw = _each(lambda a, b: -cat0(a[c:], b[c:]), ra, rb)
            dkd = _each(cat0, dkd_a, dkd_b)
            dvw = _each(cat1, dvn, dw)
            dvbk = _each(lambda t_, x: _bdot(t_, x, TN), tmat, dvw)
            dvb = _each(lambda x: x[:, :LANES], dvbk)
            dkbe = _each(lambda x: x[:, LANES:], dvbk)
            dt_ = _each(lambda x, a, b: _bdot(x, cat1(a, b), NT), dvw, cm["vb"], cm["kbe"])
            da1 = _each(lambda t_, x: _bdot(t_, x, TN), tmat, dt_)
            dm = _each(lambda x, t_: jnp.where(strict, -_bdot(x, t_, NT), 0.0), da1, tmat)
            dkk = _each(jnp.multiply, dm, gamma)
            dqk = _each(jnp.multiply, dattn, gamma)
            z = _each(lambda a, b, c_, d: a * b + c_ * d, dm, cm["m"], dattn, attn)
            dkb = _each(lambda x, k_, y, e: _bdot(x, k_) + y * e, dkk, kv, dkbe, eg)
            dk = _each(lambda a, b, kb_, q_, x, e, y, be: _bdot(cat0(a, b), cat0(kb_, q_), TN) + x * e + y * be,
                       dkk, dqk, cm["kb"], qv, dkd, cm["ek"], dkb, beta)
            dq = _each(lambda x, k_, y, e: _bdot(x, k_) + y * e, dqk, kv, dqd, eg)

            def colsum_of(z_):
                zh = z_.astype(BF16)
                zl = (z_ - zh.astype(F32)).astype(BF16)
                return _dot(cat0(zh, zl), jnp.ones((2 * PAIR, LANES), BF16), TN)

            colsum = _each(colsum_of, z)
            ri = lax.broadcasted_iota(jnp.int32, (PAIR, LANES), 0)
            for hh, sl in enumerate(heads):
                dkd_kd = dkd[hh] * kd[hh]
                dgc = (rowsum(z[hh]) - colsum[hh] + rowsum(dqd[hh] * qd[hh]) - rowsum(dkd_kd)
                       + rowsum(dkbe[hh] * cm["kbe"][hh]))
                last_a = total(dkd_kd[:c]) + dgl_a[hh] * cm["glast_a"][hh]
                last_b = total(dkd_kd[c:]) + dgl_b[hh] * cm["glast_b"][hh]
                dgc = dgc + jnp.where(ri == c - 1, last_a, 0.0) + jnp.where(ri == PAIR - 1, last_b, 0.0)
                ds_sc[hh] = ds0[hh]
                dq_ref[rows, sl] = dq[hh]
                dk_ref[rows, sl] = dk[hh]
                dv_ref[rows, sl] = dvb[hh] * beta[hh]
                db_ref[rows, sl] = jnp.broadcast_to(rowsum(dkb[hh] * kv[hh]) + rowsum(dvb[hh] * vv[hh]), (PAIR, LANES))
                dg_ref[rows, sl] = dgc
            return 0

        lax.fori_loop(0, npair, pair, 0)

    blk, row, st = _gdn_specs(t, ts, lambda s: ns - 1 - s)
    out = jax.ShapeDtypeStruct((t, 1024), F32)
    return pl.pallas_call(
        body, name=name, grid=(GDN_HEADS // GDN_HP, ns), in_specs=[blk, blk, blk, blk, row, blk, blk, st],
        out_specs=[blk] * 5, out_shape=[out] * 5, scratch_shapes=[pltpu.VMEM((GDN_HP, LANES, LANES), F32)],
        compiler_params=_params(("parallel", "arbitrary")),
    )(q, k, v, gcb, gct, bb, do, states)


def _gate_out_proj_loss(o, proj_c, o_norm, w, hres, g, target, *, name):
    t, d = hres.shape
    tm = _tile(t, 2 * ROW_TILE)

    def body(o_ref, z_ref, on_ref, w_ref, h_ref, g_ref, t_ref, dh_ref, dhb_ref, y_ref, dg_ref, loss_ref):
        i = pl.program_id(0)
        for hd in range(GDN_HEADS):
            sl = slice(hd * LANES, (hd + 1) * LANES)
            ov = o_ref[:, sl]
            rr = lax.rsqrt(jnp.mean(ov * ov, axis=-1, keepdims=True) + RMS_EPS)
            z = z_ref[:, sl]
            y_ref[:, sl] = (ov * rr * on_ref[...] * (z * _sigmoid(z))).astype(BF16)
        x = h_ref[...] + _dot(y_ref[...], w_ref[...])
        r = lax.rsqrt(jnp.mean(x * x, axis=-1, keepdims=True) + RMS_EPS)
        xh = x * r
        err = xh * g_ref[...] - t_ref[...]
        dy = err * (1.0 / d)
        dxh = dy * g_ref[...]
        dh = r * (dxh - xh * jnp.mean(dxh * xh, axis=-1, keepdims=True))
        dh_ref[...] = dh
        dhb_ref[...] = dh.astype(BF16)

        @pl.when(i == 0)
        def _():
            dg_ref[...] = jnp.zeros_like(dg_ref)
            loss_ref[...] = jnp.zeros_like(loss_ref)

        dg_ref[...] += jnp.sum(dy * xh, axis=0, keepdims=True)
        part = 0.5 * jnp.sum(jnp.mean(err * err, axis=-1, keepdims=True), axis=0, keepdims=True)
        loss_ref[...] += jnp.broadcast_to(part, loss_ref.shape)

    row = pl.BlockSpec((tm, d), lambda i: (i, 0))
    vec = pl.BlockSpec((1, d), lambda i: (0, 0))
    return pl.pallas_call(
        body, name=name, grid=(t // tm,),
        in_specs=[row, pl.BlockSpec((tm, 1024), lambda i: (i, 3)), pl.BlockSpec((1, LANES), lambda i: (0, 0)),
                  pl.BlockSpec(w.shape, lambda i: (0, 0)), row, vec, row],
        out_specs=[row, row, row, vec, pl.BlockSpec((8, LANES), lambda i: (0, 0))],
        out_shape=[jax.ShapeDtypeStruct((t, d), F32), jax.ShapeDtypeStruct((t, d), BF16), jax.ShapeDtypeStruct((t, d), BF16),
                   jax.ShapeDtypeStruct((1, d), F32), jax.ShapeDtypeStruct((8, LANES), F32)],
        compiler_params=_params(("arbitrary",)),
    )(o, proj_c, o_norm, w, hres, g, target)


def _pad_cols(w, n):
    return jnp.pad(w, ((0, 0), (0, n - w.shape[1])))


def _layout_odd(w):
    return dict(
        winc=_pad_cols(w["w_in_c"], IN_C_PAD).astype(BF16), wout_c=w["w_out_c"].astype(BF16), conv_w=w["conv_w"],
        a_log=_pad_cols(w["a_log"], LANES), dt_bias=_pad_cols(w["dt_bias"], LANES),
        norm_c=w["norm_c"], o_norm=w["o_norm"], final_norm=w["final_norm"],
    )


def _layout_in_ab(w):
    z = lambda r, c: jnp.zeros((r, c), w["w_in_ab"].dtype)
    wi = w["w_in_ab"]
    win = jnp.concatenate([wi[:, :384], z(1024, 64), wi[:, 384:416], z(1024, 32), wi[:, 416:]], axis=1)
    pw = w["pool_w"]
    rows = []
    for g in range(4):
        rows.append(jnp.concatenate([pw[g] if j == g else jnp.zeros((128, 128), F32) for j in range(4)], axis=1))
    wpool = jnp.concatenate(rows, axis=0)
    half = MLA_ROPE // 2
    inv = 1.0 / (ROPE_THETA ** (jnp.arange(half, dtype=F32) / half))
    inv_lane = jnp.concatenate([jnp.zeros((MLA_NOPE,), F32), inv, inv, jnp.zeros((32,), F32)]).reshape(1, LANES)
    return dict(win=win.astype(BF16), wpool=wpool.astype(BF16), inv_lane=inv_lane, norm_ab=w["norm_ab"],
                q_a_norm=w["q_a_norm"], kv_a_norm=w["kv_a_norm"], pool_scale=w["pool_scale"])


def _layout_mid(w):
    wq = jnp.pad(w["w_q_b"].reshape(MLA_Q_RANK, MLA_HEADS, 96), ((0, 0), (0, 0), (0, 32))).reshape(MLA_Q_RANK, 1024)
    kv3 = w["w_kv_b"].reshape(MLA_KV_RANK, MLA_HEADS, 128)
    wk = jnp.pad(kv3[..., :MLA_NOPE], ((0, 0), (0, 0), (0, 64))).reshape(MLA_KV_RANK, 1024)
    wv = kv3[..., MLA_NOPE:].reshape(MLA_KV_RANK, 512)
    return dict(wq=wq.astype(BF16), wk=wk.astype(BF16), wv=wv.astype(BF16), wout_ab=w["w_out_ab"].astype(BF16))


def _unlayout_grads(g, names):
    out = {}
    for name in names:
        if name == "w_in_ab":
            dwin = g["win"]
            out[name] = jnp.concatenate([dwin[:384], dwin[448:480], dwin[512:]], axis=0)
        elif name == "w_q_b":
            out[name] = g["wq"].reshape(MLA_Q_RANK, MLA_HEADS, 128)[..., :96].reshape(MLA_Q_RANK, 768)
        elif name == "w_kv_b":
            out[name] = jnp.concatenate([g["wk"].reshape(MLA_KV_RANK, MLA_HEADS, 128)[..., :MLA_NOPE],
                                         g["wv"].reshape(MLA_KV_RANK, MLA_HEADS, MLA_V)], axis=-1).reshape(MLA_KV_RANK, 1024)
        elif name == "w_in_c":
            out[name] = g["winc"][:4112]
        else:
            out[name] = g[{"w_out_ab": "wout_ab", "w_out_c": "wout_c"}[name]]
    return out


def _local_step(x, pos, target, lw, more_weights=None, on_grads=None):
    mm = _matmul
    proj, hn = _rms_in_proj(x, lw["norm_ab"], lw["win"], name="rms_in_ab")
    if more_weights is not None:
        lw = {**lw, **more_weights("mid", proj)}
    q, k, v, ybraw, qn, kvn, d, cos_t, sin_t = _ab_prep(
        proj, pos, lw["inv_lane"], lw["q_a_norm"], lw["kv_a_norm"], lw["wq"], lw["wk"], lw["wv"], lw["wpool"], name="ab_prep")
    o, lse = _attn_fwd(q, k, v, name="attn_fwd")
    h1, y = _gate_out_proj(o, ybraw, proj, lw["pool_scale"], lw["wout_ab"], x, name="gate_out_ab")
    lo = lw if more_weights is None else more_weights("odd", h1)
    proj_c, hn1 = _rms_in_proj(h1, lo["norm_c"], lo["winc"], name="rms_in_c")
    q2, k2, v2, gb, bb, gt = _c_prep(proj_c, lo["conv_w"], lo["a_log"], lo["dt_bias"], name="c_prep")
    gt = gt.reshape(GDN_HEADS, 1, gt.shape[1])
    o2, states = _gdn_fwd(q2, k2, v2, gb, gt, bb, name="gdn_fwd")
    dh2, dh2b, y2, d_final, loss = _gate_out_proj_loss(
        o2, proj_c, lo["o_norm"], lo["wout_c"], h1, lo["final_norm"], target, name="gate_out_c_loss")
    g = {"final_norm": d_final}
    dy2 = mm(dh2b, lo["wout_c"], "nt", name="out_c_dx")
    g["wout_c"] = mm(y2, dh2b, "tn", name="out_c_dw")
    do2, dz2, g["o_norm"] = _o_gate_bwd(dy2, o2, proj_c, lo["o_norm"], name="gate_c_bwd")
    dq2, dk2, dv2, dgb, dbb = _gdn_bwd(q2, k2, v2, gb, gt, bb, do2, states, name="gdn_bwd")
    dproj_c, g["conv_w"], g["a_log"], g["dt_bias"] = _c_prep_bwd(
        proj_c, lo["conv_w"], lo["a_log"], lo["dt_bias"], dq2, dk2, dv2, dgb, dbb, dz2, name="c_prep_bwd")
    g["winc"] = mm(dproj_c, hn1, "tn", name="in_c_dw")
    notify = (lambda tag: 0.0) if on_grads is None else (lambda tag: on_grads(tag, g))
    pool_scale = lw["pool_scale"] + notify("odd")
    dh1, dh1b, g["norm_c"] = _matmul_rms_bwd(dproj_c, lo["winc"], h1, lo["norm_c"], dh2, name="in_c_dx_rms", with_bf16=True)
    dy = mm(dh1b, lw["wout_ab"], "nt", name="out_ab_dx")
    g["wout_ab"] = mm(y, dh1b, "tn", name="out_ab_dw")
    pool_scale = pool_scale + notify("out_ab")
    do, delta, dyb, dz, g["pool_scale"] = _gate_bwd(dy, o, ybraw, proj, pool_scale, name="gate_ab_bwd")
    dq, dk, dv = _attn_bwd(q, k, v, do, lse, delta, name="attn_bwd")
    dproj, dqraw, dkb, g["q_a_norm"], g["kv_a_norm"] = _ab_prep_bwd(
        proj, lw["q_a_norm"], lw["kv_a_norm"], dq, dk, dv, cos_t, sin_t, dyb, dz,
        lw["wq"], lw["wk"], lw["wv"], lw["wpool"], name="ab_prep_bwd")
    g["wpool"] = mm(d, dyb, "tn", name="pool_mix_dw")
    g["wq"] = mm(qn, dqraw, "tn", name="q_up_dw")
    g["wk"] = mm(kvn, dkb, "tn", name="k_up_dw")
    g["wv"] = mm(kvn, dv, "tn", name="v_up_dw")
    g["win"] = mm(dproj, hn, "tn", name="in_ab_dw")
    norm_ab = lw["norm_ab"] + notify("in_ab")
    dx, g["norm_ab"] = _matmul_rms_bwd(dproj, lw["win"], x, norm_ab, dh1, name="in_ab_dx_rms", with_bf16=False)
    return loss, dx, g


_HBM = pl.BlockSpec(memory_space=pltpu.HBM)


def _place():
    return lax.axis_index("x"), lax.axis_index("y"), lax.axis_index("c")


def _flip(v, f):
    return 1 - v if f else v


_CHIP_FLIPS = ((1, 0), (0, 1), (1, 1))
_DEV_FLIPS = tuple((fx, fy, fc) for fx in (0, 1) for fy in (0, 1) for fc in (0, 1) if fx or fy or fc)


def _rcopy(src, dst, send_sems, recv_sems, k, to):
    return pltpu.make_async_remote_copy(src_ref=src, dst_ref=dst, send_sem=send_sems.at[k], recv_sem=recv_sems.at[k],
                                        device_id=to, device_id_type=MESH)


def _my_half(ref, c, axis):
    rh = ref.shape[axis] // 2
    idx = [slice(None)] * len(ref.shape)
    idx[axis] = pl.ds(c * rh, rh)
    return ref.at[tuple(idx)]


def _gather_weights(bigs, smalls):
    nb, ns = len(bigs), len(smalls)

    def body(*refs):
        ins, outs = refs[:nb + ns], refs[nb + ns:2 * (nb + ns)]
        send_sems, recv_sems, local_sems = refs[2 * (nb + ns):]
        x, y, c = _place()
        j0 = 2 * x + y
        sib = (x, y, 1 - c)
        chips = [(_flip(x, fx), _flip(y, fy)) for fx, fy in _CHIP_FLIPS]
        local = [pltpu.make_async_copy(i_ref, o_ref.at[j0], local_sems.at[a])
                 for a, (i_ref, o_ref) in enumerate(zip(ins, outs))]
        for cp in local:
            cp.start()
        sends = []
        for k, (px, py) in enumerate(chips):
            for a in range(nb):
                sends.append(_rcopy(_my_half(ins[a], c, 0), _my_half(outs[a].at[j0], c, 0), send_sems, recv_sems,
                                    6 * a + k, (px, py, c)))
            for s in range(ns):
                sends.append(_rcopy(ins[nb + s], outs[nb + s].at[j0], send_sems, recv_sems, 6 * nb + 3 * s + k, (px, py, c)))
        for cp in sends:
            cp.start()
        for k, (px, py) in enumerate(chips):
            jk = 2 * px + py
            for a in range(nb):
                landed = _my_half(outs[a].at[jk], c, 0)
                _rcopy(landed, landed, send_sems, recv_sems, 6 * a + k, (px, py, c)).wait_recv()
                fwd = _rcopy(landed, landed, send_sems, recv_sems, 6 * a + 3 + k, sib)
                fwd.start()
                sends.append(fwd)
        for k, (px, py) in enumerate(chips):
            jk = 2 * px + py
            for a in range(nb):
                other = _my_half(outs[a].at[jk], 1 - c, 0)
                _rcopy(other, other, send_sems, recv_sems, 6 * a + 3 + k, sib).wait_recv()
            for s in range(ns):
                _rcopy(ins[nb + s], outs[nb + s].at[jk], send_sems, recv_sems, 6 * nb + 3 * s + k, (px, py, c)).wait_recv()
        for cp in sends:
            cp.wait_send()
        for cp in local:
            cp.wait()

    arrays = list(bigs) + list(smalls)
    n_sem = 6 * nb + 3 * ns
    return pl.pallas_call(
        body, name="gather_weights", in_specs=[_HBM] * len(arrays), out_specs=[_HBM] * len(arrays),
        out_shape=[jax.ShapeDtypeStruct((4,) + a.shape, a.dtype) for a in arrays],
        scratch_shapes=[pltpu.SemaphoreType.DMA((n_sem,)), pltpu.SemaphoreType.DMA((n_sem,)),
                        pltpu.SemaphoreType.DMA((len(arrays),))],
    )(*arrays)


def _core_swap_partial(gs, by_cols, *, name):
    n = len(gs)

    def body(*refs):
        ins, outs = refs[:n], refs[n:2 * n]
        send_sems, recv_sems = refs[2 * n:]
        x, y, c = _place()
        copies = [_rcopy(_my_half(i_ref, 1 - c, 2 if by_cols[a] else 1), o_ref, send_sems, recv_sems, a, (x, y, 1 - c))
                  for a, (i_ref, o_ref) in enumerate(zip(ins, outs))]
        for cp in copies:
            cp.start()
        for cp in copies:
            cp.wait()

    halved = lambda g, cols: (4, g.shape[1], g.shape[2] // 2) if cols else (4, g.shape[1] // 2, g.shape[2])
    return pl.pallas_call(
        body, name=name, in_specs=[_HBM] * n, out_specs=[_HBM] * n,
        out_shape=[jax.ShapeDtypeStruct(halved(g, cols), g.dtype) for g, cols in zip(gs, by_cols)],
        scratch_shapes=[pltpu.SemaphoreType.DMA((n,)), pltpu.SemaphoreType.DMA((n,))],
    )(*gs)


def _core_swap_sum(fs, by_cols):
    n = len(fs)

    def body(*refs):
        ins, outs = refs[:n], refs[n:2 * n]
        send_sems, recv_sems = refs[2 * n:]
        x, y, c = _place()
        axes = [1 if cols else 0 for cols in by_cols]
        copies = [_rcopy(_my_half(i_ref, c, ax), _my_half(o_ref, c, ax), send_sems, recv_sems, a, (x, y, 1 - c))
                  for a, (i_ref, o_ref, ax) in enumerate(zip(ins, outs, axes))]
        for cp in copies:
            cp.start()
        for a, cp in enumerate(copies):
            cp.wait_send()
            theirs = _my_half(outs[a], 1 - c, axes[a])
            _rcopy(theirs, theirs, send_sems, recv_sems, a, (x, y, 1 - c)).wait_recv()

    return pl.pallas_call(
        body, name="core_swap_sum", in_specs=[_HBM] * n, out_specs=[_HBM] * n,
        out_shape=[jax.ShapeDtypeStruct(f.shape, f.dtype) for f in fs],
        input_output_aliases={a: a for a in range(n)},
        scratch_shapes=[pltpu.SemaphoreType.DMA((n,)), pltpu.SemaphoreType.DMA((n,))],
    )(*fs)


_SEM = pl.BlockSpec(memory_space=pltpu.SEMAPHORE)
_ANY = pl.BlockSpec(memory_space=pl.ANY)
_DATAFLOW = pltpu.SideEffectType.DATAFLOW_SIDE_EFFECTING


def _to_chips_copies(srcs, lands, send_sems, recv_sems, per_chip_slot):
    x, y, c = _place()
    j0 = 2 * x + y
    out = []
    for k, (fx, fy) in enumerate(_CHIP_FLIPS):
        px, py = _flip(x, fx), _flip(y, fy)
        jk = 2 * px + py
        for a, (src, land) in enumerate(zip(srcs, lands)):
            piece = src.at[jk] if per_chip_slot else src
            out.append((_rcopy(piece, land.at[j0], send_sems, recv_sems, 3 * a + k, (px, py, c)),
                        _rcopy(piece, land.at[jk], send_sems, recv_sems, 3 * a + k, (px, py, c))))
    return out


def _to_chips_start(arrays, *, per_chip_slot, name, after=None):
    n = len(arrays)
    lands = [lax.empty((4,) + (a.shape[1:] if per_chip_slot else a.shape), a.dtype) for a in arrays]
    extra = [] if after is None else [after]

    def body(*refs):
        srcs, land_refs, token = refs[:n], refs[n:2 * n], refs[-1]
        send_sems, recv_sems = refs[2 * n + len(extra)], refs[2 * n + len(extra) + 1]
        for send, _ in _to_chips_copies(srcs, land_refs, send_sems, recv_sems, per_chip_slot):
            send.start()
        token[...] = jnp.zeros_like(token)

    held = [pltpu.with_memory_space_constraint(a, pltpu.HBM) for a in list(arrays) + lands]
    return pl.pallas_call(
        body, name=name, in_specs=[_HBM] * (2 * n) + [_ANY] * len(extra),
        out_specs=(_SEM, _SEM, *[_HBM] * (2 * n), pl.BlockSpec(memory_space=pltpu.VMEM)),
        out_shape=(pltpu.SemaphoreType.DMA((3 * n,)), pltpu.SemaphoreType.DMA((3 * n,)),
                   *[pltpu.HBM(a.shape, a.dtype) for a in held], jax.ShapeDtypeStruct((8, LANES), F32)),
        input_output_aliases={i: 2 + i for i in range(2 * n)},
        compiler_params=pltpu.CompilerParams(has_side_effects=_DATAFLOW),
    )(*held, *extra)


def _to_chips_wait(started, after, *, per_chip_slot, name):
    send_sems, recv_sems, held = started[0], started[1], started[2:-1]
    n = len(held) // 2

    def body(*refs):
        srcs, land_refs, s_sems, r_sems = refs[:n], refs[n:2 * n], refs[2 * n], refs[2 * n + 1]
        for send, arrival in _to_chips_copies(srcs, land_refs, s_sems, r_sems, per_chip_slot):
            send.wait_send()
            arrival.wait_recv()

    out = pl.pallas_call(
        body, name=name, in_specs=[_HBM] * (2 * n) + [_SEM, _SEM, _ANY], out_specs=[_HBM] * (2 * n),
        out_shape=[pltpu.HBM(a.shape, a.dtype) for a in held],
        input_output_aliases={i: i for i in range(2 * n)},
        compiler_params=pltpu.CompilerParams(has_side_effects=_DATAFLOW),
    )(*held, send_sems, recv_sems, after)
    return out[n:]


def _chip_exchange(ps, small):
    n = len(ps)
    rs = small.shape[0]

    def body(*refs):
        p_refs, s_ref = refs[:n], refs[n]
        l_refs, ls_ref = refs[n + 1:2 * n + 1], refs[2 * n + 1]
        send_sems, recv_sems, local_sems = refs[2 * n + 2:]
        x, y, c = _place()
        j0 = 2 * x + y
        d0 = 2 * j0 + c
        local = [pltpu.make_async_copy(p.at[j0], l.at[j0], local_sems.at[a]) for a, (p, l) in enumerate(zip(p_refs, l_refs))]
        local.append(pltpu.make_async_copy(s_ref, ls_ref.at[d0], local_sems.at[n]))
        for cp in local:
            cp.start()
        sends = []
        for k, (fx, fy) in enumerate(_CHIP_FLIPS):
            px, py = _flip(x, fx), _flip(y, fy)
            for a in range(n):
                sends.append(_rcopy(p_refs[a].at[2 * px + py], l_refs[a].at[j0], send_sems, recv_sems, 3 * a + k, (px, py, c)))
        for k, (fx, fy, fc) in enumerate(_DEV_FLIPS):
            peer = (_flip(x, fx), _flip(y, fy), _flip(c, fc))
            sends.append(_rcopy(s_ref, ls_ref.at[d0], send_sems, recv_sems, 3 * n + k, peer))
        for cp in sends:
            cp.start()
        for k, (fx, fy) in enumerate(_CHIP_FLIPS):
            px, py = _flip(x, fx), _flip(y, fy)
            for a in range(n):
                _rcopy(p_refs[a].at[j0], l_refs[a].at[2 * px + py], send_sems, recv_sems, 3 * a + k, (px, py, c)).wait_recv()
        for k, (fx, fy, fc) in enumerate(_DEV_FLIPS):
            px, py, pc = _flip(x, fx), _flip(y, fy), _flip(c, fc)
            _rcopy(s_ref, ls_ref.at[4 * px + 2 * py + pc], send_sems, recv_sems, 3 * n + k, (px, py, pc)).wait_recv()
        for cp in sends:
            cp.wait_send()
        for cp in local:
            cp.wait()

    n_sem = 3 * n + 7
    return pl.pallas_call(
        body, name="chip_exchange", in_specs=[_HBM] * (n + 1), out_specs=[_HBM] * (n + 1),
        out_shape=[jax.ShapeDtypeStruct(p.shape, F32) for p in ps] + [jax.ShapeDtypeStruct((8, rs, LANES), F32)],
        scratch_shapes=[pltpu.SemaphoreType.DMA((n_sem,)), pltpu.SemaphoreType.DMA((n_sem,)),
                        pltpu.SemaphoreType.DMA((n + 1,))],
    )(*ps, small)


def _half_blocks(rows, cols, by_cols):
    if by_cols:
        tc = _tile(cols // 2, 256)
        nb = cols // 2 // tc
        return rows, tc, nb, (lambda i, c: (0, c * nb + i))
    tr = _tile(rows // 2, 256)
    nb = rows // 2 // tr
    return tr, cols, nb, (lambda i, c: (c * nb + i, 0))


def _core_sum(g, part, core, *, name, by_cols):
    _, rows, cols = g.shape
    br, bc, nb, whole = _half_blocks(rows, cols, by_cols)
    mine = (lambda i: (0, i)) if by_cols else (lambda i: (i, 0))

    def body(c_ref, g_ref, p_ref, o_ref):
        o_ref[...] = g_ref[...] + p_ref[...]

    grid_spec = pltpu.PrefetchScalarGridSpec(
        num_scalar_prefetch=1, grid=(4, nb),
        in_specs=[pl.BlockSpec((1, br, bc), lambda j, i, c: (j,) + whole(i, c[0])),
                  pl.BlockSpec((1, br, bc), lambda j, i, c: (j,) + mine(i))],
        out_specs=pl.BlockSpec((1, br, bc), lambda j, i, c: (j,) + mine(i)),
    )
    return pl.pallas_call(
        body, name=name, grid_spec=grid_spec, out_shape=jax.ShapeDtypeStruct(part.shape, F32),
        compiler_params=_params(("parallel", "parallel")),
    )(core, g, part)


def _chip_sum(landed, core, *, name, by_cols):
    _, hr, hc = landed.shape
    rows, cols = (hr, 2 * hc) if by_cols else (2 * hr, hc)
    br, bc, nb, whole = _half_blocks(rows, cols, by_cols)
    mine = (lambda i: (0, i)) if by_cols else (lambda i: (i, 0))

    def body(c_ref, l_ref, o_ref):
        o_ref[...] = ((l_ref[0] + l_ref[1]) + l_ref[2]) + l_ref[3]

    grid_spec = pltpu.PrefetchScalarGridSpec(
        num_scalar_prefetch=1, grid=(nb,),
        in_specs=[pl.BlockSpec((4, br, bc), lambda i, c: (0,) + mine(i))],
        out_specs=pl.BlockSpec((br, bc), lambda i, c: whole(i, c[0])),
    )
    return pl.pallas_call(
        body, name=name, grid_spec=grid_spec, out_shape=jax.ShapeDtypeStruct((rows, cols), F32),
        compiler_params=_params(("parallel",)),
    )(core, landed)


_ROW_POOL_W, _ROW_NORM_AB, _ROW_FINAL, _ROW_POOL_SCALE, _ROW_Q_NORM = 0, 512, 520, 528, 532
_ROW_KV_NORM, _ROW_O_NORM, _ROW_A_LOG, _ROW_DT_BIAS, _ROW_LOSS = 534, 535, 536, 537, 538
_ROW_CONV, _ROW_NORM_C, _SMALL_ROWS = 544, 640, 672
_CONV_ROWS = CONV_WIDTH * 6


def _put_rows(dst_ref, row0, src, width):
    for r in range(width // LANES):
        dst_ref[row0 + r:row0 + r + 1, :] = src[:, r * LANES:(r + 1) * LANES]


def _pack_small(g, loss_tile):
    names = ("wpool", "norm_ab", "final_norm", "pool_scale", "q_a_norm", "kv_a_norm", "o_norm", "a_log", "dt_bias",
             "conv_w", "norm_c")

    def body(wpool, norm_ab, final_norm, pool_scale, q_norm, kv_norm, o_norm, a_log, dt_bias, conv_w, norm_c, loss, o_ref):
        o_ref[...] = jnp.zeros_like(o_ref)
        for gi in range(4):
            o_ref[_ROW_POOL_W + gi * 128:_ROW_POOL_W + (gi + 1) * 128, :] = wpool[gi * 128:(gi + 1) * 128, gi * 128:(gi + 1) * 128]
        _put_rows(o_ref, _ROW_NORM_AB, norm_ab[...], 1024)
        _put_rows(o_ref, _ROW_FINAL, final_norm[...], 1024)
        _put_rows(o_ref, _ROW_POOL_SCALE, pool_scale[...], 512)
        _put_rows(o_ref, _ROW_Q_NORM, q_norm[...], 256)
        for row, ref in ((_ROW_KV_NORM, kv_norm), (_ROW_O_NORM, o_norm), (_ROW_A_LOG, a_log), (_ROW_DT_BIAS, dt_bias)):
            o_ref[row:row + 1, :] = ref[...]
        o_ref[_ROW_LOSS:_ROW_LOSS + 1, :] = loss[0:1, :]
        for j in range(4):
            for r in range(CONV_WIDTH):
                _put_rows(o_ref, _ROW_CONV + j * _CONV_ROWS + r * 6, conv_w[r:r + 1, j * 768:(j + 1) * 768], 768)
            _put_rows(o_ref, _ROW_NORM_C + j * 8, norm_c[:, j * 256:(j + 1) * 256], 256)

    vmem = pl.BlockSpec(memory_space=pltpu.VMEM)
    return pl.pallas_call(
        body, name="pack_small", in_specs=[vmem] * 12, out_specs=vmem,
        out_shape=jax.ShapeDtypeStruct((_SMALL_ROWS, LANES), F32),
    )(*[g[n] for n in names], loss_tile)


_SMALL_NAMES = ("pool_w", "norm_ab", "final_norm", "pool_scale", "q_a_norm", "kv_a_norm", "o_norm", "a_log", "dt_bias",
                "conv_w", "norm_c")


def _take_rows(src, row0, width):
    return jnp.concatenate([src[row0 + r:row0 + r + 1, :] for r in range(width // LANES)], axis=1)


def _small_update(small_all, ws, ms, vs):
    n = len(_SMALL_NAMES)

    def body(*refs):
        a_ref = refs[0]
        w_refs, m_refs, v_refs = refs[1:1 + n], refs[1 + n:1 + 2 * n], refs[1 + 2 * n:1 + 3 * n]
        outs = refs[1 + 3 * n:1 + 7 * n]
        loss_ref, tot = refs[1 + 7 * n], refs[2 + 7 * n]
        acc = a_ref[0]
        for d in range(1, 8):
            acc = acc + a_ref[d]
        tot[...] = acc
        x, y, _ = _place()
        j0 = 2 * x + y
        conv = tot[pl.ds(pl.multiple_of(_ROW_CONV + j0 * _CONV_ROWS, 8), _CONV_ROWS), :]
        norm_c = tot[pl.ds(pl.multiple_of(_ROW_NORM_C + j0 * 8, 8), 8), :]
        whole = tot[_ROW_NORM_AB:_ROW_CONV, :]
        at = lambda row: row - _ROW_NORM_AB
        grads = {
            "norm_ab": _take_rows(whole, at(_ROW_NORM_AB), 1024), "final_norm": _take_rows(whole, at(_ROW_FINAL), 1024),
            "pool_scale": _take_rows(whole, at(_ROW_POOL_SCALE), 512), "q_a_norm": _take_rows(whole, at(_ROW_Q_NORM), 256),
            "kv_a_norm": whole[at(_ROW_KV_NORM):at(_ROW_KV_NORM) + 1, :], "o_norm": whole[at(_ROW_O_NORM):at(_ROW_O_NORM) + 1, :],
            "a_log": tot[_ROW_A_LOG:_ROW_A_LOG + 1, 0:GDN_HEADS],
            "dt_bias": tot[_ROW_DT_BIAS:_ROW_DT_BIAS + 1, 0:GDN_HEADS],
            "norm_c": _take_rows(norm_c, 0, 256),
        }
        loss_ref[...] = whole[at(_ROW_LOSS):at(_ROW_LOSS) + 1, :]
        for i, name in enumerate(_SMALL_NAMES):
            g_out = outs[4 * i]
            if name == "pool_w":
                for gi in range(4):
                    g_out[gi] = tot[_ROW_POOL_W + gi * 128:_ROW_POOL_W + (gi + 1) * 128, :]
            elif name == "conv_w":
                for r in range(CONV_WIDTH):
                    g_out[r:r + 1, :] = _take_rows(conv, r * 6, 768)
            else:
                g_out[...] = grads[name]
            _adam_update(g_out, w_refs[i], m_refs[i], v_refs[i], *outs[4 * i + 1:4 * i + 4])

    vmem = pl.BlockSpec(memory_space=pltpu.VMEM)
    out_shape = [jax.ShapeDtypeStruct(w.shape, F32) for w in ws for _ in range(4)] + [jax.ShapeDtypeStruct((1, LANES), F32)]
    return pl.pallas_call(
        body, name="small_update", in_specs=[vmem] * (1 + 3 * n), out_specs=[vmem] * (4 * n + 1), out_shape=out_shape,
        scratch_shapes=[pltpu.VMEM((_SMALL_ROWS, LANES), F32)],
        compiler_params=pltpu.CompilerParams(vmem_limit_bytes=VMEM_LIMIT),
    )(small_all, *ws, *ms, *vs)


def _adam_update(g_ref, w_ref, m_ref, v_ref, d_ref, mo_ref, vo_ref):
    gv = g_ref[...]
    mn = ADAM_B1 * m_ref[...] + (1.0 - ADAM_B1) * gv
    vn = ADAM_B2 * v_ref[...] + (1.0 - ADAM_B2) * (gv * gv)
    mo_ref[...] = mn
    vo_ref[...] = vn
    c1 = 1.0 - ADAM_B1 ** ADAM_STEP
    c2 = 1.0 - ADAM_B2 ** ADAM_STEP
    d_ref[...] = -ADAM_LR * ((mn / c1) / (jnp.sqrt(vn / c2) + ADAM_EPS) + ADAM_WD * w_ref[...])


def _adamw_rows(g, w, m, v, *, name):
    rows, cols = g.shape
    if rows % LANES == 0:
        tr = _tile(rows, 512)
        blk, steps = pl.BlockSpec((tr, cols), lambda i: (i, 0)), rows // tr
    else:
        tc = _tile(cols, 256)
        blk, steps = pl.BlockSpec((rows, tc), lambda i: (0, i)), cols // tc

    def body(*refs):
        _adam_update(*refs)

    out = jax.ShapeDtypeStruct((rows, cols), F32)
    return pl.pallas_call(
        body, name=name, grid=(steps,), in_specs=[blk] * 4, out_specs=[blk] * 3, out_shape=[out] * 3,
        compiler_params=_params(("parallel",)),
    )(g, w, m, v)


_ADAM_ROWWISE = ("w_in_ab", "w_q_b", "w_kv_b", "w_out_ab", "w_in_c", "w_out_c")


_SHARD_AXIS = {"w_in_ab": 1, "w_q_b": 1, "w_kv_b": 1, "w_out_ab": 0, "w_in_c": 1, "w_out_c": 0, "conv_w": 1, "norm_c": 1}
_ALL_NAMES = ("norm_ab", "w_in_ab", "q_a_norm", "w_q_b", "kv_a_norm", "w_kv_b", "pool_w", "pool_scale", "w_out_ab",
              "norm_c", "w_in_c", "conv_w", "a_log", "dt_bias", "o_norm", "w_out_c", "final_norm")


def _join_shards(a, axis):
    _, r, c = a.shape
    return a.reshape(4 * r, c) if axis == 0 else jnp.transpose(a, (1, 0, 2)).reshape(r, 4 * c)


def _split_shards(a, axis):
    r, c = a.shape
    return a.reshape(4, r // 4, c) if axis == 0 else jnp.transpose(a.reshape(r, 4, c // 4), (1, 0, 2))


def kernel(x, positions, norm_ab, w_in_ab, q_a_norm, w_q_b, kv_a_norm, w_kv_b, pool_w, pool_scale, w_out_ab, norm_c, w_in_c, conv_w, a_log, dt_bias, o_norm, w_out_c, final_norm, loss_target, m_norm_ab, m_w_in_ab, m_q_a_norm, m_w_q_b, m_kv_a_norm, m_w_kv_b, m_pool_w, m_pool_scale, m_w_out_ab, m_norm_c, m_w_in_c, m_conv_w, m_a_log, m_dt_bias, m_o_norm, m_w_out_c, m_final_norm, v_norm_ab, v_w_in_ab, v_q_a_norm, v_w_q_b, v_kv_a_norm, v_w_kv_b, v_pool_w, v_pool_scale, v_w_out_ab, v_norm_c, v_w_in_c, v_conv_w, v_a_log, v_dt_bias, v_o_norm, v_w_out_c, v_final_norm):
    given = dict(locals())
    c = lax.axis_index("c")
    t = x.shape[1]

    def shard_of(prefix, name):
        a = given[prefix + name]
        return a.reshape(a.shape[1:]) if a.ndim > 2 else a.reshape(1, -1)

    big, big_even, big_odd, small_sharded = _ADAM_ROWWISE, _ADAM_ROWWISE[:4], _ADAM_ROWWISE[4:], ("conv_w", "norm_c")
    chip = 2 * lax.axis_index("x") + lax.axis_index("y")
    core = c.astype(jnp.int32).reshape(1)
    later = {"mid": big_even[1:], "odd": big_odd + small_sharded}
    travelling = {}

    def send(tag, after=None):
        shards = [shard_of("", n).astype(BF16) if n in big else shard_of("", n) for n in later[tag]]
        started = _to_chips_start(shards, per_chip_slot=False, name="gather_" + tag + "_start", after=after)
        travelling[tag] = (shards, started)
        return started[-1][0, 0]

    mid_sent = send("mid")
    gathered = _gather_weights([shard_of("", "w_in_ab").astype(BF16)], [])
    full = {"w_in_ab": _join_shards(gathered[0], _SHARD_AXIS["w_in_ab"])}
    for name in ("norm_ab", "q_a_norm", "kv_a_norm", "pool_w", "pool_scale"):
        full[name] = shard_of("", name)
    lw = _layout_in_ab(full)
    lw["norm_ab"] = lw["norm_ab"] + mid_sent

    def more_weights(tag, after):
        shards, started = travelling[tag]
        landed = _to_chips_wait(started, after, per_chip_slot=False, name="gather_" + tag + "_wait")
        w = {}
        for name, land, own in zip(later[tag], landed, shards):
            w[name] = _join_shards(lax.dynamic_update_index_in_dim(land, own, chip, 0), _SHARD_AXIS[name])
        if tag == "mid":
            out = _layout_mid(w)
            out["wq"] = out["wq"] + send("odd", after=landed[0]).astype(BF16)
            return out
        for name in ("a_log", "dt_bias", "o_norm", "final_norm"):
            w[name] = shard_of("", name)
        return _layout_odd(w)

    transposed = ("w_in_ab", "w_in_c")

    def chip_partials(names, grads, tag):
        by_cols = [n in transposed for n in names]
        slots = [_split_shards(grads[n], 0 if n in transposed else _SHARD_AXIS[n]) for n in names]
        partial = _core_swap_partial(slots, by_cols, name="core_swap_partial_" + tag)
        return [_core_sum(s, p, core, name="core_sum_" + n, by_cols=b) for n, s, p, b in zip(names, slots, partial, by_cols)]

    groups = {"odd": big_odd, "out_ab": ("w_out_ab",), "in_ab": ("w_in_ab", "w_q_b", "w_kv_b")}
    sent = {}

    def on_grads(tag, g):
        part = chip_partials(groups[tag], _unlayout_grads(g, groups[tag]), tag)
        sent[tag] = (part, _to_chips_start(part, per_chip_slot=True, name="exchange_" + tag + "_start"))
        return sent[tag][1][-1][0, 0]

    loss_tile, dx, g = _local_step(x[0], positions.reshape(t, 1), loss_target[0], lw, more_weights, on_grads)
    small_all = _chip_exchange([], _pack_small(g, loss_tile))[-1]
    halves = {}
    for tag, names in groups.items():
        part, started = sent[tag]
        landed = _to_chips_wait(started, small_all, per_chip_slot=True, name="exchange_" + tag + "_wait")
        for n, l, p in zip(names, landed, part):
            l = lax.dynamic_update_index_in_dim(l, lax.dynamic_index_in_dim(p, chip, 0, keepdims=False), chip, 0)
            halves[n] = _chip_sum(l, core, name="chip_sum_" + n, by_cols=n in transposed)
    gbig = dict(zip(big, _core_swap_sum([halves[n] for n in big], [n in transposed for n in big])))

    res = {}
    for name in big:
        operands = [gbig[name], shard_of("", name), shard_of("m_", name), shard_of("v_", name)]
        flip = name in transposed
        if flip:
            operands[1:] = [jnp.transpose(a) for a in operands[1:]]
        out = (operands[0],) + tuple(_adamw_rows(*operands, name="adamw_" + name))
        out = [jnp.transpose(a) for a in out] if flip else out
        res["grad", name], res["delta", name], res["m", name], res["v", name] = out
    out = _small_update(small_all, [shard_of("", n) for n in _SMALL_NAMES], [shard_of("m_", n) for n in _SMALL_NAMES],
                        [shard_of("v_", n) for n in _SMALL_NAMES])
    for i, name in enumerate(_SMALL_NAMES):
        res["grad", name], res["delta", name], res["m", name], res["v", name] = out[4 * i:4 * i + 4]
    res = {k: a.reshape(given[k[1]].shape) for k, a in res.items()}
    loss = out[-1][0, 0]
    outs = [loss, dx.reshape(x.shape)]
    for key in ("grad", "delta", "m", "v"):
        outs += [res[key, n] for n in _ALL_NAMES]
    return tuple(outs)
```

```python
import functools

import jax
import jax.numpy as jnp
from jax import lax
from jax.experimental import pallas as pl
from jax.experimental.pallas import tpu as pltpu

F32 = jnp.float32
BF16 = jnp.bfloat16
HI = lax.Precision.HIGHEST
MESH = pl.DeviceIdType.MESH

RMS_EPS = 1e-6
MLA_HEADS = 8
MLA_Q_RANK = 256
MLA_KV_RANK = 128
MLA_NOPE = 64
MLA_ROPE = 32
MLA_V = 64
ROPE_THETA = 10000.0
POOL_WINDOWS = (2, 4, 8, 16)
POOL_GROUP = 128
POOL_WIDTH = 512
POOL_HALO = 16
GDN_HEADS = 8
GDN_DK = 128
CONV_WIDTH = 4
CONV_HALO = 8
CHUNK = 64
IN_AB_PAD = 2048
IN_C_PAD = 4224
ATT_SCALE = (MLA_NOPE + MLA_ROPE) ** -0.5
LOG2E = 1.4426950408889634

ADAM_LR = 0.001
ADAM_B1 = 0.9
ADAM_B2 = 0.999
ADAM_EPS = 1e-08
ADAM_WD = 0.01
ADAM_STEP = 10

LANES = 128
VMEM_LIMIT = 56 * 1024 * 1024

ROW_TILE = 256
ATT_TILE = 1024
GDN_TILE = 256
MM_TILE = (1024, 1408, 2048)

NN = (((1,), (0,)), ((), ()))
NT = (((1,), (1,)), ((), ()))
TN = (((0,), (0,)), ((), ()))


def _dot(a, b, dims=NN, prec=None):
    return lax.dot_general(a, b, dims, precision=prec, preferred_element_type=F32)


def _tile(n, pref):
    if n <= pref:
        return n
    step = LANES if pref >= LANES else 8
    for t in range(pref - pref % step, 0, -step):
        if n % t == 0:
            return t
    return n


def _params(sem):
    return pltpu.CompilerParams(dimension_semantics=sem, vmem_limit_bytes=VMEM_LIMIT)


def _sigmoid(x):
    return 0.5 * jnp.tanh(0.5 * x) + 0.5


def _softplus(x):
    return jnp.maximum(x, 0.0) + jnp.log(1.0 + jnp.exp(-jnp.abs(x)))


def _matmul(a, b, mode, *, name):
    if mode == "nn":
        (m, k), (k2, n) = a.shape, b.shape
    elif mode == "nt":
        (m, k), (n, k2) = a.shape, b.shape
    else:
        (k, m), (k2, n) = a.shape, b.shape
    assert k == k2, (a.shape, b.shape, mode)
    tm, tn, tk = _tile(m, MM_TILE[0]), _tile(n, MM_TILE[1]), _tile(k, MM_TILE[2])
    nk = k // tk
    if mode == "tn":
        a_spec = pl.BlockSpec((tk, tm), lambda i, j, kk: (kk, i))
    else:
        a_spec = pl.BlockSpec((tm, tk), lambda i, j, kk: (i, kk))
    if mode == "nt":
        b_spec = pl.BlockSpec((tn, tk), lambda i, j, kk: (j, kk))
    else:
        b_spec = pl.BlockSpec((tk, tn), lambda i, j, kk: (kk, j))
    o_spec = pl.BlockSpec((tm, tn), lambda i, j, kk: (i, j))
    dims = {"nn": NN, "nt": NT, "tn": TN}[mode]

    def body(a_ref, b_ref, o_ref, *scratch):
        if nk == 1:
            o_ref[...] = _dot(a_ref[...], b_ref[...], dims)
            return
        acc = scratch[0]
        kk = pl.program_id(2)

        @pl.when(kk == 0)
        def _():
            acc[...] = jnp.zeros_like(acc)

        acc[...] += _dot(a_ref[...], b_ref[...], dims)

        @pl.when(kk == nk - 1)
        def _():
            o_ref[...] = acc[...]

    return pl.pallas_call(
        body, name=name, grid=(m // tm, n // tn, nk), in_specs=[a_spec, b_spec], out_specs=o_spec,
        out_shape=jax.ShapeDtypeStruct((m, n), F32),
        scratch_shapes=[pltpu.VMEM((tm, tn), F32)] if nk > 1 else [],
        compiler_params=_params(("parallel", "parallel", "arbitrary")),
    )(a, b)


def _rms_in_proj(h, g, w, *, name):
    t, d = h.shape
    n = w.shape[1]
    tm, tn = _tile(t, MM_TILE[0]), _tile(n, MM_TILE[1])

    def body(h_ref, g_ref, w_ref, o_ref, hn_ref):
        @pl.when(pl.program_id(1) == 0)
        def _():
            x = h_ref[...]
            r = lax.rsqrt(jnp.mean(x * x, axis=-1, keepdims=True) + RMS_EPS)
            hn_ref[...] = (x * r * g_ref[...]).astype(BF16)

        o_ref[...] = _dot(hn_ref[...], w_ref[...])

    return pl.pallas_call(
        body, name=name, grid=(t // tm, n // tn),
        in_specs=[pl.BlockSpec((tm, d), lambda i, j: (i, 0)), pl.BlockSpec((1, d), lambda i, j: (0, 0)),
                  pl.BlockSpec((d, tn), lambda i, j: (0, j))],
        out_specs=[pl.BlockSpec((tm, tn), lambda i, j: (i, j)), pl.BlockSpec((tm, d), lambda i, j: (i, 0))],
        out_shape=[jax.ShapeDtypeStruct((t, n), F32), jax.ShapeDtypeStruct((t, d), BF16)],
        compiler_params=_params(("parallel", "arbitrary")),
    )(h, g, w)


def _matmul_rms_bwd(dproj, w, h, g, dres, *, name, with_bf16):
    t, k = dproj.shape
    d = w.shape[0]
    tm = _tile(t, 2 * ROW_TILE)

    def body(dp_ref, w_ref, h_ref, g_ref, dres_ref, *outs):
        i = pl.program_id(0)
        dh_ref, dg_ref = outs[0], outs[-1]
        dyv = _dot(dp_ref[...], w_ref[...], NT)
        x = h_ref[...]
        r = lax.rsqrt(jnp.mean(x * x, axis=-1, keepdims=True) + RMS_EPS)
        xh = x * r
        dxh = dyv * g_ref[...]
        dh = dres_ref[...] + r * (dxh - xh * jnp.mean(dxh * xh, axis=-1, keepdims=True))
        dh_ref[...] = dh
        if with_bf16:
            outs[1][...] = dh.astype(BF16)

        @pl.when(i == 0)
        def _():
            dg_ref[...] = jnp.zeros_like(dg_ref)

        dg_ref[...] += jnp.sum(dyv * xh, axis=0, keepdims=True)

    row = pl.BlockSpec((tm, d), lambda i: (i, 0))
    vec = pl.BlockSpec((1, d), lambda i: (0, 0))
    out_shape = [jax.ShapeDtypeStruct((t, d), F32)] + ([jax.ShapeDtypeStruct((t, d), BF16)] if with_bf16 else [])
    out_specs = [row] * len(out_shape) + [vec]
    out_shape.append(jax.ShapeDtypeStruct((1, d), F32))
    return pl.pallas_call(
        body, name=name, grid=(t // tm,),
        in_specs=[pl.BlockSpec((tm, k), lambda i: (i, 0)), pl.BlockSpec((d, k), lambda i: (0, 0)), row, vec, row],
        out_specs=out_specs, out_shape=out_shape, compiler_params=_params(("arbitrary",)),
    )(dproj, w, h, g, dres)


def _rope_partner(x):
    lane = lax.broadcasted_iota(jnp.int32, x.shape, 1)
    swapped = jnp.where(lane < MLA_NOPE + MLA_ROPE // 2, pltpu.roll(x, LANES - 16, 1), pltpu.roll(x, 16, 1))
    return jnp.where((lane >= MLA_NOPE) & (lane < MLA_NOPE + MLA_ROPE), swapped, 0.0)


def _pool_counts(row0, tm, w):
    t_idx = row0 + lax.broadcasted_iota(jnp.int32, (tm, POOL_GROUP), 0)
    return jnp.minimum(t_idx + 1, w).astype(F32)


def _ab_prep(proj, pos, inv_freq, q_a_norm, kv_a_norm, wq, wk, wv, wpool, *, name):
    t = proj.shape[0]
    tm = _tile(t, ROW_TILE)
    hb = tm // POOL_HALO

    def body(p_ref, halo_ref, pos_ref, inv_ref, qg_ref, kg_ref, wq_ref, wk_ref, wv_ref, wp_ref,
             q_ref, k_ref, v_ref, yb_ref, qn_ref, kvn_ref, d_ref, cos_ref, sin_ref, ext):
        i = pl.program_id(0)
        ql = p_ref[:, 0:MLA_Q_RANK]
        r = lax.rsqrt(jnp.mean(ql * ql, axis=-1, keepdims=True) + RMS_EPS)
        qn = (ql * r * qg_ref[...]).astype(BF16)
        qn_ref[...] = qn
        kl = p_ref[:, MLA_Q_RANK:MLA_Q_RANK + MLA_KV_RANK]
        r = lax.rsqrt(jnp.mean(kl * kl, axis=-1, keepdims=True) + RMS_EPS)
        kvn = (kl * r * kg_ref[...]).astype(BF16)
        kvn_ref[...] = kvn
        ang = pos_ref[...].astype(F32) * inv_ref[...]
        lane = lax.broadcasted_iota(jnp.int32, (tm, LANES), 1)
        in_rope = (lane >= MLA_NOPE) & (lane < MLA_NOPE + MLA_ROPE)
        cos_t = jnp.where(in_rope, jnp.cos(ang), 1.0)
        sin_t = jnp.where(in_rope, jnp.sin(ang), 0.0)
        sin_t = jnp.where(lane < MLA_NOPE + MLA_ROPE // 2, -sin_t, sin_t)
        cos_ref[...] = cos_t
        sin_ref[...] = sin_t
        kr = p_ref[:, 384:512]
        kr = kr * cos_t + _rope_partner(kr) * sin_t
        qraw = _dot(qn, wq_ref[...])
        kvk = _dot(kvn, wk_ref[...])
        for h in range(MLA_HEADS):
            sl = slice(h * LANES, (h + 1) * LANES)
            qh = qraw[:, sl]
            q_ref[:, sl] = ((qh * cos_t + _rope_partner(qh) * sin_t) * (ATT_SCALE * LOG2E)).astype(BF16)
            k_ref[:, sl] = (kvk[:, sl] + kr).astype(BF16)
        v_ref[...] = _dot(kvn, wv_ref[...]).astype(BF16)
        xp = p_ref[:, 512:1024]
        ext[0:POOL_HALO, :] = jnp.where(i > 0, halo_ref[...], 0.0)
        ext[POOL_HALO:POOL_HALO + tm, :] = xp
        for g, w in enumerate(POOL_WINDOWS):
            lo = g * POOL_GROUP
            acc = ext[POOL_HALO:POOL_HALO + tm, lo:lo + POOL_GROUP]
            for s in range(1, w):
                acc = acc + ext[POOL_HALO - s:POOL_HALO - s + tm, lo:lo + POOL_GROUP]
            cnt = _pool_counts(i * tm, tm, w)
            d_ref[:, lo:lo + POOL_GROUP] = (acc / cnt - xp[:, lo:lo + POOL_GROUP]).astype(BF16)
        yb_ref[...] = _dot(d_ref[...], wp_ref[...])

    row = lambda w: pl.BlockSpec((tm, w), lambda i: (i, 0))
    vec = lambda w: pl.BlockSpec((1, w), lambda i: (0, 0))
    whole = lambda a: pl.BlockSpec(a.shape, lambda i: (0, 0))
    return pl.pallas_call(
        body, name=name, grid=(t // tm,),
        in_specs=[row(1024), pl.BlockSpec((POOL_HALO, POOL_WIDTH), lambda i: (jnp.maximum(i * hb - 1, 0), 1)),
                  pl.BlockSpec((tm, 1), lambda i: (i, 0)), vec(LANES), vec(MLA_Q_RANK), vec(MLA_KV_RANK),
                  whole(wq), whole(wk), whole(wv), whole(wpool)],
        out_specs=[row(1024), row(1024), row(512), row(512), row(MLA_Q_RANK), row(MLA_KV_RANK), row(POOL_WIDTH),
                   row(LANES), row(LANES)],
        out_shape=[jax.ShapeDtypeStruct((t, 1024), BF16), jax.ShapeDtypeStruct((t, 1024), BF16),
                   jax.ShapeDtypeStruct((t, 512), BF16), jax.ShapeDtypeStruct((t, 512), F32),
                   jax.ShapeDtypeStruct((t, MLA_Q_RANK), BF16), jax.ShapeDtypeStruct((t, MLA_KV_RANK), BF16),
                   jax.ShapeDtypeStruct((t, POOL_WIDTH), BF16), jax.ShapeDtypeStruct((t, LANES), F32),
                   jax.ShapeDtypeStruct((t, LANES), F32)],
        scratch_shapes=[pltpu.VMEM((tm + POOL_HALO, POOL_WIDTH), F32)],
        compiler_params=_params(("parallel",)),
    )(proj, proj, pos, inv_freq, q_a_norm, kv_a_norm, wq, wk, wv, wpool)


def _ab_prep_bwd(proj, q_a_norm, kv_a_norm, dq, dk, dv, cos_t, sin_t, dyb, dz, wq, wk, wv, wpool, *, name):
    t = proj.shape[0]
    tm = _tile(t, ROW_TILE)
    hb = tm // POOL_HALO
    last_halo = t // POOL_HALO - 1
    nt = t // tm

    def body(p_ref, qg_ref, kg_ref, dq_ref, dk_ref, dv_ref, c_ref, s_ref, dyb_ref, dybn_ref, dz_ref,
             wq_ref, wk_ref, wv_ref, wp_ref, dp_ref, dqr_ref, dkb_ref, dqg_ref, dkg_ref, ext):
        i = pl.program_id(0)

        @pl.when(i == 0)
        def _():
            dqg_ref[...] = jnp.zeros_like(dqg_ref)
            dkg_ref[...] = jnp.zeros_like(dkg_ref)

        def norm_bwd(x, g, dy, dg_ref):
            r = lax.rsqrt(jnp.mean(x * x, axis=-1, keepdims=True) + RMS_EPS)
            xh = x * r
            dxh = dy * g
            dg_ref[...] += jnp.sum(dy * xh, axis=0, keepdims=True)
            return r * (dxh - xh * jnp.mean(dxh * xh, axis=-1, keepdims=True))

        c, s = c_ref[...], s_ref[...]
        lane = lax.broadcasted_iota(jnp.int32, (tm, LANES), 1)
        in_rope = (lane >= MLA_NOPE) & (lane < MLA_NOPE + MLA_ROPE)
        dkr = jnp.zeros((tm, LANES), F32)
        for h in range(MLA_HEADS):
            sl = slice(h * LANES, (h + 1) * LANES)
            g = dq_ref[:, sl]
            dqr_ref[:, sl] = ((g * c + _rope_partner(g * s)) * ATT_SCALE).astype(BF16)
            gk = dk_ref[:, sl]
            dkb_ref[:, sl] = gk.astype(BF16)
            dkr = dkr + jnp.where(in_rope, gk, 0.0)
        dkr = dkr * c + _rope_partner(dkr * s)
        dqn = _dot(dqr_ref[...], wq_ref[...], NT)
        dkvn = _dot(dkb_ref[...], wk_ref[...], NT) + _dot(dv_ref[...], wv_ref[...], NT)
        dql = norm_bwd(p_ref[:, 0:MLA_Q_RANK], qg_ref[...], dqn, dqg_ref)
        dp_ref[:, 0:MLA_Q_RANK] = dql.astype(BF16)
        dkl = norm_bwd(p_ref[:, MLA_Q_RANK:384], kg_ref[...], dkvn, dkg_ref)
        dp_ref[:, MLA_Q_RANK:384] = dkl.astype(BF16)
        dp_ref[:, 384:512] = dkr.astype(BF16)
        ddv = _dot(dyb_ref[...], wp_ref[...], NT)
        ddn = _dot(dybn_ref[...], wp_ref[...], NT)
        for g, w in enumerate(POOL_WINDOWS):
            lo = g * POOL_GROUP
            ext[0:tm, lo:lo + POOL_GROUP] = ddv[:, lo:lo + POOL_GROUP] / _pool_counts(i * tm, tm, w)
            nxt = ddn[:, lo:lo + POOL_GROUP] / _pool_counts((i + 1) * tm, POOL_HALO, w)
            ext[tm:tm + POOL_HALO, lo:lo + POOL_GROUP] = jnp.where(i < nt - 1, nxt, 0.0)
        for g, w in enumerate(POOL_WINDOWS):
            lo = g * POOL_GROUP
            acc = ext[0:tm, lo:lo + POOL_GROUP]
            for s in range(1, w):
                acc = acc + ext[s:s + tm, lo:lo + POOL_GROUP]
            dp_ref[:, 512 + lo:512 + lo + POOL_GROUP] = (acc - ddv[:, lo:lo + POOL_GROUP]).astype(BF16)
        dp_ref[:, 1024:2048] = dz_ref[...]

    row = lambda w: pl.BlockSpec((tm, w), lambda i: (i, 0))
    vec = lambda w: pl.BlockSpec((1, w), lambda i: (0, 0))
    whole = lambda a: pl.BlockSpec(a.shape, lambda i: (0, 0))
    return pl.pallas_call(
        body, name=name, grid=(nt,),
        in_specs=[row(1024), vec(MLA_Q_RANK), vec(MLA_KV_RANK), row(1024), row(1024), row(512), row(LANES), row(LANES),
                  row(POOL_WIDTH),
                  pl.BlockSpec((POOL_HALO, POOL_WIDTH), lambda i: (jnp.minimum((i + 1) * hb, last_halo), 0)),
                  row(1024), whole(wq), whole(wk), whole(wv), whole(wpool)],
        out_specs=[row(IN_AB_PAD), row(1024), row(1024), vec(MLA_Q_RANK), vec(MLA_KV_RANK)],
        out_shape=[jax.ShapeDtypeStruct((t, IN_AB_PAD), BF16), jax.ShapeDtypeStruct((t, 1024), BF16),
                   jax.ShapeDtypeStruct((t, 1024), BF16), jax.ShapeDtypeStruct((1, MLA_Q_RANK), F32),
                   jax.ShapeDtypeStruct((1, MLA_KV_RANK), F32)],
        scratch_shapes=[pltpu.VMEM((tm + POOL_HALO, POOL_WIDTH), F32)],
        compiler_params=_params(("arbitrary",)),
    )(proj, q_a_norm, kv_a_norm, dq, dk, dv, cos_t, sin_t, dyb, dyb, dz, wq, wk, wv, wpool)


def _gate_out_proj(o, ybraw, proj, pool_scale, w, hres, *, name):
    t = o.shape[0]
    tm = _tile(t, 2 * ROW_TILE)

    def body(o_ref, yb_ref, z_ref, ps_ref, w_ref, h_ref, ho_ref, y_ref):
        z = z_ref[...]
        sz = z * _sigmoid(z)
        y_ref[:, 0:512] = (o_ref[...] * sz[:, 0:512]).astype(BF16)
        y_ref[:, 512:1024] = (yb_ref[...] * ps_ref[...] * sz[:, 512:1024]).astype(BF16)
        ho_ref[...] = h_ref[...] + _dot(y_ref[...], w_ref[...])

    row = lambda w_: pl.BlockSpec((tm, w_), lambda i: (i, 0))
    return pl.pallas_call(
        body, name=name, grid=(t // tm,),
        in_specs=[row(512), row(512), pl.BlockSpec((tm, 1024), lambda i: (i, 1)), pl.BlockSpec((1, 512), lambda i: (0, 0)),
                  pl.BlockSpec(w.shape, lambda i: (0, 0)), row(1024)],
        out_specs=[row(1024), row(1024)],
        out_shape=[jax.ShapeDtypeStruct((t, 1024), F32), jax.ShapeDtypeStruct((t, 1024), BF16)],
        compiler_params=_params(("parallel",)),
    )(o, ybraw, proj, pool_scale, w, hres)


def _gate_bwd(dy, o, ybraw, proj, pool_scale, *, name):
    t = o.shape[0]
    tm = _tile(t, ROW_TILE)

    def body(dy_ref, o_ref, yb_ref, z_ref, ps_ref, do_ref, dl_ref, dyb_ref, dz_ref, dps_ref):
        i = pl.program_id(0)
        z = z_ref[...]
        sg = _sigmoid(z)
        sz = z * sg
        dsz = sg * (1.0 + z * (1.0 - sg))
        dyv = dy_ref[...]
        dcat = dyv * sz
        ov = o_ref[...]
        ybs = yb_ref[...] * ps_ref[...]
        dz_ref[:, 0:512] = (dyv[:, 0:512] * ov * dsz[:, 0:512]).astype(BF16)
        dz_ref[:, 512:1024] = (dyv[:, 512:1024] * ybs * dsz[:, 512:1024]).astype(BF16)
        do = dcat[:, 0:512]
        do_ref[...] = do.astype(BF16)
        r_i = (lax.broadcasted_iota(jnp.int32, (1024, 512), 0) % 512) // MLA_V
        c_i = lax.broadcasted_iota(jnp.int32, (1024, 512), 1) // MLA_V
        prod = do * ov
        hi = prod.astype(BF16)
        lo = (prod - hi.astype(F32)).astype(BF16)
        dl_ref[...] = _dot(jnp.concatenate([hi, lo], axis=1), (r_i == c_i).astype(BF16))
        dyb_ref[...] = (dcat[:, 512:1024] * ps_ref[...]).astype(BF16)

        @pl.when(i == 0)
        def _():
            dps_ref[...] = jnp.zeros_like(dps_ref)

        dps_ref[...] += jnp.sum(dcat[:, 512:1024] * yb_ref[...], axis=0, keepdims=True)

    row = lambda w: pl.BlockSpec((tm, w), lambda i: (i, 0))
    vec = pl.BlockSpec((1, 512), lambda i: (0, 0))
    return pl.pallas_call(
        body, name=name, grid=(t // tm,),
        in_specs=[row(1024), row(512), row(512), pl.BlockSpec((tm, 1024), lambda i: (i, 1)), vec],
        out_specs=[row(512), row(512), row(512), row(1024), vec],
        out_shape=[jax.ShapeDtypeStruct((t, 512), BF16), jax.ShapeDtypeStruct((t, 512), F32),
                   jax.ShapeDtypeStruct((t, 512), BF16), jax.ShapeDtypeStruct((t, 1024), BF16),
                   jax.ShapeDtypeStruct((1, 512), F32)],
        compiler_params=_params(("arbitrary",)),
    )(dy, o, ybraw, proj, pool_scale)


ATT_HP_FWD = 4
ATT_HP_BWD = 2


def _diag_mask(tq):
    return lax.broadcasted_iota(jnp.int32, (tq, tq), 1) <= lax.broadcasted_iota(jnp.int32, (tq, tq), 0)


def _block_schedule(nq, key_major):
    if key_major:
        pairs = [(qi, ki) for ki in range(nq) for qi in range(ki, nq)]
    else:
        pairs = [(qi, ki) for qi in range(nq) for ki in range(qi + 1)]
    return jnp.asarray([p[0] for p in pairs], jnp.int32), jnp.asarray([p[1] for p in pairs], jnp.int32)


def _attn_fwd(q, k, v, *, name):
    t = q.shape[0]
    tq = _tile(t, ATT_TILE)
    nq = t // tq
    hp = ATT_HP_FWD
    qi_tab, ki_tab = _block_schedule(nq, key_major=False)

    def body(qi_ref, ki_ref, q_ref, k_ref, v_ref, o_ref, lse_ref, m_sc, l_sc, acc_sc):
        step = pl.program_id(1)
        qi, ki = qi_ref[step], ki_ref[step]

        @pl.when(ki == 0)
        def _():
            m_sc[...] = jnp.full_like(m_sc, -jnp.inf)
            l_sc[...] = jnp.zeros_like(l_sc)
            acc_sc[...] = jnp.zeros_like(acc_sc)

        def block(on_diagonal):
            scores = []
            for h in range(hp):
                sl = slice(h * LANES, (h + 1) * LANES)
                scores.append(_dot(q_ref[:, sl], k_ref[:, sl], NT))
            if on_diagonal:
                mask = _diag_mask(tq)
                scores = [jnp.where(mask, s, -jnp.inf) for s in scores]
            for h, s in enumerate(scores):
                vv = v_ref[:, (h // 2) * LANES:(h // 2 + 1) * LANES]
                m_prev = m_sc[h]
                m_new = jnp.maximum(m_prev, jnp.max(s, axis=-1, keepdims=True))
                alpha = jnp.exp2(m_prev - m_new)
                p = jnp.exp2(s - m_new[:, 0:1])
                l_sc[h] = alpha * l_sc[h] + jnp.sum(p, axis=-1, keepdims=True)
                acc_sc[h] = alpha * acc_sc[h] + _dot(p.astype(BF16), vv)
                m_sc[h] = m_new

        pl.when(ki < qi)(functools.partial(block, False))
        pl.when(ki == qi)(functools.partial(block, True))

        @pl.when(ki == qi)
        def _():
            first = lax.broadcasted_iota(jnp.int32, (tq, LANES), 1) < MLA_V
            for pr in range(hp // 2):
                a, b = 2 * pr, 2 * pr + 1
                sl = slice(pr * LANES, (pr + 1) * LANES)
                o_ref[:, sl] = jnp.where(first, acc_sc[a] / l_sc[a], acc_sc[b] / l_sc[b])
                lse_ref[:, sl] = jnp.where(first, m_sc[a] + jnp.log2(l_sc[a]), m_sc[b] + jnp.log2(l_sc[b]))

    grid_spec = pltpu.PrefetchScalarGridSpec(
        num_scalar_prefetch=2, grid=(MLA_HEADS // hp, qi_tab.shape[0]),
        in_specs=[pl.BlockSpec((tq, hp * LANES), lambda g, s, qt, kt: (qt[s], g)),
                  pl.BlockSpec((tq, hp * LANES), lambda g, s, qt, kt: (kt[s], g)),
                  pl.BlockSpec((tq, hp * MLA_V), lambda g, s, qt, kt: (kt[s], g))],
        out_specs=[pl.BlockSpec((tq, hp * MLA_V), lambda g, s, qt, kt: (qt[s], g)),
                   pl.BlockSpec((tq, hp * MLA_V), lambda g, s, qt, kt: (qt[s], g))],
        scratch_shapes=[pltpu.VMEM((hp, tq, LANES), F32)] * 3,
    )
    return pl.pallas_call(
        body, name=name, grid_spec=grid_spec,
        out_shape=[jax.ShapeDtypeStruct((t, 512), F32), jax.ShapeDtypeStruct((t, 512), F32)],
        compiler_params=_params(("parallel", "arbitrary")),
    )(qi_tab, ki_tab, q, k, v)


def _attn_bwd(q, k, v, do, lse, delta, *, name):
    t = q.shape[0]
    tq = _tile(t, ATT_TILE)
    nq = t // tq
    hp = ATT_HP_BWD
    qi_tab, ki_tab = _block_schedule(nq, key_major=True)

    def body(qi_ref, ki_ref, q_ref, k_ref, v_ref, do_ref, lse_ref, dl_ref, dq_ref, dk_ref, dv_ref, dk_sc, dv_sc):
        step = pl.program_id(1)
        qi, ki = qi_ref[step], ki_ref[step]

        @pl.when(step == 0)
        def _():
            dq_ref[...] = jnp.zeros_like(dq_ref)

        @pl.when(qi == ki)
        def _():
            dk_sc[...] = jnp.zeros_like(dk_sc)
            dv_sc[...] = jnp.zeros_like(dv_sc)

        def block(on_diagonal):
            lane = lax.broadcasted_iota(jnp.int32, (tq, LANES), 1)
            rows = pl.ds(pl.multiple_of(qi * tq, tq), tq)
            heads = [slice(h * LANES, (h + 1) * LANES) for h in range(hp)]
            scores = [_dot(q_ref[:, sl], k_ref[:, sl], NT) for sl in heads]
            dps = []
            for h in range(hp):
                dov = do_ref[:, (h // 2) * LANES:(h // 2 + 1) * LANES]
                mine = (lane < MLA_V) if h % 2 == 0 else (lane >= MLA_V)
                dps.append(_dot(jnp.where(mine, dov, jnp.zeros_like(dov)), v_ref[:, (h // 2) * LANES:(h // 2 + 1) * LANES], NT))
            mask = _diag_mask(tq) if on_diagonal else None
            for h, sl in enumerate(heads):
                col = (h // 2) * LANES + (h % 2) * MLA_V
                p = jnp.exp2(scores[h] - lse_ref[:, col:col + 1])
                if on_diagonal:
                    p = jnp.where(mask, p, 0.0)
                ds = (p * (dps[h] - dl_ref[:, col:col + 1])).astype(BF16)
                dv_sc[h] += _dot(p.astype(BF16), do_ref[:, (h // 2) * LANES:(h // 2 + 1) * LANES], TN)
                dk_sc[h] += _dot(ds, q_ref[:, sl], TN)
                dq_ref[rows, sl] += _dot(ds, k_ref[:, sl], NN)

        pl.when(qi > ki)(functools.partial(block, False))
        pl.when(qi == ki)(functools.partial(block, True))

        @pl.when(qi == nq - 1)
        def _():
            first = lax.broadcasted_iota(jnp.int32, (tq, LANES), 1) < MLA_V
            for h in range(hp):
                dk_ref[:, h * LANES:(h + 1) * LANES] = dk_sc[h] * (1.0 / LOG2E)
            for pr in range(hp // 2):
                dv_ref[:, pr * LANES:(pr + 1) * LANES] = jnp.where(first, dv_sc[2 * pr], dv_sc[2 * pr + 1]).astype(BF16)

    qrow = lambda w: pl.BlockSpec((tq, w), lambda g, s, qt, kt: (qt[s], g))
    krow = lambda w: pl.BlockSpec((tq, w), lambda g, s, qt, kt: (kt[s], g))
    grid_spec = pltpu.PrefetchScalarGridSpec(
        num_scalar_prefetch=2, grid=(MLA_HEADS // hp, qi_tab.shape[0]),
        in_specs=[qrow(hp * LANES), krow(hp * LANES), krow(hp * MLA_V), qrow(hp * MLA_V), qrow(hp * MLA_V), qrow(hp * MLA_V)],
        out_specs=[pl.BlockSpec((t, hp * LANES), lambda g, s, qt, kt: (0, g)), krow(hp * LANES), krow(hp * MLA_V)],
        scratch_shapes=[pltpu.VMEM((hp, tq, LANES), F32), pltpu.VMEM((hp, tq, LANES), F32)],
    )
    return pl.pallas_call(
        body, name=name, grid_spec=grid_spec,
        out_shape=[jax.ShapeDtypeStruct((t, 1024), F32), jax.ShapeDtypeStruct((t, 1024), F32),
                   jax.ShapeDtypeStruct((t, 512), BF16)],
        compiler_params=_params(("parallel", "arbitrary")),
    )(qi_tab, ki_tab, q, k, v, do, lse, delta)


def _conv_rows(ext, tm, w_ref, sec):
    c0 = sec * 1024
    y = ext[CONV_HALO - 3:CONV_HALO - 3 + tm, c0:c0 + 1024] * w_ref[0:1, c0:c0 + 1024]
    for j in range(1, CONV_WIDTH):
        y = y + ext[CONV_HALO - 3 + j:CONV_HALO - 3 + j + tm, c0:c0 + 1024] * w_ref[j:j + 1, c0:c0 + 1024]
    return y


def _c_prep(proj_c, conv_w, a_log, dt_bias, *, name):
    t = proj_c.shape[0]
    tm = _tile(t, ROW_TILE)
    hb = tm // CONV_HALO

    def body(p_ref, halo_ref, ab_ref, w_ref, al_ref, dtb_ref, q_ref, k_ref, v_ref, g_ref, b_ref, gt_ref, ext):
        i = pl.program_id(0)
        ext[0:CONV_HALO, :] = jnp.where(i > 0, halo_ref[...], 0.0)
        ext[CONV_HALO:CONV_HALO + tm, :] = p_ref[...]
        for sec, o_ref in enumerate((q_ref, k_ref, v_ref)):
            y = _conv_rows(ext, tm, w_ref, sec)
            y = y * _sigmoid(y)
            if sec == 2:
                o_ref[...] = y
                continue
            scale = GDN_DK ** -0.5 if sec == 0 else 1.0
            for h in range(GDN_HEADS):
                sl = slice(h * LANES, (h + 1) * LANES)
                blk = y[:, sl]
                r = lax.rsqrt(jnp.sum(blk * blk, axis=-1, keepdims=True) + RMS_EPS)
                o_ref[:, sl] = blk * (r * scale)
        ab = ab_ref[...]
        g = -jnp.exp(al_ref[...]) * _softplus(ab + dtb_ref[...])
        beta = _sigmoid(ab)
        ri = lax.broadcasted_iota(jnp.int32, (tm, tm), 0)
        ci = lax.broadcasted_iota(jnp.int32, (tm, tm), 1)
        lower = ((ri // CHUNK) == (ci // CHUNK)) & (ri >= ci)
        gc = _dot(lower.astype(F32), g, NN, HI)
        eye = lax.broadcasted_iota(jnp.int32, (LANES, LANES), 0) == lax.broadcasted_iota(jnp.int32, (LANES, LANES), 1)
        gt_ref[...] = _dot(eye.astype(F32), gc, NT, HI)[0:GDN_HEADS, :]
        for h in range(GDN_HEADS):
            sl = slice(h * LANES, (h + 1) * LANES)
            g_ref[:, sl] = jnp.broadcast_to(gc[:, h:h + 1], (tm, LANES))
            b_ref[:, sl] = jnp.broadcast_to(beta[:, GDN_HEADS + h:GDN_HEADS + h + 1], (tm, LANES))

    row = lambda w: pl.BlockSpec((tm, w), lambda i: (i, 0))
    vec = lambda r, w: pl.BlockSpec((r, w), lambda i: (0, 0))
    out = jax.ShapeDtypeStruct((t, 1024), F32)
    return pl.pallas_call(
        body, name=name, grid=(t // tm,),
        in_specs=[row(3072), pl.BlockSpec((CONV_HALO, 3072), lambda i: (jnp.maximum(i * hb - 1, 0), 0)),
                  pl.BlockSpec((tm, LANES), lambda i: (i, 32)), vec(CONV_WIDTH, 3072), vec(1, LANES), vec(1, LANES)],
        out_specs=[row(1024)] * 5 + [pl.BlockSpec((GDN_HEADS, tm), lambda i: (0, i))],
        out_shape=[out] * 5 + [jax.ShapeDtypeStruct((GDN_HEADS, t), F32)],
        scratch_shapes=[pltpu.VMEM((tm + CONV_HALO, 3072), F32)],
        compiler_params=_params(("parallel",)),
    )(proj_c, proj_c, proj_c, conv_w, a_log, dt_bias)


def _c_prep_bwd(proj_c, conv_w, a_log, dt_bias, dq, dk, dv, dgb, dbb, dz, *, name):
    t = proj_c.shape[0]
    tm = _tile(t, ROW_TILE // 2)
    hb = tm // CONV_HALO
    nt = t // tm
    rev = lambda i: nt - 1 - i

    def body(p_ref, halo_ref, ab_ref, w_ref, al_ref, dtb_ref, dq_ref, dk_ref, dv_ref, dg_ref, db_ref, dz_ref,
             dp_ref, dw_ref, dal_ref, ddt_ref, ext, dyext, carry, taps):
        step = pl.program_id(0)
        i = rev(step)

        @pl.when(step == 0)
        def _():
            dw_ref[...] = jnp.zeros_like(dw_ref)
            dal_ref[...] = jnp.zeros_like(dal_ref)
            ddt_ref[...] = jnp.zeros_like(ddt_ref)
            carry[...] = jnp.zeros_like(carry)

        ext[0:CONV_HALO, :] = jnp.where(i > 0, halo_ref[...], 0.0)
        ext[CONV_HALO:CONV_HALO + tm, :] = p_ref[...]
        for sec, g_ref in enumerate((dq_ref, dk_ref, dv_ref)):
            c0 = sec * 1024
            for j in range(CONV_WIDTH):
                taps[j] = ext[CONV_HALO - 3 + j:CONV_HALO - 3 + j + tm, c0:c0 + 1024]
            y = taps[0] * w_ref[0:1, c0:c0 + 1024]
            for j in range(1, CONV_WIDTH):
                y = y + taps[j] * w_ref[j:j + 1, c0:c0 + 1024]
            sg = _sigmoid(y)
            act = y * sg
            if sec == 2:
                dact = g_ref[...]
            else:
                scale = GDN_DK ** -0.5 if sec == 0 else 1.0
                parts = []
                for h in range(GDN_HEADS):
                    sl = slice(h * LANES, (h + 1) * LANES)
                    blk = act[:, sl]
                    r = lax.rsqrt(jnp.sum(blk * blk, axis=-1, keepdims=True) + RMS_EPS)
                    n = blk * r
                    dn = g_ref[:, sl] * scale
                    parts.append(r * (dn - n * jnp.sum(dn * n, axis=-1, keepdims=True)))
                dact = jnp.concatenate(parts, axis=-1)
            dy = dact * (sg * (1.0 + y * (1.0 - sg)))
            dyext[0:tm, c0:c0 + 1024] = dy
            for j in range(CONV_WIDTH):
                dw_ref[j:j + 1, c0:c0 + 1024] += jnp.sum(dy * taps[j], axis=0, keepdims=True)
        dyext[tm:tm + CONV_HALO, :] = carry[...]
        carry[...] = dyext[0:CONV_HALO, :]
        for sec in range(3):
            c0 = sec * 1024
            dx = dyext[3:3 + tm, c0:c0 + 1024] * w_ref[0:1, c0:c0 + 1024]
            for j in range(1, CONV_WIDTH):
                dx = dx + dyext[3 - j:3 - j + tm, c0:c0 + 1024] * w_ref[j:j + 1, c0:c0 + 1024]
            dp_ref[:, c0:c0 + 1024] = dx.astype(BF16)
        dp_ref[:, 3072:4096] = dz_ref[...]
        lane = lax.broadcasted_iota(jnp.int32, (tm, LANES), 1)
        dg = jnp.zeros((tm, LANES), F32)
        dbeta = jnp.zeros((tm, LANES), F32)
        for h in range(GDN_HEADS):
            sl = slice(h * LANES, (h + 1) * LANES)
            dg = dg + jnp.where(lane == h, dg_ref[:, sl], 0.0)
            dbeta = dbeta + jnp.where(lane == GDN_HEADS + h, db_ref[:, sl], 0.0)
        ri = lax.broadcasted_iota(jnp.int32, (tm, tm), 0)
        ci = lax.broadcasted_iota(jnp.int32, (tm, tm), 1)
        upper = ((ri // CHUNK) == (ci // CHUNK)) & (ri <= ci)
        dg = _dot(upper.astype(F32), dg, NN, HI)
        pre = ab_ref[...] + dtb_ref[...]
        s = _sigmoid(pre)
        a_exp = jnp.exp(al_ref[...])
        dg_da = dg * (-a_exp * s)
        dp_ref[:, 4096:IN_C_PAD] = (dg_da + dbeta * s * (1.0 - s)).astype(BF16)
        dal_ref[...] += jnp.sum(dg * (-a_exp * _softplus(pre)), axis=0, keepdims=True)
        ddt_ref[...] += jnp.sum(dg_da, axis=0, keepdims=True)

    row = lambda w: pl.BlockSpec((tm, w), lambda s: (rev(s), 0))
    vec = lambda r, w: pl.BlockSpec((r, w), lambda s: (0, 0))
    return pl.pallas_call(
        body, name=name, grid=(nt,),
        in_specs=[row(3072), pl.BlockSpec((CONV_HALO, 3072), lambda s: (jnp.maximum(rev(s) * hb - 1, 0), 0)),
                  pl.BlockSpec((tm, LANES), lambda s: (rev(s), 32)), vec(CONV_WIDTH, 3072), vec(1, LANES), vec(1, LANES),
                  row(1024), row(1024), row(1024), row(1024), row(1024), row(1024)],
        out_specs=[row(IN_C_PAD), vec(CONV_WIDTH, 3072), vec(1, LANES), vec(1, LANES)],
        out_shape=[jax.ShapeDtypeStruct((t, IN_C_PAD), BF16), jax.ShapeDtypeStruct((CONV_WIDTH, 3072), F32),
                   jax.ShapeDtypeStruct((1, LANES), F32), jax.ShapeDtypeStruct((1, LANES), F32)],
        scratch_shapes=[pltpu.VMEM((tm + CONV_HALO, 3072), F32), pltpu.VMEM((tm + CONV_HALO, 3072), F32),
                        pltpu.VMEM((CONV_HALO, 3072), F32), pltpu.VMEM((CONV_WIDTH, tm, 1024), F32)],
        compiler_params=_params(("arbitrary",)),
    )(proj_c, proj_c, proj_c, conv_w, a_log, dt_bias, dq, dk, dv, dgb, dbb, dz)


def _o_gate_bwd(dy, o, proj_c, o_norm, *, name):
    t = o.shape[0]
    tm = _tile(t, 2 * ROW_TILE)

    def body(dy_ref, o_ref, z_ref, g_ref, do_ref, dz_ref, dg_ref):
        i = pl.program_id(0)

        @pl.when(i == 0)
        def _():
            dg_ref[...] = jnp.zeros_like(dg_ref)

        dg = jnp.zeros((1, LANES), F32)
        for h in range(GDN_HEADS):
            sl = slice(h * LANES, (h + 1) * LANES)
            x = o_ref[:, sl]
            r = lax.rsqrt(jnp.mean(x * x, axis=-1, keepdims=True) + RMS_EPS)
            xh = x * r
            z = z_ref[:, sl]
            sg = _sigmoid(z)
            dyv = dy_ref[:, sl]
            dn = dyv * (z * sg)
            dz_ref[:, sl] = (dyv * xh * g_ref[...] * (sg * (1.0 + z * (1.0 - sg)))).astype(BF16)
            dxh = dn * g_ref[...]
            do_ref[:, sl] = r * (dxh - xh * jnp.mean(dxh * xh, axis=-1, keepdims=True))
            dg = dg + jnp.sum(dn * xh, axis=0, keepdims=True)
        dg_ref[...] += dg

    row = pl.BlockSpec((tm, 1024), lambda i: (i, 0))
    vec = pl.BlockSpec((1, LANES), lambda i: (0, 0))
    return pl.pallas_call(
        body, name=name, grid=(t // tm,), in_specs=[row, row, pl.BlockSpec((tm, 1024), lambda i: (i, 3)), vec],
        out_specs=[row, row, vec],
        out_shape=[jax.ShapeDtypeStruct((t, 1024), F32), jax.ShapeDtypeStruct((t, 1024), BF16),
                   jax.ShapeDtypeStruct((1, LANES), F32)],
        compiler_params=_params(("arbitrary",)),
    )(dy, o, proj_c, o_norm)


PAIR = 2 * CHUNK
GDN_HP = 8


def _bdot(a, b, dims=NN):
    return _dot(a.astype(BF16), b.astype(BF16), dims)


def _each(f, *lists):
    return [f(*args) for args in zip(*lists)]


def _pair_common(q, k, v, gci, gcj, beta):
    ri = lax.broadcasted_iota(jnp.int32, (PAIR, PAIR), 0)
    ci = lax.broadcasted_iota(jnp.int32, (PAIR, PAIR), 1)
    same = (ri // CHUNK) == (ci // CHUNK)
    incl = same & (ri >= ci)
    strict = same & (ri > ci)
    eye = (ri == ci).astype(F32)
    first = lax.broadcasted_iota(jnp.int32, (PAIR, LANES), 0) < CHUNK
    gamma = _each(lambda gi, gj: jnp.where(incl, jnp.exp(jnp.minimum(gi - gj, 0.0)), 0.0), gci, gcj)
    kb = _each(jnp.multiply, k, beta)
    kk = _each(lambda a, b: _bdot(a, b, NT), kb, k)
    qk = _each(lambda a, b: _bdot(a, b, NT), q, k)
    m = _each(lambda x, g: jnp.where(strict, x * g, 0.0), kk, gamma)
    tm_ = _each(lambda x: eye - x, m)
    pw = _each(lambda x: _bdot(x, x), m)
    for it in range(5):
        tm_ = _each(lambda x, p: x + _bdot(x, p), tm_, pw)
        if it < 4:
            pw = _each(lambda p: _bdot(p, p), pw)
    eg = _each(jnp.exp, gci)
    vb = _each(jnp.multiply, v, beta)
    kbe = _each(jnp.multiply, kb, eg)
    uw = _each(lambda x, a, b: _bdot(x, jnp.concatenate([a, b], axis=1)), tm_, vb, kbe)
    attn = _each(lambda x, g: jnp.where(incl, x * g, 0.0), qk, gamma)
    gl_a = _each(lambda g: g[CHUNK - 1:CHUNK, :], gci)
    gl_b = _each(lambda g: g[PAIR - 1:PAIR, :], gci)
    ek = _each(lambda a, b, g: jnp.exp(jnp.where(first, a, b) - g), gl_a, gl_b, gci)
    return dict(incl=incl, strict=strict, gamma=gamma, kb=kb, m=m, tm=tm_, eg=eg, vb=vb, kbe=kbe,
                u=_each(lambda x: x[:, :LANES], uw), w=_each(lambda x: x[:, LANES:], uw), attn=attn,
                qd=_each(jnp.multiply, q, eg), ek=ek, kd=_each(jnp.multiply, k, ek),
                glast_a=_each(jnp.exp, gl_a), glast_b=_each(jnp.exp, gl_b))


def _gdn_specs(t, ts, order):
    nc = ts // CHUNK
    blk = pl.BlockSpec((ts, GDN_HP * LANES), lambda h, s: (order(s), h))
    row = pl.BlockSpec((GDN_HP, 1, ts), lambda h, s: (h, 0, order(s)))
    st = pl.BlockSpec((GDN_HP, nc, LANES, LANES), lambda h, s: (h, order(s), 0, 0))
    return blk, row, st


def _gdn_fwd(q, k, v, gcb, gct, bb, *, name):
    t = q.shape[0]
    ts = _tile(t, GDN_TILE)
    npair = ts // PAIR

    def body(q_ref, k_ref, v_ref, g_ref, gt_ref, b_ref, o_ref, st_ref, s_sc):
        @pl.when(pl.program_id(1) == 0)
        def _():
            s_sc[...] = jnp.zeros_like(s_sc)

        def pair(pi, _):
            rows = pl.ds(pl.multiple_of(pi * PAIR, PAIR), PAIR)
            heads = [slice(hh * LANES, (hh + 1) * LANES) for hh in range(GDN_HP)]
            c = CHUNK
            cat0 = lambda *xs: jnp.concatenate(xs, axis=0)
            s0 = [s_sc[hh] for hh in range(GDN_HP)]
            cm = _pair_common([q_ref[rows, sl] for sl in heads], [k_ref[rows, sl] for sl in heads],
                              [v_ref[rows, sl] for sl in heads], [g_ref[rows, sl] for sl in heads],
                              [gt_ref[hh, :, rows] for hh in range(GDN_HP)], [b_ref[rows, sl] for sl in heads])
            u, w, qd, kd = cm["u"], cm["w"], cm["qd"], cm["kd"]
            r0 = _each(lambda w_, q_, s: _bdot(cat0(w_[:c], q_[:c]), s), w, qd, s0)
            vn_a = _each(lambda u_, r: u_[:c] - r[:c], u, r0)
            s1 = _each(lambda s, gl, k_, vn: s * gl + _bdot(k_[:c], vn, TN), s0, cm["glast_a"], kd, vn_a)
            r1 = _each(lambda w_, q_, s: _bdot(cat0(w_[c:], q_[c:]), s), w, qd, s1)
            vn_b = _each(lambda u_, r: u_[c:] - r[:c], u, r1)
            s2 = _each(lambda s, gl, k_, vn: s * gl + _bdot(k_[c:], vn, TN), s1, cm["glast_b"], kd, vn_b)
            o = _each(lambda ra, rb, at, va, vb_: cat0(ra[c:], rb[c:]) + _bdot(at, cat0(va, vb_)),
                      r0, r1, cm["attn"], vn_a, vn_b)
            for hh, sl in enumerate(heads):
                st_ref[hh, 2 * pi] = s0[hh]
                st_ref[hh, 2 * pi + 1] = s1[hh]
                s_sc[hh] = s2[hh]
                o_ref[rows, sl] = o[hh]
            return 0

        lax.fori_loop(0, npair, pair, 0)

    blk, row, st = _gdn_specs(t, ts, lambda s: s)
    return pl.pallas_call(
        body, name=name, grid=(GDN_HEADS // GDN_HP, t // ts), in_specs=[blk, blk, blk, blk, row, blk],
        out_specs=[blk, st],
        out_shape=[jax.ShapeDtypeStruct((t, 1024), F32), jax.ShapeDtypeStruct((GDN_HEADS, t // CHUNK, LANES, LANES), F32)],
        scratch_shapes=[pltpu.VMEM((GDN_HP, LANES, LANES), F32)],
        compiler_params=_params(("parallel", "arbitrary")),
    )(q, k, v, gcb, gct, bb)


def _gdn_bwd(q, k, v, gcb, gct, bb, do, states, *, name):
    t = q.shape[0]
    ts = _tile(t, GDN_TILE)
    npair = ts // PAIR
    ns = t // ts
    c = CHUNK

    def body(q_ref, k_ref, v_ref, g_ref, gt_ref, b_ref, do_ref, st_ref, dq_ref, dk_ref, dv_ref, dg_ref, db_ref, ds_sc):
        @pl.when(pl.program_id(1) == 0)
        def _():
            ds_sc[...] = jnp.zeros_like(ds_sc)

        rowsum = lambda x: jnp.sum(x, axis=-1, keepdims=True)
        total = lambda x: jnp.sum(rowsum(x), axis=0, keepdims=True)
        cat0 = lambda *xs: jnp.concatenate(xs, axis=0)
        cat1 = lambda *xs: jnp.concatenate(xs, axis=1)

        def pair(step, _):
            pi = npair - 1 - step
            rows = pl.ds(pl.multiple_of(pi * PAIR, PAIR), PAIR)
            heads = [slice(hh * LANES, (hh + 1) * LANES) for hh in range(GDN_HP)]
            hs = range(GDN_HP)
            qv, kv, vv = ([r[rows, sl] for sl in heads] for r in (q_ref, k_ref, v_ref))
            beta = [b_ref[rows, sl] for sl in heads]
            dov = [do_ref[rows, sl] for sl in heads]
            s0 = [st_ref[hh, 2 * pi] for hh in hs]
            s1 = [st_ref[hh, 2 * pi + 1] for hh in hs]
            ds2 = [ds_sc[hh] for hh in hs]
            cm = _pair_common(qv, kv, vv, [g_ref[rows, sl] for sl in heads], [gt_ref[hh, :, rows] for hh in hs], beta)
            u, w, qd, kd, attn = cm["u"], cm["w"], cm["qd"], cm["kd"], cm["attn"]
            tmat, gamma, eg = cm["tm"], cm["gamma"], cm["eg"]
            incl, strict = cm["incl"], cm["strict"]
            vn_a = _each(lambda u_, w_, s: u_[:c] - _bdot(w_[:c], s), u, w, s0)
            vn_b = _each(lambda u_, w_, s: u_[c:] - _bdot(w_[c:], s), u, w, s1)
            vn = _each(cat0, vn_a, vn_b)
            dvn_att = _each(lambda a, d: _bdot(a, d, TN), attn, dov)
            dattn = _each(lambda d, v_: jnp.where(incl, _bdot(d, v_, NT), 0.0), dov, vn)
            dvn_b = _each(lambda x, k_, d: x[c:] + _bdot(k_[c:], d), dvn_att, kd, ds2)
            rb = _each(lambda d, x, s: _bdot(cat0(d[c:], x), s, NT), dov, dvn_b, s1)
            dkd_b = _each(lambda v_, d: _bdot(v_, d, NT), vn_b, ds2)
            dgl_b = _each(lambda d, s: total(d * s), ds2, s1)
            ds1 = _each(lambda d, gl, q_, w_, o_, x: d * gl + _bdot(cat0(q_[c:], w_[c:]), cat0(o_[c:], -x), TN),
                        ds2, cm["glast_b"], qd, w, dov, dvn_b)
            dvn_a = _each(lambda x, k_, d: x[:c] + _bdot(k_[:c], d), dvn_att, kd, ds1)
            ra = _each(lambda d, x, s: _bdot(cat0(d[:c], x), s, NT), dov, dvn_a, s0)
            dkd_a = _each(lambda v_, d: _bdot(v_, d, NT), vn_a, ds1)
            dgl_a = _each(lambda d, s: total(d * s), ds1, s0)
            ds0 = _each(lambda d, gl, q_, w_, o_, x: d * gl + _bdot(cat0(q_[:c], w_[:c]), cat0(o_[:c], -x), TN),
                        ds1, cm["glast_a"], qd, w, dov, dvn_a)
            dvn = _each(cat0, dvn_a, dvn_b)
            dqd = _each(lambda a, b: cat0(a[:c], b[:c]), ra, rb)
            dw = _each(lambda a, b: -cat0(a[c:], b[c:]), ra, rb)
            dkd = _each(cat0, dkd_a, dkd_b)
            dvw = _each(cat1, dvn, dw)
            dvbk = _each(lambda t_, x: _bdot(t_, x, TN), tmat, dvw)
            dvb = _each(lambda x: x[:, :LANES], dvbk)
            dkbe = _each(lambda x: x[:, LANES:], dvbk)
            dt_ = _each(lambda x, a, b: _bdot(x, cat1(a, b), NT), dvw, cm["vb"], cm["kbe"])
            da1 = _each(lambda t_, x: _bdot(t_, x, TN), tmat, dt_)
            dm = _each(lambda x, t_: jnp.where(strict, -_bdot(x, t_, NT), 0.0), da1, tmat)
            dkk = _each(jnp.multiply, dm, gamma)
            dqk = _each(jnp.multiply, dattn, gamma)
            z = _each(lambda a, b, c_, d: a * b + c_ * d, dm, cm["m"], dattn, attn)
            dkb = _each(lambda x, k_, y, e: _bdot(x, k_) + y * e, dkk, kv, dkbe, eg)
            dk = _each(lambda a, b, kb_, q_, x, e, y, be: _bdot(cat0(a, b), cat0(kb_, q_), TN) + x * e + y * be,
                       dkk, dqk, cm["kb"], qv, dkd, cm["ek"], dkb, beta)
            dq = _each(lambda x, k_, y, e: _bdot(x, k_) + y * e, dqk, kv, dqd, eg)

            def colsum_of(z_):
                zh = z_.astype(BF16)
                zl = (z_ - zh.astype(F32)).astype(BF16)
                return _dot(cat0(zh, zl), jnp.ones((2 * PAIR, LANES), BF16), TN)

            colsum = _each(colsum_of, z)
            ri = lax.broadcasted_iota(jnp.int32, (PAIR, LANES), 0)
            for hh, sl in enumerate(heads):
                dkd_kd = dkd[hh] * kd[hh]
                dgc = (rowsum(z[hh]) - colsum[hh] + rowsum(dqd[hh] * qd[hh]) - rowsum(dkd_kd)
                       + rowsum(dkbe[hh] * cm["kbe"][hh]))
                last_a = total(dkd_kd[:c]) + dgl_a[hh] * cm["glast_a"][hh]
                last_b = total(dkd_kd[c:]) + dgl_b[hh] * cm["glast_b"][hh]
                dgc = dgc + jnp.where(ri == c - 1, last_a, 0.0) + jnp.where(ri == PAIR - 1, last_b, 0.0)
                ds_sc[hh] = ds0[hh]
                dq_ref[rows, sl] = dq[hh]
                dk_ref[rows, sl] = dk[hh]
                dv_ref[rows, sl] = dvb[hh] * beta[hh]
                db_ref[rows, sl] = jnp.broadcast_to(rowsum(dkb[hh] * kv[hh]) + rowsum(dvb[hh] * vv[hh]), (PAIR, LANES))
                dg_ref[rows, sl] = dgc
            return 0

        lax.fori_loop(0, npair, pair, 0)

    blk, row, st = _gdn_specs(t, ts, lambda s: ns - 1 - s)
    out = jax.ShapeDtypeStruct((t, 1024), F32)
    return pl.pallas_call(
        body, name=name, grid=(GDN_HEADS // GDN_HP, ns), in_specs=[blk, blk, blk, blk, row, blk, blk, st],
        out_specs=[blk] * 5, out_shape=[out] * 5, scratch_shapes=[pltpu.VMEM((GDN_HP, LANES, LANES), F32)],
        compiler_params=_params(("parallel", "arbitrary")),
    )(q, k, v, gcb, gct, bb, do, states)


def _gate_out_proj_loss(o, proj_c, o_norm, w, hres, g, target, *, name):
    t, d = hres.shape
    tm = _tile(t, 2 * ROW_TILE)

    def body(o_ref, z_ref, on_ref, w_ref, h_ref, g_ref, t_ref, dh_ref, dhb_ref, y_ref, dg_ref, loss_ref):
        i = pl.program_id(0)
        for hd in range(GDN_HEADS):
            sl = slice(hd * LANES, (hd + 1) * LANES)
            ov = o_ref[:, sl]
            rr = lax.rsqrt(jnp.mean(ov * ov, axis=-1, keepdims=True) + RMS_EPS)
            z = z_ref[:, sl]
            y_ref[:, sl] = (ov * rr * on_ref[...] * (z * _sigmoid(z))).astype(BF16)
        x = h_ref[...] + _dot(y_ref[...], w_ref[...])
        r = lax.rsqrt(jnp.mean(x * x, axis=-1, keepdims=True) + RMS_EPS)
        xh = x * r
        err = xh * g_ref[...] - t_ref[...]
        dy = err * (1.0 / d)
        dxh = dy * g_ref[...]
        dh = r * (dxh - xh * jnp.mean(dxh * xh, axis=-1, keepdims=True))
        dh_ref[...] = dh
        dhb_ref[...] = dh.astype(BF16)

        @pl.when(i == 0)
        def _():
            dg_ref[...] = jnp.zeros_like(dg_ref)
            loss_ref[...] = jnp.zeros_like(loss_ref)

        dg_ref[...] += jnp.sum(dy * xh, axis=0, keepdims=True)
        part = 0.5 * jnp.sum(jnp.mean(err * err, axis=-1, keepdims=True), axis=0, keepdims=True)
        loss_ref[...] += jnp.broadcast_to(part, loss_ref.shape)

    row = pl.BlockSpec((tm, d), lambda i: (i, 0))
    vec = pl.BlockSpec((1, d), lambda i: (0, 0))
    return pl.pallas_call(
        body, name=name, grid=(t // tm,),
        in_specs=[row, pl.BlockSpec((tm, 1024), lambda i: (i, 3)), pl.BlockSpec((1, LANES), lambda i: (0, 0)),
                  pl.BlockSpec(w.shape, lambda i: (0, 0)), row, vec, row],
        out_specs=[row, row, row, vec, pl.BlockSpec((8, LANES), lambda i: (0, 0))],
        out_shape=[jax.ShapeDtypeStruct((t, d), F32), jax.ShapeDtypeStruct((t, d), BF16), jax.ShapeDtypeStruct((t, d), BF16),
                   jax.ShapeDtypeStruct((1, d), F32), jax.ShapeDtypeStruct((8, LANES), F32)],
        compiler_params=_params(("arbitrary",)),
    )(o, proj_c, o_norm, w, hres, g, target)


def _pad_cols(w, n):
    return jnp.pad(w, ((0, 0), (0, n - w.shape[1])))


def _layout_odd(w):
    return dict(
        winc=_pad_cols(w["w_in_c"], IN_C_PAD).astype(BF16), wout_c=w["w_out_c"].astype(BF16), conv_w=w["conv_w"],
        a_log=_pad_cols(w["a_log"], LANES), dt_bias=_pad_cols(w["dt_bias"], LANES),
        norm_c=w["norm_c"], o_norm=w["o_norm"], final_norm=w["final_norm"],
    )


def _layout_in_ab(w):
    z = lambda r, c: jnp.zeros((r, c), w["w_in_ab"].dtype)
    wi = w["w_in_ab"]
    win = jnp.concatenate([wi[:, :384], z(1024, 64), wi[:, 384:416], z(1024, 32), wi[:, 416:]], axis=1)
    pw = w["pool_w"]
    rows = []
    for g in range(4):
        rows.append(jnp.concatenate([pw[g] if j == g else jnp.zeros((128, 128), F32) for j in range(4)], axis=1))
    wpool = jnp.concatenate(rows, axis=0)
    half = MLA_ROPE // 2
    inv = 1.0 / (ROPE_THETA ** (jnp.arange(half, dtype=F32) / half))
    inv_lane = jnp.concatenate([jnp.zeros((MLA_NOPE,), F32), inv, inv, jnp.zeros((32,), F32)]).reshape(1, LANES)
    return dict(win=win.astype(BF16), wpool=wpool.astype(BF16), inv_lane=inv_lane, norm_ab=w["norm_ab"],
                q_a_norm=w["q_a_norm"], kv_a_norm=w["kv_a_norm"], pool_scale=w["pool_scale"])


def _layout_mid(w):
    wq = jnp.pad(w["w_q_b"].reshape(MLA_Q_RANK, MLA_HEADS, 96), ((0, 0), (0, 0), (0, 32))).reshape(MLA_Q_RANK, 1024)
    kv3 = w["w_kv_b"].reshape(MLA_KV_RANK, MLA_HEADS, 128)
    wk = jnp.pad(kv3[..., :MLA_NOPE], ((0, 0), (0, 0), (0, 64))).reshape(MLA_KV_RANK, 1024)
    wv = kv3[..., MLA_NOPE:].reshape(MLA_KV_RANK, 512)
    return dict(wq=wq.astype(BF16), wk=wk.astype(BF16), wv=wv.astype(BF16), wout_ab=w["w_out_ab"].astype(BF16))


def _unlayout_grads(g, names):
    out = {}
    for name in names:
        if name == "w_in_ab":
            dwin = g["win"]
            out[name] = jnp.concatenate([dwin[:384], dwin[448:480], dwin[512:]], axis=0)
        elif name == "w_q_b":
            out[name] = g["wq"].reshape(MLA_Q_RANK, MLA_HEADS, 128)[..., :96].reshape(MLA_Q_RANK, 768)
        elif name == "w_kv_b":
            out[name] = jnp.concatenate([g["wk"].reshape(MLA_KV_RANK, MLA_HEADS, 128)[..., :MLA_NOPE],
                                         g["wv"].reshape(MLA_KV_RANK, MLA_HEADS, MLA_V)], axis=-1).reshape(MLA_KV_RANK, 1024)
        elif name == "w_in_c":
            out[name] = g["winc"][:4112]
        else:
            out[name] = g[{"w_out_ab": "wout_ab", "w_out_c": "wout_c"}[name]]
    return out


def _local_step(x, pos, target, lw, more_weights=None, on_grads=None):
    mm = _matmul
    proj, hn = _rms_in_proj(x, lw["norm_ab"], lw["win"], name="rms_in_ab")
    if more_weights is not None:
        lw = {**lw, **more_weights("mid", proj)}
    q, k, v, ybraw, qn, kvn, d, cos_t, sin_t = _ab_prep(
        proj, pos, lw["inv_lane"], lw["q_a_norm"], lw["kv_a_norm"], lw["wq"], lw["wk"], lw["wv"], lw["wpool"], name="ab_prep")
    o, lse = _attn_fwd(q, k, v, name="attn_fwd")
    h1, y = _gate_out_proj(o, ybraw, proj, lw["pool_scale"], lw["wout_ab"], x, name="gate_out_ab")
    lo = lw if more_weights is None else more_weights("odd", h1)
    proj_c, hn1 = _rms_in_proj(h1, lo["norm_c"], lo["winc"], name="rms_in_c")
    q2, k2, v2, gb, bb, gt = _c_prep(proj_c, lo["conv_w"], lo["a_log"], lo["dt_bias"], name="c_prep")
    gt = gt.reshape(GDN_HEADS, 1, gt.shape[1])
    o2, states = _gdn_fwd(q2, k2, v2, gb, gt, bb, name="gdn_fwd")
    dh2, dh2b, y2, d_final, loss = _gate_out_proj_loss(
        o2, proj_c, lo["o_norm"], lo["wout_c"], h1, lo["final_norm"], target, name="gate_out_c_loss")
    g = {"final_norm": d_final}
    dy2 = mm(dh2b, lo["wout_c"], "nt", name="out_c_dx")
    g["wout_c"] = mm(y2, dh2b, "tn", name="out_c_dw")
    do2, dz2, g["o_norm"] = _o_gate_bwd(dy2, o2, proj_c, lo["o_norm"], name="gate_c_bwd")
    dq2, dk2, dv2, dgb, dbb = _gdn_bwd(q2, k2, v2, gb, gt, bb, do2, states, name="gdn_bwd")
    dproj_c, g["conv_w"], g["a_log"], g["dt_bias"] = _c_prep_bwd(
        proj_c, lo["conv_w"], lo["a_log"], lo["dt_bias"], dq2, dk2, dv2, dgb, dbb, dz2, name="c_prep_bwd")
    g["winc"] = mm(dproj_c, hn1, "tn", name="in_c_dw")
    notify = (lambda tag: 0.0) if on_grads is None else (lambda tag: on_grads(tag, g))
    pool_scale = lw["pool_scale"] + notify("odd")
    dh1, dh1b, g["norm_c"] = _matmul_rms_bwd(dproj_c, lo["winc"], h1, lo["norm_c"], dh2, name="in_c_dx_rms", with_bf16=True)
    dy = mm(dh1b, lw["wout_ab"], "nt", name="out_ab_dx")
    g["wout_ab"] = mm(y, dh1b, "tn", name="out_ab_dw")
    pool_scale = pool_scale + notify("out_ab")
    do, delta, dyb, dz, g["pool_scale"] = _gate_bwd(dy, o, ybraw, proj, pool_scale, name="gate_ab_bwd")
    dq, dk, dv = _attn_bwd(q, k, v, do, lse, delta, name="attn_bwd")
    dproj, dqraw, dkb, g["q_a_norm"], g["kv_a_norm"] = _ab_prep_bwd(
        proj, lw["q_a_norm"], lw["kv_a_norm"], dq, dk, dv, cos_t, sin_t, dyb, dz,
        lw["wq"], lw["wk"], lw["wv"], lw["wpool"], name="ab_prep_bwd")
    g["wpool"] = mm(d, dyb, "tn", name="pool_mix_dw")
    g["wq"] = mm(qn, dqraw, "tn", name="q_up_dw")
    g["wk"] = mm(kvn, dkb, "tn", name="k_up_dw")
    g["wv"] = mm(kvn, dv, "tn", name="v_up_dw")
    g["win"] = mm(dproj, hn, "tn", name="in_ab_dw")
    norm_ab = lw["norm_ab"] + notify("in_ab")
    dx, g["norm_ab"] = _matmul_rms_bwd(dproj, lw["win"], x, norm_ab, dh1, name="in_ab_dx_rms", with_bf16=False)
    return loss, dx, g


_HBM = pl.BlockSpec(memory_space=pltpu.HBM)


def _place():
    return lax.axis_index("x"), lax.axis_index("y"), lax.axis_index("c")


def _flip(v, f):
    return 1 - v if f else v


_CHIP_FLIPS = ((1, 0), (0, 1), (1, 1))
_DEV_FLIPS = tuple((fx, fy, fc) for fx in (0, 1) for fy in (0, 1) for fc in (0, 1) if fx or fy or fc)


def _rcopy(src, dst, send_sems, recv_sems, k, to):
    return pltpu.make_async_remote_copy(src_ref=src, dst_ref=dst, send_sem=send_sems.at[k], recv_sem=recv_sems.at[k],
                                        device_id=to, device_id_type=MESH)


def _my_half(ref, c, axis):
    rh = ref.shape[axis] // 2
    idx = [slice(None)] * len(ref.shape)
    idx[axis] = pl.ds(c * rh, rh)
    return ref.at[tuple(idx)]


def _gather_weights(bigs, smalls):
    nb, ns = len(bigs), len(smalls)

    def body(*refs):
        ins, outs = refs[:nb + ns], refs[nb + ns:2 * (nb + ns)]
        send_sems, recv_sems, local_sems = refs[2 * (nb + ns):]
        x, y, c = _place()
        j0 = 2 * x + y
        sib = (x, y, 1 - c)
        chips = [(_flip(x, fx), _flip(y, fy)) for fx, fy in _CHIP_FLIPS]
        local = [pltpu.make_async_copy(i_ref, o_ref.at[j0], local_sems.at[a])
                 for a, (i_ref, o_ref) in enumerate(zip(ins, outs))]
        for cp in local:
            cp.start()
        sends = []
        for k, (px, py) in enumerate(chips):
            for a in range(nb):
                sends.append(_rcopy(_my_half(ins[a], c, 0), _my_half(outs[a].at[j0], c, 0), send_sems, recv_sems,
                                    6 * a + k, (px, py, c)))
            for s in range(ns):
                sends.append(_rcopy(ins[nb + s], outs[nb + s].at[j0], send_sems, recv_sems, 6 * nb + 3 * s + k, (px, py, c)))
        for cp in sends:
            cp.start()
        for k, (px, py) in enumerate(chips):
            jk = 2 * px + py
            for a in range(nb):
                landed = _my_half(outs[a].at[jk], c, 0)
                _rcopy(landed, landed, send_sems, recv_sems, 6 * a + k, (px, py, c)).wait_recv()
                fwd = _rcopy(landed, landed, send_sems, recv_sems, 6 * a + 3 + k, sib)
                fwd.start()
                sends.append(fwd)
        for k, (px, py) in enumerate(chips):
            jk = 2 * px + py
            for a in range(nb):
                other = _my_half(outs[a].at[jk], 1 - c, 0)
                _rcopy(other, other, send_sems, recv_sems, 6 * a + 3 + k, sib).wait_recv()
            for s in range(ns):
                _rcopy(ins[nb + s], outs[nb + s].at[jk], send_sems, recv_sems, 6 * nb + 3 * s + k, (px, py, c)).wait_recv()
        for cp in sends:
            cp.wait_send()
        for cp in local:
            cp.wait()

    arrays = list(bigs) + list(smalls)
    n_sem = 6 * nb + 3 * ns
    return pl.pallas_call(
        body, name="gather_weights", in_specs=[_HBM] * len(arrays), out_specs=[_HBM] * len(arrays),
        out_shape=[jax.ShapeDtypeStruct((4,) + a.shape, a.dtype) for a in arrays],
        scratch_shapes=[pltpu.SemaphoreType.DMA((n_sem,)), pltpu.SemaphoreType.DMA((n_sem,)),
                        pltpu.SemaphoreType.DMA((len(arrays),))],
    )(*arrays)


def _core_swap_partial(gs, by_cols, *, name):
    n = len(gs)

    def body(*refs):
        ins, outs = refs[:n], refs[n:2 * n]
        send_sems, recv_sems = refs[2 * n:]
        x, y, c = _place()
        copies = [_rcopy(_my_half(i_ref, 1 - c, 2 if by_cols[a] else 1), o_ref, send_sems, recv_sems, a, (x, y, 1 - c))
                  for a, (i_ref, o_ref) in enumerate(zip(ins, outs))]
        for cp in copies:
            cp.start()
        for cp in copies:
            cp.wait()

    halved = lambda g, cols: (4, g.shape[1], g.shape[2] // 2) if cols else (4, g.shape[1] // 2, g.shape[2])
    return pl.pallas_call(
        body, name=name, in_specs=[_HBM] * n, out_specs=[_HBM] * n,
        out_shape=[jax.ShapeDtypeStruct(halved(g, cols), g.dtype) for g, cols in zip(gs, by_cols)],
        scratch_shapes=[pltpu.SemaphoreType.DMA((n,)), pltpu.SemaphoreType.DMA((n,))],
    )(*gs)


def _core_swap_sum(fs, by_cols):
    n = len(fs)

    def body(*refs):
        ins, outs = refs[:n], refs[n:2 * n]
        send_sems, recv_sems = refs[2 * n:]
        x, y, c = _place()
        axes = [1 if cols else 0 for cols in by_cols]
        copies = [_rcopy(_my_half(i_ref, c, ax), _my_half(o_ref, c, ax), send_sems, recv_sems, a, (x, y, 1 - c))
                  for a, (i_ref, o_ref, ax) in enumerate(zip(ins, outs, axes))]
        for cp in copies:
            cp.start()
        for a, cp in enumerate(copies):
            cp.wait_send()
            theirs = _my_half(outs[a], 1 - c, axes[a])
            _rcopy(theirs, theirs, send_sems, recv_sems, a, (x, y, 1 - c)).wait_recv()

    return pl.pallas_call(
        body, name="core_swap_sum", in_specs=[_HBM] * n, out_specs=[_HBM] * n,
        out_shape=[jax.ShapeDtypeStruct(f.shape, f.dtype) for f in fs],
        input_output_aliases={a: a for a in range(n)},
        scratch_shapes=[pltpu.SemaphoreType.DMA((n,)), pltpu.SemaphoreType.DMA((n,))],
    )(*fs)


_SEM = pl.BlockSpec(memory_space=pltpu.SEMAPHORE)
_ANY = pl.BlockSpec(memory_space=pl.ANY)
_DATAFLOW = pltpu.SideEffectType.DATAFLOW_SIDE_EFFECTING


def _to_chips_copies(srcs, lands, send_sems, recv_sems, per_chip_slot):
    x, y, c = _place()
    j0 = 2 * x + y
    out = []
    for k, (fx, fy) in enumerate(_CHIP_FLIPS):
        px, py = _flip(x, fx), _flip(y, fy)
        jk = 2 * px + py
        for a, (src, land) in enumerate(zip(srcs, lands)):
            piece = src.at[jk] if per_chip_slot else src
            out.append((_rcopy(piece, land.at[j0], send_sems, recv_sems, 3 * a + k, (px, py, c)),
                        _rcopy(piece, land.at[jk], send_sems, recv_sems, 3 * a + k, (px, py, c))))
    return out


def _to_chips_start(arrays, *, per_chip_slot, name, after=None):
    n = len(arrays)
    lands = [lax.empty((4,) + (a.shape[1:] if per_chip_slot else a.shape), a.dtype) for a in arrays]
    extra = [] if after is None else [after]

    def body(*refs):
        srcs, land_refs, token = refs[:n], refs[n:2 * n], refs[-1]
        send_sems, recv_sems = refs[2 * n + len(extra)], refs[2 * n + len(extra) + 1]
        for send, _ in _to_chips_copies(srcs, land_refs, send_sems, recv_sems, per_chip_slot):
            send.start()
        token[...] = jnp.zeros_like(token)

    held = [pltpu.with_memory_space_constraint(a, pltpu.HBM) for a in list(arrays) + lands]
    return pl.pallas_call(
        body, name=name, in_specs=[_HBM] * (2 * n) + [_ANY] * len(extra),
        out_specs=(_SEM, _SEM, *[_HBM] * (2 * n), pl.BlockSpec(memory_space=pltpu.VMEM)),
        out_shape=(pltpu.SemaphoreType.DMA((3 * n,)), pltpu.SemaphoreType.DMA((3 * n,)),
                   *[pltpu.HBM(a.shape, a.dtype) for a in held], jax.ShapeDtypeStruct((8, LANES), F32)),
        input_output_aliases={i: 2 + i for i in range(2 * n)},
        compiler_params=pltpu.CompilerParams(has_side_effects=_DATAFLOW),
    )(*held, *extra)


def _to_chips_wait(started, after, *, per_chip_slot, name):
    send_sems, recv_sems, held = started[0], started[1], started[2:-1]
    n = len(held) // 2

    def body(*refs):
        srcs, land_refs, s_sems, r_sems = refs[:n], refs[n:2 * n], refs[2 * n], refs[2 * n + 1]
        for send, arrival in _to_chips_copies(srcs, land_refs, s_sems, r_sems, per_chip_slot):
            send.wait_send()
            arrival.wait_recv()

    out = pl.pallas_call(
        body, name=name, in_specs=[_HBM] * (2 * n) + [_SEM, _SEM, _ANY], out_specs=[_HBM] * (2 * n),
        out_shape=[pltpu.HBM(a.shape, a.dtype) for a in held],
        input_output_aliases={i: i for i in range(2 * n)},
        compiler_params=pltpu.CompilerParams(has_side_effects=_DATAFLOW),
    )(*held, send_sems, recv_sems, after)
    return out[n:]


def _chip_exchange(ps, small):
    n = len(ps)
    rs = small.shape[0]

    def body(*refs):
        p_refs, s_ref = refs[:n], refs[n]
        l_refs, ls_ref = refs[n + 1:2 * n + 1], refs[2 * n + 1]
        send_sems, recv_sems, local_sems = refs[2 * n + 2:]
        x, y, c = _place()
        j0 = 2 * x + y
        d0 = 2 * j0 + c
        local = [pltpu.make_async_copy(p.at[j0], l.at[j0], local_sems.at[a]) for a, (p, l) in enumerate(zip(p_refs, l_refs))]
        local.append(pltpu.make_async_copy(s_ref, ls_ref.at[d0], local_sems.at[n]))
        for cp in local:
            cp.start()
        sends = []
        for k, (fx, fy) in enumerate(_CHIP_FLIPS):
            px, py = _flip(x, fx), _flip(y, fy)
            for a in range(n):
                sends.append(_rcopy(p_refs[a].at[2 * px + py], l_refs[a].at[j0], send_sems, recv_sems, 3 * a + k, (px, py, c)))
        for k, (fx, fy, fc) in enumerate(_DEV_FLIPS):
            peer = (_flip(x, fx), _flip(y, fy), _flip(c, fc))
            sends.append(_rcopy(s_ref, ls_ref.at[d0], send_sems, recv_sems, 3 * n + k, peer))
        for cp in sends:
            cp.start()
        for k, (fx, fy) in enumerate(_CHIP_FLIPS):
            px, py = _flip(x, fx), _flip(y, fy)
            for a in range(n):
                _rcopy(p_refs[a].at[j0], l_refs[a].at[2 * px + py], send_sems, recv_sems, 3 * a + k, (px, py, c)).wait_recv()
        for k, (fx, fy, fc) in enumerate(_DEV_FLIPS):
            px, py, pc = _flip(x, fx), _flip(y, fy), _flip(c, fc)
            _rcopy(s_ref, ls_ref.at[4 * px + 2 * py + pc], send_sems, recv_sems, 3 * n + k, (px, py, pc)).wait_recv()
        for cp in sends:
            cp.wait_send()
        for cp in local:
            cp.wait()

    n_sem = 3 * n + 7
    return pl.pallas_call(
        body, name="chip_exchange", in_specs=[_HBM] * (n + 1), out_specs=[_HBM] * (n + 1),
        out_shape=[jax.ShapeDtypeStruct(p.shape, F32) for p in ps] + [jax.ShapeDtypeStruct((8, rs, LANES), F32)],
        scratch_shapes=[pltpu.SemaphoreType.DMA((n_sem,)), pltpu.SemaphoreType.DMA((n_sem,)),
                        pltpu.SemaphoreType.DMA((n + 1,))],
    )(*ps, small)


def _half_blocks(rows, cols, by_cols):
    if by_cols:
        tc = _tile(cols // 2, 256)
        nb = cols // 2 // tc
        return rows, tc, nb, (lambda i, c: (0, c * nb + i))
    tr = _tile(rows // 2, 256)
    nb = rows // 2 // tr
    return tr, cols, nb, (lambda i, c: (c * nb + i, 0))


def _core_sum(g, part, core, *, name, by_cols):
    _, rows, cols = g.shape
    br, bc, nb, whole = _half_blocks(rows, cols, by_cols)
    mine = (lambda i: (0, i)) if by_cols else (lambda i: (i, 0))

    def body(c_ref, g_ref, p_ref, o_ref):
        o_ref[...] = g_ref[...] + p_ref[...]

    grid_spec = pltpu.PrefetchScalarGridSpec(
        num_scalar_prefetch=1, grid=(4, nb),
        in_specs=[pl.BlockSpec((1, br, bc), lambda j, i, c: (j,) + whole(i, c[0])),
                  pl.BlockSpec((1, br, bc), lambda j, i, c: (j,) + mine(i))],
        out_specs=pl.BlockSpec((1, br, bc), lambda j, i, c: (j,) + mine(i)),
    )
    return pl.pallas_call(
        body, name=name, grid_spec=grid_spec, out_shape=jax.ShapeDtypeStruct(part.shape, F32),
        compiler_params=_params(("parallel", "parallel")),
    )(core, g, part)


def _chip_sum(landed, core, *, name, by_cols):
    _, hr, hc = landed.shape
    rows, cols = (hr, 2 * hc) if by_cols else (2 * hr, hc)
    br, bc, nb, whole = _half_blocks(rows, cols, by_cols)
    mine = (lambda i: (0, i)) if by_cols else (lambda i: (i, 0))

    def body(c_ref, l_ref, o_ref):
        o_ref[...] = ((l_ref[0] + l_ref[1]) + l_ref[2]) + l_ref[3]

    grid_spec = pltpu.PrefetchScalarGridSpec(
        num_scalar_prefetch=1, grid=(nb,),
        in_specs=[pl.BlockSpec((4, br, bc), lambda i, c: (0,) + mine(i))],
        out_specs=pl.BlockSpec((br, bc), lambda i, c: whole(i, c[0])),
    )
    return pl.pallas_call(
        body, name=name, grid_spec=grid_spec, out_shape=jax.ShapeDtypeStruct((rows, cols), F32),
        compiler_params=_params(("parallel",)),
    )(core, landed)


_ROW_POOL_W, _ROW_NORM_AB, _ROW_FINAL, _ROW_POOL_SCALE, _ROW_Q_NORM = 0, 512, 520, 528, 532
_ROW_KV_NORM, _ROW_O_NORM, _ROW_A_LOG, _ROW_DT_BIAS, _ROW_LOSS = 534, 535, 536, 537, 538
_ROW_CONV, _ROW_NORM_C, _SMALL_ROWS = 544, 640, 672
_CONV_ROWS = CONV_WIDTH * 6


def _put_rows(dst_ref, row0, src, width):
    for r in range(width // LANES):
        dst_ref[row0 + r:row0 + r + 1, :] = src[:, r * LANES:(r + 1) * LANES]


def _pack_small(g, loss_tile):
    names = ("wpool", "norm_ab", "final_norm", "pool_scale", "q_a_norm", "kv_a_norm", "o_norm", "a_log", "dt_bias",
             "conv_w", "norm_c")

    def body(wpool, norm_ab, final_norm, pool_scale, q_norm, kv_norm, o_norm, a_log, dt_bias, conv_w, norm_c, loss, o_ref):
        o_ref[...] = jnp.zeros_like(o_ref)
        for gi in range(4):
            o_ref[_ROW_POOL_W + gi * 128:_ROW_POOL_W + (gi + 1) * 128, :] = wpool[gi * 128:(gi + 1) * 128, gi * 128:(gi + 1) * 128]
        _put_rows(o_ref, _ROW_NORM_AB, norm_ab[...], 1024)
        _put_rows(o_ref, _ROW_FINAL, final_norm[...], 1024)
        _put_rows(o_ref, _ROW_POOL_SCALE, pool_scale[...], 512)
        _put_rows(o_ref, _ROW_Q_NORM, q_norm[...], 256)
        for row, ref in ((_ROW_KV_NORM, kv_norm), (_ROW_O_NORM, o_norm), (_ROW_A_LOG, a_log), (_ROW_DT_BIAS, dt_bias)):
            o_ref[row:row + 1, :] = ref[...]
        o_ref[_ROW_LOSS:_ROW_LOSS + 1, :] = loss[0:1, :]
        for j in range(4):
            for r in range(CONV_WIDTH):
                _put_rows(o_ref, _ROW_CONV + j * _CONV_ROWS + r * 6, conv_w[r:r + 1, j * 768:(j + 1) * 768], 768)
            _put_rows(o_ref, _ROW_NORM_C + j * 8, norm_c[:, j * 256:(j + 1) * 256], 256)

    vmem = pl.BlockSpec(memory_space=pltpu.VMEM)
    return pl.pallas_call(
        body, name="pack_small", in_specs=[vmem] * 12, out_specs=vmem,
        out_shape=jax.ShapeDtypeStruct((_SMALL_ROWS, LANES), F32),
    )(*[g[n] for n in names], loss_tile)


_SMALL_NAMES = ("pool_w", "norm_ab", "final_norm", "pool_scale", "q_a_norm", "kv_a_norm", "o_norm", "a_log", "dt_bias",
                "conv_w", "norm_c")


def _take_rows(src, row0, width):
    return jnp.concatenate([src[row0 + r:row0 + r + 1, :] for r in range(width // LANES)], axis=1)


def _small_update(small_all, ws, ms, vs):
    n = len(_SMALL_NAMES)

    def body(*refs):
        a_ref = refs[0]
        w_refs, m_refs, v_refs = refs[1:1 + n], refs[1 + n:1 + 2 * n], refs[1 + 2 * n:1 + 3 * n]
        outs = refs[1 + 3 * n:1 + 7 * n]
        loss_ref, tot = refs[1 + 7 * n], refs[2 + 7 * n]
        acc = a_ref[0]
        for d in range(1, 8):
            acc = acc + a_ref[d]
        tot[...] = acc
        x, y, _ = _place()
        j0 = 2 * x + y
        conv = tot[pl.ds(pl.multiple_of(_ROW_CONV + j0 * _CONV_ROWS, 8), _CONV_ROWS), :]
        norm_c = tot[pl.ds(pl.multiple_of(_ROW_NORM_C + j0 * 8, 8), 8), :]
        whole = tot[_ROW_NORM_AB:_ROW_CONV, :]
        at = lambda row: row - _ROW_NORM_AB
        grads = {
            "norm_ab": _take_rows(whole, at(_ROW_NORM_AB), 1024), "final_norm": _take_rows(whole, at(_ROW_FINAL), 1024),
            "pool_scale": _take_rows(whole, at(_ROW_POOL_SCALE), 512), "q_a_norm": _take_rows(whole, at(_ROW_Q_NORM), 256),
            "kv_a_norm": whole[at(_ROW_KV_NORM):at(_ROW_KV_NORM) + 1, :], "o_norm": whole[at(_ROW_O_NORM):at(_ROW_O_NORM) + 1, :],
            "a_log": tot[_ROW_A_LOG:_ROW_A_LOG + 1, 0:GDN_HEADS],
            "dt_bias": tot[_ROW_DT_BIAS:_ROW_DT_BIAS + 1, 0:GDN_HEADS],
            "norm_c": _take_rows(norm_c, 0, 256),
        }
        loss_ref[...] = whole[at(_ROW_LOSS):at(_ROW_LOSS) + 1, :]
        for i, name in enumerate(_SMALL_NAMES):
            g_out = outs[4 * i]
            if name == "pool_w":
                for gi in range(4):
                    g_out[gi] = tot[_ROW_POOL_W + gi * 128:_ROW_POOL_W + (gi + 1) * 128, :]
            elif name == "conv_w":
                for r in range(CONV_WIDTH):
                    g_out[r:r + 1, :] = _take_rows(conv, r * 6, 768)
            else:
                g_out[...] = grads[name]
            _adam_update(g_out, w_refs[i], m_refs[i], v_refs[i], *outs[4 * i + 1:4 * i + 4])

    vmem = pl.BlockSpec(memory_space=pltpu.VMEM)
    out_shape = [jax.ShapeDtypeStruct(w.shape, F32) for w in ws for _ in range(4)] + [jax.ShapeDtypeStruct((1, LANES), F32)]
    return pl.pallas_call(
        body, name="small_update", in_specs=[vmem] * (1 + 3 * n), out_specs=[vmem] * (4 * n + 1), out_shape=out_shape,
        scratch_shapes=[pltpu.VMEM((_SMALL_ROWS, LANES), F32)],
        compiler_params=pltpu.CompilerParams(vmem_limit_bytes=VMEM_LIMIT),
    )(small_all, *ws, *ms, *vs)


def _adam_update(g_ref, w_ref, m_ref, v_ref, d_ref, mo_ref, vo_ref):
    gv = g_ref[...]
    mn = ADAM_B1 * m_ref[...] + (1.0 - ADAM_B1) * gv
    vn = ADAM_B2 * v_ref[...] + (1.0 - ADAM_B2) * (gv * gv)
    mo_ref[...] = mn
    vo_ref[...] = vn
    c1 = 1.0 - ADAM_B1 ** ADAM_STEP
    c2 = 1.0 - ADAM_B2 ** ADAM_STEP
    d_ref[...] = -ADAM_LR * ((mn / c1) / (jnp.sqrt(vn / c2) + ADAM_EPS) + ADAM_WD * w_ref[...])


def _adamw_rows(g, w, m, v, *, name):
    rows, cols = g.shape
    if rows % LANES == 0:
        tr = _tile(rows, 512)
        blk, steps = pl.BlockSpec((tr, cols), lambda i: (i, 0)), rows // tr
    else:
        tc = _tile(cols, 256)
        blk, steps = pl.BlockSpec((rows, tc), lambda i: (0, i)), cols // tc

    def body(*refs):
        _adam_update(*refs)

    out = jax.ShapeDtypeStruct((rows, cols), F32)
    return pl.pallas_call(
        body, name=name, grid=(steps,), in_specs=[blk] * 4, out_specs=[blk] * 3, out_shape=[out] * 3,
        compiler_params=_params(("parallel",)),
    )(g, w, m, v)


_ADAM_ROWWISE = ("w_in_ab", "w_q_b", "w_kv_b", "w_out_ab", "w_in_c", "w_out_c")


_SHARD_AXIS = {"w_in_ab": 1, "w_q_b": 1, "w_kv_b": 1, "w_out_ab": 0, "w_in_c": 1, "w_out_c": 0, "conv_w": 1, "norm_c": 1}
_ALL_NAMES = ("norm_ab", "w_in_ab", "q_a_norm", "w_q_b", "kv_a_norm", "w_kv_b", "pool_w", "pool_scale", "w_out_ab",
              "norm_c", "w_in_c", "conv_w", "a_log", "dt_bias", "o_norm", "w_out_c", "final_norm")


def _join_shards(a, axis):
    _, r, c = a.shape
    return a.reshape(4 * r, c) if axis == 0 else jnp.transpose(a, (1, 0, 2)).reshape(r, 4 * c)


def _split_shards(a, axis):
    r, c = a.shape
    return a.reshape(4, r // 4, c) if axis == 0 else jnp.transpose(a.reshape(r, 4, c // 4), (1, 0, 2))


def kernel(x, positions, norm_ab, w_in_ab, q_a_norm, w_q_b, kv_a_norm, w_kv_b, pool_w, pool_scale, w_out_ab, norm_c, w_in_c, conv_w, a_log, dt_bias, o_norm, w_out_c, final_norm, loss_target, m_norm_ab, m_w_in_ab, m_q_a_norm, m_w_q_b, m_kv_a_norm, m_w_kv_b, m_pool_w, m_pool_scale, m_w_out_ab, m_norm_c, m_w_in_c, m_conv_w, m_a_log, m_dt_bias, m_o_norm, m_w_out_c, m_final_norm, v_norm_ab, v_w_in_ab, v_q_a_norm, v_w_q_b, v_kv_a_norm, v_w_kv_b, v_pool_w, v_pool_scale, v_w_out_ab, v_norm_c, v_w_in_c, v_conv_w, v_a_log, v_dt_bias, v_o_norm, v_w_out_c, v_final_norm):
    given = dict(locals())
    c = lax.axis_index("c")
    t = x.shape[1]

    def shard_of(prefix, name):
        a = given[prefix + name]
        return a.reshape(a.shape[1:]) if a.ndim > 2 else a.reshape(1, -1)

    big, big_even, big_odd, small_sharded = _ADAM_ROWWISE, _ADAM_ROWWISE[:4], _ADAM_ROWWISE[4:], ("conv_w", "norm_c")
    chip = 2 * lax.axis_index("x") + lax.axis_index("y")
    core = c.astype(jnp.int32).reshape(1)
    later = {"mid": big_even[1:], "odd": big_odd + small_sharded}
    travelling = {}

    def send(tag, after=None):
        shards = [shard_of("", n).astype(BF16) if n in big else shard_of("", n) for n in later[tag]]
        started = _to_chips_start(shards, per_chip_slot=False, name="gather_" + tag + "_start", after=after)
        travelling[tag] = (shards, started)
        return started[-1][0, 0]

    mid_sent = send("mid")
    gathered = _gather_weights([shard_of("", "w_in_ab").astype(BF16)], [])
    full = {"w_in_ab": _join_shards(gathered[0], _SHARD_AXIS["w_in_ab"])}
    for name in ("norm_ab", "q_a_norm", "kv_a_norm", "pool_w", "pool_scale"):
        full[name] = shard_of("", name)
    lw = _layout_in_ab(full)
    lw["norm_ab"] = lw["norm_ab"] + mid_sent

    def more_weights(tag, after):
        shards, started = travelling[tag]
        landed = _to_chips_wait(started, after, per_chip_slot=False, name="gather_" + tag + "_wait")
        w = {}
        for name, land, own in zip(later[tag], landed, shards):
            w[name] = _join_shards(lax.dynamic_update_index_in_dim(land, own, chip, 0), _SHARD_AXIS[name])
        if tag == "mid":
            out = _layout_mid(w)
            out["wq"] = out["wq"] + send("odd", after=landed[0]).astype(BF16)
            return out
        for name in ("a_log", "dt_bias", "o_norm", "final_norm"):
            w[name] = shard_of("", name)
        return _layout_odd(w)

    transposed = ("w_in_ab", "w_in_c")

    def chip_partials(names, grads, tag):
        by_cols = [n in transposed for n in names]
        slots = [_split_shards(grads[n], 0 if n in transposed else _SHARD_AXIS[n]) for n in names]
        partial = _core_swap_partial(slots, by_cols, name="core_swap_partial_" + tag)
        return [_core_sum(s, p, core, name="core_sum_" + n, by_cols=b) for n, s, p, b in zip(names, slots, partial, by_cols)]

    groups = {"odd": big_odd, "out_ab": ("w_out_ab",), "in_ab": ("w_in_ab", "w_q_b", "w_kv_b")}
    sent = {}

    def on_grads(tag, g):
        part = chip_partials(groups[tag], _unlayout_grads(g, groups[tag]), tag)
        sent[tag] = (part, _to_chips_start(part, per_chip_slot=True, name="exchange_" + tag + "_start"))
        return sent[tag][1][-1][0, 0]

    loss_tile, dx, g = _local_step(x[0], positions.reshape(t, 1), loss_target[0], lw, more_weights, on_grads)
    small_all = _chip_exchange([], _pack_small(g, loss_tile))[-1]
    halves = {}
    for tag, names in groups.items():
        part, started = sent[tag]
        landed = _to_chips_wait(started, small_all, per_chip_slot=True, name="exchange_" + tag + "_wait")
        for n, l, p in zip(names, landed, part):
            l = lax.dynamic_update_index_in_dim(l, lax.dynamic_index_in_dim(p, chip, 0, keepdims=False), chip, 0)
            halves[n] = _chip_sum(l, core, name="chip_sum_" + n, by_cols=n in transposed)
    gbig = dict(zip(big, _core_swap_sum([halves[n] for n in big], [n in transposed for n in big])))

    res = {}
    for name in big:
        operands = [gbig[name], shard_of("", name), shard_of("m_", name), shard_of("v_", name)]
        flip = name in transposed
        if flip:
            operands[1:] = [jnp.transpose(a) for a in operands[1:]]
        out = (operands[0],) + tuple(_adamw_rows(*operands, name="adamw_" + name))
        out = [jnp.transpose(a) for a in out] if flip else out
        res["grad", name], res["delta", name], res["m", name], res["v", name] = out
    out = _small_update(small_all, [shard_of("", n) for n in _SMALL_NAMES], [shard_of("m_", n) for n in _SMALL_NAMES],
                        [shard_of("v_", n) for n in _SMALL_NAMES])
    for i, name in enumerate(_SMALL_NAMES):
        res["grad", name], res["delta", name], res["m", name], res["v", name] = out[4 * i:4 * i + 4]
    res = {k: a.reshape(given[k[1]].shape) for k, a in res.items()}
    loss = out[-1][0, 0]
    outs = [loss, dx.reshape(x.shape)]
    for key in ("grad", "delta", "m", "v"):
        outs += [res[key, n] for n in _ALL_NAMES]
    return tuple(outs)
```

```python
import functools

import jax
import jax.numpy as jnp
from jax import lax
from jax.experimental import pallas as pl
from jax.experimental.pallas import tpu as pltpu

F32 = jnp.float32
BF16 = jnp.bfloat16
HI = lax.Precision.HIGHEST
MESH = pl.DeviceIdType.MESH

RMS_EPS = 1e-6
MLA_HEADS = 8
MLA_Q_RANK = 256
MLA_KV_RANK = 128
MLA_NOPE = 64
MLA_ROPE = 32
MLA_V = 64
ROPE_THETA = 10000.0
POOL_WINDOWS = (2, 4, 8, 16)
POOL_GROUP = 128
POOL_WIDTH = 512
POOL_HALO = 16
GDN_HEADS = 8
GDN_DK = 128
CONV_WIDTH = 4
CONV_HALO = 8
CHUNK = 64
IN_AB_PAD = 2048
IN_C_PAD = 4224
ATT_SCALE = (MLA_NOPE + MLA_ROPE) ** -0.5
LOG2E = 1.4426950408889634

ADAM_LR = 0.001
ADAM_B1 = 0.9
ADAM_B2 = 0.999
ADAM_EPS = 1e-08
ADAM_WD = 0.01
ADAM_STEP = 10

LANES = 128
VMEM_LIMIT = 56 * 1024 * 1024

ROW_TILE = 256
ATT_TILE = 1024
GDN_TILE = 256
MM_TILE = (1024, 1408, 2048)

NN = (((1,), (0,)), ((), ()))
NT = (((1,), (1,)), ((), ()))
TN = (((0,), (0,)), ((), ()))


def _dot(a, b, dims=NN, prec=None):
    return lax.dot_general(a, b, dims, precision=prec, preferred_element_type=F32)


def _tile(n, pref):
    if n <= pref:
        return n
    step = LANES if pref >= LANES else 8
    for t in range(pref - pref % step, 0, -step):
        if n % t == 0:
            return t
    return n


def _params(sem):
    return pltpu.CompilerParams(dimension_semantics=sem, vmem_limit_bytes=VMEM_LIMIT)


def _sigmoid(x):
    return 0.5 * jnp.tanh(0.5 * x) + 0.5


def _softplus(x):
    return jnp.maximum(x, 0.0) + jnp.log(1.0 + jnp.exp(-jnp.abs(x)))


def _matmul(a, b, mode, *, name):
    if mode == "nn":
        (m, k), (k2, n) = a.shape, b.shape
    elif mode == "nt":
        (m, k), (n, k2) = a.shape, b.shape
    else:
        (k, m), (k2, n) = a.shape, b.shape
    assert k == k2, (a.shape, b.shape, mode)
    tm, tn, tk = _tile(m, MM_TILE[0]), _tile(n, MM_TILE[1]), _tile(k, MM_TILE[2])
    nk = k // tk
    if mode == "tn":
        a_spec = pl.BlockSpec((tk, tm), lambda i, j, kk: (kk, i))
    else:
        a_spec = pl.BlockSpec((tm, tk), lambda i, j, kk: (i, kk))
    if mode == "nt":
        b_spec = pl.BlockSpec((tn, tk), lambda i, j, kk: (j, kk))
    else:
        b_spec = pl.BlockSpec((tk, tn), lambda i, j, kk: (kk, j))
    o_spec = pl.BlockSpec((tm, tn), lambda i, j, kk: (i, j))
    dims = {"nn": NN, "nt": NT, "tn": TN}[mode]

    def body(a_ref, b_ref, o_ref, *scratch):
        if nk == 1:
            o_ref[...] = _dot(a_ref[...], b_ref[...], dims)
            return
        acc = scratch[0]
        kk = pl.program_id(2)

        @pl.when(kk == 0)
        def _():
            acc[...] = jnp.zeros_like(acc)

        acc[...] += _dot(a_ref[...], b_ref[...], dims)

        @pl.when(kk == nk - 1)
        def _():
            o_ref[...] = acc[...]

    return pl.pallas_call(
        body, name=name, grid=(m // tm, n // tn, nk), in_specs=[a_spec, b_spec], out_specs=o_spec,
        out_shape=jax.ShapeDtypeStruct((m, n), F32),
        scratch_shapes=[pltpu.VMEM((tm, tn), F32)] if nk > 1 else [],
        compiler_params=_params(("parallel", "parallel", "arbitrary")),
    )(a, b)


def _rms_in_proj(h, g, w, *, name):
    t, d = h.shape
    n = w.shape[1]
    tm, tn = _tile(t, MM_TILE[0]), _tile(n, MM_TILE[1])

    def body(h_ref, g_ref, w_ref, o_ref, hn_ref):
        @pl.when(pl.program_id(1) == 0)
        def _():
            x = h_ref[...]
            r = lax.rsqrt(jnp.mean(x * x, axis=-1, keepdims=True) + RMS_EPS)
            hn_ref[...] = (x * r * g_ref[...]).astype(BF16)

        o_ref[...] = _dot(hn_ref[...], w_ref[...])

    return pl.pallas_call(
        body, name=name, grid=(t // tm, n // tn),
        in_specs=[pl.BlockSpec((tm, d), lambda i, j: (i, 0)), pl.BlockSpec((1, d), lambda i, j: (0, 0)),
                  pl.BlockSpec((d, tn), lambda i, j: (0, j))],
        out_specs=[pl.BlockSpec((tm, tn), lambda i, j: (i, j)), pl.BlockSpec((tm, d), lambda i, j: (i, 0))],
        out_shape=[jax.ShapeDtypeStruct((t, n), F32), jax.ShapeDtypeStruct((t, d), BF16)],
        compiler_params=_params(("parallel", "arbitrary")),
    )(h, g, w)


def _matmul_rms_bwd(dproj, w, h, g, dres, *, name, with_bf16):
    t, k = dproj.shape
    d = w.shape[0]
    tm = _tile(t, 2 * ROW_TILE)

    def body(dp_ref, w_ref, h_ref, g_ref, dres_ref, *outs):
        i = pl.program_id(0)
        dh_ref, dg_ref = outs[0], outs[-1]
        dyv = _dot(dp_ref[...], w_ref[...], NT)
        x = h_ref[...]
        r = lax.rsqrt(jnp.mean(x * x, axis=-1, keepdims=True) + RMS_EPS)
        xh = x * r
        dxh = dyv * g_ref[...]
        dh = dres_ref[...] + r * (dxh - xh * jnp.mean(dxh * xh, axis=-1, keepdims=True))
        dh_ref[...] = dh
        if with_bf16:
            outs[1][...] = dh.astype(BF16)

        @pl.when(i == 0)
        def _():
            dg_ref[...] = jnp.zeros_like(dg_ref)

        dg_ref[...] += jnp.sum(dyv * xh, axis=0, keepdims=True)

    row = pl.BlockSpec((tm, d), lambda i: (i, 0))
    vec = pl.BlockSpec((1, d), lambda i: (0, 0))
    out_shape = [jax.ShapeDtypeStruct((t, d), F32)] + ([jax.ShapeDtypeStruct((t, d), BF16)] if with_bf16 else [])
    out_specs = [row] * len(out_shape) + [vec]
    out_shape.append(jax.ShapeDtypeStruct((1, d), F32))
    return pl.pallas_call(
        body, name=name, grid=(t // tm,),
        in_specs=[pl.BlockSpec((tm, k), lambda i: (i, 0)), pl.BlockSpec((d, k), lambda i: (0, 0)), row, vec, row],
        out_specs=out_specs, out_shape=out_shape, compiler_params=_params(("arbitrary",)),
    )(dproj, w, h, g, dres)


def _rope_partner(x):
    lane = lax.broadcasted_iota(jnp.int32, x.shape, 1)
    swapped = jnp.where(lane < MLA_NOPE + MLA_ROPE // 2, pltpu.roll(x, LANES - 16, 1), pltpu.roll(x, 16, 1))
    return jnp.where((lane >= MLA_NOPE) & (lane < MLA_NOPE + MLA_ROPE), swapped, 0.0)


def _pool_counts(row0, tm, w):
    t_idx = row0 + lax.broadcasted_iota(jnp.int32, (tm, POOL_GROUP), 0)
    return jnp.minimum(t_idx + 1, w).astype(F32)


def _ab_prep(proj, pos, inv_freq, q_a_norm, kv_a_norm, wq, wk, wv, wpool, *, name):
    t = proj.shape[0]
    tm = _tile(t, ROW_TILE)
    hb = tm // POOL_HALO

    def body(p_ref, halo_ref, pos_ref, inv_ref, qg_ref, kg_ref, wq_ref, wk_ref, wv_ref, wp_ref,
             q_ref, k_ref, v_ref, yb_ref, qn_ref, kvn_ref, d_ref, cos_ref, sin_ref, ext):
        i = pl.program_id(0)
        ql = p_ref[:, 0:MLA_Q_RANK]
        r = lax.rsqrt(jnp.mean(ql * ql, axis=-1, keepdims=True) + RMS_EPS)
        qn = (ql * r * qg_ref[...]).astype(BF16)
        qn_ref[...] = qn
        kl = p_ref[:, MLA_Q_RANK:MLA_Q_RANK + MLA_KV_RANK]
        r = lax.rsqrt(jnp.mean(kl * kl, axis=-1, keepdims=True) + RMS_EPS)
        kvn = (kl * r * kg_ref[...]).astype(BF16)
        kvn_ref[...] = kvn
        ang = pos_ref[...].astype(F32) * inv_ref[...]
        lane = lax.broadcasted_iota(jnp.int32, (tm, LANES), 1)
        in_rope = (lane >= MLA_NOPE) & (lane < MLA_NOPE + MLA_ROPE)
        cos_t = jnp.where(in_rope, jnp.cos(ang), 1.0)
        sin_t = jnp.where(in_rope, jnp.sin(ang), 0.0)
        sin_t = jnp.where(lane < MLA_NOPE + MLA_ROPE // 2, -sin_t, sin_t)
        cos_ref[...] = cos_t
        sin_ref[...] = sin_t
        kr = p_ref[:, 384:512]
        kr = kr * cos_t + _rope_partner(kr) * sin_t
        qraw = _dot(qn, wq_ref[...])
        kvk = _dot(kvn, wk_ref[...])
        for h in range(MLA_HEADS):
            sl = slice(h * LANES, (h + 1) * LANES)
            qh = qraw[:, sl]
            q_ref[:, sl] = ((qh * cos_t + _rope_partner(qh) * sin_t) * (ATT_SCALE * LOG2E)).astype(BF16)
            k_ref[:, sl] = (kvk[:, sl] + kr).astype(BF16)
        v_ref[...] = _dot(kvn, wv_ref[...]).astype(BF16)
        xp = p_ref[:, 512:1024]
        ext[0:POOL_HALO, :] = jnp.where(i > 0, halo_ref[...], 0.0)
        ext[POOL_HALO:POOL_HALO + tm, :] = xp
        for g, w in enumerate(POOL_WINDOWS):
            lo = g * POOL_GROUP
            acc = ext[POOL_HALO:POOL_HALO + tm, lo:lo + POOL_GROUP]
            for s in range(1, w):
                acc = acc + ext[POOL_HALO - s:POOL_HALO - s + tm, lo:lo + POOL_GROUP]
            cnt = _pool_counts(i * tm, tm, w)
            d_ref[:, lo:lo + POOL_GROUP] = (acc / cnt - xp[:, lo:lo + POOL_GROUP]).astype(BF16)
        yb_ref[...] = _dot(d_ref[...], wp_ref[...])

    row = lambda w: pl.BlockSpec((tm, w), lambda i: (i, 0))
    vec = lambda w: pl.BlockSpec((1, w), lambda i: (0, 0))
    whole = lambda a: pl.BlockSpec(a.shape, lambda i: (0, 0))
    return pl.pallas_call(
        body, name=name, grid=(t // tm,),
        in_specs=[row(1024), pl.BlockSpec((POOL_HALO, POOL_WIDTH), lambda i: (jnp.maximum(i * hb - 1, 0), 1)),
                  pl.BlockSpec((tm, 1), lambda i: (i, 0)), vec(LANES), vec(MLA_Q_RANK), vec(MLA_KV_RANK),
                  whole(wq), whole(wk), whole(wv), whole(wpool)],
        out_specs=[row(1024), row(1024), row(512), row(512), row(MLA_Q_RANK), row(MLA_KV_RANK), row(POOL_WIDTH),
                   row(LANES), row(LANES)],
        out_shape=[jax.ShapeDtypeStruct((t, 1024), BF16), jax.ShapeDtypeStruct((t, 1024), BF16),
                   jax.ShapeDtypeStruct((t, 512), BF16), jax.ShapeDtypeStruct((t, 512), F32),
                   jax.ShapeDtypeStruct((t, MLA_Q_RANK), BF16), jax.ShapeDtypeStruct((t, MLA_KV_RANK), BF16),
                   jax.ShapeDtypeStruct((t, POOL_WIDTH), BF16), jax.ShapeDtypeStruct((t, LANES), F32),
                   jax.ShapeDtypeStruct((t, LANES), F32)],
        scratch_shapes=[pltpu.VMEM((tm + POOL_HALO, POOL_WIDTH), F32)],
        compiler_params=_params(("parallel",)),
    )(proj, proj, pos, inv_freq, q_a_norm, kv_a_norm, wq, wk, wv, wpool)


def _ab_prep_bwd(proj, q_a_norm, kv_a_norm, dq, dk, dv, cos_t, sin_t, dyb, dz, wq, wk, wv, wpool, *, name):
    t = proj.shape[0]
    tm = _tile(t, ROW_TILE)
    hb = tm // POOL_HALO
    last_halo = t // POOL_HALO - 1
    nt = t // tm

    def body(p_ref, qg_ref, kg_ref, dq_ref, dk_ref, dv_ref, c_ref, s_ref, dyb_ref, dybn_ref, dz_ref,
             wq_ref, wk_ref, wv_ref, wp_ref, dp_ref, dqr_ref, dkb_ref, dqg_ref, dkg_ref, ext):
        i = pl.program_id(0)

        @pl.when(i == 0)
        def _():
            dqg_ref[...] = jnp.zeros_like(dqg_ref)
            dkg_ref[...] = jnp.zeros_like(dkg_ref)

        def norm_bwd(x, g, dy, dg_ref):
            r = lax.rsqrt(jnp.mean(x * x, axis=-1, keepdims=True) + RMS_EPS)
            xh = x * r
            dxh = dy * g
            dg_ref[...] += jnp.sum(dy * xh, axis=0, keepdims=True)
            return r * (dxh - xh * jnp.mean(dxh * xh, axis=-1, keepdims=True))

        c, s = c_ref[...], s_ref[...]
        lane = lax.broadcasted_iota(jnp.int32, (tm, LANES), 1)
        in_rope = (lane >= MLA_NOPE) & (lane < MLA_NOPE + MLA_ROPE)
        dkr = jnp.zeros((tm, LANES), F32)
        for h in range(MLA_HEADS):
            sl = slice(h * LANES, (h + 1) * LANES)
            g = dq_ref[:, sl]
            dqr_ref[:, sl] = ((g * c + _rope_partner(g * s)) * ATT_SCALE).astype(BF16)
            gk = dk_ref[:, sl]
            dkb_ref[:, sl] = gk.astype(BF16)
            dkr = dkr + jnp.where(in_rope, gk, 0.0)
        dkr = dkr * c + _rope_partner(dkr * s)
        dqn = _dot(dqr_ref[...], wq_ref[...], NT)
        dkvn = _dot(dkb_ref[...], wk_ref[...], NT) + _dot(dv_ref[...], wv_ref[...], NT)
        dql = norm_bwd(p_ref[:, 0:MLA_Q_RANK], qg_ref[...], dqn, dqg_ref)
        dp_ref[:, 0:MLA_Q_RANK] = dql.astype(BF16)
        dkl = norm_bwd(p_ref[:, MLA_Q_RANK:384], kg_ref[...], dkvn, dkg_ref)
        dp_ref[:, MLA_Q_RANK:384] = dkl.astype(BF16)
        dp_ref[:, 384:512] = dkr.astype(BF16)
        ddv = _dot(dyb_ref[...], wp_ref[...], NT)
        ddn = _dot(dybn_ref[...], wp_ref[...], NT)
        for g, w in enumerate(POOL_WINDOWS):
            lo = g * POOL_GROUP
            ext[0:tm, lo:lo + POOL_GROUP] = ddv[:, lo:lo + POOL_GROUP] / _pool_counts(i * tm, tm, w)
            nxt = ddn[:, lo:lo + POOL_GROUP] / _pool_counts((i + 1) * tm, POOL_HALO, w)
            ext[tm:tm + POOL_HALO, lo:lo + POOL_GROUP] = jnp.where(i < nt - 1, nxt, 0.0)
        for g, w in enumerate(POOL_WINDOWS):
            lo = g * POOL_GROUP
            acc = ext[0:tm, lo:lo + POOL_GROUP]
            for s in range(1, w):
                acc = acc + ext[s:s + tm, lo:lo + POOL_GROUP]
            dp_ref[:, 512 + lo:512 + lo + POOL_GROUP] = (acc - ddv[:, lo:lo + POOL_GROUP]).astype(BF16)
        dp_ref[:, 1024:2048] = dz_ref[...]

    row = lambda w: pl.BlockSpec((tm, w), lambda i: (i, 0))
    vec = lambda w: pl.BlockSpec((1, w), lambda i: (0, 0))
    whole = lambda a: pl.BlockSpec(a.shape, lambda i: (0, 0))
    return pl.pallas_call(
        body, name=name, grid=(nt,),
        in_specs=[row(1024), vec(MLA_Q_RANK), vec(MLA_KV_RANK), row(1024), row(1024), row(512), row(LANES), row(LANES),
                  row(POOL_WIDTH),
                  pl.BlockSpec((POOL_HALO, POOL_WIDTH), lambda i: (jnp.minimum((i + 1) * hb, last_halo), 0)),
                  row(1024), whole(wq), whole(wk), whole(wv), whole(wpool)],
        out_specs=[row(IN_AB_PAD), row(1024), row(1024), vec(MLA_Q_RANK), vec(MLA_KV_RANK)],
        out_shape=[jax.ShapeDtypeStruct((t, IN_AB_PAD), BF16), jax.ShapeDtypeStruct((t, 1024), BF16),
                   jax.ShapeDtypeStruct((t, 1024), BF16), jax.ShapeDtypeStruct((1, MLA_Q_RANK), F32),
                   jax.ShapeDtypeStruct((1, MLA_KV_RANK), F32)],
        scratch_shapes=[pltpu.VMEM((tm + POOL_HALO, POOL_WIDTH), F32)],
        compiler_params=_params(("arbitrary",)),
    )(proj, q_a_norm, kv_a_norm, dq, dk, dv, cos_t, sin_t, dyb, dyb, dz, wq, wk, wv, wpool)


def _gate_out_proj(o, ybraw, proj, pool_scale, w, hres, *, name):
    t = o.shape[0]
    tm = _tile(t, 2 * ROW_TILE)

    def body(o_ref, yb_ref, z_ref, ps_ref, w_ref, h_ref, ho_ref, y_ref):
        z = z_ref[...]
        sz = z * _sigmoid(z)
        y_ref[:, 0:512] = (o_ref[...] * sz[:, 0:512]).astype(BF16)
        y_ref[:, 512:1024] = (yb_ref[...] * ps_ref[...] * sz[:, 512:1024]).astype(BF16)
        ho_ref[...] = h_ref[...] + _dot(y_ref[...], w_ref[...])

    row = lambda w_: pl.BlockSpec((tm, w_), lambda i: (i, 0))
    return pl.pallas_call(
        body, name=name, grid=(t // tm,),
        in_specs=[row(512), row(512), pl.BlockSpec((tm, 1024), lambda i: (i, 1)), pl.BlockSpec((1, 512), lambda i: (0, 0)),
                  pl.BlockSpec(w.shape, lambda i: (0, 0)), row(1024)],
        out_specs=[row(1024), row(1024)],
        out_shape=[jax.ShapeDtypeStruct((t, 1024), F32), jax.ShapeDtypeStruct((t, 1024), BF16)],
        compiler_params=_params(("parallel",)),
    )(o, ybraw, proj, pool_scale, w, hres)


def _gate_bwd(dh, w, o, ybraw, proj, pool_scale, *, name):
    t = o.shape[0]
    tm = _tile(t, ROW_TILE)

    def body(dh_ref, w_ref, o_ref, yb_ref, z_ref, ps_ref, do_ref, dl_ref, dyb_ref, dz_ref, dps_ref):
        i = pl.program_id(0)
        z = z_ref[...]
        sg = _sigmoid(z)
        sz = z * sg
        dsz = sg * (1.0 + z * (1.0 - sg))
        dyv = _dot(dh_ref[...], w_ref[...], NT)
        dcat = dyv * sz
        ov = o_ref[...]
        ybs = yb_ref[...] * ps_ref[...]
        dz_ref[:, 0:512] = (dyv[:, 0:512] * ov * dsz[:, 0:512]).astype(BF16)
        dz_ref[:, 512:1024] = (dyv[:, 512:1024] * ybs * dsz[:, 512:1024]).astype(BF16)
        do = dcat[:, 0:512]
        do_ref[...] = do.astype(BF16)
        r_i = (lax.broadcasted_iota(jnp.int32, (1024, 512), 0) % 512) // MLA_V
        c_i = lax.broadcasted_iota(jnp.int32, (1024, 512), 1) // MLA_V
        prod = do * ov
        hi = prod.astype(BF16)
        lo = (prod - hi.astype(F32)).astype(BF16)
        dl_ref[...] = _dot(jnp.concatenate([hi, lo], axis=1), (r_i == c_i).astype(BF16))
        dyb_ref[...] = (dcat[:, 512:1024] * ps_ref[...]).astype(BF16)

        @pl.when(i == 0)
        def _():
            dps_ref[...] = jnp.zeros_like(dps_ref)

        dps_ref[...] += jnp.sum(dcat[:, 512:1024] * yb_ref[...], axis=0, keepdims=True)

    row = lambda w: pl.BlockSpec((tm, w), lambda i: (i, 0))
    vec = pl.BlockSpec((1, 512), lambda i: (0, 0))
    return pl.pallas_call(
        body, name=name, grid=(t // tm,),
        in_specs=[row(1024), pl.BlockSpec(w.shape, lambda i: (0, 0)), row(512), row(512),
                  pl.BlockSpec((tm, 1024), lambda i: (i, 1)), vec],
        out_specs=[row(512), row(512), row(512), row(1024), vec],
        out_shape=[jax.ShapeDtypeStruct((t, 512), BF16), jax.ShapeDtypeStruct((t, 512), F32),
                   jax.ShapeDtypeStruct((t, 512), BF16), jax.ShapeDtypeStruct((t, 1024), BF16),
                   jax.ShapeDtypeStruct((1, 512), F32)],
        compiler_params=_params(("arbitrary",)),
    )(dh, w, o, ybraw, proj, pool_scale)


ATT_HP_FWD = 4
ATT_HP_BWD = 2


def _diag_mask(tq):
    return lax.broadcasted_iota(jnp.int32, (tq, tq), 1) <= lax.broadcasted_iota(jnp.int32, (tq, tq), 0)


def _block_schedule(nq, key_major):
    if key_major:
        pairs = [(qi, ki) for ki in range(nq) for qi in range(ki, nq)]
    else:
        pairs = [(qi, ki) for qi in range(nq) for ki in range(qi + 1)]
    return jnp.asarray([p[0] for p in pairs], jnp.int32), jnp.asarray([p[1] for p in pairs], jnp.int32)


def _attn_fwd(q, k, v, *, name):
    t = q.shape[0]
    tq = _tile(t, ATT_TILE)
    nq = t // tq
    hp = ATT_HP_FWD
    qi_tab, ki_tab = _block_schedule(nq, key_major=False)

    def body(qi_ref, ki_ref, q_ref, k_ref, v_ref, o_ref, lse_ref, m_sc, l_sc, acc_sc):
        step = pl.program_id(1)
        qi, ki = qi_ref[step], ki_ref[step]

        @pl.when(ki == 0)
        def _():
            m_sc[...] = jnp.full_like(m_sc, -jnp.inf)
            l_sc[...] = jnp.zeros_like(l_sc)
            acc_sc[...] = jnp.zeros_like(acc_sc)

        def block(on_diagonal):
            scores = []
            for h in range(hp):
                sl = slice(h * LANES, (h + 1) * LANES)
                scores.append(_dot(q_ref[:, sl], k_ref[:, sl], NT))
            if on_diagonal:
                mask = _diag_mask(tq)
                scores = [jnp.where(mask, s, -jnp.inf) for s in scores]
            for h, s in enumerate(scores):
                vv = v_ref[:, (h // 2) * LANES:(h // 2 + 1) * LANES]
                m_prev = m_sc[h]
                m_new = jnp.maximum(m_prev, jnp.max(s, axis=-1, keepdims=True))
                alpha = jnp.exp2(m_prev - m_new)
                p = jnp.exp2(s - m_new[:, 0:1])
                l_sc[h] = alpha * l_sc[h] + jnp.sum(p, axis=-1, keepdims=True)
                acc_sc[h] = alpha * acc_sc[h] + _dot(p.astype(BF16), vv)
                m_sc[h] = m_new

        pl.when(ki < qi)(functools.partial(block, False))
        pl.when(ki == qi)(functools.partial(block, True))

        @pl.when(ki == qi)
        def _():
            first = lax.broadcasted_iota(jnp.int32, (tq, LANES), 1) < MLA_V
            for pr in range(hp // 2):
                a, b = 2 * pr, 2 * pr + 1
                sl = slice(pr * LANES, (pr + 1) * LANES)
                o_ref[:, sl] = jnp.where(first, acc_sc[a] / l_sc[a], acc_sc[b] / l_sc[b])
                lse_ref[:, sl] = jnp.where(first, m_sc[a] + jnp.log2(l_sc[a]), m_sc[b] + jnp.log2(l_sc[b]))

    grid_spec = pltpu.PrefetchScalarGridSpec(
        num_scalar_prefetch=2, grid=(MLA_HEADS // hp, qi_tab.shape[0]),
        in_specs=[pl.BlockSpec((tq, hp * LANES), lambda g, s, qt, kt: (qt[s], g)),
                  pl.BlockSpec((tq, hp * LANES), lambda g, s, qt, kt: (kt[s], g)),
                  pl.BlockSpec((tq, hp * MLA_V), lambda g, s, qt, kt: (kt[s], g))],
        out_specs=[pl.BlockSpec((tq, hp * MLA_V), lambda g, s, qt, kt: (qt[s], g)),
                   pl.BlockSpec((tq, hp * MLA_V), lambda g, s, qt, kt: (qt[s], g))],
        scratch_shapes=[pltpu.VMEM((hp, tq, LANES), F32)] * 3,
    )
    return pl.pallas_call(
        body, name=name, grid_spec=grid_spec,
        out_shape=[jax.ShapeDtypeStruct((t, 512), F32), jax.ShapeDtypeStruct((t, 512), F32)],
        compiler_params=_params(("parallel", "arbitrary")),
    )(qi_tab, ki_tab, q, k, v)


def _attn_bwd(q, k, v, do, lse, delta, *, name):
    t = q.shape[0]
    tq = _tile(t, ATT_TILE)
    nq = t // tq
    hp = ATT_HP_BWD
    qi_tab, ki_tab = _block_schedule(nq, key_major=True)

    def body(qi_ref, ki_ref, q_ref, k_ref, v_ref, do_ref, lse_ref, dl_ref, dq_ref, dk_ref, dv_ref, dk_sc, dv_sc):
        step = pl.program_id(1)
        qi, ki = qi_ref[step], ki_ref[step]

        @pl.when(step == 0)
        def _():
            dq_ref[...] = jnp.zeros_like(dq_ref)

        @pl.when(qi == ki)
        def _():
            dk_sc[...] = jnp.zeros_like(dk_sc)
            dv_sc[...] = jnp.zeros_like(dv_sc)

        def block(on_diagonal):
            lane = lax.broadcasted_iota(jnp.int32, (tq, LANES), 1)
            rows = pl.ds(pl.multiple_of(qi * tq, tq), tq)
            heads = [slice(h * LANES, (h + 1) * LANES) for h in range(hp)]
            scores = [_dot(q_ref[:, sl], k_ref[:, sl], NT) for sl in heads]
            dps = []
            for h in range(hp):
                dov = do_ref[:, (h // 2) * LANES:(h // 2 + 1) * LANES]
                mine = (lane < MLA_V) if h % 2 == 0 else (lane >= MLA_V)
                dps.append(_dot(jnp.where(mine, dov, jnp.zeros_like(dov)), v_ref[:, (h // 2) * LANES:(h // 2 + 1) * LANES], NT))
            mask = _diag_mask(tq) if on_diagonal else None
            for h, sl in enumerate(heads):
                col = (h // 2) * LANES + (h % 2) * MLA_V
                p = jnp.exp2(scores[h] - lse_ref[:, col:col + 1])
                if on_diagonal:
                    p = jnp.where(mask, p, 0.0)
                ds = (p * (dps[h] - dl_ref[:, col:col + 1])).astype(BF16)
                dv_sc[h] += _dot(p.astype(BF16), do_ref[:, (h // 2) * LANES:(h // 2 + 1) * LANES], TN)
                dk_sc[h] += _dot(ds, q_ref[:, sl], TN)
                dq_ref[rows, sl] += _dot(ds, k_ref[:, sl], NN)

        pl.when(qi > ki)(functools.partial(block, False))
        pl.when(qi == ki)(functools.partial(block, True))

        @pl.when(qi == nq - 1)
        def _():
            first = lax.broadcasted_iota(jnp.int32, (tq, LANES), 1) < MLA_V
            for h in range(hp):
                dk_ref[:, h * LANES:(h + 1) * LANES] = dk_sc[h] * (1.0 / LOG2E)
            for pr in range(hp // 2):
                dv_ref[:, pr * LANES:(pr + 1) * LANES] = jnp.where(first, dv_sc[2 * pr], dv_sc[2 * pr + 1]).astype(BF16)

    qrow = lambda w: pl.BlockSpec((tq, w), lambda g, s, qt, kt: (qt[s], g))
    krow = lambda w: pl.BlockSpec((tq, w), lambda g, s, qt, kt: (kt[s], g))
    grid_spec = pltpu.PrefetchScalarGridSpec(
        num_scalar_prefetch=2, grid=(MLA_HEADS // hp, qi_tab.shape[0]),
        in_specs=[qrow(hp * LANES), krow(hp * LANES), krow(hp * MLA_V), qrow(hp * MLA_V), qrow(hp * MLA_V), qrow(hp * MLA_V)],
        out_specs=[pl.BlockSpec((t, hp * LANES), lambda g, s, qt, kt: (0, g)), krow(hp * LANES), krow(hp * MLA_V)],
        scratch_shapes=[pltpu.VMEM((hp, tq, LANES), F32), pltpu.VMEM((hp, tq, LANES), F32)],
    )
    return pl.pallas_call(
        body, name=name, grid_spec=grid_spec,
        out_shape=[jax.ShapeDtypeStruct((t, 1024), F32), jax.ShapeDtypeStruct((t, 1024), F32),
                   jax.ShapeDtypeStruct((t, 512), BF16)],
        compiler_params=_params(("parallel", "arbitrary")),
    )(qi_tab, ki_tab, q, k, v, do, lse, delta)


def _conv_rows(ext, tm, w_ref, sec):
    c0 = sec * 1024
    y = ext[CONV_HALO - 3:CONV_HALO - 3 + tm, c0:c0 + 1024] * w_ref[0:1, c0:c0 + 1024]
    for j in range(1, CONV_WIDTH):
        y = y + ext[CONV_HALO - 3 + j:CONV_HALO - 3 + j + tm, c0:c0 + 1024] * w_ref[j:j + 1, c0:c0 + 1024]
    return y


def _c_prep(proj_c, conv_w, a_log, dt_bias, *, name):
    t = proj_c.shape[0]
    tm = _tile(t, ROW_TILE)
    hb = tm // CONV_HALO

    def body(p_ref, halo_ref, ab_ref, w_ref, al_ref, dtb_ref, q_ref, k_ref, v_ref, g_ref, b_ref, gt_ref, ext):
        i = pl.program_id(0)
        ext[0:CONV_HALO, :] = jnp.where(i > 0, halo_ref[...], 0.0)
        ext[CONV_HALO:CONV_HALO + tm, :] = p_ref[...]
        for sec, o_ref in enumerate((q_ref, k_ref, v_ref)):
            y = _conv_rows(ext, tm, w_ref, sec)
            y = y * _sigmoid(y)
            if sec == 2:
                o_ref[...] = y
                continue
            scale = GDN_DK ** -0.5 if sec == 0 else 1.0
            for h in range(GDN_HEADS):
                sl = slice(h * LANES, (h + 1) * LANES)
                blk = y[:, sl]
                r = lax.rsqrt(jnp.sum(blk * blk, axis=-1, keepdims=True) + RMS_EPS)
                o_ref[:, sl] = blk * (r * scale)
        ab = ab_ref[...]
        g = -jnp.exp(al_ref[...]) * _softplus(ab + dtb_ref[...])
        beta = _sigmoid(ab)
        ri = lax.broadcasted_iota(jnp.int32, (tm, tm), 0)
        ci = lax.broadcasted_iota(jnp.int32, (tm, tm), 1)
        lower = ((ri // CHUNK) == (ci // CHUNK)) & (ri >= ci)
        gc = _dot(lower.astype(F32), g, NN, HI)
        eye = lax.broadcasted_iota(jnp.int32, (LANES, LANES), 0) == lax.broadcasted_iota(jnp.int32, (LANES, LANES), 1)
        gt_ref[...] = _dot(eye.astype(F32), gc, NT, HI)[0:GDN_HEADS, :]
        for h in range(GDN_HEADS):
            sl = slice(h * LANES, (h + 1) * LANES)
            g_ref[:, sl] = jnp.broadcast_to(gc[:, h:h + 1], (tm, LANES))
            b_ref[:, sl] = jnp.broadcast_to(beta[:, GDN_HEADS + h:GDN_HEADS + h + 1], (tm, LANES))

    row = lambda w: pl.BlockSpec((tm, w), lambda i: (i, 0))
    vec = lambda r, w: pl.BlockSpec((r, w), lambda i: (0, 0))
    out = jax.ShapeDtypeStruct((t, 1024), F32)
    return pl.pallas_call(
        body, name=name, grid=(t // tm,),
        in_specs=[row(3072), pl.BlockSpec((CONV_HALO, 3072), lambda i: (jnp.maximum(i * hb - 1, 0), 0)),
                  pl.BlockSpec((tm, LANES), lambda i: (i, 32)), vec(CONV_WIDTH, 3072), vec(1, LANES), vec(1, LANES)],
        out_specs=[row(1024)] * 5 + [pl.BlockSpec((GDN_HEADS, tm), lambda i: (0, i))],
        out_shape=[out] * 5 + [jax.ShapeDtypeStruct((GDN_HEADS, t), F32)],
        scratch_shapes=[pltpu.VMEM((tm + CONV_HALO, 3072), F32)],
        compiler_params=_params(("parallel",)),
    )(proj_c, proj_c, proj_c, conv_w, a_log, dt_bias)


def _c_prep_bwd(proj_c, conv_w, a_log, dt_bias, dq, dk, dv, dgb, dbb, dz, *, name):
    t = proj_c.shape[0]
    tm = _tile(t, ROW_TILE)
    hb = tm // CONV_HALO
    nt = t // tm
    rev = lambda i: nt - 1 - i

    def body(p_ref, halo_ref, ab_ref, w_ref, al_ref, dtb_ref, dq_ref, dk_ref, dv_ref, dg_ref, db_ref, dz_ref,
             dp_ref, dw_ref, dal_ref, ddt_ref, ext, dyext, carry, taps):
        step = pl.program_id(0)
        i = rev(step)

        @pl.when(step == 0)
        def _():
            dw_ref[...] = jnp.zeros_like(dw_ref)
            dal_ref[...] = jnp.zeros_like(dal_ref)
            ddt_ref[...] = jnp.zeros_like(ddt_ref)
            carry[...] = jnp.zeros_like(carry)

        ext[0:CONV_HALO, :] = jnp.where(i > 0, halo_ref[...], 0.0)
        ext[CONV_HALO:CONV_HALO + tm, :] = p_ref[...]
        for sec, g_ref in enumerate((dq_ref, dk_ref, dv_ref)):
            c0 = sec * 1024
            for j in range(CONV_WIDTH):
                taps[j] = ext[CONV_HALO - 3 + j:CONV_HALO - 3 + j + tm, c0:c0 + 1024]
            y = taps[0] * w_ref[0:1, c0:c0 + 1024]
            for j in range(1, CONV_WIDTH):
                y = y + taps[j] * w_ref[j:j + 1, c0:c0 + 1024]
            sg = _sigmoid(y)
            act = y * sg
            if sec == 2:
                dact = g_ref[...]
            else:
                scale = GDN_DK ** -0.5 if sec == 0 else 1.0
                parts = []
                for h in range(GDN_HEADS):
                    sl = slice(h * LANES, (h + 1) * LANES)
                    blk = act[:, sl]
                    r = lax.rsqrt(jnp.sum(blk * blk, axis=-1, keepdims=True) + RMS_EPS)
                    n = blk * r
                    dn = g_ref[:, sl] * scale
                    parts.append(r * (dn - n * jnp.sum(dn * n, axis=-1, keepdims=True)))
                dact = jnp.concatenate(parts, axis=-1)
            dy = dact * (sg * (1.0 + y * (1.0 - sg)))
            dyext[0:tm, c0:c0 + 1024] = dy
            for j in range(CONV_WIDTH):
                dw_ref[j:j + 1, c0:c0 + 1024] += jnp.sum(dy * taps[j], axis=0, keepdims=True)
        dyext[tm:tm + CONV_HALO, :] = carry[...]
        carry[...] = dyext[0:CONV_HALO, :]
        for sec in range(3):
            c0 = sec * 1024
            dx = dyext[3:3 + tm, c0:c0 + 1024] * w_ref[0:1, c0:c0 + 1024]
            for j in range(1, CONV_WIDTH):
                dx = dx + dyext[3 - j:3 - j + tm, c0:c0 + 1024] * w_ref[j:j + 1, c0:c0 + 1024]
            dp_ref[:, c0:c0 + 1024] = dx.astype(BF16)
        dp_ref[:, 3072:4096] = dz_ref[...]
        lane = lax.broadcasted_iota(jnp.int32, (tm, LANES), 1)
        dg = jnp.zeros((tm, LANES), F32)
        dbeta = jnp.zeros((tm, LANES), F32)
        for h in range(GDN_HEADS):
            sl = slice(h * LANES, (h + 1) * LANES)
            dg = dg + jnp.where(lane == h, dg_ref[:, sl], 0.0)
            dbeta = dbeta + jnp.where(lane == GDN_HEADS + h, db_ref[:, sl], 0.0)
        ri = lax.broadcasted_iota(jnp.int32, (tm, tm), 0)
        ci = lax.broadcasted_iota(jnp.int32, (tm, tm), 1)
        upper = ((ri // CHUNK) == (ci // CHUNK)) & (ri <= ci)
        dg = _dot(upper.astype(F32), dg, NN, HI)
        pre = ab_ref[...] + dtb_ref[...]
        s = _sigmoid(pre)
        a_exp = jnp.exp(al_ref[...])
        dg_da = dg * (-a_exp * s)
        dp_ref[:, 4096:IN_C_PAD] = (dg_da + dbeta * s * (1.0 - s)).astype(BF16)
        dal_ref[...] += jnp.sum(dg * (-a_exp * _softplus(pre)), axis=0, keepdims=True)
        ddt_ref[...] += jnp.sum(dg_da, axis=0, keepdims=True)

    row = lambda w: pl.BlockSpec((tm, w), lambda s: (rev(s), 0))
    vec = lambda r, w: pl.BlockSpec((r, w), lambda s: (0, 0))
    return pl.pallas_call(
        body, name=name, grid=(nt,),
        in_specs=[row(3072), pl.BlockSpec((CONV_HALO, 3072), lambda s: (jnp.maximum(rev(s) * hb - 1, 0), 0)),
                  pl.BlockSpec((tm, LANES), lambda s: (rev(s), 32)), vec(CONV_WIDTH, 3072), vec(1, LANES), vec(1, LANES),
                  row(1024), row(1024), row(1024), row(1024), row(1024), row(1024)],
        out_specs=[row(IN_C_PAD), vec(CONV_WIDTH, 3072), vec(1, LANES), vec(1, LANES)],
        out_shape=[jax.ShapeDtypeStruct((t, IN_C_PAD), BF16), jax.ShapeDtypeStruct((CONV_WIDTH, 3072), F32),
                   jax.ShapeDtypeStruct((1, LANES), F32), jax.ShapeDtypeStruct((1, LANES), F32)],
        scratch_shapes=[pltpu.VMEM((tm + CONV_HALO, 3072), F32), pltpu.VMEM((tm + CONV_HALO, 3072), F32),
                        pltpu.VMEM((CONV_HALO, 3072), F32), pltpu.VMEM((CONV_WIDTH, tm, 1024), F32)],
        compiler_params=_params(("arbitrary",)),
    )(proj_c, proj_c, proj_c, conv_w, a_log, dt_bias, dq, dk, dv, dgb, dbb, dz)


def _o_gate_bwd(dh, w, o, proj_c, o_norm, *, name):
    t = o.shape[0]
    tm = _tile(t, 2 * ROW_TILE)

    def body(dh_ref, w_ref, o_ref, z_ref, g_ref, do_ref, dz_ref, dg_ref, dy_ref):
        i = pl.program_id(0)

        @pl.when(i == 0)
        def _():
            dg_ref[...] = jnp.zeros_like(dg_ref)

        dy_ref[...] = _dot(dh_ref[...], w_ref[...], NT)
        dg = jnp.zeros((1, LANES), F32)
        for h in range(GDN_HEADS):
            sl = slice(h * LANES, (h + 1) * LANES)
            x = o_ref[:, sl]
            r = lax.rsqrt(jnp.mean(x * x, axis=-1, keepdims=True) + RMS_EPS)
            xh = x * r
            z = z_ref[:, sl]
            sg = _sigmoid(z)
            dyv = dy_ref[:, sl]
            dn = dyv * (z * sg)
            dz_ref[:, sl] = (dyv * xh * g_ref[...] * (sg * (1.0 + z * (1.0 - sg)))).astype(BF16)
            dxh = dn * g_ref[...]
            do_ref[:, sl] = r * (dxh - xh * jnp.mean(dxh * xh, axis=-1, keepdims=True))
            dg = dg + jnp.sum(dn * xh, axis=0, keepdims=True)
        dg_ref[...] += dg

    row = pl.BlockSpec((tm, 1024), lambda i: (i, 0))
    vec = pl.BlockSpec((1, LANES), lambda i: (0, 0))
    return pl.pallas_call(
        body, name=name, grid=(t // tm,),
        in_specs=[row, pl.BlockSpec(w.shape, lambda i: (0, 0)), row, pl.BlockSpec((tm, 1024), lambda i: (i, 3)), vec],
        out_specs=[row, row, vec],
        out_shape=[jax.ShapeDtypeStruct((t, 1024), F32), jax.ShapeDtypeStruct((t, 1024), BF16),
                   jax.ShapeDtypeStruct((1, LANES), F32)],
        scratch_shapes=[pltpu.VMEM((tm, 1024), F32)],
        compiler_params=_params(("arbitrary",)),
    )(dh, w, o, proj_c, o_norm)


PAIR = 2 * CHUNK
GDN_HP = 8


def _bdot(a, b, dims=NN):
    return _dot(a.astype(BF16), b.astype(BF16), dims)


def _each(f, *lists):
    return [f(*args) for args in zip(*lists)]


def _pair_common(q, k, v, gci, gcj, beta):
    ri = lax.broadcasted_iota(jnp.int32, (PAIR, PAIR), 0)
    ci = lax.broadcasted_iota(jnp.int32, (PAIR, PAIR), 1)
    same = (ri // CHUNK) == (ci // CHUNK)
    incl = same & (ri >= ci)
    strict = same & (ri > ci)
    eye = (ri == ci).astype(F32)
    first = lax.broadcasted_iota(jnp.int32, (PAIR, LANES), 0) < CHUNK
    gamma = _each(lambda gi, gj: jnp.where(incl, jnp.exp(jnp.minimum(gi - gj, 0.0)), 0.0), gci, gcj)
    kb = _each(jnp.multiply, k, beta)
    kk = _each(lambda a, b: _bdot(a, b, NT), kb, k)
    qk = _each(lambda a, b: _bdot(a, b, NT), q, k)
    m = _each(lambda x, g: jnp.where(strict, x * g, 0.0), kk, gamma)
    tm_ = _each(lambda x: eye - x, m)
    pw = _each(lambda x: _bdot(x, x), m)
    for it in range(5):
        tm_ = _each(lambda x, p: x + _bdot(x, p), tm_, pw)
        if it < 4:
            pw = _each(lambda p: _bdot(p, p), pw)
    eg = _each(jnp.exp, gci)
    vb = _each(jnp.multiply, v, beta)
    kbe = _each(jnp.multiply, kb, eg)
    uw = _each(lambda x, a, b: _bdot(x, jnp.concatenate([a, b], axis=1)), tm_, vb, kbe)
    attn = _each(lambda x, g: jnp.where(incl, x * g, 0.0), qk, gamma)
    gl_a = _each(lambda g: g[CHUNK - 1:CHUNK, :], gci)
    gl_b = _each(lambda g: g[PAIR - 1:PAIR, :], gci)
    ek = _each(lambda a, b, g: jnp.exp(jnp.where(first, a, b) - g), gl_a, gl_b, gci)
    return dict(incl=incl, strict=strict, gamma=gamma, kb=kb, m=m, tm=tm_, eg=eg, vb=vb, kbe=kbe,
                u=_each(lambda x: x[:, :LANES], uw), w=_each(lambda x: x[:, LANES:], uw), attn=attn,
                qd=_each(jnp.multiply, q, eg), ek=ek, kd=_each(jnp.multiply, k, ek),
                glast_a=_each(jnp.exp, gl_a), glast_b=_each(jnp.exp, gl_b))


def _gdn_specs(t, ts, order):
    nc = ts // CHUNK
    blk = pl.BlockSpec((ts, GDN_HP * LANES), lambda h, s: (order(s), h))
    row = pl.BlockSpec((GDN_HP, 1, ts), lambda h, s: (h, 0, order(s)))
    st = pl.BlockSpec((GDN_HP, nc, LANES, LANES), lambda h, s: (h, order(s), 0, 0))
    return blk, row, st


def _gdn_fwd(q, k, v, gcb, gct, bb, *, name):
    t = q.shape[0]
    ts = _tile(t, GDN_TILE)
    npair = ts // PAIR

    def body(q_ref, k_ref, v_ref, g_ref, gt_ref, b_ref, o_ref, st_ref, s_sc):
        @pl.when(pl.program_id(1) == 0)
        def _():
            s_sc[...] = jnp.zeros_like(s_sc)

        def pair(pi, _):
            rows = pl.ds(pl.multiple_of(pi * PAIR, PAIR), PAIR)
            heads = [slice(hh * LANES, (hh + 1) * LANES) for hh in range(GDN_HP)]
            c = CHUNK
            cat0 = lambda *xs: jnp.concatenate(xs, axis=0)
            s0 = [s_sc[hh] for hh in range(GDN_HP)]
            cm = _pair_common([q_ref[rows, sl] for sl in heads], [k_ref[rows, sl] for sl in heads],
                              [v_ref[rows, sl] for sl in heads], [g_ref[rows, sl] for sl in heads],
                              [gt_ref[hh, :, rows] for hh in range(GDN_HP)], [b_ref[rows, sl] for sl in heads])
            u, w, qd, kd = cm["u"], cm["w"], cm["qd"], cm["kd"]
            r0 = _each(lambda w_, q_, s: _bdot(cat0(w_[:c], q_[:c]), s), w, qd, s0)
            vn_a = _each(lambda u_, r: u_[:c] - r[:c], u, r0)
            s1 = _each(lambda s, gl, k_, vn: s * gl + _bdot(k_[:c], vn, TN), s0, cm["glast_a"], kd, vn_a)
            r1 = _each(lambda w_, q_, s: _bdot(cat0(w_[c:], q_[c:]), s), w, qd, s1)
            vn_b = _each(lambda u_, r: u_[c:] - r[:c], u, r1)
            s2 = _each(lambda s, gl, k_, vn: s * gl + _bdot(k_[c:], vn, TN), s1, cm["glast_b"], kd, vn_b)
            o = _each(lambda ra, rb, at, va, vb_: cat0(ra[c:], rb[c:]) + _bdot(at, cat0(va, vb_)),
                      r0, r1, cm["attn"], vn_a, vn_b)
            for hh, sl in enumerate(heads):
                st_ref[hh, 2 * pi] = s0[hh]
                st_ref[hh, 2 * pi + 1] = s1[hh]
                s_sc[hh] = s2[hh]
                o_ref[rows, sl] = o[hh]
            return 0

        lax.fori_loop(0, npair, pair, 0)

    blk, row, st = _gdn_specs(t, ts, lambda s: s)
    return pl.pallas_call(
        body, name=name, grid=(GDN_HEADS // GDN_HP, t // ts), in_specs=[blk, blk, blk, blk, row, blk],
        out_specs=[blk, st],
        out_shape=[jax.ShapeDtypeStruct((t, 1024), F32), jax.ShapeDtypeStruct((GDN_HEADS, t // CHUNK, LANES, LANES), F32)],
        scratch_shapes=[pltpu.VMEM((GDN_HP, LANES, LANES), F32)],
        compiler_params=_params(("parallel", "arbitrary")),
    )(q, k, v, gcb, gct, bb)


def _gdn_bwd(q, k, v, gcb, gct, bb, do, states, *, name):
    t = q.shape[0]
    ts = _tile(t, GDN_TILE)
    npair = ts // PAIR
    ns = t // ts
    c = CHUNK

    def body(q_ref, k_ref, v_ref, g_ref, gt_ref, b_ref, do_ref, st_ref, dq_ref, dk_ref, dv_ref, dg_ref, db_ref, ds_sc):
        @pl.when(pl.program_id(1) == 0)
        def _():
            ds_sc[...] = jnp.zeros_like(ds_sc)

        rowsum = lambda x: jnp.sum(x, axis=-1, keepdims=True)
        total = lambda x: jnp.sum(rowsum(x), axis=0, keepdims=True)
        cat0 = lambda *xs: jnp.concatenate(xs, axis=0)
        cat1 = lambda *xs: jnp.concatenate(xs, axis=1)

        def pair(step, _):
            pi = npair - 1 - step
            rows = pl.ds(pl.multiple_of(pi * PAIR, PAIR), PAIR)
            heads = [slice(hh * LANES, (hh + 1) * LANES) for hh in range(GDN_HP)]
            hs = range(GDN_HP)
            qv, kv, vv = ([r[rows, sl] for sl in heads] for r in (q_ref, k_ref, v_ref))
            beta = [b_ref[rows, sl] for sl in heads]
            dov = [do_ref[rows, sl] for sl in heads]
            s0 = [st_ref[hh, 2 * pi] for hh in hs]
            s1 = [st_ref[hh, 2 * pi + 1] for hh in hs]
            ds2 = [ds_sc[hh] for hh in hs]
            cm = _pair_common(qv, kv, vv, [g_ref[rows, sl] for sl in heads], [gt_ref[hh, :, rows] for hh in hs], beta)
            u, w, qd, kd, attn = cm["u"], cm["w"], cm["qd"], cm["kd"], cm["attn"]
            tmat, gamma, eg = cm["tm"], cm["gamma"], cm["eg"]
            incl, strict = cm["incl"], cm["strict"]
            vn_a = _each(lambda u_, w_, s: u_[:c] - _bdot(w_[:c], s), u, w, s0)
            vn_b = _each(lambda u_, w_, s: u_[c:] - _bdot(w_[c:], s), u, w, s1)
            vn = _each(cat0, vn_a, vn_b)
            dvn_att = _each(lambda a, d: _bdot(a, d, TN), attn, dov)
            dattn = _each(lambda d, v_: jnp.where(incl, _bdot(d, v_, NT), 0.0), dov, vn)
            dvn_b = _each(lambda x, k_, d: x[c:] + _bdot(k_[c:], d), dvn_att, kd, ds2)
            rb = _each(lambda d, x, s: _bdot(cat0(d[c:], x), s, NT), dov, dvn_b, s1)
            dkd_b = _each(lambda v_, d: _bdot(v_, d, NT), vn_b, ds2)
            dgl_b = _each(lambda d, s: total(d * s), ds2, s1)
            ds1 = _each(lambda d, gl, q_, w_, o_, x: d * gl + _bdot(cat0(q_[c:], w_[c:]), cat0(o_[c:], -x), TN),
                        ds2, cm["glast_b"], qd, w, dov, dvn_b)
            dvn_a = _each(lambda x, k_, d: x[:c] + _bdot(k_[:c], d), dvn_att, kd, ds1)
            ra = _each(lambda d, x, s: _bdot(cat0(d[:c], x), s, NT), dov, dvn_a, s0)
            dkd_a = _each(lambda v_, d: _bdot(v_, d, NT), vn_a, ds1)
            dgl_a = _each(lambda d, s: total(d * s), ds1, s0)
            ds0 = _each(lambda d, gl, q_, w_, o_, x: d * gl + _bdot(cat0(q_[:c], w_[:c]), cat0(o_[:c], -x), TN),
                        ds1, cm["glast_a"], qd, w, dov, dvn_a)
            dvn = _each(cat0, dvn_a, dvn_b)
            dqd = _each(lambda a, b: cat0(a[:c], b[:c]), ra, rb)
            dw = _each(lambda a, b: -cat0(a[c:], b[c:]), ra, rb)
            dkd = _each(cat0, dkd_a, dkd_b)
            dvw = _each(cat1, dvn, dw)
            dvbk = _each(lambda t_, x: _bdot(t_, x, TN), tmat, dvw)
            dvb = _each(lambda x: x[:, :LANES], dvbk)
            dkbe = _each(lambda x: x[:, LANES:], dvbk)
            dt_ = _each(lambda x, a, b: _bdot(x, cat1(a, b), NT), dvw, cm["vb"], cm["kbe"])
            da1 = _each(lambda t_, x: _bdot(t_, x, TN), tmat, dt_)
            dm = _each(lambda x, t_: jnp.where(strict, -_bdot(x, t_, NT), 0.0), da1, tmat)
            dkk = _each(jnp.multiply, dm, gamma)
            dqk = _each(jnp.multiply, dattn, gamma)
            z = _each(lambda a, b, c_, d: a * b + c_ * d, dm, cm["m"], dattn, attn)
            dkb = _each(lambda x, k_, y, e: _bdot(x, k_) + y * e, dkk, kv, dkbe, eg)
            dk = _each(lambda a, b, kb_, q_, x, e, y, be: _bdot(cat0(a, b), cat0(kb_, q_), TN) + x * e + y * be,
                       dkk, dqk, cm["kb"], qv, dkd, cm["ek"], dkb, beta)
            dq = _each(lambda x, k_, y, e: _bdot(x, k_) + y * e, dqk, kv, dqd, eg)

            def colsum_of(z_):
                zh = z_.astype(BF16)
                zl = (z_ - zh.astype(F32)).astype(BF16)
                return _dot(cat0(zh, zl), jnp.ones((2 * PAIR, LANES), BF16), TN)

            colsum = _each(colsum_of, z)
            ri = lax.broadcasted_iota(jnp.int32, (PAIR, LANES), 0)
            for hh, sl in enumerate(heads):
                dkd_kd = dkd[hh] * kd[hh]
                dgc = (rowsum(z[hh]) - colsum[hh] + rowsum(dqd[hh] * qd[hh]) - rowsum(dkd_kd)
                       + rowsum(dkbe[hh] * cm["kbe"][hh]))
                last_a = total(dkd_kd[:c]) + dgl_a[hh] * cm["glast_a"][hh]
                last_b = total(dkd_kd[c:]) + dgl_b[hh] * cm["glast_b"][hh]
                dgc = dgc + jnp.where(ri == c - 1, last_a, 0.0) + jnp.where(ri == PAIR - 1, last_b, 0.0)
                ds_sc[hh] = ds0[hh]
                dq_ref[rows, sl] = dq[hh]
                dk_ref[rows, sl] = dk[hh]
                dv_ref[rows, sl] = dvb[hh] * beta[hh]
                db_ref[rows, sl] = jnp.broadcast_to(rowsum(dkb[hh] * kv[hh]) + rowsum(dvb[hh] * vv[hh]), (PAIR, LANES))
                dg_ref[rows, sl] = dgc
            return 0

        lax.fori_loop(0, npair, pair, 0)

    blk, row, st = _gdn_specs(t, ts, lambda s: ns - 1 - s)
    out = jax.ShapeDtypeStruct((t, 1024), F32)
    return pl.pallas_call(
        body, name=name, grid=(GDN_HEADS // GDN_HP, ns), in_specs=[blk, blk, blk, blk, row, blk, blk, st],
        out_specs=[blk] * 5, out_shape=[out] * 5, scratch_shapes=[pltpu.VMEM((GDN_HP, LANES, LANES), F32)],
        compiler_params=_params(("parallel", "arbitrary")),
    )(q, k, v, gcb, gct, bb, do, states)


def _gate_out_proj_loss(o, proj_c, o_norm, w, hres, g, target, *, name):
    t, d = hres.shape
    tm = _tile(t, 2 * ROW_TILE)

    def body(o_ref, z_ref, on_ref, w_ref, h_ref, g_ref, t_ref, dh_ref, dhb_ref, y_ref, dg_ref, loss_ref):
        i = pl.program_id(0)
        for hd in range(GDN_HEADS):
            sl = slice(hd * LANES, (hd + 1) * LANES)
            ov = o_ref[:, sl]
            rr = lax.rsqrt(jnp.mean(ov * ov, axis=-1, keepdims=True) + RMS_EPS)
            z = z_ref[:, sl]
            y_ref[:, sl] = (ov * rr * on_ref[...] * (z * _sigmoid(z))).astype(BF16)
        x = h_ref[...] + _dot(y_ref[...], w_ref[...])
        r = lax.rsqrt(jnp.mean(x * x, axis=-1, keepdims=True) + RMS_EPS)
        xh = x * r
        err = xh * g_ref[...] - t_ref[...]
        dy = err * (1.0 / d)
        dxh = dy * g_ref[...]
        dh = r * (dxh - xh * jnp.mean(dxh * xh, axis=-1, keepdims=True))
        dh_ref[...] = dh
        dhb_ref[...] = dh.astype(BF16)

        @pl.when(i == 0)
        def _():
            dg_ref[...] = jnp.zeros_like(dg_ref)
            loss_ref[...] = jnp.zeros_like(loss_ref)

        dg_ref[...] += jnp.sum(dy * xh, axis=0, keepdims=True)
        part = 0.5 * jnp.sum(jnp.mean(err * err, axis=-1, keepdims=True), axis=0, keepdims=True)
        loss_ref[...] += jnp.broadcast_to(part, loss_ref.shape)

    row = pl.BlockSpec((tm, d), lambda i: (i, 0))
    vec = pl.BlockSpec((1, d), lambda i: (0, 0))
    return pl.pallas_call(
        body, name=name, grid=(t // tm,),
        in_specs=[row, pl.BlockSpec((tm, 1024), lambda i: (i, 3)), pl.BlockSpec((1, LANES), lambda i: (0, 0)),
                  pl.BlockSpec(w.shape, lambda i: (0, 0)), row, vec, row],
        out_specs=[row, row, row, vec, pl.BlockSpec((8, LANES), lambda i: (0, 0))],
        out_shape=[jax.ShapeDtypeStruct((t, d), F32), jax.ShapeDtypeStruct((t, d), BF16), jax.ShapeDtypeStruct((t, d), BF16),
                   jax.ShapeDtypeStruct((1, d), F32), jax.ShapeDtypeStruct((8, LANES), F32)],
        compiler_params=_params(("arbitrary",)),
    )(o, proj_c, o_norm, w, hres, g, target)


def _pad_cols(w, n):
    return jnp.pad(w, ((0, 0), (0, n - w.shape[1])))


def _layout_odd(w):
    return dict(
        winc=_pad_cols(w["w_in_c"], IN_C_PAD).astype(BF16), wout_c=w["w_out_c"].astype(BF16), conv_w=w["conv_w"],
        a_log=_pad_cols(w["a_log"], LANES), dt_bias=_pad_cols(w["dt_bias"], LANES),
        norm_c=w["norm_c"], o_norm=w["o_norm"], final_norm=w["final_norm"],
    )


def _layout_in_ab(w):
    z = lambda r, c: jnp.zeros((r, c), w["w_in_ab"].dtype)
    wi = w["w_in_ab"]
    win = jnp.concatenate([wi[:, :384], z(1024, 64), wi[:, 384:416], z(1024, 32), wi[:, 416:]], axis=1)
    pw = w["pool_w"]
    rows = []
    for g in range(4):
        rows.append(jnp.concatenate([pw[g] if j == g else jnp.zeros((128, 128), F32) for j in range(4)], axis=1))
    wpool = jnp.concatenate(rows, axis=0)
    half = MLA_ROPE // 2
    inv = 1.0 / (ROPE_THETA ** (jnp.arange(half, dtype=F32) / half))
    inv_lane = jnp.concatenate([jnp.zeros((MLA_NOPE,), F32), inv, inv, jnp.zeros((32,), F32)]).reshape(1, LANES)
    return dict(win=win.astype(BF16), wpool=wpool.astype(BF16), inv_lane=inv_lane, norm_ab=w["norm_ab"],
                q_a_norm=w["q_a_norm"], kv_a_norm=w["kv_a_norm"], pool_scale=w["pool_scale"])


def _layout_mid(w):
    wq = jnp.pad(w["w_q_b"].reshape(MLA_Q_RANK, MLA_HEADS, 96), ((0, 0), (0, 0), (0, 32))).reshape(MLA_Q_RANK, 1024)
    kv3 = w["w_kv_b"].reshape(MLA_KV_RANK, MLA_HEADS, 128)
    wk = jnp.pad(kv3[..., :MLA_NOPE], ((0, 0), (0, 0), (0, 64))).reshape(MLA_KV_RANK, 1024)
    wv = kv3[..., MLA_NOPE:].reshape(MLA_KV_RANK, 512)
    return dict(wq=wq.astype(BF16), wk=wk.astype(BF16), wv=wv.astype(BF16), wout_ab=w["w_out_ab"].astype(BF16))


def _unlayout_grads(g, names):
    out = {}
    for name in names:
        if name == "w_in_ab":
            dwin = g["win"]
            out[name] = jnp.concatenate([dwin[:384], dwin[448:480], dwin[512:]], axis=0)
        elif name == "w_q_b":
            out[name] = g["wq"].reshape(MLA_Q_RANK, MLA_HEADS, 128)[..., :96].reshape(MLA_Q_RANK, 768)
        elif name == "w_kv_b":
            out[name] = jnp.concatenate([g["wk"].reshape(MLA_KV_RANK, MLA_HEADS, 128)[..., :MLA_NOPE],
                                         g["wv"].reshape(MLA_KV_RANK, MLA_HEADS, MLA_V)], axis=-1).reshape(MLA_KV_RANK, 1024)
        elif name == "w_in_c":
            out[name] = g["winc"][:4112]
        else:
            out[name] = g[{"w_out_ab": "wout_ab", "w_out_c": "wout_c"}[name]]
    return out


def _local_step(x, pos, target, lw, more_weights=None, on_grads=None):
    mm = _matmul
    proj, hn = _rms_in_proj(x, lw["norm_ab"], lw["win"], name="rms_in_ab")
    if more_weights is not None:
        lw = {**lw, **more_weights("mid", proj)}
    q, k, v, ybraw, qn, kvn, d, cos_t, sin_t = _ab_prep(
        proj, pos, lw["inv_lane"], lw["q_a_norm"], lw["kv_a_norm"], lw["wq"], lw["wk"], lw["wv"], lw["wpool"], name="ab_prep")
    o, lse = _attn_fwd(q, k, v, name="attn_fwd")
    h1, y = _gate_out_proj(o, ybraw, proj, lw["pool_scale"], lw["wout_ab"], x, name="gate_out_ab")
    lo = lw if more_weights is None else more_weights("odd", h1)
    proj_c, hn1 = _rms_in_proj(h1, lo["norm_c"], lo["winc"], name="rms_in_c")
    q2, k2, v2, gb, bb, gt = _c_prep(proj_c, lo["conv_w"], lo["a_log"], lo["dt_bias"], name="c_prep")
    gt = gt.reshape(GDN_HEADS, 1, gt.shape[1])
    o2, states = _gdn_fwd(q2, k2, v2, gb, gt, bb, name="gdn_fwd")
    dh2, dh2b, y2, d_final, loss = _gate_out_proj_loss(
        o2, proj_c, lo["o_norm"], lo["wout_c"], h1, lo["final_norm"], target, name="gate_out_c_loss")
    g = {"final_norm": d_final}
    g["wout_c"] = mm(y2, dh2b, "tn", name="out_c_dw")
    do2, dz2, g["o_norm"] = _o_gate_bwd(dh2b, lo["wout_c"], o2, proj_c, lo["o_norm"], name="gate_c_bwd")
    dq2, dk2, dv2, dgb, dbb = _gdn_bwd(q2, k2, v2, gb, gt, bb, do2, states, name="gdn_bwd")
    dproj_c, g["conv_w"], g["a_log"], g["dt_bias"] = _c_prep_bwd(
        proj_c, lo["conv_w"], lo["a_log"], lo["dt_bias"], dq2, dk2, dv2, dgb, dbb, dz2, name="c_prep_bwd")
    g["winc"] = mm(dproj_c, hn1, "tn", name="in_c_dw")
    notify = (lambda tag: 0.0) if on_grads is None else (lambda tag: on_grads(tag, g))
    pool_scale = lw["pool_scale"] + notify("odd")
    dh1, dh1b, g["norm_c"] = _matmul_rms_bwd(dproj_c, lo["winc"], h1, lo["norm_c"], dh2, name="in_c_dx_rms", with_bf16=True)
    g["wout_ab"] = mm(y, dh1b, "tn", name="out_ab_dw")
    pool_scale = pool_scale + notify("out_ab")
    do, delta, dyb, dz, g["pool_scale"] = _gate_bwd(dh1b, lw["wout_ab"], o, ybraw, proj, pool_scale, name="gate_ab_bwd")
    dq, dk, dv = _attn_bwd(q, k, v, do, lse, delta, name="attn_bwd")
    dproj, dqraw, dkb, g["q_a_norm"], g["kv_a_norm"] = _ab_prep_bwd(
        proj, lw["q_a_norm"], lw["kv_a_norm"], dq, dk, dv, cos_t, sin_t, dyb, dz,
        lw["wq"], lw["wk"], lw["wv"], lw["wpool"], name="ab_prep_bwd")
    g["wpool"] = mm(d, dyb, "tn", name="pool_mix_dw")
    g["wq"] = mm(qn, dqraw, "tn", name="q_up_dw")
    g["wk"] = mm(kvn, dkb, "tn", name="k_up_dw")
    g["wv"] = mm(kvn, dv, "tn", name="v_up_dw")
    g["win"] = mm(dproj, hn, "tn", name="in_ab_dw")
    norm_ab = lw["norm_ab"] + notify("in_ab")
    dx, g["norm_ab"] = _matmul_rms_bwd(dproj, lw["win"], x, norm_ab, dh1, name="in_ab_dx_rms", with_bf16=False)
    return loss, dx, g


_HBM = pl.BlockSpec(memory_space=pltpu.HBM)


def _place():
    return lax.axis_index("x"), lax.axis_index("y"), lax.axis_index("c")


def _flip(v, f):
    return 1 - v if f else v


_CHIP_FLIPS = ((1, 0), (0, 1), (1, 1))
_DEV_FLIPS = tuple((fx, fy, fc) for fx in (0, 1) for fy in (0, 1) for fc in (0, 1) if fx or fy or fc)


def _rcopy(src, dst, send_sems, recv_sems, k, to):
    return pltpu.make_async_remote_copy(src_ref=src, dst_ref=dst, send_sem=send_sems.at[k], recv_sem=recv_sems.at[k],
                                        device_id=to, device_id_type=MESH)


def _my_half(ref, c, axis):
    rh = ref.shape[axis] // 2
    idx = [slice(None)] * len(ref.shape)
    idx[axis] = pl.ds(c * rh, rh)
    return ref.at[tuple(idx)]


def _gather_weights(bigs, smalls):
    nb, ns = len(bigs), len(smalls)

    def body(*refs):
        ins, outs = refs[:nb + ns], refs[nb + ns:2 * (nb + ns)]
        send_sems, recv_sems, local_sems = refs[2 * (nb + ns):]
        x, y, c = _place()
        j0 = 2 * x + y
        sib = (x, y, 1 - c)
        chips = [(_flip(x, fx), _flip(y, fy)) for fx, fy in _CHIP_FLIPS]
        local = [pltpu.make_async_copy(i_ref, o_ref.at[j0], local_sems.at[a])
                 for a, (i_ref, o_ref) in enumerate(zip(ins, outs))]
        for cp in local:
            cp.start()
        sends = []
        for k, (px, py) in enumerate(chips):
            for a in range(nb):
                sends.append(_rcopy(_my_half(ins[a], c, 0), _my_half(outs[a].at[j0], c, 0), send_sems, recv_sems,
                                    6 * a + k, (px, py, c)))
            for s in range(ns):
                sends.append(_rcopy(ins[nb + s], outs[nb + s].at[j0], send_sems, recv_sems, 6 * nb + 3 * s + k, (px, py, c)))
        for cp in sends:
            cp.start()
        for k, (px, py) in enumerate(chips):
            jk = 2 * px + py
            for a in range(nb):
                landed = _my_half(outs[a].at[jk], c, 0)
                _rcopy(landed, landed, send_sems, recv_sems, 6 * a + k, (px, py, c)).wait_recv()
                fwd = _rcopy(landed, landed, send_sems, recv_sems, 6 * a + 3 + k, sib)
                fwd.start()
                sends.append(fwd)
        for k, (px, py) in enumerate(chips):
            jk = 2 * px + py
            for a in range(nb):
                other = _my_half(outs[a].at[jk], 1 - c, 0)
                _rcopy(other, other, send_sems, recv_sems, 6 * a + 3 + k, sib).wait_recv()
            for s in range(ns):
                _rcopy(ins[nb + s], outs[nb + s].at[jk], send_sems, recv_sems, 6 * nb + 3 * s + k, (px, py, c)).wait_recv()
        for cp in sends:
            cp.wait_send()
        for cp in local:
            cp.wait()

    arrays = list(bigs) + list(smalls)
    n_sem = 6 * nb + 3 * ns
    return pl.pallas_call(
        body, name="gather_weights", in_specs=[_HBM] * len(arrays), out_specs=[_HBM] * len(arrays),
        out_shape=[jax.ShapeDtypeStruct((4,) + a.shape, a.dtype) for a in arrays],
        scratch_shapes=[pltpu.SemaphoreType.DMA((n_sem,)), pltpu.SemaphoreType.DMA((n_sem,)),
                        pltpu.SemaphoreType.DMA((len(arrays),))],
    )(*arrays)


def _core_swap_partial(gs, by_cols, *, name):
    n = len(gs)

    def body(*refs):
        ins, outs = refs[:n], refs[n:2 * n]
        send_sems, recv_sems = refs[2 * n:]
        x, y, c = _place()
        copies = [_rcopy(_my_half(i_ref, 1 - c, 2 if by_cols[a] else 1), o_ref, send_sems, recv_sems, a, (x, y, 1 - c))
                  for a, (i_ref, o_ref) in enumerate(zip(ins, outs))]
        for cp in copies:
            cp.start()
        for cp in copies:
            cp.wait()

    halved = lambda g, cols: (4, g.shape[1], g.shape[2] // 2) if cols else (4, g.shape[1] // 2, g.shape[2])
    return pl.pallas_call(
        body, name=name, in_specs=[_HBM] * n, out_specs=[_HBM] * n,
        out_shape=[jax.ShapeDtypeStruct(halved(g, cols), g.dtype) for g, cols in zip(gs, by_cols)],
        scratch_shapes=[pltpu.SemaphoreType.DMA((n,)), pltpu.SemaphoreType.DMA((n,))],
    )(*gs)


def _core_swap_sum(fs, by_cols):
    n = len(fs)

    def body(*refs):
        ins, outs = refs[:n], refs[n:2 * n]
        send_sems, recv_sems = refs[2 * n:]
        x, y, c = _place()
        axes = [1 if cols else 0 for cols in by_cols]
        copies = [_rcopy(_my_half(i_ref, c, ax), _my_half(o_ref, c, ax), send_sems, recv_sems, a, (x, y, 1 - c))
                  for a, (i_ref, o_ref, ax) in enumerate(zip(ins, outs, axes))]
        for cp in copies:
            cp.start()
        for a, cp in enumerate(copies):
            cp.wait_send()
            theirs = _my_half(outs[a], 1 - c, axes[a])
            _rcopy(theirs, theirs, send_sems, recv_sems, a, (x, y, 1 - c)).wait_recv()

    return pl.pallas_call(
        body, name="core_swap_sum", in_specs=[_HBM] * n, out_specs=[_HBM] * n,
        out_shape=[jax.ShapeDtypeStruct(f.shape, f.dtype) for f in fs],
        input_output_aliases={a: a for a in range(n)},
        scratch_shapes=[pltpu.SemaphoreType.DMA((n,)), pltpu.SemaphoreType.DMA((n,))],
    )(*fs)


_SEM = pl.BlockSpec(memory_space=pltpu.SEMAPHORE)
_ANY = pl.BlockSpec(memory_space=pl.ANY)
_DATAFLOW = pltpu.SideEffectType.DATAFLOW_SIDE_EFFECTING


def _to_chips_copies(srcs, lands, send_sems, recv_sems, per_chip_slot):
    x, y, c = _place()
    j0 = 2 * x + y
    out = []
    for k, (fx, fy) in enumerate(_CHIP_FLIPS):
        px, py = _flip(x, fx), _flip(y, fy)
        jk = 2 * px + py
        for a, (src, land) in enumerate(zip(srcs, lands)):
            piece = src.at[jk] if per_chip_slot else src
            out.append((_rcopy(piece, land.at[j0], send_sems, recv_sems, 3 * a + k, (px, py, c)),
                        _rcopy(piece, land.at[jk], send_sems, recv_sems, 3 * a + k, (px, py, c))))
    return out


def _to_chips_start(arrays, *, per_chip_slot, name, after=None):
    n = len(arrays)
    lands = [lax.empty((4,) + (a.shape[1:] if per_chip_slot else a.shape), a.dtype) for a in arrays]
    extra = [] if after is None else [after]

    def body(*refs):
        srcs, land_refs, token = refs[:n], refs[n:2 * n], refs[-1]
        send_sems, recv_sems = refs[2 * n + len(extra)], refs[2 * n + len(extra) + 1]
        for send, _ in _to_chips_copies(srcs, land_refs, send_sems, recv_sems, per_chip_slot):
            send.start()
        token[...] = jnp.zeros_like(token)

    held = [pltpu.with_memory_space_constraint(a, pltpu.HBM) for a in list(arrays) + lands]
    return pl.pallas_call(
        body, name=name, in_specs=[_HBM] * (2 * n) + [_ANY] * len(extra),
        out_specs=(_SEM, _SEM, *[_HBM] * (2 * n), pl.BlockSpec(memory_space=pltpu.VMEM)),
        out_shape=(pltpu.SemaphoreType.DMA((3 * n,)), pltpu.SemaphoreType.DMA((3 * n,)),
                   *[pltpu.HBM(a.shape, a.dtype) for a in held], jax.ShapeDtypeStruct((8, LANES), F32)),
        input_output_aliases={i: 2 + i for i in range(2 * n)},
        compiler_params=pltpu.CompilerParams(has_side_effects=_DATAFLOW),
    )(*held, *extra)


def _to_chips_wait(started, after, *, per_chip_slot, name):
    send_sems, recv_sems, held = started[0], started[1], started[2:-1]
    n = len(held) // 2

    def body(*refs):
        srcs, land_refs, s_sems, r_sems = refs[:n], refs[n:2 * n], refs[2 * n], refs[2 * n + 1]
        for send, arrival in _to_chips_copies(srcs, land_refs, s_sems, r_sems, per_chip_slot):
            send.wait_send()
            arrival.wait_recv()

    out = pl.pallas_call(
        body, name=name, in_specs=[_HBM] * (2 * n) + [_SEM, _SEM, _ANY], out_specs=[_HBM] * (2 * n),
        out_shape=[pltpu.HBM(a.shape, a.dtype) for a in held],
        input_output_aliases={i: i for i in range(2 * n)},
        compiler_params=pltpu.CompilerParams(has_side_effects=_DATAFLOW),
    )(*held, send_sems, recv_sems, after)
    return out[n:]


def _chip_exchange(ps, small):
    n = len(ps)
    rs = small.shape[0]

    def body(*refs):
        p_refs, s_ref = refs[:n], refs[n]
        l_refs, ls_ref = refs[n + 1:2 * n + 1], refs[2 * n + 1]
        send_sems, recv_sems, local_sems = refs[2 * n + 2:]
        x, y, c = _place()
        j0 = 2 * x + y
        d0 = 2 * j0 + c
        local = [pltpu.make_async_copy(p.at[j0], l.at[j0], local_sems.at[a]) for a, (p, l) in enumerate(zip(p_refs, l_refs))]
        local.append(pltpu.make_async_copy(s_ref, ls_ref.at[d0], local_sems.at[n]))
        for cp in local:
            cp.start()
        sends = []
        for k, (fx, fy) in enumerate(_CHIP_FLIPS):
            px, py = _flip(x, fx), _flip(y, fy)
            for a in range(n):
                sends.append(_rcopy(p_refs[a].at[2 * px + py], l_refs[a].at[j0], send_sems, recv_sems, 3 * a + k, (px, py, c)))
        for k, (fx, fy, fc) in enumerate(_DEV_FLIPS):
            peer = (_flip(x, fx), _flip(y, fy), _flip(c, fc))
            sends.append(_rcopy(s_ref, ls_ref.at[d0], send_sems, recv_sems, 3 * n + k, peer))
        for cp in sends:
            cp.start()
        for k, (fx, fy) in enumerate(_CHIP_FLIPS):
            px, py = _flip(x, fx), _flip(y, fy)
            for a in range(n):
                _rcopy(p_refs[a].at[j0], l_refs[a].at[2 * px + py], send_sems, recv_sems, 3 * a + k, (px, py, c)).wait_recv()
        for k, (fx, fy, fc) in enumerate(_DEV_FLIPS):
            px, py, pc = _flip(x, fx), _flip(y, fy), _flip(c, fc)
            _rcopy(s_ref, ls_ref.at[4 * px + 2 * py + pc], send_sems, recv_sems, 3 * n + k, (px, py, pc)).wait_recv()
        for cp in sends:
            cp.wait_send()
        for cp in local:
            cp.wait()

    n_sem = 3 * n + 7
    return pl.pallas_call(
        body, name="chip_exchange", in_specs=[_HBM] * (n + 1), out_specs=[_HBM] * (n + 1),
        out_shape=[jax.ShapeDtypeStruct(p.shape, F32) for p in ps] + [jax.ShapeDtypeStruct((8, rs, LANES), F32)],
        scratch_shapes=[pltpu.SemaphoreType.DMA((n_sem,)), pltpu.SemaphoreType.DMA((n_sem,)),
                        pltpu.SemaphoreType.DMA((n + 1,))],
    )(*ps, small)


def _half_blocks(rows, cols, by_cols):
    if by_cols:
        tc = _tile(cols // 2, 256)
        nb = cols // 2 // tc
        return rows, tc, nb, (lambda i, c: (0, c * nb + i))
    tr = _tile(rows // 2, 256)
    nb = rows // 2 // tr
    return tr, cols, nb, (lambda i, c: (c * nb + i, 0))


def _core_sum(g, part, core, *, name, by_cols):
    _, rows, cols = g.shape
    br, bc, nb, whole = _half_blocks(rows, cols, by_cols)
    mine = (lambda i: (0, i)) if by_cols else (lambda i: (i, 0))

    def body(c_ref, g_ref, p_ref, o_ref):
        o_ref[...] = g_ref[...] + p_ref[...]

    grid_spec = pltpu.PrefetchScalarGridSpec(
        num_scalar_prefetch=1, grid=(4, nb),
        in_specs=[pl.BlockSpec((1, br, bc), lambda j, i, c: (j,) + whole(i, c[0])),
                  pl.BlockSpec((1, br, bc), lambda j, i, c: (j,) + mine(i))],
        out_specs=pl.BlockSpec((1, br, bc), lambda j, i, c: (j,) + mine(i)),
    )
    return pl.pallas_call(
        body, name=name, grid_spec=grid_spec, out_shape=jax.ShapeDtypeStruct(part.shape, F32),
        compiler_params=_params(("parallel", "parallel")),
    )(core, g, part)


def _chip_sum(landed, core, *, name, by_cols):
    _, hr, hc = landed.shape
    rows, cols = (hr, 2 * hc) if by_cols else (2 * hr, hc)
    br, bc, nb, whole = _half_blocks(rows, cols, by_cols)
    mine = (lambda i: (0, i)) if by_cols else (lambda i: (i, 0))

    def body(c_ref, l_ref, o_ref):
        o_ref[...] = ((l_ref[0] + l_ref[1]) + l_ref[2]) + l_ref[3]

    grid_spec = pltpu.PrefetchScalarGridSpec(
        num_scalar_prefetch=1, grid=(nb,),
        in_specs=[pl.BlockSpec((4, br, bc), lambda i, c: (0,) + mine(i))],
        out_specs=pl.BlockSpec((br, bc), lambda i, c: whole(i, c[0])),
    )
    return pl.pallas_call(
        body, name=name, grid_spec=grid_spec, out_shape=jax.ShapeDtypeStruct((rows, cols), F32),
        compiler_params=_params(("parallel",)),
    )(core, landed)


_ROW_POOL_W, _ROW_NORM_AB, _ROW_FINAL, _ROW_POOL_SCALE, _ROW_Q_NORM = 0, 512, 520, 528, 532
_ROW_KV_NORM, _ROW_O_NORM, _ROW_A_LOG, _ROW_DT_BIAS, _ROW_LOSS = 534, 535, 536, 537, 538
_ROW_CONV, _ROW_NORM_C, _SMALL_ROWS = 544, 640, 672
_CONV_ROWS = CONV_WIDTH * 6


def _put_rows(dst_ref, row0, src, width):
    for r in range(width // LANES):
        dst_ref[row0 + r:row0 + r + 1, :] = src[:, r * LANES:(r + 1) * LANES]


def _pack_small(g, loss_tile):
    names = ("wpool", "norm_ab", "final_norm", "pool_scale", "q_a_norm", "kv_a_norm", "o_norm", "a_log", "dt_bias",
             "conv_w", "norm_c")

    def body(wpool, norm_ab, final_norm, pool_scale, q_norm, kv_norm, o_norm, a_log, dt_bias, conv_w, norm_c, loss, o_ref):
        o_ref[...] = jnp.zeros_like(o_ref)
        for gi in range(4):
            o_ref[_ROW_POOL_W + gi * 128:_ROW_POOL_W + (gi + 1) * 128, :] = wpool[gi * 128:(gi + 1) * 128, gi * 128:(gi + 1) * 128]
        _put_rows(o_ref, _ROW_NORM_AB, norm_ab[...], 1024)
        _put_rows(o_ref, _ROW_FINAL, final_norm[...], 1024)
        _put_rows(o_ref, _ROW_POOL_SCALE, pool_scale[...], 512)
        _put_rows(o_ref, _ROW_Q_NORM, q_norm[...], 256)
        for row, ref in ((_ROW_KV_NORM, kv_norm), (_ROW_O_NORM, o_norm), (_ROW_A_LOG, a_log), (_ROW_DT_BIAS, dt_bias)):
            o_ref[row:row + 1, :] = ref[...]
        o_ref[_ROW_LOSS:_ROW_LOSS + 1, :] = loss[0:1, :]
        for j in range(4):
            for r in range(CONV_WIDTH):
                _put_rows(o_ref, _ROW_CONV + j * _CONV_ROWS + r * 6, conv_w[r:r + 1, j * 768:(j + 1) * 768], 768)
            _put_rows(o_ref, _ROW_NORM_C + j * 8, norm_c[:, j * 256:(j + 1) * 256], 256)

    vmem = pl.BlockSpec(memory_space=pltpu.VMEM)
    return pl.pallas_call(
        body, name="pack_small", in_specs=[vmem] * 12, out_specs=vmem,
        out_shape=jax.ShapeDtypeStruct((_SMALL_ROWS, LANES), F32),
    )(*[g[n] for n in names], loss_tile)


_SMALL_NAMES = ("pool_w", "norm_ab", "final_norm", "pool_scale", "q_a_norm", "kv_a_norm", "o_norm", "a_log", "dt_bias",
                "conv_w", "norm_c")


def _take_rows(src, row0, width):
    return jnp.concatenate([src[row0 + r:row0 + r + 1, :] for r in range(width // LANES)], axis=1)


def _small_update(small_all, ws, ms, vs):
    n = len(_SMALL_NAMES)

    def body(*refs):
        a_ref = refs[0]
        w_refs, m_refs, v_refs = refs[1:1 + n], refs[1 + n:1 + 2 * n], refs[1 + 2 * n:1 + 3 * n]
        outs = refs[1 + 3 * n:1 + 7 * n]
        loss_ref, tot = refs[1 + 7 * n], refs[2 + 7 * n]
        acc = a_ref[0]
        for d in range(1, 8):
            acc = acc + a_ref[d]
        tot[...] = acc
        x, y, _ = _place()
        j0 = 2 * x + y
        conv = tot[pl.ds(pl.multiple_of(_ROW_CONV + j0 * _CONV_ROWS, 8), _CONV_ROWS), :]
        norm_c = tot[pl.ds(pl.multiple_of(_ROW_NORM_C + j0 * 8, 8), 8), :]
        whole = tot[_ROW_NORM_AB:_ROW_CONV, :]
        at = lambda row: row - _ROW_NORM_AB
        grads = {
            "norm_ab": _take_rows(whole, at(_ROW_NORM_AB), 1024), "final_norm": _take_rows(whole, at(_ROW_FINAL), 1024),
            "pool_scale": _take_rows(whole, at(_ROW_POOL_SCALE), 512), "q_a_norm": _take_rows(whole, at(_ROW_Q_NORM), 256),
            "kv_a_norm": whole[at(_ROW_KV_NORM):at(_ROW_KV_NORM) + 1, :], "o_norm": whole[at(_ROW_O_NORM):at(_ROW_O_NORM) + 1, :],
            "a_log": tot[_ROW_A_LOG:_ROW_A_LOG + 1, 0:GDN_HEADS],
            "dt_bias": tot[_ROW_DT_BIAS:_ROW_DT_BIAS + 1, 0:GDN_HEADS],
            "norm_c": _take_rows(norm_c, 0, 256),
        }
        loss_ref[...] = whole[at(_ROW_LOSS):at(_ROW_LOSS) + 1, :]
        for i, name in enumerate(_SMALL_NAMES):
            g_out = outs[4 * i]
            if name == "pool_w":
                for gi in range(4):
                    g_out[gi] = tot[_ROW_POOL_W + gi * 128:_ROW_POOL_W + (gi + 1) * 128, :]
            elif name == "conv_w":
                for r in range(CONV_WIDTH):
                    g_out[r:r + 1, :] = _take_rows(conv, r * 6, 768)
            else:
                g_out[...] = grads[name]
            _adam_update(g_out, w_refs[i], m_refs[i], v_refs[i], *outs[4 * i + 1:4 * i + 4])

    vmem = pl.BlockSpec(memory_space=pltpu.VMEM)
    out_shape = [jax.ShapeDtypeStruct(w.shape, F32) for w in ws for _ in range(4)] + [jax.ShapeDtypeStruct((1, LANES), F32)]
    return pl.pallas_call(
        body, name="small_update", in_specs=[vmem] * (1 + 3 * n), out_specs=[vmem] * (4 * n + 1), out_shape=out_shape,
        scratch_shapes=[pltpu.VMEM((_SMALL_ROWS, LANES), F32)],
        compiler_params=pltpu.CompilerParams(vmem_limit_bytes=VMEM_LIMIT),
    )(small_all, *ws, *ms, *vs)


def _adam_update(g_ref, w_ref, m_ref, v_ref, d_ref, mo_ref, vo_ref):
    gv = g_ref[...]
    mn = ADAM_B1 * m_ref[...] + (1.0 - ADAM_B1) * gv
    vn = ADAM_B2 * v_ref[...] + (1.0 - ADAM_B2) * (gv * gv)
    mo_ref[...] = mn
    vo_ref[...] = vn
    c1 = 1.0 - ADAM_B1 ** ADAM_STEP
    c2 = 1.0 - ADAM_B2 ** ADAM_STEP
    d_ref[...] = -ADAM_LR * ((mn / c1) / (jnp.sqrt(vn / c2) + ADAM_EPS) + ADAM_WD * w_ref[...])


def _adamw_rows(g, w, m, v, *, name):
    rows, cols = g.shape
    if rows % LANES == 0:
        tr = _tile(rows, 512)
        blk, steps = pl.BlockSpec((tr, cols), lambda i: (i, 0)), rows // tr
    else:
        tc = _tile(cols, 256)
        blk, steps = pl.BlockSpec((rows, tc), lambda i: (0, i)), cols // tc

    def body(*refs):
        _adam_update(*refs)

    out = jax.ShapeDtypeStruct((rows, cols), F32)
    return pl.pallas_call(
        body, name=name, grid=(steps,), in_specs=[blk] * 4, out_specs=[blk] * 3, out_shape=[out] * 3,
        compiler_params=_params(("parallel",)),
    )(g, w, m, v)


_ADAM_ROWWISE = ("w_in_ab", "w_q_b", "w_kv_b", "w_out_ab", "w_in_c", "w_out_c")


_SHARD_AXIS = {"w_in_ab": 1, "w_q_b": 1, "w_kv_b": 1, "w_out_ab": 0, "w_in_c": 1, "w_out_c": 0, "conv_w": 1, "norm_c": 1}
_ALL_NAMES = ("norm_ab", "w_in_ab", "q_a_norm", "w_q_b", "kv_a_norm", "w_kv_b", "pool_w", "pool_scale", "w_out_ab",
              "norm_c", "w_in_c", "conv_w", "a_log", "dt_bias", "o_norm", "w_out_c", "final_norm")


def _join_shards(a, axis):
    _, r, c = a.shape
    return a.reshape(4 * r, c) if axis == 0 else jnp.transpose(a, (1, 0, 2)).reshape(r, 4 * c)


def _split_shards(a, axis):
    r, c = a.shape
    return a.reshape(4, r // 4, c) if axis == 0 else jnp.transpose(a.reshape(r, 4, c // 4), (1, 0, 2))


def kernel(x, positions, norm_ab, w_in_ab, q_a_norm, w_q_b, kv_a_norm, w_kv_b, pool_w, pool_scale, w_out_ab, norm_c, w_in_c, conv_w, a_log, dt_bias, o_norm, w_out_c, final_norm, loss_target, m_norm_ab, m_w_in_ab, m_q_a_norm, m_w_q_b, m_kv_a_norm, m_w_kv_b, m_pool_w, m_pool_scale, m_w_out_ab, m_norm_c, m_w_in_c, m_conv_w, m_a_log, m_dt_bias, m_o_norm, m_w_out_c, m_final_norm, v_norm_ab, v_w_in_ab, v_q_a_norm, v_w_q_b, v_kv_a_norm, v_w_kv_b, v_pool_w, v_pool_scale, v_w_out_ab, v_norm_c, v_w_in_c, v_conv_w, v_a_log, v_dt_bias, v_o_norm, v_w_out_c, v_final_norm):
    given = dict(locals())
    c = lax.axis_index("c")
    t = x.shape[1]

    def shard_of(prefix, name):
        a = given[prefix + name]
        return a.reshape(a.shape[1:]) if a.ndim > 2 else a.reshape(1, -1)

    big, big_even, big_odd, small_sharded = _ADAM_ROWWISE, _ADAM_ROWWISE[:4], _ADAM_ROWWISE[4:], ("conv_w", "norm_c")
    chip = 2 * lax.axis_index("x") + lax.axis_index("y")
    core = c.astype(jnp.int32).reshape(1)
    later = {"mid": big_even[1:], "odd": big_odd + small_sharded}
    travelling = {}

    def send(tag, after=None):
        shards = [shard_of("", n).astype(BF16) if n in big else shard_of("", n) for n in later[tag]]
        started = _to_chips_start(shards, per_chip_slot=False, name="gather_" + tag + "_start", after=after)
        travelling[tag] = (shards, started)
        return started[-1][0, 0]

    mid_sent = send("mid")
    gathered = _gather_weights([shard_of("", "w_in_ab").astype(BF16)], [])
    full = {"w_in_ab": _join_shards(gathered[0], _SHARD_AXIS["w_in_ab"])}
    for name in ("norm_ab", "q_a_norm", "kv_a_norm", "pool_w", "pool_scale"):
        full[name] = shard_of("", name)
    lw = _layout_in_ab(full)
    lw["norm_ab"] = lw["norm_ab"] + mid_sent

    def more_weights(tag, after):
        shards, started = travelling[tag]
        landed = _to_chips_wait(started, after, per_chip_slot=False, name="gather_" + tag + "_wait")
        w = {}
        for name, land, own in zip(later[tag], landed, shards):
            w[name] = _join_shards(lax.dynamic_update_index_in_dim(land, own, chip, 0), _SHARD_AXIS[name])
        if tag == "mid":
            out = _layout_mid(w)
            out["wq"] = out["wq"] + send("odd", after=landed[0]).astype(BF16)
            return out
        for name in ("a_log", "dt_bias", "o_norm", "final_norm"):
            w[name] = shard_of("", name)
        return _layout_odd(w)

    transposed = ("w_in_ab", "w_in_c")

    def chip_partials(names, grads, tag):
        by_cols = [n in transposed for n in names]
        slots = [_split_shards(grads[n], 0 if n in transposed else _SHARD_AXIS[n]) for n in names]
        partial = _core_swap_partial(slots, by_cols, name="core_swap_partial_" + tag)
        return [_core_sum(s, p, core, name="core_sum_" + n, by_cols=b) for n, s, p, b in zip(names, slots, partial, by_cols)]

    groups = {"odd": big_odd, "out_ab": ("w_out_ab",), "in_ab": ("w_in_ab", "w_q_b", "w_kv_b")}
    sent = {}

    def on_grads(tag, g):
        part = chip_partials(groups[tag], _unlayout_grads(g, groups[tag]), tag)
        sent[tag] = (part, _to_chips_start(part, per_chip_slot=True, name="exchange_" + tag + "_start"))
        return sent[tag][1][-1][0, 0]

    loss_tile, dx, g = _local_step(x[0], positions.reshape(t, 1), loss_target[0], lw, more_weights, on_grads)
    small_all = _chip_exchange([], _pack_small(g, loss_tile))[-1]
    halves = {}
    for tag, names in groups.items():
        part, started = sent[tag]
        landed = _to_chips_wait(started, small_all, per_chip_slot=True, name="exchange_" + tag + "_wait")
        for n, l, p in zip(names, landed, part):
            l = lax.dynamic_update_index_in_dim(l, lax.dynamic_index_in_dim(p, chip, 0, keepdims=False), chip, 0)
            halves[n] = _chip_sum(l, core, name="chip_sum_" + n, by_cols=n in transposed)
    gbig = dict(zip(big, _core_swap_sum([halves[n] for n in big], [n in transposed for n in big])))

    res = {}
    for name in big:
        operands = [gbig[name], shard_of("", name), shard_of("m_", name), shard_of("v_", name)]
        flip = name in transposed
        if flip:
            operands[1:] = [jnp.transpose(a) for a in operands[1:]]
        out = (operands[0],) + tuple(_adamw_rows(*operands, name="adamw_" + name))
        out = [jnp.transpose(a) for a in out] if flip else out
        res["grad", name], res["delta", name], res["m", name], res["v", name] = out
    out = _small_update(small_all, [shard_of("", n) for n in _SMALL_NAMES], [shard_of("m_", n) for n in _SMALL_NAMES],
                        [shard_of("v_", n) for n in _SMALL_NAMES])
    for i, name in enumerate(_SMALL_NAMES):
        res["grad", name], res["delta", name], res["m", name], res["v", name] = out[4 * i:4 * i + 4]
    res = {k: a.reshape(given[k[1]].shape) for k, a in res.items()}
    loss = out[-1][0, 0]
    outs = [loss, dx.reshape(x.shape)]
    for key in ("grad", "delta", "m", "v"):
        outs += [res[key, n] for n in _ALL_NAMES]
    return tuple(outs)
```

```python
import functools

import jax
import jax.numpy as jnp
from jax import lax
from jax.experimental import pallas as pl
from jax.experimental.pallas import tpu as pltpu

F32 = jnp.float32
BF16 = jnp.bfloat16
HI = lax.Precision.HIGHEST
MESH = pl.DeviceIdType.MESH

RMS_EPS = 1e-6
MLA_HEADS = 8
MLA_Q_RANK = 256
MLA_KV_RANK = 128
MLA_NOPE = 64
MLA_ROPE = 32
MLA_V = 64
ROPE_THETA = 10000.0
POOL_WINDOWS = (2, 4, 8, 16)
POOL_GROUP = 128
POOL_WIDTH = 512
POOL_HALO = 16
GDN_HEADS = 8
GDN_DK = 128
CONV_WIDTH = 4
CONV_HALO = 8
CHUNK = 64
IN_AB_PAD = 2048
IN_C_PAD = 4224
ATT_SCALE = (MLA_NOPE + MLA_ROPE) ** -0.5
LOG2E = 1.4426950408889634

ADAM_LR = 0.001
ADAM_B1 = 0.9
ADAM_B2 = 0.999
ADAM_EPS = 1e-08
ADAM_WD = 0.01
ADAM_STEP = 10

LANES = 128
VMEM_LIMIT = 56 * 1024 * 1024

ROW_TILE = 256
ATT_TILE = 1024
GDN_TILE = 256
MM_TILE = (1024, 1408, 2048)

NN = (((1,), (0,)), ((), ()))
NT = (((1,), (1,)), ((), ()))
TN = (((0,), (0,)), ((), ()))


def _dot(a, b, dims=NN, prec=None):
    return lax.dot_general(a, b, dims, precision=prec, preferred_element_type=F32)


def _tile(n, pref):
    if n <= pref:
        return n
    step = LANES if pref >= LANES else 8
    for t in range(pref - pref % step, 0, -step):
        if n % t == 0:
            return t
    return n


def _params(sem):
    return pltpu.CompilerParams(dimension_semantics=sem, vmem_limit_bytes=VMEM_LIMIT)


def _sigmoid(x):
    return 0.5 * jnp.tanh(0.5 * x) + 0.5


def _softplus(x):
    return jnp.maximum(x, 0.0) + jnp.log(1.0 + jnp.exp(-jnp.abs(x)))


def _matmul(a, b, mode, *, name):
    if mode == "nn":
        (m, k), (k2, n) = a.shape, b.shape
    elif mode == "nt":
        (m, k), (n, k2) = a.shape, b.shape
    else:
        (k, m), (k2, n) = a.shape, b.shape
    assert k == k2, (a.shape, b.shape, mode)
    tm, tn, tk = _tile(m, MM_TILE[0]), _tile(n, MM_TILE[1]), _tile(k, MM_TILE[2])
    nk = k // tk
    if mode == "tn":
        a_spec = pl.BlockSpec((tk, tm), lambda i, j, kk: (kk, i))
    else:
        a_spec = pl.BlockSpec((tm, tk), lambda i, j, kk: (i, kk))
    if mode == "nt":
        b_spec = pl.BlockSpec((tn, tk), lambda i, j, kk: (j, kk))
    else:
        b_spec = pl.BlockSpec((tk, tn), lambda i, j, kk: (kk, j))
    o_spec = pl.BlockSpec((tm, tn), lambda i, j, kk: (i, j))
    dims = {"nn": NN, "nt": NT, "tn": TN}[mode]

    def body(a_ref, b_ref, o_ref, *scratch):
        if nk == 1:
            o_ref[...] = _dot(a_ref[...], b_ref[...], dims)
            return
        acc = scratch[0]
        kk = pl.program_id(2)

        @pl.when(kk == 0)
        def _():
            acc[...] = jnp.zeros_like(acc)

        acc[...] += _dot(a_ref[...], b_ref[...], dims)

        @pl.when(kk == nk - 1)
        def _():
            o_ref[...] = acc[...]

    return pl.pallas_call(
        body, name=name, grid=(m // tm, n // tn, nk), in_specs=[a_spec, b_spec], out_specs=o_spec,
        out_shape=jax.ShapeDtypeStruct((m, n), F32),
        scratch_shapes=[pltpu.VMEM((tm, tn), F32)] if nk > 1 else [],
        compiler_params=_params(("parallel", "parallel", "arbitrary")),
    )(a, b)


def _rms_in_proj(h, g, w, *, name):
    t, d = h.shape
    n = w.shape[1]
    tm, tn = _tile(t, MM_TILE[0]), _tile(n, MM_TILE[1])

    def body(h_ref, g_ref, w_ref, o_ref, hn_ref):
        @pl.when(pl.program_id(1) == 0)
        def _():
            x = h_ref[...]
            r = lax.rsqrt(jnp.mean(x * x, axis=-1, keepdims=True) + RMS_EPS)
            hn_ref[...] = (x * r * g_ref[...]).astype(BF16)

        o_ref[...] = _dot(hn_ref[...], w_ref[...])

    return pl.pallas_call(
        body, name=name, grid=(t // tm, n // tn),
        in_specs=[pl.BlockSpec((tm, d), lambda i, j: (i, 0)), pl.BlockSpec((1, d), lambda i, j: (0, 0)),
                  pl.BlockSpec((d, tn), lambda i, j: (0, j))],
        out_specs=[pl.BlockSpec((tm, tn), lambda i, j: (i, j)), pl.BlockSpec((tm, d), lambda i, j: (i, 0))],
        out_shape=[jax.ShapeDtypeStruct((t, n), F32), jax.ShapeDtypeStruct((t, d), BF16)],
        compiler_params=_params(("parallel", "arbitrary")),
    )(h, g, w)


def _matmul_rms_bwd(dproj, w, h, g, dres, *, name, with_bf16):
    t, k = dproj.shape
    d = w.shape[0]
    tm = _tile(t, 2 * ROW_TILE)

    def body(dp_ref, w_ref, h_ref, g_ref, dres_ref, *outs):
        i = pl.program_id(0)
        dh_ref, dg_ref = outs[0], outs[-1]
        dyv = _dot(dp_ref[...], w_ref[...], NT)
        x = h_ref[...]
        r = lax.rsqrt(jnp.mean(x * x, axis=-1, keepdims=True) + RMS_EPS)
        xh = x * r
        dxh = dyv * g_ref[...]
        dh = dres_ref[...] + r * (dxh - xh * jnp.mean(dxh * xh, axis=-1, keepdims=True))
        dh_ref[...] = dh
        if with_bf16:
            outs[1][...] = dh.astype(BF16)

        @pl.when(i == 0)
        def _():
            dg_ref[...] = jnp.zeros_like(dg_ref)

        dg_ref[...] += jnp.sum(dyv * xh, axis=0, keepdims=True)

    row = pl.BlockSpec((tm, d), lambda i: (i, 0))
    vec = pl.BlockSpec((1, d), lambda i: (0, 0))
    out_shape = [jax.ShapeDtypeStruct((t, d), F32)] + ([jax.ShapeDtypeStruct((t, d), BF16)] if with_bf16 else [])
    out_specs = [row] * len(out_shape) + [vec]
    out_shape.append(jax.ShapeDtypeStruct((1, d), F32))
    return pl.pallas_call(
        body, name=name, grid=(t // tm,),
        in_specs=[pl.BlockSpec((tm, k), lambda i: (i, 0)), pl.BlockSpec((d, k), lambda i: (0, 0)), row, vec, row],
        out_specs=out_specs, out_shape=out_shape, compiler_params=_params(("arbitrary",)),
    )(dproj, w, h, g, dres)


def _rope_partner(x):
    lane = lax.broadcasted_iota(jnp.int32, x.shape, 1)
    swapped = jnp.where(lane < MLA_NOPE + MLA_ROPE // 2, pltpu.roll(x, LANES - 16, 1), pltpu.roll(x, 16, 1))
    return jnp.where((lane >= MLA_NOPE) & (lane < MLA_NOPE + MLA_ROPE), swapped, 0.0)


def _pool_counts(row0, tm, w):
    t_idx = row0 + lax.broadcasted_iota(jnp.int32, (tm, POOL_GROUP), 0)
    return jnp.minimum(t_idx + 1, w).astype(F32)


def _ab_prep(proj, pos, inv_freq, q_a_norm, kv_a_norm, wq, wk, wv, wpool, *, name):
    t = proj.shape[0]
    tm = _tile(t, ROW_TILE)
    hb = tm // POOL_HALO

    def body(p_ref, halo_ref, pos_ref, inv_ref, qg_ref, kg_ref, wq_ref, wk_ref, wv_ref, wp_ref,
             q_ref, k_ref, v_ref, yb_ref, qn_ref, kvn_ref, d_ref, cos_ref, sin_ref, ext):
        i = pl.program_id(0)
        ql = p_ref[:, 0:MLA_Q_RANK]
        r = lax.rsqrt(jnp.mean(ql * ql, axis=-1, keepdims=True) + RMS_EPS)
        qn = (ql * r * qg_ref[...]).astype(BF16)
        qn_ref[...] = qn
        kl = p_ref[:, MLA_Q_RANK:MLA_Q_RANK + MLA_KV_RANK]
        r = lax.rsqrt(jnp.mean(kl * kl, axis=-1, keepdims=True) + RMS_EPS)
        kvn = (kl * r * kg_ref[...]).astype(BF16)
        kvn_ref[...] = kvn
        ang = pos_ref[...].astype(F32) * inv_ref[...]
        lane = lax.broadcasted_iota(jnp.int32, (tm, LANES), 1)
        in_rope = (lane >= MLA_NOPE) & (lane < MLA_NOPE + MLA_ROPE)
        cos_t = jnp.where(in_rope, jnp.cos(ang), 1.0)
        sin_t = jnp.where(in_rope, jnp.sin(ang), 0.0)
        sin_t = jnp.where(lane < MLA_NOPE + MLA_ROPE // 2, -sin_t, sin_t)
        cos_ref[...] = cos_t
        sin_ref[...] = sin_t
        kr = p_ref[:, 384:512]
        kr = kr * cos_t + _rope_partner(kr) * sin_t
        qraw = _dot(qn, wq_ref[...])
        kvk = _dot(kvn, wk_ref[...])
        for h in range(MLA_HEADS):
            sl = slice(h * LANES, (h + 1) * LANES)
            qh = qraw[:, sl]
            q_ref[:, sl] = ((qh * cos_t + _rope_partner(qh) * sin_t) * (ATT_SCALE * LOG2E)).astype(BF16)
            k_ref[:, sl] = (kvk[:, sl] + kr).astype(BF16)
        v_ref[...] = _dot(kvn, wv_ref[...]).astype(BF16)
        xp = p_ref[:, 512:1024]
        ext[0:POOL_HALO, :] = jnp.where(i > 0, halo_ref[...], 0.0)
        ext[POOL_HALO:POOL_HALO + tm, :] = xp
        for g, w in enumerate(POOL_WINDOWS):
            lo = g * POOL_GROUP
            acc = ext[POOL_HALO:POOL_HALO + tm, lo:lo + POOL_GROUP]
            for s in range(1, w):
                acc = acc + ext[POOL_HALO - s:POOL_HALO - s + tm, lo:lo + POOL_GROUP]
            cnt = _pool_counts(i * tm, tm, w)
            d_ref[:, lo:lo + POOL_GROUP] = (acc / cnt - xp[:, lo:lo + POOL_GROUP]).astype(BF16)
        yb_ref[...] = _dot(d_ref[...], wp_ref[...])

    row = lambda w: pl.BlockSpec((tm, w), lambda i: (i, 0))
    vec = lambda w: pl.BlockSpec((1, w), lambda i: (0, 0))
    whole = lambda a: pl.BlockSpec(a.shape, lambda i: (0, 0))
    return pl.pallas_call(
        body, name=name, grid=(t // tm,),
        in_specs=[row(1024), pl.BlockSpec((POOL_HALO, POOL_WIDTH), lambda i: (jnp.maximum(i * hb - 1, 0), 1)),
                  pl.BlockSpec((tm, 1), lambda i: (i, 0)), vec(LANES), vec(MLA_Q_RANK), vec(MLA_KV_RANK),
                  whole(wq), whole(wk), whole(wv), whole(wpool)],
        out_specs=[row(1024), row(1024), row(512), row(512), row(MLA_Q_RANK), row(MLA_KV_RANK), row(POOL_WIDTH),
                   row(LANES), row(LANES)],
        out_shape=[jax.ShapeDtypeStruct((t, 1024), BF16), jax.ShapeDtypeStruct((t, 1024), BF16),
                   jax.ShapeDtypeStruct((t, 512), BF16), jax.ShapeDtypeStruct((t, 512), F32),
                   jax.ShapeDtypeStruct((t, MLA_Q_RANK), BF16), jax.ShapeDtypeStruct((t, MLA_KV_RANK), BF16),
                   jax.ShapeDtypeStruct((t, POOL_WIDTH), BF16), jax.ShapeDtypeStruct((t, LANES), F32),
                   jax.ShapeDtypeStruct((t, LANES), F32)],
        scratch_shapes=[pltpu.VMEM((tm + POOL_HALO, POOL_WIDTH), F32)],
        compiler_params=_params(("parallel",)),
    )(proj, proj, pos, inv_freq, q_a_norm, kv_a_norm, wq, wk, wv, wpool)


def _ab_prep_bwd(proj, q_a_norm, kv_a_norm, dq, dk, dv, cos_t, sin_t, dyb, dz, qn, kvn, d, wq, wk, wv, wpool, *, name):
    t = proj.shape[0]
    tm = _tile(t, ROW_TILE)
    hb = tm // POOL_HALO
    last_halo = t // POOL_HALO - 1
    nt = t // tm

    def body(p_ref, qg_ref, kg_ref, dq_ref, dk_ref, dv_ref, c_ref, s_ref, dyb_ref, dybn_ref, dz_ref,
             qn_ref, kvn_ref, d_ref, wq_ref, wk_ref, wv_ref, wp_ref,
             dp_ref, dqg_ref, dkg_ref, dwq_ref, dwk_ref, dwv_ref, dwp_ref, ext, dqr_ref, dkb_ref):
        i = pl.program_id(0)

        @pl.when(i == 0)
        def _():
            for ref in (dqg_ref, dkg_ref, dwq_ref, dwk_ref, dwv_ref, dwp_ref):
                ref[...] = jnp.zeros_like(ref)

        def norm_bwd(x, g, dy, dg_ref):
            r = lax.rsqrt(jnp.mean(x * x, axis=-1, keepdims=True) + RMS_EPS)
            xh = x * r
            dxh = dy * g
            dg_ref[...] += jnp.sum(dy * xh, axis=0, keepdims=True)
            return r * (dxh - xh * jnp.mean(dxh * xh, axis=-1, keepdims=True))

        c, s = c_ref[...], s_ref[...]
        lane = lax.broadcasted_iota(jnp.int32, (tm, LANES), 1)
        in_rope = (lane >= MLA_NOPE) & (lane < MLA_NOPE + MLA_ROPE)
        dkr = jnp.zeros((tm, LANES), F32)
        for h in range(MLA_HEADS):
            sl = slice(h * LANES, (h + 1) * LANES)
            g = dq_ref[:, sl]
            dqr_ref[:, sl] = ((g * c + _rope_partner(g * s)) * ATT_SCALE).astype(BF16)
            gk = dk_ref[:, sl]
            dkb_ref[:, sl] = gk.astype(BF16)
            dkr = dkr + jnp.where(in_rope, gk, 0.0)
        dkr = dkr * c + _rope_partner(dkr * s)
        dqn = _dot(dqr_ref[...], wq_ref[...], NT)
        dkvn = _dot(dkb_ref[...], wk_ref[...], NT) + _dot(dv_ref[...], wv_ref[...], NT)
        dql = norm_bwd(p_ref[:, 0:MLA_Q_RANK], qg_ref[...], dqn, dqg_ref)
        dp_ref[:, 0:MLA_Q_RANK] = dql.astype(BF16)
        dkl = norm_bwd(p_ref[:, MLA_Q_RANK:384], kg_ref[...], dkvn, dkg_ref)
        dp_ref[:, MLA_Q_RANK:384] = dkl.astype(BF16)
        dp_ref[:, 384:512] = dkr.astype(BF16)
        ddv = _dot(dyb_ref[...], wp_ref[...], NT)
        ddn = _dot(dybn_ref[...], wp_ref[...], NT)
        for g, w in enumerate(POOL_WINDOWS):
            lo = g * POOL_GROUP
            ext[0:tm, lo:lo + POOL_GROUP] = ddv[:, lo:lo + POOL_GROUP] / _pool_counts(i * tm, tm, w)
            nxt = ddn[:, lo:lo + POOL_GROUP] / _pool_counts((i + 1) * tm, POOL_HALO, w)
            ext[tm:tm + POOL_HALO, lo:lo + POOL_GROUP] = jnp.where(i < nt - 1, nxt, 0.0)
        for g, w in enumerate(POOL_WINDOWS):
            lo = g * POOL_GROUP
            acc = ext[0:tm, lo:lo + POOL_GROUP]
            for s in range(1, w):
                acc = acc + ext[s:s + tm, lo:lo + POOL_GROUP]
            dp_ref[:, 512 + lo:512 + lo + POOL_GROUP] = (acc - ddv[:, lo:lo + POOL_GROUP]).astype(BF16)
        dp_ref[:, 1024:2048] = dz_ref[...]
        dwq_ref[...] += _dot(qn_ref[...], dqr_ref[...], TN)
        dwk_ref[...] += _dot(kvn_ref[...], dkb_ref[...], TN)
        dwv_ref[...] += _dot(kvn_ref[...], dv_ref[...], TN)
        dwp_ref[...] += _dot(d_ref[...], dyb_ref[...], TN)

    row = lambda w: pl.BlockSpec((tm, w), lambda i: (i, 0))
    vec = lambda w: pl.BlockSpec((1, w), lambda i: (0, 0))
    whole = lambda a: pl.BlockSpec(a.shape, lambda i: (0, 0))
    weights = (wq, wk, wv, wpool)
    return pl.pallas_call(
        body, name=name, grid=(nt,),
        in_specs=[row(1024), vec(MLA_Q_RANK), vec(MLA_KV_RANK), row(1024), row(1024), row(512), row(LANES), row(LANES),
                  row(POOL_WIDTH),
                  pl.BlockSpec((POOL_HALO, POOL_WIDTH), lambda i: (jnp.minimum((i + 1) * hb, last_halo), 0)),
                  row(1024), row(MLA_Q_RANK), row(MLA_KV_RANK), row(POOL_WIDTH)] + [whole(w) for w in weights],
        out_specs=[row(IN_AB_PAD), vec(MLA_Q_RANK), vec(MLA_KV_RANK)] + [whole(w) for w in weights],
        out_shape=[jax.ShapeDtypeStruct((t, IN_AB_PAD), BF16), jax.ShapeDtypeStruct((1, MLA_Q_RANK), F32),
                   jax.ShapeDtypeStruct((1, MLA_KV_RANK), F32)] + [jax.ShapeDtypeStruct(w.shape, F32) for w in weights],
        scratch_shapes=[pltpu.VMEM((tm + POOL_HALO, POOL_WIDTH), F32), pltpu.VMEM((tm, 1024), BF16),
                        pltpu.VMEM((tm, 1024), BF16)],
        compiler_params=_params(("arbitrary",)),
    )(proj, q_a_norm, kv_a_norm, dq, dk, dv, cos_t, sin_t, dyb, dyb, dz, qn, kvn, d, wq, wk, wv, wpool)


def _gate_out_proj(o, ybraw, proj, pool_scale, w, hres, *, name):
    t = o.shape[0]
    tm = _tile(t, 2 * ROW_TILE)

    def body(o_ref, yb_ref, z_ref, ps_ref, w_ref, h_ref, ho_ref, y_ref):
        z = z_ref[...]
        sz = z * _sigmoid(z)
        y_ref[:, 0:512] = (o_ref[...] * sz[:, 0:512]).astype(BF16)
        y_ref[:, 512:1024] = (yb_ref[...] * ps_ref[...] * sz[:, 512:1024]).astype(BF16)
        ho_ref[...] = h_ref[...] + _dot(y_ref[...], w_ref[...])

    row = lambda w_: pl.BlockSpec((tm, w_), lambda i: (i, 0))
    return pl.pallas_call(
        body, name=name, grid=(t // tm,),
        in_specs=[row(512), row(512), pl.BlockSpec((tm, 1024), lambda i: (i, 1)), pl.BlockSpec((1, 512), lambda i: (0, 0)),
                  pl.BlockSpec(w.shape, lambda i: (0, 0)), row(1024)],
        out_specs=[row(1024), row(1024)],
        out_shape=[jax.ShapeDtypeStruct((t, 1024), F32), jax.ShapeDtypeStruct((t, 1024), BF16)],
        compiler_params=_params(("parallel",)),
    )(o, ybraw, proj, pool_scale, w, hres)


def _gate_bwd(dh, w, o, ybraw, proj, pool_scale, *, name):
    t = o.shape[0]
    tm = _tile(t, ROW_TILE)

    def body(dh_ref, w_ref, o_ref, yb_ref, z_ref, ps_ref, do_ref, dl_ref, dyb_ref, dz_ref, dps_ref):
        i = pl.program_id(0)
        z = z_ref[...]
        sg = _sigmoid(z)
        sz = z * sg
        dsz = sg * (1.0 + z * (1.0 - sg))
        dyv = _dot(dh_ref[...], w_ref[...], NT)
        dcat = dyv * sz
        ov = o_ref[...]
        ybs = yb_ref[...] * ps_ref[...]
        dz_ref[:, 0:512] = (dyv[:, 0:512] * ov * dsz[:, 0:512]).astype(BF16)
        dz_ref[:, 512:1024] = (dyv[:, 512:1024] * ybs * dsz[:, 512:1024]).astype(BF16)
        do = dcat[:, 0:512]
        do_ref[...] = do.astype(BF16)
        r_i = (lax.broadcasted_iota(jnp.int32, (1024, 512), 0) % 512) // MLA_V
        c_i = lax.broadcasted_iota(jnp.int32, (1024, 512), 1) // MLA_V
        prod = do * ov
        hi = prod.astype(BF16)
        lo = (prod - hi.astype(F32)).astype(BF16)
        dl_ref[...] = _dot(jnp.concatenate([hi, lo], axis=1), (r_i == c_i).astype(BF16))
        dyb_ref[...] = (dcat[:, 512:1024] * ps_ref[...]).astype(BF16)

        @pl.when(i == 0)
        def _():
            dps_ref[...] = jnp.zeros_like(dps_ref)

        dps_ref[...] += jnp.sum(dcat[:, 512:1024] * yb_ref[...], axis=0, keepdims=True)

    row = lambda w: pl.BlockSpec((tm, w), lambda i: (i, 0))
    vec = pl.BlockSpec((1, 512), lambda i: (0, 0))
    return pl.pallas_call(
        body, name=name, grid=(t // tm,),
        in_specs=[row(1024), pl.BlockSpec(w.shape, lambda i: (0, 0)), row(512), row(512),
                  pl.BlockSpec((tm, 1024), lambda i: (i, 1)), vec],
        out_specs=[row(512), row(512), row(512), row(1024), vec],
        out_shape=[jax.ShapeDtypeStruct((t, 512), BF16), jax.ShapeDtypeStruct((t, 512), F32),
                   jax.ShapeDtypeStruct((t, 512), BF16), jax.ShapeDtypeStruct((t, 1024), BF16),
                   jax.ShapeDtypeStruct((1, 512), F32)],
        compiler_params=_params(("arbitrary",)),
    )(dh, w, o, ybraw, proj, pool_scale)


ATT_HP_FWD = 4
ATT_HP_BWD = 2


def _diag_mask(tq):
    return lax.broadcasted_iota(jnp.int32, (tq, tq), 1) <= lax.broadcasted_iota(jnp.int32, (tq, tq), 0)


def _block_schedule(nq, key_major):
    if key_major:
        pairs = [(qi, ki) for ki in range(nq) for qi in range(ki, nq)]
    else:
        pairs = [(qi, ki) for qi in range(nq) for ki in range(qi + 1)]
    return jnp.asarray([p[0] for p in pairs], jnp.int32), jnp.asarray([p[1] for p in pairs], jnp.int32)


def _attn_fwd(q, k, v, *, name):
    t = q.shape[0]
    tq = _tile(t, ATT_TILE)
    nq = t // tq
    hp = ATT_HP_FWD
    qi_tab, ki_tab = _block_schedule(nq, key_major=False)

    def body(qi_ref, ki_ref, q_ref, k_ref, v_ref, o_ref, lse_ref, m_sc, l_sc, acc_sc):
        step = pl.program_id(1)
        qi, ki = qi_ref[step], ki_ref[step]

        @pl.when(ki == 0)
        def _():
            m_sc[...] = jnp.full_like(m_sc, -jnp.inf)
            l_sc[...] = jnp.zeros_like(l_sc)
            acc_sc[...] = jnp.zeros_like(acc_sc)

        def block(on_diagonal):
            scores = []
            for h in range(hp):
                sl = slice(h * LANES, (h + 1) * LANES)
                scores.append(_dot(q_ref[:, sl], k_ref[:, sl], NT))
            if on_diagonal:
                mask = _diag_mask(tq)
                scores = [jnp.where(mask, s, -jnp.inf) for s in scores]
            for h, s in enumerate(scores):
                vv = v_ref[:, (h // 2) * LANES:(h // 2 + 1) * LANES]
                m_prev = m_sc[h]
                m_new = jnp.maximum(m_prev, jnp.max(s, axis=-1, keepdims=True))
                alpha = jnp.exp2(m_prev - m_new)
                p = jnp.exp2(s - m_new[:, 0:1])
                l_sc[h] = alpha * l_sc[h] + jnp.sum(p, axis=-1, keepdims=True)
                acc_sc[h] = alpha * acc_sc[h] + _dot(p.astype(BF16), vv)
                m_sc[h] = m_new

        pl.when(ki < qi)(functools.partial(block, False))
        pl.when(ki == qi)(functools.partial(block, True))

        @pl.when(ki == qi)
        def _():
            first = lax.broadcasted_iota(jnp.int32, (tq, LANES), 1) < MLA_V
            for pr in range(hp // 2):
                a, b = 2 * pr, 2 * pr + 1
                sl = slice(pr * LANES, (pr + 1) * LANES)
                o_ref[:, sl] = jnp.where(first, acc_sc[a] / l_sc[a], acc_sc[b] / l_sc[b])
                lse_ref[:, sl] = jnp.where(first, m_sc[a] + jnp.log2(l_sc[a]), m_sc[b] + jnp.log2(l_sc[b]))

    grid_spec = pltpu.PrefetchScalarGridSpec(
        num_scalar_prefetch=2, grid=(MLA_HEADS // hp, qi_tab.shape[0]),
        in_specs=[pl.BlockSpec((tq, hp * LANES), lambda g, s, qt, kt: (qt[s], g)),
                  pl.BlockSpec((tq, hp * LANES), lambda g, s, qt, kt: (kt[s], g)),
                  pl.BlockSpec((tq, hp * MLA_V), lambda g, s, qt, kt: (kt[s], g))],
        out_specs=[pl.BlockSpec((tq, hp * MLA_V), lambda g, s, qt, kt: (qt[s], g)),
                   pl.BlockSpec((tq, hp * MLA_V), lambda g, s, qt, kt: (qt[s], g))],
        scratch_shapes=[pltpu.VMEM((hp, tq, LANES), F32)] * 3,
    )
    return pl.pallas_call(
        body, name=name, grid_spec=grid_spec,
        out_shape=[jax.ShapeDtypeStruct((t, 512), F32), jax.ShapeDtypeStruct((t, 512), F32)],
        compiler_params=_params(("parallel", "arbitrary")),
    )(qi_tab, ki_tab, q, k, v)


def _attn_bwd(q, k, v, do, lse, delta, *, name):
    t = q.shape[0]
    tq = _tile(t, ATT_TILE)
    nq = t // tq
    hp = ATT_HP_BWD
    qi_tab, ki_tab = _block_schedule(nq, key_major=True)

    def body(qi_ref, ki_ref, q_ref, k_ref, v_ref, do_ref, lse_ref, dl_ref, dq_ref, dk_ref, dv_ref, dk_sc, dv_sc):
        step = pl.program_id(1)
        qi, ki = qi_ref[step], ki_ref[step]

        @pl.when(step == 0)
        def _():
            dq_ref[...] = jnp.zeros_like(dq_ref)

        @pl.when(qi == ki)
        def _():
            dk_sc[...] = jnp.zeros_like(dk_sc)
            dv_sc[...] = jnp.zeros_like(dv_sc)

        def block(on_diagonal):
            lane = lax.broadcasted_iota(jnp.int32, (tq, LANES), 1)
            rows = pl.ds(pl.multiple_of(qi * tq, tq), tq)
            heads = [slice(h * LANES, (h + 1) * LANES) for h in range(hp)]
            scores = [_dot(q_ref[:, sl], k_ref[:, sl], NT) for sl in heads]
            dps = []
            for h in range(hp):
                dov = do_ref[:, (h // 2) * LANES:(h // 2 + 1) * LANES]
                mine = (lane < MLA_V) if h % 2 == 0 else (lane >= MLA_V)
                dps.append(_dot(jnp.where(mine, dov, jnp.zeros_like(dov)), v_ref[:, (h // 2) * LANES:(h // 2 + 1) * LANES], NT))
            mask = _diag_mask(tq) if on_diagonal else None
            for h, sl in enumerate(heads):
                col = (h // 2) * LANES + (h % 2) * MLA_V
                p = jnp.exp2(scores[h] - lse_ref[:, col:col + 1])
                if on_diagonal:
                    p = jnp.where(mask, p, 0.0)
                ds = (p * (dps[h] - dl_ref[:, col:col + 1])).astype(BF16)
                dv_sc[h] += _dot(p.astype(BF16), do_ref[:, (h // 2) * LANES:(h // 2 + 1) * LANES], TN)
                dk_sc[h] += _dot(ds, q_ref[:, sl], TN)
                dq_ref[rows, sl] += _dot(ds, k_ref[:, sl], NN)

        pl.when(qi > ki)(functools.partial(block, False))
        pl.when(qi == ki)(functools.partial(block, True))

        @pl.when(qi == nq - 1)
        def _():
            first = lax.broadcasted_iota(jnp.int32, (tq, LANES), 1) < MLA_V
            for h in range(hp):
                dk_ref[:, h * LANES:(h + 1) * LANES] = dk_sc[h] * (1.0 / LOG2E)
            for pr in range(hp // 2):
                dv_ref[:, pr * LANES:(pr + 1) * LANES] = jnp.where(first, dv_sc[2 * pr], dv_sc[2 * pr + 1]).astype(BF16)

    qrow = lambda w: pl.BlockSpec((tq, w), lambda g, s, qt, kt: (qt[s], g))
    krow = lambda w: pl.BlockSpec((tq, w), lambda g, s, qt, kt: (kt[s], g))
    grid_spec = pltpu.PrefetchScalarGridSpec(
        num_scalar_prefetch=2, grid=(MLA_HEADS // hp, qi_tab.shape[0]),
        in_specs=[qrow(hp * LANES), krow(hp * LANES), krow(hp * MLA_V), qrow(hp * MLA_V), qrow(hp * MLA_V), qrow(hp * MLA_V)],
        out_specs=[pl.BlockSpec((t, hp * LANES), lambda g, s, qt, kt: (0, g)), krow(hp * LANES), krow(hp * MLA_V)],
        scratch_shapes=[pltpu.VMEM((hp, tq, LANES), F32), pltpu.VMEM((hp, tq, LANES), F32)],
    )
    return pl.pallas_call(
        body, name=name, grid_spec=grid_spec,
        out_shape=[jax.ShapeDtypeStruct((t, 1024), F32), jax.ShapeDtypeStruct((t, 1024), F32),
                   jax.ShapeDtypeStruct((t, 512), BF16)],
        compiler_params=_params(("parallel", "arbitrary")),
    )(qi_tab, ki_tab, q, k, v, do, lse, delta)


def _conv_rows(ext, tm, w_ref, sec):
    c0 = sec * 1024
    y = ext[CONV_HALO - 3:CONV_HALO - 3 + tm, c0:c0 + 1024] * w_ref[0:1, c0:c0 + 1024]
    for j in range(1, CONV_WIDTH):
        y = y + ext[CONV_HALO - 3 + j:CONV_HALO - 3 + j + tm, c0:c0 + 1024] * w_ref[j:j + 1, c0:c0 + 1024]
    return y


def _c_prep(proj_c, conv_w, a_log, dt_bias, *, name):
    t = proj_c.shape[0]
    tm = _tile(t, ROW_TILE)
    hb = tm // CONV_HALO

    def body(p_ref, halo_ref, ab_ref, w_ref, al_ref, dtb_ref, q_ref, k_ref, v_ref, g_ref, b_ref, gt_ref, ext):
        i = pl.program_id(0)
        ext[0:CONV_HALO, :] = jnp.where(i > 0, halo_ref[...], 0.0)
        ext[CONV_HALO:CONV_HALO + tm, :] = p_ref[...]
        for sec, o_ref in enumerate((q_ref, k_ref, v_ref)):
            y = _conv_rows(ext, tm, w_ref, sec)
            y = y * _sigmoid(y)
            if sec == 2:
                o_ref[...] = y
                continue
            scale = GDN_DK ** -0.5 if sec == 0 else 1.0
            for h in range(GDN_HEADS):
                sl = slice(h * LANES, (h + 1) * LANES)
                blk = y[:, sl]
                r = lax.rsqrt(jnp.sum(blk * blk, axis=-1, keepdims=True) + RMS_EPS)
                o_ref[:, sl] = blk * (r * scale)
        ab = ab_ref[...]
        g = -jnp.exp(al_ref[...]) * _softplus(ab + dtb_ref[...])
        beta = _sigmoid(ab)
        ri = lax.broadcasted_iota(jnp.int32, (tm, tm), 0)
        ci = lax.broadcasted_iota(jnp.int32, (tm, tm), 1)
        lower = ((ri // CHUNK) == (ci // CHUNK)) & (ri >= ci)
        gc = _dot(lower.astype(F32), g, NN, HI)
        eye = lax.broadcasted_iota(jnp.int32, (LANES, LANES), 0) == lax.broadcasted_iota(jnp.int32, (LANES, LANES), 1)
        gt_ref[...] = _dot(eye.astype(F32), gc, NT, HI)[0:GDN_HEADS, :]
        for h in range(GDN_HEADS):
            sl = slice(h * LANES, (h + 1) * LANES)
            g_ref[:, sl] = jnp.broadcast_to(gc[:, h:h + 1], (tm, LANES))
            b_ref[:, sl] = jnp.broadcast_to(beta[:, GDN_HEADS + h:GDN_HEADS + h + 1], (tm, LANES))

    row = lambda w: pl.BlockSpec((tm, w), lambda i: (i, 0))
    vec = lambda r, w: pl.BlockSpec((r, w), lambda i: (0, 0))
    out = jax.ShapeDtypeStruct((t, 1024), F32)
    return pl.pallas_call(
        body, name=name, grid=(t // tm,),
        in_specs=[row(3072), pl.BlockSpec((CONV_HALO, 3072), lambda i: (jnp.maximum(i * hb - 1, 0), 0)),
                  pl.BlockSpec((tm, LANES), lambda i: (i, 32)), vec(CONV_WIDTH, 3072), vec(1, LANES), vec(1, LANES)],
        out_specs=[row(1024)] * 5 + [pl.BlockSpec((GDN_HEADS, tm), lambda i: (0, i))],
        out_shape=[out] * 5 + [jax.ShapeDtypeStruct((GDN_HEADS, t), F32)],
        scratch_shapes=[pltpu.VMEM((tm + CONV_HALO, 3072), F32)],
        compiler_params=_params(("parallel",)),
    )(proj_c, proj_c, proj_c, conv_w, a_log, dt_bias)


def _c_prep_bwd(proj_c, conv_w, a_log, dt_bias, dq, dk, dv, dgb, dbb, dz, *, name):
    t = proj_c.shape[0]
    tm = _tile(t, ROW_TILE)
    hb = tm // CONV_HALO
    nt = t // tm
    rev = lambda i: nt - 1 - i

    def body(p_ref, halo_ref, ab_ref, w_ref, al_ref, dtb_ref, dq_ref, dk_ref, dv_ref, dg_ref, db_ref, dz_ref,
             dp_ref, dw_ref, dal_ref, ddt_ref, ext, dyext, carry, taps):
        step = pl.program_id(0)
        i = rev(step)

        @pl.when(step == 0)
        def _():
            dw_ref[...] = jnp.zeros_like(dw_ref)
            dal_ref[...] = jnp.zeros_like(dal_ref)
            ddt_ref[...] = jnp.zeros_like(ddt_ref)
            carry[...] = jnp.zeros_like(carry)

        ext[0:CONV_HALO, :] = jnp.where(i > 0, halo_ref[...], 0.0)
        ext[CONV_HALO:CONV_HALO + tm, :] = p_ref[...]
        for sec, g_ref in enumerate((dq_ref, dk_ref, dv_ref)):
            c0 = sec * 1024
            for j in range(CONV_WIDTH):
                taps[j] = ext[CONV_HALO - 3 + j:CONV_HALO - 3 + j + tm, c0:c0 + 1024]
            y = taps[0] * w_ref[0:1, c0:c0 + 1024]
            for j in range(1, CONV_WIDTH):
                y = y + taps[j] * w_ref[j:j + 1, c0:c0 + 1024]
            sg = _sigmoid(y)
            act = y * sg
            if sec == 2:
                dact = g_ref[...]
            else:
                scale = GDN_DK ** -0.5 if sec == 0 else 1.0
                parts = []
                for h in range(GDN_HEADS):
                    sl = slice(h * LANES, (h + 1) * LANES)
                    blk = act[:, sl]
                    r = lax.rsqrt(jnp.sum(blk * blk, axis=-1, keepdims=True) + RMS_EPS)
                    n = blk * r
                    dn = g_ref[:, sl] * scale
                    parts.append(r * (dn - n * jnp.sum(dn * n, axis=-1, keepdims=True)))
                dact = jnp.concatenate(parts, axis=-1)
            dy = dact * (sg * (1.0 + y * (1.0 - sg)))
            dyext[0:tm, c0:c0 + 1024] = dy
            for j in range(CONV_WIDTH):
                dw_ref[j:j + 1, c0:c0 + 1024] += jnp.sum(dy * taps[j], axis=0, keepdims=True)
        dyext[tm:tm + CONV_HALO, :] = carry[...]
        carry[...] = dyext[0:CONV_HALO, :]
        for sec in range(3):
            c0 = sec * 1024
            dx = dyext[3:3 + tm, c0:c0 + 1024] * w_ref[0:1, c0:c0 + 1024]
            for j in range(1, CONV_WIDTH):
                dx = dx + dyext[3 - j:3 - j + tm, c0:c0 + 1024] * w_ref[j:j + 1, c0:c0 + 1024]
            dp_ref[:, c0:c0 + 1024] = dx.astype(BF16)
        dp_ref[:, 3072:4096] = dz_ref[...]
        lane = lax.broadcasted_iota(jnp.int32, (tm, LANES), 1)
        dg = jnp.zeros((tm, LANES), F32)
        dbeta = jnp.zeros((tm, LANES), F32)
        for h in range(GDN_HEADS):
            sl = slice(h * LANES, (h + 1) * LANES)
            dg = dg + jnp.where(lane == h, dg_ref[:, sl], 0.0)
            dbeta = dbeta + jnp.where(lane == GDN_HEADS + h, db_ref[:, sl], 0.0)
        ri = lax.broadcasted_iota(jnp.int32, (tm, tm), 0)
        ci = lax.broadcasted_iota(jnp.int32, (tm, tm), 1)
        upper = ((ri // CHUNK) == (ci // CHUNK)) & (ri <= ci)
        dg = _dot(upper.astype(F32), dg, NN, HI)
        pre = ab_ref[...] + dtb_ref[...]
        s = _sigmoid(pre)
        a_exp = jnp.exp(al_ref[...])
        dg_da = dg * (-a_exp * s)
        dp_ref[:, 4096:IN_C_PAD] = (dg_da + dbeta * s * (1.0 - s)).astype(BF16)
        dal_ref[...] += jnp.sum(dg * (-a_exp * _softplus(pre)), axis=0, keepdims=True)
        ddt_ref[...] += jnp.sum(dg_da, axis=0, keepdims=True)

    row = lambda w: pl.BlockSpec((tm, w), lambda s: (rev(s), 0))
    vec = lambda r, w: pl.BlockSpec((r, w), lambda s: (0, 0))
    return pl.pallas_call(
        body, name=name, grid=(nt,),
        in_specs=[row(3072), pl.BlockSpec((CONV_HALO, 3072), lambda s: (jnp.maximum(rev(s) * hb - 1, 0), 0)),
                  pl.BlockSpec((tm, LANES), lambda s: (rev(s), 32)), vec(CONV_WIDTH, 3072), vec(1, LANES), vec(1, LANES),
                  row(1024), row(1024), row(1024), row(1024), row(1024), row(1024)],
        out_specs=[row(IN_C_PAD), vec(CONV_WIDTH, 3072), vec(1, LANES), vec(1, LANES)],
        out_shape=[jax.ShapeDtypeStruct((t, IN_C_PAD), BF16), jax.ShapeDtypeStruct((CONV_WIDTH, 3072), F32),
                   jax.ShapeDtypeStruct((1, LANES), F32), jax.ShapeDtypeStruct((1, LANES), F32)],
        scratch_shapes=[pltpu.VMEM((tm + CONV_HALO, 3072), F32), pltpu.VMEM((tm + CONV_HALO, 3072), F32),
                        pltpu.VMEM((CONV_HALO, 3072), F32), pltpu.VMEM((CONV_WIDTH, tm, 1024), F32)],
        compiler_params=_params(("arbitrary",)),
    )(proj_c, proj_c, proj_c, conv_w, a_log, dt_bias, dq, dk, dv, dgb, dbb, dz)


def _o_gate_bwd(dh, w, o, proj_c, o_norm, *, name):
    t = o.shape[0]
    tm = _tile(t, 2 * ROW_TILE)

    def body(dh_ref, w_ref, o_ref, z_ref, g_ref, do_ref, dz_ref, dg_ref, dy_ref):
        i = pl.program_id(0)

        @pl.when(i == 0)
        def _():
            dg_ref[...] = jnp.zeros_like(dg_ref)

        dy_ref[...] = _dot(dh_ref[...], w_ref[...], NT)
        dg = jnp.zeros((1, LANES), F32)
        for h in range(GDN_HEADS):
            sl = slice(h * LANES, (h + 1) * LANES)
            x = o_ref[:, sl]
            r = lax.rsqrt(jnp.mean(x * x, axis=-1, keepdims=True) + RMS_EPS)
            xh = x * r
            z = z_ref[:, sl]
            sg = _sigmoid(z)
            dyv = dy_ref[:, sl]
            dn = dyv * (z * sg)
            dz_ref[:, sl] = (dyv * xh * g_ref[...] * (sg * (1.0 + z * (1.0 - sg)))).astype(BF16)
            dxh = dn * g_ref[...]
            do_ref[:, sl] = r * (dxh - xh * jnp.mean(dxh * xh, axis=-1, keepdims=True))
            dg = dg + jnp.sum(dn * xh, axis=0, keepdims=True)
        dg_ref[...] += dg

    row = pl.BlockSpec((tm, 1024), lambda i: (i, 0))
    vec = pl.BlockSpec((1, LANES), lambda i: (0, 0))
    return pl.pallas_call(
        body, name=name, grid=(t // tm,),
        in_specs=[row, pl.BlockSpec(w.shape, lambda i: (0, 0)), row, pl.BlockSpec((tm, 1024), lambda i: (i, 3)), vec],
        out_specs=[row, row, vec],
        out_shape=[jax.ShapeDtypeStruct((t, 1024), F32), jax.ShapeDtypeStruct((t, 1024), BF16),
                   jax.ShapeDtypeStruct((1, LANES), F32)],
        scratch_shapes=[pltpu.VMEM((tm, 1024), F32)],
        compiler_params=_params(("arbitrary",)),
    )(dh, w, o, proj_c, o_norm)


PAIR = 2 * CHUNK
GDN_HP = 8


def _bdot(a, b, dims=NN):
    return _dot(a.astype(BF16), b.astype(BF16), dims)


def _each(f, *lists):
    return [f(*args) for args in zip(*lists)]


def _pair_common(q, k, v, gci, gcj, beta):
    ri = lax.broadcasted_iota(jnp.int32, (PAIR, PAIR), 0)
    ci = lax.broadcasted_iota(jnp.int32, (PAIR, PAIR), 1)
    same = (ri // CHUNK) == (ci // CHUNK)
    incl = same & (ri >= ci)
    strict = same & (ri > ci)
    eye = (ri == ci).astype(F32)
    first = lax.broadcasted_iota(jnp.int32, (PAIR, LANES), 0) < CHUNK
    gamma = _each(lambda gi, gj: jnp.where(incl, jnp.exp(jnp.minimum(gi - gj, 0.0)), 0.0), gci, gcj)
    kb = _each(jnp.multiply, k, beta)
    kk = _each(lambda a, b: _bdot(a, b, NT), kb, k)
    qk = _each(lambda a, b: _bdot(a, b, NT), q, k)
    m = _each(lambda x, g: jnp.where(strict, x * g, 0.0), kk, gamma)
    tm_ = _each(lambda x: eye - x, m)
    pw = _each(lambda x: _bdot(x, x), m)
    for it in range(5):
        tm_ = _each(lambda x, p: x + _bdot(x, p), tm_, pw)
        if it < 4:
            pw = _each(lambda p: _bdot(p, p), pw)
    eg = _each(jnp.exp, gci)
    vb = _each(jnp.multiply, v, beta)
    kbe = _each(jnp.multiply, kb, eg)
    uw = _each(lambda x, a, b: _bdot(x, jnp.concatenate([a, b], axis=1)), tm_, vb, kbe)
    attn = _each(lambda x, g: jnp.where(incl, x * g, 0.0), qk, gamma)
    gl_a = _each(lambda g: g[CHUNK - 1:CHUNK, :], gci)
    gl_b = _each(lambda g: g[PAIR - 1:PAIR, :], gci)
    ek = _each(lambda a, b, g: jnp.exp(jnp.where(first, a, b) - g), gl_a, gl_b, gci)
    return dict(incl=incl, strict=strict, gamma=gamma, kb=kb, m=m, tm=tm_, eg=eg, vb=vb, kbe=kbe,
                u=_each(lambda x: x[:, :LANES], uw), w=_each(lambda x: x[:, LANES:], uw), attn=attn,
                qd=_each(jnp.multiply, q, eg), ek=ek, kd=_each(jnp.multiply, k, ek),
                glast_a=_each(jnp.exp, gl_a), glast_b=_each(jnp.exp, gl_b))


def _gdn_specs(t, ts, order):
    nc = ts // CHUNK
    blk = pl.BlockSpec((ts, GDN_HP * LANES), lambda h, s: (order(s), h))
    row = pl.BlockSpec((GDN_HP, 1, ts), lambda h, s: (h, 0, order(s)))
    st = pl.BlockSpec((GDN_HP, nc, LANES, LANES), lambda h, s: (h, order(s), 0, 0))
    return blk, row, st


def _gdn_fwd(q, k, v, gcb, gct, bb, *, name):
    t = q.shape[0]
    ts = _tile(t, GDN_TILE)
    npair = ts // PAIR

    def body(q_ref, k_ref, v_ref, g_ref, gt_ref, b_ref, o_ref, st_ref, s_sc):
        @pl.when(pl.program_id(1) == 0)
        def _():
            s_sc[...] = jnp.zeros_like(s_sc)

        def pair(pi, _):
            rows = pl.ds(pl.multiple_of(pi * PAIR, PAIR), PAIR)
            heads = [slice(hh * LANES, (hh + 1) * LANES) for hh in range(GDN_HP)]
            c = CHUNK
            cat0 = lambda *xs: jnp.concatenate(xs, axis=0)
            s0 = [s_sc[hh] for hh in range(GDN_HP)]
            cm = _pair_common([q_ref[rows, sl] for sl in heads], [k_ref[rows, sl] for sl in heads],
                              [v_ref[rows, sl] for sl in heads], [g_ref[rows, sl] for sl in heads],
                              [gt_ref[hh, :, rows] for hh in range(GDN_HP)], [b_ref[rows, sl] for sl in heads])
            u, w, qd, kd = cm["u"], cm["w"], cm["qd"], cm["kd"]
            r0 = _each(lambda w_, q_, s: _bdot(cat0(w_[:c], q_[:c]), s), w, qd, s0)
            vn_a = _each(lambda u_, r: u_[:c] - r[:c], u, r0)
            s1 = _each(lambda s, gl, k_, vn: s * gl + _bdot(k_[:c], vn, TN), s0, cm["glast_a"], kd, vn_a)
            r1 = _each(lambda w_, q_, s: _bdot(cat0(w_[c:], q_[c:]), s), w, qd, s1)
            vn_b = _each(lambda u_, r: u_[c:] - r[:c], u, r1)
            s2 = _each(lambda s, gl, k_, vn: s * gl + _bdot(k_[c:], vn, TN), s1, cm["glast_b"], kd, vn_b)
            o = _each(lambda ra, rb, at, va, vb_: cat0(ra[c:], rb[c:]) + _bdot(at, cat0(va, vb_)),
                      r0, r1, cm["attn"], vn_a, vn_b)
            for hh, sl in enumerate(heads):
                st_ref[hh, 2 * pi] = s0[hh]
                st_ref[hh, 2 * pi + 1] = s1[hh]
                s_sc[hh] = s2[hh]
                o_ref[rows, sl] = o[hh]
            return 0

        lax.fori_loop(0, npair, pair, 0)

    blk, row, st = _gdn_specs(t, ts, lambda s: s)
    return pl.pallas_call(
        body, name=name, grid=(GDN_HEADS // GDN_HP, t // ts), in_specs=[blk, blk, blk, blk, row, blk],
        out_specs=[blk, st],
        out_shape=[jax.ShapeDtypeStruct((t, 1024), F32), jax.ShapeDtypeStruct((GDN_HEADS, t // CHUNK, LANES, LANES), F32)],
        scratch_shapes=[pltpu.VMEM((GDN_HP, LANES, LANES), F32)],
        compiler_params=_params(("parallel", "arbitrary")),
    )(q, k, v, gcb, gct, bb)


def _gdn_bwd(q, k, v, gcb, gct, bb, do, states, *, name):
    t = q.shape[0]
    ts = _tile(t, GDN_TILE)
    npair = ts // PAIR
    ns = t // ts
    c = CHUNK

    def body(q_ref, k_ref, v_ref, g_ref, gt_ref, b_ref, do_ref, st_ref, dq_ref, dk_ref, dv_ref, dg_ref, db_ref, ds_sc):
        @pl.when(pl.program_id(1) == 0)
        def _():
            ds_sc[...] = jnp.zeros_like(ds_sc)

        rowsum = lambda x: jnp.sum(x, axis=-1, keepdims=True)
        total = lambda x: jnp.sum(rowsum(x), axis=0, keepdims=True)
        cat0 = lambda *xs: jnp.concatenate(xs, axis=0)
        cat1 = lambda *xs: jnp.concatenate(xs, axis=1)

        def pair(step, _):
            pi = npair - 1 - step
            rows = pl.ds(pl.multiple_of(pi * PAIR, PAIR), PAIR)
            heads = [slice(hh * LANES, (hh + 1) * LANES) for hh in range(GDN_HP)]
            hs = range(GDN_HP)
            qv, kv, vv = ([r[rows, sl] for sl in heads] for r in (q_ref, k_ref, v_ref))
            beta = [b_ref[rows, sl] for sl in heads]
            dov = [do_ref[rows, sl] for sl in heads]
            s0 = [st_ref[hh, 2 * pi] for hh in hs]
            s1 = [st_ref[hh, 2 * pi + 1] for hh in hs]
            ds2 = [ds_sc[hh] for hh in hs]
            cm = _pair_common(qv, kv, vv, [g_ref[rows, sl] for sl in heads], [gt_ref[hh, :, rows] for hh in hs], beta)
            u, w, qd, kd, attn = cm["u"], cm["w"], cm["qd"], cm["kd"], cm["attn"]
            tmat, gamma, eg = cm["tm"], cm["gamma"], cm["eg"]
            incl, strict = cm["incl"], cm["strict"]
            vn_a = _each(lambda u_, w_, s: u_[:c] - _bdot(w_[:c], s), u, w, s0)
            vn_b = _each(lambda u_, w_, s: u_[c:] - _bdot(w_[c:], s), u, w, s1)
            vn = _each(cat0, vn_a, vn_b)
            dvn_att = _each(lambda a, d: _bdot(a, d, TN), attn, dov)
            dattn = _each(lambda d, v_: jnp.where(incl, _bdot(d, v_, NT), 0.0), dov, vn)
            dvn_b = _each(lambda x, k_, d: x[c:] + _bdot(k_[c:], d), dvn_att, kd, ds2)
            rb = _each(lambda d, x, s: _bdot(cat0(d[c:], x), s, NT), dov, dvn_b, s1)
            dkd_b = _each(lambda v_, d: _bdot(v_, d, NT), vn_b, ds2)
            dgl_b = _each(lambda d, s: total(d * s), ds2, s1)
            ds1 = _each(lambda d, gl, q_, w_, o_, x: d * gl + _bdot(cat0(q_[c:], w_[c:]), cat0(o_[c:], -x), TN),
                        ds2, cm["glast_b"], qd, w, dov, dvn_b)
            dvn_a = _each(lambda x, k_, d: x[:c] + _bdot(k_[:c], d), dvn_att, kd, ds1)
            ra = _each(lambda d, x, s: _bdot(cat0(d[:c], x), s, NT), dov, dvn_a, s0)
            dkd_a = _each(lambda v_, d: _bdot(v_, d, NT), vn_a, ds1)
            dgl_a = _each(lambda d, s: total(d * s), ds1, s0)
            ds0 = _each(lambda d, gl, q_, w_, o_, x: d * gl + _bdot(cat0(q_[:c], w_[:c]), cat0(o_[:c], -x), TN),
                        ds1, cm["glast_a"], qd, w, dov, dvn_a)
            dvn = _each(cat0, dvn_a, dvn_b)
            dqd = _each(lambda a, b: cat0(a[:c], b[:c]), ra, rb)
            dw = _each(lambda a, b: -cat0(a[c:], b[c:]), ra, rb)
            dkd = _each(cat0, dkd_a, dkd_b)
            dvw = _each(cat1, dvn, dw)
            dvbk = _each(lambda t_, x: _bdot(t_, x, TN), tmat, dvw)
            dvb = _each(lambda x: x[:, :LANES], dvbk)
            dkbe = _each(lambda x: x[:, LANES:], dvbk)
            dt_ = _each(lambda x, a, b: _bdot(x, cat1(a, b), NT), dvw, cm["vb"], cm["kbe"])
            da1 = _each(lambda t_, x: _bdot(t_, x, TN), tmat, dt_)
            dm = _each(lambda x, t_: jnp.where(strict, -_bdot(x, t_, NT), 0.0), da1, tmat)
            dkk = _each(jnp.multiply, dm, gamma)
            dqk = _each(jnp.multiply, dattn, gamma)
            z = _each(lambda a, b, c_, d: a * b + c_ * d, dm, cm["m"], dattn, attn)
            dkb = _each(lambda x, k_, y, e: _bdot(x, k_) + y * e, dkk, kv, dkbe, eg)
            dk = _each(lambda a, b, kb_, q_, x, e, y, be: _bdot(cat0(a, b), cat0(kb_, q_), TN) + x * e + y * be,
                       dkk, dqk, cm["kb"], qv, dkd, cm["ek"], dkb, beta)
            dq = _each(lambda x, k_, y, e: _bdot(x, k_) + y * e, dqk, kv, dqd, eg)

            def colsum_of(z_):
                zh = z_.astype(BF16)
                zl = (z_ - zh.astype(F32)).astype(BF16)
                return _dot(cat0(zh, zl), jnp.ones((2 * PAIR, LANES), BF16), TN)

            colsum = _each(colsum_of, z)
            ri = lax.broadcasted_iota(jnp.int32, (PAIR, LANES), 0)
            for hh, sl in enumerate(heads):
                dkd_kd = dkd[hh] * kd[hh]
                dgc = (rowsum(z[hh]) - colsum[hh] + rowsum(dqd[hh] * qd[hh]) - rowsum(dkd_kd)
                       + rowsum(dkbe[hh] * cm["kbe"][hh]))
                last_a = total(dkd_kd[:c]) + dgl_a[hh] * cm["glast_a"][hh]
                last_b = total(dkd_kd[c:]) + dgl_b[hh] * cm["glast_b"][hh]
                dgc = dgc + jnp.where(ri == c - 1, last_a, 0.0) + jnp.where(ri == PAIR - 1, last_b, 0.0)
                ds_sc[hh] = ds0[hh]
                dq_ref[rows, sl] = dq[hh]
                dk_ref[rows, sl] = dk[hh]
                dv_ref[rows, sl] = dvb[hh] * beta[hh]
                db_ref[rows, sl] = jnp.broadcast_to(rowsum(dkb[hh] * kv[hh]) + rowsum(dvb[hh] * vv[hh]), (PAIR, LANES))
                dg_ref[rows, sl] = dgc
            return 0

        lax.fori_loop(0, npair, pair, 0)

    blk, row, st = _gdn_specs(t, ts, lambda s: ns - 1 - s)
    out = jax.ShapeDtypeStruct((t, 1024), F32)
    return pl.pallas_call(
        body, name=name, grid=(GDN_HEADS // GDN_HP, ns), in_specs=[blk, blk, blk, blk, row, blk, blk, st],
        out_specs=[blk] * 5, out_shape=[out] * 5, scratch_shapes=[pltpu.VMEM((GDN_HP, LANES, LANES), F32)],
        compiler_params=_params(("parallel", "arbitrary")),
    )(q, k, v, gcb, gct, bb, do, states)


def _gate_out_proj_loss(o, proj_c, o_norm, w, hres, g, target, *, name):
    t, d = hres.shape
    tm = _tile(t, 2 * ROW_TILE)

    def body(o_ref, z_ref, on_ref, w_ref, h_ref, g_ref, t_ref, dh_ref, dhb_ref, dg_ref, loss_ref, dw_ref, y_ref):
        i = pl.program_id(0)
        for hd in range(GDN_HEADS):
            sl = slice(hd * LANES, (hd + 1) * LANES)
            ov = o_ref[:, sl]
            rr = lax.rsqrt(jnp.mean(ov * ov, axis=-1, keepdims=True) + RMS_EPS)
            z = z_ref[:, sl]
            y_ref[:, sl] = (ov * rr * on_ref[...] * (z * _sigmoid(z))).astype(BF16)
        x = h_ref[...] + _dot(y_ref[...], w_ref[...])
        r = lax.rsqrt(jnp.mean(x * x, axis=-1, keepdims=True) + RMS_EPS)
        xh = x * r
        err = xh * g_ref[...] - t_ref[...]
        dy = err * (1.0 / d)
        dxh = dy * g_ref[...]
        dh = r * (dxh - xh * jnp.mean(dxh * xh, axis=-1, keepdims=True))
        dh_ref[...] = dh
        dhb_ref[...] = dh.astype(BF16)

        @pl.when(i == 0)
        def _():
            dg_ref[...] = jnp.zeros_like(dg_ref)
            loss_ref[...] = jnp.zeros_like(loss_ref)
            dw_ref[...] = jnp.zeros_like(dw_ref)

        dg_ref[...] += jnp.sum(dy * xh, axis=0, keepdims=True)
        part = 0.5 * jnp.sum(jnp.mean(err * err, axis=-1, keepdims=True), axis=0, keepdims=True)
        loss_ref[...] += jnp.broadcast_to(part, loss_ref.shape)
        dw_ref[...] += _dot(y_ref[...], dhb_ref[...], TN)

    row = pl.BlockSpec((tm, d), lambda i: (i, 0))
    vec = pl.BlockSpec((1, d), lambda i: (0, 0))
    return pl.pallas_call(
        body, name=name, grid=(t // tm,),
        in_specs=[row, pl.BlockSpec((tm, 1024), lambda i: (i, 3)), pl.BlockSpec((1, LANES), lambda i: (0, 0)),
                  pl.BlockSpec(w.shape, lambda i: (0, 0)), row, vec, row],
        out_specs=[row, row, vec, pl.BlockSpec((8, LANES), lambda i: (0, 0)), pl.BlockSpec(w.shape, lambda i: (0, 0))],
        out_shape=[jax.ShapeDtypeStruct((t, d), F32), jax.ShapeDtypeStruct((t, d), BF16),
                   jax.ShapeDtypeStruct((1, d), F32), jax.ShapeDtypeStruct((8, LANES), F32),
                   jax.ShapeDtypeStruct(w.shape, F32)],
        scratch_shapes=[pltpu.VMEM((tm, d), BF16)],
        compiler_params=_params(("arbitrary",)),
    )(o, proj_c, o_norm, w, hres, g, target)


def _pad_cols(w, n):
    return jnp.pad(w, ((0, 0), (0, n - w.shape[1])))


def _layout_odd(w):
    return dict(
        winc=_pad_cols(w["w_in_c"], IN_C_PAD).astype(BF16), wout_c=w["w_out_c"].astype(BF16), conv_w=w["conv_w"],
        a_log=_pad_cols(w["a_log"], LANES), dt_bias=_pad_cols(w["dt_bias"], LANES),
        norm_c=w["norm_c"], o_norm=w["o_norm"], final_norm=w["final_norm"],
    )


def _layout_in_ab(w):
    z = lambda r, c: jnp.zeros((r, c), w["w_in_ab"].dtype)
    wi = w["w_in_ab"]
    win = jnp.concatenate([wi[:, :384], z(1024, 64), wi[:, 384:416], z(1024, 32), wi[:, 416:]], axis=1)
    pw = w["pool_w"]
    rows = []
    for g in range(4):
        rows.append(jnp.concatenate([pw[g] if j == g else jnp.zeros((128, 128), F32) for j in range(4)], axis=1))
    wpool = jnp.concatenate(rows, axis=0)
    half = MLA_ROPE // 2
    inv = 1.0 / (ROPE_THETA ** (jnp.arange(half, dtype=F32) / half))
    inv_lane = jnp.concatenate([jnp.zeros((MLA_NOPE,), F32), inv, inv, jnp.zeros((32,), F32)]).reshape(1, LANES)
    return dict(win=win.astype(BF16), wpool=wpool.astype(BF16), inv_lane=inv_lane, norm_ab=w["norm_ab"],
                q_a_norm=w["q_a_norm"], kv_a_norm=w["kv_a_norm"], pool_scale=w["pool_scale"])


def _layout_mid(w):
    wq = jnp.pad(w["w_q_b"].reshape(MLA_Q_RANK, MLA_HEADS, 96), ((0, 0), (0, 0), (0, 32))).reshape(MLA_Q_RANK, 1024)
    kv3 = w["w_kv_b"].reshape(MLA_KV_RANK, MLA_HEADS, 128)
    wk = jnp.pad(kv3[..., :MLA_NOPE], ((0, 0), (0, 0), (0, 64))).reshape(MLA_KV_RANK, 1024)
    wv = kv3[..., MLA_NOPE:].reshape(MLA_KV_RANK, 512)
    return dict(wq=wq.astype(BF16), wk=wk.astype(BF16), wv=wv.astype(BF16), wout_ab=w["w_out_ab"].astype(BF16))


def _unlayout_grads(g, names):
    out = {}
    for name in names:
        if name == "w_in_ab":
            dwin = g["win"]
            out[name] = jnp.concatenate([dwin[:384], dwin[448:480], dwin[512:]], axis=0)
        elif name == "w_q_b":
            out[name] = g["wq"].reshape(MLA_Q_RANK, MLA_HEADS, 128)[..., :96].reshape(MLA_Q_RANK, 768)
        elif name == "w_kv_b":
            out[name] = jnp.concatenate([g["wk"].reshape(MLA_KV_RANK, MLA_HEADS, 128)[..., :MLA_NOPE],
                                         g["wv"].reshape(MLA_KV_RANK, MLA_HEADS, MLA_V)], axis=-1).reshape(MLA_KV_RANK, 1024)
        elif name == "w_in_c":
            out[name] = g["winc"][:4112]
        else:
            out[name] = g[{"w_out_ab": "wout_ab", "w_out_c": "wout_c"}[name]]
    return out


def _local_step(x, pos, target, lw, more_weights=None, on_grads=None):
    mm = _matmul
    proj, hn = _rms_in_proj(x, lw["norm_ab"], lw["win"], name="rms_in_ab")
    if more_weights is not None:
        lw = {**lw, **more_weights("mid", proj)}
    q, k, v, ybraw, qn, kvn, d, cos_t, sin_t = _ab_prep(
        proj, pos, lw["inv_lane"], lw["q_a_norm"], lw["kv_a_norm"], lw["wq"], lw["wk"], lw["wv"], lw["wpool"], name="ab_prep")
    o, lse = _attn_fwd(q, k, v, name="attn_fwd")
    h1, y = _gate_out_proj(o, ybraw, proj, lw["pool_scale"], lw["wout_ab"], x, name="gate_out_ab")
    lo = lw if more_weights is None else more_weights("odd", h1)
    proj_c, hn1 = _rms_in_proj(h1, lo["norm_c"], lo["winc"], name="rms_in_c")
    q2, k2, v2, gb, bb, gt = _c_prep(proj_c, lo["conv_w"], lo["a_log"], lo["dt_bias"], name="c_prep")
    gt = gt.reshape(GDN_HEADS, 1, gt.shape[1])
    o2, states = _gdn_fwd(q2, k2, v2, gb, gt, bb, name="gdn_fwd")
    dh2, dh2b, d_final, loss, d_wout_c = _gate_out_proj_loss(
        o2, proj_c, lo["o_norm"], lo["wout_c"], h1, lo["final_norm"], target, name="gate_out_c_loss")
    g = {"final_norm": d_final, "wout_c": d_wout_c}
    do2, dz2, g["o_norm"] = _o_gate_bwd(dh2b, lo["wout_c"], o2, proj_c, lo["o_norm"], name="gate_c_bwd")
    dq2, dk2, dv2, dgb, dbb = _gdn_bwd(q2, k2, v2, gb, gt, bb, do2, states, name="gdn_bwd")
    dproj_c, g["conv_w"], g["a_log"], g["dt_bias"] = _c_prep_bwd(
        proj_c, lo["conv_w"], lo["a_log"], lo["dt_bias"], dq2, dk2, dv2, dgb, dbb, dz2, name="c_prep_bwd")
    g["winc"] = mm(dproj_c, hn1, "tn", name="in_c_dw")
    notify = (lambda tag: 0.0) if on_grads is None else (lambda tag: on_grads(tag, g))
    pool_scale = lw["pool_scale"] + notify("odd")
    dh1, dh1b, g["norm_c"] = _matmul_rms_bwd(dproj_c, lo["winc"], h1, lo["norm_c"], dh2, name="in_c_dx_rms", with_bf16=True)
    g["wout_ab"] = mm(y, dh1b, "tn", name="out_ab_dw")
    pool_scale = pool_scale + notify("out_ab")
    do, delta, dyb, dz, g["pool_scale"] = _gate_bwd(dh1b, lw["wout_ab"], o, ybraw, proj, pool_scale, name="gate_ab_bwd")
    dq, dk, dv = _attn_bwd(q, k, v, do, lse, delta, name="attn_bwd")
    dproj, g["q_a_norm"], g["kv_a_norm"], g["wq"], g["wk"], g["wv"], g["wpool"] = _ab_prep_bwd(
        proj, lw["q_a_norm"], lw["kv_a_norm"], dq, dk, dv, cos_t, sin_t, dyb, dz, qn, kvn, d,
        lw["wq"], lw["wk"], lw["wv"], lw["wpool"], name="ab_prep_bwd")
    g["win"] = mm(dproj, hn, "tn", name="in_ab_dw")
    norm_ab = lw["norm_ab"] + notify("in_ab")
    dx, g["norm_ab"] = _matmul_rms_bwd(dproj, lw["win"], x, norm_ab, dh1, name="in_ab_dx_rms", with_bf16=False)
    return loss, dx, g


_HBM = pl.BlockSpec(memory_space=pltpu.HBM)


def _place():
    return lax.axis_index("x"), lax.axis_index("y"), lax.axis_index("c")


def _flip(v, f):
    return 1 - v if f else v


_CHIP_FLIPS = ((1, 0), (0, 1), (1, 1))
_DEV_FLIPS = tuple((fx, fy, fc) for fx in (0, 1) for fy in (0, 1) for fc in (0, 1) if fx or fy or fc)


def _rcopy(src, dst, send_sems, recv_sems, k, to):
    return pltpu.make_async_remote_copy(src_ref=src, dst_ref=dst, send_sem=send_sems.at[k], recv_sem=recv_sems.at[k],
                                        device_id=to, device_id_type=MESH)


def _my_half(ref, c, axis):
    rh = ref.shape[axis] // 2
    idx = [slice(None)] * len(ref.shape)
    idx[axis] = pl.ds(c * rh, rh)
    return ref.at[tuple(idx)]


def _gather_weights(bigs, smalls):
    nb, ns = len(bigs), len(smalls)

    def body(*refs):
        ins, outs = refs[:nb + ns], refs[nb + ns:2 * (nb + ns)]
        send_sems, recv_sems, local_sems = refs[2 * (nb + ns):]
        x, y, c = _place()
        j0 = 2 * x + y
        sib = (x, y, 1 - c)
        chips = [(_flip(x, fx), _flip(y, fy)) for fx, fy in _CHIP_FLIPS]
        local = [pltpu.make_async_copy(i_ref, o_ref.at[j0], local_sems.at[a])
                 for a, (i_ref, o_ref) in enumerate(zip(ins, outs))]
        for cp in local:
            cp.start()
        sends = []
        for k, (px, py) in enumerate(chips):
            for a in range(nb):
                sends.append(_rcopy(_my_half(ins[a], c, 0), _my_half(outs[a].at[j0], c, 0), send_sems, recv_sems,
                                    6 * a + k, (px, py, c)))
            for s in range(ns):
                sends.append(_rcopy(ins[nb + s], outs[nb + s].at[j0], send_sems, recv_sems, 6 * nb + 3 * s + k, (px, py, c)))
        for cp in sends:
            cp.start()
        for k, (px, py) in enumerate(chips):
            jk = 2 * px + py
            for a in range(nb):
                landed = _my_half(outs[a].at[jk], c, 0)
                _rcopy(landed, landed, send_sems, recv_sems, 6 * a + k, (px, py, c)).wait_recv()
                fwd = _rcopy(landed, landed, send_sems, recv_sems, 6 * a + 3 + k, sib)
                fwd.start()
                sends.append(fwd)
        for k, (px, py) in enumerate(chips):
            jk = 2 * px + py
            for a in range(nb):
                other = _my_half(outs[a].at[jk], 1 - c, 0)
                _rcopy(other, other, send_sems, recv_sems, 6 * a + 3 + k, sib).wait_recv()
            for s in range(ns):
                _rcopy(ins[nb + s], outs[nb + s].at[jk], send_sems, recv_sems, 6 * nb + 3 * s + k, (px, py, c)).wait_recv()
        for cp in sends:
            cp.wait_send()
        for cp in local:
            cp.wait()

    arrays = list(bigs) + list(smalls)
    n_sem = 6 * nb + 3 * ns
    return pl.pallas_call(
        body, name="gather_weights", in_specs=[_HBM] * len(arrays), out_specs=[_HBM] * len(arrays),
        out_shape=[jax.ShapeDtypeStruct((4,) + a.shape, a.dtype) for a in arrays],
        scratch_shapes=[pltpu.SemaphoreType.DMA((n_sem,)), pltpu.SemaphoreType.DMA((n_sem,)),
                        pltpu.SemaphoreType.DMA((len(arrays),))],
    )(*arrays)


def _core_swap_partial(gs, by_cols, *, name):
    n = len(gs)

    def body(*refs):
        ins, outs = refs[:n], refs[n:2 * n]
        send_sems, recv_sems = refs[2 * n:]
        x, y, c = _place()
        copies = [_rcopy(_my_half(i_ref, 1 - c, 2 if by_cols[a] else 1), o_ref, send_sems, recv_sems, a, (x, y, 1 - c))
                  for a, (i_ref, o_ref) in enumerate(zip(ins, outs))]
        for cp in copies:
            cp.start()
        for cp in copies:
            cp.wait()

    halved = lambda g, cols: (4, g.shape[1], g.shape[2] // 2) if cols else (4, g.shape[1] // 2, g.shape[2])
    return pl.pallas_call(
        body, name=name, in_specs=[_HBM] * n, out_specs=[_HBM] * n,
        out_shape=[jax.ShapeDtypeStruct(halved(g, cols), g.dtype) for g, cols in zip(gs, by_cols)],
        scratch_shapes=[pltpu.SemaphoreType.DMA((n,)), pltpu.SemaphoreType.DMA((n,))],
    )(*gs)


def _core_swap_sum(fs, by_cols):
    n = len(fs)

    def body(*refs):
        ins, outs = refs[:n], refs[n:2 * n]
        send_sems, recv_sems = refs[2 * n:]
        x, y, c = _place()
        axes = [1 if cols else 0 for cols in by_cols]
        copies = [_rcopy(_my_half(i_ref, c, ax), _my_half(o_ref, c, ax), send_sems, recv_sems, a, (x, y, 1 - c))
                  for a, (i_ref, o_ref, ax) in enumerate(zip(ins, outs, axes))]
        for cp in copies:
            cp.start()
        for a, cp in enumerate(copies):
            cp.wait_send()
            theirs = _my_half(outs[a], 1 - c, axes[a])
            _rcopy(theirs, theirs, send_sems, recv_sems, a, (x, y, 1 - c)).wait_recv()

    return pl.pallas_call(
        body, name="core_swap_sum", in_specs=[_HBM] * n, out_specs=[_HBM] * n,
        out_shape=[jax.ShapeDtypeStruct(f.shape, f.dtype) for f in fs],
        input_output_aliases={a: a for a in range(n)},
        scratch_shapes=[pltpu.SemaphoreType.DMA((n,)), pltpu.SemaphoreType.DMA((n,))],
    )(*fs)


_SEM = pl.BlockSpec(memory_space=pltpu.SEMAPHORE)
_ANY = pl.BlockSpec(memory_space=pl.ANY)
_DATAFLOW = pltpu.SideEffectType.DATAFLOW_SIDE_EFFECTING


def _to_chips_copies(srcs, lands, send_sems, recv_sems, per_chip_slot):
    x, y, c = _place()
    j0 = 2 * x + y
    out = []
    for k, (fx, fy) in enumerate(_CHIP_FLIPS):
        px, py = _flip(x, fx), _flip(y, fy)
        jk = 2 * px + py
        for a, (src, land) in enumerate(zip(srcs, lands)):
            piece = src.at[jk] if per_chip_slot else src
            out.append((_rcopy(piece, land.at[j0], send_sems, recv_sems, 3 * a + k, (px, py, c)),
                        _rcopy(piece, land.at[jk], send_sems, recv_sems, 3 * a + k, (px, py, c))))
    return out


def _to_chips_start(arrays, *, per_chip_slot, name, after=None):
    n = len(arrays)
    lands = [lax.empty((4,) + (a.shape[1:] if per_chip_slot else a.shape), a.dtype) for a in arrays]
    extra = [] if after is None else [after]

    def body(*refs):
        srcs, land_refs, token = refs[:n], refs[n:2 * n], refs[-1]
        send_sems, recv_sems = refs[2 * n + len(extra)], refs[2 * n + len(extra) + 1]
        for send, _ in _to_chips_copies(srcs, land_refs, send_sems, recv_sems, per_chip_slot):
            send.start()
        token[...] = jnp.zeros_like(token)

    held = [pltpu.with_memory_space_constraint(a, pltpu.HBM) for a in list(arrays) + lands]
    return pl.pallas_call(
        body, name=name, in_specs=[_HBM] * (2 * n) + [_ANY] * len(extra),
        out_specs=(_SEM, _SEM, *[_HBM] * (2 * n), pl.BlockSpec(memory_space=pltpu.VMEM)),
        out_shape=(pltpu.SemaphoreType.DMA((3 * n,)), pltpu.SemaphoreType.DMA((3 * n,)),
                   *[pltpu.HBM(a.shape, a.dtype) for a in held], jax.ShapeDtypeStruct((8, LANES), F32)),
        input_output_aliases={i: 2 + i for i in range(2 * n)},
        compiler_params=pltpu.CompilerParams(has_side_effects=_DATAFLOW),
    )(*held, *extra)


def _to_chips_wait(started, after, *, per_chip_slot, name):
    send_sems, recv_sems, held = started[0], started[1], started[2:-1]
    n = len(held) // 2

    def body(*refs):
        srcs, land_refs, s_sems, r_sems = refs[:n], refs[n:2 * n], refs[2 * n], refs[2 * n + 1]
        for send, arrival in _to_chips_copies(srcs, land_refs, s_sems, r_sems, per_chip_slot):
            send.wait_send()
            arrival.wait_recv()

    out = pl.pallas_call(
        body, name=name, in_specs=[_HBM] * (2 * n) + [_SEM, _SEM, _ANY], out_specs=[_HBM] * (2 * n),
        out_shape=[pltpu.HBM(a.shape, a.dtype) for a in held],
        input_output_aliases={i: i for i in range(2 * n)},
        compiler_params=pltpu.CompilerParams(has_side_effects=_DATAFLOW),
    )(*held, send_sems, recv_sems, after)
    return out[n:]


def _chip_exchange(ps, small):
    n = len(ps)
    rs = small.shape[0]

    def body(*refs):
        p_refs, s_ref = refs[:n], refs[n]
        l_refs, ls_ref = refs[n + 1:2 * n + 1], refs[2 * n + 1]
        send_sems, recv_sems, local_sems = refs[2 * n + 2:]
        x, y, c = _place()
        j0 = 2 * x + y
        d0 = 2 * j0 + c
        local = [pltpu.make_async_copy(p.at[j0], l.at[j0], local_sems.at[a]) for a, (p, l) in enumerate(zip(p_refs, l_refs))]
        local.append(pltpu.make_async_copy(s_ref, ls_ref.at[d0], local_sems.at[n]))
        for cp in local:
            cp.start()
        sends = []
        for k, (fx, fy) in enumerate(_CHIP_FLIPS):
            px, py = _flip(x, fx), _flip(y, fy)
            for a in range(n):
                sends.append(_rcopy(p_refs[a].at[2 * px + py], l_refs[a].at[j0], send_sems, recv_sems, 3 * a + k, (px, py, c)))
        for k, (fx, fy, fc) in enumerate(_DEV_FLIPS):
            peer = (_flip(x, fx), _flip(y, fy), _flip(c, fc))
            sends.append(_rcopy(s_ref, ls_ref.at[d0], send_sems, recv_sems, 3 * n + k, peer))
        for cp in sends:
            cp.start()
        for k, (fx, fy) in enumerate(_CHIP_FLIPS):
            px, py = _flip(x, fx), _flip(y, fy)
            for a in range(n):
                _rcopy(p_refs[a].at[j0], l_refs[a].at[2 * px + py], send_sems, recv_sems, 3 * a + k, (px, py, c)).wait_recv()
        for k, (fx, fy, fc) in enumerate(_DEV_FLIPS):
            px, py, pc = _flip(x, fx), _flip(y, fy), _flip(c, fc)
            _rcopy(s_ref, ls_ref.at[4 * px + 2 * py + pc], send_sems, recv_sems, 3 * n + k, (px, py, pc)).wait_recv()
        for cp in sends:
            cp.wait_send()
        for cp in local:
            cp.wait()

    n_sem = 3 * n + 7
    return pl.pallas_call(
        body, name="chip_exchange", in_specs=[_HBM] * (n + 1), out_specs=[_HBM] * (n + 1),
        out_shape=[jax.ShapeDtypeStruct(p.shape, F32) for p in ps] + [jax.ShapeDtypeStruct((8, rs, LANES), F32)],
        scratch_shapes=[pltpu.SemaphoreType.DMA((n_sem,)), pltpu.SemaphoreType.DMA((n_sem,)),
                        pltpu.SemaphoreType.DMA((n + 1,))],
    )(*ps, small)


def _half_blocks(rows, cols, by_cols):
    if by_cols:
        tc = _tile(cols // 2, 256)
        nb = cols // 2 // tc
        return rows, tc, nb, (lambda i, c: (0, c * nb + i))
    tr = _tile(rows // 2, 256)
    nb = rows // 2 // tr
    return tr, cols, nb, (lambda i, c: (c * nb + i, 0))


def _core_sum(g, part, core, *, name, by_cols):
    _, rows, cols = g.shape
    br, bc, nb, whole = _half_blocks(rows, cols, by_cols)
    mine = (lambda i: (0, i)) if by_cols else (lambda i: (i, 0))

    def body(c_ref, g_ref, p_ref, o_ref):
        o_ref[...] = g_ref[...] + p_ref[...]

    grid_spec = pltpu.PrefetchScalarGridSpec(
        num_scalar_prefetch=1, grid=(4, nb),
        in_specs=[pl.BlockSpec((1, br, bc), lambda j, i, c: (j,) + whole(i, c[0])),
                  pl.BlockSpec((1, br, bc), lambda j, i, c: (j,) + mine(i))],
        out_specs=pl.BlockSpec((1, br, bc), lambda j, i, c: (j,) + mine(i)),
    )
    return pl.pallas_call(
        body, name=name, grid_spec=grid_spec, out_shape=jax.ShapeDtypeStruct(part.shape, F32),
        compiler_params=_params(("parallel", "parallel")),
    )(core, g, part)


def _chip_sum(landed, core, *, name, by_cols):
    _, hr, hc = landed.shape
    rows, cols = (hr, 2 * hc) if by_cols else (2 * hr, hc)
    br, bc, nb, whole = _half_blocks(rows, cols, by_cols)
    mine = (lambda i: (0, i)) if by_cols else (lambda i: (i, 0))

    def body(c_ref, l_ref, o_ref):
        o_ref[...] = ((l_ref[0] + l_ref[1]) + l_ref[2]) + l_ref[3]

    grid_spec = pltpu.PrefetchScalarGridSpec(
        num_scalar_prefetch=1, grid=(nb,),
        in_specs=[pl.BlockSpec((4, br, bc), lambda i, c: (0,) + mine(i))],
        out_specs=pl.BlockSpec((br, bc), lambda i, c: whole(i, c[0])),
    )
    return pl.pallas_call(
        body, name=name, grid_spec=grid_spec, out_shape=jax.ShapeDtypeStruct((rows, cols), F32),
        compiler_params=_params(("parallel",)),
    )(core, landed)


_ROW_POOL_W, _ROW_NORM_AB, _ROW_FINAL, _ROW_POOL_SCALE, _ROW_Q_NORM = 0, 512, 520, 528, 532
_ROW_KV_NORM, _ROW_O_NORM, _ROW_A_LOG, _ROW_DT_BIAS, _ROW_LOSS = 534, 535, 536, 537, 538
_ROW_CONV, _ROW_NORM_C, _SMALL_ROWS = 544, 640, 672
_CONV_ROWS = CONV_WIDTH * 6


def _put_rows(dst_ref, row0, src, width):
    for r in range(width // LANES):
        dst_ref[row0 + r:row0 + r + 1, :] = src[:, r * LANES:(r + 1) * LANES]


def _pack_small(g, loss_tile):
    names = ("wpool", "norm_ab", "final_norm", "pool_scale", "q_a_norm", "kv_a_norm", "o_norm", "a_log", "dt_bias",
             "conv_w", "norm_c")

    def body(wpool, norm_ab, final_norm, pool_scale, q_norm, kv_norm, o_norm, a_log, dt_bias, conv_w, norm_c, loss, o_ref):
        o_ref[...] = jnp.zeros_like(o_ref)
        for gi in range(4):
            o_ref[_ROW_POOL_W + gi * 128:_ROW_POOL_W + (gi + 1) * 128, :] = wpool[gi * 128:(gi + 1) * 128, gi * 128:(gi + 1) * 128]
        _put_rows(o_ref, _ROW_NORM_AB, norm_ab[...], 1024)
        _put_rows(o_ref, _ROW_FINAL, final_norm[...], 1024)
        _put_rows(o_ref, _ROW_POOL_SCALE, pool_scale[...], 512)
        _put_rows(o_ref, _ROW_Q_NORM, q_norm[...], 256)
        for row, ref in ((_ROW_KV_NORM, kv_norm), (_ROW_O_NORM, o_norm), (_ROW_A_LOG, a_log), (_ROW_DT_BIAS, dt_bias)):
            o_ref[row:row + 1, :] = ref[...]
        o_ref[_ROW_LOSS:_ROW_LOSS + 1, :] = loss[0:1, :]
        for j in range(4):
            for r in range(CONV_WIDTH):
                _put_rows(o_ref, _ROW_CONV + j * _CONV_ROWS + r * 6, conv_w[r:r + 1, j * 768:(j + 1) * 768], 768)
            _put_rows(o_ref, _ROW_NORM_C + j * 8, norm_c[:, j * 256:(j + 1) * 256], 256)

    vmem = pl.BlockSpec(memory_space=pltpu.VMEM)
    return pl.pallas_call(
        body, name="pack_small", in_specs=[vmem] * 12, out_specs=vmem,
        out_shape=jax.ShapeDtypeStruct((_SMALL_ROWS, LANES), F32),
    )(*[g[n] for n in names], loss_tile)


_SMALL_NAMES = ("pool_w", "norm_ab", "final_norm", "pool_scale", "q_a_norm", "kv_a_norm", "o_norm", "a_log", "dt_bias",
                "conv_w", "norm_c")


def _take_rows(src, row0, width):
    return jnp.concatenate([src[row0 + r:row0 + r + 1, :] for r in range(width // LANES)], axis=1)


def _small_update(small_all, ws, ms, vs):
    n = len(_SMALL_NAMES)

    def body(*refs):
        a_ref = refs[0]
        w_refs, m_refs, v_refs = refs[1:1 + n], refs[1 + n:1 + 2 * n], refs[1 + 2 * n:1 + 3 * n]
        outs = refs[1 + 3 * n:1 + 7 * n]
        loss_ref, tot = refs[1 + 7 * n], refs[2 + 7 * n]
        acc = a_ref[0]
        for d in range(1, 8):
            acc = acc + a_ref[d]
        tot[...] = acc
        x, y, _ = _place()
        j0 = 2 * x + y
        conv = tot[pl.ds(pl.multiple_of(_ROW_CONV + j0 * _CONV_ROWS, 8), _CONV_ROWS), :]
        norm_c = tot[pl.ds(pl.multiple_of(_ROW_NORM_C + j0 * 8, 8), 8), :]
        whole = tot[_ROW_NORM_AB:_ROW_CONV, :]
        at = lambda row: row - _ROW_NORM_AB
        grads = {
            "norm_ab": _take_rows(whole, at(_ROW_NORM_AB), 1024), "final_norm": _take_rows(whole, at(_ROW_FINAL), 1024),
            "pool_scale": _take_rows(whole, at(_ROW_POOL_SCALE), 512), "q_a_norm": _take_rows(whole, at(_ROW_Q_NORM), 256),
            "kv_a_norm": whole[at(_ROW_KV_NORM):at(_ROW_KV_NORM) + 1, :], "o_norm": whole[at(_ROW_O_NORM):at(_ROW_O_NORM) + 1, :],
            "a_log": tot[_ROW_A_LOG:_ROW_A_LOG + 1, 0:GDN_HEADS],
            "dt_bias": tot[_ROW_DT_BIAS:_ROW_DT_BIAS + 1, 0:GDN_HEADS],
            "norm_c": _take_rows(norm_c, 0, 256),
        }
        loss_ref[...] = whole[at(_ROW_LOSS):at(_ROW_LOSS) + 1, :]
        for i, name in enumerate(_SMALL_NAMES):
            g_out = outs[4 * i]
            if name == "pool_w":
                for gi in range(4):
                    g_out[gi] = tot[_ROW_POOL_W + gi * 128:_ROW_POOL_W + (gi + 1) * 128, :]
            elif name == "conv_w":
                for r in range(CONV_WIDTH):
                    g_out[r:r + 1, :] = _take_rows(conv, r * 6, 768)
            else:
                g_out[...] = grads[name]
            _adam_update(g_out, w_refs[i], m_refs[i], v_refs[i], *outs[4 * i + 1:4 * i + 4])

    vmem = pl.BlockSpec(memory_space=pltpu.VMEM)
    out_shape = [jax.ShapeDtypeStruct(w.shape, F32) for w in ws for _ in range(4)] + [jax.ShapeDtypeStruct((1, LANES), F32)]
    return pl.pallas_call(
        body, name="small_update", in_specs=[vmem] * (1 + 3 * n), out_specs=[vmem] * (4 * n + 1), out_shape=out_shape,
        scratch_shapes=[pltpu.VMEM((_SMALL_ROWS, LANES), F32)],
        compiler_params=pltpu.CompilerParams(vmem_limit_bytes=VMEM_LIMIT),
    )(small_all, *ws, *ms, *vs)


def _adam_update(g_ref, w_ref, m_ref, v_ref, d_ref, mo_ref, vo_ref):
    gv = g_ref[...]
    mn = ADAM_B1 * m_ref[...] + (1.0 - ADAM_B1) * gv
    vn = ADAM_B2 * v_ref[...] + (1.0 - ADAM_B2) * (gv * gv)
    mo_ref[...] = mn
    vo_ref[...] = vn
    c1 = 1.0 - ADAM_B1 ** ADAM_STEP
    c2 = 1.0 - ADAM_B2 ** ADAM_STEP
    d_ref[...] = -ADAM_LR * ((mn / c1) / (jnp.sqrt(vn / c2) + ADAM_EPS) + ADAM_WD * w_ref[...])


def _adamw_rows(g, w, m, v, *, name):
    rows, cols = g.shape
    if rows % LANES == 0:
        tr = _tile(rows, 512)
        blk, steps = pl.BlockSpec((tr, cols), lambda i: (i, 0)), rows // tr
    else:
        tc = _tile(cols, 256)
        blk, steps = pl.BlockSpec((rows, tc), lambda i: (0, i)), cols // tc

    def body(*refs):
        _adam_update(*refs)

    out = jax.ShapeDtypeStruct((rows, cols), F32)
    return pl.pallas_call(
        body, name=name, grid=(steps,), in_specs=[blk] * 4, out_specs=[blk] * 3, out_shape=[out] * 3,
        compiler_params=_params(("parallel",)),
    )(g, w, m, v)


_ADAM_ROWWISE = ("w_in_ab", "w_q_b", "w_kv_b", "w_out_ab", "w_in_c", "w_out_c")


_SHARD_AXIS = {"w_in_ab": 1, "w_q_b": 1, "w_kv_b": 1, "w_out_ab": 0, "w_in_c": 1, "w_out_c": 0, "conv_w": 1, "norm_c": 1}
_ALL_NAMES = ("norm_ab", "w_in_ab", "q_a_norm", "w_q_b", "kv_a_norm", "w_kv_b", "pool_w", "pool_scale", "w_out_ab",
              "norm_c", "w_in_c", "conv_w", "a_log", "dt_bias", "o_norm", "w_out_c", "final_norm")


def _join_shards(a, axis):
    _, r, c = a.shape
    return a.reshape(4 * r, c) if axis == 0 else jnp.transpose(a, (1, 0, 2)).reshape(r, 4 * c)


def _split_shards(a, axis):
    r, c = a.shape
    return a.reshape(4, r // 4, c) if axis == 0 else jnp.transpose(a.reshape(r, 4, c // 4), (1, 0, 2))


def kernel(x, positions, norm_ab, w_in_ab, q_a_norm, w_q_b, kv_a_norm, w_kv_b, pool_w, pool_scale, w_out_ab, norm_c, w_in_c, conv_w, a_log, dt_bias, o_norm, w_out_c, final_norm, loss_target, m_norm_ab, m_w_in_ab, m_q_a_norm, m_w_q_b, m_kv_a_norm, m_w_kv_b, m_pool_w, m_pool_scale, m_w_out_ab, m_norm_c, m_w_in_c, m_conv_w, m_a_log, m_dt_bias, m_o_norm, m_w_out_c, m_final_norm, v_norm_ab, v_w_in_ab, v_q_a_norm, v_w_q_b, v_kv_a_norm, v_w_kv_b, v_pool_w, v_pool_scale, v_w_out_ab, v_norm_c, v_w_in_c, v_conv_w, v_a_log, v_dt_bias, v_o_norm, v_w_out_c, v_final_norm):
    given = dict(locals())
    c = lax.axis_index("c")
    t = x.shape[1]

    def shard_of(prefix, name):
        a = given[prefix + name]
        return a.reshape(a.shape[1:]) if a.ndim > 2 else a.reshape(1, -1)

    big, big_even, big_odd, small_sharded = _ADAM_ROWWISE, _ADAM_ROWWISE[:4], _ADAM_ROWWISE[4:], ("conv_w", "norm_c")
    chip = 2 * lax.axis_index("x") + lax.axis_index("y")
    core = c.astype(jnp.int32).reshape(1)
    later = {"mid": big_even[1:], "odd": big_odd + small_sharded}
    travelling = {}

    def send(tag, after=None):
        shards = [shard_of("", n).astype(BF16) if n in big else shard_of("", n) for n in later[tag]]
        started = _to_chips_start(shards, per_chip_slot=False, name="gather_" + tag + "_start", after=after)
        travelling[tag] = (shards, started)
        return started[-1][0, 0]

    mid_sent = send("mid")
    gathered = _gather_weights([shard_of("", "w_in_ab").astype(BF16)], [])
    full = {"w_in_ab": _join_shards(gathered[0], _SHARD_AXIS["w_in_ab"])}
    for name in ("norm_ab", "q_a_norm", "kv_a_norm", "pool_w", "pool_scale"):
        full[name] = shard_of("", name)
    lw = _layout_in_ab(full)
    lw["norm_ab"] = lw["norm_ab"] + mid_sent

    def more_weights(tag, after):
        shards, started = travelling[tag]
        landed = _to_chips_wait(started, after, per_chip_slot=False, name="gather_" + tag + "_wait")
        w = {}
        for name, land, own in zip(later[tag], landed, shards):
            w[name] = _join_shards(lax.dynamic_update_index_in_dim(land, own, chip, 0), _SHARD_AXIS[name])
        if tag == "mid":
            out = _layout_mid(w)
            out["wq"] = out["wq"] + send("odd", after=landed[0]).astype(BF16)
            return out
        for name in ("a_log", "dt_bias", "o_norm", "final_norm"):
            w[name] = shard_of("", name)
        return _layout_odd(w)

    transposed = ("w_in_ab", "w_in_c")

    def chip_partials(names, grads, tag):
        by_cols = [n in transposed for n in names]
        slots = [_split_shards(grads[n], 0 if n in transposed else _SHARD_AXIS[n]) for n in names]
        partial = _core_swap_partial(slots, by_cols, name="core_swap_partial_" + tag)
        return [_core_sum(s, p, core, name="core_sum_" + n, by_cols=b) for n, s, p, b in zip(names, slots, partial, by_cols)]

    groups = {"odd": big_odd, "out_ab": ("w_out_ab",), "in_ab": ("w_in_ab", "w_q_b", "w_kv_b")}
    sent = {}

    def on_grads(tag, g):
        part = chip_partials(groups[tag], _unlayout_grads(g, groups[tag]), tag)
        sent[tag] = (part, _to_chips_start(part, per_chip_slot=True, name="exchange_" + tag + "_start"))
        return sent[tag][1][-1][0, 0]

    loss_tile, dx, g = _local_step(x[0], positions.reshape(t, 1), loss_target[0], lw, more_weights, on_grads)
    small_all = _chip_exchange([], _pack_small(g, loss_tile))[-1]
    halves = {}
    for tag, names in groups.items():
        part, started = sent[tag]
        landed = _to_chips_wait(started, small_all, per_chip_slot=True, name="exchange_" + tag + "_wait")
        for n, l, p in zip(names, landed, part):
            l = lax.dynamic_update_index_in_dim(l, lax.dynamic_index_in_dim(p, chip, 0, keepdims=False), chip, 0)
            halves[n] = _chip_sum(l, core, name="chip_sum_" + n, by_cols=n in transposed)
    gbig = dict(zip(big, _core_swap_sum([halves[n] for n in big], [n in transposed for n in big])))

    res = {}
    for name in big:
        operands = [gbig[name], shard_of("", name), shard_of("m_", name), shard_of("v_", name)]
        flip = name in transposed
        if flip:
            operands[1:] = [jnp.transpose(a) for a in operands[1:]]
        out = (operands[0],) + tuple(_adamw_rows(*operands, name="adamw_" + name))
        out = [jnp.transpose(a) for a in out] if flip else out
        res["grad", name], res["delta", name], res["m", name], res["v", name] = out
    out = _small_update(small_all, [shard_of("", n) for n in _SMALL_NAMES], [shard_of("m_", n) for n in _SMALL_NAMES],
                        [shard_of("v_", n) for n in _SMALL_NAMES])
    for i, name in enumerate(_SMALL_NAMES):
        res["grad", name], res["delta", name], res["m", name], res["v", name] = out[4 * i:4 * i + 4]
    res = {k: a.reshape(given[k[1]].shape) for k, a in res.items()}
    loss = out[-1][0, 0]
    outs = [loss, dx.reshape(x.shape)]
    for key in ("grad", "delta", "m", "v"):
        outs += [res[key, n] for n in _ALL_NAMES]
    return tuple(outs)
```

```python
import functools

import jax
import jax.numpy as jnp
from jax import lax
from jax.experimental import pallas as pl
from jax.experimental.pallas import tpu as pltpu

F32 = jnp.float32
BF16 = jnp.bfloat16
HI = lax.Precision.HIGHEST
MESH = pl.DeviceIdType.MESH

RMS_EPS = 1e-6
MLA_HEADS = 8
MLA_Q_RANK = 256
MLA_KV_RANK = 128
MLA_NOPE = 64
MLA_ROPE = 32
MLA_V = 64
ROPE_THETA = 10000.0
POOL_WINDOWS = (2, 4, 8, 16)
POOL_GROUP = 128
POOL_WIDTH = 512
POOL_HALO = 16
GDN_HEADS = 8
GDN_DK = 128
CONV_WIDTH = 4
CONV_HALO = 8
CHUNK = 64
IN_AB_PAD = 2048
IN_C_PAD = 4224
ATT_SCALE = (MLA_NOPE + MLA_ROPE) ** -0.5
LOG2E = 1.4426950408889634

ADAM_LR = 0.001
ADAM_B1 = 0.9
ADAM_B2 = 0.999
ADAM_EPS = 1e-08
ADAM_WD = 0.01
ADAM_STEP = 10

LANES = 128
VMEM_LIMIT = 56 * 1024 * 1024

ROW_TILE = 256
ATT_TILE = 1024
GDN_TILE = 256
MM_TILE = (1024, 1408, 2048)

NN = (((1,), (0,)), ((), ()))
NT = (((1,), (1,)), ((), ()))
TN = (((0,), (0,)), ((), ()))


def _dot(a, b, dims=NN, prec=None):
    return lax.dot_general(a, b, dims, precision=prec, preferred_element_type=F32)


def _tile(n, pref):
    if n <= pref:
        return n
    step = LANES if pref >= LANES else 8
    for t in range(pref - pref % step, 0, -step):
        if n % t == 0:
            return t
    return n


def _params(sem):
    return pltpu.CompilerParams(dimension_semantics=sem, vmem_limit_bytes=VMEM_LIMIT)


def _sigmoid(x):
    return 0.5 * jnp.tanh(0.5 * x) + 0.5


def _softplus(x):
    return jnp.maximum(x, 0.0) + jnp.log(1.0 + jnp.exp(-jnp.abs(x)))


def _matmul(a, b, mode, *, name):
    if mode == "nn":
        (m, k), (k2, n) = a.shape, b.shape
    elif mode == "nt":
        (m, k), (n, k2) = a.shape, b.shape
    else:
        (k, m), (k2, n) = a.shape, b.shape
    assert k == k2, (a.shape, b.shape, mode)
    tm, tn, tk = _tile(m, MM_TILE[0]), _tile(n, MM_TILE[1]), _tile(k, MM_TILE[2])
    nk = k // tk
    if mode == "tn":
        a_spec = pl.BlockSpec((tk, tm), lambda i, j, kk: (kk, i))
    else:
        a_spec = pl.BlockSpec((tm, tk), lambda i, j, kk: (i, kk))
    if mode == "nt":
        b_spec = pl.BlockSpec((tn, tk), lambda i, j, kk: (j, kk))
    else:
        b_spec = pl.BlockSpec((tk, tn), lambda i, j, kk: (kk, j))
    o_spec = pl.BlockSpec((tm, tn), lambda i, j, kk: (i, j))
    dims = {"nn": NN, "nt": NT, "tn": TN}[mode]

    def body(a_ref, b_ref, o_ref, *scratch):
        if nk == 1:
            o_ref[...] = _dot(a_ref[...], b_ref[...], dims)
            return
        acc = scratch[0]
        kk = pl.program_id(2)

        @pl.when(kk == 0)
        def _():
            acc[...] = jnp.zeros_like(acc)

        acc[...] += _dot(a_ref[...], b_ref[...], dims)

        @pl.when(kk == nk - 1)
        def _():
            o_ref[...] = acc[...]

    return pl.pallas_call(
        body, name=name, grid=(m // tm, n // tn, nk), in_specs=[a_spec, b_spec], out_specs=o_spec,
        out_shape=jax.ShapeDtypeStruct((m, n), F32),
        scratch_shapes=[pltpu.VMEM((tm, tn), F32)] if nk > 1 else [],
        compiler_params=_params(("parallel", "parallel", "arbitrary")),
    )(a, b)


def _rms_in_proj(h, g, w, *, name):
    t, d = h.shape
    n = w.shape[1]
    tm, tn = _tile(t, MM_TILE[0]), _tile(n, MM_TILE[1])

    def body(h_ref, g_ref, w_ref, o_ref, hn_ref):
        @pl.when(pl.program_id(1) == 0)
        def _():
            x = h_ref[...]
            r = lax.rsqrt(jnp.mean(x * x, axis=-1, keepdims=True) + RMS_EPS)
            hn_ref[...] = (x * r * g_ref[...]).astype(BF16)

        o_ref[...] = _dot(hn_ref[...], w_ref[...])

    return pl.pallas_call(
        body, name=name, grid=(t // tm, n // tn),
        in_specs=[pl.BlockSpec((tm, d), lambda i, j: (i, 0)), pl.BlockSpec((1, d), lambda i, j: (0, 0)),
                  pl.BlockSpec((d, tn), lambda i, j: (0, j))],
        out_specs=[pl.BlockSpec((tm, tn), lambda i, j: (i, j)), pl.BlockSpec((tm, d), lambda i, j: (i, 0))],
        out_shape=[jax.ShapeDtypeStruct((t, n), F32), jax.ShapeDtypeStruct((t, d), BF16)],
        compiler_params=_params(("parallel", "arbitrary")),
    )(h, g, w)


def _matmul_rms_bwd(dproj, w, h, g, dres, *, name, prev_y=None):
    t, k = dproj.shape
    d = w.shape[0]
    tm = _tile(t, 2 * ROW_TILE)
    chained = prev_y is not None

    def body(dp_ref, w_ref, h_ref, g_ref, dres_ref, *rest):
        i = pl.program_id(0)
        dh_ref, dg_ref = rest[-4:-2] if chained else rest
        dyv = _dot(dp_ref[...], w_ref[...], NT)
        x = h_ref[...]
        r = lax.rsqrt(jnp.mean(x * x, axis=-1, keepdims=True) + RMS_EPS)
        xh = x * r
        dxh = dyv * g_ref[...]
        dh = dres_ref[...] + r * (dxh - xh * jnp.mean(dxh * xh, axis=-1, keepdims=True))
        dh_ref[...] = dh

        @pl.when(i == 0)
        def _():
            dg_ref[...] = jnp.zeros_like(dg_ref)
            if chained:
                rest[-1][...] = jnp.zeros_like(rest[-1])

        dg_ref[...] += jnp.sum(dyv * xh, axis=0, keepdims=True)
        if chained:
            y_ref, dhb_ref, dw_ref = rest[0], rest[-2], rest[-1]
            dhb_ref[...] = dh.astype(BF16)
            dw_ref[...] += _dot(y_ref[...], dhb_ref[...], TN)

    row = pl.BlockSpec((tm, d), lambda i: (i, 0))
    vec = pl.BlockSpec((1, d), lambda i: (0, 0))
    in_specs = [pl.BlockSpec((tm, k), lambda i: (i, 0)), pl.BlockSpec((d, k), lambda i: (0, 0)), row, vec, row]
    out_specs, out_shape = [row, vec], [jax.ShapeDtypeStruct((t, d), F32), jax.ShapeDtypeStruct((1, d), F32)]
    args = [dproj, w, h, g, dres]
    if chained:
        in_specs.append(row)
        args.append(prev_y)
        out_specs += [row, pl.BlockSpec((d, d), lambda i: (0, 0))]
        out_shape += [jax.ShapeDtypeStruct((t, d), BF16), jax.ShapeDtypeStruct((d, d), F32)]
    return pl.pallas_call(
        body, name=name, grid=(t // tm,), in_specs=in_specs, out_specs=out_specs, out_shape=out_shape,
        compiler_params=_params(("arbitrary",)),
    )(*args)


def _rope_partner(x):
    lane = lax.broadcasted_iota(jnp.int32, x.shape, 1)
    swapped = jnp.where(lane < MLA_NOPE + MLA_ROPE // 2, pltpu.roll(x, LANES - 16, 1), pltpu.roll(x, 16, 1))
    return jnp.where((lane >= MLA_NOPE) & (lane < MLA_NOPE + MLA_ROPE), swapped, 0.0)


def _pool_counts(row0, tm, w):
    t_idx = row0 + lax.broadcasted_iota(jnp.int32, (tm, POOL_GROUP), 0)
    return jnp.minimum(t_idx + 1, w).astype(F32)


def _ab_prep(proj, pos, inv_freq, q_a_norm, kv_a_norm, wq, wk, wv, wpool, *, name):
    t = proj.shape[0]
    tm = _tile(t, ROW_TILE)
    hb = tm // POOL_HALO

    def body(p_ref, halo_ref, pos_ref, inv_ref, qg_ref, kg_ref, wq_ref, wk_ref, wv_ref, wp_ref,
             q_ref, k_ref, v_ref, yb_ref, qn_ref, kvn_ref, d_ref, cos_ref, sin_ref, ext):
        i = pl.program_id(0)
        ql = p_ref[:, 0:MLA_Q_RANK]
        r = lax.rsqrt(jnp.mean(ql * ql, axis=-1, keepdims=True) + RMS_EPS)
        qn = (ql * r * qg_ref[...]).astype(BF16)
        qn_ref[...] = qn
        kl = p_ref[:, MLA_Q_RANK:MLA_Q_RANK + MLA_KV_RANK]
        r = lax.rsqrt(jnp.mean(kl * kl, axis=-1, keepdims=True) + RMS_EPS)
        kvn = (kl * r * kg_ref[...]).astype(BF16)
        kvn_ref[...] = kvn
        ang = pos_ref[...].astype(F32) * inv_ref[...]
        lane = lax.broadcasted_iota(jnp.int32, (tm, LANES), 1)
        in_rope = (lane >= MLA_NOPE) & (lane < MLA_NOPE + MLA_ROPE)
        cos_t = jnp.where(in_rope, jnp.cos(ang), 1.0)
        sin_t = jnp.where(in_rope, jnp.sin(ang), 0.0)
        sin_t = jnp.where(lane < MLA_NOPE + MLA_ROPE // 2, -sin_t, sin_t)
        cos_ref[...] = cos_t
        sin_ref[...] = sin_t
        kr = p_ref[:, 384:512]
        kr = kr * cos_t + _rope_partner(kr) * sin_t
        qraw = _dot(qn, wq_ref[...])
        kvk = _dot(kvn, wk_ref[...])
        for h in range(MLA_HEADS):
            sl = slice(h * LANES, (h + 1) * LANES)
            qh = qraw[:, sl]
            q_ref[:, sl] = ((qh * cos_t + _rope_partner(qh) * sin_t) * (ATT_SCALE * LOG2E)).astype(BF16)
            k_ref[:, sl] = (kvk[:, sl] + kr).astype(BF16)
        v_ref[...] = _dot(kvn, wv_ref[...]).astype(BF16)
        xp = p_ref[:, 512:1024]
        ext[0:POOL_HALO, :] = jnp.where(i > 0, halo_ref[...], 0.0)
        ext[POOL_HALO:POOL_HALO + tm, :] = xp
        for g, w in enumerate(POOL_WINDOWS):
            lo = g * POOL_GROUP
            acc = ext[POOL_HALO:POOL_HALO + tm, lo:lo + POOL_GROUP]
            for s in range(1, w):
                acc = acc + ext[POOL_HALO - s:POOL_HALO - s + tm, lo:lo + POOL_GROUP]
            cnt = _pool_counts(i * tm, tm, w)
            d_ref[:, lo:lo + POOL_GROUP] = (acc / cnt - xp[:, lo:lo + POOL_GROUP]).astype(BF16)
        yb_ref[...] = _dot(d_ref[...], wp_ref[...])

    row = lambda w: pl.BlockSpec((tm, w), lambda i: (i, 0))
    vec = lambda w: pl.BlockSpec((1, w), lambda i: (0, 0))
    whole = lambda a: pl.BlockSpec(a.shape, lambda i: (0, 0))
    return pl.pallas_call(
        body, name=name, grid=(t // tm,),
        in_specs=[row(1024), pl.BlockSpec((POOL_HALO, POOL_WIDTH), lambda i: (jnp.maximum(i * hb - 1, 0), 1)),
                  pl.BlockSpec((tm, 1), lambda i: (i, 0)), vec(LANES), vec(MLA_Q_RANK), vec(MLA_KV_RANK),
                  whole(wq), whole(wk), whole(wv), whole(wpool)],
        out_specs=[row(1024), row(1024), row(512), row(512), row(MLA_Q_RANK), row(MLA_KV_RANK), row(POOL_WIDTH),
                   row(LANES), row(LANES)],
        out_shape=[jax.ShapeDtypeStruct((t, 1024), BF16), jax.ShapeDtypeStruct((t, 1024), BF16),
                   jax.ShapeDtypeStruct((t, 512), BF16), jax.ShapeDtypeStruct((t, 512), F32),
                   jax.ShapeDtypeStruct((t, MLA_Q_RANK), BF16), jax.ShapeDtypeStruct((t, MLA_KV_RANK), BF16),
                   jax.ShapeDtypeStruct((t, POOL_WIDTH), BF16), jax.ShapeDtypeStruct((t, LANES), F32),
                   jax.ShapeDtypeStruct((t, LANES), F32)],
        scratch_shapes=[pltpu.VMEM((tm + POOL_HALO, POOL_WIDTH), F32)],
        compiler_params=_params(("parallel",)),
    )(proj, proj, pos, inv_freq, q_a_norm, kv_a_norm, wq, wk, wv, wpool)


def _ab_prep_bwd(proj, q_a_norm, kv_a_norm, dq, dk, dv, cos_t, sin_t, dyb, dz, qn, kvn, d, hn, wq, wk, wv, wpool, *, name):
    t = proj.shape[0]
    tm = _tile(t, ROW_TILE)
    hb = tm // POOL_HALO
    last_halo = t // POOL_HALO - 1
    nt = t // tm

    def body(p_ref, qg_ref, kg_ref, dq_ref, dk_ref, dv_ref, c_ref, s_ref, dyb_ref, dybn_ref, dz_ref,
             qn_ref, kvn_ref, d_ref, hn_ref, wq_ref, wk_ref, wv_ref, wp_ref,
             dp_ref, dqg_ref, dkg_ref, dwq_ref, dwk_ref, dwv_ref, dwp_ref, dwin_ref, ext, dqr_ref, dkb_ref):
        i = pl.program_id(0)

        @pl.when(i == 0)
        def _():
            for ref in (dqg_ref, dkg_ref, dwq_ref, dwk_ref, dwv_ref, dwp_ref, dwin_ref):
                ref[...] = jnp.zeros_like(ref)

        def norm_bwd(x, g, dy, dg_ref):
            r = lax.rsqrt(jnp.mean(x * x, axis=-1, keepdims=True) + RMS_EPS)
            xh = x * r
            dxh = dy * g
            dg_ref[...] += jnp.sum(dy * xh, axis=0, keepdims=True)
            return r * (dxh - xh * jnp.mean(dxh * xh, axis=-1, keepdims=True))

        c, s = c_ref[...], s_ref[...]
        lane = lax.broadcasted_iota(jnp.int32, (tm, LANES), 1)
        in_rope = (lane >= MLA_NOPE) & (lane < MLA_NOPE + MLA_ROPE)
        dkr = jnp.zeros((tm, LANES), F32)
        for h in range(MLA_HEADS):
            sl = slice(h * LANES, (h + 1) * LANES)
            g = dq_ref[:, sl]
            dqr_ref[:, sl] = ((g * c + _rope_partner(g * s)) * ATT_SCALE).astype(BF16)
            gk = dk_ref[:, sl]
            dkb_ref[:, sl] = gk.astype(BF16)
            dkr = dkr + jnp.where(in_rope, gk, 0.0)
        dkr = dkr * c + _rope_partner(dkr * s)
        dqn = _dot(dqr_ref[...], wq_ref[...], NT)
        dkvn = _dot(dkb_ref[...], wk_ref[...], NT) + _dot(dv_ref[...], wv_ref[...], NT)
        dql = norm_bwd(p_ref[:, 0:MLA_Q_RANK], qg_ref[...], dqn, dqg_ref)
        dp_ref[:, 0:MLA_Q_RANK] = dql.astype(BF16)
        dkl = norm_bwd(p_ref[:, MLA_Q_RANK:384], kg_ref[...], dkvn, dkg_ref)
        dp_ref[:, MLA_Q_RANK:384] = dkl.astype(BF16)
        dp_ref[:, 384:512] = dkr.astype(BF16)
        ddv = _dot(dyb_ref[...], wp_ref[...], NT)
        ddn = _dot(dybn_ref[...], wp_ref[...], NT)
        for g, w in enumerate(POOL_WINDOWS):
            lo = g * POOL_GROUP
            ext[0:tm, lo:lo + POOL_GROUP] = ddv[:, lo:lo + POOL_GROUP] / _pool_counts(i * tm, tm, w)
            nxt = ddn[:, lo:lo + POOL_GROUP] / _pool_counts((i + 1) * tm, POOL_HALO, w)
            ext[tm:tm + POOL_HALO, lo:lo + POOL_GROUP] = jnp.where(i < nt - 1, nxt, 0.0)
        for g, w in enumerate(POOL_WINDOWS):
            lo = g * POOL_GROUP
            acc = ext[0:tm, lo:lo + POOL_GROUP]
            for s in range(1, w):
                acc = acc + ext[s:s + tm, lo:lo + POOL_GROUP]
            dp_ref[:, 512 + lo:512 + lo + POOL_GROUP] = (acc - ddv[:, lo:lo + POOL_GROUP]).astype(BF16)
        dp_ref[:, 1024:2048] = dz_ref[...]
        dwq_ref[...] += _dot(qn_ref[...], dqr_ref[...], TN)
        dwk_ref[...] += _dot(kvn_ref[...], dkb_ref[...], TN)
        dwv_ref[...] += _dot(kvn_ref[...], dv_ref[...], TN)
        dwp_ref[...] += _dot(d_ref[...], dyb_ref[...], TN)
        dwin_ref[...] += _dot(dp_ref[...], hn_ref[...], TN)

    row = lambda w: pl.BlockSpec((tm, w), lambda i: (i, 0))
    vec = lambda w: pl.BlockSpec((1, w), lambda i: (0, 0))
    whole = lambda a: pl.BlockSpec(a.shape, lambda i: (0, 0))
    weights = (wq, wk, wv, wpool)
    return pl.pallas_call(
        body, name=name, grid=(nt,),
        in_specs=[row(1024), vec(MLA_Q_RANK), vec(MLA_KV_RANK), row(1024), row(1024), row(512), row(LANES), row(LANES),
                  row(POOL_WIDTH),
                  pl.BlockSpec((POOL_HALO, POOL_WIDTH), lambda i: (jnp.minimum((i + 1) * hb, last_halo), 0)),
                  row(1024), row(MLA_Q_RANK), row(MLA_KV_RANK), row(POOL_WIDTH), row(1024)] + [whole(w) for w in weights],
        out_specs=[row(IN_AB_PAD), vec(MLA_Q_RANK), vec(MLA_KV_RANK)] + [whole(w) for w in weights]
        + [pl.BlockSpec((IN_AB_PAD, 1024), lambda i: (0, 0))],
        out_shape=[jax.ShapeDtypeStruct((t, IN_AB_PAD), BF16), jax.ShapeDtypeStruct((1, MLA_Q_RANK), F32),
                   jax.ShapeDtypeStruct((1, MLA_KV_RANK), F32)] + [jax.ShapeDtypeStruct(w.shape, F32) for w in weights]
        + [jax.ShapeDtypeStruct((IN_AB_PAD, 1024), F32)],
        scratch_shapes=[pltpu.VMEM((tm + POOL_HALO, POOL_WIDTH), F32), pltpu.VMEM((tm, 1024), BF16),
                        pltpu.VMEM((tm, 1024), BF16)],
        compiler_params=_params(("arbitrary",)),
    )(proj, q_a_norm, kv_a_norm, dq, dk, dv, cos_t, sin_t, dyb, dyb, dz, qn, kvn, d, hn, wq, wk, wv, wpool)


def _gate_out_proj(o, ybraw, proj, pool_scale, w, hres, *, name):
    t = o.shape[0]
    tm = _tile(t, 2 * ROW_TILE)

    def body(o_ref, yb_ref, z_ref, ps_ref, w_ref, h_ref, ho_ref, y_ref):
        z = z_ref[...]
        sz = z * _sigmoid(z)
        y_ref[:, 0:512] = (o_ref[...] * sz[:, 0:512]).astype(BF16)
        y_ref[:, 512:1024] = (yb_ref[...] * ps_ref[...] * sz[:, 512:1024]).astype(BF16)
        ho_ref[...] = h_ref[...] + _dot(y_ref[...], w_ref[...])

    row = lambda w_: pl.BlockSpec((tm, w_), lambda i: (i, 0))
    return pl.pallas_call(
        body, name=name, grid=(t // tm,),
        in_specs=[row(512), row(512), pl.BlockSpec((tm, 1024), lambda i: (i, 1)), pl.BlockSpec((1, 512), lambda i: (0, 0)),
                  pl.BlockSpec(w.shape, lambda i: (0, 0)), row(1024)],
        out_specs=[row(1024), row(1024)],
        out_shape=[jax.ShapeDtypeStruct((t, 1024), F32), jax.ShapeDtypeStruct((t, 1024), BF16)],
        compiler_params=_params(("parallel",)),
    )(o, ybraw, proj, pool_scale, w, hres)


def _gate_bwd(dh, w, o, ybraw, proj, pool_scale, *, name):
    t = o.shape[0]
    tm = _tile(t, ROW_TILE)

    def body(dh_ref, w_ref, o_ref, yb_ref, z_ref, ps_ref, do_ref, dl_ref, dyb_ref, dz_ref, dps_ref):
        i = pl.program_id(0)
        z = z_ref[...]
        sg = _sigmoid(z)
        sz = z * sg
        dsz = sg * (1.0 + z * (1.0 - sg))
        dyv = _dot(dh_ref[...], w_ref[...], NT)
        dcat = dyv * sz
        ov = o_ref[...]
        ybs = yb_ref[...] * ps_ref[...]
        dz_ref[:, 0:512] = (dyv[:, 0:512] * ov * dsz[:, 0:512]).astype(BF16)
        dz_ref[:, 512:1024] = (dyv[:, 512:1024] * ybs * dsz[:, 512:1024]).astype(BF16)
        do = dcat[:, 0:512]
        do_ref[...] = do.astype(BF16)
        r_i = (lax.broadcasted_iota(jnp.int32, (1024, 512), 0) % 512) // MLA_V
        c_i = lax.broadcasted_iota(jnp.int32, (1024, 512), 1) // MLA_V
        prod = do * ov
        hi = prod.astype(BF16)
        lo = (prod - hi.astype(F32)).astype(BF16)
        dl_ref[...] = _dot(jnp.concatenate([hi, lo], axis=1), (r_i == c_i).astype(BF16))
        dyb_ref[...] = (dcat[:, 512:1024] * ps_ref[...]).astype(BF16)

        @pl.when(i == 0)
        def _():
            dps_ref[...] = jnp.zeros_like(dps_ref)

        dps_ref[...] += jnp.sum(dcat[:, 512:1024] * yb_ref[...], axis=0, keepdims=True)

    row = lambda w: pl.BlockSpec((tm, w), lambda i: (i, 0))
    vec = pl.BlockSpec((1, 512), lambda i: (0, 0))
    return pl.pallas_call(
        body, name=name, grid=(t // tm,),
        in_specs=[row(1024), pl.BlockSpec(w.shape, lambda i: (0, 0)), row(512), row(512),
                  pl.BlockSpec((tm, 1024), lambda i: (i, 1)), vec],
        out_specs=[row(512), row(512), row(512), row(1024), vec],
        out_shape=[jax.ShapeDtypeStruct((t, 512), BF16), jax.ShapeDtypeStruct((t, 512), F32),
                   jax.ShapeDtypeStruct((t, 512), BF16), jax.ShapeDtypeStruct((t, 1024), BF16),
                   jax.ShapeDtypeStruct((1, 512), F32)],
        compiler_params=_params(("arbitrary",)),
    )(dh, w, o, ybraw, proj, pool_scale)


ATT_HP_FWD = 4
ATT_HP_BWD = 2


def _diag_mask(tq):
    return lax.broadcasted_iota(jnp.int32, (tq, tq), 1) <= lax.broadcasted_iota(jnp.int32, (tq, tq), 0)


def _block_schedule(nq, key_major):
    if key_major:
        pairs = [(qi, ki) for ki in range(nq) for qi in range(ki, nq)]
    else:
        pairs = [(qi, ki) for qi in range(nq) for ki in range(qi + 1)]
    return jnp.asarray([p[0] for p in pairs], jnp.int32), jnp.asarray([p[1] for p in pairs], jnp.int32)


def _attn_fwd(q, k, v, *, name):
    t = q.shape[0]
    tq = _tile(t, ATT_TILE)
    nq = t // tq
    hp = ATT_HP_FWD
    qi_tab, ki_tab = _block_schedule(nq, key_major=False)

    def body(qi_ref, ki_ref, q_ref, k_ref, v_ref, o_ref, lse_ref, m_sc, l_sc, acc_sc):
        step = pl.program_id(1)
        qi, ki = qi_ref[step], ki_ref[step]

        @pl.when(ki == 0)
        def _():
            m_sc[...] = jnp.full_like(m_sc, -jnp.inf)
            l_sc[...] = jnp.zeros_like(l_sc)
            acc_sc[...] = jnp.zeros_like(acc_sc)

        def block(on_diagonal):
            scores = []
            for h in range(hp):
                sl = slice(h * LANES, (h + 1) * LANES)
                scores.append(_dot(q_ref[:, sl], k_ref[:, sl], NT))
            if on_diagonal:
                mask = _diag_mask(tq)
                scores = [jnp.where(mask, s, -jnp.inf) for s in scores]
            for h, s in enumerate(scores):
                vv = v_ref[:, (h // 2) * LANES:(h // 2 + 1) * LANES]
                m_prev = m_sc[h]
                m_new = jnp.maximum(m_prev, jnp.max(s, axis=-1, keepdims=True))
                alpha = jnp.exp2(m_prev - m_new)
                p = jnp.exp2(s - m_new[:, 0:1])
                l_sc[h] = alpha * l_sc[h] + jnp.sum(p, axis=-1, keepdims=True)
                acc_sc[h] = alpha * acc_sc[h] + _dot(p.astype(BF16), vv)
                m_sc[h] = m_new

        pl.when(ki < qi)(functools.partial(block, False))
        pl.when(ki == qi)(functools.partial(block, True))

        @pl.when(ki == qi)
        def _():
            first = lax.broadcasted_iota(jnp.int32, (tq, LANES), 1) < MLA_V
            for pr in range(hp // 2):
                a, b = 2 * pr, 2 * pr + 1
                sl = slice(pr * LANES, (pr + 1) * LANES)
                o_ref[:, sl] = jnp.where(first, acc_sc[a] / l_sc[a], acc_sc[b] / l_sc[b])
                lse_ref[:, sl] = jnp.where(first, m_sc[a] + jnp.log2(l_sc[a]), m_sc[b] + jnp.log2(l_sc[b]))

    grid_spec = pltpu.PrefetchScalarGridSpec(
        num_scalar_prefetch=2, grid=(MLA_HEADS // hp, qi_tab.shape[0]),
        in_specs=[pl.BlockSpec((tq, hp * LANES), lambda g, s, qt, kt: (qt[s], g)),
                  pl.BlockSpec((tq, hp * LANES), lambda g, s, qt, kt: (kt[s], g)),
                  pl.BlockSpec((tq, hp * MLA_V), lambda g, s, qt, kt: (kt[s], g))],
        out_specs=[pl.BlockSpec((tq, hp * MLA_V), lambda g, s, qt, kt: (qt[s], g)),
                   pl.BlockSpec((tq, hp * MLA_V), lambda g, s, qt, kt: (qt[s], g))],
        scratch_shapes=[pltpu.VMEM((hp, tq, LANES), F32)] * 3,
    )
    return pl.pallas_call(
        body, name=name, grid_spec=grid_spec,
        out_shape=[jax.ShapeDtypeStruct((t, 512), F32), jax.ShapeDtypeStruct((t, 512), F32)],
        compiler_params=_params(("parallel", "arbitrary")),
    )(qi_tab, ki_tab, q, k, v)


def _attn_bwd(q, k, v, do, lse, delta, *, name):
    t = q.shape[0]
    tq = _tile(t, ATT_TILE)
    nq = t // tq
    hp = ATT_HP_BWD
    qi_tab, ki_tab = _block_schedule(nq, key_major=True)

    def body(qi_ref, ki_ref, q_ref, k_ref, v_ref, do_ref, lse_ref, dl_ref, dq_ref, dk_ref, dv_ref, dk_sc, dv_sc):
        step = pl.program_id(1)
        qi, ki = qi_ref[step], ki_ref[step]

        @pl.when(step == 0)
        def _():
            dq_ref[...] = jnp.zeros_like(dq_ref)

        @pl.when(qi == ki)
        def _():
            dk_sc[...] = jnp.zeros_like(dk_sc)
            dv_sc[...] = jnp.zeros_like(dv_sc)

        def block(on_diagonal):
            lane = lax.broadcasted_iota(jnp.int32, (tq, LANES), 1)
            rows = pl.ds(pl.multiple_of(qi * tq, tq), tq)
            heads = [slice(h * LANES, (h + 1) * LANES) for h in range(hp)]
            scores = [_dot(q_ref[:, sl], k_ref[:, sl], NT) for sl in heads]
            dps = []
            for h in range(hp):
                dov = do_ref[:, (h // 2) * LANES:(h // 2 + 1) * LANES]
                mine = (lane < MLA_V) if h % 2 == 0 else (lane >= MLA_V)
                dps.append(_dot(jnp.where(mine, dov, jnp.zeros_like(dov)), v_ref[:, (h // 2) * LANES:(h // 2 + 1) * LANES], NT))
            mask = _diag_mask(tq) if on_diagonal else None
            for h, sl in enumerate(heads):
                col = (h // 2) * LANES + (h % 2) * MLA_V
                p = jnp.exp2(scores[h] - lse_ref[:, col:col + 1])
                if on_diagonal:
                    p = jnp.where(mask, p, 0.0)
                ds = (p * (dps[h] - dl_ref[:, col:col + 1])).astype(BF16)
                dv_sc[h] += _dot(p.astype(BF16), do_ref[:, (h // 2) * LANES:(h // 2 + 1) * LANES], TN)
                dk_sc[h] += _dot(ds, q_ref[:, sl], TN)
                dq_ref[rows, sl] += _dot(ds, k_ref[:, sl], NN)

        pl.when(qi > ki)(functools.partial(block, False))
        pl.when(qi == ki)(functools.partial(block, True))

        @pl.when(qi == nq - 1)
        def _():
            first = lax.broadcasted_iota(jnp.int32, (tq, LANES), 1) < MLA_V
            for h in range(hp):
                dk_ref[:, h * LANES:(h + 1) * LANES] = dk_sc[h] * (1.0 / LOG2E)
            for pr in range(hp // 2):
                dv_ref[:, pr * LANES:(pr + 1) * LANES] = jnp.where(first, dv_sc[2 * pr], dv_sc[2 * pr + 1]).astype(BF16)

    qrow = lambda w: pl.BlockSpec((tq, w), lambda g, s, qt, kt: (qt[s], g))
    krow = lambda w: pl.BlockSpec((tq, w), lambda g, s, qt, kt: (kt[s], g))
    grid_spec = pltpu.PrefetchScalarGridSpec(
        num_scalar_prefetch=2, grid=(MLA_HEADS // hp, qi_tab.shape[0]),
        in_specs=[qrow(hp * LANES), krow(hp * LANES), krow(hp * MLA_V), qrow(hp * MLA_V), qrow(hp * MLA_V), qrow(hp * MLA_V)],
        out_specs=[pl.BlockSpec((t, hp * LANES), lambda g, s, qt, kt: (0, g)), krow(hp * LANES), krow(hp * MLA_V)],
        scratch_shapes=[pltpu.VMEM((hp, tq, LANES), F32), pltpu.VMEM((hp, tq, LANES), F32)],
    )
    return pl.pallas_call(
        body, name=name, grid_spec=grid_spec,
        out_shape=[jax.ShapeDtypeStruct((t, 1024), F32), jax.ShapeDtypeStruct((t, 1024), F32),
                   jax.ShapeDtypeStruct((t, 512), BF16)],
        compiler_params=_params(("parallel", "arbitrary")),
    )(qi_tab, ki_tab, q, k, v, do, lse, delta)


def _conv_rows(ext, tm, w_ref, sec):
    c0 = sec * 1024
    y = ext[CONV_HALO - 3:CONV_HALO - 3 + tm, c0:c0 + 1024] * w_ref[0:1, c0:c0 + 1024]
    for j in range(1, CONV_WIDTH):
        y = y + ext[CONV_HALO - 3 + j:CONV_HALO - 3 + j + tm, c0:c0 + 1024] * w_ref[j:j + 1, c0:c0 + 1024]
    return y


def _c_prep(proj_c, conv_w, a_log, dt_bias, *, name):
    t = proj_c.shape[0]
    tm = _tile(t, ROW_TILE)
    hb = tm // CONV_HALO

    def body(p_ref, halo_ref, ab_ref, w_ref, al_ref, dtb_ref, q_ref, k_ref, v_ref, g_ref, b_ref, gt_ref, ext):
        i = pl.program_id(0)
        ext[0:CONV_HALO, :] = jnp.where(i > 0, halo_ref[...], 0.0)
        ext[CONV_HALO:CONV_HALO + tm, :] = p_ref[...]
        for sec, o_ref in enumerate((q_ref, k_ref, v_ref)):
            y = _conv_rows(ext, tm, w_ref, sec)
            y = y * _sigmoid(y)
            if sec == 2:
                o_ref[...] = y
                continue
            scale = GDN_DK ** -0.5 if sec == 0 else 1.0
            for h in range(GDN_HEADS):
                sl = slice(h * LANES, (h + 1) * LANES)
                blk = y[:, sl]
                r = lax.rsqrt(jnp.sum(blk * blk, axis=-1, keepdims=True) + RMS_EPS)
                o_ref[:, sl] = blk * (r * scale)
        ab = ab_ref[...]
        g = -jnp.exp(al_ref[...]) * _softplus(ab + dtb_ref[...])
        beta = _sigmoid(ab)
        ri = lax.broadcasted_iota(jnp.int32, (tm, tm), 0)
        ci = lax.broadcasted_iota(jnp.int32, (tm, tm), 1)
        lower = ((ri // CHUNK) == (ci // CHUNK)) & (ri >= ci)
        gc = _dot(lower.astype(F32), g, NN, HI)
        eye = lax.broadcasted_iota(jnp.int32, (LANES, LANES), 0) == lax.broadcasted_iota(jnp.int32, (LANES, LANES), 1)
        gt_ref[...] = _dot(eye.astype(F32), gc, NT, HI)[0:GDN_HEADS, :]
        for h in range(GDN_HEADS):
            sl = slice(h * LANES, (h + 1) * LANES)
            g_ref[:, sl] = jnp.broadcast_to(gc[:, h:h + 1], (tm, LANES))
            b_ref[:, sl] = jnp.broadcast_to(beta[:, GDN_HEADS + h:GDN_HEADS + h + 1], (tm, LANES))

    row = lambda w: pl.BlockSpec((tm, w), lambda i: (i, 0))
    vec = lambda r, w: pl.BlockSpec((r, w), lambda i: (0, 0))
    out = jax.ShapeDtypeStruct((t, 1024), F32)
    return pl.pallas_call(
        body, name=name, grid=(t // tm,),
        in_specs=[row(3072), pl.BlockSpec((CONV_HALO, 3072), lambda i: (jnp.maximum(i * hb - 1, 0), 0)),
                  pl.BlockSpec((tm, LANES), lambda i: (i, 32)), vec(CONV_WIDTH, 3072), vec(1, LANES), vec(1, LANES)],
        out_specs=[row(1024)] * 5 + [pl.BlockSpec((GDN_HEADS, tm), lambda i: (0, i))],
        out_shape=[out] * 5 + [jax.ShapeDtypeStruct((GDN_HEADS, t), F32)],
        scratch_shapes=[pltpu.VMEM((tm + CONV_HALO, 3072), F32)],
        compiler_params=_params(("parallel",)),
    )(proj_c, proj_c, proj_c, conv_w, a_log, dt_bias)


def _c_prep_bwd(proj_c, conv_w, a_log, dt_bias, dq, dk, dv, dgb, dbb, dz, *, name):
    t = proj_c.shape[0]
    tm = _tile(t, ROW_TILE)
    hb = tm // CONV_HALO
    nt = t // tm
    rev = lambda i: nt - 1 - i

    def body(p_ref, halo_ref, ab_ref, w_ref, al_ref, dtb_ref, dq_ref, dk_ref, dv_ref, dg_ref, db_ref, dz_ref,
             dp_ref, dw_ref, dal_ref, ddt_ref, ext, dyext, carry, taps):
        step = pl.program_id(0)
        i = rev(step)

        @pl.when(step == 0)
        def _():
            dw_ref[...] = jnp.zeros_like(dw_ref)
            dal_ref[...] = jnp.zeros_like(dal_ref)
            ddt_ref[...] = jnp.zeros_like(ddt_ref)
            carry[...] = jnp.zeros_like(carry)

        ext[0:CONV_HALO, :] = jnp.where(i > 0, halo_ref[...], 0.0)
        ext[CONV_HALO:CONV_HALO + tm, :] = p_ref[...]
        for sec, g_ref in enumerate((dq_ref, dk_ref, dv_ref)):
            c0 = sec * 1024
            for j in range(CONV_WIDTH):
                taps[j] = ext[CONV_HALO - 3 + j:CONV_HALO - 3 + j + tm, c0:c0 + 1024]
            y = taps[0] * w_ref[0:1, c0:c0 + 1024]
            for j in range(1, CONV_WIDTH):
                y = y + taps[j] * w_ref[j:j + 1, c0:c0 + 1024]
            sg = _sigmoid(y)
            act = y * sg
            if sec == 2:
                dact = g_ref[...]
            else:
                scale = GDN_DK ** -0.5 if sec == 0 else 1.0
                parts = []
                for h in range(GDN_HEADS):
                    sl = slice(h * LANES, (h + 1) * LANES)
                    blk = act[:, sl]
                    r = lax.rsqrt(jnp.sum(blk * blk, axis=-1, keepdims=True) + RMS_EPS)
                    n = blk * r
                    dn = g_ref[:, sl] * scale
                    parts.append(r * (dn - n * jnp.sum(dn * n, axis=-1, keepdims=True)))
                dact = jnp.concatenate(parts, axis=-1)
            dy = dact * (sg * (1.0 + y * (1.0 - sg)))
            dyext[0:tm, c0:c0 + 1024] = dy
            for j in range(CONV_WIDTH):
                dw_ref[j:j + 1, c0:c0 + 1024] += jnp.sum(dy * taps[j], axis=0, keepdims=True)
        dyext[tm:tm + CONV_HALO, :] = carry[...]
        carry[...] = dyext[0:CONV_HALO, :]
        for sec in range(3):
            c0 = sec * 1024
            dx = dyext[3:3 + tm, c0:c0 + 1024] * w_ref[0:1, c0:c0 + 1024]
            for j in range(1, CONV_WIDTH):
                dx = dx + dyext[3 - j:3 - j + tm, c0:c0 + 1024] * w_ref[j:j + 1, c0:c0 + 1024]
            dp_ref[:, c0:c0 + 1024] = dx.astype(BF16)
        dp_ref[:, 3072:4096] = dz_ref[...]
        lane = lax.broadcasted_iota(jnp.int32, (tm, LANES), 1)
        dg = jnp.zeros((tm, LANES), F32)
        dbeta = jnp.zeros((tm, LANES), F32)
        for h in range(GDN_HEADS):
            sl = slice(h * LANES, (h + 1) * LANES)
            dg = dg + jnp.where(lane == h, dg_ref[:, sl], 0.0)
            dbeta = dbeta + jnp.where(lane == GDN_HEADS + h, db_ref[:, sl], 0.0)
        ri = lax.broadcasted_iota(jnp.int32, (tm, tm), 0)
        ci = lax.broadcasted_iota(jnp.int32, (tm, tm), 1)
        upper = ((ri // CHUNK) == (ci // CHUNK)) & (ri <= ci)
        dg = _dot(upper.astype(F32), dg, NN, HI)
        pre = ab_ref[...] + dtb_ref[...]
        s = _sigmoid(pre)
        a_exp = jnp.exp(al_ref[...])
        dg_da = dg * (-a_exp * s)
        dp_ref[:, 4096:IN_C_PAD] = (dg_da + dbeta * s * (1.0 - s)).astype(BF16)
        dal_ref[...] += jnp.sum(dg * (-a_exp * _softplus(pre)), axis=0, keepdims=True)
        ddt_ref[...] += jnp.sum(dg_da, axis=0, keepdims=True)

    row = lambda w: pl.BlockSpec((tm, w), lambda s: (rev(s), 0))
    vec = lambda r, w: pl.BlockSpec((r, w), lambda s: (0, 0))
    return pl.pallas_call(
        body, name=name, grid=(nt,),
        in_specs=[row(3072), pl.BlockSpec((CONV_HALO, 3072), lambda s: (jnp.maximum(rev(s) * hb - 1, 0), 0)),
                  pl.BlockSpec((tm, LANES), lambda s: (rev(s), 32)), vec(CONV_WIDTH, 3072), vec(1, LANES), vec(1, LANES),
                  row(1024), row(1024), row(1024), row(1024), row(1024), row(1024)],
        out_specs=[row(IN_C_PAD), vec(CONV_WIDTH, 3072), vec(1, LANES), vec(1, LANES)],
        out_shape=[jax.ShapeDtypeStruct((t, IN_C_PAD), BF16), jax.ShapeDtypeStruct((CONV_WIDTH, 3072), F32),
                   jax.ShapeDtypeStruct((1, LANES), F32), jax.ShapeDtypeStruct((1, LANES), F32)],
        scratch_shapes=[pltpu.VMEM((tm + CONV_HALO, 3072), F32), pltpu.VMEM((tm + CONV_HALO, 3072), F32),
                        pltpu.VMEM((CONV_HALO, 3072), F32), pltpu.VMEM((CONV_WIDTH, tm, 1024), F32)],
        compiler_params=_params(("arbitrary",)),
    )(proj_c, proj_c, proj_c, conv_w, a_log, dt_bias, dq, dk, dv, dgb, dbb, dz)


def _o_gate_bwd(dh, w, o, proj_c, o_norm, *, name):
    t = o.shape[0]
    tm = _tile(t, 2 * ROW_TILE)

    def body(dh_ref, w_ref, o_ref, z_ref, g_ref, do_ref, dz_ref, dg_ref, dy_ref):
        i = pl.program_id(0)

        @pl.when(i == 0)
        def _():
            dg_ref[...] = jnp.zeros_like(dg_ref)

        dy_ref[...] = _dot(dh_ref[...], w_ref[...], NT)
        dg = jnp.zeros((1, LANES), F32)
        for h in range(GDN_HEADS):
            sl = slice(h * LANES, (h + 1) * LANES)
            x = o_ref[:, sl]
            r = lax.rsqrt(jnp.mean(x * x, axis=-1, keepdims=True) + RMS_EPS)
            xh = x * r
            z = z_ref[:, sl]
            sg = _sigmoid(z)
            dyv = dy_ref[:, sl]
            dn = dyv * (z * sg)
            dz_ref[:, sl] = (dyv * xh * g_ref[...] * (sg * (1.0 + z * (1.0 - sg)))).astype(BF16)
            dxh = dn * g_ref[...]
            do_ref[:, sl] = r * (dxh - xh * jnp.mean(dxh * xh, axis=-1, keepdims=True))
            dg = dg + jnp.sum(dn * xh, axis=0, keepdims=True)
        dg_ref[...] += dg

    row = pl.BlockSpec((tm, 1024), lambda i: (i, 0))
    vec = pl.BlockSpec((1, LANES), lambda i: (0, 0))
    return pl.pallas_call(
        body, name=name, grid=(t // tm,),
        in_specs=[row, pl.BlockSpec(w.shape, lambda i: (0, 0)), row, pl.BlockSpec((tm, 1024), lambda i: (i, 3)), vec],
        out_specs=[row, row, vec],
        out_shape=[jax.ShapeDtypeStruct((t, 1024), F32), jax.ShapeDtypeStruct((t, 1024), BF16),
                   jax.ShapeDtypeStruct((1, LANES), F32)],
        scratch_shapes=[pltpu.VMEM((tm, 1024), F32)],
        compiler_params=_params(("arbitrary",)),
    )(dh, w, o, proj_c, o_norm)


PAIR = 2 * CHUNK
GDN_HP = 8


def _bdot(a, b, dims=NN):
    return _dot(a.astype(BF16), b.astype(BF16), dims)


def _each(f, *lists):
    return [f(*args) for args in zip(*lists)]


def _pair_common(q, k, v, gci, gcj, beta):
    ri = lax.broadcasted_iota(jnp.int32, (PAIR, PAIR), 0)
    ci = lax.broadcasted_iota(jnp.int32, (PAIR, PAIR), 1)
    same = (ri // CHUNK) == (ci // CHUNK)
    incl = same & (ri >= ci)
    strict = same & (ri > ci)
    eye = (ri == ci).astype(F32)
    first = lax.broadcasted_iota(jnp.int32, (PAIR, LANES), 0) < CHUNK
    gamma = _each(lambda gi, gj: jnp.where(incl, jnp.exp(jnp.minimum(gi - gj, 0.0)), 0.0), gci, gcj)
    kb = _each(jnp.multiply, k, beta)
    kk = _each(lambda a, b: _bdot(a, b, NT), kb, k)
    qk = _each(lambda a, b: _bdot(a, b, NT), q, k)
    m = _each(lambda x, g: jnp.where(strict, x * g, 0.0), kk, gamma)
    tm_ = _each(lambda x: eye - x, m)
    pw = _each(lambda x: _bdot(x, x), m)
    for it in range(5):
        tm_ = _each(lambda x, p: x + _bdot(x, p), tm_, pw)
        if it < 4:
            pw = _each(lambda p: _bdot(p, p), pw)
    eg = _each(jnp.exp, gci)
    vb = _each(jnp.multiply, v, beta)
    kbe = _each(jnp.multiply, kb, eg)
    uw = _each(lambda x, a, b: _bdot(x, jnp.concatenate([a, b], axis=1)), tm_, vb, kbe)
    attn = _each(lambda x, g: jnp.where(incl, x * g, 0.0), qk, gamma)
    gl_a = _each(lambda g: g[CHUNK - 1:CHUNK, :], gci)
    gl_b = _each(lambda g: g[PAIR - 1:PAIR, :], gci)
    ek = _each(lambda a, b, g: jnp.exp(jnp.where(first, a, b) - g), gl_a, gl_b, gci)
    return dict(incl=incl, strict=strict, gamma=gamma, kb=kb, m=m, tm=tm_, eg=eg, vb=vb, kbe=kbe,
                u=_each(lambda x: x[:, :LANES], uw), w=_each(lambda x: x[:, LANES:], uw), attn=attn,
                qd=_each(jnp.multiply, q, eg), ek=ek, kd=_each(jnp.multiply, k, ek),
                glast_a=_each(jnp.exp, gl_a), glast_b=_each(jnp.exp, gl_b))


def _gdn_specs(t, ts, order):
    nc = ts // CHUNK
    blk = pl.BlockSpec((ts, GDN_HP * LANES), lambda h, s: (order(s), h))
    row = pl.BlockSpec((GDN_HP, 1, ts), lambda h, s: (h, 0, order(s)))
    st = pl.BlockSpec((GDN_HP, nc, LANES, LANES), lambda h, s: (h, order(s), 0, 0))
    return blk, row, st


def _gdn_fwd(q, k, v, gcb, gct, bb, *, name):
    t = q.shape[0]
    ts = _tile(t, GDN_TILE)
    npair = ts // PAIR

    def body(q_ref, k_ref, v_ref, g_ref, gt_ref, b_ref, o_ref, st_ref, s_sc):
        @pl.when(pl.program_id(1) == 0)
        def _():
            s_sc[...] = jnp.zeros_like(s_sc)

        def pair(pi, _):
            rows = pl.ds(pl.multiple_of(pi * PAIR, PAIR), PAIR)
            heads = [slice(hh * LANES, (hh + 1) * LANES) for hh in range(GDN_HP)]
            c = CHUNK
            cat0 = lambda *xs: jnp.concatenate(xs, axis=0)
            s0 = [s_sc[hh] for hh in range(GDN_HP)]
            cm = _pair_common([q_ref[rows, sl] for sl in heads], [k_ref[rows, sl] for sl in heads],
                              [v_ref[rows, sl] for sl in heads], [g_ref[rows, sl] for sl in heads],
                              [gt_ref[hh, :, rows] for hh in range(GDN_HP)], [b_ref[rows, sl] for sl in heads])
            u, w, qd, kd = cm["u"], cm["w"], cm["qd"], cm["kd"]
            r0 = _each(lambda w_, q_, s: _bdot(cat0(w_[:c], q_[:c]), s), w, qd, s0)
            vn_a = _each(lambda u_, r: u_[:c] - r[:c], u, r0)
            s1 = _each(lambda s, gl, k_, vn: s * gl + _bdot(k_[:c], vn, TN), s0, cm["glast_a"], kd, vn_a)
            r1 = _each(lambda w_, q_, s: _bdot(cat0(w_[c:], q_[c:]), s), w, qd, s1)
            vn_b = _each(lambda u_, r: u_[c:] - r[:c], u, r1)
            s2 = _each(lambda s, gl, k_, vn: s * gl + _bdot(k_[c:], vn, TN), s1, cm["glast_b"], kd, vn_b)
            o = _each(lambda ra, rb, at, va, vb_: cat0(ra[c:], rb[c:]) + _bdot(at, cat0(va, vb_)),
                      r0, r1, cm["attn"], vn_a, vn_b)
            for hh, sl in enumerate(heads):
                st_ref[hh, 2 * pi] = s0[hh]
                st_ref[hh, 2 * pi + 1] = s1[hh]
                s_sc[hh] = s2[hh]
                o_ref[rows, sl] = o[hh]
            return 0

        lax.fori_loop(0, npair, pair, 0)

    blk, row, st = _gdn_specs(t, ts, lambda s: s)
    return pl.pallas_call(
        body, name=name, grid=(GDN_HEADS // GDN_HP, t // ts), in_specs=[blk, blk, blk, blk, row, blk],
        out_specs=[blk, st],
        out_shape=[jax.ShapeDtypeStruct((t, 1024), F32), jax.ShapeDtypeStruct((GDN_HEADS, t // CHUNK, LANES, LANES), F32)],
        scratch_shapes=[pltpu.VMEM((GDN_HP, LANES, LANES), F32)],
        compiler_params=_params(("parallel", "arbitrary")),
    )(q, k, v, gcb, gct, bb)


def _gdn_bwd(q, k, v, gcb, gct, bb, do, states, *, name):
    t = q.shape[0]
    ts = _tile(t, GDN_TILE)
    npair = ts // PAIR
    ns = t // ts
    c = CHUNK

    def body(q_ref, k_ref, v_ref, g_ref, gt_ref, b_ref, do_ref, st_ref, dq_ref, dk_ref, dv_ref, dg_ref, db_ref, ds_sc):
        @pl.when(pl.program_id(1) == 0)
        def _():
            ds_sc[...] = jnp.zeros_like(ds_sc)

        rowsum = lambda x: jnp.sum(x, axis=-1, keepdims=True)
        total = lambda x: jnp.sum(rowsum(x), axis=0, keepdims=True)
        cat0 = lambda *xs: jnp.concatenate(xs, axis=0)
        cat1 = lambda *xs: jnp.concatenate(xs, axis=1)

        def pair(step, _):
            pi = npair - 1 - step
            rows = pl.ds(pl.multiple_of(pi * PAIR, PAIR), PAIR)
            heads = [slice(hh * LANES, (hh + 1) * LANES) for hh in range(GDN_HP)]
            hs = range(GDN_HP)
            qv, kv, vv = ([r[rows, sl] for sl in heads] for r in (q_ref, k_ref, v_ref))
            beta = [b_ref[rows, sl] for sl in heads]
            dov = [do_ref[rows, sl] for sl in heads]
            s0 = [st_ref[hh, 2 * pi] for hh in hs]
            s1 = [st_ref[hh, 2 * pi + 1] for hh in hs]
            ds2 = [ds_sc[hh] for hh in hs]
            cm = _pair_common(qv, kv, vv, [g_ref[rows, sl] for sl in heads], [gt_ref[hh, :, rows] for hh in hs], beta)
            u, w, qd, kd, attn = cm["u"], cm["w"], cm["qd"], cm["kd"], cm["attn"]
            tmat, gamma, eg = cm["tm"], cm["gamma"], cm["eg"]
            incl, strict = cm["incl"], cm["strict"]
            vn_a = _each(lambda u_, w_, s: u_[:c] - _bdot(w_[:c], s), u, w, s0)
            vn_b = _each(lambda u_, w_, s: u_[c:] - _bdot(w_[c:], s), u, w, s1)
            vn = _each(cat0, vn_a, vn_b)
            dvn_att = _each(lambda a, d: _bdot(a, d, TN), attn, dov)
            dattn = _each(lambda d, v_: jnp.where(incl, _bdot(d, v_, NT), 0.0), dov, vn)
            dvn_b = _each(lambda x, k_, d: x[c:] + _bdot(k_[c:], d), dvn_att, kd, ds2)
            rb = _each(lambda d, x, s: _bdot(cat0(d[c:], x), s, NT), dov, dvn_b, s1)
            dkd_b = _each(lambda v_, d: _bdot(v_, d, NT), vn_b, ds2)
            dgl_b = _each(lambda d, s: total(d * s), ds2, s1)
            ds1 = _each(lambda d, gl, q_, w_, o_, x: d * gl + _bdot(cat0(q_[c:], w_[c:]), cat0(o_[c:], -x), TN),
                        ds2, cm["glast_b"], qd, w, dov, dvn_b)
            dvn_a = _each(lambda x, k_, d: x[:c] + _bdot(k_[:c], d), dvn_att, kd, ds1)
            ra = _each(lambda d, x, s: _bdot(cat0(d[:c], x), s, NT), dov, dvn_a, s0)
            dkd_a = _each(lambda v_, d: _bdot(v_, d, NT), vn_a, ds1)
            dgl_a = _each(lambda d, s: total(d * s), ds1, s0)
            ds0 = _each(lambda d, gl, q_, w_, o_, x: d * gl + _bdot(cat0(q_[:c], w_[:c]), cat0(o_[:c], -x), TN),
                        ds1, cm["glast_a"], qd, w, dov, dvn_a)
            dvn = _each(cat0, dvn_a, dvn_b)
            dqd = _each(lambda a, b: cat0(a[:c], b[:c]), ra, rb)
            dw = _each(lambda a, b: -cat0(a[c:], b[c:]), ra, rb)
            dkd = _each(cat0, dkd_a, dkd_b)
            dvw = _each(cat1, dvn, dw)
            dvbk = _each(lambda t_, x: _bdot(t_, x, TN), tmat, dvw)
            dvb = _each(lambda x: x[:, :LANES], dvbk)
            dkbe = _each(lambda x: x[:, LANES:], dvbk)
            dt_ = _each(lambda x, a, b: _bdot(x, cat1(a, b), NT), dvw, cm["vb"], cm["kbe"])
            da1 = _each(lambda t_, x: _bdot(t_, x, TN), tmat, dt_)
            dm = _each(lambda x, t_: jnp.where(strict, -_bdot(x, t_, NT), 0.0), da1, tmat)
            dkk = _each(jnp.multiply, dm, gamma)
            dqk = _each(jnp.multiply, dattn, gamma)
            z = _each(lambda a, b, c_, d: a * b + c_ * d, dm, cm["m"], dattn, attn)
            dkb = _each(lambda x, k_, y, e: _bdot(x, k_) + y * e, dkk, kv, dkbe, eg)
            dk = _each(lambda a, b, kb_, q_, x, e, y, be: _bdot(cat0(a, b), cat0(kb_, q_), TN) + x * e + y * be,
                       dkk, dqk, cm["kb"], qv, dkd, cm["ek"], dkb, beta)
            dq = _each(lambda x, k_, y, e: _bdot(x, k_) + y * e, dqk, kv, dqd, eg)

            def colsum_of(z_):
                zh = z_.astype(BF16)
                zl = (z_ - zh.astype(F32)).astype(BF16)
                return _dot(cat0(zh, zl), jnp.ones((2 * PAIR, LANES), BF16), TN)

            colsum = _each(colsum_of, z)
            ri = lax.broadcasted_iota(jnp.int32, (PAIR, LANES), 0)
            for hh, sl in enumerate(heads):
                dkd_kd = dkd[hh] * kd[hh]
                dgc = (rowsum(z[hh]) - colsum[hh] + rowsum(dqd[hh] * qd[hh]) - rowsum(dkd_kd)
                       + rowsum(dkbe[hh] * cm["kbe"][hh]))
                last_a = total(dkd_kd[:c]) + dgl_a[hh] * cm["glast_a"][hh]
                last_b = total(dkd_kd[c:]) + dgl_b[hh] * cm["glast_b"][hh]
                dgc = dgc + jnp.where(ri == c - 1, last_a, 0.0) + jnp.where(ri == PAIR - 1, last_b, 0.0)
                ds_sc[hh] = ds0[hh]
                dq_ref[rows, sl] = dq[hh]
                dk_ref[rows, sl] = dk[hh]
                dv_ref[rows, sl] = dvb[hh] * beta[hh]
                db_ref[rows, sl] = jnp.broadcast_to(rowsum(dkb[hh] * kv[hh]) + rowsum(dvb[hh] * vv[hh]), (PAIR, LANES))
                dg_ref[rows, sl] = dgc
            return 0

        lax.fori_loop(0, npair, pair, 0)

    blk, row, st = _gdn_specs(t, ts, lambda s: ns - 1 - s)
    out = jax.ShapeDtypeStruct((t, 1024), F32)
    return pl.pallas_call(
        body, name=name, grid=(GDN_HEADS // GDN_HP, ns), in_specs=[blk, blk, blk, blk, row, blk, blk, st],
        out_specs=[blk] * 5, out_shape=[out] * 5, scratch_shapes=[pltpu.VMEM((GDN_HP, LANES, LANES), F32)],
        compiler_params=_params(("parallel", "arbitrary")),
    )(q, k, v, gcb, gct, bb, do, states)


def _gate_out_proj_loss(o, proj_c, o_norm, w, hres, g, target, *, name):
    t, d = hres.shape
    tm = _tile(t, 2 * ROW_TILE)

    def body(o_ref, z_ref, on_ref, w_ref, h_ref, g_ref, t_ref, dh_ref, dhb_ref, dg_ref, loss_ref, dw_ref, y_ref):
        i = pl.program_id(0)
        for hd in range(GDN_HEADS):
            sl = slice(hd * LANES, (hd + 1) * LANES)
            ov = o_ref[:, sl]
            rr = lax.rsqrt(jnp.mean(ov * ov, axis=-1, keepdims=True) + RMS_EPS)
            z = z_ref[:, sl]
            y_ref[:, sl] = (ov * rr * on_ref[...] * (z * _sigmoid(z))).astype(BF16)
        x = h_ref[...] + _dot(y_ref[...], w_ref[...])
        r = lax.rsqrt(jnp.mean(x * x, axis=-1, keepdims=True) + RMS_EPS)
        xh = x * r
        err = xh * g_ref[...] - t_ref[...]
        dy = err * (1.0 / d)
        dxh = dy * g_ref[...]
        dh = r * (dxh - xh * jnp.mean(dxh * xh, axis=-1, keepdims=True))
        dh_ref[...] = dh
        dhb_ref[...] = dh.astype(BF16)

        @pl.when(i == 0)
        def _():
            dg_ref[...] = jnp.zeros_like(dg_ref)
            loss_ref[...] = jnp.zeros_like(loss_ref)
            dw_ref[...] = jnp.zeros_like(dw_ref)

        dg_ref[...] += jnp.sum(dy * xh, axis=0, keepdims=True)
        part = 0.5 * jnp.sum(jnp.mean(err * err, axis=-1, keepdims=True), axis=0, keepdims=True)
        loss_ref[...] += jnp.broadcast_to(part, loss_ref.shape)
        dw_ref[...] += _dot(y_ref[...], dhb_ref[...], TN)

    row = pl.BlockSpec((tm, d), lambda i: (i, 0))
    vec = pl.BlockSpec((1, d), lambda i: (0, 0))
    return pl.pallas_call(
        body, name=name, grid=(t // tm,),
        in_specs=[row, pl.BlockSpec((tm, 1024), lambda i: (i, 3)), pl.BlockSpec((1, LANES), lambda i: (0, 0)),
                  pl.BlockSpec(w.shape, lambda i: (0, 0)), row, vec, row],
        out_specs=[row, row, vec, pl.BlockSpec((8, LANES), lambda i: (0, 0)), pl.BlockSpec(w.shape, lambda i: (0, 0))],
        out_shape=[jax.ShapeDtypeStruct((t, d), F32), jax.ShapeDtypeStruct((t, d), BF16),
                   jax.ShapeDtypeStruct((1, d), F32), jax.ShapeDtypeStruct((8, LANES), F32),
                   jax.ShapeDtypeStruct(w.shape, F32)],
        scratch_shapes=[pltpu.VMEM((tm, d), BF16)],
        compiler_params=_params(("arbitrary",)),
    )(o, proj_c, o_norm, w, hres, g, target)


def _pad_cols(w, n):
    return jnp.pad(w, ((0, 0), (0, n - w.shape[1])))


def _layout_odd(w):
    return dict(
        winc=_pad_cols(w["w_in_c"], IN_C_PAD).astype(BF16), wout_c=w["w_out_c"].astype(BF16), conv_w=w["conv_w"],
        a_log=_pad_cols(w["a_log"], LANES), dt_bias=_pad_cols(w["dt_bias"], LANES),
        norm_c=w["norm_c"], o_norm=w["o_norm"], final_norm=w["final_norm"],
    )


def _layout_in_ab(w):
    z = lambda r, c: jnp.zeros((r, c), w["w_in_ab"].dtype)
    wi = w["w_in_ab"]
    win = jnp.concatenate([wi[:, :384], z(1024, 64), wi[:, 384:416], z(1024, 32), wi[:, 416:]], axis=1)
    pw = w["pool_w"]
    rows = []
    for g in range(4):
        rows.append(jnp.concatenate([pw[g] if j == g else jnp.zeros((128, 128), F32) for j in range(4)], axis=1))
    wpool = jnp.concatenate(rows, axis=0)
    half = MLA_ROPE // 2
    inv = 1.0 / (ROPE_THETA ** (jnp.arange(half, dtype=F32) / half))
    inv_lane = jnp.concatenate([jnp.zeros((MLA_NOPE,), F32), inv, inv, jnp.zeros((32,), F32)]).reshape(1, LANES)
    return dict(win=win.astype(BF16), wpool=wpool.astype(BF16), inv_lane=inv_lane, norm_ab=w["norm_ab"],
                q_a_norm=w["q_a_norm"], kv_a_norm=w["kv_a_norm"], pool_scale=w["pool_scale"])


def _layout_mid(w):
    wq = jnp.pad(w["w_q_b"].reshape(MLA_Q_RANK, MLA_HEADS, 96), ((0, 0), (0, 0), (0, 32))).reshape(MLA_Q_RANK, 1024)
    kv3 = w["w_kv_b"].reshape(MLA_KV_RANK, MLA_HEADS, 128)
    wk = jnp.pad(kv3[..., :MLA_NOPE], ((0, 0), (0, 0), (0, 64))).reshape(MLA_KV_RANK, 1024)
    wv = kv3[..., MLA_NOPE:].reshape(MLA_KV_RANK, 512)
    return dict(wq=wq.astype(BF16), wk=wk.astype(BF16), wv=wv.astype(BF16), wout_ab=w["w_out_ab"].astype(BF16))


def _unlayout_grads(g, names):
    out = {}
    for name in names:
        if name == "w_in_ab":
            dwin = g["win"]
            out[name] = jnp.concatenate([dwin[:384], dwin[448:480], dwin[512:]], axis=0)
        elif name == "w_q_b":
            out[name] = g["wq"].reshape(MLA_Q_RANK, MLA_HEADS, 128)[..., :96].reshape(MLA_Q_RANK, 768)
        elif name == "w_kv_b":
            out[name] = jnp.concatenate([g["wk"].reshape(MLA_KV_RANK, MLA_HEADS, 128)[..., :MLA_NOPE],
                                         g["wv"].reshape(MLA_KV_RANK, MLA_HEADS, MLA_V)], axis=-1).reshape(MLA_KV_RANK, 1024)
        elif name == "w_in_c":
            out[name] = g["winc"][:4112]
        else:
            out[name] = g[{"w_out_ab": "wout_ab", "w_out_c": "wout_c"}[name]]
    return out


def _local_step(x, pos, target, lw, more_weights=None, on_grads=None):
    mm = _matmul
    proj, hn = _rms_in_proj(x, lw["norm_ab"], lw["win"], name="rms_in_ab")
    if more_weights is not None:
        lw = {**lw, **more_weights("mid", proj)}
    q, k, v, ybraw, qn, kvn, d, cos_t, sin_t = _ab_prep(
        proj, pos, lw["inv_lane"], lw["q_a_norm"], lw["kv_a_norm"], lw["wq"], lw["wk"], lw["wv"], lw["wpool"], name="ab_prep")
    o, lse = _attn_fwd(q, k, v, name="attn_fwd")
    h1, y = _gate_out_proj(o, ybraw, proj, lw["pool_scale"], lw["wout_ab"], x, name="gate_out_ab")
    lo = lw if more_weights is None else more_weights("odd", h1)
    proj_c, hn1 = _rms_in_proj(h1, lo["norm_c"], lo["winc"], name="rms_in_c")
    q2, k2, v2, gb, bb, gt = _c_prep(proj_c, lo["conv_w"], lo["a_log"], lo["dt_bias"], name="c_prep")
    gt = gt.reshape(GDN_HEADS, 1, gt.shape[1])
    o2, states = _gdn_fwd(q2, k2, v2, gb, gt, bb, name="gdn_fwd")
    dh2, dh2b, d_final, loss, d_wout_c = _gate_out_proj_loss(
        o2, proj_c, lo["o_norm"], lo["wout_c"], h1, lo["final_norm"], target, name="gate_out_c_loss")
    g = {"final_norm": d_final, "wout_c": d_wout_c}
    do2, dz2, g["o_norm"] = _o_gate_bwd(dh2b, lo["wout_c"], o2, proj_c, lo["o_norm"], name="gate_c_bwd")
    dq2, dk2, dv2, dgb, dbb = _gdn_bwd(q2, k2, v2, gb, gt, bb, do2, states, name="gdn_bwd")
    dproj_c, g["conv_w"], g["a_log"], g["dt_bias"] = _c_prep_bwd(
        proj_c, lo["conv_w"], lo["a_log"], lo["dt_bias"], dq2, dk2, dv2, dgb, dbb, dz2, name="c_prep_bwd")
    g["winc"] = mm(dproj_c, hn1, "tn", name="in_c_dw")
    notify = (lambda tag: 0.0) if on_grads is None else (lambda tag: on_grads(tag, g))
    pool_scale = lw["pool_scale"] + notify("odd")
    dh1, g["norm_c"], dh1b, g["wout_ab"] = _matmul_rms_bwd(
        dproj_c, lo["winc"], h1, lo["norm_c"], dh2, name="in_c_dx_rms", prev_y=y)
    pool_scale = pool_scale + notify("out_ab")
    do, delta, dyb, dz, g["pool_scale"] = _gate_bwd(dh1b, lw["wout_ab"], o, ybraw, proj, pool_scale, name="gate_ab_bwd")
    dq, dk, dv = _attn_bwd(q, k, v, do, lse, delta, name="attn_bwd")
    dproj, g["q_a_norm"], g["kv_a_norm"], g["wq"], g["wk"], g["wv"], g["wpool"], g["win"] = _ab_prep_bwd(
        proj, lw["q_a_norm"], lw["kv_a_norm"], dq, dk, dv, cos_t, sin_t, dyb, dz, qn, kvn, d, hn,
        lw["wq"], lw["wk"], lw["wv"], lw["wpool"], name="ab_prep_bwd")
    norm_ab = lw["norm_ab"] + notify("in_ab")
    dx, g["norm_ab"] = _matmul_rms_bwd(dproj, lw["win"], x, norm_ab, dh1, name="in_ab_dx_rms")
    return loss, dx, g


_HBM = pl.BlockSpec(memory_space=pltpu.HBM)


def _place():
    return lax.axis_index("x"), lax.axis_index("y"), lax.axis_index("c")


def _flip(v, f):
    return 1 - v if f else v


_CHIP_FLIPS = ((1, 0), (0, 1), (1, 1))
_DEV_FLIPS = tuple((fx, fy, fc) for fx in (0, 1) for fy in (0, 1) for fc in (0, 1) if fx or fy or fc)


def _rcopy(src, dst, send_sems, recv_sems, k, to):
    return pltpu.make_async_remote_copy(src_ref=src, dst_ref=dst, send_sem=send_sems.at[k], recv_sem=recv_sems.at[k],
                                        device_id=to, device_id_type=MESH)


def _my_half(ref, c, axis):
    rh = ref.shape[axis] // 2
    idx = [slice(None)] * len(ref.shape)
    idx[axis] = pl.ds(c * rh, rh)
    return ref.at[tuple(idx)]


def _gather_weights(bigs, smalls):
    nb, ns = len(bigs), len(smalls)

    def body(*refs):
        ins, outs = refs[:nb + ns], refs[nb + ns:2 * (nb + ns)]
        send_sems, recv_sems, local_sems = refs[2 * (nb + ns):]
        x, y, c = _place()
        j0 = 2 * x + y
        sib = (x, y, 1 - c)
        chips = [(_flip(x, fx), _flip(y, fy)) for fx, fy in _CHIP_FLIPS]
        local = [pltpu.make_async_copy(i_ref, o_ref.at[j0], local_sems.at[a])
                 for a, (i_ref, o_ref) in enumerate(zip(ins, outs))]
        for cp in local:
            cp.start()
        sends = []
        for k, (px, py) in enumerate(chips):
            for a in range(nb):
                sends.append(_rcopy(_my_half(ins[a], c, 0), _my_half(outs[a].at[j0], c, 0), send_sems, recv_sems,
                                    6 * a + k, (px, py, c)))
            for s in range(ns):
                sends.append(_rcopy(ins[nb + s], outs[nb + s].at[j0], send_sems, recv_sems, 6 * nb + 3 * s + k, (px, py, c)))
        for cp in sends:
            cp.start()
        for k, (px, py) in enumerate(chips):
            jk = 2 * px + py
            for a in range(nb):
                landed = _my_half(outs[a].at[jk], c, 0)
                _rcopy(landed, landed, send_sems, recv_sems, 6 * a + k, (px, py, c)).wait_recv()
                fwd = _rcopy(landed, landed, send_sems, recv_sems, 6 * a + 3 + k, sib)
                fwd.start()
                sends.append(fwd)
        for k, (px, py) in enumerate(chips):
            jk = 2 * px + py
            for a in range(nb):
                other = _my_half(outs[a].at[jk], 1 - c, 0)
                _rcopy(other, other, send_sems, recv_sems, 6 * a + 3 + k, sib).wait_recv()
            for s in range(ns):
                _rcopy(ins[nb + s], outs[nb + s].at[jk], send_sems, recv_sems, 6 * nb + 3 * s + k, (px, py, c)).wait_recv()
        for cp in sends:
            cp.wait_send()
        for cp in local:
            cp.wait()

    arrays = list(bigs) + list(smalls)
    n_sem = 6 * nb + 3 * ns
    return pl.pallas_call(
        body, name="gather_weights", in_specs=[_HBM] * len(arrays), out_specs=[_HBM] * len(arrays),
        out_shape=[jax.ShapeDtypeStruct((4,) + a.shape, a.dtype) for a in arrays],
        scratch_shapes=[pltpu.SemaphoreType.DMA((n_sem,)), pltpu.SemaphoreType.DMA((n_sem,)),
                        pltpu.SemaphoreType.DMA((len(arrays),))],
    )(*arrays)


def _core_swap_partial(gs, by_cols, *, name):
    n = len(gs)

    def body(*refs):
        ins, outs = refs[:n], refs[n:2 * n]
        send_sems, recv_sems = refs[2 * n:]
        x, y, c = _place()
        copies = [_rcopy(_my_half(i_ref, 1 - c, 2 if by_cols[a] else 1), o_ref, send_sems, recv_sems, a, (x, y, 1 - c))
                  for a, (i_ref, o_ref) in enumerate(zip(ins, outs))]
        for cp in copies:
            cp.start()
        for cp in copies:
            cp.wait()

    halved = lambda g, cols: (4, g.shape[1], g.shape[2] // 2) if cols else (4, g.shape[1] // 2, g.shape[2])
    return pl.pallas_call(
        body, name=name, in_specs=[_HBM] * n, out_specs=[_HBM] * n,
        out_shape=[jax.ShapeDtypeStruct(halved(g, cols), g.dtype) for g, cols in zip(gs, by_cols)],
        scratch_shapes=[pltpu.SemaphoreType.DMA((n,)), pltpu.SemaphoreType.DMA((n,))],
    )(*gs)


def _core_swap_sum(fs, by_cols):
    n = len(fs)

    def body(*refs):
        ins, outs = refs[:n], refs[n:2 * n]
        send_sems, recv_sems = refs[2 * n:]
        x, y, c = _place()
        axes = [1 if cols else 0 for cols in by_cols]
        copies = [_rcopy(_my_half(i_ref, c, ax), _my_half(o_ref, c, ax), send_sems, recv_sems, a, (x, y, 1 - c))
                  for a, (i_ref, o_ref, ax) in enumerate(zip(ins, outs, axes))]
        for cp in copies:
            cp.start()
        for a, cp in enumerate(copies):
            cp.wait_send()
            theirs = _my_half(outs[a], 1 - c, axes[a])
            _rcopy(theirs, theirs, send_sems, recv_sems, a, (x, y, 1 - c)).wait_recv()

    return pl.pallas_call(
        body, name="core_swap_sum", in_specs=[_HBM] * n, out_specs=[_HBM] * n,
        out_shape=[jax.ShapeDtypeStruct(f.shape, f.dtype) for f in fs],
        input_output_aliases={a: a for a in range(n)},
        scratch_shapes=[pltpu.SemaphoreType.DMA((n,)), pltpu.SemaphoreType.DMA((n,))],
    )(*fs)


_SEM = pl.BlockSpec(memory_space=pltpu.SEMAPHORE)
_ANY = pl.BlockSpec(memory_space=pl.ANY)
_DATAFLOW = pltpu.SideEffectType.DATAFLOW_SIDE_EFFECTING


def _to_chips_copies(srcs, lands, send_sems, recv_sems, per_chip_slot):
    x, y, c = _place()
    j0 = 2 * x + y
    out = []
    for k, (fx, fy) in enumerate(_CHIP_FLIPS):
        px, py = _flip(x, fx), _flip(y, fy)
        jk = 2 * px + py
        for a, (src, land) in enumerate(zip(srcs, lands)):
            piece = src.at[jk] if per_chip_slot else src
            out.append((_rcopy(piece, land.at[j0], send_sems, recv_sems, 3 * a + k, (px, py, c)),
                        _rcopy(piece, land.at[jk], send_sems, recv_sems, 3 * a + k, (px, py, c))))
    return out


def _to_chips_start(arrays, *, per_chip_slot, name, after=None):
    n = len(arrays)
    lands = [lax.empty((4,) + (a.shape[1:] if per_chip_slot else a.shape), a.dtype) for a in arrays]
    extra = [] if after is None else [after]

    def body(*refs):
        srcs, land_refs, token = refs[:n], refs[n:2 * n], refs[-1]
        send_sems, recv_sems = refs[2 * n + len(extra)], refs[2 * n + len(extra) + 1]
        for send, _ in _to_chips_copies(srcs, land_refs, send_sems, recv_sems, per_chip_slot):
            send.start()
        token[...] = jnp.zeros_like(token)

    held = [pltpu.with_memory_space_constraint(a, pltpu.HBM) for a in list(arrays) + lands]
    return pl.pallas_call(
        body, name=name, in_specs=[_HBM] * (2 * n) + [_ANY] * len(extra),
        out_specs=(_SEM, _SEM, *[_HBM] * (2 * n), pl.BlockSpec(memory_space=pltpu.VMEM)),
        out_shape=(pltpu.SemaphoreType.DMA((3 * n,)), pltpu.SemaphoreType.DMA((3 * n,)),
                   *[pltpu.HBM(a.shape, a.dtype) for a in held], jax.ShapeDtypeStruct((8, LANES), F32)),
        input_output_aliases={i: 2 + i for i in range(2 * n)},
        compiler_params=pltpu.CompilerParams(has_side_effects=_DATAFLOW),
    )(*held, *extra)


def _to_chips_wait(started, after, *, per_chip_slot, name):
    send_sems, recv_sems, held = started[0], started[1], started[2:-1]
    n = len(held) // 2

    def body(*refs):
        srcs, land_refs, s_sems, r_sems = refs[:n], refs[n:2 * n], refs[2 * n], refs[2 * n + 1]
        for send, arrival in _to_chips_copies(srcs, land_refs, s_sems, r_sems, per_chip_slot):
            send.wait_send()
            arrival.wait_recv()

    out = pl.pallas_call(
        body, name=name, in_specs=[_HBM] * (2 * n) + [_SEM, _SEM, _ANY], out_specs=[_HBM] * (2 * n),
        out_shape=[pltpu.HBM(a.shape, a.dtype) for a in held],
        input_output_aliases={i: i for i in range(2 * n)},
        compiler_params=pltpu.CompilerParams(has_side_effects=_DATAFLOW),
    )(*held, send_sems, recv_sems, after)
    return out[n:]


def _chip_exchange(ps, small):
    n = len(ps)
    rs = small.shape[0]

    def body(*refs):
        p_refs, s_ref = refs[:n], refs[n]
        l_refs, ls_ref = refs[n + 1:2 * n + 1], refs[2 * n + 1]
        send_sems, recv_sems, local_sems = refs[2 * n + 2:]
        x, y, c = _place()
        j0 = 2 * x + y
        d0 = 2 * j0 + c
        local = [pltpu.make_async_copy(p.at[j0], l.at[j0], local_sems.at[a]) for a, (p, l) in enumerate(zip(p_refs, l_refs))]
        local.append(pltpu.make_async_copy(s_ref, ls_ref.at[d0], local_sems.at[n]))
        for cp in local:
            cp.start()
        sends = []
        for k, (fx, fy) in enumerate(_CHIP_FLIPS):
            px, py = _flip(x, fx), _flip(y, fy)
            for a in range(n):
                sends.append(_rcopy(p_refs[a].at[2 * px + py], l_refs[a].at[j0], send_sems, recv_sems, 3 * a + k, (px, py, c)))
        for k, (fx, fy, fc) in enumerate(_DEV_FLIPS):
            peer = (_flip(x, fx), _flip(y, fy), _flip(c, fc))
            sends.append(_rcopy(s_ref, ls_ref.at[d0], send_sems, recv_sems, 3 * n + k, peer))
        for cp in sends:
            cp.start()
        for k, (fx, fy) in enumerate(_CHIP_FLIPS):
            px, py = _flip(x, fx), _flip(y, fy)
            for a in range(n):
                _rcopy(p_refs[a].at[j0], l_refs[a].at[2 * px + py], send_sems, recv_sems, 3 * a + k, (px, py, c)).wait_recv()
        for k, (fx, fy, fc) in enumerate(_DEV_FLIPS):
            px, py, pc = _flip(x, fx), _flip(y, fy), _flip(c, fc)
            _rcopy(s_ref, ls_ref.at[4 * px + 2 * py + pc], send_sems, recv_sems, 3 * n + k, (px, py, pc)).wait_recv()
        for cp in sends:
            cp.wait_send()
        for cp in local:
            cp.wait()

    n_sem = 3 * n + 7
    return pl.pallas_call(
        body, name="chip_exchange", in_specs=[_HBM] * (n + 1), out_specs=[_HBM] * (n + 1),
        out_shape=[jax.ShapeDtypeStruct(p.shape, F32) for p in ps] + [jax.ShapeDtypeStruct((8, rs, LANES), F32)],
        scratch_shapes=[pltpu.SemaphoreType.DMA((n_sem,)), pltpu.SemaphoreType.DMA((n_sem,)),
                        pltpu.SemaphoreType.DMA((n + 1,))],
    )(*ps, small)


def _half_blocks(rows, cols, by_cols):
    if by_cols:
        tc = _tile(cols // 2, 256)
        nb = cols // 2 // tc
        return rows, tc, nb, (lambda i, c: (0, c * nb + i))
    tr = _tile(rows // 2, 256)
    nb = rows // 2 // tr
    return tr, cols, nb, (lambda i, c: (c * nb + i, 0))


def _core_sum(g, part, core, *, name, by_cols):
    _, rows, cols = g.shape
    br, bc, nb, whole = _half_blocks(rows, cols, by_cols)
    mine = (lambda i: (0, i)) if by_cols else (lambda i: (i, 0))

    def body(c_ref, g_ref, p_ref, o_ref):
        o_ref[...] = g_ref[...] + p_ref[...]

    grid_spec = pltpu.PrefetchScalarGridSpec(
        num_scalar_prefetch=1, grid=(4, nb),
        in_specs=[pl.BlockSpec((1, br, bc), lambda j, i, c: (j,) + whole(i, c[0])),
                  pl.BlockSpec((1, br, bc), lambda j, i, c: (j,) + mine(i))],
        out_specs=pl.BlockSpec((1, br, bc), lambda j, i, c: (j,) + mine(i)),
    )
    return pl.pallas_call(
        body, name=name, grid_spec=grid_spec, out_shape=jax.ShapeDtypeStruct(part.shape, F32),
        compiler_params=_params(("parallel", "parallel")),
    )(core, g, part)


def _chip_sum(landed, core, *, name, by_cols):
    _, hr, hc = landed.shape
    rows, cols = (hr, 2 * hc) if by_cols else (2 * hr, hc)
    br, bc, nb, whole = _half_blocks(rows, cols, by_cols)
    mine = (lambda i: (0, i)) if by_cols else (lambda i: (i, 0))

    def body(c_ref, l_ref, o_ref):
        o_ref[...] = ((l_ref[0] + l_ref[1]) + l_ref[2]) + l_ref[3]

    grid_spec = pltpu.PrefetchScalarGridSpec(
        num_scalar_prefetch=1, grid=(nb,),
        in_specs=[pl.BlockSpec((4, br, bc), lambda i, c: (0,) + mine(i))],
        out_specs=pl.BlockSpec((br, bc), lambda i, c: whole(i, c[0])),
    )
    return pl.pallas_call(
        body, name=name, grid_spec=grid_spec, out_shape=jax.ShapeDtypeStruct((rows, cols), F32),
        compiler_params=_params(("parallel",)),
    )(core, landed)


_ROW_POOL_W, _ROW_NORM_AB, _ROW_FINAL, _ROW_POOL_SCALE, _ROW_Q_NORM = 0, 512, 520, 528, 532
_ROW_KV_NORM, _ROW_O_NORM, _ROW_A_LOG, _ROW_DT_BIAS, _ROW_LOSS = 534, 535, 536, 537, 538
_ROW_CONV, _ROW_NORM_C, _SMALL_ROWS = 544, 640, 672
_CONV_ROWS = CONV_WIDTH * 6


def _put_rows(dst_ref, row0, src, width):
    for r in range(width // LANES):
        dst_ref[row0 + r:row0 + r + 1, :] = src[:, r * LANES:(r + 1) * LANES]


def _pack_small(g, loss_tile):
    names = ("wpool", "norm_ab", "final_norm", "pool_scale", "q_a_norm", "kv_a_norm", "o_norm", "a_log", "dt_bias",
             "conv_w", "norm_c")

    def body(wpool, norm_ab, final_norm, pool_scale, q_norm, kv_norm, o_norm, a_log, dt_bias, conv_w, norm_c, loss, o_ref):
        o_ref[...] = jnp.zeros_like(o_ref)
        for gi in range(4):
            o_ref[_ROW_POOL_W + gi * 128:_ROW_POOL_W + (gi + 1) * 128, :] = wpool[gi * 128:(gi + 1) * 128, gi * 128:(gi + 1) * 128]
        _put_rows(o_ref, _ROW_NORM_AB, norm_ab[...], 1024)
        _put_rows(o_ref, _ROW_FINAL, final_norm[...], 1024)
        _put_rows(o_ref, _ROW_POOL_SCALE, pool_scale[...], 512)
        _put_rows(o_ref, _ROW_Q_NORM, q_norm[...], 256)
        for row, ref in ((_ROW_KV_NORM, kv_norm), (_ROW_O_NORM, o_norm), (_ROW_A_LOG, a_log), (_ROW_DT_BIAS, dt_bias)):
            o_ref[row:row + 1, :] = ref[...]
        o_ref[_ROW_LOSS:_ROW_LOSS + 1, :] = loss[0:1, :]
        for j in range(4):
            for r in range(CONV_WIDTH):
                _put_rows(o_ref, _ROW_CONV + j * _CONV_ROWS + r * 6, conv_w[r:r + 1, j * 768:(j + 1) * 768], 768)
            _put_rows(o_ref, _ROW_NORM_C + j * 8, norm_c[:, j * 256:(j + 1) * 256], 256)

    vmem = pl.BlockSpec(memory_space=pltpu.VMEM)
    return pl.pallas_call(
        body, name="pack_small", in_specs=[vmem] * 12, out_specs=vmem,
        out_shape=jax.ShapeDtypeStruct((_SMALL_ROWS, LANES), F32),
    )(*[g[n] for n in names], loss_tile)


_SMALL_NAMES = ("pool_w", "norm_ab", "final_norm", "pool_scale", "q_a_norm", "kv_a_norm", "o_norm", "a_log", "dt_bias",
                "conv_w", "norm_c")


def _take_rows(src, row0, width):
    return jnp.concatenate([src[row0 + r:row0 + r + 1, :] for r in range(width // LANES)], axis=1)


def _small_update(small_all, ws, ms, vs):
    n = len(_SMALL_NAMES)

    def body(*refs):
        a_ref = refs[0]
        w_refs, m_refs, v_refs = refs[1:1 + n], refs[1 + n:1 + 2 * n], refs[1 + 2 * n:1 + 3 * n]
        outs = refs[1 + 3 * n:1 + 7 * n]
        loss_ref, tot = refs[1 + 7 * n], refs[2 + 7 * n]
        acc = a_ref[0]
        for d in range(1, 8):
            acc = acc + a_ref[d]
        tot[...] = acc
        x, y, _ = _place()
        j0 = 2 * x + y
        conv = tot[pl.ds(pl.multiple_of(_ROW_CONV + j0 * _CONV_ROWS, 8), _CONV_ROWS), :]
        norm_c = tot[pl.ds(pl.multiple_of(_ROW_NORM_C + j0 * 8, 8), 8), :]
        whole = tot[_ROW_NORM_AB:_ROW_CONV, :]
        at = lambda row: row - _ROW_NORM_AB
        grads = {
            "norm_ab": _take_rows(whole, at(_ROW_NORM_AB), 1024), "final_norm": _take_rows(whole, at(_ROW_FINAL), 1024),
            "pool_scale": _take_rows(whole, at(_ROW_POOL_SCALE), 512), "q_a_norm": _take_rows(whole, at(_ROW_Q_NORM), 256),
            "kv_a_norm": whole[at(_ROW_KV_NORM):at(_ROW_KV_NORM) + 1, :], "o_norm": whole[at(_ROW_O_NORM):at(_ROW_O_NORM) + 1, :],
            "a_log": tot[_ROW_A_LOG:_ROW_A_LOG + 1, 0:GDN_HEADS],
            "dt_bias": tot[_ROW_DT_BIAS:_ROW_DT_BIAS + 1, 0:GDN_HEADS],
            "norm_c": _take_rows(norm_c, 0, 256),
        }
        loss_ref[...] = whole[at(_ROW_LOSS):at(_ROW_LOSS) + 1, :]
        for i, name in enumerate(_SMALL_NAMES):
            g_out = outs[4 * i]
            if name == "pool_w":
                for gi in range(4):
                    g_out[gi] = tot[_ROW_POOL_W + gi * 128:_ROW_POOL_W + (gi + 1) * 128, :]
            elif name == "conv_w":
                for r in range(CONV_WIDTH):
                    g_out[r:r + 1, :] = _take_rows(conv, r * 6, 768)
            else:
                g_out[...] = grads[name]
            _adam_update(g_out, w_refs[i], m_refs[i], v_refs[i], *outs[4 * i + 1:4 * i + 4])

    vmem = pl.BlockSpec(memory_space=pltpu.VMEM)
    out_shape = [jax.ShapeDtypeStruct(w.shape, F32) for w in ws for _ in range(4)] + [jax.ShapeDtypeStruct((1, LANES), F32)]
    return pl.pallas_call(
        body, name="small_update", in_specs=[vmem] * (1 + 3 * n), out_specs=[vmem] * (4 * n + 1), out_shape=out_shape,
        scratch_shapes=[pltpu.VMEM((_SMALL_ROWS, LANES), F32)],
        compiler_params=pltpu.CompilerParams(vmem_limit_bytes=VMEM_LIMIT),
    )(small_all, *ws, *ms, *vs)


def _adam_update(g_ref, w_ref, m_ref, v_ref, d_ref, mo_ref, vo_ref):
    gv = g_ref[...]
    mn = ADAM_B1 * m_ref[...] + (1.0 - ADAM_B1) * gv
    vn = ADAM_B2 * v_ref[...] + (1.0 - ADAM_B2) * (gv * gv)
    mo_ref[...] = mn
    vo_ref[...] = vn
    c1 = 1.0 - ADAM_B1 ** ADAM_STEP
    c2 = 1.0 - ADAM_B2 ** ADAM_STEP
    d_ref[...] = -ADAM_LR * ((mn / c1) / (jnp.sqrt(vn / c2) + ADAM_EPS) + ADAM_WD * w_ref[...])


def _adamw_rows(g, w, m, v, *, name):
    rows, cols = g.shape
    if rows % LANES == 0:
        tr = _tile(rows, 512)
        blk, steps = pl.BlockSpec((tr, cols), lambda i: (i, 0)), rows // tr
    else:
        tc = _tile(cols, 256)
        blk, steps = pl.BlockSpec((rows, tc), lambda i: (0, i)), cols // tc

    def body(*refs):
        _adam_update(*refs)

    out = jax.ShapeDtypeStruct((rows, cols), F32)
    return pl.pallas_call(
        body, name=name, grid=(steps,), in_specs=[blk] * 4, out_specs=[blk] * 3, out_shape=[out] * 3,
        compiler_params=_params(("parallel",)),
    )(g, w, m, v)


_ADAM_ROWWISE = ("w_in_ab", "w_q_b", "w_kv_b", "w_out_ab", "w_in_c", "w_out_c")


_SHARD_AXIS = {"w_in_ab": 1, "w_q_b": 1, "w_kv_b": 1, "w_out_ab": 0, "w_in_c": 1, "w_out_c": 0, "conv_w": 1, "norm_c": 1}
_ALL_NAMES = ("norm_ab", "w_in_ab", "q_a_norm", "w_q_b", "kv_a_norm", "w_kv_b", "pool_w", "pool_scale", "w_out_ab",
              "norm_c", "w_in_c", "conv_w", "a_log", "dt_bias", "o_norm", "w_out_c", "final_norm")


def _join_shards(a, axis):
    _, r, c = a.shape
    return a.reshape(4 * r, c) if axis == 0 else jnp.transpose(a, (1, 0, 2)).reshape(r, 4 * c)


def _split_shards(a, axis):
    r, c = a.shape
    return a.reshape(4, r // 4, c) if axis == 0 else jnp.transpose(a.reshape(r, 4, c // 4), (1, 0, 2))


def kernel(x, positions, norm_ab, w_in_ab, q_a_norm, w_q_b, kv_a_norm, w_kv_b, pool_w, pool_scale, w_out_ab, norm_c, w_in_c, conv_w, a_log, dt_bias, o_norm, w_out_c, final_norm, loss_target, m_norm_ab, m_w_in_ab, m_q_a_norm, m_w_q_b, m_kv_a_norm, m_w_kv_b, m_pool_w, m_pool_scale, m_w_out_ab, m_norm_c, m_w_in_c, m_conv_w, m_a_log, m_dt_bias, m_o_norm, m_w_out_c, m_final_norm, v_norm_ab, v_w_in_ab, v_q_a_norm, v_w_q_b, v_kv_a_norm, v_w_kv_b, v_pool_w, v_pool_scale, v_w_out_ab, v_norm_c, v_w_in_c, v_conv_w, v_a_log, v_dt_bias, v_o_norm, v_w_out_c, v_final_norm):
    given = dict(locals())
    c = lax.axis_index("c")
    t = x.shape[1]

    def shard_of(prefix, name):
        a = given[prefix + name]
        return a.reshape(a.shape[1:]) if a.ndim > 2 else a.reshape(1, -1)

    big, big_even, big_odd, small_sharded = _ADAM_ROWWISE, _ADAM_ROWWISE[:4], _ADAM_ROWWISE[4:], ("conv_w", "norm_c")
    chip = 2 * lax.axis_index("x") + lax.axis_index("y")
    core = c.astype(jnp.int32).reshape(1)
    later = {"mid": big_even[1:], "odd": big_odd + small_sharded}
    travelling = {}

    def send(tag, after=None):
        shards = [shard_of("", n).astype(BF16) if n in big else shard_of("", n) for n in later[tag]]
        started = _to_chips_start(shards, per_chip_slot=False, name="gather_" + tag + "_start", after=after)
        travelling[tag] = (shards, started)
        return started[-1][0, 0]

    mid_sent = send("mid")
    gathered = _gather_weights([shard_of("", "w_in_ab").astype(BF16)], [])
    full = {"w_in_ab": _join_shards(gathered[0], _SHARD_AXIS["w_in_ab"])}
    for name in ("norm_ab", "q_a_norm", "kv_a_norm", "pool_w", "pool_scale"):
        full[name] = shard_of("", name)
    lw = _layout_in_ab(full)
    lw["norm_ab"] = lw["norm_ab"] + mid_sent

    def more_weights(tag, after):
        shards, started = travelling[tag]
        landed = _to_chips_wait(started, after, per_chip_slot=False, name="gather_" + tag + "_wait")
        w = {}
        for name, land, own in zip(later[tag], landed, shards):
            w[name] = _join_shards(lax.dynamic_update_index_in_dim(land, own, chip, 0), _SHARD_AXIS[name])
        if tag == "mid":
            out = _layout_mid(w)
            out["wq"] = out["wq"] + send("odd", after=landed[0]).astype(BF16)
            return out
        for name in ("a_log", "dt_bias", "o_norm", "final_norm"):
            w[name] = shard_of("", name)
        return _layout_odd(w)

    transposed = ("w_in_ab", "w_in_c")

    def chip_partials(names, grads, tag):
        by_cols = [n in transposed for n in names]
        slots = [_split_shards(grads[n], 0 if n in transposed else _SHARD_AXIS[n]) for n in names]
        partial = _core_swap_partial(slots, by_cols, name="core_swap_partial_" + tag)
        return [_core_sum(s, p, core, name="core_sum_" + n, by_cols=b) for n, s, p, b in zip(names, slots, partial, by_cols)]

    groups = {"odd": big_odd, "out_ab": ("w_out_ab",), "in_ab": ("w_in_ab", "w_q_b", "w_kv_b")}
    sent = {}

    def on_grads(tag, g):
        part = chip_partials(groups[tag], _unlayout_grads(g, groups[tag]), tag)
        sent[tag] = (part, _to_chips_start(part, per_chip_slot=True, name="exchange_" + tag + "_start"))
        return sent[tag][1][-1][0, 0]

    loss_tile, dx, g = _local_step(x[0], positions.reshape(t, 1), loss_target[0], lw, more_weights, on_grads)
    small_all = _chip_exchange([], _pack_small(g, loss_tile))[-1]
    halves = {}
    for tag, names in groups.items():
        part, started = sent[tag]
        landed = _to_chips_wait(started, small_all, per_chip_slot=True, name="exchange_" + tag + "_wait")
        for n, l, p in zip(names, landed, part):
            l = lax.dynamic_update_index_in_dim(l, lax.dynamic_index_in_dim(p, chip, 0, keepdims=False), chip, 0)
            halves[n] = _chip_sum(l, core, name="chip_sum_" + n, by_cols=n in transposed)
    gbig = dict(zip(big, _core_swap_sum([halves[n] for n in big], [n in transposed for n in big])))

    res = {}
    for name in big:
        operands = [gbig[name], shard_of("", name), shard_of("m_", name), shard_of("v_", name)]
        flip = name in transposed
        if flip:
            operands[1:] = [jnp.transpose(a) for a in operands[1:]]
        out = (operands[0],) + tuple(_adamw_rows(*operands, name="adamw_" + name))
        out = [jnp.transpose(a) for a in out] if flip else out
        res["grad", name], res["delta", name], res["m", name], res["v", name] = out
    out = _small_update(small_all, [shard_of("", n) for n in _SMALL_NAMES], [shard_of("m_", n) for n in _SMALL_NAMES],
                        [shard_of("v_", n) for n in _SMALL_NAMES])
    for i, name in enumerate(_SMALL_NAMES):
        res["grad", name], res["delta", name], res["m", name], res["v", name] = out[4 * i:4 * i + 4]
    res = {k: a.reshape(given[k[1]].shape) for k, a in res.items()}
    loss = out[-1][0, 0]
    outs = [loss, dx.reshape(x.shape)]
    for key in ("grad", "delta", "m", "v"):
        outs += [res[key, n] for n in _ALL_NAMES]
    return tuple(outs)
```

```python
import functools

import jax
import jax.numpy as jnp
from jax import lax
from jax.experimental import pallas as pl
from jax.experimental.pallas import tpu as pltpu

F32 = jnp.float32
BF16 = jnp.bfloat16
HI = lax.Precision.HIGHEST
MESH = pl.DeviceIdType.MESH

RMS_EPS = 1e-6
MLA_HEADS = 8
MLA_Q_RANK = 256
MLA_KV_RANK = 128
MLA_NOPE = 64
MLA_ROPE = 32
MLA_V = 64
ROPE_THETA = 10000.0
POOL_WINDOWS = (2, 4, 8, 16)
POOL_GROUP = 128
POOL_WIDTH = 512
POOL_HALO = 16
GDN_HEADS = 8
GDN_DK = 128
CONV_WIDTH = 4
CONV_HALO = 8
CHUNK = 64
IN_AB_PAD = 2048
IN_C_PAD = 4224
ATT_SCALE = (MLA_NOPE + MLA_ROPE) ** -0.5
LOG2E = 1.4426950408889634

ADAM_LR = 0.001
ADAM_B1 = 0.9
ADAM_B2 = 0.999
ADAM_EPS = 1e-08
ADAM_WD = 0.01
ADAM_STEP = 10

LANES = 128
VMEM_LIMIT = 56 * 1024 * 1024

ROW_TILE = 256
ATT_TILE = 1024
GDN_TILE = 256
MM_TILE = (1024, 1408, 2048)

NN = (((1,), (0,)), ((), ()))
NT = (((1,), (1,)), ((), ()))
TN = (((0,), (0,)), ((), ()))


def _dot(a, b, dims=NN, prec=None):
    return lax.dot_general(a, b, dims, precision=prec, preferred_element_type=F32)


def _tile(n, pref):
    if n <= pref:
        return n
    step = LANES if pref >= LANES else 8
    for t in range(pref - pref % step, 0, -step):
        if n % t == 0:
            return t
    return n


def _params(sem):
    return pltpu.CompilerParams(dimension_semantics=sem, vmem_limit_bytes=VMEM_LIMIT)


def _sigmoid(x):
    return 0.5 * jnp.tanh(0.5 * x) + 0.5


def _softplus(x):
    return jnp.maximum(x, 0.0) + jnp.log(1.0 + jnp.exp(-jnp.abs(x)))


def _matmul(a, b, mode, *, name):
    if mode == "nn":
        (m, k), (k2, n) = a.shape, b.shape
    elif mode == "nt":
        (m, k), (n, k2) = a.shape, b.shape
    else:
        (k, m), (k2, n) = a.shape, b.shape
    assert k == k2, (a.shape, b.shape, mode)
    tm, tn, tk = _tile(m, MM_TILE[0]), _tile(n, MM_TILE[1]), _tile(k, MM_TILE[2])
    nk = k // tk
    if mode == "tn":
        a_spec = pl.BlockSpec((tk, tm), lambda i, j, kk: (kk, i))
    else:
        a_spec = pl.BlockSpec((tm, tk), lambda i, j, kk: (i, kk))
    if mode == "nt":
        b_spec = pl.BlockSpec((tn, tk), lambda i, j, kk: (j, kk))
    else:
        b_spec = pl.BlockSpec((tk, tn), lambda i, j, kk: (kk, j))
    o_spec = pl.BlockSpec((tm, tn), lambda i, j, kk: (i, j))
    dims = {"nn": NN, "nt": NT, "tn": TN}[mode]

    def body(a_ref, b_ref, o_ref, *scratch):
        if nk == 1:
            o_ref[...] = _dot(a_ref[...], b_ref[...], dims)
            return
        acc = scratch[0]
        kk = pl.program_id(2)

        @pl.when(kk == 0)
        def _():
            acc[...] = jnp.zeros_like(acc)

        acc[...] += _dot(a_ref[...], b_ref[...], dims)

        @pl.when(kk == nk - 1)
        def _():
            o_ref[...] = acc[...]

    return pl.pallas_call(
        body, name=name, grid=(m // tm, n // tn, nk), in_specs=[a_spec, b_spec], out_specs=o_spec,
        out_shape=jax.ShapeDtypeStruct((m, n), F32),
        scratch_shapes=[pltpu.VMEM((tm, tn), F32)] if nk > 1 else [],
        compiler_params=_params(("parallel", "parallel", "arbitrary")),
    )(a, b)


def _rms_in_proj(h, g, w, *, name):
    t, d = h.shape
    n = w.shape[1]
    tm, tn = _tile(t, MM_TILE[0]), _tile(n, MM_TILE[1])

    def body(h_ref, g_ref, w_ref, o_ref, hn_ref):
        @pl.when(pl.program_id(1) == 0)
        def _():
            x = h_ref[...]
            r = lax.rsqrt(jnp.mean(x * x, axis=-1, keepdims=True) + RMS_EPS)
            hn_ref[...] = (x * r * g_ref[...]).astype(BF16)

        o_ref[...] = _dot(hn_ref[...], w_ref[...])

    return pl.pallas_call(
        body, name=name, grid=(t // tm, n // tn),
        in_specs=[pl.BlockSpec((tm, d), lambda i, j: (i, 0)), pl.BlockSpec((1, d), lambda i, j: (0, 0)),
                  pl.BlockSpec((d, tn), lambda i, j: (0, j))],
        out_specs=[pl.BlockSpec((tm, tn), lambda i, j: (i, j)), pl.BlockSpec((tm, d), lambda i, j: (i, 0))],
        out_shape=[jax.ShapeDtypeStruct((t, n), F32), jax.ShapeDtypeStruct((t, d), BF16)],
        compiler_params=_params(("parallel", "arbitrary")),
    )(h, g, w)


def _matmul_rms_bwd(dproj, w, h, g, dres, *, name, prev_y=None):
    t, k = dproj.shape
    d = w.shape[0]
    tm = _tile(t, 2 * ROW_TILE)
    chained = prev_y is not None

    def body(dp_ref, w_ref, h_ref, g_ref, dres_ref, *rest):
        i = pl.program_id(0)
        dh_ref, dg_ref = rest[-4:-2] if chained else rest
        dyv = _dot(dp_ref[...], w_ref[...], NT)
        x = h_ref[...]
        r = lax.rsqrt(jnp.mean(x * x, axis=-1, keepdims=True) + RMS_EPS)
        xh = x * r
        dxh = dyv * g_ref[...]
        dh = dres_ref[...] + r * (dxh - xh * jnp.mean(dxh * xh, axis=-1, keepdims=True))
        dh_ref[...] = dh

        @pl.when(i == 0)
        def _():
            dg_ref[...] = jnp.zeros_like(dg_ref)
            if chained:
                rest[-1][...] = jnp.zeros_like(rest[-1])

        dg_ref[...] += jnp.sum(dyv * xh, axis=0, keepdims=True)
        if chained:
            y_ref, dhb_ref, dw_ref = rest[0], rest[-2], rest[-1]
            dhb_ref[...] = dh.astype(BF16)
            dw_ref[...] += _dot(y_ref[...], dhb_ref[...], TN)

    row = pl.BlockSpec((tm, d), lambda i: (i, 0))
    vec = pl.BlockSpec((1, d), lambda i: (0, 0))
    in_specs = [pl.BlockSpec((tm, k), lambda i: (i, 0)), pl.BlockSpec((d, k), lambda i: (0, 0)), row, vec, row]
    out_specs, out_shape = [row, vec], [jax.ShapeDtypeStruct((t, d), F32), jax.ShapeDtypeStruct((1, d), F32)]
    args = [dproj, w, h, g, dres]
    if chained:
        in_specs.append(row)
        args.append(prev_y)
        out_specs += [row, pl.BlockSpec((d, d), lambda i: (0, 0))]
        out_shape += [jax.ShapeDtypeStruct((t, d), BF16), jax.ShapeDtypeStruct((d, d), F32)]
    return pl.pallas_call(
        body, name=name, grid=(t // tm,), in_specs=in_specs, out_specs=out_specs, out_shape=out_shape,
        compiler_params=_params(("arbitrary",)),
    )(*args)


def _rope_partner(x):
    lane = lax.broadcasted_iota(jnp.int32, x.shape, 1)
    swapped = jnp.where(lane < MLA_NOPE + MLA_ROPE // 2, pltpu.roll(x, LANES - 16, 1), pltpu.roll(x, 16, 1))
    return jnp.where((lane >= MLA_NOPE) & (lane < MLA_NOPE + MLA_ROPE), swapped, 0.0)


def _pool_counts(row0, tm, w):
    t_idx = row0 + lax.broadcasted_iota(jnp.int32, (tm, POOL_GROUP), 0)
    return jnp.minimum(t_idx + 1, w).astype(F32)


def _ab_prep(proj, pos, inv_freq, q_a_norm, kv_a_norm, wq, wk, wv, wpool, *, name):
    t = proj.shape[0]
    tm = _tile(t, ROW_TILE)
    hb = tm // POOL_HALO

    def body(p_ref, halo_ref, pos_ref, inv_ref, qg_ref, kg_ref, wq_ref, wk_ref, wv_ref, wp_ref,
             q_ref, k_ref, v_ref, yb_ref, qn_ref, kvn_ref, d_ref, cos_ref, sin_ref, ext):
        i = pl.program_id(0)
        ql = p_ref[:, 0:MLA_Q_RANK]
        r = lax.rsqrt(jnp.mean(ql * ql, axis=-1, keepdims=True) + RMS_EPS)
        qn = (ql * r * qg_ref[...]).astype(BF16)
        qn_ref[...] = qn
        kl = p_ref[:, MLA_Q_RANK:MLA_Q_RANK + MLA_KV_RANK]
        r = lax.rsqrt(jnp.mean(kl * kl, axis=-1, keepdims=True) + RMS_EPS)
        kvn = (kl * r * kg_ref[...]).astype(BF16)
        kvn_ref[...] = kvn
        ang = pos_ref[...].astype(F32) * inv_ref[...]
        lane = lax.broadcasted_iota(jnp.int32, (tm, LANES), 1)
        in_rope = (lane >= MLA_NOPE) & (lane < MLA_NOPE + MLA_ROPE)
        cos_t = jnp.where(in_rope, jnp.cos(ang), 1.0)
        sin_t = jnp.where(in_rope, jnp.sin(ang), 0.0)
        sin_t = jnp.where(lane < MLA_NOPE + MLA_ROPE // 2, -sin_t, sin_t)
        cos_ref[...] = cos_t
        sin_ref[...] = sin_t
        kr = p_ref[:, 384:512]
        kr = kr * cos_t + _rope_partner(kr) * sin_t
        qraw = _dot(qn, wq_ref[...])
        kvk = _dot(kvn, wk_ref[...])
        for h in range(MLA_HEADS):
            sl = slice(h * LANES, (h + 1) * LANES)
            qh = qraw[:, sl]
            q_ref[:, sl] = ((qh * cos_t + _rope_partner(qh) * sin_t) * (ATT_SCALE * LOG2E)).astype(BF16)
            k_ref[:, sl] = (kvk[:, sl] + kr).astype(BF16)
        v_ref[...] = _dot(kvn, wv_ref[...]).astype(BF16)
        xp = p_ref[:, 512:1024]
        ext[0:POOL_HALO, :] = jnp.where(i > 0, halo_ref[...], 0.0)
        ext[POOL_HALO:POOL_HALO + tm, :] = xp
        for g, w in enumerate(POOL_WINDOWS):
            lo = g * POOL_GROUP
            acc = ext[POOL_HALO:POOL_HALO + tm, lo:lo + POOL_GROUP]
            for s in range(1, w):
                acc = acc + ext[POOL_HALO - s:POOL_HALO - s + tm, lo:lo + POOL_GROUP]
            cnt = _pool_counts(i * tm, tm, w)
            d_ref[:, lo:lo + POOL_GROUP] = (acc / cnt - xp[:, lo:lo + POOL_GROUP]).astype(BF16)
        yb_ref[...] = _dot(d_ref[...], wp_ref[...])

    row = lambda w: pl.BlockSpec((tm, w), lambda i: (i, 0))
    vec = lambda w: pl.BlockSpec((1, w), lambda i: (0, 0))
    whole = lambda a: pl.BlockSpec(a.shape, lambda i: (0, 0))
    return pl.pallas_call(
        body, name=name, grid=(t // tm,),
        in_specs=[row(1024), pl.BlockSpec((POOL_HALO, POOL_WIDTH), lambda i: (jnp.maximum(i * hb - 1, 0), 1)),
                  pl.BlockSpec((tm, 1), lambda i: (i, 0)), vec(LANES), vec(MLA_Q_RANK), vec(MLA_KV_RANK),
                  whole(wq), whole(wk), whole(wv), whole(wpool)],
        out_specs=[row(1024), row(1024), row(512), row(512), row(MLA_Q_RANK), row(MLA_KV_RANK), row(POOL_WIDTH),
                   row(LANES), row(LANES)],
        out_shape=[jax.ShapeDtypeStruct((t, 1024), BF16), jax.ShapeDtypeStruct((t, 1024), BF16),
                   jax.ShapeDtypeStruct((t, 512), BF16), jax.ShapeDtypeStruct((t, 512), F32),
                   jax.ShapeDtypeStruct((t, MLA_Q_RANK), BF16), jax.ShapeDtypeStruct((t, MLA_KV_RANK), BF16),
                   jax.ShapeDtypeStruct((t, POOL_WIDTH), BF16), jax.ShapeDtypeStruct((t, LANES), F32),
                   jax.ShapeDtypeStruct((t, LANES), F32)],
        scratch_shapes=[pltpu.VMEM((tm + POOL_HALO, POOL_WIDTH), F32)],
        compiler_params=_params(("parallel",)),
    )(proj, proj, pos, inv_freq, q_a_norm, kv_a_norm, wq, wk, wv, wpool)


def _ab_prep_bwd(proj, q_a_norm, kv_a_norm, dq, dk, dv, cos_t, sin_t, dyb, dz, qn, kvn, d, hn, wq, wk, wv, wpool, *, name):
    t = proj.shape[0]
    tm = _tile(t, ROW_TILE)
    hb = tm // POOL_HALO
    last_halo = t // POOL_HALO - 1
    nt = t // tm

    def body(p_ref, qg_ref, kg_ref, dq_ref, dk_ref, dv_ref, c_ref, s_ref, dyb_ref, dybn_ref, dz_ref,
             qn_ref, kvn_ref, d_ref, hn_ref, wq_ref, wk_ref, wv_ref, wp_ref,
             dp_ref, dqg_ref, dkg_ref, dwq_ref, dwk_ref, dwv_ref, dwp_ref, dwin_ref, ext, dqr_ref, dkb_ref):
        i = pl.program_id(0)

        @pl.when(i == 0)
        def _():
            for ref in (dqg_ref, dkg_ref, dwq_ref, dwk_ref, dwv_ref, dwp_ref, dwin_ref):
                ref[...] = jnp.zeros_like(ref)

        def norm_bwd(x, g, dy, dg_ref):
            r = lax.rsqrt(jnp.mean(x * x, axis=-1, keepdims=True) + RMS_EPS)
            xh = x * r
            dxh = dy * g
            dg_ref[...] += jnp.sum(dy * xh, axis=0, keepdims=True)
            return r * (dxh - xh * jnp.mean(dxh * xh, axis=-1, keepdims=True))

        c, s = c_ref[...], s_ref[...]
        lane = lax.broadcasted_iota(jnp.int32, (tm, LANES), 1)
        in_rope = (lane >= MLA_NOPE) & (lane < MLA_NOPE + MLA_ROPE)
        dkr = jnp.zeros((tm, LANES), F32)
        for h in range(MLA_HEADS):
            sl = slice(h * LANES, (h + 1) * LANES)
            g = dq_ref[:, sl]
            dqr_ref[:, sl] = ((g * c + _rope_partner(g * s)) * ATT_SCALE).astype(BF16)
            gk = dk_ref[:, sl]
            dkb_ref[:, sl] = gk.astype(BF16)
            dkr = dkr + jnp.where(in_rope, gk, 0.0)
        dkr = dkr * c + _rope_partner(dkr * s)
        dqn = _dot(dqr_ref[...], wq_ref[...], NT)
        dkvn = _dot(dkb_ref[...], wk_ref[...], NT) + _dot(dv_ref[...], wv_ref[...], NT)
        dql = norm_bwd(p_ref[:, 0:MLA_Q_RANK], qg_ref[...], dqn, dqg_ref)
        dp_ref[:, 0:MLA_Q_RANK] = dql.astype(BF16)
        dkl = norm_bwd(p_ref[:, MLA_Q_RANK:384], kg_ref[...], dkvn, dkg_ref)
        dp_ref[:, MLA_Q_RANK:384] = dkl.astype(BF16)
        dp_ref[:, 384:512] = dkr.astype(BF16)
        ddv = _dot(dyb_ref[...], wp_ref[...], NT)
        ddn = _dot(dybn_ref[...], wp_ref[...], NT)
        for g, w in enumerate(POOL_WINDOWS):
            lo = g * POOL_GROUP
            ext[0:tm, lo:lo + POOL_GROUP] = ddv[:, lo:lo + POOL_GROUP] / _pool_counts(i * tm, tm, w)
            nxt = ddn[:, lo:lo + POOL_GROUP] / _pool_counts((i + 1) * tm, POOL_HALO, w)
            ext[tm:tm + POOL_HALO, lo:lo + POOL_GROUP] = jnp.where(i < nt - 1, nxt, 0.0)
        for g, w in enumerate(POOL_WINDOWS):
            lo = g * POOL_GROUP
            acc = ext[0:tm, lo:lo + POOL_GROUP]
            for s in range(1, w):
                acc = acc + ext[s:s + tm, lo:lo + POOL_GROUP]
            dp_ref[:, 512 + lo:512 + lo + POOL_GROUP] = (acc - ddv[:, lo:lo + POOL_GROUP]).astype(BF16)
        dp_ref[:, 1024:2048] = dz_ref[...]
        dwq_ref[...] += _dot(qn_ref[...], dqr_ref[...], TN)
        dwk_ref[...] += _dot(kvn_ref[...], dkb_ref[...], TN)
        dwv_ref[...] += _dot(kvn_ref[...], dv_ref[...], TN)
        dwp_ref[...] += _dot(d_ref[...], dyb_ref[...], TN)
        dwin_ref[...] += _dot(dp_ref[...], hn_ref[...], TN)

    row = lambda w: pl.BlockSpec((tm, w), lambda i: (i, 0))
    vec = lambda w: pl.BlockSpec((1, w), lambda i: (0, 0))
    whole = lambda a: pl.BlockSpec(a.shape, lambda i: (0, 0))
    weights = (wq, wk, wv, wpool)
    return pl.pallas_call(
        body, name=name, grid=(nt,),
        in_specs=[row(1024), vec(MLA_Q_RANK), vec(MLA_KV_RANK), row(1024), row(1024), row(512), row(LANES), row(LANES),
                  row(POOL_WIDTH),
                  pl.BlockSpec((POOL_HALO, POOL_WIDTH), lambda i: (jnp.minimum((i + 1) * hb, last_halo), 0)),
                  row(1024), row(MLA_Q_RANK), row(MLA_KV_RANK), row(POOL_WIDTH), row(1024)] + [whole(w) for w in weights],
        out_specs=[row(IN_AB_PAD), vec(MLA_Q_RANK), vec(MLA_KV_RANK)] + [whole(w) for w in weights]
        + [pl.BlockSpec((IN_AB_PAD, 1024), lambda i: (0, 0))],
        out_shape=[jax.ShapeDtypeStruct((t, IN_AB_PAD), BF16), jax.ShapeDtypeStruct((1, MLA_Q_RANK), F32),
                   jax.ShapeDtypeStruct((1, MLA_KV_RANK), F32)] + [jax.ShapeDtypeStruct(w.shape, F32) for w in weights]
        + [jax.ShapeDtypeStruct((IN_AB_PAD, 1024), F32)],
        scratch_shapes=[pltpu.VMEM((tm + POOL_HALO, POOL_WIDTH), F32), pltpu.VMEM((tm, 1024), BF16),
                        pltpu.VMEM((tm, 1024), BF16)],
        compiler_params=_params(("arbitrary",)),
    )(proj, q_a_norm, kv_a_norm, dq, dk, dv, cos_t, sin_t, dyb, dyb, dz, qn, kvn, d, hn, wq, wk, wv, wpool)


def _gate_out_proj(o, ybraw, proj, pool_scale, w, hres, *, name):
    t = o.shape[0]
    tm = _tile(t, 2 * ROW_TILE)

    def body(o_ref, yb_ref, z_ref, ps_ref, w_ref, h_ref, ho_ref, y_ref):
        z = z_ref[...]
        sz = z * _sigmoid(z)
        y_ref[:, 0:512] = (o_ref[...] * sz[:, 0:512]).astype(BF16)
        y_ref[:, 512:1024] = (yb_ref[...] * ps_ref[...] * sz[:, 512:1024]).astype(BF16)
        ho_ref[...] = h_ref[...] + _dot(y_ref[...], w_ref[...])

    row = lambda w_: pl.BlockSpec((tm, w_), lambda i: (i, 0))
    return pl.pallas_call(
        body, name=name, grid=(t // tm,),
        in_specs=[row(512), row(512), pl.BlockSpec((tm, 1024), lambda i: (i, 1)), pl.BlockSpec((1, 512), lambda i: (0, 0)),
                  pl.BlockSpec(w.shape, lambda i: (0, 0)), row(1024)],
        out_specs=[row(1024), row(1024)],
        out_shape=[jax.ShapeDtypeStruct((t, 1024), F32), jax.ShapeDtypeStruct((t, 1024), BF16)],
        compiler_params=_params(("parallel",)),
    )(o, ybraw, proj, pool_scale, w, hres)


def _gate_bwd(dh, w, o, ybraw, proj, pool_scale, *, name):
    t = o.shape[0]
    tm = _tile(t, ROW_TILE)

    def body(dh_ref, w_ref, o_ref, yb_ref, z_ref, ps_ref, do_ref, dl_ref, dyb_ref, dz_ref, dps_ref):
        i = pl.program_id(0)
        z = z_ref[...]
        sg = _sigmoid(z)
        sz = z * sg
        dsz = sg * (1.0 + z * (1.0 - sg))
        dyv = _dot(dh_ref[...], w_ref[...], NT)
        dcat = dyv * sz
        ov = o_ref[...]
        ybs = yb_ref[...] * ps_ref[...]
        dz_ref[:, 0:512] = (dyv[:, 0:512] * ov * dsz[:, 0:512]).astype(BF16)
        dz_ref[:, 512:1024] = (dyv[:, 512:1024] * ybs * dsz[:, 512:1024]).astype(BF16)
        do = dcat[:, 0:512]
        do_ref[...] = do.astype(BF16)
        r_i = (lax.broadcasted_iota(jnp.int32, (1024, 512), 0) % 512) // MLA_V
        c_i = lax.broadcasted_iota(jnp.int32, (1024, 512), 1) // MLA_V
        prod = do * ov
        hi = prod.astype(BF16)
        lo = (prod - hi.astype(F32)).astype(BF16)
        dl_ref[...] = _dot(jnp.concatenate([hi, lo], axis=1), (r_i == c_i).astype(BF16))
        dyb_ref[...] = (dcat[:, 512:1024] * ps_ref[...]).astype(BF16)

        @pl.when(i == 0)
        def _():
            dps_ref[...] = jnp.zeros_like(dps_ref)

        dps_ref[...] += jnp.sum(dcat[:, 512:1024] * yb_ref[...], axis=0, keepdims=True)

    row = lambda w: pl.BlockSpec((tm, w), lambda i: (i, 0))
    vec = pl.BlockSpec((1, 512), lambda i: (0, 0))
    return pl.pallas_call(
        body, name=name, grid=(t // tm,),
        in_specs=[row(1024), pl.BlockSpec(w.shape, lambda i: (0, 0)), row(512), row(512),
                  pl.BlockSpec((tm, 1024), lambda i: (i, 1)), vec],
        out_specs=[row(512), row(512), row(512), row(1024), vec],
        out_shape=[jax.ShapeDtypeStruct((t, 512), BF16), jax.ShapeDtypeStruct((t, 512), F32),
                   jax.ShapeDtypeStruct((t, 512), BF16), jax.ShapeDtypeStruct((t, 1024), BF16),
                   jax.ShapeDtypeStruct((1, 512), F32)],
        compiler_params=_params(("arbitrary",)),
    )(dh, w, o, ybraw, proj, pool_scale)


ATT_HP_FWD = 4
ATT_HP_BWD = 2


def _diag_mask(tq):
    return lax.broadcasted_iota(jnp.int32, (tq, tq), 1) <= lax.broadcasted_iota(jnp.int32, (tq, tq), 0)


def _block_schedule(nq, key_major):
    if key_major:
        pairs = [(qi, ki) for ki in range(nq) for qi in range(ki, nq)]
    else:
        pairs = [(qi, ki) for qi in range(nq) for ki in range(qi + 1)]
    return jnp.asarray([p[0] for p in pairs], jnp.int32), jnp.asarray([p[1] for p in pairs], jnp.int32)


def _attn_fwd(q, k, v, *, name):
    t = q.shape[0]
    tq = _tile(t, ATT_TILE)
    nq = t // tq
    hp = ATT_HP_FWD
    qi_tab, ki_tab = _block_schedule(nq, key_major=False)

    def body(qi_ref, ki_ref, q_ref, k_ref, v_ref, o_ref, lse_ref, m_sc, l_sc, acc_sc):
        step = pl.program_id(1)
        qi, ki = qi_ref[step], ki_ref[step]

        @pl.when(ki == 0)
        def _():
            m_sc[...] = jnp.full_like(m_sc, -jnp.inf)
            l_sc[...] = jnp.zeros_like(l_sc)
            acc_sc[...] = jnp.zeros_like(acc_sc)

        def block(on_diagonal):
            scores = []
            for h in range(hp):
                sl = slice(h * LANES, (h + 1) * LANES)
                scores.append(_dot(q_ref[:, sl], k_ref[:, sl], NT))
            if on_diagonal:
                mask = _diag_mask(tq)
                scores = [jnp.where(mask, s, -jnp.inf) for s in scores]
            for h, s in enumerate(scores):
                vv = v_ref[:, (h // 2) * LANES:(h // 2 + 1) * LANES]
                m_prev = m_sc[h]
                m_new = jnp.maximum(m_prev, jnp.max(s, axis=-1, keepdims=True))
                alpha = jnp.exp2(m_prev - m_new)
                p = jnp.exp2(s - m_new[:, 0:1])
                l_sc[h] = alpha * l_sc[h] + jnp.sum(p, axis=-1, keepdims=True)
                acc_sc[h] = alpha * acc_sc[h] + _dot(p.astype(BF16), vv)
                m_sc[h] = m_new

        pl.when(ki < qi)(functools.partial(block, False))
        pl.when(ki == qi)(functools.partial(block, True))

        @pl.when(ki == qi)
        def _():
            first = lax.broadcasted_iota(jnp.int32, (tq, LANES), 1) < MLA_V
            for pr in range(hp // 2):
                a, b = 2 * pr, 2 * pr + 1
                sl = slice(pr * LANES, (pr + 1) * LANES)
                o_ref[:, sl] = jnp.where(first, acc_sc[a] / l_sc[a], acc_sc[b] / l_sc[b])
                lse_ref[:, sl] = jnp.where(first, m_sc[a] + jnp.log2(l_sc[a]), m_sc[b] + jnp.log2(l_sc[b]))

    grid_spec = pltpu.PrefetchScalarGridSpec(
        num_scalar_prefetch=2, grid=(MLA_HEADS // hp, qi_tab.shape[0]),
        in_specs=[pl.BlockSpec((tq, hp * LANES), lambda g, s, qt, kt: (qt[s], g)),
                  pl.BlockSpec((tq, hp * LANES), lambda g, s, qt, kt: (kt[s], g)),
                  pl.BlockSpec((tq, hp * MLA_V), lambda g, s, qt, kt: (kt[s], g))],
        out_specs=[pl.BlockSpec((tq, hp * MLA_V), lambda g, s, qt, kt: (qt[s], g)),
                   pl.BlockSpec((tq, hp * MLA_V), lambda g, s, qt, kt: (qt[s], g))],
        scratch_shapes=[pltpu.VMEM((hp, tq, LANES), F32)] * 3,
    )
    return pl.pallas_call(
        body, name=name, grid_spec=grid_spec,
        out_shape=[jax.ShapeDtypeStruct((t, 512), F32), jax.ShapeDtypeStruct((t, 512), F32)],
        compiler_params=_params(("parallel", "arbitrary")),
    )(qi_tab, ki_tab, q, k, v)


def _attn_bwd(q, k, v, do, lse, delta, *, name):
    t = q.shape[0]
    tq = _tile(t, ATT_TILE)
    nq = t // tq
    hp = ATT_HP_BWD
    qi_tab, ki_tab = _block_schedule(nq, key_major=True)

    def body(qi_ref, ki_ref, q_ref, k_ref, v_ref, do_ref, lse_ref, dl_ref, dq_ref, dk_ref, dv_ref, dk_sc, dv_sc):
        step = pl.program_id(1)
        qi, ki = qi_ref[step], ki_ref[step]

        @pl.when(step == 0)
        def _():
            dq_ref[...] = jnp.zeros_like(dq_ref)

        @pl.when(qi == ki)
        def _():
            dk_sc[...] = jnp.zeros_like(dk_sc)
            dv_sc[...] = jnp.zeros_like(dv_sc)

        def block(on_diagonal):
            lane = lax.broadcasted_iota(jnp.int32, (tq, LANES), 1)
            rows = pl.ds(pl.multiple_of(qi * tq, tq), tq)
            heads = [slice(h * LANES, (h + 1) * LANES) for h in range(hp)]
            scores = [_dot(q_ref[:, sl], k_ref[:, sl], NT) for sl in heads]
            dps = []
            for h in range(hp):
                dov = do_ref[:, (h // 2) * LANES:(h // 2 + 1) * LANES]
                mine = (lane < MLA_V) if h % 2 == 0 else (lane >= MLA_V)
                dps.append(_dot(jnp.where(mine, dov, jnp.zeros_like(dov)), v_ref[:, (h // 2) * LANES:(h // 2 + 1) * LANES], NT))
            mask = _diag_mask(tq) if on_diagonal else None
            for h, sl in enumerate(heads):
                col = (h // 2) * LANES + (h % 2) * MLA_V
                p = jnp.exp2(scores[h] - lse_ref[:, col:col + 1])
                if on_diagonal:
                    p = jnp.where(mask, p, 0.0)
                ds = (p * (dps[h] - dl_ref[:, col:col + 1])).astype(BF16)
                dv_sc[h] += _dot(p.astype(BF16), do_ref[:, (h // 2) * LANES:(h // 2 + 1) * LANES], TN)
                dk_sc[h] += _dot(ds, q_ref[:, sl], TN)
                dq_ref[rows, sl] += _dot(ds, k_ref[:, sl], NN)

        pl.when(qi > ki)(functools.partial(block, False))
        pl.when(qi == ki)(functools.partial(block, True))

        @pl.when(qi == nq - 1)
        def _():
            first = lax.broadcasted_iota(jnp.int32, (tq, LANES), 1) < MLA_V
            for h in range(hp):
                dk_ref[:, h * LANES:(h + 1) * LANES] = dk_sc[h] * (1.0 / LOG2E)
            for pr in range(hp // 2):
                dv_ref[:, pr * LANES:(pr + 1) * LANES] = jnp.where(first, dv_sc[2 * pr], dv_sc[2 * pr + 1]).astype(BF16)

    qrow = lambda w: pl.BlockSpec((tq, w), lambda g, s, qt, kt: (qt[s], g))
    krow = lambda w: pl.BlockSpec((tq, w), lambda g, s, qt, kt: (kt[s], g))
    grid_spec = pltpu.PrefetchScalarGridSpec(
        num_scalar_prefetch=2, grid=(MLA_HEADS // hp, qi_tab.shape[0]),
        in_specs=[qrow(hp * LANES), krow(hp * LANES), krow(hp * MLA_V), qrow(hp * MLA_V), qrow(hp * MLA_V), qrow(hp * MLA_V)],
        out_specs=[pl.BlockSpec((t, hp * LANES), lambda g, s, qt, kt: (0, g)), krow(hp * LANES), krow(hp * MLA_V)],
        scratch_shapes=[pltpu.VMEM((hp, tq, LANES), F32), pltpu.VMEM((hp, tq, LANES), F32)],
    )
    return pl.pallas_call(
        body, name=name, grid_spec=grid_spec,
        out_shape=[jax.ShapeDtypeStruct((t, 1024), F32), jax.ShapeDtypeStruct((t, 1024), F32),
                   jax.ShapeDtypeStruct((t, 512), BF16)],
        compiler_params=_params(("parallel", "arbitrary")),
    )(qi_tab, ki_tab, q, k, v, do, lse, delta)


def _conv_rows(ext, tm, w_ref, sec):
    c0 = sec * 1024
    y = ext[CONV_HALO - 3:CONV_HALO - 3 + tm, c0:c0 + 1024] * w_ref[0:1, c0:c0 + 1024]
    for j in range(1, CONV_WIDTH):
        y = y + ext[CONV_HALO - 3 + j:CONV_HALO - 3 + j + tm, c0:c0 + 1024] * w_ref[j:j + 1, c0:c0 + 1024]
    return y


def _c_prep(proj_c, conv_w, a_log, dt_bias, *, name):
    t = proj_c.shape[0]
    tm = _tile(t, ROW_TILE)
    hb = tm // CONV_HALO

    def body(p_ref, halo_ref, ab_ref, w_ref, al_ref, dtb_ref, q_ref, k_ref, v_ref, g_ref, b_ref, gt_ref, ext):
        i = pl.program_id(0)
        ext[0:CONV_HALO, :] = jnp.where(i > 0, halo_ref[...], 0.0)
        ext[CONV_HALO:CONV_HALO + tm, :] = p_ref[...]
        for sec, o_ref in enumerate((q_ref, k_ref, v_ref)):
            y = _conv_rows(ext, tm, w_ref, sec)
            y = y * _sigmoid(y)
            if sec == 2:
                o_ref[...] = y
                continue
            scale = GDN_DK ** -0.5 if sec == 0 else 1.0
            for h in range(GDN_HEADS):
                sl = slice(h * LANES, (h + 1) * LANES)
                blk = y[:, sl]
                r = lax.rsqrt(jnp.sum(blk * blk, axis=-1, keepdims=True) + RMS_EPS)
                o_ref[:, sl] = blk * (r * scale)
        ab = ab_ref[...]
        g = -jnp.exp(al_ref[...]) * _softplus(ab + dtb_ref[...])
        beta = _sigmoid(ab)
        ri = lax.broadcasted_iota(jnp.int32, (tm, tm), 0)
        ci = lax.broadcasted_iota(jnp.int32, (tm, tm), 1)
        lower = ((ri // CHUNK) == (ci // CHUNK)) & (ri >= ci)
        gc = _dot(lower.astype(F32), g, NN, HI)
        eye = lax.broadcasted_iota(jnp.int32, (LANES, LANES), 0) == lax.broadcasted_iota(jnp.int32, (LANES, LANES), 1)
        gt_ref[...] = _dot(eye.astype(F32), gc, NT, HI)[0:GDN_HEADS, :]
        for h in range(GDN_HEADS):
            sl = slice(h * LANES, (h + 1) * LANES)
            g_ref[:, sl] = jnp.broadcast_to(gc[:, h:h + 1], (tm, LANES))
            b_ref[:, sl] = jnp.broadcast_to(beta[:, GDN_HEADS + h:GDN_HEADS + h + 1], (tm, LANES))

    row = lambda w: pl.BlockSpec((tm, w), lambda i: (i, 0))
    vec = lambda r, w: pl.BlockSpec((r, w), lambda i: (0, 0))
    out = jax.ShapeDtypeStruct((t, 1024), F32)
    return pl.pallas_call(
        body, name=name, grid=(t // tm,),
        in_specs=[row(3072), pl.BlockSpec((CONV_HALO, 3072), lambda i: (jnp.maximum(i * hb - 1, 0), 0)),
                  pl.BlockSpec((tm, LANES), lambda i: (i, 32)), vec(CONV_WIDTH, 3072), vec(1, LANES), vec(1, LANES)],
        out_specs=[row(1024)] * 5 + [pl.BlockSpec((GDN_HEADS, tm), lambda i: (0, i))],
        out_shape=[out] * 5 + [jax.ShapeDtypeStruct((GDN_HEADS, t), F32)],
        scratch_shapes=[pltpu.VMEM((tm + CONV_HALO, 3072), F32)],
        compiler_params=_params(("parallel",)),
    )(proj_c, proj_c, proj_c, conv_w, a_log, dt_bias)


def _c_prep_bwd(proj_c, conv_w, a_log, dt_bias, dq, dk, dv, dgb, dbb, dz, *, name):
    t = proj_c.shape[0]
    tm = _tile(t, ROW_TILE)
    hb = tm // CONV_HALO
    nt = t // tm
    rev = lambda i: nt - 1 - i

    def body(p_ref, halo_ref, ab_ref, w_ref, al_ref, dtb_ref, dq_ref, dk_ref, dv_ref, dg_ref, db_ref, dz_ref,
             dp_ref, dw_ref, dal_ref, ddt_ref, ext, dyext, carry, taps):
        step = pl.program_id(0)
        i = rev(step)

        @pl.when(step == 0)
        def _():
            dw_ref[...] = jnp.zeros_like(dw_ref)
            dal_ref[...] = jnp.zeros_like(dal_ref)
            ddt_ref[...] = jnp.zeros_like(ddt_ref)
            carry[...] = jnp.zeros_like(carry)

        ext[0:CONV_HALO, :] = jnp.where(i > 0, halo_ref[...], 0.0)
        ext[CONV_HALO:CONV_HALO + tm, :] = p_ref[...]
        for sec, g_ref in enumerate((dq_ref, dk_ref, dv_ref)):
            c0 = sec * 1024
            for j in range(CONV_WIDTH):
                taps[j] = ext[CONV_HALO - 3 + j:CONV_HALO - 3 + j + tm, c0:c0 + 1024]
            y = taps[0] * w_ref[0:1, c0:c0 + 1024]
            for j in range(1, CONV_WIDTH):
                y = y + taps[j] * w_ref[j:j + 1, c0:c0 + 1024]
            sg = _sigmoid(y)
            act = y * sg
            if sec == 2:
                dact = g_ref[...]
            else:
                scale = GDN_DK ** -0.5 if sec == 0 else 1.0
                parts = []
                for h in range(GDN_HEADS):
                    sl = slice(h * LANES, (h + 1) * LANES)
                    blk = act[:, sl]
                    r = lax.rsqrt(jnp.sum(blk * blk, axis=-1, keepdims=True) + RMS_EPS)
                    n = blk * r
                    dn = g_ref[:, sl] * scale
                    parts.append(r * (dn - n * jnp.sum(dn * n, axis=-1, keepdims=True)))
                dact = jnp.concatenate(parts, axis=-1)
            dy = dact * (sg * (1.0 + y * (1.0 - sg)))
            dyext[0:tm, c0:c0 + 1024] = dy
            for j in range(CONV_WIDTH):
                dw_ref[j:j + 1, c0:c0 + 1024] += jnp.sum(dy * taps[j], axis=0, keepdims=True)
        dyext[tm:tm + CONV_HALO, :] = carry[...]
        carry[...] = dyext[0:CONV_HALO, :]
        for sec in range(3):
            c0 = sec * 1024
            dx = dyext[3:3 + tm, c0:c0 + 1024] * w_ref[0:1, c0:c0 + 1024]
            for j in range(1, CONV_WIDTH):
                dx = dx + dyext[3 - j:3 - j + tm, c0:c0 + 1024] * w_ref[j:j + 1, c0:c0 + 1024]
            dp_ref[:, c0:c0 + 1024] = dx.astype(BF16)
        dp_ref[:, 3072:4096] = dz_ref[...]
        lane = lax.broadcasted_iota(jnp.int32, (tm, LANES), 1)
        dg = jnp.zeros((tm, LANES), F32)
        dbeta = jnp.zeros((tm, LANES), F32)
        for h in range(GDN_HEADS):
            sl = slice(h * LANES, (h + 1) * LANES)
            dg = dg + jnp.where(lane == h, dg_ref[:, sl], 0.0)
            dbeta = dbeta + jnp.where(lane == GDN_HEADS + h, db_ref[:, sl], 0.0)
        ri = lax.broadcasted_iota(jnp.int32, (tm, tm), 0)
        ci = lax.broadcasted_iota(jnp.int32, (tm, tm), 1)
        upper = ((ri // CHUNK) == (ci // CHUNK)) & (ri <= ci)
        dg = _dot(upper.astype(F32), dg, NN, HI)
        pre = ab_ref[...] + dtb_ref[...]
        s = _sigmoid(pre)
        a_exp = jnp.exp(al_ref[...])
        dg_da = dg * (-a_exp * s)
        dp_ref[:, 4096:IN_C_PAD] = (dg_da + dbeta * s * (1.0 - s)).astype(BF16)
        dal_ref[...] += jnp.sum(dg * (-a_exp * _softplus(pre)), axis=0, keepdims=True)
        ddt_ref[...] += jnp.sum(dg_da, axis=0, keepdims=True)

    row = lambda w: pl.BlockSpec((tm, w), lambda s: (rev(s), 0))
    vec = lambda r, w: pl.BlockSpec((r, w), lambda s: (0, 0))
    return pl.pallas_call(
        body, name=name, grid=(nt,),
        in_specs=[row(3072), pl.BlockSpec((CONV_HALO, 3072), lambda s: (jnp.maximum(rev(s) * hb - 1, 0), 0)),
                  pl.BlockSpec((tm, LANES), lambda s: (rev(s), 32)), vec(CONV_WIDTH, 3072), vec(1, LANES), vec(1, LANES),
                  row(1024), row(1024), row(1024), row(1024), row(1024), row(1024)],
        out_specs=[row(IN_C_PAD), vec(CONV_WIDTH, 3072), vec(1, LANES), vec(1, LANES)],
        out_shape=[jax.ShapeDtypeStruct((t, IN_C_PAD), BF16), jax.ShapeDtypeStruct((CONV_WIDTH, 3072), F32),
                   jax.ShapeDtypeStruct((1, LANES), F32), jax.ShapeDtypeStruct((1, LANES), F32)],
        scratch_shapes=[pltpu.VMEM((tm + CONV_HALO, 3072), F32), pltpu.VMEM((tm + CONV_HALO, 3072), F32),
                        pltpu.VMEM((CONV_HALO, 3072), F32), pltpu.VMEM((CONV_WIDTH, tm, 1024), F32)],
        compiler_params=_params(("arbitrary",)),
    )(proj_c, proj_c, proj_c, conv_w, a_log, dt_bias, dq, dk, dv, dgb, dbb, dz)


def _o_gate_bwd(dh, w, o, proj_c, o_norm, *, name):
    t = o.shape[0]
    tm = _tile(t, 2 * ROW_TILE)

    def body(dh_ref, w_ref, o_ref, z_ref, g_ref, do_ref, dz_ref, dg_ref, dy_ref):
        i = pl.program_id(0)

        @pl.when(i == 0)
        def _():
            dg_ref[...] = jnp.zeros_like(dg_ref)

        dy_ref[...] = _dot(dh_ref[...], w_ref[...], NT)
        dg = jnp.zeros((1, LANES), F32)
        for h in range(GDN_HEADS):
            sl = slice(h * LANES, (h + 1) * LANES)
            x = o_ref[:, sl]
            r = lax.rsqrt(jnp.mean(x * x, axis=-1, keepdims=True) + RMS_EPS)
            xh = x * r
            z = z_ref[:, sl]
            sg = _sigmoid(z)
            dyv = dy_ref[:, sl]
            dn = dyv * (z * sg)
            dz_ref[:, sl] = (dyv * xh * g_ref[...] * (sg * (1.0 + z * (1.0 - sg)))).astype(BF16)
            dxh = dn * g_ref[...]
            do_ref[:, sl] = r * (dxh - xh * jnp.mean(dxh * xh, axis=-1, keepdims=True))
            dg = dg + jnp.sum(dn * xh, axis=0, keepdims=True)
        dg_ref[...] += dg

    row = pl.BlockSpec((tm, 1024), lambda i: (i, 0))
    vec = pl.BlockSpec((1, LANES), lambda i: (0, 0))
    return pl.pallas_call(
        body, name=name, grid=(t // tm,),
        in_specs=[row, pl.BlockSpec(w.shape, lambda i: (0, 0)), row, pl.BlockSpec((tm, 1024), lambda i: (i, 3)), vec],
        out_specs=[row, row, vec],
        out_shape=[jax.ShapeDtypeStruct((t, 1024), F32), jax.ShapeDtypeStruct((t, 1024), BF16),
                   jax.ShapeDtypeStruct((1, LANES), F32)],
        scratch_shapes=[pltpu.VMEM((tm, 1024), F32)],
        compiler_params=_params(("arbitrary",)),
    )(dh, w, o, proj_c, o_norm)


PAIR = 2 * CHUNK
GDN_HP = 8


def _bdot(a, b, dims=NN):
    return _dot(a.astype(BF16), b.astype(BF16), dims)


def _each(f, *lists):
    return [f(*args) for args in zip(*lists)]


def _pair_common(q, k, v, gci, gcj, beta):
    ri = lax.broadcasted_iota(jnp.int32, (PAIR, PAIR), 0)
    ci = lax.broadcasted_iota(jnp.int32, (PAIR, PAIR), 1)
    same = (ri // CHUNK) == (ci // CHUNK)
    incl = same & (ri >= ci)
    strict = same & (ri > ci)
    eye = (ri == ci).astype(F32)
    first = lax.broadcasted_iota(jnp.int32, (PAIR, LANES), 0) < CHUNK
    gamma = _each(lambda gi, gj: jnp.where(incl, jnp.exp(jnp.minimum(gi - gj, 0.0)), 0.0), gci, gcj)
    kb = _each(jnp.multiply, k, beta)
    kk = _each(lambda a, b: _bdot(a, b, NT), kb, k)
    qk = _each(lambda a, b: _bdot(a, b, NT), q, k)
    m = _each(lambda x, g: jnp.where(strict, x * g, 0.0), kk, gamma)
    tm_ = _each(lambda x: eye - x, m)
    pw = _each(lambda x: _bdot(x, x), m)
    for it in range(5):
        tm_ = _each(lambda x, p: x + _bdot(x, p), tm_, pw)
        if it < 4:
            pw = _each(lambda p: _bdot(p, p), pw)
    eg = _each(jnp.exp, gci)
    vb = _each(jnp.multiply, v, beta)
    kbe = _each(jnp.multiply, kb, eg)
    uw = _each(lambda x, a, b: _bdot(x, jnp.concatenate([a, b], axis=1)), tm_, vb, kbe)
    attn = _each(lambda x, g: jnp.where(incl, x * g, 0.0), qk, gamma)
    gl_a = _each(lambda g: g[CHUNK - 1:CHUNK, :], gci)
    gl_b = _each(lambda g: g[PAIR - 1:PAIR, :], gci)
    ek = _each(lambda a, b, g: jnp.exp(jnp.where(first, a, b) - g), gl_a, gl_b, gci)
    return dict(incl=incl, strict=strict, gamma=gamma, kb=kb, m=m, tm=tm_, eg=eg, vb=vb, kbe=kbe,
                u=_each(lambda x: x[:, :LANES], uw), w=_each(lambda x: x[:, LANES:], uw), attn=attn,
                qd=_each(jnp.multiply, q, eg), ek=ek, kd=_each(jnp.multiply, k, ek),
                glast_a=_each(jnp.exp, gl_a), glast_b=_each(jnp.exp, gl_b))


def _gdn_specs(t, ts, order):
    nc = ts // CHUNK
    blk = pl.BlockSpec((ts, GDN_HP * LANES), lambda h, s: (order(s), h))
    row = pl.BlockSpec((GDN_HP, 1, ts), lambda h, s: (h, 0, order(s)))
    st = pl.BlockSpec((GDN_HP, nc, LANES, LANES), lambda h, s: (h, order(s), 0, 0))
    return blk, row, st


def _gdn_fwd(q, k, v, gcb, gct, bb, *, name):
    t = q.shape[0]
    ts = _tile(t, GDN_TILE)
    npair = ts // PAIR

    def body(q_ref, k_ref, v_ref, g_ref, gt_ref, b_ref, o_ref, st_ref, s_sc):
        @pl.when(pl.program_id(1) == 0)
        def _():
            s_sc[...] = jnp.zeros_like(s_sc)

        def pair(pi, _):
            rows = pl.ds(pl.multiple_of(pi * PAIR, PAIR), PAIR)
            heads = [slice(hh * LANES, (hh + 1) * LANES) for hh in range(GDN_HP)]
            c = CHUNK
            cat0 = lambda *xs: jnp.concatenate(xs, axis=0)
            s0 = [s_sc[hh] for hh in range(GDN_HP)]
            cm = _pair_common([q_ref[rows, sl] for sl in heads], [k_ref[rows, sl] for sl in heads],
                              [v_ref[rows, sl] for sl in heads], [g_ref[rows, sl] for sl in heads],
                              [gt_ref[hh, :, rows] for hh in range(GDN_HP)], [b_ref[rows, sl] for sl in heads])
            u, w, qd, kd = cm["u"], cm["w"], cm["qd"], cm["kd"]
            r0 = _each(lambda w_, q_, s: _bdot(cat0(w_[:c], q_[:c]), s), w, qd, s0)
            vn_a = _each(lambda u_, r: u_[:c] - r[:c], u, r0)
            s1 = _each(lambda s, gl, k_, vn: s * gl + _bdot(k_[:c], vn, TN), s0, cm["glast_a"], kd, vn_a)
            r1 = _each(lambda w_, q_, s: _bdot(cat0(w_[c:], q_[c:]), s), w, qd, s1)
            vn_b = _each(lambda u_, r: u_[c:] - r[:c], u, r1)
            s2 = _each(lambda s, gl, k_, vn: s * gl + _bdot(k_[c:], vn, TN), s1, cm["glast_b"], kd, vn_b)
            o = _each(lambda ra, rb, at, va, vb_: cat0(ra[c:], rb[c:]) + _bdot(at, cat0(va, vb_)),
                      r0, r1, cm["attn"], vn_a, vn_b)
            for hh, sl in enumerate(heads):
                st_ref[hh, 2 * pi] = s0[hh]
                st_ref[hh, 2 * pi + 1] = s1[hh]
                s_sc[hh] = s2[hh]
                o_ref[rows, sl] = o[hh]
            return 0

        lax.fori_loop(0, npair, pair, 0)

    blk, row, st = _gdn_specs(t, ts, lambda s: s)
    return pl.pallas_call(
        body, name=name, grid=(GDN_HEADS // GDN_HP, t // ts), in_specs=[blk, blk, blk, blk, row, blk],
        out_specs=[blk, st],
        out_shape=[jax.ShapeDtypeStruct((t, 1024), F32), jax.ShapeDtypeStruct((GDN_HEADS, t // CHUNK, LANES, LANES), F32)],
        scratch_shapes=[pltpu.VMEM((GDN_HP, LANES, LANES), F32)],
        compiler_params=_params(("parallel", "arbitrary")),
    )(q, k, v, gcb, gct, bb)


def _gdn_bwd(q, k, v, gcb, gct, bb, do, states, *, name):
    t = q.shape[0]
    ts = _tile(t, GDN_TILE)
    npair = ts // PAIR
    ns = t // ts
    c = CHUNK

    def body(q_ref, k_ref, v_ref, g_ref, gt_ref, b_ref, do_ref, st_ref, dq_ref, dk_ref, dv_ref, dg_ref, db_ref, ds_sc):
        @pl.when(pl.program_id(1) == 0)
        def _():
            ds_sc[...] = jnp.zeros_like(ds_sc)

        rowsum = lambda x: jnp.sum(x, axis=-1, keepdims=True)
        total = lambda x: jnp.sum(rowsum(x), axis=0, keepdims=True)
        cat0 = lambda *xs: jnp.concatenate(xs, axis=0)
        cat1 = lambda *xs: jnp.concatenate(xs, axis=1)

        def pair(step, _):
            pi = npair - 1 - step
            rows = pl.ds(pl.multiple_of(pi * PAIR, PAIR), PAIR)
            heads = [slice(hh * LANES, (hh + 1) * LANES) for hh in range(GDN_HP)]
            hs = range(GDN_HP)
            qv, kv, vv = ([r[rows, sl] for sl in heads] for r in (q_ref, k_ref, v_ref))
            beta = [b_ref[rows, sl] for sl in heads]
            dov = [do_ref[rows, sl] for sl in heads]
            s0 = [st_ref[hh, 2 * pi] for hh in hs]
            s1 = [st_ref[hh, 2 * pi + 1] for hh in hs]
            ds2 = [ds_sc[hh] for hh in hs]
            cm = _pair_common(qv, kv, vv, [g_ref[rows, sl] for sl in heads], [gt_ref[hh, :, rows] for hh in hs], beta)
            u, w, qd, kd, attn = cm["u"], cm["w"], cm["qd"], cm["kd"], cm["attn"]
            tmat, gamma, eg = cm["tm"], cm["gamma"], cm["eg"]
            incl, strict = cm["incl"], cm["strict"]
            vn_a = _each(lambda u_, w_, s: u_[:c] - _bdot(w_[:c], s), u, w, s0)
            vn_b = _each(lambda u_, w_, s: u_[c:] - _bdot(w_[c:], s), u, w, s1)
            vn = _each(cat0, vn_a, vn_b)
            dvn_att = _each(lambda a, d: _bdot(a, d, TN), attn, dov)
            dattn = _each(lambda d, v_: jnp.where(incl, _bdot(d, v_, NT), 0.0), dov, vn)
            dvn_b = _each(lambda x, k_, d: x[c:] + _bdot(k_[c:], d), dvn_att, kd, ds2)
            rb = _each(lambda d, x, s: _bdot(cat0(d[c:], x), s, NT), dov, dvn_b, s1)
            dkd_b = _each(lambda v_, d: _bdot(v_, d, NT), vn_b, ds2)
            dgl_b = _each(lambda d, s: total(d * s), ds2, s1)
            ds1 = _each(lambda d, gl, q_, w_, o_, x: d * gl + _bdot(cat0(q_[c:], w_[c:]), cat0(o_[c:], -x), TN),
                        ds2, cm["glast_b"], qd, w, dov, dvn_b)
            dvn_a = _each(lambda x, k_, d: x[:c] + _bdot(k_[:c], d), dvn_att, kd, ds1)
            ra = _each(lambda d, x, s: _bdot(cat0(d[:c], x), s, NT), dov, dvn_a, s0)
            dkd_a = _each(lambda v_, d: _bdot(v_, d, NT), vn_a, ds1)
            dgl_a = _each(lambda d, s: total(d * s), ds1, s0)
            ds0 = _each(lambda d, gl, q_, w_, o_, x: d * gl + _bdot(cat0(q_[:c], w_[:c]), cat0(o_[:c], -x), TN),
                        ds1, cm["glast_a"], qd, w, dov, dvn_a)
            dvn = _each(cat0, dvn_a, dvn_b)
            dqd = _each(lambda a, b: cat0(a[:c], b[:c]), ra, rb)
            dw = _each(lambda a, b: -cat0(a[c:], b[c:]), ra, rb)
            dkd = _each(cat0, dkd_a, dkd_b)
            dvw = _each(cat1, dvn, dw)
            dvbk = _each(lambda t_, x: _bdot(t_, x, TN), tmat, dvw)
            dvb = _each(lambda x: x[:, :LANES], dvbk)
            dkbe = _each(lambda x: x[:, LANES:], dvbk)
            dt_ = _each(lambda x, a, b: _bdot(x, cat1(a, b), NT), dvw, cm["vb"], cm["kbe"])
            da1 = _each(lambda t_, x: _bdot(t_, x, TN), tmat, dt_)
            dm = _each(lambda x, t_: jnp.where(strict, -_bdot(x, t_, NT), 0.0), da1, tmat)
            dkk = _each(jnp.multiply, dm, gamma)
            dqk = _each(jnp.multiply, dattn, gamma)
            z = _each(lambda a, b, c_, d: a * b + c_ * d, dm, cm["m"], dattn, attn)
            dkb = _each(lambda x, k_, y, e: _bdot(x, k_) + y * e, dkk, kv, dkbe, eg)
            dk = _each(lambda a, b, kb_, q_, x, e, y, be: _bdot(cat0(a, b), cat0(kb_, q_), TN) + x * e + y * be,
                       dkk, dqk, cm["kb"], qv, dkd, cm["ek"], dkb, beta)
            dq = _each(lambda x, k_, y, e: _bdot(x, k_) + y * e, dqk, kv, dqd, eg)

            def colsum_of(z_):
                zh = z_.astype(BF16)
                zl = (z_ - zh.astype(F32)).astype(BF16)
                return _dot(cat0(zh, zl), jnp.ones((2 * PAIR, LANES), BF16), TN)

            colsum = _each(colsum_of, z)
            ri = lax.broadcasted_iota(jnp.int32, (PAIR, LANES), 0)
            for hh, sl in enumerate(heads):
                dkd_kd = dkd[hh] * kd[hh]
                dgc = (rowsum(z[hh]) - colsum[hh] + rowsum(dqd[hh] * qd[hh]) - rowsum(dkd_kd)
                       + rowsum(dkbe[hh] * cm["kbe"][hh]))
                last_a = total(dkd_kd[:c]) + dgl_a[hh] * cm["glast_a"][hh]
                last_b = total(dkd_kd[c:]) + dgl_b[hh] * cm["glast_b"][hh]
                dgc = dgc + jnp.where(ri == c - 1, last_a, 0.0) + jnp.where(ri == PAIR - 1, last_b, 0.0)
                ds_sc[hh] = ds0[hh]
                dq_ref[rows, sl] = dq[hh]
                dk_ref[rows, sl] = dk[hh]
                dv_ref[rows, sl] = dvb[hh] * beta[hh]
                db_ref[rows, sl] = jnp.broadcast_to(rowsum(dkb[hh] * kv[hh]) + rowsum(dvb[hh] * vv[hh]), (PAIR, LANES))
                dg_ref[rows, sl] = dgc
            return 0

        lax.fori_loop(0, npair, pair, 0)

    blk, row, st = _gdn_specs(t, ts, lambda s: ns - 1 - s)
    out = jax.ShapeDtypeStruct((t, 1024), F32)
    return pl.pallas_call(
        body, name=name, grid=(GDN_HEADS // GDN_HP, ns), in_specs=[blk, blk, blk, blk, row, blk, blk, st],
        out_specs=[blk] * 5, out_shape=[out] * 5, scratch_shapes=[pltpu.VMEM((GDN_HP, LANES, LANES), F32)],
        compiler_params=_params(("parallel", "arbitrary")),
    )(q, k, v, gcb, gct, bb, do, states)


def _gate_out_proj_loss(o, proj_c, o_norm, w, hres, g, target, *, name):
    t, d = hres.shape
    tm = _tile(t, 2 * ROW_TILE)

    def body(o_ref, z_ref, on_ref, w_ref, h_ref, g_ref, t_ref, dh_ref, dhb_ref, dg_ref, loss_ref, dw_ref, y_ref):
        i = pl.program_id(0)
        for hd in range(GDN_HEADS):
            sl = slice(hd * LANES, (hd + 1) * LANES)
            ov = o_ref[:, sl]
            rr = lax.rsqrt(jnp.mean(ov * ov, axis=-1, keepdims=True) + RMS_EPS)
            z = z_ref[:, sl]
            y_ref[:, sl] = (ov * rr * on_ref[...] * (z * _sigmoid(z))).astype(BF16)
        x = h_ref[...] + _dot(y_ref[...], w_ref[...])
        r = lax.rsqrt(jnp.mean(x * x, axis=-1, keepdims=True) + RMS_EPS)
        xh = x * r
        err = xh * g_ref[...] - t_ref[...]
        dy = err * (1.0 / d)
        dxh = dy * g_ref[...]
        dh = r * (dxh - xh * jnp.mean(dxh * xh, axis=-1, keepdims=True))
        dh_ref[...] = dh
        dhb_ref[...] = dh.astype(BF16)

        @pl.when(i == 0)
        def _():
            dg_ref[...] = jnp.zeros_like(dg_ref)
            loss_ref[...] = jnp.zeros_like(loss_ref)
            dw_ref[...] = jnp.zeros_like(dw_ref)

        dg_ref[...] += jnp.sum(dy * xh, axis=0, keepdims=True)
        part = 0.5 * jnp.sum(jnp.mean(err * err, axis=-1, keepdims=True), axis=0, keepdims=True)
        loss_ref[...] += jnp.broadcast_to(part, loss_ref.shape)
        dw_ref[...] += _dot(y_ref[...], dhb_ref[...], TN)

    row = pl.BlockSpec((tm, d), lambda i: (i, 0))
    vec = pl.BlockSpec((1, d), lambda i: (0, 0))
    return pl.pallas_call(
        body, name=name, grid=(t // tm,),
        in_specs=[row, pl.BlockSpec((tm, 1024), lambda i: (i, 3)), pl.BlockSpec((1, LANES), lambda i: (0, 0)),
                  pl.BlockSpec(w.shape, lambda i: (0, 0)), row, vec, row],
        out_specs=[row, row, vec, pl.BlockSpec((8, LANES), lambda i: (0, 0)), pl.BlockSpec(w.shape, lambda i: (0, 0))],
        out_shape=[jax.ShapeDtypeStruct((t, d), F32), jax.ShapeDtypeStruct((t, d), BF16),
                   jax.ShapeDtypeStruct((1, d), F32), jax.ShapeDtypeStruct((8, LANES), F32),
                   jax.ShapeDtypeStruct(w.shape, F32)],
        scratch_shapes=[pltpu.VMEM((tm, d), BF16)],
        compiler_params=_params(("arbitrary",)),
    )(o, proj_c, o_norm, w, hres, g, target)


def _pad_cols(w, n):
    return jnp.pad(w, ((0, 0), (0, n - w.shape[1])))


def _layout_odd(w):
    return dict(
        winc=_pad_cols(w["w_in_c"], IN_C_PAD).astype(BF16), wout_c=w["w_out_c"].astype(BF16), conv_w=w["conv_w"],
        a_log=_pad_cols(w["a_log"], LANES), dt_bias=_pad_cols(w["dt_bias"], LANES),
        norm_c=w["norm_c"], o_norm=w["o_norm"], final_norm=w["final_norm"],
    )


def _layout_in_ab(w):
    z = lambda r, c: jnp.zeros((r, c), w["w_in_ab"].dtype)
    wi = w["w_in_ab"]
    win = jnp.concatenate([wi[:, :384], z(1024, 64), wi[:, 384:416], z(1024, 32), wi[:, 416:]], axis=1)
    pw = w["pool_w"]
    rows = []
    for g in range(4):
        rows.append(jnp.concatenate([pw[g] if j == g else jnp.zeros((128, 128), F32) for j in range(4)], axis=1))
    wpool = jnp.concatenate(rows, axis=0)
    half = MLA_ROPE // 2
    inv = 1.0 / (ROPE_THETA ** (jnp.arange(half, dtype=F32) / half))
    inv_lane = jnp.concatenate([jnp.zeros((MLA_NOPE,), F32), inv, inv, jnp.zeros((32,), F32)]).reshape(1, LANES)
    return dict(win=win.astype(BF16), wpool=wpool.astype(BF16), inv_lane=inv_lane, norm_ab=w["norm_ab"],
                q_a_norm=w["q_a_norm"], kv_a_norm=w["kv_a_norm"], pool_scale=w["pool_scale"])


def _layout_mid(w):
    wq = jnp.pad(w["w_q_b"].reshape(MLA_Q_RANK, MLA_HEADS, 96), ((0, 0), (0, 0), (0, 32))).reshape(MLA_Q_RANK, 1024)
    kv3 = w["w_kv_b"].reshape(MLA_KV_RANK, MLA_HEADS, 128)
    wk = jnp.pad(kv3[..., :MLA_NOPE], ((0, 0), (0, 0), (0, 64))).reshape(MLA_KV_RANK, 1024)
    wv = kv3[..., MLA_NOPE:].reshape(MLA_KV_RANK, 512)
    return dict(wq=wq.astype(BF16), wk=wk.astype(BF16), wv=wv.astype(BF16), wout_ab=w["w_out_ab"].astype(BF16))


def _unlayout_grads(g, names):
    out = {}
    for name in names:
        if name == "w_in_ab":
            dwin = g["win"]
            out[name] = jnp.concatenate([dwin[:384], dwin[448:480], dwin[512:]], axis=0)
        elif name == "w_q_b":
            out[name] = g["wq"].reshape(MLA_Q_RANK, MLA_HEADS, 128)[..., :96].reshape(MLA_Q_RANK, 768)
        elif name == "w_kv_b":
            out[name] = jnp.concatenate([g["wk"].reshape(MLA_KV_RANK, MLA_HEADS, 128)[..., :MLA_NOPE],
                                         g["wv"].reshape(MLA_KV_RANK, MLA_HEADS, MLA_V)], axis=-1).reshape(MLA_KV_RANK, 1024)
        elif name == "w_in_c":
            out[name] = g["winc"][:4112]
        else:
            out[name] = g[{"w_out_ab": "wout_ab", "w_out_c": "wout_c"}[name]]
    return out


def _local_step(x, pos, target, lw, more_weights=None, on_grads=None):
    mm = _matmul
    proj, hn = _rms_in_proj(x, lw["norm_ab"], lw["win"], name="rms_in_ab")
    if more_weights is not None:
        lw = {**lw, **more_weights("mid", proj)}
    q, k, v, ybraw, qn, kvn, d, cos_t, sin_t = _ab_prep(
        proj, pos, lw["inv_lane"], lw["q_a_norm"], lw["kv_a_norm"], lw["wq"], lw["wk"], lw["wv"], lw["wpool"], name="ab_prep")
    o, lse = _attn_fwd(q, k, v, name="attn_fwd")
    h1, y = _gate_out_proj(o, ybraw, proj, lw["pool_scale"], lw["wout_ab"], x, name="gate_out_ab")
    lo = lw if more_weights is None else more_weights("odd", h1)
    proj_c, hn1 = _rms_in_proj(h1, lo["norm_c"], lo["winc"], name="rms_in_c")
    q2, k2, v2, gb, bb, gt = _c_prep(proj_c, lo["conv_w"], lo["a_log"], lo["dt_bias"], name="c_prep")
    gt = gt.reshape(GDN_HEADS, 1, gt.shape[1])
    o2, states = _gdn_fwd(q2, k2, v2, gb, gt, bb, name="gdn_fwd")
    dh2, dh2b, d_final, loss, d_wout_c = _gate_out_proj_loss(
        o2, proj_c, lo["o_norm"], lo["wout_c"], h1, lo["final_norm"], target, name="gate_out_c_loss")
    g = {"final_norm": d_final, "wout_c": d_wout_c, "loss": loss}
    do2, dz2, g["o_norm"] = _o_gate_bwd(dh2b, lo["wout_c"], o2, proj_c, lo["o_norm"], name="gate_c_bwd")
    dq2, dk2, dv2, dgb, dbb = _gdn_bwd(q2, k2, v2, gb, gt, bb, do2, states, name="gdn_bwd")
    dproj_c, g["conv_w"], g["a_log"], g["dt_bias"] = _c_prep_bwd(
        proj_c, lo["conv_w"], lo["a_log"], lo["dt_bias"], dq2, dk2, dv2, dgb, dbb, dz2, name="c_prep_bwd")
    g["winc"] = mm(dproj_c, hn1, "tn", name="in_c_dw")
    notify = (lambda tag: 0.0) if on_grads is None else (lambda tag: on_grads(tag, g))
    pool_scale = lw["pool_scale"] + notify("odd")
    dh1, g["norm_c"], dh1b, g["wout_ab"] = _matmul_rms_bwd(
        dproj_c, lo["winc"], h1, lo["norm_c"], dh2, name="in_c_dx_rms", prev_y=y)
    pool_scale = pool_scale + notify("out_ab")
    do, delta, dyb, dz, g["pool_scale"] = _gate_bwd(dh1b, lw["wout_ab"], o, ybraw, proj, pool_scale, name="gate_ab_bwd")
    dq, dk, dv = _attn_bwd(q, k, v, do, lse, delta, name="attn_bwd")
    dproj, g["q_a_norm"], g["kv_a_norm"], g["wq"], g["wk"], g["wv"], g["wpool"], g["win"] = _ab_prep_bwd(
        proj, lw["q_a_norm"], lw["kv_a_norm"], dq, dk, dv, cos_t, sin_t, dyb, dz, qn, kvn, d, hn,
        lw["wq"], lw["wk"], lw["wv"], lw["wpool"], name="ab_prep_bwd")
    norm_ab = lw["norm_ab"] + notify("in_ab")
    dx, g["norm_ab"] = _matmul_rms_bwd(dproj, lw["win"], x, norm_ab, dh1, name="in_ab_dx_rms")
    return loss, dx, g


_HBM = pl.BlockSpec(memory_space=pltpu.HBM)


def _place():
    return lax.axis_index("x"), lax.axis_index("y"), lax.axis_index("c")


def _flip(v, f):
    return 1 - v if f else v


_CHIP_FLIPS = ((1, 0), (0, 1), (1, 1))
_DEV_FLIPS = tuple((fx, fy, fc) for fx in (0, 1) for fy in (0, 1) for fc in (0, 1) if fx or fy or fc)


def _rcopy(src, dst, send_sems, recv_sems, k, to):
    return pltpu.make_async_remote_copy(src_ref=src, dst_ref=dst, send_sem=send_sems.at[k], recv_sem=recv_sems.at[k],
                                        device_id=to, device_id_type=MESH)


def _my_half(ref, c, axis):
    rh = ref.shape[axis] // 2
    idx = [slice(None)] * len(ref.shape)
    idx[axis] = pl.ds(c * rh, rh)
    return ref.at[tuple(idx)]


def _gather_weights(bigs, smalls):
    nb, ns = len(bigs), len(smalls)

    def body(*refs):
        ins, outs = refs[:nb + ns], refs[nb + ns:2 * (nb + ns)]
        send_sems, recv_sems, local_sems = refs[2 * (nb + ns):]
        x, y, c = _place()
        j0 = 2 * x + y
        sib = (x, y, 1 - c)
        chips = [(_flip(x, fx), _flip(y, fy)) for fx, fy in _CHIP_FLIPS]
        local = [pltpu.make_async_copy(i_ref, o_ref.at[j0], local_sems.at[a])
                 for a, (i_ref, o_ref) in enumerate(zip(ins, outs))]
        for cp in local:
            cp.start()
        sends = []
        for k, (px, py) in enumerate(chips):
            for a in range(nb):
                sends.append(_rcopy(_my_half(ins[a], c, 0), _my_half(outs[a].at[j0], c, 0), send_sems, recv_sems,
                                    6 * a + k, (px, py, c)))
            for s in range(ns):
                sends.append(_rcopy(ins[nb + s], outs[nb + s].at[j0], send_sems, recv_sems, 6 * nb + 3 * s + k, (px, py, c)))
        for cp in sends:
            cp.start()
        for k, (px, py) in enumerate(chips):
            jk = 2 * px + py
            for a in range(nb):
                landed = _my_half(outs[a].at[jk], c, 0)
                _rcopy(landed, landed, send_sems, recv_sems, 6 * a + k, (px, py, c)).wait_recv()
                fwd = _rcopy(landed, landed, send_sems, recv_sems, 6 * a + 3 + k, sib)
                fwd.start()
                sends.append(fwd)
        for k, (px, py) in enumerate(chips):
            jk = 2 * px + py
            for a in range(nb):
                other = _my_half(outs[a].at[jk], 1 - c, 0)
                _rcopy(other, other, send_sems, recv_sems, 6 * a + 3 + k, sib).wait_recv()
            for s in range(ns):
                _rcopy(ins[nb + s], outs[nb + s].at[jk], send_sems, recv_sems, 6 * nb + 3 * s + k, (px, py, c)).wait_recv()
        for cp in sends:
            cp.wait_send()
        for cp in local:
            cp.wait()

    arrays = list(bigs) + list(smalls)
    n_sem = 6 * nb + 3 * ns
    return pl.pallas_call(
        body, name="gather_weights", in_specs=[_HBM] * len(arrays), out_specs=[_HBM] * len(arrays),
        out_shape=[jax.ShapeDtypeStruct((4,) + a.shape, a.dtype) for a in arrays],
        scratch_shapes=[pltpu.SemaphoreType.DMA((n_sem,)), pltpu.SemaphoreType.DMA((n_sem,)),
                        pltpu.SemaphoreType.DMA((len(arrays),))],
    )(*arrays)


def _core_swap_partial(gs, by_cols, *, name):
    n = len(gs)

    def body(*refs):
        ins, outs = refs[:n], refs[n:2 * n]
        send_sems, recv_sems = refs[2 * n:]
        x, y, c = _place()
        copies = [_rcopy(_my_half(i_ref, 1 - c, 2 if by_cols[a] else 1), o_ref, send_sems, recv_sems, a, (x, y, 1 - c))
                  for a, (i_ref, o_ref) in enumerate(zip(ins, outs))]
        for cp in copies:
            cp.start()
        for cp in copies:
            cp.wait()

    halved = lambda g, cols: (4, g.shape[1], g.shape[2] // 2) if cols else (4, g.shape[1] // 2, g.shape[2])
    return pl.pallas_call(
        body, name=name, in_specs=[_HBM] * n, out_specs=[_HBM] * n,
        out_shape=[jax.ShapeDtypeStruct(halved(g, cols), g.dtype) for g, cols in zip(gs, by_cols)],
        scratch_shapes=[pltpu.SemaphoreType.DMA((n,)), pltpu.SemaphoreType.DMA((n,))],
    )(*gs)


def _core_swap_sum(fs, by_cols):
    n = len(fs)

    def body(*refs):
        ins, outs = refs[:n], refs[n:2 * n]
        send_sems, recv_sems = refs[2 * n:]
        x, y, c = _place()
        axes = [1 if cols else 0 for cols in by_cols]
        copies = [_rcopy(_my_half(i_ref, c, ax), _my_half(o_ref, c, ax), send_sems, recv_sems, a, (x, y, 1 - c))
                  for a, (i_ref, o_ref, ax) in enumerate(zip(ins, outs, axes))]
        for cp in copies:
            cp.start()
        for a, cp in enumerate(copies):
            cp.wait_send()
            theirs = _my_half(outs[a], 1 - c, axes[a])
            _rcopy(theirs, theirs, send_sems, recv_sems, a, (x, y, 1 - c)).wait_recv()

    return pl.pallas_call(
        body, name="core_swap_sum", in_specs=[_HBM] * n, out_specs=[_HBM] * n,
        out_shape=[jax.ShapeDtypeStruct(f.shape, f.dtype) for f in fs],
        input_output_aliases={a: a for a in range(n)},
        scratch_shapes=[pltpu.SemaphoreType.DMA((n,)), pltpu.SemaphoreType.DMA((n,))],
    )(*fs)


_SEM = pl.BlockSpec(memory_space=pltpu.SEMAPHORE)
_ANY = pl.BlockSpec(memory_space=pl.ANY)
_DATAFLOW = pltpu.SideEffectType.DATAFLOW_SIDE_EFFECTING


def _peer_flips(to_all):
    return _DEV_FLIPS if to_all else tuple((fx, fy, 0) for fx, fy in _CHIP_FLIPS)


def _to_chips_copies(srcs, lands, send_sems, recv_sems, per_chip_slot, to_all=False):
    x, y, c = _place()
    flips = _peer_flips(to_all)
    index = (lambda px, py, pc: 4 * px + 2 * py + pc) if to_all else (lambda px, py, pc: 2 * px + py)
    me = index(x, y, c)
    out = []
    for k, (fx, fy, fc) in enumerate(flips):
        peer = (_flip(x, fx), _flip(y, fy), _flip(c, fc))
        theirs = index(*peer)
        for a, (src, land) in enumerate(zip(srcs, lands)):
            piece = src.at[theirs] if per_chip_slot else src
            out.append((_rcopy(piece, land.at[me], send_sems, recv_sems, len(flips) * a + k, peer),
                        _rcopy(piece, land.at[theirs], send_sems, recv_sems, len(flips) * a + k, peer)))
    return out


def _to_chips_start(arrays, *, per_chip_slot, name, after=None, to_all=False):
    n = len(arrays)
    peers = len(_peer_flips(to_all))
    lands = [lax.empty((peers + 1,) + (a.shape[1:] if per_chip_slot else a.shape), a.dtype) for a in arrays]
    extra = [] if after is None else [after]

    def body(*refs):
        srcs, land_refs, token = refs[:n], refs[n:2 * n], refs[-1]
        send_sems, recv_sems = refs[2 * n + len(extra)], refs[2 * n + len(extra) + 1]
        for send, _ in _to_chips_copies(srcs, land_refs, send_sems, recv_sems, per_chip_slot, to_all):
            send.start()
        token[...] = jnp.zeros_like(token)

    held = [pltpu.with_memory_space_constraint(a, pltpu.HBM) for a in list(arrays) + lands]
    return pl.pallas_call(
        body, name=name, in_specs=[_HBM] * (2 * n) + [_ANY] * len(extra),
        out_specs=(_SEM, _SEM, *[_HBM] * (2 * n), pl.BlockSpec(memory_space=pltpu.VMEM)),
        out_shape=(pltpu.SemaphoreType.DMA((peers * n,)), pltpu.SemaphoreType.DMA((peers * n,)),
                   *[pltpu.HBM(a.shape, a.dtype) for a in held], jax.ShapeDtypeStruct((8, LANES), F32)),
        input_output_aliases={i: 2 + i for i in range(2 * n)},
        compiler_params=pltpu.CompilerParams(has_side_effects=_DATAFLOW),
    )(*held, *extra)


def _to_chips_wait(started, after, *, per_chip_slot, name, to_all=False):
    send_sems, recv_sems, held = started[0], started[1], started[2:-1]
    n = len(held) // 2

    def body(*refs):
        srcs, land_refs, s_sems, r_sems = refs[:n], refs[n:2 * n], refs[2 * n], refs[2 * n + 1]
        for send, arrival in _to_chips_copies(srcs, land_refs, s_sems, r_sems, per_chip_slot, to_all):
            send.wait_send()
            arrival.wait_recv()

    out = pl.pallas_call(
        body, name=name, in_specs=[_HBM] * (2 * n) + [_SEM, _SEM, _ANY], out_specs=[_HBM] * (2 * n),
        out_shape=[pltpu.HBM(a.shape, a.dtype) for a in held],
        input_output_aliases={i: i for i in range(2 * n)},
        compiler_params=pltpu.CompilerParams(has_side_effects=_DATAFLOW),
    )(*held, send_sems, recv_sems, after)
    return out[n:]


def _chip_exchange(ps, small):
    n = len(ps)
    rs = small.shape[0]

    def body(*refs):
        p_refs, s_ref = refs[:n], refs[n]
        l_refs, ls_ref = refs[n + 1:2 * n + 1], refs[2 * n + 1]
        send_sems, recv_sems, local_sems = refs[2 * n + 2:]
        x, y, c = _place()
        j0 = 2 * x + y
        d0 = 2 * j0 + c
        local = [pltpu.make_async_copy(p.at[j0], l.at[j0], local_sems.at[a]) for a, (p, l) in enumerate(zip(p_refs, l_refs))]
        local.append(pltpu.make_async_copy(s_ref, ls_ref.at[d0], local_sems.at[n]))
        for cp in local:
            cp.start()
        sends = []
        for k, (fx, fy) in enumerate(_CHIP_FLIPS):
            px, py = _flip(x, fx), _flip(y, fy)
            for a in range(n):
                sends.append(_rcopy(p_refs[a].at[2 * px + py], l_refs[a].at[j0], send_sems, recv_sems, 3 * a + k, (px, py, c)))
        for k, (fx, fy, fc) in enumerate(_DEV_FLIPS):
            peer = (_flip(x, fx), _flip(y, fy), _flip(c, fc))
            sends.append(_rcopy(s_ref, ls_ref.at[d0], send_sems, recv_sems, 3 * n + k, peer))
        for cp in sends:
            cp.start()
        for k, (fx, fy) in enumerate(_CHIP_FLIPS):
            px, py = _flip(x, fx), _flip(y, fy)
            for a in range(n):
                _rcopy(p_refs[a].at[j0], l_refs[a].at[2 * px + py], send_sems, recv_sems, 3 * a + k, (px, py, c)).wait_recv()
        for k, (fx, fy, fc) in enumerate(_DEV_FLIPS):
            px, py, pc = _flip(x, fx), _flip(y, fy), _flip(c, fc)
            _rcopy(s_ref, ls_ref.at[4 * px + 2 * py + pc], send_sems, recv_sems, 3 * n + k, (px, py, pc)).wait_recv()
        for cp in sends:
            cp.wait_send()
        for cp in local:
            cp.wait()

    n_sem = 3 * n + 7
    return pl.pallas_call(
        body, name="chip_exchange", in_specs=[_HBM] * (n + 1), out_specs=[_HBM] * (n + 1),
        out_shape=[jax.ShapeDtypeStruct(p.shape, F32) for p in ps] + [jax.ShapeDtypeStruct((8, rs, LANES), F32)],
        scratch_shapes=[pltpu.SemaphoreType.DMA((n_sem,)), pltpu.SemaphoreType.DMA((n_sem,)),
                        pltpu.SemaphoreType.DMA((n + 1,))],
    )(*ps, small)


def _half_blocks(rows, cols, by_cols):
    if by_cols:
        tc = _tile(cols // 2, 256)
        nb = cols // 2 // tc
        return rows, tc, nb, (lambda i, c: (0, c * nb + i))
    tr = _tile(rows // 2, 256)
    nb = rows // 2 // tr
    return tr, cols, nb, (lambda i, c: (c * nb + i, 0))


def _core_sum(g, part, core, *, name, by_cols):
    _, rows, cols = g.shape
    br, bc, nb, whole = _half_blocks(rows, cols, by_cols)
    mine = (lambda i: (0, i)) if by_cols else (lambda i: (i, 0))

    def body(c_ref, g_ref, p_ref, o_ref):
        o_ref[...] = g_ref[...] + p_ref[...]

    grid_spec = pltpu.PrefetchScalarGridSpec(
        num_scalar_prefetch=1, grid=(4, nb),
        in_specs=[pl.BlockSpec((1, br, bc), lambda j, i, c: (j,) + whole(i, c[0])),
                  pl.BlockSpec((1, br, bc), lambda j, i, c: (j,) + mine(i))],
        out_specs=pl.BlockSpec((1, br, bc), lambda j, i, c: (j,) + mine(i)),
    )
    return pl.pallas_call(
        body, name=name, grid_spec=grid_spec, out_shape=jax.ShapeDtypeStruct(part.shape, F32),
        compiler_params=_params(("parallel", "parallel")),
    )(core, g, part)


def _chip_sum(landed, core, *, name, by_cols):
    _, hr, hc = landed.shape
    rows, cols = (hr, 2 * hc) if by_cols else (2 * hr, hc)
    br, bc, nb, whole = _half_blocks(rows, cols, by_cols)
    mine = (lambda i: (0, i)) if by_cols else (lambda i: (i, 0))

    def body(c_ref, l_ref, o_ref):
        o_ref[...] = ((l_ref[0] + l_ref[1]) + l_ref[2]) + l_ref[3]

    grid_spec = pltpu.PrefetchScalarGridSpec(
        num_scalar_prefetch=1, grid=(nb,),
        in_specs=[pl.BlockSpec((4, br, bc), lambda i, c: (0,) + mine(i))],
        out_specs=pl.BlockSpec((br, bc), lambda i, c: whole(i, c[0])),
    )
    return pl.pallas_call(
        body, name=name, grid_spec=grid_spec, out_shape=jax.ShapeDtypeStruct((rows, cols), F32),
        compiler_params=_params(("parallel",)),
    )(core, landed)


_ROW_POOL_W, _ROW_NORM_AB, _ROW_FINAL, _ROW_POOL_SCALE, _ROW_Q_NORM = 0, 512, 520, 528, 532
_ROW_KV_NORM, _ROW_O_NORM, _ROW_A_LOG, _ROW_DT_BIAS, _ROW_LOSS = 534, 535, 536, 537, 538
_ROW_CONV, _ROW_NORM_C, _SMALL_ROWS = 544, 640, 672
_CONV_ROWS = CONV_WIDTH * 6


def _put_rows(dst_ref, row0, src, width):
    for r in range(width // LANES):
        dst_ref[row0 + r:row0 + r + 1, :] = src[:, r * LANES:(r + 1) * LANES]


def _pack_small(g, loss_tile):
    names = ("wpool", "norm_ab", "final_norm", "pool_scale", "q_a_norm", "kv_a_norm", "o_norm", "a_log", "dt_bias",
             "conv_w", "norm_c")

    def body(wpool, norm_ab, final_norm, pool_scale, q_norm, kv_norm, o_norm, a_log, dt_bias, conv_w, norm_c, loss, o_ref):
        o_ref[...] = jnp.zeros_like(o_ref)
        for gi in range(4):
            o_ref[_ROW_POOL_W + gi * 128:_ROW_POOL_W + (gi + 1) * 128, :] = wpool[gi * 128:(gi + 1) * 128, gi * 128:(gi + 1) * 128]
        _put_rows(o_ref, _ROW_NORM_AB, norm_ab[...], 1024)
        _put_rows(o_ref, _ROW_FINAL, final_norm[...], 1024)
        _put_rows(o_ref, _ROW_POOL_SCALE, pool_scale[...], 512)
        _put_rows(o_ref, _ROW_Q_NORM, q_norm[...], 256)
        for row, ref in ((_ROW_KV_NORM, kv_norm), (_ROW_O_NORM, o_norm), (_ROW_A_LOG, a_log), (_ROW_DT_BIAS, dt_bias)):
            o_ref[row:row + 1, :] = ref[...]
        o_ref[_ROW_LOSS:_ROW_LOSS + 1, :] = loss[0:1, :]
        for j in range(4):
            for r in range(CONV_WIDTH):
                _put_rows(o_ref, _ROW_CONV + j * _CONV_ROWS + r * 6, conv_w[r:r + 1, j * 768:(j + 1) * 768], 768)
            _put_rows(o_ref, _ROW_NORM_C + j * 8, norm_c[:, j * 256:(j + 1) * 256], 256)

    vmem = pl.BlockSpec(memory_space=pltpu.VMEM)
    return pl.pallas_call(
        body, name="pack_small", in_specs=[vmem] * 12, out_specs=vmem,
        out_shape=jax.ShapeDtypeStruct((_SMALL_ROWS, LANES), F32),
    )(*[g[n] for n in names], loss_tile)


_SMALL_NAMES = ("pool_w", "norm_ab", "final_norm", "pool_scale", "q_a_norm", "kv_a_norm", "o_norm", "a_log", "dt_bias",
                "conv_w", "norm_c")


def _take_rows(src, row0, width):
    return jnp.concatenate([src[row0 + r:row0 + r + 1, :] for r in range(width // LANES)], axis=1)


def _small_update(small_all, late_all, ws, ms, vs):
    n = len(_SMALL_NAMES)

    def body(*refs):
        a_ref, late_ref = refs[0], refs[1]
        refs = refs[1:]
        w_refs, m_refs, v_refs = refs[1:1 + n], refs[1 + n:1 + 2 * n], refs[1 + 2 * n:1 + 3 * n]
        outs = refs[1 + 3 * n:1 + 7 * n]
        loss_ref, tot = refs[1 + 7 * n], refs[2 + 7 * n]
        acc, late = a_ref[0], late_ref[0]
        for d in range(1, 8):
            acc = acc + a_ref[d]
            late = late + late_ref[d]
        tot[...] = acc
        x, y, _ = _place()
        j0 = 2 * x + y
        conv = tot[pl.ds(pl.multiple_of(_ROW_CONV + j0 * _CONV_ROWS, 8), _CONV_ROWS), :]
        norm_c = tot[pl.ds(pl.multiple_of(_ROW_NORM_C + j0 * 8, 8), 8), :]
        whole = tot[_ROW_NORM_AB:_ROW_CONV, :]
        at = lambda row: row - _ROW_NORM_AB
        grads = {
            "norm_ab": _take_rows(late, 0, 1024), "final_norm": _take_rows(whole, at(_ROW_FINAL), 1024),
            "pool_scale": _take_rows(whole, at(_ROW_POOL_SCALE), 512), "q_a_norm": _take_rows(whole, at(_ROW_Q_NORM), 256),
            "kv_a_norm": whole[at(_ROW_KV_NORM):at(_ROW_KV_NORM) + 1, :], "o_norm": whole[at(_ROW_O_NORM):at(_ROW_O_NORM) + 1, :],
            "a_log": tot[_ROW_A_LOG:_ROW_A_LOG + 1, 0:GDN_HEADS],
            "dt_bias": tot[_ROW_DT_BIAS:_ROW_DT_BIAS + 1, 0:GDN_HEADS],
            "norm_c": _take_rows(norm_c, 0, 256),
        }
        loss_ref[...] = whole[at(_ROW_LOSS):at(_ROW_LOSS) + 1, :]
        for i, name in enumerate(_SMALL_NAMES):
            g_out = outs[4 * i]
            if name == "pool_w":
                for gi in range(4):
                    g_out[gi] = tot[_ROW_POOL_W + gi * 128:_ROW_POOL_W + (gi + 1) * 128, :]
            elif name == "conv_w":
                for r in range(CONV_WIDTH):
                    g_out[r:r + 1, :] = _take_rows(conv, r * 6, 768)
            else:
                g_out[...] = grads[name]
            _adam_update(g_out, w_refs[i], m_refs[i], v_refs[i], *outs[4 * i + 1:4 * i + 4])

    vmem = pl.BlockSpec(memory_space=pltpu.VMEM)
    out_shape = [jax.ShapeDtypeStruct(w.shape, F32) for w in ws for _ in range(4)] + [jax.ShapeDtypeStruct((1, LANES), F32)]
    return pl.pallas_call(
        body, name="small_update", in_specs=[vmem] * (2 + 3 * n), out_specs=[vmem] * (4 * n + 1), out_shape=out_shape,
        scratch_shapes=[pltpu.VMEM((_SMALL_ROWS, LANES), F32)],
        compiler_params=pltpu.CompilerParams(vmem_limit_bytes=VMEM_LIMIT),
    )(small_all, late_all, *ws, *ms, *vs)


def _adam_update(g_ref, w_ref, m_ref, v_ref, d_ref, mo_ref, vo_ref):
    gv = g_ref[...]
    mn = ADAM_B1 * m_ref[...] + (1.0 - ADAM_B1) * gv
    vn = ADAM_B2 * v_ref[...] + (1.0 - ADAM_B2) * (gv * gv)
    mo_ref[...] = mn
    vo_ref[...] = vn
    c1 = 1.0 - ADAM_B1 ** ADAM_STEP
    c2 = 1.0 - ADAM_B2 ** ADAM_STEP
    d_ref[...] = -ADAM_LR * ((mn / c1) / (jnp.sqrt(vn / c2) + ADAM_EPS) + ADAM_WD * w_ref[...])


def _adamw_rows(g, w, m, v, *, name):
    rows, cols = g.shape
    if rows % LANES == 0:
        tr = _tile(rows, 512)
        blk, steps = pl.BlockSpec((tr, cols), lambda i: (i, 0)), rows // tr
    else:
        tc = _tile(cols, 256)
        blk, steps = pl.BlockSpec((rows, tc), lambda i: (0, i)), cols // tc

    def body(*refs):
        _adam_update(*refs)

    out = jax.ShapeDtypeStruct((rows, cols), F32)
    return pl.pallas_call(
        body, name=name, grid=(steps,), in_specs=[blk] * 4, out_specs=[blk] * 3, out_shape=[out] * 3,
        compiler_params=_params(("parallel",)),
    )(g, w, m, v)


_ADAM_ROWWISE = ("w_in_ab", "w_q_b", "w_kv_b", "w_out_ab", "w_in_c", "w_out_c")


_SHARD_AXIS = {"w_in_ab": 1, "w_q_b": 1, "w_kv_b": 1, "w_out_ab": 0, "w_in_c": 1, "w_out_c": 0, "conv_w": 1, "norm_c": 1}
_ALL_NAMES = ("norm_ab", "w_in_ab", "q_a_norm", "w_q_b", "kv_a_norm", "w_kv_b", "pool_w", "pool_scale", "w_out_ab",
              "norm_c", "w_in_c", "conv_w", "a_log", "dt_bias", "o_norm", "w_out_c", "final_norm")


def _join_shards(a, axis):
    _, r, c = a.shape
    return a.reshape(4 * r, c) if axis == 0 else jnp.transpose(a, (1, 0, 2)).reshape(r, 4 * c)


def _split_shards(a, axis):
    r, c = a.shape
    return a.reshape(4, r // 4, c) if axis == 0 else jnp.transpose(a.reshape(r, 4, c // 4), (1, 0, 2))


def kernel(x, positions, norm_ab, w_in_ab, q_a_norm, w_q_b, kv_a_norm, w_kv_b, pool_w, pool_scale, w_out_ab, norm_c, w_in_c, conv_w, a_log, dt_bias, o_norm, w_out_c, final_norm, loss_target, m_norm_ab, m_w_in_ab, m_q_a_norm, m_w_q_b, m_kv_a_norm, m_w_kv_b, m_pool_w, m_pool_scale, m_w_out_ab, m_norm_c, m_w_in_c, m_conv_w, m_a_log, m_dt_bias, m_o_norm, m_w_out_c, m_final_norm, v_norm_ab, v_w_in_ab, v_q_a_norm, v_w_q_b, v_kv_a_norm, v_w_kv_b, v_pool_w, v_pool_scale, v_w_out_ab, v_norm_c, v_w_in_c, v_conv_w, v_a_log, v_dt_bias, v_o_norm, v_w_out_c, v_final_norm):
    given = dict(locals())
    c = lax.axis_index("c")
    t = x.shape[1]

    def shard_of(prefix, name):
        a = given[prefix + name]
        return a.reshape(a.shape[1:]) if a.ndim > 2 else a.reshape(1, -1)

    big, big_even, big_odd, small_sharded = _ADAM_ROWWISE, _ADAM_ROWWISE[:4], _ADAM_ROWWISE[4:], ("conv_w", "norm_c")
    chip = 2 * lax.axis_index("x") + lax.axis_index("y")
    core = c.astype(jnp.int32).reshape(1)
    later = {"mid": big_even[1:], "odd": big_odd + small_sharded}
    travelling = {}

    def send(tag, after=None):
        shards = [shard_of("", n).astype(BF16) if n in big else shard_of("", n) for n in later[tag]]
        started = _to_chips_start(shards, per_chip_slot=False, name="gather_" + tag + "_start", after=after)
        travelling[tag] = (shards, started)
        return started[-1][0, 0]

    mid_sent = send("mid")
    gathered = _gather_weights([shard_of("", "w_in_ab").astype(BF16)], [])
    full = {"w_in_ab": _join_shards(gathered[0], _SHARD_AXIS["w_in_ab"])}
    for name in ("norm_ab", "q_a_norm", "kv_a_norm", "pool_w", "pool_scale"):
        full[name] = shard_of("", name)
    lw = _layout_in_ab(full)
    lw["norm_ab"] = lw["norm_ab"] + mid_sent

    def more_weights(tag, after):
        shards, started = travelling[tag]
        landed = _to_chips_wait(started, after, per_chip_slot=False, name="gather_" + tag + "_wait")
        w = {}
        for name, land, own in zip(later[tag], landed, shards):
            w[name] = _join_shards(lax.dynamic_update_index_in_dim(land, own, chip, 0), _SHARD_AXIS[name])
        if tag == "mid":
            out = _layout_mid(w)
            out["wq"] = out["wq"] + send("odd", after=landed[0]).astype(BF16)
            return out
        for name in ("a_log", "dt_bias", "o_norm", "final_norm"):
            w[name] = shard_of("", name)
        return _layout_odd(w)

    transposed = ("w_in_ab", "w_in_c")

    def chip_partials(names, grads, tag):
        by_cols = [n in transposed for n in names]
        slots = [_split_shards(grads[n], 0 if n in transposed else _SHARD_AXIS[n]) for n in names]
        partial = _core_swap_partial(slots, by_cols, name="core_swap_partial_" + tag)
        return [_core_sum(s, p, core, name="core_sum_" + n, by_cols=b) for n, s, p, b in zip(names, slots, partial, by_cols)]

    groups = {"odd": big_odd, "out_ab": ("w_out_ab",), "in_ab": ("w_in_ab", "w_q_b", "w_kv_b")}
    sent = {}

    def on_grads(tag, g):
        part = chip_partials(groups[tag], _unlayout_grads(g, groups[tag]), tag)
        sent[tag] = (part, _to_chips_start(part, per_chip_slot=True, name="exchange_" + tag + "_start"))
        token = sent[tag][1][-1][0, 0]
        if tag == "in_ab":
            pack = _pack_small({**g, "norm_ab": jnp.zeros((1, 1024), F32)}, g["loss"])
            sent["small"] = (pack, _to_chips_start([pack], per_chip_slot=False, to_all=True, name="exchange_small_start"))
            token = token + sent["small"][1][-1][0, 0]
        return token

    loss_tile, dx, g = _local_step(x[0], positions.reshape(t, 1), loss_target[0], lw, more_weights, on_grads)
    late_all = _chip_exchange([], g["norm_ab"].reshape(8, LANES))[-1]
    pack, started = sent.pop("small")
    landed = _to_chips_wait(started, late_all, per_chip_slot=False, to_all=True, name="exchange_small_wait")[0]
    small_all = lax.dynamic_update_index_in_dim(landed, pack, 2 * chip + c, 0)
    halves = {}
    for tag, names in groups.items():
        part, started = sent[tag]
        landed = _to_chips_wait(started, late_all, per_chip_slot=True, name="exchange_" + tag + "_wait")
        for n, l, p in zip(names, landed, part):
            l = lax.dynamic_update_index_in_dim(l, lax.dynamic_index_in_dim(p, chip, 0, keepdims=False), chip, 0)
            halves[n] = _chip_sum(l, core, name="chip_sum_" + n, by_cols=n in transposed)
    gbig = dict(zip(big, _core_swap_sum([halves[n] for n in big], [n in transposed for n in big])))

    res = {}
    for name in big:
        operands = [gbig[name], shard_of("", name), shard_of("m_", name), shard_of("v_", name)]
        flip = name in transposed
        if flip:
            operands[1:] = [jnp.transpose(a) for a in operands[1:]]
        out = (operands[0],) + tuple(_adamw_rows(*operands, name="adamw_" + name))
        out = [jnp.transpose(a) for a in out] if flip else out
        res["grad", name], res["delta", name], res["m", name], res["v", name] = out
    out = _small_update(small_all, late_all, [shard_of("", n) for n in _SMALL_NAMES], [shard_of("m_", n) for n in _SMALL_NAMES],
                        [shard_of("v_", n) for n in _SMALL_NAMES])
    for i, name in enumerate(_SMALL_NAMES):
        res["grad", name], res["delta", name], res["m", name], res["v", name] = out[4 * i:4 * i + 4]
    res = {k: a.reshape(given[k[1]].shape) for k, a in res.items()}
    loss = out[-1][0, 0]
    outs = [loss, dx.reshape(x.shape)]
    for key in ("grad", "delta", "m", "v"):
        outs += [res[key, n] for n in _ALL_NAMES]
    return tuple(outs)
```

```python
import functools

import jax
import jax.numpy as jnp
from jax import lax
from jax.experimental import pallas as pl
from jax.experimental.pallas import tpu as pltpu

F32 = jnp.float32
BF16 = jnp.bfloat16
HI = lax.Precision.HIGHEST
MESH = pl.DeviceIdType.MESH

RMS_EPS = 1e-6
MLA_HEADS = 8
MLA_Q_RANK = 256
MLA_KV_RANK = 128
MLA_NOPE = 64
MLA_ROPE = 32
MLA_V = 64
ROPE_THETA = 10000.0
POOL_WINDOWS = (2, 4, 8, 16)
POOL_GROUP = 128
POOL_WIDTH = 512
POOL_HALO = 16
GDN_HEADS = 8
GDN_DK = 128
CONV_WIDTH = 4
CONV_HALO = 8
CHUNK = 64
IN_AB_PAD = 2048
IN_C_PAD = 4224
ATT_SCALE = (MLA_NOPE + MLA_ROPE) ** -0.5
LOG2E = 1.4426950408889634

ADAM_LR = 0.001
ADAM_B1 = 0.9
ADAM_B2 = 0.999
ADAM_EPS = 1e-08
ADAM_WD = 0.01
ADAM_STEP = 10

LANES = 128
VMEM_LIMIT = 56 * 1024 * 1024

ROW_TILE = 256
ATT_TILE = 1024
GDN_TILE = 256
MM_TILE = (1024, 1408, 2048)

NN = (((1,), (0,)), ((), ()))
NT = (((1,), (1,)), ((), ()))
TN = (((0,), (0,)), ((), ()))


def _dot(a, b, dims=NN, prec=None):
    return lax.dot_general(a, b, dims, precision=prec, preferred_element_type=F32)


def _tile(n, pref):
    if n <= pref:
        return n
    step = LANES if pref >= LANES else 8
    for t in range(pref - pref % step, 0, -step):
        if n % t == 0:
            return t
    return n


def _params(sem):
    return pltpu.CompilerParams(dimension_semantics=sem, vmem_limit_bytes=VMEM_LIMIT)


def _sigmoid(x):
    return 0.5 * jnp.tanh(0.5 * x) + 0.5


def _softplus(x):
    return jnp.maximum(x, 0.0) + jnp.log(1.0 + jnp.exp(-jnp.abs(x)))


def _matmul(a, b, mode, *, name):
    if mode == "nn":
        (m, k), (k2, n) = a.shape, b.shape
    elif mode == "nt":
        (m, k), (n, k2) = a.shape, b.shape
    else:
        (k, m), (k2, n) = a.shape, b.shape
    assert k == k2, (a.shape, b.shape, mode)
    tm, tn, tk = _tile(m, MM_TILE[0]), _tile(n, MM_TILE[1]), _tile(k, MM_TILE[2])
    nk = k // tk
    if mode == "tn":
        a_spec = pl.BlockSpec((tk, tm), lambda i, j, kk: (kk, i))
    else:
        a_spec = pl.BlockSpec((tm, tk), lambda i, j, kk: (i, kk))
    if mode == "nt":
        b_spec = pl.BlockSpec((tn, tk), lambda i, j, kk: (j, kk))
    else:
        b_spec = pl.BlockSpec((tk, tn), lambda i, j, kk: (kk, j))
    o_spec = pl.BlockSpec((tm, tn), lambda i, j, kk: (i, j))
    dims = {"nn": NN, "nt": NT, "tn": TN}[mode]

    def body(a_ref, b_ref, o_ref, *scratch):
        if nk == 1:
            o_ref[...] = _dot(a_ref[...], b_ref[...], dims)
            return
        acc = scratch[0]
        kk = pl.program_id(2)

        @pl.when(kk == 0)
        def _():
            acc[...] = jnp.zeros_like(acc)

        acc[...] += _dot(a_ref[...], b_ref[...], dims)

        @pl.when(kk == nk - 1)
        def _():
            o_ref[...] = acc[...]

    return pl.pallas_call(
        body, name=name, grid=(m // tm, n // tn, nk), in_specs=[a_spec, b_spec], out_specs=o_spec,
        out_shape=jax.ShapeDtypeStruct((m, n), F32),
        scratch_shapes=[pltpu.VMEM((tm, tn), F32)] if nk > 1 else [],
        compiler_params=_params(("parallel", "parallel", "arbitrary")),
    )(a, b)


def _rms_in_proj(h, g, w, *, name):
    t, d = h.shape
    n = w.shape[1]
    tm, tn = _tile(t, MM_TILE[0]), _tile(n, MM_TILE[1])

    def body(h_ref, g_ref, w_ref, o_ref, hn_ref):
        @pl.when(pl.program_id(1) == 0)
        def _():
            x = h_ref[...]
            r = lax.rsqrt(jnp.mean(x * x, axis=-1, keepdims=True) + RMS_EPS)
            hn_ref[...] = (x * r * g_ref[...]).astype(BF16)

        o_ref[...] = _dot(hn_ref[...], w_ref[...])

    return pl.pallas_call(
        body, name=name, grid=(t // tm, n // tn),
        in_specs=[pl.BlockSpec((tm, d), lambda i, j: (i, 0)), pl.BlockSpec((1, d), lambda i, j: (0, 0)),
                  pl.BlockSpec((d, tn), lambda i, j: (0, j))],
        out_specs=[pl.BlockSpec((tm, tn), lambda i, j: (i, j)), pl.BlockSpec((tm, d), lambda i, j: (i, 0))],
        out_shape=[jax.ShapeDtypeStruct((t, n), F32), jax.ShapeDtypeStruct((t, d), BF16)],
        compiler_params=_params(("parallel", "arbitrary")),
    )(h, g, w)


def _matmul_rms_bwd(dproj, w, h, g, dres, *, name, prev_y=None):
    t, k = dproj.shape
    d = w.shape[0]
    tm = _tile(t, 2 * ROW_TILE)
    chained = prev_y is not None

    def body(dp_ref, w_ref, h_ref, g_ref, dres_ref, *rest):
        i = pl.program_id(0)
        dh_ref, dg_ref = rest[-4:-2] if chained else rest
        dyv = _dot(dp_ref[...], w_ref[...], NT)
        x = h_ref[...]
        r = lax.rsqrt(jnp.mean(x * x, axis=-1, keepdims=True) + RMS_EPS)
        xh = x * r
        dxh = dyv * g_ref[...]
        dh = dres_ref[...] + r * (dxh - xh * jnp.mean(dxh * xh, axis=-1, keepdims=True))
        dh_ref[...] = dh

        @pl.when(i == 0)
        def _():
            dg_ref[...] = jnp.zeros_like(dg_ref)
            if chained:
                rest[-1][...] = jnp.zeros_like(rest[-1])

        dg_ref[...] += jnp.sum(dyv * xh, axis=0, keepdims=True)
        if chained:
            y_ref, dhb_ref, dw_ref = rest[0], rest[-2], rest[-1]
            dhb_ref[...] = dh.astype(BF16)
            dw_ref[...] += _dot(y_ref[...], dhb_ref[...], TN)

    row = pl.BlockSpec((tm, d), lambda i: (i, 0))
    vec = pl.BlockSpec((1, d), lambda i: (0, 0))
    in_specs = [pl.BlockSpec((tm, k), lambda i: (i, 0)), pl.BlockSpec((d, k), lambda i: (0, 0)), row, vec, row]
    out_specs, out_shape = [row, vec], [jax.ShapeDtypeStruct((t, d), F32), jax.ShapeDtypeStruct((1, d), F32)]
    args = [dproj, w, h, g, dres]
    if chained:
        in_specs.append(row)
        args.append(prev_y)
        out_specs += [row, pl.BlockSpec((d, d), lambda i: (0, 0))]
        out_shape += [jax.ShapeDtypeStruct((t, d), BF16), jax.ShapeDtypeStruct((d, d), F32)]
    return pl.pallas_call(
        body, name=name, grid=(t // tm,), in_specs=in_specs, out_specs=out_specs, out_shape=out_shape,
        compiler_params=_params(("arbitrary",)),
    )(*args)


def _rope_partner(x):
    lane = lax.broadcasted_iota(jnp.int32, x.shape, 1)
    swapped = jnp.where(lane < MLA_NOPE + MLA_ROPE // 2, pltpu.roll(x, LANES - 16, 1), pltpu.roll(x, 16, 1))
    return jnp.where((lane >= MLA_NOPE) & (lane < MLA_NOPE + MLA_ROPE), swapped, 0.0)


def _pool_counts(row0, tm, w):
    t_idx = row0 + lax.broadcasted_iota(jnp.int32, (tm, POOL_GROUP), 0)
    return jnp.minimum(t_idx + 1, w).astype(F32)


def _ab_prep(proj, pos, inv_freq, q_a_norm, kv_a_norm, wq, wk, wv, wpool, *, name):
    t = proj.shape[0]
    tm = _tile(t, ROW_TILE)
    hb = tm // POOL_HALO

    def body(p_ref, halo_ref, pos_ref, inv_ref, qg_ref, kg_ref, wq_ref, wk_ref, wv_ref, wp_ref,
             q_ref, k_ref, v_ref, yb_ref, qn_ref, kvn_ref, d_ref, cos_ref, sin_ref, ext):
        i = pl.program_id(0)
        ql = p_ref[:, 0:MLA_Q_RANK]
        r = lax.rsqrt(jnp.mean(ql * ql, axis=-1, keepdims=True) + RMS_EPS)
        qn = (ql * r * qg_ref[...]).astype(BF16)
        qn_ref[...] = qn
        kl = p_ref[:, MLA_Q_RANK:MLA_Q_RANK + MLA_KV_RANK]
        r = lax.rsqrt(jnp.mean(kl * kl, axis=-1, keepdims=True) + RMS_EPS)
        kvn = (kl * r * kg_ref[...]).astype(BF16)
        kvn_ref[...] = kvn
        ang = pos_ref[...].astype(F32) * inv_ref[...]
        lane = lax.broadcasted_iota(jnp.int32, (tm, LANES), 1)
        in_rope = (lane >= MLA_NOPE) & (lane < MLA_NOPE + MLA_ROPE)
        cos_t = jnp.where(in_rope, jnp.cos(ang), 1.0)
        sin_t = jnp.where(in_rope, jnp.sin(ang), 0.0)
        sin_t = jnp.where(lane < MLA_NOPE + MLA_ROPE // 2, -sin_t, sin_t)
        cos_ref[...] = cos_t
        sin_ref[...] = sin_t
        kr = p_ref[:, 384:512]
        kr = kr * cos_t + _rope_partner(kr) * sin_t
        qraw = _dot(qn, wq_ref[...])
        kvk = _dot(kvn, wk_ref[...])
        for h in range(MLA_HEADS):
            sl = slice(h * LANES, (h + 1) * LANES)
            qh = qraw[:, sl]
            q_ref[:, sl] = ((qh * cos_t + _rope_partner(qh) * sin_t) * (ATT_SCALE * LOG2E)).astype(BF16)
            k_ref[:, sl] = (kvk[:, sl] + kr).astype(BF16)
        v_ref[...] = _dot(kvn, wv_ref[...]).astype(BF16)
        xp = p_ref[:, 512:1024]
        ext[0:POOL_HALO, :] = jnp.where(i > 0, halo_ref[...], 0.0)
        ext[POOL_HALO:POOL_HALO + tm, :] = xp
        for g, w in enumerate(POOL_WINDOWS):
            lo = g * POOL_GROUP
            acc = ext[POOL_HALO:POOL_HALO + tm, lo:lo + POOL_GROUP]
            for s in range(1, w):
                acc = acc + ext[POOL_HALO - s:POOL_HALO - s + tm, lo:lo + POOL_GROUP]
            cnt = _pool_counts(i * tm, tm, w)
            d_ref[:, lo:lo + POOL_GROUP] = (acc / cnt - xp[:, lo:lo + POOL_GROUP]).astype(BF16)
        yb_ref[...] = _dot(d_ref[...], wp_ref[...])

    row = lambda w: pl.BlockSpec((tm, w), lambda i: (i, 0))
    vec = lambda w: pl.BlockSpec((1, w), lambda i: (0, 0))
    whole = lambda a: pl.BlockSpec(a.shape, lambda i: (0, 0))
    return pl.pallas_call(
        body, name=name, grid=(t // tm,),
        in_specs=[row(1024), pl.BlockSpec((POOL_HALO, POOL_WIDTH), lambda i: (jnp.maximum(i * hb - 1, 0), 1)),
                  pl.BlockSpec((tm, 1), lambda i: (i, 0)), vec(LANES), vec(MLA_Q_RANK), vec(MLA_KV_RANK),
                  whole(wq), whole(wk), whole(wv), whole(wpool)],
        out_specs=[row(1024), row(1024), row(512), row(512), row(MLA_Q_RANK), row(MLA_KV_RANK), row(POOL_WIDTH),
                   row(LANES), row(LANES)],
        out_shape=[jax.ShapeDtypeStruct((t, 1024), BF16), jax.ShapeDtypeStruct((t, 1024), BF16),
                   jax.ShapeDtypeStruct((t, 512), BF16), jax.ShapeDtypeStruct((t, 512), F32),
                   jax.ShapeDtypeStruct((t, MLA_Q_RANK), BF16), jax.ShapeDtypeStruct((t, MLA_KV_RANK), BF16),
                   jax.ShapeDtypeStruct((t, POOL_WIDTH), BF16), jax.ShapeDtypeStruct((t, LANES), F32),
                   jax.ShapeDtypeStruct((t, LANES), F32)],
        scratch_shapes=[pltpu.VMEM((tm + POOL_HALO, POOL_WIDTH), F32)],
        compiler_params=_params(("parallel",)),
    )(proj, proj, pos, inv_freq, q_a_norm, kv_a_norm, wq, wk, wv, wpool)


def _ab_prep_bwd(proj, q_a_norm, kv_a_norm, dq, dk, dv, cos_t, sin_t, dyb, dz, qn, kvn, d, hn, wq, wk, wv, wpool, *, name):
    t = proj.shape[0]
    tm = _tile(t, ROW_TILE)
    hb = tm // POOL_HALO
    last_halo = t // POOL_HALO - 1
    nt = t // tm

    def body(p_ref, qg_ref, kg_ref, dq_ref, dk_ref, dv_ref, c_ref, s_ref, dyb_ref, dybn_ref, dz_ref,
             qn_ref, kvn_ref, d_ref, hn_ref, wq_ref, wk_ref, wv_ref, wp_ref,
             dp_ref, dqg_ref, dkg_ref, dwq_ref, dwk_ref, dwv_ref, dwp_ref, dwin_ref, ext, dqr_ref, dkb_ref):
        i = pl.program_id(0)

        @pl.when(i == 0)
        def _():
            for ref in (dqg_ref, dkg_ref, dwq_ref, dwk_ref, dwv_ref, dwp_ref, dwin_ref):
                ref[...] = jnp.zeros_like(ref)

        def norm_bwd(x, g, dy, dg_ref):
            r = lax.rsqrt(jnp.mean(x * x, axis=-1, keepdims=True) + RMS_EPS)
            xh = x * r
            dxh = dy * g
            dg_ref[...] += jnp.sum(dy * xh, axis=0, keepdims=True)
            return r * (dxh - xh * jnp.mean(dxh * xh, axis=-1, keepdims=True))

        c, s = c_ref[...], s_ref[...]
        lane = lax.broadcasted_iota(jnp.int32, (tm, LANES), 1)
        in_rope = (lane >= MLA_NOPE) & (lane < MLA_NOPE + MLA_ROPE)
        dkr = jnp.zeros((tm, LANES), F32)
        for h in range(MLA_HEADS):
            sl = slice(h * LANES, (h + 1) * LANES)
            g = dq_ref[:, sl]
            dqr_ref[:, sl] = ((g * c + _rope_partner(g * s)) * ATT_SCALE).astype(BF16)
            gk = dk_ref[:, sl]
            dkb_ref[:, sl] = gk.astype(BF16)
            dkr = dkr + jnp.where(in_rope, gk, 0.0)
        dkr = dkr * c + _rope_partner(dkr * s)
        dqn = _dot(dqr_ref[...], wq_ref[...], NT)
        dkvn = _dot(dkb_ref[...], wk_ref[...], NT) + _dot(dv_ref[...], wv_ref[...], NT)
        dql = norm_bwd(p_ref[:, 0:MLA_Q_RANK], qg_ref[...], dqn, dqg_ref)
        dp_ref[:, 0:MLA_Q_RANK] = dql.astype(BF16)
        dkl = norm_bwd(p_ref[:, MLA_Q_RANK:384], kg_ref[...], dkvn, dkg_ref)
        dp_ref[:, MLA_Q_RANK:384] = dkl.astype(BF16)
        dp_ref[:, 384:512] = dkr.astype(BF16)
        ddv = _dot(dyb_ref[...], wp_ref[...], NT)
        ddn = _dot(dybn_ref[...], wp_ref[...], NT)
        for g, w in enumerate(POOL_WINDOWS):
            lo = g * POOL_GROUP
            ext[0:tm, lo:lo + POOL_GROUP] = ddv[:, lo:lo + POOL_GROUP] / _pool_counts(i * tm, tm, w)
            nxt = ddn[:, lo:lo + POOL_GROUP] / _pool_counts((i + 1) * tm, POOL_HALO, w)
            ext[tm:tm + POOL_HALO, lo:lo + POOL_GROUP] = jnp.where(i < nt - 1, nxt, 0.0)
        for g, w in enumerate(POOL_WINDOWS):
            lo = g * POOL_GROUP
            acc = ext[0:tm, lo:lo + POOL_GROUP]
            for s in range(1, w):
                acc = acc + ext[s:s + tm, lo:lo + POOL_GROUP]
            dp_ref[:, 512 + lo:512 + lo + POOL_GROUP] = (acc - ddv[:, lo:lo + POOL_GROUP]).astype(BF16)
        dp_ref[:, 1024:2048] = dz_ref[...]
        dwq_ref[...] += _dot(qn_ref[...], dqr_ref[...], TN)
        dwk_ref[...] += _dot(kvn_ref[...], dkb_ref[...], TN)
        dwv_ref[...] += _dot(kvn_ref[...], dv_ref[...], TN)
        dwp_ref[...] += _dot(d_ref[...], dyb_ref[...], TN)
        dwin_ref[...] += _dot(dp_ref[...], hn_ref[...], TN)

    row = lambda w: pl.BlockSpec((tm, w), lambda i: (i, 0))
    vec = lambda w: pl.BlockSpec((1, w), lambda i: (0, 0))
    whole = lambda a: pl.BlockSpec(a.shape, lambda i: (0, 0))
    weights = (wq, wk, wv, wpool)
    return pl.pallas_call(
        body, name=name, grid=(nt,),
        in_specs=[row(1024), vec(MLA_Q_RANK), vec(MLA_KV_RANK), row(1024), row(1024), row(512), row(LANES), row(LANES),
                  row(POOL_WIDTH),
                  pl.BlockSpec((POOL_HALO, POOL_WIDTH), lambda i: (jnp.minimum((i + 1) * hb, last_halo), 0)),
                  row(1024), row(MLA_Q_RANK), row(MLA_KV_RANK), row(POOL_WIDTH), row(1024)] + [whole(w) for w in weights],
        out_specs=[row(IN_AB_PAD), vec(MLA_Q_RANK), vec(MLA_KV_RANK)] + [whole(w) for w in weights]
        + [pl.BlockSpec((IN_AB_PAD, 1024), lambda i: (0, 0))],
        out_shape=[jax.ShapeDtypeStruct((t, IN_AB_PAD), BF16), jax.ShapeDtypeStruct((1, MLA_Q_RANK), F32),
                   jax.ShapeDtypeStruct((1, MLA_KV_RANK), F32)] + [jax.ShapeDtypeStruct(w.shape, F32) for w in weights]
        + [jax.ShapeDtypeStruct((IN_AB_PAD, 1024), F32)],
        scratch_shapes=[pltpu.VMEM((tm + POOL_HALO, POOL_WIDTH), F32), pltpu.VMEM((tm, 1024), BF16),
                        pltpu.VMEM((tm, 1024), BF16)],
        compiler_params=_params(("arbitrary",)),
    )(proj, q_a_norm, kv_a_norm, dq, dk, dv, cos_t, sin_t, dyb, dyb, dz, qn, kvn, d, hn, wq, wk, wv, wpool)


def _gate_out_proj(o, ybraw, proj, pool_scale, w, hres, *, name):
    t = o.shape[0]
    tm = _tile(t, 2 * ROW_TILE)

    def body(o_ref, yb_ref, z_ref, ps_ref, w_ref, h_ref, ho_ref, y_ref):
        z = z_ref[...]
        sz = z * _sigmoid(z)
        y_ref[:, 0:512] = (o_ref[...] * sz[:, 0:512]).astype(BF16)
        y_ref[:, 512:1024] = (yb_ref[...] * ps_ref[...] * sz[:, 512:1024]).astype(BF16)
        ho_ref[...] = h_ref[...] + _dot(y_ref[...], w_ref[...])

    row = lambda w_: pl.BlockSpec((tm, w_), lambda i: (i, 0))
    return pl.pallas_call(
        body, name=name, grid=(t // tm,),
        in_specs=[row(512), row(512), pl.BlockSpec((tm, 1024), lambda i: (i, 1)), pl.BlockSpec((1, 512), lambda i: (0, 0)),
                  pl.BlockSpec(w.shape, lambda i: (0, 0)), row(1024)],
        out_specs=[row(1024), row(1024)],
        out_shape=[jax.ShapeDtypeStruct((t, 1024), F32), jax.ShapeDtypeStruct((t, 1024), BF16)],
        compiler_params=_params(("parallel",)),
    )(o, ybraw, proj, pool_scale, w, hres)


def _gate_bwd(dh, w, o, ybraw, proj, pool_scale, *, name):
    t = o.shape[0]
    tm = _tile(t, ROW_TILE)

    def body(dh_ref, w_ref, o_ref, yb_ref, z_ref, ps_ref, do_ref, dl_ref, dyb_ref, dz_ref, dps_ref):
        i = pl.program_id(0)
        z = z_ref[...]
        sg = _sigmoid(z)
        sz = z * sg
        dsz = sg * (1.0 + z * (1.0 - sg))
        dyv = _dot(dh_ref[...], w_ref[...], NT)
        dcat = dyv * sz
        ov = o_ref[...]
        ybs = yb_ref[...] * ps_ref[...]
        dz_ref[:, 0:512] = (dyv[:, 0:512] * ov * dsz[:, 0:512]).astype(BF16)
        dz_ref[:, 512:1024] = (dyv[:, 512:1024] * ybs * dsz[:, 512:1024]).astype(BF16)
        do = dcat[:, 0:512]
        do_ref[...] = do.astype(BF16)
        r_i = (lax.broadcasted_iota(jnp.int32, (1024, 512), 0) % 512) // MLA_V
        c_i = lax.broadcasted_iota(jnp.int32, (1024, 512), 1) // MLA_V
        prod = do * ov
        hi = prod.astype(BF16)
        lo = (prod - hi.astype(F32)).astype(BF16)
        dl_ref[...] = _dot(jnp.concatenate([hi, lo], axis=1), (r_i == c_i).astype(BF16))
        dyb_ref[...] = (dcat[:, 512:1024] * ps_ref[...]).astype(BF16)

        @pl.when(i == 0)
        def _():
            dps_ref[...] = jnp.zeros_like(dps_ref)

        dps_ref[...] += jnp.sum(dcat[:, 512:1024] * yb_ref[...], axis=0, keepdims=True)

    row = lambda w: pl.BlockSpec((tm, w), lambda i: (i, 0))
    vec = pl.BlockSpec((1, 512), lambda i: (0, 0))
    return pl.pallas_call(
        body, name=name, grid=(t // tm,),
        in_specs=[row(1024), pl.BlockSpec(w.shape, lambda i: (0, 0)), row(512), row(512),
                  pl.BlockSpec((tm, 1024), lambda i: (i, 1)), vec],
        out_specs=[row(512), row(512), row(512), row(1024), vec],
        out_shape=[jax.ShapeDtypeStruct((t, 512), BF16), jax.ShapeDtypeStruct((t, 512), F32),
                   jax.ShapeDtypeStruct((t, 512), BF16), jax.ShapeDtypeStruct((t, 1024), BF16),
                   jax.ShapeDtypeStruct((1, 512), F32)],
        compiler_params=_params(("arbitrary",)),
    )(dh, w, o, ybraw, proj, pool_scale)


ATT_HP_FWD = 4
ATT_HP_BWD = 2


def _diag_mask(tq):
    return lax.broadcasted_iota(jnp.int32, (tq, tq), 1) <= lax.broadcasted_iota(jnp.int32, (tq, tq), 0)


def _block_schedule(nq, key_major):
    if key_major:
        pairs = [(qi, ki) for ki in range(nq) for qi in range(ki, nq)]
    else:
        pairs = [(qi, ki) for qi in range(nq) for ki in range(qi + 1)]
    return jnp.asarray([p[0] for p in pairs], jnp.int32), jnp.asarray([p[1] for p in pairs], jnp.int32)


def _attn_fwd(q, k, v, *, name):
    t = q.shape[0]
    tq = _tile(t, ATT_TILE)
    nq = t // tq
    hp = ATT_HP_FWD
    qi_tab, ki_tab = _block_schedule(nq, key_major=False)

    def body(qi_ref, ki_ref, q_ref, k_ref, v_ref, o_ref, lse_ref, m_sc, l_sc, acc_sc):
        step = pl.program_id(1)
        qi, ki = qi_ref[step], ki_ref[step]

        @pl.when(ki == 0)
        def _():
            m_sc[...] = jnp.full_like(m_sc, -jnp.inf)
            l_sc[...] = jnp.zeros_like(l_sc)
            acc_sc[...] = jnp.zeros_like(acc_sc)

        def block(on_diagonal):
            scores = []
            for h in range(hp):
                sl = slice(h * LANES, (h + 1) * LANES)
                scores.append(_dot(q_ref[:, sl], k_ref[:, sl], NT))
            if on_diagonal:
                mask = _diag_mask(tq)
                scores = [jnp.where(mask, s, -jnp.inf) for s in scores]
            for h, s in enumerate(scores):
                vv = v_ref[:, (h // 2) * LANES:(h // 2 + 1) * LANES]
                m_prev = m_sc[h]
                m_new = jnp.maximum(m_prev, jnp.max(s, axis=-1, keepdims=True))
                alpha = jnp.exp2(m_prev - m_new)
                p = jnp.exp2(s - m_new[:, 0:1])
                l_sc[h] = alpha * l_sc[h] + jnp.sum(p, axis=-1, keepdims=True)
                acc_sc[h] = alpha * acc_sc[h] + _dot(p.astype(BF16), vv)
                m_sc[h] = m_new

        pl.when(ki < qi)(functools.partial(block, False))
        pl.when(ki == qi)(functools.partial(block, True))

        @pl.when(ki == qi)
        def _():
            first = lax.broadcasted_iota(jnp.int32, (tq, LANES), 1) < MLA_V
            for pr in range(hp // 2):
                a, b = 2 * pr, 2 * pr + 1
                sl = slice(pr * LANES, (pr + 1) * LANES)
                o_ref[:, sl] = jnp.where(first, acc_sc[a] / l_sc[a], acc_sc[b] / l_sc[b])
                lse_ref[:, sl] = jnp.where(first, m_sc[a] + jnp.log2(l_sc[a]), m_sc[b] + jnp.log2(l_sc[b]))

    grid_spec = pltpu.PrefetchScalarGridSpec(
        num_scalar_prefetch=2, grid=(MLA_HEADS // hp, qi_tab.shape[0]),
        in_specs=[pl.BlockSpec((tq, hp * LANES), lambda g, s, qt, kt: (qt[s], g)),
                  pl.BlockSpec((tq, hp * LANES), lambda g, s, qt, kt: (kt[s], g)),
                  pl.BlockSpec((tq, hp * MLA_V), lambda g, s, qt, kt: (kt[s], g))],
        out_specs=[pl.BlockSpec((tq, hp * MLA_V), lambda g, s, qt, kt: (qt[s], g)),
                   pl.BlockSpec((tq, hp * MLA_V), lambda g, s, qt, kt: (qt[s], g))],
        scratch_shapes=[pltpu.VMEM((hp, tq, LANES), F32)] * 3,
    )
    return pl.pallas_call(
        body, name=name, grid_spec=grid_spec,
        out_shape=[jax.ShapeDtypeStruct((t, 512), F32), jax.ShapeDtypeStruct((t, 512), F32)],
        compiler_params=_params(("parallel", "arbitrary")),
    )(qi_tab, ki_tab, q, k, v)


def _attn_bwd(q, k, v, do, lse, delta, *, name):
    t = q.shape[0]
    tq = _tile(t, ATT_TILE)
    nq = t // tq
    hp = ATT_HP_BWD
    qi_tab, ki_tab = _block_schedule(nq, key_major=True)

    def body(qi_ref, ki_ref, q_ref, k_ref, v_ref, do_ref, lse_ref, dl_ref, dq_ref, dk_ref, dv_ref, dk_sc, dv_sc):
        step = pl.program_id(1)
        qi, ki = qi_ref[step], ki_ref[step]

        @pl.when(step == 0)
        def _():
            dq_ref[...] = jnp.zeros_like(dq_ref)

        @pl.when(qi == ki)
        def _():
            dk_sc[...] = jnp.zeros_like(dk_sc)
            dv_sc[...] = jnp.zeros_like(dv_sc)

        def block(on_diagonal):
            lane = lax.broadcasted_iota(jnp.int32, (tq, LANES), 1)
            rows = pl.ds(pl.multiple_of(qi * tq, tq), tq)
            heads = [slice(h * LANES, (h + 1) * LANES) for h in range(hp)]
            scores = [_dot(q_ref[:, sl], k_ref[:, sl], NT) for sl in heads]
            dps = []
            for h in range(hp):
                dov = do_ref[:, (h // 2) * LANES:(h // 2 + 1) * LANES]
                mine = (lane < MLA_V) if h % 2 == 0 else (lane >= MLA_V)
                dps.append(_dot(jnp.where(mine, dov, jnp.zeros_like(dov)), v_ref[:, (h // 2) * LANES:(h // 2 + 1) * LANES], NT))
            mask = _diag_mask(tq) if on_diagonal else None
            for h, sl in enumerate(heads):
                col = (h // 2) * LANES + (h % 2) * MLA_V
                p = jnp.exp2(scores[h] - lse_ref[:, col:col + 1])
                if on_diagonal:
                    p = jnp.where(mask, p, 0.0)
                ds = (p * (dps[h] - dl_ref[:, col:col + 1])).astype(BF16)
                dv_sc[h] += _dot(p.astype(BF16), do_ref[:, (h // 2) * LANES:(h // 2 + 1) * LANES], TN)
                dk_sc[h] += _dot(ds, q_ref[:, sl], TN)
                dq_ref[rows, sl] += _dot(ds, k_ref[:, sl], NN)

        pl.when(qi > ki)(functools.partial(block, False))
        pl.when(qi == ki)(functools.partial(block, True))

        @pl.when(qi == nq - 1)
        def _():
            first = lax.broadcasted_iota(jnp.int32, (tq, LANES), 1) < MLA_V
            for h in range(hp):
                dk_ref[:, h * LANES:(h + 1) * LANES] = dk_sc[h] * (1.0 / LOG2E)
            for pr in range(hp // 2):
                dv_ref[:, pr * LANES:(pr + 1) * LANES] = jnp.where(first, dv_sc[2 * pr], dv_sc[2 * pr + 1]).astype(BF16)

    qrow = lambda w: pl.BlockSpec((tq, w), lambda g, s, qt, kt: (qt[s], g))
    krow = lambda w: pl.BlockSpec((tq, w), lambda g, s, qt, kt: (kt[s], g))
    grid_spec = pltpu.PrefetchScalarGridSpec(
        num_scalar_prefetch=2, grid=(MLA_HEADS // hp, qi_tab.shape[0]),
        in_specs=[qrow(hp * LANES), krow(hp * LANES), krow(hp * MLA_V), qrow(hp * MLA_V), qrow(hp * MLA_V), qrow(hp * MLA_V)],
        out_specs=[pl.BlockSpec((t, hp * LANES), lambda g, s, qt, kt: (0, g)), krow(hp * LANES), krow(hp * MLA_V)],
        scratch_shapes=[pltpu.VMEM((hp, tq, LANES), F32), pltpu.VMEM((hp, tq, LANES), F32)],
    )
    return pl.pallas_call(
        body, name=name, grid_spec=grid_spec,
        out_shape=[jax.ShapeDtypeStruct((t, 1024), F32), jax.ShapeDtypeStruct((t, 1024), F32),
                   jax.ShapeDtypeStruct((t, 512), BF16)],
        compiler_params=_params(("parallel", "arbitrary")),
    )(qi_tab, ki_tab, q, k, v, do, lse, delta)


def _conv_rows(ext, tm, w_ref, sec):
    c0 = sec * 1024
    y = ext[CONV_HALO - 3:CONV_HALO - 3 + tm, c0:c0 + 1024] * w_ref[0:1, c0:c0 + 1024]
    for j in range(1, CONV_WIDTH):
        y = y + ext[CONV_HALO - 3 + j:CONV_HALO - 3 + j + tm, c0:c0 + 1024] * w_ref[j:j + 1, c0:c0 + 1024]
    return y


def _c_prep(proj_c, conv_w, a_log, dt_bias, *, name):
    t = proj_c.shape[0]
    tm = _tile(t, ROW_TILE)
    hb = tm // CONV_HALO

    def body(p_ref, halo_ref, ab_ref, w_ref, al_ref, dtb_ref, q_ref, k_ref, v_ref, g_ref, b_ref, gt_ref, ext):
        i = pl.program_id(0)
        ext[0:CONV_HALO, :] = jnp.where(i > 0, halo_ref[...], 0.0)
        ext[CONV_HALO:CONV_HALO + tm, :] = p_ref[...]
        for sec, o_ref in enumerate((q_ref, k_ref, v_ref)):
            y = _conv_rows(ext, tm, w_ref, sec)
            y = y * _sigmoid(y)
            if sec == 2:
                o_ref[...] = y
                continue
            scale = GDN_DK ** -0.5 if sec == 0 else 1.0
            for h in range(GDN_HEADS):
                sl = slice(h * LANES, (h + 1) * LANES)
                blk = y[:, sl]
                r = lax.rsqrt(jnp.sum(blk * blk, axis=-1, keepdims=True) + RMS_EPS)
                o_ref[:, sl] = blk * (r * scale)
        ab = ab_ref[...]
        g = -jnp.exp(al_ref[...]) * _softplus(ab + dtb_ref[...])
        beta = _sigmoid(ab)
        ri = lax.broadcasted_iota(jnp.int32, (tm, tm), 0)
        ci = lax.broadcasted_iota(jnp.int32, (tm, tm), 1)
        lower = ((ri // CHUNK) == (ci // CHUNK)) & (ri >= ci)
        gc = _dot(lower.astype(F32), g, NN, HI)
        eye = lax.broadcasted_iota(jnp.int32, (LANES, LANES), 0) == lax.broadcasted_iota(jnp.int32, (LANES, LANES), 1)
        gt_ref[...] = _dot(eye.astype(F32), gc, NT, HI)[0:GDN_HEADS, :]
        for h in range(GDN_HEADS):
            sl = slice(h * LANES, (h + 1) * LANES)
            g_ref[:, sl] = jnp.broadcast_to(gc[:, h:h + 1], (tm, LANES))
            b_ref[:, sl] = jnp.broadcast_to(beta[:, GDN_HEADS + h:GDN_HEADS + h + 1], (tm, LANES))

    row = lambda w: pl.BlockSpec((tm, w), lambda i: (i, 0))
    vec = lambda r, w: pl.BlockSpec((r, w), lambda i: (0, 0))
    out = jax.ShapeDtypeStruct((t, 1024), F32)
    return pl.pallas_call(
        body, name=name, grid=(t // tm,),
        in_specs=[row(3072), pl.BlockSpec((CONV_HALO, 3072), lambda i: (jnp.maximum(i * hb - 1, 0), 0)),
                  pl.BlockSpec((tm, LANES), lambda i: (i, 32)), vec(CONV_WIDTH, 3072), vec(1, LANES), vec(1, LANES)],
        out_specs=[row(1024)] * 5 + [pl.BlockSpec((GDN_HEADS, tm), lambda i: (0, i))],
        out_shape=[out] * 5 + [jax.ShapeDtypeStruct((GDN_HEADS, t), F32)],
        scratch_shapes=[pltpu.VMEM((tm + CONV_HALO, 3072), F32)],
        compiler_params=_params(("parallel",)),
    )(proj_c, proj_c, proj_c, conv_w, a_log, dt_bias)


def _c_prep_bwd(proj_c, conv_w, a_log, dt_bias, dq, dk, dv, dgb, dbb, dz, *, name):
    t = proj_c.shape[0]
    tm = _tile(t, ROW_TILE)
    hb = tm // CONV_HALO
    nt = t // tm
    rev = lambda i: nt - 1 - i

    def body(p_ref, halo_ref, ab_ref, w_ref, al_ref, dtb_ref, dq_ref, dk_ref, dv_ref, dg_ref, db_ref, dz_ref,
             dp_ref, dw_ref, dal_ref, ddt_ref, ext, dyext, carry, taps):
        step = pl.program_id(0)
        i = rev(step)

        @pl.when(step == 0)
        def _():
            dw_ref[...] = jnp.zeros_like(dw_ref)
            dal_ref[...] = jnp.zeros_like(dal_ref)
            ddt_ref[...] = jnp.zeros_like(ddt_ref)
            carry[...] = jnp.zeros_like(carry)

        ext[0:CONV_HALO, :] = jnp.where(i > 0, halo_ref[...], 0.0)
        ext[CONV_HALO:CONV_HALO + tm, :] = p_ref[...]
        for sec, g_ref in enumerate((dq_ref, dk_ref, dv_ref)):
            c0 = sec * 1024
            for j in range(CONV_WIDTH):
                taps[j] = ext[CONV_HALO - 3 + j:CONV_HALO - 3 + j + tm, c0:c0 + 1024]
            y = taps[0] * w_ref[0:1, c0:c0 + 1024]
            for j in range(1, CONV_WIDTH):
                y = y + taps[j] * w_ref[j:j + 1, c0:c0 + 1024]
            sg = _sigmoid(y)
            act = y * sg
            if sec == 2:
                dact = g_ref[...]
            else:
                scale = GDN_DK ** -0.5 if sec == 0 else 1.0
                parts = []
                for h in range(GDN_HEADS):
                    sl = slice(h * LANES, (h + 1) * LANES)
                    blk = act[:, sl]
                    r = lax.rsqrt(jnp.sum(blk * blk, axis=-1, keepdims=True) + RMS_EPS)
                    n = blk * r
                    dn = g_ref[:, sl] * scale
                    parts.append(r * (dn - n * jnp.sum(dn * n, axis=-1, keepdims=True)))
                dact = jnp.concatenate(parts, axis=-1)
            dy = dact * (sg * (1.0 + y * (1.0 - sg)))
            dyext[0:tm, c0:c0 + 1024] = dy
            for j in range(CONV_WIDTH):
                dw_ref[j:j + 1, c0:c0 + 1024] += jnp.sum(dy * taps[j], axis=0, keepdims=True)
        dyext[tm:tm + CONV_HALO, :] = carry[...]
        carry[...] = dyext[0:CONV_HALO, :]
        for sec in range(3):
            c0 = sec * 1024
            dx = dyext[3:3 + tm, c0:c0 + 1024] * w_ref[0:1, c0:c0 + 1024]
            for j in range(1, CONV_WIDTH):
                dx = dx + dyext[3 - j:3 - j + tm, c0:c0 + 1024] * w_ref[j:j + 1, c0:c0 + 1024]
            dp_ref[:, c0:c0 + 1024] = dx.astype(BF16)
        dp_ref[:, 3072:4096] = dz_ref[...]
        lane = lax.broadcasted_iota(jnp.int32, (tm, LANES), 1)
        dg = jnp.zeros((tm, LANES), F32)
        dbeta = jnp.zeros((tm, LANES), F32)
        for h in range(GDN_HEADS):
            sl = slice(h * LANES, (h + 1) * LANES)
            dg = dg + jnp.where(lane == h, dg_ref[:, sl], 0.0)
            dbeta = dbeta + jnp.where(lane == GDN_HEADS + h, db_ref[:, sl], 0.0)
        ri = lax.broadcasted_iota(jnp.int32, (tm, tm), 0)
        ci = lax.broadcasted_iota(jnp.int32, (tm, tm), 1)
        upper = ((ri // CHUNK) == (ci // CHUNK)) & (ri <= ci)
        dg = _dot(upper.astype(F32), dg, NN, HI)
        pre = ab_ref[...] + dtb_ref[...]
        s = _sigmoid(pre)
        a_exp = jnp.exp(al_ref[...])
        dg_da = dg * (-a_exp * s)
        dp_ref[:, 4096:IN_C_PAD] = (dg_da + dbeta * s * (1.0 - s)).astype(BF16)
        dal_ref[...] += jnp.sum(dg * (-a_exp * _softplus(pre)), axis=0, keepdims=True)
        ddt_ref[...] += jnp.sum(dg_da, axis=0, keepdims=True)

    row = lambda w: pl.BlockSpec((tm, w), lambda s: (rev(s), 0))
    vec = lambda r, w: pl.BlockSpec((r, w), lambda s: (0, 0))
    return pl.pallas_call(
        body, name=name, grid=(nt,),
        in_specs=[row(3072), pl.BlockSpec((CONV_HALO, 3072), lambda s: (jnp.maximum(rev(s) * hb - 1, 0), 0)),
                  pl.BlockSpec((tm, LANES), lambda s: (rev(s), 32)), vec(CONV_WIDTH, 3072), vec(1, LANES), vec(1, LANES),
                  row(1024), row(1024), row(1024), row(1024), row(1024), row(1024)],
        out_specs=[row(IN_C_PAD), vec(CONV_WIDTH, 3072), vec(1, LANES), vec(1, LANES)],
        out_shape=[jax.ShapeDtypeStruct((t, IN_C_PAD), BF16), jax.ShapeDtypeStruct((CONV_WIDTH, 3072), F32),
                   jax.ShapeDtypeStruct((1, LANES), F32), jax.ShapeDtypeStruct((1, LANES), F32)],
        scratch_shapes=[pltpu.VMEM((tm + CONV_HALO, 3072), F32), pltpu.VMEM((tm + CONV_HALO, 3072), F32),
                        pltpu.VMEM((CONV_HALO, 3072), F32), pltpu.VMEM((CONV_WIDTH, tm, 1024), F32)],
        compiler_params=_params(("arbitrary",)),
    )(proj_c, proj_c, proj_c, conv_w, a_log, dt_bias, dq, dk, dv, dgb, dbb, dz)


def _o_gate_bwd(dh, w, o, proj_c, o_norm, *, name):
    t = o.shape[0]
    tm = _tile(t, 2 * ROW_TILE)

    def body(dh_ref, w_ref, o_ref, z_ref, g_ref, do_ref, dz_ref, dg_ref, dy_ref):
        i = pl.program_id(0)

        @pl.when(i == 0)
        def _():
            dg_ref[...] = jnp.zeros_like(dg_ref)

        dy_ref[...] = _dot(dh_ref[...], w_ref[...], NT)
        dg = jnp.zeros((1, LANES), F32)
        for h in range(GDN_HEADS):
            sl = slice(h * LANES, (h + 1) * LANES)
            x = o_ref[:, sl]
            r = lax.rsqrt(jnp.mean(x * x, axis=-1, keepdims=True) + RMS_EPS)
            xh = x * r
            z = z_ref[:, sl]
            sg = _sigmoid(z)
            dyv = dy_ref[:, sl]
            dn = dyv * (z * sg)
            dz_ref[:, sl] = (dyv * xh * g_ref[...] * (sg * (1.0 + z * (1.0 - sg)))).astype(BF16)
            dxh = dn * g_ref[...]
            do_ref[:, sl] = r * (dxh - xh * jnp.mean(dxh * xh, axis=-1, keepdims=True))
            dg = dg + jnp.sum(dn * xh, axis=0, keepdims=True)
        dg_ref[...] += dg

    row = pl.BlockSpec((tm, 1024), lambda i: (i, 0))
    vec = pl.BlockSpec((1, LANES), lambda i: (0, 0))
    return pl.pallas_call(
        body, name=name, grid=(t // tm,),
        in_specs=[row, pl.BlockSpec(w.shape, lambda i: (0, 0)), row, pl.BlockSpec((tm, 1024), lambda i: (i, 3)), vec],
        out_specs=[row, row, vec],
        out_shape=[jax.ShapeDtypeStruct((t, 1024), F32), jax.ShapeDtypeStruct((t, 1024), BF16),
                   jax.ShapeDtypeStruct((1, LANES), F32)],
        scratch_shapes=[pltpu.VMEM((tm, 1024), F32)],
        compiler_params=_params(("arbitrary",)),
    )(dh, w, o, proj_c, o_norm)


PAIR = 2 * CHUNK
GDN_HP = 8


def _bdot(a, b, dims=NN):
    return _dot(a.astype(BF16), b.astype(BF16), dims)


def _each(f, *lists):
    return [f(*args) for args in zip(*lists)]


def _pair_common(q, k, v, gci, gcj, beta):
    ri = lax.broadcasted_iota(jnp.int32, (PAIR, PAIR), 0)
    ci = lax.broadcasted_iota(jnp.int32, (PAIR, PAIR), 1)
    same = (ri // CHUNK) == (ci // CHUNK)
    incl = same & (ri >= ci)
    strict = same & (ri > ci)
    eye = (ri == ci).astype(F32)
    first = lax.broadcasted_iota(jnp.int32, (PAIR, LANES), 0) < CHUNK
    gamma = _each(lambda gi, gj: jnp.where(incl, jnp.exp(jnp.minimum(gi - gj, 0.0)), 0.0), gci, gcj)
    kb = _each(jnp.multiply, k, beta)
    kk = _each(lambda a, b: _bdot(a, b, NT), kb, k)
    qk = _each(lambda a, b: _bdot(a, b, NT), q, k)
    m = _each(lambda x, g: jnp.where(strict, x * g, 0.0), kk, gamma)
    tm_ = _each(lambda x: eye - x, m)
    pw = _each(lambda x: _bdot(x, x), m)
    for it in range(5):
        tm_ = _each(lambda x, p: x + _bdot(x, p), tm_, pw)
        if it < 4:
            pw = _each(lambda p: _bdot(p, p), pw)
    eg = _each(jnp.exp, gci)
    vb = _each(jnp.multiply, v, beta)
    kbe = _each(jnp.multiply, kb, eg)
    uw = _each(lambda x, a, b: _bdot(x, jnp.concatenate([a, b], axis=1)), tm_, vb, kbe)
    attn = _each(lambda x, g: jnp.where(incl, x * g, 0.0), qk, gamma)
    gl_a = _each(lambda g: g[CHUNK - 1:CHUNK, :], gci)
    gl_b = _each(lambda g: g[PAIR - 1:PAIR, :], gci)
    ek = _each(lambda a, b, g: jnp.exp(jnp.where(first, a, b) - g), gl_a, gl_b, gci)
    return dict(incl=incl, strict=strict, gamma=gamma, kb=kb, m=m, tm=tm_, eg=eg, vb=vb, kbe=kbe,
                u=_each(lambda x: x[:, :LANES], uw), w=_each(lambda x: x[:, LANES:], uw), attn=attn,
                qd=_each(jnp.multiply, q, eg), ek=ek, kd=_each(jnp.multiply, k, ek),
                glast_a=_each(jnp.exp, gl_a), glast_b=_each(jnp.exp, gl_b))


def _gdn_specs(t, ts, order):
    nc = ts // CHUNK
    blk = pl.BlockSpec((ts, GDN_HP * LANES), lambda h, s: (order(s), h))
    row = pl.BlockSpec((GDN_HP, 1, ts), lambda h, s: (h, 0, order(s)))
    st = pl.BlockSpec((GDN_HP, nc, LANES, LANES), lambda h, s: (h, order(s), 0, 0))
    return blk, row, st


def _gdn_fwd(q, k, v, gcb, gct, bb, *, name):
    t = q.shape[0]
    ts = _tile(t, GDN_TILE)
    npair = ts // PAIR

    def body(q_ref, k_ref, v_ref, g_ref, gt_ref, b_ref, o_ref, st_ref, s_sc):
        @pl.when(pl.program_id(1) == 0)
        def _():
            s_sc[...] = jnp.zeros_like(s_sc)

        def pair(pi, _):
            rows = pl.ds(pl.multiple_of(pi * PAIR, PAIR), PAIR)
            heads = [slice(hh * LANES, (hh + 1) * LANES) for hh in range(GDN_HP)]
            c = CHUNK
            cat0 = lambda *xs: jnp.concatenate(xs, axis=0)
            s0 = [s_sc[hh] for hh in range(GDN_HP)]
            cm = _pair_common([q_ref[rows, sl] for sl in heads], [k_ref[rows, sl] for sl in heads],
                              [v_ref[rows, sl] for sl in heads], [g_ref[rows, sl] for sl in heads],
                              [gt_ref[hh, :, rows] for hh in range(GDN_HP)], [b_ref[rows, sl] for sl in heads])
            u, w, qd, kd = cm["u"], cm["w"], cm["qd"], cm["kd"]
            r0 = _each(lambda w_, q_, s: _bdot(cat0(w_[:c], q_[:c]), s), w, qd, s0)
            vn_a = _each(lambda u_, r: u_[:c] - r[:c], u, r0)
            s1 = _each(lambda s, gl, k_, vn: s * gl + _bdot(k_[:c], vn, TN), s0, cm["glast_a"], kd, vn_a)
            r1 = _each(lambda w_, q_, s: _bdot(cat0(w_[c:], q_[c:]), s), w, qd, s1)
            vn_b = _each(lambda u_, r: u_[c:] - r[:c], u, r1)
            s2 = _each(lambda s, gl, k_, vn: s * gl + _bdot(k_[c:], vn, TN), s1, cm["glast_b"], kd, vn_b)
            o = _each(lambda ra, rb, at, va, vb_: cat0(ra[c:], rb[c:]) + _bdot(at, cat0(va, vb_)),
                      r0, r1, cm["attn"], vn_a, vn_b)
            for hh, sl in enumerate(heads):
                st_ref[hh, 2 * pi] = s0[hh]
                st_ref[hh, 2 * pi + 1] = s1[hh]
                s_sc[hh] = s2[hh]
                o_ref[rows, sl] = o[hh]
            return 0

        lax.fori_loop(0, npair, pair, 0)

    blk, row, st = _gdn_specs(t, ts, lambda s: s)
    return pl.pallas_call(
        body, name=name, grid=(GDN_HEADS // GDN_HP, t // ts), in_specs=[blk, blk, blk, blk, row, blk],
        out_specs=[blk, st],
        out_shape=[jax.ShapeDtypeStruct((t, 1024), F32), jax.ShapeDtypeStruct((GDN_HEADS, t // CHUNK, LANES, LANES), F32)],
        scratch_shapes=[pltpu.VMEM((GDN_HP, LANES, LANES), F32)],
        compiler_params=_params(("parallel", "arbitrary")),
    )(q, k, v, gcb, gct, bb)


def _gdn_bwd(q, k, v, gcb, gct, bb, do, states, *, name):
    t = q.shape[0]
    ts = _tile(t, GDN_TILE)
    npair = ts // PAIR
    ns = t // ts
    c = CHUNK

    def body(q_ref, k_ref, v_ref, g_ref, gt_ref, b_ref, do_ref, st_ref, dq_ref, dk_ref, dv_ref, dg_ref, db_ref, ds_sc):
        @pl.when(pl.program_id(1) == 0)
        def _():
            ds_sc[...] = jnp.zeros_like(ds_sc)

        rowsum = lambda x: jnp.sum(x, axis=-1, keepdims=True)
        total = lambda x: jnp.sum(rowsum(x), axis=0, keepdims=True)
        cat0 = lambda *xs: jnp.concatenate(xs, axis=0)
        cat1 = lambda *xs: jnp.concatenate(xs, axis=1)

        def pair(step, _):
            pi = npair - 1 - step
            rows = pl.ds(pl.multiple_of(pi * PAIR, PAIR), PAIR)
            heads = [slice(hh * LANES, (hh + 1) * LANES) for hh in range(GDN_HP)]
            hs = range(GDN_HP)
            qv, kv, vv = ([r[rows, sl] for sl in heads] for r in (q_ref, k_ref, v_ref))
            beta = [b_ref[rows, sl] for sl in heads]
            dov = [do_ref[rows, sl] for sl in heads]
            s0 = [st_ref[hh, 2 * pi] for hh in hs]
            s1 = [st_ref[hh, 2 * pi + 1] for hh in hs]
            ds2 = [ds_sc[hh] for hh in hs]
            cm = _pair_common(qv, kv, vv, [g_ref[rows, sl] for sl in heads], [gt_ref[hh, :, rows] for hh in hs], beta)
            u, w, qd, kd, attn = cm["u"], cm["w"], cm["qd"], cm["kd"], cm["attn"]
            tmat, gamma, eg = cm["tm"], cm["gamma"], cm["eg"]
            incl, strict = cm["incl"], cm["strict"]
            vn_a = _each(lambda u_, w_, s: u_[:c] - _bdot(w_[:c], s), u, w, s0)
            vn_b = _each(lambda u_, w_, s: u_[c:] - _bdot(w_[c:], s), u, w, s1)
            vn = _each(cat0, vn_a, vn_b)
            dvn_att = _each(lambda a, d: _bdot(a, d, TN), attn, dov)
            dattn = _each(lambda d, v_: jnp.where(incl, _bdot(d, v_, NT), 0.0), dov, vn)
            dvn_b = _each(lambda x, k_, d: x[c:] + _bdot(k_[c:], d), dvn_att, kd, ds2)
            rb = _each(lambda d, x, s: _bdot(cat0(d[c:], x), s, NT), dov, dvn_b, s1)
            dkd_b = _each(lambda v_, d: _bdot(v_, d, NT), vn_b, ds2)
            dgl_b = _each(lambda d, s: total(d * s), ds2, s1)
            ds1 = _each(lambda d, gl, q_, w_, o_, x: d * gl + _bdot(cat0(q_[c:], w_[c:]), cat0(o_[c:], -x), TN),
                        ds2, cm["glast_b"], qd, w, dov, dvn_b)
            dvn_a = _each(lambda x, k_, d: x[:c] + _bdot(k_[:c], d), dvn_att, kd, ds1)
            ra = _each(lambda d, x, s: _bdot(cat0(d[:c], x), s, NT), dov, dvn_a, s0)
            dkd_a = _each(lambda v_, d: _bdot(v_, d, NT), vn_a, ds1)
            dgl_a = _each(lambda d, s: total(d * s), ds1, s0)
            ds0 = _each(lambda d, gl, q_, w_, o_, x: d * gl + _bdot(cat0(q_[:c], w_[:c]), cat0(o_[:c], -x), TN),
                        ds1, cm["glast_a"], qd, w, dov, dvn_a)
            dvn = _each(cat0, dvn_a, dvn_b)
            dqd = _each(lambda a, b: cat0(a[:c], b[:c]), ra, rb)
            dw = _each(lambda a, b: -cat0(a[c:], b[c:]), ra, rb)
            dkd = _each(cat0, dkd_a, dkd_b)
            dvw = _each(cat1, dvn, dw)
            dvbk = _each(lambda t_, x: _bdot(t_, x, TN), tmat, dvw)
            dvb = _each(lambda x: x[:, :LANES], dvbk)
            dkbe = _each(lambda x: x[:, LANES:], dvbk)
            dt_ = _each(lambda x, a, b: _bdot(x, cat1(a, b), NT), dvw, cm["vb"], cm["kbe"])
            da1 = _each(lambda t_, x: _bdot(t_, x, TN), tmat, dt_)
            dm = _each(lambda x, t_: jnp.where(strict, -_bdot(x, t_, NT), 0.0), da1, tmat)
            dkk = _each(jnp.multiply, dm, gamma)
            dqk = _each(jnp.multiply, dattn, gamma)
            z = _each(lambda a, b, c_, d: a * b + c_ * d, dm, cm["m"], dattn, attn)
            dkb = _each(lambda x, k_, y, e: _bdot(x, k_) + y * e, dkk, kv, dkbe, eg)
            dk = _each(lambda a, b, kb_, q_, x, e, y, be: _bdot(cat0(a, b), cat0(kb_, q_), TN) + x * e + y * be,
                       dkk, dqk, cm["kb"], qv, dkd, cm["ek"], dkb, beta)
            dq = _each(lambda x, k_, y, e: _bdot(x, k_) + y * e, dqk, kv, dqd, eg)

            def colsum_of(z_):
                zh = z_.astype(BF16)
                zl = (z_ - zh.astype(F32)).astype(BF16)
                return _dot(cat0(zh, zl), jnp.ones((2 * PAIR, LANES), BF16), TN)

            colsum = _each(colsum_of, z)
            ri = lax.broadcasted_iota(jnp.int32, (PAIR, LANES), 0)
            for hh, sl in enumerate(heads):
                dkd_kd = dkd[hh] * kd[hh]
                dgc = (rowsum(z[hh]) - colsum[hh] + rowsum(dqd[hh] * qd[hh]) - rowsum(dkd_kd)
                       + rowsum(dkbe[hh] * cm["kbe"][hh]))
                last_a = total(dkd_kd[:c]) + dgl_a[hh] * cm["glast_a"][hh]
                last_b = total(dkd_kd[c:]) + dgl_b[hh] * cm["glast_b"][hh]
                dgc = dgc + jnp.where(ri == c - 1, last_a, 0.0) + jnp.where(ri == PAIR - 1, last_b, 0.0)
                ds_sc[hh] = ds0[hh]
                dq_ref[rows, sl] = dq[hh]
                dk_ref[rows, sl] = dk[hh]
                dv_ref[rows, sl] = dvb[hh] * beta[hh]
                db_ref[rows, sl] = jnp.broadcast_to(rowsum(dkb[hh] * kv[hh]) + rowsum(dvb[hh] * vv[hh]), (PAIR, LANES))
                dg_ref[rows, sl] = dgc
            return 0

        lax.fori_loop(0, npair, pair, 0)

    blk, row, st = _gdn_specs(t, ts, lambda s: ns - 1 - s)
    out = jax.ShapeDtypeStruct((t, 1024), F32)
    return pl.pallas_call(
        body, name=name, grid=(GDN_HEADS // GDN_HP, ns), in_specs=[blk, blk, blk, blk, row, blk, blk, st],
        out_specs=[blk] * 5, out_shape=[out] * 5, scratch_shapes=[pltpu.VMEM((GDN_HP, LANES, LANES), F32)],
        compiler_params=_params(("parallel", "arbitrary")),
    )(q, k, v, gcb, gct, bb, do, states)


def _gate_out_proj_loss(o, proj_c, o_norm, w, hres, g, target, *, name):
    t, d = hres.shape
    tm = _tile(t, 2 * ROW_TILE)

    def body(o_ref, z_ref, on_ref, w_ref, h_ref, g_ref, t_ref, dh_ref, dhb_ref, dg_ref, loss_ref, dw_ref, y_ref):
        i = pl.program_id(0)
        for hd in range(GDN_HEADS):
            sl = slice(hd * LANES, (hd + 1) * LANES)
            ov = o_ref[:, sl]
            rr = lax.rsqrt(jnp.mean(ov * ov, axis=-1, keepdims=True) + RMS_EPS)
            z = z_ref[:, sl]
            y_ref[:, sl] = (ov * rr * on_ref[...] * (z * _sigmoid(z))).astype(BF16)
        x = h_ref[...] + _dot(y_ref[...], w_ref[...])
        r = lax.rsqrt(jnp.mean(x * x, axis=-1, keepdims=True) + RMS_EPS)
        xh = x * r
        err = xh * g_ref[...] - t_ref[...]
        dy = err * (1.0 / d)
        dxh = dy * g_ref[...]
        dh = r * (dxh - xh * jnp.mean(dxh * xh, axis=-1, keepdims=True))
        dh_ref[...] = dh
        dhb_ref[...] = dh.astype(BF16)

        @pl.when(i == 0)
        def _():
            dg_ref[...] = jnp.zeros_like(dg_ref)
            loss_ref[...] = jnp.zeros_like(loss_ref)
            dw_ref[...] = jnp.zeros_like(dw_ref)

        dg_ref[...] += jnp.sum(dy * xh, axis=0, keepdims=True)
        part = 0.5 * jnp.sum(jnp.mean(err * err, axis=-1, keepdims=True), axis=0, keepdims=True)
        loss_ref[...] += jnp.broadcast_to(part, loss_ref.shape)
        dw_ref[...] += _dot(y_ref[...], dhb_ref[...], TN)

    row = pl.BlockSpec((tm, d), lambda i: (i, 0))
    vec = pl.BlockSpec((1, d), lambda i: (0, 0))
    return pl.pallas_call(
        body, name=name, grid=(t // tm,),
        in_specs=[row, pl.BlockSpec((tm, 1024), lambda i: (i, 3)), pl.BlockSpec((1, LANES), lambda i: (0, 0)),
                  pl.BlockSpec(w.shape, lambda i: (0, 0)), row, vec, row],
        out_specs=[row, row, vec, pl.BlockSpec((8, LANES), lambda i: (0, 0)), pl.BlockSpec(w.shape, lambda i: (0, 0))],
        out_shape=[jax.ShapeDtypeStruct((t, d), F32), jax.ShapeDtypeStruct((t, d), BF16),
                   jax.ShapeDtypeStruct((1, d), F32), jax.ShapeDtypeStruct((8, LANES), F32),
                   jax.ShapeDtypeStruct(w.shape, F32)],
        scratch_shapes=[pltpu.VMEM((tm, d), BF16)],
        compiler_params=_params(("arbitrary",)),
    )(o, proj_c, o_norm, w, hres, g, target)


def _pad_cols(w, n):
    return jnp.pad(w, ((0, 0), (0, n - w.shape[1])))


def _layout_odd(w):
    return dict(
        winc=_pad_cols(w["w_in_c"], IN_C_PAD).astype(BF16), wout_c=w["w_out_c"].astype(BF16), conv_w=w["conv_w"],
        a_log=_pad_cols(w["a_log"], LANES), dt_bias=_pad_cols(w["dt_bias"], LANES),
        norm_c=w["norm_c"], o_norm=w["o_norm"], final_norm=w["final_norm"],
    )


def _layout_in_ab(w):
    z = lambda r, c: jnp.zeros((r, c), w["w_in_ab"].dtype)
    wi = w["w_in_ab"]
    win = jnp.concatenate([wi[:, :384], z(1024, 64), wi[:, 384:416], z(1024, 32), wi[:, 416:]], axis=1)
    pw = w["pool_w"]
    rows = []
    for g in range(4):
        rows.append(jnp.concatenate([pw[g] if j == g else jnp.zeros((128, 128), F32) for j in range(4)], axis=1))
    wpool = jnp.concatenate(rows, axis=0)
    half = MLA_ROPE // 2
    inv = 1.0 / (ROPE_THETA ** (jnp.arange(half, dtype=F32) / half))
    inv_lane = jnp.concatenate([jnp.zeros((MLA_NOPE,), F32), inv, inv, jnp.zeros((32,), F32)]).reshape(1, LANES)
    return dict(win=win.astype(BF16), wpool=wpool.astype(BF16), inv_lane=inv_lane, norm_ab=w["norm_ab"],
                q_a_norm=w["q_a_norm"], kv_a_norm=w["kv_a_norm"], pool_scale=w["pool_scale"])


def _layout_mid(w):
    wq = jnp.pad(w["w_q_b"].reshape(MLA_Q_RANK, MLA_HEADS, 96), ((0, 0), (0, 0), (0, 32))).reshape(MLA_Q_RANK, 1024)
    kv3 = w["w_kv_b"].reshape(MLA_KV_RANK, MLA_HEADS, 128)
    wk = jnp.pad(kv3[..., :MLA_NOPE], ((0, 0), (0, 0), (0, 64))).reshape(MLA_KV_RANK, 1024)
    wv = kv3[..., MLA_NOPE:].reshape(MLA_KV_RANK, 512)
    return dict(wq=wq.astype(BF16), wk=wk.astype(BF16), wv=wv.astype(BF16), wout_ab=w["w_out_ab"].astype(BF16))


def _unlayout_grads(g, names):
    out = {}
    for name in names:
        if name == "w_in_ab":
            dwin = g["win"]
            out[name] = jnp.concatenate([dwin[:384], dwin[448:480], dwin[512:]], axis=0)
        elif name == "w_q_b":
            out[name] = g["wq"].reshape(MLA_Q_RANK, MLA_HEADS, 128)[..., :96].reshape(MLA_Q_RANK, 768)
        elif name == "w_kv_b":
            out[name] = jnp.concatenate([g["wk"].reshape(MLA_KV_RANK, MLA_HEADS, 128)[..., :MLA_NOPE],
                                         g["wv"].reshape(MLA_KV_RANK, MLA_HEADS, MLA_V)], axis=-1).reshape(MLA_KV_RANK, 1024)
        elif name == "w_in_c":
            out[name] = g["winc"][:4112]
        else:
            out[name] = g[{"w_out_ab": "wout_ab", "w_out_c": "wout_c"}[name]]
    return out


def _local_step(x, pos, target, lw, more_weights=None, on_grads=None):
    mm = _matmul
    proj, hn = _rms_in_proj(x, lw["norm_ab"], lw["win"], name="rms_in_ab")
    if more_weights is not None:
        lw = {**lw, **more_weights("mid", proj)}
    q, k, v, ybraw, qn, kvn, d, cos_t, sin_t = _ab_prep(
        proj, pos, lw["inv_lane"], lw["q_a_norm"], lw["kv_a_norm"], lw["wq"], lw["wk"], lw["wv"], lw["wpool"], name="ab_prep")
    o, lse = _attn_fwd(q, k, v, name="attn_fwd")
    h1, y = _gate_out_proj(o, ybraw, proj, lw["pool_scale"], lw["wout_ab"], x, name="gate_out_ab")
    lo = lw if more_weights is None else more_weights("odd", h1)
    proj_c, hn1 = _rms_in_proj(h1, lo["norm_c"], lo["winc"], name="rms_in_c")
    q2, k2, v2, gb, bb, gt = _c_prep(proj_c, lo["conv_w"], lo["a_log"], lo["dt_bias"], name="c_prep")
    gt = gt.reshape(GDN_HEADS, 1, gt.shape[1])
    o2, states = _gdn_fwd(q2, k2, v2, gb, gt, bb, name="gdn_fwd")
    dh2, dh2b, d_final, loss, d_wout_c = _gate_out_proj_loss(
        o2, proj_c, lo["o_norm"], lo["wout_c"], h1, lo["final_norm"], target, name="gate_out_c_loss")
    g = {"final_norm": d_final, "wout_c": d_wout_c, "loss": loss}
    do2, dz2, g["o_norm"] = _o_gate_bwd(dh2b, lo["wout_c"], o2, proj_c, lo["o_norm"], name="gate_c_bwd")
    dq2, dk2, dv2, dgb, dbb = _gdn_bwd(q2, k2, v2, gb, gt, bb, do2, states, name="gdn_bwd")
    dproj_c, g["conv_w"], g["a_log"], g["dt_bias"] = _c_prep_bwd(
        proj_c, lo["conv_w"], lo["a_log"], lo["dt_bias"], dq2, dk2, dv2, dgb, dbb, dz2, name="c_prep_bwd")
    g["winc"] = mm(dproj_c, hn1, "tn", name="in_c_dw")
    notify = (lambda tag, after=None: 0.0) if on_grads is None else (lambda tag, after=None: on_grads(tag, g, after))
    dh1, g["norm_c"], dh1b, g["wout_ab"] = _matmul_rms_bwd(
        dproj_c, lo["winc"], h1, lo["norm_c"] + notify("odd"), dh2, name="in_c_dx_rms", prev_y=y)
    pool_scale = lw["pool_scale"] + notify("odd_go", dh1b) + notify("out_ab")
    do, delta, dyb, dz, g["pool_scale"] = _gate_bwd(dh1b, lw["wout_ab"], o, ybraw, proj, pool_scale, name="gate_ab_bwd")
    dq, dk, dv = _attn_bwd(q, k, v, do, lse, delta, name="attn_bwd")
    dproj, g["q_a_norm"], g["kv_a_norm"], g["wq"], g["wk"], g["wv"], g["wpool"], g["win"] = _ab_prep_bwd(
        proj, lw["q_a_norm"], lw["kv_a_norm"], dq, dk, dv, cos_t, sin_t, dyb, dz, qn, kvn, d, hn,
        lw["wq"], lw["wk"], lw["wv"], lw["wpool"], name="ab_prep_bwd")
    norm_ab = lw["norm_ab"] + notify("in_ab")
    dx, g["norm_ab"] = _matmul_rms_bwd(dproj, lw["win"], x, norm_ab, dh1, name="in_ab_dx_rms")
    return loss, dx, g


_HBM = pl.BlockSpec(memory_space=pltpu.HBM)


def _place():
    return lax.axis_index("x"), lax.axis_index("y"), lax.axis_index("c")


def _flip(v, f):
    return 1 - v if f else v


_CHIP_FLIPS = ((1, 0), (0, 1), (1, 1))
_DEV_FLIPS = tuple((fx, fy, fc) for fx in (0, 1) for fy in (0, 1) for fc in (0, 1) if fx or fy or fc)


def _rcopy(src, dst, send_sems, recv_sems, k, to):
    return pltpu.make_async_remote_copy(src_ref=src, dst_ref=dst, send_sem=send_sems.at[k], recv_sem=recv_sems.at[k],
                                        device_id=to, device_id_type=MESH)


def _my_half(ref, c, axis):
    rh = ref.shape[axis] // 2
    idx = [slice(None)] * len(ref.shape)
    idx[axis] = pl.ds(c * rh, rh)
    return ref.at[tuple(idx)]


def _gather_weights(bigs, smalls):
    nb, ns = len(bigs), len(smalls)

    def body(*refs):
        ins, outs = refs[:nb + ns], refs[nb + ns:2 * (nb + ns)]
        send_sems, recv_sems, local_sems = refs[2 * (nb + ns):]
        x, y, c = _place()
        j0 = 2 * x + y
        sib = (x, y, 1 - c)
        chips = [(_flip(x, fx), _flip(y, fy)) for fx, fy in _CHIP_FLIPS]
        local = [pltpu.make_async_copy(i_ref, o_ref.at[j0], local_sems.at[a])
                 for a, (i_ref, o_ref) in enumerate(zip(ins, outs))]
        for cp in local:
            cp.start()
        sends = []
        for k, (px, py) in enumerate(chips):
            for a in range(nb):
                sends.append(_rcopy(_my_half(ins[a], c, 0), _my_half(outs[a].at[j0], c, 0), send_sems, recv_sems,
                                    6 * a + k, (px, py, c)))
            for s in range(ns):
                sends.append(_rcopy(ins[nb + s], outs[nb + s].at[j0], send_sems, recv_sems, 6 * nb + 3 * s + k, (px, py, c)))
        for cp in sends:
            cp.start()
        for k, (px, py) in enumerate(chips):
            jk = 2 * px + py
            for a in range(nb):
                landed = _my_half(outs[a].at[jk], c, 0)
                _rcopy(landed, landed, send_sems, recv_sems, 6 * a + k, (px, py, c)).wait_recv()
                fwd = _rcopy(landed, landed, send_sems, recv_sems, 6 * a + 3 + k, sib)
                fwd.start()
                sends.append(fwd)
        for k, (px, py) in enumerate(chips):
            jk = 2 * px + py
            for a in range(nb):
                other = _my_half(outs[a].at[jk], 1 - c, 0)
                _rcopy(other, other, send_sems, recv_sems, 6 * a + 3 + k, sib).wait_recv()
            for s in range(ns):
                _rcopy(ins[nb + s], outs[nb + s].at[jk], send_sems, recv_sems, 6 * nb + 3 * s + k, (px, py, c)).wait_recv()
        for cp in sends:
            cp.wait_send()
        for cp in local:
            cp.wait()

    arrays = list(bigs) + list(smalls)
    n_sem = 6 * nb + 3 * ns
    return pl.pallas_call(
        body, name="gather_weights", in_specs=[_HBM] * len(arrays), out_specs=[_HBM] * len(arrays),
        out_shape=[jax.ShapeDtypeStruct((4,) + a.shape, a.dtype) for a in arrays],
        scratch_shapes=[pltpu.SemaphoreType.DMA((n_sem,)), pltpu.SemaphoreType.DMA((n_sem,)),
                        pltpu.SemaphoreType.DMA((len(arrays),))],
    )(*arrays)


def _core_swap_partial(gs, by_cols, *, name):
    n = len(gs)

    def body(*refs):
        ins, outs = refs[:n], refs[n:2 * n]
        send_sems, recv_sems = refs[2 * n:]
        x, y, c = _place()
        copies = [_rcopy(_my_half(i_ref, 1 - c, 2 if by_cols[a] else 1), o_ref, send_sems, recv_sems, a, (x, y, 1 - c))
                  for a, (i_ref, o_ref) in enumerate(zip(ins, outs))]
        for cp in copies:
            cp.start()
        for cp in copies:
            cp.wait()

    halved = lambda g, cols: (4, g.shape[1], g.shape[2] // 2) if cols else (4, g.shape[1] // 2, g.shape[2])
    return pl.pallas_call(
        body, name=name, in_specs=[_HBM] * n, out_specs=[_HBM] * n,
        out_shape=[jax.ShapeDtypeStruct(halved(g, cols), g.dtype) for g, cols in zip(gs, by_cols)],
        scratch_shapes=[pltpu.SemaphoreType.DMA((n,)), pltpu.SemaphoreType.DMA((n,))],
    )(*gs)


def _core_swap_partial_start(gs, by_cols, *, name):
    n = len(gs)
    halved = lambda g, cols: (4, g.shape[1], g.shape[2] // 2) if cols else (4, g.shape[1] // 2, g.shape[2])
    lands = [lax.empty(halved(g, cols), g.dtype) for g, cols in zip(gs, by_cols)]

    def body(*refs):
        ins, land_refs, send_sems, recv_sems, token = refs[:n], refs[n:2 * n], refs[2 * n], refs[2 * n + 1], refs[-1]
        x, y, c = _place()
        for a in range(n):
            _rcopy(_my_half(ins[a], 1 - c, 2 if by_cols[a] else 1), land_refs[a], send_sems, recv_sems, a,
                   (x, y, 1 - c)).start()
        token[...] = jnp.zeros_like(token)

    held = [pltpu.with_memory_space_constraint(a, pltpu.HBM) for a in list(gs) + lands]
    return pl.pallas_call(
        body, name=name, in_specs=[_HBM] * (2 * n),
        out_specs=(_SEM, _SEM, *[_HBM] * (2 * n), pl.BlockSpec(memory_space=pltpu.VMEM)),
        out_shape=(pltpu.SemaphoreType.DMA((n,)), pltpu.SemaphoreType.DMA((n,)),
                   *[pltpu.HBM(a.shape, a.dtype) for a in held], jax.ShapeDtypeStruct((8, LANES), F32)),
        input_output_aliases={i: 2 + i for i in range(2 * n)},
        compiler_params=pltpu.CompilerParams(has_side_effects=_DATAFLOW),
    )(*held)


def _core_swap_partial_wait(started, after, by_cols, *, name):
    send_sems, recv_sems, held = started[0], started[1], started[2:-1]
    n = len(held) // 2

    def body(*refs):
        ins, land_refs, s_sems, r_sems = refs[:n], refs[n:2 * n], refs[2 * n], refs[2 * n + 1]
        x, y, c = _place()
        for a in range(n):
            cp = _rcopy(_my_half(ins[a], 1 - c, 2 if by_cols[a] else 1), land_refs[a], s_sems, r_sems, a, (x, y, 1 - c))
            cp.wait_send()
            cp.wait_recv()

    out = pl.pallas_call(
        body, name=name, in_specs=[_HBM] * (2 * n) + [_SEM, _SEM, _ANY], out_specs=[_HBM] * (2 * n),
        out_shape=[pltpu.HBM(a.shape, a.dtype) for a in held],
        input_output_aliases={i: i for i in range(2 * n)},
        compiler_params=pltpu.CompilerParams(has_side_effects=_DATAFLOW),
    )(*held, send_sems, recv_sems, after)
    return out[:n], out[n:]


def _core_swap_sum(fs, by_cols):
    n = len(fs)

    def body(*refs):
        ins, outs = refs[:n], refs[n:2 * n]
        send_sems, recv_sems = refs[2 * n:]
        x, y, c = _place()
        axes = [1 if cols else 0 for cols in by_cols]
        copies = [_rcopy(_my_half(i_ref, c, ax), _my_half(o_ref, c, ax), send_sems, recv_sems, a, (x, y, 1 - c))
                  for a, (i_ref, o_ref, ax) in enumerate(zip(ins, outs, axes))]
        for cp in copies:
            cp.start()
        for a, cp in enumerate(copies):
            cp.wait_send()
            theirs = _my_half(outs[a], 1 - c, axes[a])
            _rcopy(theirs, theirs, send_sems, recv_sems, a, (x, y, 1 - c)).wait_recv()

    return pl.pallas_call(
        body, name="core_swap_sum", in_specs=[_HBM] * n, out_specs=[_HBM] * n,
        out_shape=[jax.ShapeDtypeStruct(f.shape, f.dtype) for f in fs],
        input_output_aliases={a: a for a in range(n)},
        scratch_shapes=[pltpu.SemaphoreType.DMA((n,)), pltpu.SemaphoreType.DMA((n,))],
    )(*fs)


_SEM = pl.BlockSpec(memory_space=pltpu.SEMAPHORE)
_ANY = pl.BlockSpec(memory_space=pl.ANY)
_DATAFLOW = pltpu.SideEffectType.DATAFLOW_SIDE_EFFECTING


def _peer_flips(to_all):
    return _DEV_FLIPS if to_all else tuple((fx, fy, 0) for fx, fy in _CHIP_FLIPS)


def _to_chips_copies(srcs, lands, send_sems, recv_sems, per_chip_slot, to_all=False):
    x, y, c = _place()
    flips = _peer_flips(to_all)
    index = (lambda px, py, pc: 4 * px + 2 * py + pc) if to_all else (lambda px, py, pc: 2 * px + py)
    me = index(x, y, c)
    out = []
    for k, (fx, fy, fc) in enumerate(flips):
        peer = (_flip(x, fx), _flip(y, fy), _flip(c, fc))
        theirs = index(*peer)
        for a, (src, land) in enumerate(zip(srcs, lands)):
            piece = src.at[theirs] if per_chip_slot else src
            out.append((_rcopy(piece, land.at[me], send_sems, recv_sems, len(flips) * a + k, peer),
                        _rcopy(piece, land.at[theirs], send_sems, recv_sems, len(flips) * a + k, peer)))
    return out


def _to_chips_start(arrays, *, per_chip_slot, name, after=None, to_all=False):
    n = len(arrays)
    peers = len(_peer_flips(to_all))
    lands = [lax.empty((peers + 1,) + (a.shape[1:] if per_chip_slot else a.shape), a.dtype) for a in arrays]
    extra = [] if after is None else [after]

    def body(*refs):
        srcs, land_refs, token = refs[:n], refs[n:2 * n], refs[-1]
        send_sems, recv_sems = refs[2 * n + len(extra)], refs[2 * n + len(extra) + 1]
        for send, _ in _to_chips_copies(srcs, land_refs, send_sems, recv_sems, per_chip_slot, to_all):
            send.start()
        token[...] = jnp.zeros_like(token)

    held = [pltpu.with_memory_space_constraint(a, pltpu.HBM) for a in list(arrays) + lands]
    return pl.pallas_call(
        body, name=name, in_specs=[_HBM] * (2 * n) + [_ANY] * len(extra),
        out_specs=(_SEM, _SEM, *[_HBM] * (2 * n), pl.BlockSpec(memory_space=pltpu.VMEM)),
        out_shape=(pltpu.SemaphoreType.DMA((peers * n,)), pltpu.SemaphoreType.DMA((peers * n,)),
                   *[pltpu.HBM(a.shape, a.dtype) for a in held], jax.ShapeDtypeStruct((8, LANES), F32)),
        input_output_aliases={i: 2 + i for i in range(2 * n)},
        compiler_params=pltpu.CompilerParams(has_side_effects=_DATAFLOW),
    )(*held, *extra)


def _to_chips_wait(started, after, *, per_chip_slot, name, to_all=False):
    send_sems, recv_sems, held = started[0], started[1], started[2:-1]
    n = len(held) // 2

    def body(*refs):
        srcs, land_refs, s_sems, r_sems = refs[:n], refs[n:2 * n], refs[2 * n], refs[2 * n + 1]
        for send, arrival in _to_chips_copies(srcs, land_refs, s_sems, r_sems, per_chip_slot, to_all):
            send.wait_send()
            arrival.wait_recv()

    out = pl.pallas_call(
        body, name=name, in_specs=[_HBM] * (2 * n) + [_SEM, _SEM, _ANY], out_specs=[_HBM] * (2 * n),
        out_shape=[pltpu.HBM(a.shape, a.dtype) for a in held],
        input_output_aliases={i: i for i in range(2 * n)},
        compiler_params=pltpu.CompilerParams(has_side_effects=_DATAFLOW),
    )(*held, send_sems, recv_sems, after)
    return out[n:]


def _chip_exchange(ps, small):
    n = len(ps)
    rs = small.shape[0]

    def body(*refs):
        p_refs, s_ref = refs[:n], refs[n]
        l_refs, ls_ref = refs[n + 1:2 * n + 1], refs[2 * n + 1]
        send_sems, recv_sems, local_sems = refs[2 * n + 2:]
        x, y, c = _place()
        j0 = 2 * x + y
        d0 = 2 * j0 + c
        local = [pltpu.make_async_copy(p.at[j0], l.at[j0], local_sems.at[a]) for a, (p, l) in enumerate(zip(p_refs, l_refs))]
        local.append(pltpu.make_async_copy(s_ref, ls_ref.at[d0], local_sems.at[n]))
        for cp in local:
            cp.start()
        sends = []
        for k, (fx, fy) in enumerate(_CHIP_FLIPS):
            px, py = _flip(x, fx), _flip(y, fy)
            for a in range(n):
                sends.append(_rcopy(p_refs[a].at[2 * px + py], l_refs[a].at[j0], send_sems, recv_sems, 3 * a + k, (px, py, c)))
        for k, (fx, fy, fc) in enumerate(_DEV_FLIPS):
            peer = (_flip(x, fx), _flip(y, fy), _flip(c, fc))
            sends.append(_rcopy(s_ref, ls_ref.at[d0], send_sems, recv_sems, 3 * n + k, peer))
        for cp in sends:
            cp.start()
        for k, (fx, fy) in enumerate(_CHIP_FLIPS):
            px, py = _flip(x, fx), _flip(y, fy)
            for a in range(n):
                _rcopy(p_refs[a].at[j0], l_refs[a].at[2 * px + py], send_sems, recv_sems, 3 * a + k, (px, py, c)).wait_recv()
        for k, (fx, fy, fc) in enumerate(_DEV_FLIPS):
            px, py, pc = _flip(x, fx), _flip(y, fy), _flip(c, fc)
            _rcopy(s_ref, ls_ref.at[4 * px + 2 * py + pc], send_sems, recv_sems, 3 * n + k, (px, py, pc)).wait_recv()
        for cp in sends:
            cp.wait_send()
        for cp in local:
            cp.wait()

    n_sem = 3 * n + 7
    return pl.pallas_call(
        body, name="chip_exchange", in_specs=[_HBM] * (n + 1), out_specs=[_HBM] * (n + 1),
        out_shape=[jax.ShapeDtypeStruct(p.shape, F32) for p in ps] + [jax.ShapeDtypeStruct((8, rs, LANES), F32)],
        scratch_shapes=[pltpu.SemaphoreType.DMA((n_sem,)), pltpu.SemaphoreType.DMA((n_sem,)),
                        pltpu.SemaphoreType.DMA((n + 1,))],
    )(*ps, small)


def _half_blocks(rows, cols, by_cols):
    if by_cols:
        tc = _tile(cols // 2, 256)
        nb = cols // 2 // tc
        return rows, tc, nb, (lambda i, c: (0, c * nb + i))
    tr = _tile(rows // 2, 256)
    nb = rows // 2 // tr
    return tr, cols, nb, (lambda i, c: (c * nb + i, 0))


def _core_sum(g, part, core, *, name, by_cols):
    _, rows, cols = g.shape
    br, bc, nb, whole = _half_blocks(rows, cols, by_cols)
    mine = (lambda i: (0, i)) if by_cols else (lambda i: (i, 0))

    def body(c_ref, g_ref, p_ref, o_ref):
        o_ref[...] = g_ref[...] + p_ref[...]

    grid_spec = pltpu.PrefetchScalarGridSpec(
        num_scalar_prefetch=1, grid=(4, nb),
        in_specs=[pl.BlockSpec((1, br, bc), lambda j, i, c: (j,) + whole(i, c[0])),
                  pl.BlockSpec((1, br, bc), lambda j, i, c: (j,) + mine(i))],
        out_specs=pl.BlockSpec((1, br, bc), lambda j, i, c: (j,) + mine(i)),
    )
    return pl.pallas_call(
        body, name=name, grid_spec=grid_spec, out_shape=jax.ShapeDtypeStruct(part.shape, F32),
        compiler_params=_params(("parallel", "parallel")),
    )(core, g, part)


def _chip_sum(landed, core, *, name, by_cols):
    _, hr, hc = landed.shape
    rows, cols = (hr, 2 * hc) if by_cols else (2 * hr, hc)
    br, bc, nb, whole = _half_blocks(rows, cols, by_cols)
    mine = (lambda i: (0, i)) if by_cols else (lambda i: (i, 0))

    def body(c_ref, l_ref, o_ref):
        o_ref[...] = ((l_ref[0] + l_ref[1]) + l_ref[2]) + l_ref[3]

    grid_spec = pltpu.PrefetchScalarGridSpec(
        num_scalar_prefetch=1, grid=(nb,),
        in_specs=[pl.BlockSpec((4, br, bc), lambda i, c: (0,) + mine(i))],
        out_specs=pl.BlockSpec((br, bc), lambda i, c: whole(i, c[0])),
    )
    return pl.pallas_call(
        body, name=name, grid_spec=grid_spec, out_shape=jax.ShapeDtypeStruct((rows, cols), F32),
        compiler_params=_params(("parallel",)),
    )(core, landed)


_ROW_POOL_W, _ROW_NORM_AB, _ROW_FINAL, _ROW_POOL_SCALE, _ROW_Q_NORM = 0, 512, 520, 528, 532
_ROW_KV_NORM, _ROW_O_NORM, _ROW_A_LOG, _ROW_DT_BIAS, _ROW_LOSS = 534, 535, 536, 537, 538
_ROW_CONV, _ROW_NORM_C, _SMALL_ROWS = 544, 640, 672
_CONV_ROWS = CONV_WIDTH * 6


def _put_rows(dst_ref, row0, src, width):
    for r in range(width // LANES):
        dst_ref[row0 + r:row0 + r + 1, :] = src[:, r * LANES:(r + 1) * LANES]


def _pack_small(g, loss_tile):
    names = ("wpool", "norm_ab", "final_norm", "pool_scale", "q_a_norm", "kv_a_norm", "o_norm", "a_log", "dt_bias",
             "conv_w", "norm_c")

    def body(wpool, norm_ab, final_norm, pool_scale, q_norm, kv_norm, o_norm, a_log, dt_bias, conv_w, norm_c, loss, o_ref):
        o_ref[...] = jnp.zeros_like(o_ref)
        for gi in range(4):
            o_ref[_ROW_POOL_W + gi * 128:_ROW_POOL_W + (gi + 1) * 128, :] = wpool[gi * 128:(gi + 1) * 128, gi * 128:(gi + 1) * 128]
        _put_rows(o_ref, _ROW_NORM_AB, norm_ab[...], 1024)
        _put_rows(o_ref, _ROW_FINAL, final_norm[...], 1024)
        _put_rows(o_ref, _ROW_POOL_SCALE, pool_scale[...], 512)
        _put_rows(o_ref, _ROW_Q_NORM, q_norm[...], 256)
        for row, ref in ((_ROW_KV_NORM, kv_norm), (_ROW_O_NORM, o_norm), (_ROW_A_LOG, a_log), (_ROW_DT_BIAS, dt_bias)):
            o_ref[row:row + 1, :] = ref[...]
        o_ref[_ROW_LOSS:_ROW_LOSS + 1, :] = loss[0:1, :]
        for j in range(4):
            for r in range(CONV_WIDTH):
                _put_rows(o_ref, _ROW_CONV + j * _CONV_ROWS + r * 6, conv_w[r:r + 1, j * 768:(j + 1) * 768], 768)
            _put_rows(o_ref, _ROW_NORM_C + j * 8, norm_c[:, j * 256:(j + 1) * 256], 256)

    vmem = pl.BlockSpec(memory_space=pltpu.VMEM)
    return pl.pallas_call(
        body, name="pack_small", in_specs=[vmem] * 12, out_specs=vmem,
        out_shape=jax.ShapeDtypeStruct((_SMALL_ROWS, LANES), F32),
    )(*[g[n] for n in names], loss_tile)


_SMALL_NAMES = ("pool_w", "norm_ab", "final_norm", "pool_scale", "q_a_norm", "kv_a_norm", "o_norm", "a_log", "dt_bias",
                "conv_w", "norm_c")


def _take_rows(src, row0, width):
    return jnp.concatenate([src[row0 + r:row0 + r + 1, :] for r in range(width // LANES)], axis=1)


def _small_update(small_all, late_all, ws, ms, vs):
    n = len(_SMALL_NAMES)

    def body(*refs):
        a_ref, late_ref = refs[0], refs[1]
        refs = refs[1:]
        w_refs, m_refs, v_refs = refs[1:1 + n], refs[1 + n:1 + 2 * n], refs[1 + 2 * n:1 + 3 * n]
        outs = refs[1 + 3 * n:1 + 7 * n]
        loss_ref, tot = refs[1 + 7 * n], refs[2 + 7 * n]
        acc, late = a_ref[0], late_ref[0]
        for d in range(1, 8):
            acc = acc + a_ref[d]
            late = late + late_ref[d]
        tot[...] = acc
        x, y, _ = _place()
        j0 = 2 * x + y
        conv = tot[pl.ds(pl.multiple_of(_ROW_CONV + j0 * _CONV_ROWS, 8), _CONV_ROWS), :]
        norm_c = tot[pl.ds(pl.multiple_of(_ROW_NORM_C + j0 * 8, 8), 8), :]
        whole = tot[_ROW_NORM_AB:_ROW_CONV, :]
        at = lambda row: row - _ROW_NORM_AB
        grads = {
            "norm_ab": _take_rows(late, 0, 1024), "final_norm": _take_rows(whole, at(_ROW_FINAL), 1024),
            "pool_scale": _take_rows(whole, at(_ROW_POOL_SCALE), 512), "q_a_norm": _take_rows(whole, at(_ROW_Q_NORM), 256),
            "kv_a_norm": whole[at(_ROW_KV_NORM):at(_ROW_KV_NORM) + 1, :], "o_norm": whole[at(_ROW_O_NORM):at(_ROW_O_NORM) + 1, :],
            "a_log": tot[_ROW_A_LOG:_ROW_A_LOG + 1, 0:GDN_HEADS],
            "dt_bias": tot[_ROW_DT_BIAS:_ROW_DT_BIAS + 1, 0:GDN_HEADS],
            "norm_c": _take_rows(norm_c, 0, 256),
        }
        loss_ref[...] = whole[at(_ROW_LOSS):at(_ROW_LOSS) + 1, :]
        for i, name in enumerate(_SMALL_NAMES):
            g_out = outs[4 * i]
            if name == "pool_w":
                for gi in range(4):
                    g_out[gi] = tot[_ROW_POOL_W + gi * 128:_ROW_POOL_W + (gi + 1) * 128, :]
            elif name == "conv_w":
                for r in range(CONV_WIDTH):
                    g_out[r:r + 1, :] = _take_rows(conv, r * 6, 768)
            else:
                g_out[...] = grads[name]
            _adam_update(g_out, w_refs[i], m_refs[i], v_refs[i], *outs[4 * i + 1:4 * i + 4])

    vmem = pl.BlockSpec(memory_space=pltpu.VMEM)
    out_shape = [jax.ShapeDtypeStruct(w.shape, F32) for w in ws for _ in range(4)] + [jax.ShapeDtypeStruct((1, LANES), F32)]
    return pl.pallas_call(
        body, name="small_update", in_specs=[vmem] * (2 + 3 * n), out_specs=[vmem] * (4 * n + 1), out_shape=out_shape,
        scratch_shapes=[pltpu.VMEM((_SMALL_ROWS, LANES), F32)],
        compiler_params=pltpu.CompilerParams(vmem_limit_bytes=VMEM_LIMIT),
    )(small_all, late_all, *ws, *ms, *vs)


def _adam_update(g_ref, w_ref, m_ref, v_ref, d_ref, mo_ref, vo_ref):
    gv = g_ref[...]
    mn = ADAM_B1 * m_ref[...] + (1.0 - ADAM_B1) * gv
    vn = ADAM_B2 * v_ref[...] + (1.0 - ADAM_B2) * (gv * gv)
    mo_ref[...] = mn
    vo_ref[...] = vn
    c1 = 1.0 - ADAM_B1 ** ADAM_STEP
    c2 = 1.0 - ADAM_B2 ** ADAM_STEP
    d_ref[...] = -ADAM_LR * ((mn / c1) / (jnp.sqrt(vn / c2) + ADAM_EPS) + ADAM_WD * w_ref[...])


def _adamw_rows(g, w, m, v, *, name):
    rows, cols = g.shape
    if rows % LANES == 0:
        tr = _tile(rows, 512)
        blk, steps = pl.BlockSpec((tr, cols), lambda i: (i, 0)), rows // tr
    else:
        tc = _tile(cols, 256)
        blk, steps = pl.BlockSpec((rows, tc), lambda i: (0, i)), cols // tc

    def body(*refs):
        _adam_update(*refs)

    out = jax.ShapeDtypeStruct((rows, cols), F32)
    return pl.pallas_call(
        body, name=name, grid=(steps,), in_specs=[blk] * 4, out_specs=[blk] * 3, out_shape=[out] * 3,
        compiler_params=_params(("parallel",)),
    )(g, w, m, v)


_ADAM_ROWWISE = ("w_in_ab", "w_q_b", "w_kv_b", "w_out_ab", "w_in_c", "w_out_c")


_SHARD_AXIS = {"w_in_ab": 1, "w_q_b": 1, "w_kv_b": 1, "w_out_ab": 0, "w_in_c": 1, "w_out_c": 0, "conv_w": 1, "norm_c": 1}
_ALL_NAMES = ("norm_ab", "w_in_ab", "q_a_norm", "w_q_b", "kv_a_norm", "w_kv_b", "pool_w", "pool_scale", "w_out_ab",
              "norm_c", "w_in_c", "conv_w", "a_log", "dt_bias", "o_norm", "w_out_c", "final_norm")


def _join_shards(a, axis):
    _, r, c = a.shape
    return a.reshape(4 * r, c) if axis == 0 else jnp.transpose(a, (1, 0, 2)).reshape(r, 4 * c)


def _split_shards(a, axis):
    r, c = a.shape
    return a.reshape(4, r // 4, c) if axis == 0 else jnp.transpose(a.reshape(r, 4, c // 4), (1, 0, 2))


def kernel(x, positions, norm_ab, w_in_ab, q_a_norm, w_q_b, kv_a_norm, w_kv_b, pool_w, pool_scale, w_out_ab, norm_c, w_in_c, conv_w, a_log, dt_bias, o_norm, w_out_c, final_norm, loss_target, m_norm_ab, m_w_in_ab, m_q_a_norm, m_w_q_b, m_kv_a_norm, m_w_kv_b, m_pool_w, m_pool_scale, m_w_out_ab, m_norm_c, m_w_in_c, m_conv_w, m_a_log, m_dt_bias, m_o_norm, m_w_out_c, m_final_norm, v_norm_ab, v_w_in_ab, v_q_a_norm, v_w_q_b, v_kv_a_norm, v_w_kv_b, v_pool_w, v_pool_scale, v_w_out_ab, v_norm_c, v_w_in_c, v_conv_w, v_a_log, v_dt_bias, v_o_norm, v_w_out_c, v_final_norm):
    given = dict(locals())
    c = lax.axis_index("c")
    t = x.shape[1]

    def shard_of(prefix, name):
        a = given[prefix + name]
        return a.reshape(a.shape[1:]) if a.ndim > 2 else a.reshape(1, -1)

    big, big_even, big_odd, small_sharded = _ADAM_ROWWISE, _ADAM_ROWWISE[:4], _ADAM_ROWWISE[4:], ("conv_w", "norm_c")
    chip = 2 * lax.axis_index("x") + lax.axis_index("y")
    core = c.astype(jnp.int32).reshape(1)
    later = {"mid": big_even[1:], "odd": big_odd + small_sharded}
    travelling = {}

    def send(tag, after=None):
        shards = [shard_of("", n).astype(BF16) if n in big else shard_of("", n) for n in later[tag]]
        started = _to_chips_start(shards, per_chip_slot=False, name="gather_" + tag + "_start", after=after)
        travelling[tag] = (shards, started)
        return started[-1][0, 0]

    mid_sent = send("mid")
    gathered = _gather_weights([shard_of("", "w_in_ab").astype(BF16)], [])
    full = {"w_in_ab": _join_shards(gathered[0], _SHARD_AXIS["w_in_ab"])}
    for name in ("norm_ab", "q_a_norm", "kv_a_norm", "pool_w", "pool_scale"):
        full[name] = shard_of("", name)
    lw = _layout_in_ab(full)
    lw["norm_ab"] = lw["norm_ab"] + mid_sent

    def more_weights(tag, after):
        shards, started = travelling[tag]
        landed = _to_chips_wait(started, after, per_chip_slot=False, name="gather_" + tag + "_wait")
        w = {}
        for name, land, own in zip(later[tag], landed, shards):
            w[name] = _join_shards(lax.dynamic_update_index_in_dim(land, own, chip, 0), _SHARD_AXIS[name])
        if tag == "mid":
            out = _layout_mid(w)
            out["wq"] = out["wq"] + send("odd", after=landed[0]).astype(BF16)
            return out
        for name in ("a_log", "dt_bias", "o_norm", "final_norm"):
            w[name] = shard_of("", name)
        return _layout_odd(w)

    transposed = ("w_in_ab", "w_in_c")

    def chip_slots(names, g):
        grads = _unlayout_grads(g, names)
        return ([_split_shards(grads[n], 0 if n in transposed else _SHARD_AXIS[n]) for n in names],
                [n in transposed for n in names])

    def chip_partials(names, slots, partial, by_cols):
        return [_core_sum(s, p, core, name="core_sum_" + n, by_cols=b) for n, s, p, b in zip(names, slots, partial, by_cols)]

    groups = {"odd": big_odd, "out_ab": ("w_out_ab",), "in_ab": ("w_in_ab", "w_q_b", "w_kv_b")}
    sent, swapping = {}, {}

    def on_grads(tag, g, after):
        if tag == "odd":
            slots, by_cols = chip_slots(groups[tag], g)
            swapping[tag] = (slots, by_cols, _core_swap_partial_start(slots, by_cols, name="core_swap_partial_odd_start"))
            return swapping[tag][2][-1][0, 0]
        if tag == "odd_go":
            tag = "odd"
            _, by_cols, started = swapping[tag]
            slots, partial = _core_swap_partial_wait(started, after, by_cols, name="core_swap_partial_odd_wait")
        else:
            slots, by_cols = chip_slots(groups[tag], g)
            partial = _core_swap_partial(slots, by_cols, name="core_swap_partial_" + tag)
        part = chip_partials(groups[tag], slots, partial, by_cols)
        sent[tag] = (part, _to_chips_start(part, per_chip_slot=True, name="exchange_" + tag + "_start"))
        token = sent[tag][1][-1][0, 0]
        if tag == "in_ab":
            pack = _pack_small({**g, "norm_ab": jnp.zeros((1, 1024), F32)}, g["loss"])
            sent["small"] = (pack, _to_chips_start([pack], per_chip_slot=False, to_all=True, name="exchange_small_start"))
            token = token + sent["small"][1][-1][0, 0]
        return token

    loss_tile, dx, g = _local_step(x[0], positions.reshape(t, 1), loss_target[0], lw, more_weights, on_grads)
    late_all = _chip_exchange([], g["norm_ab"].reshape(8, LANES))[-1]
    pack, started = sent.pop("small")
    landed = _to_chips_wait(started, late_all, per_chip_slot=False, to_all=True, name="exchange_small_wait")[0]
    small_all = lax.dynamic_update_index_in_dim(landed, pack, 2 * chip + c, 0)
    halves = {}
    for tag, names in groups.items():
        part, started = sent[tag]
        landed = _to_chips_wait(started, late_all, per_chip_slot=True, name="exchange_" + tag + "_wait")
        for n, l, p in zip(names, landed, part):
            l = lax.dynamic_update_index_in_dim(l, lax.dynamic_index_in_dim(p, chip, 0, keepdims=False), chip, 0)
            halves[n] = _chip_sum(l, core, name="chip_sum_" + n, by_cols=n in transposed)
    gbig = dict(zip(big, _core_swap_sum([halves[n] for n in big], [n in transposed for n in big])))

    res = {}
    for name in big:
        operands = [gbig[name], shard_of("", name), shard_of("m_", name), shard_of("v_", name)]
        flip = name in transposed
        if flip:
            operands[1:] = [jnp.transpose(a) for a in operands[1:]]
        out = (operands[0],) + tuple(_adamw_rows(*operands, name="adamw_" + name))
        out = [jnp.transpose(a) for a in out] if flip else out
        res["grad", name], res["delta", name], res["m", name], res["v", name] = out
    out = _small_update(small_all, late_all, [shard_of("", n) for n in _SMALL_NAMES], [shard_of("m_", n) for n in _SMALL_NAMES],
                        [shard_of("v_", n) for n in _SMALL_NAMES])
    for i, name in enumerate(_SMALL_NAMES):
        res["grad", name], res["delta", name], res["m", name], res["v", name] = out[4 * i:4 * i + 4]
    res = {k: a.reshape(given[k[1]].shape) for k, a in res.items()}
    loss = out[-1][0, 0]
    outs = [loss, dx.reshape(x.shape)]
    for key in ("grad", "delta", "m", "v"):
        outs += [res[key, n] for n in _ALL_NAMES]
    return tuple(outs)
```

```python
import functools

import jax
import jax.numpy as jnp
from jax import lax
from jax.experimental import pallas as pl
from jax.experimental.pallas import tpu as pltpu

F32 = jnp.float32
BF16 = jnp.bfloat16
HI = lax.Precision.HIGHEST
MESH = pl.DeviceIdType.MESH

RMS_EPS = 1e-6
MLA_HEADS = 8
MLA_Q_RANK = 256
MLA_KV_RANK = 128
MLA_NOPE = 64
MLA_ROPE = 32
MLA_V = 64
ROPE_THETA = 10000.0
POOL_WINDOWS = (2, 4, 8, 16)
POOL_GROUP = 128
POOL_WIDTH = 512
POOL_HALO = 16
GDN_HEADS = 8
GDN_DK = 128
CONV_WIDTH = 4
CONV_HALO = 8
CHUNK = 64
IN_AB_PAD = 2048
IN_C_PAD = 4224
ATT_SCALE = (MLA_NOPE + MLA_ROPE) ** -0.5
LOG2E = 1.4426950408889634

ADAM_LR = 0.001
ADAM_B1 = 0.9
ADAM_B2 = 0.999
ADAM_EPS = 1e-08
ADAM_WD = 0.01
ADAM_STEP = 10

LANES = 128
VMEM_LIMIT = 56 * 1024 * 1024

ROW_TILE = 256
ATT_TILE = 1024
GDN_TILE = 256
MM_TILE = (1024, 1408, 2048)

NN = (((1,), (0,)), ((), ()))
NT = (((1,), (1,)), ((), ()))
TN = (((0,), (0,)), ((), ()))


def _dot(a, b, dims=NN, prec=None):
    return lax.dot_general(a, b, dims, precision=prec, preferred_element_type=F32)


def _tile(n, pref):
    if n <= pref:
        return n
    step = LANES if pref >= LANES else 8
    for t in range(pref - pref % step, 0, -step):
        if n % t == 0:
            return t
    return n


def _params(sem):
    return pltpu.CompilerParams(dimension_semantics=sem, vmem_limit_bytes=VMEM_LIMIT)


def _sigmoid(x):
    return 0.5 * jnp.tanh(0.5 * x) + 0.5


def _softplus(x):
    return jnp.maximum(x, 0.0) + jnp.log(1.0 + jnp.exp(-jnp.abs(x)))


def _matmul(a, b, mode, *, name):
    if mode == "nn":
        (m, k), (k2, n) = a.shape, b.shape
    elif mode == "nt":
        (m, k), (n, k2) = a.shape, b.shape
    else:
        (k, m), (k2, n) = a.shape, b.shape
    assert k == k2, (a.shape, b.shape, mode)
    tm, tn, tk = _tile(m, MM_TILE[0]), _tile(n, MM_TILE[1]), _tile(k, MM_TILE[2])
    nk = k // tk
    if mode == "tn":
        a_spec = pl.BlockSpec((tk, tm), lambda i, j, kk: (kk, i))
    else:
        a_spec = pl.BlockSpec((tm, tk), lambda i, j, kk: (i, kk))
    if mode == "nt":
        b_spec = pl.BlockSpec((tn, tk), lambda i, j, kk: (j, kk))
    else:
        b_spec = pl.BlockSpec((tk, tn), lambda i, j, kk: (kk, j))
    o_spec = pl.BlockSpec((tm, tn), lambda i, j, kk: (i, j))
    dims = {"nn": NN, "nt": NT, "tn": TN}[mode]

    def body(a_ref, b_ref, o_ref, *scratch):
        if nk == 1:
            o_ref[...] = _dot(a_ref[...], b_ref[...], dims)
            return
        acc = scratch[0]
        kk = pl.program_id(2)

        @pl.when(kk == 0)
        def _():
            acc[...] = jnp.zeros_like(acc)

        acc[...] += _dot(a_ref[...], b_ref[...], dims)

        @pl.when(kk == nk - 1)
        def _():
            o_ref[...] = acc[...]

    return pl.pallas_call(
        body, name=name, grid=(m // tm, n // tn, nk), in_specs=[a_spec, b_spec], out_specs=o_spec,
        out_shape=jax.ShapeDtypeStruct((m, n), F32),
        scratch_shapes=[pltpu.VMEM((tm, tn), F32)] if nk > 1 else [],
        compiler_params=_params(("parallel", "parallel", "arbitrary")),
    )(a, b)


def _rms_in_proj(h, g, w, *, name):
    t, d = h.shape
    n = w.shape[1]
    tm, tn = _tile(t, MM_TILE[0]), _tile(n, MM_TILE[1])

    def body(h_ref, g_ref, w_ref, o_ref, hn_ref):
        @pl.when(pl.program_id(1) == 0)
        def _():
            x = h_ref[...]
            r = lax.rsqrt(jnp.mean(x * x, axis=-1, keepdims=True) + RMS_EPS)
            hn_ref[...] = (x * r * g_ref[...]).astype(BF16)

        o_ref[...] = _dot(hn_ref[...], w_ref[...])

    return pl.pallas_call(
        body, name=name, grid=(t // tm, n // tn),
        in_specs=[pl.BlockSpec((tm, d), lambda i, j: (i, 0)), pl.BlockSpec((1, d), lambda i, j: (0, 0)),
                  pl.BlockSpec((d, tn), lambda i, j: (0, j))],
        out_specs=[pl.BlockSpec((tm, tn), lambda i, j: (i, j)), pl.BlockSpec((tm, d), lambda i, j: (i, 0))],
        out_shape=[jax.ShapeDtypeStruct((t, n), F32), jax.ShapeDtypeStruct((t, d), BF16)],
        compiler_params=_params(("parallel", "arbitrary")),
    )(h, g, w)


def _matmul_rms_bwd(dproj, w, h, g, dres, *, name, prev_y=None):
    t, k = dproj.shape
    d = w.shape[0]
    tm = _tile(t, 2 * ROW_TILE)
    chained = prev_y is not None

    def body(dp_ref, w_ref, h_ref, g_ref, dres_ref, *rest):
        i = pl.program_id(0)
        dh_ref, dg_ref = rest[-4:-2] if chained else rest
        dyv = _dot(dp_ref[...], w_ref[...], NT)
        x = h_ref[...]
        r = lax.rsqrt(jnp.mean(x * x, axis=-1, keepdims=True) + RMS_EPS)
        xh = x * r
        dxh = dyv * g_ref[...]
        dh = dres_ref[...] + r * (dxh - xh * jnp.mean(dxh * xh, axis=-1, keepdims=True))
        dh_ref[...] = dh

        @pl.when(i == 0)
        def _():
            dg_ref[...] = jnp.zeros_like(dg_ref)
            if chained:
                rest[-1][...] = jnp.zeros_like(rest[-1])

        dg_ref[...] += jnp.sum(dyv * xh, axis=0, keepdims=True)
        if chained:
            y_ref, dhb_ref, dw_ref = rest[0], rest[-2], rest[-1]
            dhb_ref[...] = dh.astype(BF16)
            dw_ref[...] += _dot(y_ref[...], dhb_ref[...], TN)

    row = pl.BlockSpec((tm, d), lambda i: (i, 0))
    vec = pl.BlockSpec((1, d), lambda i: (0, 0))
    in_specs = [pl.BlockSpec((tm, k), lambda i: (i, 0)), pl.BlockSpec((d, k), lambda i: (0, 0)), row, vec, row]
    out_specs, out_shape = [row, vec], [jax.ShapeDtypeStruct((t, d), F32), jax.ShapeDtypeStruct((1, d), F32)]
    args = [dproj, w, h, g, dres]
    if chained:
        in_specs.append(row)
        args.append(prev_y)
        out_specs += [row, pl.BlockSpec((d, d), lambda i: (0, 0))]
        out_shape += [jax.ShapeDtypeStruct((t, d), BF16), jax.ShapeDtypeStruct((d, d), F32)]
    return pl.pallas_call(
        body, name=name, grid=(t // tm,), in_specs=in_specs, out_specs=out_specs, out_shape=out_shape,
        compiler_params=_params(("arbitrary",)),
    )(*args)


def _rope_partner(x):
    lane = lax.broadcasted_iota(jnp.int32, x.shape, 1)
    swapped = jnp.where(lane < MLA_NOPE + MLA_ROPE // 2, pltpu.roll(x, LANES - 16, 1), pltpu.roll(x, 16, 1))
    return jnp.where((lane >= MLA_NOPE) & (lane < MLA_NOPE + MLA_ROPE), swapped, 0.0)


def _pool_counts(row0, tm, w):
    t_idx = row0 + lax.broadcasted_iota(jnp.int32, (tm, POOL_GROUP), 0)
    return jnp.minimum(t_idx + 1, w).astype(F32)


def _ab_prep(proj, pos, inv_freq, q_a_norm, kv_a_norm, wq, wk, wv, wpool, *, name):
    t = proj.shape[0]
    tm = _tile(t, ROW_TILE)
    hb = tm // POOL_HALO

    def body(p_ref, halo_ref, pos_ref, inv_ref, qg_ref, kg_ref, wq_ref, wk_ref, wv_ref, wp_ref,
             q_ref, k_ref, v_ref, yb_ref, qn_ref, kvn_ref, d_ref, cos_ref, sin_ref, ext):
        i = pl.program_id(0)
        ql = p_ref[:, 0:MLA_Q_RANK]
        r = lax.rsqrt(jnp.mean(ql * ql, axis=-1, keepdims=True) + RMS_EPS)
        qn = (ql * r * qg_ref[...]).astype(BF16)
        qn_ref[...] = qn
        kl = p_ref[:, MLA_Q_RANK:MLA_Q_RANK + MLA_KV_RANK]
        r = lax.rsqrt(jnp.mean(kl * kl, axis=-1, keepdims=True) + RMS_EPS)
        kvn = (kl * r * kg_ref[...]).astype(BF16)
        kvn_ref[...] = kvn
        ang = pos_ref[...].astype(F32) * inv_ref[...]
        lane = lax.broadcasted_iota(jnp.int32, (tm, LANES), 1)
        in_rope = (lane >= MLA_NOPE) & (lane < MLA_NOPE + MLA_ROPE)
        cos_t = jnp.where(in_rope, jnp.cos(ang), 1.0)
        sin_t = jnp.where(in_rope, jnp.sin(ang), 0.0)
        sin_t = jnp.where(lane < MLA_NOPE + MLA_ROPE // 2, -sin_t, sin_t)
        cos_ref[...] = cos_t
        sin_ref[...] = sin_t
        kr = p_ref[:, 384:512]
        kr = kr * cos_t + _rope_partner(kr) * sin_t
        qraw = _dot(qn, wq_ref[...])
        kvk = _dot(kvn, wk_ref[...])
        for h in range(MLA_HEADS):
            sl = slice(h * LANES, (h + 1) * LANES)
            qh = qraw[:, sl]
            q_ref[:, sl] = ((qh * cos_t + _rope_partner(qh) * sin_t) * (ATT_SCALE * LOG2E)).astype(BF16)
            k_ref[:, sl] = (kvk[:, sl] + kr).astype(BF16)
        v_ref[...] = _dot(kvn, wv_ref[...]).astype(BF16)
        xp = p_ref[:, 512:1024]
        ext[0:POOL_HALO, :] = jnp.where(i > 0, halo_ref[...], 0.0)
        ext[POOL_HALO:POOL_HALO + tm, :] = xp
        for g, w in enumerate(POOL_WINDOWS):
            lo = g * POOL_GROUP
            acc = ext[POOL_HALO:POOL_HALO + tm, lo:lo + POOL_GROUP]
            for s in range(1, w):
                acc = acc + ext[POOL_HALO - s:POOL_HALO - s + tm, lo:lo + POOL_GROUP]
            cnt = _pool_counts(i * tm, tm, w)
            d_ref[:, lo:lo + POOL_GROUP] = (acc / cnt - xp[:, lo:lo + POOL_GROUP]).astype(BF16)
        yb_ref[...] = _dot(d_ref[...], wp_ref[...])

    row = lambda w: pl.BlockSpec((tm, w), lambda i: (i, 0))
    vec = lambda w: pl.BlockSpec((1, w), lambda i: (0, 0))
    whole = lambda a: pl.BlockSpec(a.shape, lambda i: (0, 0))
    return pl.pallas_call(
        body, name=name, grid=(t // tm,),
        in_specs=[row(1024), pl.BlockSpec((POOL_HALO, POOL_WIDTH), lambda i: (jnp.maximum(i * hb - 1, 0), 1)),
                  pl.BlockSpec((tm, 1), lambda i: (i, 0)), vec(LANES), vec(MLA_Q_RANK), vec(MLA_KV_RANK),
                  whole(wq), whole(wk), whole(wv), whole(wpool)],
        out_specs=[row(1024), row(1024), row(512), row(512), row(MLA_Q_RANK), row(MLA_KV_RANK), row(POOL_WIDTH),
                   row(LANES), row(LANES)],
        out_shape=[jax.ShapeDtypeStruct((t, 1024), BF16), jax.ShapeDtypeStruct((t, 1024), BF16),
                   jax.ShapeDtypeStruct((t, 512), BF16), jax.ShapeDtypeStruct((t, 512), F32),
                   jax.ShapeDtypeStruct((t, MLA_Q_RANK), BF16), jax.ShapeDtypeStruct((t, MLA_KV_RANK), BF16),
                   jax.ShapeDtypeStruct((t, POOL_WIDTH), BF16), jax.ShapeDtypeStruct((t, LANES), F32),
                   jax.ShapeDtypeStruct((t, LANES), F32)],
        scratch_shapes=[pltpu.VMEM((tm + POOL_HALO, POOL_WIDTH), F32)],
        compiler_params=_params(("parallel",)),
    )(proj, proj, pos, inv_freq, q_a_norm, kv_a_norm, wq, wk, wv, wpool)


def _ab_prep_bwd(proj, q_a_norm, kv_a_norm, dq, dk, dv, cos_t, sin_t, dyb, dz, qn, kvn, d, hn, wq, wk, wv, wpool, *, name):
    t = proj.shape[0]
    tm = _tile(t, ROW_TILE)
    hb = tm // POOL_HALO
    last_halo = t // POOL_HALO - 1
    nt = t // tm

    def body(p_ref, qg_ref, kg_ref, dq_ref, dk_ref, dv_ref, c_ref, s_ref, dyb_ref, dybn_ref, dz_ref,
             qn_ref, kvn_ref, d_ref, hn_ref, wq_ref, wk_ref, wv_ref, wp_ref,
             dp_ref, dqg_ref, dkg_ref, dwq_ref, dwk_ref, dwv_ref, dwp_ref, dwin_ref, ext, dqr_ref, dkb_ref):
        i = pl.program_id(0)

        @pl.when(i == 0)
        def _():
            for ref in (dqg_ref, dkg_ref, dwq_ref, dwk_ref, dwv_ref, dwp_ref, dwin_ref):
                ref[...] = jnp.zeros_like(ref)

        def norm_bwd(x, g, dy, dg_ref):
            r = lax.rsqrt(jnp.mean(x * x, axis=-1, keepdims=True) + RMS_EPS)
            xh = x * r
            dxh = dy * g
            dg_ref[...] += jnp.sum(dy * xh, axis=0, keepdims=True)
            return r * (dxh - xh * jnp.mean(dxh * xh, axis=-1, keepdims=True))

        c, s = c_ref[...], s_ref[...]
        lane = lax.broadcasted_iota(jnp.int32, (tm, LANES), 1)
        in_rope = (lane >= MLA_NOPE) & (lane < MLA_NOPE + MLA_ROPE)
        dkr = jnp.zeros((tm, LANES), F32)
        for h in range(MLA_HEADS):
            sl = slice(h * LANES, (h + 1) * LANES)
            g = dq_ref[:, sl]
            dqr_ref[:, sl] = ((g * c + _rope_partner(g * s)) * ATT_SCALE).astype(BF16)
            gk = dk_ref[:, sl]
            dkb_ref[:, sl] = gk.astype(BF16)
            dkr = dkr + jnp.where(in_rope, gk, 0.0)
        dkr = dkr * c + _rope_partner(dkr * s)
        dqn = _dot(dqr_ref[...], wq_ref[...], NT)
        dkvn = _dot(dkb_ref[...], wk_ref[...], NT) + _dot(dv_ref[...], wv_ref[...], NT)
        dql = norm_bwd(p_ref[:, 0:MLA_Q_RANK], qg_ref[...], dqn, dqg_ref)
        dp_ref[:, 0:MLA_Q_RANK] = dql.astype(BF16)
        dkl = norm_bwd(p_ref[:, MLA_Q_RANK:384], kg_ref[...], dkvn, dkg_ref)
        dp_ref[:, MLA_Q_RANK:384] = dkl.astype(BF16)
        dp_ref[:, 384:512] = dkr.astype(BF16)
        ddv = _dot(dyb_ref[...], wp_ref[...], NT)
        ddn = _dot(dybn_ref[...], wp_ref[...], NT)
        for g, w in enumerate(POOL_WINDOWS):
            lo = g * POOL_GROUP
            ext[0:tm, lo:lo + POOL_GROUP] = ddv[:, lo:lo + POOL_GROUP] / _pool_counts(i * tm, tm, w)
            nxt = ddn[:, lo:lo + POOL_GROUP] / _pool_counts((i + 1) * tm, POOL_HALO, w)
            ext[tm:tm + POOL_HALO, lo:lo + POOL_GROUP] = jnp.where(i < nt - 1, nxt, 0.0)
        for g, w in enumerate(POOL_WINDOWS):
            lo = g * POOL_GROUP
            acc = ext[0:tm, lo:lo + POOL_GROUP]
            for s in range(1, w):
                acc = acc + ext[s:s + tm, lo:lo + POOL_GROUP]
            dp_ref[:, 512 + lo:512 + lo + POOL_GROUP] = (acc - ddv[:, lo:lo + POOL_GROUP]).astype(BF16)
        dp_ref[:, 1024:2048] = dz_ref[...]
        dwq_ref[...] += _dot(qn_ref[...], dqr_ref[...], TN)
        dwk_ref[...] += _dot(kvn_ref[...], dkb_ref[...], TN)
        dwv_ref[...] += _dot(kvn_ref[...], dv_ref[...], TN)
        dwp_ref[...] += _dot(d_ref[...], dyb_ref[...], TN)
        dwin_ref[...] += _dot(dp_ref[...], hn_ref[...], TN)

    row = lambda w: pl.BlockSpec((tm, w), lambda i: (i, 0))
    vec = lambda w: pl.BlockSpec((1, w), lambda i: (0, 0))
    whole = lambda a: pl.BlockSpec(a.shape, lambda i: (0, 0))
    weights = (wq, wk, wv, wpool)
    return pl.pallas_call(
        body, name=name, grid=(nt,),
        in_specs=[row(1024), vec(MLA_Q_RANK), vec(MLA_KV_RANK), row(1024), row(1024), row(512), row(LANES), row(LANES),
                  row(POOL_WIDTH),
                  pl.BlockSpec((POOL_HALO, POOL_WIDTH), lambda i: (jnp.minimum((i + 1) * hb, last_halo), 0)),
                  row(1024), row(MLA_Q_RANK), row(MLA_KV_RANK), row(POOL_WIDTH), row(1024)] + [whole(w) for w in weights],
        out_specs=[row(IN_AB_PAD), vec(MLA_Q_RANK), vec(MLA_KV_RANK)] + [whole(w) for w in weights]
        + [pl.BlockSpec((IN_AB_PAD, 1024), lambda i: (0, 0))],
        out_shape=[jax.ShapeDtypeStruct((t, IN_AB_PAD), BF16), jax.ShapeDtypeStruct((1, MLA_Q_RANK), F32),
                   jax.ShapeDtypeStruct((1, MLA_KV_RANK), F32)] + [jax.ShapeDtypeStruct(w.shape, F32) for w in weights]
        + [jax.ShapeDtypeStruct((IN_AB_PAD, 1024), F32)],
        scratch_shapes=[pltpu.VMEM((tm + POOL_HALO, POOL_WIDTH), F32), pltpu.VMEM((tm, 1024), BF16),
                        pltpu.VMEM((tm, 1024), BF16)],
        compiler_params=_params(("arbitrary",)),
    )(proj, q_a_norm, kv_a_norm, dq, dk, dv, cos_t, sin_t, dyb, dyb, dz, qn, kvn, d, hn, wq, wk, wv, wpool)


def _gate_out_proj(o, ybraw, proj, pool_scale, w, hres, *, name):
    t = o.shape[0]
    tm = _tile(t, 2 * ROW_TILE)

    def body(o_ref, yb_ref, z_ref, ps_ref, w_ref, h_ref, ho_ref, y_ref):
        z = z_ref[...]
        sz = z * _sigmoid(z)
        y_ref[:, 0:512] = (o_ref[...] * sz[:, 0:512]).astype(BF16)
        y_ref[:, 512:1024] = (yb_ref[...] * ps_ref[...] * sz[:, 512:1024]).astype(BF16)
        ho_ref[...] = h_ref[...] + _dot(y_ref[...], w_ref[...])

    row = lambda w_: pl.BlockSpec((tm, w_), lambda i: (i, 0))
    return pl.pallas_call(
        body, name=name, grid=(t // tm,),
        in_specs=[row(512), row(512), pl.BlockSpec((tm, 1024), lambda i: (i, 1)), pl.BlockSpec((1, 512), lambda i: (0, 0)),
                  pl.BlockSpec(w.shape, lambda i: (0, 0)), row(1024)],
        out_specs=[row(1024), row(1024)],
        out_shape=[jax.ShapeDtypeStruct((t, 1024), F32), jax.ShapeDtypeStruct((t, 1024), BF16)],
        compiler_params=_params(("parallel",)),
    )(o, ybraw, proj, pool_scale, w, hres)


def _gate_bwd(dh, w, o, ybraw, proj, pool_scale, *, name):
    t = o.shape[0]
    tm = _tile(t, ROW_TILE)

    def body(dh_ref, w_ref, o_ref, yb_ref, z_ref, ps_ref, do_ref, dl_ref, dyb_ref, dz_ref, dps_ref):
        i = pl.program_id(0)
        z = z_ref[...]
        sg = _sigmoid(z)
        sz = z * sg
        dsz = sg * (1.0 + z * (1.0 - sg))
        dyv = _dot(dh_ref[...], w_ref[...], NT)
        dcat = dyv * sz
        ov = o_ref[...]
        ybs = yb_ref[...] * ps_ref[...]
        dz_ref[:, 0:512] = (dyv[:, 0:512] * ov * dsz[:, 0:512]).astype(BF16)
        dz_ref[:, 512:1024] = (dyv[:, 512:1024] * ybs * dsz[:, 512:1024]).astype(BF16)
        do = dcat[:, 0:512]
        do_ref[...] = do.astype(BF16)
        r_i = (lax.broadcasted_iota(jnp.int32, (1024, 512), 0) % 512) // MLA_V
        c_i = lax.broadcasted_iota(jnp.int32, (1024, 512), 1) // MLA_V
        prod = do * ov
        hi = prod.astype(BF16)
        lo = (prod - hi.astype(F32)).astype(BF16)
        dl_ref[...] = _dot(jnp.concatenate([hi, lo], axis=1), (r_i == c_i).astype(BF16))
        dyb_ref[...] = (dcat[:, 512:1024] * ps_ref[...]).astype(BF16)

        @pl.when(i == 0)
        def _():
            dps_ref[...] = jnp.zeros_like(dps_ref)

        dps_ref[...] += jnp.sum(dcat[:, 512:1024] * yb_ref[...], axis=0, keepdims=True)

    row = lambda w: pl.BlockSpec((tm, w), lambda i: (i, 0))
    vec = pl.BlockSpec((1, 512), lambda i: (0, 0))
    return pl.pallas_call(
        body, name=name, grid=(t // tm,),
        in_specs=[row(1024), pl.BlockSpec(w.shape, lambda i: (0, 0)), row(512), row(512),
                  pl.BlockSpec((tm, 1024), lambda i: (i, 1)), vec],
        out_specs=[row(512), row(512), row(512), row(1024), vec],
        out_shape=[jax.ShapeDtypeStruct((t, 512), BF16), jax.ShapeDtypeStruct((t, 512), F32),
                   jax.ShapeDtypeStruct((t, 512), BF16), jax.ShapeDtypeStruct((t, 1024), BF16),
                   jax.ShapeDtypeStruct((1, 512), F32)],
        compiler_params=_params(("arbitrary",)),
    )(dh, w, o, ybraw, proj, pool_scale)


ATT_HP_FWD = 4
ATT_HP_BWD = 2


def _diag_mask(tq):
    return lax.broadcasted_iota(jnp.int32, (tq, tq), 1) <= lax.broadcasted_iota(jnp.int32, (tq, tq), 0)


def _block_schedule(nq, key_major):
    if key_major:
        pairs = [(qi, ki) for ki in range(nq) for qi in range(ki, nq)]
    else:
        pairs = [(qi, ki) for qi in range(nq) for ki in range(qi + 1)]
    return jnp.asarray([p[0] for p in pairs], jnp.int32), jnp.asarray([p[1] for p in pairs], jnp.int32)


def _attn_fwd(q, k, v, *, name):
    t = q.shape[0]
    tq = _tile(t, ATT_TILE)
    nq = t // tq
    hp = ATT_HP_FWD
    qi_tab, ki_tab = _block_schedule(nq, key_major=False)

    def body(qi_ref, ki_ref, q_ref, k_ref, v_ref, o_ref, lse_ref, m_sc, l_sc, acc_sc):
        step = pl.program_id(1)
        qi, ki = qi_ref[step], ki_ref[step]

        @pl.when(ki == 0)
        def _():
            m_sc[...] = jnp.full_like(m_sc, -jnp.inf)
            l_sc[...] = jnp.zeros_like(l_sc)
            acc_sc[...] = jnp.zeros_like(acc_sc)

        def block(on_diagonal):
            scores = []
            for h in range(hp):
                sl = slice(h * LANES, (h + 1) * LANES)
                scores.append(_dot(q_ref[:, sl], k_ref[:, sl], NT))
            if on_diagonal:
                mask = _diag_mask(tq)
                scores = [jnp.where(mask, s, -jnp.inf) for s in scores]
            for h, s in enumerate(scores):
                vv = v_ref[:, (h // 2) * LANES:(h // 2 + 1) * LANES]
                m_prev = m_sc[h]
                m_new = jnp.maximum(m_prev, jnp.max(s, axis=-1, keepdims=True))
                alpha = jnp.exp2(m_prev - m_new)
                p = jnp.exp2(s - m_new[:, 0:1])
                l_sc[h] = alpha * l_sc[h] + jnp.sum(p, axis=-1, keepdims=True)
                acc_sc[h] = alpha * acc_sc[h] + _dot(p.astype(BF16), vv)
                m_sc[h] = m_new

        pl.when(ki < qi)(functools.partial(block, False))
        pl.when(ki == qi)(functools.partial(block, True))

        @pl.when(ki == qi)
        def _():
            first = lax.broadcasted_iota(jnp.int32, (tq, LANES), 1) < MLA_V
            for pr in range(hp // 2):
                a, b = 2 * pr, 2 * pr + 1
                sl = slice(pr * LANES, (pr + 1) * LANES)
                o_ref[:, sl] = jnp.where(first, acc_sc[a] / l_sc[a], acc_sc[b] / l_sc[b])
                lse_ref[:, sl] = jnp.where(first, m_sc[a] + jnp.log2(l_sc[a]), m_sc[b] + jnp.log2(l_sc[b]))

    grid_spec = pltpu.PrefetchScalarGridSpec(
        num_scalar_prefetch=2, grid=(MLA_HEADS // hp, qi_tab.shape[0]),
        in_specs=[pl.BlockSpec((tq, hp * LANES), lambda g, s, qt, kt: (qt[s], g)),
                  pl.BlockSpec((tq, hp * LANES), lambda g, s, qt, kt: (kt[s], g)),
                  pl.BlockSpec((tq, hp * MLA_V), lambda g, s, qt, kt: (kt[s], g))],
        out_specs=[pl.BlockSpec((tq, hp * MLA_V), lambda g, s, qt, kt: (qt[s], g)),
                   pl.BlockSpec((tq, hp * MLA_V), lambda g, s, qt, kt: (qt[s], g))],
        scratch_shapes=[pltpu.VMEM((hp, tq, LANES), F32)] * 3,
    )
    return pl.pallas_call(
        body, name=name, grid_spec=grid_spec,
        out_shape=[jax.ShapeDtypeStruct((t, 512), F32), jax.ShapeDtypeStruct((t, 512), F32)],
        compiler_params=_params(("parallel", "arbitrary")),
    )(qi_tab, ki_tab, q, k, v)


def _attn_bwd(q, k, v, do, lse, delta, *, name):
    t = q.shape[0]
    tq = _tile(t, ATT_TILE)
    nq = t // tq
    hp = ATT_HP_BWD
    qi_tab, ki_tab = _block_schedule(nq, key_major=True)

    def body(qi_ref, ki_ref, q_ref, k_ref, v_ref, do_ref, lse_ref, dl_ref, dq_ref, dk_ref, dv_ref, dk_sc, dv_sc):
        step = pl.program_id(1)
        qi, ki = qi_ref[step], ki_ref[step]

        @pl.when(step == 0)
        def _():
            dq_ref[...] = jnp.zeros_like(dq_ref)

        @pl.when(qi == ki)
        def _():
            dk_sc[...] = jnp.zeros_like(dk_sc)
            dv_sc[...] = jnp.zeros_like(dv_sc)

        def block(on_diagonal):
            lane = lax.broadcasted_iota(jnp.int32, (tq, LANES), 1)
            rows = pl.ds(pl.multiple_of(qi * tq, tq), tq)
            heads = [slice(h * LANES, (h + 1) * LANES) for h in range(hp)]
            scores = [_dot(q_ref[:, sl], k_ref[:, sl], NT) for sl in heads]
            dps = []
            for h in range(hp):
                dov = do_ref[:, (h // 2) * LANES:(h // 2 + 1) * LANES]
                mine = (lane < MLA_V) if h % 2 == 0 else (lane >= MLA_V)
                dps.append(_dot(jnp.where(mine, dov, jnp.zeros_like(dov)), v_ref[:, (h // 2) * LANES:(h // 2 + 1) * LANES], NT))
            mask = _diag_mask(tq) if on_diagonal else None
            for h, sl in enumerate(heads):
                col = (h // 2) * LANES + (h % 2) * MLA_V
                p = jnp.exp2(scores[h] - lse_ref[:, col:col + 1])
                if on_diagonal:
                    p = jnp.where(mask, p, 0.0)
                ds = (p * (dps[h] - dl_ref[:, col:col + 1])).astype(BF16)
                dv_sc[h] += _dot(p.astype(BF16), do_ref[:, (h // 2) * LANES:(h // 2 + 1) * LANES], TN)
                dk_sc[h] += _dot(ds, q_ref[:, sl], TN)
                dq_ref[rows, sl] += _dot(ds, k_ref[:, sl], NN)

        pl.when(qi > ki)(functools.partial(block, False))
        pl.when(qi == ki)(functools.partial(block, True))

        @pl.when(qi == nq - 1)
        def _():
            first = lax.broadcasted_iota(jnp.int32, (tq, LANES), 1) < MLA_V
            for h in range(hp):
                dk_ref[:, h * LANES:(h + 1) * LANES] = dk_sc[h] * (1.0 / LOG2E)
            for pr in range(hp // 2):
                dv_ref[:, pr * LANES:(pr + 1) * LANES] = jnp.where(first, dv_sc[2 * pr], dv_sc[2 * pr + 1]).astype(BF16)

    qrow = lambda w: pl.BlockSpec((tq, w), lambda g, s, qt, kt: (qt[s], g))
    krow = lambda w: pl.BlockSpec((tq, w), lambda g, s, qt, kt: (kt[s], g))
    grid_spec = pltpu.PrefetchScalarGridSpec(
        num_scalar_prefetch=2, grid=(MLA_HEADS // hp, qi_tab.shape[0]),
        in_specs=[qrow(hp * LANES), krow(hp * LANES), krow(hp * MLA_V), qrow(hp * MLA_V), qrow(hp * MLA_V), qrow(hp * MLA_V)],
        out_specs=[pl.BlockSpec((t, hp * LANES), lambda g, s, qt, kt: (0, g)), krow(hp * LANES), krow(hp * MLA_V)],
        scratch_shapes=[pltpu.VMEM((hp, tq, LANES), F32), pltpu.VMEM((hp, tq, LANES), F32)],
    )
    return pl.pallas_call(
        body, name=name, grid_spec=grid_spec,
        out_shape=[jax.ShapeDtypeStruct((t, 1024), F32), jax.ShapeDtypeStruct((t, 1024), F32),
                   jax.ShapeDtypeStruct((t, 512), BF16)],
        compiler_params=_params(("parallel", "arbitrary")),
    )(qi_tab, ki_tab, q, k, v, do, lse, delta)


def _conv_rows(ext, tm, w_ref, sec):
    c0 = sec * 1024
    y = ext[CONV_HALO - 3:CONV_HALO - 3 + tm, c0:c0 + 1024] * w_ref[0:1, c0:c0 + 1024]
    for j in range(1, CONV_WIDTH):
        y = y + ext[CONV_HALO - 3 + j:CONV_HALO - 3 + j + tm, c0:c0 + 1024] * w_ref[j:j + 1, c0:c0 + 1024]
    return y


def _c_prep(proj_c, conv_w, a_log, dt_bias, *, name):
    t = proj_c.shape[0]
    tm = _tile(t, ROW_TILE)
    hb = tm // CONV_HALO

    def body(p_ref, halo_ref, ab_ref, w_ref, al_ref, dtb_ref, q_ref, k_ref, v_ref, g_ref, b_ref, gt_ref, ext):
        i = pl.program_id(0)
        ext[0:CONV_HALO, :] = jnp.where(i > 0, halo_ref[...], 0.0)
        ext[CONV_HALO:CONV_HALO + tm, :] = p_ref[...]
        for sec, o_ref in enumerate((q_ref, k_ref, v_ref)):
            y = _conv_rows(ext, tm, w_ref, sec)
            y = y * _sigmoid(y)
            if sec == 2:
                o_ref[...] = y
                continue
            scale = GDN_DK ** -0.5 if sec == 0 else 1.0
            for h in range(GDN_HEADS):
                sl = slice(h * LANES, (h + 1) * LANES)
                blk = y[:, sl]
                r = lax.rsqrt(jnp.sum(blk * blk, axis=-1, keepdims=True) + RMS_EPS)
                o_ref[:, sl] = blk * (r * scale)
        ab = ab_ref[...]
        g = -jnp.exp(al_ref[...]) * _softplus(ab + dtb_ref[...])
        beta = _sigmoid(ab)
        ri = lax.broadcasted_iota(jnp.int32, (tm, tm), 0)
        ci = lax.broadcasted_iota(jnp.int32, (tm, tm), 1)
        lower = ((ri // CHUNK) == (ci // CHUNK)) & (ri >= ci)
        gc = _dot(lower.astype(F32), g, NN, HI)
        eye = lax.broadcasted_iota(jnp.int32, (LANES, LANES), 0) == lax.broadcasted_iota(jnp.int32, (LANES, LANES), 1)
        gt_ref[...] = _dot(eye.astype(F32), gc, NT, HI)[0:GDN_HEADS, :]
        for h in range(GDN_HEADS):
            sl = slice(h * LANES, (h + 1) * LANES)
            g_ref[:, sl] = jnp.broadcast_to(gc[:, h:h + 1], (tm, LANES))
            b_ref[:, sl] = jnp.broadcast_to(beta[:, GDN_HEADS + h:GDN_HEADS + h + 1], (tm, LANES))

    row = lambda w: pl.BlockSpec((tm, w), lambda i: (i, 0))
    vec = lambda r, w: pl.BlockSpec((r, w), lambda i: (0, 0))
    out = jax.ShapeDtypeStruct((t, 1024), F32)
    return pl.pallas_call(
        body, name=name, grid=(t // tm,),
        in_specs=[row(3072), pl.BlockSpec((CONV_HALO, 3072), lambda i: (jnp.maximum(i * hb - 1, 0), 0)),
                  pl.BlockSpec((tm, LANES), lambda i: (i, 32)), vec(CONV_WIDTH, 3072), vec(1, LANES), vec(1, LANES)],
        out_specs=[row(1024)] * 5 + [pl.BlockSpec((GDN_HEADS, tm), lambda i: (0, i))],
        out_shape=[out] * 5 + [jax.ShapeDtypeStruct((GDN_HEADS, t), F32)],
        scratch_shapes=[pltpu.VMEM((tm + CONV_HALO, 3072), F32)],
        compiler_params=_params(("parallel",)),
    )(proj_c, proj_c, proj_c, conv_w, a_log, dt_bias)


def _c_prep_bwd(proj_c, conv_w, a_log, dt_bias, dq, dk, dv, dgb, dbb, dz, *, name):
    t = proj_c.shape[0]
    tm = _tile(t, ROW_TILE)
    hb = tm // CONV_HALO
    nt = t // tm
    rev = lambda i: nt - 1 - i

    def body(p_ref, halo_ref, ab_ref, w_ref, al_ref, dtb_ref, dq_ref, dk_ref, dv_ref, dg_ref, db_ref, dz_ref,
             dp_ref, dw_ref, dal_ref, ddt_ref, ext, dyext, carry, taps):
        step = pl.program_id(0)
        i = rev(step)

        @pl.when(step == 0)
        def _():
            dw_ref[...] = jnp.zeros_like(dw_ref)
            dal_ref[...] = jnp.zeros_like(dal_ref)
            ddt_ref[...] = jnp.zeros_like(ddt_ref)
            carry[...] = jnp.zeros_like(carry)

        ext[0:CONV_HALO, :] = jnp.where(i > 0, halo_ref[...], 0.0)
        ext[CONV_HALO:CONV_HALO + tm, :] = p_ref[...]
        for sec, g_ref in enumerate((dq_ref, dk_ref, dv_ref)):
            c0 = sec * 1024
            for j in range(CONV_WIDTH):
                taps[j] = ext[CONV_HALO - 3 + j:CONV_HALO - 3 + j + tm, c0:c0 + 1024]
            y = taps[0] * w_ref[0:1, c0:c0 + 1024]
            for j in range(1, CONV_WIDTH):
                y = y + taps[j] * w_ref[j:j + 1, c0:c0 + 1024]
            sg = _sigmoid(y)
            act = y * sg
            if sec == 2:
                dact = g_ref[...]
            else:
                scale = GDN_DK ** -0.5 if sec == 0 else 1.0
                parts = []
                for h in range(GDN_HEADS):
                    sl = slice(h * LANES, (h + 1) * LANES)
                    blk = act[:, sl]
                    r = lax.rsqrt(jnp.sum(blk * blk, axis=-1, keepdims=True) + RMS_EPS)
                    n = blk * r
                    dn = g_ref[:, sl] * scale
                    parts.append(r * (dn - n * jnp.sum(dn * n, axis=-1, keepdims=True)))
                dact = jnp.concatenate(parts, axis=-1)
            dy = dact * (sg * (1.0 + y * (1.0 - sg)))
            dyext[0:tm, c0:c0 + 1024] = dy
            for j in range(CONV_WIDTH):
                dw_ref[j:j + 1, c0:c0 + 1024] += jnp.sum(dy * taps[j], axis=0, keepdims=True)
        dyext[tm:tm + CONV_HALO, :] = carry[...]
        carry[...] = dyext[0:CONV_HALO, :]
        for sec in range(3):
            c0 = sec * 1024
            dx = dyext[3:3 + tm, c0:c0 + 1024] * w_ref[0:1, c0:c0 + 1024]
            for j in range(1, CONV_WIDTH):
                dx = dx + dyext[3 - j:3 - j + tm, c0:c0 + 1024] * w_ref[j:j + 1, c0:c0 + 1024]
            dp_ref[:, c0:c0 + 1024] = dx.astype(BF16)
        dp_ref[:, 3072:4096] = dz_ref[...]
        lane = lax.broadcasted_iota(jnp.int32, (tm, LANES), 1)
        dg = jnp.zeros((tm, LANES), F32)
        dbeta = jnp.zeros((tm, LANES), F32)
        for h in range(GDN_HEADS):
            sl = slice(h * LANES, (h + 1) * LANES)
            dg = dg + jnp.where(lane == h, dg_ref[:, sl], 0.0)
            dbeta = dbeta + jnp.where(lane == GDN_HEADS + h, db_ref[:, sl], 0.0)
        ri = lax.broadcasted_iota(jnp.int32, (tm, tm), 0)
        ci = lax.broadcasted_iota(jnp.int32, (tm, tm), 1)
        upper = ((ri // CHUNK) == (ci // CHUNK)) & (ri <= ci)
        dg = _dot(upper.astype(F32), dg, NN, HI)
        pre = ab_ref[...] + dtb_ref[...]
        s = _sigmoid(pre)
        a_exp = jnp.exp(al_ref[...])
        dg_da = dg * (-a_exp * s)
        dp_ref[:, 4096:IN_C_PAD] = (dg_da + dbeta * s * (1.0 - s)).astype(BF16)
        dal_ref[...] += jnp.sum(dg * (-a_exp * _softplus(pre)), axis=0, keepdims=True)
        ddt_ref[...] += jnp.sum(dg_da, axis=0, keepdims=True)

    row = lambda w: pl.BlockSpec((tm, w), lambda s: (rev(s), 0))
    vec = lambda r, w: pl.BlockSpec((r, w), lambda s: (0, 0))
    return pl.pallas_call(
        body, name=name, grid=(nt,),
        in_specs=[row(3072), pl.BlockSpec((CONV_HALO, 3072), lambda s: (jnp.maximum(rev(s) * hb - 1, 0), 0)),
                  pl.BlockSpec((tm, LANES), lambda s: (rev(s), 32)), vec(CONV_WIDTH, 3072), vec(1, LANES), vec(1, LANES),
                  row(1024), row(1024), row(1024), row(1024), row(1024), row(1024)],
        out_specs=[row(IN_C_PAD), vec(CONV_WIDTH, 3072), vec(1, LANES), vec(1, LANES)],
        out_shape=[jax.ShapeDtypeStruct((t, IN_C_PAD), BF16), jax.ShapeDtypeStruct((CONV_WIDTH, 3072), F32),
                   jax.ShapeDtypeStruct((1, LANES), F32), jax.ShapeDtypeStruct((1, LANES), F32)],
        scratch_shapes=[pltpu.VMEM((tm + CONV_HALO, 3072), F32), pltpu.VMEM((tm + CONV_HALO, 3072), F32),
                        pltpu.VMEM((CONV_HALO, 3072), F32), pltpu.VMEM((CONV_WIDTH, tm, 1024), F32)],
        compiler_params=_params(("arbitrary",)),
    )(proj_c, proj_c, proj_c, conv_w, a_log, dt_bias, dq, dk, dv, dgb, dbb, dz)


def _o_gate_bwd(dh, w, o, proj_c, o_norm, *, name):
    t = o.shape[0]
    tm = _tile(t, 2 * ROW_TILE)

    def body(dh_ref, w_ref, o_ref, z_ref, g_ref, do_ref, dz_ref, dg_ref, dy_ref):
        i = pl.program_id(0)

        @pl.when(i == 0)
        def _():
            dg_ref[...] = jnp.zeros_like(dg_ref)

        dy_ref[...] = _dot(dh_ref[...], w_ref[...], NT)
        dg = jnp.zeros((1, LANES), F32)
        for h in range(GDN_HEADS):
            sl = slice(h * LANES, (h + 1) * LANES)
            x = o_ref[:, sl]
            r = lax.rsqrt(jnp.mean(x * x, axis=-1, keepdims=True) + RMS_EPS)
            xh = x * r
            z = z_ref[:, sl]
            sg = _sigmoid(z)
            dyv = dy_ref[:, sl]
            dn = dyv * (z * sg)
            dz_ref[:, sl] = (dyv * xh * g_ref[...] * (sg * (1.0 + z * (1.0 - sg)))).astype(BF16)
            dxh = dn * g_ref[...]
            do_ref[:, sl] = r * (dxh - xh * jnp.mean(dxh * xh, axis=-1, keepdims=True))
            dg = dg + jnp.sum(dn * xh, axis=0, keepdims=True)
        dg_ref[...] += dg

    row = pl.BlockSpec((tm, 1024), lambda i: (i, 0))
    vec = pl.BlockSpec((1, LANES), lambda i: (0, 0))
    return pl.pallas_call(
        body, name=name, grid=(t // tm,),
        in_specs=[row, pl.BlockSpec(w.shape, lambda i: (0, 0)), row, pl.BlockSpec((tm, 1024), lambda i: (i, 3)), vec],
        out_specs=[row, row, vec],
        out_shape=[jax.ShapeDtypeStruct((t, 1024), F32), jax.ShapeDtypeStruct((t, 1024), BF16),
                   jax.ShapeDtypeStruct((1, LANES), F32)],
        scratch_shapes=[pltpu.VMEM((tm, 1024), F32)],
        compiler_params=_params(("arbitrary",)),
    )(dh, w, o, proj_c, o_norm)


PAIR = 2 * CHUNK
GDN_HP = 8


def _bdot(a, b, dims=NN):
    return _dot(a.astype(BF16), b.astype(BF16), dims)


def _each(f, *lists):
    return [f(*args) for args in zip(*lists)]


def _pair_common(q, k, v, gci, gcj, beta):
    ri = lax.broadcasted_iota(jnp.int32, (PAIR, PAIR), 0)
    ci = lax.broadcasted_iota(jnp.int32, (PAIR, PAIR), 1)
    same = (ri // CHUNK) == (ci // CHUNK)
    incl = same & (ri >= ci)
    strict = same & (ri > ci)
    eye = (ri == ci).astype(F32)
    first = lax.broadcasted_iota(jnp.int32, (PAIR, LANES), 0) < CHUNK
    gamma = _each(lambda gi, gj: jnp.where(incl, jnp.exp(jnp.minimum(gi - gj, 0.0)), 0.0), gci, gcj)
    kb = _each(jnp.multiply, k, beta)
    kk = _each(lambda a, b: _bdot(a, b, NT), kb, k)
    qk = _each(lambda a, b: _bdot(a, b, NT), q, k)
    m = _each(lambda x, g: jnp.where(strict, x * g, 0.0), kk, gamma)
    tm_ = _each(lambda x: eye - x, m)
    pw = _each(lambda x: _bdot(x, x), m)
    for it in range(5):
        tm_ = _each(lambda x, p: x + _bdot(x, p), tm_, pw)
        if it < 4:
            pw = _each(lambda p: _bdot(p, p), pw)
    eg = _each(jnp.exp, gci)
    vb = _each(jnp.multiply, v, beta)
    kbe = _each(jnp.multiply, kb, eg)
    uw = _each(lambda x, a, b: _bdot(x, jnp.concatenate([a, b], axis=1)), tm_, vb, kbe)
    attn = _each(lambda x, g: jnp.where(incl, x * g, 0.0), qk, gamma)
    gl_a = _each(lambda g: g[CHUNK - 1:CHUNK, :], gci)
    gl_b = _each(lambda g: g[PAIR - 1:PAIR, :], gci)
    ek = _each(lambda a, b, g: jnp.exp(jnp.where(first, a, b) - g), gl_a, gl_b, gci)
    return dict(incl=incl, strict=strict, gamma=gamma, kb=kb, m=m, tm=tm_, eg=eg, vb=vb, kbe=kbe,
                u=_each(lambda x: x[:, :LANES], uw), w=_each(lambda x: x[:, LANES:], uw), attn=attn,
                qd=_each(jnp.multiply, q, eg), ek=ek, kd=_each(jnp.multiply, k, ek),
                glast_a=_each(jnp.exp, gl_a), glast_b=_each(jnp.exp, gl_b))


def _gdn_specs(t, ts, order):
    nc = ts // CHUNK
    blk = pl.BlockSpec((ts, GDN_HP * LANES), lambda h, s: (order(s), h))
    row = pl.BlockSpec((GDN_HP, 1, ts), lambda h, s: (h, 0, order(s)))
    st = pl.BlockSpec((GDN_HP, nc, LANES, LANES), lambda h, s: (h, order(s), 0, 0))
    return blk, row, st


def _gdn_fwd(q, k, v, gcb, gct, bb, *, name):
    t = q.shape[0]
    ts = _tile(t, GDN_TILE)
    npair = ts // PAIR

    def body(q_ref, k_ref, v_ref, g_ref, gt_ref, b_ref, o_ref, st_ref, s_sc):
        @pl.when(pl.program_id(1) == 0)
        def _():
            s_sc[...] = jnp.zeros_like(s_sc)

        def pair(pi, _):
            rows = pl.ds(pl.multiple_of(pi * PAIR, PAIR), PAIR)
            heads = [slice(hh * LANES, (hh + 1) * LANES) for hh in range(GDN_HP)]
            c = CHUNK
            cat0 = lambda *xs: jnp.concatenate(xs, axis=0)
            s0 = [s_sc[hh] for hh in range(GDN_HP)]
            cm = _pair_common([q_ref[rows, sl] for sl in heads], [k_ref[rows, sl] for sl in heads],
                              [v_ref[rows, sl] for sl in heads], [g_ref[rows, sl] for sl in heads],
                              [gt_ref[hh, :, rows] for hh in range(GDN_HP)], [b_ref[rows, sl] for sl in heads])
            u, w, qd, kd = cm["u"], cm["w"], cm["qd"], cm["kd"]
            r0 = _each(lambda w_, q_, s: _bdot(cat0(w_[:c], q_[:c]), s), w, qd, s0)
            vn_a = _each(lambda u_, r: u_[:c] - r[:c], u, r0)
            s1 = _each(lambda s, gl, k_, vn: s * gl + _bdot(k_[:c], vn, TN), s0, cm["glast_a"], kd, vn_a)
            r1 = _each(lambda w_, q_, s: _bdot(cat0(w_[c:], q_[c:]), s), w, qd, s1)
            vn_b = _each(lambda u_, r: u_[c:] - r[:c], u, r1)
            s2 = _each(lambda s, gl, k_, vn: s * gl + _bdot(k_[c:], vn, TN), s1, cm["glast_b"], kd, vn_b)
            o = _each(lambda ra, rb, at, va, vb_: cat0(ra[c:], rb[c:]) + _bdot(at, cat0(va, vb_)),
                      r0, r1, cm["attn"], vn_a, vn_b)
            for hh, sl in enumerate(heads):
                st_ref[hh, 2 * pi] = s0[hh]
                st_ref[hh, 2 * pi + 1] = s1[hh]
                s_sc[hh] = s2[hh]
                o_ref[rows, sl] = o[hh]
            return 0

        lax.fori_loop(0, npair, pair, 0)

    blk, row, st = _gdn_specs(t, ts, lambda s: s)
    return pl.pallas_call(
        body, name=name, grid=(GDN_HEADS // GDN_HP, t // ts), in_specs=[blk, blk, blk, blk, row, blk],
        out_specs=[blk, st],
        out_shape=[jax.ShapeDtypeStruct((t, 1024), F32), jax.ShapeDtypeStruct((GDN_HEADS, t // CHUNK, LANES, LANES), F32)],
        scratch_shapes=[pltpu.VMEM((GDN_HP, LANES, LANES), F32)],
        compiler_params=_params(("parallel", "arbitrary")),
    )(q, k, v, gcb, gct, bb)


def _gdn_bwd(q, k, v, gcb, gct, bb, do, states, *, name):
    t = q.shape[0]
    ts = _tile(t, GDN_TILE)
    npair = ts // PAIR
    ns = t // ts
    c = CHUNK

    def body(q_ref, k_ref, v_ref, g_ref, gt_ref, b_ref, do_ref, st_ref, dq_ref, dk_ref, dv_ref, dg_ref, db_ref, ds_sc):
        @pl.when(pl.program_id(1) == 0)
        def _():
            ds_sc[...] = jnp.zeros_like(ds_sc)

        rowsum = lambda x: jnp.sum(x, axis=-1, keepdims=True)
        total = lambda x: jnp.sum(rowsum(x), axis=0, keepdims=True)
        cat0 = lambda *xs: jnp.concatenate(xs, axis=0)
        cat1 = lambda *xs: jnp.concatenate(xs, axis=1)

        def pair(step, _):
            pi = npair - 1 - step
            rows = pl.ds(pl.multiple_of(pi * PAIR, PAIR), PAIR)
            heads = [slice(hh * LANES, (hh + 1) * LANES) for hh in range(GDN_HP)]
            hs = range(GDN_HP)
            qv, kv, vv = ([r[rows, sl] for sl in heads] for r in (q_ref, k_ref, v_ref))
            beta = [b_ref[rows, sl] for sl in heads]
            dov = [do_ref[rows, sl] for sl in heads]
            s0 = [st_ref[hh, 2 * pi] for hh in hs]
            s1 = [st_ref[hh, 2 * pi + 1] for hh in hs]
            ds2 = [ds_sc[hh] for hh in hs]
            cm = _pair_common(qv, kv, vv, [g_ref[rows, sl] for sl in heads], [gt_ref[hh, :, rows] for hh in hs], beta)
            u, w, qd, kd, attn = cm["u"], cm["w"], cm["qd"], cm["kd"], cm["attn"]
            tmat, gamma, eg = cm["tm"], cm["gamma"], cm["eg"]
            incl, strict = cm["incl"], cm["strict"]
            vn_a = _each(lambda u_, w_, s: u_[:c] - _bdot(w_[:c], s), u, w, s0)
            vn_b = _each(lambda u_, w_, s: u_[c:] - _bdot(w_[c:], s), u, w, s1)
            vn = _each(cat0, vn_a, vn_b)
            dvn_att = _each(lambda a, d: _bdot(a, d, TN), attn, dov)
            dattn = _each(lambda d, v_: jnp.where(incl, _bdot(d, v_, NT), 0.0), dov, vn)
            dvn_b = _each(lambda x, k_, d: x[c:] + _bdot(k_[c:], d), dvn_att, kd, ds2)
            rb = _each(lambda d, x, s: _bdot(cat0(d[c:], x), s, NT), dov, dvn_b, s1)
            dkd_b = _each(lambda v_, d: _bdot(v_, d, NT), vn_b, ds2)
            dgl_b = _each(lambda d, s: total(d * s), ds2, s1)
            ds1 = _each(lambda d, gl, q_, w_, o_, x: d * gl + _bdot(cat0(q_[c:], w_[c:]), cat0(o_[c:], -x), TN),
                        ds2, cm["glast_b"], qd, w, dov, dvn_b)
            dvn_a = _each(lambda x, k_, d: x[:c] + _bdot(k_[:c], d), dvn_att, kd, ds1)
            ra = _each(lambda d, x, s: _bdot(cat0(d[:c], x), s, NT), dov, dvn_a, s0)
            dkd_a = _each(lambda v_, d: _bdot(v_, d, NT), vn_a, ds1)
            dgl_a = _each(lambda d, s: total(d * s), ds1, s0)
            ds0 = _each(lambda d, gl, q_, w_, o_, x: d * gl + _bdot(cat0(q_[:c], w_[:c]), cat0(o_[:c], -x), TN),
                        ds1, cm["glast_a"], qd, w, dov, dvn_a)
            dvn = _each(cat0, dvn_a, dvn_b)
            dqd = _each(lambda a, b: cat0(a[:c], b[:c]), ra, rb)
            dw = _each(lambda a, b: -cat0(a[c:], b[c:]), ra, rb)
            dkd = _each(cat0, dkd_a, dkd_b)
            dvw = _each(cat1, dvn, dw)
            dvbk = _each(lambda t_, x: _bdot(t_, x, TN), tmat, dvw)
            dvb = _each(lambda x: x[:, :LANES], dvbk)
            dkbe = _each(lambda x: x[:, LANES:], dvbk)
            dt_ = _each(lambda x, a, b: _bdot(x, cat1(a, b), NT), dvw, cm["vb"], cm["kbe"])
            da1 = _each(lambda t_, x: _bdot(t_, x, TN), tmat, dt_)
            dm = _each(lambda x, t_: jnp.where(strict, -_bdot(x, t_, NT), 0.0), da1, tmat)
            dkk = _each(jnp.multiply, dm, gamma)
            dqk = _each(jnp.multiply, dattn, gamma)
            z = _each(lambda a, b, c_, d: a * b + c_ * d, dm, cm["m"], dattn, attn)
            dkb = _each(lambda x, k_, y, e: _bdot(x, k_) + y * e, dkk, kv, dkbe, eg)
            dk = _each(lambda a, b, kb_, q_, x, e, y, be: _bdot(cat0(a, b), cat0(kb_, q_), TN) + x * e + y * be,
                       dkk, dqk, cm["kb"], qv, dkd, cm["ek"], dkb, beta)
            dq = _each(lambda x, k_, y, e: _bdot(x, k_) + y * e, dqk, kv, dqd, eg)

            def colsum_of(z_):
                zh = z_.astype(BF16)
                zl = (z_ - zh.astype(F32)).astype(BF16)
                return _dot(cat0(zh, zl), jnp.ones((2 * PAIR, LANES), BF16), TN)

            colsum = _each(colsum_of, z)
            ri = lax.broadcasted_iota(jnp.int32, (PAIR, LANES), 0)
            for hh, sl in enumerate(heads):
                dkd_kd = dkd[hh] * kd[hh]
                dgc = (rowsum(z[hh]) - colsum[hh] + rowsum(dqd[hh] * qd[hh]) - rowsum(dkd_kd)
                       + rowsum(dkbe[hh] * cm["kbe"][hh]))
                last_a = total(dkd_kd[:c]) + dgl_a[hh] * cm["glast_a"][hh]
                last_b = total(dkd_kd[c:]) + dgl_b[hh] * cm["glast_b"][hh]
                dgc = dgc + jnp.where(ri == c - 1, last_a, 0.0) + jnp.where(ri == PAIR - 1, last_b, 0.0)
                ds_sc[hh] = ds0[hh]
                dq_ref[rows, sl] = dq[hh]
                dk_ref[rows, sl] = dk[hh]
                dv_ref[rows, sl] = dvb[hh] * beta[hh]
                db_ref[rows, sl] = jnp.broadcast_to(rowsum(dkb[hh] * kv[hh]) + rowsum(dvb[hh] * vv[hh]), (PAIR, LANES))
                dg_ref[rows, sl] = dgc
            return 0

        lax.fori_loop(0, npair, pair, 0)

    blk, row, st = _gdn_specs(t, ts, lambda s: ns - 1 - s)
    out = jax.ShapeDtypeStruct((t, 1024), F32)
    return pl.pallas_call(
        body, name=name, grid=(GDN_HEADS // GDN_HP, ns), in_specs=[blk, blk, blk, blk, row, blk, blk, st],
        out_specs=[blk] * 5, out_shape=[out] * 5, scratch_shapes=[pltpu.VMEM((GDN_HP, LANES, LANES), F32)],
        compiler_params=_params(("parallel", "arbitrary")),
    )(q, k, v, gcb, gct, bb, do, states)


def _gate_out_proj_loss(o, proj_c, o_norm, w, hres, g, target, *, name):
    t, d = hres.shape
    tm = _tile(t, 2 * ROW_TILE)

    def body(o_ref, z_ref, on_ref, w_ref, h_ref, g_ref, t_ref, dh_ref, dhb_ref, dg_ref, loss_ref, dw_ref, y_ref):
        i = pl.program_id(0)
        for hd in range(GDN_HEADS):
            sl = slice(hd * LANES, (hd + 1) * LANES)
            ov = o_ref[:, sl]
            rr = lax.rsqrt(jnp.mean(ov * ov, axis=-1, keepdims=True) + RMS_EPS)
            z = z_ref[:, sl]
            y_ref[:, sl] = (ov * rr * on_ref[...] * (z * _sigmoid(z))).astype(BF16)
        x = h_ref[...] + _dot(y_ref[...], w_ref[...])
        r = lax.rsqrt(jnp.mean(x * x, axis=-1, keepdims=True) + RMS_EPS)
        xh = x * r
        err = xh * g_ref[...] - t_ref[...]
        dy = err * (1.0 / d)
        dxh = dy * g_ref[...]
        dh = r * (dxh - xh * jnp.mean(dxh * xh, axis=-1, keepdims=True))
        dh_ref[...] = dh
        dhb_ref[...] = dh.astype(BF16)

        @pl.when(i == 0)
        def _():
            dg_ref[...] = jnp.zeros_like(dg_ref)
            loss_ref[...] = jnp.zeros_like(loss_ref)
            dw_ref[...] = jnp.zeros_like(dw_ref)

        dg_ref[...] += jnp.sum(dy * xh, axis=0, keepdims=True)
        part = 0.5 * jnp.sum(jnp.mean(err * err, axis=-1, keepdims=True), axis=0, keepdims=True)
        loss_ref[...] += jnp.broadcast_to(part, loss_ref.shape)
        dw_ref[...] += _dot(y_ref[...], dhb_ref[...], TN)

    row = pl.BlockSpec((tm, d), lambda i: (i, 0))
    vec = pl.BlockSpec((1, d), lambda i: (0, 0))
    return pl.pallas_call(
        body, name=name, grid=(t // tm,),
        in_specs=[row, pl.BlockSpec((tm, 1024), lambda i: (i, 3)), pl.BlockSpec((1, LANES), lambda i: (0, 0)),
                  pl.BlockSpec(w.shape, lambda i: (0, 0)), row, vec, row],
        out_specs=[row, row, vec, pl.BlockSpec((8, LANES), lambda i: (0, 0)), pl.BlockSpec(w.shape, lambda i: (0, 0))],
        out_shape=[jax.ShapeDtypeStruct((t, d), F32), jax.ShapeDtypeStruct((t, d), BF16),
                   jax.ShapeDtypeStruct((1, d), F32), jax.ShapeDtypeStruct((8, LANES), F32),
                   jax.ShapeDtypeStruct(w.shape, F32)],
        scratch_shapes=[pltpu.VMEM((tm, d), BF16)],
        compiler_params=_params(("arbitrary",)),
    )(o, proj_c, o_norm, w, hres, g, target)


def _pad_cols(w, n):
    return jnp.pad(w, ((0, 0), (0, n - w.shape[1])))


def _layout_odd(w):
    return dict(
        winc=_pad_cols(w["w_in_c"], IN_C_PAD).astype(BF16), wout_c=w["w_out_c"].astype(BF16), conv_w=w["conv_w"],
        a_log=_pad_cols(w["a_log"], LANES), dt_bias=_pad_cols(w["dt_bias"], LANES),
        norm_c=w["norm_c"], o_norm=w["o_norm"], final_norm=w["final_norm"],
    )


def _layout_in_ab(w):
    z = lambda r, c: jnp.zeros((r, c), w["w_in_ab"].dtype)
    wi = w["w_in_ab"]
    win = jnp.concatenate([wi[:, :384], z(1024, 64), wi[:, 384:416], z(1024, 32), wi[:, 416:]], axis=1)
    pw = w["pool_w"]
    rows = []
    for g in range(4):
        rows.append(jnp.concatenate([pw[g] if j == g else jnp.zeros((128, 128), F32) for j in range(4)], axis=1))
    wpool = jnp.concatenate(rows, axis=0)
    half = MLA_ROPE // 2
    inv = 1.0 / (ROPE_THETA ** (jnp.arange(half, dtype=F32) / half))
    inv_lane = jnp.concatenate([jnp.zeros((MLA_NOPE,), F32), inv, inv, jnp.zeros((32,), F32)]).reshape(1, LANES)
    return dict(win=win.astype(BF16), wpool=wpool.astype(BF16), inv_lane=inv_lane, norm_ab=w["norm_ab"],
                q_a_norm=w["q_a_norm"], kv_a_norm=w["kv_a_norm"], pool_scale=w["pool_scale"])


def _layout_mid(w):
    wq = jnp.pad(w["w_q_b"].reshape(MLA_Q_RANK, MLA_HEADS, 96), ((0, 0), (0, 0), (0, 32))).reshape(MLA_Q_RANK, 1024)
    kv3 = w["w_kv_b"].reshape(MLA_KV_RANK, MLA_HEADS, 128)
    wk = jnp.pad(kv3[..., :MLA_NOPE], ((0, 0), (0, 0), (0, 64))).reshape(MLA_KV_RANK, 1024)
    wv = kv3[..., MLA_NOPE:].reshape(MLA_KV_RANK, 512)
    return dict(wq=wq.astype(BF16), wk=wk.astype(BF16), wv=wv.astype(BF16), wout_ab=w["w_out_ab"].astype(BF16))


def _unlayout_grads(g, names):
    out = {}
    for name in names:
        if name == "w_in_ab":
            dwin = g["win"]
            out[name] = jnp.concatenate([dwin[:384], dwin[448:480], dwin[512:]], axis=0)
        elif name == "w_q_b":
            out[name] = g["wq"].reshape(MLA_Q_RANK, MLA_HEADS, 128)[..., :96].reshape(MLA_Q_RANK, 768)
        elif name == "w_kv_b":
            out[name] = jnp.concatenate([g["wk"].reshape(MLA_KV_RANK, MLA_HEADS, 128)[..., :MLA_NOPE],
                                         g["wv"].reshape(MLA_KV_RANK, MLA_HEADS, MLA_V)], axis=-1).reshape(MLA_KV_RANK, 1024)
        elif name == "w_in_c":
            out[name] = g["winc"][:4112]
        else:
            out[name] = g[{"w_out_ab": "wout_ab", "w_out_c": "wout_c"}[name]]
    return out


def _local_step(x, pos, target, lw, more_weights=None, on_grads=None):
    mm = _matmul
    proj, hn = _rms_in_proj(x, lw["norm_ab"], lw["win"], name="rms_in_ab")
    if more_weights is not None:
        lw = {**lw, **more_weights("mid", proj)}
    q, k, v, ybraw, qn, kvn, d, cos_t, sin_t = _ab_prep(
        proj, pos, lw["inv_lane"], lw["q_a_norm"], lw["kv_a_norm"], lw["wq"], lw["wk"], lw["wv"], lw["wpool"], name="ab_prep")
    o, lse = _attn_fwd(q, k, v, name="attn_fwd")
    h1, y = _gate_out_proj(o, ybraw, proj, lw["pool_scale"], lw["wout_ab"], x, name="gate_out_ab")
    lo = lw if more_weights is None else more_weights("odd", h1)
    proj_c, hn1 = _rms_in_proj(h1, lo["norm_c"], lo["winc"], name="rms_in_c")
    q2, k2, v2, gb, bb, gt = _c_prep(proj_c, lo["conv_w"], lo["a_log"], lo["dt_bias"], name="c_prep")
    gt = gt.reshape(GDN_HEADS, 1, gt.shape[1])
    o2, states = _gdn_fwd(q2, k2, v2, gb, gt, bb, name="gdn_fwd")
    dh2, dh2b, d_final, loss, d_wout_c = _gate_out_proj_loss(
        o2, proj_c, lo["o_norm"], lo["wout_c"], h1, lo["final_norm"], target, name="gate_out_c_loss")
    g = {"final_norm": d_final, "wout_c": d_wout_c, "loss": loss}
    do2, dz2, g["o_norm"] = _o_gate_bwd(dh2b, lo["wout_c"], o2, proj_c, lo["o_norm"], name="gate_c_bwd")
    dq2, dk2, dv2, dgb, dbb = _gdn_bwd(q2, k2, v2, gb, gt, bb, do2, states, name="gdn_bwd")
    dproj_c, g["conv_w"], g["a_log"], g["dt_bias"] = _c_prep_bwd(
        proj_c, lo["conv_w"], lo["a_log"], lo["dt_bias"], dq2, dk2, dv2, dgb, dbb, dz2, name="c_prep_bwd")
    g["winc"] = mm(dproj_c, hn1, "tn", name="in_c_dw")
    notify = (lambda tag, after=None: 0.0) if on_grads is None else (lambda tag, after=None: on_grads(tag, g, after))
    dh1, g["norm_c"], dh1b, g["wout_ab"] = _matmul_rms_bwd(
        dproj_c, lo["winc"], h1, lo["norm_c"] + notify("odd"), dh2, name="in_c_dx_rms", prev_y=y)
    pool_scale = lw["pool_scale"] + notify("odd_go", dh1b) + notify("out_ab")
    do, delta, dyb, dz, g["pool_scale"] = _gate_bwd(dh1b, lw["wout_ab"], o, ybraw, proj, pool_scale, name="gate_ab_bwd")
    dq, dk, dv = _attn_bwd(q, k, v, do, lse, delta, name="attn_bwd")
    dproj, g["q_a_norm"], g["kv_a_norm"], g["wq"], g["wk"], g["wv"], g["wpool"], g["win"] = _ab_prep_bwd(
        proj, lw["q_a_norm"], lw["kv_a_norm"], dq, dk, dv, cos_t, sin_t, dyb, dz, qn, kvn, d, hn,
        lw["wq"], lw["wk"], lw["wv"], lw["wpool"], name="ab_prep_bwd")
    norm_ab = lw["norm_ab"] + notify("in_ab")
    dx, g["norm_ab"] = _matmul_rms_bwd(dproj, lw["win"], x, norm_ab, dh1, name="in_ab_dx_rms")
    return loss, dx, g


_HBM = pl.BlockSpec(memory_space=pltpu.HBM)


def _place():
    return lax.axis_index("x"), lax.axis_index("y"), lax.axis_index("c")


def _flip(v, f):
    return 1 - v if f else v


_CHIP_FLIPS = ((1, 0), (0, 1), (1, 1))
_DEV_FLIPS = tuple((fx, fy, fc) for fx in (0, 1) for fy in (0, 1) for fc in (0, 1) if fx or fy or fc)


def _rcopy(src, dst, send_sems, recv_sems, k, to):
    return pltpu.make_async_remote_copy(src_ref=src, dst_ref=dst, send_sem=send_sems.at[k], recv_sem=recv_sems.at[k],
                                        device_id=to, device_id_type=MESH)


def _my_half(ref, c, axis):
    rh = ref.shape[axis] // 2
    idx = [slice(None)] * len(ref.shape)
    idx[axis] = pl.ds(c * rh, rh)
    return ref.at[tuple(idx)]


def _gather_weights(bigs, smalls):
    nb, ns = len(bigs), len(smalls)

    def body(*refs):
        ins, outs = refs[:nb + ns], refs[nb + ns:2 * (nb + ns)]
        send_sems, recv_sems, local_sems = refs[2 * (nb + ns):]
        x, y, c = _place()
        j0 = 2 * x + y
        sib = (x, y, 1 - c)
        chips = [(_flip(x, fx), _flip(y, fy)) for fx, fy in _CHIP_FLIPS]
        local = [pltpu.make_async_copy(i_ref, o_ref.at[j0], local_sems.at[a])
                 for a, (i_ref, o_ref) in enumerate(zip(ins, outs))]
        for cp in local:
            cp.start()
        sends = []
        for k, (px, py) in enumerate(chips):
            for a in range(nb):
                sends.append(_rcopy(_my_half(ins[a], c, 0), _my_half(outs[a].at[j0], c, 0), send_sems, recv_sems,
                                    6 * a + k, (px, py, c)))
            for s in range(ns):
                sends.append(_rcopy(ins[nb + s], outs[nb + s].at[j0], send_sems, recv_sems, 6 * nb + 3 * s + k, (px, py, c)))
        for cp in sends:
            cp.start()
        for k, (px, py) in enumerate(chips):
            jk = 2 * px + py
            for a in range(nb):
                landed = _my_half(outs[a].at[jk], c, 0)
                _rcopy(landed, landed, send_sems, recv_sems, 6 * a + k, (px, py, c)).wait_recv()
                fwd = _rcopy(landed, landed, send_sems, recv_sems, 6 * a + 3 + k, sib)
                fwd.start()
                sends.append(fwd)
        for k, (px, py) in enumerate(chips):
            jk = 2 * px + py
            for a in range(nb):
                other = _my_half(outs[a].at[jk], 1 - c, 0)
                _rcopy(other, other, send_sems, recv_sems, 6 * a + 3 + k, sib).wait_recv()
            for s in range(ns):
                _rcopy(ins[nb + s], outs[nb + s].at[jk], send_sems, recv_sems, 6 * nb + 3 * s + k, (px, py, c)).wait_recv()
        for cp in sends:
            cp.wait_send()
        for cp in local:
            cp.wait()

    arrays = list(bigs) + list(smalls)
    n_sem = 6 * nb + 3 * ns
    return pl.pallas_call(
        body, name="gather_weights", in_specs=[_HBM] * len(arrays), out_specs=[_HBM] * len(arrays),
        out_shape=[jax.ShapeDtypeStruct((4,) + a.shape, a.dtype) for a in arrays],
        scratch_shapes=[pltpu.SemaphoreType.DMA((n_sem,)), pltpu.SemaphoreType.DMA((n_sem,)),
                        pltpu.SemaphoreType.DMA((len(arrays),))],
    )(*arrays)


def _core_swap_partial(gs, by_cols, *, name):
    n = len(gs)

    def body(*refs):
        ins, outs = refs[:n], refs[n:2 * n]
        send_sems, recv_sems = refs[2 * n:]
        x, y, c = _place()
        copies = [_rcopy(_my_half(i_ref, 1 - c, 2 if by_cols[a] else 1), o_ref, send_sems, recv_sems, a, (x, y, 1 - c))
                  for a, (i_ref, o_ref) in enumerate(zip(ins, outs))]
        for cp in copies:
            cp.start()
        for cp in copies:
            cp.wait()

    halved = lambda g, cols: (4, g.shape[1], g.shape[2] // 2) if cols else (4, g.shape[1] // 2, g.shape[2])
    return pl.pallas_call(
        body, name=name, in_specs=[_HBM] * n, out_specs=[_HBM] * n,
        out_shape=[jax.ShapeDtypeStruct(halved(g, cols), g.dtype) for g, cols in zip(gs, by_cols)],
        scratch_shapes=[pltpu.SemaphoreType.DMA((n,)), pltpu.SemaphoreType.DMA((n,))],
    )(*gs)


def _core_swap_partial_start(gs, by_cols, *, name):
    n = len(gs)
    halved = lambda g, cols: (4, g.shape[1], g.shape[2] // 2) if cols else (4, g.shape[1] // 2, g.shape[2])
    lands = [lax.empty(halved(g, cols), g.dtype) for g, cols in zip(gs, by_cols)]

    def body(*refs):
        ins, land_refs, send_sems, recv_sems, token = refs[:n], refs[n:2 * n], refs[2 * n], refs[2 * n + 1], refs[-1]
        x, y, c = _place()
        for a in range(n):
            _rcopy(_my_half(ins[a], 1 - c, 2 if by_cols[a] else 1), land_refs[a], send_sems, recv_sems, a,
                   (x, y, 1 - c)).start()
        token[...] = jnp.zeros_like(token)

    held = [pltpu.with_memory_space_constraint(a, pltpu.HBM) for a in list(gs) + lands]
    return pl.pallas_call(
        body, name=name, in_specs=[_HBM] * (2 * n),
        out_specs=(_SEM, _SEM, *[_HBM] * (2 * n), pl.BlockSpec(memory_space=pltpu.VMEM)),
        out_shape=(pltpu.SemaphoreType.DMA((n,)), pltpu.SemaphoreType.DMA((n,)),
                   *[pltpu.HBM(a.shape, a.dtype) for a in held], jax.ShapeDtypeStruct((8, LANES), F32)),
        input_output_aliases={i: 2 + i for i in range(2 * n)},
        compiler_params=pltpu.CompilerParams(has_side_effects=_DATAFLOW),
    )(*held)


def _core_swap_partial_wait(started, after, by_cols, *, name):
    send_sems, recv_sems, held = started[0], started[1], started[2:-1]
    n = len(held) // 2

    def body(*refs):
        ins, land_refs, s_sems, r_sems = refs[:n], refs[n:2 * n], refs[2 * n], refs[2 * n + 1]
        x, y, c = _place()
        for a in range(n):
            cp = _rcopy(_my_half(ins[a], 1 - c, 2 if by_cols[a] else 1), land_refs[a], s_sems, r_sems, a, (x, y, 1 - c))
            cp.wait_send()
            cp.wait_recv()

    out = pl.pallas_call(
        body, name=name, in_specs=[_HBM] * (2 * n) + [_SEM, _SEM, _ANY], out_specs=[_HBM] * (2 * n),
        out_shape=[pltpu.HBM(a.shape, a.dtype) for a in held],
        input_output_aliases={i: i for i in range(2 * n)},
        compiler_params=pltpu.CompilerParams(has_side_effects=_DATAFLOW),
    )(*held, send_sems, recv_sems, after)
    return out[:n], out[n:]


def _core_swap_sum(fs, by_cols):
    n = len(fs)

    def body(*refs):
        ins, outs = refs[:n], refs[n:2 * n]
        send_sems, recv_sems = refs[2 * n:]
        x, y, c = _place()
        axes = [1 if cols else 0 for cols in by_cols]
        copies = [_rcopy(_my_half(i_ref, c, ax), _my_half(o_ref, c, ax), send_sems, recv_sems, a, (x, y, 1 - c))
                  for a, (i_ref, o_ref, ax) in enumerate(zip(ins, outs, axes))]
        for cp in copies:
            cp.start()
        for a, cp in enumerate(copies):
            cp.wait_send()
            theirs = _my_half(outs[a], 1 - c, axes[a])
            _rcopy(theirs, theirs, send_sems, recv_sems, a, (x, y, 1 - c)).wait_recv()

    return pl.pallas_call(
        body, name="core_swap_sum", in_specs=[_HBM] * n, out_specs=[_HBM] * n,
        out_shape=[jax.ShapeDtypeStruct(f.shape, f.dtype) for f in fs],
        input_output_aliases={a: a for a in range(n)},
        scratch_shapes=[pltpu.SemaphoreType.DMA((n,)), pltpu.SemaphoreType.DMA((n,))],
    )(*fs)


_SEM = pl.BlockSpec(memory_space=pltpu.SEMAPHORE)
_ANY = pl.BlockSpec(memory_space=pl.ANY)
_DATAFLOW = pltpu.SideEffectType.DATAFLOW_SIDE_EFFECTING


def _peer_flips(to_all):
    return _DEV_FLIPS if to_all else tuple((fx, fy, 0) for fx, fy in _CHIP_FLIPS)


def _to_chips_copies(srcs, lands, send_sems, recv_sems, per_chip_slot, to_all=False):
    x, y, c = _place()
    flips = _peer_flips(to_all)
    index = (lambda px, py, pc: 4 * px + 2 * py + pc) if to_all else (lambda px, py, pc: 2 * px + py)
    me = index(x, y, c)
    out = []
    for k, (fx, fy, fc) in enumerate(flips):
        peer = (_flip(x, fx), _flip(y, fy), _flip(c, fc))
        theirs = index(*peer)
        for a, (src, land) in enumerate(zip(srcs, lands)):
            piece = src.at[theirs] if per_chip_slot else src
            out.append((_rcopy(piece, land.at[me], send_sems, recv_sems, len(flips) * a + k, peer),
                        _rcopy(piece, land.at[theirs], send_sems, recv_sems, len(flips) * a + k, peer)))
    return out


def _to_chips_start(arrays, *, per_chip_slot, name, after=None, to_all=False):
    n = len(arrays)
    peers = len(_peer_flips(to_all))
    lands = [lax.empty((peers + 1,) + (a.shape[1:] if per_chip_slot else a.shape), a.dtype) for a in arrays]
    extra = [] if after is None else [after]

    def body(*refs):
        srcs, land_refs, token = refs[:n], refs[n:2 * n], refs[-1]
        send_sems, recv_sems = refs[2 * n + len(extra)], refs[2 * n + len(extra) + 1]
        for send, _ in _to_chips_copies(srcs, land_refs, send_sems, recv_sems, per_chip_slot, to_all):
            send.start()
        token[...] = jnp.zeros_like(token)

    held = [pltpu.with_memory_space_constraint(a, pltpu.HBM) for a in list(arrays) + lands]
    return pl.pallas_call(
        body, name=name, in_specs=[_HBM] * (2 * n) + [_ANY] * len(extra),
        out_specs=(_SEM, _SEM, *[_HBM] * (2 * n), pl.BlockSpec(memory_space=pltpu.VMEM)),
        out_shape=(pltpu.SemaphoreType.DMA((peers * n,)), pltpu.SemaphoreType.DMA((peers * n,)),
                   *[pltpu.HBM(a.shape, a.dtype) for a in held], jax.ShapeDtypeStruct((8, LANES), F32)),
        input_output_aliases={i: 2 + i for i in range(2 * n)},
        compiler_params=pltpu.CompilerParams(has_side_effects=_DATAFLOW),
    )(*held, *extra)


def _to_chips_wait(started, after, *, per_chip_slot, name, to_all=False):
    send_sems, recv_sems, held = started[0], started[1], started[2:-1]
    n = len(held) // 2

    def body(*refs):
        srcs, land_refs, s_sems, r_sems = refs[:n], refs[n:2 * n], refs[2 * n], refs[2 * n + 1]
        for send, arrival in _to_chips_copies(srcs, land_refs, s_sems, r_sems, per_chip_slot, to_all):
            send.wait_send()
            arrival.wait_recv()

    out = pl.pallas_call(
        body, name=name, in_specs=[_HBM] * (2 * n) + [_SEM, _SEM, _ANY], out_specs=[_HBM] * (2 * n),
        out_shape=[pltpu.HBM(a.shape, a.dtype) for a in held],
        input_output_aliases={i: i for i in range(2 * n)},
        compiler_params=pltpu.CompilerParams(has_side_effects=_DATAFLOW),
    )(*held, send_sems, recv_sems, after)
    return out[n:]


def _chip_exchange(ps, small):
    n = len(ps)
    rs = small.shape[0]

    def body(*refs):
        p_refs, s_ref = refs[:n], refs[n]
        l_refs, ls_ref = refs[n + 1:2 * n + 1], refs[2 * n + 1]
        send_sems, recv_sems, local_sems = refs[2 * n + 2:]
        x, y, c = _place()
        j0 = 2 * x + y
        d0 = 2 * j0 + c
        local = [pltpu.make_async_copy(p.at[j0], l.at[j0], local_sems.at[a]) for a, (p, l) in enumerate(zip(p_refs, l_refs))]
        local.append(pltpu.make_async_copy(s_ref, ls_ref.at[d0], local_sems.at[n]))
        for cp in local:
            cp.start()
        sends = []
        for k, (fx, fy) in enumerate(_CHIP_FLIPS):
            px, py = _flip(x, fx), _flip(y, fy)
            for a in range(n):
                sends.append(_rcopy(p_refs[a].at[2 * px + py], l_refs[a].at[j0], send_sems, recv_sems, 3 * a + k, (px, py, c)))
        for k, (fx, fy, fc) in enumerate(_DEV_FLIPS):
            peer = (_flip(x, fx), _flip(y, fy), _flip(c, fc))
            sends.append(_rcopy(s_ref, ls_ref.at[d0], send_sems, recv_sems, 3 * n + k, peer))
        for cp in sends:
            cp.start()
        for k, (fx, fy) in enumerate(_CHIP_FLIPS):
            px, py = _flip(x, fx), _flip(y, fy)
            for a in range(n):
                _rcopy(p_refs[a].at[j0], l_refs[a].at[2 * px + py], send_sems, recv_sems, 3 * a + k, (px, py, c)).wait_recv()
        for k, (fx, fy, fc) in enumerate(_DEV_FLIPS):
            px, py, pc = _flip(x, fx), _flip(y, fy), _flip(c, fc)
            _rcopy(s_ref, ls_ref.at[4 * px + 2 * py + pc], send_sems, recv_sems, 3 * n + k, (px, py, pc)).wait_recv()
        for cp in sends:
            cp.wait_send()
        for cp in local:
            cp.wait()

    n_sem = 3 * n + 7
    return pl.pallas_call(
        body, name="chip_exchange", in_specs=[_HBM] * (n + 1), out_specs=[_HBM] * (n + 1),
        out_shape=[jax.ShapeDtypeStruct(p.shape, F32) for p in ps] + [jax.ShapeDtypeStruct((8, rs, LANES), F32)],
        scratch_shapes=[pltpu.SemaphoreType.DMA((n_sem,)), pltpu.SemaphoreType.DMA((n_sem,)),
                        pltpu.SemaphoreType.DMA((n + 1,))],
    )(*ps, small)


def _half_blocks(rows, cols, by_cols):
    if by_cols:
        tc = _tile(cols // 2, 256)
        nb = cols // 2 // tc
        return rows, tc, nb, (lambda i, c: (0, c * nb + i))
    tr = _tile(rows // 2, 256)
    nb = rows // 2 // tr
    return tr, cols, nb, (lambda i, c: (c * nb + i, 0))


def _core_sum(g, part, core, *, name, by_cols):
    _, rows, cols = g.shape
    br, bc, nb, whole = _half_blocks(rows, cols, by_cols)
    mine = (lambda i: (0, i)) if by_cols else (lambda i: (i, 0))

    def body(c_ref, g_ref, p_ref, o_ref):
        o_ref[...] = g_ref[...] + p_ref[...]

    grid_spec = pltpu.PrefetchScalarGridSpec(
        num_scalar_prefetch=1, grid=(4, nb),
        in_specs=[pl.BlockSpec((1, br, bc), lambda j, i, c: (j,) + whole(i, c[0])),
                  pl.BlockSpec((1, br, bc), lambda j, i, c: (j,) + mine(i))],
        out_specs=pl.BlockSpec((1, br, bc), lambda j, i, c: (j,) + mine(i)),
    )
    return pl.pallas_call(
        body, name=name, grid_spec=grid_spec, out_shape=jax.ShapeDtypeStruct(part.shape, F32),
        compiler_params=_params(("parallel", "parallel")),
    )(core, g, part)


def _chip_sum(landed, part, place, *, name, by_cols):
    _, hr, hc = landed.shape
    rows, cols = (hr, 2 * hc) if by_cols else (2 * hr, hc)
    br, bc, nb, whole = _half_blocks(rows, cols, by_cols)
    mine = (lambda i: (0, i)) if by_cols else (lambda i: (i, 0))

    def body(c_ref, own_ref, a_ref, b_ref, d_ref, o_ref):
        o_ref[...] = ((own_ref[0] + a_ref[0]) + b_ref[0]) + d_ref[0]

    slot = lambda k: pl.BlockSpec((1, br, bc), lambda i, c: (jnp.bitwise_xor(c[1], k),) + mine(i))
    grid_spec = pltpu.PrefetchScalarGridSpec(
        num_scalar_prefetch=1, grid=(nb,),
        in_specs=[slot(0), slot(1), slot(2), slot(3)],
        out_specs=pl.BlockSpec((br, bc), lambda i, c: whole(i, c[0])),
    )
    return pl.pallas_call(
        body, name=name, grid_spec=grid_spec, out_shape=jax.ShapeDtypeStruct((rows, cols), F32),
        compiler_params=_params(("parallel",)),
    )(place, part, landed, landed, landed)


_ROW_POOL_W, _ROW_NORM_AB, _ROW_FINAL, _ROW_POOL_SCALE, _ROW_Q_NORM = 0, 512, 520, 528, 532
_ROW_KV_NORM, _ROW_O_NORM, _ROW_A_LOG, _ROW_DT_BIAS, _ROW_LOSS = 534, 535, 536, 537, 538
_ROW_CONV, _ROW_NORM_C, _SMALL_ROWS = 544, 640, 672
_CONV_ROWS = CONV_WIDTH * 6


def _put_rows(dst_ref, row0, src, width):
    for r in range(width // LANES):
        dst_ref[row0 + r:row0 + r + 1, :] = src[:, r * LANES:(r + 1) * LANES]


def _pack_small(g, loss_tile):
    names = ("wpool", "norm_ab", "final_norm", "pool_scale", "q_a_norm", "kv_a_norm", "o_norm", "a_log", "dt_bias",
             "conv_w", "norm_c")

    def body(wpool, norm_ab, final_norm, pool_scale, q_norm, kv_norm, o_norm, a_log, dt_bias, conv_w, norm_c, loss, o_ref):
        o_ref[...] = jnp.zeros_like(o_ref)
        for gi in range(4):
            o_ref[_ROW_POOL_W + gi * 128:_ROW_POOL_W + (gi + 1) * 128, :] = wpool[gi * 128:(gi + 1) * 128, gi * 128:(gi + 1) * 128]
        _put_rows(o_ref, _ROW_NORM_AB, norm_ab[...], 1024)
        _put_rows(o_ref, _ROW_FINAL, final_norm[...], 1024)
        _put_rows(o_ref, _ROW_POOL_SCALE, pool_scale[...], 512)
        _put_rows(o_ref, _ROW_Q_NORM, q_norm[...], 256)
        for row, ref in ((_ROW_KV_NORM, kv_norm), (_ROW_O_NORM, o_norm), (_ROW_A_LOG, a_log), (_ROW_DT_BIAS, dt_bias)):
            o_ref[row:row + 1, :] = ref[...]
        o_ref[_ROW_LOSS:_ROW_LOSS + 1, :] = loss[0:1, :]
        for j in range(4):
            for r in range(CONV_WIDTH):
                _put_rows(o_ref, _ROW_CONV + j * _CONV_ROWS + r * 6, conv_w[r:r + 1, j * 768:(j + 1) * 768], 768)
            _put_rows(o_ref, _ROW_NORM_C + j * 8, norm_c[:, j * 256:(j + 1) * 256], 256)

    vmem = pl.BlockSpec(memory_space=pltpu.VMEM)
    return pl.pallas_call(
        body, name="pack_small", in_specs=[vmem] * 12, out_specs=vmem,
        out_shape=jax.ShapeDtypeStruct((_SMALL_ROWS, LANES), F32),
    )(*[g[n] for n in names], loss_tile)


_SMALL_NAMES = ("pool_w", "norm_ab", "final_norm", "pool_scale", "q_a_norm", "kv_a_norm", "o_norm", "a_log", "dt_bias",
                "conv_w", "norm_c")


def _take_rows(src, row0, width):
    return jnp.concatenate([src[row0 + r:row0 + r + 1, :] for r in range(width // LANES)], axis=1)


def _small_update(small_all, late_all, ws, ms, vs):
    n = len(_SMALL_NAMES)

    def body(*refs):
        a_ref, late_ref = refs[0], refs[1]
        refs = refs[1:]
        w_refs, m_refs, v_refs = refs[1:1 + n], refs[1 + n:1 + 2 * n], refs[1 + 2 * n:1 + 3 * n]
        outs = refs[1 + 3 * n:1 + 7 * n]
        loss_ref, tot = refs[1 + 7 * n], refs[2 + 7 * n]
        acc, late = a_ref[0], late_ref[0]
        for d in range(1, 8):
            acc = acc + a_ref[d]
            late = late + late_ref[d]
        tot[...] = acc
        x, y, _ = _place()
        j0 = 2 * x + y
        conv = tot[pl.ds(pl.multiple_of(_ROW_CONV + j0 * _CONV_ROWS, 8), _CONV_ROWS), :]
        norm_c = tot[pl.ds(pl.multiple_of(_ROW_NORM_C + j0 * 8, 8), 8), :]
        whole = tot[_ROW_NORM_AB:_ROW_CONV, :]
        at = lambda row: row - _ROW_NORM_AB
        grads = {
            "norm_ab": _take_rows(late, 0, 1024), "final_norm": _take_rows(whole, at(_ROW_FINAL), 1024),
            "pool_scale": _take_rows(whole, at(_ROW_POOL_SCALE), 512), "q_a_norm": _take_rows(whole, at(_ROW_Q_NORM), 256),
            "kv_a_norm": whole[at(_ROW_KV_NORM):at(_ROW_KV_NORM) + 1, :], "o_norm": whole[at(_ROW_O_NORM):at(_ROW_O_NORM) + 1, :],
            "a_log": tot[_ROW_A_LOG:_ROW_A_LOG + 1, 0:GDN_HEADS],
            "dt_bias": tot[_ROW_DT_BIAS:_ROW_DT_BIAS + 1, 0:GDN_HEADS],
            "norm_c": _take_rows(norm_c, 0, 256),
        }
        loss_ref[...] = whole[at(_ROW_LOSS):at(_ROW_LOSS) + 1, :]
        for i, name in enumerate(_SMALL_NAMES):
            g_out = outs[4 * i]
            if name == "pool_w":
                for gi in range(4):
                    g_out[gi] = tot[_ROW_POOL_W + gi * 128:_ROW_POOL_W + (gi + 1) * 128, :]
            elif name == "conv_w":
                for r in range(CONV_WIDTH):
                    g_out[r:r + 1, :] = _take_rows(conv, r * 6, 768)
            else:
                g_out[...] = grads[name]
            _adam_update(g_out, w_refs[i], m_refs[i], v_refs[i], *outs[4 * i + 1:4 * i + 4])

    vmem = pl.BlockSpec(memory_space=pltpu.VMEM)
    out_shape = [jax.ShapeDtypeStruct(w.shape, F32) for w in ws for _ in range(4)] + [jax.ShapeDtypeStruct((1, LANES), F32)]
    return pl.pallas_call(
        body, name="small_update", in_specs=[vmem] * (2 + 3 * n), out_specs=[vmem] * (4 * n + 1), out_shape=out_shape,
        scratch_shapes=[pltpu.VMEM((_SMALL_ROWS, LANES), F32)],
        compiler_params=pltpu.CompilerParams(vmem_limit_bytes=VMEM_LIMIT),
    )(small_all, late_all, *ws, *ms, *vs)


def _adam_update(g_ref, w_ref, m_ref, v_ref, d_ref, mo_ref, vo_ref):
    gv = g_ref[...]
    mn = ADAM_B1 * m_ref[...] + (1.0 - ADAM_B1) * gv
    vn = ADAM_B2 * v_ref[...] + (1.0 - ADAM_B2) * (gv * gv)
    mo_ref[...] = mn
    vo_ref[...] = vn
    c1 = 1.0 - ADAM_B1 ** ADAM_STEP
    c2 = 1.0 - ADAM_B2 ** ADAM_STEP
    d_ref[...] = -ADAM_LR * ((mn / c1) / (jnp.sqrt(vn / c2) + ADAM_EPS) + ADAM_WD * w_ref[...])


def _adamw_rows(g, w, m, v, *, name):
    rows, cols = g.shape
    if rows % LANES == 0:
        tr = _tile(rows, 512)
        blk, steps = pl.BlockSpec((tr, cols), lambda i: (i, 0)), rows // tr
    else:
        tc = _tile(cols, 256)
        blk, steps = pl.BlockSpec((rows, tc), lambda i: (0, i)), cols // tc

    def body(*refs):
        _adam_update(*refs)

    out = jax.ShapeDtypeStruct((rows, cols), F32)
    return pl.pallas_call(
        body, name=name, grid=(steps,), in_specs=[blk] * 4, out_specs=[blk] * 3, out_shape=[out] * 3,
        compiler_params=_params(("parallel",)),
    )(g, w, m, v)


_ADAM_ROWWISE = ("w_in_ab", "w_q_b", "w_kv_b", "w_out_ab", "w_in_c", "w_out_c")


_SHARD_AXIS = {"w_in_ab": 1, "w_q_b": 1, "w_kv_b": 1, "w_out_ab": 0, "w_in_c": 1, "w_out_c": 0, "conv_w": 1, "norm_c": 1}
_ALL_NAMES = ("norm_ab", "w_in_ab", "q_a_norm", "w_q_b", "kv_a_norm", "w_kv_b", "pool_w", "pool_scale", "w_out_ab",
              "norm_c", "w_in_c", "conv_w", "a_log", "dt_bias", "o_norm", "w_out_c", "final_norm")


def _join_shards(a, axis):
    _, r, c = a.shape
    return a.reshape(4 * r, c) if axis == 0 else jnp.transpose(a, (1, 0, 2)).reshape(r, 4 * c)


def _split_shards(a, axis):
    r, c = a.shape
    return a.reshape(4, r // 4, c) if axis == 0 else jnp.transpose(a.reshape(r, 4, c // 4), (1, 0, 2))


def kernel(x, positions, norm_ab, w_in_ab, q_a_norm, w_q_b, kv_a_norm, w_kv_b, pool_w, pool_scale, w_out_ab, norm_c, w_in_c, conv_w, a_log, dt_bias, o_norm, w_out_c, final_norm, loss_target, m_norm_ab, m_w_in_ab, m_q_a_norm, m_w_q_b, m_kv_a_norm, m_w_kv_b, m_pool_w, m_pool_scale, m_w_out_ab, m_norm_c, m_w_in_c, m_conv_w, m_a_log, m_dt_bias, m_o_norm, m_w_out_c, m_final_norm, v_norm_ab, v_w_in_ab, v_q_a_norm, v_w_q_b, v_kv_a_norm, v_w_kv_b, v_pool_w, v_pool_scale, v_w_out_ab, v_norm_c, v_w_in_c, v_conv_w, v_a_log, v_dt_bias, v_o_norm, v_w_out_c, v_final_norm):
    given = dict(locals())
    c = lax.axis_index("c")
    t = x.shape[1]

    def shard_of(prefix, name):
        a = given[prefix + name]
        return a.reshape(a.shape[1:]) if a.ndim > 2 else a.reshape(1, -1)

    big, big_even, big_odd, small_sharded = _ADAM_ROWWISE, _ADAM_ROWWISE[:4], _ADAM_ROWWISE[4:], ("conv_w", "norm_c")
    chip = 2 * lax.axis_index("x") + lax.axis_index("y")
    core = c.astype(jnp.int32).reshape(1)
    place = jnp.stack([c, chip]).astype(jnp.int32)
    later = {"mid": big_even[1:], "odd": big_odd + small_sharded}
    travelling = {}

    def send(tag, after=None):
        shards = [shard_of("", n).astype(BF16) if n in big else shard_of("", n) for n in later[tag]]
        started = _to_chips_start(shards, per_chip_slot=False, name="gather_" + tag + "_start", after=after)
        travelling[tag] = (shards, started)
        return started[-1][0, 0]

    mid_sent = send("mid")
    gathered = _gather_weights([shard_of("", "w_in_ab").astype(BF16)], [])
    full = {"w_in_ab": _join_shards(gathered[0], _SHARD_AXIS["w_in_ab"])}
    for name in ("norm_ab", "q_a_norm", "kv_a_norm", "pool_w", "pool_scale"):
        full[name] = shard_of("", name)
    lw = _layout_in_ab(full)
    lw["norm_ab"] = lw["norm_ab"] + mid_sent

    def more_weights(tag, after):
        shards, started = travelling[tag]
        landed = _to_chips_wait(started, after, per_chip_slot=False, name="gather_" + tag + "_wait")
        w = {}
        for name, land, own in zip(later[tag], landed, shards):
            w[name] = _join_shards(lax.dynamic_update_index_in_dim(land, own, chip, 0), _SHARD_AXIS[name])
        if tag == "mid":
            out = _layout_mid(w)
            out["wq"] = out["wq"] + send("odd", after=landed[0]).astype(BF16)
            return out
        for name in ("a_log", "dt_bias", "o_norm", "final_norm"):
            w[name] = shard_of("", name)
        return _layout_odd(w)

    transposed = ("w_in_ab", "w_in_c")

    def chip_slots(names, g):
        grads = _unlayout_grads(g, names)
        return ([_split_shards(grads[n], 0 if n in transposed else _SHARD_AXIS[n]) for n in names],
                [n in transposed for n in names])

    def chip_partials(names, slots, partial, by_cols):
        return [_core_sum(s, p, core, name="core_sum_" + n, by_cols=b) for n, s, p, b in zip(names, slots, partial, by_cols)]

    groups = {"odd": big_odd, "out_ab": ("w_out_ab",), "in_ab": ("w_in_ab", "w_q_b", "w_kv_b")}
    sent, swapping = {}, {}

    def on_grads(tag, g, after):
        if tag == "odd":
            slots, by_cols = chip_slots(groups[tag], g)
            swapping[tag] = (slots, by_cols, _core_swap_partial_start(slots, by_cols, name="core_swap_partial_odd_start"))
            return swapping[tag][2][-1][0, 0]
        if tag == "odd_go":
            tag = "odd"
            _, by_cols, started = swapping[tag]
            slots, partial = _core_swap_partial_wait(started, after, by_cols, name="core_swap_partial_odd_wait")
        else:
            slots, by_cols = chip_slots(groups[tag], g)
            partial = _core_swap_partial(slots, by_cols, name="core_swap_partial_" + tag)
        part = chip_partials(groups[tag], slots, partial, by_cols)
        sent[tag] = (part, _to_chips_start(part, per_chip_slot=True, name="exchange_" + tag + "_start"))
        token = sent[tag][1][-1][0, 0]
        if tag == "in_ab":
            pack = _pack_small({**g, "norm_ab": jnp.zeros((1, 1024), F32)}, g["loss"])
            sent["small"] = (pack, _to_chips_start([pack], per_chip_slot=False, to_all=True, name="exchange_small_start"))
            token = token + sent["small"][1][-1][0, 0]
        return token

    loss_tile, dx, g = _local_step(x[0], positions.reshape(t, 1), loss_target[0], lw, more_weights, on_grads)
    late_all = _chip_exchange([], g["norm_ab"].reshape(8, LANES))[-1]
    pack, started = sent.pop("small")
    landed = _to_chips_wait(started, late_all, per_chip_slot=False, to_all=True, name="exchange_small_wait")[0]
    small_all = lax.dynamic_update_index_in_dim(landed, pack, 2 * chip + c, 0)
    halves = {}
    for tag, names in groups.items():
        part, started = sent[tag]
        landed = _to_chips_wait(started, late_all, per_chip_slot=True, name="exchange_" + tag + "_wait")
        for n, l, p in zip(names, landed, part):
            halves[n] = _chip_sum(l, p, place, name="chip_sum_" + n, by_cols=n in transposed)
    gbig = dict(zip(big, _core_swap_sum([halves[n] for n in big], [n in transposed for n in big])))

    res = {}
    for name in big:
        operands = [gbig[name], shard_of("", name), shard_of("m_", name), shard_of("v_", name)]
        flip = name in transposed
        if flip:
            operands[1:] = [jnp.transpose(a) for a in operands[1:]]
        out = (operands[0],) + tuple(_adamw_rows(*operands, name="adamw_" + name))
        out = [jnp.transpose(a) for a in out] if flip else out
        res["grad", name], res["delta", name], res["m", name], res["v", name] = out
    out = _small_update(small_all, late_all, [shard_of("", n) for n in _SMALL_NAMES], [shard_of("m_", n) for n in _SMALL_NAMES],
                        [shard_of("v_", n) for n in _SMALL_NAMES])
    for i, name in enumerate(_SMALL_NAMES):
        res["grad", name], res["delta", name], res["m", name], res["v", name] = out[4 * i:4 * i + 4]
    res = {k: a.reshape(given[k[1]].shape) for k, a in res.items()}
    loss = out[-1][0, 0]
    outs = [loss, dx.reshape(x.shape)]
    for key in ("grad", "delta", "m", "v"):
        outs += [res[key, n] for n in _ALL_NAMES]
    return tuple(outs)
```

```python
import functools

import jax
import jax.numpy as jnp
from jax import lax
from jax.experimental import pallas as pl
from jax.experimental.pallas import tpu as pltpu

F32 = jnp.float32
BF16 = jnp.bfloat16
HI = lax.Precision.HIGHEST
MESH = pl.DeviceIdType.MESH

RMS_EPS = 1e-6
MLA_HEADS = 8
MLA_Q_RANK = 256
MLA_KV_RANK = 128
MLA_NOPE = 64
MLA_ROPE = 32
MLA_V = 64
ROPE_THETA = 10000.0
POOL_WINDOWS = (2, 4, 8, 16)
POOL_GROUP = 128
POOL_WIDTH = 512
POOL_HALO = 16
GDN_HEADS = 8
GDN_DK = 128
CONV_WIDTH = 4
CONV_HALO = 8
CHUNK = 64
IN_AB_PAD = 2048
IN_C_PAD = 4224
ATT_SCALE = (MLA_NOPE + MLA_ROPE) ** -0.5
LOG2E = 1.4426950408889634

ADAM_LR = 0.001
ADAM_B1 = 0.9
ADAM_B2 = 0.999
ADAM_EPS = 1e-08
ADAM_WD = 0.01
ADAM_STEP = 10

LANES = 128
VMEM_LIMIT = 56 * 1024 * 1024

ROW_TILE = 256
ATT_TILE = 1024
GDN_TILE = 256
MM_TILE = (1408, 1408, 2048)

NN = (((1,), (0,)), ((), ()))
NT = (((1,), (1,)), ((), ()))
TN = (((0,), (0,)), ((), ()))


def _dot(a, b, dims=NN, prec=None):
    return lax.dot_general(a, b, dims, precision=prec, preferred_element_type=F32)


def _tile(n, pref):
    if n <= pref:
        return n
    step = LANES if pref >= LANES else 8
    for t in range(pref - pref % step, 0, -step):
        if n % t == 0:
            return t
    return n


def _params(sem):
    return pltpu.CompilerParams(dimension_semantics=sem, vmem_limit_bytes=VMEM_LIMIT)


def _sigmoid(x):
    return 0.5 * jnp.tanh(0.5 * x) + 0.5


def _softplus(x):
    return jnp.maximum(x, 0.0) + jnp.log(1.0 + jnp.exp(-jnp.abs(x)))


def _matmul(a, b, mode, *, name):
    if mode == "nn":
        (m, k), (k2, n) = a.shape, b.shape
    elif mode == "nt":
        (m, k), (n, k2) = a.shape, b.shape
    else:
        (k, m), (k2, n) = a.shape, b.shape
    assert k == k2, (a.shape, b.shape, mode)
    tm, tn, tk = _tile(m, MM_TILE[0]), _tile(n, MM_TILE[1]), _tile(k, MM_TILE[2])
    nk = k // tk
    if mode == "tn":
        a_spec = pl.BlockSpec((tk, tm), lambda i, j, kk: (kk, i))
    else:
        a_spec = pl.BlockSpec((tm, tk), lambda i, j, kk: (i, kk))
    if mode == "nt":
        b_spec = pl.BlockSpec((tn, tk), lambda i, j, kk: (j, kk))
    else:
        b_spec = pl.BlockSpec((tk, tn), lambda i, j, kk: (kk, j))
    o_spec = pl.BlockSpec((tm, tn), lambda i, j, kk: (i, j))
    dims = {"nn": NN, "nt": NT, "tn": TN}[mode]

    def body(a_ref, b_ref, o_ref, *scratch):
        if nk == 1:
            o_ref[...] = _dot(a_ref[...], b_ref[...], dims)
            return
        acc = scratch[0]
        kk = pl.program_id(2)

        @pl.when(kk == 0)
        def _():
            acc[...] = jnp.zeros_like(acc)

        acc[...] += _dot(a_ref[...], b_ref[...], dims)

        @pl.when(kk == nk - 1)
        def _():
            o_ref[...] = acc[...]

    return pl.pallas_call(
        body, name=name, grid=(m // tm, n // tn, nk), in_specs=[a_spec, b_spec], out_specs=o_spec,
        out_shape=jax.ShapeDtypeStruct((m, n), F32),
        scratch_shapes=[pltpu.VMEM((tm, tn), F32)] if nk > 1 else [],
        compiler_params=_params(("parallel", "parallel", "arbitrary")),
    )(a, b)


def _rms_in_proj(h, g, w, *, name):
    t, d = h.shape
    n = w.shape[1]
    tm, tn = _tile(t, MM_TILE[0]), _tile(n, MM_TILE[1])

    def body(h_ref, g_ref, w_ref, o_ref, hn_ref):
        @pl.when(pl.program_id(1) == 0)
        def _():
            x = h_ref[...]
            r = lax.rsqrt(jnp.mean(x * x, axis=-1, keepdims=True) + RMS_EPS)
            hn_ref[...] = (x * r * g_ref[...]).astype(BF16)

        o_ref[...] = _dot(hn_ref[...], w_ref[...])

    return pl.pallas_call(
        body, name=name, grid=(t // tm, n // tn),
        in_specs=[pl.BlockSpec((tm, d), lambda i, j: (i, 0)), pl.BlockSpec((1, d), lambda i, j: (0, 0)),
                  pl.BlockSpec((d, tn), lambda i, j: (0, j))],
        out_specs=[pl.BlockSpec((tm, tn), lambda i, j: (i, j)), pl.BlockSpec((tm, d), lambda i, j: (i, 0))],
        out_shape=[jax.ShapeDtypeStruct((t, n), F32), jax.ShapeDtypeStruct((t, d), BF16)],
        compiler_params=_params(("parallel", "arbitrary")),
    )(h, g, w)


def _matmul_rms_bwd(dproj, w, h, g, dres, *, name, prev_y=None):
    t, k = dproj.shape
    d = w.shape[0]
    tm = _tile(t, 2 * ROW_TILE)
    chained = prev_y is not None

    def body(dp_ref, w_ref, h_ref, g_ref, dres_ref, *rest):
        i = pl.program_id(0)
        dh_ref, dg_ref = rest[-4:-2] if chained else rest
        dyv = _dot(dp_ref[...], w_ref[...], NT)
        x = h_ref[...]
        r = lax.rsqrt(jnp.mean(x * x, axis=-1, keepdims=True) + RMS_EPS)
        xh = x * r
        dxh = dyv * g_ref[...]
        dh = dres_ref[...] + r * (dxh - xh * jnp.mean(dxh * xh, axis=-1, keepdims=True))
        dh_ref[...] = dh

        @pl.when(i == 0)
        def _():
            dg_ref[...] = jnp.zeros_like(dg_ref)
            if chained:
                rest[-1][...] = jnp.zeros_like(rest[-1])

        dg_ref[...] += jnp.sum(dyv * xh, axis=0, keepdims=True)
        if chained:
            y_ref, dhb_ref, dw_ref = rest[0], rest[-2], rest[-1]
            dhb_ref[...] = dh.astype(BF16)
            dw_ref[...] += _dot(y_ref[...], dhb_ref[...], TN)

    row = pl.BlockSpec((tm, d), lambda i: (i, 0))
    vec = pl.BlockSpec((1, d), lambda i: (0, 0))
    in_specs = [pl.BlockSpec((tm, k), lambda i: (i, 0)), pl.BlockSpec((d, k), lambda i: (0, 0)), row, vec, row]
    out_specs, out_shape = [row, vec], [jax.ShapeDtypeStruct((t, d), F32), jax.ShapeDtypeStruct((1, d), F32)]
    args = [dproj, w, h, g, dres]
    if chained:
        in_specs.append(row)
        args.append(prev_y)
        out_specs += [row, pl.BlockSpec((d, d), lambda i: (0, 0))]
        out_shape += [jax.ShapeDtypeStruct((t, d), BF16), jax.ShapeDtypeStruct((d, d), F32)]
    return pl.pallas_call(
        body, name=name, grid=(t // tm,), in_specs=in_specs, out_specs=out_specs, out_shape=out_shape,
        compiler_params=_params(("arbitrary",)),
    )(*args)


def _rope_partner(x):
    lane = lax.broadcasted_iota(jnp.int32, x.shape, 1)
    swapped = jnp.where(lane < MLA_NOPE + MLA_ROPE // 2, pltpu.roll(x, LANES - 16, 1), pltpu.roll(x, 16, 1))
    return jnp.where((lane >= MLA_NOPE) & (lane < MLA_NOPE + MLA_ROPE), swapped, 0.0)


def _pool_counts(row0, tm, w):
    t_idx = row0 + lax.broadcasted_iota(jnp.int32, (tm, POOL_GROUP), 0)
    return jnp.minimum(t_idx + 1, w).astype(F32)


def _ab_prep(proj, pos, inv_freq, q_a_norm, kv_a_norm, wq, wk, wv, wpool, *, name):
    t = proj.shape[0]
    tm = _tile(t, ROW_TILE)
    hb = tm // POOL_HALO

    def body(p_ref, halo_ref, pos_ref, inv_ref, qg_ref, kg_ref, wq_ref, wk_ref, wv_ref, wp_ref,
             q_ref, k_ref, v_ref, yb_ref, qn_ref, kvn_ref, d_ref, cos_ref, sin_ref, ext):
        i = pl.program_id(0)
        ql = p_ref[:, 0:MLA_Q_RANK]
        r = lax.rsqrt(jnp.mean(ql * ql, axis=-1, keepdims=True) + RMS_EPS)
        qn = (ql * r * qg_ref[...]).astype(BF16)
        qn_ref[...] = qn
        kl = p_ref[:, MLA_Q_RANK:MLA_Q_RANK + MLA_KV_RANK]
        r = lax.rsqrt(jnp.mean(kl * kl, axis=-1, keepdims=True) + RMS_EPS)
        kvn = (kl * r * kg_ref[...]).astype(BF16)
        kvn_ref[...] = kvn
        ang = pos_ref[...].astype(F32) * inv_ref[...]
        lane = lax.broadcasted_iota(jnp.int32, (tm, LANES), 1)
        in_rope = (lane >= MLA_NOPE) & (lane < MLA_NOPE + MLA_ROPE)
        cos_t = jnp.where(in_rope, jnp.cos(ang), 1.0)
        sin_t = jnp.where(in_rope, jnp.sin(ang), 0.0)
        sin_t = jnp.where(lane < MLA_NOPE + MLA_ROPE // 2, -sin_t, sin_t)
        cos_ref[...] = cos_t
        sin_ref[...] = sin_t
        kr = p_ref[:, 384:512]
        kr = kr * cos_t + _rope_partner(kr) * sin_t
        qraw = _dot(qn, wq_ref[...])
        kvk = _dot(kvn, wk_ref[...])
        for h in range(MLA_HEADS):
            sl = slice(h * LANES, (h + 1) * LANES)
            qh = qraw[:, sl]
            q_ref[:, sl] = ((qh * cos_t + _rope_partner(qh) * sin_t) * (ATT_SCALE * LOG2E)).astype(BF16)
            k_ref[:, sl] = (kvk[:, sl] + kr).astype(BF16)
        v_ref[...] = _dot(kvn, wv_ref[...]).astype(BF16)
        xp = p_ref[:, 512:1024]
        ext[0:POOL_HALO, :] = jnp.where(i > 0, halo_ref[...], 0.0)
        ext[POOL_HALO:POOL_HALO + tm, :] = xp
        for g, w in enumerate(POOL_WINDOWS):
            lo = g * POOL_GROUP
            acc = ext[POOL_HALO:POOL_HALO + tm, lo:lo + POOL_GROUP]
            for s in range(1, w):
                acc = acc + ext[POOL_HALO - s:POOL_HALO - s + tm, lo:lo + POOL_GROUP]
            cnt = _pool_counts(i * tm, tm, w)
            d_ref[:, lo:lo + POOL_GROUP] = (acc / cnt - xp[:, lo:lo + POOL_GROUP]).astype(BF16)
        yb_ref[...] = _dot(d_ref[...], wp_ref[...])

    row = lambda w: pl.BlockSpec((tm, w), lambda i: (i, 0))
    vec = lambda w: pl.BlockSpec((1, w), lambda i: (0, 0))
    whole = lambda a: pl.BlockSpec(a.shape, lambda i: (0, 0))
    return pl.pallas_call(
        body, name=name, grid=(t // tm,),
        in_specs=[row(1024), pl.BlockSpec((POOL_HALO, POOL_WIDTH), lambda i: (jnp.maximum(i * hb - 1, 0), 1)),
                  pl.BlockSpec((tm, 1), lambda i: (i, 0)), vec(LANES), vec(MLA_Q_RANK), vec(MLA_KV_RANK),
                  whole(wq), whole(wk), whole(wv), whole(wpool)],
        out_specs=[row(1024), row(1024), row(512), row(512), row(MLA_Q_RANK), row(MLA_KV_RANK), row(POOL_WIDTH),
                   row(LANES), row(LANES)],
        out_shape=[jax.ShapeDtypeStruct((t, 1024), BF16), jax.ShapeDtypeStruct((t, 1024), BF16),
                   jax.ShapeDtypeStruct((t, 512), BF16), jax.ShapeDtypeStruct((t, 512), F32),
                   jax.ShapeDtypeStruct((t, MLA_Q_RANK), BF16), jax.ShapeDtypeStruct((t, MLA_KV_RANK), BF16),
                   jax.ShapeDtypeStruct((t, POOL_WIDTH), BF16), jax.ShapeDtypeStruct((t, LANES), F32),
                   jax.ShapeDtypeStruct((t, LANES), F32)],
        scratch_shapes=[pltpu.VMEM((tm + POOL_HALO, POOL_WIDTH), F32)],
        compiler_params=_params(("parallel",)),
    )(proj, proj, pos, inv_freq, q_a_norm, kv_a_norm, wq, wk, wv, wpool)


def _ab_prep_bwd(proj, q_a_norm, kv_a_norm, dq, dk, dv, cos_t, sin_t, dyb, dz, qn, kvn, d, hn, wq, wk, wv, wpool, *, name):
    t = proj.shape[0]
    tm = _tile(t, ROW_TILE)
    hb = tm // POOL_HALO
    last_halo = t // POOL_HALO - 1
    nt = t // tm

    def body(p_ref, qg_ref, kg_ref, dq_ref, dk_ref, dv_ref, c_ref, s_ref, dyb_ref, dybn_ref, dz_ref,
             qn_ref, kvn_ref, d_ref, hn_ref, wq_ref, wk_ref, wv_ref, wp_ref,
             dp_ref, dqg_ref, dkg_ref, dwq_ref, dwk_ref, dwv_ref, dwp_ref, dwin_ref, ext, dqr_ref, dkb_ref):
        i = pl.program_id(0)

        @pl.when(i == 0)
        def _():
            for ref in (dqg_ref, dkg_ref, dwq_ref, dwk_ref, dwv_ref, dwp_ref, dwin_ref):
                ref[...] = jnp.zeros_like(ref)

        def norm_bwd(x, g, dy, dg_ref):
            r = lax.rsqrt(jnp.mean(x * x, axis=-1, keepdims=True) + RMS_EPS)
            xh = x * r
            dxh = dy * g
            dg_ref[...] += jnp.sum(dy * xh, axis=0, keepdims=True)
            return r * (dxh - xh * jnp.mean(dxh * xh, axis=-1, keepdims=True))

        c, s = c_ref[...], s_ref[...]
        lane = lax.broadcasted_iota(jnp.int32, (tm, LANES), 1)
        in_rope = (lane >= MLA_NOPE) & (lane < MLA_NOPE + MLA_ROPE)
        dkr = jnp.zeros((tm, LANES), F32)
        for h in range(MLA_HEADS):
            sl = slice(h * LANES, (h + 1) * LANES)
            g = dq_ref[:, sl]
            dqr_ref[:, sl] = ((g * c + _rope_partner(g * s)) * ATT_SCALE).astype(BF16)
            gk = dk_ref[:, sl]
            dkb_ref[:, sl] = gk.astype(BF16)
            dkr = dkr + jnp.where(in_rope, gk, 0.0)
        dkr = dkr * c + _rope_partner(dkr * s)
        dqn = _dot(dqr_ref[...], wq_ref[...], NT)
        dkvn = _dot(dkb_ref[...], wk_ref[...], NT) + _dot(dv_ref[...], wv_ref[...], NT)
        dql = norm_bwd(p_ref[:, 0:MLA_Q_RANK], qg_ref[...], dqn, dqg_ref)
        dp_ref[:, 0:MLA_Q_RANK] = dql.astype(BF16)
        dkl = norm_bwd(p_ref[:, MLA_Q_RANK:384], kg_ref[...], dkvn, dkg_ref)
        dp_ref[:, MLA_Q_RANK:384] = dkl.astype(BF16)
        dp_ref[:, 384:512] = dkr.astype(BF16)
        ddv = _dot(dyb_ref[...], wp_ref[...], NT)
        ddn = _dot(dybn_ref[...], wp_ref[...], NT)
        for g, w in enumerate(POOL_WINDOWS):
            lo = g * POOL_GROUP
            ext[0:tm, lo:lo + POOL_GROUP] = ddv[:, lo:lo + POOL_GROUP] / _pool_counts(i * tm, tm, w)
            nxt = ddn[:, lo:lo + POOL_GROUP] / _pool_counts((i + 1) * tm, POOL_HALO, w)
            ext[tm:tm + POOL_HALO, lo:lo + POOL_GROUP] = jnp.where(i < nt - 1, nxt, 0.0)
        for g, w in enumerate(POOL_WINDOWS):
            lo = g * POOL_GROUP
            acc = ext[0:tm, lo:lo + POOL_GROUP]
            for s in range(1, w):
                acc = acc + ext[s:s + tm, lo:lo + POOL_GROUP]
            dp_ref[:, 512 + lo:512 + lo + POOL_GROUP] = (acc - ddv[:, lo:lo + POOL_GROUP]).astype(BF16)
        dp_ref[:, 1024:2048] = dz_ref[...]
        dwq_ref[...] += _dot(qn_ref[...], dqr_ref[...], TN)
        dwk_ref[...] += _dot(kvn_ref[...], dkb_ref[...], TN)
        dwv_ref[...] += _dot(kvn_ref[...], dv_ref[...], TN)
        dwp_ref[...] += _dot(d_ref[...], dyb_ref[...], TN)
        dwin_ref[...] += _dot(dp_ref[...], hn_ref[...], TN)

    row = lambda w: pl.BlockSpec((tm, w), lambda i: (i, 0))
    vec = lambda w: pl.BlockSpec((1, w), lambda i: (0, 0))
    whole = lambda a: pl.BlockSpec(a.shape, lambda i: (0, 0))
    weights = (wq, wk, wv, wpool)
    return pl.pallas_call(
        body, name=name, grid=(nt,),
        in_specs=[row(1024), vec(MLA_Q_RANK), vec(MLA_KV_RANK), row(1024), row(1024), row(512), row(LANES), row(LANES),
                  row(POOL_WIDTH),
                  pl.BlockSpec((POOL_HALO, POOL_WIDTH), lambda i: (jnp.minimum((i + 1) * hb, last_halo), 0)),
                  row(1024), row(MLA_Q_RANK), row(MLA_KV_RANK), row(POOL_WIDTH), row(1024)] + [whole(w) for w in weights],
        out_specs=[row(IN_AB_PAD), vec(MLA_Q_RANK), vec(MLA_KV_RANK)] + [whole(w) for w in weights]
        + [pl.BlockSpec((IN_AB_PAD, 1024), lambda i: (0, 0))],
        out_shape=[jax.ShapeDtypeStruct((t, IN_AB_PAD), BF16), jax.ShapeDtypeStruct((1, MLA_Q_RANK), F32),
                   jax.ShapeDtypeStruct((1, MLA_KV_RANK), F32)] + [jax.ShapeDtypeStruct(w.shape, F32) for w in weights]
        + [jax.ShapeDtypeStruct((IN_AB_PAD, 1024), F32)],
        scratch_shapes=[pltpu.VMEM((tm + POOL_HALO, POOL_WIDTH), F32), pltpu.VMEM((tm, 1024), BF16),
                        pltpu.VMEM((tm, 1024), BF16)],
        compiler_params=_params(("arbitrary",)),
    )(proj, q_a_norm, kv_a_norm, dq, dk, dv, cos_t, sin_t, dyb, dyb, dz, qn, kvn, d, hn, wq, wk, wv, wpool)


def _gate_out_proj(o, ybraw, proj, pool_scale, w, hres, *, name):
    t = o.shape[0]
    tm = _tile(t, 2 * ROW_TILE)

    def body(o_ref, yb_ref, z_ref, ps_ref, w_ref, h_ref, ho_ref, y_ref):
        z = z_ref[...]
        sz = z * _sigmoid(z)
        y_ref[:, 0:512] = (o_ref[...] * sz[:, 0:512]).astype(BF16)
        y_ref[:, 512:1024] = (yb_ref[...] * ps_ref[...] * sz[:, 512:1024]).astype(BF16)
        ho_ref[...] = h_ref[...] + _dot(y_ref[...], w_ref[...])

    row = lambda w_: pl.BlockSpec((tm, w_), lambda i: (i, 0))
    return pl.pallas_call(
        body, name=name, grid=(t // tm,),
        in_specs=[row(512), row(512), pl.BlockSpec((tm, 1024), lambda i: (i, 1)), pl.BlockSpec((1, 512), lambda i: (0, 0)),
                  pl.BlockSpec(w.shape, lambda i: (0, 0)), row(1024)],
        out_specs=[row(1024), row(1024)],
        out_shape=[jax.ShapeDtypeStruct((t, 1024), F32), jax.ShapeDtypeStruct((t, 1024), BF16)],
        compiler_params=_params(("parallel",)),
    )(o, ybraw, proj, pool_scale, w, hres)


def _gate_bwd(dh, w, o, ybraw, proj, pool_scale, *, name):
    t = o.shape[0]
    tm = _tile(t, ROW_TILE)

    def body(dh_ref, w_ref, o_ref, yb_ref, z_ref, ps_ref, do_ref, dl_ref, dyb_ref, dz_ref, dps_ref):
        i = pl.program_id(0)
        z = z_ref[...]
        sg = _sigmoid(z)
        sz = z * sg
        dsz = sg * (1.0 + z * (1.0 - sg))
        dyv = _dot(dh_ref[...], w_ref[...], NT)
        dcat = dyv * sz
        ov = o_ref[...]
        ybs = yb_ref[...] * ps_ref[...]
        dz_ref[:, 0:512] = (dyv[:, 0:512] * ov * dsz[:, 0:512]).astype(BF16)
        dz_ref[:, 512:1024] = (dyv[:, 512:1024] * ybs * dsz[:, 512:1024]).astype(BF16)
        do = dcat[:, 0:512]
        do_ref[...] = do.astype(BF16)
        r_i = (lax.broadcasted_iota(jnp.int32, (1024, 512), 0) % 512) // MLA_V
        c_i = lax.broadcasted_iota(jnp.int32, (1024, 512), 1) // MLA_V
        prod = do * ov
        hi = prod.astype(BF16)
        lo = (prod - hi.astype(F32)).astype(BF16)
        dl_ref[...] = _dot(jnp.concatenate([hi, lo], axis=1), (r_i == c_i).astype(BF16))
        dyb_ref[...] = (dcat[:, 512:1024] * ps_ref[...]).astype(BF16)

        @pl.when(i == 0)
        def _():
            dps_ref[...] = jnp.zeros_like(dps_ref)

        dps_ref[...] += jnp.sum(dcat[:, 512:1024] * yb_ref[...], axis=0, keepdims=True)

    row = lambda w: pl.BlockSpec((tm, w), lambda i: (i, 0))
    vec = pl.BlockSpec((1, 512), lambda i: (0, 0))
    return pl.pallas_call(
        body, name=name, grid=(t // tm,),
        in_specs=[row(1024), pl.BlockSpec(w.shape, lambda i: (0, 0)), row(512), row(512),
                  pl.BlockSpec((tm, 1024), lambda i: (i, 1)), vec],
        out_specs=[row(512), row(512), row(512), row(1024), vec],
        out_shape=[jax.ShapeDtypeStruct((t, 512), BF16), jax.ShapeDtypeStruct((t, 512), F32),
                   jax.ShapeDtypeStruct((t, 512), BF16), jax.ShapeDtypeStruct((t, 1024), BF16),
                   jax.ShapeDtypeStruct((1, 512), F32)],
        compiler_params=_params(("arbitrary",)),
    )(dh, w, o, ybraw, proj, pool_scale)


ATT_HP_FWD = 4
ATT_HP_BWD = 2


def _diag_mask(tq):
    return lax.broadcasted_iota(jnp.int32, (tq, tq), 1) <= lax.broadcasted_iota(jnp.int32, (tq, tq), 0)


def _block_schedule(nq, key_major):
    if key_major:
        pairs = [(qi, ki) for ki in range(nq) for qi in range(ki, nq)]
    else:
        pairs = [(qi, ki) for qi in range(nq) for ki in range(qi + 1)]
    return jnp.asarray([p[0] for p in pairs], jnp.int32), jnp.asarray([p[1] for p in pairs], jnp.int32)


def _attn_fwd(q, k, v, *, name):
    t = q.shape[0]
    tq = _tile(t, ATT_TILE)
    nq = t // tq
    hp = ATT_HP_FWD
    qi_tab, ki_tab = _block_schedule(nq, key_major=False)

    def body(qi_ref, ki_ref, q_ref, k_ref, v_ref, o_ref, lse_ref, m_sc, l_sc, acc_sc):
        step = pl.program_id(1)
        qi, ki = qi_ref[step], ki_ref[step]

        @pl.when(ki == 0)
        def _():
            m_sc[...] = jnp.full_like(m_sc, -jnp.inf)
            l_sc[...] = jnp.zeros_like(l_sc)
            acc_sc[...] = jnp.zeros_like(acc_sc)

        def block(on_diagonal):
            scores = []
            for h in range(hp):
                sl = slice(h * LANES, (h + 1) * LANES)
                scores.append(_dot(q_ref[:, sl], k_ref[:, sl], NT))
            if on_diagonal:
                mask = _diag_mask(tq)
                scores = [jnp.where(mask, s, -jnp.inf) for s in scores]
            for h, s in enumerate(scores):
                vv = v_ref[:, (h // 2) * LANES:(h // 2 + 1) * LANES]
                m_prev = m_sc[h]
                m_new = jnp.maximum(m_prev, jnp.max(s, axis=-1, keepdims=True))
                alpha = jnp.exp2(m_prev - m_new)
                p = jnp.exp2(s - m_new[:, 0:1])
                l_sc[h] = alpha * l_sc[h] + jnp.sum(p, axis=-1, keepdims=True)
                acc_sc[h] = alpha * acc_sc[h] + _dot(p.astype(BF16), vv)
                m_sc[h] = m_new

        pl.when(ki < qi)(functools.partial(block, False))
        pl.when(ki == qi)(functools.partial(block, True))

        @pl.when(ki == qi)
        def _():
            first = lax.broadcasted_iota(jnp.int32, (tq, LANES), 1) < MLA_V
            for pr in range(hp // 2):
                a, b = 2 * pr, 2 * pr + 1
                sl = slice(pr * LANES, (pr + 1) * LANES)
                o_ref[:, sl] = jnp.where(first, acc_sc[a] / l_sc[a], acc_sc[b] / l_sc[b])
                lse_ref[:, sl] = jnp.where(first, m_sc[a] + jnp.log2(l_sc[a]), m_sc[b] + jnp.log2(l_sc[b]))

    grid_spec = pltpu.PrefetchScalarGridSpec(
        num_scalar_prefetch=2, grid=(MLA_HEADS // hp, qi_tab.shape[0]),
        in_specs=[pl.BlockSpec((tq, hp * LANES), lambda g, s, qt, kt: (qt[s], g)),
                  pl.BlockSpec((tq, hp * LANES), lambda g, s, qt, kt: (kt[s], g)),
                  pl.BlockSpec((tq, hp * MLA_V), lambda g, s, qt, kt: (kt[s], g))],
        out_specs=[pl.BlockSpec((tq, hp * MLA_V), lambda g, s, qt, kt: (qt[s], g)),
                   pl.BlockSpec((tq, hp * MLA_V), lambda g, s, qt, kt: (qt[s], g))],
        scratch_shapes=[pltpu.VMEM((hp, tq, LANES), F32)] * 3,
    )
    return pl.pallas_call(
        body, name=name, grid_spec=grid_spec,
        out_shape=[jax.ShapeDtypeStruct((t, 512), F32), jax.ShapeDtypeStruct((t, 512), F32)],
        compiler_params=_params(("parallel", "arbitrary")),
    )(qi_tab, ki_tab, q, k, v)


def _attn_bwd(q, k, v, do, lse, delta, *, name):
    t = q.shape[0]
    tq = _tile(t, ATT_TILE)
    nq = t // tq
    hp = ATT_HP_BWD
    qi_tab, ki_tab = _block_schedule(nq, key_major=True)

    def body(qi_ref, ki_ref, q_ref, k_ref, v_ref, do_ref, lse_ref, dl_ref, dq_ref, dk_ref, dv_ref, dk_sc, dv_sc):
        step = pl.program_id(1)
        qi, ki = qi_ref[step], ki_ref[step]

        @pl.when(step == 0)
        def _():
            dq_ref[...] = jnp.zeros_like(dq_ref)

        @pl.when(qi == ki)
        def _():
            dk_sc[...] = jnp.zeros_like(dk_sc)
            dv_sc[...] = jnp.zeros_like(dv_sc)

        def block(on_diagonal):
            lane = lax.broadcasted_iota(jnp.int32, (tq, LANES), 1)
            rows = pl.ds(pl.multiple_of(qi * tq, tq), tq)
            heads = [slice(h * LANES, (h + 1) * LANES) for h in range(hp)]
            scores = [_dot(q_ref[:, sl], k_ref[:, sl], NT) for sl in heads]
            dps = []
            for h in range(hp):
                dov = do_ref[:, (h // 2) * LANES:(h // 2 + 1) * LANES]
                mine = (lane < MLA_V) if h % 2 == 0 else (lane >= MLA_V)
                dps.append(_dot(jnp.where(mine, dov, jnp.zeros_like(dov)), v_ref[:, (h // 2) * LANES:(h // 2 + 1) * LANES], NT))
            mask = _diag_mask(tq) if on_diagonal else None
            for h, sl in enumerate(heads):
                col = (h // 2) * LANES + (h % 2) * MLA_V
                p = jnp.exp2(scores[h] - lse_ref[:, col:col + 1])
                if on_diagonal:
                    p = jnp.where(mask, p, 0.0)
                ds = (p * (dps[h] - dl_ref[:, col:col + 1])).astype(BF16)
                dv_sc[h] += _dot(p.astype(BF16), do_ref[:, (h // 2) * LANES:(h // 2 + 1) * LANES], TN)
                dk_sc[h] += _dot(ds, q_ref[:, sl], TN)
                dq_ref[rows, sl] += _dot(ds, k_ref[:, sl], NN)

        pl.when(qi > ki)(functools.partial(block, False))
        pl.when(qi == ki)(functools.partial(block, True))

        @pl.when(qi == nq - 1)
        def _():
            first = lax.broadcasted_iota(jnp.int32, (tq, LANES), 1) < MLA_V
            for h in range(hp):
                dk_ref[:, h * LANES:(h + 1) * LANES] = dk_sc[h] * (1.0 / LOG2E)
            for pr in range(hp // 2):
                dv_ref[:, pr * LANES:(pr + 1) * LANES] = jnp.where(first, dv_sc[2 * pr], dv_sc[2 * pr + 1]).astype(BF16)

    qrow = lambda w: pl.BlockSpec((tq, w), lambda g, s, qt, kt: (qt[s], g))
    krow = lambda w: pl.BlockSpec((tq, w), lambda g, s, qt, kt: (kt[s], g))
    grid_spec = pltpu.PrefetchScalarGridSpec(
        num_scalar_prefetch=2, grid=(MLA_HEADS // hp, qi_tab.shape[0]),
        in_specs=[qrow(hp * LANES), krow(hp * LANES), krow(hp * MLA_V), qrow(hp * MLA_V), qrow(hp * MLA_V), qrow(hp * MLA_V)],
        out_specs=[pl.BlockSpec((t, hp * LANES), lambda g, s, qt, kt: (0, g)), krow(hp * LANES), krow(hp * MLA_V)],
        scratch_shapes=[pltpu.VMEM((hp, tq, LANES), F32), pltpu.VMEM((hp, tq, LANES), F32)],
    )
    return pl.pallas_call(
        body, name=name, grid_spec=grid_spec,
        out_shape=[jax.ShapeDtypeStruct((t, 1024), F32), jax.ShapeDtypeStruct((t, 1024), F32),
                   jax.ShapeDtypeStruct((t, 512), BF16)],
        compiler_params=_params(("parallel", "arbitrary")),
    )(qi_tab, ki_tab, q, k, v, do, lse, delta)


def _conv_rows(ext, tm, w_ref, sec):
    c0 = sec * 1024
    y = ext[CONV_HALO - 3:CONV_HALO - 3 + tm, c0:c0 + 1024] * w_ref[0:1, c0:c0 + 1024]
    for j in range(1, CONV_WIDTH):
        y = y + ext[CONV_HALO - 3 + j:CONV_HALO - 3 + j + tm, c0:c0 + 1024] * w_ref[j:j + 1, c0:c0 + 1024]
    return y


def _c_prep(proj_c, conv_w, a_log, dt_bias, *, name):
    t = proj_c.shape[0]
    tm = _tile(t, ROW_TILE)
    hb = tm // CONV_HALO

    def body(p_ref, halo_ref, ab_ref, w_ref, al_ref, dtb_ref, q_ref, k_ref, v_ref, g_ref, b_ref, gt_ref, ext):
        i = pl.program_id(0)
        ext[0:CONV_HALO, :] = jnp.where(i > 0, halo_ref[...], 0.0)
        ext[CONV_HALO:CONV_HALO + tm, :] = p_ref[...]
        for sec, o_ref in enumerate((q_ref, k_ref, v_ref)):
            y = _conv_rows(ext, tm, w_ref, sec)
            y = y * _sigmoid(y)
            if sec == 2:
                o_ref[...] = y
                continue
            scale = GDN_DK ** -0.5 if sec == 0 else 1.0
            for h in range(GDN_HEADS):
                sl = slice(h * LANES, (h + 1) * LANES)
                blk = y[:, sl]
                r = lax.rsqrt(jnp.sum(blk * blk, axis=-1, keepdims=True) + RMS_EPS)
                o_ref[:, sl] = blk * (r * scale)
        ab = ab_ref[...]
        g = -jnp.exp(al_ref[...]) * _softplus(ab + dtb_ref[...])
        beta = _sigmoid(ab)
        ri = lax.broadcasted_iota(jnp.int32, (tm, tm), 0)
        ci = lax.broadcasted_iota(jnp.int32, (tm, tm), 1)
        lower = ((ri // CHUNK) == (ci // CHUNK)) & (ri >= ci)
        gc = _dot(lower.astype(F32), g, NN, HI)
        eye = lax.broadcasted_iota(jnp.int32, (LANES, LANES), 0) == lax.broadcasted_iota(jnp.int32, (LANES, LANES), 1)
        gt_ref[...] = _dot(eye.astype(F32), gc, NT, HI)[0:GDN_HEADS, :]
        for h in range(GDN_HEADS):
            sl = slice(h * LANES, (h + 1) * LANES)
            g_ref[:, sl] = jnp.broadcast_to(gc[:, h:h + 1], (tm, LANES))
            b_ref[:, sl] = jnp.broadcast_to(beta[:, GDN_HEADS + h:GDN_HEADS + h + 1], (tm, LANES))

    row = lambda w: pl.BlockSpec((tm, w), lambda i: (i, 0))
    vec = lambda r, w: pl.BlockSpec((r, w), lambda i: (0, 0))
    out = jax.ShapeDtypeStruct((t, 1024), F32)
    return pl.pallas_call(
        body, name=name, grid=(t // tm,),
        in_specs=[row(3072), pl.BlockSpec((CONV_HALO, 3072), lambda i: (jnp.maximum(i * hb - 1, 0), 0)),
                  pl.BlockSpec((tm, LANES), lambda i: (i, 32)), vec(CONV_WIDTH, 3072), vec(1, LANES), vec(1, LANES)],
        out_specs=[row(1024)] * 5 + [pl.BlockSpec((GDN_HEADS, tm), lambda i: (0, i))],
        out_shape=[out] * 5 + [jax.ShapeDtypeStruct((GDN_HEADS, t), F32)],
        scratch_shapes=[pltpu.VMEM((tm + CONV_HALO, 3072), F32)],
        compiler_params=_params(("parallel",)),
    )(proj_c, proj_c, proj_c, conv_w, a_log, dt_bias)


def _c_prep_bwd(proj_c, conv_w, a_log, dt_bias, dq, dk, dv, dgb, dbb, dz, *, name):
    t = proj_c.shape[0]
    tm = _tile(t, ROW_TILE)
    hb = tm // CONV_HALO
    nt = t // tm
    rev = lambda i: nt - 1 - i

    def body(p_ref, halo_ref, ab_ref, w_ref, al_ref, dtb_ref, dq_ref, dk_ref, dv_ref, dg_ref, db_ref, dz_ref,
             dp_ref, dw_ref, dal_ref, ddt_ref, ext, dyext, carry, taps):
        step = pl.program_id(0)
        i = rev(step)

        @pl.when(step == 0)
        def _():
            dw_ref[...] = jnp.zeros_like(dw_ref)
            dal_ref[...] = jnp.zeros_like(dal_ref)
            ddt_ref[...] = jnp.zeros_like(ddt_ref)
            carry[...] = jnp.zeros_like(carry)

        ext[0:CONV_HALO, :] = jnp.where(i > 0, halo_ref[...], 0.0)
        ext[CONV_HALO:CONV_HALO + tm, :] = p_ref[...]
        for sec, g_ref in enumerate((dq_ref, dk_ref, dv_ref)):
            c0 = sec * 1024
            for j in range(CONV_WIDTH):
                taps[j] = ext[CONV_HALO - 3 + j:CONV_HALO - 3 + j + tm, c0:c0 + 1024]
            y = taps[0] * w_ref[0:1, c0:c0 + 1024]
            for j in range(1, CONV_WIDTH):
                y = y + taps[j] * w_ref[j:j + 1, c0:c0 + 1024]
            sg = _sigmoid(y)
            act = y * sg
            if sec == 2:
                dact = g_ref[...]
            else:
                scale = GDN_DK ** -0.5 if sec == 0 else 1.0
                parts = []
                for h in range(GDN_HEADS):
                    sl = slice(h * LANES, (h + 1) * LANES)
                    blk = act[:, sl]
                    r = lax.rsqrt(jnp.sum(blk * blk, axis=-1, keepdims=True) + RMS_EPS)
                    n = blk * r
                    dn = g_ref[:, sl] * scale
                    parts.append(r * (dn - n * jnp.sum(dn * n, axis=-1, keepdims=True)))
                dact = jnp.concatenate(parts, axis=-1)
            dy = dact * (sg * (1.0 + y * (1.0 - sg)))
            dyext[0:tm, c0:c0 + 1024] = dy
            for j in range(CONV_WIDTH):
                dw_ref[j:j + 1, c0:c0 + 1024] += jnp.sum(dy * taps[j], axis=0, keepdims=True)
        dyext[tm:tm + CONV_HALO, :] = carry[...]
        carry[...] = dyext[0:CONV_HALO, :]
        for sec in range(3):
            c0 = sec * 1024
            dx = dyext[3:3 + tm, c0:c0 + 1024] * w_ref[0:1, c0:c0 + 1024]
            for j in range(1, CONV_WIDTH):
                dx = dx + dyext[3 - j:3 - j + tm, c0:c0 + 1024] * w_ref[j:j + 1, c0:c0 + 1024]
            dp_ref[:, c0:c0 + 1024] = dx.astype(BF16)
        dp_ref[:, 3072:4096] = dz_ref[...]
        lane = lax.broadcasted_iota(jnp.int32, (tm, LANES), 1)
        dg = jnp.zeros((tm, LANES), F32)
        dbeta = jnp.zeros((tm, LANES), F32)
        for h in range(GDN_HEADS):
            sl = slice(h * LANES, (h + 1) * LANES)
            dg = dg + jnp.where(lane == h, dg_ref[:, sl], 0.0)
            dbeta = dbeta + jnp.where(lane == GDN_HEADS + h, db_ref[:, sl], 0.0)
        ri = lax.broadcasted_iota(jnp.int32, (tm, tm), 0)
        ci = lax.broadcasted_iota(jnp.int32, (tm, tm), 1)
        upper = ((ri // CHUNK) == (ci // CHUNK)) & (ri <= ci)
        dg = _dot(upper.astype(F32), dg, NN, HI)
        pre = ab_ref[...] + dtb_ref[...]
        s = _sigmoid(pre)
        a_exp = jnp.exp(al_ref[...])
        dg_da = dg * (-a_exp * s)
        dp_ref[:, 4096:IN_C_PAD] = (dg_da + dbeta * s * (1.0 - s)).astype(BF16)
        dal_ref[...] += jnp.sum(dg * (-a_exp * _softplus(pre)), axis=0, keepdims=True)
        ddt_ref[...] += jnp.sum(dg_da, axis=0, keepdims=True)

    row = lambda w: pl.BlockSpec((tm, w), lambda s: (rev(s), 0))
    vec = lambda r, w: pl.BlockSpec((r, w), lambda s: (0, 0))
    return pl.pallas_call(
        body, name=name, grid=(nt,),
        in_specs=[row(3072), pl.BlockSpec((CONV_HALO, 3072), lambda s: (jnp.maximum(rev(s) * hb - 1, 0), 0)),
                  pl.BlockSpec((tm, LANES), lambda s: (rev(s), 32)), vec(CONV_WIDTH, 3072), vec(1, LANES), vec(1, LANES),
                  row(1024), row(1024), row(1024), row(1024), row(1024), row(1024)],
        out_specs=[row(IN_C_PAD), vec(CONV_WIDTH, 3072), vec(1, LANES), vec(1, LANES)],
        out_shape=[jax.ShapeDtypeStruct((t, IN_C_PAD), BF16), jax.ShapeDtypeStruct((CONV_WIDTH, 3072), F32),
                   jax.ShapeDtypeStruct((1, LANES), F32), jax.ShapeDtypeStruct((1, LANES), F32)],
        scratch_shapes=[pltpu.VMEM((tm + CONV_HALO, 3072), F32), pltpu.VMEM((tm + CONV_HALO, 3072), F32),
                        pltpu.VMEM((CONV_HALO, 3072), F32), pltpu.VMEM((CONV_WIDTH, tm, 1024), F32)],
        compiler_params=_params(("arbitrary",)),
    )(proj_c, proj_c, proj_c, conv_w, a_log, dt_bias, dq, dk, dv, dgb, dbb, dz)


def _o_gate_bwd(dh, w, o, proj_c, o_norm, *, name):
    t = o.shape[0]
    tm = _tile(t, 2 * ROW_TILE)

    def body(dh_ref, w_ref, o_ref, z_ref, g_ref, do_ref, dz_ref, dg_ref, dy_ref):
        i = pl.program_id(0)

        @pl.when(i == 0)
        def _():
            dg_ref[...] = jnp.zeros_like(dg_ref)

        dy_ref[...] = _dot(dh_ref[...], w_ref[...], NT)
        dg = jnp.zeros((1, LANES), F32)
        for h in range(GDN_HEADS):
            sl = slice(h * LANES, (h + 1) * LANES)
            x = o_ref[:, sl]
            r = lax.rsqrt(jnp.mean(x * x, axis=-1, keepdims=True) + RMS_EPS)
            xh = x * r
            z = z_ref[:, sl]
            sg = _sigmoid(z)
            dyv = dy_ref[:, sl]
            dn = dyv * (z * sg)
            dz_ref[:, sl] = (dyv * xh * g_ref[...] * (sg * (1.0 + z * (1.0 - sg)))).astype(BF16)
            dxh = dn * g_ref[...]
            do_ref[:, sl] = r * (dxh - xh * jnp.mean(dxh * xh, axis=-1, keepdims=True))
            dg = dg + jnp.sum(dn * xh, axis=0, keepdims=True)
        dg_ref[...] += dg

    row = pl.BlockSpec((tm, 1024), lambda i: (i, 0))
    vec = pl.BlockSpec((1, LANES), lambda i: (0, 0))
    return pl.pallas_call(
        body, name=name, grid=(t // tm,),
        in_specs=[row, pl.BlockSpec(w.shape, lambda i: (0, 0)), row, pl.BlockSpec((tm, 1024), lambda i: (i, 3)), vec],
        out_specs=[row, row, vec],
        out_shape=[jax.ShapeDtypeStruct((t, 1024), F32), jax.ShapeDtypeStruct((t, 1024), BF16),
                   jax.ShapeDtypeStruct((1, LANES), F32)],
        scratch_shapes=[pltpu.VMEM((tm, 1024), F32)],
        compiler_params=_params(("arbitrary",)),
    )(dh, w, o, proj_c, o_norm)


PAIR = 2 * CHUNK
GDN_HP = 8


def _bdot(a, b, dims=NN):
    return _dot(a.astype(BF16), b.astype(BF16), dims)


def _each(f, *lists):
    return [f(*args) for args in zip(*lists)]


def _pair_common(q, k, v, gci, gcj, beta):
    ri = lax.broadcasted_iota(jnp.int32, (PAIR, PAIR), 0)
    ci = lax.broadcasted_iota(jnp.int32, (PAIR, PAIR), 1)
    same = (ri // CHUNK) == (ci // CHUNK)
    incl = same & (ri >= ci)
    strict = same & (ri > ci)
    eye = (ri == ci).astype(F32)
    first = lax.broadcasted_iota(jnp.int32, (PAIR, LANES), 0) < CHUNK
    gamma = _each(lambda gi, gj: jnp.where(incl, jnp.exp(jnp.minimum(gi - gj, 0.0)), 0.0), gci, gcj)
    kb = _each(jnp.multiply, k, beta)
    kk = _each(lambda a, b: _bdot(a, b, NT), kb, k)
    qk = _each(lambda a, b: _bdot(a, b, NT), q, k)
    m = _each(lambda x, g: jnp.where(strict, x * g, 0.0), kk, gamma)
    tm_ = _each(lambda x: eye - x, m)
    pw = _each(lambda x: _bdot(x, x), m)
    for it in range(5):
        tm_ = _each(lambda x, p: x + _bdot(x, p), tm_, pw)
        if it < 4:
            pw = _each(lambda p: _bdot(p, p), pw)
    eg = _each(jnp.exp, gci)
    vb = _each(jnp.multiply, v, beta)
    kbe = _each(jnp.multiply, kb, eg)
    uw = _each(lambda x, a, b: _bdot(x, jnp.concatenate([a, b], axis=1)), tm_, vb, kbe)
    attn = _each(lambda x, g: jnp.where(incl, x * g, 0.0), qk, gamma)
    gl_a = _each(lambda g: g[CHUNK - 1:CHUNK, :], gci)
    gl_b = _each(lambda g: g[PAIR - 1:PAIR, :], gci)
    ek = _each(lambda a, b, g: jnp.exp(jnp.where(first, a, b) - g), gl_a, gl_b, gci)
    return dict(incl=incl, strict=strict, gamma=gamma, kb=kb, m=m, tm=tm_, eg=eg, vb=vb, kbe=kbe,
                u=_each(lambda x: x[:, :LANES], uw), w=_each(lambda x: x[:, LANES:], uw), attn=attn,
                qd=_each(jnp.multiply, q, eg), ek=ek, kd=_each(jnp.multiply, k, ek),
                glast_a=_each(jnp.exp, gl_a), glast_b=_each(jnp.exp, gl_b))


def _gdn_specs(t, ts, order):
    nc = ts // CHUNK
    blk = pl.BlockSpec((ts, GDN_HP * LANES), lambda h, s: (order(s), h))
    row = pl.BlockSpec((GDN_HP, 1, ts), lambda h, s: (h, 0, order(s)))
    st = pl.BlockSpec((GDN_HP, nc, LANES, LANES), lambda h, s: (h, order(s), 0, 0))
    return blk, row, st


def _gdn_fwd(q, k, v, gcb, gct, bb, *, name):
    t = q.shape[0]
    ts = _tile(t, GDN_TILE)
    npair = ts // PAIR

    def body(q_ref, k_ref, v_ref, g_ref, gt_ref, b_ref, o_ref, st_ref, s_sc):
        @pl.when(pl.program_id(1) == 0)
        def _():
            s_sc[...] = jnp.zeros_like(s_sc)

        def pair(pi, _):
            rows = pl.ds(pl.multiple_of(pi * PAIR, PAIR), PAIR)
            heads = [slice(hh * LANES, (hh + 1) * LANES) for hh in range(GDN_HP)]
            c = CHUNK
            cat0 = lambda *xs: jnp.concatenate(xs, axis=0)
            s0 = [s_sc[hh] for hh in range(GDN_HP)]
            cm = _pair_common([q_ref[rows, sl] for sl in heads], [k_ref[rows, sl] for sl in heads],
                              [v_ref[rows, sl] for sl in heads], [g_ref[rows, sl] for sl in heads],
                              [gt_ref[hh, :, rows] for hh in range(GDN_HP)], [b_ref[rows, sl] for sl in heads])
            u, w, qd, kd = cm["u"], cm["w"], cm["qd"], cm["kd"]
            r0 = _each(lambda w_, q_, s: _bdot(cat0(w_[:c], q_[:c]), s), w, qd, s0)
            vn_a = _each(lambda u_, r: u_[:c] - r[:c], u, r0)
            s1 = _each(lambda s, gl, k_, vn: s * gl + _bdot(k_[:c], vn, TN), s0, cm["glast_a"], kd, vn_a)
            r1 = _each(lambda w_, q_, s: _bdot(cat0(w_[c:], q_[c:]), s), w, qd, s1)
            vn_b = _each(lambda u_, r: u_[c:] - r[:c], u, r1)
            s2 = _each(lambda s, gl, k_, vn: s * gl + _bdot(k_[c:], vn, TN), s1, cm["glast_b"], kd, vn_b)
            o = _each(lambda ra, rb, at, va, vb_: cat0(ra[c:], rb[c:]) + _bdot(at, cat0(va, vb_)),
                      r0, r1, cm["attn"], vn_a, vn_b)
            for hh, sl in enumerate(heads):
                st_ref[hh, 2 * pi] = s0[hh]
                st_ref[hh, 2 * pi + 1] = s1[hh]
                s_sc[hh] = s2[hh]
                o_ref[rows, sl] = o[hh]
            return 0

        lax.fori_loop(0, npair, pair, 0)

    blk, row, st = _gdn_specs(t, ts, lambda s: s)
    return pl.pallas_call(
        body, name=name, grid=(GDN_HEADS // GDN_HP, t // ts), in_specs=[blk, blk, blk, blk, row, blk],
        out_specs=[blk, st],
        out_shape=[jax.ShapeDtypeStruct((t, 1024), F32), jax.ShapeDtypeStruct((GDN_HEADS, t // CHUNK, LANES, LANES), F32)],
        scratch_shapes=[pltpu.VMEM((GDN_HP, LANES, LANES), F32)],
        compiler_params=_params(("parallel", "arbitrary")),
    )(q, k, v, gcb, gct, bb)


def _gdn_bwd(q, k, v, gcb, gct, bb, do, states, *, name):
    t = q.shape[0]
    ts = _tile(t, GDN_TILE)
    npair = ts // PAIR
    ns = t // ts
    c = CHUNK

    def body(q_ref, k_ref, v_ref, g_ref, gt_ref, b_ref, do_ref, st_ref, dq_ref, dk_ref, dv_ref, dg_ref, db_ref, ds_sc):
        @pl.when(pl.program_id(1) == 0)
        def _():
            ds_sc[...] = jnp.zeros_like(ds_sc)

        rowsum = lambda x: jnp.sum(x, axis=-1, keepdims=True)
        total = lambda x: jnp.sum(rowsum(x), axis=0, keepdims=True)
        cat0 = lambda *xs: jnp.concatenate(xs, axis=0)
        cat1 = lambda *xs: jnp.concatenate(xs, axis=1)

        def pair(step, _):
            pi = npair - 1 - step
            rows = pl.ds(pl.multiple_of(pi * PAIR, PAIR), PAIR)
            heads = [slice(hh * LANES, (hh + 1) * LANES) for hh in range(GDN_HP)]
            hs = range(GDN_HP)
            qv, kv, vv = ([r[rows, sl] for sl in heads] for r in (q_ref, k_ref, v_ref))
            beta = [b_ref[rows, sl] for sl in heads]
            dov = [do_ref[rows, sl] for sl in heads]
            s0 = [st_ref[hh, 2 * pi] for hh in hs]
            s1 = [st_ref[hh, 2 * pi + 1] for hh in hs]
            ds2 = [ds_sc[hh] for hh in hs]
            cm = _pair_common(qv, kv, vv, [g_ref[rows, sl] for sl in heads], [gt_ref[hh, :, rows] for hh in hs], beta)
            u, w, qd, kd, attn = cm["u"], cm["w"], cm["qd"], cm["kd"], cm["attn"]
            tmat, gamma, eg = cm["tm"], cm["gamma"], cm["eg"]
            incl, strict = cm["incl"], cm["strict"]
            vn_a = _each(lambda u_, w_, s: u_[:c] - _bdot(w_[:c], s), u, w, s0)
            vn_b = _each(lambda u_, w_, s: u_[c:] - _bdot(w_[c:], s), u, w, s1)
            vn = _each(cat0, vn_a, vn_b)
            dvn_att = _each(lambda a, d: _bdot(a, d, TN), attn, dov)
            dattn = _each(lambda d, v_: jnp.where(incl, _bdot(d, v_, NT), 0.0), dov, vn)
            dvn_b = _each(lambda x, k_, d: x[c:] + _bdot(k_[c:], d), dvn_att, kd, ds2)
            rb = _each(lambda d, x, s: _bdot(cat0(d[c:], x), s, NT), dov, dvn_b, s1)
            dkd_b = _each(lambda v_, d: _bdot(v_, d, NT), vn_b, ds2)
            dgl_b = _each(lambda d, s: total(d * s), ds2, s1)
            ds1 = _each(lambda d, gl, q_, w_, o_, x: d * gl + _bdot(cat0(q_[c:], w_[c:]), cat0(o_[c:], -x), TN),
                        ds2, cm["glast_b"], qd, w, dov, dvn_b)
            dvn_a = _each(lambda x, k_, d: x[:c] + _bdot(k_[:c], d), dvn_att, kd, ds1)
            ra = _each(lambda d, x, s: _bdot(cat0(d[:c], x), s, NT), dov, dvn_a, s0)
            dkd_a = _each(lambda v_, d: _bdot(v_, d, NT), vn_a, ds1)
            dgl_a = _each(lambda d, s: total(d * s), ds1, s0)
            ds0 = _each(lambda d, gl, q_, w_, o_, x: d * gl + _bdot(cat0(q_[:c], w_[:c]), cat0(o_[:c], -x), TN),
                        ds1, cm["glast_a"], qd, w, dov, dvn_a)
            dvn = _each(cat0, dvn_a, dvn_b)
            dqd = _each(lambda a, b: cat0(a[:c], b[:c]), ra, rb)
            dw = _each(lambda a, b: -cat0(a[c:], b[c:]), ra, rb)
            dkd = _each(cat0, dkd_a, dkd_b)
            dvw = _each(cat1, dvn, dw)
            dvbk = _each(lambda t_, x: _bdot(t_, x, TN), tmat, dvw)
            dvb = _each(lambda x: x[:, :LANES], dvbk)
            dkbe = _each(lambda x: x[:, LANES:], dvbk)
            dt_ = _each(lambda x, a, b: _bdot(x, cat1(a, b), NT), dvw, cm["vb"], cm["kbe"])
            da1 = _each(lambda t_, x: _bdot(t_, x, TN), tmat, dt_)
            dm = _each(lambda x, t_: jnp.where(strict, -_bdot(x, t_, NT), 0.0), da1, tmat)
            dkk = _each(jnp.multiply, dm, gamma)
            dqk = _each(jnp.multiply, dattn, gamma)
            z = _each(lambda a, b, c_, d: a * b + c_ * d, dm, cm["m"], dattn, attn)
            dkb = _each(lambda x, k_, y, e: _bdot(x, k_) + y * e, dkk, kv, dkbe, eg)
            dk = _each(lambda a, b, kb_, q_, x, e, y, be: _bdot(cat0(a, b), cat0(kb_, q_), TN) + x * e + y * be,
                       dkk, dqk, cm["kb"], qv, dkd, cm["ek"], dkb, beta)
            dq = _each(lambda x, k_, y, e: _bdot(x, k_) + y * e, dqk, kv, dqd, eg)

            def colsum_of(z_):
                zh = z_.astype(BF16)
                zl = (z_ - zh.astype(F32)).astype(BF16)
                return _dot(cat0(zh, zl), jnp.ones((2 * PAIR, LANES), BF16), TN)

            colsum = _each(colsum_of, z)
            ri = lax.broadcasted_iota(jnp.int32, (PAIR, LANES), 0)
            for hh, sl in enumerate(heads):
                dkd_kd = dkd[hh] * kd[hh]
                dgc = (rowsum(z[hh]) - colsum[hh] + rowsum(dqd[hh] * qd[hh]) - rowsum(dkd_kd)
                       + rowsum(dkbe[hh] * cm["kbe"][hh]))
                last_a = total(dkd_kd[:c]) + dgl_a[hh] * cm["glast_a"][hh]
                last_b = total(dkd_kd[c:]) + dgl_b[hh] * cm["glast_b"][hh]
                dgc = dgc + jnp.where(ri == c - 1, last_a, 0.0) + jnp.where(ri == PAIR - 1, last_b, 0.0)
                ds_sc[hh] = ds0[hh]
                dq_ref[rows, sl] = dq[hh]
                dk_ref[rows, sl] = dk[hh]
                dv_ref[rows, sl] = dvb[hh] * beta[hh]
                db_ref[rows, sl] = jnp.broadcast_to(rowsum(dkb[hh] * kv[hh]) + rowsum(dvb[hh] * vv[hh]), (PAIR, LANES))
                dg_ref[rows, sl] = dgc
            return 0

        lax.fori_loop(0, npair, pair, 0)

    blk, row, st = _gdn_specs(t, ts, lambda s: ns - 1 - s)
    out = jax.ShapeDtypeStruct((t, 1024), F32)
    return pl.pallas_call(
        body, name=name, grid=(GDN_HEADS // GDN_HP, ns), in_specs=[blk, blk, blk, blk, row, blk, blk, st],
        out_specs=[blk] * 5, out_shape=[out] * 5, scratch_shapes=[pltpu.VMEM((GDN_HP, LANES, LANES), F32)],
        compiler_params=_params(("parallel", "arbitrary")),
    )(q, k, v, gcb, gct, bb, do, states)


def _gate_out_proj_loss(o, proj_c, o_norm, w, hres, g, target, *, name):
    t, d = hres.shape
    tm = _tile(t, 2 * ROW_TILE)

    def body(o_ref, z_ref, on_ref, w_ref, h_ref, g_ref, t_ref, dh_ref, dhb_ref, dg_ref, loss_ref, dw_ref, y_ref):
        i = pl.program_id(0)
        for hd in range(GDN_HEADS):
            sl = slice(hd * LANES, (hd + 1) * LANES)
            ov = o_ref[:, sl]
            rr = lax.rsqrt(jnp.mean(ov * ov, axis=-1, keepdims=True) + RMS_EPS)
            z = z_ref[:, sl]
            y_ref[:, sl] = (ov * rr * on_ref[...] * (z * _sigmoid(z))).astype(BF16)
        x = h_ref[...] + _dot(y_ref[...], w_ref[...])
        r = lax.rsqrt(jnp.mean(x * x, axis=-1, keepdims=True) + RMS_EPS)
        xh = x * r
        err = xh * g_ref[...] - t_ref[...]
        dy = err * (1.0 / d)
        dxh = dy * g_ref[...]
        dh = r * (dxh - xh * jnp.mean(dxh * xh, axis=-1, keepdims=True))
        dh_ref[...] = dh
        dhb_ref[...] = dh.astype(BF16)

        @pl.when(i == 0)
        def _():
            dg_ref[...] = jnp.zeros_like(dg_ref)
            loss_ref[...] = jnp.zeros_like(loss_ref)
            dw_ref[...] = jnp.zeros_like(dw_ref)

        dg_ref[...] += jnp.sum(dy * xh, axis=0, keepdims=True)
        part = 0.5 * jnp.sum(jnp.mean(err * err, axis=-1, keepdims=True), axis=0, keepdims=True)
        loss_ref[...] += jnp.broadcast_to(part, loss_ref.shape)
        dw_ref[...] += _dot(y_ref[...], dhb_ref[...], TN)

    row = pl.BlockSpec((tm, d), lambda i: (i, 0))
    vec = pl.BlockSpec((1, d), lambda i: (0, 0))
    return pl.pallas_call(
        body, name=name, grid=(t // tm,),
        in_specs=[row, pl.BlockSpec((tm, 1024), lambda i: (i, 3)), pl.BlockSpec((1, LANES), lambda i: (0, 0)),
                  pl.BlockSpec(w.shape, lambda i: (0, 0)), row, vec, row],
        out_specs=[row, row, vec, pl.BlockSpec((8, LANES), lambda i: (0, 0)), pl.BlockSpec(w.shape, lambda i: (0, 0))],
        out_shape=[jax.ShapeDtypeStruct((t, d), F32), jax.ShapeDtypeStruct((t, d), BF16),
                   jax.ShapeDtypeStruct((1, d), F32), jax.ShapeDtypeStruct((8, LANES), F32),
                   jax.ShapeDtypeStruct(w.shape, F32)],
        scratch_shapes=[pltpu.VMEM((tm, d), BF16)],
        compiler_params=_params(("arbitrary",)),
    )(o, proj_c, o_norm, w, hres, g, target)


def _pad_cols(w, n):
    return jnp.pad(w, ((0, 0), (0, n - w.shape[1])))


def _layout_odd(w):
    return dict(
        winc=_pad_cols(w["w_in_c"], IN_C_PAD).astype(BF16), wout_c=w["w_out_c"].astype(BF16), conv_w=w["conv_w"],
        a_log=_pad_cols(w["a_log"], LANES), dt_bias=_pad_cols(w["dt_bias"], LANES),
        norm_c=w["norm_c"], o_norm=w["o_norm"], final_norm=w["final_norm"],
    )


def _layout_in_ab(w):
    z = lambda r, c: jnp.zeros((r, c), w["w_in_ab"].dtype)
    wi = w["w_in_ab"]
    win = jnp.concatenate([wi[:, :384], z(1024, 64), wi[:, 384:416], z(1024, 32), wi[:, 416:]], axis=1)
    pw = w["pool_w"]
    rows = []
    for g in range(4):
        rows.append(jnp.concatenate([pw[g] if j == g else jnp.zeros((128, 128), F32) for j in range(4)], axis=1))
    wpool = jnp.concatenate(rows, axis=0)
    half = MLA_ROPE // 2
    inv = 1.0 / (ROPE_THETA ** (jnp.arange(half, dtype=F32) / half))
    inv_lane = jnp.concatenate([jnp.zeros((MLA_NOPE,), F32), inv, inv, jnp.zeros((32,), F32)]).reshape(1, LANES)
    return dict(win=win.astype(BF16), wpool=wpool.astype(BF16), inv_lane=inv_lane, norm_ab=w["norm_ab"],
                q_a_norm=w["q_a_norm"], kv_a_norm=w["kv_a_norm"], pool_scale=w["pool_scale"])


def _layout_mid(w):
    wq = jnp.pad(w["w_q_b"].reshape(MLA_Q_RANK, MLA_HEADS, 96), ((0, 0), (0, 0), (0, 32))).reshape(MLA_Q_RANK, 1024)
    kv3 = w["w_kv_b"].reshape(MLA_KV_RANK, MLA_HEADS, 128)
    wk = jnp.pad(kv3[..., :MLA_NOPE], ((0, 0), (0, 0), (0, 64))).reshape(MLA_KV_RANK, 1024)
    wv = kv3[..., MLA_NOPE:].reshape(MLA_KV_RANK, 512)
    return dict(wq=wq.astype(BF16), wk=wk.astype(BF16), wv=wv.astype(BF16), wout_ab=w["w_out_ab"].astype(BF16))


def _unlayout_grads(g, names):
    out = {}
    for name in names:
        if name == "w_in_ab":
            dwin = g["win"]
            out[name] = jnp.concatenate([dwin[:384], dwin[448:480], dwin[512:]], axis=0)
        elif name == "w_q_b":
            out[name] = g["wq"].reshape(MLA_Q_RANK, MLA_HEADS, 128)[..., :96].reshape(MLA_Q_RANK, 768)
        elif name == "w_kv_b":
            out[name] = jnp.concatenate([g["wk"].reshape(MLA_KV_RANK, MLA_HEADS, 128)[..., :MLA_NOPE],
                                         g["wv"].reshape(MLA_KV_RANK, MLA_HEADS, MLA_V)], axis=-1).reshape(MLA_KV_RANK, 1024)
        elif name == "w_in_c":
            out[name] = g["winc"][:4112]
        else:
            out[name] = g[{"w_out_ab": "wout_ab", "w_out_c": "wout_c"}[name]]
    return out


def _local_step(x, pos, target, lw, more_weights=None, on_grads=None):
    mm = _matmul
    proj, hn = _rms_in_proj(x, lw["norm_ab"], lw["win"], name="rms_in_ab")
    if more_weights is not None:
        lw = {**lw, **more_weights("mid", proj)}
    q, k, v, ybraw, qn, kvn, d, cos_t, sin_t = _ab_prep(
        proj, pos, lw["inv_lane"], lw["q_a_norm"], lw["kv_a_norm"], lw["wq"], lw["wk"], lw["wv"], lw["wpool"], name="ab_prep")
    o, lse = _attn_fwd(q, k, v, name="attn_fwd")
    h1, y = _gate_out_proj(o, ybraw, proj, lw["pool_scale"], lw["wout_ab"], x, name="gate_out_ab")
    lo = lw if more_weights is None else more_weights("odd", h1)
    proj_c, hn1 = _rms_in_proj(h1, lo["norm_c"], lo["winc"], name="rms_in_c")
    q2, k2, v2, gb, bb, gt = _c_prep(proj_c, lo["conv_w"], lo["a_log"], lo["dt_bias"], name="c_prep")
    gt = gt.reshape(GDN_HEADS, 1, gt.shape[1])
    o2, states = _gdn_fwd(q2, k2, v2, gb, gt, bb, name="gdn_fwd")
    dh2, dh2b, d_final, loss, d_wout_c = _gate_out_proj_loss(
        o2, proj_c, lo["o_norm"], lo["wout_c"], h1, lo["final_norm"], target, name="gate_out_c_loss")
    g = {"final_norm": d_final, "wout_c": d_wout_c, "loss": loss}
    do2, dz2, g["o_norm"] = _o_gate_bwd(dh2b, lo["wout_c"], o2, proj_c, lo["o_norm"], name="gate_c_bwd")
    dq2, dk2, dv2, dgb, dbb = _gdn_bwd(q2, k2, v2, gb, gt, bb, do2, states, name="gdn_bwd")
    dproj_c, g["conv_w"], g["a_log"], g["dt_bias"] = _c_prep_bwd(
        proj_c, lo["conv_w"], lo["a_log"], lo["dt_bias"], dq2, dk2, dv2, dgb, dbb, dz2, name="c_prep_bwd")
    g["winc"] = mm(dproj_c, hn1, "tn", name="in_c_dw")
    notify = (lambda tag, after=None: 0.0) if on_grads is None else (lambda tag, after=None: on_grads(tag, g, after))
    dh1, g["norm_c"], dh1b, g["wout_ab"] = _matmul_rms_bwd(
        dproj_c, lo["winc"], h1, lo["norm_c"] + notify("odd"), dh2, name="in_c_dx_rms", prev_y=y)
    pool_scale = lw["pool_scale"] + notify("odd_go", dh1b) + notify("out_ab")
    do, delta, dyb, dz, g["pool_scale"] = _gate_bwd(dh1b, lw["wout_ab"], o, ybraw, proj, pool_scale, name="gate_ab_bwd")
    dq, dk, dv = _attn_bwd(q, k, v, do, lse, delta, name="attn_bwd")
    dproj, g["q_a_norm"], g["kv_a_norm"], g["wq"], g["wk"], g["wv"], g["wpool"], g["win"] = _ab_prep_bwd(
        proj, lw["q_a_norm"], lw["kv_a_norm"], dq, dk, dv, cos_t, sin_t, dyb, dz, qn, kvn, d, hn,
        lw["wq"], lw["wk"], lw["wv"], lw["wpool"], name="ab_prep_bwd")
    norm_ab = lw["norm_ab"] + notify("in_ab")
    dx, g["norm_ab"] = _matmul_rms_bwd(dproj, lw["win"], x, norm_ab, dh1, name="in_ab_dx_rms")
    return loss, dx, g


_HBM = pl.BlockSpec(memory_space=pltpu.HBM)


def _place():
    return lax.axis_index("x"), lax.axis_index("y"), lax.axis_index("c")


def _flip(v, f):
    return 1 - v if f else v


_CHIP_FLIPS = ((1, 0), (0, 1), (1, 1))
_DEV_FLIPS = tuple((fx, fy, fc) for fx in (0, 1) for fy in (0, 1) for fc in (0, 1) if fx or fy or fc)


def _rcopy(src, dst, send_sems, recv_sems, k, to):
    return pltpu.make_async_remote_copy(src_ref=src, dst_ref=dst, send_sem=send_sems.at[k], recv_sem=recv_sems.at[k],
                                        device_id=to, device_id_type=MESH)


def _my_half(ref, c, axis):
    rh = ref.shape[axis] // 2
    idx = [slice(None)] * len(ref.shape)
    idx[axis] = pl.ds(c * rh, rh)
    return ref.at[tuple(idx)]


def _gather_weights(bigs, smalls):
    nb, ns = len(bigs), len(smalls)

    def body(*refs):
        ins, outs = refs[:nb + ns], refs[nb + ns:2 * (nb + ns)]
        send_sems, recv_sems, local_sems = refs[2 * (nb + ns):]
        x, y, c = _place()
        j0 = 2 * x + y
        sib = (x, y, 1 - c)
        chips = [(_flip(x, fx), _flip(y, fy)) for fx, fy in _CHIP_FLIPS]
        local = [pltpu.make_async_copy(i_ref, o_ref.at[j0], local_sems.at[a])
                 for a, (i_ref, o_ref) in enumerate(zip(ins, outs))]
        for cp in local:
            cp.start()
        sends = []
        for k, (px, py) in enumerate(chips):
            for a in range(nb):
                sends.append(_rcopy(_my_half(ins[a], c, 0), _my_half(outs[a].at[j0], c, 0), send_sems, recv_sems,
                                    6 * a + k, (px, py, c)))
            for s in range(ns):
                sends.append(_rcopy(ins[nb + s], outs[nb + s].at[j0], send_sems, recv_sems, 6 * nb + 3 * s + k, (px, py, c)))
        for cp in sends:
            cp.start()
        for k, (px, py) in enumerate(chips):
            jk = 2 * px + py
            for a in range(nb):
                landed = _my_half(outs[a].at[jk], c, 0)
                _rcopy(landed, landed, send_sems, recv_sems, 6 * a + k, (px, py, c)).wait_recv()
                fwd = _rcopy(landed, landed, send_sems, recv_sems, 6 * a + 3 + k, sib)
                fwd.start()
                sends.append(fwd)
        for k, (px, py) in enumerate(chips):
            jk = 2 * px + py
            for a in range(nb):
                other = _my_half(outs[a].at[jk], 1 - c, 0)
                _rcopy(other, other, send_sems, recv_sems, 6 * a + 3 + k, sib).wait_recv()
            for s in range(ns):
                _rcopy(ins[nb + s], outs[nb + s].at[jk], send_sems, recv_sems, 6 * nb + 3 * s + k, (px, py, c)).wait_recv()
        for cp in sends:
            cp.wait_send()
        for cp in local:
            cp.wait()

    arrays = list(bigs) + list(smalls)
    n_sem = 6 * nb + 3 * ns
    return pl.pallas_call(
        body, name="gather_weights", in_specs=[_HBM] * len(arrays), out_specs=[_HBM] * len(arrays),
        out_shape=[jax.ShapeDtypeStruct((4,) + a.shape, a.dtype) for a in arrays],
        scratch_shapes=[pltpu.SemaphoreType.DMA((n_sem,)), pltpu.SemaphoreType.DMA((n_sem,)),
                        pltpu.SemaphoreType.DMA((len(arrays),))],
    )(*arrays)


def _core_swap_partial(gs, by_cols, *, name):
    n = len(gs)

    def body(*refs):
        ins, outs = refs[:n], refs[n:2 * n]
        send_sems, recv_sems = refs[2 * n:]
        x, y, c = _place()
        copies = [_rcopy(_my_half(i_ref, 1 - c, 2 if by_cols[a] else 1), o_ref, send_sems, recv_sems, a, (x, y, 1 - c))
                  for a, (i_ref, o_ref) in enumerate(zip(ins, outs))]
        for cp in copies:
            cp.start()
        for cp in copies:
            cp.wait()

    halved = lambda g, cols: (4, g.shape[1], g.shape[2] // 2) if cols else (4, g.shape[1] // 2, g.shape[2])
    return pl.pallas_call(
        body, name=name, in_specs=[_HBM] * n, out_specs=[_HBM] * n,
        out_shape=[jax.ShapeDtypeStruct(halved(g, cols), g.dtype) for g, cols in zip(gs, by_cols)],
        scratch_shapes=[pltpu.SemaphoreType.DMA((n,)), pltpu.SemaphoreType.DMA((n,))],
    )(*gs)


def _core_swap_partial_start(gs, by_cols, *, name):
    n = len(gs)
    halved = lambda g, cols: (4, g.shape[1], g.shape[2] // 2) if cols else (4, g.shape[1] // 2, g.shape[2])
    lands = [lax.empty(halved(g, cols), g.dtype) for g, cols in zip(gs, by_cols)]

    def body(*refs):
        ins, land_refs, send_sems, recv_sems, token = refs[:n], refs[n:2 * n], refs[2 * n], refs[2 * n + 1], refs[-1]
        x, y, c = _place()
        for a in range(n):
            _rcopy(_my_half(ins[a], 1 - c, 2 if by_cols[a] else 1), land_refs[a], send_sems, recv_sems, a,
                   (x, y, 1 - c)).start()
        token[...] = jnp.zeros_like(token)

    held = [pltpu.with_memory_space_constraint(a, pltpu.HBM) for a in list(gs) + lands]
    return pl.pallas_call(
        body, name=name, in_specs=[_HBM] * (2 * n),
        out_specs=(_SEM, _SEM, *[_HBM] * (2 * n), pl.BlockSpec(memory_space=pltpu.VMEM)),
        out_shape=(pltpu.SemaphoreType.DMA((n,)), pltpu.SemaphoreType.DMA((n,)),
                   *[pltpu.HBM(a.shape, a.dtype) for a in held], jax.ShapeDtypeStruct((8, LANES), F32)),
        input_output_aliases={i: 2 + i for i in range(2 * n)},
        compiler_params=pltpu.CompilerParams(has_side_effects=_DATAFLOW),
    )(*held)


def _core_swap_partial_wait(started, after, by_cols, *, name):
    send_sems, recv_sems, held = started[0], started[1], started[2:-1]
    n = len(held) // 2

    def body(*refs):
        ins, land_refs, s_sems, r_sems = refs[:n], refs[n:2 * n], refs[2 * n], refs[2 * n + 1]
        x, y, c = _place()
        for a in range(n):
            cp = _rcopy(_my_half(ins[a], 1 - c, 2 if by_cols[a] else 1), land_refs[a], s_sems, r_sems, a, (x, y, 1 - c))
            cp.wait_send()
            cp.wait_recv()

    out = pl.pallas_call(
        body, name=name, in_specs=[_HBM] * (2 * n) + [_SEM, _SEM, _ANY], out_specs=[_HBM] * (2 * n),
        out_shape=[pltpu.HBM(a.shape, a.dtype) for a in held],
        input_output_aliases={i: i for i in range(2 * n)},
        compiler_params=pltpu.CompilerParams(has_side_effects=_DATAFLOW),
    )(*held, send_sems, recv_sems, after)
    return out[:n], out[n:]


def _core_swap_sum(fs, by_cols):
    n = len(fs)

    def body(*refs):
        ins, outs = refs[:n], refs[n:2 * n]
        send_sems, recv_sems = refs[2 * n:]
        x, y, c = _place()
        axes = [1 if cols else 0 for cols in by_cols]
        copies = [_rcopy(_my_half(i_ref, c, ax), _my_half(o_ref, c, ax), send_sems, recv_sems, a, (x, y, 1 - c))
                  for a, (i_ref, o_ref, ax) in enumerate(zip(ins, outs, axes))]
        for cp in copies:
            cp.start()
        for a, cp in enumerate(copies):
            cp.wait_send()
            theirs = _my_half(outs[a], 1 - c, axes[a])
            _rcopy(theirs, theirs, send_sems, recv_sems, a, (x, y, 1 - c)).wait_recv()

    return pl.pallas_call(
        body, name="core_swap_sum", in_specs=[_HBM] * n, out_specs=[_HBM] * n,
        out_shape=[jax.ShapeDtypeStruct(f.shape, f.dtype) for f in fs],
        input_output_aliases={a: a for a in range(n)},
        scratch_shapes=[pltpu.SemaphoreType.DMA((n,)), pltpu.SemaphoreType.DMA((n,))],
    )(*fs)


_SEM = pl.BlockSpec(memory_space=pltpu.SEMAPHORE)
_ANY = pl.BlockSpec(memory_space=pl.ANY)
_DATAFLOW = pltpu.SideEffectType.DATAFLOW_SIDE_EFFECTING


def _peer_flips(to_all):
    return _DEV_FLIPS if to_all else tuple((fx, fy, 0) for fx, fy in _CHIP_FLIPS)


def _to_chips_copies(srcs, lands, send_sems, recv_sems, per_chip_slot, to_all=False):
    x, y, c = _place()
    flips = _peer_flips(to_all)
    index = (lambda px, py, pc: 4 * px + 2 * py + pc) if to_all else (lambda px, py, pc: 2 * px + py)
    me = index(x, y, c)
    out = []
    for k, (fx, fy, fc) in enumerate(flips):
        peer = (_flip(x, fx), _flip(y, fy), _flip(c, fc))
        theirs = index(*peer)
        for a, (src, land) in enumerate(zip(srcs, lands)):
            piece = src.at[theirs] if per_chip_slot else src
            out.append((_rcopy(piece, land.at[me], send_sems, recv_sems, len(flips) * a + k, peer),
                        _rcopy(piece, land.at[theirs], send_sems, recv_sems, len(flips) * a + k, peer)))
    return out


def _to_chips_start(arrays, *, per_chip_slot, name, after=None, to_all=False):
    n = len(arrays)
    peers = len(_peer_flips(to_all))
    lands = [lax.empty((peers + 1,) + (a.shape[1:] if per_chip_slot else a.shape), a.dtype) for a in arrays]
    extra = [] if after is None else [after]

    def body(*refs):
        srcs, land_refs, token = refs[:n], refs[n:2 * n], refs[-1]
        send_sems, recv_sems = refs[2 * n + len(extra)], refs[2 * n + len(extra) + 1]
        for send, _ in _to_chips_copies(srcs, land_refs, send_sems, recv_sems, per_chip_slot, to_all):
            send.start()
        token[...] = jnp.zeros_like(token)

    held = [pltpu.with_memory_space_constraint(a, pltpu.HBM) for a in list(arrays) + lands]
    return pl.pallas_call(
        body, name=name, in_specs=[_HBM] * (2 * n) + [_ANY] * len(extra),
        out_specs=(_SEM, _SEM, *[_HBM] * (2 * n), pl.BlockSpec(memory_space=pltpu.VMEM)),
        out_shape=(pltpu.SemaphoreType.DMA((peers * n,)), pltpu.SemaphoreType.DMA((peers * n,)),
                   *[pltpu.HBM(a.shape, a.dtype) for a in held], jax.ShapeDtypeStruct((8, LANES), F32)),
        input_output_aliases={i: 2 + i for i in range(2 * n)},
        compiler_params=pltpu.CompilerParams(has_side_effects=_DATAFLOW),
    )(*held, *extra)


def _to_chips_wait(started, after, *, per_chip_slot, name, to_all=False):
    send_sems, recv_sems, held = started[0], started[1], started[2:-1]
    n = len(held) // 2

    def body(*refs):
        srcs, land_refs, s_sems, r_sems = refs[:n], refs[n:2 * n], refs[2 * n], refs[2 * n + 1]
        for send, arrival in _to_chips_copies(srcs, land_refs, s_sems, r_sems, per_chip_slot, to_all):
            send.wait_send()
            arrival.wait_recv()

    out = pl.pallas_call(
        body, name=name, in_specs=[_HBM] * (2 * n) + [_SEM, _SEM, _ANY], out_specs=[_HBM] * (2 * n),
        out_shape=[pltpu.HBM(a.shape, a.dtype) for a in held],
        input_output_aliases={i: i for i in range(2 * n)},
        compiler_params=pltpu.CompilerParams(has_side_effects=_DATAFLOW),
    )(*held, send_sems, recv_sems, after)
    return out[n:]


def _chip_exchange(ps, small):
    n = len(ps)
    rs = small.shape[0]

    def body(*refs):
        p_refs, s_ref = refs[:n], refs[n]
        l_refs, ls_ref = refs[n + 1:2 * n + 1], refs[2 * n + 1]
        send_sems, recv_sems, local_sems = refs[2 * n + 2:]
        x, y, c = _place()
        j0 = 2 * x + y
        d0 = 2 * j0 + c
        local = [pltpu.make_async_copy(p.at[j0], l.at[j0], local_sems.at[a]) for a, (p, l) in enumerate(zip(p_refs, l_refs))]
        local.append(pltpu.make_async_copy(s_ref, ls_ref.at[d0], local_sems.at[n]))
        for cp in local:
            cp.start()
        sends = []
        for k, (fx, fy) in enumerate(_CHIP_FLIPS):
            px, py = _flip(x, fx), _flip(y, fy)
            for a in range(n):
                sends.append(_rcopy(p_refs[a].at[2 * px + py], l_refs[a].at[j0], send_sems, recv_sems, 3 * a + k, (px, py, c)))
        for k, (fx, fy, fc) in enumerate(_DEV_FLIPS):
            peer = (_flip(x, fx), _flip(y, fy), _flip(c, fc))
            sends.append(_rcopy(s_ref, ls_ref.at[d0], send_sems, recv_sems, 3 * n + k, peer))
        for cp in sends:
            cp.start()
        for k, (fx, fy) in enumerate(_CHIP_FLIPS):
            px, py = _flip(x, fx), _flip(y, fy)
            for a in range(n):
                _rcopy(p_refs[a].at[j0], l_refs[a].at[2 * px + py], send_sems, recv_sems, 3 * a + k, (px, py, c)).wait_recv()
        for k, (fx, fy, fc) in enumerate(_DEV_FLIPS):
            px, py, pc = _flip(x, fx), _flip(y, fy), _flip(c, fc)
            _rcopy(s_ref, ls_ref.at[4 * px + 2 * py + pc], send_sems, recv_sems, 3 * n + k, (px, py, pc)).wait_recv()
        for cp in sends:
            cp.wait_send()
        for cp in local:
            cp.wait()

    n_sem = 3 * n + 7
    return pl.pallas_call(
        body, name="chip_exchange", in_specs=[_HBM] * (n + 1), out_specs=[_HBM] * (n + 1),
        out_shape=[jax.ShapeDtypeStruct(p.shape, F32) for p in ps] + [jax.ShapeDtypeStruct((8, rs, LANES), F32)],
        scratch_shapes=[pltpu.SemaphoreType.DMA((n_sem,)), pltpu.SemaphoreType.DMA((n_sem,)),
                        pltpu.SemaphoreType.DMA((n + 1,))],
    )(*ps, small)


def _half_blocks(rows, cols, by_cols):
    if by_cols:
        tc = _tile(cols // 2, 256)
        nb = cols // 2 // tc
        return rows, tc, nb, (lambda i, c: (0, c * nb + i))
    tr = _tile(rows // 2, 256)
    nb = rows // 2 // tr
    return tr, cols, nb, (lambda i, c: (c * nb + i, 0))


def _core_sum(g, part, core, *, name, by_cols):
    _, rows, cols = g.shape
    br, bc, nb, whole = _half_blocks(rows, cols, by_cols)
    mine = (lambda i: (0, i)) if by_cols else (lambda i: (i, 0))

    def body(c_ref, g_ref, p_ref, o_ref):
        o_ref[...] = g_ref[...] + p_ref[...]

    grid_spec = pltpu.PrefetchScalarGridSpec(
        num_scalar_prefetch=1, grid=(4, nb),
        in_specs=[pl.BlockSpec((1, br, bc), lambda j, i, c: (j,) + whole(i, c[0])),
                  pl.BlockSpec((1, br, bc), lambda j, i, c: (j,) + mine(i))],
        out_specs=pl.BlockSpec((1, br, bc), lambda j, i, c: (j,) + mine(i)),
    )
    return pl.pallas_call(
        body, name=name, grid_spec=grid_spec, out_shape=jax.ShapeDtypeStruct(part.shape, F32),
        compiler_params=_params(("parallel", "parallel")),
    )(core, g, part)


def _chip_sum(landed, part, place, *, name, by_cols):
    _, hr, hc = landed.shape
    rows, cols = (hr, 2 * hc) if by_cols else (2 * hr, hc)
    br, bc, nb, whole = _half_blocks(rows, cols, by_cols)
    mine = (lambda i: (0, i)) if by_cols else (lambda i: (i, 0))

    def body(c_ref, own_ref, a_ref, b_ref, d_ref, o_ref):
        o_ref[...] = ((own_ref[0] + a_ref[0]) + b_ref[0]) + d_ref[0]

    slot = lambda k: pl.BlockSpec((1, br, bc), lambda i, c: (jnp.bitwise_xor(c[1], k),) + mine(i))
    grid_spec = pltpu.PrefetchScalarGridSpec(
        num_scalar_prefetch=1, grid=(nb,),
        in_specs=[slot(0), slot(1), slot(2), slot(3)],
        out_specs=pl.BlockSpec((br, bc), lambda i, c: whole(i, c[0])),
    )
    return pl.pallas_call(
        body, name=name, grid_spec=grid_spec, out_shape=jax.ShapeDtypeStruct((rows, cols), F32),
        compiler_params=_params(("parallel",)),
    )(place, part, landed, landed, landed)


_ROW_POOL_W, _ROW_NORM_AB, _ROW_FINAL, _ROW_POOL_SCALE, _ROW_Q_NORM = 0, 512, 520, 528, 532
_ROW_KV_NORM, _ROW_O_NORM, _ROW_A_LOG, _ROW_DT_BIAS, _ROW_LOSS = 534, 535, 536, 537, 538
_ROW_CONV, _ROW_NORM_C, _SMALL_ROWS = 544, 640, 672
_CONV_ROWS = CONV_WIDTH * 6


def _put_rows(dst_ref, row0, src, width):
    for r in range(width // LANES):
        dst_ref[row0 + r:row0 + r + 1, :] = src[:, r * LANES:(r + 1) * LANES]


def _pack_small(g, loss_tile):
    names = ("wpool", "norm_ab", "final_norm", "pool_scale", "q_a_norm", "kv_a_norm", "o_norm", "a_log", "dt_bias",
             "conv_w", "norm_c")

    def body(wpool, norm_ab, final_norm, pool_scale, q_norm, kv_norm, o_norm, a_log, dt_bias, conv_w, norm_c, loss, o_ref):
        o_ref[...] = jnp.zeros_like(o_ref)
        for gi in range(4):
            o_ref[_ROW_POOL_W + gi * 128:_ROW_POOL_W + (gi + 1) * 128, :] = wpool[gi * 128:(gi + 1) * 128, gi * 128:(gi + 1) * 128]
        _put_rows(o_ref, _ROW_NORM_AB, norm_ab[...], 1024)
        _put_rows(o_ref, _ROW_FINAL, final_norm[...], 1024)
        _put_rows(o_ref, _ROW_POOL_SCALE, pool_scale[...], 512)
        _put_rows(o_ref, _ROW_Q_NORM, q_norm[...], 256)
        for row, ref in ((_ROW_KV_NORM, kv_norm), (_ROW_O_NORM, o_norm), (_ROW_A_LOG, a_log), (_ROW_DT_BIAS, dt_bias)):
            o_ref[row:row + 1, :] = ref[...]
        o_ref[_ROW_LOSS:_ROW_LOSS + 1, :] = loss[0:1, :]
        for j in range(4):
            for r in range(CONV_WIDTH):
                _put_rows(o_ref, _ROW_CONV + j * _CONV_ROWS + r * 6, conv_w[r:r + 1, j * 768:(j + 1) * 768], 768)
            _put_rows(o_ref, _ROW_NORM_C + j * 8, norm_c[:, j * 256:(j + 1) * 256], 256)

    vmem = pl.BlockSpec(memory_space=pltpu.VMEM)
    return pl.pallas_call(
        body, name="pack_small", in_specs=[vmem] * 12, out_specs=vmem,
        out_shape=jax.ShapeDtypeStruct((_SMALL_ROWS, LANES), F32),
    )(*[g[n] for n in names], loss_tile)


_SMALL_NAMES = ("pool_w", "norm_ab", "final_norm", "pool_scale", "q_a_norm", "kv_a_norm", "o_norm", "a_log", "dt_bias",
                "conv_w", "norm_c")


def _take_rows(src, row0, width):
    return jnp.concatenate([src[row0 + r:row0 + r + 1, :] for r in range(width // LANES)], axis=1)


def _small_update(small_all, late_all, ws, ms, vs):
    n = len(_SMALL_NAMES)

    def body(*refs):
        a_ref, late_ref = refs[0], refs[1]
        refs = refs[1:]
        w_refs, m_refs, v_refs = refs[1:1 + n], refs[1 + n:1 + 2 * n], refs[1 + 2 * n:1 + 3 * n]
        outs = refs[1 + 3 * n:1 + 7 * n]
        loss_ref, tot = refs[1 + 7 * n], refs[2 + 7 * n]
        acc, late = a_ref[0], late_ref[0]
        for d in range(1, 8):
            acc = acc + a_ref[d]
            late = late + late_ref[d]
        tot[...] = acc
        x, y, _ = _place()
        j0 = 2 * x + y
        conv = tot[pl.ds(pl.multiple_of(_ROW_CONV + j0 * _CONV_ROWS, 8), _CONV_ROWS), :]
        norm_c = tot[pl.ds(pl.multiple_of(_ROW_NORM_C + j0 * 8, 8), 8), :]
        whole = tot[_ROW_NORM_AB:_ROW_CONV, :]
        at = lambda row: row - _ROW_NORM_AB
        grads = {
            "norm_ab": _take_rows(late, 0, 1024), "final_norm": _take_rows(whole, at(_ROW_FINAL), 1024),
            "pool_scale": _take_rows(whole, at(_ROW_POOL_SCALE), 512), "q_a_norm": _take_rows(whole, at(_ROW_Q_NORM), 256),
            "kv_a_norm": whole[at(_ROW_KV_NORM):at(_ROW_KV_NORM) + 1, :], "o_norm": whole[at(_ROW_O_NORM):at(_ROW_O_NORM) + 1, :],
            "a_log": tot[_ROW_A_LOG:_ROW_A_LOG + 1, 0:GDN_HEADS],
            "dt_bias": tot[_ROW_DT_BIAS:_ROW_DT_BIAS + 1, 0:GDN_HEADS],
            "norm_c": _take_rows(norm_c, 0, 256),
        }
        loss_ref[...] = whole[at(_ROW_LOSS):at(_ROW_LOSS) + 1, :]
        for i, name in enumerate(_SMALL_NAMES):
            g_out = outs[4 * i]
            if name == "pool_w":
                for gi in range(4):
                    g_out[gi] = tot[_ROW_POOL_W + gi * 128:_ROW_POOL_W + (gi + 1) * 128, :]
            elif name == "conv_w":
                for r in range(CONV_WIDTH):
                    g_out[r:r + 1, :] = _take_rows(conv, r * 6, 768)
            else:
                g_out[...] = grads[name]
            _adam_update(g_out, w_refs[i], m_refs[i], v_refs[i], *outs[4 * i + 1:4 * i + 4])

    vmem = pl.BlockSpec(memory_space=pltpu.VMEM)
    out_shape = [jax.ShapeDtypeStruct(w.shape, F32) for w in ws for _ in range(4)] + [jax.ShapeDtypeStruct((1, LANES), F32)]
    return pl.pallas_call(
        body, name="small_update", in_specs=[vmem] * (2 + 3 * n), out_specs=[vmem] * (4 * n + 1), out_shape=out_shape,
        scratch_shapes=[pltpu.VMEM((_SMALL_ROWS, LANES), F32)],
        compiler_params=pltpu.CompilerParams(vmem_limit_bytes=VMEM_LIMIT),
    )(small_all, late_all, *ws, *ms, *vs)


def _adam_update(g_ref, w_ref, m_ref, v_ref, d_ref, mo_ref, vo_ref):
    gv = g_ref[...]
    mn = ADAM_B1 * m_ref[...] + (1.0 - ADAM_B1) * gv
    vn = ADAM_B2 * v_ref[...] + (1.0 - ADAM_B2) * (gv * gv)
    mo_ref[...] = mn
    vo_ref[...] = vn
    c1 = 1.0 - ADAM_B1 ** ADAM_STEP
    c2 = 1.0 - ADAM_B2 ** ADAM_STEP
    d_ref[...] = -ADAM_LR * ((mn / c1) / (jnp.sqrt(vn / c2) + ADAM_EPS) + ADAM_WD * w_ref[...])


def _adamw_rows(g, w, m, v, *, name):
    rows, cols = g.shape
    if rows % LANES == 0:
        tr = _tile(rows, 512)
        blk, steps = pl.BlockSpec((tr, cols), lambda i: (i, 0)), rows // tr
    else:
        tc = _tile(cols, 256)
        blk, steps = pl.BlockSpec((rows, tc), lambda i: (0, i)), cols // tc

    def body(*refs):
        _adam_update(*refs)

    out = jax.ShapeDtypeStruct((rows, cols), F32)
    return pl.pallas_call(
        body, name=name, grid=(steps,), in_specs=[blk] * 4, out_specs=[blk] * 3, out_shape=[out] * 3,
        compiler_params=_params(("parallel",)),
    )(g, w, m, v)


_ADAM_ROWWISE = ("w_in_ab", "w_q_b", "w_kv_b", "w_out_ab", "w_in_c", "w_out_c")


_SHARD_AXIS = {"w_in_ab": 1, "w_q_b": 1, "w_kv_b": 1, "w_out_ab": 0, "w_in_c": 1, "w_out_c": 0, "conv_w": 1, "norm_c": 1}
_ALL_NAMES = ("norm_ab", "w_in_ab", "q_a_norm", "w_q_b", "kv_a_norm", "w_kv_b", "pool_w", "pool_scale", "w_out_ab",
              "norm_c", "w_in_c", "conv_w", "a_log", "dt_bias", "o_norm", "w_out_c", "final_norm")


def _join_shards(a, axis):
    _, r, c = a.shape
    return a.reshape(4 * r, c) if axis == 0 else jnp.transpose(a, (1, 0, 2)).reshape(r, 4 * c)


def _split_shards(a, axis):
    r, c = a.shape
    return a.reshape(4, r // 4, c) if axis == 0 else jnp.transpose(a.reshape(r, 4, c // 4), (1, 0, 2))


def kernel(x, positions, norm_ab, w_in_ab, q_a_norm, w_q_b, kv_a_norm, w_kv_b, pool_w, pool_scale, w_out_ab, norm_c, w_in_c, conv_w, a_log, dt_bias, o_norm, w_out_c, final_norm, loss_target, m_norm_ab, m_w_in_ab, m_q_a_norm, m_w_q_b, m_kv_a_norm, m_w_kv_b, m_pool_w, m_pool_scale, m_w_out_ab, m_norm_c, m_w_in_c, m_conv_w, m_a_log, m_dt_bias, m_o_norm, m_w_out_c, m_final_norm, v_norm_ab, v_w_in_ab, v_q_a_norm, v_w_q_b, v_kv_a_norm, v_w_kv_b, v_pool_w, v_pool_scale, v_w_out_ab, v_norm_c, v_w_in_c, v_conv_w, v_a_log, v_dt_bias, v_o_norm, v_w_out_c, v_final_norm):
    given = dict(locals())
    c = lax.axis_index("c")
    t = x.shape[1]

    def shard_of(prefix, name):
        a = given[prefix + name]
        return a.reshape(a.shape[1:]) if a.ndim > 2 else a.reshape(1, -1)

    big, big_even, big_odd, small_sharded = _ADAM_ROWWISE, _ADAM_ROWWISE[:4], _ADAM_ROWWISE[4:], ("conv_w", "norm_c")
    chip = 2 * lax.axis_index("x") + lax.axis_index("y")
    core = c.astype(jnp.int32).reshape(1)
    place = jnp.stack([c, chip]).astype(jnp.int32)
    later = {"mid": big_even[1:], "odd": big_odd + small_sharded}
    travelling = {}

    def send(tag, after=None):
        shards = [shard_of("", n).astype(BF16) if n in big else shard_of("", n) for n in later[tag]]
        started = _to_chips_start(shards, per_chip_slot=False, name="gather_" + tag + "_start", after=after)
        travelling[tag] = (shards, started)
        return started[-1][0, 0]

    mid_sent = send("mid")
    gathered = _gather_weights([shard_of("", "w_in_ab").astype(BF16)], [])
    full = {"w_in_ab": _join_shards(gathered[0], _SHARD_AXIS["w_in_ab"])}
    for name in ("norm_ab", "q_a_norm", "kv_a_norm", "pool_w", "pool_scale"):
        full[name] = shard_of("", name)
    lw = _layout_in_ab(full)
    lw["norm_ab"] = lw["norm_ab"] + mid_sent

    def more_weights(tag, after):
        shards, started = travelling[tag]
        landed = _to_chips_wait(started, after, per_chip_slot=False, name="gather_" + tag + "_wait")
        w = {}
        for name, land, own in zip(later[tag], landed, shards):
            w[name] = _join_shards(lax.dynamic_update_index_in_dim(land, own, chip, 0), _SHARD_AXIS[name])
        if tag == "mid":
            out = _layout_mid(w)
            out["wq"] = out["wq"] + send("odd", after=landed[0]).astype(BF16)
            return out
        for name in ("a_log", "dt_bias", "o_norm", "final_norm"):
            w[name] = shard_of("", name)
        return _layout_odd(w)

    transposed = ("w_in_ab", "w_in_c")

    def chip_slots(names, g):
        grads = _unlayout_grads(g, names)
        return ([_split_shards(grads[n], 0 if n in transposed else _SHARD_AXIS[n]) for n in names],
                [n in transposed for n in names])

    def chip_partials(names, slots, partial, by_cols):
        return [_core_sum(s, p, core, name="core_sum_" + n, by_cols=b) for n, s, p, b in zip(names, slots, partial, by_cols)]

    groups = {"odd": big_odd, "out_ab": ("w_out_ab",), "in_ab": ("w_in_ab", "w_q_b", "w_kv_b")}
    sent, swapping = {}, {}

    def on_grads(tag, g, after):
        if tag == "odd":
            slots, by_cols = chip_slots(groups[tag], g)
            swapping[tag] = (slots, by_cols, _core_swap_partial_start(slots, by_cols, name="core_swap_partial_odd_start"))
            return swapping[tag][2][-1][0, 0]
        if tag == "odd_go":
            tag = "odd"
            _, by_cols, started = swapping[tag]
            slots, partial = _core_swap_partial_wait(started, after, by_cols, name="core_swap_partial_odd_wait")
        else:
            slots, by_cols = chip_slots(groups[tag], g)
            partial = _core_swap_partial(slots, by_cols, name="core_swap_partial_" + tag)
        part = chip_partials(groups[tag], slots, partial, by_cols)
        sent[tag] = (part, _to_chips_start(part, per_chip_slot=True, name="exchange_" + tag + "_start"))
        token = sent[tag][1][-1][0, 0]
        if tag == "in_ab":
            pack = _pack_small({**g, "norm_ab": jnp.zeros((1, 1024), F32)}, g["loss"])
            sent["small"] = (pack, _to_chips_start([pack], per_chip_slot=False, to_all=True, name="exchange_small_start"))
            token = token + sent["small"][1][-1][0, 0]
        return token

    loss_tile, dx, g = _local_step(x[0], positions.reshape(t, 1), loss_target[0], lw, more_weights, on_grads)
    late_all = _chip_exchange([], g["norm_ab"].reshape(8, LANES))[-1]
    pack, started = sent.pop("small")
    landed = _to_chips_wait(started, late_all, per_chip_slot=False, to_all=True, name="exchange_small_wait")[0]
    small_all = lax.dynamic_update_index_in_dim(landed, pack, 2 * chip + c, 0)
    halves = {}
    for tag, names in groups.items():
        part, started = sent[tag]
        landed = _to_chips_wait(started, late_all, per_chip_slot=True, name="exchange_" + tag + "_wait")
        for n, l, p in zip(names, landed, part):
            halves[n] = _chip_sum(l, p, place, name="chip_sum_" + n, by_cols=n in transposed)
    gbig = dict(zip(big, _core_swap_sum([halves[n] for n in big], [n in transposed for n in big])))

    res = {}
    for name in big:
        operands = [gbig[name], shard_of("", name), shard_of("m_", name), shard_of("v_", name)]
        flip = name in transposed
        if flip:
            operands[1:] = [jnp.transpose(a) for a in operands[1:]]
        out = (operands[0],) + tuple(_adamw_rows(*operands, name="adamw_" + name))
        out = [jnp.transpose(a) for a in out] if flip else out
        res["grad", name], res["delta", name], res["m", name], res["v", name] = out
    out = _small_update(small_all, late_all, [shard_of("", n) for n in _SMALL_NAMES], [shard_of("m_", n) for n in _SMALL_NAMES],
                        [shard_of("v_", n) for n in _SMALL_NAMES])
    for i, name in enumerate(_SMALL_NAMES):
        res["grad", name], res["delta", name], res["m", name], res["v", name] = out[4 * i:4 * i + 4]
    res = {k: a.reshape(given[k[1]].shape) for k, a in res.items()}
    loss = out[-1][0, 0]
    outs = [loss, dx.reshape(x.shape)]
    for key in ("grad", "delta", "m", "v"):
        outs += [res[key, n] for n in _ALL_NAMES]
    return tuple(outs)
```

```python
import functools

import jax
import jax.numpy as jnp
from jax import lax
from jax.experimental import pallas as pl
from jax.experimental.pallas import tpu as pltpu

F32 = jnp.float32
BF16 = jnp.bfloat16
HI = lax.Precision.HIGHEST
MESH = pl.DeviceIdType.MESH

RMS_EPS = 1e-6
MLA_HEADS = 8
MLA_Q_RANK = 256
MLA_KV_RANK = 128
MLA_NOPE = 64
MLA_ROPE = 32
MLA_V = 64
ROPE_THETA = 10000.0
POOL_WINDOWS = (2, 4, 8, 16)
POOL_GROUP = 128
POOL_WIDTH = 512
POOL_HALO = 16
GDN_HEADS = 8
GDN_DK = 128
CONV_WIDTH = 4
CONV_HALO = 8
CHUNK = 64
IN_AB_PAD = 2048
IN_C_PAD = 4224
ATT_SCALE = (MLA_NOPE + MLA_ROPE) ** -0.5
LOG2E = 1.4426950408889634

ADAM_LR = 0.001
ADAM_B1 = 0.9
ADAM_B2 = 0.999
ADAM_EPS = 1e-08
ADAM_WD = 0.01
ADAM_STEP = 10

LANES = 128
VMEM_LIMIT = 56 * 1024 * 1024

ROW_TILE = 256
ATT_TILE = 1024
GDN_TILE = 256
MM_TILE = (1408, 1408, 2048)

NN = (((1,), (0,)), ((), ()))
NT = (((1,), (1,)), ((), ()))
TN = (((0,), (0,)), ((), ()))


def _dot(a, b, dims=NN, prec=None):
    return lax.dot_general(a, b, dims, precision=prec, preferred_element_type=F32)


def _tile(n, pref):
    if n <= pref:
        return n
    step = LANES if pref >= LANES else 8
    for t in range(pref - pref % step, 0, -step):
        if n % t == 0:
            return t
    return n


def _params(sem):
    return pltpu.CompilerParams(dimension_semantics=sem, vmem_limit_bytes=VMEM_LIMIT)


def _sigmoid(x):
    return 0.5 * jnp.tanh(0.5 * x) + 0.5


def _softplus(x):
    return jnp.maximum(x, 0.0) + jnp.log(1.0 + jnp.exp(-jnp.abs(x)))


def _matmul(a, b, mode, *, name):
    if mode == "nn":
        (m, k), (k2, n) = a.shape, b.shape
    elif mode == "nt":
        (m, k), (n, k2) = a.shape, b.shape
    else:
        (k, m), (k2, n) = a.shape, b.shape
    assert k == k2, (a.shape, b.shape, mode)
    tm, tn, tk = _tile(m, MM_TILE[0]), _tile(n, MM_TILE[1]), _tile(k, MM_TILE[2])
    nk = k // tk
    if mode == "tn":
        a_spec = pl.BlockSpec((tk, tm), lambda i, j, kk: (kk, i))
    else:
        a_spec = pl.BlockSpec((tm, tk), lambda i, j, kk: (i, kk))
    if mode == "nt":
        b_spec = pl.BlockSpec((tn, tk), lambda i, j, kk: (j, kk))
    else:
        b_spec = pl.BlockSpec((tk, tn), lambda i, j, kk: (kk, j))
    o_spec = pl.BlockSpec((tm, tn), lambda i, j, kk: (i, j))
    dims = {"nn": NN, "nt": NT, "tn": TN}[mode]

    def body(a_ref, b_ref, o_ref, *scratch):
        if nk == 1:
            o_ref[...] = _dot(a_ref[...], b_ref[...], dims)
            return
        acc = scratch[0]
        kk = pl.program_id(2)

        @pl.when(kk == 0)
        def _():
            acc[...] = jnp.zeros_like(acc)

        acc[...] += _dot(a_ref[...], b_ref[...], dims)

        @pl.when(kk == nk - 1)
        def _():
            o_ref[...] = acc[...]

    return pl.pallas_call(
        body, name=name, grid=(m // tm, n // tn, nk), in_specs=[a_spec, b_spec], out_specs=o_spec,
        out_shape=jax.ShapeDtypeStruct((m, n), F32),
        scratch_shapes=[pltpu.VMEM((tm, tn), F32)] if nk > 1 else [],
        compiler_params=_params(("parallel", "parallel", "arbitrary")),
    )(a, b)


def _rms_in_proj(h, g, w, *, name):
    t, d = h.shape
    n = w.shape[1]
    tm, tn = _tile(t, MM_TILE[0]), _tile(n, MM_TILE[1])

    def body(h_ref, g_ref, w_ref, o_ref, hn_ref):
        @pl.when(pl.program_id(1) == 0)
        def _():
            x = h_ref[...]
            r = lax.rsqrt(jnp.mean(x * x, axis=-1, keepdims=True) + RMS_EPS)
            hn_ref[...] = (x * r * g_ref[...]).astype(BF16)

        o_ref[...] = _dot(hn_ref[...], w_ref[...])

    return pl.pallas_call(
        body, name=name, grid=(t // tm, n // tn),
        in_specs=[pl.BlockSpec((tm, d), lambda i, j: (i, 0)), pl.BlockSpec((1, d), lambda i, j: (0, 0)),
                  pl.BlockSpec((d, tn), lambda i, j: (0, j))],
        out_specs=[pl.BlockSpec((tm, tn), lambda i, j: (i, j)), pl.BlockSpec((tm, d), lambda i, j: (i, 0))],
        out_shape=[jax.ShapeDtypeStruct((t, n), F32), jax.ShapeDtypeStruct((t, d), BF16)],
        compiler_params=_params(("parallel", "arbitrary")),
    )(h, g, w)


def _matmul_rms_bwd(dproj, w, h, g, dres, *, name, prev_y=None):
    t, k = dproj.shape
    d = w.shape[0]
    tm = _tile(t, 2 * ROW_TILE)
    chained = prev_y is not None

    def body(dp_ref, w_ref, h_ref, g_ref, dres_ref, *rest):
        i = pl.program_id(0)
        dh_ref, dg_ref = rest[-4:-2] if chained else rest
        dyv = _dot(dp_ref[...], w_ref[...], NT)
        x = h_ref[...]
        r = lax.rsqrt(jnp.mean(x * x, axis=-1, keepdims=True) + RMS_EPS)
        xh = x * r
        dxh = dyv * g_ref[...]
        dh = dres_ref[...] + r * (dxh - xh * jnp.mean(dxh * xh, axis=-1, keepdims=True))
        dh_ref[...] = dh

        @pl.when(i == 0)
        def _():
            dg_ref[...] = jnp.zeros_like(dg_ref)
            if chained:
                rest[-1][...] = jnp.zeros_like(rest[-1])

        dg_ref[...] += jnp.sum(dyv * xh, axis=0, keepdims=True)
        if chained:
            y_ref, dhb_ref, dw_ref = rest[0], rest[-2], rest[-1]
            dhb_ref[...] = dh.astype(BF16)
            dw_ref[...] += _dot(y_ref[...], dhb_ref[...], TN)

    row = pl.BlockSpec((tm, d), lambda i: (i, 0))
    vec = pl.BlockSpec((1, d), lambda i: (0, 0))
    in_specs = [pl.BlockSpec((tm, k), lambda i: (i, 0)), pl.BlockSpec((d, k), lambda i: (0, 0)), row, vec, row]
    out_specs, out_shape = [row, vec], [jax.ShapeDtypeStruct((t, d), F32), jax.ShapeDtypeStruct((1, d), F32)]
    args = [dproj, w, h, g, dres]
    if chained:
        in_specs.append(row)
        args.append(prev_y)
        out_specs += [row, pl.BlockSpec((d, d), lambda i: (0, 0))]
        out_shape += [jax.ShapeDtypeStruct((t, d), BF16), jax.ShapeDtypeStruct((d, d), F32)]
    return pl.pallas_call(
        body, name=name, grid=(t // tm,), in_specs=in_specs, out_specs=out_specs, out_shape=out_shape,
        compiler_params=_params(("arbitrary",)),
    )(*args)


def _rope_partner(x):
    lane = lax.broadcasted_iota(jnp.int32, x.shape, 1)
    swapped = jnp.where(lane < MLA_NOPE + MLA_ROPE // 2, pltpu.roll(x, LANES - 16, 1), pltpu.roll(x, 16, 1))
    return jnp.where((lane >= MLA_NOPE) & (lane < MLA_NOPE + MLA_ROPE), swapped, 0.0)


def _pool_counts(row0, tm, w):
    t_idx = row0 + lax.broadcasted_iota(jnp.int32, (tm, POOL_GROUP), 0)
    return jnp.minimum(t_idx + 1, w).astype(F32)


def _ab_prep(proj, pos, inv_freq, q_a_norm, kv_a_norm, wq, wk, wv, wpool, *, name):
    t = proj.shape[0]
    tm = _tile(t, ROW_TILE)
    hb = tm // POOL_HALO

    def body(p_ref, halo_ref, pos_ref, inv_ref, qg_ref, kg_ref, wq_ref, wk_ref, wv_ref, wp_ref,
             q_ref, k_ref, v_ref, yb_ref, qn_ref, kvn_ref, d_ref, cos_ref, sin_ref, ext):
        i = pl.program_id(0)
        ql = p_ref[:, 0:MLA_Q_RANK]
        r = lax.rsqrt(jnp.mean(ql * ql, axis=-1, keepdims=True) + RMS_EPS)
        qn = (ql * r * qg_ref[...]).astype(BF16)
        qn_ref[...] = qn
        kl = p_ref[:, MLA_Q_RANK:MLA_Q_RANK + MLA_KV_RANK]
        r = lax.rsqrt(jnp.mean(kl * kl, axis=-1, keepdims=True) + RMS_EPS)
        kvn = (kl * r * kg_ref[...]).astype(BF16)
        kvn_ref[...] = kvn
        ang = pos_ref[...].astype(F32) * inv_ref[...]
        lane = lax.broadcasted_iota(jnp.int32, (tm, LANES), 1)
        in_rope = (lane >= MLA_NOPE) & (lane < MLA_NOPE + MLA_ROPE)
        cos_t = jnp.where(in_rope, jnp.cos(ang), 1.0)
        sin_t = jnp.where(in_rope, jnp.sin(ang), 0.0)
        sin_t = jnp.where(lane < MLA_NOPE + MLA_ROPE // 2, -sin_t, sin_t)
        cos_ref[...] = cos_t
        sin_ref[...] = sin_t
        kr = p_ref[:, 384:512]
        kr = kr * cos_t + _rope_partner(kr) * sin_t
        qraw = _dot(qn, wq_ref[...])
        kvk = _dot(kvn, wk_ref[...])
        for h in range(MLA_HEADS):
            sl = slice(h * LANES, (h + 1) * LANES)
            qh = qraw[:, sl]
            q_ref[:, sl] = ((qh * cos_t + _rope_partner(qh) * sin_t) * (ATT_SCALE * LOG2E)).astype(BF16)
            k_ref[:, sl] = (kvk[:, sl] + kr).astype(BF16)
        v_ref[...] = _dot(kvn, wv_ref[...]).astype(BF16)
        xp = p_ref[:, 512:1024]
        ext[0:POOL_HALO, :] = jnp.where(i > 0, halo_ref[...], 0.0)
        ext[POOL_HALO:POOL_HALO + tm, :] = xp
        for g, w in enumerate(POOL_WINDOWS):
            lo = g * POOL_GROUP
            acc = ext[POOL_HALO:POOL_HALO + tm, lo:lo + POOL_GROUP]
            for s in range(1, w):
                acc = acc + ext[POOL_HALO - s:POOL_HALO - s + tm, lo:lo + POOL_GROUP]
            cnt = _pool_counts(i * tm, tm, w)
            d_ref[:, lo:lo + POOL_GROUP] = (acc / cnt - xp[:, lo:lo + POOL_GROUP]).astype(BF16)
        yb_ref[...] = _dot(d_ref[...], wp_ref[...])

    row = lambda w: pl.BlockSpec((tm, w), lambda i: (i, 0))
    vec = lambda w: pl.BlockSpec((1, w), lambda i: (0, 0))
    whole = lambda a: pl.BlockSpec(a.shape, lambda i: (0, 0))
    return pl.pallas_call(
        body, name=name, grid=(t // tm,),
        in_specs=[row(1024), pl.BlockSpec((POOL_HALO, POOL_WIDTH), lambda i: (jnp.maximum(i * hb - 1, 0), 1)),
                  pl.BlockSpec((tm, 1), lambda i: (i, 0)), vec(LANES), vec(MLA_Q_RANK), vec(MLA_KV_RANK),
                  whole(wq), whole(wk), whole(wv), whole(wpool)],
        out_specs=[row(1024), row(1024), row(512), row(512), row(MLA_Q_RANK), row(MLA_KV_RANK), row(POOL_WIDTH),
                   row(LANES), row(LANES)],
        out_shape=[jax.ShapeDtypeStruct((t, 1024), BF16), jax.ShapeDtypeStruct((t, 1024), BF16),
                   jax.ShapeDtypeStruct((t, 512), BF16), jax.ShapeDtypeStruct((t, 512), F32),
                   jax.ShapeDtypeStruct((t, MLA_Q_RANK), BF16), jax.ShapeDtypeStruct((t, MLA_KV_RANK), BF16),
                   jax.ShapeDtypeStruct((t, POOL_WIDTH), BF16), jax.ShapeDtypeStruct((t, LANES), F32),
                   jax.ShapeDtypeStruct((t, LANES), F32)],
        scratch_shapes=[pltpu.VMEM((tm + POOL_HALO, POOL_WIDTH), F32)],
        compiler_params=_params(("parallel",)),
    )(proj, proj, pos, inv_freq, q_a_norm, kv_a_norm, wq, wk, wv, wpool)


def _ab_prep_bwd(proj, q_a_norm, kv_a_norm, dq, dk, dv, cos_t, sin_t, dyb, dz, qn, kvn, d, hn, wq, wk, wv, wpool, *, name):
    t = proj.shape[0]
    tm = _tile(t, ROW_TILE)
    hb = tm // POOL_HALO
    last_halo = t // POOL_HALO - 1
    nt = t // tm

    def body(p_ref, qg_ref, kg_ref, dq_ref, dk_ref, dv_ref, c_ref, s_ref, dyb_ref, dybn_ref, dz_ref,
             qn_ref, kvn_ref, d_ref, hn_ref, wq_ref, wk_ref, wv_ref, wp_ref,
             dp_ref, dqg_ref, dkg_ref, dwq_ref, dwk_ref, dwv_ref, dwp_ref, dwin_ref, ext, dqr_ref, dkb_ref):
        i = pl.program_id(0)

        @pl.when(i == 0)
        def _():
            for ref in (dqg_ref, dkg_ref, dwq_ref, dwk_ref, dwv_ref, dwp_ref, dwin_ref):
                ref[...] = jnp.zeros_like(ref)

        def norm_bwd(x, g, dy, dg_ref):
            r = lax.rsqrt(jnp.mean(x * x, axis=-1, keepdims=True) + RMS_EPS)
            xh = x * r
            dxh = dy * g
            dg_ref[...] += jnp.sum(dy * xh, axis=0, keepdims=True)
            return r * (dxh - xh * jnp.mean(dxh * xh, axis=-1, keepdims=True))

        c, s = c_ref[...], s_ref[...]
        lane = lax.broadcasted_iota(jnp.int32, (tm, LANES), 1)
        in_rope = (lane >= MLA_NOPE) & (lane < MLA_NOPE + MLA_ROPE)
        dkr = jnp.zeros((tm, LANES), F32)
        for h in range(MLA_HEADS):
            sl = slice(h * LANES, (h + 1) * LANES)
            g = dq_ref[:, sl]
            dqr_ref[:, sl] = ((g * c + _rope_partner(g * s)) * ATT_SCALE).astype(BF16)
            gk = dk_ref[:, sl]
            dkb_ref[:, sl] = gk.astype(BF16)
            dkr = dkr + jnp.where(in_rope, gk, 0.0)
        dkr = dkr * c + _rope_partner(dkr * s)
        dqn = _dot(dqr_ref[...], wq_ref[...], NT)
        dkvn = _dot(dkb_ref[...], wk_ref[...], NT) + _dot(dv_ref[...], wv_ref[...], NT)
        dql = norm_bwd(p_ref[:, 0:MLA_Q_RANK], qg_ref[...], dqn, dqg_ref)
        dp_ref[:, 0:MLA_Q_RANK] = dql.astype(BF16)
        dkl = norm_bwd(p_ref[:, MLA_Q_RANK:384], kg_ref[...], dkvn, dkg_ref)
        dp_ref[:, MLA_Q_RANK:384] = dkl.astype(BF16)
        dp_ref[:, 384:512] = dkr.astype(BF16)
        ddv = _dot(dyb_ref[...], wp_ref[...], NT)
        ddn = _dot(dybn_ref[...], wp_ref[...], NT)
        for g, w in enumerate(POOL_WINDOWS):
            lo = g * POOL_GROUP
            ext[0:tm, lo:lo + POOL_GROUP] = ddv[:, lo:lo + POOL_GROUP] / _pool_counts(i * tm, tm, w)
            nxt = ddn[:, lo:lo + POOL_GROUP] / _pool_counts((i + 1) * tm, POOL_HALO, w)
            ext[tm:tm + POOL_HALO, lo:lo + POOL_GROUP] = jnp.where(i < nt - 1, nxt, 0.0)
        for g, w in enumerate(POOL_WINDOWS):
            lo = g * POOL_GROUP
            acc = ext[0:tm, lo:lo + POOL_GROUP]
            for s in range(1, w):
                acc = acc + ext[s:s + tm, lo:lo + POOL_GROUP]
            dp_ref[:, 512 + lo:512 + lo + POOL_GROUP] = (acc - ddv[:, lo:lo + POOL_GROUP]).astype(BF16)
        dp_ref[:, 1024:2048] = dz_ref[...]
        dwq_ref[...] += _dot(qn_ref[...], dqr_ref[...], TN)
        dwk_ref[...] += _dot(kvn_ref[...], dkb_ref[...], TN)
        dwv_ref[...] += _dot(kvn_ref[...], dv_ref[...], TN)
        dwp_ref[...] += _dot(d_ref[...], dyb_ref[...], TN)
        dwin_ref[...] += _dot(dp_ref[...], hn_ref[...], TN)

    row = lambda w: pl.BlockSpec((tm, w), lambda i: (i, 0))
    vec = lambda w: pl.BlockSpec((1, w), lambda i: (0, 0))
    whole = lambda a: pl.BlockSpec(a.shape, lambda i: (0, 0))
    weights = (wq, wk, wv, wpool)
    return pl.pallas_call(
        body, name=name, grid=(nt,),
        in_specs=[row(1024), vec(MLA_Q_RANK), vec(MLA_KV_RANK), row(1024), row(1024), row(512), row(LANES), row(LANES),
                  row(POOL_WIDTH),
                  pl.BlockSpec((POOL_HALO, POOL_WIDTH), lambda i: (jnp.minimum((i + 1) * hb, last_halo), 0)),
                  row(1024), row(MLA_Q_RANK), row(MLA_KV_RANK), row(POOL_WIDTH), row(1024)] + [whole(w) for w in weights],
        out_specs=[row(IN_AB_PAD), vec(MLA_Q_RANK), vec(MLA_KV_RANK)] + [whole(w) for w in weights]
        + [pl.BlockSpec((IN_AB_PAD, 1024), lambda i: (0, 0))],
        out_shape=[jax.ShapeDtypeStruct((t, IN_AB_PAD), BF16), jax.ShapeDtypeStruct((1, MLA_Q_RANK), F32),
                   jax.ShapeDtypeStruct((1, MLA_KV_RANK), F32)] + [jax.ShapeDtypeStruct(w.shape, F32) for w in weights]
        + [jax.ShapeDtypeStruct((IN_AB_PAD, 1024), F32)],
        scratch_shapes=[pltpu.VMEM((tm + POOL_HALO, POOL_WIDTH), F32), pltpu.VMEM((tm, 1024), BF16),
                        pltpu.VMEM((tm, 1024), BF16)],
        compiler_params=_params(("arbitrary",)),
    )(proj, q_a_norm, kv_a_norm, dq, dk, dv, cos_t, sin_t, dyb, dyb, dz, qn, kvn, d, hn, wq, wk, wv, wpool)


def _gate_out_proj(o, ybraw, proj, pool_scale, w, hres, *, name):
    t = o.shape[0]
    tm = _tile(t, 2 * ROW_TILE)

    def body(o_ref, yb_ref, z_ref, ps_ref, w_ref, h_ref, ho_ref, y_ref):
        z = z_ref[...]
        sz = z * _sigmoid(z)
        y_ref[:, 0:512] = (o_ref[...] * sz[:, 0:512]).astype(BF16)
        y_ref[:, 512:1024] = (yb_ref[...] * ps_ref[...] * sz[:, 512:1024]).astype(BF16)
        ho_ref[...] = h_ref[...] + _dot(y_ref[...], w_ref[...])

    row = lambda w_: pl.BlockSpec((tm, w_), lambda i: (i, 0))
    return pl.pallas_call(
        body, name=name, grid=(t // tm,),
        in_specs=[row(512), row(512), pl.BlockSpec((tm, 1024), lambda i: (i, 1)), pl.BlockSpec((1, 512), lambda i: (0, 0)),
                  pl.BlockSpec(w.shape, lambda i: (0, 0)), row(1024)],
        out_specs=[row(1024), row(1024)],
        out_shape=[jax.ShapeDtypeStruct((t, 1024), F32), jax.ShapeDtypeStruct((t, 1024), BF16)],
        compiler_params=_params(("parallel",)),
    )(o, ybraw, proj, pool_scale, w, hres)


def _gate_bwd(dh, w, o, ybraw, proj, pool_scale, *, name):
    t = o.shape[0]
    tm = _tile(t, ROW_TILE)

    def body(dh_ref, w_ref, o_ref, yb_ref, z_ref, ps_ref, do_ref, dl_ref, dyb_ref, dz_ref, dps_ref):
        i = pl.program_id(0)
        z = z_ref[...]
        sg = _sigmoid(z)
        sz = z * sg
        dsz = sg * (1.0 + z * (1.0 - sg))
        dyv = _dot(dh_ref[...], w_ref[...], NT)
        dcat = dyv * sz
        ov = o_ref[...]
        ybs = yb_ref[...] * ps_ref[...]
        dz_ref[:, 0:512] = (dyv[:, 0:512] * ov * dsz[:, 0:512]).astype(BF16)
        dz_ref[:, 512:1024] = (dyv[:, 512:1024] * ybs * dsz[:, 512:1024]).astype(BF16)
        do = dcat[:, 0:512]
        do_ref[...] = do.astype(BF16)
        r_i = (lax.broadcasted_iota(jnp.int32, (1024, 512), 0) % 512) // MLA_V
        c_i = lax.broadcasted_iota(jnp.int32, (1024, 512), 1) // MLA_V
        prod = do * ov
        hi = prod.astype(BF16)
        lo = (prod - hi.astype(F32)).astype(BF16)
        dl_ref[...] = _dot(jnp.concatenate([hi, lo], axis=1), (r_i == c_i).astype(BF16))
        dyb_ref[...] = (dcat[:, 512:1024] * ps_ref[...]).astype(BF16)

        @pl.when(i == 0)
        def _():
            dps_ref[...] = jnp.zeros_like(dps_ref)

        dps_ref[...] += jnp.sum(dcat[:, 512:1024] * yb_ref[...], axis=0, keepdims=True)

    row = lambda w: pl.BlockSpec((tm, w), lambda i: (i, 0))
    vec = pl.BlockSpec((1, 512), lambda i: (0, 0))
    return pl.pallas_call(
        body, name=name, grid=(t // tm,),
        in_specs=[row(1024), pl.BlockSpec(w.shape, lambda i: (0, 0)), row(512), row(512),
                  pl.BlockSpec((tm, 1024), lambda i: (i, 1)), vec],
        out_specs=[row(512), row(512), row(512), row(1024), vec],
        out_shape=[jax.ShapeDtypeStruct((t, 512), BF16), jax.ShapeDtypeStruct((t, 512), F32),
                   jax.ShapeDtypeStruct((t, 512), BF16), jax.ShapeDtypeStruct((t, 1024), BF16),
                   jax.ShapeDtypeStruct((1, 512), F32)],
        compiler_params=_params(("arbitrary",)),
    )(dh, w, o, ybraw, proj, pool_scale)


ATT_HP_FWD = 4
ATT_HP_BWD = 2


def _diag_mask(tq):
    return lax.broadcasted_iota(jnp.int32, (tq, tq), 1) <= lax.broadcasted_iota(jnp.int32, (tq, tq), 0)


def _block_schedule(nq, key_major):
    if key_major:
        pairs = [(qi, ki) for ki in range(nq) for qi in range(ki, nq)]
    else:
        pairs = [(qi, ki) for qi in range(nq) for ki in range(qi + 1)]
    return jnp.asarray([p[0] for p in pairs], jnp.int32), jnp.asarray([p[1] for p in pairs], jnp.int32)


def _attn_fwd(q, k, v, *, name):
    t = q.shape[0]
    tq = _tile(t, ATT_TILE)
    nq = t // tq
    hp = ATT_HP_FWD
    qi_tab, ki_tab = _block_schedule(nq, key_major=False)

    def body(qi_ref, ki_ref, q_ref, k_ref, v_ref, o_ref, lse_ref, m_sc, l_sc, acc_sc):
        step = pl.program_id(1)
        qi, ki = qi_ref[step], ki_ref[step]

        @pl.when(ki == 0)
        def _():
            m_sc[...] = jnp.full_like(m_sc, -jnp.inf)
            l_sc[...] = jnp.zeros_like(l_sc)
            acc_sc[...] = jnp.zeros_like(acc_sc)

        def block(on_diagonal):
            scores = []
            for h in range(hp):
                sl = slice(h * LANES, (h + 1) * LANES)
                scores.append(_dot(q_ref[:, sl], k_ref[:, sl], NT))
            if on_diagonal:
                mask = _diag_mask(tq)
                scores = [jnp.where(mask, s, -jnp.inf) for s in scores]
            for h, s in enumerate(scores):
                vv = v_ref[:, (h // 2) * LANES:(h // 2 + 1) * LANES]
                m_prev = m_sc[h]
                m_new = jnp.maximum(m_prev, jnp.max(s, axis=-1, keepdims=True))
                alpha = jnp.exp2(m_prev - m_new)
                p = jnp.exp2(s - m_new[:, 0:1])
                l_sc[h] = alpha * l_sc[h] + jnp.sum(p, axis=-1, keepdims=True)
                acc_sc[h] = alpha * acc_sc[h] + _dot(p.astype(BF16), vv)
                m_sc[h] = m_new

        pl.when(ki < qi)(functools.partial(block, False))
        pl.when(ki == qi)(functools.partial(block, True))

        @pl.when(ki == qi)
        def _():
            first = lax.broadcasted_iota(jnp.int32, (tq, LANES), 1) < MLA_V
            for pr in range(hp // 2):
                a, b = 2 * pr, 2 * pr + 1
                sl = slice(pr * LANES, (pr + 1) * LANES)
                o_ref[:, sl] = jnp.where(first, acc_sc[a] / l_sc[a], acc_sc[b] / l_sc[b])
                lse_ref[:, sl] = jnp.where(first, m_sc[a] + jnp.log2(l_sc[a]), m_sc[b] + jnp.log2(l_sc[b]))

    grid_spec = pltpu.PrefetchScalarGridSpec(
        num_scalar_prefetch=2, grid=(MLA_HEADS // hp, qi_tab.shape[0]),
        in_specs=[pl.BlockSpec((tq, hp * LANES), lambda g, s, qt, kt: (qt[s], g)),
                  pl.BlockSpec((tq, hp * LANES), lambda g, s, qt, kt: (kt[s], g)),
                  pl.BlockSpec((tq, hp * MLA_V), lambda g, s, qt, kt: (kt[s], g))],
        out_specs=[pl.BlockSpec((tq, hp * MLA_V), lambda g, s, qt, kt: (qt[s], g)),
                   pl.BlockSpec((tq, hp * MLA_V), lambda g, s, qt, kt: (qt[s], g))],
        scratch_shapes=[pltpu.VMEM((hp, tq, LANES), F32)] * 3,
    )
    return pl.pallas_call(
        body, name=name, grid_spec=grid_spec,
        out_shape=[jax.ShapeDtypeStruct((t, 512), F32), jax.ShapeDtypeStruct((t, 512), F32)],
        compiler_params=_params(("parallel", "arbitrary")),
    )(qi_tab, ki_tab, q, k, v)


def _attn_bwd(q, k, v, do, lse, delta, *, name):
    t = q.shape[0]
    tq = _tile(t, ATT_TILE)
    nq = t // tq
    hp = ATT_HP_BWD
    qi_tab, ki_tab = _block_schedule(nq, key_major=True)

    def body(qi_ref, ki_ref, q_ref, k_ref, v_ref, do_ref, lse_ref, dl_ref, dq_ref, dk_ref, dv_ref, dk_sc, dv_sc):
        step = pl.program_id(1)
        qi, ki = qi_ref[step], ki_ref[step]

        @pl.when(step == 0)
        def _():
            dq_ref[...] = jnp.zeros_like(dq_ref)

        @pl.when(qi == ki)
        def _():
            dk_sc[...] = jnp.zeros_like(dk_sc)
            dv_sc[...] = jnp.zeros_like(dv_sc)

        def block(on_diagonal):
            lane = lax.broadcasted_iota(jnp.int32, (tq, LANES), 1)
            rows = pl.ds(pl.multiple_of(qi * tq, tq), tq)
            heads = [slice(h * LANES, (h + 1) * LANES) for h in range(hp)]
            scores = [_dot(q_ref[:, sl], k_ref[:, sl], NT) for sl in heads]
            dps = []
            for h in range(hp):
                dov = do_ref[:, (h // 2) * LANES:(h // 2 + 1) * LANES]
                mine = (lane < MLA_V) if h % 2 == 0 else (lane >= MLA_V)
                dps.append(_dot(jnp.where(mine, dov, jnp.zeros_like(dov)), v_ref[:, (h // 2) * LANES:(h // 2 + 1) * LANES], NT))
            mask = _diag_mask(tq) if on_diagonal else None
            for h, sl in enumerate(heads):
                col = (h // 2) * LANES + (h % 2) * MLA_V
                p = jnp.exp2(scores[h] - lse_ref[:, col:col + 1])
                if on_diagonal:
                    p = jnp.where(mask, p, 0.0)
                ds = (p * (dps[h] - dl_ref[:, col:col + 1])).astype(BF16)
                dv_sc[h] += _dot(p.astype(BF16), do_ref[:, (h // 2) * LANES:(h // 2 + 1) * LANES], TN)
                dk_sc[h] += _dot(ds, q_ref[:, sl], TN)
                dq_ref[rows, sl] += _dot(ds, k_ref[:, sl], NN)

        pl.when(qi > ki)(functools.partial(block, False))
        pl.when(qi == ki)(functools.partial(block, True))

        @pl.when(qi == nq - 1)
        def _():
            first = lax.broadcasted_iota(jnp.int32, (tq, LANES), 1) < MLA_V
            for h in range(hp):
                dk_ref[:, h * LANES:(h + 1) * LANES] = dk_sc[h] * (1.0 / LOG2E)
            for pr in range(hp // 2):
                dv_ref[:, pr * LANES:(pr + 1) * LANES] = jnp.where(first, dv_sc[2 * pr], dv_sc[2 * pr + 1]).astype(BF16)

    qrow = lambda w: pl.BlockSpec((tq, w), lambda g, s, qt, kt: (qt[s], g))
    krow = lambda w: pl.BlockSpec((tq, w), lambda g, s, qt, kt: (kt[s], g))
    grid_spec = pltpu.PrefetchScalarGridSpec(
        num_scalar_prefetch=2, grid=(MLA_HEADS // hp, qi_tab.shape[0]),
        in_specs=[qrow(hp * LANES), krow(hp * LANES), krow(hp * MLA_V), qrow(hp * MLA_V), qrow(hp * MLA_V), qrow(hp * MLA_V)],
        out_specs=[pl.BlockSpec((t, hp * LANES), lambda g, s, qt, kt: (0, g)), krow(hp * LANES), krow(hp * MLA_V)],
        scratch_shapes=[pltpu.VMEM((hp, tq, LANES), F32), pltpu.VMEM((hp, tq, LANES), F32)],
    )
    return pl.pallas_call(
        body, name=name, grid_spec=grid_spec,
        out_shape=[jax.ShapeDtypeStruct((t, 1024), F32), jax.ShapeDtypeStruct((t, 1024), F32),
                   jax.ShapeDtypeStruct((t, 512), BF16)],
        compiler_params=_params(("parallel", "arbitrary")),
    )(qi_tab, ki_tab, q, k, v, do, lse, delta)


def _conv_rows(ext, tm, w_ref, sec):
    c0 = sec * 1024
    y = ext[CONV_HALO - 3:CONV_HALO - 3 + tm, c0:c0 + 1024] * w_ref[0:1, c0:c0 + 1024]
    for j in range(1, CONV_WIDTH):
        y = y + ext[CONV_HALO - 3 + j:CONV_HALO - 3 + j + tm, c0:c0 + 1024] * w_ref[j:j + 1, c0:c0 + 1024]
    return y


def _c_prep(proj_c, conv_w, a_log, dt_bias, *, name):
    t = proj_c.shape[0]
    tm = _tile(t, ROW_TILE)
    hb = tm // CONV_HALO

    def body(p_ref, halo_ref, ab_ref, w_ref, al_ref, dtb_ref, q_ref, k_ref, v_ref, g_ref, b_ref, gt_ref, ext):
        i = pl.program_id(0)
        ext[0:CONV_HALO, :] = jnp.where(i > 0, halo_ref[...], 0.0)
        ext[CONV_HALO:CONV_HALO + tm, :] = p_ref[...]
        for sec, o_ref in enumerate((q_ref, k_ref, v_ref)):
            y = _conv_rows(ext, tm, w_ref, sec)
            y = y * _sigmoid(y)
            if sec == 2:
                o_ref[...] = y
                continue
            scale = GDN_DK ** -0.5 if sec == 0 else 1.0
            for h in range(GDN_HEADS):
                sl = slice(h * LANES, (h + 1) * LANES)
                blk = y[:, sl]
                r = lax.rsqrt(jnp.sum(blk * blk, axis=-1, keepdims=True) + RMS_EPS)
                o_ref[:, sl] = blk * (r * scale)
        ab = ab_ref[...]
        g = -jnp.exp(al_ref[...]) * _softplus(ab + dtb_ref[...])
        beta = _sigmoid(ab)
        ri = lax.broadcasted_iota(jnp.int32, (tm, tm), 0)
        ci = lax.broadcasted_iota(jnp.int32, (tm, tm), 1)
        lower = ((ri // CHUNK) == (ci // CHUNK)) & (ri >= ci)
        gc = _dot(lower.astype(F32), g, NN, HI)
        eye = lax.broadcasted_iota(jnp.int32, (LANES, LANES), 0) == lax.broadcasted_iota(jnp.int32, (LANES, LANES), 1)
        gt_ref[...] = _dot(eye.astype(F32), gc, NT, HI)[0:GDN_HEADS, :]
        for h in range(GDN_HEADS):
            sl = slice(h * LANES, (h + 1) * LANES)
            g_ref[:, sl] = jnp.broadcast_to(gc[:, h:h + 1], (tm, LANES))
            b_ref[:, sl] = jnp.broadcast_to(beta[:, GDN_HEADS + h:GDN_HEADS + h + 1], (tm, LANES))

    row = lambda w: pl.BlockSpec((tm, w), lambda i: (i, 0))
    vec = lambda r, w: pl.BlockSpec((r, w), lambda i: (0, 0))
    out = jax.ShapeDtypeStruct((t, 1024), F32)
    return pl.pallas_call(
        body, name=name, grid=(t // tm,),
        in_specs=[row(3072), pl.BlockSpec((CONV_HALO, 3072), lambda i: (jnp.maximum(i * hb - 1, 0), 0)),
                  pl.BlockSpec((tm, LANES), lambda i: (i, 32)), vec(CONV_WIDTH, 3072), vec(1, LANES), vec(1, LANES)],
        out_specs=[row(1024)] * 5 + [pl.BlockSpec((GDN_HEADS, tm), lambda i: (0, i))],
        out_shape=[out] * 5 + [jax.ShapeDtypeStruct((GDN_HEADS, t), F32)],
        scratch_shapes=[pltpu.VMEM((tm + CONV_HALO, 3072), F32)],
        compiler_params=_params(("parallel",)),
    )(proj_c, proj_c, proj_c, conv_w, a_log, dt_bias)


def _c_prep_bwd(proj_c, conv_w, a_log, dt_bias, dq, dk, dv, dgb, dbb, dz, *, name):
    t = proj_c.shape[0]
    tm = _tile(t, ROW_TILE)
    hb = tm // CONV_HALO
    nt = t // tm
    rev = lambda i: nt - 1 - i

    def body(p_ref, halo_ref, ab_ref, w_ref, al_ref, dtb_ref, dq_ref, dk_ref, dv_ref, dg_ref, db_ref, dz_ref,
             dp_ref, dw_ref, dal_ref, ddt_ref, ext, dyext, carry, taps):
        step = pl.program_id(0)
        i = rev(step)

        @pl.when(step == 0)
        def _():
            dw_ref[...] = jnp.zeros_like(dw_ref)
            dal_ref[...] = jnp.zeros_like(dal_ref)
            ddt_ref[...] = jnp.zeros_like(ddt_ref)
            carry[...] = jnp.zeros_like(carry)

        ext[0:CONV_HALO, :] = jnp.where(i > 0, halo_ref[...], 0.0)
        ext[CONV_HALO:CONV_HALO + tm, :] = p_ref[...]
        for sec, g_ref in enumerate((dq_ref, dk_ref, dv_ref)):
            c0 = sec * 1024
            for j in range(CONV_WIDTH):
                taps[j] = ext[CONV_HALO - 3 + j:CONV_HALO - 3 + j + tm, c0:c0 + 1024]
            y = taps[0] * w_ref[0:1, c0:c0 + 1024]
            for j in range(1, CONV_WIDTH):
                y = y + taps[j] * w_ref[j:j + 1, c0:c0 + 1024]
            sg = _sigmoid(y)
            act = y * sg
            if sec == 2:
                dact = g_ref[...]
            else:
                scale = GDN_DK ** -0.5 if sec == 0 else 1.0
                parts = []
                for h in range(GDN_HEADS):
                    sl = slice(h * LANES, (h + 1) * LANES)
                    blk = act[:, sl]
                    r = lax.rsqrt(jnp.sum(blk * blk, axis=-1, keepdims=True) + RMS_EPS)
                    n = blk * r
                    dn = g_ref[:, sl] * scale
                    parts.append(r * (dn - n * jnp.sum(dn * n, axis=-1, keepdims=True)))
                dact = jnp.concatenate(parts, axis=-1)
            dy = dact * (sg * (1.0 + y * (1.0 - sg)))
            dyext[0:tm, c0:c0 + 1024] = dy
            for j in range(CONV_WIDTH):
                dw_ref[j:j + 1, c0:c0 + 1024] += jnp.sum(dy * taps[j], axis=0, keepdims=True)
        dyext[tm:tm + CONV_HALO, :] = carry[...]
        carry[...] = dyext[0:CONV_HALO, :]
        for sec in range(3):
            c0 = sec * 1024
            dx = dyext[3:3 + tm, c0:c0 + 1024] * w_ref[0:1, c0:c0 + 1024]
            for j in range(1, CONV_WIDTH):
                dx = dx + dyext[3 - j:3 - j + tm, c0:c0 + 1024] * w_ref[j:j + 1, c0:c0 + 1024]
            dp_ref[:, c0:c0 + 1024] = dx.astype(BF16)
        dp_ref[:, 3072:4096] = dz_ref[...]
        lane = lax.broadcasted_iota(jnp.int32, (tm, LANES), 1)
        dg = jnp.zeros((tm, LANES), F32)
        dbeta = jnp.zeros((tm, LANES), F32)
        for h in range(GDN_HEADS):
            sl = slice(h * LANES, (h + 1) * LANES)
            dg = dg + jnp.where(lane == h, dg_ref[:, sl], 0.0)
            dbeta = dbeta + jnp.where(lane == GDN_HEADS + h, db_ref[:, sl], 0.0)
        ri = lax.broadcasted_iota(jnp.int32, (tm, tm), 0)
        ci = lax.broadcasted_iota(jnp.int32, (tm, tm), 1)
        upper = ((ri // CHUNK) == (ci // CHUNK)) & (ri <= ci)
        dg = _dot(upper.astype(F32), dg, NN, HI)
        pre = ab_ref[...] + dtb_ref[...]
        s = _sigmoid(pre)
        a_exp = jnp.exp(al_ref[...])
        dg_da = dg * (-a_exp * s)
        dp_ref[:, 4096:IN_C_PAD] = (dg_da + dbeta * s * (1.0 - s)).astype(BF16)
        dal_ref[...] += jnp.sum(dg * (-a_exp * _softplus(pre)), axis=0, keepdims=True)
        ddt_ref[...] += jnp.sum(dg_da, axis=0, keepdims=True)

    row = lambda w: pl.BlockSpec((tm, w), lambda s: (rev(s), 0))
    vec = lambda r, w: pl.BlockSpec((r, w), lambda s: (0, 0))
    return pl.pallas_call(
        body, name=name, grid=(nt,),
        in_specs=[row(3072), pl.BlockSpec((CONV_HALO, 3072), lambda s: (jnp.maximum(rev(s) * hb - 1, 0), 0)),
                  pl.BlockSpec((tm, LANES), lambda s: (rev(s), 32)), vec(CONV_WIDTH, 3072), vec(1, LANES), vec(1, LANES),
                  row(1024), row(1024), row(1024), row(1024), row(1024), row(1024)],
        out_specs=[row(IN_C_PAD), vec(CONV_WIDTH, 3072), vec(1, LANES), vec(1, LANES)],
        out_shape=[jax.ShapeDtypeStruct((t, IN_C_PAD), BF16), jax.ShapeDtypeStruct((CONV_WIDTH, 3072), F32),
                   jax.ShapeDtypeStruct((1, LANES), F32), jax.ShapeDtypeStruct((1, LANES), F32)],
        scratch_shapes=[pltpu.VMEM((tm + CONV_HALO, 3072), F32), pltpu.VMEM((tm + CONV_HALO, 3072), F32),
                        pltpu.VMEM((CONV_HALO, 3072), F32), pltpu.VMEM((CONV_WIDTH, tm, 1024), F32)],
        compiler_params=_params(("arbitrary",)),
    )(proj_c, proj_c, proj_c, conv_w, a_log, dt_bias, dq, dk, dv, dgb, dbb, dz)


def _o_gate_bwd(dh, w, o, proj_c, o_norm, *, name):
    t = o.shape[0]
    tm = _tile(t, 2 * ROW_TILE)

    def body(dh_ref, w_ref, o_ref, z_ref, g_ref, do_ref, dz_ref, dg_ref, dy_ref):
        i = pl.program_id(0)

        @pl.when(i == 0)
        def _():
            dg_ref[...] = jnp.zeros_like(dg_ref)

        dy_ref[...] = _dot(dh_ref[...], w_ref[...], NT)
        dg = jnp.zeros((1, LANES), F32)
        for h in range(GDN_HEADS):
            sl = slice(h * LANES, (h + 1) * LANES)
            x = o_ref[:, sl]
            r = lax.rsqrt(jnp.mean(x * x, axis=-1, keepdims=True) + RMS_EPS)
            xh = x * r
            z = z_ref[:, sl]
            sg = _sigmoid(z)
            dyv = dy_ref[:, sl]
            dn = dyv * (z * sg)
            dz_ref[:, sl] = (dyv * xh * g_ref[...] * (sg * (1.0 + z * (1.0 - sg)))).astype(BF16)
            dxh = dn * g_ref[...]
            do_ref[:, sl] = r * (dxh - xh * jnp.mean(dxh * xh, axis=-1, keepdims=True))
            dg = dg + jnp.sum(dn * xh, axis=0, keepdims=True)
        dg_ref[...] += dg

    row = pl.BlockSpec((tm, 1024), lambda i: (i, 0))
    vec = pl.BlockSpec((1, LANES), lambda i: (0, 0))
    return pl.pallas_call(
        body, name=name, grid=(t // tm,),
        in_specs=[row, pl.BlockSpec(w.shape, lambda i: (0, 0)), row, pl.BlockSpec((tm, 1024), lambda i: (i, 3)), vec],
        out_specs=[row, row, vec],
        out_shape=[jax.ShapeDtypeStruct((t, 1024), F32), jax.ShapeDtypeStruct((t, 1024), BF16),
                   jax.ShapeDtypeStruct((1, LANES), F32)],
        scratch_shapes=[pltpu.VMEM((tm, 1024), F32)],
        compiler_params=_params(("arbitrary",)),
    )(dh, w, o, proj_c, o_norm)


PAIR = 2 * CHUNK
GDN_HP = 8


def _bdot(a, b, dims=NN):
    return _dot(a.astype(BF16), b.astype(BF16), dims)


def _each(f, *lists):
    return [f(*args) for args in zip(*lists)]


def _pair_common(q, k, v, gci, gcj, beta):
    ri = lax.broadcasted_iota(jnp.int32, (PAIR, PAIR), 0)
    ci = lax.broadcasted_iota(jnp.int32, (PAIR, PAIR), 1)
    same = (ri // CHUNK) == (ci // CHUNK)
    incl = same & (ri >= ci)
    strict = same & (ri > ci)
    eye = (ri == ci).astype(F32)
    first = lax.broadcasted_iota(jnp.int32, (PAIR, LANES), 0) < CHUNK
    gamma = _each(lambda gi, gj: jnp.where(incl, jnp.exp(jnp.minimum(gi - gj, 0.0)), 0.0), gci, gcj)
    kb = _each(jnp.multiply, k, beta)
    kk = _each(lambda a, b: _bdot(a, b, NT), kb, k)
    qk = _each(lambda a, b: _bdot(a, b, NT), q, k)
    m = _each(lambda x, g: jnp.where(strict, x * g, 0.0), kk, gamma)
    tm_ = _each(lambda x: eye - x, m)
    pw = _each(lambda x: _bdot(x, x), m)
    for it in range(5):
        if it < 4:
            both = _each(lambda x, p: _bdot(jnp.concatenate([x, p], axis=0), p), tm_, pw)
            tm_ = _each(lambda x, b: x + b[:PAIR], tm_, both)
            pw = _each(lambda b: b[PAIR:], both)
        else:
            tm_ = _each(lambda x, p: x + _bdot(x, p), tm_, pw)
    eg = _each(jnp.exp, gci)
    vb = _each(jnp.multiply, v, beta)
    kbe = _each(jnp.multiply, kb, eg)
    uw = _each(lambda x, a, b: _bdot(x, jnp.concatenate([a, b], axis=1)), tm_, vb, kbe)
    attn = _each(lambda x, g: jnp.where(incl, x * g, 0.0), qk, gamma)
    gl_a = _each(lambda g: g[CHUNK - 1:CHUNK, :], gci)
    gl_b = _each(lambda g: g[PAIR - 1:PAIR, :], gci)
    ek = _each(lambda a, b, g: jnp.exp(jnp.where(first, a, b) - g), gl_a, gl_b, gci)
    return dict(incl=incl, strict=strict, gamma=gamma, kb=kb, m=m, tm=tm_, eg=eg, vb=vb, kbe=kbe,
                u=_each(lambda x: x[:, :LANES], uw), w=_each(lambda x: x[:, LANES:], uw), attn=attn,
                qd=_each(jnp.multiply, q, eg), ek=ek, kd=_each(jnp.multiply, k, ek),
                glast_a=_each(jnp.exp, gl_a), glast_b=_each(jnp.exp, gl_b))


def _gdn_specs(t, ts, order):
    nc = ts // CHUNK
    blk = pl.BlockSpec((ts, GDN_HP * LANES), lambda h, s: (order(s), h))
    row = pl.BlockSpec((GDN_HP, 1, ts), lambda h, s: (h, 0, order(s)))
    st = pl.BlockSpec((GDN_HP, nc, LANES, LANES), lambda h, s: (h, order(s), 0, 0))
    return blk, row, st


def _gdn_fwd(q, k, v, gcb, gct, bb, *, name):
    t = q.shape[0]
    ts = _tile(t, GDN_TILE)
    npair = ts // PAIR

    def body(q_ref, k_ref, v_ref, g_ref, gt_ref, b_ref, o_ref, st_ref, s_sc):
        @pl.when(pl.program_id(1) == 0)
        def _():
            s_sc[...] = jnp.zeros_like(s_sc)

        def pair(pi, _):
            rows = pl.ds(pl.multiple_of(pi * PAIR, PAIR), PAIR)
            heads = [slice(hh * LANES, (hh + 1) * LANES) for hh in range(GDN_HP)]
            c = CHUNK
            cat0 = lambda *xs: jnp.concatenate(xs, axis=0)
            s0 = [s_sc[hh] for hh in range(GDN_HP)]
            cm = _pair_common([q_ref[rows, sl] for sl in heads], [k_ref[rows, sl] for sl in heads],
                              [v_ref[rows, sl] for sl in heads], [g_ref[rows, sl] for sl in heads],
                              [gt_ref[hh, :, rows] for hh in range(GDN_HP)], [b_ref[rows, sl] for sl in heads])
            u, w, qd, kd = cm["u"], cm["w"], cm["qd"], cm["kd"]
            r0 = _each(lambda w_, q_, s: _bdot(cat0(w_[:c], q_[:c]), s), w, qd, s0)
            vn_a = _each(lambda u_, r: u_[:c] - r[:c], u, r0)
            s1 = _each(lambda s, gl, k_, vn: s * gl + _bdot(k_[:c], vn, TN), s0, cm["glast_a"], kd, vn_a)
            r1 = _each(lambda w_, q_, s: _bdot(cat0(w_[c:], q_[c:]), s), w, qd, s1)
            vn_b = _each(lambda u_, r: u_[c:] - r[:c], u, r1)
            s2 = _each(lambda s, gl, k_, vn: s * gl + _bdot(k_[c:], vn, TN), s1, cm["glast_b"], kd, vn_b)
            o = _each(lambda ra, rb, at, va, vb_: cat0(ra[c:], rb[c:]) + _bdot(at, cat0(va, vb_)),
                      r0, r1, cm["attn"], vn_a, vn_b)
            for hh, sl in enumerate(heads):
                st_ref[hh, 2 * pi] = s0[hh]
                st_ref[hh, 2 * pi + 1] = s1[hh]
                s_sc[hh] = s2[hh]
                o_ref[rows, sl] = o[hh]
            return 0

        lax.fori_loop(0, npair, pair, 0)

    blk, row, st = _gdn_specs(t, ts, lambda s: s)
    return pl.pallas_call(
        body, name=name, grid=(GDN_HEADS // GDN_HP, t // ts), in_specs=[blk, blk, blk, blk, row, blk],
        out_specs=[blk, st],
        out_shape=[jax.ShapeDtypeStruct((t, 1024), F32), jax.ShapeDtypeStruct((GDN_HEADS, t // CHUNK, LANES, LANES), F32)],
        scratch_shapes=[pltpu.VMEM((GDN_HP, LANES, LANES), F32)],
        compiler_params=_params(("parallel", "arbitrary")),
    )(q, k, v, gcb, gct, bb)


def _gdn_bwd(q, k, v, gcb, gct, bb, do, states, *, name):
    t = q.shape[0]
    ts = _tile(t, GDN_TILE)
    npair = ts // PAIR
    ns = t // ts
    c = CHUNK

    def body(q_ref, k_ref, v_ref, g_ref, gt_ref, b_ref, do_ref, st_ref, dq_ref, dk_ref, dv_ref, dg_ref, db_ref, ds_sc):
        @pl.when(pl.program_id(1) == 0)
        def _():
            ds_sc[...] = jnp.zeros_like(ds_sc)

        rowsum = lambda x: jnp.sum(x, axis=-1, keepdims=True)
        total = lambda x: jnp.sum(rowsum(x), axis=0, keepdims=True)
        cat0 = lambda *xs: jnp.concatenate(xs, axis=0)
        cat1 = lambda *xs: jnp.concatenate(xs, axis=1)

        def pair(step, _):
            pi = npair - 1 - step
            rows = pl.ds(pl.multiple_of(pi * PAIR, PAIR), PAIR)
            heads = [slice(hh * LANES, (hh + 1) * LANES) for hh in range(GDN_HP)]
            hs = range(GDN_HP)
            qv, kv, vv = ([r[rows, sl] for sl in heads] for r in (q_ref, k_ref, v_ref))
            beta = [b_ref[rows, sl] for sl in heads]
            dov = [do_ref[rows, sl] for sl in heads]
            s0 = [st_ref[hh, 2 * pi] for hh in hs]
            s1 = [st_ref[hh, 2 * pi + 1] for hh in hs]
            ds2 = [ds_sc[hh] for hh in hs]
            cm = _pair_common(qv, kv, vv, [g_ref[rows, sl] for sl in heads], [gt_ref[hh, :, rows] for hh in hs], beta)
            u, w, qd, kd, attn = cm["u"], cm["w"], cm["qd"], cm["kd"], cm["attn"]
            tmat, gamma, eg = cm["tm"], cm["gamma"], cm["eg"]
            incl, strict = cm["incl"], cm["strict"]
            vn_a = _each(lambda u_, w_, s: u_[:c] - _bdot(w_[:c], s), u, w, s0)
            vn_b = _each(lambda u_, w_, s: u_[c:] - _bdot(w_[c:], s), u, w, s1)
            vn = _each(cat0, vn_a, vn_b)
            dvn_att = _each(lambda a, d: _bdot(a, d, TN), attn, dov)
            dattn = _each(lambda d, v_: jnp.where(incl, _bdot(d, v_, NT), 0.0), dov, vn)
            dvn_b = _each(lambda x, k_, d: x[c:] + _bdot(k_[c:], d), dvn_att, kd, ds2)
            rb = _each(lambda d, x, s: _bdot(cat0(d[c:], x), s, NT), dov, dvn_b, s1)
            dkd_b = _each(lambda v_, d: _bdot(v_, d, NT), vn_b, ds2)
            dgl_b = _each(lambda d, s: total(d * s), ds2, s1)
            ds1 = _each(lambda d, gl, q_, w_, o_, x: d * gl + _bdot(cat0(q_[c:], w_[c:]), cat0(o_[c:], -x), TN),
                        ds2, cm["glast_b"], qd, w, dov, dvn_b)
            dvn_a = _each(lambda x, k_, d: x[:c] + _bdot(k_[:c], d), dvn_att, kd, ds1)
            ra = _each(lambda d, x, s: _bdot(cat0(d[:c], x), s, NT), dov, dvn_a, s0)
            dkd_a = _each(lambda v_, d: _bdot(v_, d, NT), vn_a, ds1)
            dgl_a = _each(lambda d, s: total(d * s), ds1, s0)
            ds0 = _each(lambda d, gl, q_, w_, o_, x: d * gl + _bdot(cat0(q_[:c], w_[:c]), cat0(o_[:c], -x), TN),
                        ds1, cm["glast_a"], qd, w, dov, dvn_a)
            dvn = _each(cat0, dvn_a, dvn_b)
            dqd = _each(lambda a, b: cat0(a[:c], b[:c]), ra, rb)
            dw = _each(lambda a, b: -cat0(a[c:], b[c:]), ra, rb)
            dkd = _each(cat0, dkd_a, dkd_b)
            dvw = _each(cat1, dvn, dw)
            dvbk = _each(lambda t_, x: _bdot(t_, x, TN), tmat, dvw)
            dvb = _each(lambda x: x[:, :LANES], dvbk)
            dkbe = _each(lambda x: x[:, LANES:], dvbk)
            dt_ = _each(lambda x, a, b: _bdot(x, cat1(a, b), NT), dvw, cm["vb"], cm["kbe"])
            da1 = _each(lambda t_, x: _bdot(t_, x, TN), tmat, dt_)
            dm = _each(lambda x, t_: jnp.where(strict, -_bdot(x, t_, NT), 0.0), da1, tmat)
            dkk = _each(jnp.multiply, dm, gamma)
            dqk = _each(jnp.multiply, dattn, gamma)
            z = _each(lambda a, b, c_, d: a * b + c_ * d, dm, cm["m"], dattn, attn)
            dkb = _each(lambda x, k_, y, e: _bdot(x, k_) + y * e, dkk, kv, dkbe, eg)
            dk = _each(lambda a, b, kb_, q_, x, e, y, be: _bdot(cat0(a, b), cat0(kb_, q_), TN) + x * e + y * be,
                       dkk, dqk, cm["kb"], qv, dkd, cm["ek"], dkb, beta)
            dq = _each(lambda x, k_, y, e: _bdot(x, k_) + y * e, dqk, kv, dqd, eg)

            def colsum_of(z_):
                zh = z_.astype(BF16)
                zl = (z_ - zh.astype(F32)).astype(BF16)
                return _dot(cat0(zh, zl), jnp.ones((2 * PAIR, LANES), BF16), TN)

            colsum = _each(colsum_of, z)
            ri = lax.broadcasted_iota(jnp.int32, (PAIR, LANES), 0)
            for hh, sl in enumerate(heads):
                dkd_kd = dkd[hh] * kd[hh]
                dgc = (rowsum(z[hh]) - colsum[hh] + rowsum(dqd[hh] * qd[hh]) - rowsum(dkd_kd)
                       + rowsum(dkbe[hh] * cm["kbe"][hh]))
                last_a = total(dkd_kd[:c]) + dgl_a[hh] * cm["glast_a"][hh]
                last_b = total(dkd_kd[c:]) + dgl_b[hh] * cm["glast_b"][hh]
                dgc = dgc + jnp.where(ri == c - 1, last_a, 0.0) + jnp.where(ri == PAIR - 1, last_b, 0.0)
                ds_sc[hh] = ds0[hh]
                dq_ref[rows, sl] = dq[hh]
                dk_ref[rows, sl] = dk[hh]
                dv_ref[rows, sl] = dvb[hh] * beta[hh]
                db_ref[rows, sl] = jnp.broadcast_to(rowsum(dkb[hh] * kv[hh]) + rowsum(dvb[hh] * vv[hh]), (PAIR, LANES))
                dg_ref[rows, sl] = dgc
            return 0

        lax.fori_loop(0, npair, pair, 0)

    blk, row, st = _gdn_specs(t, ts, lambda s: ns - 1 - s)
    out = jax.ShapeDtypeStruct((t, 1024), F32)
    return pl.pallas_call(
        body, name=name, grid=(GDN_HEADS // GDN_HP, ns), in_specs=[blk, blk, blk, blk, row, blk, blk, st],
        out_specs=[blk] * 5, out_shape=[out] * 5, scratch_shapes=[pltpu.VMEM((GDN_HP, LANES, LANES), F32)],
        compiler_params=_params(("parallel", "arbitrary")),
    )(q, k, v, gcb, gct, bb, do, states)


def _gate_out_proj_loss(o, proj_c, o_norm, w, hres, g, target, *, name):
    t, d = hres.shape
    tm = _tile(t, 2 * ROW_TILE)

    def body(o_ref, z_ref, on_ref, w_ref, h_ref, g_ref, t_ref, dh_ref, dhb_ref, dg_ref, loss_ref, dw_ref, y_ref):
        i = pl.program_id(0)
        for hd in range(GDN_HEADS):
            sl = slice(hd * LANES, (hd + 1) * LANES)
            ov = o_ref[:, sl]
            rr = lax.rsqrt(jnp.mean(ov * ov, axis=-1, keepdims=True) + RMS_EPS)
            z = z_ref[:, sl]
            y_ref[:, sl] = (ov * rr * on_ref[...] * (z * _sigmoid(z))).astype(BF16)
        x = h_ref[...] + _dot(y_ref[...], w_ref[...])
        r = lax.rsqrt(jnp.mean(x * x, axis=-1, keepdims=True) + RMS_EPS)
        xh = x * r
        err = xh * g_ref[...] - t_ref[...]
        dy = err * (1.0 / d)
        dxh = dy * g_ref[...]
        dh = r * (dxh - xh * jnp.mean(dxh * xh, axis=-1, keepdims=True))
        dh_ref[...] = dh
        dhb_ref[...] = dh.astype(BF16)

        @pl.when(i == 0)
        def _():
            dg_ref[...] = jnp.zeros_like(dg_ref)
            loss_ref[...] = jnp.zeros_like(loss_ref)
            dw_ref[...] = jnp.zeros_like(dw_ref)

        dg_ref[...] += jnp.sum(dy * xh, axis=0, keepdims=True)
        part = 0.5 * jnp.sum(jnp.mean(err * err, axis=-1, keepdims=True), axis=0, keepdims=True)
        loss_ref[...] += jnp.broadcast_to(part, loss_ref.shape)
        dw_ref[...] += _dot(y_ref[...], dhb_ref[...], TN)

    row = pl.BlockSpec((tm, d), lambda i: (i, 0))
    vec = pl.BlockSpec((1, d), lambda i: (0, 0))
    return pl.pallas_call(
        body, name=name, grid=(t // tm,),
        in_specs=[row, pl.BlockSpec((tm, 1024), lambda i: (i, 3)), pl.BlockSpec((1, LANES), lambda i: (0, 0)),
                  pl.BlockSpec(w.shape, lambda i: (0, 0)), row, vec, row],
        out_specs=[row, row, vec, pl.BlockSpec((8, LANES), lambda i: (0, 0)), pl.BlockSpec(w.shape, lambda i: (0, 0))],
        out_shape=[jax.ShapeDtypeStruct((t, d), F32), jax.ShapeDtypeStruct((t, d), BF16),
                   jax.ShapeDtypeStruct((1, d), F32), jax.ShapeDtypeStruct((8, LANES), F32),
                   jax.ShapeDtypeStruct(w.shape, F32)],
        scratch_shapes=[pltpu.VMEM((tm, d), BF16)],
        compiler_params=_params(("arbitrary",)),
    )(o, proj_c, o_norm, w, hres, g, target)


def _pad_cols(w, n):
    return jnp.pad(w, ((0, 0), (0, n - w.shape[1])))


def _layout_odd(w):
    return dict(
        winc=_pad_cols(w["w_in_c"], IN_C_PAD).astype(BF16), wout_c=w["w_out_c"].astype(BF16), conv_w=w["conv_w"],
        a_log=_pad_cols(w["a_log"], LANES), dt_bias=_pad_cols(w["dt_bias"], LANES),
        norm_c=w["norm_c"], o_norm=w["o_norm"], final_norm=w["final_norm"],
    )


def _layout_in_ab(w):
    z = lambda r, c: jnp.zeros((r, c), w["w_in_ab"].dtype)
    wi = w["w_in_ab"]
    win = jnp.concatenate([wi[:, :384], z(1024, 64), wi[:, 384:416], z(1024, 32), wi[:, 416:]], axis=1)
    pw = w["pool_w"]
    rows = []
    for g in range(4):
        rows.append(jnp.concatenate([pw[g] if j == g else jnp.zeros((128, 128), F32) for j in range(4)], axis=1))
    wpool = jnp.concatenate(rows, axis=0)
    half = MLA_ROPE // 2
    inv = 1.0 / (ROPE_THETA ** (jnp.arange(half, dtype=F32) / half))
    inv_lane = jnp.concatenate([jnp.zeros((MLA_NOPE,), F32), inv, inv, jnp.zeros((32,), F32)]).reshape(1, LANES)
    return dict(win=win.astype(BF16), wpool=wpool.astype(BF16), inv_lane=inv_lane, norm_ab=w["norm_ab"],
                q_a_norm=w["q_a_norm"], kv_a_norm=w["kv_a_norm"], pool_scale=w["pool_scale"])


def _layout_mid(w):
    wq = jnp.pad(w["w_q_b"].reshape(MLA_Q_RANK, MLA_HEADS, 96), ((0, 0), (0, 0), (0, 32))).reshape(MLA_Q_RANK, 1024)
    kv3 = w["w_kv_b"].reshape(MLA_KV_RANK, MLA_HEADS, 128)
    wk = jnp.pad(kv3[..., :MLA_NOPE], ((0, 0), (0, 0), (0, 64))).reshape(MLA_KV_RANK, 1024)
    wv = kv3[..., MLA_NOPE:].reshape(MLA_KV_RANK, 512)
    return dict(wq=wq.astype(BF16), wk=wk.astype(BF16), wv=wv.astype(BF16), wout_ab=w["w_out_ab"].astype(BF16))


def _unlayout_grads(g, names):
    out = {}
    for name in names:
        if name == "w_in_ab":
            dwin = g["win"]
            out[name] = jnp.concatenate([dwin[:384], dwin[448:480], dwin[512:]], axis=0)
        elif name == "w_q_b":
            out[name] = g["wq"].reshape(MLA_Q_RANK, MLA_HEADS, 128)[..., :96].reshape(MLA_Q_RANK, 768)
        elif name == "w_kv_b":
            out[name] = jnp.concatenate([g["wk"].reshape(MLA_KV_RANK, MLA_HEADS, 128)[..., :MLA_NOPE],
                                         g["wv"].reshape(MLA_KV_RANK, MLA_HEADS, MLA_V)], axis=-1).reshape(MLA_KV_RANK, 1024)
        elif name == "w_in_c":
            out[name] = g["winc"][:4112]
        else:
            out[name] = g[{"w_out_ab": "wout_ab", "w_out_c": "wout_c"}[name]]
    return out


def _local_step(x, pos, target, lw, more_weights=None, on_grads=None):
    mm = _matmul
    proj, hn = _rms_in_proj(x, lw["norm_ab"], lw["win"], name="rms_in_ab")
    if more_weights is not None:
        lw = {**lw, **more_weights("mid", proj)}
    q, k, v, ybraw, qn, kvn, d, cos_t, sin_t = _ab_prep(
        proj, pos, lw["inv_lane"], lw["q_a_norm"], lw["kv_a_norm"], lw["wq"], lw["wk"], lw["wv"], lw["wpool"], name="ab_prep")
    o, lse = _attn_fwd(q, k, v, name="attn_fwd")
    h1, y = _gate_out_proj(o, ybraw, proj, lw["pool_scale"], lw["wout_ab"], x, name="gate_out_ab")
    lo = lw if more_weights is None else more_weights("odd", h1)
    proj_c, hn1 = _rms_in_proj(h1, lo["norm_c"], lo["winc"], name="rms_in_c")
    q2, k2, v2, gb, bb, gt = _c_prep(proj_c, lo["conv_w"], lo["a_log"], lo["dt_bias"], name="c_prep")
    gt = gt.reshape(GDN_HEADS, 1, gt.shape[1])
    o2, states = _gdn_fwd(q2, k2, v2, gb, gt, bb, name="gdn_fwd")
    dh2, dh2b, d_final, loss, d_wout_c = _gate_out_proj_loss(
        o2, proj_c, lo["o_norm"], lo["wout_c"], h1, lo["final_norm"], target, name="gate_out_c_loss")
    g = {"final_norm": d_final, "wout_c": d_wout_c, "loss": loss}
    do2, dz2, g["o_norm"] = _o_gate_bwd(dh2b, lo["wout_c"], o2, proj_c, lo["o_norm"], name="gate_c_bwd")
    dq2, dk2, dv2, dgb, dbb = _gdn_bwd(q2, k2, v2, gb, gt, bb, do2, states, name="gdn_bwd")
    dproj_c, g["conv_w"], g["a_log"], g["dt_bias"] = _c_prep_bwd(
        proj_c, lo["conv_w"], lo["a_log"], lo["dt_bias"], dq2, dk2, dv2, dgb, dbb, dz2, name="c_prep_bwd")
    g["winc"] = mm(dproj_c, hn1, "tn", name="in_c_dw")
    notify = (lambda tag, after=None: 0.0) if on_grads is None else (lambda tag, after=None: on_grads(tag, g, after))
    dh1, g["norm_c"], dh1b, g["wout_ab"] = _matmul_rms_bwd(
        dproj_c, lo["winc"], h1, lo["norm_c"] + notify("odd"), dh2, name="in_c_dx_rms", prev_y=y)
    pool_scale = lw["pool_scale"] + notify("odd_go", dh1b) + notify("out_ab")
    do, delta, dyb, dz, g["pool_scale"] = _gate_bwd(dh1b, lw["wout_ab"], o, ybraw, proj, pool_scale, name="gate_ab_bwd")
    dq, dk, dv = _attn_bwd(q, k, v, do, lse, delta, name="attn_bwd")
    dproj, g["q_a_norm"], g["kv_a_norm"], g["wq"], g["wk"], g["wv"], g["wpool"], g["win"] = _ab_prep_bwd(
        proj, lw["q_a_norm"], lw["kv_a_norm"], dq, dk, dv, cos_t, sin_t, dyb, dz, qn, kvn, d, hn,
        lw["wq"], lw["wk"], lw["wv"], lw["wpool"], name="ab_prep_bwd")
    norm_ab = lw["norm_ab"] + notify("in_ab")
    dx, g["norm_ab"] = _matmul_rms_bwd(dproj, lw["win"], x, norm_ab, dh1, name="in_ab_dx_rms")
    return loss, dx, g


_HBM = pl.BlockSpec(memory_space=pltpu.HBM)


def _place():
    return lax.axis_index("x"), lax.axis_index("y"), lax.axis_index("c")


def _flip(v, f):
    return 1 - v if f else v


_CHIP_FLIPS = ((1, 0), (0, 1), (1, 1))
_DEV_FLIPS = tuple((fx, fy, fc) for fx in (0, 1) for fy in (0, 1) for fc in (0, 1) if fx or fy or fc)


def _rcopy(src, dst, send_sems, recv_sems, k, to):
    return pltpu.make_async_remote_copy(src_ref=src, dst_ref=dst, send_sem=send_sems.at[k], recv_sem=recv_sems.at[k],
                                        device_id=to, device_id_type=MESH)


def _my_half(ref, c, axis):
    rh = ref.shape[axis] // 2
    idx = [slice(None)] * len(ref.shape)
    idx[axis] = pl.ds(c * rh, rh)
    return ref.at[tuple(idx)]


def _gather_weights(bigs, smalls):
    nb, ns = len(bigs), len(smalls)

    def body(*refs):
        ins, outs = refs[:nb + ns], refs[nb + ns:2 * (nb + ns)]
        send_sems, recv_sems, local_sems = refs[2 * (nb + ns):]
        x, y, c = _place()
        j0 = 2 * x + y
        sib = (x, y, 1 - c)
        chips = [(_flip(x, fx), _flip(y, fy)) for fx, fy in _CHIP_FLIPS]
        local = [pltpu.make_async_copy(i_ref, o_ref.at[j0], local_sems.at[a])
                 for a, (i_ref, o_ref) in enumerate(zip(ins, outs))]
        for cp in local:
            cp.start()
        sends = []
        for k, (px, py) in enumerate(chips):
            for a in range(nb):
                sends.append(_rcopy(_my_half(ins[a], c, 0), _my_half(outs[a].at[j0], c, 0), send_sems, recv_sems,
                                    6 * a + k, (px, py, c)))
            for s in range(ns):
                sends.append(_rcopy(ins[nb + s], outs[nb + s].at[j0], send_sems, recv_sems, 6 * nb + 3 * s + k, (px, py, c)))
        for cp in sends:
            cp.start()
        for k, (px, py) in enumerate(chips):
            jk = 2 * px + py
            for a in range(nb):
                landed = _my_half(outs[a].at[jk], c, 0)
                _rcopy(landed, landed, send_sems, recv_sems, 6 * a + k, (px, py, c)).wait_recv()
                fwd = _rcopy(landed, landed, send_sems, recv_sems, 6 * a + 3 + k, sib)
                fwd.start()
                sends.append(fwd)
        for k, (px, py) in enumerate(chips):
            jk = 2 * px + py
            for a in range(nb):
                other = _my_half(outs[a].at[jk], 1 - c, 0)
                _rcopy(other, other, send_sems, recv_sems, 6 * a + 3 + k, sib).wait_recv()
            for s in range(ns):
                _rcopy(ins[nb + s], outs[nb + s].at[jk], send_sems, recv_sems, 6 * nb + 3 * s + k, (px, py, c)).wait_recv()
        for cp in sends:
            cp.wait_send()
        for cp in local:
            cp.wait()

    arrays = list(bigs) + list(smalls)
    n_sem = 6 * nb + 3 * ns
    return pl.pallas_call(
        body, name="gather_weights", in_specs=[_HBM] * len(arrays), out_specs=[_HBM] * len(arrays),
        out_shape=[jax.ShapeDtypeStruct((4,) + a.shape, a.dtype) for a in arrays],
        scratch_shapes=[pltpu.SemaphoreType.DMA((n_sem,)), pltpu.SemaphoreType.DMA((n_sem,)),
                        pltpu.SemaphoreType.DMA((len(arrays),))],
    )(*arrays)


def _core_swap_partial(gs, by_cols, *, name):
    n = len(gs)

    def body(*refs):
        ins, outs = refs[:n], refs[n:2 * n]
        send_sems, recv_sems = refs[2 * n:]
        x, y, c = _place()
        copies = [_rcopy(_my_half(i_ref, 1 - c, 2 if by_cols[a] else 1), o_ref, send_sems, recv_sems, a, (x, y, 1 - c))
                  for a, (i_ref, o_ref) in enumerate(zip(ins, outs))]
        for cp in copies:
            cp.start()
        for cp in copies:
            cp.wait()

    halved = lambda g, cols: (4, g.shape[1], g.shape[2] // 2) if cols else (4, g.shape[1] // 2, g.shape[2])
    return pl.pallas_call(
        body, name=name, in_specs=[_HBM] * n, out_specs=[_HBM] * n,
        out_shape=[jax.ShapeDtypeStruct(halved(g, cols), g.dtype) for g, cols in zip(gs, by_cols)],
        scratch_shapes=[pltpu.SemaphoreType.DMA((n,)), pltpu.SemaphoreType.DMA((n,))],
    )(*gs)


def _core_swap_partial_start(gs, by_cols, *, name):
    n = len(gs)
    halved = lambda g, cols: (4, g.shape[1], g.shape[2] // 2) if cols else (4, g.shape[1] // 2, g.shape[2])
    lands = [lax.empty(halved(g, cols), g.dtype) for g, cols in zip(gs, by_cols)]

    def body(*refs):
        ins, land_refs, send_sems, recv_sems, token = refs[:n], refs[n:2 * n], refs[2 * n], refs[2 * n + 1], refs[-1]
        x, y, c = _place()
        for a in range(n):
            _rcopy(_my_half(ins[a], 1 - c, 2 if by_cols[a] else 1), land_refs[a], send_sems, recv_sems, a,
                   (x, y, 1 - c)).start()
        token[...] = jnp.zeros_like(token)

    held = [pltpu.with_memory_space_constraint(a, pltpu.HBM) for a in list(gs) + lands]
    return pl.pallas_call(
        body, name=name, in_specs=[_HBM] * (2 * n),
        out_specs=(_SEM, _SEM, *[_HBM] * (2 * n), pl.BlockSpec(memory_space=pltpu.VMEM)),
        out_shape=(pltpu.SemaphoreType.DMA((n,)), pltpu.SemaphoreType.DMA((n,)),
                   *[pltpu.HBM(a.shape, a.dtype) for a in held], jax.ShapeDtypeStruct((8, LANES), F32)),
        input_output_aliases={i: 2 + i for i in range(2 * n)},
        compiler_params=pltpu.CompilerParams(has_side_effects=_DATAFLOW),
    )(*held)


def _core_swap_partial_wait(started, after, by_cols, *, name):
    send_sems, recv_sems, held = started[0], started[1], started[2:-1]
    n = len(held) // 2

    def body(*refs):
        ins, land_refs, s_sems, r_sems = refs[:n], refs[n:2 * n], refs[2 * n], refs[2 * n + 1]
        x, y, c = _place()
        for a in range(n):
            cp = _rcopy(_my_half(ins[a], 1 - c, 2 if by_cols[a] else 1), land_refs[a], s_sems, r_sems, a, (x, y, 1 - c))
            cp.wait_send()
            cp.wait_recv()

    out = pl.pallas_call(
        body, name=name, in_specs=[_HBM] * (2 * n) + [_SEM, _SEM, _ANY], out_specs=[_HBM] * (2 * n),
        out_shape=[pltpu.HBM(a.shape, a.dtype) for a in held],
        input_output_aliases={i: i for i in range(2 * n)},
        compiler_params=pltpu.CompilerParams(has_side_effects=_DATAFLOW),
    )(*held, send_sems, recv_sems, after)
    return out[:n], out[n:]


def _core_swap_sum(fs, by_cols):
    n = len(fs)

    def body(*refs):
        ins, outs = refs[:n], refs[n:2 * n]
        send_sems, recv_sems = refs[2 * n:]
        x, y, c = _place()
        axes = [1 if cols else 0 for cols in by_cols]
        copies = [_rcopy(_my_half(i_ref, c, ax), _my_half(o_ref, c, ax), send_sems, recv_sems, a, (x, y, 1 - c))
                  for a, (i_ref, o_ref, ax) in enumerate(zip(ins, outs, axes))]
        for cp in copies:
            cp.start()
        for a, cp in enumerate(copies):
            cp.wait_send()
            theirs = _my_half(outs[a], 1 - c, axes[a])
            _rcopy(theirs, theirs, send_sems, recv_sems, a, (x, y, 1 - c)).wait_recv()

    return pl.pallas_call(
        body, name="core_swap_sum", in_specs=[_HBM] * n, out_specs=[_HBM] * n,
        out_shape=[jax.ShapeDtypeStruct(f.shape, f.dtype) for f in fs],
        input_output_aliases={a: a for a in range(n)},
        scratch_shapes=[pltpu.SemaphoreType.DMA((n,)), pltpu.SemaphoreType.DMA((n,))],
    )(*fs)


_SEM = pl.BlockSpec(memory_space=pltpu.SEMAPHORE)
_ANY = pl.BlockSpec(memory_space=pl.ANY)
_DATAFLOW = pltpu.SideEffectType.DATAFLOW_SIDE_EFFECTING


def _peer_flips(to_all):
    return _DEV_FLIPS if to_all else tuple((fx, fy, 0) for fx, fy in _CHIP_FLIPS)


def _to_chips_copies(srcs, lands, send_sems, recv_sems, per_chip_slot, to_all=False):
    x, y, c = _place()
    flips = _peer_flips(to_all)
    index = (lambda px, py, pc: 4 * px + 2 * py + pc) if to_all else (lambda px, py, pc: 2 * px + py)
    me = index(x, y, c)
    out = []
    for k, (fx, fy, fc) in enumerate(flips):
        peer = (_flip(x, fx), _flip(y, fy), _flip(c, fc))
        theirs = index(*peer)
        for a, (src, land) in enumerate(zip(srcs, lands)):
            piece = src.at[theirs] if per_chip_slot else src
            out.append((_rcopy(piece, land.at[me], send_sems, recv_sems, len(flips) * a + k, peer),
                        _rcopy(piece, land.at[theirs], send_sems, recv_sems, len(flips) * a + k, peer)))
    return out


def _to_chips_start(arrays, *, per_chip_slot, name, after=None, to_all=False):
    n = len(arrays)
    peers = len(_peer_flips(to_all))
    lands = [lax.empty((peers + 1,) + (a.shape[1:] if per_chip_slot else a.shape), a.dtype) for a in arrays]
    extra = [] if after is None else [after]

    def body(*refs):
        srcs, land_refs, token = refs[:n], refs[n:2 * n], refs[-1]
        send_sems, recv_sems = refs[2 * n + len(extra)], refs[2 * n + len(extra) + 1]
        for send, _ in _to_chips_copies(srcs, land_refs, send_sems, recv_sems, per_chip_slot, to_all):
            send.start()
        token[...] = jnp.zeros_like(token)

    held = [pltpu.with_memory_space_constraint(a, pltpu.HBM) for a in list(arrays) + lands]
    return pl.pallas_call(
        body, name=name, in_specs=[_HBM] * (2 * n) + [_ANY] * len(extra),
        out_specs=(_SEM, _SEM, *[_HBM] * (2 * n), pl.BlockSpec(memory_space=pltpu.VMEM)),
        out_shape=(pltpu.SemaphoreType.DMA((peers * n,)), pltpu.SemaphoreType.DMA((peers * n,)),
                   *[pltpu.HBM(a.shape, a.dtype) for a in held], jax.ShapeDtypeStruct((8, LANES), F32)),
        input_output_aliases={i: 2 + i for i in range(2 * n)},
        compiler_params=pltpu.CompilerParams(has_side_effects=_DATAFLOW),
    )(*held, *extra)


def _to_chips_wait(started, after, *, per_chip_slot, name, to_all=False):
    send_sems, recv_sems, held = started[0], started[1], started[2:-1]
    n = len(held) // 2

    def body(*refs):
        srcs, land_refs, s_sems, r_sems = refs[:n], refs[n:2 * n], refs[2 * n], refs[2 * n + 1]
        for send, arrival in _to_chips_copies(srcs, land_refs, s_sems, r_sems, per_chip_slot, to_all):
            send.wait_send()
            arrival.wait_recv()

    out = pl.pallas_call(
        body, name=name, in_specs=[_HBM] * (2 * n) + [_SEM, _SEM, _ANY], out_specs=[_HBM] * (2 * n),
        out_shape=[pltpu.HBM(a.shape, a.dtype) for a in held],
        input_output_aliases={i: i for i in range(2 * n)},
        compiler_params=pltpu.CompilerParams(has_side_effects=_DATAFLOW),
    )(*held, send_sems, recv_sems, after)
    return out[n:]


def _chip_exchange(ps, small):
    n = len(ps)
    rs = small.shape[0]

    def body(*refs):
        p_refs, s_ref = refs[:n], refs[n]
        l_refs, ls_ref = refs[n + 1:2 * n + 1], refs[2 * n + 1]
        send_sems, recv_sems, local_sems = refs[2 * n + 2:]
        x, y, c = _place()
        j0 = 2 * x + y
        d0 = 2 * j0 + c
        local = [pltpu.make_async_copy(p.at[j0], l.at[j0], local_sems.at[a]) for a, (p, l) in enumerate(zip(p_refs, l_refs))]
        local.append(pltpu.make_async_copy(s_ref, ls_ref.at[d0], local_sems.at[n]))
        for cp in local:
            cp.start()
        sends = []
        for k, (fx, fy) in enumerate(_CHIP_FLIPS):
            px, py = _flip(x, fx), _flip(y, fy)
            for a in range(n):
                sends.append(_rcopy(p_refs[a].at[2 * px + py], l_refs[a].at[j0], send_sems, recv_sems, 3 * a + k, (px, py, c)))
        for k, (fx, fy, fc) in enumerate(_DEV_FLIPS):
            peer = (_flip(x, fx), _flip(y, fy), _flip(c, fc))
            sends.append(_rcopy(s_ref, ls_ref.at[d0], send_sems, recv_sems, 3 * n + k, peer))
        for cp in sends:
            cp.start()
        for k, (fx, fy) in enumerate(_CHIP_FLIPS):
            px, py = _flip(x, fx), _flip(y, fy)
            for a in range(n):
                _rcopy(p_refs[a].at[j0], l_refs[a].at[2 * px + py], send_sems, recv_sems, 3 * a + k, (px, py, c)).wait_recv()
        for k, (fx, fy, fc) in enumerate(_DEV_FLIPS):
            px, py, pc = _flip(x, fx), _flip(y, fy), _flip(c, fc)
            _rcopy(s_ref, ls_ref.at[4 * px + 2 * py + pc], send_sems, recv_sems, 3 * n + k, (px, py, pc)).wait_recv()
        for cp in sends:
            cp.wait_send()
        for cp in local:
            cp.wait()

    n_sem = 3 * n + 7
    return pl.pallas_call(
        body, name="chip_exchange", in_specs=[_HBM] * (n + 1), out_specs=[_HBM] * (n + 1),
        out_shape=[jax.ShapeDtypeStruct(p.shape, F32) for p in ps] + [jax.ShapeDtypeStruct((8, rs, LANES), F32)],
        scratch_shapes=[pltpu.SemaphoreType.DMA((n_sem,)), pltpu.SemaphoreType.DMA((n_sem,)),
                        pltpu.SemaphoreType.DMA((n + 1,))],
    )(*ps, small)


def _half_blocks(rows, cols, by_cols):
    if by_cols:
        tc = _tile(cols // 2, 256)
        nb = cols // 2 // tc
        return rows, tc, nb, (lambda i, c: (0, c * nb + i))
    tr = _tile(rows // 2, 256)
    nb = rows // 2 // tr
    return tr, cols, nb, (lambda i, c: (c * nb + i, 0))


def _core_sum(g, part, core, *, name, by_cols):
    _, rows, cols = g.shape
    br, bc, nb, whole = _half_blocks(rows, cols, by_cols)
    mine = (lambda i: (0, i)) if by_cols else (lambda i: (i, 0))

    def body(c_ref, g_ref, p_ref, o_ref):
        o_ref[...] = g_ref[...] + p_ref[...]

    grid_spec = pltpu.PrefetchScalarGridSpec(
        num_scalar_prefetch=1, grid=(4, nb),
        in_specs=[pl.BlockSpec((1, br, bc), lambda j, i, c: (j,) + whole(i, c[0])),
                  pl.BlockSpec((1, br, bc), lambda j, i, c: (j,) + mine(i))],
        out_specs=pl.BlockSpec((1, br, bc), lambda j, i, c: (j,) + mine(i)),
    )
    return pl.pallas_call(
        body, name=name, grid_spec=grid_spec, out_shape=jax.ShapeDtypeStruct(part.shape, F32),
        compiler_params=_params(("parallel", "parallel")),
    )(core, g, part)


def _chip_sum(landed, part, place, *, name, by_cols):
    _, hr, hc = landed.shape
    rows, cols = (hr, 2 * hc) if by_cols else (2 * hr, hc)
    br, bc, nb, whole = _half_blocks(rows, cols, by_cols)
    mine = (lambda i: (0, i)) if by_cols else (lambda i: (i, 0))

    def body(c_ref, own_ref, a_ref, b_ref, d_ref, o_ref):
        o_ref[...] = ((own_ref[0] + a_ref[0]) + b_ref[0]) + d_ref[0]

    slot = lambda k: pl.BlockSpec((1, br, bc), lambda i, c: (jnp.bitwise_xor(c[1], k),) + mine(i))
    grid_spec = pltpu.PrefetchScalarGridSpec(
        num_scalar_prefetch=1, grid=(nb,),
        in_specs=[slot(0), slot(1), slot(2), slot(3)],
        out_specs=pl.BlockSpec((br, bc), lambda i, c: whole(i, c[0])),
    )
    return pl.pallas_call(
        body, name=name, grid_spec=grid_spec, out_shape=jax.ShapeDtypeStruct((rows, cols), F32),
        compiler_params=_params(("parallel",)),
    )(place, part, landed, landed, landed)


_ROW_POOL_W, _ROW_NORM_AB, _ROW_FINAL, _ROW_POOL_SCALE, _ROW_Q_NORM = 0, 512, 520, 528, 532
_ROW_KV_NORM, _ROW_O_NORM, _ROW_A_LOG, _ROW_DT_BIAS, _ROW_LOSS = 534, 535, 536, 537, 538
_ROW_CONV, _ROW_NORM_C, _SMALL_ROWS = 544, 640, 672
_CONV_ROWS = CONV_WIDTH * 6


def _put_rows(dst_ref, row0, src, width):
    for r in range(width // LANES):
        dst_ref[row0 + r:row0 + r + 1, :] = src[:, r * LANES:(r + 1) * LANES]


def _pack_small(g, loss_tile):
    names = ("wpool", "norm_ab", "final_norm", "pool_scale", "q_a_norm", "kv_a_norm", "o_norm", "a_log", "dt_bias",
             "conv_w", "norm_c")

    def body(wpool, norm_ab, final_norm, pool_scale, q_norm, kv_norm, o_norm, a_log, dt_bias, conv_w, norm_c, loss, o_ref):
        o_ref[...] = jnp.zeros_like(o_ref)
        for gi in range(4):
            o_ref[_ROW_POOL_W + gi * 128:_ROW_POOL_W + (gi + 1) * 128, :] = wpool[gi * 128:(gi + 1) * 128, gi * 128:(gi + 1) * 128]
        _put_rows(o_ref, _ROW_NORM_AB, norm_ab[...], 1024)
        _put_rows(o_ref, _ROW_FINAL, final_norm[...], 1024)
        _put_rows(o_ref, _ROW_POOL_SCALE, pool_scale[...], 512)
        _put_rows(o_ref, _ROW_Q_NORM, q_norm[...], 256)
        for row, ref in ((_ROW_KV_NORM, kv_norm), (_ROW_O_NORM, o_norm), (_ROW_A_LOG, a_log), (_ROW_DT_BIAS, dt_bias)):
            o_ref[row:row + 1, :] = ref[...]
        o_ref[_ROW_LOSS:_ROW_LOSS + 1, :] = loss[0:1, :]
        for j in range(4):
            for r in range(CONV_WIDTH):
                _put_rows(o_ref, _ROW_CONV + j * _CONV_ROWS + r * 6, conv_w[r:r + 1, j * 768:(j + 1) * 768], 768)
            _put_rows(o_ref, _ROW_NORM_C + j * 8, norm_c[:, j * 256:(j + 1) * 256], 256)

    vmem = pl.BlockSpec(memory_space=pltpu.VMEM)
    return pl.pallas_call(
        body, name="pack_small", in_specs=[vmem] * 12, out_specs=vmem,
        out_shape=jax.ShapeDtypeStruct((_SMALL_ROWS, LANES), F32),
    )(*[g[n] for n in names], loss_tile)


_SMALL_NAMES = ("pool_w", "norm_ab", "final_norm", "pool_scale", "q_a_norm", "kv_a_norm", "o_norm", "a_log", "dt_bias",
                "conv_w", "norm_c")


def _take_rows(src, row0, width):
    return jnp.concatenate([src[row0 + r:row0 + r + 1, :] for r in range(width // LANES)], axis=1)


def _small_update(small_all, late_all, ws, ms, vs):
    n = len(_SMALL_NAMES)

    def body(*refs):
        a_ref, late_ref = refs[0], refs[1]
        refs = refs[1:]
        w_refs, m_refs, v_refs = refs[1:1 + n], refs[1 + n:1 + 2 * n], refs[1 + 2 * n:1 + 3 * n]
        outs = refs[1 + 3 * n:1 + 7 * n]
        loss_ref, tot = refs[1 + 7 * n], refs[2 + 7 * n]
        acc, late = a_ref[0], late_ref[0]
        for d in range(1, 8):
            acc = acc + a_ref[d]
            late = late + late_ref[d]
        tot[...] = acc
        x, y, _ = _place()
        j0 = 2 * x + y
        conv = tot[pl.ds(pl.multiple_of(_ROW_CONV + j0 * _CONV_ROWS, 8), _CONV_ROWS), :]
        norm_c = tot[pl.ds(pl.multiple_of(_ROW_NORM_C + j0 * 8, 8), 8), :]
        whole = tot[_ROW_NORM_AB:_ROW_CONV, :]
        at = lambda row: row - _ROW_NORM_AB
        grads = {
            "norm_ab": _take_rows(late, 0, 1024), "final_norm": _take_rows(whole, at(_ROW_FINAL), 1024),
            "pool_scale": _take_rows(whole, at(_ROW_POOL_SCALE), 512), "q_a_norm": _take_rows(whole, at(_ROW_Q_NORM), 256),
            "kv_a_norm": whole[at(_ROW_KV_NORM):at(_ROW_KV_NORM) + 1, :], "o_norm": whole[at(_ROW_O_NORM):at(_ROW_O_NORM) + 1, :],
            "a_log": tot[_ROW_A_LOG:_ROW_A_LOG + 1, 0:GDN_HEADS],
            "dt_bias": tot[_ROW_DT_BIAS:_ROW_DT_BIAS + 1, 0:GDN_HEADS],
            "norm_c": _take_rows(norm_c, 0, 256),
        }
        loss_ref[...] = whole[at(_ROW_LOSS):at(_ROW_LOSS) + 1, :]
        for i, name in enumerate(_SMALL_NAMES):
            g_out = outs[4 * i]
            if name == "pool_w":
                for gi in range(4):
                    g_out[gi] = tot[_ROW_POOL_W + gi * 128:_ROW_POOL_W + (gi + 1) * 128, :]
            elif name == "conv_w":
                for r in range(CONV_WIDTH):
                    g_out[r:r + 1, :] = _take_rows(conv, r * 6, 768)
            else:
                g_out[...] = grads[name]
            _adam_update(g_out, w_refs[i], m_refs[i], v_refs[i], *outs[4 * i + 1:4 * i + 4])

    vmem = pl.BlockSpec(memory_space=pltpu.VMEM)
    out_shape = [jax.ShapeDtypeStruct(w.shape, F32) for w in ws for _ in range(4)] + [jax.ShapeDtypeStruct((1, LANES), F32)]
    return pl.pallas_call(
        body, name="small_update", in_specs=[vmem] * (2 + 3 * n), out_specs=[vmem] * (4 * n + 1), out_shape=out_shape,
        scratch_shapes=[pltpu.VMEM((_SMALL_ROWS, LANES), F32)],
        compiler_params=pltpu.CompilerParams(vmem_limit_bytes=VMEM_LIMIT),
    )(small_all, late_all, *ws, *ms, *vs)


def _adam_update(g_ref, w_ref, m_ref, v_ref, d_ref, mo_ref, vo_ref):
    gv = g_ref[...]
    mn = ADAM_B1 * m_ref[...] + (1.0 - ADAM_B1) * gv
    vn = ADAM_B2 * v_ref[...] + (1.0 - ADAM_B2) * (gv * gv)
    mo_ref[...] = mn
    vo_ref[...] = vn
    c1 = 1.0 - ADAM_B1 ** ADAM_STEP
    c2 = 1.0 - ADAM_B2 ** ADAM_STEP
    d_ref[...] = -ADAM_LR * ((mn / c1) / (jnp.sqrt(vn / c2) + ADAM_EPS) + ADAM_WD * w_ref[...])


def _adamw_rows(g, w, m, v, *, name):
    rows, cols = g.shape
    if rows % LANES == 0:
        tr = _tile(rows, 512)
        blk, steps = pl.BlockSpec((tr, cols), lambda i: (i, 0)), rows // tr
    else:
        tc = _tile(cols, 256)
        blk, steps = pl.BlockSpec((rows, tc), lambda i: (0, i)), cols // tc

    def body(*refs):
        _adam_update(*refs)

    out = jax.ShapeDtypeStruct((rows, cols), F32)
    return pl.pallas_call(
        body, name=name, grid=(steps,), in_specs=[blk] * 4, out_specs=[blk] * 3, out_shape=[out] * 3,
        compiler_params=_params(("parallel",)),
    )(g, w, m, v)


_ADAM_ROWWISE = ("w_in_ab", "w_q_b", "w_kv_b", "w_out_ab", "w_in_c", "w_out_c")


_SHARD_AXIS = {"w_in_ab": 1, "w_q_b": 1, "w_kv_b": 1, "w_out_ab": 0, "w_in_c": 1, "w_out_c": 0, "conv_w": 1, "norm_c": 1}
_ALL_NAMES = ("norm_ab", "w_in_ab", "q_a_norm", "w_q_b", "kv_a_norm", "w_kv_b", "pool_w", "pool_scale", "w_out_ab",
              "norm_c", "w_in_c", "conv_w", "a_log", "dt_bias", "o_norm", "w_out_c", "final_norm")


def _join_shards(a, axis):
    _, r, c = a.shape
    return a.reshape(4 * r, c) if axis == 0 else jnp.transpose(a, (1, 0, 2)).reshape(r, 4 * c)


def _split_shards(a, axis):
    r, c = a.shape
    return a.reshape(4, r // 4, c) if axis == 0 else jnp.transpose(a.reshape(r, 4, c // 4), (1, 0, 2))


def kernel(x, positions, norm_ab, w_in_ab, q_a_norm, w_q_b, kv_a_norm, w_kv_b, pool_w, pool_scale, w_out_ab, norm_c, w_in_c, conv_w, a_log, dt_bias, o_norm, w_out_c, final_norm, loss_target, m_norm_ab, m_w_in_ab, m_q_a_norm, m_w_q_b, m_kv_a_norm, m_w_kv_b, m_pool_w, m_pool_scale, m_w_out_ab, m_norm_c, m_w_in_c, m_conv_w, m_a_log, m_dt_bias, m_o_norm, m_w_out_c, m_final_norm, v_norm_ab, v_w_in_ab, v_q_a_norm, v_w_q_b, v_kv_a_norm, v_w_kv_b, v_pool_w, v_pool_scale, v_w_out_ab, v_norm_c, v_w_in_c, v_conv_w, v_a_log, v_dt_bias, v_o_norm, v_w_out_c, v_final_norm):
    given = dict(locals())
    c = lax.axis_index("c")
    t = x.shape[1]

    def shard_of(prefix, name):
        a = given[prefix + name]
        return a.reshape(a.shape[1:]) if a.ndim > 2 else a.reshape(1, -1)

    big, big_even, big_odd, small_sharded = _ADAM_ROWWISE, _ADAM_ROWWISE[:4], _ADAM_ROWWISE[4:], ("conv_w", "norm_c")
    chip = 2 * lax.axis_index("x") + lax.axis_index("y")
    core = c.astype(jnp.int32).reshape(1)
    place = jnp.stack([c, chip]).astype(jnp.int32)
    later = {"mid": big_even[1:], "odd": big_odd + small_sharded}
    travelling = {}

    def send(tag, after=None):
        shards = [shard_of("", n).astype(BF16) if n in big else shard_of("", n) for n in later[tag]]
        started = _to_chips_start(shards, per_chip_slot=False, name="gather_" + tag + "_start", after=after)
        travelling[tag] = (shards, started)
        return started[-1][0, 0]

    mid_sent = send("mid")
    gathered = _gather_weights([shard_of("", "w_in_ab").astype(BF16)], [])
    full = {"w_in_ab": _join_shards(gathered[0], _SHARD_AXIS["w_in_ab"])}
    for name in ("norm_ab", "q_a_norm", "kv_a_norm", "pool_w", "pool_scale"):
        full[name] = shard_of("", name)
    lw = _layout_in_ab(full)
    lw["norm_ab"] = lw["norm_ab"] + mid_sent

    def more_weights(tag, after):
        shards, started = travelling[tag]
        landed = _to_chips_wait(started, after, per_chip_slot=False, name="gather_" + tag + "_wait")
        w = {}
        for name, land, own in zip(later[tag], landed, shards):
            w[name] = _join_shards(lax.dynamic_update_index_in_dim(land, own, chip, 0), _SHARD_AXIS[name])
        if tag == "mid":
            out = _layout_mid(w)
            out["wq"] = out["wq"] + send("odd", after=landed[0]).astype(BF16)
            return out
        for name in ("a_log", "dt_bias", "o_norm", "final_norm"):
            w[name] = shard_of("", name)
        return _layout_odd(w)

    transposed = ("w_in_ab", "w_in_c")

    def chip_slots(names, g):
        grads = _unlayout_grads(g, names)
        return ([_split_shards(grads[n], 0 if n in transposed else _SHARD_AXIS[n]) for n in names],
                [n in transposed for n in names])

    def chip_partials(names, slots, partial, by_cols):
        return [_core_sum(s, p, core, name="core_sum_" + n, by_cols=b) for n, s, p, b in zip(names, slots, partial, by_cols)]

    groups = {"odd": big_odd, "out_ab": ("w_out_ab",), "in_ab": ("w_in_ab", "w_q_b", "w_kv_b")}
    sent, swapping = {}, {}

    def on_grads(tag, g, after):
        if tag == "odd":
            slots, by_cols = chip_slots(groups[tag], g)
            swapping[tag] = (slots, by_cols, _core_swap_partial_start(slots, by_cols, name="core_swap_partial_odd_start"))
            return swapping[tag][2][-1][0, 0]
        if tag == "odd_go":
            tag = "odd"
            _, by_cols, started = swapping[tag]
            slots, partial = _core_swap_partial_wait(started, after, by_cols, name="core_swap_partial_odd_wait")
        else:
            slots, by_cols = chip_slots(groups[tag], g)
            partial = _core_swap_partial(slots, by_cols, name="core_swap_partial_" + tag)
        part = chip_partials(groups[tag], slots, partial, by_cols)
        sent[tag] = (part, _to_chips_start(part, per_chip_slot=True, name="exchange_" + tag + "_start"))
        token = sent[tag][1][-1][0, 0]
        if tag == "in_ab":
            pack = _pack_small({**g, "norm_ab": jnp.zeros((1, 1024), F32)}, g["loss"])
            sent["small"] = (pack, _to_chips_start([pack], per_chip_slot=False, to_all=True, name="exchange_small_start"))
            token = token + sent["small"][1][-1][0, 0]
        return token

    loss_tile, dx, g = _local_step(x[0], positions.reshape(t, 1), loss_target[0], lw, more_weights, on_grads)
    late_all = _chip_exchange([], g["norm_ab"].reshape(8, LANES))[-1]
    pack, started = sent.pop("small")
    landed = _to_chips_wait(started, late_all, per_chip_slot=False, to_all=True, name="exchange_small_wait")[0]
    small_all = lax.dynamic_update_index_in_dim(landed, pack, 2 * chip + c, 0)
    halves = {}
    for tag, names in groups.items():
        part, started = sent[tag]
        landed = _to_chips_wait(started, late_all, per_chip_slot=True, name="exchange_" + tag + "_wait")
        for n, l, p in zip(names, landed, part):
            halves[n] = _chip_sum(l, p, place, name="chip_sum_" + n, by_cols=n in transposed)
    gbig = dict(zip(big, _core_swap_sum([halves[n] for n in big], [n in transposed for n in big])))

    res = {}
    for name in big:
        operands = [gbig[name], shard_of("", name), shard_of("m_", name), shard_of("v_", name)]
        flip = name in transposed
        if flip:
            operands[1:] = [jnp.transpose(a) for a in operands[1:]]
        out = (operands[0],) + tuple(_adamw_rows(*operands, name="adamw_" + name))
        out = [jnp.transpose(a) for a in out] if flip else out
        res["grad", name], res["delta", name], res["m", name], res["v", name] = out
    out = _small_update(small_all, late_all, [shard_of("", n) for n in _SMALL_NAMES], [shard_of("m_", n) for n in _SMALL_NAMES],
                        [shard_of("v_", n) for n in _SMALL_NAMES])
    for i, name in enumerate(_SMALL_NAMES):
        res["grad", name], res["delta", name], res["m", name], res["v", name] = out[4 * i:4 * i + 4]
    res = {k: a.reshape(given[k[1]].shape) for k, a in res.items()}
    loss = out[-1][0, 0]
    outs = [loss, dx.reshape(x.shape)]
    for key in ("grad", "delta", "m", "v"):
        outs += [res[key, n] for n in _ALL_NAMES]
    return tuple(outs)
```

```python
import functools

import jax
import jax.numpy as jnp
from jax import lax
from jax.experimental import pallas as pl
from jax.experimental.pallas import tpu as pltpu

F32 = jnp.float32
BF16 = jnp.bfloat16
HI = lax.Precision.HIGHEST
MESH = pl.DeviceIdType.MESH

RMS_EPS = 1e-6
MLA_HEADS = 8
MLA_Q_RANK = 256
MLA_KV_RANK = 128
MLA_NOPE = 64
MLA_ROPE = 32
MLA_V = 64
ROPE_THETA = 10000.0
POOL_WINDOWS = (2, 4, 8, 16)
POOL_GROUP = 128
POOL_WIDTH = 512
POOL_HALO = 16
GDN_HEADS = 8
GDN_DK = 128
CONV_WIDTH = 4
CONV_HALO = 8
CHUNK = 64
IN_AB_PAD = 2048
IN_C_PAD = 4224
ATT_SCALE = (MLA_NOPE + MLA_ROPE) ** -0.5
LOG2E = 1.4426950408889634

ADAM_LR = 0.001
ADAM_B1 = 0.9
ADAM_B2 = 0.999
ADAM_EPS = 1e-08
ADAM_WD = 0.01
ADAM_STEP = 10

LANES = 128
VMEM_LIMIT = 56 * 1024 * 1024

ROW_TILE = 256
ATT_TILE = 1024
GDN_TILE = 256
MM_TILE = (1408, 1408, 2048)

NN = (((1,), (0,)), ((), ()))
NT = (((1,), (1,)), ((), ()))
TN = (((0,), (0,)), ((), ()))


def _dot(a, b, dims=NN, prec=None):
    return lax.dot_general(a, b, dims, precision=prec, preferred_element_type=F32)


def _tile(n, pref):
    if n <= pref:
        return n
    step = LANES if pref >= LANES else 8
    for t in range(pref - pref % step, 0, -step):
        if n % t == 0:
            return t
    return n


def _params(sem):
    return pltpu.CompilerParams(dimension_semantics=sem, vmem_limit_bytes=VMEM_LIMIT)


def _sigmoid(x):
    return 0.5 * jnp.tanh(0.5 * x) + 0.5


def _softplus(x):
    return jnp.maximum(x, 0.0) + jnp.log(1.0 + jnp.exp(-jnp.abs(x)))


def _matmul(a, b, mode, *, name):
    if mode == "nn":
        (m, k), (k2, n) = a.shape, b.shape
    elif mode == "nt":
        (m, k), (n, k2) = a.shape, b.shape
    else:
        (k, m), (k2, n) = a.shape, b.shape
    assert k == k2, (a.shape, b.shape, mode)
    tm, tn, tk = _tile(m, MM_TILE[0]), _tile(n, MM_TILE[1]), _tile(k, MM_TILE[2])
    nk = k // tk
    if mode == "tn":
        a_spec = pl.BlockSpec((tk, tm), lambda i, j, kk: (kk, i))
    else:
        a_spec = pl.BlockSpec((tm, tk), lambda i, j, kk: (i, kk))
    if mode == "nt":
        b_spec = pl.BlockSpec((tn, tk), lambda i, j, kk: (j, kk))
    else:
        b_spec = pl.BlockSpec((tk, tn), lambda i, j, kk: (kk, j))
    o_spec = pl.BlockSpec((tm, tn), lambda i, j, kk: (i, j))
    dims = {"nn": NN, "nt": NT, "tn": TN}[mode]

    def body(a_ref, b_ref, o_ref, *scratch):
        if nk == 1:
            o_ref[...] = _dot(a_ref[...], b_ref[...], dims)
            return
        acc = scratch[0]
        kk = pl.program_id(2)

        @pl.when(kk == 0)
        def _():
            acc[...] = jnp.zeros_like(acc)

        acc[...] += _dot(a_ref[...], b_ref[...], dims)

        @pl.when(kk == nk - 1)
        def _():
            o_ref[...] = acc[...]

    return pl.pallas_call(
        body, name=name, grid=(m // tm, n // tn, nk), in_specs=[a_spec, b_spec], out_specs=o_spec,
        out_shape=jax.ShapeDtypeStruct((m, n), F32),
        scratch_shapes=[pltpu.VMEM((tm, tn), F32)] if nk > 1 else [],
        compiler_params=_params(("parallel", "parallel", "arbitrary")),
    )(a, b)


def _rms_in_proj(h, g, w, *, name):
    t, d = h.shape
    n = w.shape[1]
    tm, tn = _tile(t, MM_TILE[0]), _tile(n, MM_TILE[1])

    def body(h_ref, g_ref, w_ref, o_ref, hn_ref):
        @pl.when(pl.program_id(1) == 0)
        def _():
            x = h_ref[...]
            r = lax.rsqrt(jnp.mean(x * x, axis=-1, keepdims=True) + RMS_EPS)
            hn_ref[...] = (x * r * g_ref[...]).astype(BF16)

        o_ref[...] = _dot(hn_ref[...], w_ref[...])

    return pl.pallas_call(
        body, name=name, grid=(t // tm, n // tn),
        in_specs=[pl.BlockSpec((tm, d), lambda i, j: (i, 0)), pl.BlockSpec((1, d), lambda i, j: (0, 0)),
                  pl.BlockSpec((d, tn), lambda i, j: (0, j))],
        out_specs=[pl.BlockSpec((tm, tn), lambda i, j: (i, j)), pl.BlockSpec((tm, d), lambda i, j: (i, 0))],
        out_shape=[jax.ShapeDtypeStruct((t, n), F32), jax.ShapeDtypeStruct((t, d), BF16)],
        compiler_params=_params(("parallel", "arbitrary")),
    )(h, g, w)


def _matmul_rms_bwd(dproj, w, h, g, dres, *, name, prev_y=None):
    t, k = dproj.shape
    d = w.shape[0]
    tm = _tile(t, 2 * ROW_TILE)
    chained = prev_y is not None

    def body(dp_ref, w_ref, h_ref, g_ref, dres_ref, *rest):
        i = pl.program_id(0)
        dh_ref, dg_ref = rest[-4:-2] if chained else rest
        dyv = _dot(dp_ref[...], w_ref[...], NT)
        x = h_ref[...]
        r = lax.rsqrt(jnp.mean(x * x, axis=-1, keepdims=True) + RMS_EPS)
        xh = x * r
        dxh = dyv * g_ref[...]
        dh = dres_ref[...] + r * (dxh - xh * jnp.mean(dxh * xh, axis=-1, keepdims=True))
        dh_ref[...] = dh

        @pl.when(i == 0)
        def _():
            dg_ref[...] = jnp.zeros_like(dg_ref)
            if chained:
                rest[-1][...] = jnp.zeros_like(rest[-1])

        dg_ref[...] += jnp.sum(dyv * xh, axis=0, keepdims=True)
        if chained:
            y_ref, dhb_ref, dw_ref = rest[0], rest[-2], rest[-1]
            dhb_ref[...] = dh.astype(BF16)
            dw_ref[...] += _dot(y_ref[...], dhb_ref[...], TN)

    row = pl.BlockSpec((tm, d), lambda i: (i, 0))
    vec = pl.BlockSpec((1, d), lambda i: (0, 0))
    in_specs = [pl.BlockSpec((tm, k), lambda i: (i, 0)), pl.BlockSpec((d, k), lambda i: (0, 0)), row, vec, row]
    out_specs, out_shape = [row, vec], [jax.ShapeDtypeStruct((t, d), F32), jax.ShapeDtypeStruct((1, d), F32)]
    args = [dproj, w, h, g, dres]
    if chained:
        in_specs.append(row)
        args.append(prev_y)
        out_specs += [row, pl.BlockSpec((d, d), lambda i: (0, 0))]
        out_shape += [jax.ShapeDtypeStruct((t, d), BF16), jax.ShapeDtypeStruct((d, d), F32)]
    return pl.pallas_call(
        body, name=name, grid=(t // tm,), in_specs=in_specs, out_specs=out_specs, out_shape=out_shape,
        compiler_params=_params(("arbitrary",)),
    )(*args)


def _rope_partner(x):
    lane = lax.broadcasted_iota(jnp.int32, x.shape, 1)
    swapped = jnp.where(lane < MLA_NOPE + MLA_ROPE // 2, pltpu.roll(x, LANES - 16, 1), pltpu.roll(x, 16, 1))
    return jnp.where((lane >= MLA_NOPE) & (lane < MLA_NOPE + MLA_ROPE), swapped, 0.0)


def _pool_counts(row0, tm, w):
    t_idx = row0 + lax.broadcasted_iota(jnp.int32, (tm, POOL_GROUP), 0)
    return jnp.minimum(t_idx + 1, w).astype(F32)


def _ab_prep(proj, pos, inv_freq, q_a_norm, kv_a_norm, wq, wk, wv, wpool, *, name):
    t = proj.shape[0]
    tm = _tile(t, ROW_TILE)
    hb = tm // POOL_HALO

    def body(p_ref, halo_ref, pos_ref, inv_ref, qg_ref, kg_ref, wq_ref, wk_ref, wv_ref, wp_ref,
             q_ref, k_ref, v_ref, yb_ref, qn_ref, kvn_ref, d_ref, cos_ref, sin_ref, ext):
        i = pl.program_id(0)
        ql = p_ref[:, 0:MLA_Q_RANK]
        r = lax.rsqrt(jnp.mean(ql * ql, axis=-1, keepdims=True) + RMS_EPS)
        qn = (ql * r * qg_ref[...]).astype(BF16)
        qn_ref[...] = qn
        kl = p_ref[:, MLA_Q_RANK:MLA_Q_RANK + MLA_KV_RANK]
        r = lax.rsqrt(jnp.mean(kl * kl, axis=-1, keepdims=True) + RMS_EPS)
        kvn = (kl * r * kg_ref[...]).astype(BF16)
        kvn_ref[...] = kvn
        ang = pos_ref[...].astype(F32) * inv_ref[...]
        lane = lax.broadcasted_iota(jnp.int32, (tm, LANES), 1)
        in_rope = (lane >= MLA_NOPE) & (lane < MLA_NOPE + MLA_ROPE)
        cos_t = jnp.where(in_rope, jnp.cos(ang), 1.0)
        sin_t = jnp.where(in_rope, jnp.sin(ang), 0.0)
        sin_t = jnp.where(lane < MLA_NOPE + MLA_ROPE // 2, -sin_t, sin_t)
        cos_ref[...] = cos_t
        sin_ref[...] = sin_t
        kr = p_ref[:, 384:512]
        kr = kr * cos_t + _rope_partner(kr) * sin_t
        qraw = _dot(qn, wq_ref[...])
        kvk = _dot(kvn, wk_ref[...])
        for h in range(MLA_HEADS):
            sl = slice(h * LANES, (h + 1) * LANES)
            qh = qraw[:, sl]
            q_ref[:, sl] = ((qh * cos_t + _rope_partner(qh) * sin_t) * (ATT_SCALE * LOG2E)).astype(BF16)
            k_ref[:, sl] = (kvk[:, sl] + kr).astype(BF16)
        v_ref[...] = _dot(kvn, wv_ref[...]).astype(BF16)
        xp = p_ref[:, 512:1024]
        ext[0:POOL_HALO, :] = jnp.where(i > 0, halo_ref[...], 0.0)
        ext[POOL_HALO:POOL_HALO + tm, :] = xp
        for g, w in enumerate(POOL_WINDOWS):
            lo = g * POOL_GROUP
            acc = ext[POOL_HALO:POOL_HALO + tm, lo:lo + POOL_GROUP]
            for s in range(1, w):
                acc = acc + ext[POOL_HALO - s:POOL_HALO - s + tm, lo:lo + POOL_GROUP]
            cnt = _pool_counts(i * tm, tm, w)
            d_ref[:, lo:lo + POOL_GROUP] = (acc / cnt - xp[:, lo:lo + POOL_GROUP]).astype(BF16)
        yb_ref[...] = _dot(d_ref[...], wp_ref[...])

    row = lambda w: pl.BlockSpec((tm, w), lambda i: (i, 0))
    vec = lambda w: pl.BlockSpec((1, w), lambda i: (0, 0))
    whole = lambda a: pl.BlockSpec(a.shape, lambda i: (0, 0))
    return pl.pallas_call(
        body, name=name, grid=(t // tm,),
        in_specs=[row(1024), pl.BlockSpec((POOL_HALO, POOL_WIDTH), lambda i: (jnp.maximum(i * hb - 1, 0), 1)),
                  pl.BlockSpec((tm, 1), lambda i: (i, 0)), vec(LANES), vec(MLA_Q_RANK), vec(MLA_KV_RANK),
                  whole(wq), whole(wk), whole(wv), whole(wpool)],
        out_specs=[row(1024), row(1024), row(512), row(512), row(MLA_Q_RANK), row(MLA_KV_RANK), row(POOL_WIDTH),
                   row(LANES), row(LANES)],
        out_shape=[jax.ShapeDtypeStruct((t, 1024), BF16), jax.ShapeDtypeStruct((t, 1024), BF16),
                   jax.ShapeDtypeStruct((t, 512), BF16), jax.ShapeDtypeStruct((t, 512), F32),
                   jax.ShapeDtypeStruct((t, MLA_Q_RANK), BF16), jax.ShapeDtypeStruct((t, MLA_KV_RANK), BF16),
                   jax.ShapeDtypeStruct((t, POOL_WIDTH), BF16), jax.ShapeDtypeStruct((t, LANES), F32),
                   jax.ShapeDtypeStruct((t, LANES), F32)],
        scratch_shapes=[pltpu.VMEM((tm + POOL_HALO, POOL_WIDTH), F32)],
        compiler_params=_params(("parallel",)),
    )(proj, proj, pos, inv_freq, q_a_norm, kv_a_norm, wq, wk, wv, wpool)


def _ab_prep_bwd(proj, q_a_norm, kv_a_norm, dq, dk, dv, cos_t, sin_t, dyb, dz, qn, kvn, d, hn, wq, wk, wv, wpool, *, name):
    t = proj.shape[0]
    tm = _tile(t, ROW_TILE)
    hb = tm // POOL_HALO
    last_halo = t // POOL_HALO - 1
    nt = t // tm

    def body(p_ref, qg_ref, kg_ref, dq_ref, dk_ref, dv_ref, c_ref, s_ref, dyb_ref, dybn_ref, dz_ref,
             qn_ref, kvn_ref, d_ref, hn_ref, wq_ref, wk_ref, wv_ref, wp_ref,
             dp_ref, dqg_ref, dkg_ref, dwq_ref, dwk_ref, dwv_ref, dwp_ref, dwin_ref, ext, dqr_ref, dkb_ref):
        i = pl.program_id(0)

        @pl.when(i == 0)
        def _():
            for ref in (dqg_ref, dkg_ref, dwq_ref, dwk_ref, dwv_ref, dwp_ref, dwin_ref):
                ref[...] = jnp.zeros_like(ref)

        def norm_bwd(x, g, dy, dg_ref):
            r = lax.rsqrt(jnp.mean(x * x, axis=-1, keepdims=True) + RMS_EPS)
            xh = x * r
            dxh = dy * g
            dg_ref[...] += jnp.sum(dy * xh, axis=0, keepdims=True)
            return r * (dxh - xh * jnp.mean(dxh * xh, axis=-1, keepdims=True))

        c, s = c_ref[...], s_ref[...]
        lane = lax.broadcasted_iota(jnp.int32, (tm, LANES), 1)
        in_rope = (lane >= MLA_NOPE) & (lane < MLA_NOPE + MLA_ROPE)
        dkr = jnp.zeros((tm, LANES), F32)
        for h in range(MLA_HEADS):
            sl = slice(h * LANES, (h + 1) * LANES)
            g = dq_ref[:, sl]
            dqr_ref[:, sl] = ((g * c + _rope_partner(g * s)) * ATT_SCALE).astype(BF16)
            gk = dk_ref[:, sl]
            dkb_ref[:, sl] = gk.astype(BF16)
            dkr = dkr + jnp.where(in_rope, gk, 0.0)
        dkr = dkr * c + _rope_partner(dkr * s)
        dqn = _dot(dqr_ref[...], wq_ref[...], NT)
        dkvn = _dot(dkb_ref[...], wk_ref[...], NT) + _dot(dv_ref[...], wv_ref[...], NT)
        dql = norm_bwd(p_ref[:, 0:MLA_Q_RANK], qg_ref[...], dqn, dqg_ref)
        dp_ref[:, 0:MLA_Q_RANK] = dql.astype(BF16)
        dkl = norm_bwd(p_ref[:, MLA_Q_RANK:384], kg_ref[...], dkvn, dkg_ref)
        dp_ref[:, MLA_Q_RANK:384] = dkl.astype(BF16)
        dp_ref[:, 384:512] = dkr.astype(BF16)
        ddv = _dot(dyb_ref[...], wp_ref[...], NT)
        ddn = _dot(dybn_ref[...], wp_ref[...], NT)
        for g, w in enumerate(POOL_WINDOWS):
            lo = g * POOL_GROUP
            ext[0:tm, lo:lo + POOL_GROUP] = ddv[:, lo:lo + POOL_GROUP] / _pool_counts(i * tm, tm, w)
            nxt = ddn[:, lo:lo + POOL_GROUP] / _pool_counts((i + 1) * tm, POOL_HALO, w)
            ext[tm:tm + POOL_HALO, lo:lo + POOL_GROUP] = jnp.where(i < nt - 1, nxt, 0.0)
        for g, w in enumerate(POOL_WINDOWS):
            lo = g * POOL_GROUP
            acc = ext[0:tm, lo:lo + POOL_GROUP]
            for s in range(1, w):
                acc = acc + ext[s:s + tm, lo:lo + POOL_GROUP]
            dp_ref[:, 512 + lo:512 + lo + POOL_GROUP] = (acc - ddv[:, lo:lo + POOL_GROUP]).astype(BF16)
        dp_ref[:, 1024:2048] = dz_ref[...]
        dwq_ref[...] += _dot(qn_ref[...], dqr_ref[...], TN)
        dwk_ref[...] += _dot(kvn_ref[...], dkb_ref[...], TN)
        dwv_ref[...] += _dot(kvn_ref[...], dv_ref[...], TN)
        dwp_ref[...] += _dot(d_ref[...], dyb_ref[...], TN)
        dwin_ref[...] += _dot(dp_ref[...], hn_ref[...], TN)

    row = lambda w: pl.BlockSpec((tm, w), lambda i: (i, 0))
    vec = lambda w: pl.BlockSpec((1, w), lambda i: (0, 0))
    whole = lambda a: pl.BlockSpec(a.shape, lambda i: (0, 0))
    weights = (wq, wk, wv, wpool)
    return pl.pallas_call(
        body, name=name, grid=(nt,),
        in_specs=[row(1024), vec(MLA_Q_RANK), vec(MLA_KV_RANK), row(1024), row(1024), row(512), row(LANES), row(LANES),
                  row(POOL_WIDTH),
                  pl.BlockSpec((POOL_HALO, POOL_WIDTH), lambda i: (jnp.minimum((i + 1) * hb, last_halo), 0)),
                  row(1024), row(MLA_Q_RANK), row(MLA_KV_RANK), row(POOL_WIDTH), row(1024)] + [whole(w) for w in weights],
        out_specs=[row(IN_AB_PAD), vec(MLA_Q_RANK), vec(MLA_KV_RANK)] + [whole(w) for w in weights]
        + [pl.BlockSpec((IN_AB_PAD, 1024), lambda i: (0, 0))],
        out_shape=[jax.ShapeDtypeStruct((t, IN_AB_PAD), BF16), jax.ShapeDtypeStruct((1, MLA_Q_RANK), F32),
                   jax.ShapeDtypeStruct((1, MLA_KV_RANK), F32)] + [jax.ShapeDtypeStruct(w.shape, F32) for w in weights]
        + [jax.ShapeDtypeStruct((IN_AB_PAD, 1024), F32)],
        scratch_shapes=[pltpu.VMEM((tm + POOL_HALO, POOL_WIDTH), F32), pltpu.VMEM((tm, 1024), BF16),
                        pltpu.VMEM((tm, 1024), BF16)],
        compiler_params=_params(("arbitrary",)),
    )(proj, q_a_norm, kv_a_norm, dq, dk, dv, cos_t, sin_t, dyb, dyb, dz, qn, kvn, d, hn, wq, wk, wv, wpool)


def _gate_out_proj(o, ybraw, proj, pool_scale, w, hres, *, name):
    t = o.shape[0]
    tm = _tile(t, 2 * ROW_TILE)

    def body(o_ref, yb_ref, z_ref, ps_ref, w_ref, h_ref, ho_ref, y_ref):
        z = z_ref[...]
        sz = z * _sigmoid(z)
        y_ref[:, 0:512] = (o_ref[...] * sz[:, 0:512]).astype(BF16)
        y_ref[:, 512:1024] = (yb_ref[...] * ps_ref[...] * sz[:, 512:1024]).astype(BF16)
        ho_ref[...] = h_ref[...] + _dot(y_ref[...], w_ref[...])

    row = lambda w_: pl.BlockSpec((tm, w_), lambda i: (i, 0))
    return pl.pallas_call(
        body, name=name, grid=(t // tm,),
        in_specs=[row(512), row(512), pl.BlockSpec((tm, 1024), lambda i: (i, 1)), pl.BlockSpec((1, 512), lambda i: (0, 0)),
                  pl.BlockSpec(w.shape, lambda i: (0, 0)), row(1024)],
        out_specs=[row(1024), row(1024)],
        out_shape=[jax.ShapeDtypeStruct((t, 1024), F32), jax.ShapeDtypeStruct((t, 1024), BF16)],
        compiler_params=_params(("parallel",)),
    )(o, ybraw, proj, pool_scale, w, hres)


def _gate_bwd(dh, w, o, ybraw, proj, pool_scale, *, name):
    t = o.shape[0]
    tm = _tile(t, ROW_TILE)

    def body(dh_ref, w_ref, o_ref, yb_ref, z_ref, ps_ref, do_ref, dl_ref, dyb_ref, dz_ref, dps_ref):
        i = pl.program_id(0)
        z = z_ref[...]
        sg = _sigmoid(z)
        sz = z * sg
        dsz = sg * (1.0 + z * (1.0 - sg))
        dyv = _dot(dh_ref[...], w_ref[...], NT)
        dcat = dyv * sz
        ov = o_ref[...]
        ybs = yb_ref[...] * ps_ref[...]
        dz_ref[:, 0:512] = (dyv[:, 0:512] * ov * dsz[:, 0:512]).astype(BF16)
        dz_ref[:, 512:1024] = (dyv[:, 512:1024] * ybs * dsz[:, 512:1024]).astype(BF16)
        do = dcat[:, 0:512]
        do_ref[...] = do.astype(BF16)
        r_i = (lax.broadcasted_iota(jnp.int32, (1024, 512), 0) % 512) // MLA_V
        c_i = lax.broadcasted_iota(jnp.int32, (1024, 512), 1) // MLA_V
        prod = do * ov
        hi = prod.astype(BF16)
        lo = (prod - hi.astype(F32)).astype(BF16)
        dl_ref[...] = _dot(jnp.concatenate([hi, lo], axis=1), (r_i == c_i).astype(BF16))
        dyb_ref[...] = (dcat[:, 512:1024] * ps_ref[...]).astype(BF16)

        @pl.when(i == 0)
        def _():
            dps_ref[...] = jnp.zeros_like(dps_ref)

        dps_ref[...] += jnp.sum(dcat[:, 512:1024] * yb_ref[...], axis=0, keepdims=True)

    row = lambda w: pl.BlockSpec((tm, w), lambda i: (i, 0))
    vec = pl.BlockSpec((1, 512), lambda i: (0, 0))
    return pl.pallas_call(
        body, name=name, grid=(t // tm,),
        in_specs=[row(1024), pl.BlockSpec(w.shape, lambda i: (0, 0)), row(512), row(512),
                  pl.BlockSpec((tm, 1024), lambda i: (i, 1)), vec],
        out_specs=[row(512), row(512), row(512), row(1024), vec],
        out_shape=[jax.ShapeDtypeStruct((t, 512), BF16), jax.ShapeDtypeStruct((t, 512), F32),
                   jax.ShapeDtypeStruct((t, 512), BF16), jax.ShapeDtypeStruct((t, 1024), BF16),
                   jax.ShapeDtypeStruct((1, 512), F32)],
        compiler_params=_params(("arbitrary",)),
    )(dh, w, o, ybraw, proj, pool_scale)


ATT_HP_FWD = 4
ATT_HP_BWD = 2


def _diag_mask(tq):
    return lax.broadcasted_iota(jnp.int32, (tq, tq), 1) <= lax.broadcasted_iota(jnp.int32, (tq, tq), 0)


def _block_schedule(nq, key_major):
    if key_major:
        pairs = [(qi, ki) for ki in range(nq) for qi in range(ki, nq)]
    else:
        pairs = [(qi, ki) for qi in range(nq) for ki in range(qi + 1)]
    return jnp.asarray([p[0] for p in pairs], jnp.int32), jnp.asarray([p[1] for p in pairs], jnp.int32)


def _attn_fwd(q, k, v, *, name):
    t = q.shape[0]
    tq = _tile(t, ATT_TILE)
    nq = t // tq
    hp = ATT_HP_FWD
    qi_tab, ki_tab = _block_schedule(nq, key_major=False)

    def body(qi_ref, ki_ref, q_ref, k_ref, v_ref, o_ref, lse_ref, m_sc, l_sc, acc_sc):
        step = pl.program_id(1)
        qi, ki = qi_ref[step], ki_ref[step]

        @pl.when(ki == 0)
        def _():
            m_sc[...] = jnp.full_like(m_sc, -jnp.inf)
            l_sc[...] = jnp.zeros_like(l_sc)
            acc_sc[...] = jnp.zeros_like(acc_sc)

        def block(on_diagonal):
            scores = []
            for h in range(hp):
                sl = slice(h * LANES, (h + 1) * LANES)
                scores.append(_dot(q_ref[:, sl], k_ref[:, sl], NT))
            if on_diagonal:
                mask = _diag_mask(tq)
                scores = [jnp.where(mask, s, -jnp.inf) for s in scores]
            for h, s in enumerate(scores):
                vv = v_ref[:, (h // 2) * LANES:(h // 2 + 1) * LANES]
                m_prev = m_sc[h]
                m_new = jnp.maximum(m_prev, jnp.max(s, axis=-1, keepdims=True))
                alpha = jnp.exp2(m_prev - m_new)
                p = jnp.exp2(s - m_new[:, 0:1])
                l_sc[h] = alpha * l_sc[h] + jnp.sum(p, axis=-1, keepdims=True)
                acc_sc[h] = alpha * acc_sc[h] + _dot(p.astype(BF16), vv)
                m_sc[h] = m_new

        pl.when(ki < qi)(functools.partial(block, False))
        pl.when(ki == qi)(functools.partial(block, True))

        @pl.when(ki == qi)
        def _():
            first = lax.broadcasted_iota(jnp.int32, (tq, LANES), 1) < MLA_V
            for pr in range(hp // 2):
                a, b = 2 * pr, 2 * pr + 1
                sl = slice(pr * LANES, (pr + 1) * LANES)
                o_ref[:, sl] = jnp.where(first, acc_sc[a] / l_sc[a], acc_sc[b] / l_sc[b])
                lse_ref[:, sl] = jnp.where(first, m_sc[a] + jnp.log2(l_sc[a]), m_sc[b] + jnp.log2(l_sc[b]))

    grid_spec = pltpu.PrefetchScalarGridSpec(
        num_scalar_prefetch=2, grid=(MLA_HEADS // hp, qi_tab.shape[0]),
        in_specs=[pl.BlockSpec((tq, hp * LANES), lambda g, s, qt, kt: (qt[s], g)),
                  pl.BlockSpec((tq, hp * LANES), lambda g, s, qt, kt: (kt[s], g)),
                  pl.BlockSpec((tq, hp * MLA_V), lambda g, s, qt, kt: (kt[s], g))],
        out_specs=[pl.BlockSpec((tq, hp * MLA_V), lambda g, s, qt, kt: (qt[s], g)),
                   pl.BlockSpec((tq, hp * MLA_V), lambda g, s, qt, kt: (qt[s], g))],
        scratch_shapes=[pltpu.VMEM((hp, tq, LANES), F32)] * 3,
    )
    return pl.pallas_call(
        body, name=name, grid_spec=grid_spec,
        out_shape=[jax.ShapeDtypeStruct((t, 512), F32), jax.ShapeDtypeStruct((t, 512), F32)],
        compiler_params=_params(("parallel", "arbitrary")),
    )(qi_tab, ki_tab, q, k, v)


def _attn_bwd(q, k, v, do, lse, delta, *, name):
    t = q.shape[0]
    tq = _tile(t, ATT_TILE)
    nq = t // tq
    hp = ATT_HP_BWD
    qi_tab, ki_tab = _block_schedule(nq, key_major=True)

    def body(qi_ref, ki_ref, q_ref, k_ref, v_ref, do_ref, lse_ref, dl_ref, dq_ref, dk_ref, dv_ref, dk_sc, dv_sc):
        step = pl.program_id(1)
        qi, ki = qi_ref[step], ki_ref[step]

        @pl.when(step == 0)
        def _():
            dq_ref[...] = jnp.zeros_like(dq_ref)

        @pl.when(qi == ki)
        def _():
            dk_sc[...] = jnp.zeros_like(dk_sc)
            dv_sc[...] = jnp.zeros_like(dv_sc)

        def block(on_diagonal):
            lane = lax.broadcasted_iota(jnp.int32, (tq, LANES), 1)
            rows = pl.ds(pl.multiple_of(qi * tq, tq), tq)
            heads = [slice(h * LANES, (h + 1) * LANES) for h in range(hp)]
            scores = [_dot(q_ref[:, sl], k_ref[:, sl], NT) for sl in heads]
            dps = []
            for h in range(hp):
                dov = do_ref[:, (h // 2) * LANES:(h // 2 + 1) * LANES]
                mine = (lane < MLA_V) if h % 2 == 0 else (lane >= MLA_V)
                dps.append(_dot(jnp.where(mine, dov, jnp.zeros_like(dov)), v_ref[:, (h // 2) * LANES:(h // 2 + 1) * LANES], NT))
            mask = _diag_mask(tq) if on_diagonal else None
            for h, sl in enumerate(heads):
                col = (h // 2) * LANES + (h % 2) * MLA_V
                p = jnp.exp2(scores[h] - lse_ref[:, col:col + 1])
                if on_diagonal:
                    p = jnp.where(mask, p, 0.0)
                ds = (p * (dps[h] - dl_ref[:, col:col + 1])).astype(BF16)
                dv_sc[h] += _dot(p.astype(BF16), do_ref[:, (h // 2) * LANES:(h // 2 + 1) * LANES], TN)
                dk_sc[h] += _dot(ds, q_ref[:, sl], TN)
                dq_ref[rows, sl] += _dot(ds, k_ref[:, sl], NN)

        pl.when(qi > ki)(functools.partial(block, False))
        pl.when(qi == ki)(functools.partial(block, True))

        @pl.when(qi == nq - 1)
        def _():
            first = lax.broadcasted_iota(jnp.int32, (tq, LANES), 1) < MLA_V
            for h in range(hp):
                dk_ref[:, h * LANES:(h + 1) * LANES] = dk_sc[h] * (1.0 / LOG2E)
            for pr in range(hp // 2):
                dv_ref[:, pr * LANES:(pr + 1) * LANES] = jnp.where(first, dv_sc[2 * pr], dv_sc[2 * pr + 1]).astype(BF16)

    qrow = lambda w: pl.BlockSpec((tq, w), lambda g, s, qt, kt: (qt[s], g))
    krow = lambda w: pl.BlockSpec((tq, w), lambda g, s, qt, kt: (kt[s], g))
    grid_spec = pltpu.PrefetchScalarGridSpec(
        num_scalar_prefetch=2, grid=(MLA_HEADS // hp, qi_tab.shape[0]),
        in_specs=[qrow(hp * LANES), krow(hp * LANES), krow(hp * MLA_V), qrow(hp * MLA_V), qrow(hp * MLA_V), qrow(hp * MLA_V)],
        out_specs=[pl.BlockSpec((t, hp * LANES), lambda g, s, qt, kt: (0, g)), krow(hp * LANES), krow(hp * MLA_V)],
        scratch_shapes=[pltpu.VMEM((hp, tq, LANES), F32), pltpu.VMEM((hp, tq, LANES), F32)],
    )
    return pl.pallas_call(
        body, name=name, grid_spec=grid_spec,
        out_shape=[jax.ShapeDtypeStruct((t, 1024), F32), jax.ShapeDtypeStruct((t, 1024), F32),
                   jax.ShapeDtypeStruct((t, 512), BF16)],
        compiler_params=_params(("parallel", "arbitrary")),
    )(qi_tab, ki_tab, q, k, v, do, lse, delta)


def _conv_rows(ext, tm, w_ref, sec):
    c0 = sec * 1024
    y = ext[CONV_HALO - 3:CONV_HALO - 3 + tm, c0:c0 + 1024] * w_ref[0:1, c0:c0 + 1024]
    for j in range(1, CONV_WIDTH):
        y = y + ext[CONV_HALO - 3 + j:CONV_HALO - 3 + j + tm, c0:c0 + 1024] * w_ref[j:j + 1, c0:c0 + 1024]
    return y


def _c_prep(proj_c, conv_w, a_log, dt_bias, *, name):
    t = proj_c.shape[0]
    tm = _tile(t, ROW_TILE)
    hb = tm // CONV_HALO

    def body(p_ref, halo_ref, ab_ref, w_ref, al_ref, dtb_ref, q_ref, k_ref, v_ref, g_ref, b_ref, gt_ref, ext):
        i = pl.program_id(0)
        ext[0:CONV_HALO, :] = jnp.where(i > 0, halo_ref[...], 0.0)
        ext[CONV_HALO:CONV_HALO + tm, :] = p_ref[...]
        for sec, o_ref in enumerate((q_ref, k_ref, v_ref)):
            y = _conv_rows(ext, tm, w_ref, sec)
            y = y * _sigmoid(y)
            if sec == 2:
                o_ref[...] = y
                continue
            scale = GDN_DK ** -0.5 if sec == 0 else 1.0
            for h in range(GDN_HEADS):
                sl = slice(h * LANES, (h + 1) * LANES)
                blk = y[:, sl]
                r = lax.rsqrt(jnp.sum(blk * blk, axis=-1, keepdims=True) + RMS_EPS)
                o_ref[:, sl] = blk * (r * scale)
        ab = ab_ref[...]
        g = -jnp.exp(al_ref[...]) * _softplus(ab + dtb_ref[...])
        beta = _sigmoid(ab)
        ri = lax.broadcasted_iota(jnp.int32, (tm, tm), 0)
        ci = lax.broadcasted_iota(jnp.int32, (tm, tm), 1)
        lower = ((ri // CHUNK) == (ci // CHUNK)) & (ri >= ci)
        gc = _dot(lower.astype(F32), g, NN, HI)
        eye = lax.broadcasted_iota(jnp.int32, (LANES, LANES), 0) == lax.broadcasted_iota(jnp.int32, (LANES, LANES), 1)
        gt_ref[...] = _dot(eye.astype(F32), gc, NT, HI)[0:GDN_HEADS, :]
        for h in range(GDN_HEADS):
            sl = slice(h * LANES, (h + 1) * LANES)
            g_ref[:, sl] = jnp.broadcast_to(gc[:, h:h + 1], (tm, LANES))
            b_ref[:, sl] = jnp.broadcast_to(beta[:, GDN_HEADS + h:GDN_HEADS + h + 1], (tm, LANES))

    row = lambda w: pl.BlockSpec((tm, w), lambda i: (i, 0))
    vec = lambda r, w: pl.BlockSpec((r, w), lambda i: (0, 0))
    out = jax.ShapeDtypeStruct((t, 1024), F32)
    return pl.pallas_call(
        body, name=name, grid=(t // tm,),
        in_specs=[row(3072), pl.BlockSpec((CONV_HALO, 3072), lambda i: (jnp.maximum(i * hb - 1, 0), 0)),
                  pl.BlockSpec((tm, LANES), lambda i: (i, 32)), vec(CONV_WIDTH, 3072), vec(1, LANES), vec(1, LANES)],
        out_specs=[row(1024)] * 5 + [pl.BlockSpec((GDN_HEADS, tm), lambda i: (0, i))],
        out_shape=[out] * 5 + [jax.ShapeDtypeStruct((GDN_HEADS, t), F32)],
        scratch_shapes=[pltpu.VMEM((tm + CONV_HALO, 3072), F32)],
        compiler_params=_params(("parallel",)),
    )(proj_c, proj_c, proj_c, conv_w, a_log, dt_bias)


def _c_prep_bwd(proj_c, conv_w, a_log, dt_bias, dq, dk, dv, dgb, dbb, dz, *, name):
    t = proj_c.shape[0]
    tm = _tile(t, ROW_TILE)
    hb = tm // CONV_HALO
    nt = t // tm
    rev = lambda i: nt - 1 - i

    def body(p_ref, halo_ref, ab_ref, w_ref, al_ref, dtb_ref, dq_ref, dk_ref, dv_ref, dg_ref, db_ref, dz_ref,
             dp_ref, dw_ref, dal_ref, ddt_ref, ext, dyext, carry, taps):
        step = pl.program_id(0)
        i = rev(step)

        @pl.when(step == 0)
        def _():
            dw_ref[...] = jnp.zeros_like(dw_ref)
            dal_ref[...] = jnp.zeros_like(dal_ref)
            ddt_ref[...] = jnp.zeros_like(ddt_ref)
            carry[...] = jnp.zeros_like(carry)

        ext[0:CONV_HALO, :] = jnp.where(i > 0, halo_ref[...], 0.0)
        ext[CONV_HALO:CONV_HALO + tm, :] = p_ref[...]
        for sec, g_ref in enumerate((dq_ref, dk_ref, dv_ref)):
            c0 = sec * 1024
            for j in range(CONV_WIDTH):
                taps[j] = ext[CONV_HALO - 3 + j:CONV_HALO - 3 + j + tm, c0:c0 + 1024]
            y = taps[0] * w_ref[0:1, c0:c0 + 1024]
            for j in range(1, CONV_WIDTH):
                y = y + taps[j] * w_ref[j:j + 1, c0:c0 + 1024]
            sg = _sigmoid(y)
            act = y * sg
            if sec == 2:
                dact = g_ref[...]
            else:
                scale = GDN_DK ** -0.5 if sec == 0 else 1.0
                parts = []
                for h in range(GDN_HEADS):
                    sl = slice(h * LANES, (h + 1) * LANES)
                    blk = act[:, sl]
                    r = lax.rsqrt(jnp.sum(blk * blk, axis=-1, keepdims=True) + RMS_EPS)
                    n = blk * r
                    dn = g_ref[:, sl] * scale
                    parts.append(r * (dn - n * jnp.sum(dn * n, axis=-1, keepdims=True)))
                dact = jnp.concatenate(parts, axis=-1)
            dy = dact * (sg * (1.0 + y * (1.0 - sg)))
            dyext[0:tm, c0:c0 + 1024] = dy
            for j in range(CONV_WIDTH):
                dw_ref[j:j + 1, c0:c0 + 1024] += jnp.sum(dy * taps[j], axis=0, keepdims=True)
        dyext[tm:tm + CONV_HALO, :] = carry[...]
        carry[...] = dyext[0:CONV_HALO, :]
        for sec in range(3):
            c0 = sec * 1024
            dx = dyext[3:3 + tm, c0:c0 + 1024] * w_ref[0:1, c0:c0 + 1024]
            for j in range(1, CONV_WIDTH):
                dx = dx + dyext[3 - j:3 - j + tm, c0:c0 + 1024] * w_ref[j:j + 1, c0:c0 + 1024]
            dp_ref[:, c0:c0 + 1024] = dx.astype(BF16)
        dp_ref[:, 3072:4096] = dz_ref[...]
        lane = lax.broadcasted_iota(jnp.int32, (tm, LANES), 1)
        dg = jnp.zeros((tm, LANES), F32)
        dbeta = jnp.zeros((tm, LANES), F32)
        for h in range(GDN_HEADS):
            sl = slice(h * LANES, (h + 1) * LANES)
            dg = dg + jnp.where(lane == h, dg_ref[:, sl], 0.0)
            dbeta = dbeta + jnp.where(lane == GDN_HEADS + h, db_ref[:, sl], 0.0)
        ri = lax.broadcasted_iota(jnp.int32, (tm, tm), 0)
        ci = lax.broadcasted_iota(jnp.int32, (tm, tm), 1)
        upper = ((ri // CHUNK) == (ci // CHUNK)) & (ri <= ci)
        dg = _dot(upper.astype(F32), dg, NN, HI)
        pre = ab_ref[...] + dtb_ref[...]
        s = _sigmoid(pre)
        a_exp = jnp.exp(al_ref[...])
        dg_da = dg * (-a_exp * s)
        dp_ref[:, 4096:IN_C_PAD] = (dg_da + dbeta * s * (1.0 - s)).astype(BF16)
        dal_ref[...] += jnp.sum(dg * (-a_exp * _softplus(pre)), axis=0, keepdims=True)
        ddt_ref[...] += jnp.sum(dg_da, axis=0, keepdims=True)

    row = lambda w: pl.BlockSpec((tm, w), lambda s: (rev(s), 0))
    vec = lambda r, w: pl.BlockSpec((r, w), lambda s: (0, 0))
    return pl.pallas_call(
        body, name=name, grid=(nt,),
        in_specs=[row(3072), pl.BlockSpec((CONV_HALO, 3072), lambda s: (jnp.maximum(rev(s) * hb - 1, 0), 0)),
                  pl.BlockSpec((tm, LANES), lambda s: (rev(s), 32)), vec(CONV_WIDTH, 3072), vec(1, LANES), vec(1, LANES),
                  row(1024), row(1024), row(1024), row(1024), row(1024), row(1024)],
        out_specs=[row(IN_C_PAD), vec(CONV_WIDTH, 3072), vec(1, LANES), vec(1, LANES)],
        out_shape=[jax.ShapeDtypeStruct((t, IN_C_PAD), BF16), jax.ShapeDtypeStruct((CONV_WIDTH, 3072), F32),
                   jax.ShapeDtypeStruct((1, LANES), F32), jax.ShapeDtypeStruct((1, LANES), F32)],
        scratch_shapes=[pltpu.VMEM((tm + CONV_HALO, 3072), F32), pltpu.VMEM((tm + CONV_HALO, 3072), F32),
                        pltpu.VMEM((CONV_HALO, 3072), F32), pltpu.VMEM((CONV_WIDTH, tm, 1024), F32)],
        compiler_params=_params(("arbitrary",)),
    )(proj_c, proj_c, proj_c, conv_w, a_log, dt_bias, dq, dk, dv, dgb, dbb, dz)


def _o_gate_bwd(dh, w, o, proj_c, o_norm, *, name):
    t = o.shape[0]
    tm = _tile(t, 2 * ROW_TILE)

    def body(dh_ref, w_ref, o_ref, z_ref, g_ref, do_ref, dz_ref, dg_ref, dy_ref):
        i = pl.program_id(0)

        @pl.when(i == 0)
        def _():
            dg_ref[...] = jnp.zeros_like(dg_ref)

        dy_ref[...] = _dot(dh_ref[...], w_ref[...], NT)
        dg = jnp.zeros((1, LANES), F32)
        for h in range(GDN_HEADS):
            sl = slice(h * LANES, (h + 1) * LANES)
            x = o_ref[:, sl]
            r = lax.rsqrt(jnp.mean(x * x, axis=-1, keepdims=True) + RMS_EPS)
            xh = x * r
            z = z_ref[:, sl]
            sg = _sigmoid(z)
            dyv = dy_ref[:, sl]
            dn = dyv * (z * sg)
            dz_ref[:, sl] = (dyv * xh * g_ref[...] * (sg * (1.0 + z * (1.0 - sg)))).astype(BF16)
            dxh = dn * g_ref[...]
            do_ref[:, sl] = r * (dxh - xh * jnp.mean(dxh * xh, axis=-1, keepdims=True))
            dg = dg + jnp.sum(dn * xh, axis=0, keepdims=True)
        dg_ref[...] += dg

    row = pl.BlockSpec((tm, 1024), lambda i: (i, 0))
    vec = pl.BlockSpec((1, LANES), lambda i: (0, 0))
    return pl.pallas_call(
        body, name=name, grid=(t // tm,),
        in_specs=[row, pl.BlockSpec(w.shape, lambda i: (0, 0)), row, pl.BlockSpec((tm, 1024), lambda i: (i, 3)), vec],
        out_specs=[row, row, vec],
        out_shape=[jax.ShapeDtypeStruct((t, 1024), F32), jax.ShapeDtypeStruct((t, 1024), BF16),
                   jax.ShapeDtypeStruct((1, LANES), F32)],
        scratch_shapes=[pltpu.VMEM((tm, 1024), F32)],
        compiler_params=_params(("arbitrary",)),
    )(dh, w, o, proj_c, o_norm)


PAIR = 2 * CHUNK
GDN_HP = 8


def _bdot(a, b, dims=NN):
    return _dot(a.astype(BF16), b.astype(BF16), dims)


def _each(f, *lists):
    return [f(*args) for args in zip(*lists)]


def _pair_common(q, k, v, gci, gcj, beta):
    ri = lax.broadcasted_iota(jnp.int32, (PAIR, PAIR), 0)
    ci = lax.broadcasted_iota(jnp.int32, (PAIR, PAIR), 1)
    same = (ri // CHUNK) == (ci // CHUNK)
    incl = same & (ri >= ci)
    strict = same & (ri > ci)
    eye = (ri == ci).astype(F32)
    first = lax.broadcasted_iota(jnp.int32, (PAIR, LANES), 0) < CHUNK
    gamma = _each(lambda gi, gj: jnp.where(incl, jnp.exp(jnp.minimum(gi - gj, 0.0)), 0.0), gci, gcj)
    kb = _each(jnp.multiply, k, beta)
    kq = _each(lambda a, b, c_: _bdot(jnp.concatenate([a, b], axis=0), c_, NT), kb, q, k)
    kk = _each(lambda x: x[:PAIR], kq)
    qk = _each(lambda x: x[PAIR:], kq)
    m = _each(lambda x, g: jnp.where(strict, x * g, 0.0), kk, gamma)
    tm_ = _each(lambda x: eye - x, m)
    pw = _each(lambda x: _bdot(x, x), m)
    for it in range(5):
        if it < 4:
            both = _each(lambda x, p: _bdot(jnp.concatenate([x, p], axis=0), p), tm_, pw)
            tm_ = _each(lambda x, b: x + b[:PAIR], tm_, both)
            pw = _each(lambda b: b[PAIR:], both)
        else:
            tm_ = _each(lambda x, p: x + _bdot(x, p), tm_, pw)
    eg = _each(jnp.exp, gci)
    vb = _each(jnp.multiply, v, beta)
    kbe = _each(jnp.multiply, kb, eg)
    uw = _each(lambda x, a, b: _bdot(x, jnp.concatenate([a, b], axis=1)), tm_, vb, kbe)
    attn = _each(lambda x, g: jnp.where(incl, x * g, 0.0), qk, gamma)
    gl_a = _each(lambda g: g[CHUNK - 1:CHUNK, :], gci)
    gl_b = _each(lambda g: g[PAIR - 1:PAIR, :], gci)
    ek = _each(lambda a, b, g: jnp.exp(jnp.where(first, a, b) - g), gl_a, gl_b, gci)
    return dict(incl=incl, strict=strict, gamma=gamma, kb=kb, m=m, tm=tm_, eg=eg, vb=vb, kbe=kbe,
                u=_each(lambda x: x[:, :LANES], uw), w=_each(lambda x: x[:, LANES:], uw), attn=attn,
                qd=_each(jnp.multiply, q, eg), ek=ek, kd=_each(jnp.multiply, k, ek),
                glast_a=_each(jnp.exp, gl_a), glast_b=_each(jnp.exp, gl_b))


def _gdn_specs(t, ts, order):
    nc = ts // CHUNK
    blk = pl.BlockSpec((ts, GDN_HP * LANES), lambda h, s: (order(s), h))
    row = pl.BlockSpec((GDN_HP, 1, ts), lambda h, s: (h, 0, order(s)))
    st = pl.BlockSpec((GDN_HP, nc, LANES, LANES), lambda h, s: (h, order(s), 0, 0))
    return blk, row, st


def _gdn_fwd(q, k, v, gcb, gct, bb, *, name):
    t = q.shape[0]
    ts = _tile(t, GDN_TILE)
    npair = ts // PAIR

    def body(q_ref, k_ref, v_ref, g_ref, gt_ref, b_ref, o_ref, st_ref, s_sc):
        @pl.when(pl.program_id(1) == 0)
        def _():
            s_sc[...] = jnp.zeros_like(s_sc)

        def pair(pi, _):
            rows = pl.ds(pl.multiple_of(pi * PAIR, PAIR), PAIR)
            heads = [slice(hh * LANES, (hh + 1) * LANES) for hh in range(GDN_HP)]
            c = CHUNK
            cat0 = lambda *xs: jnp.concatenate(xs, axis=0)
            s0 = [s_sc[hh] for hh in range(GDN_HP)]
            cm = _pair_common([q_ref[rows, sl] for sl in heads], [k_ref[rows, sl] for sl in heads],
                              [v_ref[rows, sl] for sl in heads], [g_ref[rows, sl] for sl in heads],
                              [gt_ref[hh, :, rows] for hh in range(GDN_HP)], [b_ref[rows, sl] for sl in heads])
            u, w, qd, kd = cm["u"], cm["w"], cm["qd"], cm["kd"]
            r0 = _each(lambda w_, q_, s: _bdot(cat0(w_[:c], q_[:c]), s), w, qd, s0)
            vn_a = _each(lambda u_, r: u_[:c] - r[:c], u, r0)
            s1 = _each(lambda s, gl, k_, vn: s * gl + _bdot(k_[:c], vn, TN), s0, cm["glast_a"], kd, vn_a)
            r1 = _each(lambda w_, q_, s: _bdot(cat0(w_[c:], q_[c:]), s), w, qd, s1)
            vn_b = _each(lambda u_, r: u_[c:] - r[:c], u, r1)
            s2 = _each(lambda s, gl, k_, vn: s * gl + _bdot(k_[c:], vn, TN), s1, cm["glast_b"], kd, vn_b)
            o = _each(lambda ra, rb, at, va, vb_: cat0(ra[c:], rb[c:]) + _bdot(at, cat0(va, vb_)),
                      r0, r1, cm["attn"], vn_a, vn_b)
            for hh, sl in enumerate(heads):
                st_ref[hh, 2 * pi] = s0[hh]
                st_ref[hh, 2 * pi + 1] = s1[hh]
                s_sc[hh] = s2[hh]
                o_ref[rows, sl] = o[hh]
            return 0

        lax.fori_loop(0, npair, pair, 0)

    blk, row, st = _gdn_specs(t, ts, lambda s: s)
    return pl.pallas_call(
        body, name=name, grid=(GDN_HEADS // GDN_HP, t // ts), in_specs=[blk, blk, blk, blk, row, blk],
        out_specs=[blk, st],
        out_shape=[jax.ShapeDtypeStruct((t, 1024), F32), jax.ShapeDtypeStruct((GDN_HEADS, t // CHUNK, LANES, LANES), F32)],
        scratch_shapes=[pltpu.VMEM((GDN_HP, LANES, LANES), F32)],
        compiler_params=_params(("parallel", "arbitrary")),
    )(q, k, v, gcb, gct, bb)


def _gdn_bwd(q, k, v, gcb, gct, bb, do, states, *, name):
    t = q.shape[0]
    ts = _tile(t, GDN_TILE)
    npair = ts // PAIR
    ns = t // ts
    c = CHUNK

    def body(q_ref, k_ref, v_ref, g_ref, gt_ref, b_ref, do_ref, st_ref, dq_ref, dk_ref, dv_ref, dg_ref, db_ref, ds_sc):
        @pl.when(pl.program_id(1) == 0)
        def _():
            ds_sc[...] = jnp.zeros_like(ds_sc)

        rowsum = lambda x: jnp.sum(x, axis=-1, keepdims=True)
        total = lambda x: jnp.sum(rowsum(x), axis=0, keepdims=True)
        cat0 = lambda *xs: jnp.concatenate(xs, axis=0)
        cat1 = lambda *xs: jnp.concatenate(xs, axis=1)

        def pair(step, _):
            pi = npair - 1 - step
            rows = pl.ds(pl.multiple_of(pi * PAIR, PAIR), PAIR)
            heads = [slice(hh * LANES, (hh + 1) * LANES) for hh in range(GDN_HP)]
            hs = range(GDN_HP)
            qv, kv, vv = ([r[rows, sl] for sl in heads] for r in (q_ref, k_ref, v_ref))
            beta = [b_ref[rows, sl] for sl in heads]
            dov = [do_ref[rows, sl] for sl in heads]
            s0 = [st_ref[hh, 2 * pi] for hh in hs]
            s1 = [st_ref[hh, 2 * pi + 1] for hh in hs]
            ds2 = [ds_sc[hh] for hh in hs]
            cm = _pair_common(qv, kv, vv, [g_ref[rows, sl] for sl in heads], [gt_ref[hh, :, rows] for hh in hs], beta)
            u, w, qd, kd, attn = cm["u"], cm["w"], cm["qd"], cm["kd"], cm["attn"]
            tmat, gamma, eg = cm["tm"], cm["gamma"], cm["eg"]
            incl, strict = cm["incl"], cm["strict"]
            vn_a = _each(lambda u_, w_, s: u_[:c] - _bdot(w_[:c], s), u, w, s0)
            vn_b = _each(lambda u_, w_, s: u_[c:] - _bdot(w_[c:], s), u, w, s1)
            vn = _each(cat0, vn_a, vn_b)
            dvn_att = _each(lambda a, d: _bdot(a, d, TN), attn, dov)
            dattn = _each(lambda d, v_: jnp.where(incl, _bdot(d, v_, NT), 0.0), dov, vn)
            dvn_b = _each(lambda x, k_, d: x[c:] + _bdot(k_[c:], d), dvn_att, kd, ds2)
            rb = _each(lambda d, x, s: _bdot(cat0(d[c:], x), s, NT), dov, dvn_b, s1)
            dkd_b = _each(lambda v_, d: _bdot(v_, d, NT), vn_b, ds2)
            dgl_b = _each(lambda d, s: total(d * s), ds2, s1)
            ds1 = _each(lambda d, gl, q_, w_, o_, x: d * gl + _bdot(cat0(q_[c:], w_[c:]), cat0(o_[c:], -x), TN),
                        ds2, cm["glast_b"], qd, w, dov, dvn_b)
            dvn_a = _each(lambda x, k_, d: x[:c] + _bdot(k_[:c], d), dvn_att, kd, ds1)
            ra = _each(lambda d, x, s: _bdot(cat0(d[:c], x), s, NT), dov, dvn_a, s0)
            dkd_a = _each(lambda v_, d: _bdot(v_, d, NT), vn_a, ds1)
            dgl_a = _each(lambda d, s: total(d * s), ds1, s0)
            ds0 = _each(lambda d, gl, q_, w_, o_, x: d * gl + _bdot(cat0(q_[:c], w_[:c]), cat0(o_[:c], -x), TN),
                        ds1, cm["glast_a"], qd, w, dov, dvn_a)
            dvn = _each(cat0, dvn_a, dvn_b)
            dqd = _each(lambda a, b: cat0(a[:c], b[:c]), ra, rb)
            dw = _each(lambda a, b: -cat0(a[c:], b[c:]), ra, rb)
            dkd = _each(cat0, dkd_a, dkd_b)
            dvw = _each(cat1, dvn, dw)
            dvbk = _each(lambda t_, x: _bdot(t_, x, TN), tmat, dvw)
            dvb = _each(lambda x: x[:, :LANES], dvbk)
            dkbe = _each(lambda x: x[:, LANES:], dvbk)
            dt_ = _each(lambda x, a, b: _bdot(x, cat1(a, b), NT), dvw, cm["vb"], cm["kbe"])
            da1 = _each(lambda t_, x: _bdot(t_, x, TN), tmat, dt_)
            dm = _each(lambda x, t_: jnp.where(strict, -_bdot(x, t_, NT), 0.0), da1, tmat)
            dkk = _each(jnp.multiply, dm, gamma)
            dqk = _each(jnp.multiply, dattn, gamma)
            z = _each(lambda a, b, c_, d: a * b + c_ * d, dm, cm["m"], dattn, attn)
            dkq = _each(lambda a, b, k_: _bdot(cat0(a, b), k_), dkk, dqk, kv)
            dkb = _each(lambda x, y, e: x[:PAIR] + y * e, dkq, dkbe, eg)
            dk = _each(lambda a, b, kb_, q_, x, e, y, be: _bdot(cat0(a, b), cat0(kb_, q_), TN) + x * e + y * be,
                       dkk, dqk, cm["kb"], qv, dkd, cm["ek"], dkb, beta)
            dq = _each(lambda x, y, e: x[PAIR:] + y * e, dkq, dqd, eg)

            def colsum_of(z_):
                zh = z_.astype(BF16)
                zl = (z_ - zh.astype(F32)).astype(BF16)
                return _dot(cat0(zh, zl), jnp.ones((2 * PAIR, LANES), BF16), TN)

            colsum = _each(colsum_of, z)
            ri = lax.broadcasted_iota(jnp.int32, (PAIR, LANES), 0)
            for hh, sl in enumerate(heads):
                dkd_kd = dkd[hh] * kd[hh]
                dgc = (rowsum(z[hh]) - colsum[hh] + rowsum(dqd[hh] * qd[hh]) - rowsum(dkd_kd)
                       + rowsum(dkbe[hh] * cm["kbe"][hh]))
                last_a = total(dkd_kd[:c]) + dgl_a[hh] * cm["glast_a"][hh]
                last_b = total(dkd_kd[c:]) + dgl_b[hh] * cm["glast_b"][hh]
                dgc = dgc + jnp.where(ri == c - 1, last_a, 0.0) + jnp.where(ri == PAIR - 1, last_b, 0.0)
                ds_sc[hh] = ds0[hh]
                dq_ref[rows, sl] = dq[hh]
                dk_ref[rows, sl] = dk[hh]
                dv_ref[rows, sl] = dvb[hh] * beta[hh]
                db_ref[rows, sl] = jnp.broadcast_to(rowsum(dkb[hh] * kv[hh]) + rowsum(dvb[hh] * vv[hh]), (PAIR, LANES))
                dg_ref[rows, sl] = dgc
            return 0

        lax.fori_loop(0, npair, pair, 0)

    blk, row, st = _gdn_specs(t, ts, lambda s: ns - 1 - s)
    out = jax.ShapeDtypeStruct((t, 1024), F32)
    return pl.pallas_call(
        body, name=name, grid=(GDN_HEADS // GDN_HP, ns), in_specs=[blk, blk, blk, blk, row, blk, blk, st],
        out_specs=[blk] * 5, out_shape=[out] * 5, scratch_shapes=[pltpu.VMEM((GDN_HP, LANES, LANES), F32)],
        compiler_params=_params(("parallel", "arbitrary")),
    )(q, k, v, gcb, gct, bb, do, states)


def _gate_out_proj_loss(o, proj_c, o_norm, w, hres, g, target, *, name):
    t, d = hres.shape
    tm = _tile(t, 2 * ROW_TILE)

    def body(o_ref, z_ref, on_ref, w_ref, h_ref, g_ref, t_ref, dh_ref, dhb_ref, dg_ref, loss_ref, dw_ref, y_ref):
        i = pl.program_id(0)
        for hd in range(GDN_HEADS):
            sl = slice(hd * LANES, (hd + 1) * LANES)
            ov = o_ref[:, sl]
            rr = lax.rsqrt(jnp.mean(ov * ov, axis=-1, keepdims=True) + RMS_EPS)
            z = z_ref[:, sl]
            y_ref[:, sl] = (ov * rr * on_ref[...] * (z * _sigmoid(z))).astype(BF16)
        x = h_ref[...] + _dot(y_ref[...], w_ref[...])
        r = lax.rsqrt(jnp.mean(x * x, axis=-1, keepdims=True) + RMS_EPS)
        xh = x * r
        err = xh * g_ref[...] - t_ref[...]
        dy = err * (1.0 / d)
        dxh = dy * g_ref[...]
        dh = r * (dxh - xh * jnp.mean(dxh * xh, axis=-1, keepdims=True))
        dh_ref[...] = dh
        dhb_ref[...] = dh.astype(BF16)

        @pl.when(i == 0)
        def _():
            dg_ref[...] = jnp.zeros_like(dg_ref)
            loss_ref[...] = jnp.zeros_like(loss_ref)
            dw_ref[...] = jnp.zeros_like(dw_ref)

        dg_ref[...] += jnp.sum(dy * xh, axis=0, keepdims=True)
        part = 0.5 * jnp.sum(jnp.mean(err * err, axis=-1, keepdims=True), axis=0, keepdims=True)
        loss_ref[...] += jnp.broadcast_to(part, loss_ref.shape)
        dw_ref[...] += _dot(y_ref[...], dhb_ref[...], TN)

    row = pl.BlockSpec((tm, d), lambda i: (i, 0))
    vec = pl.BlockSpec((1, d), lambda i: (0, 0))
    return pl.pallas_call(
        body, name=name, grid=(t // tm,),
        in_specs=[row, pl.BlockSpec((tm, 1024), lambda i: (i, 3)), pl.BlockSpec((1, LANES), lambda i: (0, 0)),
                  pl.BlockSpec(w.shape, lambda i: (0, 0)), row, vec, row],
        out_specs=[row, row, vec, pl.BlockSpec((8, LANES), lambda i: (0, 0)), pl.BlockSpec(w.shape, lambda i: (0, 0))],
        out_shape=[jax.ShapeDtypeStruct((t, d), F32), jax.ShapeDtypeStruct((t, d), BF16),
                   jax.ShapeDtypeStruct((1, d), F32), jax.ShapeDtypeStruct((8, LANES), F32),
                   jax.ShapeDtypeStruct(w.shape, F32)],
        scratch_shapes=[pltpu.VMEM((tm, d), BF16)],
        compiler_params=_params(("arbitrary",)),
    )(o, proj_c, o_norm, w, hres, g, target)


def _pad_cols(w, n):
    return jnp.pad(w, ((0, 0), (0, n - w.shape[1])))


def _layout_odd(w):
    return dict(
        winc=_pad_cols(w["w_in_c"], IN_C_PAD).astype(BF16), wout_c=w["w_out_c"].astype(BF16), conv_w=w["conv_w"],
        a_log=_pad_cols(w["a_log"], LANES), dt_bias=_pad_cols(w["dt_bias"], LANES),
        norm_c=w["norm_c"], o_norm=w["o_norm"], final_norm=w["final_norm"],
    )


def _layout_in_ab(w):
    z = lambda r, c: jnp.zeros((r, c), w["w_in_ab"].dtype)
    wi = w["w_in_ab"]
    win = jnp.concatenate([wi[:, :384], z(1024, 64), wi[:, 384:416], z(1024, 32), wi[:, 416:]], axis=1)
    pw = w["pool_w"]
    rows = []
    for g in range(4):
        rows.append(jnp.concatenate([pw[g] if j == g else jnp.zeros((128, 128), F32) for j in range(4)], axis=1))
    wpool = jnp.concatenate(rows, axis=0)
    half = MLA_ROPE // 2
    inv = 1.0 / (ROPE_THETA ** (jnp.arange(half, dtype=F32) / half))
    inv_lane = jnp.concatenate([jnp.zeros((MLA_NOPE,), F32), inv, inv, jnp.zeros((32,), F32)]).reshape(1, LANES)
    return dict(win=win.astype(BF16), wpool=wpool.astype(BF16), inv_lane=inv_lane, norm_ab=w["norm_ab"],
                q_a_norm=w["q_a_norm"], kv_a_norm=w["kv_a_norm"], pool_scale=w["pool_scale"])


def _layout_mid(w):
    wq = jnp.pad(w["w_q_b"].reshape(MLA_Q_RANK, MLA_HEADS, 96), ((0, 0), (0, 0), (0, 32))).reshape(MLA_Q_RANK, 1024)
    kv3 = w["w_kv_b"].reshape(MLA_KV_RANK, MLA_HEADS, 128)
    wk = jnp.pad(kv3[..., :MLA_NOPE], ((0, 0), (0, 0), (0, 64))).reshape(MLA_KV_RANK, 1024)
    wv = kv3[..., MLA_NOPE:].reshape(MLA_KV_RANK, 512)
    return dict(wq=wq.astype(BF16), wk=wk.astype(BF16), wv=wv.astype(BF16), wout_ab=w["w_out_ab"].astype(BF16))


def _unlayout_grads(g, names):
    out = {}
    for name in names:
        if name == "w_in_ab":
            dwin = g["win"]
            out[name] = jnp.concatenate([dwin[:384], dwin[448:480], dwin[512:]], axis=0)
        elif name == "w_q_b":
            out[name] = g["wq"].reshape(MLA_Q_RANK, MLA_HEADS, 128)[..., :96].reshape(MLA_Q_RANK, 768)
        elif name == "w_kv_b":
            out[name] = jnp.concatenate([g["wk"].reshape(MLA_KV_RANK, MLA_HEADS, 128)[..., :MLA_NOPE],
                                         g["wv"].reshape(MLA_KV_RANK, MLA_HEADS, MLA_V)], axis=-1).reshape(MLA_KV_RANK, 1024)
        elif name == "w_in_c":
            out[name] = g["winc"][:4112]
        else:
            out[name] = g[{"w_out_ab": "wout_ab", "w_out_c": "wout_c"}[name]]
    return out


def _local_step(x, pos, target, lw, more_weights=None, on_grads=None):
    mm = _matmul
    proj, hn = _rms_in_proj(x, lw["norm_ab"], lw["win"], name="rms_in_ab")
    if more_weights is not None:
        lw = {**lw, **more_weights("mid", proj)}
    q, k, v, ybraw, qn, kvn, d, cos_t, sin_t = _ab_prep(
        proj, pos, lw["inv_lane"], lw["q_a_norm"], lw["kv_a_norm"], lw["wq"], lw["wk"], lw["wv"], lw["wpool"], name="ab_prep")
    o, lse = _attn_fwd(q, k, v, name="attn_fwd")
    h1, y = _gate_out_proj(o, ybraw, proj, lw["pool_scale"], lw["wout_ab"], x, name="gate_out_ab")
    lo = lw if more_weights is None else more_weights("odd", h1)
    proj_c, hn1 = _rms_in_proj(h1, lo["norm_c"], lo["winc"], name="rms_in_c")
    q2, k2, v2, gb, bb, gt = _c_prep(proj_c, lo["conv_w"], lo["a_log"], lo["dt_bias"], name="c_prep")
    gt = gt.reshape(GDN_HEADS, 1, gt.shape[1])
    o2, states = _gdn_fwd(q2, k2, v2, gb, gt, bb, name="gdn_fwd")
    dh2, dh2b, d_final, loss, d_wout_c = _gate_out_proj_loss(
        o2, proj_c, lo["o_norm"], lo["wout_c"], h1, lo["final_norm"], target, name="gate_out_c_loss")
    g = {"final_norm": d_final, "wout_c": d_wout_c, "loss": loss}
    do2, dz2, g["o_norm"] = _o_gate_bwd(dh2b, lo["wout_c"], o2, proj_c, lo["o_norm"], name="gate_c_bwd")
    dq2, dk2, dv2, dgb, dbb = _gdn_bwd(q2, k2, v2, gb, gt, bb, do2, states, name="gdn_bwd")
    dproj_c, g["conv_w"], g["a_log"], g["dt_bias"] = _c_prep_bwd(
        proj_c, lo["conv_w"], lo["a_log"], lo["dt_bias"], dq2, dk2, dv2, dgb, dbb, dz2, name="c_prep_bwd")
    g["winc"] = mm(dproj_c, hn1, "tn", name="in_c_dw")
    notify = (lambda tag, after=None: 0.0) if on_grads is None else (lambda tag, after=None: on_grads(tag, g, after))
    dh1, g["norm_c"], dh1b, g["wout_ab"] = _matmul_rms_bwd(
        dproj_c, lo["winc"], h1, lo["norm_c"] + notify("odd"), dh2, name="in_c_dx_rms", prev_y=y)
    pool_scale = lw["pool_scale"] + notify("odd_go", dh1b) + notify("out_ab")
    do, delta, dyb, dz, g["pool_scale"] = _gate_bwd(dh1b, lw["wout_ab"], o, ybraw, proj, pool_scale, name="gate_ab_bwd")
    dq, dk, dv = _attn_bwd(q, k, v, do, lse, delta, name="attn_bwd")
    dproj, g["q_a_norm"], g["kv_a_norm"], g["wq"], g["wk"], g["wv"], g["wpool"], g["win"] = _ab_prep_bwd(
        proj, lw["q_a_norm"], lw["kv_a_norm"], dq, dk, dv, cos_t, sin_t, dyb, dz, qn, kvn, d, hn,
        lw["wq"], lw["wk"], lw["wv"], lw["wpool"], name="ab_prep_bwd")
    norm_ab = lw["norm_ab"] + notify("in_ab")
    dx, g["norm_ab"] = _matmul_rms_bwd(dproj, lw["win"], x, norm_ab, dh1, name="in_ab_dx_rms")
    return loss, dx, g


_HBM = pl.BlockSpec(memory_space=pltpu.HBM)


def _place():
    return lax.axis_index("x"), lax.axis_index("y"), lax.axis_index("c")


def _flip(v, f):
    return 1 - v if f else v


_CHIP_FLIPS = ((1, 0), (0, 1), (1, 1))
_DEV_FLIPS = tuple((fx, fy, fc) for fx in (0, 1) for fy in (0, 1) for fc in (0, 1) if fx or fy or fc)


def _rcopy(src, dst, send_sems, recv_sems, k, to):
    return pltpu.make_async_remote_copy(src_ref=src, dst_ref=dst, send_sem=send_sems.at[k], recv_sem=recv_sems.at[k],
                                        device_id=to, device_id_type=MESH)


def _my_half(ref, c, axis):
    rh = ref.shape[axis] // 2
    idx = [slice(None)] * len(ref.shape)
    idx[axis] = pl.ds(c * rh, rh)
    return ref.at[tuple(idx)]


def _gather_weights(bigs, smalls):
    nb, ns = len(bigs), len(smalls)

    def body(*refs):
        ins, outs = refs[:nb + ns], refs[nb + ns:2 * (nb + ns)]
        send_sems, recv_sems, local_sems = refs[2 * (nb + ns):]
        x, y, c = _place()
        j0 = 2 * x + y
        sib = (x, y, 1 - c)
        chips = [(_flip(x, fx), _flip(y, fy)) for fx, fy in _CHIP_FLIPS]
        local = [pltpu.make_async_copy(i_ref, o_ref.at[j0], local_sems.at[a])
                 for a, (i_ref, o_ref) in enumerate(zip(ins, outs))]
        for cp in local:
            cp.start()
        sends = []
        for k, (px, py) in enumerate(chips):
            for a in range(nb):
                sends.append(_rcopy(_my_half(ins[a], c, 0), _my_half(outs[a].at[j0], c, 0), send_sems, recv_sems,
                                    6 * a + k, (px, py, c)))
            for s in range(ns):
                sends.append(_rcopy(ins[nb + s], outs[nb + s].at[j0], send_sems, recv_sems, 6 * nb + 3 * s + k, (px, py, c)))
        for cp in sends:
            cp.start()
        for k, (px, py) in enumerate(chips):
            jk = 2 * px + py
            for a in range(nb):
                landed = _my_half(outs[a].at[jk], c, 0)
                _rcopy(landed, landed, send_sems, recv_sems, 6 * a + k, (px, py, c)).wait_recv()
                fwd = _rcopy(landed, landed, send_sems, recv_sems, 6 * a + 3 + k, sib)
                fwd.start()
                sends.append(fwd)
        for k, (px, py) in enumerate(chips):
            jk = 2 * px + py
            for a in range(nb):
                other = _my_half(outs[a].at[jk], 1 - c, 0)
                _rcopy(other, other, send_sems, recv_sems, 6 * a + 3 + k, sib).wait_recv()
            for s in range(ns):
                _rcopy(ins[nb + s], outs[nb + s].at[jk], send_sems, recv_sems, 6 * nb + 3 * s + k, (px, py, c)).wait_recv()
        for cp in sends:
            cp.wait_send()
        for cp in local:
            cp.wait()

    arrays = list(bigs) + list(smalls)
    n_sem = 6 * nb + 3 * ns
    return pl.pallas_call(
        body, name="gather_weights", in_specs=[_HBM] * len(arrays), out_specs=[_HBM] * len(arrays),
        out_shape=[jax.ShapeDtypeStruct((4,) + a.shape, a.dtype) for a in arrays],
        scratch_shapes=[pltpu.SemaphoreType.DMA((n_sem,)), pltpu.SemaphoreType.DMA((n_sem,)),
                        pltpu.SemaphoreType.DMA((len(arrays),))],
    )(*arrays)


def _core_swap_partial(gs, by_cols, *, name):
    n = len(gs)

    def body(*refs):
        ins, outs = refs[:n], refs[n:2 * n]
        send_sems, recv_sems = refs[2 * n:]
        x, y, c = _place()
        copies = [_rcopy(_my_half(i_ref, 1 - c, 2 if by_cols[a] else 1), o_ref, send_sems, recv_sems, a, (x, y, 1 - c))
                  for a, (i_ref, o_ref) in enumerate(zip(ins, outs))]
        for cp in copies:
            cp.start()
        for cp in copies:
            cp.wait()

    halved = lambda g, cols: (4, g.shape[1], g.shape[2] // 2) if cols else (4, g.shape[1] // 2, g.shape[2])
    return pl.pallas_call(
        body, name=name, in_specs=[_HBM] * n, out_specs=[_HBM] * n,
        out_shape=[jax.ShapeDtypeStruct(halved(g, cols), g.dtype) for g, cols in zip(gs, by_cols)],
        scratch_shapes=[pltpu.SemaphoreType.DMA((n,)), pltpu.SemaphoreType.DMA((n,))],
    )(*gs)


def _core_swap_partial_start(gs, by_cols, *, name):
    n = len(gs)
    halved = lambda g, cols: (4, g.shape[1], g.shape[2] // 2) if cols else (4, g.shape[1] // 2, g.shape[2])
    lands = [lax.empty(halved(g, cols), g.dtype) for g, cols in zip(gs, by_cols)]

    def body(*refs):
        ins, land_refs, send_sems, recv_sems, token = refs[:n], refs[n:2 * n], refs[2 * n], refs[2 * n + 1], refs[-1]
        x, y, c = _place()
        for a in range(n):
            _rcopy(_my_half(ins[a], 1 - c, 2 if by_cols[a] else 1), land_refs[a], send_sems, recv_sems, a,
                   (x, y, 1 - c)).start()
        token[...] = jnp.zeros_like(token)

    held = [pltpu.with_memory_space_constraint(a, pltpu.HBM) for a in list(gs) + lands]
    return pl.pallas_call(
        body, name=name, in_specs=[_HBM] * (2 * n),
        out_specs=(_SEM, _SEM, *[_HBM] * (2 * n), pl.BlockSpec(memory_space=pltpu.VMEM)),
        out_shape=(pltpu.SemaphoreType.DMA((n,)), pltpu.SemaphoreType.DMA((n,)),
                   *[pltpu.HBM(a.shape, a.dtype) for a in held], jax.ShapeDtypeStruct((8, LANES), F32)),
        input_output_aliases={i: 2 + i for i in range(2 * n)},
        compiler_params=pltpu.CompilerParams(has_side_effects=_DATAFLOW),
    )(*held)


def _core_swap_partial_wait(started, after, by_cols, *, name):
    send_sems, recv_sems, held = started[0], started[1], started[2:-1]
    n = len(held) // 2

    def body(*refs):
        ins, land_refs, s_sems, r_sems = refs[:n], refs[n:2 * n], refs[2 * n], refs[2 * n + 1]
        x, y, c = _place()
        for a in range(n):
            cp = _rcopy(_my_half(ins[a], 1 - c, 2 if by_cols[a] else 1), land_refs[a], s_sems, r_sems, a, (x, y, 1 - c))
            cp.wait_send()
            cp.wait_recv()

    out = pl.pallas_call(
        body, name=name, in_specs=[_HBM] * (2 * n) + [_SEM, _SEM, _ANY], out_specs=[_HBM] * (2 * n),
        out_shape=[pltpu.HBM(a.shape, a.dtype) for a in held],
        input_output_aliases={i: i for i in range(2 * n)},
        compiler_params=pltpu.CompilerParams(has_side_effects=_DATAFLOW),
    )(*held, send_sems, recv_sems, after)
    return out[:n], out[n:]


def _core_swap_sum(fs, by_cols):
    n = len(fs)

    def body(*refs):
        ins, outs = refs[:n], refs[n:2 * n]
        send_sems, recv_sems = refs[2 * n:]
        x, y, c = _place()
        axes = [1 if cols else 0 for cols in by_cols]
        copies = [_rcopy(_my_half(i_ref, c, ax), _my_half(o_ref, c, ax), send_sems, recv_sems, a, (x, y, 1 - c))
                  for a, (i_ref, o_ref, ax) in enumerate(zip(ins, outs, axes))]
        for cp in copies:
            cp.start()
        for a, cp in enumerate(copies):
            cp.wait_send()
            theirs = _my_half(outs[a], 1 - c, axes[a])
            _rcopy(theirs, theirs, send_sems, recv_sems, a, (x, y, 1 - c)).wait_recv()

    return pl.pallas_call(
        body, name="core_swap_sum", in_specs=[_HBM] * n, out_specs=[_HBM] * n,
        out_shape=[jax.ShapeDtypeStruct(f.shape, f.dtype) for f in fs],
        input_output_aliases={a: a for a in range(n)},
        scratch_shapes=[pltpu.SemaphoreType.DMA((n,)), pltpu.SemaphoreType.DMA((n,))],
    )(*fs)


_SEM = pl.BlockSpec(memory_space=pltpu.SEMAPHORE)
_ANY = pl.BlockSpec(memory_space=pl.ANY)
_DATAFLOW = pltpu.SideEffectType.DATAFLOW_SIDE_EFFECTING


def _peer_flips(to_all):
    return _DEV_FLIPS if to_all else tuple((fx, fy, 0) for fx, fy in _CHIP_FLIPS)


def _to_chips_copies(srcs, lands, send_sems, recv_sems, per_chip_slot, to_all=False):
    x, y, c = _place()
    flips = _peer_flips(to_all)
    index = (lambda px, py, pc: 4 * px + 2 * py + pc) if to_all else (lambda px, py, pc: 2 * px + py)
    me = index(x, y, c)
    out = []
    for k, (fx, fy, fc) in enumerate(flips):
        peer = (_flip(x, fx), _flip(y, fy), _flip(c, fc))
        theirs = index(*peer)
        for a, (src, land) in enumerate(zip(srcs, lands)):
            piece = src.at[theirs] if per_chip_slot else src
            out.append((_rcopy(piece, land.at[me], send_sems, recv_sems, len(flips) * a + k, peer),
                        _rcopy(piece, land.at[theirs], send_sems, recv_sems, len(flips) * a + k, peer)))
    return out


def _to_chips_start(arrays, *, per_chip_slot, name, after=None, to_all=False):
    n = len(arrays)
    peers = len(_peer_flips(to_all))
    lands = [lax.empty((peers + 1,) + (a.shape[1:] if per_chip_slot else a.shape), a.dtype) for a in arrays]
    extra = [] if after is None else [after]

    def body(*refs):
        srcs, land_refs, token = refs[:n], refs[n:2 * n], refs[-1]
        send_sems, recv_sems = refs[2 * n + len(extra)], refs[2 * n + len(extra) + 1]
        for send, _ in _to_chips_copies(srcs, land_refs, send_sems, recv_sems, per_chip_slot, to_all):
            send.start()
        token[...] = jnp.zeros_like(token)

    held = [pltpu.with_memory_space_constraint(a, pltpu.HBM) for a in list(arrays) + lands]
    return pl.pallas_call(
        body, name=name, in_specs=[_HBM] * (2 * n) + [_ANY] * len(extra),
        out_specs=(_SEM, _SEM, *[_HBM] * (2 * n), pl.BlockSpec(memory_space=pltpu.VMEM)),
        out_shape=(pltpu.SemaphoreType.DMA((peers * n,)), pltpu.SemaphoreType.DMA((peers * n,)),
                   *[pltpu.HBM(a.shape, a.dtype) for a in held], jax.ShapeDtypeStruct((8, LANES), F32)),
        input_output_aliases={i: 2 + i for i in range(2 * n)},
        compiler_params=pltpu.CompilerParams(has_side_effects=_DATAFLOW),
    )(*held, *extra)


def _to_chips_wait(started, after, *, per_chip_slot, name, to_all=False):
    send_sems, recv_sems, held = started[0], started[1], started[2:-1]
    n = len(held) // 2

    def body(*refs):
        srcs, land_refs, s_sems, r_sems = refs[:n], refs[n:2 * n], refs[2 * n], refs[2 * n + 1]
        for send, arrival in _to_chips_copies(srcs, land_refs, s_sems, r_sems, per_chip_slot, to_all):
            send.wait_send()
            arrival.wait_recv()

    out = pl.pallas_call(
        body, name=name, in_specs=[_HBM] * (2 * n) + [_SEM, _SEM, _ANY], out_specs=[_HBM] * (2 * n),
        out_shape=[pltpu.HBM(a.shape, a.dtype) for a in held],
        input_output_aliases={i: i for i in range(2 * n)},
        compiler_params=pltpu.CompilerParams(has_side_effects=_DATAFLOW),
    )(*held, send_sems, recv_sems, after)
    return out[n:]


def _chip_exchange(ps, small):
    n = len(ps)
    rs = small.shape[0]

    def body(*refs):
        p_refs, s_ref = refs[:n], refs[n]
        l_refs, ls_ref = refs[n + 1:2 * n + 1], refs[2 * n + 1]
        send_sems, recv_sems, local_sems = refs[2 * n + 2:]
        x, y, c = _place()
        j0 = 2 * x + y
        d0 = 2 * j0 + c
        local = [pltpu.make_async_copy(p.at[j0], l.at[j0], local_sems.at[a]) for a, (p, l) in enumerate(zip(p_refs, l_refs))]
        local.append(pltpu.make_async_copy(s_ref, ls_ref.at[d0], local_sems.at[n]))
        for cp in local:
            cp.start()
        sends = []
        for k, (fx, fy) in enumerate(_CHIP_FLIPS):
            px, py = _flip(x, fx), _flip(y, fy)
            for a in range(n):
                sends.append(_rcopy(p_refs[a].at[2 * px + py], l_refs[a].at[j0], send_sems, recv_sems, 3 * a + k, (px, py, c)))
        for k, (fx, fy, fc) in enumerate(_DEV_FLIPS):
            peer = (_flip(x, fx), _flip(y, fy), _flip(c, fc))
            sends.append(_rcopy(s_ref, ls_ref.at[d0], send_sems, recv_sems, 3 * n + k, peer))
        for cp in sends:
            cp.start()
        for k, (fx, fy) in enumerate(_CHIP_FLIPS):
            px, py = _flip(x, fx), _flip(y, fy)
            for a in range(n):
                _rcopy(p_refs[a].at[j0], l_refs[a].at[2 * px + py], send_sems, recv_sems, 3 * a + k, (px, py, c)).wait_recv()
        for k, (fx, fy, fc) in enumerate(_DEV_FLIPS):
            px, py, pc = _flip(x, fx), _flip(y, fy), _flip(c, fc)
            _rcopy(s_ref, ls_ref.at[4 * px + 2 * py + pc], send_sems, recv_sems, 3 * n + k, (px, py, pc)).wait_recv()
        for cp in sends:
            cp.wait_send()
        for cp in local:
            cp.wait()

    n_sem = 3 * n + 7
    return pl.pallas_call(
        body, name="chip_exchange", in_specs=[_HBM] * (n + 1), out_specs=[_HBM] * (n + 1),
        out_shape=[jax.ShapeDtypeStruct(p.shape, F32) for p in ps] + [jax.ShapeDtypeStruct((8, rs, LANES), F32)],
        scratch_shapes=[pltpu.SemaphoreType.DMA((n_sem,)), pltpu.SemaphoreType.DMA((n_sem,)),
                        pltpu.SemaphoreType.DMA((n + 1,))],
    )(*ps, small)


def _half_blocks(rows, cols, by_cols):
    if by_cols:
        tc = _tile(cols // 2, 256)
        nb = cols // 2 // tc
        return rows, tc, nb, (lambda i, c: (0, c * nb + i))
    tr = _tile(rows // 2, 256)
    nb = rows // 2 // tr
    return tr, cols, nb, (lambda i, c: (c * nb + i, 0))


def _core_sum(g, part, core, *, name, by_cols):
    _, rows, cols = g.shape
    br, bc, nb, whole = _half_blocks(rows, cols, by_cols)
    mine = (lambda i: (0, i)) if by_cols else (lambda i: (i, 0))

    def body(c_ref, g_ref, p_ref, o_ref):
        o_ref[...] = g_ref[...] + p_ref[...]

    grid_spec = pltpu.PrefetchScalarGridSpec(
        num_scalar_prefetch=1, grid=(4, nb),
        in_specs=[pl.BlockSpec((1, br, bc), lambda j, i, c: (j,) + whole(i, c[0])),
                  pl.BlockSpec((1, br, bc), lambda j, i, c: (j,) + mine(i))],
        out_specs=pl.BlockSpec((1, br, bc), lambda j, i, c: (j,) + mine(i)),
    )
    return pl.pallas_call(
        body, name=name, grid_spec=grid_spec, out_shape=jax.ShapeDtypeStruct(part.shape, F32),
        compiler_params=_params(("parallel", "parallel")),
    )(core, g, part)


def _chip_sum(landed, part, place, *, name, by_cols):
    _, hr, hc = landed.shape
    rows, cols = (hr, 2 * hc) if by_cols else (2 * hr, hc)
    br, bc, nb, whole = _half_blocks(rows, cols, by_cols)
    mine = (lambda i: (0, i)) if by_cols else (lambda i: (i, 0))

    def body(c_ref, own_ref, a_ref, b_ref, d_ref, o_ref):
        o_ref[...] = ((own_ref[0] + a_ref[0]) + b_ref[0]) + d_ref[0]

    slot = lambda k: pl.BlockSpec((1, br, bc), lambda i, c: (jnp.bitwise_xor(c[1], k),) + mine(i))
    grid_spec = pltpu.PrefetchScalarGridSpec(
        num_scalar_prefetch=1, grid=(nb,),
        in_specs=[slot(0), slot(1), slot(2), slot(3)],
        out_specs=pl.BlockSpec((br, bc), lambda i, c: whole(i, c[0])),
    )
    return pl.pallas_call(
        body, name=name, grid_spec=grid_spec, out_shape=jax.ShapeDtypeStruct((rows, cols), F32),
        compiler_params=_params(("parallel",)),
    )(place, part, landed, landed, landed)


_ROW_POOL_W, _ROW_NORM_AB, _ROW_FINAL, _ROW_POOL_SCALE, _ROW_Q_NORM = 0, 512, 520, 528, 532
_ROW_KV_NORM, _ROW_O_NORM, _ROW_A_LOG, _ROW_DT_BIAS, _ROW_LOSS = 534, 535, 536, 537, 538
_ROW_CONV, _ROW_NORM_C, _SMALL_ROWS = 544, 640, 672
_CONV_ROWS = CONV_WIDTH * 6


def _put_rows(dst_ref, row0, src, width):
    for r in range(width // LANES):
        dst_ref[row0 + r:row0 + r + 1, :] = src[:, r * LANES:(r + 1) * LANES]


def _pack_small(g, loss_tile):
    names = ("wpool", "norm_ab", "final_norm", "pool_scale", "q_a_norm", "kv_a_norm", "o_norm", "a_log", "dt_bias",
             "conv_w", "norm_c")

    def body(wpool, norm_ab, final_norm, pool_scale, q_norm, kv_norm, o_norm, a_log, dt_bias, conv_w, norm_c, loss, o_ref):
        o_ref[...] = jnp.zeros_like(o_ref)
        for gi in range(4):
            o_ref[_ROW_POOL_W + gi * 128:_ROW_POOL_W + (gi + 1) * 128, :] = wpool[gi * 128:(gi + 1) * 128, gi * 128:(gi + 1) * 128]
        _put_rows(o_ref, _ROW_NORM_AB, norm_ab[...], 1024)
        _put_rows(o_ref, _ROW_FINAL, final_norm[...], 1024)
        _put_rows(o_ref, _ROW_POOL_SCALE, pool_scale[...], 512)
        _put_rows(o_ref, _ROW_Q_NORM, q_norm[...], 256)
        for row, ref in ((_ROW_KV_NORM, kv_norm), (_ROW_O_NORM, o_norm), (_ROW_A_LOG, a_log), (_ROW_DT_BIAS, dt_bias)):
            o_ref[row:row + 1, :] = ref[...]
        o_ref[_ROW_LOSS:_ROW_LOSS + 1, :] = loss[0:1, :]
        for j in range(4):
            for r in range(CONV_WIDTH):
                _put_rows(o_ref, _ROW_CONV + j * _CONV_ROWS + r * 6, conv_w[r:r + 1, j * 768:(j + 1) * 768], 768)
            _put_rows(o_ref, _ROW_NORM_C + j * 8, norm_c[:, j * 256:(j + 1) * 256], 256)

    vmem = pl.BlockSpec(memory_space=pltpu.VMEM)
    return pl.pallas_call(
        body, name="pack_small", in_specs=[vmem] * 12, out_specs=vmem,
        out_shape=jax.ShapeDtypeStruct((_SMALL_ROWS, LANES), F32),
    )(*[g[n] for n in names], loss_tile)


_SMALL_NAMES = ("pool_w", "norm_ab", "final_norm", "pool_scale", "q_a_norm", "kv_a_norm", "o_norm", "a_log", "dt_bias",
                "conv_w", "norm_c")


def _take_rows(src, row0, width):
    return jnp.concatenate([src[row0 + r:row0 + r + 1, :] for r in range(width // LANES)], axis=1)


def _small_update(small_all, late_all, ws, ms, vs):
    n = len(_SMALL_NAMES)

    def body(*refs):
        a_ref, late_ref = refs[0], refs[1]
        refs = refs[1:]
        w_refs, m_refs, v_refs = refs[1:1 + n], refs[1 + n:1 + 2 * n], refs[1 + 2 * n:1 + 3 * n]
        outs = refs[1 + 3 * n:1 + 7 * n]
        loss_ref, tot = refs[1 + 7 * n], refs[2 + 7 * n]
        acc, late = a_ref[0], late_ref[0]
        for d in range(1, 8):
            acc = acc + a_ref[d]
            late = late + late_ref[d]
        tot[...] = acc
        x, y, _ = _place()
        j0 = 2 * x + y
        conv = tot[pl.ds(pl.multiple_of(_ROW_CONV + j0 * _CONV_ROWS, 8), _CONV_ROWS), :]
        norm_c = tot[pl.ds(pl.multiple_of(_ROW_NORM_C + j0 * 8, 8), 8), :]
        whole = tot[_ROW_NORM_AB:_ROW_CONV, :]
        at = lambda row: row - _ROW_NORM_AB
        grads = {
            "norm_ab": _take_rows(late, 0, 1024), "final_norm": _take_rows(whole, at(_ROW_FINAL), 1024),
            "pool_scale": _take_rows(whole, at(_ROW_POOL_SCALE), 512), "q_a_norm": _take_rows(whole, at(_ROW_Q_NORM), 256),
            "kv_a_norm": whole[at(_ROW_KV_NORM):at(_ROW_KV_NORM) + 1, :], "o_norm": whole[at(_ROW_O_NORM):at(_ROW_O_NORM) + 1, :],
            "a_log": tot[_ROW_A_LOG:_ROW_A_LOG + 1, 0:GDN_HEADS],
            "dt_bias": tot[_ROW_DT_BIAS:_ROW_DT_BIAS + 1, 0:GDN_HEADS],
            "norm_c": _take_rows(norm_c, 0, 256),
        }
        loss_ref[...] = whole[at(_ROW_LOSS):at(_ROW_LOSS) + 1, :]
        for i, name in enumerate(_SMALL_NAMES):
            g_out = outs[4 * i]
            if name == "pool_w":
                for gi in range(4):
                    g_out[gi] = tot[_ROW_POOL_W + gi * 128:_ROW_POOL_W + (gi + 1) * 128, :]
            elif name == "conv_w":
                for r in range(CONV_WIDTH):
                    g_out[r:r + 1, :] = _take_rows(conv, r * 6, 768)
            else:
                g_out[...] = grads[name]
            _adam_update(g_out, w_refs[i], m_refs[i], v_refs[i], *outs[4 * i + 1:4 * i + 4])

    vmem = pl.BlockSpec(memory_space=pltpu.VMEM)
    out_shape = [jax.ShapeDtypeStruct(w.shape, F32) for w in ws for _ in range(4)] + [jax.ShapeDtypeStruct((1, LANES), F32)]
    return pl.pallas_call(
        body, name="small_update", in_specs=[vmem] * (2 + 3 * n), out_specs=[vmem] * (4 * n + 1), out_shape=out_shape,
        scratch_shapes=[pltpu.VMEM((_SMALL_ROWS, LANES), F32)],
        compiler_params=pltpu.CompilerParams(vmem_limit_bytes=VMEM_LIMIT),
    )(small_all, late_all, *ws, *ms, *vs)


def _adam_update(g_ref, w_ref, m_ref, v_ref, d_ref, mo_ref, vo_ref):
    gv = g_ref[...]
    mn = ADAM_B1 * m_ref[...] + (1.0 - ADAM_B1) * gv
    vn = ADAM_B2 * v_ref[...] + (1.0 - ADAM_B2) * (gv * gv)
    mo_ref[...] = mn
    vo_ref[...] = vn
    c1 = 1.0 - ADAM_B1 ** ADAM_STEP
    c2 = 1.0 - ADAM_B2 ** ADAM_STEP
    d_ref[...] = -ADAM_LR * ((mn / c1) / (jnp.sqrt(vn / c2) + ADAM_EPS) + ADAM_WD * w_ref[...])


def _adamw_rows(g, w, m, v, *, name):
    rows, cols = g.shape
    if rows % LANES == 0:
        tr = _tile(rows, 512)
        blk, steps = pl.BlockSpec((tr, cols), lambda i: (i, 0)), rows // tr
    else:
        tc = _tile(cols, 256)
        blk, steps = pl.BlockSpec((rows, tc), lambda i: (0, i)), cols // tc

    def body(*refs):
        _adam_update(*refs)

    out = jax.ShapeDtypeStruct((rows, cols), F32)
    return pl.pallas_call(
        body, name=name, grid=(steps,), in_specs=[blk] * 4, out_specs=[blk] * 3, out_shape=[out] * 3,
        compiler_params=_params(("parallel",)),
    )(g, w, m, v)


_ADAM_ROWWISE = ("w_in_ab", "w_q_b", "w_kv_b", "w_out_ab", "w_in_c", "w_out_c")


_SHARD_AXIS = {"w_in_ab": 1, "w_q_b": 1, "w_kv_b": 1, "w_out_ab": 0, "w_in_c": 1, "w_out_c": 0, "conv_w": 1, "norm_c": 1}
_ALL_NAMES = ("norm_ab", "w_in_ab", "q_a_norm", "w_q_b", "kv_a_norm", "w_kv_b", "pool_w", "pool_scale", "w_out_ab",
              "norm_c", "w_in_c", "conv_w", "a_log", "dt_bias", "o_norm", "w_out_c", "final_norm")


def _join_shards(a, axis):
    _, r, c = a.shape
    return a.reshape(4 * r, c) if axis == 0 else jnp.transpose(a, (1, 0, 2)).reshape(r, 4 * c)


def _split_shards(a, axis):
    r, c = a.shape
    return a.reshape(4, r // 4, c) if axis == 0 else jnp.transpose(a.reshape(r, 4, c // 4), (1, 0, 2))


def kernel(x, positions, norm_ab, w_in_ab, q_a_norm, w_q_b, kv_a_norm, w_kv_b, pool_w, pool_scale, w_out_ab, norm_c, w_in_c, conv_w, a_log, dt_bias, o_norm, w_out_c, final_norm, loss_target, m_norm_ab, m_w_in_ab, m_q_a_norm, m_w_q_b, m_kv_a_norm, m_w_kv_b, m_pool_w, m_pool_scale, m_w_out_ab, m_norm_c, m_w_in_c, m_conv_w, m_a_log, m_dt_bias, m_o_norm, m_w_out_c, m_final_norm, v_norm_ab, v_w_in_ab, v_q_a_norm, v_w_q_b, v_kv_a_norm, v_w_kv_b, v_pool_w, v_pool_scale, v_w_out_ab, v_norm_c, v_w_in_c, v_conv_w, v_a_log, v_dt_bias, v_o_norm, v_w_out_c, v_final_norm):
    given = dict(locals())
    c = lax.axis_index("c")
    t = x.shape[1]

    def shard_of(prefix, name):
        a = given[prefix + name]
        return a.reshape(a.shape[1:]) if a.ndim > 2 else a.reshape(1, -1)

    big, big_even, big_odd, small_sharded = _ADAM_ROWWISE, _ADAM_ROWWISE[:4], _ADAM_ROWWISE[4:], ("conv_w", "norm_c")
    chip = 2 * lax.axis_index("x") + lax.axis_index("y")
    core = c.astype(jnp.int32).reshape(1)
    place = jnp.stack([c, chip]).astype(jnp.int32)
    later = {"mid": big_even[1:], "odd": big_odd + small_sharded}
    travelling = {}

    def send(tag, after=None):
        shards = [shard_of("", n).astype(BF16) if n in big else shard_of("", n) for n in later[tag]]
        started = _to_chips_start(shards, per_chip_slot=False, name="gather_" + tag + "_start", after=after)
        travelling[tag] = (shards, started)
        return started[-1][0, 0]

    mid_sent = send("mid")
    gathered = _gather_weights([shard_of("", "w_in_ab").astype(BF16)], [])
    full = {"w_in_ab": _join_shards(gathered[0], _SHARD_AXIS["w_in_ab"])}
    for name in ("norm_ab", "q_a_norm", "kv_a_norm", "pool_w", "pool_scale"):
        full[name] = shard_of("", name)
    lw = _layout_in_ab(full)
    lw["norm_ab"] = lw["norm_ab"] + mid_sent

    def more_weights(tag, after):
        shards, started = travelling[tag]
        landed = _to_chips_wait(started, after, per_chip_slot=False, name="gather_" + tag + "_wait")
        w = {}
        for name, land, own in zip(later[tag], landed, shards):
            w[name] = _join_shards(lax.dynamic_update_index_in_dim(land, own, chip, 0), _SHARD_AXIS[name])
        if tag == "mid":
            out = _layout_mid(w)
            out["wq"] = out["wq"] + send("odd", after=landed[0]).astype(BF16)
            return out
        for name in ("a_log", "dt_bias", "o_norm", "final_norm"):
            w[name] = shard_of("", name)
        return _layout_odd(w)

    transposed = ("w_in_ab", "w_in_c")

    def chip_slots(names, g):
        grads = _unlayout_grads(g, names)
        return ([_split_shards(grads[n], 0 if n in transposed else _SHARD_AXIS[n]) for n in names],
                [n in transposed for n in names])

    def chip_partials(names, slots, partial, by_cols):
        return [_core_sum(s, p, core, name="core_sum_" + n, by_cols=b) for n, s, p, b in zip(names, slots, partial, by_cols)]

    groups = {"odd": big_odd, "out_ab": ("w_out_ab",), "in_ab": ("w_in_ab", "w_q_b", "w_kv_b")}
    sent, swapping = {}, {}

    def on_grads(tag, g, after):
        if tag == "odd":
            slots, by_cols = chip_slots(groups[tag], g)
            swapping[tag] = (slots, by_cols, _core_swap_partial_start(slots, by_cols, name="core_swap_partial_odd_start"))
            return swapping[tag][2][-1][0, 0]
        if tag == "odd_go":
            tag = "odd"
            _, by_cols, started = swapping[tag]
            slots, partial = _core_swap_partial_wait(started, after, by_cols, name="core_swap_partial_odd_wait")
        else:
            slots, by_cols = chip_slots(groups[tag], g)
            partial = _core_swap_partial(slots, by_cols, name="core_swap_partial_" + tag)
        part = chip_partials(groups[tag], slots, partial, by_cols)
        sent[tag] = (part, _to_chips_start(part, per_chip_slot=True, name="exchange_" + tag + "_start"))
        token = sent[tag][1][-1][0, 0]
        if tag == "in_ab":
            pack = _pack_small({**g, "norm_ab": jnp.zeros((1, 1024), F32)}, g["loss"])
            sent["small"] = (pack, _to_chips_start([pack], per_chip_slot=False, to_all=True, name="exchange_small_start"))
            token = token + sent["small"][1][-1][0, 0]
        return token

    loss_tile, dx, g = _local_step(x[0], positions.reshape(t, 1), loss_target[0], lw, more_weights, on_grads)
    late_all = _chip_exchange([], g["norm_ab"].reshape(8, LANES))[-1]
    pack, started = sent.pop("small")
    landed = _to_chips_wait(started, late_all, per_chip_slot=False, to_all=True, name="exchange_small_wait")[0]
    small_all = lax.dynamic_update_index_in_dim(landed, pack, 2 * chip + c, 0)
    halves = {}
    for tag, names in groups.items():
        part, started = sent[tag]
        landed = _to_chips_wait(started, late_all, per_chip_slot=True, name="exchange_" + tag + "_wait")
        for n, l, p in zip(names, landed, part):
            halves[n] = _chip_sum(l, p, place, name="chip_sum_" + n, by_cols=n in transposed)
    gbig = dict(zip(big, _core_swap_sum([halves[n] for n in big], [n in transposed for n in big])))

    res = {}
    for name in big:
        operands = [gbig[name], shard_of("", name), shard_of("m_", name), shard_of("v_", name)]
        flip = name in transposed
        if flip:
            operands[1:] = [jnp.transpose(a) for a in operands[1:]]
        out = (operands[0],) + tuple(_adamw_rows(*operands, name="adamw_" + name))
        out = [jnp.transpose(a) for a in out] if flip else out
        res["grad", name], res["delta", name], res["m", name], res["v", name] = out
    out = _small_update(small_all, late_all, [shard_of("", n) for n in _SMALL_NAMES], [shard_of("m_", n) for n in _SMALL_NAMES],
                        [shard_of("v_", n) for n in _SMALL_NAMES])
    for i, name in enumerate(_SMALL_NAMES):
        res["grad", name], res["delta", name], res["m", name], res["v", name] = out[4 * i:4 * i + 4]
    res = {k: a.reshape(given[k[1]].shape) for k, a in res.items()}
    loss = out[-1][0, 0]
    outs = [loss, dx.reshape(x.shape)]
    for key in ("grad", "delta", "m", "v"):
        outs += [res[key, n] for n in _ALL_NAMES]
    return tuple(outs)
```

```python
import functools

import jax
import jax.numpy as jnp
from jax import lax
from jax.experimental import pallas as pl
from jax.experimental.pallas import tpu as pltpu

F32 = jnp.float32
BF16 = jnp.bfloat16
HI = lax.Precision.HIGHEST
MESH = pl.DeviceIdType.MESH

RMS_EPS = 1e-6
MLA_HEADS = 8
MLA_Q_RANK = 256
MLA_KV_RANK = 128
MLA_NOPE = 64
MLA_ROPE = 32
MLA_V = 64
ROPE_THETA = 10000.0
POOL_WINDOWS = (2, 4, 8, 16)
POOL_GROUP = 128
POOL_WIDTH = 512
POOL_HALO = 16
GDN_HEADS = 8
GDN_DK = 128
CONV_WIDTH = 4
CONV_HALO = 8
CHUNK = 64
IN_AB_PAD = 2048
IN_C_PAD = 4224
ATT_SCALE = (MLA_NOPE + MLA_ROPE) ** -0.5
LOG2E = 1.4426950408889634

ADAM_LR = 0.001
ADAM_B1 = 0.9
ADAM_B2 = 0.999
ADAM_EPS = 1e-08
ADAM_WD = 0.01
ADAM_STEP = 10

LANES = 128
VMEM_LIMIT = 56 * 1024 * 1024

ROW_TILE = 256
ATT_TILE = 1024
GDN_TILE = 256
MM_TILE = (1408, 1408, 2048)

NN = (((1,), (0,)), ((), ()))
NT = (((1,), (1,)), ((), ()))
TN = (((0,), (0,)), ((), ()))


def _dot(a, b, dims=NN, prec=None):
    return lax.dot_general(a, b, dims, precision=prec, preferred_element_type=F32)


def _tile(n, pref):
    if n <= pref:
        return n
    step = LANES if pref >= LANES else 8
    for t in range(pref - pref % step, 0, -step):
        if n % t == 0:
            return t
    return n


def _params(sem):
    return pltpu.CompilerParams(dimension_semantics=sem, vmem_limit_bytes=VMEM_LIMIT)


def _sigmoid(x):
    return 0.5 * jnp.tanh(0.5 * x) + 0.5


def _softplus(x):
    return jnp.maximum(x, 0.0) + jnp.log(1.0 + jnp.exp(-jnp.abs(x)))


def _matmul(a, b, mode, *, name):
    if mode == "nn":
        (m, k), (k2, n) = a.shape, b.shape
    elif mode == "nt":
        (m, k), (n, k2) = a.shape, b.shape
    else:
        (k, m), (k2, n) = a.shape, b.shape
    assert k == k2, (a.shape, b.shape, mode)
    tm, tn, tk = _tile(m, MM_TILE[0]), _tile(n, MM_TILE[1]), _tile(k, MM_TILE[2])
    nk = k // tk
    if mode == "tn":
        a_spec = pl.BlockSpec((tk, tm), lambda i, j, kk: (kk, i))
    else:
        a_spec = pl.BlockSpec((tm, tk), lambda i, j, kk: (i, kk))
    if mode == "nt":
        b_spec = pl.BlockSpec((tn, tk), lambda i, j, kk: (j, kk))
    else:
        b_spec = pl.BlockSpec((tk, tn), lambda i, j, kk: (kk, j))
    o_spec = pl.BlockSpec((tm, tn), lambda i, j, kk: (i, j))
    dims = {"nn": NN, "nt": NT, "tn": TN}[mode]

    def body(a_ref, b_ref, o_ref, *scratch):
        if nk == 1:
            o_ref[...] = _dot(a_ref[...], b_ref[...], dims)
            return
        acc = scratch[0]
        kk = pl.program_id(2)

        @pl.when(kk == 0)
        def _():
            acc[...] = jnp.zeros_like(acc)

        acc[...] += _dot(a_ref[...], b_ref[...], dims)

        @pl.when(kk == nk - 1)
        def _():
            o_ref[...] = acc[...]

    return pl.pallas_call(
        body, name=name, grid=(m // tm, n // tn, nk), in_specs=[a_spec, b_spec], out_specs=o_spec,
        out_shape=jax.ShapeDtypeStruct((m, n), F32),
        scratch_shapes=[pltpu.VMEM((tm, tn), F32)] if nk > 1 else [],
        compiler_params=_params(("parallel", "parallel", "arbitrary")),
    )(a, b)


def _rms_in_proj(h, g, w, *, name):
    t, d = h.shape
    n = w.shape[1]
    tm, tn = _tile(t, MM_TILE[0]), _tile(n, MM_TILE[1])

    def body(h_ref, g_ref, w_ref, o_ref, hn_ref):
        @pl.when(pl.program_id(1) == 0)
        def _():
            x = h_ref[...]
            r = lax.rsqrt(jnp.mean(x * x, axis=-1, keepdims=True) + RMS_EPS)
            hn_ref[...] = (x * r * g_ref[...]).astype(BF16)

        o_ref[...] = _dot(hn_ref[...], w_ref[...])

    return pl.pallas_call(
        body, name=name, grid=(t // tm, n // tn),
        in_specs=[pl.BlockSpec((tm, d), lambda i, j: (i, 0)), pl.BlockSpec((1, d), lambda i, j: (0, 0)),
                  pl.BlockSpec((d, tn), lambda i, j: (0, j))],
        out_specs=[pl.BlockSpec((tm, tn), lambda i, j: (i, j)), pl.BlockSpec((tm, d), lambda i, j: (i, 0))],
        out_shape=[jax.ShapeDtypeStruct((t, n), F32), jax.ShapeDtypeStruct((t, d), BF16)],
        compiler_params=_params(("parallel", "arbitrary")),
    )(h, g, w)


def _matmul_rms_bwd(dproj, w, h, g, dres, *, name, prev_y=None):
    t, k = dproj.shape
    d = w.shape[0]
    tm = _tile(t, 2 * ROW_TILE)
    chained = prev_y is not None

    def body(dp_ref, w_ref, h_ref, g_ref, dres_ref, *rest):
        i = pl.program_id(0)
        dh_ref, dg_ref = rest[-4:-2] if chained else rest
        dyv = _dot(dp_ref[...], w_ref[...], NT)
        x = h_ref[...]
        r = lax.rsqrt(jnp.mean(x * x, axis=-1, keepdims=True) + RMS_EPS)
        xh = x * r
        dxh = dyv * g_ref[...]
        dh = dres_ref[...] + r * (dxh - xh * jnp.mean(dxh * xh, axis=-1, keepdims=True))
        dh_ref[...] = dh

        @pl.when(i == 0)
        def _():
            dg_ref[...] = jnp.zeros_like(dg_ref)
            if chained:
                rest[-1][...] = jnp.zeros_like(rest[-1])

        dg_ref[...] += jnp.sum(dyv * xh, axis=0, keepdims=True)
        if chained:
            y_ref, dhb_ref, dw_ref = rest[0], rest[-2], rest[-1]
            dhb_ref[...] = dh.astype(BF16)
            dw_ref[...] += _dot(y_ref[...], dhb_ref[...], TN)

    row = pl.BlockSpec((tm, d), lambda i: (i, 0))
    vec = pl.BlockSpec((1, d), lambda i: (0, 0))
    in_specs = [pl.BlockSpec((tm, k), lambda i: (i, 0)), pl.BlockSpec((d, k), lambda i: (0, 0)), row, vec, row]
    out_specs, out_shape = [row, vec], [jax.ShapeDtypeStruct((t, d), F32), jax.ShapeDtypeStruct((1, d), F32)]
    args = [dproj, w, h, g, dres]
    if chained:
        in_specs.append(row)
        args.append(prev_y)
        out_specs += [row, pl.BlockSpec((d, d), lambda i: (0, 0))]
        out_shape += [jax.ShapeDtypeStruct((t, d), BF16), jax.ShapeDtypeStruct((d, d), F32)]
    return pl.pallas_call(
        body, name=name, grid=(t // tm,), in_specs=in_specs, out_specs=out_specs, out_shape=out_shape,
        compiler_params=_params(("arbitrary",)),
    )(*args)


def _rope_partner(x):
    lane = lax.broadcasted_iota(jnp.int32, x.shape, 1)
    swapped = jnp.where(lane < MLA_NOPE + MLA_ROPE // 2, pltpu.roll(x, LANES - 16, 1), pltpu.roll(x, 16, 1))
    return jnp.where((lane >= MLA_NOPE) & (lane < MLA_NOPE + MLA_ROPE), swapped, 0.0)


def _pool_counts(row0, tm, w):
    t_idx = row0 + lax.broadcasted_iota(jnp.int32, (tm, POOL_GROUP), 0)
    return jnp.minimum(t_idx + 1, w).astype(F32)


def _ab_prep(proj, pos, inv_freq, q_a_norm, kv_a_norm, wq, wk, wv, wpool, *, name):
    t = proj.shape[0]
    tm = _tile(t, ROW_TILE)
    hb = tm // POOL_HALO

    def body(p_ref, halo_ref, pos_ref, inv_ref, qg_ref, kg_ref, wq_ref, wk_ref, wv_ref, wp_ref,
             q_ref, k_ref, v_ref, yb_ref, qn_ref, kvn_ref, d_ref, cos_ref, sin_ref, ext):
        i = pl.program_id(0)
        ql = p_ref[:, 0:MLA_Q_RANK]
        r = lax.rsqrt(jnp.mean(ql * ql, axis=-1, keepdims=True) + RMS_EPS)
        qn = (ql * r * qg_ref[...]).astype(BF16)
        qn_ref[...] = qn
        kl = p_ref[:, MLA_Q_RANK:MLA_Q_RANK + MLA_KV_RANK]
        r = lax.rsqrt(jnp.mean(kl * kl, axis=-1, keepdims=True) + RMS_EPS)
        kvn = (kl * r * kg_ref[...]).astype(BF16)
        kvn_ref[...] = kvn
        ang = pos_ref[...].astype(F32) * inv_ref[...]
        lane = lax.broadcasted_iota(jnp.int32, (tm, LANES), 1)
        in_rope = (lane >= MLA_NOPE) & (lane < MLA_NOPE + MLA_ROPE)
        cos_t = jnp.where(in_rope, jnp.cos(ang), 1.0)
        sin_t = jnp.where(in_rope, jnp.sin(ang), 0.0)
        sin_t = jnp.where(lane < MLA_NOPE + MLA_ROPE // 2, -sin_t, sin_t)
        cos_ref[...] = cos_t
        sin_ref[...] = sin_t
        kr = p_ref[:, 384:512]
        kr = kr * cos_t + _rope_partner(kr) * sin_t
        qraw = _dot(qn, wq_ref[...])
        kvk = _dot(kvn, wk_ref[...])
        for h in range(MLA_HEADS):
            sl = slice(h * LANES, (h + 1) * LANES)
            qh = qraw[:, sl]
            q_ref[:, sl] = ((qh * cos_t + _rope_partner(qh) * sin_t) * (ATT_SCALE * LOG2E)).astype(BF16)
            k_ref[:, sl] = (kvk[:, sl] + kr).astype(BF16)
        v_ref[...] = _dot(kvn, wv_ref[...]).astype(BF16)
        xp = p_ref[:, 512:1024]
        ext[0:POOL_HALO, :] = jnp.where(i > 0, halo_ref[...], 0.0)
        ext[POOL_HALO:POOL_HALO + tm, :] = xp
        for g, w in enumerate(POOL_WINDOWS):
            lo = g * POOL_GROUP
            acc = ext[POOL_HALO:POOL_HALO + tm, lo:lo + POOL_GROUP]
            for s in range(1, w):
                acc = acc + ext[POOL_HALO - s:POOL_HALO - s + tm, lo:lo + POOL_GROUP]
            cnt = _pool_counts(i * tm, tm, w)
            d_ref[:, lo:lo + POOL_GROUP] = (acc / cnt - xp[:, lo:lo + POOL_GROUP]).astype(BF16)
        yb_ref[...] = _dot(d_ref[...], wp_ref[...])

    row = lambda w: pl.BlockSpec((tm, w), lambda i: (i, 0))
    vec = lambda w: pl.BlockSpec((1, w), lambda i: (0, 0))
    whole = lambda a: pl.BlockSpec(a.shape, lambda i: (0, 0))
    return pl.pallas_call(
        body, name=name, grid=(t // tm,),
        in_specs=[row(1024), pl.BlockSpec((POOL_HALO, POOL_WIDTH), lambda i: (jnp.maximum(i * hb - 1, 0), 1)),
                  pl.BlockSpec((tm, 1), lambda i: (i, 0)), vec(LANES), vec(MLA_Q_RANK), vec(MLA_KV_RANK),
                  whole(wq), whole(wk), whole(wv), whole(wpool)],
        out_specs=[row(1024), row(1024), row(512), row(512), row(MLA_Q_RANK), row(MLA_KV_RANK), row(POOL_WIDTH),
                   row(LANES), row(LANES)],
        out_shape=[jax.ShapeDtypeStruct((t, 1024), BF16), jax.ShapeDtypeStruct((t, 1024), BF16),
                   jax.ShapeDtypeStruct((t, 512), BF16), jax.ShapeDtypeStruct((t, 512), F32),
                   jax.ShapeDtypeStruct((t, MLA_Q_RANK), BF16), jax.ShapeDtypeStruct((t, MLA_KV_RANK), BF16),
                   jax.ShapeDtypeStruct((t, POOL_WIDTH), BF16), jax.ShapeDtypeStruct((t, LANES), F32),
                   jax.ShapeDtypeStruct((t, LANES), F32)],
        scratch_shapes=[pltpu.VMEM((tm + POOL_HALO, POOL_WIDTH), F32)],
        compiler_params=_params(("parallel",)),
    )(proj, proj, pos, inv_freq, q_a_norm, kv_a_norm, wq, wk, wv, wpool)


def _ab_prep_bwd(proj, q_a_norm, kv_a_norm, dq, dk, dv, cos_t, sin_t, dyb, dz, qn, kvn, d, hn, wq, wk, wv, wpool, *, name):
    t = proj.shape[0]
    tm = _tile(t, ROW_TILE)
    hb = tm // POOL_HALO
    last_halo = t // POOL_HALO - 1
    nt = t // tm

    def body(p_ref, qg_ref, kg_ref, dq_ref, dk_ref, dv_ref, c_ref, s_ref, dyb_ref, dybn_ref, dz_ref,
             qn_ref, kvn_ref, d_ref, hn_ref, wq_ref, wk_ref, wv_ref, wp_ref,
             dp_ref, dqg_ref, dkg_ref, dwq_ref, dwk_ref, dwv_ref, dwp_ref, dwin_ref, ext, dqr_ref, dkb_ref):
        i = pl.program_id(0)

        @pl.when(i == 0)
        def _():
            for ref in (dqg_ref, dkg_ref, dwq_ref, dwk_ref, dwv_ref, dwp_ref, dwin_ref):
                ref[...] = jnp.zeros_like(ref)

        def norm_bwd(x, g, dy, dg_ref):
            r = lax.rsqrt(jnp.mean(x * x, axis=-1, keepdims=True) + RMS_EPS)
            xh = x * r
            dxh = dy * g
            dg_ref[...] += jnp.sum(dy * xh, axis=0, keepdims=True)
            return r * (dxh - xh * jnp.mean(dxh * xh, axis=-1, keepdims=True))

        c, s = c_ref[...], s_ref[...]
        lane = lax.broadcasted_iota(jnp.int32, (tm, LANES), 1)
        in_rope = (lane >= MLA_NOPE) & (lane < MLA_NOPE + MLA_ROPE)
        dkr = jnp.zeros((tm, LANES), F32)
        for h in range(MLA_HEADS):
            sl = slice(h * LANES, (h + 1) * LANES)
            g = dq_ref[:, sl]
            dqr_ref[:, sl] = ((g * c + _rope_partner(g * s)) * ATT_SCALE).astype(BF16)
            gk = dk_ref[:, sl]
            dkb_ref[:, sl] = gk.astype(BF16)
            dkr = dkr + jnp.where(in_rope, gk, 0.0)
        dkr = dkr * c + _rope_partner(dkr * s)
        dqn = _dot(dqr_ref[...], wq_ref[...], NT)
        dkvn = _dot(dkb_ref[...], wk_ref[...], NT) + _dot(dv_ref[...], wv_ref[...], NT)
        dql = norm_bwd(p_ref[:, 0:MLA_Q_RANK], qg_ref[...], dqn, dqg_ref)
        dp_ref[:, 0:MLA_Q_RANK] = dql.astype(BF16)
        dkl = norm_bwd(p_ref[:, MLA_Q_RANK:384], kg_ref[...], dkvn, dkg_ref)
        dp_ref[:, MLA_Q_RANK:384] = dkl.astype(BF16)
        dp_ref[:, 384:512] = dkr.astype(BF16)
        ddv = _dot(dyb_ref[...], wp_ref[...], NT)
        ddn = _dot(dybn_ref[...], wp_ref[...], NT)
        for g, w in enumerate(POOL_WINDOWS):
            lo = g * POOL_GROUP
            ext[0:tm, lo:lo + POOL_GROUP] = ddv[:, lo:lo + POOL_GROUP] / _pool_counts(i * tm, tm, w)
            nxt = ddn[:, lo:lo + POOL_GROUP] / _pool_counts((i + 1) * tm, POOL_HALO, w)
            ext[tm:tm + POOL_HALO, lo:lo + POOL_GROUP] = jnp.where(i < nt - 1, nxt, 0.0)
        for g, w in enumerate(POOL_WINDOWS):
            lo = g * POOL_GROUP
            acc = ext[0:tm, lo:lo + POOL_GROUP]
            for s in range(1, w):
                acc = acc + ext[s:s + tm, lo:lo + POOL_GROUP]
            dp_ref[:, 512 + lo:512 + lo + POOL_GROUP] = (acc - ddv[:, lo:lo + POOL_GROUP]).astype(BF16)
        dp_ref[:, 1024:2048] = dz_ref[...]
        dwq_ref[...] += _dot(qn_ref[...], dqr_ref[...], TN)
        dwk_ref[...] += _dot(kvn_ref[...], dkb_ref[...], TN)
        dwv_ref[...] += _dot(kvn_ref[...], dv_ref[...], TN)
        dwp_ref[...] += _dot(d_ref[...], dyb_ref[...], TN)
        dwin_ref[...] += _dot(dp_ref[...], hn_ref[...], TN)

    row = lambda w: pl.BlockSpec((tm, w), lambda i: (i, 0))
    vec = lambda w: pl.BlockSpec((1, w), lambda i: (0, 0))
    whole = lambda a: pl.BlockSpec(a.shape, lambda i: (0, 0))
    weights = (wq, wk, wv, wpool)
    return pl.pallas_call(
        body, name=name, grid=(nt,),
        in_specs=[row(1024), vec(MLA_Q_RANK), vec(MLA_KV_RANK), row(1024), row(1024), row(512), row(LANES), row(LANES),
                  row(POOL_WIDTH),
                  pl.BlockSpec((POOL_HALO, POOL_WIDTH), lambda i: (jnp.minimum((i + 1) * hb, last_halo), 0)),
                  row(1024), row(MLA_Q_RANK), row(MLA_KV_RANK), row(POOL_WIDTH), row(1024)] + [whole(w) for w in weights],
        out_specs=[row(IN_AB_PAD), vec(MLA_Q_RANK), vec(MLA_KV_RANK)] + [whole(w) for w in weights]
        + [pl.BlockSpec((IN_AB_PAD, 1024), lambda i: (0, 0))],
        out_shape=[jax.ShapeDtypeStruct((t, IN_AB_PAD), BF16), jax.ShapeDtypeStruct((1, MLA_Q_RANK), F32),
                   jax.ShapeDtypeStruct((1, MLA_KV_RANK), F32)] + [jax.ShapeDtypeStruct(w.shape, F32) for w in weights]
        + [jax.ShapeDtypeStruct((IN_AB_PAD, 1024), F32)],
        scratch_shapes=[pltpu.VMEM((tm + POOL_HALO, POOL_WIDTH), F32), pltpu.VMEM((tm, 1024), BF16),
                        pltpu.VMEM((tm, 1024), BF16)],
        compiler_params=_params(("arbitrary",)),
    )(proj, q_a_norm, kv_a_norm, dq, dk, dv, cos_t, sin_t, dyb, dyb, dz, qn, kvn, d, hn, wq, wk, wv, wpool)


def _gate_out_proj(o, ybraw, proj, pool_scale, w, hres, *, name):
    t = o.shape[0]
    tm = _tile(t, 2 * ROW_TILE)

    def body(o_ref, yb_ref, z_ref, ps_ref, w_ref, h_ref, ho_ref, y_ref):
        z = z_ref[...]
        sz = z * _sigmoid(z)
        y_ref[:, 0:512] = (o_ref[...] * sz[:, 0:512]).astype(BF16)
        y_ref[:, 512:1024] = (yb_ref[...] * ps_ref[...] * sz[:, 512:1024]).astype(BF16)
        ho_ref[...] = h_ref[...] + _dot(y_ref[...], w_ref[...])

    row = lambda w_: pl.BlockSpec((tm, w_), lambda i: (i, 0))
    return pl.pallas_call(
        body, name=name, grid=(t // tm,),
        in_specs=[row(512), row(512), pl.BlockSpec((tm, 1024), lambda i: (i, 1)), pl.BlockSpec((1, 512), lambda i: (0, 0)),
                  pl.BlockSpec(w.shape, lambda i: (0, 0)), row(1024)],
        out_specs=[row(1024), row(1024)],
        out_shape=[jax.ShapeDtypeStruct((t, 1024), F32), jax.ShapeDtypeStruct((t, 1024), BF16)],
        compiler_params=_params(("parallel",)),
    )(o, ybraw, proj, pool_scale, w, hres)


def _gate_bwd(dh, w, o, ybraw, proj, pool_scale, *, name):
    t = o.shape[0]
    tm = _tile(t, ROW_TILE)

    def body(dh_ref, w_ref, o_ref, yb_ref, z_ref, ps_ref, do_ref, dl_ref, dyb_ref, dz_ref, dps_ref):
        i = pl.program_id(0)
        z = z_ref[...]
        sg = _sigmoid(z)
        sz = z * sg
        dsz = sg * (1.0 + z * (1.0 - sg))
        dyv = _dot(dh_ref[...], w_ref[...], NT)
        dcat = dyv * sz
        ov = o_ref[...]
        ybs = yb_ref[...] * ps_ref[...]
        dz_ref[:, 0:512] = (dyv[:, 0:512] * ov * dsz[:, 0:512]).astype(BF16)
        dz_ref[:, 512:1024] = (dyv[:, 512:1024] * ybs * dsz[:, 512:1024]).astype(BF16)
        do = dcat[:, 0:512]
        do_ref[...] = do.astype(BF16)
        r_i = (lax.broadcasted_iota(jnp.int32, (1024, 512), 0) % 512) // MLA_V
        c_i = lax.broadcasted_iota(jnp.int32, (1024, 512), 1) // MLA_V
        prod = do * ov
        hi = prod.astype(BF16)
        lo = (prod - hi.astype(F32)).astype(BF16)
        dl_ref[...] = _dot(jnp.concatenate([hi, lo], axis=1), (r_i == c_i).astype(BF16))
        dyb_ref[...] = (dcat[:, 512:1024] * ps_ref[...]).astype(BF16)

        @pl.when(i == 0)
        def _():
            dps_ref[...] = jnp.zeros_like(dps_ref)

        dps_ref[...] += jnp.sum(dcat[:, 512:1024] * yb_ref[...], axis=0, keepdims=True)

    row = lambda w: pl.BlockSpec((tm, w), lambda i: (i, 0))
    vec = pl.BlockSpec((1, 512), lambda i: (0, 0))
    return pl.pallas_call(
        body, name=name, grid=(t // tm,),
        in_specs=[row(1024), pl.BlockSpec(w.shape, lambda i: (0, 0)), row(512), row(512),
                  pl.BlockSpec((tm, 1024), lambda i: (i, 1)), vec],
        out_specs=[row(512), row(512), row(512), row(1024), vec],
        out_shape=[jax.ShapeDtypeStruct((t, 512), BF16), jax.ShapeDtypeStruct((t, 512), F32),
                   jax.ShapeDtypeStruct((t, 512), BF16), jax.ShapeDtypeStruct((t, 1024), BF16),
                   jax.ShapeDtypeStruct((1, 512), F32)],
        compiler_params=_params(("arbitrary",)),
    )(dh, w, o, ybraw, proj, pool_scale)


ATT_HP_FWD = 4
ATT_HP_BWD = 2


def _diag_mask(tq):
    return lax.broadcasted_iota(jnp.int32, (tq, tq), 1) <= lax.broadcasted_iota(jnp.int32, (tq, tq), 0)


def _block_schedule(nq, key_major):
    if key_major:
        pairs = [(qi, ki) for ki in range(nq) for qi in range(ki, nq)]
    else:
        pairs = [(qi, ki) for qi in range(nq) for ki in range(qi + 1)]
    return jnp.asarray([p[0] for p in pairs], jnp.int32), jnp.asarray([p[1] for p in pairs], jnp.int32)


def _attn_fwd(q, k, v, *, name):
    t = q.shape[0]
    tq = _tile(t, ATT_TILE)
    nq = t // tq
    hp = ATT_HP_FWD
    qi_tab, ki_tab = _block_schedule(nq, key_major=False)

    def body(qi_ref, ki_ref, q_ref, k_ref, v_ref, o_ref, lse_ref, m_sc, l_sc, acc_sc):
        step = pl.program_id(1)
        qi, ki = qi_ref[step], ki_ref[step]

        @pl.when(ki == 0)
        def _():
            m_sc[...] = jnp.full_like(m_sc, -jnp.inf)
            l_sc[...] = jnp.zeros_like(l_sc)
            acc_sc[...] = jnp.zeros_like(acc_sc)

        def block(on_diagonal):
            scores = []
            for h in range(hp):
                sl = slice(h * LANES, (h + 1) * LANES)
                scores.append(_dot(q_ref[:, sl], k_ref[:, sl], NT))
            if on_diagonal:
                mask = _diag_mask(tq)
                scores = [jnp.where(mask, s, -jnp.inf) for s in scores]
            for h, s in enumerate(scores):
                vv = v_ref[:, (h // 2) * LANES:(h // 2 + 1) * LANES]
                m_prev = m_sc[h]
                m_new = jnp.maximum(m_prev, jnp.max(s, axis=-1, keepdims=True))
                alpha = jnp.exp2(m_prev - m_new)
                p = jnp.exp2(s - m_new[:, 0:1])
                l_sc[h] = alpha * l_sc[h] + jnp.sum(p, axis=-1, keepdims=True)
                acc_sc[h] = alpha * acc_sc[h] + _dot(p.astype(BF16), vv)
                m_sc[h] = m_new

        pl.when(ki < qi)(functools.partial(block, False))
        pl.when(ki == qi)(functools.partial(block, True))

        @pl.when(ki == qi)
        def _():
            first = lax.broadcasted_iota(jnp.int32, (tq, LANES), 1) < MLA_V
            for pr in range(hp // 2):
                a, b = 2 * pr, 2 * pr + 1
                sl = slice(pr * LANES, (pr + 1) * LANES)
                o_ref[:, sl] = jnp.where(first, acc_sc[a] / l_sc[a], acc_sc[b] / l_sc[b])
                lse_ref[:, sl] = jnp.where(first, m_sc[a] + jnp.log2(l_sc[a]), m_sc[b] + jnp.log2(l_sc[b]))

    grid_spec = pltpu.PrefetchScalarGridSpec(
        num_scalar_prefetch=2, grid=(MLA_HEADS // hp, qi_tab.shape[0]),
        in_specs=[pl.BlockSpec((tq, hp * LANES), lambda g, s, qt, kt: (qt[s], g)),
                  pl.BlockSpec((tq, hp * LANES), lambda g, s, qt, kt: (kt[s], g)),
                  pl.BlockSpec((tq, hp * MLA_V), lambda g, s, qt, kt: (kt[s], g))],
        out_specs=[pl.BlockSpec((tq, hp * MLA_V), lambda g, s, qt, kt: (qt[s], g)),
                   pl.BlockSpec((tq, hp * MLA_V), lambda g, s, qt, kt: (qt[s], g))],
        scratch_shapes=[pltpu.VMEM((hp, tq, LANES), F32)] * 3,
    )
    return pl.pallas_call(
        body, name=name, grid_spec=grid_spec,
        out_shape=[jax.ShapeDtypeStruct((t, 512), F32), jax.ShapeDtypeStruct((t, 512), F32)],
        compiler_params=_params(("parallel", "arbitrary")),
    )(qi_tab, ki_tab, q, k, v)


def _attn_bwd(q, k, v, do, lse, delta, *, name):
    t = q.shape[0]
    tq = _tile(t, ATT_TILE)
    nq = t // tq
    hp = ATT_HP_BWD
    qi_tab, ki_tab = _block_schedule(nq, key_major=True)

    def body(qi_ref, ki_ref, q_ref, k_ref, v_ref, do_ref, lse_ref, dl_ref, dq_ref, dk_ref, dv_ref, dk_sc, dv_sc):
        step = pl.program_id(1)
        qi, ki = qi_ref[step], ki_ref[step]

        @pl.when(step == 0)
        def _():
            dq_ref[...] = jnp.zeros_like(dq_ref)

        @pl.when(qi == ki)
        def _():
            dk_sc[...] = jnp.zeros_like(dk_sc)
            dv_sc[...] = jnp.zeros_like(dv_sc)

        def block(on_diagonal):
            lane = lax.broadcasted_iota(jnp.int32, (tq, LANES), 1)
            rows = pl.ds(pl.multiple_of(qi * tq, tq), tq)
            heads = [slice(h * LANES, (h + 1) * LANES) for h in range(hp)]
            scores = [_dot(q_ref[:, sl], k_ref[:, sl], NT) for sl in heads]
            dps = []
            for h in range(hp):
                dov = do_ref[:, (h // 2) * LANES:(h // 2 + 1) * LANES]
                mine = (lane < MLA_V) if h % 2 == 0 else (lane >= MLA_V)
                dps.append(_dot(jnp.where(mine, dov, jnp.zeros_like(dov)), v_ref[:, (h // 2) * LANES:(h // 2 + 1) * LANES], NT))
            mask = _diag_mask(tq) if on_diagonal else None
            for h, sl in enumerate(heads):
                col = (h // 2) * LANES + (h % 2) * MLA_V
                p = jnp.exp2(scores[h] - lse_ref[:, col:col + 1])
                if on_diagonal:
                    p = jnp.where(mask, p, 0.0)
                ds = (p * (dps[h] - dl_ref[:, col:col + 1])).astype(BF16)
                dv_sc[h] += _dot(p.astype(BF16), do_ref[:, (h // 2) * LANES:(h // 2 + 1) * LANES], TN)
                dk_sc[h] += _dot(ds, q_ref[:, sl], TN)
                dq_ref[rows, sl] += _dot(ds, k_ref[:, sl], NN)

        pl.when(qi > ki)(functools.partial(block, False))
        pl.when(qi == ki)(functools.partial(block, True))

        @pl.when(qi == nq - 1)
        def _():
            first = lax.broadcasted_iota(jnp.int32, (tq, LANES), 1) < MLA_V
            for h in range(hp):
                dk_ref[:, h * LANES:(h + 1) * LANES] = dk_sc[h] * (1.0 / LOG2E)
            for pr in range(hp // 2):
                dv_ref[:, pr * LANES:(pr + 1) * LANES] = jnp.where(first, dv_sc[2 * pr], dv_sc[2 * pr + 1]).astype(BF16)

    qrow = lambda w: pl.BlockSpec((tq, w), lambda g, s, qt, kt: (qt[s], g))
    krow = lambda w: pl.BlockSpec((tq, w), lambda g, s, qt, kt: (kt[s], g))
    grid_spec = pltpu.PrefetchScalarGridSpec(
        num_scalar_prefetch=2, grid=(MLA_HEADS // hp, qi_tab.shape[0]),
        in_specs=[qrow(hp * LANES), krow(hp * LANES), krow(hp * MLA_V), qrow(hp * MLA_V), qrow(hp * MLA_V), qrow(hp * MLA_V)],
        out_specs=[pl.BlockSpec((t, hp * LANES), lambda g, s, qt, kt: (0, g)), krow(hp * LANES), krow(hp * MLA_V)],
        scratch_shapes=[pltpu.VMEM((hp, tq, LANES), F32), pltpu.VMEM((hp, tq, LANES), F32)],
    )
    return pl.pallas_call(
        body, name=name, grid_spec=grid_spec,
        out_shape=[jax.ShapeDtypeStruct((t, 1024), F32), jax.ShapeDtypeStruct((t, 1024), F32),
                   jax.ShapeDtypeStruct((t, 512), BF16)],
        compiler_params=_params(("parallel", "arbitrary")),
    )(qi_tab, ki_tab, q, k, v, do, lse, delta)


def _conv_rows(ext, tm, w_ref, sec):
    c0 = sec * 1024
    y = ext[CONV_HALO - 3:CONV_HALO - 3 + tm, c0:c0 + 1024] * w_ref[0:1, c0:c0 + 1024]
    for j in range(1, CONV_WIDTH):
        y = y + ext[CONV_HALO - 3 + j:CONV_HALO - 3 + j + tm, c0:c0 + 1024] * w_ref[j:j + 1, c0:c0 + 1024]
    return y


def _c_prep(proj_c, conv_w, a_log, dt_bias, *, name):
    t = proj_c.shape[0]
    tm = _tile(t, ROW_TILE)
    hb = tm // CONV_HALO

    def body(p_ref, halo_ref, ab_ref, w_ref, al_ref, dtb_ref, q_ref, k_ref, v_ref, g_ref, b_ref, gt_ref, ext):
        i = pl.program_id(0)
        ext[0:CONV_HALO, :] = jnp.where(i > 0, halo_ref[...], 0.0)
        ext[CONV_HALO:CONV_HALO + tm, :] = p_ref[...]
        for sec, o_ref in enumerate((q_ref, k_ref, v_ref)):
            y = _conv_rows(ext, tm, w_ref, sec)
            y = y * _sigmoid(y)
            if sec == 2:
                o_ref[...] = y
                continue
            scale = GDN_DK ** -0.5 if sec == 0 else 1.0
            for h in range(GDN_HEADS):
                sl = slice(h * LANES, (h + 1) * LANES)
                blk = y[:, sl]
                r = lax.rsqrt(jnp.sum(blk * blk, axis=-1, keepdims=True) + RMS_EPS)
                o_ref[:, sl] = blk * (r * scale)
        ab = ab_ref[...]
        g = -jnp.exp(al_ref[...]) * _softplus(ab + dtb_ref[...])
        beta = _sigmoid(ab)
        ri = lax.broadcasted_iota(jnp.int32, (tm, tm), 0)
        ci = lax.broadcasted_iota(jnp.int32, (tm, tm), 1)
        lower = ((ri // CHUNK) == (ci // CHUNK)) & (ri >= ci)
        gc = _dot(lower.astype(F32), g, NN, HI)
        eye = lax.broadcasted_iota(jnp.int32, (LANES, LANES), 0) == lax.broadcasted_iota(jnp.int32, (LANES, LANES), 1)
        gt_ref[...] = _dot(eye.astype(F32), gc, NT, HI)[0:GDN_HEADS, :]
        for h in range(GDN_HEADS):
            sl = slice(h * LANES, (h + 1) * LANES)
            g_ref[:, sl] = jnp.broadcast_to(gc[:, h:h + 1], (tm, LANES))
            b_ref[:, sl] = jnp.broadcast_to(beta[:, GDN_HEADS + h:GDN_HEADS + h + 1], (tm, LANES))

    row = lambda w: pl.BlockSpec((tm, w), lambda i: (i, 0))
    vec = lambda r, w: pl.BlockSpec((r, w), lambda i: (0, 0))
    out = jax.ShapeDtypeStruct((t, 1024), F32)
    return pl.pallas_call(
        body, name=name, grid=(t // tm,),
        in_specs=[row(3072), pl.BlockSpec((CONV_HALO, 3072), lambda i: (jnp.maximum(i * hb - 1, 0), 0)),
                  pl.BlockSpec((tm, LANES), lambda i: (i, 32)), vec(CONV_WIDTH, 3072), vec(1, LANES), vec(1, LANES)],
        out_specs=[row(1024)] * 5 + [pl.BlockSpec((GDN_HEADS, tm), lambda i: (0, i))],
        out_shape=[out] * 5 + [jax.ShapeDtypeStruct((GDN_HEADS, t), F32)],
        scratch_shapes=[pltpu.VMEM((tm + CONV_HALO, 3072), F32)],
        compiler_params=_params(("parallel",)),
    )(proj_c, proj_c, proj_c, conv_w, a_log, dt_bias)


def _c_prep_bwd(proj_c, conv_w, a_log, dt_bias, dq, dk, dv, dgb, dbb, dz, *, name):
    t = proj_c.shape[0]
    tm = _tile(t, ROW_TILE)
    hb = tm // CONV_HALO
    nt = t // tm
    rev = lambda i: nt - 1 - i

    def body(p_ref, halo_ref, ab_ref, w_ref, al_ref, dtb_ref, dq_ref, dk_ref, dv_ref, dg_ref, db_ref, dz_ref,
             dp_ref, dw_ref, dal_ref, ddt_ref, ext, dyext, carry, taps):
        step = pl.program_id(0)
        i = rev(step)

        @pl.when(step == 0)
        def _():
            dw_ref[...] = jnp.zeros_like(dw_ref)
            dal_ref[...] = jnp.zeros_like(dal_ref)
            ddt_ref[...] = jnp.zeros_like(ddt_ref)
            carry[...] = jnp.zeros_like(carry)

        ext[0:CONV_HALO, :] = jnp.where(i > 0, halo_ref[...], 0.0)
        ext[CONV_HALO:CONV_HALO + tm, :] = p_ref[...]
        for sec, g_ref in enumerate((dq_ref, dk_ref, dv_ref)):
            c0 = sec * 1024
            for j in range(CONV_WIDTH):
                taps[j] = ext[CONV_HALO - 3 + j:CONV_HALO - 3 + j + tm, c0:c0 + 1024]
            y = taps[0] * w_ref[0:1, c0:c0 + 1024]
            for j in range(1, CONV_WIDTH):
                y = y + taps[j] * w_ref[j:j + 1, c0:c0 + 1024]
            sg = _sigmoid(y)
            act = y * sg
            if sec == 2:
                dact = g_ref[...]
            else:
                scale = GDN_DK ** -0.5 if sec == 0 else 1.0
                parts = []
                for h in range(GDN_HEADS):
                    sl = slice(h * LANES, (h + 1) * LANES)
                    blk = act[:, sl]
                    r = lax.rsqrt(jnp.sum(blk * blk, axis=-1, keepdims=True) + RMS_EPS)
                    n = blk * r
                    dn = g_ref[:, sl] * scale
                    parts.append(r * (dn - n * jnp.sum(dn * n, axis=-1, keepdims=True)))
                dact = jnp.concatenate(parts, axis=-1)
            dy = dact * (sg * (1.0 + y * (1.0 - sg)))
            dyext[0:tm, c0:c0 + 1024] = dy
            for j in range(CONV_WIDTH):
                dw_ref[j:j + 1, c0:c0 + 1024] += jnp.sum(dy * taps[j], axis=0, keepdims=True)
        dyext[tm:tm + CONV_HALO, :] = carry[...]
        carry[...] = dyext[0:CONV_HALO, :]
        for sec in range(3):
            c0 = sec * 1024
            dx = dyext[3:3 + tm, c0:c0 + 1024] * w_ref[0:1, c0:c0 + 1024]
            for j in range(1, CONV_WIDTH):
                dx = dx + dyext[3 - j:3 - j + tm, c0:c0 + 1024] * w_ref[j:j + 1, c0:c0 + 1024]
            dp_ref[:, c0:c0 + 1024] = dx.astype(BF16)
        dp_ref[:, 3072:4096] = dz_ref[...]
        lane = lax.broadcasted_iota(jnp.int32, (tm, LANES), 1)
        dg = jnp.zeros((tm, LANES), F32)
        dbeta = jnp.zeros((tm, LANES), F32)
        for h in range(GDN_HEADS):
            sl = slice(h * LANES, (h + 1) * LANES)
            dg = dg + jnp.where(lane == h, dg_ref[:, sl], 0.0)
            dbeta = dbeta + jnp.where(lane == GDN_HEADS + h, db_ref[:, sl], 0.0)
        ri = lax.broadcasted_iota(jnp.int32, (tm, tm), 0)
        ci = lax.broadcasted_iota(jnp.int32, (tm, tm), 1)
        upper = ((ri // CHUNK) == (ci // CHUNK)) & (ri <= ci)
        dg = _dot(upper.astype(F32), dg, NN, HI)
        pre = ab_ref[...] + dtb_ref[...]
        s = _sigmoid(pre)
        a_exp = jnp.exp(al_ref[...])
        dg_da = dg * (-a_exp * s)
        dp_ref[:, 4096:IN_C_PAD] = (dg_da + dbeta * s * (1.0 - s)).astype(BF16)
        dal_ref[...] += jnp.sum(dg * (-a_exp * _softplus(pre)), axis=0, keepdims=True)
        ddt_ref[...] += jnp.sum(dg_da, axis=0, keepdims=True)

    row = lambda w: pl.BlockSpec((tm, w), lambda s: (rev(s), 0))
    vec = lambda r, w: pl.BlockSpec((r, w), lambda s: (0, 0))
    return pl.pallas_call(
        body, name=name, grid=(nt,),
        in_specs=[row(3072), pl.BlockSpec((CONV_HALO, 3072), lambda s: (jnp.maximum(rev(s) * hb - 1, 0), 0)),
                  pl.BlockSpec((tm, LANES), lambda s: (rev(s), 32)), vec(CONV_WIDTH, 3072), vec(1, LANES), vec(1, LANES),
                  row(1024), row(1024), row(1024), row(1024), row(1024), row(1024)],
        out_specs=[row(IN_C_PAD), vec(CONV_WIDTH, 3072), vec(1, LANES), vec(1, LANES)],
        out_shape=[jax.ShapeDtypeStruct((t, IN_C_PAD), BF16), jax.ShapeDtypeStruct((CONV_WIDTH, 3072), F32),
                   jax.ShapeDtypeStruct((1, LANES), F32), jax.ShapeDtypeStruct((1, LANES), F32)],
        scratch_shapes=[pltpu.VMEM((tm + CONV_HALO, 3072), F32), pltpu.VMEM((tm + CONV_HALO, 3072), F32),
                        pltpu.VMEM((CONV_HALO, 3072), F32), pltpu.VMEM((CONV_WIDTH, tm, 1024), F32)],
        compiler_params=_params(("arbitrary",)),
    )(proj_c, proj_c, proj_c, conv_w, a_log, dt_bias, dq, dk, dv, dgb, dbb, dz)


def _o_gate_bwd(dh, w, o, proj_c, o_norm, *, name):
    t = o.shape[0]
    tm = _tile(t, 2 * ROW_TILE)

    def body(dh_ref, w_ref, o_ref, z_ref, g_ref, do_ref, dz_ref, dg_ref, dy_ref):
        i = pl.program_id(0)

        @pl.when(i == 0)
        def _():
            dg_ref[...] = jnp.zeros_like(dg_ref)

        dy_ref[...] = _dot(dh_ref[...], w_ref[...], NT)
        dg = jnp.zeros((1, LANES), F32)
        for h in range(GDN_HEADS):
            sl = slice(h * LANES, (h + 1) * LANES)
            x = o_ref[:, sl]
            r = lax.rsqrt(jnp.mean(x * x, axis=-1, keepdims=True) + RMS_EPS)
            xh = x * r
            z = z_ref[:, sl]
            sg = _sigmoid(z)
            dyv = dy_ref[:, sl]
            dn = dyv * (z * sg)
            dz_ref[:, sl] = (dyv * xh * g_ref[...] * (sg * (1.0 + z * (1.0 - sg)))).astype(BF16)
            dxh = dn * g_ref[...]
            do_ref[:, sl] = r * (dxh - xh * jnp.mean(dxh * xh, axis=-1, keepdims=True))
            dg = dg + jnp.sum(dn * xh, axis=0, keepdims=True)
        dg_ref[...] += dg

    row = pl.BlockSpec((tm, 1024), lambda i: (i, 0))
    vec = pl.BlockSpec((1, LANES), lambda i: (0, 0))
    return pl.pallas_call(
        body, name=name, grid=(t // tm,),
        in_specs=[row, pl.BlockSpec(w.shape, lambda i: (0, 0)), row, pl.BlockSpec((tm, 1024), lambda i: (i, 3)), vec],
        out_specs=[row, row, vec],
        out_shape=[jax.ShapeDtypeStruct((t, 1024), F32), jax.ShapeDtypeStruct((t, 1024), BF16),
                   jax.ShapeDtypeStruct((1, LANES), F32)],
        scratch_shapes=[pltpu.VMEM((tm, 1024), F32)],
        compiler_params=_params(("arbitrary",)),
    )(dh, w, o, proj_c, o_norm)


PAIR = 2 * CHUNK
GDN_HP = 8


def _bdot(a, b, dims=NN):
    return _dot(a.astype(BF16), b.astype(BF16), dims)


def _each(f, *lists):
    return [f(*args) for args in zip(*lists)]


def _pair_common(q, k, v, gci, gcj, beta):
    ri = lax.broadcasted_iota(jnp.int32, (PAIR, PAIR), 0)
    ci = lax.broadcasted_iota(jnp.int32, (PAIR, PAIR), 1)
    same = (ri // CHUNK) == (ci // CHUNK)
    incl = same & (ri >= ci)
    strict = same & (ri > ci)
    eye = (ri == ci).astype(F32)
    first = lax.broadcasted_iota(jnp.int32, (PAIR, LANES), 0) < CHUNK
    gamma = _each(lambda gi, gj: jnp.where(incl, jnp.exp(jnp.minimum(gi - gj, 0.0)), 0.0), gci, gcj)
    kb = _each(jnp.multiply, k, beta)
    kq = _each(lambda a, b, c_: _bdot(jnp.concatenate([a, b], axis=0), c_, NT), kb, q, k)
    kk = _each(lambda x: x[:PAIR], kq)
    qk = _each(lambda x: x[PAIR:], kq)
    m = _each(lambda x, g: jnp.where(strict, x * g, 0.0), kk, gamma)
    tm_ = _each(lambda x: eye - x, m)
    pw = _each(lambda x: _bdot(x, x), m)
    for it in range(5):
        if it < 4:
            both = _each(lambda x, p: _bdot(jnp.concatenate([x, p], axis=0), p), tm_, pw)
            tm_ = _each(lambda x, b: x + b[:PAIR], tm_, both)
            pw = _each(lambda b: b[PAIR:], both)
        else:
            tm_ = _each(lambda x, p: x + _bdot(x, p), tm_, pw)
    eg = _each(jnp.exp, gci)
    vb = _each(jnp.multiply, v, beta)
    kbe = _each(jnp.multiply, kb, eg)
    uw = _each(lambda x, a, b: _bdot(x, jnp.concatenate([a, b], axis=1)), tm_, vb, kbe)
    attn = _each(lambda x, g: jnp.where(incl, x * g, 0.0), qk, gamma)
    gl_a = _each(lambda g: g[CHUNK - 1:CHUNK, :], gci)
    gl_b = _each(lambda g: g[PAIR - 1:PAIR, :], gci)
    ek = _each(lambda a, b, g: jnp.exp(jnp.where(first, a, b) - g), gl_a, gl_b, gci)
    return dict(incl=incl, strict=strict, gamma=gamma, kb=kb, m=m, tm=tm_, eg=eg, vb=vb, kbe=kbe,
                u=_each(lambda x: x[:, :LANES], uw), w=_each(lambda x: x[:, LANES:], uw), attn=attn,
                qd=_each(jnp.multiply, q, eg), ek=ek, kd=_each(jnp.multiply, k, ek),
                glast_a=_each(jnp.exp, gl_a), glast_b=_each(jnp.exp, gl_b))


def _gdn_specs(t, ts, order):
    nc = ts // CHUNK
    blk = pl.BlockSpec((ts, GDN_HP * LANES), lambda h, s: (order(s), h))
    row = pl.BlockSpec((GDN_HP, 1, ts), lambda h, s: (h, 0, order(s)))
    st = pl.BlockSpec((GDN_HP, nc, LANES, LANES), lambda h, s: (h, order(s), 0, 0))
    return blk, row, st


def _gdn_fwd(q, k, v, gcb, gct, bb, *, name):
    t = q.shape[0]
    ts = _tile(t, GDN_TILE)
    npair = ts // PAIR

    def body(q_ref, k_ref, v_ref, g_ref, gt_ref, b_ref, o_ref, st_ref, s_sc):
        @pl.when(pl.program_id(1) == 0)
        def _():
            s_sc[...] = jnp.zeros_like(s_sc)

        def pair(pi, _):
            rows = pl.ds(pl.multiple_of(pi * PAIR, PAIR), PAIR)
            heads = [slice(hh * LANES, (hh + 1) * LANES) for hh in range(GDN_HP)]
            c = CHUNK
            cat0 = lambda *xs: jnp.concatenate(xs, axis=0)
            s0 = [s_sc[hh] for hh in range(GDN_HP)]
            cm = _pair_common([q_ref[rows, sl] for sl in heads], [k_ref[rows, sl] for sl in heads],
                              [v_ref[rows, sl] for sl in heads], [g_ref[rows, sl] for sl in heads],
                              [gt_ref[hh, :, rows] for hh in range(GDN_HP)], [b_ref[rows, sl] for sl in heads])
            u, w, qd, kd = cm["u"], cm["w"], cm["qd"], cm["kd"]
            r0 = _each(lambda w_, q_, s: _bdot(cat0(w_[:c], q_[:c]), s), w, qd, s0)
            vn_a = _each(lambda u_, r: u_[:c] - r[:c], u, r0)
            s1 = _each(lambda s, gl, k_, vn: s * gl + _bdot(k_[:c], vn, TN), s0, cm["glast_a"], kd, vn_a)
            r1 = _each(lambda w_, q_, s: _bdot(cat0(w_[c:], q_[c:]), s), w, qd, s1)
            vn_b = _each(lambda u_, r: u_[c:] - r[:c], u, r1)
            s2 = _each(lambda s, gl, k_, vn: s * gl + _bdot(k_[c:], vn, TN), s1, cm["glast_b"], kd, vn_b)
            o = _each(lambda ra, rb, at, va, vb_: cat0(ra[c:], rb[c:]) + _bdot(at, cat0(va, vb_)),
                      r0, r1, cm["attn"], vn_a, vn_b)
            for hh, sl in enumerate(heads):
                st_ref[hh, 2 * pi] = s0[hh]
                st_ref[hh, 2 * pi + 1] = s1[hh]
                s_sc[hh] = s2[hh]
                o_ref[rows, sl] = o[hh]
            return 0

        lax.fori_loop(0, npair, pair, 0)

    blk, row, st = _gdn_specs(t, ts, lambda s: s)
    return pl.pallas_call(
        body, name=name, grid=(GDN_HEADS // GDN_HP, t // ts), in_specs=[blk, blk, blk, blk, row, blk],
        out_specs=[blk, st],
        out_shape=[jax.ShapeDtypeStruct((t, 1024), F32), jax.ShapeDtypeStruct((GDN_HEADS, t // CHUNK, LANES, LANES), F32)],
        scratch_shapes=[pltpu.VMEM((GDN_HP, LANES, LANES), F32)],
        compiler_params=_params(("parallel", "arbitrary")),
    )(q, k, v, gcb, gct, bb)


def _gdn_bwd(q, k, v, gcb, gct, bb, do, states, *, name):
    t = q.shape[0]
    ts = _tile(t, GDN_TILE)
    npair = ts // PAIR
    ns = t // ts
    c = CHUNK

    def body(q_ref, k_ref, v_ref, g_ref, gt_ref, b_ref, do_ref, st_ref, dq_ref, dk_ref, dv_ref, dg_ref, db_ref, ds_sc):
        @pl.when(pl.program_id(1) == 0)
        def _():
            ds_sc[...] = jnp.zeros_like(ds_sc)

        rowsum = lambda x: jnp.sum(x, axis=-1, keepdims=True)
        total = lambda x: jnp.sum(rowsum(x), axis=0, keepdims=True)
        cat0 = lambda *xs: jnp.concatenate(xs, axis=0)
        cat1 = lambda *xs: jnp.concatenate(xs, axis=1)

        def pair(step, _):
            pi = npair - 1 - step
            rows = pl.ds(pl.multiple_of(pi * PAIR, PAIR), PAIR)
            heads = [slice(hh * LANES, (hh + 1) * LANES) for hh in range(GDN_HP)]
            hs = range(GDN_HP)
            qv, kv, vv = ([r[rows, sl] for sl in heads] for r in (q_ref, k_ref, v_ref))
            beta = [b_ref[rows, sl] for sl in heads]
            dov = [do_ref[rows, sl] for sl in heads]
            s0 = [st_ref[hh, 2 * pi] for hh in hs]
            s1 = [st_ref[hh, 2 * pi + 1] for hh in hs]
            ds2 = [ds_sc[hh] for hh in hs]
            cm = _pair_common(qv, kv, vv, [g_ref[rows, sl] for sl in heads], [gt_ref[hh, :, rows] for hh in hs], beta)
            u, w, qd, kd, attn = cm["u"], cm["w"], cm["qd"], cm["kd"], cm["attn"]
            tmat, gamma, eg = cm["tm"], cm["gamma"], cm["eg"]
            incl, strict = cm["incl"], cm["strict"]
            vn_a = _each(lambda u_, w_, s: u_[:c] - _bdot(w_[:c], s), u, w, s0)
            vn_b = _each(lambda u_, w_, s: u_[c:] - _bdot(w_[c:], s), u, w, s1)
            vn = _each(cat0, vn_a, vn_b)
            dvn_att = _each(lambda a, d: _bdot(a, d, TN), attn, dov)
            dattn = _each(lambda d, v_: jnp.where(incl, _bdot(d, v_, NT), 0.0), dov, vn)
            dvn_b = _each(lambda x, k_, d: x[c:] + _bdot(k_[c:], d), dvn_att, kd, ds2)
            rb = _each(lambda d, x, s: _bdot(cat0(d[c:], x), s, NT), dov, dvn_b, s1)
            dkd_b = _each(lambda v_, d: _bdot(v_, d, NT), vn_b, ds2)
            dgl_b = _each(lambda d, s: total(d * s), ds2, s1)
            ds1 = _each(lambda d, gl, q_, w_, o_, x: d * gl + _bdot(cat0(q_[c:], w_[c:]), cat0(o_[c:], -x), TN),
                        ds2, cm["glast_b"], qd, w, dov, dvn_b)
            dvn_a = _each(lambda x, k_, d: x[:c] + _bdot(k_[:c], d), dvn_att, kd, ds1)
            ra = _each(lambda d, x, s: _bdot(cat0(d[:c], x), s, NT), dov, dvn_a, s0)
            dkd_a = _each(lambda v_, d: _bdot(v_, d, NT), vn_a, ds1)
            dgl_a = _each(lambda d, s: total(d * s), ds1, s0)
            ds0 = _each(lambda d, gl, q_, w_, o_, x: d * gl + _bdot(cat0(q_[:c], w_[:c]), cat0(o_[:c], -x), TN),
                        ds1, cm["glast_a"], qd, w, dov, dvn_a)
            dvn = _each(cat0, dvn_a, dvn_b)
            dqd = _each(lambda a, b: cat0(a[:c], b[:c]), ra, rb)
            dw = _each(lambda a, b: -cat0(a[c:], b[c:]), ra, rb)
            dkd = _each(cat0, dkd_a, dkd_b)
            dvw = _each(cat1, dvn, dw)
            dt_ = _each(lambda x, a, b: _bdot(x, cat1(a, b), NT), dvw, cm["vb"], cm["kbe"])
            dvbk = _each(lambda t_, x, y: _bdot(t_, cat1(x, y), TN), tmat, dvw, dt_)
            dvb = _each(lambda x: x[:, :LANES], dvbk)
            dkbe = _each(lambda x: x[:, LANES:2 * LANES], dvbk)
            da1 = _each(lambda x: x[:, 2 * LANES:], dvbk)
            dm = _each(lambda x, t_: jnp.where(strict, -_bdot(x, t_, NT), 0.0), da1, tmat)
            dkk = _each(jnp.multiply, dm, gamma)
            dqk = _each(jnp.multiply, dattn, gamma)
            z = _each(lambda a, b, c_, d: a * b + c_ * d, dm, cm["m"], dattn, attn)
            dkq = _each(lambda a, b, k_: _bdot(cat0(a, b), k_), dkk, dqk, kv)
            dkb = _each(lambda x, y, e: x[:PAIR] + y * e, dkq, dkbe, eg)
            dk = _each(lambda a, b, kb_, q_, x, e, y, be: _bdot(cat0(a, b), cat0(kb_, q_), TN) + x * e + y * be,
                       dkk, dqk, cm["kb"], qv, dkd, cm["ek"], dkb, beta)
            dq = _each(lambda x, y, e: x[PAIR:] + y * e, dkq, dqd, eg)

            def colsum_of(z_):
                zh = z_.astype(BF16)
                zl = (z_ - zh.astype(F32)).astype(BF16)
                return _dot(cat0(zh, zl), jnp.ones((2 * PAIR, LANES), BF16), TN)

            colsum = _each(colsum_of, z)
            ri = lax.broadcasted_iota(jnp.int32, (PAIR, LANES), 0)
            for hh, sl in enumerate(heads):
                dkd_kd = dkd[hh] * kd[hh]
                dgc = (rowsum(z[hh]) - colsum[hh] + rowsum(dqd[hh] * qd[hh]) - rowsum(dkd_kd)
                       + rowsum(dkbe[hh] * cm["kbe"][hh]))
                last_a = total(dkd_kd[:c]) + dgl_a[hh] * cm["glast_a"][hh]
                last_b = total(dkd_kd[c:]) + dgl_b[hh] * cm["glast_b"][hh]
                dgc = dgc + jnp.where(ri == c - 1, last_a, 0.0) + jnp.where(ri == PAIR - 1, last_b, 0.0)
                ds_sc[hh] = ds0[hh]
                dq_ref[rows, sl] = dq[hh]
                dk_ref[rows, sl] = dk[hh]
                dv_ref[rows, sl] = dvb[hh] * beta[hh]
                db_ref[rows, sl] = jnp.broadcast_to(rowsum(dkb[hh] * kv[hh]) + rowsum(dvb[hh] * vv[hh]), (PAIR, LANES))
                dg_ref[rows, sl] = dgc
            return 0

        lax.fori_loop(0, npair, pair, 0)

    blk, row, st = _gdn_specs(t, ts, lambda s: ns - 1 - s)
    out = jax.ShapeDtypeStruct((t, 1024), F32)
    return pl.pallas_call(
        body, name=name, grid=(GDN_HEADS // GDN_HP, ns), in_specs=[blk, blk, blk, blk, row, blk, blk, st],
        out_specs=[blk] * 5, out_shape=[out] * 5, scratch_shapes=[pltpu.VMEM((GDN_HP, LANES, LANES), F32)],
        compiler_params=_params(("parallel", "arbitrary")),
    )(q, k, v, gcb, gct, bb, do, states)


def _gate_out_proj_loss(o, proj_c, o_norm, w, hres, g, target, *, name):
    t, d = hres.shape
    tm = _tile(t, 2 * ROW_TILE)

    def body(o_ref, z_ref, on_ref, w_ref, h_ref, g_ref, t_ref, dh_ref, dhb_ref, dg_ref, loss_ref, dw_ref, y_ref):
        i = pl.program_id(0)
        for hd in range(GDN_HEADS):
            sl = slice(hd * LANES, (hd + 1) * LANES)
            ov = o_ref[:, sl]
            rr = lax.rsqrt(jnp.mean(ov * ov, axis=-1, keepdims=True) + RMS_EPS)
            z = z_ref[:, sl]
            y_ref[:, sl] = (ov * rr * on_ref[...] * (z * _sigmoid(z))).astype(BF16)
        x = h_ref[...] + _dot(y_ref[...], w_ref[...])
        r = lax.rsqrt(jnp.mean(x * x, axis=-1, keepdims=True) + RMS_EPS)
        xh = x * r
        err = xh * g_ref[...] - t_ref[...]
        dy = err * (1.0 / d)
        dxh = dy * g_ref[...]
        dh = r * (dxh - xh * jnp.mean(dxh * xh, axis=-1, keepdims=True))
        dh_ref[...] = dh
        dhb_ref[...] = dh.astype(BF16)

        @pl.when(i == 0)
        def _():
            dg_ref[...] = jnp.zeros_like(dg_ref)
            loss_ref[...] = jnp.zeros_like(loss_ref)
            dw_ref[...] = jnp.zeros_like(dw_ref)

        dg_ref[...] += jnp.sum(dy * xh, axis=0, keepdims=True)
        part = 0.5 * jnp.sum(jnp.mean(err * err, axis=-1, keepdims=True), axis=0, keepdims=True)
        loss_ref[...] += jnp.broadcast_to(part, loss_ref.shape)
        dw_ref[...] += _dot(y_ref[...], dhb_ref[...], TN)

    row = pl.BlockSpec((tm, d), lambda i: (i, 0))
    vec = pl.BlockSpec((1, d), lambda i: (0, 0))
    return pl.pallas_call(
        body, name=name, grid=(t // tm,),
        in_specs=[row, pl.BlockSpec((tm, 1024), lambda i: (i, 3)), pl.BlockSpec((1, LANES), lambda i: (0, 0)),
                  pl.BlockSpec(w.shape, lambda i: (0, 0)), row, vec, row],
        out_specs=[row, row, vec, pl.BlockSpec((8, LANES), lambda i: (0, 0)), pl.BlockSpec(w.shape, lambda i: (0, 0))],
        out_shape=[jax.ShapeDtypeStruct((t, d), F32), jax.ShapeDtypeStruct((t, d), BF16),
                   jax.ShapeDtypeStruct((1, d), F32), jax.ShapeDtypeStruct((8, LANES), F32),
                   jax.ShapeDtypeStruct(w.shape, F32)],
        scratch_shapes=[pltpu.VMEM((tm, d), BF16)],
        compiler_params=_params(("arbitrary",)),
    )(o, proj_c, o_norm, w, hres, g, target)


def _pad_cols(w, n):
    return jnp.pad(w, ((0, 0), (0, n - w.shape[1])))


def _layout_odd(w):
    return dict(
        winc=_pad_cols(w["w_in_c"], IN_C_PAD).astype(BF16), wout_c=w["w_out_c"].astype(BF16), conv_w=w["conv_w"],
        a_log=_pad_cols(w["a_log"], LANES), dt_bias=_pad_cols(w["dt_bias"], LANES),
        norm_c=w["norm_c"], o_norm=w["o_norm"], final_norm=w["final_norm"],
    )


def _layout_in_ab(w):
    z = lambda r, c: jnp.zeros((r, c), w["w_in_ab"].dtype)
    wi = w["w_in_ab"]
    win = jnp.concatenate([wi[:, :384], z(1024, 64), wi[:, 384:416], z(1024, 32), wi[:, 416:]], axis=1)
    pw = w["pool_w"]
    rows = []
    for g in range(4):
        rows.append(jnp.concatenate([pw[g] if j == g else jnp.zeros((128, 128), F32) for j in range(4)], axis=1))
    wpool = jnp.concatenate(rows, axis=0)
    half = MLA_ROPE // 2
    inv = 1.0 / (ROPE_THETA ** (jnp.arange(half, dtype=F32) / half))
    inv_lane = jnp.concatenate([jnp.zeros((MLA_NOPE,), F32), inv, inv, jnp.zeros((32,), F32)]).reshape(1, LANES)
    return dict(win=win.astype(BF16), wpool=wpool.astype(BF16), inv_lane=inv_lane, norm_ab=w["norm_ab"],
                q_a_norm=w["q_a_norm"], kv_a_norm=w["kv_a_norm"], pool_scale=w["pool_scale"])


def _layout_mid(w):
    wq = jnp.pad(w["w_q_b"].reshape(MLA_Q_RANK, MLA_HEADS, 96), ((0, 0), (0, 0), (0, 32))).reshape(MLA_Q_RANK, 1024)
    kv3 = w["w_kv_b"].reshape(MLA_KV_RANK, MLA_HEADS, 128)
    wk = jnp.pad(kv3[..., :MLA_NOPE], ((0, 0), (0, 0), (0, 64))).reshape(MLA_KV_RANK, 1024)
    wv = kv3[..., MLA_NOPE:].reshape(MLA_KV_RANK, 512)
    return dict(wq=wq.astype(BF16), wk=wk.astype(BF16), wv=wv.astype(BF16), wout_ab=w["w_out_ab"].astype(BF16))


def _unlayout_grads(g, names):
    out = {}
    for name in names:
        if name == "w_in_ab":
            dwin = g["win"]
            out[name] = jnp.concatenate([dwin[:384], dwin[448:480], dwin[512:]], axis=0)
        elif name == "w_q_b":
            out[name] = g["wq"].reshape(MLA_Q_RANK, MLA_HEADS, 128)[..., :96].reshape(MLA_Q_RANK, 768)
        elif name == "w_kv_b":
            out[name] = jnp.concatenate([g["wk"].reshape(MLA_KV_RANK, MLA_HEADS, 128)[..., :MLA_NOPE],
                                         g["wv"].reshape(MLA_KV_RANK, MLA_HEADS, MLA_V)], axis=-1).reshape(MLA_KV_RANK, 1024)
        elif name == "w_in_c":
            out[name] = g["winc"][:4112]
        else:
            out[name] = g[{"w_out_ab": "wout_ab", "w_out_c": "wout_c"}[name]]
    return out


def _local_step(x, pos, target, lw, more_weights=None, on_grads=None):
    mm = _matmul
    proj, hn = _rms_in_proj(x, lw["norm_ab"], lw["win"], name="rms_in_ab")
    if more_weights is not None:
        lw = {**lw, **more_weights("mid", proj)}
    q, k, v, ybraw, qn, kvn, d, cos_t, sin_t = _ab_prep(
        proj, pos, lw["inv_lane"], lw["q_a_norm"], lw["kv_a_norm"], lw["wq"], lw["wk"], lw["wv"], lw["wpool"], name="ab_prep")
    o, lse = _attn_fwd(q, k, v, name="attn_fwd")
    h1, y = _gate_out_proj(o, ybraw, proj, lw["pool_scale"], lw["wout_ab"], x, name="gate_out_ab")
    lo = lw if more_weights is None else more_weights("odd", h1)
    proj_c, hn1 = _rms_in_proj(h1, lo["norm_c"], lo["winc"], name="rms_in_c")
    q2, k2, v2, gb, bb, gt = _c_prep(proj_c, lo["conv_w"], lo["a_log"], lo["dt_bias"], name="c_prep")
    gt = gt.reshape(GDN_HEADS, 1, gt.shape[1])
    o2, states = _gdn_fwd(q2, k2, v2, gb, gt, bb, name="gdn_fwd")
    dh2, dh2b, d_final, loss, d_wout_c = _gate_out_proj_loss(
        o2, proj_c, lo["o_norm"], lo["wout_c"], h1, lo["final_norm"], target, name="gate_out_c_loss")
    g = {"final_norm": d_final, "wout_c": d_wout_c, "loss": loss}
    do2, dz2, g["o_norm"] = _o_gate_bwd(dh2b, lo["wout_c"], o2, proj_c, lo["o_norm"], name="gate_c_bwd")
    dq2, dk2, dv2, dgb, dbb = _gdn_bwd(q2, k2, v2, gb, gt, bb, do2, states, name="gdn_bwd")
    dproj_c, g["conv_w"], g["a_log"], g["dt_bias"] = _c_prep_bwd(
        proj_c, lo["conv_w"], lo["a_log"], lo["dt_bias"], dq2, dk2, dv2, dgb, dbb, dz2, name="c_prep_bwd")
    g["winc"] = mm(dproj_c, hn1, "tn", name="in_c_dw")
    notify = (lambda tag, after=None: 0.0) if on_grads is None else (lambda tag, after=None: on_grads(tag, g, after))
    dh1, g["norm_c"], dh1b, g["wout_ab"] = _matmul_rms_bwd(
        dproj_c, lo["winc"], h1, lo["norm_c"] + notify("odd"), dh2, name="in_c_dx_rms", prev_y=y)
    pool_scale = lw["pool_scale"] + notify("odd_go", dh1b) + notify("out_ab")
    do, delta, dyb, dz, g["pool_scale"] = _gate_bwd(dh1b, lw["wout_ab"], o, ybraw, proj, pool_scale, name="gate_ab_bwd")
    dq, dk, dv = _attn_bwd(q, k, v, do, lse, delta, name="attn_bwd")
    dproj, g["q_a_norm"], g["kv_a_norm"], g["wq"], g["wk"], g["wv"], g["wpool"], g["win"] = _ab_prep_bwd(
        proj, lw["q_a_norm"], lw["kv_a_norm"], dq, dk, dv, cos_t, sin_t, dyb, dz, qn, kvn, d, hn,
        lw["wq"], lw["wk"], lw["wv"], lw["wpool"], name="ab_prep_bwd")
    norm_ab = lw["norm_ab"] + notify("in_ab")
    dx, g["norm_ab"] = _matmul_rms_bwd(dproj, lw["win"], x, norm_ab, dh1, name="in_ab_dx_rms")
    return loss, dx, g


_HBM = pl.BlockSpec(memory_space=pltpu.HBM)


def _place():
    return lax.axis_index("x"), lax.axis_index("y"), lax.axis_index("c")


def _flip(v, f):
    return 1 - v if f else v


_CHIP_FLIPS = ((1, 0), (0, 1), (1, 1))
_DEV_FLIPS = tuple((fx, fy, fc) for fx in (0, 1) for fy in (0, 1) for fc in (0, 1) if fx or fy or fc)


def _rcopy(src, dst, send_sems, recv_sems, k, to):
    return pltpu.make_async_remote_copy(src_ref=src, dst_ref=dst, send_sem=send_sems.at[k], recv_sem=recv_sems.at[k],
                                        device_id=to, device_id_type=MESH)


def _my_half(ref, c, axis):
    rh = ref.shape[axis] // 2
    idx = [slice(None)] * len(ref.shape)
    idx[axis] = pl.ds(c * rh, rh)
    return ref.at[tuple(idx)]


def _gather_weights(bigs, smalls):
    nb, ns = len(bigs), len(smalls)

    def body(*refs):
        ins, outs = refs[:nb + ns], refs[nb + ns:2 * (nb + ns)]
        send_sems, recv_sems, local_sems = refs[2 * (nb + ns):]
        x, y, c = _place()
        j0 = 2 * x + y
        sib = (x, y, 1 - c)
        chips = [(_flip(x, fx), _flip(y, fy)) for fx, fy in _CHIP_FLIPS]
        local = [pltpu.make_async_copy(i_ref, o_ref.at[j0], local_sems.at[a])
                 for a, (i_ref, o_ref) in enumerate(zip(ins, outs))]
        for cp in local:
            cp.start()
        sends = []
        for k, (px, py) in enumerate(chips):
            for a in range(nb):
                sends.append(_rcopy(_my_half(ins[a], c, 0), _my_half(outs[a].at[j0], c, 0), send_sems, recv_sems,
                                    6 * a + k, (px, py, c)))
            for s in range(ns):
                sends.append(_rcopy(ins[nb + s], outs[nb + s].at[j0], send_sems, recv_sems, 6 * nb + 3 * s + k, (px, py, c)))
        for cp in sends:
            cp.start()
        for k, (px, py) in enumerate(chips):
            jk = 2 * px + py
            for a in range(nb):
                landed = _my_half(outs[a].at[jk], c, 0)
                _rcopy(landed, landed, send_sems, recv_sems, 6 * a + k, (px, py, c)).wait_recv()
                fwd = _rcopy(landed, landed, send_sems, recv_sems, 6 * a + 3 + k, sib)
                fwd.start()
                sends.append(fwd)
        for k, (px, py) in enumerate(chips):
            jk = 2 * px + py
            for a in range(nb):
                other = _my_half(outs[a].at[jk], 1 - c, 0)
                _rcopy(other, other, send_sems, recv_sems, 6 * a + 3 + k, sib).wait_recv()
            for s in range(ns):
                _rcopy(ins[nb + s], outs[nb + s].at[jk], send_sems, recv_sems, 6 * nb + 3 * s + k, (px, py, c)).wait_recv()
        for cp in sends:
            cp.wait_send()
        for cp in local:
            cp.wait()

    arrays = list(bigs) + list(smalls)
    n_sem = 6 * nb + 3 * ns
    return pl.pallas_call(
        body, name="gather_weights", in_specs=[_HBM] * len(arrays), out_specs=[_HBM] * len(arrays),
        out_shape=[jax.ShapeDtypeStruct((4,) + a.shape, a.dtype) for a in arrays],
        scratch_shapes=[pltpu.SemaphoreType.DMA((n_sem,)), pltpu.SemaphoreType.DMA((n_sem,)),
                        pltpu.SemaphoreType.DMA((len(arrays),))],
    )(*arrays)


def _core_swap_partial(gs, by_cols, *, name):
    n = len(gs)

    def body(*refs):
        ins, outs = refs[:n], refs[n:2 * n]
        send_sems, recv_sems = refs[2 * n:]
        x, y, c = _place()
        copies = [_rcopy(_my_half(i_ref, 1 - c, 2 if by_cols[a] else 1), o_ref, send_sems, recv_sems, a, (x, y, 1 - c))
                  for a, (i_ref, o_ref) in enumerate(zip(ins, outs))]
        for cp in copies:
            cp.start()
        for cp in copies:
            cp.wait()

    halved = lambda g, cols: (4, g.shape[1], g.shape[2] // 2) if cols else (4, g.shape[1] // 2, g.shape[2])
    return pl.pallas_call(
        body, name=name, in_specs=[_HBM] * n, out_specs=[_HBM] * n,
        out_shape=[jax.ShapeDtypeStruct(halved(g, cols), g.dtype) for g, cols in zip(gs, by_cols)],
        scratch_shapes=[pltpu.SemaphoreType.DMA((n,)), pltpu.SemaphoreType.DMA((n,))],
    )(*gs)


def _core_swap_partial_start(gs, by_cols, *, name):
    n = len(gs)
    halved = lambda g, cols: (4, g.shape[1], g.shape[2] // 2) if cols else (4, g.shape[1] // 2, g.shape[2])
    lands = [lax.empty(halved(g, cols), g.dtype) for g, cols in zip(gs, by_cols)]

    def body(*refs):
        ins, land_refs, send_sems, recv_sems, token = refs[:n], refs[n:2 * n], refs[2 * n], refs[2 * n + 1], refs[-1]
        x, y, c = _place()
        for a in range(n):
            _rcopy(_my_half(ins[a], 1 - c, 2 if by_cols[a] else 1), land_refs[a], send_sems, recv_sems, a,
                   (x, y, 1 - c)).start()
        token[...] = jnp.zeros_like(token)

    held = [pltpu.with_memory_space_constraint(a, pltpu.HBM) for a in list(gs) + lands]
    return pl.pallas_call(
        body, name=name, in_specs=[_HBM] * (2 * n),
        out_specs=(_SEM, _SEM, *[_HBM] * (2 * n), pl.BlockSpec(memory_space=pltpu.VMEM)),
        out_shape=(pltpu.SemaphoreType.DMA((n,)), pltpu.SemaphoreType.DMA((n,)),
                   *[pltpu.HBM(a.shape, a.dtype) for a in held], jax.ShapeDtypeStruct((8, LANES), F32)),
        input_output_aliases={i: 2 + i for i in range(2 * n)},
        compiler_params=pltpu.CompilerParams(has_side_effects=_DATAFLOW),
    )(*held)


def _core_swap_partial_wait(started, after, by_cols, *, name):
    send_sems, recv_sems, held = started[0], started[1], started[2:-1]
    n = len(held) // 2

    def body(*refs):
        ins, land_refs, s_sems, r_sems = refs[:n], refs[n:2 * n], refs[2 * n], refs[2 * n + 1]
        x, y, c = _place()
        for a in range(n):
            cp = _rcopy(_my_half(ins[a], 1 - c, 2 if by_cols[a] else 1), land_refs[a], s_sems, r_sems, a, (x, y, 1 - c))
            cp.wait_send()
            cp.wait_recv()

    out = pl.pallas_call(
        body, name=name, in_specs=[_HBM] * (2 * n) + [_SEM, _SEM, _ANY], out_specs=[_HBM] * (2 * n),
        out_shape=[pltpu.HBM(a.shape, a.dtype) for a in held],
        input_output_aliases={i: i for i in range(2 * n)},
        compiler_params=pltpu.CompilerParams(has_side_effects=_DATAFLOW),
    )(*held, send_sems, recv_sems, after)
    return out[:n], out[n:]


def _core_swap_sum(fs, by_cols):
    n = len(fs)

    def body(*refs):
        ins, outs = refs[:n], refs[n:2 * n]
        send_sems, recv_sems = refs[2 * n:]
        x, y, c = _place()
        axes = [1 if cols else 0 for cols in by_cols]
        copies = [_rcopy(_my_half(i_ref, c, ax), _my_half(o_ref, c, ax), send_sems, recv_sems, a, (x, y, 1 - c))
                  for a, (i_ref, o_ref, ax) in enumerate(zip(ins, outs, axes))]
        for cp in copies:
            cp.start()
        for a, cp in enumerate(copies):
            cp.wait_send()
            theirs = _my_half(outs[a], 1 - c, axes[a])
            _rcopy(theirs, theirs, send_sems, recv_sems, a, (x, y, 1 - c)).wait_recv()

    return pl.pallas_call(
        body, name="core_swap_sum", in_specs=[_HBM] * n, out_specs=[_HBM] * n,
        out_shape=[jax.ShapeDtypeStruct(f.shape, f.dtype) for f in fs],
        input_output_aliases={a: a for a in range(n)},
        scratch_shapes=[pltpu.SemaphoreType.DMA((n,)), pltpu.SemaphoreType.DMA((n,))],
    )(*fs)


_SEM = pl.BlockSpec(memory_space=pltpu.SEMAPHORE)
_ANY = pl.BlockSpec(memory_space=pl.ANY)
_DATAFLOW = pltpu.SideEffectType.DATAFLOW_SIDE_EFFECTING


def _peer_flips(to_all):
    return _DEV_FLIPS if to_all else tuple((fx, fy, 0) for fx, fy in _CHIP_FLIPS)


def _to_chips_copies(srcs, lands, send_sems, recv_sems, per_chip_slot, to_all=False):
    x, y, c = _place()
    flips = _peer_flips(to_all)
    index = (lambda px, py, pc: 4 * px + 2 * py + pc) if to_all else (lambda px, py, pc: 2 * px + py)
    me = index(x, y, c)
    out = []
    for k, (fx, fy, fc) in enumerate(flips):
        peer = (_flip(x, fx), _flip(y, fy), _flip(c, fc))
        theirs = index(*peer)
        for a, (src, land) in enumerate(zip(srcs, lands)):
            piece = src.at[theirs] if per_chip_slot else src
            out.append((_rcopy(piece, land.at[me], send_sems, recv_sems, len(flips) * a + k, peer),
                        _rcopy(piece, land.at[theirs], send_sems, recv_sems, len(flips) * a + k, peer)))
    return out


def _to_chips_start(arrays, *, per_chip_slot, name, after=None, to_all=False):
    n = len(arrays)
    peers = len(_peer_flips(to_all))
    lands = [lax.empty((peers + 1,) + (a.shape[1:] if per_chip_slot else a.shape), a.dtype) for a in arrays]
    extra = [] if after is None else [after]

    def body(*refs):
        srcs, land_refs, token = refs[:n], refs[n:2 * n], refs[-1]
        send_sems, recv_sems = refs[2 * n + len(extra)], refs[2 * n + len(extra) + 1]
        for send, _ in _to_chips_copies(srcs, land_refs, send_sems, recv_sems, per_chip_slot, to_all):
            send.start()
        token[...] = jnp.zeros_like(token)

    held = [pltpu.with_memory_space_constraint(a, pltpu.HBM) for a in list(arrays) + lands]
    return pl.pallas_call(
        body, name=name, in_specs=[_HBM] * (2 * n) + [_ANY] * len(extra),
        out_specs=(_SEM, _SEM, *[_HBM] * (2 * n), pl.BlockSpec(memory_space=pltpu.VMEM)),
        out_shape=(pltpu.SemaphoreType.DMA((peers * n,)), pltpu.SemaphoreType.DMA((peers * n,)),
                   *[pltpu.HBM(a.shape, a.dtype) for a in held], jax.ShapeDtypeStruct((8, LANES), F32)),
        input_output_aliases={i: 2 + i for i in range(2 * n)},
        compiler_params=pltpu.CompilerParams(has_side_effects=_DATAFLOW),
    )(*held, *extra)


def _to_chips_wait(started, after, *, per_chip_slot, name, to_all=False):
    send_sems, recv_sems, held = started[0], started[1], started[2:-1]
    n = len(held) // 2

    def body(*refs):
        srcs, land_refs, s_sems, r_sems = refs[:n], refs[n:2 * n], refs[2 * n], refs[2 * n + 1]
        for send, arrival in _to_chips_copies(srcs, land_refs, s_sems, r_sems, per_chip_slot, to_all):
            send.wait_send()
            arrival.wait_recv()

    out = pl.pallas_call(
        body, name=name, in_specs=[_HBM] * (2 * n) + [_SEM, _SEM, _ANY], out_specs=[_HBM] * (2 * n),
        out_shape=[pltpu.HBM(a.shape, a.dtype) for a in held],
        input_output_aliases={i: i for i in range(2 * n)},
        compiler_params=pltpu.CompilerParams(has_side_effects=_DATAFLOW),
    )(*held, send_sems, recv_sems, after)
    return out[n:]


def _chip_exchange(ps, small):
    n = len(ps)
    rs = small.shape[0]

    def body(*refs):
        p_refs, s_ref = refs[:n], refs[n]
        l_refs, ls_ref = refs[n + 1:2 * n + 1], refs[2 * n + 1]
        send_sems, recv_sems, local_sems = refs[2 * n + 2:]
        x, y, c = _place()
        j0 = 2 * x + y
        d0 = 2 * j0 + c
        local = [pltpu.make_async_copy(p.at[j0], l.at[j0], local_sems.at[a]) for a, (p, l) in enumerate(zip(p_refs, l_refs))]
        local.append(pltpu.make_async_copy(s_ref, ls_ref.at[d0], local_sems.at[n]))
        for cp in local:
            cp.start()
        sends = []
        for k, (fx, fy) in enumerate(_CHIP_FLIPS):
            px, py = _flip(x, fx), _flip(y, fy)
            for a in range(n):
                sends.append(_rcopy(p_refs[a].at[2 * px + py], l_refs[a].at[j0], send_sems, recv_sems, 3 * a + k, (px, py, c)))
        for k, (fx, fy, fc) in enumerate(_DEV_FLIPS):
            peer = (_flip(x, fx), _flip(y, fy), _flip(c, fc))
            sends.append(_rcopy(s_ref, ls_ref.at[d0], send_sems, recv_sems, 3 * n + k, peer))
        for cp in sends:
            cp.start()
        for k, (fx, fy) in enumerate(_CHIP_FLIPS):
            px, py = _flip(x, fx), _flip(y, fy)
            for a in range(n):
                _rcopy(p_refs[a].at[j0], l_refs[a].at[2 * px + py], send_sems, recv_sems, 3 * a + k, (px, py, c)).wait_recv()
        for k, (fx, fy, fc) in enumerate(_DEV_FLIPS):
            px, py, pc = _flip(x, fx), _flip(y, fy), _flip(c, fc)
            _rcopy(s_ref, ls_ref.at[4 * px + 2 * py + pc], send_sems, recv_sems, 3 * n + k, (px, py, pc)).wait_recv()
        for cp in sends:
            cp.wait_send()
        for cp in local:
            cp.wait()

    n_sem = 3 * n + 7
    return pl.pallas_call(
        body, name="chip_exchange", in_specs=[_HBM] * (n + 1), out_specs=[_HBM] * (n + 1),
        out_shape=[jax.ShapeDtypeStruct(p.shape, F32) for p in ps] + [jax.ShapeDtypeStruct((8, rs, LANES), F32)],
        scratch_shapes=[pltpu.SemaphoreType.DMA((n_sem,)), pltpu.SemaphoreType.DMA((n_sem,)),
                        pltpu.SemaphoreType.DMA((n + 1,))],
    )(*ps, small)


def _half_blocks(rows, cols, by_cols):
    if by_cols:
        tc = _tile(cols // 2, 256)
        nb = cols // 2 // tc
        return rows, tc, nb, (lambda i, c: (0, c * nb + i))
    tr = _tile(rows // 2, 256)
    nb = rows // 2 // tr
    return tr, cols, nb, (lambda i, c: (c * nb + i, 0))


def _core_sum(g, part, core, *, name, by_cols):
    _, rows, cols = g.shape
    br, bc, nb, whole = _half_blocks(rows, cols, by_cols)
    mine = (lambda i: (0, i)) if by_cols else (lambda i: (i, 0))

    def body(c_ref, g_ref, p_ref, o_ref):
        o_ref[...] = g_ref[...] + p_ref[...]

    grid_spec = pltpu.PrefetchScalarGridSpec(
        num_scalar_prefetch=1, grid=(4, nb),
        in_specs=[pl.BlockSpec((1, br, bc), lambda j, i, c: (j,) + whole(i, c[0])),
                  pl.BlockSpec((1, br, bc), lambda j, i, c: (j,) + mine(i))],
        out_specs=pl.BlockSpec((1, br, bc), lambda j, i, c: (j,) + mine(i)),
    )
    return pl.pallas_call(
        body, name=name, grid_spec=grid_spec, out_shape=jax.ShapeDtypeStruct(part.shape, F32),
        compiler_params=_params(("parallel", "parallel")),
    )(core, g, part)


def _chip_sum(landed, part, place, *, name, by_cols):
    _, hr, hc = landed.shape
    rows, cols = (hr, 2 * hc) if by_cols else (2 * hr, hc)
    br, bc, nb, whole = _half_blocks(rows, cols, by_cols)
    mine = (lambda i: (0, i)) if by_cols else (lambda i: (i, 0))

    def body(c_ref, own_ref, a_ref, b_ref, d_ref, o_ref):
        o_ref[...] = ((own_ref[0] + a_ref[0]) + b_ref[0]) + d_ref[0]

    slot = lambda k: pl.BlockSpec((1, br, bc), lambda i, c: (jnp.bitwise_xor(c[1], k),) + mine(i))
    grid_spec = pltpu.PrefetchScalarGridSpec(
        num_scalar_prefetch=1, grid=(nb,),
        in_specs=[slot(0), slot(1), slot(2), slot(3)],
        out_specs=pl.BlockSpec((br, bc), lambda i, c: whole(i, c[0])),
    )
    return pl.pallas_call(
        body, name=name, grid_spec=grid_spec, out_shape=jax.ShapeDtypeStruct((rows, cols), F32),
        compiler_params=_params(("parallel",)),
    )(place, part, landed, landed, landed)


_ROW_POOL_W, _ROW_NORM_AB, _ROW_FINAL, _ROW_POOL_SCALE, _ROW_Q_NORM = 0, 512, 520, 528, 532
_ROW_KV_NORM, _ROW_O_NORM, _ROW_A_LOG, _ROW_DT_BIAS, _ROW_LOSS = 534, 535, 536, 537, 538
_ROW_CONV, _ROW_NORM_C, _SMALL_ROWS = 544, 640, 672
_CONV_ROWS = CONV_WIDTH * 6


def _put_rows(dst_ref, row0, src, width):
    for r in range(width // LANES):
        dst_ref[row0 + r:row0 + r + 1, :] = src[:, r * LANES:(r + 1) * LANES]


def _pack_small(g, loss_tile):
    names = ("wpool", "norm_ab", "final_norm", "pool_scale", "q_a_norm", "kv_a_norm", "o_norm", "a_log", "dt_bias",
             "conv_w", "norm_c")

    def body(wpool, norm_ab, final_norm, pool_scale, q_norm, kv_norm, o_norm, a_log, dt_bias, conv_w, norm_c, loss, o_ref):
        o_ref[...] = jnp.zeros_like(o_ref)
        for gi in range(4):
            o_ref[_ROW_POOL_W + gi * 128:_ROW_POOL_W + (gi + 1) * 128, :] = wpool[gi * 128:(gi + 1) * 128, gi * 128:(gi + 1) * 128]
        _put_rows(o_ref, _ROW_NORM_AB, norm_ab[...], 1024)
        _put_rows(o_ref, _ROW_FINAL, final_norm[...], 1024)
        _put_rows(o_ref, _ROW_POOL_SCALE, pool_scale[...], 512)
        _put_rows(o_ref, _ROW_Q_NORM, q_norm[...], 256)
        for row, ref in ((_ROW_KV_NORM, kv_norm), (_ROW_O_NORM, o_norm), (_ROW_A_LOG, a_log), (_ROW_DT_BIAS, dt_bias)):
            o_ref[row:row + 1, :] = ref[...]
        o_ref[_ROW_LOSS:_ROW_LOSS + 1, :] = loss[0:1, :]
        for j in range(4):
            for r in range(CONV_WIDTH):
                _put_rows(o_ref, _ROW_CONV + j * _CONV_ROWS + r * 6, conv_w[r:r + 1, j * 768:(j + 1) * 768], 768)
            _put_rows(o_ref, _ROW_NORM_C + j * 8, norm_c[:, j * 256:(j + 1) * 256], 256)

    vmem = pl.BlockSpec(memory_space=pltpu.VMEM)
    return pl.pallas_call(
        body, name="pack_small", in_specs=[vmem] * 12, out_specs=vmem,
        out_shape=jax.ShapeDtypeStruct((_SMALL_ROWS, LANES), F32),
    )(*[g[n] for n in names], loss_tile)


_SMALL_NAMES = ("pool_w", "norm_ab", "final_norm", "pool_scale", "q_a_norm", "kv_a_norm", "o_norm", "a_log", "dt_bias",
                "conv_w", "norm_c")


def _take_rows(src, row0, width):
    return jnp.concatenate([src[row0 + r:row0 + r + 1, :] for r in range(width // LANES)], axis=1)


def _small_update(small_all, late_all, ws, ms, vs):
    n = len(_SMALL_NAMES)

    def body(*refs):
        a_ref, late_ref = refs[0], refs[1]
        refs = refs[1:]
        w_refs, m_refs, v_refs = refs[1:1 + n], refs[1 + n:1 + 2 * n], refs[1 + 2 * n:1 + 3 * n]
        outs = refs[1 + 3 * n:1 + 7 * n]
        loss_ref, tot = refs[1 + 7 * n], refs[2 + 7 * n]
        acc, late = a_ref[0], late_ref[0]
        for d in range(1, 8):
            acc = acc + a_ref[d]
            late = late + late_ref[d]
        tot[...] = acc
        x, y, _ = _place()
        j0 = 2 * x + y
        conv = tot[pl.ds(pl.multiple_of(_ROW_CONV + j0 * _CONV_ROWS, 8), _CONV_ROWS), :]
        norm_c = tot[pl.ds(pl.multiple_of(_ROW_NORM_C + j0 * 8, 8), 8), :]
        whole = tot[_ROW_NORM_AB:_ROW_CONV, :]
        at = lambda row: row - _ROW_NORM_AB
        grads = {
            "norm_ab": _take_rows(late, 0, 1024), "final_norm": _take_rows(whole, at(_ROW_FINAL), 1024),
            "pool_scale": _take_rows(whole, at(_ROW_POOL_SCALE), 512), "q_a_norm": _take_rows(whole, at(_ROW_Q_NORM), 256),
            "kv_a_norm": whole[at(_ROW_KV_NORM):at(_ROW_KV_NORM) + 1, :], "o_norm": whole[at(_ROW_O_NORM):at(_ROW_O_NORM) + 1, :],
            "a_log": tot[_ROW_A_LOG:_ROW_A_LOG + 1, 0:GDN_HEADS],
            "dt_bias": tot[_ROW_DT_BIAS:_ROW_DT_BIAS + 1, 0:GDN_HEADS],
            "norm_c": _take_rows(norm_c, 0, 256),
        }
        loss_ref[...] = whole[at(_ROW_LOSS):at(_ROW_LOSS) + 1, :]
        for i, name in enumerate(_SMALL_NAMES):
            g_out = outs[4 * i]
            if name == "pool_w":
                for gi in range(4):
                    g_out[gi] = tot[_ROW_POOL_W + gi * 128:_ROW_POOL_W + (gi + 1) * 128, :]
            elif name == "conv_w":
                for r in range(CONV_WIDTH):
                    g_out[r:r + 1, :] = _take_rows(conv, r * 6, 768)
            else:
                g_out[...] = grads[name]
            _adam_update(g_out, w_refs[i], m_refs[i], v_refs[i], *outs[4 * i + 1:4 * i + 4])

    vmem = pl.BlockSpec(memory_space=pltpu.VMEM)
    out_shape = [jax.ShapeDtypeStruct(w.shape, F32) for w in ws for _ in range(4)] + [jax.ShapeDtypeStruct((1, LANES), F32)]
    return pl.pallas_call(
        body, name="small_update", in_specs=[vmem] * (2 + 3 * n), out_specs=[vmem] * (4 * n + 1), out_shape=out_shape,
        scratch_shapes=[pltpu.VMEM((_SMALL_ROWS, LANES), F32)],
        compiler_params=pltpu.CompilerParams(vmem_limit_bytes=VMEM_LIMIT),
    )(small_all, late_all, *ws, *ms, *vs)


def _adam_update(g_ref, w_ref, m_ref, v_ref, d_ref, mo_ref, vo_ref):
    gv = g_ref[...]
    mn = ADAM_B1 * m_ref[...] + (1.0 - ADAM_B1) * gv
    vn = ADAM_B2 * v_ref[...] + (1.0 - ADAM_B2) * (gv * gv)
    mo_ref[...] = mn
    vo_ref[...] = vn
    c1 = 1.0 - ADAM_B1 ** ADAM_STEP
    c2 = 1.0 - ADAM_B2 ** ADAM_STEP
    d_ref[...] = -ADAM_LR * ((mn / c1) / (jnp.sqrt(vn / c2) + ADAM_EPS) + ADAM_WD * w_ref[...])


def _adamw_rows(g, w, m, v, *, name):
    rows, cols = g.shape
    if rows % LANES == 0:
        tr = _tile(rows, 512)
        blk, steps = pl.BlockSpec((tr, cols), lambda i: (i, 0)), rows // tr
    else:
        tc = _tile(cols, 256)
        blk, steps = pl.BlockSpec((rows, tc), lambda i: (0, i)), cols // tc

    def body(*refs):
        _adam_update(*refs)

    out = jax.ShapeDtypeStruct((rows, cols), F32)
    return pl.pallas_call(
        body, name=name, grid=(steps,), in_specs=[blk] * 4, out_specs=[blk] * 3, out_shape=[out] * 3,
        compiler_params=_params(("parallel",)),
    )(g, w, m, v)


_ADAM_ROWWISE = ("w_in_ab", "w_q_b", "w_kv_b", "w_out_ab", "w_in_c", "w_out_c")


_SHARD_AXIS = {"w_in_ab": 1, "w_q_b": 1, "w_kv_b": 1, "w_out_ab": 0, "w_in_c": 1, "w_out_c": 0, "conv_w": 1, "norm_c": 1}
_ALL_NAMES = ("norm_ab", "w_in_ab", "q_a_norm", "w_q_b", "kv_a_norm", "w_kv_b", "pool_w", "pool_scale", "w_out_ab",
              "norm_c", "w_in_c", "conv_w", "a_log", "dt_bias", "o_norm", "w_out_c", "final_norm")


def _join_shards(a, axis):
    _, r, c = a.shape
    return a.reshape(4 * r, c) if axis == 0 else jnp.transpose(a, (1, 0, 2)).reshape(r, 4 * c)


def _split_shards(a, axis):
    r, c = a.shape
    return a.reshape(4, r // 4, c) if axis == 0 else jnp.transpose(a.reshape(r, 4, c // 4), (1, 0, 2))


def kernel(x, positions, norm_ab, w_in_ab, q_a_norm, w_q_b, kv_a_norm, w_kv_b, pool_w, pool_scale, w_out_ab, norm_c, w_in_c, conv_w, a_log, dt_bias, o_norm, w_out_c, final_norm, loss_target, m_norm_ab, m_w_in_ab, m_q_a_norm, m_w_q_b, m_kv_a_norm, m_w_kv_b, m_pool_w, m_pool_scale, m_w_out_ab, m_norm_c, m_w_in_c, m_conv_w, m_a_log, m_dt_bias, m_o_norm, m_w_out_c, m_final_norm, v_norm_ab, v_w_in_ab, v_q_a_norm, v_w_q_b, v_kv_a_norm, v_w_kv_b, v_pool_w, v_pool_scale, v_w_out_ab, v_norm_c, v_w_in_c, v_conv_w, v_a_log, v_dt_bias, v_o_norm, v_w_out_c, v_final_norm):
    given = dict(locals())
    c = lax.axis_index("c")
    t = x.shape[1]

    def shard_of(prefix, name):
        a = given[prefix + name]
        return a.reshape(a.shape[1:]) if a.ndim > 2 else a.reshape(1, -1)

    big, big_even, big_odd, small_sharded = _ADAM_ROWWISE, _ADAM_ROWWISE[:4], _ADAM_ROWWISE[4:], ("conv_w", "norm_c")
    chip = 2 * lax.axis_index("x") + lax.axis_index("y")
    core = c.astype(jnp.int32).reshape(1)
    place = jnp.stack([c, chip]).astype(jnp.int32)
    later = {"mid": big_even[1:], "odd": big_odd + small_sharded}
    travelling = {}

    def send(tag, after=None):
        shards = [shard_of("", n).astype(BF16) if n in big else shard_of("", n) for n in later[tag]]
        started = _to_chips_start(shards, per_chip_slot=False, name="gather_" + tag + "_start", after=after)
        travelling[tag] = (shards, started)
        return started[-1][0, 0]

    mid_sent = send("mid")
    gathered = _gather_weights([shard_of("", "w_in_ab").astype(BF16)], [])
    full = {"w_in_ab": _join_shards(gathered[0], _SHARD_AXIS["w_in_ab"])}
    for name in ("norm_ab", "q_a_norm", "kv_a_norm", "pool_w", "pool_scale"):
        full[name] = shard_of("", name)
    lw = _layout_in_ab(full)
    lw["norm_ab"] = lw["norm_ab"] + mid_sent

    def more_weights(tag, after):
        shards, started = travelling[tag]
        landed = _to_chips_wait(started, after, per_chip_slot=False, name="gather_" + tag + "_wait")
        w = {}
        for name, land, own in zip(later[tag], landed, shards):
            w[name] = _join_shards(lax.dynamic_update_index_in_dim(land, own, chip, 0), _SHARD_AXIS[name])
        if tag == "mid":
            out = _layout_mid(w)
            out["wq"] = out["wq"] + send("odd", after=landed[0]).astype(BF16)
            return out
        for name in ("a_log", "dt_bias", "o_norm", "final_norm"):
            w[name] = shard_of("", name)
        return _layout_odd(w)

    transposed = ("w_in_ab", "w_in_c")

    def chip_slots(names, g):
        grads = _unlayout_grads(g, names)
        return ([_split_shards(grads[n], 0 if n in transposed else _SHARD_AXIS[n]) for n in names],
                [n in transposed for n in names])

    def chip_partials(names, slots, partial, by_cols):
        return [_core_sum(s, p, core, name="core_sum_" + n, by_cols=b) for n, s, p, b in zip(names, slots, partial, by_cols)]

    groups = {"odd": big_odd, "out_ab": ("w_out_ab",), "in_ab": ("w_in_ab", "w_q_b", "w_kv_b")}
    sent, swapping = {}, {}

    def on_grads(tag, g, after):
        if tag == "odd":
            slots, by_cols = chip_slots(groups[tag], g)
            swapping[tag] = (slots, by_cols, _core_swap_partial_start(slots, by_cols, name="core_swap_partial_odd_start"))
            return swapping[tag][2][-1][0, 0]
        if tag == "odd_go":
            tag = "odd"
            _, by_cols, started = swapping[tag]
            slots, partial = _core_swap_partial_wait(started, after, by_cols, name="core_swap_partial_odd_wait")
        else:
            slots, by_cols = chip_slots(groups[tag], g)
            partial = _core_swap_partial(slots, by_cols, name="core_swap_partial_" + tag)
        part = chip_partials(groups[tag], slots, partial, by_cols)
        sent[tag] = (part, _to_chips_start(part, per_chip_slot=True, name="exchange_" + tag + "_start"))
        token = sent[tag][1][-1][0, 0]
        if tag == "in_ab":
            pack = _pack_small({**g, "norm_ab": jnp.zeros((1, 1024), F32)}, g["loss"])
            sent["small"] = (pack, _to_chips_start([pack], per_chip_slot=False, to_all=True, name="exchange_small_start"))
            token = token + sent["small"][1][-1][0, 0]
        return token

    loss_tile, dx, g = _local_step(x[0], positions.reshape(t, 1), loss_target[0], lw, more_weights, on_grads)
    late_all = _chip_exchange([], g["norm_ab"].reshape(8, LANES))[-1]
    pack, started = sent.pop("small")
    landed = _to_chips_wait(started, late_all, per_chip_slot=False, to_all=True, name="exchange_small_wait")[0]
    small_all = lax.dynamic_update_index_in_dim(landed, pack, 2 * chip + c, 0)
    halves = {}
    for tag, names in groups.items():
        part, started = sent[tag]
        landed = _to_chips_wait(started, late_all, per_chip_slot=True, name="exchange_" + tag + "_wait")
        for n, l, p in zip(names, landed, part):
            halves[n] = _chip_sum(l, p, place, name="chip_sum_" + n, by_cols=n in transposed)
    gbig = dict(zip(big, _core_swap_sum([halves[n] for n in big], [n in transposed for n in big])))

    res = {}
    for name in big:
        operands = [gbig[name], shard_of("", name), shard_of("m_", name), shard_of("v_", name)]
        flip = name in transposed
        if flip:
            operands[1:] = [jnp.transpose(a) for a in operands[1:]]
        out = (operands[0],) + tuple(_adamw_rows(*operands, name="adamw_" + name))
        out = [jnp.transpose(a) for a in out] if flip else out
        res["grad", name], res["delta", name], res["m", name], res["v", name] = out
    out = _small_update(small_all, late_all, [shard_of("", n) for n in _SMALL_NAMES], [shard_of("m_", n) for n in _SMALL_NAMES],
                        [shard_of("v_", n) for n in _SMALL_NAMES])
    for i, name in enumerate(_SMALL_NAMES):
        res["grad", name], res["delta", name], res["m", name], res["v", name] = out[4 * i:4 * i + 4]
    res = {k: a.reshape(given[k[1]].shape) for k, a in res.items()}
    loss = out[-1][0, 0]
    outs = [loss, dx.reshape(x.shape)]
    for key in ("grad", "delta", "m", "v"):
        outs += [res[key, n] for n in _ALL_NAMES]
    return tuple(outs)
```
